```python
import jax, jax.numpy as jnp
from jax import lax
import numpy as np

D_MODEL = 1024
BATCH = 8
SEQ = 2048
DEPTH = 2

HEAD_DIM = 64
N_ATTN_HEADS = 8
ATTN_WIDTH = N_ATTN_HEADS * HEAD_DIM
DILATED_PATTERNS = ((128, 1), (512, 4), (2048, 16))
ROPE_THETA = 10000.0
NEG_INF = -1e30

N_DELTA_HEADS = 4
DELTA_DK = 128
DELTA_DV = 128
DELTA_K_WIDTH = N_DELTA_HEADS * DELTA_DK
DELTA_V_WIDTH = N_DELTA_HEADS * DELTA_DV
DELTA_CONV = 4
DELTA_CHUNK = 64

MIX_WIDTH = ATTN_WIDTH + DELTA_V_WIDTH
IN_COLS = 3 * ATTN_WIDTH + 2 * DELTA_K_WIDTH + 2 * DELTA_V_WIDTH + 2 * N_DELTA_HEADS

D_FF = 2816
FFN_CONV = 3
EPS = 1e-6

kernel_name = 'hybrid_dilated_swa_gated_deltanet_convglu'


def rms_norm(x, w):
    xf = x.astype(jnp.float32)
    y = xf * lax.rsqrt(jnp.mean(xf * xf, axis=-1, keepdims=True) + EPS)
    return (y * w.astype(jnp.float32)).astype(x.dtype)


def rope_tables(seq, dim):
    inv = 1.0 / (ROPE_THETA ** (jnp.arange(0, dim, 2, dtype=jnp.float32) / dim))
    ang = jnp.arange(seq, dtype=jnp.float32)[:, None] * inv[None, :]
    return jnp.cos(ang), jnp.sin(ang)


def apply_rope(x, cos, sin):
    x1, x2 = jnp.split(x, 2, axis=-1)
    c = cos[None, :, None, :]
    s = sin[None, :, None, :]
    return jnp.concatenate([x1 * c - x2 * s, x1 * s + x2 * c], axis=-1)


def causal_dwconv(x, w):
    K, C = w.shape
    return lax.conv_general_dilated(
        x, w[:, None, :].astype(x.dtype), window_strides=(1,), padding=[(K - 1, 0)],
        dimension_numbers=('NWC', 'WIO', 'NWC'), feature_group_count=C)


def dilated_branch(q, k, v, window, dilation):
    B, S, H, D = q.shape
    n = window // dilation
    L = S // dilation
    nb = -(-L // n)
    Lp = nb * n

    def split(t):
        return t.reshape(B, L, dilation, H, D).transpose(0, 2, 3, 1, 4)

    qs = jnp.pad(split(q), ((0, 0), (0, 0), (0, 0), (0, Lp - L), (0, 0)))
    qs = qs.reshape(B, dilation, H, nb, n, D)

    def kv_blocks(t):
        tp = jnp.pad(split(t), ((0, 0), (0, 0), (0, 0), (n, Lp - L), (0, 0)))
        prev = tp[:, :, :, :Lp].reshape(B, dilation, H, nb, n, D)
        cur = tp[:, :, :, n:].reshape(B, dilation, H, nb, n, D)
        return jnp.concatenate([prev, cur], axis=4)

    kb = kv_blocks(k)
    vb = kv_blocks(v)
    s = jnp.einsum('brhnqd,brhnkd->brhnqk', qs, kb) * (D ** -0.5)
    a = jnp.arange(n)[:, None]
    c = jnp.arange(2 * n)[None, :]
    key_pos = (jnp.arange(nb)[:, None, None] - 1) * n + c[None]
    valid = ((c >= a) & (c <= a + n))[None] & (key_pos >= 0)
    s = jnp.where(valid, s, NEG_INF)
    m = jnp.max(s, axis=-1, keepdims=True)
    e = jnp.exp(s - m)
    l = jnp.sum(e, axis=-1, keepdims=True)
    o = jnp.einsum('brhnqk,brhnkd->brhnqd', e, vb) / l
    lse = (m + jnp.log(l))[..., 0]

    o = o.reshape(B, dilation, H, Lp, D)[:, :, :, :L]
    o = o.transpose(0, 3, 1, 2, 4).reshape(B, S, H, D)
    lse = lse.reshape(B, dilation, H, Lp)[:, :, :, :L]
    lse = lse.transpose(0, 3, 1, 2).reshape(B, S, H)
    return o, lse


def dilated_attention(q, k, v):
    outs, lses = [], []
    for window, dilation in DILATED_PATTERNS:
        o, lse = dilated_branch(q, k, v, window, dilation)
        outs.append(o)
        lses.append(lse)
    wts = jax.nn.softmax(jnp.stack(lses, axis=0), axis=0)
    return jnp.sum(wts[..., None] * jnp.stack(outs, axis=0), axis=0)


def gated_delta_rule(q, k, v, beta, g):
    B, S, H, K = q.shape
    V = v.shape[-1]
    C = DELTA_CHUNK
    N = S // C
    q = q * lax.rsqrt(jnp.sum(q * q, axis=-1, keepdims=True) + EPS) * (K ** -0.5)
    k = k * lax.rsqrt(jnp.sum(k * k, axis=-1, keepdims=True) + EPS)

    def chunk(t):
        return jnp.moveaxis(t.reshape((B, N, C, H) + t.shape[3:]), 3, 1)

    q, k, v, beta, g = chunk(q), chunk(k), chunk(v), chunk(beta), chunk(g)
    g = jnp.cumsum(g, axis=-1)
    tril = jnp.tril(jnp.ones((C, C), dtype=bool))
    strict = jnp.tril(jnp.ones((C, C), dtype=bool), -1)
    decay = jnp.exp(jnp.where(tril, g[..., :, None] - g[..., None, :], -jnp.inf))
    kb = k * beta[..., None]
    A = jnp.where(strict, jnp.einsum('bhnik,bhnjk->bhnij', kb, k) * decay, 0.0)
    eye = jnp.eye(C, dtype=A.dtype)
    T = lax.linalg.triangular_solve(A + eye, jnp.broadcast_to(eye, A.shape), left_side=True,
                                    lower=True, unit_diagonal=True)
    u = jnp.einsum('bhnij,bhnjv->bhniv', T, v * beta[..., None])
    w = jnp.einsum('bhnij,bhnjk->bhnik', T, kb * jnp.exp(g)[..., None])
    qk = jnp.einsum('bhnik,bhnjk->bhnij', q, k) * decay
    q_dec = q * jnp.exp(g)[..., None]
    k_dec = k * jnp.exp(g[..., -1:] - g)[..., None]
    g_last = jnp.exp(g[..., -1])

    def step(state, xs):
        q_i, k_i, u_i, w_i, qk_i, gl = xs
        v_new = u_i - jnp.einsum('bhck,bhkv->bhcv', w_i, state)
        o = jnp.einsum('bhck,bhkv->bhcv', q_i, state) + jnp.einsum('bhij,bhjv->bhiv', qk_i, v_new)
        state = state * gl[..., None, None] + jnp.einsum('bhck,bhcv->bhkv', k_i, v_new)
        return state, o

    xs = (jnp.moveaxis(q_dec, 2, 0), jnp.moveaxis(k_dec, 2, 0), jnp.moveaxis(u, 2, 0),
          jnp.moveaxis(w, 2, 0), jnp.moveaxis(qk, 2, 0), jnp.moveaxis(g_last, 2, 0))
    _, o = lax.scan(step, jnp.zeros((B, H, K, V), jnp.float32), xs)
    return jnp.transpose(o, (1, 0, 3, 2, 4)).reshape(B, S, H, V)


def hybrid_mixer(h, w_in, dn_conv_w, dn_a_log, dn_dt_bias, dn_norm_w, w_out, cos, sin):
    B, S, _ = h.shape
    proj = h @ w_in
    cuts = [ATTN_WIDTH, 2 * ATTN_WIDTH, 3 * ATTN_WIDTH,
            3 * ATTN_WIDTH + 2 * DELTA_K_WIDTH + DELTA_V_WIDTH,
            3 * ATTN_WIDTH + 2 * DELTA_K_WIDTH + 2 * DELTA_V_WIDTH,
            3 * ATTN_WIDTH + 2 * DELTA_K_WIDTH + 2 * DELTA_V_WIDTH + N_DELTA_HEADS]
    aq, ak, av, dn_qkv, dn_z, dn_b, dn_a = jnp.split(proj, cuts, axis=-1)

    aq = apply_rope(aq.reshape(B, S, N_ATTN_HEADS, HEAD_DIM).astype(jnp.float32), cos, sin)
    ak = apply_rope(ak.reshape(B, S, N_ATTN_HEADS, HEAD_DIM).astype(jnp.float32), cos, sin)
    av = av.reshape(B, S, N_ATTN_HEADS, HEAD_DIM).astype(jnp.float32)
    attn_out = dilated_attention(aq, ak, av).reshape(B, S, ATTN_WIDTH).astype(h.dtype)

    qkv = jax.nn.silu(causal_dwconv(dn_qkv, dn_conv_w)).astype(jnp.float32)
    dq, dk, dv = jnp.split(qkv, [DELTA_K_WIDTH, 2 * DELTA_K_WIDTH], axis=-1)
    beta = jax.nn.sigmoid(dn_b.astype(jnp.float32))
    g = -jnp.exp(dn_a_log.astype(jnp.float32)) * jax.nn.softplus(
        dn_a.astype(jnp.float32) + dn_dt_bias.astype(jnp.float32))
    o = gated_delta_rule(dq.reshape(B, S, N_DELTA_HEADS, DELTA_DK),
                         dk.reshape(B, S, N_DELTA_HEADS, DELTA_DK),
                         dv.reshape(B, S, N_DELTA_HEADS, DELTA_DV), beta, g)
    z = dn_z.reshape(B, S, N_DELTA_HEADS, DELTA_DV).astype(jnp.float32)
    dn_out = (rms_norm(o, dn_norm_w) * jax.nn.silu(z)).reshape(B, S, DELTA_V_WIDTH).astype(h.dtype)

    return jnp.concatenate([attn_out, dn_out], axis=-1) @ w_out


def conv_glu_ffn(h, ffn_w_in, ffn_conv_w, ffn_conv_b, ffn_w_out):
    u = causal_dwconv(h @ ffn_w_in, ffn_conv_w) + ffn_conv_b
    gate, up = jnp.split(u, 2, axis=-1)
    return (jax.nn.gelu(gate, approximate=True) * up) @ ffn_w_out


def _fwd_setup_inputs(seed: int = 0) -> dict:
    key = jax.random.key(seed)
    ks = jax.random.split(key, 16)
    f32 = jnp.float32
    nrm = lambda k, shape, scale: jax.random.normal(k, shape, f32) * scale
    dt = jnp.exp(jax.random.uniform(ks[4], (DEPTH, N_DELTA_HEADS), f32) *
                 (jnp.log(0.1) - jnp.log(0.001)) + jnp.log(0.001))
    return {
        'x': nrm(ks[0], (BATCH, SEQ, D_MODEL), 1.0),
        'w_in': nrm(ks[1], (DEPTH, D_MODEL, IN_COLS), D_MODEL ** -0.5),
        'dn_conv_w': nrm(ks[2], (DEPTH, DELTA_CONV, 2 * DELTA_K_WIDTH + DELTA_V_WIDTH), DELTA_CONV ** -0.5),
        'dn_a_log': jnp.log(jax.random.uniform(ks[3], (DEPTH, N_DELTA_HEADS), f32, 1.0, 16.0)),
        'dn_dt_bias': dt + jnp.log(-jnp.expm1(-dt)),
        'dn_norm_w': 1.0 + nrm(ks[5], (DEPTH, DELTA_DV), 0.05),
        'w_out': nrm(ks[6], (DEPTH, MIX_WIDTH, D_MODEL), MIX_WIDTH ** -0.5),
        'ffn_w_in': nrm(ks[7], (DEPTH, D_MODEL, 2 * D_FF), D_MODEL ** -0.5),
        'ffn_conv_w': nrm(ks[8], (DEPTH, FFN_CONV, 2 * D_FF), FFN_CONV ** -0.5),
        'ffn_conv_b': nrm(ks[9], (DEPTH, 2 * D_FF), 0.01),
        'ffn_w_out': nrm(ks[10], (DEPTH, D_FF, D_MODEL), D_FF ** -0.5),
        'norm_pre_mix': 1.0 + nrm(ks[11], (DEPTH, D_MODEL), 0.05),
        'norm_post_mix': 1.0 + nrm(ks[12], (DEPTH, D_MODEL), 0.05),
        'norm_pre_ffn': 1.0 + nrm(ks[13], (DEPTH, D_MODEL), 0.05),
        'norm_post_ffn': 1.0 + nrm(ks[14], (DEPTH, D_MODEL), 0.05),
    }


def _fwd_reference(x, w_in, dn_conv_w, dn_a_log, dn_dt_bias, dn_norm_w, w_out, ffn_w_in, ffn_conv_w,
              ffn_conv_b, ffn_w_out, norm_pre_mix, norm_post_mix, norm_pre_ffn, norm_post_ffn):
    cos, sin = rope_tables(x.shape[1], HEAD_DIM)
    for l in range(DEPTH):
        h = rms_norm(x, norm_pre_mix[l])
        mix = hybrid_mixer(h, w_in[l], dn_conv_w[l], dn_a_log[l], dn_dt_bias[l], dn_norm_w[l],
                           w_out[l], cos, sin)
        x = x + rms_norm(mix, norm_post_mix[l]).astype(x.dtype)
        h = rms_norm(x, norm_pre_ffn[l])
        f = conv_glu_ffn(h, ffn_w_in[l], ffn_conv_w[l], ffn_conv_b[l], ffn_w_out[l])
        x = x + rms_norm(f, norm_post_ffn[l]).astype(x.dtype)
    return x


import jax as _jax
import jax.numpy as _jnp

TWIN_FORMAT = 'train_step'
FWD_PARAMS = ['x', 'w_in', 'dn_conv_w', 'dn_a_log', 'dn_dt_bias', 'dn_norm_w', 'w_out', 'ffn_w_in', 'ffn_conv_w', 'ffn_conv_b', 'ffn_w_out', 'norm_pre_mix', 'norm_post_mix', 'norm_pre_ffn', 'norm_post_ffn']
TWIN_WEIGHTS = ['w_in', 'dn_conv_w', 'dn_a_log', 'dn_dt_bias', 'dn_norm_w', 'w_out', 'ffn_w_in', 'ffn_conv_w', 'ffn_conv_b', 'ffn_w_out', 'norm_pre_mix', 'norm_post_mix', 'norm_pre_ffn', 'norm_post_ffn']
TWIN_DIFF_INPUT = 'x'
TWIN_INPUTS = ['x', 'w_in', 'dn_conv_w', 'dn_a_log', 'dn_dt_bias', 'dn_norm_w', 'w_out', 'ffn_w_in', 'ffn_conv_w', 'ffn_conv_b', 'ffn_w_out', 'norm_pre_mix', 'norm_post_mix', 'norm_pre_ffn', 'norm_post_ffn', 'loss_target', 'm_w_in', 'm_dn_conv_w', 'm_dn_a_log', 'm_dn_dt_bias', 'm_dn_norm_w', 'm_w_out', 'm_ffn_w_in', 'm_ffn_conv_w', 'm_ffn_conv_b', 'm_ffn_w_out', 'm_norm_pre_mix', 'm_norm_post_mix', 'm_norm_pre_ffn', 'm_norm_post_ffn', 'v_w_in', 'v_dn_conv_w', 'v_dn_a_log', 'v_dn_dt_bias', 'v_dn_norm_w', 'v_w_out', 'v_ffn_w_in', 'v_ffn_conv_w', 'v_ffn_conv_b', 'v_ffn_w_out', 'v_norm_pre_mix', 'v_norm_post_mix', 'v_norm_pre_ffn', 'v_norm_post_ffn']
TWIN_OUTPUTS = ['loss', 'grad_x', 'grad_w_in', 'grad_dn_conv_w', 'grad_dn_a_log', 'grad_dn_dt_bias', 'grad_dn_norm_w', 'grad_w_out', 'grad_ffn_w_in', 'grad_ffn_conv_w', 'grad_ffn_conv_b', 'grad_ffn_w_out', 'grad_norm_pre_mix', 'grad_norm_post_mix', 'grad_norm_pre_ffn', 'grad_norm_post_ffn', 'delta_w_in', 'delta_dn_conv_w', 'delta_dn_a_log', 'delta_dn_dt_bias', 'delta_dn_norm_w', 'delta_w_out', 'delta_ffn_w_in', 'delta_ffn_conv_w', 'delta_ffn_conv_b', 'delta_ffn_w_out', 'delta_norm_pre_mix', 'delta_norm_post_mix', 'delta_norm_pre_ffn', 'delta_norm_post_ffn', 'new_m_w_in', 'new_m_dn_conv_w', 'new_m_dn_a_log', 'new_m_dn_dt_bias', 'new_m_dn_norm_w', 'new_m_w_out', 'new_m_ffn_w_in', 'new_m_ffn_conv_w', 'new_m_ffn_conv_b', 'new_m_ffn_w_out', 'new_m_norm_pre_mix', 'new_m_norm_post_mix', 'new_m_norm_pre_ffn', 'new_m_norm_post_ffn', 'new_v_w_in', 'new_v_dn_conv_w', 'new_v_dn_a_log', 'new_v_dn_dt_bias', 'new_v_dn_norm_w', 'new_v_w_out', 'new_v_ffn_w_in', 'new_v_ffn_conv_w', 'new_v_ffn_conv_b', 'new_v_ffn_w_out', 'new_v_norm_pre_mix', 'new_v_norm_post_mix', 'new_v_norm_pre_ffn', 'new_v_norm_post_ffn']
TWIN_LEAF_KINDS = {'loss': 'loss', 'grad_x': 'grad_x', 'grad_w_in': 'grad_w', 'grad_dn_conv_w': 'grad_w', 'grad_dn_a_log': 'grad_w', 'grad_dn_dt_bias': 'grad_w', 'grad_dn_norm_w': 'grad_w', 'grad_w_out': 'grad_w', 'grad_ffn_w_in': 'grad_w', 'grad_ffn_conv_w': 'grad_w', 'grad_ffn_conv_b': 'grad_w', 'grad_ffn_w_out': 'grad_w', 'grad_norm_pre_mix': 'grad_w', 'grad_norm_post_mix': 'grad_w', 'grad_norm_pre_ffn': 'grad_w', 'grad_norm_post_ffn': 'grad_w', 'delta_w_in': 'delta_w', 'delta_dn_conv_w': 'delta_w', 'delta_dn_a_log': 'delta_w', 'delta_dn_dt_bias': 'delta_w', 'delta_dn_norm_w': 'delta_w', 'delta_w_out': 'delta_w', 'delta_ffn_w_in': 'delta_w', 'delta_ffn_conv_w': 'delta_w', 'delta_ffn_conv_b': 'delta_w', 'delta_ffn_w_out': 'delta_w', 'delta_norm_pre_mix': 'delta_w', 'delta_norm_post_mix': 'delta_w', 'delta_norm_pre_ffn': 'delta_w', 'delta_norm_post_ffn': 'delta_w', 'new_m_w_in': 'new_m', 'new_m_dn_conv_w': 'new_m', 'new_m_dn_a_log': 'new_m', 'new_m_dn_dt_bias': 'new_m', 'new_m_dn_norm_w': 'new_m', 'new_m_w_out': 'new_m', 'new_m_ffn_w_in': 'new_m', 'new_m_ffn_conv_w': 'new_m', 'new_m_ffn_conv_b': 'new_m', 'new_m_ffn_w_out': 'new_m', 'new_m_norm_pre_mix': 'new_m', 'new_m_norm_post_mix': 'new_m', 'new_m_norm_pre_ffn': 'new_m', 'new_m_norm_post_ffn': 'new_m', 'new_v_w_in': 'new_v', 'new_v_dn_conv_w': 'new_v', 'new_v_dn_a_log': 'new_v', 'new_v_dn_dt_bias': 'new_v', 'new_v_dn_norm_w': 'new_v', 'new_v_w_out': 'new_v', 'new_v_ffn_w_in': 'new_v', 'new_v_ffn_conv_w': 'new_v', 'new_v_ffn_conv_b': 'new_v', 'new_v_ffn_w_out': 'new_v', 'new_v_norm_pre_mix': 'new_v', 'new_v_norm_post_mix': 'new_v', 'new_v_norm_pre_ffn': 'new_v', 'new_v_norm_post_ffn': 'new_v'}


def _forward(args):
    return _fwd_reference(*[args[k] for k in FWD_PARAMS])


def _output_shape():
    out = _jax.eval_shape(lambda: _forward(_fwd_setup_inputs(0)))
    return out.shape, out.dtype

N_MICROBATCH = 1
ADAM_LR = 0.001
ADAM_B1 = 0.9
ADAM_B2 = 0.999
ADAM_EPS = 1e-08
ADAM_WD = 0.01
ADAM_STEP = 10
PER_EXAMPLE_BATCH_AXIS = {'x': 0, 'loss_target': 0}
SHARED_INPUTS = []
_WEIGHT_DTYPES = {'w_in': _jnp.float32, 'dn_conv_w': _jnp.float32, 'dn_a_log': _jnp.float32, 'dn_dt_bias': _jnp.float32, 'dn_norm_w': _jnp.float32, 'w_out': _jnp.float32, 'ffn_w_in': _jnp.float32, 'ffn_conv_w': _jnp.float32, 'ffn_conv_b': _jnp.float32, 'ffn_w_out': _jnp.float32, 'norm_pre_mix': _jnp.float32, 'norm_post_mix': _jnp.float32, 'norm_pre_ffn': _jnp.float32, 'norm_post_ffn': _jnp.float32}
MOMENT_SCALE = {'w_in': 6.397261e-01, 'dn_conv_w': 9.231319e-01, 'dn_a_log': 5.302679e+00, 'dn_dt_bias': 5.045048e+00, 'dn_norm_w': 3.801861e+00, 'w_out': 1.446878e+00, 'ffn_w_in': 3.393966e-01, 'ffn_conv_w': 3.908108e-01, 'ffn_conv_b': 1.427790e+00, 'ffn_w_out': 6.650564e-01, 'norm_pre_mix': 1.227712e+00, 'norm_post_mix': 1.607556e+01, 'norm_pre_ffn': 7.555332e-01, 'norm_post_ffn': 1.595694e+01}


def _to_microbatches(a, axis):
    t = _jnp.moveaxis(a, axis, 0)
    t = t.reshape((N_MICROBATCH, t.shape[0] // N_MICROBATCH) + t.shape[1:])
    return _jnp.moveaxis(t, 1, axis + 1)


def setup_inputs(seed: int = 0) -> dict:
    inp = _fwd_setup_inputs(seed)
    key = _jax.random.fold_in(_jax.random.key(seed), 7919)
    shape, _ = _output_shape()
    out = dict(inp)
    out["loss_target"] = _jax.random.normal(_jax.random.fold_in(key, 0), shape, _jnp.float32)
    for i, name in enumerate(TWIN_WEIGHTS):
        w = inp[name].astype(_jnp.float32)
        if MOMENT_SCALE is None:
            s = _jnp.sqrt(_jnp.mean(_jnp.square(w)) + 1e-30)
        else:
            s = MOMENT_SCALE[name]
        km, kv = _jax.random.split(_jax.random.fold_in(key, i + 1))
        out[name] = w
        out["m_" + name] = s * _jax.random.normal(km, w.shape, _jnp.float32)
        out["v_" + name] = (s * s) * _jax.random.uniform(kv, w.shape, _jnp.float32, 0.5, 1.5)
    if N_MICROBATCH > 1:
        for name, axis in PER_EXAMPLE_BATCH_AXIS.items():
            out[name] = _to_microbatches(out[name], axis)
    return {'x': out['x'], 'w_in': out['w_in'], 'dn_conv_w': out['dn_conv_w'], 'dn_a_log': out['dn_a_log'], 'dn_dt_bias': out['dn_dt_bias'], 'dn_norm_w': out['dn_norm_w'], 'w_out': out['w_out'], 'ffn_w_in': out['ffn_w_in'], 'ffn_conv_w': out['ffn_conv_w'], 'ffn_conv_b': out['ffn_conv_b'], 'ffn_w_out': out['ffn_w_out'], 'norm_pre_mix': out['norm_pre_mix'], 'norm_post_mix': out['norm_post_mix'], 'norm_pre_ffn': out['norm_pre_ffn'], 'norm_post_ffn': out['norm_post_ffn'], 'loss_target': out['loss_target'], 'm_w_in': out['m_w_in'], 'm_dn_conv_w': out['m_dn_conv_w'], 'm_dn_a_log': out['m_dn_a_log'], 'm_dn_dt_bias': out['m_dn_dt_bias'], 'm_dn_norm_w': out['m_dn_norm_w'], 'm_w_out': out['m_w_out'], 'm_ffn_w_in': out['m_ffn_w_in'], 'm_ffn_conv_w': out['m_ffn_conv_w'], 'm_ffn_conv_b': out['m_ffn_conv_b'], 'm_ffn_w_out': out['m_ffn_w_out'], 'm_norm_pre_mix': out['m_norm_pre_mix'], 'm_norm_post_mix': out['m_norm_post_mix'], 'm_norm_pre_ffn': out['m_norm_pre_ffn'], 'm_norm_post_ffn': out['m_norm_post_ffn'], 'v_w_in': out['v_w_in'], 'v_dn_conv_w': out['v_dn_conv_w'], 'v_dn_a_log': out['v_dn_a_log'], 'v_dn_dt_bias': out['v_dn_dt_bias'], 'v_dn_norm_w': out['v_dn_norm_w'], 'v_w_out': out['v_w_out'], 'v_ffn_w_in': out['v_ffn_w_in'], 'v_ffn_conv_w': out['v_ffn_conv_w'], 'v_ffn_conv_b': out['v_ffn_conv_b'], 'v_ffn_w_out': out['v_ffn_w_out'], 'v_norm_pre_mix': out['v_norm_pre_mix'], 'v_norm_post_mix': out['v_norm_post_mix'], 'v_norm_pre_ffn': out['v_norm_pre_ffn'], 'v_norm_post_ffn': out['v_norm_post_ffn']}


def _loss(weights, diff, rest, loss_target):
    with _jax.named_scope("forward"):
        args = {**rest, TWIN_DIFF_INPUT: diff, **{k: w.astype(_WEIGHT_DTYPES[k]) for k, w in weights.items()}}
        y = _forward(args)
    with _jax.named_scope("loss_head"):
        err = _jnp.square(y.astype(_jnp.float32) - loss_target)
        return 0.5 * _jnp.sum(_jnp.mean(err, axis=-1)) if err.ndim else 0.5 * err


def _adamw(w, g, m, v):
    m = ADAM_B1 * m + (1.0 - ADAM_B1) * g
    v = ADAM_B2 * v + (1.0 - ADAM_B2) * _jnp.square(g)
    m_hat = m / (1.0 - ADAM_B1 ** ADAM_STEP)
    v_hat = v / (1.0 - ADAM_B2 ** ADAM_STEP)
    delta = -ADAM_LR * (m_hat / (_jnp.sqrt(v_hat) + ADAM_EPS) + ADAM_WD * w)
    return delta, m, v


def reference(x, w_in, dn_conv_w, dn_a_log, dn_dt_bias, dn_norm_w, w_out, ffn_w_in, ffn_conv_w, ffn_conv_b, ffn_w_out, norm_pre_mix, norm_post_mix, norm_pre_ffn, norm_post_ffn, loss_target, m_w_in, m_dn_conv_w, m_dn_a_log, m_dn_dt_bias, m_dn_norm_w, m_w_out, m_ffn_w_in, m_ffn_conv_w, m_ffn_conv_b, m_ffn_w_out, m_norm_pre_mix, m_norm_post_mix, m_norm_pre_ffn, m_norm_post_ffn, v_w_in, v_dn_conv_w, v_dn_a_log, v_dn_dt_bias, v_dn_norm_w, v_w_out, v_ffn_w_in, v_ffn_conv_w, v_ffn_conv_b, v_ffn_w_out, v_norm_pre_mix, v_norm_post_mix, v_norm_pre_ffn, v_norm_post_ffn):
    given = dict(x=x, w_in=w_in, dn_conv_w=dn_conv_w, dn_a_log=dn_a_log, dn_dt_bias=dn_dt_bias, dn_norm_w=dn_norm_w, w_out=w_out, ffn_w_in=ffn_w_in, ffn_conv_w=ffn_conv_w, ffn_conv_b=ffn_conv_b, ffn_w_out=ffn_w_out, norm_pre_mix=norm_pre_mix, norm_post_mix=norm_post_mix, norm_pre_ffn=norm_pre_ffn, norm_post_ffn=norm_post_ffn, loss_target=loss_target, m_w_in=m_w_in, m_dn_conv_w=m_dn_conv_w, m_dn_a_log=m_dn_a_log, m_dn_dt_bias=m_dn_dt_bias, m_dn_norm_w=m_dn_norm_w, m_w_out=m_w_out, m_ffn_w_in=m_ffn_w_in, m_ffn_conv_w=m_ffn_conv_w, m_ffn_conv_b=m_ffn_conv_b, m_ffn_w_out=m_ffn_w_out, m_norm_pre_mix=m_norm_pre_mix, m_norm_post_mix=m_norm_post_mix, m_norm_pre_ffn=m_norm_pre_ffn, m_norm_post_ffn=m_norm_post_ffn, v_w_in=v_w_in, v_dn_conv_w=v_dn_conv_w, v_dn_a_log=v_dn_a_log, v_dn_dt_bias=v_dn_dt_bias, v_dn_norm_w=v_dn_norm_w, v_w_out=v_w_out, v_ffn_w_in=v_ffn_w_in, v_ffn_conv_w=v_ffn_conv_w, v_ffn_conv_b=v_ffn_conv_b, v_ffn_w_out=v_ffn_w_out, v_norm_pre_mix=v_norm_pre_mix, v_norm_post_mix=v_norm_post_mix, v_norm_pre_ffn=v_norm_pre_ffn, v_norm_post_ffn=v_norm_post_ffn)
    weights = {n: given[n] for n in TWIN_WEIGHTS}
    shared = {n: given[n] for n in SHARED_INPUTS}
    per_example = {n: given[n] for n in ['x']}
    grad_fn = _jax.value_and_grad(_loss, argnums=(0, 1))

    def one_microbatch(ex, loss_target):
        ex = dict(ex)
        diff = ex.pop(TWIN_DIFF_INPUT)
        return grad_fn(weights, diff, {**shared, **ex}, loss_target)

    if N_MICROBATCH == 1:
        loss, (grad_w, grad_x) = one_microbatch(per_example, given["loss_target"])
    else:
        def body(carry, xs):
            loss_sum, grad_sum = carry
            l_k, (gw_k, gx_k) = one_microbatch(xs[0], xs[1])
            with _jax.named_scope("update"):
                return (loss_sum + l_k, _jax.tree.map(_jnp.add, grad_sum, gw_k)), gx_k

        init = (_jnp.zeros((), _jnp.float32), _jax.tree.map(_jnp.zeros_like, weights))
        (loss, grad_w), grad_x = _jax.lax.scan(body, init, (per_example, given["loss_target"]))
    with _jax.named_scope("update"):
        delta_w, new_m, new_v = {}, {}, {}
        for n in TWIN_WEIGHTS:
            delta_w[n], new_m[n], new_v[n] = _adamw(weights[n], grad_w[n], given["m_" + n], given["v_" + n])
    return (loss, grad_x, *[grad_w[n] for n in TWIN_WEIGHTS], *[delta_w[n] for n in TWIN_WEIGHTS],
            *[new_m[n] for n in TWIN_WEIGHTS], *[new_v[n] for n in TWIN_WEIGHTS])
```

```python
import functools
import math

import jax
import jax.numpy as jnp
from jax import lax
from jax.experimental import pallas as pl
from jax.experimental.pallas import tpu as pltpu

F32 = jnp.float32
BF16 = jnp.bfloat16
HI = lax.Precision.HIGHEST
MESH = pl.DeviceIdType.MESH

N_DEV = 8
SEQ = 2048
D_MODEL = 1024
DEPTH = 2
N_PAIR = 4
HEAD_DIM = 64
ATTN_W = 512
ATTN_BLK = 128
DILATIONS = (1, 4, 16)
SEGMENT_BLOCKS = (16, 4, 1)
NDH = 4
CH = 64
NCH = SEQ // CH
IN_COLS = 3592
IN_PAD = 3840
D_FF = 2816
EPS = 1e-6
NEG = -1e30
ROPE_THETA = 10000.0

ADAM_LR, ADAM_B1, ADAM_B2, ADAM_EPS, ADAM_WD, ADAM_STEP = 0.001, 0.9, 0.999, 1e-08, 0.01, 10

VMEM_LIMIT = 56 * 1024 * 1024


def _cp(*sem):
    return pltpu.CompilerParams(dimension_semantics=sem, vmem_limit_bytes=VMEM_LIMIT)


def _dot(a, b, dims, precision=None):
    if precision is None:
        a = a.astype(BF16)
        b = b.astype(BF16)
    return lax.dot_general(a, b, (dims, ((), ())), preferred_element_type=F32, precision=precision)


def _make_mm(precision):
    @jax.custom_vjp
    def nn(a, b):
        return _dot(a, b, ((1,), (0,)), precision)

    @jax.custom_vjp
    def nt(a, b):
        return _dot(a, b, ((1,), (1,)), precision)

    @jax.custom_vjp
    def tn(a, b):
        return _dot(a, b, ((0,), (0,)), precision)

    nn.defvjp(lambda a, b: (nn(a, b), (a, b)), lambda r, g: (nt(g, r[1]), tn(r[0], g)))
    nt.defvjp(lambda a, b: (nt(a, b), (a, b)), lambda r, g: (nn(g, r[1]), tn(g, r[0])))
    tn.defvjp(lambda a, b: (tn(a, b), (a, b)), lambda r, g: (nt(r[1], g), nn(r[0], g)))
    return nn, nt, tn


MM, MM_NT, MM_TN = _make_mm(None)
MMH, _, _ = _make_mm(HI)


def _matmul(a, b, *, ta=False, tb=False, tm, tn, tk, name, out_dtype=F32):
    (k_dim, m_dim) = a.shape if ta else a.shape[::-1]
    (n_dim, k2) = b.shape if tb else b.shape[::-1]
    assert k_dim == k2 and m_dim % tm == 0 and n_dim % tn == 0 and k_dim % tk == 0, (a.shape, b.shape, tm, tn, tk)
    nk = k_dim // tk
    dims = ((0 if ta else 1,), (1 if tb else 0,))

    def body(a_ref, b_ref, o_ref, acc_ref):
        k = pl.program_id(2)
        p = _dot(a_ref[...], b_ref[...], dims)

        @pl.when(k == 0)
        def _():
            acc_ref[...] = p

        @pl.when(k > 0)
        def _():
            acc_ref[...] += p

        @pl.when(k == nk - 1)
        def _():
            o_ref[...] = acc_ref[...].astype(out_dtype)

    a_spec = pl.BlockSpec((tk, tm), lambda i, j, k: (k, i)) if ta else pl.BlockSpec((tm, tk), lambda i, j, k: (i, k))
    b_spec = pl.BlockSpec((tn, tk), lambda i, j, k: (j, k)) if tb else pl.BlockSpec((tk, tn), lambda i, j, k: (k, j))
    return pl.pallas_call(
        body, name=name,
        grid=(m_dim // tm, n_dim // tn, nk),
        in_specs=[a_spec, b_spec],
        out_specs=pl.BlockSpec((tm, tn), lambda i, j, k: (i, j)),
        out_shape=jax.ShapeDtypeStruct((m_dim, n_dim), out_dtype),
        scratch_shapes=[pltpu.VMEM((tm, tn), F32)],
        compiler_params=_cp("parallel", "parallel", "arbitrary"),
    )(a, b)


NORM_ROWS = 256


def _rms(x, w):
    return x * lax.rsqrt(jnp.mean(x * x, axis=1, keepdims=True) + EPS) * w


def _norm_fwd(x, w_row, name, out_dtype=BF16):
    def body(x_ref, w_ref, o_ref):
        o_ref[...] = _rms(x_ref[...], w_ref[...]).astype(out_dtype)

    return pl.pallas_call(
        body, name=name, grid=(SEQ // NORM_ROWS,),
        in_specs=[pl.BlockSpec((NORM_ROWS, D_MODEL), lambda i: (i, 0)), pl.BlockSpec((1, D_MODEL), lambda i: (0, 0))],
        out_specs=pl.BlockSpec((NORM_ROWS, D_MODEL), lambda i: (i, 0)),
        out_shape=jax.ShapeDtypeStruct((SEQ, D_MODEL), out_dtype),
        compiler_params=_cp("parallel"),
    )(x, w_row)


def _resnorm_fwd(x, f, w_row, name):
    def body(x_ref, f_ref, w_ref, o_ref):
        o_ref[...] = x_ref[...] + _rms(f_ref[...], w_ref[...])

    blk = pl.BlockSpec((NORM_ROWS, D_MODEL), lambda i: (i, 0))
    return pl.pallas_call(
        body, name=name, grid=(SEQ // NORM_ROWS,),
        in_specs=[blk, blk, pl.BlockSpec((1, D_MODEL), lambda i: (0, 0))],
        out_specs=blk, out_shape=jax.ShapeDtypeStruct((SEQ, D_MODEL), F32),
        compiler_params=_cp("parallel"),
    )(x, f, w_row)


def _norm_bwd(x, w_row, dy, add, name):
    has_add = add is not None

    def body(*refs):
        if has_add:
            x_ref, w_ref, dy_ref, add_ref, dx_ref, dw_ref = refs
        else:
            x_ref, w_ref, dy_ref, dx_ref, dw_ref = refs
        _, vjp = jax.vjp(_rms, x_ref[...], w_ref[...])
        dx, dw = vjp(dy_ref[...])
        dx_ref[...] = dx + add_ref[...] if has_add else dx

        @pl.when(pl.program_id(0) == 0)
        def _():
            dw_ref[...] = jnp.zeros_like(dw_ref)

        dw_ref[...] += dw

    blk = pl.BlockSpec((NORM_ROWS, D_MODEL), lambda i: (i, 0))
    row = pl.BlockSpec((1, D_MODEL), lambda i: (0, 0))
    ins = [x, w_row, dy] + ([add] if has_add else [])
    return pl.pallas_call(
        body, name=name, grid=(SEQ // NORM_ROWS,),
        in_specs=[blk, row, blk] + ([blk] if has_add else []),
        out_specs=[blk, row],
        out_shape=[jax.ShapeDtypeStruct((SEQ, D_MODEL), F32), jax.ShapeDtypeStruct((1, D_MODEL), F32)],
        compiler_params=_cp("arbitrary"),
    )(*ins)


def _loss_fwd_bwd(y, target):
    def body(y_ref, t_ref, loss_ref, dy_ref):
        err = y_ref[...] - t_ref[...]
        dy_ref[...] = err * (1.0 / D_MODEL)

        @pl.when(pl.program_id(0) == 0)
        def _():
            loss_ref[...] = jnp.zeros_like(loss_ref)

        part = jnp.sum(jnp.sum(err * err, axis=1, keepdims=True) * (1.0 / D_MODEL), axis=0, keepdims=True)
        loss_ref[...] += 0.5 * jnp.broadcast_to(part, loss_ref.shape)

    blk = pl.BlockSpec((NORM_ROWS, D_MODEL), lambda i: (i, 0))
    return pl.pallas_call(
        body, name="loss", grid=(SEQ // NORM_ROWS,),
        in_specs=[blk, blk],
        out_specs=[pl.BlockSpec((1, 128), lambda i: (0, 0)), blk],
        out_shape=[jax.ShapeDtypeStruct((1, 128), F32), jax.ShapeDtypeStruct((SEQ, D_MODEL), F32)],
        compiler_params=_cp("arbitrary"),
    )(y, target)


def _make_shift(j):
    def down(x):
        row = lax.broadcasted_iota(jnp.int32, x.shape, 0)
        return jnp.where(row >= j, pltpu.roll(x, j, 0), 0.0)

    def up(x):
        n = x.shape[0]
        row = lax.broadcasted_iota(jnp.int32, x.shape, 0)
        return jnp.where(row < n - j, pltpu.roll(x, n - j, 0), 0.0)

    f = jax.custom_vjp(down)
    f.defvjp(lambda x: (down(x), None), lambda _, g: (up(g),))
    return f


_SHIFT = {j: _make_shift(j) for j in (1, 2, 3)}


def _causal_conv(x, taps):
    n = len(taps)
    acc = x * taps[n - 1]
    for k in range(n - 1):
        acc = acc + _SHIFT[n - 1 - k](x) * taps[k]
    return acc


def _tap_rows(w_ref):
    return tuple(w_ref[k:k + 1, :] for k in range(w_ref.shape[0]))


def _sigmoid(x):
    return 1.0 / (1.0 + jnp.exp(-x))


def _silu(x):
    return x * _sigmoid(x)


def _softplus(x):
    return jnp.maximum(x, 0.0) + jnp.log(1.0 + jnp.exp(-jnp.abs(x)))


def _gelu_tanh(x):
    return 0.5 * x * (1.0 + jnp.tanh(math.sqrt(2.0 / math.pi) * (x + 0.044715 * (x * x * x))))


def _dnconv_fn(x, taps):
    return _silu(_causal_conv(x, taps))


DN_QKV_BLK0 = 3 * ATTN_W // 128
DN_QKV_BLKS = 1536 // 128


def _dnconv_fwd(proj, conv_w):
    def body(x_ref, w_ref, o_ref):
        o_ref[...] = _dnconv_fn(x_ref[...], _tap_rows(w_ref))

    return pl.pallas_call(
        body, name="dnconv_fwd", grid=(DN_QKV_BLKS,),
        in_specs=[pl.BlockSpec((SEQ, 128), lambda j: (0, DN_QKV_BLK0 + j)), pl.BlockSpec((4, 128), lambda j: (0, j))],
        out_specs=pl.BlockSpec((SEQ, 128), lambda j: (0, j)),
        out_shape=jax.ShapeDtypeStruct((SEQ, 1536), F32),
        compiler_params=_cp("parallel"),
    )(proj, conv_w)


def _dnconv_bwd(proj, conv_w, dc):
    def body(x_ref, w_ref, dc_ref, dx_ref, dw_ref):
        _, vjp = jax.vjp(_dnconv_fn, x_ref[...], _tap_rows(w_ref))
        dx, dw = vjp(dc_ref[...])
        dx_ref[...] = dx
        for k, row in enumerate(dw):
            dw_ref[k:k + 1, :] = row

    return pl.pallas_call(
        body, name="dnconv_bwd", grid=(DN_QKV_BLKS,),
        in_specs=[pl.BlockSpec((SEQ, 128), lambda j: (0, DN_QKV_BLK0 + j)), pl.BlockSpec((4, 128), lambda j: (0, j)),
                  pl.BlockSpec((SEQ, 128), lambda j: (0, j))],
        out_specs=[pl.BlockSpec((SEQ, 128), lambda j: (0, j)), pl.BlockSpec((4, 128), lambda j: (0, j))],
        out_shape=[jax.ShapeDtypeStruct((SEQ, 1536), F32), jax.ShapeDtypeStruct((4, 1536), F32)],
        compiler_params=_cp("parallel"),
    )(proj, conv_w, dc)


FF_BLKS = D_FF // 128


def _ffact_fn(pg, pu, wg, wu, bg, bu):
    return _gelu_tanh(_causal_conv(pg, wg) + bg) * (_causal_conv(pu, wu) + bu)


def _ffact_specs():
    col = lambda off: pl.BlockSpec((SEQ, 128), lambda j, off=off: (0, off + j))
    w = lambda off: pl.BlockSpec((3, 128), lambda j, off=off: (0, off + j))
    b = lambda off: pl.BlockSpec((1, 128), lambda j, off=off: (0, off + j))
    return [col(0), col(FF_BLKS), w(0), w(FF_BLKS), b(0), b(FF_BLKS)]


def _ffact_fwd(pre, conv_w, conv_b):
    def body(pg_ref, pu_ref, wg_ref, wu_ref, bg_ref, bu_ref, o_ref):
        o_ref[...] = _ffact_fn(pg_ref[...], pu_ref[...], _tap_rows(wg_ref), _tap_rows(wu_ref), bg_ref[...],
                               bu_ref[...]).astype(BF16)

    return pl.pallas_call(
        body, name="ffact_fwd", grid=(FF_BLKS,),
        in_specs=_ffact_specs(),
        out_specs=pl.BlockSpec((SEQ, 128), lambda j: (0, j)),
        out_shape=jax.ShapeDtypeStruct((SEQ, D_FF), BF16),
        compiler_params=_cp("parallel"),
    )(pre, pre, conv_w, conv_w, conv_b, conv_b)


def _ffact_bwd(pre, conv_w, conv_b, dact):
    def body(pg_ref, pu_ref, wg_ref, wu_ref, bg_ref, bu_ref, da_ref, dpg_ref, dpu_ref, dw_ref, db_ref):
        _, vjp = jax.vjp(_ffact_fn, pg_ref[...], pu_ref[...], _tap_rows(wg_ref), _tap_rows(wu_ref), bg_ref[...],
                         bu_ref[...])
        dpg, dpu, dwg, dwu, dbg, dbu = vjp(da_ref[...].astype(F32))
        dpg_ref[...] = dpg
        dpu_ref[...] = dpu
        for k in range(3):
            dw_ref[0, k:k + 1, :] = dwg[k]
            dw_ref[1, k:k + 1, :] = dwu[k]
        db_ref[0] = dbg
        db_ref[1] = dbu

    col = pl.BlockSpec((SEQ, 128), lambda j: (0, j))
    return pl.pallas_call(
        body, name="ffact_bwd", grid=(FF_BLKS,),
        in_specs=_ffact_specs() + [col],
        out_specs=[col, col, pl.BlockSpec((2, 3, 128), lambda j: (0, 0, j)), pl.BlockSpec((2, 1, 128), lambda j: (0, 0, j))],
        out_shape=[jax.ShapeDtypeStruct((SEQ, D_FF), F32), jax.ShapeDtypeStruct((SEQ, D_FF), F32),
                   jax.ShapeDtypeStruct((2, 3, D_FF), F32), jax.ShapeDtypeStruct((2, 1, D_FF), F32)],
        compiler_params=_cp("parallel"),
    )(pre, pre, conv_w, conv_w, conv_b, conv_b, dact)


ROPE_ROWS = 512


def _rope_tables():
    inv = 1.0 / (ROPE_THETA ** (jnp.arange(0, HEAD_DIM, 2, dtype=F32) / HEAD_DIM))
    ang = jnp.arange(SEQ, dtype=F32)[:, None] * inv[None, :]
    cos = jnp.tile(jnp.cos(ang), (1, 4))
    sin = jnp.tile(jnp.sin(ang), (1, 4))
    sign = jnp.where((jnp.arange(128) % HEAD_DIM) < HEAD_DIM // 2, -1.0, 1.0).astype(F32)
    return cos, sin * sign[None, :]


def _rope(x, cos, sin_signed):
    lane = lax.broadcasted_iota(jnp.int32, x.shape, 1)
    partner = jnp.where((lane % HEAD_DIM) < HEAD_DIM // 2, pltpu.roll(x, 128 - HEAD_DIM // 2, 1),
                        pltpu.roll(x, HEAD_DIM // 2, 1))
    return x * cos + partner * sin_signed


def _rope_qk(src, q_blk0, k_blk0, cos, sin_signed, name):
    def body(q_ref, k_ref, c_ref, s_ref, qo_ref, ko_ref):
        qo_ref[...] = _rope(q_ref[...], c_ref[...], s_ref[...])
        ko_ref[...] = _rope(k_ref[...], c_ref[...], s_ref[...])

    tab = pl.BlockSpec((ROPE_ROWS, 128), lambda i, j: (i, 0))
    out = pl.BlockSpec((ROPE_ROWS, 128), lambda i, j: (i, j))
    return pl.pallas_call(
        body, name=name, grid=(SEQ // ROPE_ROWS, N_PAIR),
        in_specs=[pl.BlockSpec((ROPE_ROWS, 128), lambda i, j: (i, q_blk0 + j)),
                  pl.BlockSpec((ROPE_ROWS, 128), lambda i, j: (i, k_blk0 + j)), tab, tab],
        out_specs=[out, out],
        out_shape=[jax.ShapeDtypeStruct((SEQ, ATTN_W), F32)] * 2,
        compiler_params=_cp("parallel", "parallel"),
    )(src, src, cos, sin_signed)


N_BLK = SEQ // ATTN_BLK


def _to_branches(t):
    out = []
    for d in DILATIONS:
        out.append(t if d == 1 else t.reshape(SEQ // d, d, t.shape[1]).transpose(1, 0, 2).reshape(SEQ, t.shape[1]))
    return jnp.stack(out)


def _from_branches(t3):
    out = []
    for b, d in enumerate(DILATIONS):
        t = t3[b]
        out.append(t if d == 1 else t.reshape(d, SEQ // d, t.shape[1]).transpose(1, 0, 2).reshape(SEQ, t.shape[1]))
    return out


def _has_prev(branch, blk):
    return jnp.where(branch == 0, blk > 0, jnp.where(branch == 1, (blk % SEGMENT_BLOCKS[1]) != 0, False))


def _band_masks():
    a = lax.broadcasted_iota(jnp.int32, (ATTN_BLK, ATTN_BLK), 0)
    c = lax.broadcasted_iota(jnp.int32, (ATTN_BLK, ATTN_BLK), 1)
    return c >= a, c <= a


def _head_masks():
    lane = lax.broadcasted_iota(jnp.int32, (1, 128), 1)
    return [(lane // HEAD_DIM) == h for h in range(2)]


def _attn_fwd(q3, k3, v3):
    scale = HEAD_DIM ** -0.5

    def body(q_ref, kp_ref, kc_ref, vp_ref, vc_ref, o_ref, lse_ref):
        prev_ok = _has_prev(pl.program_id(0), pl.program_id(2))
        m_prev, m_cur = _band_masks()
        m_prev = m_prev & prev_ok
        q, kp, kc, vp, vc = q_ref[0], kp_ref[0], kc_ref[0], vp_ref[0], vc_ref[0]
        outs, lses = [], []
        for hm in _head_masks():
            qh = jnp.where(hm, q, 0.0)
            sp = jnp.where(m_prev, MM_NT(qh, kp) * scale, NEG)
            sc = jnp.where(m_cur, MM_NT(qh, kc) * scale, NEG)
            m = jnp.maximum(jnp.max(sp, axis=1, keepdims=True), jnp.max(sc, axis=1, keepdims=True))
            ep = jnp.exp(sp - m)
            ec = jnp.exp(sc - m)
            l = jnp.sum(ep, axis=1, keepdims=True) + jnp.sum(ec, axis=1, keepdims=True)
            outs.append((MM(ep, vp) + MM(ec, vc)) / l)
            lses.append(m + jnp.log(l))
        h0 = _head_masks()[0]
        o_ref[0] = jnp.where(h0, outs[0], outs[1])
        lse_ref[0] = jnp.where(h0, lses[0], lses[1])

    cur = pl.BlockSpec((1, ATTN_BLK, 128), lambda b, j, i: (b, i, j))
    prev = pl.BlockSpec((1, ATTN_BLK, 128), lambda b, j, i: (b, jnp.maximum(i - 1, 0), j))
    return pl.pallas_call(
        body, name="attn_fwd", grid=(3, N_PAIR, N_BLK),
        in_specs=[cur, prev, cur, prev, cur],
        out_specs=[cur, cur],
        out_shape=[jax.ShapeDtypeStruct((3, SEQ, ATTN_W), F32)] * 2,
        compiler_params=_cp("parallel", "parallel", "arbitrary"),
    )(q3, k3, k3, v3, v3)


def _attn_combine(o3, lse3):
    def body(o_ref, l_ref, out_ref, lse_ref):
        l0, l1, l2 = l_ref[0], l_ref[1], l_ref[2]
        m = jnp.maximum(jnp.maximum(l0, l1), l2)
        w0, w1, w2 = jnp.exp(l0 - m), jnp.exp(l1 - m), jnp.exp(l2 - m)
        den = w0 + w1 + w2
        out_ref[...] = (w0 * o_ref[0] + w1 * o_ref[1] + w2 * o_ref[2]) / den
        lse_ref[...] = m + jnp.log(den)

    b3 = pl.BlockSpec((3, 256, ATTN_W), lambda i: (0, i, 0))
    b1 = pl.BlockSpec((256, ATTN_W), lambda i: (i, 0))
    return pl.pallas_call(
        body, name="attn_combine", grid=(SEQ // 256,),
        in_specs=[b3, b3], out_specs=[b1, b1],
        out_shape=[jax.ShapeDtypeStruct((SEQ, ATTN_W), F32)] * 2,
        compiler_params=_cp("parallel"),
    )(o3, lse3)


def _attn_bwd(q3, k3, v3, out3, lse3, do3):
    scale = HEAD_DIM ** -0.5

    def body(q_ref, kp_ref, kc_ref, vp_ref, vc_ref, out_ref, lse_ref, do_ref,
             dq_ref, dk_ref, dv_ref, carry_k, carry_v):
        step = pl.program_id(2)
        blk = N_BLK - 1 - step
        prev_ok = _has_prev(pl.program_id(0), blk)
        m_prev, m_cur = _band_masks()
        m_prev = m_prev & prev_ok
        q, kp, kc, vp, vc = q_ref[0], kp_ref[0], kc_ref[0], vp_ref[0], vc_ref[0]
        do, lse = do_ref[0], lse_ref[0]
        dod = do * out_ref[0]
        heads = _head_masks()
        dq = []
        dk_cur = jnp.zeros((ATTN_BLK, 128), F32)
        dv_cur = jnp.zeros((ATTN_BLK, 128), F32)
        dk_prev = jnp.zeros((ATTN_BLK, 128), F32)
        dv_prev = jnp.zeros((ATTN_BLK, 128), F32)
        for hm in heads:
            qh = jnp.where(hm, q, 0.0)
            doh = jnp.where(hm, do, 0.0)
            lse_h = jnp.max(jnp.where(hm, lse, NEG), axis=1, keepdims=True)
            delta = jnp.sum(jnp.where(hm, dod, 0.0), axis=1, keepdims=True)
            pp = jnp.exp(jnp.where(m_prev, MM_NT(qh, kp) * scale, NEG) - lse_h)
            pc = jnp.exp(jnp.where(m_cur, MM_NT(qh, kc) * scale, NEG) - lse_h)
            dsp = pp * (MM_NT(doh, vp) - delta) * scale
            dsc = pc * (MM_NT(doh, vc) - delta) * scale
            dq.append(MM(dsp, kp) + MM(dsc, kc))
            dk_prev += MM_TN(dsp, qh)
            dk_cur += MM_TN(dsc, qh)
            dv_prev += MM_TN(pp, doh)
            dv_cur += MM_TN(pc, doh)

        @pl.when(step == 0)
        def _():
            carry_k[...] = jnp.zeros_like(carry_k)
            carry_v[...] = jnp.zeros_like(carry_v)

        dq_ref[0] = jnp.where(heads[0], dq[0], dq[1])
        dk_ref[0] = dk_cur + carry_k[...]
        dv_ref[0] = dv_cur + carry_v[...]
        carry_k[...] = dk_prev
        carry_v[...] = dv_prev

    cur = pl.BlockSpec((1, ATTN_BLK, 128), lambda b, j, s: (b, N_BLK - 1 - s, j))
    prev = pl.BlockSpec((1, ATTN_BLK, 128), lambda b, j, s: (b, jnp.maximum(N_BLK - 2 - s, 0), j))
    return pl.pallas_call(
        body, name="attn_bwd", grid=(3, N_PAIR, N_BLK),
        in_specs=[cur, prev, cur, prev, cur, cur, cur, cur],
        out_specs=[cur, cur, cur],
        out_shape=[jax.ShapeDtypeStruct((3, SEQ, ATTN_W), F32)] * 3,
        scratch_shapes=[pltpu.VMEM((ATTN_BLK, 128), F32)] * 2,
        compiler_params=_cp("arbitrary", "arbitrary", "arbitrary"),
    )(q3, k3, k3, v3, v3, out3, lse3, do3)


DN_Z_BLK0 = 3072 // 128
DN_TAIL_BLK = 3584 // 128


def _delta_chunk(head, qr, kr, vr, z, tail, alog_row, dt_row, nw, state):
    c = qr.shape[0]
    lane = lax.broadcasted_iota(jnp.int32, (1, 128), 1)
    sel_b = (lane == head).astype(F32)
    sel_a = (lane == head + NDH).astype(F32)
    b_raw = jnp.sum(tail * sel_b, axis=1, keepdims=True)
    a_raw = jnp.sum(tail * sel_a, axis=1, keepdims=True)
    a_log = jnp.sum(alog_row * sel_b, axis=1, keepdims=True)
    dt_b = jnp.sum(dt_row * sel_b, axis=1, keepdims=True)
    beta = _sigmoid(b_raw)
    g = -jnp.exp(a_log) * _softplus(a_raw + dt_b)

    q = qr * lax.rsqrt(jnp.sum(qr * qr, axis=1, keepdims=True) + EPS) * (128 ** -0.5)
    k = kr * lax.rsqrt(jnp.sum(kr * kr, axis=1, keepdims=True) + EPS)

    ri = lax.broadcasted_iota(jnp.int32, (c, c), 0)
    ci = lax.broadcasted_iota(jnp.int32, (c, c), 1)
    tril = ri >= ci
    eye = (ri == ci).astype(F32)
    gc = MMH(tril.astype(F32), g)
    g_row = MMH(jnp.ones((c, c), F32), eye * gc)
    decay = jnp.where(tril, jnp.exp(jnp.where(tril, gc - g_row, 0.0)), 0.0)
    kb = k * beta
    a_mat = jnp.where(ri > ci, MM_NT(kb, k) * decay, 0.0)
    power = -a_mat
    t_inv = eye + power
    for _ in range(5):
        power = MMH(power, power)
        t_inv = t_inv + MMH(t_inv, power)
    eg = jnp.exp(gc)
    u = MM(t_inv, vr * beta)
    w = MM(t_inv, kb * eg)
    qk = MM_NT(q, k) * decay
    g_tot = jnp.sum(g, axis=0, keepdims=True)
    v_new = u - MM(w, state)
    o = MM(q * eg, state) + MM(qk, v_new)
    new_state = state * jnp.exp(g_tot) + MM_TN(k * jnp.exp(g_tot - gc), v_new)
    return _rms(o, nw) * _silu(z), new_state


def _delta_fwd(c_qkv, proj, alog_row, dt_row, nw):
    def body(q_ref, k_ref, v_ref, z_ref, tail_ref, al_ref, dt_ref, nw_ref, y_ref, st_ref, state):
        @pl.when(pl.program_id(1) == 0)
        def _():
            state[...] = jnp.zeros_like(state)

        st_ref[0, 0] = state[...]
        y, new_state = _delta_chunk(pl.program_id(0), q_ref[...], k_ref[...], v_ref[...], z_ref[...], tail_ref[...],
                                    al_ref[...], dt_ref[...], nw_ref[...], state[...])
        y_ref[...] = y
        state[...] = new_state

    blk = lambda off: pl.BlockSpec((CH, 128), lambda h, n, off=off: (n, off + h))
    row = pl.BlockSpec((1, 128), lambda h, n: (0, 0))
    return pl.pallas_call(
        body, name="delta_fwd", grid=(NDH, NCH),
        in_specs=[blk(0), blk(4), blk(8), blk(DN_Z_BLK0), pl.BlockSpec((CH, 128), lambda h, n: (n, DN_TAIL_BLK)),
                  row, row, row],
        out_specs=[pl.BlockSpec((CH, 128), lambda h, n: (n, h)),
                   pl.BlockSpec((1, 1, 128, 128), lambda h, n: (h, n, 0, 0))],
        out_shape=[jax.ShapeDtypeStruct((SEQ, 512), F32), jax.ShapeDtypeStruct((NDH, NCH, 128, 128), F32)],
        scratch_shapes=[pltpu.VMEM((128, 128), F32)],
        compiler_params=_cp("arbitrary", "arbitrary"),
    )(c_qkv, c_qkv, c_qkv, proj, proj, alog_row, dt_row, nw)


def _delta_bwd(c_qkv, proj, alog_row, dt_row, nw, states, dcat):
    def body(q_ref, k_ref, v_ref, z_ref, tail_ref, al_ref, dt_ref, nw_ref, st_ref, dy_ref,
             dq_ref, dk_ref, dv_ref, dz_ref, dtail_ref, dal_ref, ddt_ref, dnw_ref, dstate):
        h = pl.program_id(0)
        n = pl.program_id(1)

        @pl.when(n == 0)
        def _():
            dstate[...] = jnp.zeros_like(dstate)

        @pl.when((n == 0) & (h == 0))
        def _():
            dal_ref[...] = jnp.zeros_like(dal_ref)
            ddt_ref[...] = jnp.zeros_like(ddt_ref)
            dnw_ref[...] = jnp.zeros_like(dnw_ref)

        _, vjp = jax.vjp(functools.partial(_delta_chunk, h), q_ref[...], k_ref[...], v_ref[...], z_ref[...],
                         tail_ref[...], al_ref[...], dt_ref[...], nw_ref[...], st_ref[0, 0])
        dq, dk, dv, dz, dtail, dal, ddt, dnw, dst = vjp((dy_ref[...], dstate[...]))
        dq_ref[...] = dq
        dk_ref[...] = dk
        dv_ref[...] = dv
        dz_ref[...] = dz
        dtail_ref[0] = dtail
        dal_ref[...] += dal
        ddt_ref[...] += ddt
        dnw_ref[...] += dnw
        dstate[...] = dst

    rev = lambda n: NCH - 1 - n
    blk = lambda off: pl.BlockSpec((CH, 128), lambda h, n, off=off: (rev(n), off + h))
    row = pl.BlockSpec((1, 128), lambda h, n: (0, 0))
    oblk = pl.BlockSpec((CH, 128), lambda h, n: (rev(n), h))
    return pl.pallas_call(
        body, name="delta_bwd", grid=(NDH, NCH),
        in_specs=[blk(0), blk(4), blk(8), blk(DN_Z_BLK0), pl.BlockSpec((CH, 128), lambda h, n: (rev(n), DN_TAIL_BLK)),
                  row, row, row, pl.BlockSpec((1, 1, 128, 128), lambda h, n: (h, rev(n), 0, 0)), blk(4)],
        out_specs=[oblk, oblk, oblk, oblk, pl.BlockSpec((1, CH, 128), lambda h, n: (h, rev(n), 0)), row, row, row],
        out_shape=[jax.ShapeDtypeStruct((SEQ, 512), F32)] * 4 + [jax.ShapeDtypeStruct((NDH, SEQ, 128), F32)]
        + [jax.ShapeDtypeStruct((1, 128), F32)] * 3,
        scratch_shapes=[pltpu.VMEM((128, 128), F32)],
        compiler_params=_cp("arbitrary", "arbitrary"),
    )(c_qkv, c_qkv, c_qkv, proj, proj, alog_row, dt_row, nw, states, dcat)


def _place():
    x, y, c = lax.axis_index("x"), lax.axis_index("y"), lax.axis_index("c")
    other_chips = [(1 - x, y), (x, 1 - y), (1 - x, 1 - y)]
    return x, y, c, other_chips


HBM_SPEC = pl.BlockSpec(memory_space=pltpu.HBM)


def _all_gather_hbm(shards, name):
    n = len(shards)

    def body(*refs):
        ins, outs = refs[:n], refs[n:2 * n]
        send_sems, recv_sems, local_sems = refs[2 * n:]
        x, y, c, chips = _place()
        me, sibling = (x, y, c), (x, y, 1 - c)

        def copy(b, k, block, to, src=None):
            slot = outs[b].at[4 * block[0] + 2 * block[1] + block[2]]
            return pltpu.make_async_remote_copy(
                src_ref=slot if src is None else src, dst_ref=slot,
                send_sem=send_sems.at[b, k], recv_sem=recv_sems.at[b, k], device_id=to, device_id_type=MESH)

        mine = [pltpu.make_async_copy(ins[b], outs[b].at[4 * x + 2 * y + c], local_sems.at[b]) for b in range(n)]
        for cp in mine:
            cp.start()
        first = []
        for b in range(n):
            first.append(copy(b, 0, me, sibling, src=ins[b]))
            first += [copy(b, 1 + j, me, (*chip, c), src=ins[b]) for j, chip in enumerate(chips)]
        for cp in first:
            cp.start()
        passed = []
        for b in range(n):
            for j, chip in enumerate(chips):
                copy(b, 1 + j, (*chip, c), me).wait_recv()
                fwd = copy(b, 4 + j, (*chip, c), sibling)
                fwd.start()
                passed.append(fwd)
        for b in range(n):
            copy(b, 0, sibling, me).wait_recv()
            for j, chip in enumerate(chips):
                copy(b, 4 + j, (*chip, 1 - c), me).wait_recv()
        for cp in first + passed:
            cp.wait_send()
        for cp in mine:
            cp.wait()

    return pl.pallas_call(
        body, name=name,
        in_specs=[HBM_SPEC] * n, out_specs=[HBM_SPEC] * n,
        out_shape=[jax.ShapeDtypeStruct((N_DEV,) + s.shape, s.dtype) for s in shards],
        scratch_shapes=[pltpu.SemaphoreType.DMA((n, 7)), pltpu.SemaphoreType.DMA((n, 7)), pltpu.SemaphoreType.DMA((n,))],
    )(*shards)


def _exchange_sibling(gs, name):
    n = len(gs)

    def body(*refs):
        ins, outs = refs[:n], refs[n:2 * n]
        send_sems, recv_sems = refs[2 * n:]
        x, y, c, _ = _place()
        copies = []
        for b in range(n):
            for p in range(4):
                copies.append(pltpu.make_async_remote_copy(
                    src_ref=ins[b].at[2 * p + (1 - c)], dst_ref=outs[b].at[p],
                    send_sem=send_sems.at[b, p], recv_sem=recv_sems.at[b, p],
                    device_id=(x, y, 1 - c), device_id_type=MESH))
        for cp in copies:
            cp.start()
        for cp in copies:
            cp.wait()

    return pl.pallas_call(
        body, name=name,
        in_specs=[HBM_SPEC] * n, out_specs=[HBM_SPEC] * n,
        out_shape=[jax.ShapeDtypeStruct((4,) + g.shape[1:], g.dtype) for g in gs],
        scratch_shapes=[pltpu.SemaphoreType.DMA((n, 4)), pltpu.SemaphoreType.DMA((n, 4))],
    )(*gs)


def _exchange_chips(hs, name):
    n = len(hs)

    def body(*refs):
        ins, outs = refs[:n], refs[n:2 * n]
        send_sems, recv_sems, local_sems = refs[2 * n:]
        x, y, c, chips = _place()
        my_chip = 2 * x + y
        local = [pltpu.make_async_copy(ins[b].at[my_chip], outs[b].at[my_chip], local_sems.at[b]) for b in range(n)]
        for cp in local:
            cp.start()
        copies = []
        for b in range(n):
            for k, (px, py) in enumerate(chips):
                copies.append(pltpu.make_async_remote_copy(
                    src_ref=ins[b].at[2 * px + py], dst_ref=outs[b].at[my_chip],
                    send_sem=send_sems.at[b, k], recv_sem=recv_sems.at[b, k],
                    device_id=(px, py, c), device_id_type=MESH))
        for cp in copies:
            cp.start()
        for b in range(n):
            for k, (px, py) in enumerate(chips):
                pltpu.make_async_remote_copy(
                    src_ref=ins[b].at[2 * px + py], dst_ref=outs[b].at[2 * px + py],
                    send_sem=send_sems.at[b, k], recv_sem=recv_sems.at[b, k],
                    device_id=(px, py, c), device_id_type=MESH).wait_recv()
        for cp in copies:
            cp.wait_send()
        for cp in local:
            cp.wait()

    return pl.pallas_call(
        body, name=name,
        in_specs=[HBM_SPEC] * n, out_specs=[HBM_SPEC] * n,
        out_shape=[jax.ShapeDtypeStruct(h.shape, h.dtype) for h in hs],
        scratch_shapes=[pltpu.SemaphoreType.DMA((n, 3)), pltpu.SemaphoreType.DMA((n, 3)), pltpu.SemaphoreType.DMA((n,))],
    )(*hs)


def _pair_add(g, r, core, name):
    _, nl, nr, nc = g.shape
    tr = nr // 2 if nr % 16 == 0 else nr

    def body(core_ref, g_ref, r_ref, o_ref):
        o_ref[...] = g_ref[...] + r_ref[...]

    return pl.pallas_call(
        body, name=name,
        grid_spec=pltpu.PrefetchScalarGridSpec(
            num_scalar_prefetch=1, grid=(4, nl, nr // tr),
            in_specs=[pl.BlockSpec((1, 1, tr, nc), lambda p, l, i, core: (2 * p + core[0], l, i, 0)),
                      pl.BlockSpec((1, 1, tr, nc), lambda p, l, i, core: (p, l, i, 0))],
            out_specs=pl.BlockSpec((1, 1, tr, nc), lambda p, l, i, core: (p, l, i, 0))),
        out_shape=jax.ShapeDtypeStruct(r.shape, F32),
        compiler_params=_cp("parallel", "parallel", "parallel"),
    )(core, g, r)


def _all_gather_sum_small(v):
    rows = v.shape[0]

    def body(x_ref, sum_ref, out_ref, send_sems, recv_sems, local_sem):
        x, y, c, chips = _place()
        me, sibling = (x, y, c), (x, y, 1 - c)

        def block(px, py, pc):
            return out_ref.at[pl.ds((4 * px + 2 * py + pc) * rows, rows), :]

        def copy(k, blk, to, src=None):
            return pltpu.make_async_remote_copy(
                src_ref=block(*blk) if src is None else src, dst_ref=block(*blk),
                send_sem=send_sems.at[k], recv_sem=recv_sems.at[k], device_id=to, device_id_type=MESH)

        mine = pltpu.make_async_copy(x_ref, block(*me), local_sem)
        mine.start()
        first = [copy(0, me, sibling, src=x_ref)]
        first += [copy(1 + j, me, (*chip, c), src=x_ref) for j, chip in enumerate(chips)]
        for cp in first:
            cp.start()
        passed = [copy(4 + j, (*chip, c), sibling) for j, chip in enumerate(chips)]
        for j, chip in enumerate(chips):
            copy(1 + j, (*chip, c), me).wait_recv()
            passed[j].start()
        copy(0, sibling, me).wait_recv()
        for j, chip in enumerate(chips):
            copy(4 + j, (*chip, 1 - c), me).wait_recv()
        for cp in first + passed:
            cp.wait_send()
        mine.wait()
        total = out_ref[pl.ds(0, rows), :]
        for d in range(1, N_DEV):
            total = total + out_ref[pl.ds(d * rows, rows), :]
        sum_ref[...] = total

    vm = pl.BlockSpec(memory_space=pltpu.VMEM)
    return pl.pallas_call(
        body, name="small_all_reduce",
        in_specs=[vm], out_specs=[vm],
        out_shape=[jax.ShapeDtypeStruct((rows, 128), F32)],
        scratch_shapes=[pltpu.VMEM((N_DEV * rows, 128), F32), pltpu.SemaphoreType.DMA((7,)),
                        pltpu.SemaphoreType.DMA((7,)), pltpu.SemaphoreType.DMA],
    )(v)[0]


def _adamw(w, g, m, v):
    m = ADAM_B1 * m + (1.0 - ADAM_B1) * g
    v = ADAM_B2 * v + (1.0 - ADAM_B2) * (g * g)
    m_hat = m / (1.0 - ADAM_B1 ** ADAM_STEP)
    v_hat = v / (1.0 - ADAM_B2 ** ADAM_STEP)
    delta = -ADAM_LR * (m_hat / (jnp.sqrt(v_hat) + ADAM_EPS) + ADAM_WD * w)
    return delta, m, v


def _adamw_sharded(parts, w, m, v, name):
    nl, nr, nc = w.shape
    tr = nr // 4 if nr % 32 == 0 else nr

    def body(p_ref, w_ref, m_ref, v_ref, g_ref, d_ref, nm_ref, nv_ref):
        g = (p_ref[0] + p_ref[1]) + (p_ref[2] + p_ref[3])
        delta, nm, nv = _adamw(w_ref[...], g, m_ref[...], v_ref[...])
        g_ref[...] = g
        d_ref[...] = delta
        nm_ref[...] = nm
        nv_ref[...] = nv

    blk = pl.BlockSpec((1, tr, nc), lambda l, i: (l, i, 0))
    return pl.pallas_call(
        body, name=name, grid=(nl, nr // tr),
        in_specs=[pl.BlockSpec((4, 1, tr, nc), lambda l, i: (0, l, i, 0)), blk, blk, blk],
        out_specs=[blk] * 4,
        out_shape=[jax.ShapeDtypeStruct(w.shape, F32)] * 4,
        compiler_params=_cp("parallel", "parallel"),
    )(parts, w, m, v)


def _adamw_small(g, w, m, v):
    def body(g_ref, w_ref, m_ref, v_ref, d_ref, nm_ref, nv_ref):
        delta, nm, nv = _adamw(w_ref[...], g_ref[...], m_ref[...], v_ref[...])
        d_ref[...] = delta
        nm_ref[...] = nm
        nv_ref[...] = nv

    return pl.pallas_call(
        body, name="adamw_small",
        out_shape=[jax.ShapeDtypeStruct(g.shape, F32)] * 3,
    )(g, w, m, v)


def _pack(arrays, rows):
    flat = jnp.concatenate([a.reshape(-1).astype(F32) for a in arrays])
    return jnp.pad(flat, (0, rows * 128 - flat.shape[0])).reshape(rows, 128)


def _unpack(packed, shapes):
    flat = packed.reshape(-1)
    out, off = [], 0
    for s in shapes:
        n = math.prod(s)
        out.append(flat[off:off + n].reshape(s))
        off += n
    return out


def _row(v, width=None):
    v = v.reshape(1, -1)
    return v if width is None else jnp.pad(v, ((0, 0), (0, width - v.shape[1])))


def _layer_fwd(x, wts, tables):
    cos, sin_s = tables
    h = _norm_fwd(x, wts["norm_pre_mix"], "norm_pre_mix")
    proj = _matmul(h, wts["w_in"], tm=512, tn=768, tk=1024, name="mm_proj")
    qr, kr = _rope_qk(proj, 0, N_PAIR, cos, sin_s, "rope_fwd")
    q3, k3, v3 = _to_branches(qr), _to_branches(kr), _to_branches(proj[:, 2 * ATTN_W:3 * ATTN_W])
    o3, lse3 = _attn_fwd(q3, k3, v3)
    attn, lse = _attn_combine(jnp.stack(_from_branches(o3)), jnp.stack(_from_branches(lse3)))
    c_qkv = _dnconv_fwd(proj, wts["dn_conv_w"])
    dn, states = _delta_fwd(c_qkv, proj, wts["dn_a_log"], wts["dn_dt_bias"], wts["dn_norm_w"])
    cat = jnp.concatenate([attn, dn], axis=1)
    mix = _matmul(cat, wts["w_out"], tm=512, tn=1024, tk=1024, name="mm_mix")
    x1 = _resnorm_fwd(x, mix, wts["norm_post_mix"], "norm_post_mix")
    h2 = _norm_fwd(x1, wts["norm_pre_ffn"], "norm_pre_ffn")
    pre = _matmul(h2, wts["ffn_w_in"], tm=512, tn=512, tk=1024, name="mm_ffn_in")
    act = _ffact_fwd(pre, wts["ffn_conv_w"], wts["ffn_conv_b"])
    f = _matmul(act, wts["ffn_w_out"], tm=512, tn=1024, tk=1408, name="mm_ffn_out")
    x2 = _resnorm_fwd(x1, f, wts["norm_post_ffn"], "norm_post_ffn")
    saved = dict(x=x, h=h, proj=proj, q3=q3, k3=k3, v3=v3, attn=attn, lse=lse, c_qkv=c_qkv, states=states,
                 cat=cat, mix=mix, x1=x1, h2=h2, pre=pre, act=act, f=f)
    return x2, saved


def _layer_bwd(dx2, wts, s, tables):
    cos, sin_s = tables
    g = {}
    df, g["norm_post_ffn"] = _norm_bwd(s["f"], wts["norm_post_ffn"], dx2, None, "norm_post_ffn_bwd")
    dact = _matmul(df, wts["ffn_w_out"], tb=True, tm=512, tn=1408, tk=1024, name="mm_dact", out_dtype=F32)
    g["ffn_w_out"] = _matmul(s["act"], df, ta=True, tm=1408, tn=1024, tk=512, name="mm_dw_ffn_out")
    dpg, dpu, dcw, dcb = _ffact_bwd(s["pre"], wts["ffn_conv_w"], wts["ffn_conv_b"], dact)
    g["ffn_conv_w"] = jnp.concatenate([dcw[0], dcw[1]], axis=1)
    g["ffn_conv_b"] = jnp.concatenate([dcb[0], dcb[1]], axis=1)
    dpre = jnp.concatenate([dpg, dpu], axis=1)
    dh2 = _matmul(dpre, wts["ffn_w_in"], tb=True, tm=512, tn=1024, tk=512, name="mm_dh2")
    g["ffn_w_in"] = _matmul(s["h2"], dpre, ta=True, tm=1024, tn=512, tk=512, name="mm_dw_ffn_in")
    dx1, g["norm_pre_ffn"] = _norm_bwd(s["x1"], wts["norm_pre_ffn"], dh2, dx2, "norm_pre_ffn_bwd")
    dmix, g["norm_post_mix"] = _norm_bwd(s["mix"], wts["norm_post_mix"], dx1, None, "norm_post_mix_bwd")
    dcat = _matmul(dmix, wts["w_out"], tb=True, tm=512, tn=1024, tk=1024, name="mm_dcat")
    g["w_out"] = _matmul(s["cat"], dmix, ta=True, tm=1024, tn=1024, tk=512, name="mm_dw_out")
    do3 = _to_branches(dcat[:, :ATTN_W])
    dq3, dk3, dv3 = _attn_bwd(s["q3"], s["k3"], s["v3"], _to_branches(s["attn"]), _to_branches(s["lse"]), do3)
    dqr = sum(_from_branches(dq3))
    dkr = sum(_from_branches(dk3))
    dv = sum(_from_branches(dv3))
    dq, dk = _rope_qk(jnp.concatenate([dqr, dkr], axis=1), 0, N_PAIR, cos, -sin_s, "rope_bwd")
    ddq, ddk, ddv, dz, dtail, g["dn_a_log"], g["dn_dt_bias"], g["dn_norm_w"] = _delta_bwd(
        s["c_qkv"], s["proj"], wts["dn_a_log"], wts["dn_dt_bias"], wts["dn_norm_w"], s["states"], dcat)
    dqkv, g["dn_conv_w"] = _dnconv_bwd(s["proj"], wts["dn_conv_w"], jnp.concatenate([ddq, ddk, ddv], axis=1))
    tail = jnp.sum(dtail, axis=0)
    dproj = jnp.concatenate([dq, dk, dv, dqkv, dz, tail, jnp.zeros((SEQ, IN_PAD - IN_COLS - 120), F32)], axis=1)
    dh = _matmul(dproj, wts["w_in"], tb=True, tm=512, tn=1024, tk=768, name="mm_dh")
    g["w_in"] = _matmul(s["h"], dproj, ta=True, tm=1024, tn=768, tk=512, name="mm_dw_in")
    dx, g["norm_pre_mix"] = _norm_bwd(s["x"], wts["norm_pre_mix"], dh, dx1, "norm_pre_mix_bwd")
    return dx, g


BIG = ("w_in", "w_out", "ffn_w_in", "ffn_w_out")
SMALL_SHARDED = ("dn_conv_w", "ffn_conv_w")
REPLICATED = ("dn_a_log", "dn_dt_bias", "dn_norm_w", "ffn_conv_b", "norm_pre_mix", "norm_post_mix", "norm_pre_ffn",
              "norm_post_ffn")
WEIGHTS = ("w_in", "dn_conv_w", "dn_a_log", "dn_dt_bias", "dn_norm_w", "w_out", "ffn_w_in", "ffn_conv_w", "ffn_conv_b",
           "ffn_w_out", "norm_pre_mix", "norm_post_mix", "norm_pre_ffn", "norm_post_ffn")
FULL_SHAPE = dict(dn_conv_w=(DEPTH, 4, 1536), ffn_conv_w=(DEPTH, 3, 2 * D_FF), dn_a_log=(DEPTH, NDH),
                  dn_dt_bias=(DEPTH, NDH), dn_norm_w=(DEPTH, 128), ffn_conv_b=(DEPTH, 2 * D_FF),
                  norm_pre_mix=(DEPTH, D_MODEL), norm_post_mix=(DEPTH, D_MODEL), norm_pre_ffn=(DEPTH, D_MODEL),
                  norm_post_ffn=(DEPTH, D_MODEL))
SMALL_GRAD_ORDER = REPLICATED + SMALL_SHARDED
SMALL_GRAD_ROWS = 520
SMALL_W_ROWS = 48
SMALL_ADAM_ROWS = 200


def _cols_to_devices(t):
    nl, nr, nc = t.shape
    return t.reshape(nl, nr, N_DEV, nc // N_DEV).transpose(2, 0, 1, 3)


def _rows_to_devices(t):
    nl, nr, nc = t.shape
    return t.reshape(nl, N_DEV, nr // N_DEV, nc).transpose(1, 0, 2, 3)


def kernel(x, w_in, dn_conv_w, dn_a_log, dn_dt_bias, dn_norm_w, w_out, ffn_w_in, ffn_conv_w, ffn_conv_b, ffn_w_out, norm_pre_mix, norm_post_mix, norm_pre_ffn, norm_post_ffn, loss_target, m_w_in, m_dn_conv_w, m_dn_a_log, m_dn_dt_bias, m_dn_norm_w, m_w_out, m_ffn_w_in, m_ffn_conv_w, m_ffn_conv_b, m_ffn_w_out, m_norm_pre_mix, m_norm_post_mix, m_norm_pre_ffn, m_norm_post_ffn, v_w_in, v_dn_conv_w, v_dn_a_log, v_dn_dt_bias, v_dn_norm_w, v_w_out, v_ffn_w_in, v_ffn_conv_w, v_ffn_conv_b, v_ffn_w_out, v_norm_pre_mix, v_norm_post_mix, v_norm_pre_ffn, v_norm_post_ffn):
    local = dict(w_in=w_in, dn_conv_w=dn_conv_w, dn_a_log=dn_a_log, dn_dt_bias=dn_dt_bias, dn_norm_w=dn_norm_w,
                 w_out=w_out, ffn_w_in=ffn_w_in, ffn_conv_w=ffn_conv_w, ffn_conv_b=ffn_conv_b, ffn_w_out=ffn_w_out,
                 norm_pre_mix=norm_pre_mix, norm_post_mix=norm_post_mix, norm_pre_ffn=norm_pre_ffn,
                 norm_post_ffn=norm_post_ffn)
    mom_m = dict(w_in=m_w_in, dn_conv_w=m_dn_conv_w, dn_a_log=m_dn_a_log, dn_dt_bias=m_dn_dt_bias,
                 dn_norm_w=m_dn_norm_w, w_out=m_w_out, ffn_w_in=m_ffn_w_in, ffn_conv_w=m_ffn_conv_w,
                 ffn_conv_b=m_ffn_conv_b, ffn_w_out=m_ffn_w_out, norm_pre_mix=m_norm_pre_mix,
                 norm_post_mix=m_norm_post_mix, norm_pre_ffn=m_norm_pre_ffn, norm_post_ffn=m_norm_post_ffn)
    mom_v = dict(w_in=v_w_in, dn_conv_w=v_dn_conv_w, dn_a_log=v_dn_a_log, dn_dt_bias=v_dn_dt_bias,
                 dn_norm_w=v_dn_norm_w, w_out=v_w_out, ffn_w_in=v_ffn_w_in, ffn_conv_w=v_ffn_conv_w,
                 ffn_conv_b=v_ffn_conv_b, ffn_w_out=v_ffn_w_out, norm_pre_mix=v_norm_pre_mix,
                 norm_post_mix=v_norm_post_mix, norm_pre_ffn=v_norm_pre_ffn, norm_post_ffn=v_norm_post_ffn)
    dev = 4 * lax.axis_index("x") + 2 * lax.axis_index("y") + lax.axis_index("c")
    core = lax.axis_index("c").astype(jnp.int32).reshape(1)

    small_w = _pack([dn_conv_w, ffn_conv_w], SMALL_W_ROWS)
    gathered = _all_gather_hbm([local[n].astype(BF16) for n in BIG] + [small_w], "weights_all_gather")
    g_in, g_out, g_fin, g_fout, g_small = gathered
    full = dict(
        w_in=jnp.pad(g_in.transpose(1, 2, 0, 3).reshape(DEPTH, D_MODEL, IN_COLS), ((0, 0), (0, 0), (0, IN_PAD - IN_COLS))),
        w_out=g_out.transpose(1, 0, 2, 3).reshape(DEPTH, D_MODEL, D_MODEL),
        ffn_w_in=g_fin.transpose(1, 2, 0, 3).reshape(DEPTH, D_MODEL, 2 * D_FF),
        ffn_w_out=g_fout.transpose(1, 0, 2, 3).reshape(DEPTH, D_FF, D_MODEL))
    n_dn, n_ff = DEPTH * 4 * 192, DEPTH * 3 * 704
    sm = g_small.reshape(N_DEV, -1)
    full["dn_conv_w"] = sm[:, :n_dn].reshape(N_DEV, DEPTH, 4, 192).transpose(1, 2, 0, 3).reshape(DEPTH, 4, 1536)
    full["ffn_conv_w"] = sm[:, n_dn:n_dn + n_ff].reshape(N_DEV, DEPTH, 3, 704).transpose(1, 2, 0, 3).reshape(DEPTH, 3, 2 * D_FF)

    def layer_weights(l):
        wts = {n: full[n][l] for n in BIG + SMALL_SHARDED}
        wts["dn_a_log"] = _row(dn_a_log[l], 128)
        wts["dn_dt_bias"] = _row(dn_dt_bias[l], 128)
        for n in ("dn_norm_w", "ffn_conv_b", "norm_pre_mix", "norm_post_mix", "norm_pre_ffn", "norm_post_ffn"):
            wts[n] = _row(local[n][l])
        return wts

    tables = _rope_tables()
    act, saved = x[0], []
    for l in range(DEPTH):
        act, s = _layer_fwd(act, layer_weights(l), tables)
        saved.append(s)
    loss_part, dact = _loss_fwd_bwd(act, loss_target[0])
    grads = [None] * DEPTH
    for l in reversed(range(DEPTH)):
        dact, grads[l] = _layer_bwd(dact, layer_weights(l), saved[l], tables)
    grad_x = dact[None]

    def stacked(name):
        t = jnp.stack([grads[l][name] for l in range(DEPTH)])
        if name == "w_in":
            t = t[:, :, :IN_COLS]
        if name in ("dn_a_log", "dn_dt_bias"):
            t = t[:, 0, :NDH]
        return t.reshape(FULL_SHAPE[name]) if name in FULL_SHAPE else t

    to_dev = [_cols_to_devices(stacked("w_in")), _rows_to_devices(stacked("w_out")),
              _cols_to_devices(stacked("ffn_w_in")), _rows_to_devices(stacked("ffn_w_out"))]
    from_sibling = _exchange_sibling(to_dev, "grads_to_sibling")
    chip_sums = [_pair_add(gd, r, core, "grads_pair_add_" + n) for gd, r, n in zip(to_dev, from_sibling, BIG)]
    parts = _exchange_chips(chip_sums, "grads_to_chips")
    small_part = _pack([stacked(n) for n in SMALL_GRAD_ORDER] + [loss_part[0, :1]], SMALL_GRAD_ROWS)
    small_sum = _all_gather_sum_small(small_part)
    small_g = dict(zip(SMALL_GRAD_ORDER + ("loss",), _unpack(small_sum, [FULL_SHAPE[n] for n in SMALL_GRAD_ORDER] + [(1,)])))
    loss = small_g["loss"][0]
    small_g["dn_conv_w"] = lax.dynamic_slice_in_dim(small_g["dn_conv_w"], dev * 192, 192, axis=2)
    small_g["ffn_conv_w"] = lax.dynamic_slice_in_dim(small_g["ffn_conv_w"], dev * 704, 704, axis=2)

    out_g, out_d, out_m, out_v = {}, {}, {}, {}
    for n, p in zip(BIG, parts):
        out_g[n], out_d[n], out_m[n], out_v[n] = _adamw_sharded(p, local[n], mom_m[n], mom_v[n], "adamw_" + n)
    shapes = [small_g[n].shape for n in SMALL_GRAD_ORDER]
    d_s, m_s, v_s = _adamw_small(_pack([small_g[n] for n in SMALL_GRAD_ORDER], SMALL_ADAM_ROWS),
                                 _pack([local[n] for n in SMALL_GRAD_ORDER], SMALL_ADAM_ROWS),
                                 _pack([mom_m[n] for n in SMALL_GRAD_ORDER], SMALL_ADAM_ROWS),
                                 _pack([mom_v[n] for n in SMALL_GRAD_ORDER], SMALL_ADAM_ROWS))
    for n, d, m, v in zip(SMALL_GRAD_ORDER, _unpack(d_s, shapes), _unpack(m_s, shapes), _unpack(v_s, shapes)):
        out_g[n], out_d[n], out_m[n], out_v[n] = small_g[n], d, m, v
    return (loss, grad_x, *[out_g[n] for n in WEIGHTS], *[out_d[n] for n in WEIGHTS],
            *[out_m[n] for n in WEIGHTS], *[out_v[n] for n in WEIGHTS])
```

```python
import functools
import math

import jax
import jax.numpy as jnp
from jax import lax
from jax.experimental import pallas as pl
from jax.experimental.pallas import tpu as pltpu

F32 = jnp.float32
BF16 = jnp.bfloat16
HI = lax.Precision.HIGHEST
MESH = pl.DeviceIdType.MESH

N_DEV = 8
SEQ = 2048
D_MODEL = 1024
DEPTH = 2
N_PAIR = 4
HEAD_DIM = 64
ATTN_W = 512
ATTN_BLK = 128
DILATIONS = (1, 4, 16)
SEGMENT_BLOCKS = (16, 4, 1)
N_BLK = SEQ // ATTN_BLK
NDH = 4
CH = 64
NCH = SEQ // CH
IN_COLS = 3592
IN_PAD = 3840
QKV_W = 3 * ATTN_W
DN_QKV_BLK0 = QKV_W // 128
DN_QKV_BLKS = 1536 // 128
DN_Z_COL = 3072
DN_TAIL_BLK = 3584 // 128
D_FF = 2816
FF_BLKS = D_FF // 128
EPS = 1e-6
NEG = -1e30
ROPE_THETA = 10000.0

ADAM_LR, ADAM_B1, ADAM_B2, ADAM_EPS, ADAM_WD, ADAM_STEP = 0.001, 0.9, 0.999, 1e-08, 0.01, 10

VMEM_LIMIT = 56 * 1024 * 1024


def _cp(*sem):
    return pltpu.CompilerParams(dimension_semantics=sem, vmem_limit_bytes=VMEM_LIMIT)


def _dot(a, b, dims, precision=None):
    if precision is None:
        a = a.astype(BF16)
        b = b.astype(BF16)
    return lax.dot_general(a, b, (dims, ((), ())), preferred_element_type=F32, precision=precision)


def _make_mm(precision):
    @jax.custom_vjp
    def nn(a, b):
        return _dot(a, b, ((1,), (0,)), precision)

    @jax.custom_vjp
    def nt(a, b):
        return _dot(a, b, ((1,), (1,)), precision)

    @jax.custom_vjp
    def tn(a, b):
        return _dot(a, b, ((0,), (0,)), precision)

    nn.defvjp(lambda a, b: (nn(a, b), (a, b)), lambda r, g: (nt(g, r[1]), tn(r[0], g)))
    nt.defvjp(lambda a, b: (nt(a, b), (a, b)), lambda r, g: (nn(g, r[1]), tn(g, r[0])))
    tn.defvjp(lambda a, b: (tn(a, b), (a, b)), lambda r, g: (nt(r[1], g), nn(r[0], g)))
    return nn, nt, tn


MM, MM_NT, MM_TN = _make_mm(None)
MMH, _, _ = _make_mm(HI)


def _matmul(a, b, *, ta=False, tb=False, tm, tn, tk, name, out_dtype=F32):
    (k_dim, m_dim) = a.shape if ta else a.shape[::-1]
    (n_dim, k2) = b.shape if tb else b.shape[::-1]
    assert k_dim == k2 and m_dim % tm == 0 and n_dim % tn == 0 and k_dim % tk == 0, (a.shape, b.shape, tm, tn, tk)
    nk = k_dim // tk
    dims = ((0 if ta else 1,), (1 if tb else 0,))

    def body(a_ref, b_ref, o_ref, acc_ref):
        k = pl.program_id(2)
        p = _dot(a_ref[...], b_ref[...], dims)

        @pl.when(k == 0)
        def _():
            acc_ref[...] = p

        @pl.when(k > 0)
        def _():
            acc_ref[...] += p

        @pl.when(k == nk - 1)
        def _():
            o_ref[...] = acc_ref[...].astype(out_dtype)

    a_spec = pl.BlockSpec((tk, tm), lambda i, j, k: (k, i)) if ta else pl.BlockSpec((tm, tk), lambda i, j, k: (i, k))
    b_spec = pl.BlockSpec((tn, tk), lambda i, j, k: (j, k)) if tb else pl.BlockSpec((tk, tn), lambda i, j, k: (k, j))
    return pl.pallas_call(
        body, name=name,
        grid=(m_dim // tm, n_dim // tn, nk),
        in_specs=[a_spec, b_spec],
        out_specs=pl.BlockSpec((tm, tn), lambda i, j, k: (i, j)),
        out_shape=jax.ShapeDtypeStruct((m_dim, n_dim), out_dtype),
        scratch_shapes=[pltpu.VMEM((tm, tn), F32)],
        compiler_params=_cp("parallel", "parallel", "arbitrary"),
    )(a, b)


NORM_ROWS = 256


def _rms(x, w):
    return x * lax.rsqrt(jnp.mean(x * x, axis=1, keepdims=True) + EPS) * w


def _norm_fwd(x, w_row, name, out_dtype=BF16):
    def body(x_ref, w_ref, o_ref):
        o_ref[...] = _rms(x_ref[...], w_ref[...]).astype(out_dtype)

    return pl.pallas_call(
        body, name=name, grid=(SEQ // NORM_ROWS,),
        in_specs=[pl.BlockSpec((NORM_ROWS, D_MODEL), lambda i: (i, 0)), pl.BlockSpec((1, D_MODEL), lambda i: (0, 0))],
        out_specs=pl.BlockSpec((NORM_ROWS, D_MODEL), lambda i: (i, 0)),
        out_shape=jax.ShapeDtypeStruct((SEQ, D_MODEL), out_dtype),
        compiler_params=_cp("parallel"),
    )(x, w_row)


def _resnorm_fwd(x, f, w_row, name):
    def body(x_ref, f_ref, w_ref, o_ref):
        o_ref[...] = x_ref[...] + _rms(f_ref[...], w_ref[...])

    blk = pl.BlockSpec((NORM_ROWS, D_MODEL), lambda i: (i, 0))
    return pl.pallas_call(
        body, name=name, grid=(SEQ // NORM_ROWS,),
        in_specs=[blk, blk, pl.BlockSpec((1, D_MODEL), lambda i: (0, 0))],
        out_specs=blk, out_shape=jax.ShapeDtypeStruct((SEQ, D_MODEL), F32),
        compiler_params=_cp("parallel"),
    )(x, f, w_row)


def _norm_bwd(x, w_row, dy, add, name):
    has_add = add is not None

    def body(*refs):
        if has_add:
            x_ref, w_ref, dy_ref, add_ref, dx_ref, dw_ref = refs
        else:
            x_ref, w_ref, dy_ref, dx_ref, dw_ref = refs
        _, vjp = jax.vjp(_rms, x_ref[...], w_ref[...])
        dx, dw = vjp(dy_ref[...])
        dx_ref[...] = dx + add_ref[...] if has_add else dx

        @pl.when(pl.program_id(0) == 0)
        def _():
            dw_ref[...] = jnp.zeros_like(dw_ref)

        dw_ref[...] += dw

    blk = pl.BlockSpec((NORM_ROWS, D_MODEL), lambda i: (i, 0))
    row = pl.BlockSpec((1, D_MODEL), lambda i: (0, 0))
    ins = [x, w_row, dy] + ([add] if has_add else [])
    return pl.pallas_call(
        body, name=name, grid=(SEQ // NORM_ROWS,),
        in_specs=[blk, row, blk] + ([blk] if has_add else []),
        out_specs=[blk, row],
        out_shape=[jax.ShapeDtypeStruct((SEQ, D_MODEL), F32), jax.ShapeDtypeStruct((1, D_MODEL), F32)],
        compiler_params=_cp("arbitrary"),
    )(*ins)


def _loss_fwd_bwd(y, target):
    def body(y_ref, t_ref, loss_ref, dy_ref):
        err = y_ref[...] - t_ref[...]
        dy_ref[...] = err * (1.0 / D_MODEL)

        @pl.when(pl.program_id(0) == 0)
        def _():
            loss_ref[...] = jnp.zeros_like(loss_ref)

        part = jnp.sum(jnp.sum(err * err, axis=1, keepdims=True) * (1.0 / D_MODEL), axis=0, keepdims=True)
        loss_ref[...] += 0.5 * jnp.broadcast_to(part, loss_ref.shape)

    blk = pl.BlockSpec((NORM_ROWS, D_MODEL), lambda i: (i, 0))
    return pl.pallas_call(
        body, name="loss", grid=(SEQ // NORM_ROWS,),
        in_specs=[blk, blk],
        out_specs=[pl.BlockSpec((1, 128), lambda i: (0, 0)), blk],
        out_shape=[jax.ShapeDtypeStruct((1, 128), F32), jax.ShapeDtypeStruct((SEQ, D_MODEL), F32)],
        compiler_params=_cp("arbitrary"),
    )(y, target)


def _make_shift(j):
    def down(x):
        row = lax.broadcasted_iota(jnp.int32, x.shape, 0)
        return jnp.where(row >= j, pltpu.roll(x, j, 0), 0.0)

    def up(x):
        n = x.shape[0]
        row = lax.broadcasted_iota(jnp.int32, x.shape, 0)
        return jnp.where(row < n - j, pltpu.roll(x, n - j, 0), 0.0)

    f = jax.custom_vjp(down)
    f.defvjp(lambda x: (down(x), None), lambda _, g: (up(g),))
    return f


_SHIFT = {j: _make_shift(j) for j in (1, 2, 3)}


def _causal_conv(x, taps):
    n = len(taps)
    acc = x * taps[n - 1]
    for k in range(n - 1):
        acc = acc + _SHIFT[n - 1 - k](x) * taps[k]
    return acc


def _tap_rows(w_ref, lanes=slice(None)):
    return tuple(w_ref[k:k + 1, lanes] for k in range(w_ref.shape[0]))


def _sigmoid(x):
    return 1.0 / (1.0 + jnp.exp(-x))


def _silu(x):
    return x * _sigmoid(x)


def _softplus(x):
    return jnp.maximum(x, 0.0) + jnp.log(1.0 + jnp.exp(-jnp.abs(x)))


def _gelu_tanh(x):
    return 0.5 * x * (1.0 + jnp.tanh(math.sqrt(2.0 / math.pi) * (x + 0.044715 * (x * x * x))))


def _dnconv_fn(x, taps):
    return _silu(_causal_conv(x, taps))


def _dnconv_fwd(proj, conv_w):
    def body(x_ref, w_ref, o_ref):
        o_ref[...] = _dnconv_fn(x_ref[...], _tap_rows(w_ref))

    return pl.pallas_call(
        body, name="dnconv_fwd", grid=(DN_QKV_BLKS,),
        in_specs=[pl.BlockSpec((SEQ, 128), lambda j: (0, DN_QKV_BLK0 + j)), pl.BlockSpec((4, 128), lambda j: (0, j))],
        out_specs=pl.BlockSpec((SEQ, 128), lambda j: (0, j)),
        out_shape=jax.ShapeDtypeStruct((SEQ, 1536), F32),
        compiler_params=_cp("parallel"),
    )(proj, conv_w)


def _dnconv_bwd(proj, conv_w, dc, dproj):
    def body(x_ref, w_ref, dc_ref, _, dx_ref, dw_ref):
        _, vjp = jax.vjp(_dnconv_fn, x_ref[...], _tap_rows(w_ref))
        dx, dw = vjp(dc_ref[...])
        dx_ref[...] = dx
        for k, row in enumerate(dw):
            dw_ref[k:k + 1, :] = row

    return pl.pallas_call(
        body, name="dnconv_bwd", grid=(DN_QKV_BLKS,),
        in_specs=[pl.BlockSpec((SEQ, 128), lambda j: (0, DN_QKV_BLK0 + j)), pl.BlockSpec((4, 128), lambda j: (0, j)),
                  pl.BlockSpec((SEQ, 128), lambda j: (0, j)), pl.BlockSpec(memory_space=pl.ANY)],
        out_specs=[pl.BlockSpec((SEQ, 128), lambda j: (0, DN_QKV_BLK0 + j)), pl.BlockSpec((4, 128), lambda j: (0, j))],
        out_shape=[jax.ShapeDtypeStruct((SEQ, IN_PAD), F32), jax.ShapeDtypeStruct((4, 1536), F32)],
        input_output_aliases={3: 0},
        compiler_params=_cp("parallel"),
    )(proj, conv_w, dc, dproj)


def _ffact_fn(pg, pu, wg, wu, bg, bu):
    return _gelu_tanh(_causal_conv(pg, wg) + bg) * (_causal_conv(pu, wu) + bu)


def _ffact_args(p_ref, w_ref, b_ref):
    g, u = slice(0, 128), slice(128, 256)
    return (p_ref[:, g], p_ref[:, u], _tap_rows(w_ref, g), _tap_rows(w_ref, u), b_ref[:, g], b_ref[:, u])


def _ffact_fwd(pre, conv_w, conv_b):
    def body(p_ref, w_ref, b_ref, o_ref):
        o_ref[...] = _ffact_fn(*_ffact_args(p_ref, w_ref, b_ref)).astype(BF16)

    return pl.pallas_call(
        body, name="ffact_fwd", grid=(FF_BLKS,),
        in_specs=[pl.BlockSpec((SEQ, 256), lambda j: (0, j)), pl.BlockSpec((3, 256), lambda j: (0, j)),
                  pl.BlockSpec((1, 256), lambda j: (0, j))],
        out_specs=pl.BlockSpec((SEQ, 128), lambda j: (0, j)),
        out_shape=jax.ShapeDtypeStruct((SEQ, D_FF), BF16),
        compiler_params=_cp("parallel"),
    )(pre, conv_w, conv_b)


def _ffact_bwd(pre, conv_w, conv_b, dact):
    def body(p_ref, w_ref, b_ref, da_ref, dp_ref, dw_ref, db_ref):
        _, vjp = jax.vjp(_ffact_fn, *_ffact_args(p_ref, w_ref, b_ref))
        dpg, dpu, dwg, dwu, dbg, dbu = vjp(da_ref[...].astype(F32))
        dp_ref[:, 0:128] = dpg
        dp_ref[:, 128:256] = dpu
        for k in range(3):
            dw_ref[k:k + 1, 0:128] = dwg[k]
            dw_ref[k:k + 1, 128:256] = dwu[k]
        db_ref[:, 0:128] = dbg
        db_ref[:, 128:256] = dbu

    return pl.pallas_call(
        body, name="ffact_bwd", grid=(FF_BLKS,),
        in_specs=[pl.BlockSpec((SEQ, 256), lambda j: (0, j)), pl.BlockSpec((3, 256), lambda j: (0, j)),
                  pl.BlockSpec((1, 256), lambda j: (0, j)), pl.BlockSpec((SEQ, 128), lambda j: (0, j))],
        out_specs=[pl.BlockSpec((SEQ, 256), lambda j: (0, j)), pl.BlockSpec((3, 256), lambda j: (0, j)),
                   pl.BlockSpec((1, 256), lambda j: (0, j))],
        out_shape=[jax.ShapeDtypeStruct((SEQ, 2 * D_FF), F32), jax.ShapeDtypeStruct((3, 2 * D_FF), F32),
                   jax.ShapeDtypeStruct((1, 2 * D_FF), F32)],
        compiler_params=_cp("parallel"),
    )(pre, conv_w, conv_b, dact)


def _interleave_ff(t):
    lead = t.shape[:-1]
    return t.reshape(lead + (2, FF_BLKS, 128)).swapaxes(-3, -2).reshape(lead + (2 * D_FF,))


def _deinterleave_ff(t):
    lead = t.shape[:-1]
    return t.reshape(lead + (FF_BLKS, 2, 128)).swapaxes(-3, -2).reshape(lead + (2 * D_FF,))


def _rope_tables():
    inv = 1.0 / (ROPE_THETA ** (jnp.arange(0, HEAD_DIM, 2, dtype=F32) / HEAD_DIM))
    ang = jnp.arange(SEQ, dtype=F32)[:, None] * inv[None, :]
    cos = jnp.tile(jnp.cos(ang), (1, 4))
    sin = jnp.tile(jnp.sin(ang), (1, 4))
    sign = jnp.where((jnp.arange(128) % HEAD_DIM) < HEAD_DIM // 2, -1.0, 1.0).astype(F32)
    return cos, sin * sign[None, :]


def _rope(x, cos, sin_signed):
    lane = lax.broadcasted_iota(jnp.int32, x.shape, 1)
    partner = jnp.where((lane % HEAD_DIM) < HEAD_DIM // 2, pltpu.roll(x, 128 - HEAD_DIM // 2, 1),
                        pltpu.roll(x, HEAD_DIM // 2, 1))
    return x * cos + partner * sin_signed


def _pairs_from_qkv(t):
    lead = t.shape[:-1]
    return t.reshape(lead + (3, N_PAIR, 128)).swapaxes(-3, -2).reshape(lead + (QKV_W,))


def _qkv_from_pairs(t):
    lead = t.shape[:-1]
    return t.reshape(lead + (N_PAIR, 3, 128)).swapaxes(-3, -2).reshape(lead + (QKV_W,))


def _band_masks():
    a = lax.broadcasted_iota(jnp.int32, (ATTN_BLK, ATTN_BLK), 0)
    c = lax.broadcasted_iota(jnp.int32, (ATTN_BLK, ATTN_BLK), 1)
    return c >= a, c <= a


def _head_masks():
    lane = lax.broadcasted_iota(jnp.int32, (1, 128), 1)
    return [(lane // HEAD_DIM) == h for h in range(2)]


def _block_rows(branch, t):
    d, per_seg = DILATIONS[branch], SEGMENT_BLOCKS[branch]
    if d == 1:
        start = pl.multiple_of(t * ATTN_BLK, ATTN_BLK)
        prev = pl.multiple_of(jnp.maximum(t - 1, 0) * ATTN_BLK, ATTN_BLK)
        return pl.ds(start, ATTN_BLK), pl.ds(prev, ATTN_BLK), t > 0
    r, n = t // per_seg, t % per_seg
    start = n * (ATTN_BLK * d) + r
    prev = jnp.maximum(n - 1, 0) * (ATTN_BLK * d) + r
    return pl.ds(start, ATTN_BLK, stride=d), pl.ds(prev, ATTN_BLK, stride=d), n > 0


def _attn_fwd(proj, cos, sin_signed):
    scale = HEAD_DIM ** -0.5

    def body(qkv_ref, cos_ref, sin_ref, out_ref, lse_ref, q_s, k_s, v_s, *branch_s):
        o_s, l_s = branch_s[:3], branch_s[3:]
        q_s[...] = _rope(qkv_ref[:, 0:128], cos_ref[...], sin_ref[...])
        k_s[...] = _rope(qkv_ref[:, 128:256], cos_ref[...], sin_ref[...])
        v_s[...] = qkv_ref[:, 256:384]
        m_prev0, m_cur = _band_masks()
        heads = _head_masks()
        for branch in range(3):
            def block(t, carry, branch=branch):
                rows, prows, has_prev = _block_rows(branch, t)
                m_prev = m_prev0 & has_prev
                q, kc, vc = q_s[rows, :], k_s[rows, :], v_s[rows, :]
                kp, vp = k_s[prows, :], v_s[prows, :]
                outs, lses = [], []
                for hm in heads:
                    qh = jnp.where(hm, q, 0.0)
                    sp = jnp.where(m_prev, MM_NT(qh, kp) * scale, NEG)
                    sc = jnp.where(m_cur, MM_NT(qh, kc) * scale, NEG)
                    m = jnp.maximum(jnp.max(sp, axis=1, keepdims=True), jnp.max(sc, axis=1, keepdims=True))
                    ep = jnp.exp(sp - m)
                    ec = jnp.exp(sc - m)
                    l = jnp.sum(ep, axis=1, keepdims=True) + jnp.sum(ec, axis=1, keepdims=True)
                    outs.append((MM(ep, vp) + MM(ec, vc)) / l)
                    lses.append(m + jnp.log(l))
                o_s[branch][rows, :] = jnp.where(heads[0], outs[0], outs[1])
                l_s[branch][rows, :] = jnp.where(heads[0], lses[0], lses[1])
                return carry

            lax.fori_loop(0, N_BLK, block, 0)
        l0, l1, l2 = l_s[0][...], l_s[1][...], l_s[2][...]
        m = jnp.maximum(jnp.maximum(l0, l1), l2)
        w0, w1, w2 = jnp.exp(l0 - m), jnp.exp(l1 - m), jnp.exp(l2 - m)
        den = w0 + w1 + w2
        out_ref[...] = (w0 * o_s[0][...] + w1 * o_s[1][...] + w2 * o_s[2][...]) / den
        lse_ref[...] = m + jnp.log(den)

    tab = pl.BlockSpec((SEQ, 128), lambda j: (0, 0))
    col = pl.BlockSpec((SEQ, 128), lambda j: (0, j))
    return pl.pallas_call(
        body, name="attn_fwd", grid=(N_PAIR,),
        in_specs=[pl.BlockSpec((SEQ, 384), lambda j: (0, j)), tab, tab],
        out_specs=[col, col],
        out_shape=[jax.ShapeDtypeStruct((SEQ, 2 * ATTN_W), F32), jax.ShapeDtypeStruct((SEQ, ATTN_W), F32)],
        scratch_shapes=[pltpu.VMEM((SEQ, 128), F32)] * 9,
        compiler_params=_cp("parallel"),
    )(proj, cos, sin_signed)


def _attn_bwd(proj, cos, sin_signed, cat, lse, dcat):
    scale = HEAD_DIM ** -0.5

    def body(qkv_ref, cos_ref, sin_ref, out_ref, lse_ref, do_ref, dqkv_ref, q_s, k_s, v_s, dq_s, dk_s, dv_s, dod_s):
        q_s[...] = _rope(qkv_ref[:, 0:128], cos_ref[...], sin_ref[...])
        k_s[...] = _rope(qkv_ref[:, 128:256], cos_ref[...], sin_ref[...])
        v_s[...] = qkv_ref[:, 256:384]
        dq_s[...] = jnp.zeros_like(dq_s)
        dk_s[...] = jnp.zeros_like(dk_s)
        dv_s[...] = jnp.zeros_like(dv_s)
        dod_s[...] = do_ref[...] * out_ref[...]
        m_prev0, m_cur = _band_masks()
        heads = _head_masks()
        for branch in range(3):
            def block(t, carry, branch=branch):
                rows, prows, has_prev = _block_rows(branch, t)
                m_prev = m_prev0 & has_prev
                q, kc, vc = q_s[rows, :], k_s[rows, :], v_s[rows, :]
                kp, vp = k_s[prows, :], v_s[prows, :]
                do, lse_b, dod = do_ref[rows, :], lse_ref[rows, :], dod_s[rows, :]
                dq = []
                dk_cur = jnp.zeros((ATTN_BLK, 128), F32)
                dv_cur = jnp.zeros((ATTN_BLK, 128), F32)
                dk_prev = jnp.zeros((ATTN_BLK, 128), F32)
                dv_prev = jnp.zeros((ATTN_BLK, 128), F32)
                for hm in heads:
                    qh = jnp.where(hm, q, 0.0)
                    doh = jnp.where(hm, do, 0.0)
                    lse_h = jnp.max(jnp.where(hm, lse_b, NEG), axis=1, keepdims=True)
                    delta = jnp.sum(jnp.where(hm, dod, 0.0), axis=1, keepdims=True)
                    pp = jnp.exp(jnp.where(m_prev, MM_NT(qh, kp) * scale, NEG) - lse_h)
                    pc = jnp.exp(jnp.where(m_cur, MM_NT(qh, kc) * scale, NEG) - lse_h)
                    dsp = pp * (MM_NT(doh, vp) - delta) * scale
                    dsc = pc * (MM_NT(doh, vc) - delta) * scale
                    dq.append(MM(dsp, kp) + MM(dsc, kc))
                    dk_prev += MM_TN(dsp, qh)
                    dk_cur += MM_TN(dsc, qh)
                    dv_prev += MM_TN(pp, doh)
                    dv_cur += MM_TN(pc, doh)
                dq_s[rows, :] += jnp.where(heads[0], dq[0], dq[1])
                dk_s[rows, :] += dk_cur
                dv_s[rows, :] += dv_cur

                @pl.when(has_prev)
                def _():
                    dk_s[prows, :] += dk_prev
                    dv_s[prows, :] += dv_prev

                return carry

            lax.fori_loop(0, N_BLK, block, 0)
        dqkv_ref[:, 0:128] = _rope(dq_s[...], cos_ref[...], -sin_ref[...])
        dqkv_ref[:, 128:256] = _rope(dk_s[...], cos_ref[...], -sin_ref[...])
        dqkv_ref[:, 256:384] = dv_s[...]

    tab = pl.BlockSpec((SEQ, 128), lambda j: (0, 0))
    col = pl.BlockSpec((SEQ, 128), lambda j: (0, j))
    qkv = pl.BlockSpec((SEQ, 384), lambda j: (0, j))
    return pl.pallas_call(
        body, name="attn_bwd", grid=(N_PAIR,),
        in_specs=[qkv, tab, tab, col, col, col],
        out_specs=qkv,
        out_shape=jax.ShapeDtypeStruct((SEQ, IN_PAD), F32),
        scratch_shapes=[pltpu.VMEM((SEQ, 128), F32)] * 7,
        compiler_params=_cp("parallel"),
    )(proj, cos, sin_signed, cat, lse, dcat)


def _delta_chunk(head, qr, kr, vr, z, tail, alog_row, dt_row, nw, state):
    c = qr.shape[0]
    lane = lax.broadcasted_iota(jnp.int32, (1, 128), 1)
    sel_b = (lane == head).astype(F32)
    sel_a = (lane == head + NDH).astype(F32)
    b_raw = jnp.sum(tail * sel_b, axis=1, keepdims=True)
    a_raw = jnp.sum(tail * sel_a, axis=1, keepdims=True)
    a_log = jnp.sum(alog_row * sel_b, axis=1, keepdims=True)
    dt_b = jnp.sum(dt_row * sel_b, axis=1, keepdims=True)
    beta = _sigmoid(b_raw)
    g = -jnp.exp(a_log) * _softplus(a_raw + dt_b)

    q = qr * lax.rsqrt(jnp.sum(qr * qr, axis=1, keepdims=True) + EPS) * (128 ** -0.5)
    k = kr * lax.rsqrt(jnp.sum(kr * kr, axis=1, keepdims=True) + EPS)

    ri = lax.broadcasted_iota(jnp.int32, (c, c), 0)
    ci = lax.broadcasted_iota(jnp.int32, (c, c), 1)
    tril = ri >= ci
    eye = (ri == ci).astype(F32)
    gc = MMH(tril.astype(F32), g)
    g_row = MMH(jnp.ones((c, c), F32), eye * gc)
    decay = jnp.where(tril, jnp.exp(jnp.where(tril, gc - g_row, 0.0)), 0.0)
    kb = k * beta
    a_mat = jnp.where(ri > ci, MM_NT(kb, k) * decay, 0.0)
    power = -a_mat
    t_inv = eye + power
    for _ in range(5):
        power = MMH(power, power)
        t_inv = t_inv + MMH(t_inv, power)
    eg = jnp.exp(gc)
    u = MM(t_inv, vr * beta)
    w = MM(t_inv, kb * eg)
    qk = MM_NT(q, k) * decay
    g_tot = jnp.sum(g, axis=0, keepdims=True)
    v_new = u - MM(w, state)
    o = MM(q * eg, state) + MM(qk, v_new)
    new_state = state * jnp.exp(g_tot) + MM_TN(k * jnp.exp(g_tot - gc), v_new)
    return _rms(o, nw) * _silu(z), new_state


def _head_cols(v, h, off=0):
    return v[:, off + 128 * h:off + 128 * (h + 1)]


def _delta_fwd(c_qkv, proj, alog_row, dt_row, nw, cat):
    def body(c_ref, z_ref, tail_ref, al_ref, dt_ref, nw_ref, _, y_ref, st_ref, state):
        @pl.when(pl.program_id(0) == 0)
        def _():
            state[...] = jnp.zeros_like(state)

        cv, zv, tail = c_ref[...], z_ref[...], tail_ref[...]
        ys = []
        for h in range(NDH):
            st_ref[0, h] = state[h]
            y, new_state = _delta_chunk(h, _head_cols(cv, h), _head_cols(cv, h, 512), _head_cols(cv, h, 1024),
                                        _head_cols(zv, h), tail, al_ref[...], dt_ref[...], nw_ref[...], state[h])
            ys.append(y)
            state[h] = new_state
        y_ref[...] = jnp.concatenate(ys, axis=1)

    row = pl.BlockSpec((1, 128), lambda n: (0, 0))
    return pl.pallas_call(
        body, name="delta_fwd", grid=(NCH,),
        in_specs=[pl.BlockSpec((CH, 1536), lambda n: (n, 0)), pl.BlockSpec((CH, 512), lambda n: (n, DN_Z_COL // 512)),
                  pl.BlockSpec((CH, 128), lambda n: (n, DN_TAIL_BLK)), row, row, row, pl.BlockSpec(memory_space=pl.ANY)],
        out_specs=[pl.BlockSpec((CH, 512), lambda n: (n, 1)),
                   pl.BlockSpec((1, NDH, 128, 128), lambda n: (n, 0, 0, 0))],
        out_shape=[jax.ShapeDtypeStruct((SEQ, 2 * ATTN_W), F32), jax.ShapeDtypeStruct((NCH, NDH, 128, 128), F32)],
        scratch_shapes=[pltpu.VMEM((NDH, 128, 128), F32)],
        input_output_aliases={6: 0},
        compiler_params=_cp("arbitrary"),
    )(c_qkv, proj, proj, alog_row, dt_row, nw, cat)


def _delta_bwd(c_qkv, proj, alog_row, dt_row, nw, states, dcat, dproj):
    def body(c_ref, z_ref, tail_ref, al_ref, dt_ref, nw_ref, st_ref, dy_ref, _,
             dp_ref, dc_ref, dal_ref, ddt_ref, dnw_ref, dstate):
        @pl.when(pl.program_id(0) == 0)
        def _():
            dstate[...] = jnp.zeros_like(dstate)
            dal_ref[...] = jnp.zeros_like(dal_ref)
            ddt_ref[...] = jnp.zeros_like(ddt_ref)
            dnw_ref[...] = jnp.zeros_like(dnw_ref)

        cv, zv, tail, dy = c_ref[...], z_ref[...], tail_ref[...], dy_ref[...]
        dqs, dks, dvs, dzs = [], [], [], []
        dtail = jnp.zeros((CH, 128), F32)
        dal = jnp.zeros((1, 128), F32)
        ddt = jnp.zeros((1, 128), F32)
        dnw = jnp.zeros((1, 128), F32)
        for h in range(NDH):
            _, vjp = jax.vjp(functools.partial(_delta_chunk, h), _head_cols(cv, h), _head_cols(cv, h, 512),
                             _head_cols(cv, h, 1024), _head_cols(zv, h), tail, al_ref[...], dt_ref[...],
                             nw_ref[...], st_ref[0, h])
            dq, dk, dv, dz, dt_h, dal_h, ddt_h, dnw_h, dst = vjp((_head_cols(dy, h), dstate[h]))
            dqs.append(dq)
            dks.append(dk)
            dvs.append(dv)
            dzs.append(dz)
            dtail += dt_h
            dal += dal_h
            ddt += ddt_h
            dnw += dnw_h
            dstate[h] = dst
        dc_ref[...] = jnp.concatenate(dqs + dks + dvs, axis=1)
        dp_ref[...] = jnp.concatenate(dzs + [dtail, jnp.zeros((CH, 128), F32)], axis=1)
        dal_ref[...] += dal
        ddt_ref[...] += ddt
        dnw_ref[...] += dnw

    rev = lambda n: NCH - 1 - n
    row = pl.BlockSpec((1, 128), lambda n: (0, 0))
    return pl.pallas_call(
        body, name="delta_bwd", grid=(NCH,),
        in_specs=[pl.BlockSpec((CH, 1536), lambda n: (rev(n), 0)),
                  pl.BlockSpec((CH, 512), lambda n: (rev(n), DN_Z_COL // 512)),
                  pl.BlockSpec((CH, 128), lambda n: (rev(n), DN_TAIL_BLK)), row, row, row,
                  pl.BlockSpec((1, NDH, 128, 128), lambda n: (rev(n), 0, 0, 0)),
                  pl.BlockSpec((CH, 512), lambda n: (rev(n), 1)), pl.BlockSpec(memory_space=pl.ANY)],
        out_specs=[pl.BlockSpec((CH, 768), lambda n: (rev(n), DN_Z_COL // 768)),
                   pl.BlockSpec((CH, 1536), lambda n: (rev(n), 0)), row, row, row],
        out_shape=[jax.ShapeDtypeStruct((SEQ, IN_PAD), F32), jax.ShapeDtypeStruct((SEQ, 1536), F32)]
        + [jax.ShapeDtypeStruct((1, 128), F32)] * 3,
        scratch_shapes=[pltpu.VMEM((NDH, 128, 128), F32)],
        input_output_aliases={8: 0},
        compiler_params=_cp("arbitrary"),
    )(c_qkv, proj, proj, alog_row, dt_row, nw, states, dcat, dproj)


def _place():
    x, y, c = lax.axis_index("x"), lax.axis_index("y"), lax.axis_index("c")
    other_chips = [(1 - x, y), (x, 1 - y), (1 - x, 1 - y)]
    return x, y, c, other_chips


HBM_SPEC = pl.BlockSpec(memory_space=pltpu.HBM)


def _all_gather_hbm(shards, name):
    n = len(shards)

    def body(*refs):
        ins, outs = refs[:n], refs[n:2 * n]
        send_sems, recv_sems, local_sems = refs[2 * n:]
        x, y, c, chips = _place()
        me, sibling = (x, y, c), (x, y, 1 - c)

        def copy(b, k, block, to, src=None):
            slot = outs[b].at[4 * block[0] + 2 * block[1] + block[2]]
            return pltpu.make_async_remote_copy(
                src_ref=slot if src is None else src, dst_ref=slot,
                send_sem=send_sems.at[b, k], recv_sem=recv_sems.at[b, k], device_id=to, device_id_type=MESH)

        mine = [pltpu.make_async_copy(ins[b], outs[b].at[4 * x + 2 * y + c], local_sems.at[b]) for b in range(n)]
        for cp in mine:
            cp.start()
        first = []
        for b in range(n):
            first.append(copy(b, 0, me, sibling, src=ins[b]))
            first += [copy(b, 1 + j, me, (*chip, c), src=ins[b]) for j, chip in enumerate(chips)]
        for cp in first:
            cp.start()
        passed = []
        for b in range(n):
            for j, chip in enumerate(chips):
                copy(b, 1 + j, (*chip, c), me).wait_recv()
                fwd = copy(b, 4 + j, (*chip, c), sibling)
                fwd.start()
                passed.append(fwd)
        for b in range(n):
            copy(b, 0, sibling, me).wait_recv()
            for j, chip in enumerate(chips):
                copy(b, 4 + j, (*chip, 1 - c), me).wait_recv()
        for cp in first + passed:
            cp.wait_send()
        for cp in mine:
            cp.wait()

    return pl.pallas_call(
        body, name=name,
        in_specs=[HBM_SPEC] * n, out_specs=[HBM_SPEC] * n,
        out_shape=[jax.ShapeDtypeStruct((N_DEV,) + s.shape, s.dtype) for s in shards],
        scratch_shapes=[pltpu.SemaphoreType.DMA((n, 7)), pltpu.SemaphoreType.DMA((n, 7)), pltpu.SemaphoreType.DMA((n,))],
    )(*shards)


def _exchange_sibling(gs, name):
    n = len(gs)

    def body(*refs):
        ins, outs = refs[:n], refs[n:2 * n]
        send_sems, recv_sems = refs[2 * n:]
        x, y, c, _ = _place()
        copies = []
        for b in range(n):
            for p in range(4):
                copies.append(pltpu.make_async_remote_copy(
                    src_ref=ins[b].at[2 * p + (1 - c)], dst_ref=outs[b].at[p],
                    send_sem=send_sems.at[b, p], recv_sem=recv_sems.at[b, p],
                    device_id=(x, y, 1 - c), device_id_type=MESH))
        for cp in copies:
            cp.start()
        for cp in copies:
            cp.wait()

    return pl.pallas_call(
        body, name=name,
        in_specs=[HBM_SPEC] * n, out_specs=[HBM_SPEC] * n,
        out_shape=[jax.ShapeDtypeStruct((4,) + g.shape[1:], g.dtype) for g in gs],
        scratch_shapes=[pltpu.SemaphoreType.DMA((n, 4)), pltpu.SemaphoreType.DMA((n, 4))],
    )(*gs)


def _exchange_chips(hs, name):
    n, nl = len(hs), len(hs[0])

    def body(*refs):
        ins = [refs[b * nl:(b + 1) * nl] for b in range(n)]
        outs = refs[n * nl:n * nl + n]
        send_sems, recv_sems, local_sems = refs[n * nl + n:]
        x, y, c, chips = _place()
        my_chip = 2 * x + y
        local, sends, waits = [], [], []
        for b in range(n):
            for l in range(nl):
                s = b * nl + l
                local.append(pltpu.make_async_copy(ins[b][l].at[my_chip], outs[b].at[my_chip, l], local_sems.at[s]))
                for k, (px, py) in enumerate(chips):
                    peer = 2 * px + py
                    sends.append(pltpu.make_async_remote_copy(
                        src_ref=ins[b][l].at[peer], dst_ref=outs[b].at[my_chip, l],
                        send_sem=send_sems.at[s, k], recv_sem=recv_sems.at[s, k],
                        device_id=(px, py, c), device_id_type=MESH))
                    waits.append(pltpu.make_async_remote_copy(
                        src_ref=ins[b][l].at[peer], dst_ref=outs[b].at[peer, l],
                        send_sem=send_sems.at[s, k], recv_sem=recv_sems.at[s, k],
                        device_id=(px, py, c), device_id_type=MESH))
        for cp in local + sends:
            cp.start()
        for cp in waits:
            cp.wait_recv()
        for cp in sends:
            cp.wait_send()
        for cp in local:
            cp.wait()

    flat = [h for hb in hs for h in hb]
    return pl.pallas_call(
        body, name=name,
        in_specs=[HBM_SPEC] * (n * nl), out_specs=[HBM_SPEC] * n,
        out_shape=[jax.ShapeDtypeStruct((4, nl) + hb[0].shape[1:], hb[0].dtype) for hb in hs],
        scratch_shapes=[pltpu.SemaphoreType.DMA((n * nl, 3)), pltpu.SemaphoreType.DMA((n * nl, 3)),
                        pltpu.SemaphoreType.DMA((n * nl,))],
    )(*flat)


def _pair_add(g, r, core, name):
    _, nr, nc = g.shape
    tr = nr // 2

    def body(core_ref, g_ref, r_ref, o_ref):
        o_ref[...] = (g_ref[...] + r_ref[...]).astype(BF16)

    return pl.pallas_call(
        body, name=name,
        grid_spec=pltpu.PrefetchScalarGridSpec(
            num_scalar_prefetch=1, grid=(4, nr // tr),
            in_specs=[pl.BlockSpec((1, tr, nc), lambda p, i, core: (2 * p + core[0], i, 0)),
                      pl.BlockSpec((1, tr, nc), lambda p, i, core: (p, i, 0))],
            out_specs=pl.BlockSpec((1, tr, nc), lambda p, i, core: (p, i, 0))),
        out_shape=jax.ShapeDtypeStruct(r.shape, BF16),
        compiler_params=_cp("parallel", "parallel"),
    )(core, g, r)


def _all_gather_sum_small(v):
    rows = v.shape[0]

    def body(x_ref, sum_ref, out_ref, send_sems, recv_sems, local_sem):
        x, y, c, chips = _place()
        me, sibling = (x, y, c), (x, y, 1 - c)

        def block(px, py, pc):
            return out_ref.at[pl.ds((4 * px + 2 * py + pc) * rows, rows), :]

        def copy(k, blk, to, src=None):
            return pltpu.make_async_remote_copy(
                src_ref=block(*blk) if src is None else src, dst_ref=block(*blk),
                send_sem=send_sems.at[k], recv_sem=recv_sems.at[k], device_id=to, device_id_type=MESH)

        mine = pltpu.make_async_copy(x_ref, block(*me), local_sem)
        mine.start()
        first = [copy(0, me, sibling, src=x_ref)]
        first += [copy(1 + j, me, (*chip, c), src=x_ref) for j, chip in enumerate(chips)]
        for cp in first:
            cp.start()
        passed = [copy(4 + j, (*chip, c), sibling) for j, chip in enumerate(chips)]
        for j, chip in enumerate(chips):
            copy(1 + j, (*chip, c), me).wait_recv()
            passed[j].start()
        copy(0, sibling, me).wait_recv()
        for j, chip in enumerate(chips):
            copy(4 + j, (*chip, 1 - c), me).wait_recv()
        for cp in first + passed:
            cp.wait_send()
        mine.wait()
        total = out_ref[pl.ds(0, rows), :]
        for d in range(1, N_DEV):
            total = total + out_ref[pl.ds(d * rows, rows), :]
        sum_ref[...] = total

    vm = pl.BlockSpec(memory_space=pltpu.VMEM)
    return pl.pallas_call(
        body, name="small_all_reduce",
        in_specs=[vm], out_specs=[vm],
        out_shape=[jax.ShapeDtypeStruct((rows, 128), F32)],
        scratch_shapes=[pltpu.VMEM((N_DEV * rows, 128), F32), pltpu.SemaphoreType.DMA((7,)),
                        pltpu.SemaphoreType.DMA((7,)), pltpu.SemaphoreType.DMA],
    )(v)[0]


def _adamw(w, g, m, v):
    m = ADAM_B1 * m + (1.0 - ADAM_B1) * g
    v = ADAM_B2 * v + (1.0 - ADAM_B2) * (g * g)
    m_hat = m / (1.0 - ADAM_B1 ** ADAM_STEP)
    v_hat = v / (1.0 - ADAM_B2 ** ADAM_STEP)
    delta = -ADAM_LR * (m_hat / (jnp.sqrt(v_hat) + ADAM_EPS) + ADAM_WD * w)
    return delta, m, v


ADAM_ROWS = dict(w_in=256, w_out=128, ffn_w_in=256, ffn_w_out=176)


def _adamw_sharded(parts, w, m, v, tr, name):
    nl, nr, nc = w.shape

    def body(p_ref, w_ref, m_ref, v_ref, g_ref, d_ref, nm_ref, nv_ref):
        p = p_ref[...].astype(F32)
        g = (p[0] + p[1]) + (p[2] + p[3])
        delta, nm, nv = _adamw(w_ref[...], g, m_ref[...], v_ref[...])
        g_ref[...] = g
        d_ref[...] = delta
        nm_ref[...] = nm
        nv_ref[...] = nv

    blk = pl.BlockSpec((1, tr, nc), lambda l, i: (l, i, 0))
    return pl.pallas_call(
        body, name=name, grid=(nl, nr // tr),
        in_specs=[pl.BlockSpec((4, 1, tr, nc), lambda l, i: (0, l, i, 0)), blk, blk, blk],
        out_specs=[blk] * 4,
        out_shape=[jax.ShapeDtypeStruct(w.shape, F32)] * 4,
        compiler_params=_cp("parallel", "parallel"),
    )(parts, w, m, v)


def _adamw_small(g, w, m, v):
    def body(g_ref, w_ref, m_ref, v_ref, d_ref, nm_ref, nv_ref):
        delta, nm, nv = _adamw(w_ref[...], g_ref[...], m_ref[...], v_ref[...])
        d_ref[...] = delta
        nm_ref[...] = nm
        nv_ref[...] = nv

    return pl.pallas_call(
        body, name="adamw_small",
        out_shape=[jax.ShapeDtypeStruct(g.shape, F32)] * 3,
    )(g, w, m, v)


def _pack(arrays, rows):
    flat = jnp.concatenate([a.reshape(-1).astype(F32) for a in arrays])
    return jnp.pad(flat, (0, rows * 128 - flat.shape[0])).reshape(rows, 128)


def _unpack(packed, shapes):
    flat = packed.reshape(-1)
    out, off = [], 0
    for s in shapes:
        n = math.prod(s)
        out.append(flat[off:off + n].reshape(s))
        off += n
    return out


def _row(v, width=None):
    v = v.reshape(1, -1)
    return v if width is None else jnp.pad(v, ((0, 0), (0, width - v.shape[1])))


def _layer_fwd(x, wts, tables):
    h = _norm_fwd(x, wts["norm_pre_mix"], "norm_pre_mix")
    proj = _matmul(h, wts["w_in"], tm=512, tn=768, tk=1024, name="mm_proj")
    cat, lse = _attn_fwd(proj, *tables)
    c_qkv = _dnconv_fwd(proj, wts["dn_conv_w"])
    cat, states = _delta_fwd(c_qkv, proj, wts["dn_a_log"], wts["dn_dt_bias"], wts["dn_norm_w"], cat)
    mix = _matmul(cat, wts["w_out"], tm=512, tn=1024, tk=1024, name="mm_mix")
    x1 = _resnorm_fwd(x, mix, wts["norm_post_mix"], "norm_post_mix")
    h2 = _norm_fwd(x1, wts["norm_pre_ffn"], "norm_pre_ffn")
    pre = _matmul(h2, wts["ffn_w_in"], tm=512, tn=512, tk=1024, name="mm_ffn_in")
    act = _ffact_fwd(pre, wts["ffn_conv_w"], wts["ffn_conv_b"])
    f = _matmul(act, wts["ffn_w_out"], tm=512, tn=1024, tk=1408, name="mm_ffn_out")
    x2 = _resnorm_fwd(x1, f, wts["norm_post_ffn"], "norm_post_ffn")
    saved = dict(x=x, h=h, proj=proj, lse=lse, c_qkv=c_qkv, states=states, cat=cat, mix=mix, x1=x1, h2=h2, pre=pre,
                 act=act, f=f)
    return x2, saved


def _layer_bwd(dx2, wts, s, tables):
    g = {}
    df, g["norm_post_ffn"] = _norm_bwd(s["f"], wts["norm_post_ffn"], dx2, None, "norm_post_ffn_bwd")
    dact = _matmul(df, wts["ffn_w_out"], tb=True, tm=512, tn=1408, tk=1024, name="mm_dact", out_dtype=BF16)
    g["ffn_w_out"] = _matmul(s["act"], df, ta=True, tm=1408, tn=1024, tk=512, name="mm_dw_ffn_out")
    dpre, g["ffn_conv_w"], g["ffn_conv_b"] = _ffact_bwd(s["pre"], wts["ffn_conv_w"], wts["ffn_conv_b"], dact)
    dh2 = _matmul(dpre, wts["ffn_w_in"], tb=True, tm=512, tn=1024, tk=512, name="mm_dh2")
    g["ffn_w_in"] = _matmul(s["h2"], dpre, ta=True, tm=1024, tn=512, tk=512, name="mm_dw_ffn_in")
    dx1, g["norm_pre_ffn"] = _norm_bwd(s["x1"], wts["norm_pre_ffn"], dh2, dx2, "norm_pre_ffn_bwd")
    dmix, g["norm_post_mix"] = _norm_bwd(s["mix"], wts["norm_post_mix"], dx1, None, "norm_post_mix_bwd")
    dcat = _matmul(dmix, wts["w_out"], tb=True, tm=512, tn=1024, tk=1024, name="mm_dcat")
    g["w_out"] = _matmul(s["cat"], dmix, ta=True, tm=1024, tn=1024, tk=512, name="mm_dw_out")
    dproj = _attn_bwd(s["proj"], *tables, s["cat"], s["lse"], dcat)
    dproj, dc, g["dn_a_log"], g["dn_dt_bias"], g["dn_norm_w"] = _delta_bwd(
        s["c_qkv"], s["proj"], wts["dn_a_log"], wts["dn_dt_bias"], wts["dn_norm_w"], s["states"], dcat, dproj)
    dproj, g["dn_conv_w"] = _dnconv_bwd(s["proj"], wts["dn_conv_w"], dc, dproj)
    dh = _matmul(dproj, wts["w_in"], tb=True, tm=512, tn=1024, tk=768, name="mm_dh")
    g["w_in"] = _matmul(s["h"], dproj, ta=True, tm=1024, tn=768, tk=512, name="mm_dw_in")
    dx, g["norm_pre_mix"] = _norm_bwd(s["x"], wts["norm_pre_mix"], dh, dx1, "norm_pre_mix_bwd")
    return dx, g


BIG = ("w_in", "w_out", "ffn_w_in", "ffn_w_out")
SMALL_SHARDED = ("dn_conv_w", "ffn_conv_w")
REPLICATED = ("dn_a_log", "dn_dt_bias", "dn_norm_w", "ffn_conv_b", "norm_pre_mix", "norm_post_mix", "norm_pre_ffn",
              "norm_post_ffn")
WEIGHTS = ("w_in", "dn_conv_w", "dn_a_log", "dn_dt_bias", "dn_norm_w", "w_out", "ffn_w_in", "ffn_conv_w", "ffn_conv_b",
           "ffn_w_out", "norm_pre_mix", "norm_post_mix", "norm_pre_ffn", "norm_post_ffn")
FULL_SHAPE = dict(dn_conv_w=(DEPTH, 4, 1536), ffn_conv_w=(DEPTH, 3, 2 * D_FF), dn_a_log=(DEPTH, NDH),
                  dn_dt_bias=(DEPTH, NDH), dn_norm_w=(DEPTH, 128), ffn_conv_b=(DEPTH, 2 * D_FF),
                  norm_pre_mix=(DEPTH, D_MODEL), norm_post_mix=(DEPTH, D_MODEL), norm_pre_ffn=(DEPTH, D_MODEL),
                  norm_post_ffn=(DEPTH, D_MODEL))
SMALL_GRAD_ORDER = REPLICATED + SMALL_SHARDED
SMALL_GRAD_ROWS = 520
SMALL_W_ROWS = 48
SMALL_ADAM_ROWS = 200


def _w_in_to_kernel_order(t):
    t = jnp.concatenate([_pairs_from_qkv(t[..., :QKV_W]), t[..., QKV_W:]], axis=-1)
    return jnp.pad(t, [(0, 0)] * (t.ndim - 1) + [(0, IN_PAD - IN_COLS)])


def _w_in_from_kernel_order(t):
    return jnp.concatenate([_qkv_from_pairs(t[..., :QKV_W]), t[..., QKV_W:IN_COLS]], axis=-1)


def _cols_to_devices(t):
    nr, nc = t.shape
    return t.reshape(nr, N_DEV, nc // N_DEV).transpose(1, 0, 2)


def _cols_from_devices(t):
    return t.transpose(1, 0, 2).reshape(t.shape[1], -1)


def kernel(x, w_in, dn_conv_w, dn_a_log, dn_dt_bias, dn_norm_w, w_out, ffn_w_in, ffn_conv_w, ffn_conv_b, ffn_w_out, norm_pre_mix, norm_post_mix, norm_pre_ffn, norm_post_ffn, loss_target, m_w_in, m_dn_conv_w, m_dn_a_log, m_dn_dt_bias, m_dn_norm_w, m_w_out, m_ffn_w_in, m_ffn_conv_w, m_ffn_conv_b, m_ffn_w_out, m_norm_pre_mix, m_norm_post_mix, m_norm_pre_ffn, m_norm_post_ffn, v_w_in, v_dn_conv_w, v_dn_a_log, v_dn_dt_bias, v_dn_norm_w, v_w_out, v_ffn_w_in, v_ffn_conv_w, v_ffn_conv_b, v_ffn_w_out, v_norm_pre_mix, v_norm_post_mix, v_norm_pre_ffn, v_norm_post_ffn):
    local = dict(w_in=w_in, dn_conv_w=dn_conv_w, dn_a_log=dn_a_log, dn_dt_bias=dn_dt_bias, dn_norm_w=dn_norm_w,
                 w_out=w_out, ffn_w_in=ffn_w_in, ffn_conv_w=ffn_conv_w, ffn_conv_b=ffn_conv_b, ffn_w_out=ffn_w_out,
                 norm_pre_mix=norm_pre_mix, norm_post_mix=norm_post_mix, norm_pre_ffn=norm_pre_ffn,
                 norm_post_ffn=norm_post_ffn)
    mom_m = dict(w_in=m_w_in, dn_conv_w=m_dn_conv_w, dn_a_log=m_dn_a_log, dn_dt_bias=m_dn_dt_bias,
                 dn_norm_w=m_dn_norm_w, w_out=m_w_out, ffn_w_in=m_ffn_w_in, ffn_conv_w=m_ffn_conv_w,
                 ffn_conv_b=m_ffn_conv_b, ffn_w_out=m_ffn_w_out, norm_pre_mix=m_norm_pre_mix,
                 norm_post_mix=m_norm_post_mix, norm_pre_ffn=m_norm_pre_ffn, norm_post_ffn=m_norm_post_ffn)
    mom_v = dict(w_in=v_w_in, dn_conv_w=v_dn_conv_w, dn_a_log=v_dn_a_log, dn_dt_bias=v_dn_dt_bias,
                 dn_norm_w=v_dn_norm_w, w_out=v_w_out, ffn_w_in=v_ffn_w_in, ffn_conv_w=v_ffn_conv_w,
                 ffn_conv_b=v_ffn_conv_b, ffn_w_out=v_ffn_w_out, norm_pre_mix=v_norm_pre_mix,
                 norm_post_mix=v_norm_post_mix, norm_pre_ffn=v_norm_pre_ffn, norm_post_ffn=v_norm_post_ffn)
    dev = 4 * lax.axis_index("x") + 2 * lax.axis_index("y") + lax.axis_index("c")
    core = lax.axis_index("c").astype(jnp.int32).reshape(1)

    small_w = _pack([dn_conv_w, ffn_conv_w], SMALL_W_ROWS)
    shards = [local[n][l].astype(BF16) for n in BIG for l in range(DEPTH)]
    gathered = _all_gather_hbm(shards + [small_w], "weights_all_gather")
    g_small = gathered[-1]
    gathered = {n: gathered[i * DEPTH:(i + 1) * DEPTH] for i, n in enumerate(BIG)}
    n_dn, n_ff = DEPTH * 4 * 192, DEPTH * 3 * 704
    sm = g_small.reshape(N_DEV, -1)
    full_dn_conv = sm[:, :n_dn].reshape(N_DEV, DEPTH, 4, 192).transpose(1, 2, 0, 3).reshape(DEPTH, 4, 1536)
    full_ff_conv = _interleave_ff(
        sm[:, n_dn:n_dn + n_ff].reshape(N_DEV, DEPTH, 3, 704).transpose(1, 2, 0, 3).reshape(DEPTH, 3, 2 * D_FF))

    def layer_weights(l):
        wts = dict(
            w_in=_w_in_to_kernel_order(_cols_from_devices(gathered["w_in"][l])),
            w_out=gathered["w_out"][l].reshape(D_MODEL, D_MODEL),
            ffn_w_in=_interleave_ff(_cols_from_devices(gathered["ffn_w_in"][l])),
            ffn_w_out=gathered["ffn_w_out"][l].reshape(D_FF, D_MODEL),
            dn_conv_w=full_dn_conv[l], ffn_conv_w=full_ff_conv[l],
            ffn_conv_b=_interleave_ff(_row(ffn_conv_b[l])),
            dn_a_log=_row(dn_a_log[l], 128), dn_dt_bias=_row(dn_dt_bias[l], 128))
        for n in ("dn_norm_w", "norm_pre_mix", "norm_post_mix", "norm_pre_ffn", "norm_post_ffn"):
            wts[n] = _row(local[n][l])
        return wts

    tables = _rope_tables()
    weights = [layer_weights(l) for l in range(DEPTH)]
    act, saved = x[0], []
    for l in range(DEPTH):
        act, s = _layer_fwd(act, weights[l], tables)
        saved.append(s)
    loss_part, dact = _loss_fwd_bwd(act, loss_target[0])
    grads = [None] * DEPTH
    for l in reversed(range(DEPTH)):
        dact, grads[l] = _layer_bwd(dact, weights[l], saved[l], tables)
    grad_x = dact[None]

    def to_devices(name, l):
        t = grads[l][name]
        if name == "w_in":
            return _cols_to_devices(_w_in_from_kernel_order(t))
        if name == "ffn_w_in":
            return _cols_to_devices(_deinterleave_ff(t))
        return t.reshape(N_DEV, t.shape[0] // N_DEV, t.shape[1])

    to_dev = [to_devices(n, l) for n in BIG for l in range(DEPTH)]
    from_sibling = _exchange_sibling(to_dev, "grads_to_sibling")
    chip_sums = [_pair_add(gd, r, core, "grads_pair_add_%d" % i) for i, (gd, r) in enumerate(zip(to_dev, from_sibling))]
    parts = _exchange_chips([chip_sums[i * DEPTH:(i + 1) * DEPTH] for i in range(len(BIG))], "grads_to_chips")

    def small_grad(name):
        t = jnp.stack([grads[l][name] for l in range(DEPTH)])
        if name in ("dn_a_log", "dn_dt_bias"):
            t = t[:, 0, :NDH]
        if name in ("ffn_conv_w", "ffn_conv_b"):
            t = _deinterleave_ff(t)
        return t.reshape(FULL_SHAPE[name])

    small_part = _pack([small_grad(n) for n in SMALL_GRAD_ORDER] + [loss_part[0, :1]], SMALL_GRAD_ROWS)
    small_sum = _all_gather_sum_small(small_part)
    small_g = dict(zip(SMALL_GRAD_ORDER + ("loss",), _unpack(small_sum, [FULL_SHAPE[n] for n in SMALL_GRAD_ORDER] + [(1,)])))
    loss = small_g["loss"][0]
    small_g["dn_conv_w"] = lax.dynamic_slice_in_dim(small_g["dn_conv_w"], dev * 192, 192, axis=2)
    small_g["ffn_conv_w"] = lax.dynamic_slice_in_dim(small_g["ffn_conv_w"], dev * 704, 704, axis=2)

    out_g, out_d, out_m, out_v = {}, {}, {}, {}
    for n, p in zip(BIG, parts):
        out_g[n], out_d[n], out_m[n], out_v[n] = _adamw_sharded(p, local[n], mom_m[n], mom_v[n], ADAM_ROWS[n], "adamw_" + n)
    shapes = [small_g[n].shape for n in SMALL_GRAD_ORDER]
    d_s, m_s, v_s = _adamw_small(_pack([small_g[n] for n in SMALL_GRAD_ORDER], SMALL_ADAM_ROWS),
                                 _pack([local[n] for n in SMALL_GRAD_ORDER], SMALL_ADAM_ROWS),
                                 _pack([mom_m[n] for n in SMALL_GRAD_ORDER], SMALL_ADAM_ROWS),
                                 _pack([mom_v[n] for n in SMALL_GRAD_ORDER], SMALL_ADAM_ROWS))
    for n, d, m, v in zip(SMALL_GRAD_ORDER, _unpack(d_s, shapes), _unpack(m_s, shapes), _unpack(v_s, shapes)):
        out_g[n], out_d[n], out_m[n], out_v[n] = small_g[n], d, m, v
    return (loss, grad_x, *[out_g[n] for n in WEIGHTS], *[out_d[n] for n in WEIGHTS],
            *[out_m[n] for n in WEIGHTS], *[out_v[n] for n in WEIGHTS])
```

```python
import functools
import math

import jax
import jax.numpy as jnp
from jax import lax
from jax.experimental import pallas as pl
from jax.experimental.pallas import tpu as pltpu

F32 = jnp.float32
BF16 = jnp.bfloat16
HI = lax.Precision.HIGHEST
MESH = pl.DeviceIdType.MESH

N_DEV = 8
SEQ = 2048
D_MODEL = 1024
DEPTH = 2
N_PAIR = 4
HEAD_DIM = 64
ATTN_W = 512
ATTN_BLK = 128
DILATIONS = (1, 4, 16)
SEGMENT_BLOCKS = (16, 4, 1)
N_BLK = SEQ // ATTN_BLK
NDH = 4
CH = 64
NCH = SEQ // CH
IN_COLS = 3592
IN_PAD = 3840
QKV_W = 3 * ATTN_W
DN_QKV_BLK0 = QKV_W // 128
DN_QKV_BLKS = 1536 // 128
DN_Z_COL = 3072
DN_TAIL_BLK = 3584 // 128
D_FF = 2816
FF_BLKS = D_FF // 128
EPS = 1e-6
NEG = -1e30
ROPE_THETA = 10000.0

ADAM_LR, ADAM_B1, ADAM_B2, ADAM_EPS, ADAM_WD, ADAM_STEP = 0.001, 0.9, 0.999, 1e-08, 0.01, 10

VMEM_LIMIT = 56 * 1024 * 1024


def _cp(*sem):
    return pltpu.CompilerParams(dimension_semantics=sem, vmem_limit_bytes=VMEM_LIMIT)


def _dot(a, b, dims, precision=None):
    if precision is None:
        a = a.astype(BF16)
        b = b.astype(BF16)
    return lax.dot_general(a, b, (dims, ((), ())), preferred_element_type=F32, precision=precision)


def _make_mm(precision):
    @jax.custom_vjp
    def nn(a, b):
        return _dot(a, b, ((1,), (0,)), precision)

    @jax.custom_vjp
    def nt(a, b):
        return _dot(a, b, ((1,), (1,)), precision)

    @jax.custom_vjp
    def tn(a, b):
        return _dot(a, b, ((0,), (0,)), precision)

    nn.defvjp(lambda a, b: (nn(a, b), (a, b)), lambda r, g: (nt(g, r[1]), tn(r[0], g)))
    nt.defvjp(lambda a, b: (nt(a, b), (a, b)), lambda r, g: (nn(g, r[1]), tn(g, r[0])))
    tn.defvjp(lambda a, b: (tn(a, b), (a, b)), lambda r, g: (nt(r[1], g), nn(r[0], g)))
    return nn, nt, tn


MM, MM_NT, MM_TN = _make_mm(None)
MMH, _, _ = _make_mm(HI)


def _matmul(a, b, *, ta=False, tb=False, tm, tn, tk, name, out_dtype=F32):
    (k_dim, m_dim) = a.shape if ta else a.shape[::-1]
    (n_dim, k2) = b.shape if tb else b.shape[::-1]
    assert k_dim == k2 and m_dim % tm == 0 and n_dim % tn == 0 and k_dim % tk == 0, (a.shape, b.shape, tm, tn, tk)
    nk = k_dim // tk
    dims = ((0 if ta else 1,), (1 if tb else 0,))

    def body(a_ref, b_ref, o_ref, acc_ref):
        k = pl.program_id(2)
        p = _dot(a_ref[...], b_ref[...], dims)

        @pl.when(k == 0)
        def _():
            acc_ref[...] = p

        @pl.when(k > 0)
        def _():
            acc_ref[...] += p

        @pl.when(k == nk - 1)
        def _():
            o_ref[...] = acc_ref[...].astype(out_dtype)

    a_spec = pl.BlockSpec((tk, tm), lambda i, j, k: (k, i)) if ta else pl.BlockSpec((tm, tk), lambda i, j, k: (i, k))
    b_spec = pl.BlockSpec((tn, tk), lambda i, j, k: (j, k)) if tb else pl.BlockSpec((tk, tn), lambda i, j, k: (k, j))
    return pl.pallas_call(
        body, name=name,
        grid=(m_dim // tm, n_dim // tn, nk),
        in_specs=[a_spec, b_spec],
        out_specs=pl.BlockSpec((tm, tn), lambda i, j, k: (i, j)),
        out_shape=jax.ShapeDtypeStruct((m_dim, n_dim), out_dtype),
        scratch_shapes=[pltpu.VMEM((tm, tn), F32)],
        compiler_params=_cp("parallel", "parallel", "arbitrary"),
    )(a, b)


NORM_ROWS = 256


def _rms(x, w):
    return x * lax.rsqrt(jnp.mean(x * x, axis=1, keepdims=True) + EPS) * w


def _norm_fwd(x, w_row, name, out_dtype=BF16):
    def body(x_ref, w_ref, o_ref):
        o_ref[...] = _rms(x_ref[...], w_ref[...]).astype(out_dtype)

    return pl.pallas_call(
        body, name=name, grid=(SEQ // NORM_ROWS,),
        in_specs=[pl.BlockSpec((NORM_ROWS, D_MODEL), lambda i: (i, 0)), pl.BlockSpec((1, D_MODEL), lambda i: (0, 0))],
        out_specs=pl.BlockSpec((NORM_ROWS, D_MODEL), lambda i: (i, 0)),
        out_shape=jax.ShapeDtypeStruct((SEQ, D_MODEL), out_dtype),
        compiler_params=_cp("parallel"),
    )(x, w_row)


def _resnorm_fwd(x, f, w_row, name):
    def body(x_ref, f_ref, w_ref, o_ref):
        o_ref[...] = x_ref[...] + _rms(f_ref[...], w_ref[...])

    blk = pl.BlockSpec((NORM_ROWS, D_MODEL), lambda i: (i, 0))
    return pl.pallas_call(
        body, name=name, grid=(SEQ // NORM_ROWS,),
        in_specs=[blk, blk, pl.BlockSpec((1, D_MODEL), lambda i: (0, 0))],
        out_specs=blk, out_shape=jax.ShapeDtypeStruct((SEQ, D_MODEL), F32),
        compiler_params=_cp("parallel"),
    )(x, f, w_row)


def _norm_bwd(x, w_row, dy, add, name):
    has_add = add is not None

    def body(*refs):
        if has_add:
            x_ref, w_ref, dy_ref, add_ref, dx_ref, dw_ref = refs
        else:
            x_ref, w_ref, dy_ref, dx_ref, dw_ref = refs
        _, vjp = jax.vjp(_rms, x_ref[...], w_ref[...])
        dx, dw = vjp(dy_ref[...])
        dx_ref[...] = dx + add_ref[...] if has_add else dx

        @pl.when(pl.program_id(0) == 0)
        def _():
            dw_ref[...] = jnp.zeros_like(dw_ref)

        dw_ref[...] += dw

    blk = pl.BlockSpec((NORM_ROWS, D_MODEL), lambda i: (i, 0))
    row = pl.BlockSpec((1, D_MODEL), lambda i: (0, 0))
    ins = [x, w_row, dy] + ([add] if has_add else [])
    return pl.pallas_call(
        body, name=name, grid=(SEQ // NORM_ROWS,),
        in_specs=[blk, row, blk] + ([blk] if has_add else []),
        out_specs=[blk, row],
        out_shape=[jax.ShapeDtypeStruct((SEQ, D_MODEL), F32), jax.ShapeDtypeStruct((1, D_MODEL), F32)],
        compiler_params=_cp("arbitrary"),
    )(*ins)


def _loss_fwd_bwd(y, target):
    def body(y_ref, t_ref, loss_ref, dy_ref):
        err = y_ref[...] - t_ref[...]
        dy_ref[...] = err * (1.0 / D_MODEL)

        @pl.when(pl.program_id(0) == 0)
        def _():
            loss_ref[...] = jnp.zeros_like(loss_ref)

        part = jnp.sum(jnp.sum(err * err, axis=1, keepdims=True) * (1.0 / D_MODEL), axis=0, keepdims=True)
        loss_ref[...] += 0.5 * jnp.broadcast_to(part, loss_ref.shape)

    blk = pl.BlockSpec((NORM_ROWS, D_MODEL), lambda i: (i, 0))
    return pl.pallas_call(
        body, name="loss", grid=(SEQ // NORM_ROWS,),
        in_specs=[blk, blk],
        out_specs=[pl.BlockSpec((1, 128), lambda i: (0, 0)), blk],
        out_shape=[jax.ShapeDtypeStruct((1, 128), F32), jax.ShapeDtypeStruct((SEQ, D_MODEL), F32)],
        compiler_params=_cp("arbitrary"),
    )(y, target)


def _make_shift(j):
    def down(x):
        row = lax.broadcasted_iota(jnp.int32, x.shape, 0)
        return jnp.where(row >= j, pltpu.roll(x, j, 0), 0.0)

    def up(x):
        n = x.shape[0]
        row = lax.broadcasted_iota(jnp.int32, x.shape, 0)
        return jnp.where(row < n - j, pltpu.roll(x, n - j, 0), 0.0)

    f = jax.custom_vjp(down)
    f.defvjp(lambda x: (down(x), None), lambda _, g: (up(g),))
    return f


_SHIFT = {j: _make_shift(j) for j in (1, 2, 3)}


def _causal_conv(x, taps):
    n = len(taps)
    acc = x * taps[n - 1]
    for k in range(n - 1):
        acc = acc + _SHIFT[n - 1 - k](x) * taps[k]
    return acc


def _tap_rows(w_ref, lanes=slice(None)):
    return tuple(w_ref[k:k + 1, lanes] for k in range(w_ref.shape[0]))


def _sigmoid(x):
    return 1.0 / (1.0 + jnp.exp(-x))


def _silu(x):
    return x * _sigmoid(x)


def _softplus(x):
    return jnp.maximum(x, 0.0) + jnp.log(1.0 + jnp.exp(-jnp.abs(x)))


def _gelu_tanh(x):
    return 0.5 * x * (1.0 + jnp.tanh(math.sqrt(2.0 / math.pi) * (x + 0.044715 * (x * x * x))))


def _dnconv_fn(x, taps):
    return _silu(_causal_conv(x, taps))


def _dnconv_fwd(proj, conv_w):
    def body(x_ref, w_ref, o_ref):
        o_ref[...] = _dnconv_fn(x_ref[...], _tap_rows(w_ref))

    return pl.pallas_call(
        body, name="dnconv_fwd", grid=(DN_QKV_BLKS,),
        in_specs=[pl.BlockSpec((SEQ, 128), lambda j: (0, DN_QKV_BLK0 + j)), pl.BlockSpec((4, 128), lambda j: (0, j))],
        out_specs=pl.BlockSpec((SEQ, 128), lambda j: (0, j)),
        out_shape=jax.ShapeDtypeStruct((SEQ, 1536), F32),
        compiler_params=_cp("parallel"),
    )(proj, conv_w)


def _dnconv_bwd(proj, conv_w, dc, dproj):
    def body(x_ref, w_ref, dc_ref, _, dx_ref, dw_ref):
        _, vjp = jax.vjp(_dnconv_fn, x_ref[...], _tap_rows(w_ref))
        dx, dw = vjp(dc_ref[...])
        dx_ref[...] = dx
        for k, row in enumerate(dw):
            dw_ref[k:k + 1, :] = row

    return pl.pallas_call(
        body, name="dnconv_bwd", grid=(DN_QKV_BLKS,),
        in_specs=[pl.BlockSpec((SEQ, 128), lambda j: (0, DN_QKV_BLK0 + j)), pl.BlockSpec((4, 128), lambda j: (0, j)),
                  pl.BlockSpec((SEQ, 128), lambda j: (0, j)), pl.BlockSpec(memory_space=pl.ANY)],
        out_specs=[pl.BlockSpec((SEQ, 128), lambda j: (0, DN_QKV_BLK0 + j)), pl.BlockSpec((4, 128), lambda j: (0, j))],
        out_shape=[jax.ShapeDtypeStruct((SEQ, IN_PAD), F32), jax.ShapeDtypeStruct((4, 1536), F32)],
        input_output_aliases={3: 0},
        compiler_params=_cp("parallel"),
    )(proj, conv_w, dc, dproj)


def _ffact_fn(pg, pu, wg, wu, bg, bu):
    return _gelu_tanh(_causal_conv(pg, wg) + bg) * (_causal_conv(pu, wu) + bu)


def _ffact_args(p_ref, w_ref, b_ref):
    g, u = slice(0, 128), slice(128, 256)
    return (p_ref[:, g], p_ref[:, u], _tap_rows(w_ref, g), _tap_rows(w_ref, u), b_ref[:, g], b_ref[:, u])


def _ffact_fwd(pre, conv_w, conv_b):
    def body(p_ref, w_ref, b_ref, o_ref):
        o_ref[...] = _ffact_fn(*_ffact_args(p_ref, w_ref, b_ref)).astype(BF16)

    return pl.pallas_call(
        body, name="ffact_fwd", grid=(FF_BLKS,),
        in_specs=[pl.BlockSpec((SEQ, 256), lambda j: (0, j)), pl.BlockSpec((3, 256), lambda j: (0, j)),
                  pl.BlockSpec((1, 256), lambda j: (0, j))],
        out_specs=pl.BlockSpec((SEQ, 128), lambda j: (0, j)),
        out_shape=jax.ShapeDtypeStruct((SEQ, D_FF), BF16),
        compiler_params=_cp("parallel"),
    )(pre, conv_w, conv_b)


def _ffact_bwd(pre, conv_w, conv_b, dact):
    def body(p_ref, w_ref, b_ref, da_ref, dp_ref, dw_ref, db_ref):
        _, vjp = jax.vjp(_ffact_fn, *_ffact_args(p_ref, w_ref, b_ref))
        dpg, dpu, dwg, dwu, dbg, dbu = vjp(da_ref[...].astype(F32))
        dp_ref[:, 0:128] = dpg
        dp_ref[:, 128:256] = dpu
        for k in range(3):
            dw_ref[k:k + 1, 0:128] = dwg[k]
            dw_ref[k:k + 1, 128:256] = dwu[k]
        db_ref[:, 0:128] = dbg
        db_ref[:, 128:256] = dbu

    return pl.pallas_call(
        body, name="ffact_bwd", grid=(FF_BLKS,),
        in_specs=[pl.BlockSpec((SEQ, 256), lambda j: (0, j)), pl.BlockSpec((3, 256), lambda j: (0, j)),
                  pl.BlockSpec((1, 256), lambda j: (0, j)), pl.BlockSpec((SEQ, 128), lambda j: (0, j))],
        out_specs=[pl.BlockSpec((SEQ, 256), lambda j: (0, j)), pl.BlockSpec((3, 256), lambda j: (0, j)),
                   pl.BlockSpec((1, 256), lambda j: (0, j))],
        out_shape=[jax.ShapeDtypeStruct((SEQ, 2 * D_FF), F32), jax.ShapeDtypeStruct((3, 2 * D_FF), F32),
                   jax.ShapeDtypeStruct((1, 2 * D_FF), F32)],
        compiler_params=_cp("parallel"),
    )(pre, conv_w, conv_b, dact)


def _interleave_ff(t):
    lead = t.shape[:-1]
    return t.reshape(lead + (2, FF_BLKS, 128)).swapaxes(-3, -2).reshape(lead + (2 * D_FF,))


def _deinterleave_ff(t):
    lead = t.shape[:-1]
    return t.reshape(lead + (FF_BLKS, 2, 128)).swapaxes(-3, -2).reshape(lead + (2 * D_FF,))


def _rope_tables():
    inv = 1.0 / (ROPE_THETA ** (jnp.arange(0, HEAD_DIM, 2, dtype=F32) / HEAD_DIM))
    ang = jnp.arange(SEQ, dtype=F32)[:, None] * inv[None, :]
    cos = jnp.tile(jnp.cos(ang), (1, 4))
    sin = jnp.tile(jnp.sin(ang), (1, 4))
    sign = jnp.where((jnp.arange(128) % HEAD_DIM) < HEAD_DIM // 2, -1.0, 1.0).astype(F32)
    return cos, sin * sign[None, :]


def _rope(x, cos, sin_signed):
    lane = lax.broadcasted_iota(jnp.int32, x.shape, 1)
    partner = jnp.where((lane % HEAD_DIM) < HEAD_DIM // 2, pltpu.roll(x, 128 - HEAD_DIM // 2, 1),
                        pltpu.roll(x, HEAD_DIM // 2, 1))
    return x * cos + partner * sin_signed


def _pairs_from_qkv(t):
    lead = t.shape[:-1]
    return t.reshape(lead + (3, N_PAIR, 128)).swapaxes(-3, -2).reshape(lead + (QKV_W,))


def _qkv_from_pairs(t):
    lead = t.shape[:-1]
    return t.reshape(lead + (N_PAIR, 3, 128)).swapaxes(-3, -2).reshape(lead + (QKV_W,))


def _band_masks():
    a = lax.broadcasted_iota(jnp.int32, (ATTN_BLK, ATTN_BLK), 0)
    c = lax.broadcasted_iota(jnp.int32, (ATTN_BLK, ATTN_BLK), 1)
    return c >= a, c <= a


def _head_masks():
    lane = lax.broadcasted_iota(jnp.int32, (1, 128), 1)
    return [(lane // HEAD_DIM) == h for h in range(2)]


def _block_rows(branch, t):
    d, per_seg = DILATIONS[branch], SEGMENT_BLOCKS[branch]
    if d == 1:
        start = pl.multiple_of(t * ATTN_BLK, ATTN_BLK)
        prev = pl.multiple_of(jnp.maximum(t - 1, 0) * ATTN_BLK, ATTN_BLK)
        return pl.ds(start, ATTN_BLK), pl.ds(prev, ATTN_BLK), t > 0
    r, n = t // per_seg, t % per_seg
    start = n * (ATTN_BLK * d) + r
    prev = jnp.maximum(n - 1, 0) * (ATTN_BLK * d) + r
    return pl.ds(start, ATTN_BLK, stride=d), pl.ds(prev, ATTN_BLK, stride=d), n > 0


def _attn_fwd(proj, cos, sin_signed):
    scale = HEAD_DIM ** -0.5

    def body(qkv_ref, cos_ref, sin_ref, out_ref, lse_ref, q_s, k_s, v_s, *branch_s):
        o_s, l_s = branch_s[:3], branch_s[3:]
        q_s[...] = _rope(qkv_ref[:, 0:128], cos_ref[...], sin_ref[...])
        k_s[...] = _rope(qkv_ref[:, 128:256], cos_ref[...], sin_ref[...])
        v_s[...] = qkv_ref[:, 256:384]
        m_prev0, m_cur = _band_masks()
        heads = _head_masks()
        for branch in range(3):
            def block(t, carry, branch=branch):
                rows, prows, has_prev = _block_rows(branch, t)
                m_prev = m_prev0 & has_prev
                q, kc, vc = q_s[rows, :], k_s[rows, :], v_s[rows, :]
                kp, vp = k_s[prows, :], v_s[prows, :]
                outs, lses = [], []
                for hm in heads:
                    qh = jnp.where(hm, q, 0.0)
                    sp = jnp.where(m_prev, MM_NT(qh, kp) * scale, NEG)
                    sc = jnp.where(m_cur, MM_NT(qh, kc) * scale, NEG)
                    m = jnp.maximum(jnp.max(sp, axis=1, keepdims=True), jnp.max(sc, axis=1, keepdims=True))
                    ep = jnp.exp(sp - m)
                    ec = jnp.exp(sc - m)
                    l = jnp.sum(ep, axis=1, keepdims=True) + jnp.sum(ec, axis=1, keepdims=True)
                    outs.append((MM(ep, vp) + MM(ec, vc)) / l)
                    lses.append(m + jnp.log(l))
                o_s[branch][rows, :] = jnp.where(heads[0], outs[0], outs[1])
                l_s[branch][rows, :] = jnp.where(heads[0], lses[0], lses[1])
                return carry

            lax.fori_loop(0, N_BLK, block, 0)
        l0, l1, l2 = l_s[0][...], l_s[1][...], l_s[2][...]
        m = jnp.maximum(jnp.maximum(l0, l1), l2)
        w0, w1, w2 = jnp.exp(l0 - m), jnp.exp(l1 - m), jnp.exp(l2 - m)
        den = w0 + w1 + w2
        out_ref[...] = (w0 * o_s[0][...] + w1 * o_s[1][...] + w2 * o_s[2][...]) / den
        lse_ref[...] = m + jnp.log(den)

    tab = pl.BlockSpec((SEQ, 128), lambda j: (0, 0))
    col = pl.BlockSpec((SEQ, 128), lambda j: (0, j))
    return pl.pallas_call(
        body, name="attn_fwd", grid=(N_PAIR,),
        in_specs=[pl.BlockSpec((SEQ, 384), lambda j: (0, j)), tab, tab],
        out_specs=[col, col],
        out_shape=[jax.ShapeDtypeStruct((SEQ, 2 * ATTN_W), F32), jax.ShapeDtypeStruct((SEQ, ATTN_W), F32)],
        scratch_shapes=[pltpu.VMEM((SEQ, 128), F32)] * 9,
        compiler_params=_cp("parallel"),
    )(proj, cos, sin_signed)


def _attn_bwd(proj, cos, sin_signed, cat, lse, dcat):
    scale = HEAD_DIM ** -0.5

    def body(qkv_ref, cos_ref, sin_ref, out_ref, lse_ref, do_ref, dqkv_ref, q_s, k_s, v_s, dq_s, dk_s, dv_s, dod_s):
        q_s[...] = _rope(qkv_ref[:, 0:128], cos_ref[...], sin_ref[...])
        k_s[...] = _rope(qkv_ref[:, 128:256], cos_ref[...], sin_ref[...])
        v_s[...] = qkv_ref[:, 256:384]
        dq_s[...] = jnp.zeros_like(dq_s)
        dk_s[...] = jnp.zeros_like(dk_s)
        dv_s[...] = jnp.zeros_like(dv_s)
        dod_s[...] = do_ref[...] * out_ref[...]
        m_prev0, m_cur = _band_masks()
        heads = _head_masks()
        for branch in range(3):
            def block(t, carry, branch=branch):
                rows, prows, has_prev = _block_rows(branch, t)
                m_prev = m_prev0 & has_prev
                q, kc, vc = q_s[rows, :], k_s[rows, :], v_s[rows, :]
                kp, vp = k_s[prows, :], v_s[prows, :]
                do, lse_b, dod = do_ref[rows, :], lse_ref[rows, :], dod_s[rows, :]
                dq = []
                dk_cur = jnp.zeros((ATTN_BLK, 128), F32)
                dv_cur = jnp.zeros((ATTN_BLK, 128), F32)
                dk_prev = jnp.zeros((ATTN_BLK, 128), F32)
                dv_prev = jnp.zeros((ATTN_BLK, 128), F32)
                for hm in heads:
                    qh = jnp.where(hm, q, 0.0)
                    doh = jnp.where(hm, do, 0.0)
                    lse_h = jnp.max(jnp.where(hm, lse_b, NEG), axis=1, keepdims=True)
                    delta = jnp.sum(jnp.where(hm, dod, 0.0), axis=1, keepdims=True)
                    pp = jnp.exp(jnp.where(m_prev, MM_NT(qh, kp) * scale, NEG) - lse_h)
                    pc = jnp.exp(jnp.where(m_cur, MM_NT(qh, kc) * scale, NEG) - lse_h)
                    dsp = pp * (MM_NT(doh, vp) - delta) * scale
                    dsc = pc * (MM_NT(doh, vc) - delta) * scale
                    dq.append(MM(dsp, kp) + MM(dsc, kc))
                    dk_prev += MM_TN(dsp, qh)
                    dk_cur += MM_TN(dsc, qh)
                    dv_prev += MM_TN(pp, doh)
                    dv_cur += MM_TN(pc, doh)
                dq_s[rows, :] += jnp.where(heads[0], dq[0], dq[1])
                dk_s[rows, :] += dk_cur
                dv_s[rows, :] += dv_cur

                @pl.when(has_prev)
                def _():
                    dk_s[prows, :] += dk_prev
                    dv_s[prows, :] += dv_prev

                return carry

            lax.fori_loop(0, N_BLK, block, 0)
        dqkv_ref[:, 0:128] = _rope(dq_s[...], cos_ref[...], -sin_ref[...])
        dqkv_ref[:, 128:256] = _rope(dk_s[...], cos_ref[...], -sin_ref[...])
        dqkv_ref[:, 256:384] = dv_s[...]

    tab = pl.BlockSpec((SEQ, 128), lambda j: (0, 0))
    col = pl.BlockSpec((SEQ, 128), lambda j: (0, j))
    qkv = pl.BlockSpec((SEQ, 384), lambda j: (0, j))
    return pl.pallas_call(
        body, name="attn_bwd", grid=(N_PAIR,),
        in_specs=[qkv, tab, tab, col, col, col],
        out_specs=qkv,
        out_shape=jax.ShapeDtypeStruct((SEQ, IN_PAD), F32),
        scratch_shapes=[pltpu.VMEM((SEQ, 128), F32)] * 7,
        compiler_params=_cp("parallel"),
    )(proj, cos, sin_signed, cat, lse, dcat)


def _bdot(a, b, dims, precision=None):
    if precision is None:
        a = a.astype(BF16)
        b = b.astype(BF16)
    return lax.dot_general(a, b, (dims, ((0,), (0,))), preferred_element_type=F32, precision=precision)


def _make_bmm(precision):
    @jax.custom_vjp
    def nn(a, b):
        return _bdot(a, b, ((2,), (1,)), precision)

    @jax.custom_vjp
    def nt(a, b):
        return _bdot(a, b, ((2,), (2,)), precision)

    @jax.custom_vjp
    def tn(a, b):
        return _bdot(a, b, ((1,), (1,)), precision)

    nn.defvjp(lambda a, b: (nn(a, b), (a, b)), lambda r, g: (nt(g, r[1]), tn(r[0], g)))
    nt.defvjp(lambda a, b: (nt(a, b), (a, b)), lambda r, g: (nn(g, r[1]), tn(g, r[0])))
    tn.defvjp(lambda a, b: (tn(a, b), (a, b)), lambda r, g: (nt(r[1], g), nn(r[0], g)))
    return nn, nt, tn


BMM, BMM_NT, BMM_TN = _make_bmm(None)
BMMH, _, _ = _make_bmm(HI)


def _head_lanes(t, off):
    lane = lax.broadcasted_iota(jnp.int32, (1, 128), 1)
    return jnp.concatenate(
        [jnp.sum(t * (lane == off + h).astype(F32), axis=1, keepdims=True)[None] for h in range(NDH)], axis=0)


def _delta_chunk(qr, kr, vr, z, tail, alog_row, dt_row, nw, state):
    c = qr.shape[1]
    beta = _sigmoid(_head_lanes(tail, 0))
    g = -jnp.exp(_head_lanes(alog_row, 0)) * _softplus(_head_lanes(tail, NDH) + _head_lanes(dt_row, 0))

    q = qr * lax.rsqrt(jnp.sum(qr * qr, axis=2, keepdims=True) + EPS) * (128 ** -0.5)
    k = kr * lax.rsqrt(jnp.sum(kr * kr, axis=2, keepdims=True) + EPS)

    ri = lax.broadcasted_iota(jnp.int32, (c, c), 0)
    ci = lax.broadcasted_iota(jnp.int32, (c, c), 1)
    tril = ri >= ci
    eye = (ri == ci).astype(F32)
    lane = lax.broadcasted_iota(jnp.int32, (1, 128), 1)
    g_lanes = sum(g[h] * (lane == h).astype(F32) for h in range(NDH))
    gc = _head_lanes(MMH(tril.astype(F32), g_lanes), 0)
    g_row = BMMH(jnp.ones((NDH, c, c), F32), eye * gc)
    decay = jnp.where(tril, jnp.exp(jnp.where(tril, gc - g_row, 0.0)), 0.0)
    kb = k * beta
    a_mat = jnp.where(ri > ci, BMM_NT(kb, k) * decay, 0.0)
    power = -a_mat
    t_inv = eye + power
    for _ in range(5):
        power = BMMH(power, power)
        t_inv = t_inv + BMMH(t_inv, power)
    eg = jnp.exp(gc)
    u = BMM(t_inv, vr * beta)
    w = BMM(t_inv, kb * eg)
    qk = BMM_NT(q, k) * decay
    g_tot = jnp.sum(g, axis=1, keepdims=True)
    v_new = u - BMM(w, state)
    o = BMM(q * eg, state) + BMM(qk, v_new)
    new_state = state * jnp.exp(g_tot) + BMM_TN(k * jnp.exp(g_tot - gc), v_new)
    on = o * lax.rsqrt(jnp.mean(o * o, axis=2, keepdims=True) + EPS) * nw
    return on * _silu(z), new_state


def _heads(v, off=0):
    return jnp.concatenate([v[None, :, off + 128 * h:off + 128 * (h + 1)] for h in range(NDH)], axis=0)


def _unheads(t):
    return jnp.concatenate([t[h] for h in range(NDH)], axis=1)


def _delta_fwd(c_qkv, proj, alog_row, dt_row, nw, cat):
    def body(c_ref, z_ref, tail_ref, al_ref, dt_ref, nw_ref, _, y_ref, st_ref, state):
        @pl.when(pl.program_id(0) == 0)
        def _():
            state[...] = jnp.zeros_like(state)

        cv = c_ref[...]
        st_ref[0] = state[...]
        y, new_state = _delta_chunk(_heads(cv), _heads(cv, 512), _heads(cv, 1024), _heads(z_ref[...]), tail_ref[...],
                                    al_ref[...], dt_ref[...], nw_ref[...], state[...])
        y_ref[...] = _unheads(y)
        state[...] = new_state

    row = pl.BlockSpec((1, 128), lambda n: (0, 0))
    return pl.pallas_call(
        body, name="delta_fwd", grid=(NCH,),
        in_specs=[pl.BlockSpec((CH, 1536), lambda n: (n, 0)), pl.BlockSpec((CH, 512), lambda n: (n, DN_Z_COL // 512)),
                  pl.BlockSpec((CH, 128), lambda n: (n, DN_TAIL_BLK)), row, row, row, pl.BlockSpec(memory_space=pl.ANY)],
        out_specs=[pl.BlockSpec((CH, 512), lambda n: (n, 1)),
                   pl.BlockSpec((1, NDH, 128, 128), lambda n: (n, 0, 0, 0))],
        out_shape=[jax.ShapeDtypeStruct((SEQ, 2 * ATTN_W), F32), jax.ShapeDtypeStruct((NCH, NDH, 128, 128), F32)],
        scratch_shapes=[pltpu.VMEM((NDH, 128, 128), F32)],
        input_output_aliases={6: 0},
        compiler_params=_cp("arbitrary"),
    )(c_qkv, proj, proj, alog_row, dt_row, nw, cat)


def _delta_bwd(c_qkv, proj, alog_row, dt_row, nw, states, dcat, dproj):
    def body(c_ref, z_ref, tail_ref, al_ref, dt_ref, nw_ref, st_ref, dy_ref, _,
             dp_ref, dc_ref, dal_ref, ddt_ref, dnw_ref, dstate):
        @pl.when(pl.program_id(0) == 0)
        def _():
            dstate[...] = jnp.zeros_like(dstate)
            dal_ref[...] = jnp.zeros_like(dal_ref)
            ddt_ref[...] = jnp.zeros_like(ddt_ref)
            dnw_ref[...] = jnp.zeros_like(dnw_ref)

        cv = c_ref[...]
        _, vjp = jax.vjp(_delta_chunk, _heads(cv), _heads(cv, 512), _heads(cv, 1024), _heads(z_ref[...]),
                         tail_ref[...], al_ref[...], dt_ref[...], nw_ref[...], st_ref[0])
        dq, dk, dv, dz, dtail, dal, ddt, dnw, dst = vjp((_heads(dy_ref[...]), dstate[...]))
        dstate[...] = dst
        dc_ref[...] = jnp.concatenate([_unheads(dq), _unheads(dk), _unheads(dv)], axis=1)
        dp_ref[...] = jnp.concatenate([_unheads(dz), dtail, jnp.zeros((CH, 128), F32)], axis=1)
        dal_ref[...] += dal
        ddt_ref[...] += ddt
        dnw_ref[...] += dnw

    rev = lambda n: NCH - 1 - n
    row = pl.BlockSpec((1, 128), lambda n: (0, 0))
    return pl.pallas_call(
        body, name="delta_bwd", grid=(NCH,),
        in_specs=[pl.BlockSpec((CH, 1536), lambda n: (rev(n), 0)),
                  pl.BlockSpec((CH, 512), lambda n: (rev(n), DN_Z_COL // 512)),
                  pl.BlockSpec((CH, 128), lambda n: (rev(n), DN_TAIL_BLK)), row, row, row,
                  pl.BlockSpec((1, NDH, 128, 128), lambda n: (rev(n), 0, 0, 0)),
                  pl.BlockSpec((CH, 512), lambda n: (rev(n), 1)), pl.BlockSpec(memory_space=pl.ANY)],
        out_specs=[pl.BlockSpec((CH, 768), lambda n: (rev(n), DN_Z_COL // 768)),
                   pl.BlockSpec((CH, 1536), lambda n: (rev(n), 0)), row, row, row],
        out_shape=[jax.ShapeDtypeStruct((SEQ, IN_PAD), F32), jax.ShapeDtypeStruct((SEQ, 1536), F32)]
        + [jax.ShapeDtypeStruct((1, 128), F32)] * 3,
        scratch_shapes=[pltpu.VMEM((NDH, 128, 128), F32)],
        input_output_aliases={8: 0},
        compiler_params=_cp("arbitrary"),
    )(c_qkv, proj, proj, alog_row, dt_row, nw, states, dcat, dproj)


def _place():
    x, y, c = lax.axis_index("x"), lax.axis_index("y"), lax.axis_index("c")
    other_chips = [(1 - x, y), (x, 1 - y), (1 - x, 1 - y)]
    return x, y, c, other_chips


HBM_SPEC = pl.BlockSpec(memory_space=pltpu.HBM)


def _all_gather_hbm(shards, name):
    n = len(shards)

    def body(*refs):
        ins, outs = refs[:n], refs[n:2 * n]
        send_sems, recv_sems, local_sems = refs[2 * n:]
        x, y, c, chips = _place()
        me, sibling = (x, y, c), (x, y, 1 - c)

        def copy(b, k, block, to, src=None):
            slot = outs[b].at[4 * block[0] + 2 * block[1] + block[2]]
            return pltpu.make_async_remote_copy(
                src_ref=slot if src is None else src, dst_ref=slot,
                send_sem=send_sems.at[b, k], recv_sem=recv_sems.at[b, k], device_id=to, device_id_type=MESH)

        mine = [pltpu.make_async_copy(ins[b], outs[b].at[4 * x + 2 * y + c], local_sems.at[b]) for b in range(n)]
        for cp in mine:
            cp.start()
        first = []
        for b in range(n):
            first.append(copy(b, 0, me, sibling, src=ins[b]))
            first += [copy(b, 1 + j, me, (*chip, c), src=ins[b]) for j, chip in enumerate(chips)]
        for cp in first:
            cp.start()
        passed = []
        for b in range(n):
            for j, chip in enumerate(chips):
                copy(b, 1 + j, (*chip, c), me).wait_recv()
                fwd = copy(b, 4 + j, (*chip, c), sibling)
                fwd.start()
                passed.append(fwd)
        for b in range(n):
            copy(b, 0, sibling, me).wait_recv()
            for j, chip in enumerate(chips):
                copy(b, 4 + j, (*chip, 1 - c), me).wait_recv()
        for cp in first + passed:
            cp.wait_send()
        for cp in mine:
            cp.wait()

    return pl.pallas_call(
        body, name=name,
        in_specs=[HBM_SPEC] * n, out_specs=[HBM_SPEC] * n,
        out_shape=[jax.ShapeDtypeStruct((N_DEV,) + s.shape, s.dtype) for s in shards],
        scratch_shapes=[pltpu.SemaphoreType.DMA((n, 7)), pltpu.SemaphoreType.DMA((n, 7)), pltpu.SemaphoreType.DMA((n,))],
    )(*shards)


def _exchange_sibling(gs, name):
    n = len(gs)

    def body(*refs):
        ins, outs = refs[:n], refs[n:2 * n]
        send_sems, recv_sems = refs[2 * n:]
        x, y, c, _ = _place()
        copies = []
        for b in range(n):
            for p in range(4):
                copies.append(pltpu.make_async_remote_copy(
                    src_ref=ins[b].at[2 * p + (1 - c)], dst_ref=outs[b].at[p],
                    send_sem=send_sems.at[b, p], recv_sem=recv_sems.at[b, p],
                    device_id=(x, y, 1 - c), device_id_type=MESH))
        for cp in copies:
            cp.start()
        for cp in copies:
            cp.wait()

    return pl.pallas_call(
        body, name=name,
        in_specs=[HBM_SPEC] * n, out_specs=[HBM_SPEC] * n,
        out_shape=[jax.ShapeDtypeStruct((4,) + g.shape[1:], g.dtype) for g in gs],
        scratch_shapes=[pltpu.SemaphoreType.DMA((n, 4)), pltpu.SemaphoreType.DMA((n, 4))],
    )(*gs)


def _exchange_chips(hs, name):
    n, nl = len(hs), len(hs[0])

    def body(*refs):
        ins = [refs[b * nl:(b + 1) * nl] for b in range(n)]
        outs = refs[n * nl:n * nl + n]
        send_sems, recv_sems, local_sems = refs[n * nl + n:]
        x, y, c, chips = _place()
        my_chip = 2 * x + y
        local, sends, waits = [], [], []
        for b in range(n):
            for l in range(nl):
                s = b * nl + l
                local.append(pltpu.make_async_copy(ins[b][l].at[my_chip], outs[b].at[my_chip, l], local_sems.at[s]))
                for k, (px, py) in enumerate(chips):
                    peer = 2 * px + py
                    sends.append(pltpu.make_async_remote_copy(
                        src_ref=ins[b][l].at[peer], dst_ref=outs[b].at[my_chip, l],
                        send_sem=send_sems.at[s, k], recv_sem=recv_sems.at[s, k],
                        device_id=(px, py, c), device_id_type=MESH))
                    waits.append(pltpu.make_async_remote_copy(
                        src_ref=ins[b][l].at[peer], dst_ref=outs[b].at[peer, l],
                        send_sem=send_sems.at[s, k], recv_sem=recv_sems.at[s, k],
                        device_id=(px, py, c), device_id_type=MESH))
        for cp in local + sends:
            cp.start()
        for cp in waits:
            cp.wait_recv()
        for cp in sends:
            cp.wait_send()
        for cp in local:
            cp.wait()

    flat = [h for hb in hs for h in hb]
    return pl.pallas_call(
        body, name=name,
        in_specs=[HBM_SPEC] * (n * nl), out_specs=[HBM_SPEC] * n,
        out_shape=[jax.ShapeDtypeStruct((4, nl) + hb[0].shape[1:], hb[0].dtype) for hb in hs],
        scratch_shapes=[pltpu.SemaphoreType.DMA((n * nl, 3)), pltpu.SemaphoreType.DMA((n * nl, 3)),
                        pltpu.SemaphoreType.DMA((n * nl,))],
    )(*flat)


def _pair_add(g, r, core, name):
    _, nr, nc = g.shape
    tr = nr // 2

    def body(core_ref, g_ref, r_ref, o_ref):
        o_ref[...] = (g_ref[...] + r_ref[...]).astype(BF16)

    return pl.pallas_call(
        body, name=name,
        grid_spec=pltpu.PrefetchScalarGridSpec(
            num_scalar_prefetch=1, grid=(4, nr // tr),
            in_specs=[pl.BlockSpec((1, tr, nc), lambda p, i, core: (2 * p + core[0], i, 0)),
                      pl.BlockSpec((1, tr, nc), lambda p, i, core: (p, i, 0))],
            out_specs=pl.BlockSpec((1, tr, nc), lambda p, i, core: (p, i, 0))),
        out_shape=jax.ShapeDtypeStruct(r.shape, BF16),
        compiler_params=_cp("parallel", "parallel"),
    )(core, g, r)


def _all_gather_sum_small(v):
    rows = v.shape[0]

    def body(x_ref, sum_ref, out_ref, send_sems, recv_sems, local_sem):
        x, y, c, chips = _place()
        me, sibling = (x, y, c), (x, y, 1 - c)

        def block(px, py, pc):
            return out_ref.at[pl.ds((4 * px + 2 * py + pc) * rows, rows), :]

        def copy(k, blk, to, src=None):
            return pltpu.make_async_remote_copy(
                src_ref=block(*blk) if src is None else src, dst_ref=block(*blk),
                send_sem=send_sems.at[k], recv_sem=recv_sems.at[k], device_id=to, device_id_type=MESH)

        mine = pltpu.make_async_copy(x_ref, block(*me), local_sem)
        mine.start()
        first = [copy(0, me, sibling, src=x_ref)]
        first += [copy(1 + j, me, (*chip, c), src=x_ref) for j, chip in enumerate(chips)]
        for cp in first:
            cp.start()
        passed = [copy(4 + j, (*chip, c), sibling) for j, chip in enumerate(chips)]
        for j, chip in enumerate(chips):
            copy(1 + j, (*chip, c), me).wait_recv()
            passed[j].start()
        copy(0, sibling, me).wait_recv()
        for j, chip in enumerate(chips):
            copy(4 + j, (*chip, 1 - c), me).wait_recv()
        for cp in first + passed:
            cp.wait_send()
        mine.wait()
        total = out_ref[pl.ds(0, rows), :]
        for d in range(1, N_DEV):
            total = total + out_ref[pl.ds(d * rows, rows), :]
        sum_ref[...] = total

    vm = pl.BlockSpec(memory_space=pltpu.VMEM)
    return pl.pallas_call(
        body, name="small_all_reduce",
        in_specs=[vm], out_specs=[vm],
        out_shape=[jax.ShapeDtypeStruct((rows, 128), F32)],
        scratch_shapes=[pltpu.VMEM((N_DEV * rows, 128), F32), pltpu.SemaphoreType.DMA((7,)),
                        pltpu.SemaphoreType.DMA((7,)), pltpu.SemaphoreType.DMA],
    )(v)[0]


def _adamw(w, g, m, v):
    m = ADAM_B1 * m + (1.0 - ADAM_B1) * g
    v = ADAM_B2 * v + (1.0 - ADAM_B2) * (g * g)
    m_hat = m / (1.0 - ADAM_B1 ** ADAM_STEP)
    v_hat = v / (1.0 - ADAM_B2 ** ADAM_STEP)
    delta = -ADAM_LR * (m_hat / (jnp.sqrt(v_hat) + ADAM_EPS) + ADAM_WD * w)
    return delta, m, v


ADAM_ROWS = dict(w_in=256, w_out=128, ffn_w_in=256, ffn_w_out=176)


def _adamw_sharded(parts, w, m, v, tr, name):
    nl, nr, nc = w.shape

    def body(p_ref, w_ref, m_ref, v_ref, g_ref, d_ref, nm_ref, nv_ref):
        p = p_ref[...].astype(F32)
        g = (p[0] + p[1]) + (p[2] + p[3])
        delta, nm, nv = _adamw(w_ref[...], g, m_ref[...], v_ref[...])
        g_ref[...] = g
        d_ref[...] = delta
        nm_ref[...] = nm
        nv_ref[...] = nv

    blk = pl.BlockSpec((1, tr, nc), lambda l, i: (l, i, 0))
    return pl.pallas_call(
        body, name=name, grid=(nl, nr // tr),
        in_specs=[pl.BlockSpec((4, 1, tr, nc), lambda l, i: (0, l, i, 0)), blk, blk, blk],
        out_specs=[blk] * 4,
        out_shape=[jax.ShapeDtypeStruct(w.shape, F32)] * 4,
        compiler_params=_cp("parallel", "parallel"),
    )(parts, w, m, v)


def _adamw_small(g, w, m, v):
    def body(g_ref, w_ref, m_ref, v_ref, d_ref, nm_ref, nv_ref):
        delta, nm, nv = _adamw(w_ref[...], g_ref[...], m_ref[...], v_ref[...])
        d_ref[...] = delta
        nm_ref[...] = nm
        nv_ref[...] = nv

    return pl.pallas_call(
        body, name="adamw_small",
        out_shape=[jax.ShapeDtypeStruct(g.shape, F32)] * 3,
    )(g, w, m, v)


def _pack(arrays, rows):
    flat = jnp.concatenate([a.reshape(-1).astype(F32) for a in arrays])
    return jnp.pad(flat, (0, rows * 128 - flat.shape[0])).reshape(rows, 128)


def _unpack(packed, shapes):
    flat = packed.reshape(-1)
    out, off = [], 0
    for s in shapes:
        n = math.prod(s)
        out.append(flat[off:off + n].reshape(s))
        off += n
    return out


def _row(v, width=None):
    v = v.reshape(1, -1)
    return v if width is None else jnp.pad(v, ((0, 0), (0, width - v.shape[1])))


def _layer_fwd(x, wts, tables):
    h = _norm_fwd(x, wts["norm_pre_mix"], "norm_pre_mix")
    proj = _matmul(h, wts["w_in"], tm=512, tn=768, tk=1024, name="mm_proj")
    cat, lse = _attn_fwd(proj, *tables)
    c_qkv = _dnconv_fwd(proj, wts["dn_conv_w"])
    cat, states = _delta_fwd(c_qkv, proj, wts["dn_a_log"], wts["dn_dt_bias"], wts["dn_norm_w"], cat)
    mix = _matmul(cat, wts["w_out"], tm=512, tn=1024, tk=1024, name="mm_mix")
    x1 = _resnorm_fwd(x, mix, wts["norm_post_mix"], "norm_post_mix")
    h2 = _norm_fwd(x1, wts["norm_pre_ffn"], "norm_pre_ffn")
    pre = _matmul(h2, wts["ffn_w_in"], tm=512, tn=512, tk=1024, name="mm_ffn_in")
    act = _ffact_fwd(pre, wts["ffn_conv_w"], wts["ffn_conv_b"])
    f = _matmul(act, wts["ffn_w_out"], tm=512, tn=1024, tk=1408, name="mm_ffn_out")
    x2 = _resnorm_fwd(x1, f, wts["norm_post_ffn"], "norm_post_ffn")
    saved = dict(x=x, h=h, proj=proj, lse=lse, c_qkv=c_qkv, states=states, cat=cat, mix=mix, x1=x1, h2=h2, pre=pre,
                 act=act, f=f)
    return x2, saved


def _layer_bwd(dx2, wts, s, tables):
    g = {}
    df, g["norm_post_ffn"] = _norm_bwd(s["f"], wts["norm_post_ffn"], dx2, None, "norm_post_ffn_bwd")
    dact = _matmul(df, wts["ffn_w_out"], tb=True, tm=512, tn=1408, tk=1024, name="mm_dact", out_dtype=BF16)
    g["ffn_w_out"] = _matmul(s["act"], df, ta=True, tm=1408, tn=1024, tk=512, name="mm_dw_ffn_out")
    dpre, g["ffn_conv_w"], g["ffn_conv_b"] = _ffact_bwd(s["pre"], wts["ffn_conv_w"], wts["ffn_conv_b"], dact)
    dh2 = _matmul(dpre, wts["ffn_w_in"], tb=True, tm=512, tn=1024, tk=512, name="mm_dh2")
    g["ffn_w_in"] = _matmul(s["h2"], dpre, ta=True, tm=1024, tn=512, tk=512, name="mm_dw_ffn_in")
    dx1, g["norm_pre_ffn"] = _norm_bwd(s["x1"], wts["norm_pre_ffn"], dh2, dx2, "norm_pre_ffn_bwd")
    dmix, g["norm_post_mix"] = _norm_bwd(s["mix"], wts["norm_post_mix"], dx1, None, "norm_post_mix_bwd")
    dcat = _matmul(dmix, wts["w_out"], tb=True, tm=512, tn=1024, tk=1024, name="mm_dcat")
    g["w_out"] = _matmul(s["cat"], dmix, ta=True, tm=1024, tn=1024, tk=512, name="mm_dw_out")
    dproj = _attn_bwd(s["proj"], *tables, s["cat"], s["lse"], dcat)
    dproj, dc, g["dn_a_log"], g["dn_dt_bias"], g["dn_norm_w"] = _delta_bwd(
        s["c_qkv"], s["proj"], wts["dn_a_log"], wts["dn_dt_bias"], wts["dn_norm_w"], s["states"], dcat, dproj)
    dproj, g["dn_conv_w"] = _dnconv_bwd(s["proj"], wts["dn_conv_w"], dc, dproj)
    dh = _matmul(dproj, wts["w_in"], tb=True, tm=512, tn=1024, tk=768, name="mm_dh")
    g["w_in"] = _matmul(s["h"], dproj, ta=True, tm=1024, tn=768, tk=512, name="mm_dw_in")
    dx, g["norm_pre_mix"] = _norm_bwd(s["x"], wts["norm_pre_mix"], dh, dx1, "norm_pre_mix_bwd")
    return dx, g


BIG = ("w_in", "w_out", "ffn_w_in", "ffn_w_out")
SMALL_SHARDED = ("dn_conv_w", "ffn_conv_w")
REPLICATED = ("dn_a_log", "dn_dt_bias", "dn_norm_w", "ffn_conv_b", "norm_pre_mix", "norm_post_mix", "norm_pre_ffn",
              "norm_post_ffn")
WEIGHTS = ("w_in", "dn_conv_w", "dn_a_log", "dn_dt_bias", "dn_norm_w", "w_out", "ffn_w_in", "ffn_conv_w", "ffn_conv_b",
           "ffn_w_out", "norm_pre_mix", "norm_post_mix", "norm_pre_ffn", "norm_post_ffn")
FULL_SHAPE = dict(dn_conv_w=(DEPTH, 4, 1536), ffn_conv_w=(DEPTH, 3, 2 * D_FF), dn_a_log=(DEPTH, NDH),
                  dn_dt_bias=(DEPTH, NDH), dn_norm_w=(DEPTH, 128), ffn_conv_b=(DEPTH, 2 * D_FF),
                  norm_pre_mix=(DEPTH, D_MODEL), norm_post_mix=(DEPTH, D_MODEL), norm_pre_ffn=(DEPTH, D_MODEL),
                  norm_post_ffn=(DEPTH, D_MODEL))
SMALL_GRAD_ORDER = REPLICATED + SMALL_SHARDED
SMALL_GRAD_ROWS = 520
SMALL_W_ROWS = 48
SMALL_ADAM_ROWS = 200


def _w_in_to_kernel_order(t):
    t = jnp.concatenate([_pairs_from_qkv(t[..., :QKV_W]), t[..., QKV_W:]], axis=-1)
    return jnp.pad(t, [(0, 0)] * (t.ndim - 1) + [(0, IN_PAD - IN_COLS)])


def _w_in_from_kernel_order(t):
    return jnp.concatenate([_qkv_from_pairs(t[..., :QKV_W]), t[..., QKV_W:IN_COLS]], axis=-1)


def _cols_to_devices(t):
    nr, nc = t.shape
    return t.reshape(nr, N_DEV, nc // N_DEV).transpose(1, 0, 2)


def _cols_from_devices(t):
    return t.transpose(1, 0, 2).reshape(t.shape[1], -1)


def kernel(x, w_in, dn_conv_w, dn_a_log, dn_dt_bias, dn_norm_w, w_out, ffn_w_in, ffn_conv_w, ffn_conv_b, ffn_w_out, norm_pre_mix, norm_post_mix, norm_pre_ffn, norm_post_ffn, loss_target, m_w_in, m_dn_conv_w, m_dn_a_log, m_dn_dt_bias, m_dn_norm_w, m_w_out, m_ffn_w_in, m_ffn_conv_w, m_ffn_conv_b, m_ffn_w_out, m_norm_pre_mix, m_norm_post_mix, m_norm_pre_ffn, m_norm_post_ffn, v_w_in, v_dn_conv_w, v_dn_a_log, v_dn_dt_bias, v_dn_norm_w, v_w_out, v_ffn_w_in, v_ffn_conv_w, v_ffn_conv_b, v_ffn_w_out, v_norm_pre_mix, v_norm_post_mix, v_norm_pre_ffn, v_norm_post_ffn):
    local = dict(w_in=w_in, dn_conv_w=dn_conv_w, dn_a_log=dn_a_log, dn_dt_bias=dn_dt_bias, dn_norm_w=dn_norm_w,
                 w_out=w_out, ffn_w_in=ffn_w_in, ffn_conv_w=ffn_conv_w, ffn_conv_b=ffn_conv_b, ffn_w_out=ffn_w_out,
                 norm_pre_mix=norm_pre_mix, norm_post_mix=norm_post_mix, norm_pre_ffn=norm_pre_ffn,
                 norm_post_ffn=norm_post_ffn)
    mom_m = dict(w_in=m_w_in, dn_conv_w=m_dn_conv_w, dn_a_log=m_dn_a_log, dn_dt_bias=m_dn_dt_bias,
                 dn_norm_w=m_dn_norm_w, w_out=m_w_out, ffn_w_in=m_ffn_w_in, ffn_conv_w=m_ffn_conv_w,
                 ffn_conv_b=m_ffn_conv_b, ffn_w_out=m_ffn_w_out, norm_pre_mix=m_norm_pre_mix,
                 norm_post_mix=m_norm_post_mix, norm_pre_ffn=m_norm_pre_ffn, norm_post_ffn=m_norm_post_ffn)
    mom_v = dict(w_in=v_w_in, dn_conv_w=v_dn_conv_w, dn_a_log=v_dn_a_log, dn_dt_bias=v_dn_dt_bias,
                 dn_norm_w=v_dn_norm_w, w_out=v_w_out, ffn_w_in=v_ffn_w_in, ffn_conv_w=v_ffn_conv_w,
                 ffn_conv_b=v_ffn_conv_b, ffn_w_out=v_ffn_w_out, norm_pre_mix=v_norm_pre_mix,
                 norm_post_mix=v_norm_post_mix, norm_pre_ffn=v_norm_pre_ffn, norm_post_ffn=v_norm_post_ffn)
    dev = 4 * lax.axis_index("x") + 2 * lax.axis_index("y") + lax.axis_index("c")
    core = lax.axis_index("c").astype(jnp.int32).reshape(1)

    small_w = _pack([dn_conv_w, ffn_conv_w], SMALL_W_ROWS)
    shards = [local[n][l].astype(BF16) for n in BIG for l in range(DEPTH)]
    gathered = _all_gather_hbm(shards + [small_w], "weights_all_gather")
    g_small = gathered[-1]
    gathered = {n: gathered[i * DEPTH:(i + 1) * DEPTH] for i, n in enumerate(BIG)}
    n_dn, n_ff = DEPTH * 4 * 192, DEPTH * 3 * 704
    sm = g_small.reshape(N_DEV, -1)
    full_dn_conv = sm[:, :n_dn].reshape(N_DEV, DEPTH, 4, 192).transpose(1, 2, 0, 3).reshape(DEPTH, 4, 1536)
    full_ff_conv = _interleave_ff(
        sm[:, n_dn:n_dn + n_ff].reshape(N_DEV, DEPTH, 3, 704).transpose(1, 2, 0, 3).reshape(DEPTH, 3, 2 * D_FF))

    def layer_weights(l):
        wts = dict(
            w_in=_w_in_to_kernel_order(_cols_from_devices(gathered["w_in"][l])),
            w_out=gathered["w_out"][l].reshape(D_MODEL, D_MODEL),
            ffn_w_in=_interleave_ff(_cols_from_devices(gathered["ffn_w_in"][l])),
            ffn_w_out=gathered["ffn_w_out"][l].reshape(D_FF, D_MODEL),
            dn_conv_w=full_dn_conv[l], ffn_conv_w=full_ff_conv[l],
            ffn_conv_b=_interleave_ff(_row(ffn_conv_b[l])),
            dn_a_log=_row(dn_a_log[l], 128), dn_dt_bias=_row(dn_dt_bias[l], 128))
        for n in ("dn_norm_w", "norm_pre_mix", "norm_post_mix", "norm_pre_ffn", "norm_post_ffn"):
            wts[n] = _row(local[n][l])
        return wts

    tables = _rope_tables()
    weights = [layer_weights(l) for l in range(DEPTH)]
    act, saved = x[0], []
    for l in range(DEPTH):
        act, s = _layer_fwd(act, weights[l], tables)
        saved.append(s)
    loss_part, dact = _loss_fwd_bwd(act, loss_target[0])
    grads = [None] * DEPTH
    for l in reversed(range(DEPTH)):
        dact, grads[l] = _layer_bwd(dact, weights[l], saved[l], tables)
    grad_x = dact[None]

    def to_devices(name, l):
        t = grads[l][name]
        if name == "w_in":
            return _cols_to_devices(_w_in_from_kernel_order(t))
        if name == "ffn_w_in":
            return _cols_to_devices(_deinterleave_ff(t))
        return t.reshape(N_DEV, t.shape[0] // N_DEV, t.shape[1])

    to_dev = [to_devices(n, l) for n in BIG for l in range(DEPTH)]
    from_sibling = _exchange_sibling(to_dev, "grads_to_sibling")
    chip_sums = [_pair_add(gd, r, core, "grads_pair_add_%d" % i) for i, (gd, r) in enumerate(zip(to_dev, from_sibling))]
    parts = _exchange_chips([chip_sums[i * DEPTH:(i + 1) * DEPTH] for i in range(len(BIG))], "grads_to_chips")

    def small_grad(name):
        t = jnp.stack([grads[l][name] for l in range(DEPTH)])
        if name in ("dn_a_log", "dn_dt_bias"):
            t = t[:, 0, :NDH]
        if name in ("ffn_conv_w", "ffn_conv_b"):
            t = _deinterleave_ff(t)
        return t.reshape(FULL_SHAPE[name])

    small_part = _pack([small_grad(n) for n in SMALL_GRAD_ORDER] + [loss_part[0, :1]], SMALL_GRAD_ROWS)
    small_sum = _all_gather_sum_small(small_part)
    small_g = dict(zip(SMALL_GRAD_ORDER + ("loss",), _unpack(small_sum, [FULL_SHAPE[n] for n in SMALL_GRAD_ORDER] + [(1,)])))
    loss = small_g["loss"][0]
    small_g["dn_conv_w"] = lax.dynamic_slice_in_dim(small_g["dn_conv_w"], dev * 192, 192, axis=2)
    small_g["ffn_conv_w"] = lax.dynamic_slice_in_dim(small_g["ffn_conv_w"], dev * 704, 704, axis=2)

    out_g, out_d, out_m, out_v = {}, {}, {}, {}
    for n, p in zip(BIG, parts):
        out_g[n], out_d[n], out_m[n], out_v[n] = _adamw_sharded(p, local[n], mom_m[n], mom_v[n], ADAM_ROWS[n], "adamw_" + n)
    shapes = [small_g[n].shape for n in SMALL_GRAD_ORDER]
    d_s, m_s, v_s = _adamw_small(_pack([small_g[n] for n in SMALL_GRAD_ORDER], SMALL_ADAM_ROWS),
                                 _pack([local[n] for n in SMALL_GRAD_ORDER], SMALL_ADAM_ROWS),
                                 _pack([mom_m[n] for n in SMALL_GRAD_ORDER], SMALL_ADAM_ROWS),
                                 _pack([mom_v[n] for n in SMALL_GRAD_ORDER], SMALL_ADAM_ROWS))
    for n, d, m, v in zip(SMALL_GRAD_ORDER, _unpack(d_s, shapes), _unpack(m_s, shapes), _unpack(v_s, shapes)):
        out_g[n], out_d[n], out_m[n], out_v[n] = small_g[n], d, m, v
    return (loss, grad_x, *[out_g[n] for n in WEIGHTS], *[out_d[n] for n in WEIGHTS],
            *[out_m[n] for n in WEIGHTS], *[out_v[n] for n in WEIGHTS])
```

```python
import functools
import math

import jax
import jax.numpy as jnp
from jax import lax
from jax.experimental import pallas as pl
from jax.experimental.pallas import tpu as pltpu

F32 = jnp.float32
BF16 = jnp.bfloat16
HI = lax.Precision.HIGHEST
MESH = pl.DeviceIdType.MESH

N_DEV = 8
SEQ = 2048
D_MODEL = 1024
DEPTH = 2
N_PAIR = 4
HEAD_DIM = 64
ATTN_W = 512
ATTN_BLK = 128
DILATIONS = (1, 4, 16)
SEGMENT_BLOCKS = (16, 4, 1)
N_BLK = SEQ // ATTN_BLK
NDH = 4
CH = 64
NCH = SEQ // CH
IN_COLS = 3592
IN_PAD = 3840
QKV_W = 3 * ATTN_W
DN_QKV_BLK0 = QKV_W // 128
DN_QKV_BLKS = 1536 // 128
DN_Z_COL = 3072
DN_TAIL_BLK = 3584 // 128
D_FF = 2816
FF_BLKS = D_FF // 128
EPS = 1e-6
NEG = -1e30
ROPE_THETA = 10000.0

ADAM_LR, ADAM_B1, ADAM_B2, ADAM_EPS, ADAM_WD, ADAM_STEP = 0.001, 0.9, 0.999, 1e-08, 0.01, 10

VMEM_LIMIT = 56 * 1024 * 1024


def _cp(*sem):
    return pltpu.CompilerParams(dimension_semantics=sem, vmem_limit_bytes=VMEM_LIMIT)


def _dot(a, b, dims, precision=None):
    if precision is None:
        a = a.astype(BF16)
        b = b.astype(BF16)
    return lax.dot_general(a, b, (dims, ((), ())), preferred_element_type=F32, precision=precision)


def _make_mm(precision):
    @jax.custom_vjp
    def nn(a, b):
        return _dot(a, b, ((1,), (0,)), precision)

    @jax.custom_vjp
    def nt(a, b):
        return _dot(a, b, ((1,), (1,)), precision)

    @jax.custom_vjp
    def tn(a, b):
        return _dot(a, b, ((0,), (0,)), precision)

    nn.defvjp(lambda a, b: (nn(a, b), (a, b)), lambda r, g: (nt(g, r[1]), tn(r[0], g)))
    nt.defvjp(lambda a, b: (nt(a, b), (a, b)), lambda r, g: (nn(g, r[1]), tn(g, r[0])))
    tn.defvjp(lambda a, b: (tn(a, b), (a, b)), lambda r, g: (nt(r[1], g), nn(r[0], g)))
    return nn, nt, tn


MM, MM_NT, MM_TN = _make_mm(None)
MMH, _, _ = _make_mm(HI)


def _matmul(a, b, *, ta=False, tb=False, tm, tn, tk, name, out_dtype=F32):
    (k_dim, m_dim) = a.shape if ta else a.shape[::-1]
    (n_dim, k2) = b.shape if tb else b.shape[::-1]
    assert k_dim == k2 and m_dim % tm == 0 and n_dim % tn == 0 and k_dim % tk == 0, (a.shape, b.shape, tm, tn, tk)
    nk = k_dim // tk
    dims = ((0 if ta else 1,), (1 if tb else 0,))

    def body(a_ref, b_ref, o_ref, *acc):
        p = _dot(a_ref[...], b_ref[...], dims)
        if nk == 1:
            o_ref[...] = p.astype(out_dtype)
            return
        acc_ref, k = acc[0], pl.program_id(2)

        @pl.when(k == 0)
        def _():
            acc_ref[...] = p

        @pl.when(k > 0)
        def _():
            acc_ref[...] += p

        @pl.when(k == nk - 1)
        def _():
            o_ref[...] = acc_ref[...].astype(out_dtype)

    a_spec = pl.BlockSpec((tk, tm), lambda i, j, k: (k, i)) if ta else pl.BlockSpec((tm, tk), lambda i, j, k: (i, k))
    b_spec = pl.BlockSpec((tn, tk), lambda i, j, k: (j, k)) if tb else pl.BlockSpec((tk, tn), lambda i, j, k: (k, j))
    return pl.pallas_call(
        body, name=name,
        grid=(m_dim // tm, n_dim // tn, nk),
        in_specs=[a_spec, b_spec],
        out_specs=pl.BlockSpec((tm, tn), lambda i, j, k: (i, j)),
        out_shape=jax.ShapeDtypeStruct((m_dim, n_dim), out_dtype),
        scratch_shapes=[pltpu.VMEM((tm, tn), F32)] if nk > 1 else [],
        compiler_params=_cp("parallel", "parallel", "arbitrary"),
    )(a, b)


NORM_ROWS = 256


def _rms(x, w):
    return x * lax.rsqrt(jnp.mean(x * x, axis=1, keepdims=True) + EPS) * w


def _norm_fwd(x, w_row, name, out_dtype=BF16):
    def body(x_ref, w_ref, o_ref):
        o_ref[...] = _rms(x_ref[...], w_ref[...]).astype(out_dtype)

    return pl.pallas_call(
        body, name=name, grid=(SEQ // NORM_ROWS,),
        in_specs=[pl.BlockSpec((NORM_ROWS, D_MODEL), lambda i: (i, 0)), pl.BlockSpec((1, D_MODEL), lambda i: (0, 0))],
        out_specs=pl.BlockSpec((NORM_ROWS, D_MODEL), lambda i: (i, 0)),
        out_shape=jax.ShapeDtypeStruct((SEQ, D_MODEL), out_dtype),
        compiler_params=_cp("parallel"),
    )(x, w_row)


def _resnorm_fwd(x, f, w_row, name):
    def body(x_ref, f_ref, w_ref, o_ref):
        o_ref[...] = x_ref[...] + _rms(f_ref[...], w_ref[...])

    blk = pl.BlockSpec((NORM_ROWS, D_MODEL), lambda i: (i, 0))
    return pl.pallas_call(
        body, name=name, grid=(SEQ // NORM_ROWS,),
        in_specs=[blk, blk, pl.BlockSpec((1, D_MODEL), lambda i: (0, 0))],
        out_specs=blk, out_shape=jax.ShapeDtypeStruct((SEQ, D_MODEL), F32),
        compiler_params=_cp("parallel"),
    )(x, f, w_row)


def _norm_bwd(x, w_row, dy, add, name):
    has_add = add is not None

    def body(*refs):
        if has_add:
            x_ref, w_ref, dy_ref, add_ref, dx_ref, dw_ref = refs
        else:
            x_ref, w_ref, dy_ref, dx_ref, dw_ref = refs
        _, vjp = jax.vjp(_rms, x_ref[...], w_ref[...])
        dx, dw = vjp(dy_ref[...])
        dx_ref[...] = dx + add_ref[...] if has_add else dx

        @pl.when(pl.program_id(0) == 0)
        def _():
            dw_ref[...] = jnp.zeros_like(dw_ref)

        dw_ref[...] += dw

    blk = pl.BlockSpec((NORM_ROWS, D_MODEL), lambda i: (i, 0))
    row = pl.BlockSpec((1, D_MODEL), lambda i: (0, 0))
    ins = [x, w_row, dy] + ([add] if has_add else [])
    return pl.pallas_call(
        body, name=name, grid=(SEQ // NORM_ROWS,),
        in_specs=[blk, row, blk] + ([blk] if has_add else []),
        out_specs=[blk, row],
        out_shape=[jax.ShapeDtypeStruct((SEQ, D_MODEL), F32), jax.ShapeDtypeStruct((1, D_MODEL), F32)],
        compiler_params=_cp("arbitrary"),
    )(*ins)


def _loss_fwd_bwd(y, target):
    def body(y_ref, t_ref, loss_ref, dy_ref):
        err = y_ref[...] - t_ref[...]
        dy_ref[...] = err * (1.0 / D_MODEL)

        @pl.when(pl.program_id(0) == 0)
        def _():
            loss_ref[...] = jnp.zeros_like(loss_ref)

        part = jnp.sum(jnp.sum(err * err, axis=1, keepdims=True) * (1.0 / D_MODEL), axis=0, keepdims=True)
        loss_ref[...] += 0.5 * jnp.broadcast_to(part, loss_ref.shape)

    blk = pl.BlockSpec((NORM_ROWS, D_MODEL), lambda i: (i, 0))
    return pl.pallas_call(
        body, name="loss", grid=(SEQ // NORM_ROWS,),
        in_specs=[blk, blk],
        out_specs=[pl.BlockSpec((1, 128), lambda i: (0, 0)), blk],
        out_shape=[jax.ShapeDtypeStruct((1, 128), F32), jax.ShapeDtypeStruct((SEQ, D_MODEL), F32)],
        compiler_params=_cp("arbitrary"),
    )(y, target)


def _make_shift(j):
    def down(x):
        row = lax.broadcasted_iota(jnp.int32, x.shape, 0)
        return jnp.where(row >= j, pltpu.roll(x, j, 0), 0.0)

    def up(x):
        n = x.shape[0]
        row = lax.broadcasted_iota(jnp.int32, x.shape, 0)
        return jnp.where(row < n - j, pltpu.roll(x, n - j, 0), 0.0)

    f = jax.custom_vjp(down)
    f.defvjp(lambda x: (down(x), None), lambda _, g: (up(g),))
    return f


_SHIFT = {j: _make_shift(j) for j in (1, 2, 3)}


def _causal_conv(x, taps):
    n = len(taps)
    acc = x * taps[n - 1]
    for k in range(n - 1):
        acc = acc + _SHIFT[n - 1 - k](x) * taps[k]
    return acc


def _tap_rows(w_ref, lanes=slice(None)):
    return tuple(w_ref[k:k + 1, lanes] for k in range(w_ref.shape[0]))


def _sigmoid(x):
    return 1.0 / (1.0 + jnp.exp(-x))


def _silu(x):
    return x * _sigmoid(x)


def _softplus(x):
    return jnp.maximum(x, 0.0) + jnp.log(1.0 + jnp.exp(-jnp.abs(x)))


def _gelu_tanh(x):
    return 0.5 * x * (1.0 + jnp.tanh(math.sqrt(2.0 / math.pi) * (x + 0.044715 * (x * x * x))))


def _dnconv_fn(x, taps):
    return _silu(_causal_conv(x, taps))


def _dnconv_fwd(proj, conv_w):
    def body(x_ref, w_ref, o_ref):
        o_ref[...] = _dnconv_fn(x_ref[...], _tap_rows(w_ref))

    return pl.pallas_call(
        body, name="dnconv_fwd", grid=(DN_QKV_BLKS,),
        in_specs=[pl.BlockSpec((SEQ, 128), lambda j: (0, DN_QKV_BLK0 + j)), pl.BlockSpec((4, 128), lambda j: (0, j))],
        out_specs=pl.BlockSpec((SEQ, 128), lambda j: (0, j)),
        out_shape=jax.ShapeDtypeStruct((SEQ, 1536), F32),
        compiler_params=_cp("parallel"),
    )(proj, conv_w)


def _dnconv_bwd(proj, conv_w, dc, dproj):
    def body(x_ref, w_ref, dc_ref, _, dx_ref, dw_ref):
        _, vjp = jax.vjp(_dnconv_fn, x_ref[...], _tap_rows(w_ref))
        dx, dw = vjp(dc_ref[...])
        dx_ref[...] = dx
        for k, row in enumerate(dw):
            dw_ref[k:k + 1, :] = row

    return pl.pallas_call(
        body, name="dnconv_bwd", grid=(DN_QKV_BLKS,),
        in_specs=[pl.BlockSpec((SEQ, 128), lambda j: (0, DN_QKV_BLK0 + j)), pl.BlockSpec((4, 128), lambda j: (0, j)),
                  pl.BlockSpec((SEQ, 128), lambda j: (0, j)), pl.BlockSpec(memory_space=pl.ANY)],
        out_specs=[pl.BlockSpec((SEQ, 128), lambda j: (0, DN_QKV_BLK0 + j)), pl.BlockSpec((4, 128), lambda j: (0, j))],
        out_shape=[jax.ShapeDtypeStruct((SEQ, IN_PAD), F32), jax.ShapeDtypeStruct((4, 1536), F32)],
        input_output_aliases={3: 0},
        compiler_params=_cp("parallel"),
    )(proj, conv_w, dc, dproj)


def _ffact_fn(pg, pu, wg, wu, bg, bu):
    return _gelu_tanh(_causal_conv(pg, wg) + bg) * (_causal_conv(pu, wu) + bu)


def _ffact_args(p_ref, w_ref, b_ref):
    g, u = slice(0, 128), slice(128, 256)
    return (p_ref[:, g], p_ref[:, u], _tap_rows(w_ref, g), _tap_rows(w_ref, u), b_ref[:, g], b_ref[:, u])


def _ffact_fwd(pre, conv_w, conv_b):
    def body(p_ref, w_ref, b_ref, o_ref):
        o_ref[...] = _ffact_fn(*_ffact_args(p_ref, w_ref, b_ref)).astype(BF16)

    return pl.pallas_call(
        body, name="ffact_fwd", grid=(FF_BLKS,),
        in_specs=[pl.BlockSpec((SEQ, 256), lambda j: (0, j)), pl.BlockSpec((3, 256), lambda j: (0, j)),
                  pl.BlockSpec((1, 256), lambda j: (0, j))],
        out_specs=pl.BlockSpec((SEQ, 128), lambda j: (0, j)),
        out_shape=jax.ShapeDtypeStruct((SEQ, D_FF), BF16),
        compiler_params=_cp("parallel"),
    )(pre, conv_w, conv_b)


def _ffact_bwd(pre, conv_w, conv_b, dact):
    def body(p_ref, w_ref, b_ref, da_ref, dp_ref, dw_ref, db_ref):
        _, vjp = jax.vjp(_ffact_fn, *_ffact_args(p_ref, w_ref, b_ref))
        dpg, dpu, dwg, dwu, dbg, dbu = vjp(da_ref[...].astype(F32))
        dp_ref[:, 0:128] = dpg
        dp_ref[:, 128:256] = dpu
        for k in range(3):
            dw_ref[k:k + 1, 0:128] = dwg[k]
            dw_ref[k:k + 1, 128:256] = dwu[k]
        db_ref[:, 0:128] = dbg
        db_ref[:, 128:256] = dbu

    return pl.pallas_call(
        body, name="ffact_bwd", grid=(FF_BLKS,),
        in_specs=[pl.BlockSpec((SEQ, 256), lambda j: (0, j)), pl.BlockSpec((3, 256), lambda j: (0, j)),
                  pl.BlockSpec((1, 256), lambda j: (0, j)), pl.BlockSpec((SEQ, 128), lambda j: (0, j))],
        out_specs=[pl.BlockSpec((SEQ, 256), lambda j: (0, j)), pl.BlockSpec((3, 256), lambda j: (0, j)),
                   pl.BlockSpec((1, 256), lambda j: (0, j))],
        out_shape=[jax.ShapeDtypeStruct((SEQ, 2 * D_FF), F32), jax.ShapeDtypeStruct((3, 2 * D_FF), F32),
                   jax.ShapeDtypeStruct((1, 2 * D_FF), F32)],
        compiler_params=_cp("parallel"),
    )(pre, conv_w, conv_b, dact)


def _interleave_ff(t):
    lead = t.shape[:-1]
    return t.reshape(lead + (2, FF_BLKS, 128)).swapaxes(-3, -2).reshape(lead + (2 * D_FF,))


def _deinterleave_ff(t):
    lead = t.shape[:-1]
    return t.reshape(lead + (FF_BLKS, 2, 128)).swapaxes(-3, -2).reshape(lead + (2 * D_FF,))


def _rope_tables():
    inv = 1.0 / (ROPE_THETA ** (jnp.arange(0, HEAD_DIM, 2, dtype=F32) / HEAD_DIM))
    ang = jnp.arange(SEQ, dtype=F32)[:, None] * inv[None, :]
    cos = jnp.tile(jnp.cos(ang), (1, 4))
    sin = jnp.tile(jnp.sin(ang), (1, 4))
    sign = jnp.where((jnp.arange(128) % HEAD_DIM) < HEAD_DIM // 2, -1.0, 1.0).astype(F32)
    return cos, sin * sign[None, :]


def _rope(x, cos, sin_signed):
    lane = lax.broadcasted_iota(jnp.int32, x.shape, 1)
    partner = jnp.where((lane % HEAD_DIM) < HEAD_DIM // 2, pltpu.roll(x, 128 - HEAD_DIM // 2, 1),
                        pltpu.roll(x, HEAD_DIM // 2, 1))
    return x * cos + partner * sin_signed


def _pairs_from_qkv(t):
    lead = t.shape[:-1]
    return t.reshape(lead + (3, N_PAIR, 128)).swapaxes(-3, -2).reshape(lead + (QKV_W,))


def _qkv_from_pairs(t):
    lead = t.shape[:-1]
    return t.reshape(lead + (N_PAIR, 3, 128)).swapaxes(-3, -2).reshape(lead + (QKV_W,))


def _band_masks():
    a = lax.broadcasted_iota(jnp.int32, (ATTN_BLK, ATTN_BLK), 0)
    c = lax.broadcasted_iota(jnp.int32, (ATTN_BLK, ATTN_BLK), 1)
    return c >= a, c <= a


def _head_masks():
    lane = lax.broadcasted_iota(jnp.int32, (1, 128), 1)
    return [(lane // HEAD_DIM) == h for h in range(2)]


def _block_rows(branch, t):
    d, per_seg = DILATIONS[branch], SEGMENT_BLOCKS[branch]
    if d == 1:
        start = pl.multiple_of(t * ATTN_BLK, ATTN_BLK)
        prev = pl.multiple_of(jnp.maximum(t - 1, 0) * ATTN_BLK, ATTN_BLK)
        return pl.ds(start, ATTN_BLK), pl.ds(prev, ATTN_BLK), t > 0
    r, n = t // per_seg, t % per_seg
    start = n * (ATTN_BLK * d) + r
    prev = jnp.maximum(n - 1, 0) * (ATTN_BLK * d) + r
    return pl.ds(start, ATTN_BLK, stride=d), pl.ds(prev, ATTN_BLK, stride=d), n > 0


def _attn_fwd(proj, cos, sin_signed):
    scale = HEAD_DIM ** -0.5

    def body(qkv_ref, cos_ref, sin_ref, out_ref, lse_ref, q_s, k_s, v_s, *branch_s):
        o_s, l_s = branch_s[:3], branch_s[3:]
        q_s[...] = _rope(qkv_ref[:, 0:128], cos_ref[...], sin_ref[...])
        k_s[...] = _rope(qkv_ref[:, 128:256], cos_ref[...], sin_ref[...])
        v_s[...] = qkv_ref[:, 256:384]
        m_prev0, m_cur = _band_masks()
        heads = _head_masks()
        for branch in range(3):
            def block(t, carry, branch=branch):
                rows, prows, has_prev = _block_rows(branch, t)
                with_prev = SEGMENT_BLOCKS[branch] > 1
                m_prev = m_prev0 & has_prev
                q, kc, vc = q_s[rows, :], k_s[rows, :], v_s[rows, :]
                if with_prev:
                    kp, vp = k_s[prows, :], v_s[prows, :]
                outs, lses = [], []
                for hm in heads:
                    qh = jnp.where(hm, q, 0.0)
                    sc = jnp.where(m_cur, MM_NT(qh, kc) * scale, NEG)
                    m = jnp.max(sc, axis=1, keepdims=True)
                    if with_prev:
                        sp = jnp.where(m_prev, MM_NT(qh, kp) * scale, NEG)
                        m = jnp.maximum(jnp.max(sp, axis=1, keepdims=True), m)
                    ec = jnp.exp(sc - m)
                    l = jnp.sum(ec, axis=1, keepdims=True)
                    acc = MM(ec, vc)
                    if with_prev:
                        ep = jnp.exp(sp - m)
                        l = l + jnp.sum(ep, axis=1, keepdims=True)
                        acc = acc + MM(ep, vp)
                    outs.append(acc / l)
                    lses.append(m + jnp.log(l))
                o_s[branch][rows, :] = jnp.where(heads[0], outs[0], outs[1])
                l_s[branch][rows, :] = jnp.where(heads[0], lses[0], lses[1])
                return carry

            lax.fori_loop(0, N_BLK, block, 0)
        l0, l1, l2 = l_s[0][...], l_s[1][...], l_s[2][...]
        m = jnp.maximum(jnp.maximum(l0, l1), l2)
        w0, w1, w2 = jnp.exp(l0 - m), jnp.exp(l1 - m), jnp.exp(l2 - m)
        den = w0 + w1 + w2
        out_ref[...] = (w0 * o_s[0][...] + w1 * o_s[1][...] + w2 * o_s[2][...]) / den
        lse_ref[...] = m + jnp.log(den)

    tab = pl.BlockSpec((SEQ, 128), lambda j: (0, 0))
    col = pl.BlockSpec((SEQ, 128), lambda j: (0, j))
    return pl.pallas_call(
        body, name="attn_fwd", grid=(N_PAIR,),
        in_specs=[pl.BlockSpec((SEQ, 384), lambda j: (0, j)), tab, tab],
        out_specs=[col, col],
        out_shape=[jax.ShapeDtypeStruct((SEQ, 2 * ATTN_W), F32), jax.ShapeDtypeStruct((SEQ, ATTN_W), F32)],
        scratch_shapes=[pltpu.VMEM((SEQ, 128), F32)] * 9,
        compiler_params=_cp("parallel"),
    )(proj, cos, sin_signed)


def _attn_bwd(proj, cos, sin_signed, cat, lse, dcat):
    scale = HEAD_DIM ** -0.5

    def body(qkv_ref, cos_ref, sin_ref, out_ref, lse_ref, do_ref, dqkv_ref, q_s, k_s, v_s, dq_s, dk_s, dv_s, dod_s):
        q_s[...] = _rope(qkv_ref[:, 0:128], cos_ref[...], sin_ref[...])
        k_s[...] = _rope(qkv_ref[:, 128:256], cos_ref[...], sin_ref[...])
        v_s[...] = qkv_ref[:, 256:384]
        dq_s[...] = jnp.zeros_like(dq_s)
        dk_s[...] = jnp.zeros_like(dk_s)
        dv_s[...] = jnp.zeros_like(dv_s)
        dod_s[...] = do_ref[...] * out_ref[...]
        m_prev0, m_cur = _band_masks()
        heads = _head_masks()
        for branch in range(3):
            def block(t, carry, branch=branch):
                rows, prows, has_prev = _block_rows(branch, t)
                with_prev = SEGMENT_BLOCKS[branch] > 1
                m_prev = m_prev0 & has_prev
                q, kc, vc = q_s[rows, :], k_s[rows, :], v_s[rows, :]
                if with_prev:
                    kp, vp = k_s[prows, :], v_s[prows, :]
                do, lse_b, dod = do_ref[rows, :], lse_ref[rows, :], dod_s[rows, :]
                dq = []
                dk_cur = jnp.zeros((ATTN_BLK, 128), F32)
                dv_cur = jnp.zeros((ATTN_BLK, 128), F32)
                dk_prev = jnp.zeros((ATTN_BLK, 128), F32)
                dv_prev = jnp.zeros((ATTN_BLK, 128), F32)
                for hm in heads:
                    qh = jnp.where(hm, q, 0.0)
                    doh = jnp.where(hm, do, 0.0)
                    lse_h = jnp.max(jnp.where(hm, lse_b, NEG), axis=1, keepdims=True)
                    delta = jnp.sum(jnp.where(hm, dod, 0.0), axis=1, keepdims=True)
                    pc = jnp.exp(jnp.where(m_cur, MM_NT(qh, kc) * scale, NEG) - lse_h)
                    dsc = pc * (MM_NT(doh, vc) - delta) * scale
                    dq_h = MM(dsc, kc)
                    dk_cur += MM_TN(dsc, qh)
                    dv_cur += MM_TN(pc, doh)
                    if with_prev:
                        pp = jnp.exp(jnp.where(m_prev, MM_NT(qh, kp) * scale, NEG) - lse_h)
                        dsp = pp * (MM_NT(doh, vp) - delta) * scale
                        dq_h = dq_h + MM(dsp, kp)
                        dk_prev += MM_TN(dsp, qh)
                        dv_prev += MM_TN(pp, doh)
                    dq.append(dq_h)
                dq_s[rows, :] += jnp.where(heads[0], dq[0], dq[1])
                dk_s[rows, :] += dk_cur
                dv_s[rows, :] += dv_cur

                if with_prev:
                    @pl.when(has_prev)
                    def _():
                        dk_s[prows, :] += dk_prev
                        dv_s[prows, :] += dv_prev

                return carry

            lax.fori_loop(0, N_BLK, block, 0)
        dqkv_ref[:, 0:128] = _rope(dq_s[...], cos_ref[...], -sin_ref[...])
        dqkv_ref[:, 128:256] = _rope(dk_s[...], cos_ref[...], -sin_ref[...])
        dqkv_ref[:, 256:384] = dv_s[...]

    tab = pl.BlockSpec((SEQ, 128), lambda j: (0, 0))
    col = pl.BlockSpec((SEQ, 128), lambda j: (0, j))
    qkv = pl.BlockSpec((SEQ, 384), lambda j: (0, j))
    return pl.pallas_call(
        body, name="attn_bwd", grid=(N_PAIR,),
        in_specs=[qkv, tab, tab, col, col, col],
        out_specs=qkv,
        out_shape=jax.ShapeDtypeStruct((SEQ, IN_PAD), F32),
        scratch_shapes=[pltpu.VMEM((SEQ, 128), F32)] * 7,
        compiler_params=_cp("parallel"),
    )(proj, cos, sin_signed, cat, lse, dcat)


def _bdot(a, b, dims, precision=None):
    if precision is None:
        a = a.astype(BF16)
        b = b.astype(BF16)
    return lax.dot_general(a, b, (dims, ((0,), (0,))), preferred_element_type=F32, precision=precision)


def _make_bmm(precision):
    @jax.custom_vjp
    def nn(a, b):
        return _bdot(a, b, ((2,), (1,)), precision)

    @jax.custom_vjp
    def nt(a, b):
        return _bdot(a, b, ((2,), (2,)), precision)

    @jax.custom_vjp
    def tn(a, b):
        return _bdot(a, b, ((1,), (1,)), precision)

    nn.defvjp(lambda a, b: (nn(a, b), (a, b)), lambda r, g: (nt(g, r[1]), tn(r[0], g)))
    nt.defvjp(lambda a, b: (nt(a, b), (a, b)), lambda r, g: (nn(g, r[1]), tn(g, r[0])))
    tn.defvjp(lambda a, b: (tn(a, b), (a, b)), lambda r, g: (nt(r[1], g), nn(r[0], g)))
    return nn, nt, tn


BMM, BMM_NT, BMM_TN = _make_bmm(None)
BMMH, _, _ = _make_bmm(HI)
BMM3, _, _ = _make_bmm(lax.Precision.HIGH)


def _head_lanes(t, off):
    lane = lax.broadcasted_iota(jnp.int32, (1, 128), 1)
    return jnp.concatenate(
        [jnp.sum(t * (lane == off + h).astype(F32), axis=1, keepdims=True)[None] for h in range(NDH)], axis=0)


def _delta_chunk(qr, kr, vr, z, tail, alog_row, dt_row, nw, state):
    c = qr.shape[1]
    beta = _sigmoid(_head_lanes(tail, 0))
    g = -jnp.exp(_head_lanes(alog_row, 0)) * _softplus(_head_lanes(tail, NDH) + _head_lanes(dt_row, 0))

    q = qr * lax.rsqrt(jnp.sum(qr * qr, axis=2, keepdims=True) + EPS) * (128 ** -0.5)
    k = kr * lax.rsqrt(jnp.sum(kr * kr, axis=2, keepdims=True) + EPS)

    ri = lax.broadcasted_iota(jnp.int32, (c, c), 0)
    ci = lax.broadcasted_iota(jnp.int32, (c, c), 1)
    tril = ri >= ci
    eye = (ri == ci).astype(F32)
    lane = lax.broadcasted_iota(jnp.int32, (1, 128), 1)
    g_lanes = sum(g[h] * (lane == h).astype(F32) for h in range(NDH))
    gc = _head_lanes(MMH(tril.astype(F32), g_lanes), 0)
    g_row = BMMH(jnp.ones((NDH, c, c), F32), eye * gc)
    decay = jnp.where(tril, jnp.exp(jnp.where(tril, gc - g_row, 0.0)), 0.0)
    kb = k * beta
    a_mat = jnp.where(ri > ci, BMM_NT(kb, k) * decay, 0.0)
    power = -a_mat
    t_inv = eye + power
    for _ in range(5):
        power = BMM3(power, power)
        t_inv = t_inv + BMM3(t_inv, power)
    eg = jnp.exp(gc)
    u = BMM(t_inv, vr * beta)
    w = BMM(t_inv, kb * eg)
    qk = BMM_NT(q, k) * decay
    g_tot = jnp.sum(g, axis=1, keepdims=True)
    v_new = u - BMM(w, state)
    o = BMM(q * eg, state) + BMM(qk, v_new)
    new_state = state * jnp.exp(g_tot) + BMM_TN(k * jnp.exp(g_tot - gc), v_new)
    on = o * lax.rsqrt(jnp.mean(o * o, axis=2, keepdims=True) + EPS) * nw
    return on * _silu(z), new_state


def _heads(v, off=0):
    return jnp.concatenate([v[None, :, off + 128 * h:off + 128 * (h + 1)] for h in range(NDH)], axis=0)


def _unheads(t):
    return jnp.concatenate([t[h] for h in range(NDH)], axis=1)


def _delta_fwd(c_qkv, proj, alog_row, dt_row, nw, cat):
    def body(c_ref, z_ref, tail_ref, al_ref, dt_ref, nw_ref, _, y_ref, st_ref, state):
        @pl.when(pl.program_id(0) == 0)
        def _():
            state[...] = jnp.zeros_like(state)

        cv = c_ref[...]
        st_ref[0] = state[...]
        y, new_state = _delta_chunk(_heads(cv), _heads(cv, 512), _heads(cv, 1024), _heads(z_ref[...]), tail_ref[...],
                                    al_ref[...], dt_ref[...], nw_ref[...], state[...])
        y_ref[...] = _unheads(y)
        state[...] = new_state

    row = pl.BlockSpec((1, 128), lambda n: (0, 0))
    return pl.pallas_call(
        body, name="delta_fwd", grid=(NCH,),
        in_specs=[pl.BlockSpec((CH, 1536), lambda n: (n, 0)), pl.BlockSpec((CH, 512), lambda n: (n, DN_Z_COL // 512)),
                  pl.BlockSpec((CH, 128), lambda n: (n, DN_TAIL_BLK)), row, row, row, pl.BlockSpec(memory_space=pl.ANY)],
        out_specs=[pl.BlockSpec((CH, 512), lambda n: (n, 1)),
                   pl.BlockSpec((1, NDH, 128, 128), lambda n: (n, 0, 0, 0))],
        out_shape=[jax.ShapeDtypeStruct((SEQ, 2 * ATTN_W), F32), jax.ShapeDtypeStruct((NCH, NDH, 128, 128), F32)],
        scratch_shapes=[pltpu.VMEM((NDH, 128, 128), F32)],
        input_output_aliases={6: 0},
        compiler_params=_cp("arbitrary"),
    )(c_qkv, proj, proj, alog_row, dt_row, nw, cat)


def _delta_bwd(c_qkv, proj, alog_row, dt_row, nw, states, dcat, dproj):
    def body(c_ref, z_ref, tail_ref, al_ref, dt_ref, nw_ref, st_ref, dy_ref, _,
             dp_ref, dc_ref, dal_ref, ddt_ref, dnw_ref, dstate):
        @pl.when(pl.program_id(0) == 0)
        def _():
            dstate[...] = jnp.zeros_like(dstate)
            dal_ref[...] = jnp.zeros_like(dal_ref)
            ddt_ref[...] = jnp.zeros_like(ddt_ref)
            dnw_ref[...] = jnp.zeros_like(dnw_ref)

        cv = c_ref[...]
        _, vjp = jax.vjp(_delta_chunk, _heads(cv), _heads(cv, 512), _heads(cv, 1024), _heads(z_ref[...]),
                         tail_ref[...], al_ref[...], dt_ref[...], nw_ref[...], st_ref[0])
        dq, dk, dv, dz, dtail, dal, ddt, dnw, dst = vjp((_heads(dy_ref[...]), dstate[...]))
        dstate[...] = dst
        dc_ref[...] = jnp.concatenate([_unheads(dq), _unheads(dk), _unheads(dv)], axis=1)
        dp_ref[...] = jnp.concatenate([_unheads(dz), dtail, jnp.zeros((CH, 128), F32)], axis=1)
        dal_ref[...] += dal
        ddt_ref[...] += ddt
        dnw_ref[...] += dnw

    rev = lambda n: NCH - 1 - n
    row = pl.BlockSpec((1, 128), lambda n: (0, 0))
    return pl.pallas_call(
        body, name="delta_bwd", grid=(NCH,),
        in_specs=[pl.BlockSpec((CH, 1536), lambda n: (rev(n), 0)),
                  pl.BlockSpec((CH, 512), lambda n: (rev(n), DN_Z_COL // 512)),
                  pl.BlockSpec((CH, 128), lambda n: (rev(n), DN_TAIL_BLK)), row, row, row,
                  pl.BlockSpec((1, NDH, 128, 128), lambda n: (rev(n), 0, 0, 0)),
                  pl.BlockSpec((CH, 512), lambda n: (rev(n), 1)), pl.BlockSpec(memory_space=pl.ANY)],
        out_specs=[pl.BlockSpec((CH, 768), lambda n: (rev(n), DN_Z_COL // 768)),
                   pl.BlockSpec((CH, 1536), lambda n: (rev(n), 0)), row, row, row],
        out_shape=[jax.ShapeDtypeStruct((SEQ, IN_PAD), F32), jax.ShapeDtypeStruct((SEQ, 1536), F32)]
        + [jax.ShapeDtypeStruct((1, 128), F32)] * 3,
        scratch_shapes=[pltpu.VMEM((NDH, 128, 128), F32)],
        input_output_aliases={8: 0},
        compiler_params=_cp("arbitrary"),
    )(c_qkv, proj, proj, alog_row, dt_row, nw, states, dcat, dproj)


def _place():
    x, y, c = lax.axis_index("x"), lax.axis_index("y"), lax.axis_index("c")
    other_chips = [(1 - x, y), (x, 1 - y), (1 - x, 1 - y)]
    return x, y, c, other_chips


HBM_SPEC = pl.BlockSpec(memory_space=pltpu.HBM)


def _all_gather_hbm(shards, name):
    n = len(shards)

    def body(*refs):
        ins, outs = refs[:n], refs[n:2 * n]
        send_sems, recv_sems, local_sems = refs[2 * n:]
        x, y, c, chips = _place()
        me, sibling = (x, y, c), (x, y, 1 - c)

        def copy(b, k, block, to, src=None):
            slot = outs[b].at[4 * block[0] + 2 * block[1] + block[2]]
            return pltpu.make_async_remote_copy(
                src_ref=slot if src is None else src, dst_ref=slot,
                send_sem=send_sems.at[b, k], recv_sem=recv_sems.at[b, k], device_id=to, device_id_type=MESH)

        mine = [pltpu.make_async_copy(ins[b], outs[b].at[4 * x + 2 * y + c], local_sems.at[b]) for b in range(n)]
        for cp in mine:
            cp.start()
        first = []
        for b in range(n):
            first.append(copy(b, 0, me, sibling, src=ins[b]))
            first += [copy(b, 1 + j, me, (*chip, c), src=ins[b]) for j, chip in enumerate(chips)]
        for cp in first:
            cp.start()
        passed = []
        for b in range(n):
            for j, chip in enumerate(chips):
                copy(b, 1 + j, (*chip, c), me).wait_recv()
                fwd = copy(b, 4 + j, (*chip, c), sibling)
                fwd.start()
                passed.append(fwd)
        for b in range(n):
            copy(b, 0, sibling, me).wait_recv()
            for j, chip in enumerate(chips):
                copy(b, 4 + j, (*chip, 1 - c), me).wait_recv()
        for cp in first + passed:
            cp.wait_send()
        for cp in mine:
            cp.wait()

    return pl.pallas_call(
        body, name=name,
        in_specs=[HBM_SPEC] * n, out_specs=[HBM_SPEC] * n,
        out_shape=[jax.ShapeDtypeStruct((N_DEV,) + s.shape, s.dtype) for s in shards],
        scratch_shapes=[pltpu.SemaphoreType.DMA((n, 7)), pltpu.SemaphoreType.DMA((n, 7)), pltpu.SemaphoreType.DMA((n,))],
    )(*shards)


def _exchange_sibling(gs, name):
    n = len(gs)

    def body(*refs):
        ins, outs = refs[:n], refs[n:2 * n]
        send_sems, recv_sems = refs[2 * n:]
        x, y, c, _ = _place()
        copies = []
        for b in range(n):
            for p in range(4):
                copies.append(pltpu.make_async_remote_copy(
                    src_ref=ins[b].at[2 * p + (1 - c)], dst_ref=outs[b].at[p],
                    send_sem=send_sems.at[b, p], recv_sem=recv_sems.at[b, p],
                    device_id=(x, y, 1 - c), device_id_type=MESH))
        for cp in copies:
            cp.start()
        for cp in copies:
            cp.wait()

    return pl.pallas_call(
        body, name=name,
        in_specs=[HBM_SPEC] * n, out_specs=[HBM_SPEC] * n,
        out_shape=[jax.ShapeDtypeStruct((4,) + g.shape[1:], g.dtype) for g in gs],
        scratch_shapes=[pltpu.SemaphoreType.DMA((n, 4)), pltpu.SemaphoreType.DMA((n, 4))],
    )(*gs)


def _exchange_chips(hs, name):
    n, nl = len(hs), len(hs[0])

    def body(*refs):
        ins = [refs[b * nl:(b + 1) * nl] for b in range(n)]
        outs = refs[n * nl:n * nl + n]
        send_sems, recv_sems, local_sems = refs[n * nl + n:]
        x, y, c, chips = _place()
        my_chip = 2 * x + y
        local, sends, waits = [], [], []
        for b in range(n):
            for l in range(nl):
                s = b * nl + l
                local.append(pltpu.make_async_copy(ins[b][l].at[my_chip], outs[b].at[my_chip, l], local_sems.at[s]))
                for k, (px, py) in enumerate(chips):
                    peer = 2 * px + py
                    sends.append(pltpu.make_async_remote_copy(
                        src_ref=ins[b][l].at[peer], dst_ref=outs[b].at[my_chip, l],
                        send_sem=send_sems.at[s, k], recv_sem=recv_sems.at[s, k],
                        device_id=(px, py, c), device_id_type=MESH))
                    waits.append(pltpu.make_async_remote_copy(
                        src_ref=ins[b][l].at[peer], dst_ref=outs[b].at[peer, l],
                        send_sem=send_sems.at[s, k], recv_sem=recv_sems.at[s, k],
                        device_id=(px, py, c), device_id_type=MESH))
        for cp in local + sends:
            cp.start()
        for cp in waits:
            cp.wait_recv()
        for cp in sends:
            cp.wait_send()
        for cp in local:
            cp.wait()

    flat = [h for hb in hs for h in hb]
    return pl.pallas_call(
        body, name=name,
        in_specs=[HBM_SPEC] * (n * nl), out_specs=[HBM_SPEC] * n,
        out_shape=[jax.ShapeDtypeStruct((4, nl) + hb[0].shape[1:], hb[0].dtype) for hb in hs],
        scratch_shapes=[pltpu.SemaphoreType.DMA((n * nl, 3)), pltpu.SemaphoreType.DMA((n * nl, 3)),
                        pltpu.SemaphoreType.DMA((n * nl,))],
    )(*flat)


def _pair_add(g, r, core, name):
    _, nr, nc = g.shape
    tr = nr // 2

    def body(core_ref, g_ref, r_ref, o_ref):
        o_ref[...] = (g_ref[...] + r_ref[...]).astype(BF16)

    return pl.pallas_call(
        body, name=name,
        grid_spec=pltpu.PrefetchScalarGridSpec(
            num_scalar_prefetch=1, grid=(4, nr // tr),
            in_specs=[pl.BlockSpec((1, tr, nc), lambda p, i, core: (2 * p + core[0], i, 0)),
                      pl.BlockSpec((1, tr, nc), lambda p, i, core: (p, i, 0))],
            out_specs=pl.BlockSpec((1, tr, nc), lambda p, i, core: (p, i, 0))),
        out_shape=jax.ShapeDtypeStruct(r.shape, BF16),
        compiler_params=_cp("parallel", "parallel"),
    )(core, g, r)


def _all_gather_sum_small(v):
    rows = v.shape[0]

    def body(x_ref, sum_ref, out_ref, send_sems, recv_sems, local_sem):
        x, y, c, chips = _place()
        me, sibling = (x, y, c), (x, y, 1 - c)

        def block(px, py, pc):
            return out_ref.at[pl.ds((4 * px + 2 * py + pc) * rows, rows), :]

        def copy(k, blk, to, src=None):
            return pltpu.make_async_remote_copy(
                src_ref=block(*blk) if src is None else src, dst_ref=block(*blk),
                send_sem=send_sems.at[k], recv_sem=recv_sems.at[k], device_id=to, device_id_type=MESH)

        mine = pltpu.make_async_copy(x_ref, block(*me), local_sem)
        mine.start()
        first = [copy(0, me, sibling, src=x_ref)]
        first += [copy(1 + j, me, (*chip, c), src=x_ref) for j, chip in enumerate(chips)]
        for cp in first:
            cp.start()
        passed = [copy(4 + j, (*chip, c), sibling) for j, chip in enumerate(chips)]
        for j, chip in enumerate(chips):
            copy(1 + j, (*chip, c), me).wait_recv()
            passed[j].start()
        copy(0, sibling, me).wait_recv()
        for j, chip in enumerate(chips):
            copy(4 + j, (*chip, 1 - c), me).wait_recv()
        for cp in first + passed:
            cp.wait_send()
        mine.wait()
        total = out_ref[pl.ds(0, rows), :]
        for d in range(1, N_DEV):
            total = total + out_ref[pl.ds(d * rows, rows), :]
        sum_ref[...] = total

    vm = pl.BlockSpec(memory_space=pltpu.VMEM)
    return pl.pallas_call(
        body, name="small_all_reduce",
        in_specs=[vm], out_specs=[vm],
        out_shape=[jax.ShapeDtypeStruct((rows, 128), F32)],
        scratch_shapes=[pltpu.VMEM((N_DEV * rows, 128), F32), pltpu.SemaphoreType.DMA((7,)),
                        pltpu.SemaphoreType.DMA((7,)), pltpu.SemaphoreType.DMA],
    )(v)[0]


def _adamw(w, g, m, v):
    m = ADAM_B1 * m + (1.0 - ADAM_B1) * g
    v = ADAM_B2 * v + (1.0 - ADAM_B2) * (g * g)
    m_hat = m / (1.0 - ADAM_B1 ** ADAM_STEP)
    v_hat = v / (1.0 - ADAM_B2 ** ADAM_STEP)
    delta = -ADAM_LR * (m_hat / (jnp.sqrt(v_hat) + ADAM_EPS) + ADAM_WD * w)
    return delta, m, v


ADAM_ROWS = dict(w_in=256, w_out=128, ffn_w_in=256, ffn_w_out=176)


def _adamw_sharded(parts, w, m, v, tr, name):
    nl, nr, nc = w.shape

    def body(p_ref, w_ref, m_ref, v_ref, g_ref, d_ref, nm_ref, nv_ref):
        p = p_ref[...].astype(F32)
        g = (p[0] + p[1]) + (p[2] + p[3])
        delta, nm, nv = _adamw(w_ref[...], g, m_ref[...], v_ref[...])
        g_ref[...] = g
        d_ref[...] = delta
        nm_ref[...] = nm
        nv_ref[...] = nv

    blk = pl.BlockSpec((1, tr, nc), lambda l, i: (l, i, 0))
    return pl.pallas_call(
        body, name=name, grid=(nl, nr // tr),
        in_specs=[pl.BlockSpec((4, 1, tr, nc), lambda l, i: (0, l, i, 0)), blk, blk, blk],
        out_specs=[blk] * 4,
        out_shape=[jax.ShapeDtypeStruct(w.shape, F32)] * 4,
        compiler_params=_cp("parallel", "parallel"),
    )(parts, w, m, v)


def _adamw_small(g, w, m, v):
    def body(g_ref, w_ref, m_ref, v_ref, d_ref, nm_ref, nv_ref):
        delta, nm, nv = _adamw(w_ref[...], g_ref[...], m_ref[...], v_ref[...])
        d_ref[...] = delta
        nm_ref[...] = nm
        nv_ref[...] = nv

    return pl.pallas_call(
        body, name="adamw_small",
        out_shape=[jax.ShapeDtypeStruct(g.shape, F32)] * 3,
    )(g, w, m, v)


def _pack(arrays, rows):
    flat = jnp.concatenate([a.reshape(-1).astype(F32) for a in arrays])
    return jnp.pad(flat, (0, rows * 128 - flat.shape[0])).reshape(rows, 128)


def _unpack(packed, shapes):
    flat = packed.reshape(-1)
    out, off = [], 0
    for s in shapes:
        n = math.prod(s)
        out.append(flat[off:off + n].reshape(s))
        off += n
    return out


def _row(v, width=None):
    v = v.reshape(1, -1)
    return v if width is None else jnp.pad(v, ((0, 0), (0, width - v.shape[1])))


def _layer_fwd(x, wts, tables):
    h = _norm_fwd(x, wts["norm_pre_mix"], "norm_pre_mix")
    proj = _matmul(h, wts["w_in"], tm=512, tn=768, tk=1024, name="mm_proj")
    cat, lse = _attn_fwd(proj, *tables)
    c_qkv = _dnconv_fwd(proj, wts["dn_conv_w"])
    cat, states = _delta_fwd(c_qkv, proj, wts["dn_a_log"], wts["dn_dt_bias"], wts["dn_norm_w"], cat)
    mix = _matmul(cat, wts["w_out"], tm=512, tn=1024, tk=1024, name="mm_mix")
    x1 = _resnorm_fwd(x, mix, wts["norm_post_mix"], "norm_post_mix")
    h2 = _norm_fwd(x1, wts["norm_pre_ffn"], "norm_pre_ffn")
    pre = _matmul(h2, wts["ffn_w_in"], tm=512, tn=512, tk=1024, name="mm_ffn_in")
    act = _ffact_fwd(pre, wts["ffn_conv_w"], wts["ffn_conv_b"])
    f = _matmul(act, wts["ffn_w_out"], tm=512, tn=1024, tk=D_FF, name="mm_ffn_out")
    x2 = _resnorm_fwd(x1, f, wts["norm_post_ffn"], "norm_post_ffn")
    saved = dict(x=x, h=h, proj=proj, lse=lse, c_qkv=c_qkv, states=states, cat=cat, mix=mix, x1=x1, h2=h2, pre=pre,
                 act=act, f=f)
    return x2, saved


def _layer_bwd(dx2, wts, s, tables):
    g = {}
    df, g["norm_post_ffn"] = _norm_bwd(s["f"], wts["norm_post_ffn"], dx2, None, "norm_post_ffn_bwd")
    dact = _matmul(df, wts["ffn_w_out"], tb=True, tm=512, tn=1408, tk=1024, name="mm_dact", out_dtype=BF16)
    g["ffn_w_out"] = _matmul(s["act"], df, ta=True, tm=1408, tn=512, tk=SEQ, name="mm_dw_ffn_out")
    dpre, g["ffn_conv_w"], g["ffn_conv_b"] = _ffact_bwd(s["pre"], wts["ffn_conv_w"], wts["ffn_conv_b"], dact)
    dh2 = _matmul(dpre, wts["ffn_w_in"], tb=True, tm=1024, tn=1024, tk=1408, name="mm_dh2")
    g["ffn_w_in"] = _matmul(s["h2"], dpre, ta=True, tm=1024, tn=512, tk=SEQ, name="mm_dw_ffn_in")
    dx1, g["norm_pre_ffn"] = _norm_bwd(s["x1"], wts["norm_pre_ffn"], dh2, dx2, "norm_pre_ffn_bwd")
    dmix, g["norm_post_mix"] = _norm_bwd(s["mix"], wts["norm_post_mix"], dx1, None, "norm_post_mix_bwd")
    dcat = _matmul(dmix, wts["w_out"], tb=True, tm=512, tn=1024, tk=1024, name="mm_dcat")
    g["w_out"] = _matmul(s["cat"], dmix, ta=True, tm=1024, tn=512, tk=SEQ, name="mm_dw_out")
    dproj = _attn_bwd(s["proj"], *tables, s["cat"], s["lse"], dcat)
    dproj, dc, g["dn_a_log"], g["dn_dt_bias"], g["dn_norm_w"] = _delta_bwd(
        s["c_qkv"], s["proj"], wts["dn_a_log"], wts["dn_dt_bias"], wts["dn_norm_w"], s["states"], dcat, dproj)
    dproj, g["dn_conv_w"] = _dnconv_bwd(s["proj"], wts["dn_conv_w"], dc, dproj)
    dh = _matmul(dproj, wts["w_in"], tb=True, tm=1024, tn=1024, tk=1280, name="mm_dh")
    g["w_in"] = _matmul(s["h"], dproj, ta=True, tm=1024, tn=768, tk=SEQ, name="mm_dw_in")
    dx, g["norm_pre_mix"] = _norm_bwd(s["x"], wts["norm_pre_mix"], dh, dx1, "norm_pre_mix_bwd")
    return dx, g


BIG = ("w_in", "w_out", "ffn_w_in", "ffn_w_out")
SMALL_SHARDED = ("dn_conv_w", "ffn_conv_w")
REPLICATED = ("dn_a_log", "dn_dt_bias", "dn_norm_w", "ffn_conv_b", "norm_pre_mix", "norm_post_mix", "norm_pre_ffn",
              "norm_post_ffn")
WEIGHTS = ("w_in", "dn_conv_w", "dn_a_log", "dn_dt_bias", "dn_norm_w", "w_out", "ffn_w_in", "ffn_conv_w", "ffn_conv_b",
           "ffn_w_out", "norm_pre_mix", "norm_post_mix", "norm_pre_ffn", "norm_post_ffn")
FULL_SHAPE = dict(dn_conv_w=(DEPTH, 4, 1536), ffn_conv_w=(DEPTH, 3, 2 * D_FF), dn_a_log=(DEPTH, NDH),
                  dn_dt_bias=(DEPTH, NDH), dn_norm_w=(DEPTH, 128), ffn_conv_b=(DEPTH, 2 * D_FF),
                  norm_pre_mix=(DEPTH, D_MODEL), norm_post_mix=(DEPTH, D_MODEL), norm_pre_ffn=(DEPTH, D_MODEL),
                  norm_post_ffn=(DEPTH, D_MODEL))
SMALL_GRAD_ORDER = REPLICATED + SMALL_SHARDED
SMALL_GRAD_ROWS = 520
SMALL_W_ROWS = 48
SMALL_ADAM_ROWS = 200


def _w_in_to_kernel_order(t):
    t = jnp.concatenate([_pairs_from_qkv(t[..., :QKV_W]), t[..., QKV_W:]], axis=-1)
    return jnp.pad(t, [(0, 0)] * (t.ndim - 1) + [(0, IN_PAD - IN_COLS)])


def _w_in_from_kernel_order(t):
    return jnp.concatenate([_qkv_from_pairs(t[..., :QKV_W]), t[..., QKV_W:IN_COLS]], axis=-1)


def _cols_to_devices(t):
    nr, nc = t.shape
    return t.reshape(nr, N_DEV, nc // N_DEV).transpose(1, 0, 2)


def _cols_from_devices(t):
    return t.transpose(1, 0, 2).reshape(t.shape[1], -1)


def kernel(x, w_in, dn_conv_w, dn_a_log, dn_dt_bias, dn_norm_w, w_out, ffn_w_in, ffn_conv_w, ffn_conv_b, ffn_w_out, norm_pre_mix, norm_post_mix, norm_pre_ffn, norm_post_ffn, loss_target, m_w_in, m_dn_conv_w, m_dn_a_log, m_dn_dt_bias, m_dn_norm_w, m_w_out, m_ffn_w_in, m_ffn_conv_w, m_ffn_conv_b, m_ffn_w_out, m_norm_pre_mix, m_norm_post_mix, m_norm_pre_ffn, m_norm_post_ffn, v_w_in, v_dn_conv_w, v_dn_a_log, v_dn_dt_bias, v_dn_norm_w, v_w_out, v_ffn_w_in, v_ffn_conv_w, v_ffn_conv_b, v_ffn_w_out, v_norm_pre_mix, v_norm_post_mix, v_norm_pre_ffn, v_norm_post_ffn):
    local = dict(w_in=w_in, dn_conv_w=dn_conv_w, dn_a_log=dn_a_log, dn_dt_bias=dn_dt_bias, dn_norm_w=dn_norm_w,
                 w_out=w_out, ffn_w_in=ffn_w_in, ffn_conv_w=ffn_conv_w, ffn_conv_b=ffn_conv_b, ffn_w_out=ffn_w_out,
                 norm_pre_mix=norm_pre_mix, norm_post_mix=norm_post_mix, norm_pre_ffn=norm_pre_ffn,
                 norm_post_ffn=norm_post_ffn)
    mom_m = dict(w_in=m_w_in, dn_conv_w=m_dn_conv_w, dn_a_log=m_dn_a_log, dn_dt_bias=m_dn_dt_bias,
                 dn_norm_w=m_dn_norm_w, w_out=m_w_out, ffn_w_in=m_ffn_w_in, ffn_conv_w=m_ffn_conv_w,
                 ffn_conv_b=m_ffn_conv_b, ffn_w_out=m_ffn_w_out, norm_pre_mix=m_norm_pre_mix,
                 norm_post_mix=m_norm_post_mix, norm_pre_ffn=m_norm_pre_ffn, norm_post_ffn=m_norm_post_ffn)
    mom_v = dict(w_in=v_w_in, dn_conv_w=v_dn_conv_w, dn_a_log=v_dn_a_log, dn_dt_bias=v_dn_dt_bias,
                 dn_norm_w=v_dn_norm_w, w_out=v_w_out, ffn_w_in=v_ffn_w_in, ffn_conv_w=v_ffn_conv_w,
                 ffn_conv_b=v_ffn_conv_b, ffn_w_out=v_ffn_w_out, norm_pre_mix=v_norm_pre_mix,
                 norm_post_mix=v_norm_post_mix, norm_pre_ffn=v_norm_pre_ffn, norm_post_ffn=v_norm_post_ffn)
    dev = 4 * lax.axis_index("x") + 2 * lax.axis_index("y") + lax.axis_index("c")
    core = lax.axis_index("c").astype(jnp.int32).reshape(1)

    small_w = _pack([dn_conv_w, ffn_conv_w], SMALL_W_ROWS)
    shards = [local[n][l].astype(BF16) for n in BIG for l in range(DEPTH)]
    gathered = _all_gather_hbm(shards + [small_w], "weights_all_gather")
    g_small = gathered[-1]
    gathered = {n: gathered[i * DEPTH:(i + 1) * DEPTH] for i, n in enumerate(BIG)}
    n_dn, n_ff = DEPTH * 4 * 192, DEPTH * 3 * 704
    sm = g_small.reshape(N_DEV, -1)
    full_dn_conv = sm[:, :n_dn].reshape(N_DEV, DEPTH, 4, 192).transpose(1, 2, 0, 3).reshape(DEPTH, 4, 1536)
    full_ff_conv = _interleave_ff(
        sm[:, n_dn:n_dn + n_ff].reshape(N_DEV, DEPTH, 3, 704).transpose(1, 2, 0, 3).reshape(DEPTH, 3, 2 * D_FF))

    def layer_weights(l):
        wts = dict(
            w_in=_w_in_to_kernel_order(_cols_from_devices(gathered["w_in"][l])),
            w_out=gathered["w_out"][l].reshape(D_MODEL, D_MODEL),
            ffn_w_in=_interleave_ff(_cols_from_devices(gathered["ffn_w_in"][l])),
            ffn_w_out=gathered["ffn_w_out"][l].reshape(D_FF, D_MODEL),
            dn_conv_w=full_dn_conv[l], ffn_conv_w=full_ff_conv[l],
            ffn_conv_b=_interleave_ff(_row(ffn_conv_b[l])),
            dn_a_log=_row(dn_a_log[l], 128), dn_dt_bias=_row(dn_dt_bias[l], 128))
        for n in ("dn_norm_w", "norm_pre_mix", "norm_post_mix", "norm_pre_ffn", "norm_post_ffn"):
            wts[n] = _row(local[n][l])
        return wts

    tables = _rope_tables()
    weights = [layer_weights(l) for l in range(DEPTH)]
    act, saved = x[0], []
    for l in range(DEPTH):
        act, s = _layer_fwd(act, weights[l], tables)
        saved.append(s)
    loss_part, dact = _loss_fwd_bwd(act, loss_target[0])
    grads = [None] * DEPTH
    for l in reversed(range(DEPTH)):
        dact, grads[l] = _layer_bwd(dact, weights[l], saved[l], tables)
    grad_x = dact[None]

    def to_devices(name, l):
        t = grads[l][name]
        if name == "w_in":
            return _cols_to_devices(_w_in_from_kernel_order(t))
        if name == "ffn_w_in":
            return _cols_to_devices(_deinterleave_ff(t))
        return t.reshape(N_DEV, t.shape[0] // N_DEV, t.shape[1])

    to_dev = [to_devices(n, l) for n in BIG for l in range(DEPTH)]
    from_sibling = _exchange_sibling(to_dev, "grads_to_sibling")
    chip_sums = [_pair_add(gd, r, core, "grads_pair_add_%d" % i) for i, (gd, r) in enumerate(zip(to_dev, from_sibling))]
    parts = _exchange_chips([chip_sums[i * DEPTH:(i + 1) * DEPTH] for i in range(len(BIG))], "grads_to_chips")

    def small_grad(name):
        t = jnp.stack([grads[l][name] for l in range(DEPTH)])
        if name in ("dn_a_log", "dn_dt_bias"):
            t = t[:, 0, :NDH]
        if name in ("ffn_conv_w", "ffn_conv_b"):
            t = _deinterleave_ff(t)
        return t.reshape(FULL_SHAPE[name])

    small_part = _pack([small_grad(n) for n in SMALL_GRAD_ORDER] + [loss_part[0, :1]], SMALL_GRAD_ROWS)
    small_sum = _all_gather_sum_small(small_part)
    small_g = dict(zip(SMALL_GRAD_ORDER + ("loss",), _unpack(small_sum, [FULL_SHAPE[n] for n in SMALL_GRAD_ORDER] + [(1,)])))
    loss = small_g["loss"][0]
    small_g["dn_conv_w"] = lax.dynamic_slice_in_dim(small_g["dn_conv_w"], dev * 192, 192, axis=2)
    small_g["ffn_conv_w"] = lax.dynamic_slice_in_dim(small_g["ffn_conv_w"], dev * 704, 704, axis=2)

    out_g, out_d, out_m, out_v = {}, {}, {}, {}
    for n, p in zip(BIG, parts):
        out_g[n], out_d[n], out_m[n], out_v[n] = _adamw_sharded(p, local[n], mom_m[n], mom_v[n], ADAM_ROWS[n], "adamw_" + n)
    shapes = [small_g[n].shape for n in SMALL_GRAD_ORDER]
    d_s, m_s, v_s = _adamw_small(_pack([small_g[n] for n in SMALL_GRAD_ORDER], SMALL_ADAM_ROWS),
                                 _pack([local[n] for n in SMALL_GRAD_ORDER], SMALL_ADAM_ROWS),
                                 _pack([mom_m[n] for n in SMALL_GRAD_ORDER], SMALL_ADAM_ROWS),
                                 _pack([mom_v[n] for n in SMALL_GRAD_ORDER], SMALL_ADAM_ROWS))
    for n, d, m, v in zip(SMALL_GRAD_ORDER, _unpack(d_s, shapes), _unpack(m_s, shapes), _unpack(v_s, shapes)):
        out_g[n], out_d[n], out_m[n], out_v[n] = small_g[n], d, m, v
    return (loss, grad_x, *[out_g[n] for n in WEIGHTS], *[out_d[n] for n in WEIGHTS],
            *[out_m[n] for n in WEIGHTS], *[out_v[n] for n in WEIGHTS])
```

```python
import functools
import math

import jax
import jax.numpy as jnp
from jax import lax
from jax.experimental import pallas as pl
from jax.experimental.pallas import tpu as pltpu

F32 = jnp.float32
BF16 = jnp.bfloat16
HI = lax.Precision.HIGHEST
MESH = pl.DeviceIdType.MESH

N_DEV = 8
SEQ = 2048
D_MODEL = 1024
DEPTH = 2
N_PAIR = 4
HEAD_DIM = 64
ATTN_W = 512
ATTN_BLK = 128
DILATIONS = (1, 4, 16)
SEGMENT_BLOCKS = (16, 4, 1)
N_BLK = SEQ // ATTN_BLK
NDH = 4
CH = 64
NCH = SEQ // CH
IN_COLS = 3592
IN_PAD = 3840
QKV_W = 3 * ATTN_W
DN_QKV_BLK0 = QKV_W // 128
DN_QKV_BLKS = 1536 // 128
DN_Z_COL = 3072
DN_TAIL_BLK = 3584 // 128
D_FF = 2816
FF_BLKS = D_FF // 128
EPS = 1e-6
NEG = -1e30
ROPE_THETA = 10000.0

ADAM_LR, ADAM_B1, ADAM_B2, ADAM_EPS, ADAM_WD, ADAM_STEP = 0.001, 0.9, 0.999, 1e-08, 0.01, 10

VMEM_LIMIT = 56 * 1024 * 1024


def _cp(*sem):
    return pltpu.CompilerParams(dimension_semantics=sem, vmem_limit_bytes=VMEM_LIMIT)


def _dot(a, b, dims, precision=None):
    if precision is None:
        a = a.astype(BF16)
        b = b.astype(BF16)
    return lax.dot_general(a, b, (dims, ((), ())), preferred_element_type=F32, precision=precision)


def _make_mm(precision):
    @jax.custom_vjp
    def nn(a, b):
        return _dot(a, b, ((1,), (0,)), precision)

    @jax.custom_vjp
    def nt(a, b):
        return _dot(a, b, ((1,), (1,)), precision)

    @jax.custom_vjp
    def tn(a, b):
        return _dot(a, b, ((0,), (0,)), precision)

    nn.defvjp(lambda a, b: (nn(a, b), (a, b)), lambda r, g: (nt(g, r[1]), tn(r[0], g)))
    nt.defvjp(lambda a, b: (nt(a, b), (a, b)), lambda r, g: (nn(g, r[1]), tn(g, r[0])))
    tn.defvjp(lambda a, b: (tn(a, b), (a, b)), lambda r, g: (nt(r[1], g), nn(r[0], g)))
    return nn, nt, tn


MM, MM_NT, MM_TN = _make_mm(None)
MMH, _, _ = _make_mm(HI)


def _matmul(a, b, *, ta=False, tb=False, tm, tn, tk, name, out_dtype=F32):
    (k_dim, m_dim) = a.shape if ta else a.shape[::-1]
    (n_dim, k2) = b.shape if tb else b.shape[::-1]
    assert k_dim == k2 and m_dim % tm == 0 and n_dim % tn == 0 and k_dim % tk == 0, (a.shape, b.shape, tm, tn, tk)
    nk = k_dim // tk
    dims = ((0 if ta else 1,), (1 if tb else 0,))

    def body(a_ref, b_ref, o_ref, *acc):
        p = _dot(a_ref[...], b_ref[...], dims)
        if nk == 1:
            o_ref[...] = p.astype(out_dtype)
            return
        acc_ref, k = acc[0], pl.program_id(2)

        @pl.when(k == 0)
        def _():
            acc_ref[...] = p

        @pl.when(k > 0)
        def _():
            acc_ref[...] += p

        @pl.when(k == nk - 1)
        def _():
            o_ref[...] = acc_ref[...].astype(out_dtype)

    a_spec = pl.BlockSpec((tk, tm), lambda i, j, k: (k, i)) if ta else pl.BlockSpec((tm, tk), lambda i, j, k: (i, k))
    b_spec = pl.BlockSpec((tn, tk), lambda i, j, k: (j, k)) if tb else pl.BlockSpec((tk, tn), lambda i, j, k: (k, j))
    return pl.pallas_call(
        body, name=name,
        grid=(m_dim // tm, n_dim // tn, nk),
        in_specs=[a_spec, b_spec],
        out_specs=pl.BlockSpec((tm, tn), lambda i, j, k: (i, j)),
        out_shape=jax.ShapeDtypeStruct((m_dim, n_dim), out_dtype),
        scratch_shapes=[pltpu.VMEM((tm, tn), F32)] if nk > 1 else [],
        compiler_params=_cp("parallel", "parallel", "arbitrary"),
    )(a, b)


NORM_ROWS = 256


def _rms(x, w):
    return x * lax.rsqrt(jnp.mean(x * x, axis=1, keepdims=True) + EPS) * w


def _norm_fwd(x, w_row, name, out_dtype=BF16):
    def body(x_ref, w_ref, o_ref):
        o_ref[...] = _rms(x_ref[...], w_ref[...]).astype(out_dtype)

    return pl.pallas_call(
        body, name=name, grid=(SEQ // NORM_ROWS,),
        in_specs=[pl.BlockSpec((NORM_ROWS, D_MODEL), lambda i: (i, 0)), pl.BlockSpec((1, D_MODEL), lambda i: (0, 0))],
        out_specs=pl.BlockSpec((NORM_ROWS, D_MODEL), lambda i: (i, 0)),
        out_shape=jax.ShapeDtypeStruct((SEQ, D_MODEL), out_dtype),
        compiler_params=_cp("parallel"),
    )(x, w_row)


def _resnorm_fwd(x, f, w_row, name):
    def body(x_ref, f_ref, w_ref, o_ref):
        o_ref[...] = x_ref[...] + _rms(f_ref[...], w_ref[...])

    blk = pl.BlockSpec((NORM_ROWS, D_MODEL), lambda i: (i, 0))
    return pl.pallas_call(
        body, name=name, grid=(SEQ // NORM_ROWS,),
        in_specs=[blk, blk, pl.BlockSpec((1, D_MODEL), lambda i: (0, 0))],
        out_specs=blk, out_shape=jax.ShapeDtypeStruct((SEQ, D_MODEL), F32),
        compiler_params=_cp("parallel"),
    )(x, f, w_row)


def _norm_bwd(x, w_row, dy, add, name):
    has_add = add is not None

    def body(*refs):
        if has_add:
            x_ref, w_ref, dy_ref, add_ref, dx_ref, dw_ref = refs
        else:
            x_ref, w_ref, dy_ref, dx_ref, dw_ref = refs
        _, vjp = jax.vjp(_rms, x_ref[...], w_ref[...])
        dx, dw = vjp(dy_ref[...])
        dx_ref[...] = dx + add_ref[...] if has_add else dx

        @pl.when(pl.program_id(0) == 0)
        def _():
            dw_ref[...] = jnp.zeros_like(dw_ref)

        dw_ref[...] += dw

    blk = pl.BlockSpec((NORM_ROWS, D_MODEL), lambda i: (i, 0))
    row = pl.BlockSpec((1, D_MODEL), lambda i: (0, 0))
    ins = [x, w_row, dy] + ([add] if has_add else [])
    return pl.pallas_call(
        body, name=name, grid=(SEQ // NORM_ROWS,),
        in_specs=[blk, row, blk] + ([blk] if has_add else []),
        out_specs=[blk, row],
        out_shape=[jax.ShapeDtypeStruct((SEQ, D_MODEL), F32), jax.ShapeDtypeStruct((1, D_MODEL), F32)],
        compiler_params=_cp("arbitrary"),
    )(*ins)


def _loss_fwd_bwd(y, target):
    def body(y_ref, t_ref, loss_ref, dy_ref):
        err = y_ref[...] - t_ref[...]
        dy_ref[...] = err * (1.0 / D_MODEL)

        @pl.when(pl.program_id(0) == 0)
        def _():
            loss_ref[...] = jnp.zeros_like(loss_ref)

        part = jnp.sum(jnp.sum(err * err, axis=1, keepdims=True) * (1.0 / D_MODEL), axis=0, keepdims=True)
        loss_ref[...] += 0.5 * jnp.broadcast_to(part, loss_ref.shape)

    blk = pl.BlockSpec((NORM_ROWS, D_MODEL), lambda i: (i, 0))
    return pl.pallas_call(
        body, name="loss", grid=(SEQ // NORM_ROWS,),
        in_specs=[blk, blk],
        out_specs=[pl.BlockSpec((1, 128), lambda i: (0, 0)), blk],
        out_shape=[jax.ShapeDtypeStruct((1, 128), F32), jax.ShapeDtypeStruct((SEQ, D_MODEL), F32)],
        compiler_params=_cp("arbitrary"),
    )(y, target)


def _make_shift(j):
    def down(x):
        row = lax.broadcasted_iota(jnp.int32, x.shape, 0)
        return jnp.where(row >= j, pltpu.roll(x, j, 0), 0.0)

    def up(x):
        n = x.shape[0]
        row = lax.broadcasted_iota(jnp.int32, x.shape, 0)
        return jnp.where(row < n - j, pltpu.roll(x, n - j, 0), 0.0)

    f = jax.custom_vjp(down)
    f.defvjp(lambda x: (down(x), None), lambda _, g: (up(g),))
    return f


_SHIFT = {j: _make_shift(j) for j in (1, 2, 3)}


def _causal_conv(x, taps):
    n = len(taps)
    acc = x * taps[n - 1]
    for k in range(n - 1):
        acc = acc + _SHIFT[n - 1 - k](x) * taps[k]
    return acc


def _tap_rows(w_ref, lanes=slice(None)):
    return tuple(w_ref[k:k + 1, lanes] for k in range(w_ref.shape[0]))


def _sigmoid(x):
    return 1.0 / (1.0 + jnp.exp(-x))


def _silu(x):
    return x * _sigmoid(x)


def _softplus(x):
    return jnp.maximum(x, 0.0) + jnp.log(1.0 + jnp.exp(-jnp.abs(x)))


def _gelu_tanh(x):
    return 0.5 * x * (1.0 + jnp.tanh(math.sqrt(2.0 / math.pi) * (x + 0.044715 * (x * x * x))))


def _dnconv_fn(x, taps):
    return _silu(_causal_conv(x, taps))


def _dnconv_fwd(proj, conv_w):
    def body(x_ref, w_ref, o_ref):
        o_ref[...] = _dnconv_fn(x_ref[...], _tap_rows(w_ref))

    return pl.pallas_call(
        body, name="dnconv_fwd", grid=(DN_QKV_BLKS,),
        in_specs=[pl.BlockSpec((SEQ, 128), lambda j: (0, DN_QKV_BLK0 + j)), pl.BlockSpec((4, 128), lambda j: (0, j))],
        out_specs=pl.BlockSpec((SEQ, 128), lambda j: (0, j)),
        out_shape=jax.ShapeDtypeStruct((SEQ, 1536), F32),
        compiler_params=_cp("parallel"),
    )(proj, conv_w)


def _dnconv_bwd(proj, conv_w, dc, dproj):
    def body(x_ref, w_ref, dc_ref, _, dx_ref, dw_ref):
        _, vjp = jax.vjp(_dnconv_fn, x_ref[...], _tap_rows(w_ref))
        dx, dw = vjp(dc_ref[...])
        dx_ref[...] = dx
        for k, row in enumerate(dw):
            dw_ref[k:k + 1, :] = row

    return pl.pallas_call(
        body, name="dnconv_bwd", grid=(DN_QKV_BLKS,),
        in_specs=[pl.BlockSpec((SEQ, 128), lambda j: (0, DN_QKV_BLK0 + j)), pl.BlockSpec((4, 128), lambda j: (0, j)),
                  pl.BlockSpec((SEQ, 128), lambda j: (0, j)), pl.BlockSpec(memory_space=pl.ANY)],
        out_specs=[pl.BlockSpec((SEQ, 128), lambda j: (0, DN_QKV_BLK0 + j)), pl.BlockSpec((4, 128), lambda j: (0, j))],
        out_shape=[jax.ShapeDtypeStruct((SEQ, IN_PAD), F32), jax.ShapeDtypeStruct((4, 1536), F32)],
        input_output_aliases={3: 0},
        compiler_params=_cp("parallel"),
    )(proj, conv_w, dc, dproj)


def _ffact_fn(pg, pu, wg, wu, bg, bu):
    return _gelu_tanh(_causal_conv(pg, wg) + bg) * (_causal_conv(pu, wu) + bu)


def _ffact_args(p_ref, w_ref, b_ref):
    g, u = slice(0, 128), slice(128, 256)
    return (p_ref[:, g], p_ref[:, u], _tap_rows(w_ref, g), _tap_rows(w_ref, u), b_ref[:, g], b_ref[:, u])


def _ffact_fwd(pre, conv_w, conv_b):
    def body(p_ref, w_ref, b_ref, o_ref):
        o_ref[...] = _ffact_fn(*_ffact_args(p_ref, w_ref, b_ref)).astype(BF16)

    return pl.pallas_call(
        body, name="ffact_fwd", grid=(FF_BLKS,),
        in_specs=[pl.BlockSpec((SEQ, 256), lambda j: (0, j)), pl.BlockSpec((3, 256), lambda j: (0, j)),
                  pl.BlockSpec((1, 256), lambda j: (0, j))],
        out_specs=pl.BlockSpec((SEQ, 128), lambda j: (0, j)),
        out_shape=jax.ShapeDtypeStruct((SEQ, D_FF), BF16),
        compiler_params=_cp("parallel"),
    )(pre, conv_w, conv_b)


def _ffact_bwd(pre, conv_w, conv_b, dact):
    def body(p_ref, w_ref, b_ref, da_ref, dp_ref, dw_ref, db_ref):
        _, vjp = jax.vjp(_ffact_fn, *_ffact_args(p_ref, w_ref, b_ref))
        dpg, dpu, dwg, dwu, dbg, dbu = vjp(da_ref[...].astype(F32))
        dp_ref[:, 0:128] = dpg
        dp_ref[:, 128:256] = dpu
        for k in range(3):
            dw_ref[k:k + 1, 0:128] = dwg[k]
            dw_ref[k:k + 1, 128:256] = dwu[k]
        db_ref[:, 0:128] = dbg
        db_ref[:, 128:256] = dbu

    return pl.pallas_call(
        body, name="ffact_bwd", grid=(FF_BLKS,),
        in_specs=[pl.BlockSpec((SEQ, 256), lambda j: (0, j)), pl.BlockSpec((3, 256), lambda j: (0, j)),
                  pl.BlockSpec((1, 256), lambda j: (0, j)), pl.BlockSpec((SEQ, 128), lambda j: (0, j))],
        out_specs=[pl.BlockSpec((SEQ, 256), lambda j: (0, j)), pl.BlockSpec((3, 256), lambda j: (0, j)),
                   pl.BlockSpec((1, 256), lambda j: (0, j))],
        out_shape=[jax.ShapeDtypeStruct((SEQ, 2 * D_FF), F32), jax.ShapeDtypeStruct((3, 2 * D_FF), F32),
                   jax.ShapeDtypeStruct((1, 2 * D_FF), F32)],
        compiler_params=_cp("parallel"),
    )(pre, conv_w, conv_b, dact)


def _interleave_ff(t):
    lead = t.shape[:-1]
    return t.reshape(lead + (2, FF_BLKS, 128)).swapaxes(-3, -2).reshape(lead + (2 * D_FF,))


def _deinterleave_ff(t):
    lead = t.shape[:-1]
    return t.reshape(lead + (FF_BLKS, 2, 128)).swapaxes(-3, -2).reshape(lead + (2 * D_FF,))


def _rope_tables():
    inv = 1.0 / (ROPE_THETA ** (jnp.arange(0, HEAD_DIM, 2, dtype=F32) / HEAD_DIM))
    ang = jnp.arange(SEQ, dtype=F32)[:, None] * inv[None, :]
    cos = jnp.tile(jnp.cos(ang), (1, 4))
    sin = jnp.tile(jnp.sin(ang), (1, 4))
    sign = jnp.where((jnp.arange(128) % HEAD_DIM) < HEAD_DIM // 2, -1.0, 1.0).astype(F32)
    return cos, sin * sign[None, :]


def _rope(x, cos, sin_signed):
    lane = lax.broadcasted_iota(jnp.int32, x.shape, 1)
    partner = jnp.where((lane % HEAD_DIM) < HEAD_DIM // 2, pltpu.roll(x, 128 - HEAD_DIM // 2, 1),
                        pltpu.roll(x, HEAD_DIM // 2, 1))
    return x * cos + partner * sin_signed


def _pairs_from_qkv(t):
    lead = t.shape[:-1]
    return t.reshape(lead + (3, N_PAIR, 128)).swapaxes(-3, -2).reshape(lead + (QKV_W,))


def _qkv_from_pairs(t):
    lead = t.shape[:-1]
    return t.reshape(lead + (N_PAIR, 3, 128)).swapaxes(-3, -2).reshape(lead + (QKV_W,))


def _band_masks():
    a = lax.broadcasted_iota(jnp.int32, (ATTN_BLK, ATTN_BLK), 0)
    c = lax.broadcasted_iota(jnp.int32, (ATTN_BLK, ATTN_BLK), 1)
    return c >= a, c <= a


def _head_masks():
    lane = lax.broadcasted_iota(jnp.int32, (1, 128), 1)
    return [(lane // HEAD_DIM) == h for h in range(2)]


def _block_rows(branch, t):
    d, per_seg = DILATIONS[branch], SEGMENT_BLOCKS[branch]
    if d == 1:
        start = pl.multiple_of(t * ATTN_BLK, ATTN_BLK)
        prev = pl.multiple_of(jnp.maximum(t - 1, 0) * ATTN_BLK, ATTN_BLK)
        return pl.ds(start, ATTN_BLK), pl.ds(prev, ATTN_BLK), t > 0
    r, n = t // per_seg, t % per_seg
    start = n * (ATTN_BLK * d) + r
    prev = jnp.maximum(n - 1, 0) * (ATTN_BLK * d) + r
    return pl.ds(start, ATTN_BLK, stride=d), pl.ds(prev, ATTN_BLK, stride=d), n > 0


def _attn_fwd(proj, cos, sin_signed):
    scale = HEAD_DIM ** -0.5

    def body(qkv_ref, cos_ref, sin_ref, out_ref, lse_ref, q_s, k_s, v_s, *branch_s):
        o_s, l_s = branch_s[:3], branch_s[3:]
        q_s[...] = _rope(qkv_ref[:, 0:128], cos_ref[...], sin_ref[...])
        k_s[...] = _rope(qkv_ref[:, 128:256], cos_ref[...], sin_ref[...])
        v_s[...] = qkv_ref[:, 256:384]
        m_prev0, m_cur = _band_masks()
        heads = _head_masks()
        for branch in range(3):
            def block(t, carry, branch=branch):
                rows, prows, has_prev = _block_rows(branch, t)
                with_prev = SEGMENT_BLOCKS[branch] > 1
                m_prev = m_prev0 & has_prev
                q, kc, vc = q_s[rows, :], k_s[rows, :], v_s[rows, :]
                if with_prev:
                    kp, vp = k_s[prows, :], v_s[prows, :]
                outs, lses = [], []
                for hm in heads:
                    qh = jnp.where(hm, q, 0.0)
                    sc = jnp.where(m_cur, MM_NT(qh, kc) * scale, NEG)
                    m = jnp.max(sc, axis=1, keepdims=True)
                    if with_prev:
                        sp = jnp.where(m_prev, MM_NT(qh, kp) * scale, NEG)
                        m = jnp.maximum(jnp.max(sp, axis=1, keepdims=True), m)
                    ec = jnp.exp(sc - m)
                    l = jnp.sum(ec, axis=1, keepdims=True)
                    acc = MM(ec, vc)
                    if with_prev:
                        ep = jnp.exp(sp - m)
                        l = l + jnp.sum(ep, axis=1, keepdims=True)
                        acc = acc + MM(ep, vp)
                    outs.append(acc / l)
                    lses.append(m + jnp.log(l))
                o_s[branch][rows, :] = jnp.where(heads[0], outs[0], outs[1])
                l_s[branch][rows, :] = jnp.where(heads[0], lses[0], lses[1])
                return carry

            lax.fori_loop(0, N_BLK, block, 0)
        l0, l1, l2 = l_s[0][...], l_s[1][...], l_s[2][...]
        m = jnp.maximum(jnp.maximum(l0, l1), l2)
        w0, w1, w2 = jnp.exp(l0 - m), jnp.exp(l1 - m), jnp.exp(l2 - m)
        den = w0 + w1 + w2
        out_ref[...] = (w0 * o_s[0][...] + w1 * o_s[1][...] + w2 * o_s[2][...]) / den
        lse_ref[...] = m + jnp.log(den)

    tab = pl.BlockSpec((SEQ, 128), lambda j: (0, 0))
    col = pl.BlockSpec((SEQ, 128), lambda j: (0, j))
    return pl.pallas_call(
        body, name="attn_fwd", grid=(N_PAIR,),
        in_specs=[pl.BlockSpec((SEQ, 384), lambda j: (0, j)), tab, tab],
        out_specs=[col, col],
        out_shape=[jax.ShapeDtypeStruct((SEQ, 2 * ATTN_W), F32), jax.ShapeDtypeStruct((SEQ, ATTN_W), F32)],
        scratch_shapes=[pltpu.VMEM((SEQ, 128), F32)] * 9,
        compiler_params=_cp("parallel"),
    )(proj, cos, sin_signed)


def _attn_bwd(proj, cos, sin_signed, cat, lse, dcat):
    scale = HEAD_DIM ** -0.5

    def body(qkv_ref, cos_ref, sin_ref, out_ref, lse_ref, do_ref, dqkv_ref, q_s, k_s, v_s, dq_s, dk_s, dv_s, dod_s):
        q_s[...] = _rope(qkv_ref[:, 0:128], cos_ref[...], sin_ref[...])
        k_s[...] = _rope(qkv_ref[:, 128:256], cos_ref[...], sin_ref[...])
        v_s[...] = qkv_ref[:, 256:384]
        dq_s[...] = jnp.zeros_like(dq_s)
        dk_s[...] = jnp.zeros_like(dk_s)
        dv_s[...] = jnp.zeros_like(dv_s)
        dod_s[...] = do_ref[...] * out_ref[...]
        m_prev0, m_cur = _band_masks()
        heads = _head_masks()
        for branch in range(3):
            def block(t, carry, branch=branch):
                rows, prows, has_prev = _block_rows(branch, t)
                with_prev = SEGMENT_BLOCKS[branch] > 1
                m_prev = m_prev0 & has_prev
                q, kc, vc = q_s[rows, :], k_s[rows, :], v_s[rows, :]
                if with_prev:
                    kp, vp = k_s[prows, :], v_s[prows, :]
                do, lse_b, dod = do_ref[rows, :], lse_ref[rows, :], dod_s[rows, :]
                dq = []
                dk_cur = jnp.zeros((ATTN_BLK, 128), F32)
                dv_cur = jnp.zeros((ATTN_BLK, 128), F32)
                dk_prev = jnp.zeros((ATTN_BLK, 128), F32)
                dv_prev = jnp.zeros((ATTN_BLK, 128), F32)
                for hm in heads:
                    qh = jnp.where(hm, q, 0.0)
                    doh = jnp.where(hm, do, 0.0)
                    lse_h = jnp.max(jnp.where(hm, lse_b, NEG), axis=1, keepdims=True)
                    delta = jnp.sum(jnp.where(hm, dod, 0.0), axis=1, keepdims=True)
                    pc = jnp.exp(jnp.where(m_cur, MM_NT(qh, kc) * scale, NEG) - lse_h)
                    dsc = pc * (MM_NT(doh, vc) - delta) * scale
                    dq_h = MM(dsc, kc)
                    dk_cur += MM_TN(dsc, qh)
                    dv_cur += MM_TN(pc, doh)
                    if with_prev:
                        pp = jnp.exp(jnp.where(m_prev, MM_NT(qh, kp) * scale, NEG) - lse_h)
                        dsp = pp * (MM_NT(doh, vp) - delta) * scale
                        dq_h = dq_h + MM(dsp, kp)
                        dk_prev += MM_TN(dsp, qh)
                        dv_prev += MM_TN(pp, doh)
                    dq.append(dq_h)
                dq_s[rows, :] += jnp.where(heads[0], dq[0], dq[1])
                dk_s[rows, :] += dk_cur
                dv_s[rows, :] += dv_cur

                if with_prev:
                    @pl.when(has_prev)
                    def _():
                        dk_s[prows, :] += dk_prev
                        dv_s[prows, :] += dv_prev

                return carry

            lax.fori_loop(0, N_BLK, block, 0)
        dqkv_ref[:, 0:128] = _rope(dq_s[...], cos_ref[...], -sin_ref[...])
        dqkv_ref[:, 128:256] = _rope(dk_s[...], cos_ref[...], -sin_ref[...])
        dqkv_ref[:, 256:384] = dv_s[...]

    tab = pl.BlockSpec((SEQ, 128), lambda j: (0, 0))
    col = pl.BlockSpec((SEQ, 128), lambda j: (0, j))
    qkv = pl.BlockSpec((SEQ, 384), lambda j: (0, j))
    return pl.pallas_call(
        body, name="attn_bwd", grid=(N_PAIR,),
        in_specs=[qkv, tab, tab, col, col, col],
        out_specs=qkv,
        out_shape=jax.ShapeDtypeStruct((SEQ, IN_PAD), F32),
        scratch_shapes=[pltpu.VMEM((SEQ, 128), F32)] * 7,
        compiler_params=_cp("parallel"),
    )(proj, cos, sin_signed, cat, lse, dcat)


def _bdot(a, b, dims, precision=None):
    if precision is None:
        a = a.astype(BF16)
        b = b.astype(BF16)
    return lax.dot_general(a, b, (dims, ((0,), (0,))), preferred_element_type=F32, precision=precision)


def _make_bmm(precision):
    @jax.custom_vjp
    def nn(a, b):
        return _bdot(a, b, ((2,), (1,)), precision)

    @jax.custom_vjp
    def nt(a, b):
        return _bdot(a, b, ((2,), (2,)), precision)

    @jax.custom_vjp
    def tn(a, b):
        return _bdot(a, b, ((1,), (1,)), precision)

    nn.defvjp(lambda a, b: (nn(a, b), (a, b)), lambda r, g: (nt(g, r[1]), tn(r[0], g)))
    nt.defvjp(lambda a, b: (nt(a, b), (a, b)), lambda r, g: (nn(g, r[1]), tn(g, r[0])))
    tn.defvjp(lambda a, b: (tn(a, b), (a, b)), lambda r, g: (nt(r[1], g), nn(r[0], g)))
    return nn, nt, tn


BMM, BMM_NT, BMM_TN = _make_bmm(None)
BMMH, _, _ = _make_bmm(HI)
BMM3, _, _ = _make_bmm(lax.Precision.HIGH)


def _head_lanes(t, off):
    lane = lax.broadcasted_iota(jnp.int32, (1, 128), 1)
    return jnp.concatenate(
        [jnp.sum(t * (lane == off + h).astype(F32), axis=1, keepdims=True)[None] for h in range(NDH)], axis=0)


def _delta_chunk(qr, kr, vr, z, tail, alog_row, dt_row, nw, state):
    c = qr.shape[1]
    beta = _sigmoid(_head_lanes(tail, 0))
    g = -jnp.exp(_head_lanes(alog_row, 0)) * _softplus(_head_lanes(tail, NDH) + _head_lanes(dt_row, 0))

    q = qr * lax.rsqrt(jnp.sum(qr * qr, axis=2, keepdims=True) + EPS) * (128 ** -0.5)
    k = kr * lax.rsqrt(jnp.sum(kr * kr, axis=2, keepdims=True) + EPS)

    ri = lax.broadcasted_iota(jnp.int32, (c, c), 0)
    ci = lax.broadcasted_iota(jnp.int32, (c, c), 1)
    tril = ri >= ci
    eye = (ri == ci).astype(F32)
    lane = lax.broadcasted_iota(jnp.int32, (1, 128), 1)
    g_lanes = sum(g[h] * (lane == h).astype(F32) for h in range(NDH))
    gc = _head_lanes(MMH(tril.astype(F32), g_lanes), 0)
    g_row = BMMH(jnp.ones((NDH, c, c), F32), eye * gc)
    decay = jnp.where(tril, jnp.exp(jnp.where(tril, gc - g_row, 0.0)), 0.0)
    kb = k * beta
    a_mat = jnp.where(ri > ci, BMM_NT(kb, k) * decay, 0.0)
    power = -a_mat
    t_inv = eye + power
    for _ in range(5):
        power = BMM3(power, power)
        t_inv = t_inv + BMM3(t_inv, power)
    eg = jnp.exp(gc)
    u = BMM(t_inv, vr * beta)
    w = BMM(t_inv, kb * eg)
    qk = BMM_NT(q, k) * decay
    g_tot = jnp.sum(g, axis=1, keepdims=True)
    v_new = u - BMM(w, state)
    o = BMM(q * eg, state) + BMM(qk, v_new)
    new_state = state * jnp.exp(g_tot) + BMM_TN(k * jnp.exp(g_tot - gc), v_new)
    on = o * lax.rsqrt(jnp.mean(o * o, axis=2, keepdims=True) + EPS) * nw
    return on * _silu(z), new_state


def _heads(v, off=0):
    return jnp.concatenate([v[None, :, off + 128 * h:off + 128 * (h + 1)] for h in range(NDH)], axis=0)


def _unheads(t):
    return jnp.concatenate([t[h] for h in range(NDH)], axis=1)


def _delta_fwd(c_qkv, proj, alog_row, dt_row, nw, cat):
    def body(c_ref, z_ref, tail_ref, al_ref, dt_ref, nw_ref, _, y_ref, st_ref, state):
        @pl.when(pl.program_id(0) == 0)
        def _():
            state[...] = jnp.zeros_like(state)

        cv = c_ref[...]
        st_ref[0] = state[...]
        y, new_state = _delta_chunk(_heads(cv), _heads(cv, 512), _heads(cv, 1024), _heads(z_ref[...]), tail_ref[...],
                                    al_ref[...], dt_ref[...], nw_ref[...], state[...])
        y_ref[...] = _unheads(y)
        state[...] = new_state

    row = pl.BlockSpec((1, 128), lambda n: (0, 0))
    return pl.pallas_call(
        body, name="delta_fwd", grid=(NCH,),
        in_specs=[pl.BlockSpec((CH, 1536), lambda n: (n, 0)), pl.BlockSpec((CH, 512), lambda n: (n, DN_Z_COL // 512)),
                  pl.BlockSpec((CH, 128), lambda n: (n, DN_TAIL_BLK)), row, row, row, pl.BlockSpec(memory_space=pl.ANY)],
        out_specs=[pl.BlockSpec((CH, 512), lambda n: (n, 1)),
                   pl.BlockSpec((1, NDH, 128, 128), lambda n: (n, 0, 0, 0))],
        out_shape=[jax.ShapeDtypeStruct((SEQ, 2 * ATTN_W), F32), jax.ShapeDtypeStruct((NCH, NDH, 128, 128), F32)],
        scratch_shapes=[pltpu.VMEM((NDH, 128, 128), F32)],
        input_output_aliases={6: 0},
        compiler_params=_cp("arbitrary"),
    )(c_qkv, proj, proj, alog_row, dt_row, nw, cat)


def _delta_bwd(c_qkv, proj, alog_row, dt_row, nw, states, dcat, dproj):
    def body(c_ref, z_ref, tail_ref, al_ref, dt_ref, nw_ref, st_ref, dy_ref, _,
             dp_ref, dc_ref, dal_ref, ddt_ref, dnw_ref, dstate):
        @pl.when(pl.program_id(0) == 0)
        def _():
            dstate[...] = jnp.zeros_like(dstate)
            dal_ref[...] = jnp.zeros_like(dal_ref)
            ddt_ref[...] = jnp.zeros_like(ddt_ref)
            dnw_ref[...] = jnp.zeros_like(dnw_ref)

        cv = c_ref[...]
        _, vjp = jax.vjp(_delta_chunk, _heads(cv), _heads(cv, 512), _heads(cv, 1024), _heads(z_ref[...]),
                         tail_ref[...], al_ref[...], dt_ref[...], nw_ref[...], st_ref[0])
        dq, dk, dv, dz, dtail, dal, ddt, dnw, dst = vjp((_heads(dy_ref[...]), dstate[...]))
        dstate[...] = dst
        dc_ref[...] = jnp.concatenate([_unheads(dq), _unheads(dk), _unheads(dv)], axis=1)
        dp_ref[...] = jnp.concatenate([_unheads(dz), dtail, jnp.zeros((CH, 128), F32)], axis=1)
        dal_ref[...] += dal
        ddt_ref[...] += ddt
        dnw_ref[...] += dnw

    rev = lambda n: NCH - 1 - n
    row = pl.BlockSpec((1, 128), lambda n: (0, 0))
    return pl.pallas_call(
        body, name="delta_bwd", grid=(NCH,),
        in_specs=[pl.BlockSpec((CH, 1536), lambda n: (rev(n), 0)),
                  pl.BlockSpec((CH, 512), lambda n: (rev(n), DN_Z_COL // 512)),
                  pl.BlockSpec((CH, 128), lambda n: (rev(n), DN_TAIL_BLK)), row, row, row,
                  pl.BlockSpec((1, NDH, 128, 128), lambda n: (rev(n), 0, 0, 0)),
                  pl.BlockSpec((CH, 512), lambda n: (rev(n), 1)), pl.BlockSpec(memory_space=pl.ANY)],
        out_specs=[pl.BlockSpec((CH, 768), lambda n: (rev(n), DN_Z_COL // 768)),
                   pl.BlockSpec((CH, 1536), lambda n: (rev(n), 0)), row, row, row],
        out_shape=[jax.ShapeDtypeStruct((SEQ, IN_PAD), F32), jax.ShapeDtypeStruct((SEQ, 1536), F32)]
        + [jax.ShapeDtypeStruct((1, 128), F32)] * 3,
        scratch_shapes=[pltpu.VMEM((NDH, 128, 128), F32)],
        input_output_aliases={8: 0},
        compiler_params=_cp("arbitrary"),
    )(c_qkv, proj, proj, alog_row, dt_row, nw, states, dcat, dproj)


def _place():
    x, y, c = lax.axis_index("x"), lax.axis_index("y"), lax.axis_index("c")
    other_chips = [(1 - x, y), (x, 1 - y), (1 - x, 1 - y)]
    return x, y, c, other_chips


HBM_SPEC = pl.BlockSpec(memory_space=pltpu.HBM)


def _all_gather_hbm(shards, name):
    n = len(shards)

    def body(*refs):
        ins, outs = refs[:n], refs[n:2 * n]
        send_sems, recv_sems, local_sems = refs[2 * n:]
        x, y, c, chips = _place()
        me, sibling = (x, y, c), (x, y, 1 - c)

        def copy(b, k, block, to, src=None):
            slot = outs[b].at[4 * block[0] + 2 * block[1] + block[2]]
            return pltpu.make_async_remote_copy(
                src_ref=slot if src is None else src, dst_ref=slot,
                send_sem=send_sems.at[b, k], recv_sem=recv_sems.at[b, k], device_id=to, device_id_type=MESH)

        mine = [pltpu.make_async_copy(ins[b], outs[b].at[4 * x + 2 * y + c], local_sems.at[b]) for b in range(n)]
        for cp in mine:
            cp.start()
        first = []
        for b in range(n):
            first.append(copy(b, 0, me, sibling, src=ins[b]))
            first += [copy(b, 1 + j, me, (*chip, c), src=ins[b]) for j, chip in enumerate(chips)]
        for cp in first:
            cp.start()
        passed = []
        for b in range(n):
            for j, chip in enumerate(chips):
                copy(b, 1 + j, (*chip, c), me).wait_recv()
                fwd = copy(b, 4 + j, (*chip, c), sibling)
                fwd.start()
                passed.append(fwd)
        for b in range(n):
            copy(b, 0, sibling, me).wait_recv()
            for j, chip in enumerate(chips):
                copy(b, 4 + j, (*chip, 1 - c), me).wait_recv()
        for cp in first + passed:
            cp.wait_send()
        for cp in mine:
            cp.wait()

    return pl.pallas_call(
        body, name=name,
        in_specs=[HBM_SPEC] * n, out_specs=[HBM_SPEC] * n,
        out_shape=[jax.ShapeDtypeStruct((N_DEV,) + s.shape, s.dtype) for s in shards],
        scratch_shapes=[pltpu.SemaphoreType.DMA((n, 7)), pltpu.SemaphoreType.DMA((n, 7)), pltpu.SemaphoreType.DMA((n,))],
    )(*shards)


def _exchange_sibling(gs, name):
    n = len(gs)

    def body(*refs):
        ins, outs = refs[:n], refs[n:2 * n]
        send_sems, recv_sems = refs[2 * n:]
        x, y, c, _ = _place()
        copies = []
        for b in range(n):
            for p in range(4):
                copies.append(pltpu.make_async_remote_copy(
                    src_ref=ins[b].at[2 * p + (1 - c)], dst_ref=outs[b].at[p],
                    send_sem=send_sems.at[b, p], recv_sem=recv_sems.at[b, p],
                    device_id=(x, y, 1 - c), device_id_type=MESH))
        for cp in copies:
            cp.start()
        for cp in copies:
            cp.wait()

    return pl.pallas_call(
        body, name=name,
        in_specs=[HBM_SPEC] * n, out_specs=[HBM_SPEC] * n,
        out_shape=[jax.ShapeDtypeStruct((4,) + g.shape[1:], g.dtype) for g in gs],
        scratch_shapes=[pltpu.SemaphoreType.DMA((n, 4)), pltpu.SemaphoreType.DMA((n, 4))],
    )(*gs)


def _exchange_chips(hs, name):
    n, nl = len(hs), len(hs[0])

    def body(*refs):
        ins = [refs[b * nl:(b + 1) * nl] for b in range(n)]
        outs = refs[n * nl:n * nl + n]
        send_sems, recv_sems, local_sems = refs[n * nl + n:]
        x, y, c, chips = _place()
        my_chip = 2 * x + y
        local, sends, waits = [], [], []
        for b in range(n):
            for l in range(nl):
                s = b * nl + l
                local.append(pltpu.make_async_copy(ins[b][l].at[my_chip], outs[b].at[my_chip, l], local_sems.at[s]))
                for k, (px, py) in enumerate(chips):
                    peer = 2 * px + py
                    sends.append(pltpu.make_async_remote_copy(
                        src_ref=ins[b][l].at[peer], dst_ref=outs[b].at[my_chip, l],
                        send_sem=send_sems.at[s, k], recv_sem=recv_sems.at[s, k],
                        device_id=(px, py, c), device_id_type=MESH))
                    waits.append(pltpu.make_async_remote_copy(
                        src_ref=ins[b][l].at[peer], dst_ref=outs[b].at[peer, l],
                        send_sem=send_sems.at[s, k], recv_sem=recv_sems.at[s, k],
                        device_id=(px, py, c), device_id_type=MESH))
        for cp in local + sends:
            cp.start()
        for cp in waits:
            cp.wait_recv()
        for cp in sends:
            cp.wait_send()
        for cp in local:
            cp.wait()

    flat = [h for hb in hs for h in hb]
    return pl.pallas_call(
        body, name=name,
        in_specs=[HBM_SPEC] * (n * nl), out_specs=[HBM_SPEC] * n,
        out_shape=[jax.ShapeDtypeStruct((4, nl) + hb[0].shape[1:], hb[0].dtype) for hb in hs],
        scratch_shapes=[pltpu.SemaphoreType.DMA((n * nl, 3)), pltpu.SemaphoreType.DMA((n * nl, 3)),
                        pltpu.SemaphoreType.DMA((n * nl,))],
    )(*flat)


def _pair_add(g, r, core, name):
    _, nr, nc = g.shape
    tr = nr // 2 if nr % 32 == 0 else nr

    def body(core_ref, g_ref, r_ref, o_ref):
        o_ref[...] = (g_ref[...] + r_ref[...]).astype(BF16)

    return pl.pallas_call(
        body, name=name,
        grid_spec=pltpu.PrefetchScalarGridSpec(
            num_scalar_prefetch=1, grid=(4, nr // tr),
            in_specs=[pl.BlockSpec((1, tr, nc), lambda p, i, core: (2 * p + core[0], i, 0)),
                      pl.BlockSpec((1, tr, nc), lambda p, i, core: (p, i, 0))],
            out_specs=pl.BlockSpec((1, tr, nc), lambda p, i, core: (p, i, 0))),
        out_shape=jax.ShapeDtypeStruct(r.shape, BF16),
        compiler_params=_cp("parallel", "parallel"),
    )(core, g, r)


def _all_gather_sum_small(v):
    rows = v.shape[0]

    def body(x_ref, sum_ref, out_ref, send_sems, recv_sems, local_sem):
        x, y, c, chips = _place()
        me, sibling = (x, y, c), (x, y, 1 - c)

        def block(px, py, pc):
            return out_ref.at[pl.ds((4 * px + 2 * py + pc) * rows, rows), :]

        def copy(k, blk, to, src=None):
            return pltpu.make_async_remote_copy(
                src_ref=block(*blk) if src is None else src, dst_ref=block(*blk),
                send_sem=send_sems.at[k], recv_sem=recv_sems.at[k], device_id=to, device_id_type=MESH)

        mine = pltpu.make_async_copy(x_ref, block(*me), local_sem)
        mine.start()
        first = [copy(0, me, sibling, src=x_ref)]
        first += [copy(1 + j, me, (*chip, c), src=x_ref) for j, chip in enumerate(chips)]
        for cp in first:
            cp.start()
        passed = [copy(4 + j, (*chip, c), sibling) for j, chip in enumerate(chips)]
        for j, chip in enumerate(chips):
            copy(1 + j, (*chip, c), me).wait_recv()
            passed[j].start()
        copy(0, sibling, me).wait_recv()
        for j, chip in enumerate(chips):
            copy(4 + j, (*chip, 1 - c), me).wait_recv()
        for cp in first + passed:
            cp.wait_send()
        mine.wait()
        total = out_ref[pl.ds(0, rows), :]
        for d in range(1, N_DEV):
            total = total + out_ref[pl.ds(d * rows, rows), :]
        sum_ref[...] = total

    vm = pl.BlockSpec(memory_space=pltpu.VMEM)
    return pl.pallas_call(
        body, name="small_all_reduce",
        in_specs=[vm], out_specs=[vm],
        out_shape=[jax.ShapeDtypeStruct((rows, 128), F32)],
        scratch_shapes=[pltpu.VMEM((N_DEV * rows, 128), F32), pltpu.SemaphoreType.DMA((7,)),
                        pltpu.SemaphoreType.DMA((7,)), pltpu.SemaphoreType.DMA],
    )(v)[0]


def _adamw(w, g, m, v):
    m = ADAM_B1 * m + (1.0 - ADAM_B1) * g
    v = ADAM_B2 * v + (1.0 - ADAM_B2) * (g * g)
    m_hat = m / (1.0 - ADAM_B1 ** ADAM_STEP)
    v_hat = v / (1.0 - ADAM_B2 ** ADAM_STEP)
    delta = -ADAM_LR * (m_hat / (jnp.sqrt(v_hat) + ADAM_EPS) + ADAM_WD * w)
    return delta, m, v


ADAM_ROWS = dict(w_in=256, w_out=128, ffn_w_in=256, ffn_w_out=176)


def _sum_chips(p):
    p = p.astype(F32)
    return (p[0] + p[1]) + (p[2] + p[3])


def _sum_parts(parts, name):
    _, nl, nr, nc = parts.shape

    def body(p_ref, g_ref):
        g_ref[...] = _sum_chips(p_ref[...])

    return pl.pallas_call(
        body, name=name, grid=(nl,),
        in_specs=[pl.BlockSpec((4, 1, nr, nc), lambda l: (0, l, 0, 0))],
        out_specs=pl.BlockSpec((1, nr, nc), lambda l: (l, 0, 0)),
        out_shape=jax.ShapeDtypeStruct((nl, nr, nc), F32),
        compiler_params=_cp("parallel"),
    )(parts)


def _adamw_sharded(parts, w, m, v, tr, name):
    nl, nr, nc = w.shape
    n_parts = parts.shape[0]

    def body(p_ref, w_ref, m_ref, v_ref, g_ref, d_ref, nm_ref, nv_ref):
        g = _sum_chips(p_ref[...]) if n_parts == 4 else p_ref[0]
        delta, nm, nv = _adamw(w_ref[...], g, m_ref[...], v_ref[...])
        g_ref[...] = g
        d_ref[...] = delta
        nm_ref[...] = nm
        nv_ref[...] = nv

    blk = pl.BlockSpec((1, tr, nc), lambda l, i: (l, i, 0))
    return pl.pallas_call(
        body, name=name, grid=(nl, nr // tr),
        in_specs=[pl.BlockSpec((n_parts, 1, tr, nc), lambda l, i: (0, l, i, 0)), blk, blk, blk],
        out_specs=[blk] * 4,
        out_shape=[jax.ShapeDtypeStruct(w.shape, F32)] * 4,
        compiler_params=_cp("parallel", "parallel"),
    )(parts, w, m, v)


def _adamw_small(g, w, m, v):
    def body(g_ref, w_ref, m_ref, v_ref, d_ref, nm_ref, nv_ref):
        delta, nm, nv = _adamw(w_ref[...], g_ref[...], m_ref[...], v_ref[...])
        d_ref[...] = delta
        nm_ref[...] = nm
        nv_ref[...] = nv

    return pl.pallas_call(
        body, name="adamw_small",
        out_shape=[jax.ShapeDtypeStruct(g.shape, F32)] * 3,
    )(g, w, m, v)


def _pack(arrays, rows):
    flat = jnp.concatenate([a.reshape(-1).astype(F32) for a in arrays])
    return jnp.pad(flat, (0, rows * 128 - flat.shape[0])).reshape(rows, 128)


def _unpack(packed, shapes):
    flat = packed.reshape(-1)
    out, off = [], 0
    for s in shapes:
        n = math.prod(s)
        out.append(flat[off:off + n].reshape(s))
        off += n
    return out


def _row(v, width=None):
    v = v.reshape(1, -1)
    return v if width is None else jnp.pad(v, ((0, 0), (0, width - v.shape[1])))


def _layer_fwd(x, wts, tables):
    h = _norm_fwd(x, wts["norm_pre_mix"], "norm_pre_mix")
    proj = _matmul(h, wts["w_in"], tb=True, tm=512, tn=768, tk=1024, name="mm_proj")
    cat, lse = _attn_fwd(proj, *tables)
    c_qkv = _dnconv_fwd(proj, wts["dn_conv_w"])
    cat, states = _delta_fwd(c_qkv, proj, wts["dn_a_log"], wts["dn_dt_bias"], wts["dn_norm_w"], cat)
    mix = _matmul(cat, wts["w_out"], tm=512, tn=1024, tk=1024, name="mm_mix")
    x1 = _resnorm_fwd(x, mix, wts["norm_post_mix"], "norm_post_mix")
    h2 = _norm_fwd(x1, wts["norm_pre_ffn"], "norm_pre_ffn")
    pre = _matmul(h2, wts["ffn_w_in"], tb=True, tm=512, tn=512, tk=1024, name="mm_ffn_in")
    act = _ffact_fwd(pre, wts["ffn_conv_w"], wts["ffn_conv_b"])
    f = _matmul(act, wts["ffn_w_out"], tm=512, tn=1024, tk=D_FF, name="mm_ffn_out")
    x2 = _resnorm_fwd(x1, f, wts["norm_post_ffn"], "norm_post_ffn")
    saved = dict(x=x, h=h, proj=proj, lse=lse, c_qkv=c_qkv, states=states, cat=cat, mix=mix, x1=x1, h2=h2, pre=pre,
                 act=act, f=f)
    return x2, saved


def _layer_bwd(dx2, wts, s, tables):
    g = {}
    df, g["norm_post_ffn"] = _norm_bwd(s["f"], wts["norm_post_ffn"], dx2, None, "norm_post_ffn_bwd")
    dact = _matmul(df, wts["ffn_w_out"], tb=True, tm=512, tn=1408, tk=1024, name="mm_dact", out_dtype=BF16)
    g["ffn_w_out"] = _matmul(s["act"], df, ta=True, tm=1408, tn=512, tk=SEQ, name="mm_dw_ffn_out")
    dpre, g["ffn_conv_w"], g["ffn_conv_b"] = _ffact_bwd(s["pre"], wts["ffn_conv_w"], wts["ffn_conv_b"], dact)
    dh2 = _matmul(dpre, wts["ffn_w_in"], tm=1024, tn=1024, tk=1408, name="mm_dh2")
    g["ffn_w_in"] = _matmul(dpre, s["h2"], ta=True, tm=512, tn=1024, tk=SEQ, name="mm_dw_ffn_in")
    dx1, g["norm_pre_ffn"] = _norm_bwd(s["x1"], wts["norm_pre_ffn"], dh2, dx2, "norm_pre_ffn_bwd")
    dmix, g["norm_post_mix"] = _norm_bwd(s["mix"], wts["norm_post_mix"], dx1, None, "norm_post_mix_bwd")
    dcat = _matmul(dmix, wts["w_out"], tb=True, tm=512, tn=1024, tk=1024, name="mm_dcat")
    g["w_out"] = _matmul(s["cat"], dmix, ta=True, tm=1024, tn=512, tk=SEQ, name="mm_dw_out")
    dproj = _attn_bwd(s["proj"], *tables, s["cat"], s["lse"], dcat)
    dproj, dc, g["dn_a_log"], g["dn_dt_bias"], g["dn_norm_w"] = _delta_bwd(
        s["c_qkv"], s["proj"], wts["dn_a_log"], wts["dn_dt_bias"], wts["dn_norm_w"], s["states"], dcat, dproj)
    dproj, g["dn_conv_w"] = _dnconv_bwd(s["proj"], wts["dn_conv_w"], dc, dproj)
    dh = _matmul(dproj, wts["w_in"], tm=1024, tn=1024, tk=1280, name="mm_dh")
    g["w_in"] = _matmul(dproj, s["h"], ta=True, tm=768, tn=1024, tk=SEQ, name="mm_dw_in")
    dx, g["norm_pre_mix"] = _norm_bwd(s["x"], wts["norm_pre_mix"], dh, dx1, "norm_pre_mix_bwd")
    return dx, g


BIG = ("w_in", "w_out", "ffn_w_in", "ffn_w_out")
COLUMN_SHARDED = ("w_in", "ffn_w_in")
SMALL_SHARDED = ("dn_conv_w", "ffn_conv_w")
REPLICATED = ("dn_a_log", "dn_dt_bias", "dn_norm_w", "ffn_conv_b", "norm_pre_mix", "norm_post_mix", "norm_pre_ffn",
              "norm_post_ffn")
WEIGHTS = ("w_in", "dn_conv_w", "dn_a_log", "dn_dt_bias", "dn_norm_w", "w_out", "ffn_w_in", "ffn_conv_w", "ffn_conv_b",
           "ffn_w_out", "norm_pre_mix", "norm_post_mix", "norm_pre_ffn", "norm_post_ffn")
FULL_SHAPE = dict(dn_conv_w=(DEPTH, 4, 1536), ffn_conv_w=(DEPTH, 3, 2 * D_FF), dn_a_log=(DEPTH, NDH),
                  dn_dt_bias=(DEPTH, NDH), dn_norm_w=(DEPTH, 128), ffn_conv_b=(DEPTH, 2 * D_FF),
                  norm_pre_mix=(DEPTH, D_MODEL), norm_post_mix=(DEPTH, D_MODEL), norm_pre_ffn=(DEPTH, D_MODEL),
                  norm_post_ffn=(DEPTH, D_MODEL))
SMALL_GRAD_ORDER = REPLICATED + SMALL_SHARDED
SMALL_GRAD_ROWS = 520
SMALL_W_ROWS = 48
SMALL_ADAM_ROWS = 200


def _w_in_rows_to_kernel_order(t):
    qkv = t[:QKV_W].reshape(3, N_PAIR, 128, -1).swapaxes(0, 1).reshape(QKV_W, -1)
    return jnp.pad(jnp.concatenate([qkv, t[QKV_W:]], axis=0), ((0, IN_PAD - IN_COLS), (0, 0)))


def _w_in_rows_from_kernel_order(t):
    qkv = t[:QKV_W].reshape(N_PAIR, 3, 128, -1).swapaxes(0, 1).reshape(QKV_W, -1)
    return jnp.concatenate([qkv, t[QKV_W:IN_COLS]], axis=0)


def _interleave_ff_rows(t):
    return t.reshape(2, FF_BLKS, 128, -1).swapaxes(0, 1).reshape(2 * D_FF, -1)


def _deinterleave_ff_rows(t):
    return t.reshape(FF_BLKS, 2, 128, -1).swapaxes(0, 1).reshape(2 * D_FF, -1)


def kernel(x, w_in, dn_conv_w, dn_a_log, dn_dt_bias, dn_norm_w, w_out, ffn_w_in, ffn_conv_w, ffn_conv_b, ffn_w_out, norm_pre_mix, norm_post_mix, norm_pre_ffn, norm_post_ffn, loss_target, m_w_in, m_dn_conv_w, m_dn_a_log, m_dn_dt_bias, m_dn_norm_w, m_w_out, m_ffn_w_in, m_ffn_conv_w, m_ffn_conv_b, m_ffn_w_out, m_norm_pre_mix, m_norm_post_mix, m_norm_pre_ffn, m_norm_post_ffn, v_w_in, v_dn_conv_w, v_dn_a_log, v_dn_dt_bias, v_dn_norm_w, v_w_out, v_ffn_w_in, v_ffn_conv_w, v_ffn_conv_b, v_ffn_w_out, v_norm_pre_mix, v_norm_post_mix, v_norm_pre_ffn, v_norm_post_ffn):
    local = dict(w_in=w_in, dn_conv_w=dn_conv_w, dn_a_log=dn_a_log, dn_dt_bias=dn_dt_bias, dn_norm_w=dn_norm_w,
                 w_out=w_out, ffn_w_in=ffn_w_in, ffn_conv_w=ffn_conv_w, ffn_conv_b=ffn_conv_b, ffn_w_out=ffn_w_out,
                 norm_pre_mix=norm_pre_mix, norm_post_mix=norm_post_mix, norm_pre_ffn=norm_pre_ffn,
                 norm_post_ffn=norm_post_ffn)
    mom_m = dict(w_in=m_w_in, dn_conv_w=m_dn_conv_w, dn_a_log=m_dn_a_log, dn_dt_bias=m_dn_dt_bias,
                 dn_norm_w=m_dn_norm_w, w_out=m_w_out, ffn_w_in=m_ffn_w_in, ffn_conv_w=m_ffn_conv_w,
                 ffn_conv_b=m_ffn_conv_b, ffn_w_out=m_ffn_w_out, norm_pre_mix=m_norm_pre_mix,
                 norm_post_mix=m_norm_post_mix, norm_pre_ffn=m_norm_pre_ffn, norm_post_ffn=m_norm_post_ffn)
    mom_v = dict(w_in=v_w_in, dn_conv_w=v_dn_conv_w, dn_a_log=v_dn_a_log, dn_dt_bias=v_dn_dt_bias,
                 dn_norm_w=v_dn_norm_w, w_out=v_w_out, ffn_w_in=v_ffn_w_in, ffn_conv_w=v_ffn_conv_w,
                 ffn_conv_b=v_ffn_conv_b, ffn_w_out=v_ffn_w_out, norm_pre_mix=v_norm_pre_mix,
                 norm_post_mix=v_norm_post_mix, norm_pre_ffn=v_norm_pre_ffn, norm_post_ffn=v_norm_post_ffn)
    dev = 4 * lax.axis_index("x") + 2 * lax.axis_index("y") + lax.axis_index("c")
    core = lax.axis_index("c").astype(jnp.int32).reshape(1)

    small_w = _pack([dn_conv_w, ffn_conv_w], SMALL_W_ROWS)
    shards = [local[n][l].astype(BF16).T if n in COLUMN_SHARDED else local[n][l].astype(BF16)
              for n in BIG for l in range(DEPTH)]
    gathered = _all_gather_hbm(shards + [small_w], "weights_all_gather")
    g_small = gathered[-1]
    gathered = {n: gathered[i * DEPTH:(i + 1) * DEPTH] for i, n in enumerate(BIG)}
    n_dn, n_ff = DEPTH * 4 * 192, DEPTH * 3 * 704
    sm = g_small.reshape(N_DEV, -1)
    full_dn_conv = sm[:, :n_dn].reshape(N_DEV, DEPTH, 4, 192).transpose(1, 2, 0, 3).reshape(DEPTH, 4, 1536)
    full_ff_conv = _interleave_ff(
        sm[:, n_dn:n_dn + n_ff].reshape(N_DEV, DEPTH, 3, 704).transpose(1, 2, 0, 3).reshape(DEPTH, 3, 2 * D_FF))

    def layer_weights(l):
        wts = dict(
            w_in=_w_in_rows_to_kernel_order(gathered["w_in"][l].reshape(IN_COLS, D_MODEL)),
            w_out=gathered["w_out"][l].reshape(D_MODEL, D_MODEL),
            ffn_w_in=_interleave_ff_rows(gathered["ffn_w_in"][l].reshape(2 * D_FF, D_MODEL)),
            ffn_w_out=gathered["ffn_w_out"][l].reshape(D_FF, D_MODEL),
            dn_conv_w=full_dn_conv[l], ffn_conv_w=full_ff_conv[l],
            ffn_conv_b=_interleave_ff(_row(ffn_conv_b[l])),
            dn_a_log=_row(dn_a_log[l], 128), dn_dt_bias=_row(dn_dt_bias[l], 128))
        for n in ("dn_norm_w", "norm_pre_mix", "norm_post_mix", "norm_pre_ffn", "norm_post_ffn"):
            wts[n] = _row(local[n][l])
        return wts

    tables = _rope_tables()
    weights = [layer_weights(l) for l in range(DEPTH)]
    act, saved = x[0], []
    for l in range(DEPTH):
        act, s = _layer_fwd(act, weights[l], tables)
        saved.append(s)
    loss_part, dact = _loss_fwd_bwd(act, loss_target[0])
    grads = [None] * DEPTH
    for l in reversed(range(DEPTH)):
        dact, grads[l] = _layer_bwd(dact, weights[l], saved[l], tables)
    grad_x = dact[None]

    def to_devices(name, l):
        t = grads[l][name]
        if name == "w_in":
            t = _w_in_rows_from_kernel_order(t)
        if name == "ffn_w_in":
            t = _deinterleave_ff_rows(t)
        return t.reshape(N_DEV, t.shape[0] // N_DEV, t.shape[1])

    to_dev = [to_devices(n, l) for n in BIG for l in range(DEPTH)]
    from_sibling = _exchange_sibling(to_dev, "grads_to_sibling")
    chip_sums = [_pair_add(gd, r, core, "grads_pair_add_%d" % i) for i, (gd, r) in enumerate(zip(to_dev, from_sibling))]
    parts = _exchange_chips([chip_sums[i * DEPTH:(i + 1) * DEPTH] for i in range(len(BIG))], "grads_to_chips")

    def small_grad(name):
        t = jnp.stack([grads[l][name] for l in range(DEPTH)])
        if name in ("dn_a_log", "dn_dt_bias"):
            t = t[:, 0, :NDH]
        if name in ("ffn_conv_w", "ffn_conv_b"):
            t = _deinterleave_ff(t)
        return t.reshape(FULL_SHAPE[name])

    small_part = _pack([small_grad(n) for n in SMALL_GRAD_ORDER] + [loss_part[0, :1]], SMALL_GRAD_ROWS)
    small_sum = _all_gather_sum_small(small_part)
    small_g = dict(zip(SMALL_GRAD_ORDER + ("loss",), _unpack(small_sum, [FULL_SHAPE[n] for n in SMALL_GRAD_ORDER] + [(1,)])))
    loss = small_g["loss"][0]
    small_g["dn_conv_w"] = lax.dynamic_slice_in_dim(small_g["dn_conv_w"], dev * 192, 192, axis=2)
    small_g["ffn_conv_w"] = lax.dynamic_slice_in_dim(small_g["ffn_conv_w"], dev * 704, 704, axis=2)

    out_g, out_d, out_m, out_v = {}, {}, {}, {}
    for n, p in zip(BIG, parts):
        if n in COLUMN_SHARDED:
            p = _sum_parts(p, "grad_sum_" + n).transpose(0, 2, 1)[None]
        out_g[n], out_d[n], out_m[n], out_v[n] = _adamw_sharded(p, local[n], mom_m[n], mom_v[n], ADAM_ROWS[n], "adamw_" + n)
    shapes = [small_g[n].shape for n in SMALL_GRAD_ORDER]
    d_s, m_s, v_s = _adamw_small(_pack([small_g[n] for n in SMALL_GRAD_ORDER], SMALL_ADAM_ROWS),
                                 _pack([local[n] for n in SMALL_GRAD_ORDER], SMALL_ADAM_ROWS),
                                 _pack([mom_m[n] for n in SMALL_GRAD_ORDER], SMALL_ADAM_ROWS),
                                 _pack([mom_v[n] for n in SMALL_GRAD_ORDER], SMALL_ADAM_ROWS))
    for n, d, m, v in zip(SMALL_GRAD_ORDER, _unpack(d_s, shapes), _unpack(m_s, shapes), _unpack(v_s, shapes)):
        out_g[n], out_d[n], out_m[n], out_v[n] = small_g[n], d, m, v
    return (loss, grad_x, *[out_g[n] for n in WEIGHTS], *[out_d[n] for n in WEIGHTS],
            *[out_m[n] for n in WEIGHTS], *[out_v[n] for n in WEIGHTS])
```

```python
import functools
import math

import jax
import jax.numpy as jnp
from jax import lax
from jax.experimental import pallas as pl
from jax.experimental.pallas import tpu as pltpu

F32 = jnp.float32
BF16 = jnp.bfloat16
HI = lax.Precision.HIGHEST
MESH = pl.DeviceIdType.MESH

N_DEV = 8
SEQ = 2048
D_MODEL = 1024
DEPTH = 2
N_PAIR = 4
HEAD_DIM = 64
ATTN_W = 512
ATTN_BLK = 128
DILATIONS = (1, 4, 16)
SEGMENT_BLOCKS = (16, 4, 1)
N_BLK = SEQ // ATTN_BLK
NDH = 4
CH = 64
NCH = SEQ // CH
IN_COLS = 3592
IN_PAD = 3840
QKV_W = 3 * ATTN_W
DN_QKV_BLK0 = QKV_W // 128
DN_QKV_BLKS = 1536 // 128
DN_Z_COL = 3072
DN_TAIL_BLK = 3584 // 128
D_FF = 2816
FF_BLKS = D_FF // 128
EPS = 1e-6
NEG = -1e30
ROPE_THETA = 10000.0

ADAM_LR, ADAM_B1, ADAM_B2, ADAM_EPS, ADAM_WD, ADAM_STEP = 0.001, 0.9, 0.999, 1e-08, 0.01, 10

VMEM_LIMIT = 56 * 1024 * 1024


def _cp(*sem):
    return pltpu.CompilerParams(dimension_semantics=sem, vmem_limit_bytes=VMEM_LIMIT)


class Exchange:
    def __init__(self, operands, out_shapes, sems, start, middle, finish):
        self.operands, self.out_shapes, self.sems = list(operands), list(out_shapes), list(sems)
        self.start, self.middle, self.finish = start, middle, finish


HBM_SPEC = pl.BlockSpec(memory_space=pltpu.HBM)


def _hosted_call(body, *, name, steps, in_specs, out_specs, out_shape, scratch_shapes, operands, exchanges=(),
                 aliases=None):
    n_in, n_out, n_scr = len(in_specs), len(out_specs), len(scratch_shapes)

    def take(refs, pos, counts):
        groups = []
        for c in counts:
            groups.append(refs[pos:pos + c])
            pos += c
        return groups, pos

    def full_body(*refs):
        ins, pos = refs[:n_in], n_in
        ex_ins, pos = take(refs, pos, [len(e.operands) for e in exchanges])
        outs, pos = refs[pos:pos + n_out], pos + n_out
        ex_outs, pos = take(refs, pos, [len(e.out_shapes) for e in exchanges])
        scr, pos = refs[pos:pos + n_scr], pos + n_scr
        ex_sems, pos = take(refs, pos, [len(e.sems) for e in exchanges])
        step = pl.program_id(0)
        for e, a, b, s in zip(exchanges, ex_ins, ex_outs, ex_sems):
            pl.when(step == 0)(functools.partial(e.start, a, b, s))
            if e.middle is not None:
                pl.when(step == steps // 2)(functools.partial(e.middle, a, b, s))
        body(*ins, *outs, *scr)
        for e, a, b, s in zip(exchanges, ex_ins, ex_outs, ex_sems):
            pl.when(step == steps - 1)(functools.partial(e.finish, a, b, s))

    n_ex_in = sum(len(e.operands) for e in exchanges)
    n_ex_out = sum(len(e.out_shapes) for e in exchanges)
    results = pl.pallas_call(
        full_body, name=name, grid=(steps,),
        in_specs=list(in_specs) + [HBM_SPEC] * n_ex_in,
        out_specs=list(out_specs) + [HBM_SPEC] * n_ex_out,
        out_shape=list(out_shape) + [s for e in exchanges for s in e.out_shapes],
        scratch_shapes=list(scratch_shapes) + [s for e in exchanges for s in e.sems],
        input_output_aliases=aliases or {},
        compiler_params=_cp("arbitrary"),
    )(*operands, *[a for e in exchanges for a in e.operands])
    ex_results, _ = take(results, n_out, [len(e.out_shapes) for e in exchanges])
    return results[:n_out], ex_results


def _dot(a, b, dims, precision=None):
    if precision is None:
        a = a.astype(BF16)
        b = b.astype(BF16)
    return lax.dot_general(a, b, (dims, ((), ())), preferred_element_type=F32, precision=precision)


def _make_mm(precision):
    @jax.custom_vjp
    def nn(a, b):
        return _dot(a, b, ((1,), (0,)), precision)

    @jax.custom_vjp
    def nt(a, b):
        return _dot(a, b, ((1,), (1,)), precision)

    @jax.custom_vjp
    def tn(a, b):
        return _dot(a, b, ((0,), (0,)), precision)

    nn.defvjp(lambda a, b: (nn(a, b), (a, b)), lambda r, g: (nt(g, r[1]), tn(r[0], g)))
    nt.defvjp(lambda a, b: (nt(a, b), (a, b)), lambda r, g: (nn(g, r[1]), tn(g, r[0])))
    tn.defvjp(lambda a, b: (tn(a, b), (a, b)), lambda r, g: (nt(r[1], g), nn(r[0], g)))
    return nn, nt, tn


MM, MM_NT, MM_TN = _make_mm(None)
MMH, _, _ = _make_mm(HI)


def _matmul(a, b, *, ta=False, tb=False, tm, tn, tk, name, out_dtype=F32):
    (k_dim, m_dim) = a.shape if ta else a.shape[::-1]
    (n_dim, k2) = b.shape if tb else b.shape[::-1]
    assert k_dim == k2 and m_dim % tm == 0 and n_dim % tn == 0 and k_dim % tk == 0, (a.shape, b.shape, tm, tn, tk)
    nk = k_dim // tk
    dims = ((0 if ta else 1,), (1 if tb else 0,))

    def body(a_ref, b_ref, o_ref, *acc):
        p = _dot(a_ref[...], b_ref[...], dims)
        if nk == 1:
            o_ref[...] = p.astype(out_dtype)
            return
        acc_ref, k = acc[0], pl.program_id(2)

        @pl.when(k == 0)
        def _():
            acc_ref[...] = p

        @pl.when(k > 0)
        def _():
            acc_ref[...] += p

        @pl.when(k == nk - 1)
        def _():
            o_ref[...] = acc_ref[...].astype(out_dtype)

    a_spec = pl.BlockSpec((tk, tm), lambda i, j, k: (k, i)) if ta else pl.BlockSpec((tm, tk), lambda i, j, k: (i, k))
    b_spec = pl.BlockSpec((tn, tk), lambda i, j, k: (j, k)) if tb else pl.BlockSpec((tk, tn), lambda i, j, k: (k, j))
    return pl.pallas_call(
        body, name=name,
        grid=(m_dim // tm, n_dim // tn, nk),
        in_specs=[a_spec, b_spec],
        out_specs=pl.BlockSpec((tm, tn), lambda i, j, k: (i, j)),
        out_shape=jax.ShapeDtypeStruct((m_dim, n_dim), out_dtype),
        scratch_shapes=[pltpu.VMEM((tm, tn), F32)] if nk > 1 else [],
        compiler_params=_cp("parallel", "parallel", "arbitrary"),
    )(a, b)


NORM_ROWS = 256


def _rms(x, w):
    return x * lax.rsqrt(jnp.mean(x * x, axis=1, keepdims=True) + EPS) * w


def _norm_fwd(x, w_row, name, out_dtype=BF16):
    def body(x_ref, w_ref, o_ref):
        o_ref[...] = _rms(x_ref[...], w_ref[...]).astype(out_dtype)

    return pl.pallas_call(
        body, name=name, grid=(SEQ // NORM_ROWS,),
        in_specs=[pl.BlockSpec((NORM_ROWS, D_MODEL), lambda i: (i, 0)), pl.BlockSpec((1, D_MODEL), lambda i: (0, 0))],
        out_specs=pl.BlockSpec((NORM_ROWS, D_MODEL), lambda i: (i, 0)),
        out_shape=jax.ShapeDtypeStruct((SEQ, D_MODEL), out_dtype),
        compiler_params=_cp("parallel"),
    )(x, w_row)


def _resnorm_fwd(x, f, w_row, name):
    def body(x_ref, f_ref, w_ref, o_ref):
        o_ref[...] = x_ref[...] + _rms(f_ref[...], w_ref[...])

    blk = pl.BlockSpec((NORM_ROWS, D_MODEL), lambda i: (i, 0))
    return pl.pallas_call(
        body, name=name, grid=(SEQ // NORM_ROWS,),
        in_specs=[blk, blk, pl.BlockSpec((1, D_MODEL), lambda i: (0, 0))],
        out_specs=blk, out_shape=jax.ShapeDtypeStruct((SEQ, D_MODEL), F32),
        compiler_params=_cp("parallel"),
    )(x, f, w_row)


def _norm_bwd(x, w_row, dy, add, name):
    has_add = add is not None

    def body(*refs):
        if has_add:
            x_ref, w_ref, dy_ref, add_ref, dx_ref, dw_ref = refs
        else:
            x_ref, w_ref, dy_ref, dx_ref, dw_ref = refs
        _, vjp = jax.vjp(_rms, x_ref[...], w_ref[...])
        dx, dw = vjp(dy_ref[...])
        dx_ref[...] = dx + add_ref[...] if has_add else dx

        @pl.when(pl.program_id(0) == 0)
        def _():
            dw_ref[...] = jnp.zeros_like(dw_ref)

        dw_ref[...] += dw

    blk = pl.BlockSpec((NORM_ROWS, D_MODEL), lambda i: (i, 0))
    row = pl.BlockSpec((1, D_MODEL), lambda i: (0, 0))
    ins = [x, w_row, dy] + ([add] if has_add else [])
    return pl.pallas_call(
        body, name=name, grid=(SEQ // NORM_ROWS,),
        in_specs=[blk, row, blk] + ([blk] if has_add else []),
        out_specs=[blk, row],
        out_shape=[jax.ShapeDtypeStruct((SEQ, D_MODEL), F32), jax.ShapeDtypeStruct((1, D_MODEL), F32)],
        compiler_params=_cp("arbitrary"),
    )(*ins)


def _loss_fwd_bwd(y, target):
    def body(y_ref, t_ref, loss_ref, dy_ref):
        err = y_ref[...] - t_ref[...]
        dy_ref[...] = err * (1.0 / D_MODEL)

        @pl.when(pl.program_id(0) == 0)
        def _():
            loss_ref[...] = jnp.zeros_like(loss_ref)

        part = jnp.sum(jnp.sum(err * err, axis=1, keepdims=True) * (1.0 / D_MODEL), axis=0, keepdims=True)
        loss_ref[...] += 0.5 * jnp.broadcast_to(part, loss_ref.shape)

    blk = pl.BlockSpec((NORM_ROWS, D_MODEL), lambda i: (i, 0))
    return pl.pallas_call(
        body, name="loss", grid=(SEQ // NORM_ROWS,),
        in_specs=[blk, blk],
        out_specs=[pl.BlockSpec((1, 128), lambda i: (0, 0)), blk],
        out_shape=[jax.ShapeDtypeStruct((1, 128), F32), jax.ShapeDtypeStruct((SEQ, D_MODEL), F32)],
        compiler_params=_cp("arbitrary"),
    )(y, target)


def _make_shift(j):
    def down(x):
        row = lax.broadcasted_iota(jnp.int32, x.shape, 0)
        return jnp.where(row >= j, pltpu.roll(x, j, 0), 0.0)

    def up(x):
        n = x.shape[0]
        row = lax.broadcasted_iota(jnp.int32, x.shape, 0)
        return jnp.where(row < n - j, pltpu.roll(x, n - j, 0), 0.0)

    f = jax.custom_vjp(down)
    f.defvjp(lambda x: (down(x), None), lambda _, g: (up(g),))
    return f


_SHIFT = {j: _make_shift(j) for j in (1, 2, 3)}


def _causal_conv(x, taps):
    n = len(taps)
    acc = x * taps[n - 1]
    for k in range(n - 1):
        acc = acc + _SHIFT[n - 1 - k](x) * taps[k]
    return acc


def _tap_rows(w_ref, lanes=slice(None)):
    return tuple(w_ref[k:k + 1, lanes] for k in range(w_ref.shape[0]))


def _sigmoid(x):
    return 1.0 / (1.0 + jnp.exp(-x))


def _silu(x):
    return x * _sigmoid(x)


def _softplus(x):
    return jnp.maximum(x, 0.0) + jnp.log(1.0 + jnp.exp(-jnp.abs(x)))


def _gelu_tanh(x):
    return 0.5 * x * (1.0 + jnp.tanh(math.sqrt(2.0 / math.pi) * (x + 0.044715 * (x * x * x))))


def _dnconv_fn(x, taps):
    return _silu(_causal_conv(x, taps))


def _dnconv_fwd(proj, conv_w):
    def body(x_ref, w_ref, o_ref):
        o_ref[...] = _dnconv_fn(x_ref[...], _tap_rows(w_ref))

    return pl.pallas_call(
        body, name="dnconv_fwd", grid=(DN_QKV_BLKS,),
        in_specs=[pl.BlockSpec((SEQ, 128), lambda j: (0, DN_QKV_BLK0 + j)), pl.BlockSpec((4, 128), lambda j: (0, j))],
        out_specs=pl.BlockSpec((SEQ, 128), lambda j: (0, j)),
        out_shape=jax.ShapeDtypeStruct((SEQ, 1536), F32),
        compiler_params=_cp("parallel"),
    )(proj, conv_w)


def _dnconv_bwd(proj, conv_w, dc, dproj):
    def body(x_ref, w_ref, dc_ref, _, dx_ref, dw_ref):
        _, vjp = jax.vjp(_dnconv_fn, x_ref[...], _tap_rows(w_ref))
        dx, dw = vjp(dc_ref[...])
        dx_ref[...] = dx
        for k, row in enumerate(dw):
            dw_ref[k:k + 1, :] = row

    return pl.pallas_call(
        body, name="dnconv_bwd", grid=(DN_QKV_BLKS,),
        in_specs=[pl.BlockSpec((SEQ, 128), lambda j: (0, DN_QKV_BLK0 + j)), pl.BlockSpec((4, 128), lambda j: (0, j)),
                  pl.BlockSpec((SEQ, 128), lambda j: (0, j)), pl.BlockSpec(memory_space=pl.ANY)],
        out_specs=[pl.BlockSpec((SEQ, 128), lambda j: (0, DN_QKV_BLK0 + j)), pl.BlockSpec((4, 128), lambda j: (0, j))],
        out_shape=[jax.ShapeDtypeStruct((SEQ, IN_PAD), F32), jax.ShapeDtypeStruct((4, 1536), F32)],
        input_output_aliases={3: 0},
        compiler_params=_cp("parallel"),
    )(proj, conv_w, dc, dproj)


def _ffact_fn(pg, pu, wg, wu, bg, bu):
    return _gelu_tanh(_causal_conv(pg, wg) + bg) * (_causal_conv(pu, wu) + bu)


def _ffact_args(p_ref, w_ref, b_ref):
    g, u = slice(0, 128), slice(128, 256)
    return (p_ref[:, g], p_ref[:, u], _tap_rows(w_ref, g), _tap_rows(w_ref, u), b_ref[:, g], b_ref[:, u])


def _ffact_fwd(pre, conv_w, conv_b):
    def body(p_ref, w_ref, b_ref, o_ref):
        o_ref[...] = _ffact_fn(*_ffact_args(p_ref, w_ref, b_ref)).astype(BF16)

    return pl.pallas_call(
        body, name="ffact_fwd", grid=(FF_BLKS,),
        in_specs=[pl.BlockSpec((SEQ, 256), lambda j: (0, j)), pl.BlockSpec((3, 256), lambda j: (0, j)),
                  pl.BlockSpec((1, 256), lambda j: (0, j))],
        out_specs=pl.BlockSpec((SEQ, 128), lambda j: (0, j)),
        out_shape=jax.ShapeDtypeStruct((SEQ, D_FF), BF16),
        compiler_params=_cp("parallel"),
    )(pre, conv_w, conv_b)


def _ffact_bwd(pre, conv_w, conv_b, dact, exchanges=()):
    def body(p_ref, w_ref, b_ref, da_ref, dp_ref, dw_ref, db_ref):
        _, vjp = jax.vjp(_ffact_fn, *_ffact_args(p_ref, w_ref, b_ref))
        dpg, dpu, dwg, dwu, dbg, dbu = vjp(da_ref[...].astype(F32))
        dp_ref[:, 0:128] = dpg
        dp_ref[:, 128:256] = dpu
        for k in range(3):
            dw_ref[k:k + 1, 0:128] = dwg[k]
            dw_ref[k:k + 1, 128:256] = dwu[k]
        db_ref[:, 0:128] = dbg
        db_ref[:, 128:256] = dbu

    return _hosted_call(
        body, name="ffact_bwd", steps=FF_BLKS,
        in_specs=[pl.BlockSpec((SEQ, 256), lambda j: (0, j)), pl.BlockSpec((3, 256), lambda j: (0, j)),
                  pl.BlockSpec((1, 256), lambda j: (0, j)), pl.BlockSpec((SEQ, 128), lambda j: (0, j))],
        out_specs=[pl.BlockSpec((SEQ, 256), lambda j: (0, j)), pl.BlockSpec((3, 256), lambda j: (0, j)),
                   pl.BlockSpec((1, 256), lambda j: (0, j))],
        out_shape=[jax.ShapeDtypeStruct((SEQ, 2 * D_FF), F32), jax.ShapeDtypeStruct((3, 2 * D_FF), F32),
                   jax.ShapeDtypeStruct((1, 2 * D_FF), F32)],
        scratch_shapes=[], operands=(pre, conv_w, conv_b, dact), exchanges=exchanges)


def _interleave_ff(t):
    lead = t.shape[:-1]
    return t.reshape(lead + (2, FF_BLKS, 128)).swapaxes(-3, -2).reshape(lead + (2 * D_FF,))


def _deinterleave_ff(t):
    lead = t.shape[:-1]
    return t.reshape(lead + (FF_BLKS, 2, 128)).swapaxes(-3, -2).reshape(lead + (2 * D_FF,))


def _rope_tables():
    inv = 1.0 / (ROPE_THETA ** (jnp.arange(0, HEAD_DIM, 2, dtype=F32) / HEAD_DIM))
    ang = jnp.arange(SEQ, dtype=F32)[:, None] * inv[None, :]
    cos = jnp.tile(jnp.cos(ang), (1, 4))
    sin = jnp.tile(jnp.sin(ang), (1, 4))
    sign = jnp.where((jnp.arange(128) % HEAD_DIM) < HEAD_DIM // 2, -1.0, 1.0).astype(F32)
    return cos, sin * sign[None, :]


def _rope(x, cos, sin_signed):
    lane = lax.broadcasted_iota(jnp.int32, x.shape, 1)
    partner = jnp.where((lane % HEAD_DIM) < HEAD_DIM // 2, pltpu.roll(x, 128 - HEAD_DIM // 2, 1),
                        pltpu.roll(x, HEAD_DIM // 2, 1))
    return x * cos + partner * sin_signed


def _pairs_from_qkv(t):
    lead = t.shape[:-1]
    return t.reshape(lead + (3, N_PAIR, 128)).swapaxes(-3, -2).reshape(lead + (QKV_W,))


def _qkv_from_pairs(t):
    lead = t.shape[:-1]
    return t.reshape(lead + (N_PAIR, 3, 128)).swapaxes(-3, -2).reshape(lead + (QKV_W,))


def _band_masks():
    a = lax.broadcasted_iota(jnp.int32, (ATTN_BLK, ATTN_BLK), 0)
    c = lax.broadcasted_iota(jnp.int32, (ATTN_BLK, ATTN_BLK), 1)
    return c >= a, c <= a


def _head_masks():
    lane = lax.broadcasted_iota(jnp.int32, (1, 128), 1)
    return [(lane // HEAD_DIM) == h for h in range(2)]


def _block_rows(branch, t):
    d, per_seg = DILATIONS[branch], SEGMENT_BLOCKS[branch]
    if d == 1:
        start = pl.multiple_of(t * ATTN_BLK, ATTN_BLK)
        prev = pl.multiple_of(jnp.maximum(t - 1, 0) * ATTN_BLK, ATTN_BLK)
        return pl.ds(start, ATTN_BLK), pl.ds(prev, ATTN_BLK), t > 0
    r, n = t // per_seg, t % per_seg
    start = n * (ATTN_BLK * d) + r
    prev = jnp.maximum(n - 1, 0) * (ATTN_BLK * d) + r
    return pl.ds(start, ATTN_BLK, stride=d), pl.ds(prev, ATTN_BLK, stride=d), n > 0


def _attn_fwd(proj, cos, sin_signed, exchanges=()):
    scale = HEAD_DIM ** -0.5

    def body(qkv_ref, cos_ref, sin_ref, out_ref, lse_ref, q_s, k_s, v_s, *branch_s):
        o_s, l_s = branch_s[:3], branch_s[3:]
        q_s[...] = _rope(qkv_ref[:, 0:128], cos_ref[...], sin_ref[...])
        k_s[...] = _rope(qkv_ref[:, 128:256], cos_ref[...], sin_ref[...])
        v_s[...] = qkv_ref[:, 256:384]
        m_prev0, m_cur = _band_masks()
        heads = _head_masks()
        for branch in range(3):
            def block(t, carry, branch=branch):
                rows, prows, has_prev = _block_rows(branch, t)
                with_prev = SEGMENT_BLOCKS[branch] > 1
                m_prev = m_prev0 & has_prev
                q, kc, vc = q_s[rows, :], k_s[rows, :], v_s[rows, :]
                if with_prev:
                    kp, vp = k_s[prows, :], v_s[prows, :]
                outs, lses = [], []
                for hm in heads:
                    qh = jnp.where(hm, q, 0.0)
                    sc = jnp.where(m_cur, MM_NT(qh, kc) * scale, NEG)
                    m = jnp.max(sc, axis=1, keepdims=True)
                    if with_prev:
                        sp = jnp.where(m_prev, MM_NT(qh, kp) * scale, NEG)
                        m = jnp.maximum(jnp.max(sp, axis=1, keepdims=True), m)
                    ec = jnp.exp(sc - m)
                    l = jnp.sum(ec, axis=1, keepdims=True)
                    acc = MM(ec, vc)
                    if with_prev:
                        ep = jnp.exp(sp - m)
                        l = l + jnp.sum(ep, axis=1, keepdims=True)
                        acc = acc + MM(ep, vp)
                    outs.append(acc / l)
                    lses.append(m + jnp.log(l))
                o_s[branch][rows, :] = jnp.where(heads[0], outs[0], outs[1])
                l_s[branch][rows, :] = jnp.where(heads[0], lses[0], lses[1])
                return carry

            lax.fori_loop(0, N_BLK, block, 0)
        l0, l1, l2 = l_s[0][...], l_s[1][...], l_s[2][...]
        m = jnp.maximum(jnp.maximum(l0, l1), l2)
        w0, w1, w2 = jnp.exp(l0 - m), jnp.exp(l1 - m), jnp.exp(l2 - m)
        den = w0 + w1 + w2
        out_ref[...] = (w0 * o_s[0][...] + w1 * o_s[1][...] + w2 * o_s[2][...]) / den
        lse_ref[...] = m + jnp.log(den)

    tab = pl.BlockSpec((SEQ, 128), lambda j: (0, 0))
    col = pl.BlockSpec((SEQ, 128), lambda j: (0, j))
    return _hosted_call(
        body, name="attn_fwd", steps=N_PAIR,
        in_specs=[pl.BlockSpec((SEQ, 384), lambda j: (0, j)), tab, tab],
        out_specs=[col, col],
        out_shape=[jax.ShapeDtypeStruct((SEQ, 2 * ATTN_W), F32), jax.ShapeDtypeStruct((SEQ, ATTN_W), F32)],
        scratch_shapes=[pltpu.VMEM((SEQ, 128), F32)] * 9,
        operands=(proj, cos, sin_signed), exchanges=exchanges)


def _attn_bwd(proj, cos, sin_signed, cat, lse, dcat, dproj, exchanges=()):
    scale = HEAD_DIM ** -0.5

    def body(qkv_ref, cos_ref, sin_ref, out_ref, lse_ref, do_ref, _, dqkv_ref, q_s, k_s, v_s, dq_s, dk_s, dv_s,
             dod_s):
        q_s[...] = _rope(qkv_ref[:, 0:128], cos_ref[...], sin_ref[...])
        k_s[...] = _rope(qkv_ref[:, 128:256], cos_ref[...], sin_ref[...])
        v_s[...] = qkv_ref[:, 256:384]
        dq_s[...] = jnp.zeros_like(dq_s)
        dk_s[...] = jnp.zeros_like(dk_s)
        dv_s[...] = jnp.zeros_like(dv_s)
        dod_s[...] = do_ref[...] * out_ref[...]
        m_prev0, m_cur = _band_masks()
        heads = _head_masks()
        for branch in range(3):
            def block(t, carry, branch=branch):
                rows, prows, has_prev = _block_rows(branch, t)
                with_prev = SEGMENT_BLOCKS[branch] > 1
                m_prev = m_prev0 & has_prev
                q, kc, vc = q_s[rows, :], k_s[rows, :], v_s[rows, :]
                if with_prev:
                    kp, vp = k_s[prows, :], v_s[prows, :]
                do, lse_b, dod = do_ref[rows, :], lse_ref[rows, :], dod_s[rows, :]
                dq = []
                dk_cur = jnp.zeros((ATTN_BLK, 128), F32)
                dv_cur = jnp.zeros((ATTN_BLK, 128), F32)
                dk_prev = jnp.zeros((ATTN_BLK, 128), F32)
                dv_prev = jnp.zeros((ATTN_BLK, 128), F32)
                for hm in heads:
                    qh = jnp.where(hm, q, 0.0)
                    doh = jnp.where(hm, do, 0.0)
                    lse_h = jnp.max(jnp.where(hm, lse_b, NEG), axis=1, keepdims=True)
                    delta = jnp.sum(jnp.where(hm, dod, 0.0), axis=1, keepdims=True)
                    pc = jnp.exp(jnp.where(m_cur, MM_NT(qh, kc) * scale, NEG) - lse_h)
                    dsc = pc * (MM_NT(doh, vc) - delta) * scale
                    dq_h = MM(dsc, kc)
                    dk_cur += MM_TN(dsc, qh)
                    dv_cur += MM_TN(pc, doh)
                    if with_prev:
                        pp = jnp.exp(jnp.where(m_prev, MM_NT(qh, kp) * scale, NEG) - lse_h)
                        dsp = pp * (MM_NT(doh, vp) - delta) * scale
                        dq_h = dq_h + MM(dsp, kp)
                        dk_prev += MM_TN(dsp, qh)
                        dv_prev += MM_TN(pp, doh)
                    dq.append(dq_h)
                dq_s[rows, :] += jnp.where(heads[0], dq[0], dq[1])
                dk_s[rows, :] += dk_cur
                dv_s[rows, :] += dv_cur

                if with_prev:
                    @pl.when(has_prev)
                    def _():
                        dk_s[prows, :] += dk_prev
                        dv_s[prows, :] += dv_prev

                return carry

            lax.fori_loop(0, N_BLK, block, 0)
        dqkv_ref[:, 0:128] = _rope(dq_s[...], cos_ref[...], -sin_ref[...])
        dqkv_ref[:, 128:256] = _rope(dk_s[...], cos_ref[...], -sin_ref[...])
        dqkv_ref[:, 256:384] = dv_s[...]

    tab = pl.BlockSpec((SEQ, 128), lambda j: (0, 0))
    col = pl.BlockSpec((SEQ, 128), lambda j: (0, j))
    qkv = pl.BlockSpec((SEQ, 384), lambda j: (0, j))
    (dproj,), results = _hosted_call(
        body, name="attn_bwd", steps=N_PAIR,
        in_specs=[qkv, tab, tab, col, col, col, pl.BlockSpec(memory_space=pl.ANY)],
        out_specs=[qkv],
        out_shape=[jax.ShapeDtypeStruct((SEQ, IN_PAD), F32)],
        scratch_shapes=[pltpu.VMEM((SEQ, 128), F32)] * 7,
        operands=(proj, cos, sin_signed, cat, lse, dcat, dproj), exchanges=exchanges, aliases={6: 0})
    return dproj, results


def _bdot(a, b, dims, precision=None):
    if precision is None:
        a = a.astype(BF16)
        b = b.astype(BF16)
    return lax.dot_general(a, b, (dims, ((0,), (0,))), preferred_element_type=F32, precision=precision)


def _make_bmm(precision):
    @jax.custom_vjp
    def nn(a, b):
        return _bdot(a, b, ((2,), (1,)), precision)

    @jax.custom_vjp
    def nt(a, b):
        return _bdot(a, b, ((2,), (2,)), precision)

    @jax.custom_vjp
    def tn(a, b):
        return _bdot(a, b, ((1,), (1,)), precision)

    nn.defvjp(lambda a, b: (nn(a, b), (a, b)), lambda r, g: (nt(g, r[1]), tn(r[0], g)))
    nt.defvjp(lambda a, b: (nt(a, b), (a, b)), lambda r, g: (nn(g, r[1]), tn(g, r[0])))
    tn.defvjp(lambda a, b: (tn(a, b), (a, b)), lambda r, g: (nt(r[1], g), nn(r[0], g)))
    return nn, nt, tn


BMM, BMM_NT, BMM_TN = _make_bmm(None)
BMMH, _, _ = _make_bmm(HI)
BMM3, _, _ = _make_bmm(lax.Precision.HIGH)


def _head_lanes(t, off):
    lane = lax.broadcasted_iota(jnp.int32, (1, 128), 1)
    return jnp.concatenate(
        [jnp.sum(t * (lane == off + h).astype(F32), axis=1, keepdims=True)[None] for h in range(NDH)], axis=0)


def _delta_chunk(qr, kr, vr, z, tail, alog_row, dt_row, nw, state):
    c = qr.shape[1]
    beta = _sigmoid(_head_lanes(tail, 0))
    g = -jnp.exp(_head_lanes(alog_row, 0)) * _softplus(_head_lanes(tail, NDH) + _head_lanes(dt_row, 0))

    q = qr * lax.rsqrt(jnp.sum(qr * qr, axis=2, keepdims=True) + EPS) * (128 ** -0.5)
    k = kr * lax.rsqrt(jnp.sum(kr * kr, axis=2, keepdims=True) + EPS)

    ri = lax.broadcasted_iota(jnp.int32, (c, c), 0)
    ci = lax.broadcasted_iota(jnp.int32, (c, c), 1)
    tril = ri >= ci
    eye = (ri == ci).astype(F32)
    lane = lax.broadcasted_iota(jnp.int32, (1, 128), 1)
    g_lanes = sum(g[h] * (lane == h).astype(F32) for h in range(NDH))
    gc = _head_lanes(MMH(tril.astype(F32), g_lanes), 0)
    g_row = BMMH(jnp.ones((NDH, c, c), F32), eye * gc)
    decay = jnp.where(tril, jnp.exp(jnp.where(tril, gc - g_row, 0.0)), 0.0)
    kb = k * beta
    a_mat = jnp.where(ri > ci, BMM_NT(kb, k) * decay, 0.0)
    power = -a_mat
    t_inv = eye + power
    for _ in range(5):
        power = BMM3(power, power)
        t_inv = t_inv + BMM3(t_inv, power)
    eg = jnp.exp(gc)
    u = BMM(t_inv, vr * beta)
    w = BMM(t_inv, kb * eg)
    qk = BMM_NT(q, k) * decay
    g_tot = jnp.sum(g, axis=1, keepdims=True)
    v_new = u - BMM(w, state)
    o = BMM(q * eg, state) + BMM(qk, v_new)
    new_state = state * jnp.exp(g_tot) + BMM_TN(k * jnp.exp(g_tot - gc), v_new)
    on = o * lax.rsqrt(jnp.mean(o * o, axis=2, keepdims=True) + EPS) * nw
    return on * _silu(z), new_state


def _heads(v, off=0):
    return jnp.concatenate([v[None, :, off + 128 * h:off + 128 * (h + 1)] for h in range(NDH)], axis=0)


def _unheads(t):
    return jnp.concatenate([t[h] for h in range(NDH)], axis=1)


def _delta_fwd(c_qkv, proj, alog_row, dt_row, nw, cat, exchanges=()):
    def body(c_ref, z_ref, tail_ref, al_ref, dt_ref, nw_ref, _, y_ref, st_ref, state):
        @pl.when(pl.program_id(0) == 0)
        def _():
            state[...] = jnp.zeros_like(state)

        cv = c_ref[...]
        st_ref[0] = state[...]
        y, new_state = _delta_chunk(_heads(cv), _heads(cv, 512), _heads(cv, 1024), _heads(z_ref[...]), tail_ref[...],
                                    al_ref[...], dt_ref[...], nw_ref[...], state[...])
        y_ref[...] = _unheads(y)
        state[...] = new_state

    row = pl.BlockSpec((1, 128), lambda n: (0, 0))
    return _hosted_call(
        body, name="delta_fwd", steps=NCH,
        in_specs=[pl.BlockSpec((CH, 1536), lambda n: (n, 0)), pl.BlockSpec((CH, 512), lambda n: (n, DN_Z_COL // 512)),
                  pl.BlockSpec((CH, 128), lambda n: (n, DN_TAIL_BLK)), row, row, row, pl.BlockSpec(memory_space=pl.ANY)],
        out_specs=[pl.BlockSpec((CH, 512), lambda n: (n, 1)),
                   pl.BlockSpec((1, NDH, 128, 128), lambda n: (n, 0, 0, 0))],
        out_shape=[jax.ShapeDtypeStruct((SEQ, 2 * ATTN_W), F32), jax.ShapeDtypeStruct((NCH, NDH, 128, 128), F32)],
        scratch_shapes=[pltpu.VMEM((NDH, 128, 128), F32)],
        operands=(c_qkv, proj, proj, alog_row, dt_row, nw, cat), exchanges=exchanges, aliases={6: 0})


def _delta_bwd(c_qkv, proj, alog_row, dt_row, nw, states, dcat, exchanges=()):
    def body(c_ref, z_ref, tail_ref, al_ref, dt_ref, nw_ref, st_ref, dy_ref,
             dp_ref, dc_ref, dal_ref, ddt_ref, dnw_ref, dstate):
        @pl.when(pl.program_id(0) == 0)
        def _():
            dstate[...] = jnp.zeros_like(dstate)
            dal_ref[...] = jnp.zeros_like(dal_ref)
            ddt_ref[...] = jnp.zeros_like(ddt_ref)
            dnw_ref[...] = jnp.zeros_like(dnw_ref)

        cv = c_ref[...]
        _, vjp = jax.vjp(_delta_chunk, _heads(cv), _heads(cv, 512), _heads(cv, 1024), _heads(z_ref[...]),
                         tail_ref[...], al_ref[...], dt_ref[...], nw_ref[...], st_ref[0])
        dq, dk, dv, dz, dtail, dal, ddt, dnw, dst = vjp((_heads(dy_ref[...]), dstate[...]))
        dstate[...] = dst
        dc_ref[...] = jnp.concatenate([_unheads(dq), _unheads(dk), _unheads(dv)], axis=1)
        dp_ref[...] = jnp.concatenate([_unheads(dz), dtail, jnp.zeros((CH, 128), F32)], axis=1)
        dal_ref[...] += dal
        ddt_ref[...] += ddt
        dnw_ref[...] += dnw

    rev = lambda n: NCH - 1 - n
    row = pl.BlockSpec((1, 128), lambda n: (0, 0))
    return _hosted_call(
        body, name="delta_bwd", steps=NCH,
        in_specs=[pl.BlockSpec((CH, 1536), lambda n: (rev(n), 0)),
                  pl.BlockSpec((CH, 512), lambda n: (rev(n), DN_Z_COL // 512)),
                  pl.BlockSpec((CH, 128), lambda n: (rev(n), DN_TAIL_BLK)), row, row, row,
                  pl.BlockSpec((1, NDH, 128, 128), lambda n: (rev(n), 0, 0, 0)),
                  pl.BlockSpec((CH, 512), lambda n: (rev(n), 1))],
        out_specs=[pl.BlockSpec((CH, 768), lambda n: (rev(n), DN_Z_COL // 768)),
                   pl.BlockSpec((CH, 1536), lambda n: (rev(n), 0)), row, row, row],
        out_shape=[jax.ShapeDtypeStruct((SEQ, IN_PAD), F32), jax.ShapeDtypeStruct((SEQ, 1536), F32)]
        + [jax.ShapeDtypeStruct((1, 128), F32)] * 3,
        scratch_shapes=[pltpu.VMEM((NDH, 128, 128), F32)],
        operands=(c_qkv, proj, proj, alog_row, dt_row, nw, states, dcat), exchanges=exchanges)


def _place():
    x, y, c = lax.axis_index("x"), lax.axis_index("y"), lax.axis_index("c")
    other_chips = [(1 - x, y), (x, 1 - y), (1 - x, 1 - y)]
    return x, y, c, other_chips


def _gather_exchange(shards):
    n = len(shards)

    def copies(ins, outs, sems):
        send_sems, recv_sems, local_sems = sems
        x, y, c, chips = _place()
        me, sibling = (x, y, c), (x, y, 1 - c)

        def copy(b, k, block, to, src=None):
            slot = outs[b].at[4 * block[0] + 2 * block[1] + block[2]]
            return pltpu.make_async_remote_copy(
                src_ref=slot if src is None else src, dst_ref=slot,
                send_sem=send_sems.at[b, k], recv_sem=recv_sems.at[b, k], device_id=to, device_id_type=MESH)

        mine = [pltpu.make_async_copy(ins[b], outs[b].at[4 * x + 2 * y + c], local_sems.at[b]) for b in range(n)]
        first = []
        for b in range(n):
            first.append(copy(b, 0, me, sibling, src=ins[b]))
            first += [copy(b, 1 + j, me, (*chip, c), src=ins[b]) for j, chip in enumerate(chips)]
        over_ici = [copy(b, 1 + j, (*chip, c), me) for b in range(n) for j, chip in enumerate(chips)]
        passed = [copy(b, 4 + j, (*chip, c), sibling) for b in range(n) for j, chip in enumerate(chips)]
        from_sibling = []
        for b in range(n):
            from_sibling.append(copy(b, 0, sibling, me))
            from_sibling += [copy(b, 4 + j, (*chip, 1 - c), me) for j, chip in enumerate(chips)]
        return mine, first, over_ici, passed, from_sibling

    def start(ins, outs, sems):
        mine, first, _, _, _ = copies(ins, outs, sems)
        for cp in mine + first:
            cp.start()

    def middle(ins, outs, sems):
        _, _, over_ici, passed, _ = copies(ins, outs, sems)
        for arrived, onward in zip(over_ici, passed):
            arrived.wait_recv()
            onward.start()

    def finish(ins, outs, sems):
        mine, first, _, passed, from_sibling = copies(ins, outs, sems)
        for cp in from_sibling:
            cp.wait_recv()
        for cp in first + passed:
            cp.wait_send()
        for cp in mine:
            cp.wait()

    return Exchange(shards, [jax.ShapeDtypeStruct((N_DEV,) + s.shape, s.dtype) for s in shards],
                    [pltpu.SemaphoreType.DMA((n, 7)), pltpu.SemaphoreType.DMA((n, 7)), pltpu.SemaphoreType.DMA((n,))],
                    start, middle, finish)


def _sibling_exchange(gs):
    n = len(gs)

    def copies(ins, outs, sems):
        send_sems, recv_sems = sems
        x, y, c, _ = _place()
        return [pltpu.make_async_remote_copy(
            src_ref=ins[b].at[2 * p + (1 - c)], dst_ref=outs[b].at[p],
            send_sem=send_sems.at[b, p], recv_sem=recv_sems.at[b, p],
            device_id=(x, y, 1 - c), device_id_type=MESH) for b in range(n) for p in range(4)]

    def start(ins, outs, sems):
        for cp in copies(ins, outs, sems):
            cp.start()

    def finish(ins, outs, sems):
        for cp in copies(ins, outs, sems):
            cp.wait()

    return Exchange(gs, [jax.ShapeDtypeStruct((4,) + g.shape[1:], g.dtype) for g in gs],
                    [pltpu.SemaphoreType.DMA((n, 4)), pltpu.SemaphoreType.DMA((n, 4))], start, None, finish)


def _chips_exchange(hs):
    n = len(hs)

    def copies(ins, outs, sems):
        send_sems, recv_sems, local_sems = sems
        x, y, c, chips = _place()
        my_chip = 2 * x + y
        local = [pltpu.make_async_copy(ins[b].at[my_chip], outs[b].at[my_chip], local_sems.at[b]) for b in range(n)]
        sends, arrivals = [], []
        for b in range(n):
            for k, (px, py) in enumerate(chips):
                peer = 2 * px + py
                sends.append(pltpu.make_async_remote_copy(
                    src_ref=ins[b].at[peer], dst_ref=outs[b].at[my_chip],
                    send_sem=send_sems.at[b, k], recv_sem=recv_sems.at[b, k],
                    device_id=(px, py, c), device_id_type=MESH))
                arrivals.append(pltpu.make_async_remote_copy(
                    src_ref=ins[b].at[peer], dst_ref=outs[b].at[peer],
                    send_sem=send_sems.at[b, k], recv_sem=recv_sems.at[b, k],
                    device_id=(px, py, c), device_id_type=MESH))
        return local, sends, arrivals

    def start(ins, outs, sems):
        local, sends, _ = copies(ins, outs, sems)
        for cp in local + sends:
            cp.start()

    def finish(ins, outs, sems):
        local, sends, arrivals = copies(ins, outs, sems)
        for cp in arrivals:
            cp.wait_recv()
        for cp in sends:
            cp.wait_send()
        for cp in local:
            cp.wait()

    return Exchange(hs, [jax.ShapeDtypeStruct(h.shape, h.dtype) for h in hs],
                    [pltpu.SemaphoreType.DMA((n, 3)), pltpu.SemaphoreType.DMA((n, 3)), pltpu.SemaphoreType.DMA((n,))],
                    start, None, finish)


def _run_exchange(exchange, name):
    n_in, n_out = len(exchange.operands), len(exchange.out_shapes)

    def body(*refs):
        ins, outs, sems = refs[:n_in], refs[n_in:n_in + n_out], refs[n_in + n_out:]
        exchange.start(ins, outs, sems)
        if exchange.middle is not None:
            exchange.middle(ins, outs, sems)
        exchange.finish(ins, outs, sems)

    return pl.pallas_call(
        body, name=name,
        in_specs=[HBM_SPEC] * n_in, out_specs=[HBM_SPEC] * n_out,
        out_shape=exchange.out_shapes, scratch_shapes=exchange.sems,
    )(*exchange.operands)


def _pair_add(g, r, core, name):
    _, nr, nc = g.shape
    tr = nr // 2 if nr % 32 == 0 else nr

    def body(core_ref, g_ref, r_ref, o_ref):
        o_ref[...] = (g_ref[...] + r_ref[...]).astype(BF16)

    return pl.pallas_call(
        body, name=name,
        grid_spec=pltpu.PrefetchScalarGridSpec(
            num_scalar_prefetch=1, grid=(4, nr // tr),
            in_specs=[pl.BlockSpec((1, tr, nc), lambda p, i, core: (2 * p + core[0], i, 0)),
                      pl.BlockSpec((1, tr, nc), lambda p, i, core: (p, i, 0))],
            out_specs=pl.BlockSpec((1, tr, nc), lambda p, i, core: (p, i, 0))),
        out_shape=jax.ShapeDtypeStruct(r.shape, BF16),
        compiler_params=_cp("parallel", "parallel"),
    )(core, g, r)


def _all_gather_sum_small(v):
    rows = v.shape[0]

    def body(x_ref, sum_ref, out_ref, send_sems, recv_sems, local_sem):
        x, y, c, chips = _place()
        me, sibling = (x, y, c), (x, y, 1 - c)

        def block(px, py, pc):
            return out_ref.at[pl.ds((4 * px + 2 * py + pc) * rows, rows), :]

        def copy(k, blk, to, src=None):
            return pltpu.make_async_remote_copy(
                src_ref=block(*blk) if src is None else src, dst_ref=block(*blk),
                send_sem=send_sems.at[k], recv_sem=recv_sems.at[k], device_id=to, device_id_type=MESH)

        mine = pltpu.make_async_copy(x_ref, block(*me), local_sem)
        mine.start()
        first = [copy(0, me, sibling, src=x_ref)]
        first += [copy(1 + j, me, (*chip, c), src=x_ref) for j, chip in enumerate(chips)]
        for cp in first:
            cp.start()
        passed = [copy(4 + j, (*chip, c), sibling) for j, chip in enumerate(chips)]
        for j, chip in enumerate(chips):
            copy(1 + j, (*chip, c), me).wait_recv()
            passed[j].start()
        copy(0, sibling, me).wait_recv()
        for j, chip in enumerate(chips):
            copy(4 + j, (*chip, 1 - c), me).wait_recv()
        for cp in first + passed:
            cp.wait_send()
        mine.wait()
        total = out_ref[pl.ds(0, rows), :]
        for d in range(1, N_DEV):
            total = total + out_ref[pl.ds(d * rows, rows), :]
        sum_ref[...] = total

    vm = pl.BlockSpec(memory_space=pltpu.VMEM)
    return pl.pallas_call(
        body, name="small_all_reduce",
        in_specs=[vm], out_specs=[vm],
        out_shape=[jax.ShapeDtypeStruct((rows, 128), F32)],
        scratch_shapes=[pltpu.VMEM((N_DEV * rows, 128), F32), pltpu.SemaphoreType.DMA((7,)),
                        pltpu.SemaphoreType.DMA((7,)), pltpu.SemaphoreType.DMA],
    )(v)[0]


def _adamw(w, g, m, v):
    m = ADAM_B1 * m + (1.0 - ADAM_B1) * g
    v = ADAM_B2 * v + (1.0 - ADAM_B2) * (g * g)
    m_hat = m / (1.0 - ADAM_B1 ** ADAM_STEP)
    v_hat = v / (1.0 - ADAM_B2 ** ADAM_STEP)
    delta = -ADAM_LR * (m_hat / (jnp.sqrt(v_hat) + ADAM_EPS) + ADAM_WD * w)
    return delta, m, v


ADAM_ROWS = dict(w_in=256, w_out=128, ffn_w_in=256, ffn_w_out=176)


def _sum_chips(p):
    p = p.astype(F32)
    return (p[0] + p[1]) + (p[2] + p[3])


def _sum_parts(parts, name):
    def body(p_ref, g_ref):
        g_ref[...] = _sum_chips(p_ref[...])

    return pl.pallas_call(body, name=name, out_shape=jax.ShapeDtypeStruct(parts.shape[1:], F32),
                          compiler_params=_cp())(parts)


def _adamw_sharded(parts, w, m, v, tr, name):
    nl, nr, nc = w.shape
    n_parts = parts[0].shape[0]

    def body(*refs):
        p_refs, (w_ref, m_ref, v_ref, g_ref, d_ref, nm_ref, nv_ref) = refs[:nl], refs[nl:]
        layer = pl.program_id(0)
        p = p_refs[0][...]
        for l in range(1, nl):
            p = jnp.where(layer == l, p_refs[l][...], p)
        g = _sum_chips(p) if n_parts == 4 else p[0]
        delta, nm, nv = _adamw(w_ref[0], g, m_ref[0], v_ref[0])
        g_ref[0] = g
        d_ref[0] = delta
        nm_ref[0] = nm
        nv_ref[0] = nv

    blk = pl.BlockSpec((1, tr, nc), lambda l, i: (l, i, 0))
    return pl.pallas_call(
        body, name=name, grid=(nl, nr // tr),
        in_specs=[pl.BlockSpec((n_parts, tr, nc), lambda l, i: (0, i, 0))] * nl + [blk, blk, blk],
        out_specs=[blk] * 4,
        out_shape=[jax.ShapeDtypeStruct(w.shape, F32)] * 4,
        compiler_params=_cp("parallel", "parallel"),
    )(*parts, w, m, v)


def _adamw_small(g, w, m, v):
    def body(g_ref, w_ref, m_ref, v_ref, d_ref, nm_ref, nv_ref):
        delta, nm, nv = _adamw(w_ref[...], g_ref[...], m_ref[...], v_ref[...])
        d_ref[...] = delta
        nm_ref[...] = nm
        nv_ref[...] = nv

    return pl.pallas_call(
        body, name="adamw_small",
        out_shape=[jax.ShapeDtypeStruct(g.shape, F32)] * 3,
    )(g, w, m, v)


def _pack(arrays, rows):
    flat = jnp.concatenate([a.reshape(-1).astype(F32) for a in arrays])
    return jnp.pad(flat, (0, rows * 128 - flat.shape[0])).reshape(rows, 128)


def _unpack(packed, shapes):
    flat = packed.reshape(-1)
    out, off = [], 0
    for s in shapes:
        n = math.prod(s)
        out.append(flat[off:off + n].reshape(s))
        off += n
    return out


def _row(v, width=None):
    v = v.reshape(1, -1)
    return v if width is None else jnp.pad(v, ((0, 0), (0, width - v.shape[1])))


def _layer_fwd(x, wts, tables, attn_exchanges=(), delta_exchanges=(), on_attn=None, on_delta=None):
    h = _norm_fwd(x, wts["norm_pre_mix"], "norm_pre_mix")
    proj = _matmul(h, wts["w_in"], tb=True, tm=512, tn=768, tk=1024, name="mm_proj")
    (cat, lse), got = _attn_fwd(proj, *tables, exchanges=attn_exchanges)
    if on_attn is not None:
        on_attn(got)
    c_qkv = _dnconv_fwd(proj, wts["dn_conv_w"])
    (cat, states), got = _delta_fwd(c_qkv, proj, wts["dn_a_log"], wts["dn_dt_bias"], wts["dn_norm_w"], cat,
                                    exchanges=delta_exchanges)
    if on_delta is not None:
        on_delta(got)
    mix = _matmul(cat, wts["w_out"], tm=512, tn=1024, tk=1024, name="mm_mix")
    x1 = _resnorm_fwd(x, mix, wts["norm_post_mix"], "norm_post_mix")
    h2 = _norm_fwd(x1, wts["norm_pre_ffn"], "norm_pre_ffn")
    pre = _matmul(h2, wts["ffn_w_in"], tb=True, tm=512, tn=512, tk=1024, name="mm_ffn_in")
    act = _ffact_fwd(pre, wts["ffn_conv_w"], wts["ffn_conv_b"])
    f = _matmul(act, wts["ffn_w_out"], tm=512, tn=1024, tk=D_FF, name="mm_ffn_out")
    x2 = _resnorm_fwd(x1, f, wts["norm_post_ffn"], "norm_post_ffn")
    saved = dict(x=x, h=h, proj=proj, lse=lse, c_qkv=c_qkv, states=states, cat=cat, mix=mix, x1=x1, h2=h2, pre=pre,
                 act=act, f=f)
    return x2, saved


def _layer_bwd(dx2, wts, s, tables, ffact_exchanges=(), delta_exchanges=None, attn_exchanges=None):
    g = {}
    df, g["norm_post_ffn"] = _norm_bwd(s["f"], wts["norm_post_ffn"], dx2, None, "norm_post_ffn_bwd")
    dact = _matmul(df, wts["ffn_w_out"], tb=True, tm=512, tn=1408, tk=1024, name="mm_dact", out_dtype=BF16)
    g["ffn_w_out"] = _matmul(s["act"], df, ta=True, tm=1408, tn=512, tk=SEQ, name="mm_dw_ffn_out")
    (dpre, g["ffn_conv_w"], g["ffn_conv_b"]), got = _ffact_bwd(s["pre"], wts["ffn_conv_w"], wts["ffn_conv_b"], dact,
                                                               exchanges=ffact_exchanges)
    dh2 = _matmul(dpre, wts["ffn_w_in"], tm=1024, tn=1024, tk=1408, name="mm_dh2")
    g["ffn_w_in"] = _matmul(dpre, s["h2"], ta=True, tm=512, tn=1024, tk=SEQ, name="mm_dw_ffn_in")
    dx1, g["norm_pre_ffn"] = _norm_bwd(s["x1"], wts["norm_pre_ffn"], dh2, dx2, "norm_pre_ffn_bwd")
    dmix, g["norm_post_mix"] = _norm_bwd(s["mix"], wts["norm_post_mix"], dx1, None, "norm_post_mix_bwd")
    dcat = _matmul(dmix, wts["w_out"], tb=True, tm=512, tn=1024, tk=1024, name="mm_dcat")
    g["w_out"] = _matmul(s["cat"], dmix, ta=True, tm=1024, tn=512, tk=SEQ, name="mm_dw_out")
    (dproj, dc, g["dn_a_log"], g["dn_dt_bias"], g["dn_norm_w"]), got = _delta_bwd(
        s["c_qkv"], s["proj"], wts["dn_a_log"], wts["dn_dt_bias"], wts["dn_norm_w"], s["states"], dcat,
        exchanges=delta_exchanges(g, got) if delta_exchanges is not None else ())
    dproj, got = _attn_bwd(s["proj"], *tables, s["cat"], s["lse"], dcat, dproj,
                           exchanges=attn_exchanges(got) if attn_exchanges is not None else ())
    dproj, g["dn_conv_w"] = _dnconv_bwd(s["proj"], wts["dn_conv_w"], dc, dproj)
    dh = _matmul(dproj, wts["w_in"], tm=1024, tn=1024, tk=1280, name="mm_dh")
    g["w_in"] = _matmul(dproj, s["h"], ta=True, tm=768, tn=1024, tk=SEQ, name="mm_dw_in")
    dx, g["norm_pre_mix"] = _norm_bwd(s["x"], wts["norm_pre_mix"], dh, dx1, "norm_pre_mix_bwd")
    return dx, g, got


BIG = ("w_in", "w_out", "ffn_w_in", "ffn_w_out")
COLUMN_SHARDED = ("w_in", "ffn_w_in")
SMALL_SHARDED = ("dn_conv_w", "ffn_conv_w")
REPLICATED = ("dn_a_log", "dn_dt_bias", "dn_norm_w", "ffn_conv_b", "norm_pre_mix", "norm_post_mix", "norm_pre_ffn",
              "norm_post_ffn")
WEIGHTS = ("w_in", "dn_conv_w", "dn_a_log", "dn_dt_bias", "dn_norm_w", "w_out", "ffn_w_in", "ffn_conv_w", "ffn_conv_b",
           "ffn_w_out", "norm_pre_mix", "norm_post_mix", "norm_pre_ffn", "norm_post_ffn")
FULL_SHAPE = dict(dn_conv_w=(DEPTH, 4, 1536), ffn_conv_w=(DEPTH, 3, 2 * D_FF), dn_a_log=(DEPTH, NDH),
                  dn_dt_bias=(DEPTH, NDH), dn_norm_w=(DEPTH, 128), ffn_conv_b=(DEPTH, 2 * D_FF),
                  norm_pre_mix=(DEPTH, D_MODEL), norm_post_mix=(DEPTH, D_MODEL), norm_pre_ffn=(DEPTH, D_MODEL),
                  norm_post_ffn=(DEPTH, D_MODEL))
SMALL_GRAD_ORDER = REPLICATED + SMALL_SHARDED
SMALL_GRAD_ROWS = 520
SMALL_W_ROWS = 48
SMALL_ADAM_ROWS = 200


def _w_in_rows_to_kernel_order(t):
    qkv = t[:QKV_W].reshape(3, N_PAIR, 128, -1).swapaxes(0, 1).reshape(QKV_W, -1)
    return jnp.pad(jnp.concatenate([qkv, t[QKV_W:]], axis=0), ((0, IN_PAD - IN_COLS), (0, 0)))


def _w_in_rows_from_kernel_order(t):
    qkv = t[:QKV_W].reshape(N_PAIR, 3, 128, -1).swapaxes(0, 1).reshape(QKV_W, -1)
    return jnp.concatenate([qkv, t[QKV_W:IN_COLS]], axis=0)


def _interleave_ff_rows(t):
    return t.reshape(2, FF_BLKS, 128, -1).swapaxes(0, 1).reshape(2 * D_FF, -1)


def _deinterleave_ff_rows(t):
    return t.reshape(FF_BLKS, 2, 128, -1).swapaxes(0, 1).reshape(2 * D_FF, -1)


def kernel(x, w_in, dn_conv_w, dn_a_log, dn_dt_bias, dn_norm_w, w_out, ffn_w_in, ffn_conv_w, ffn_conv_b, ffn_w_out, norm_pre_mix, norm_post_mix, norm_pre_ffn, norm_post_ffn, loss_target, m_w_in, m_dn_conv_w, m_dn_a_log, m_dn_dt_bias, m_dn_norm_w, m_w_out, m_ffn_w_in, m_ffn_conv_w, m_ffn_conv_b, m_ffn_w_out, m_norm_pre_mix, m_norm_post_mix, m_norm_pre_ffn, m_norm_post_ffn, v_w_in, v_dn_conv_w, v_dn_a_log, v_dn_dt_bias, v_dn_norm_w, v_w_out, v_ffn_w_in, v_ffn_conv_w, v_ffn_conv_b, v_ffn_w_out, v_norm_pre_mix, v_norm_post_mix, v_norm_pre_ffn, v_norm_post_ffn):
    local = dict(w_in=w_in, dn_conv_w=dn_conv_w, dn_a_log=dn_a_log, dn_dt_bias=dn_dt_bias, dn_norm_w=dn_norm_w,
                 w_out=w_out, ffn_w_in=ffn_w_in, ffn_conv_w=ffn_conv_w, ffn_conv_b=ffn_conv_b, ffn_w_out=ffn_w_out,
                 norm_pre_mix=norm_pre_mix, norm_post_mix=norm_post_mix, norm_pre_ffn=norm_pre_ffn,
                 norm_post_ffn=norm_post_ffn)
    mom_m = dict(w_in=m_w_in, dn_conv_w=m_dn_conv_w, dn_a_log=m_dn_a_log, dn_dt_bias=m_dn_dt_bias,
                 dn_norm_w=m_dn_norm_w, w_out=m_w_out, ffn_w_in=m_ffn_w_in, ffn_conv_w=m_ffn_conv_w,
                 ffn_conv_b=m_ffn_conv_b, ffn_w_out=m_ffn_w_out, norm_pre_mix=m_norm_pre_mix,
                 norm_post_mix=m_norm_post_mix, norm_pre_ffn=m_norm_pre_ffn, norm_post_ffn=m_norm_post_ffn)
    mom_v = dict(w_in=v_w_in, dn_conv_w=v_dn_conv_w, dn_a_log=v_dn_a_log, dn_dt_bias=v_dn_dt_bias,
                 dn_norm_w=v_dn_norm_w, w_out=v_w_out, ffn_w_in=v_ffn_w_in, ffn_conv_w=v_ffn_conv_w,
                 ffn_conv_b=v_ffn_conv_b, ffn_w_out=v_ffn_w_out, norm_pre_mix=v_norm_pre_mix,
                 norm_post_mix=v_norm_post_mix, norm_pre_ffn=v_norm_pre_ffn, norm_post_ffn=v_norm_post_ffn)
    dev = 4 * lax.axis_index("x") + 2 * lax.axis_index("y") + lax.axis_index("c")
    core = lax.axis_index("c").astype(jnp.int32).reshape(1)

    def shard(n, l):
        s = local[n][l].astype(BF16)
        return s.T if n in COLUMN_SHARDED else s

    def matrix(n, gathered):
        if n == "w_in":
            return _w_in_rows_to_kernel_order(gathered.reshape(IN_COLS, D_MODEL))
        if n == "ffn_w_in":
            return _interleave_ff_rows(gathered.reshape(2 * D_FF, D_MODEL))
        return gathered.reshape(-1, D_MODEL)

    small_w = _pack([dn_conv_w, ffn_conv_w], SMALL_W_ROWS)
    g_w_in0, g_small = _run_exchange(_gather_exchange([shard("w_in", 0), small_w]), "weights_all_gather")
    n_dn, n_ff = DEPTH * 4 * 192, DEPTH * 3 * 704
    sm = g_small.reshape(N_DEV, -1)
    full_dn_conv = sm[:, :n_dn].reshape(N_DEV, DEPTH, 4, 192).transpose(1, 2, 0, 3).reshape(DEPTH, 4, 1536)
    full_ff_conv = _interleave_ff(
        sm[:, n_dn:n_dn + n_ff].reshape(N_DEV, DEPTH, 3, 704).transpose(1, 2, 0, 3).reshape(DEPTH, 3, 2 * D_FF))

    def small_weights(l):
        wts = dict(dn_conv_w=full_dn_conv[l], ffn_conv_w=full_ff_conv[l], ffn_conv_b=_interleave_ff(_row(ffn_conv_b[l])),
                   dn_a_log=_row(dn_a_log[l], 128), dn_dt_bias=_row(dn_dt_bias[l], 128))
        for n in ("dn_norm_w", "norm_pre_mix", "norm_post_mix", "norm_pre_ffn", "norm_post_ffn"):
            wts[n] = _row(local[n][l])
        return wts

    weights = [small_weights(l) for l in range(DEPTH)]
    weights[0]["w_in"] = matrix("w_in", g_w_in0)

    def gather_behind(wanted):
        def deliver(got):
            for (n, l), g in zip(wanted, got[0]):
                weights[l][n] = matrix(n, g)

        return [_gather_exchange([shard(n, l) for n, l in wanted])], deliver

    tables = _rope_tables()
    ex_attn0, on_attn0 = gather_behind([("w_out", 0), ("ffn_w_in", 0), ("ffn_w_out", 0)])
    ex_delta0, on_delta0 = gather_behind([("w_in", 1), ("w_out", 1)])
    ex_attn1, on_attn1 = gather_behind([("ffn_w_in", 1), ("ffn_w_out", 1)])
    act, saved0 = _layer_fwd(x[0], weights[0], tables, ex_attn0, ex_delta0, on_attn0, on_delta0)
    act, saved1 = _layer_fwd(act, weights[1], tables, ex_attn1, (), on_attn1, None)
    loss_part, dact = _loss_fwd_bwd(act, loss_target[0])

    def to_devices(name, t):
        if name == "w_in":
            t = _w_in_rows_from_kernel_order(t)
        if name == "ffn_w_in":
            t = _deinterleave_ff_rows(t)
        return t.reshape(N_DEV, t.shape[0] // N_DEV, t.shape[1])

    def pair_sums(names, layer, to_dev, from_sibling):
        return [_pair_add(gd, r, core, "grads_pair_add_%s_%d" % (n, layer))
                for n, gd, r in zip(names, to_dev, from_sibling)]

    grads = [None] * DEPTH
    dact, grads[1], _ = _layer_bwd(dact, weights[1], saved1, tables)
    to_dev1 = [to_devices(n, grads[1][n]) for n in BIG]
    early = ("w_out", "ffn_w_in", "ffn_w_out")
    parts, stash = {}, {}

    def delta_exchanges(g, got_ffact):
        stash["to_dev0"] = [to_devices(n, g[n]) for n in early]
        return [_chips_exchange(pair_sums(BIG, 1, to_dev1, got_ffact[0])), _sibling_exchange(stash["to_dev0"])]

    def attn_exchanges(got_delta):
        for n, p in zip(BIG, got_delta[0]):
            parts[n, 1] = p
        return [_chips_exchange(pair_sums(early, 0, stash["to_dev0"], got_delta[1]))]

    dact, grads[0], got_attn = _layer_bwd(dact, weights[0], saved0, tables, [_sibling_exchange(to_dev1)],
                                          delta_exchanges, attn_exchanges)
    for n, p in zip(early, got_attn[0]):
        parts[n, 0] = p
    grad_x = dact[None]
    last = [to_devices("w_in", grads[0]["w_in"])]
    from_sibling = _run_exchange(_sibling_exchange(last), "grads_to_sibling")
    parts["w_in", 0], = _run_exchange(_chips_exchange(pair_sums(("w_in",), 0, last, from_sibling)), "grads_to_chips")

    def small_grad(name):
        t = jnp.stack([grads[l][name] for l in range(DEPTH)])
        if name in ("dn_a_log", "dn_dt_bias"):
            t = t[:, 0, :NDH]
        if name in ("ffn_conv_w", "ffn_conv_b"):
            t = _deinterleave_ff(t)
        return t.reshape(FULL_SHAPE[name])

    small_part = _pack([small_grad(n) for n in SMALL_GRAD_ORDER] + [loss_part[0, :1]], SMALL_GRAD_ROWS)
    small_sum = _all_gather_sum_small(small_part)
    small_g = dict(zip(SMALL_GRAD_ORDER + ("loss",), _unpack(small_sum, [FULL_SHAPE[n] for n in SMALL_GRAD_ORDER] + [(1,)])))
    loss = small_g["loss"][0]
    small_g["dn_conv_w"] = lax.dynamic_slice_in_dim(small_g["dn_conv_w"], dev * 192, 192, axis=2)
    small_g["ffn_conv_w"] = lax.dynamic_slice_in_dim(small_g["ffn_conv_w"], dev * 704, 704, axis=2)

    out_g, out_d, out_m, out_v = {}, {}, {}, {}
    for n in BIG:
        p = [parts[n, l] for l in range(DEPTH)]
        if n in COLUMN_SHARDED:
            p = [_sum_parts(t, "grad_sum_%s_%d" % (n, l)).T[None] for l, t in enumerate(p)]
        out_g[n], out_d[n], out_m[n], out_v[n] = _adamw_sharded(p, local[n], mom_m[n], mom_v[n], ADAM_ROWS[n], "adamw_" + n)
    shapes = [small_g[n].shape for n in SMALL_GRAD_ORDER]
    d_s, m_s, v_s = _adamw_small(_pack([small_g[n] for n in SMALL_GRAD_ORDER], SMALL_ADAM_ROWS),
                                 _pack([local[n] for n in SMALL_GRAD_ORDER], SMALL_ADAM_ROWS),
                                 _pack([mom_m[n] for n in SMALL_GRAD_ORDER], SMALL_ADAM_ROWS),
                                 _pack([mom_v[n] for n in SMALL_GRAD_ORDER], SMALL_ADAM_ROWS))
    for n, d, m, v in zip(SMALL_GRAD_ORDER, _unpack(d_s, shapes), _unpack(m_s, shapes), _unpack(v_s, shapes)):
        out_g[n], out_d[n], out_m[n], out_v[n] = small_g[n], d, m, v
    return (loss, grad_x, *[out_g[n] for n in WEIGHTS], *[out_d[n] for n in WEIGHTS],
            *[out_m[n] for n in WEIGHTS], *[out_v[n] for n in WEIGHTS])
```

```python
import functools
import math

import jax
import jax.numpy as jnp
from jax import lax
from jax.experimental import pallas as pl
from jax.experimental.pallas import tpu as pltpu

F32 = jnp.float32
BF16 = jnp.bfloat16
HI = lax.Precision.HIGHEST
MESH = pl.DeviceIdType.MESH

N_DEV = 8
SEQ = 2048
D_MODEL = 1024
DEPTH = 2
N_PAIR = 4
HEAD_DIM = 64
ATTN_W = 512
ATTN_BLK = 128
DILATIONS = (1, 4, 16)
SEGMENT_BLOCKS = (16, 4, 1)
N_BLK = SEQ // ATTN_BLK
NDH = 4
CH = 64
NCH = SEQ // CH
IN_COLS = 3592
IN_PAD = 3840
QKV_W = 3 * ATTN_W
DN_QKV_BLK0 = QKV_W // 128
DN_QKV_BLKS = 1536 // 128
DN_Z_COL = 3072
DN_TAIL_BLK = 3584 // 128
D_FF = 2816
FF_BLKS = D_FF // 128
EPS = 1e-6
NEG = -1e30
ROPE_THETA = 10000.0

ADAM_LR, ADAM_B1, ADAM_B2, ADAM_EPS, ADAM_WD, ADAM_STEP = 0.001, 0.9, 0.999, 1e-08, 0.01, 10

VMEM_LIMIT = 56 * 1024 * 1024


def _cp(*sem):
    return pltpu.CompilerParams(dimension_semantics=sem, vmem_limit_bytes=VMEM_LIMIT)


class Exchange:
    def __init__(self, operands, out_shapes, sems, start, middle, finish):
        self.operands, self.out_shapes, self.sems = list(operands), list(out_shapes), list(sems)
        self.start, self.middle, self.finish = start, middle, finish


HBM_SPEC = pl.BlockSpec(memory_space=pltpu.HBM)


def _hosted_call(body, *, name, steps, in_specs, out_specs, out_shape, scratch_shapes, operands, exchanges=(),
                 aliases=None):
    n_in, n_out, n_scr = len(in_specs), len(out_specs), len(scratch_shapes)

    def take(refs, pos, counts):
        groups = []
        for c in counts:
            groups.append(refs[pos:pos + c])
            pos += c
        return groups, pos

    def full_body(*refs):
        ins, pos = refs[:n_in], n_in
        ex_ins, pos = take(refs, pos, [len(e.operands) for e in exchanges])
        outs, pos = refs[pos:pos + n_out], pos + n_out
        ex_outs, pos = take(refs, pos, [len(e.out_shapes) for e in exchanges])
        scr, pos = refs[pos:pos + n_scr], pos + n_scr
        ex_sems, pos = take(refs, pos, [len(e.sems) for e in exchanges])
        step = pl.program_id(0)
        for e, a, b, s in zip(exchanges, ex_ins, ex_outs, ex_sems):
            pl.when(step == 0)(functools.partial(e.start, a, b, s))
            if e.middle is not None:
                pl.when(step == steps // 2)(functools.partial(e.middle, a, b, s))
        body(*ins, *outs, *scr)
        for e, a, b, s in zip(exchanges, ex_ins, ex_outs, ex_sems):
            pl.when(step == steps - 1)(functools.partial(e.finish, a, b, s))

    n_ex_in = sum(len(e.operands) for e in exchanges)
    n_ex_out = sum(len(e.out_shapes) for e in exchanges)
    results = pl.pallas_call(
        full_body, name=name, grid=(steps,),
        in_specs=list(in_specs) + [HBM_SPEC] * n_ex_in,
        out_specs=list(out_specs) + [HBM_SPEC] * n_ex_out,
        out_shape=list(out_shape) + [s for e in exchanges for s in e.out_shapes],
        scratch_shapes=list(scratch_shapes) + [s for e in exchanges for s in e.sems],
        input_output_aliases=aliases or {},
        compiler_params=_cp("arbitrary"),
    )(*operands, *[a for e in exchanges for a in e.operands])
    ex_results, _ = take(results, n_out, [len(e.out_shapes) for e in exchanges])
    return results[:n_out], ex_results


def _dot(a, b, dims, precision=None):
    if precision is None:
        a = a.astype(BF16)
        b = b.astype(BF16)
    return lax.dot_general(a, b, (dims, ((), ())), preferred_element_type=F32, precision=precision)


def _make_mm(precision):
    @jax.custom_vjp
    def nn(a, b):
        return _dot(a, b, ((1,), (0,)), precision)

    @jax.custom_vjp
    def nt(a, b):
        return _dot(a, b, ((1,), (1,)), precision)

    @jax.custom_vjp
    def tn(a, b):
        return _dot(a, b, ((0,), (0,)), precision)

    nn.defvjp(lambda a, b: (nn(a, b), (a, b)), lambda r, g: (nt(g, r[1]), tn(r[0], g)))
    nt.defvjp(lambda a, b: (nt(a, b), (a, b)), lambda r, g: (nn(g, r[1]), tn(g, r[0])))
    tn.defvjp(lambda a, b: (tn(a, b), (a, b)), lambda r, g: (nt(r[1], g), nn(r[0], g)))
    return nn, nt, tn


MM, MM_NT, MM_TN = _make_mm(None)
MMH, _, _ = _make_mm(HI)


def _matmul(a, b, *, ta=False, tb=False, tm, tn, tk, name, out_dtype=F32):
    (k_dim, m_dim) = a.shape if ta else a.shape[::-1]
    (n_dim, k2) = b.shape if tb else b.shape[::-1]
    assert k_dim == k2 and m_dim % tm == 0 and n_dim % tn == 0 and k_dim % tk == 0, (a.shape, b.shape, tm, tn, tk)
    nk = k_dim // tk
    dims = ((0 if ta else 1,), (1 if tb else 0,))

    def body(a_ref, b_ref, o_ref, *acc):
        p = _dot(a_ref[...], b_ref[...], dims)
        if nk == 1:
            o_ref[...] = p.astype(out_dtype)
            return
        acc_ref, k = acc[0], pl.program_id(2)

        @pl.when(k == 0)
        def _():
            acc_ref[...] = p

        @pl.when(k > 0)
        def _():
            acc_ref[...] += p

        @pl.when(k == nk - 1)
        def _():
            o_ref[...] = acc_ref[...].astype(out_dtype)

    a_spec = pl.BlockSpec((tk, tm), lambda i, j, k: (k, i)) if ta else pl.BlockSpec((tm, tk), lambda i, j, k: (i, k))
    b_spec = pl.BlockSpec((tn, tk), lambda i, j, k: (j, k)) if tb else pl.BlockSpec((tk, tn), lambda i, j, k: (k, j))
    return pl.pallas_call(
        body, name=name,
        grid=(m_dim // tm, n_dim // tn, nk),
        in_specs=[a_spec, b_spec],
        out_specs=pl.BlockSpec((tm, tn), lambda i, j, k: (i, j)),
        out_shape=jax.ShapeDtypeStruct((m_dim, n_dim), out_dtype),
        scratch_shapes=[pltpu.VMEM((tm, tn), F32)] if nk > 1 else [],
        compiler_params=_cp("parallel", "parallel", "arbitrary"),
    )(a, b)


NORM_ROWS = 256


def _rms(x, w):
    return x * lax.rsqrt(jnp.mean(x * x, axis=1, keepdims=True) + EPS) * w


def _norm_fwd(x, w_row, name, out_dtype=BF16):
    def body(x_ref, w_ref, o_ref):
        o_ref[...] = _rms(x_ref[...], w_ref[...]).astype(out_dtype)

    return pl.pallas_call(
        body, name=name, grid=(SEQ // NORM_ROWS,),
        in_specs=[pl.BlockSpec((NORM_ROWS, D_MODEL), lambda i: (i, 0)), pl.BlockSpec((1, D_MODEL), lambda i: (0, 0))],
        out_specs=pl.BlockSpec((NORM_ROWS, D_MODEL), lambda i: (i, 0)),
        out_shape=jax.ShapeDtypeStruct((SEQ, D_MODEL), out_dtype),
        compiler_params=_cp("parallel"),
    )(x, w_row)


def _resnorm_fwd(x, f, w_row, name):
    def body(x_ref, f_ref, w_ref, o_ref):
        o_ref[...] = x_ref[...] + _rms(f_ref[...], w_ref[...])

    blk = pl.BlockSpec((NORM_ROWS, D_MODEL), lambda i: (i, 0))
    return pl.pallas_call(
        body, name=name, grid=(SEQ // NORM_ROWS,),
        in_specs=[blk, blk, pl.BlockSpec((1, D_MODEL), lambda i: (0, 0))],
        out_specs=blk, out_shape=jax.ShapeDtypeStruct((SEQ, D_MODEL), F32),
        compiler_params=_cp("parallel"),
    )(x, f, w_row)


def _norm_bwd(x, w_row, dy, add, name):
    has_add = add is not None

    def body(*refs):
        if has_add:
            x_ref, w_ref, dy_ref, add_ref, dx_ref, dw_ref = refs
        else:
            x_ref, w_ref, dy_ref, dx_ref, dw_ref = refs
        _, vjp = jax.vjp(_rms, x_ref[...], w_ref[...])
        dx, dw = vjp(dy_ref[...])
        dx_ref[...] = dx + add_ref[...] if has_add else dx

        @pl.when(pl.program_id(0) == 0)
        def _():
            dw_ref[...] = jnp.zeros_like(dw_ref)

        dw_ref[...] += dw

    blk = pl.BlockSpec((NORM_ROWS, D_MODEL), lambda i: (i, 0))
    row = pl.BlockSpec((1, D_MODEL), lambda i: (0, 0))
    ins = [x, w_row, dy] + ([add] if has_add else [])
    return pl.pallas_call(
        body, name=name, grid=(SEQ // NORM_ROWS,),
        in_specs=[blk, row, blk] + ([blk] if has_add else []),
        out_specs=[blk, row],
        out_shape=[jax.ShapeDtypeStruct((SEQ, D_MODEL), F32), jax.ShapeDtypeStruct((1, D_MODEL), F32)],
        compiler_params=_cp("arbitrary"),
    )(*ins)


def _loss_fwd_bwd(y, target):
    def body(y_ref, t_ref, loss_ref, dy_ref):
        err = y_ref[...] - t_ref[...]
        dy_ref[...] = err * (1.0 / D_MODEL)

        @pl.when(pl.program_id(0) == 0)
        def _():
            loss_ref[...] = jnp.zeros_like(loss_ref)

        part = jnp.sum(jnp.sum(err * err, axis=1, keepdims=True) * (1.0 / D_MODEL), axis=0, keepdims=True)
        loss_ref[...] += 0.5 * jnp.broadcast_to(part, loss_ref.shape)

    blk = pl.BlockSpec((NORM_ROWS, D_MODEL), lambda i: (i, 0))
    return pl.pallas_call(
        body, name="loss", grid=(SEQ // NORM_ROWS,),
        in_specs=[blk, blk],
        out_specs=[pl.BlockSpec((1, 128), lambda i: (0, 0)), blk],
        out_shape=[jax.ShapeDtypeStruct((1, 128), F32), jax.ShapeDtypeStruct((SEQ, D_MODEL), F32)],
        compiler_params=_cp("arbitrary"),
    )(y, target)


def _make_shift(j):
    def down(x):
        row = lax.broadcasted_iota(jnp.int32, x.shape, 0)
        return jnp.where(row >= j, pltpu.roll(x, j, 0), 0.0)

    def up(x):
        n = x.shape[0]
        row = lax.broadcasted_iota(jnp.int32, x.shape, 0)
        return jnp.where(row < n - j, pltpu.roll(x, n - j, 0), 0.0)

    f = jax.custom_vjp(down)
    f.defvjp(lambda x: (down(x), None), lambda _, g: (up(g),))
    return f


_SHIFT = {j: _make_shift(j) for j in (1, 2, 3)}


def _causal_conv(x, taps):
    n = len(taps)
    acc = x * taps[n - 1]
    for k in range(n - 1):
        acc = acc + _SHIFT[n - 1 - k](x) * taps[k]
    return acc


def _tap_rows(w_ref, lanes=slice(None)):
    return tuple(w_ref[k:k + 1, lanes] for k in range(w_ref.shape[0]))


def _sigmoid(x):
    return 1.0 / (1.0 + jnp.exp(-x))


def _silu(x):
    return x * _sigmoid(x)


def _softplus(x):
    return jnp.maximum(x, 0.0) + jnp.log(1.0 + jnp.exp(-jnp.abs(x)))


def _gelu_tanh(x):
    return 0.5 * x * (1.0 + jnp.tanh(math.sqrt(2.0 / math.pi) * (x + 0.044715 * (x * x * x))))


def _dnconv_fn(x, taps):
    return _silu(_causal_conv(x, taps))


def _dnconv_fwd(proj, conv_w):
    def body(x_ref, w_ref, o_ref):
        o_ref[...] = _dnconv_fn(x_ref[...], _tap_rows(w_ref))

    return pl.pallas_call(
        body, name="dnconv_fwd", grid=(DN_QKV_BLKS,),
        in_specs=[pl.BlockSpec((SEQ, 128), lambda j: (0, DN_QKV_BLK0 + j)), pl.BlockSpec((4, 128), lambda j: (0, j))],
        out_specs=pl.BlockSpec((SEQ, 128), lambda j: (0, j)),
        out_shape=jax.ShapeDtypeStruct((SEQ, 1536), F32),
        compiler_params=_cp("parallel"),
    )(proj, conv_w)


def _dnconv_bwd(proj, conv_w, dc, dproj):
    def body(x_ref, w_ref, dc_ref, _, dx_ref, dw_ref):
        _, vjp = jax.vjp(_dnconv_fn, x_ref[...], _tap_rows(w_ref))
        dx, dw = vjp(dc_ref[...])
        dx_ref[...] = dx
        for k, row in enumerate(dw):
            dw_ref[k:k + 1, :] = row

    return pl.pallas_call(
        body, name="dnconv_bwd", grid=(DN_QKV_BLKS,),
        in_specs=[pl.BlockSpec((SEQ, 128), lambda j: (0, DN_QKV_BLK0 + j)), pl.BlockSpec((4, 128), lambda j: (0, j)),
                  pl.BlockSpec((SEQ, 128), lambda j: (0, j)), pl.BlockSpec(memory_space=pl.ANY)],
        out_specs=[pl.BlockSpec((SEQ, 128), lambda j: (0, DN_QKV_BLK0 + j)), pl.BlockSpec((4, 128), lambda j: (0, j))],
        out_shape=[jax.ShapeDtypeStruct((SEQ, IN_PAD), F32), jax.ShapeDtypeStruct((4, 1536), F32)],
        input_output_aliases={3: 0},
        compiler_params=_cp("parallel"),
    )(proj, conv_w, dc, dproj)


def _ffact_fn(pg, pu, wg, wu, bg, bu):
    return _gelu_tanh(_causal_conv(pg, wg) + bg) * (_causal_conv(pu, wu) + bu)


def _ffact_args(p_ref, w_ref, b_ref):
    g, u = slice(0, 128), slice(128, 256)
    return (p_ref[:, g], p_ref[:, u], _tap_rows(w_ref, g), _tap_rows(w_ref, u), b_ref[:, g], b_ref[:, u])


def _ffact_fwd(pre, conv_w, conv_b):
    def body(p_ref, w_ref, b_ref, o_ref):
        o_ref[...] = _ffact_fn(*_ffact_args(p_ref, w_ref, b_ref)).astype(BF16)

    return pl.pallas_call(
        body, name="ffact_fwd", grid=(FF_BLKS,),
        in_specs=[pl.BlockSpec((SEQ, 256), lambda j: (0, j)), pl.BlockSpec((3, 256), lambda j: (0, j)),
                  pl.BlockSpec((1, 256), lambda j: (0, j))],
        out_specs=pl.BlockSpec((SEQ, 128), lambda j: (0, j)),
        out_shape=jax.ShapeDtypeStruct((SEQ, D_FF), BF16),
        compiler_params=_cp("parallel"),
    )(pre, conv_w, conv_b)


def _ffact_bwd(pre, conv_w, conv_b, dact, exchanges=()):
    def body(p_ref, w_ref, b_ref, da_ref, dp_ref, dw_ref, db_ref):
        _, vjp = jax.vjp(_ffact_fn, *_ffact_args(p_ref, w_ref, b_ref))
        dpg, dpu, dwg, dwu, dbg, dbu = vjp(da_ref[...].astype(F32))
        dp_ref[:, 0:128] = dpg
        dp_ref[:, 128:256] = dpu
        for k in range(3):
            dw_ref[k:k + 1, 0:128] = dwg[k]
            dw_ref[k:k + 1, 128:256] = dwu[k]
        db_ref[:, 0:128] = dbg
        db_ref[:, 128:256] = dbu

    return _hosted_call(
        body, name="ffact_bwd", steps=FF_BLKS,
        in_specs=[pl.BlockSpec((SEQ, 256), lambda j: (0, j)), pl.BlockSpec((3, 256), lambda j: (0, j)),
                  pl.BlockSpec((1, 256), lambda j: (0, j)), pl.BlockSpec((SEQ, 128), lambda j: (0, j))],
        out_specs=[pl.BlockSpec((SEQ, 256), lambda j: (0, j)), pl.BlockSpec((3, 256), lambda j: (0, j)),
                   pl.BlockSpec((1, 256), lambda j: (0, j))],
        out_shape=[jax.ShapeDtypeStruct((SEQ, 2 * D_FF), F32), jax.ShapeDtypeStruct((3, 2 * D_FF), F32),
                   jax.ShapeDtypeStruct((1, 2 * D_FF), F32)],
        scratch_shapes=[], operands=(pre, conv_w, conv_b, dact), exchanges=exchanges)


def _interleave_ff(t):
    lead = t.shape[:-1]
    return t.reshape(lead + (2, FF_BLKS, 128)).swapaxes(-3, -2).reshape(lead + (2 * D_FF,))


def _deinterleave_ff(t):
    lead = t.shape[:-1]
    return t.reshape(lead + (FF_BLKS, 2, 128)).swapaxes(-3, -2).reshape(lead + (2 * D_FF,))


def _rope_tables():
    inv = 1.0 / (ROPE_THETA ** (jnp.arange(0, HEAD_DIM, 2, dtype=F32) / HEAD_DIM))
    ang = jnp.arange(SEQ, dtype=F32)[:, None] * inv[None, :]
    cos = jnp.tile(jnp.cos(ang), (1, 4))
    sin = jnp.tile(jnp.sin(ang), (1, 4))
    sign = jnp.where((jnp.arange(128) % HEAD_DIM) < HEAD_DIM // 2, -1.0, 1.0).astype(F32)
    return cos, sin * sign[None, :]


def _rope(x, cos, sin_signed):
    lane = lax.broadcasted_iota(jnp.int32, x.shape, 1)
    partner = jnp.where((lane % HEAD_DIM) < HEAD_DIM // 2, pltpu.roll(x, 128 - HEAD_DIM // 2, 1),
                        pltpu.roll(x, HEAD_DIM // 2, 1))
    return x * cos + partner * sin_signed


def _pairs_from_qkv(t):
    lead = t.shape[:-1]
    return t.reshape(lead + (3, N_PAIR, 128)).swapaxes(-3, -2).reshape(lead + (QKV_W,))


def _qkv_from_pairs(t):
    lead = t.shape[:-1]
    return t.reshape(lead + (N_PAIR, 3, 128)).swapaxes(-3, -2).reshape(lead + (QKV_W,))


def _head_masks():
    lane = lax.broadcasted_iota(jnp.int32, (1, 128), 1)
    return [(lane // HEAD_DIM) == h for h in range(2)]


def _both_heads(x):
    return jnp.concatenate([jnp.where(hm, x, 0.0)[None] for hm in _head_masks()], axis=0)


def _block_keys(branch, k_s, v_s, rows, prows, has_prev):
    a = lax.broadcasted_iota(jnp.int32, (ATTN_BLK, ATTN_BLK), 0)
    c = lax.broadcasted_iota(jnp.int32, (ATTN_BLK, ATTN_BLK), 1)
    keys, values, mask = k_s[rows, :], v_s[rows, :], c <= a
    if SEGMENT_BLOCKS[branch] > 1:
        keys = jnp.concatenate([k_s[prows, :], keys], axis=0)
        values = jnp.concatenate([v_s[prows, :], values], axis=0)
        mask = jnp.concatenate([(c >= a) & has_prev, mask], axis=1)
    twice = lambda t: jnp.broadcast_to(t[None], (2,) + t.shape)
    return twice(keys), twice(values), mask


def _block_rows(branch, t):
    d, per_seg = DILATIONS[branch], SEGMENT_BLOCKS[branch]
    if d == 1:
        start = pl.multiple_of(t * ATTN_BLK, ATTN_BLK)
        prev = pl.multiple_of(jnp.maximum(t - 1, 0) * ATTN_BLK, ATTN_BLK)
        return pl.ds(start, ATTN_BLK), pl.ds(prev, ATTN_BLK), t > 0
    r, n = t // per_seg, t % per_seg
    start = n * (ATTN_BLK * d) + r
    prev = jnp.maximum(n - 1, 0) * (ATTN_BLK * d) + r
    return pl.ds(start, ATTN_BLK, stride=d), pl.ds(prev, ATTN_BLK, stride=d), n > 0


def _attn_fwd(proj, cos, sin_signed, exchanges=()):
    scale = HEAD_DIM ** -0.5

    def body(qkv_ref, cos_ref, sin_ref, out_ref, lse_ref, q_s, k_s, v_s, *branch_s):
        o_s, l_s = branch_s[:3], branch_s[3:]
        q_s[...] = _rope(qkv_ref[:, 0:128], cos_ref[...], sin_ref[...])
        k_s[...] = _rope(qkv_ref[:, 128:256], cos_ref[...], sin_ref[...])
        v_s[...] = qkv_ref[:, 256:384]
        heads = _head_masks()
        for branch in range(3):
            def block(t, carry, branch=branch):
                rows, prows, has_prev = _block_rows(branch, t)
                keys, values, mask = _block_keys(branch, k_s, v_s, rows, prows, has_prev)
                s = jnp.where(mask, BMM_NT(_both_heads(q_s[rows, :]), keys) * scale, NEG)
                m = jnp.max(s, axis=2, keepdims=True)
                e = jnp.exp(s - m)
                l = jnp.sum(e, axis=2, keepdims=True)
                o = BMM(e, values) / l
                lse_b = m + jnp.log(l)
                o_s[branch][rows, :] = jnp.where(heads[0], o[0], o[1])
                l_s[branch][rows, :] = jnp.where(heads[0], lse_b[0], lse_b[1])
                return carry

            lax.fori_loop(0, N_BLK, block, 0)
        l0, l1, l2 = l_s[0][...], l_s[1][...], l_s[2][...]
        m = jnp.maximum(jnp.maximum(l0, l1), l2)
        w0, w1, w2 = jnp.exp(l0 - m), jnp.exp(l1 - m), jnp.exp(l2 - m)
        den = w0 + w1 + w2
        out_ref[...] = (w0 * o_s[0][...] + w1 * o_s[1][...] + w2 * o_s[2][...]) / den
        lse_ref[...] = m + jnp.log(den)

    tab = pl.BlockSpec((SEQ, 128), lambda j: (0, 0))
    col = pl.BlockSpec((SEQ, 128), lambda j: (0, j))
    return _hosted_call(
        body, name="attn_fwd", steps=N_PAIR,
        in_specs=[pl.BlockSpec((SEQ, 384), lambda j: (0, j)), tab, tab],
        out_specs=[col, col],
        out_shape=[jax.ShapeDtypeStruct((SEQ, 2 * ATTN_W), F32), jax.ShapeDtypeStruct((SEQ, ATTN_W), F32)],
        scratch_shapes=[pltpu.VMEM((SEQ, 128), F32)] * 9,
        operands=(proj, cos, sin_signed), exchanges=exchanges)


def _attn_bwd(proj, cos, sin_signed, cat, lse, dcat, dproj, exchanges=()):
    scale = HEAD_DIM ** -0.5

    def body(qkv_ref, cos_ref, sin_ref, out_ref, lse_ref, do_ref, _, dqkv_ref, q_s, k_s, v_s, dq_s, dk_s, dv_s,
             dod_s):
        q_s[...] = _rope(qkv_ref[:, 0:128], cos_ref[...], sin_ref[...])
        k_s[...] = _rope(qkv_ref[:, 128:256], cos_ref[...], sin_ref[...])
        v_s[...] = qkv_ref[:, 256:384]
        dq_s[...] = jnp.zeros_like(dq_s)
        dk_s[...] = jnp.zeros_like(dk_s)
        dv_s[...] = jnp.zeros_like(dv_s)
        dod_s[...] = do_ref[...] * out_ref[...]
        heads = _head_masks()
        for branch in range(3):
            def block(t, carry, branch=branch):
                rows, prows, has_prev = _block_rows(branch, t)
                keys, values, mask = _block_keys(branch, k_s, v_s, rows, prows, has_prev)
                q2, do2 = _both_heads(q_s[rows, :]), _both_heads(do_ref[rows, :])
                lse_b, dod = lse_ref[rows, :], dod_s[rows, :]
                lse2 = jnp.concatenate(
                    [jnp.max(jnp.where(hm, lse_b, NEG), axis=1, keepdims=True)[None] for hm in heads], axis=0)
                delta = jnp.concatenate(
                    [jnp.sum(jnp.where(hm, dod, 0.0), axis=1, keepdims=True)[None] for hm in heads], axis=0)
                p = jnp.exp(jnp.where(mask, BMM_NT(q2, keys) * scale, NEG) - lse2)
                ds = p * (BMM_NT(do2, values) - delta) * scale
                dq = BMM(ds, keys)
                dk = BMM_TN(ds, q2)
                dv = BMM_TN(p, do2)
                dk, dv = dk[0] + dk[1], dv[0] + dv[1]
                dq_s[rows, :] += jnp.where(heads[0], dq[0], dq[1])
                if SEGMENT_BLOCKS[branch] > 1:
                    dk_s[rows, :] += dk[ATTN_BLK:]
                    dv_s[rows, :] += dv[ATTN_BLK:]

                    @pl.when(has_prev)
                    def _():
                        dk_s[prows, :] += dk[:ATTN_BLK]
                        dv_s[prows, :] += dv[:ATTN_BLK]
                else:
                    dk_s[rows, :] += dk
                    dv_s[rows, :] += dv
                return carry

            lax.fori_loop(0, N_BLK, block, 0)
        dqkv_ref[:, 0:128] = _rope(dq_s[...], cos_ref[...], -sin_ref[...])
        dqkv_ref[:, 128:256] = _rope(dk_s[...], cos_ref[...], -sin_ref[...])
        dqkv_ref[:, 256:384] = dv_s[...]

    tab = pl.BlockSpec((SEQ, 128), lambda j: (0, 0))
    col = pl.BlockSpec((SEQ, 128), lambda j: (0, j))
    qkv = pl.BlockSpec((SEQ, 384), lambda j: (0, j))
    (dproj,), results = _hosted_call(
        body, name="attn_bwd", steps=N_PAIR,
        in_specs=[qkv, tab, tab, col, col, col, pl.BlockSpec(memory_space=pl.ANY)],
        out_specs=[qkv],
        out_shape=[jax.ShapeDtypeStruct((SEQ, IN_PAD), F32)],
        scratch_shapes=[pltpu.VMEM((SEQ, 128), F32)] * 7,
        operands=(proj, cos, sin_signed, cat, lse, dcat, dproj), exchanges=exchanges, aliases={6: 0})
    return dproj, results


def _bdot(a, b, dims, precision=None):
    if precision is None:
        a = a.astype(BF16)
        b = b.astype(BF16)
    return lax.dot_general(a, b, (dims, ((0,), (0,))), preferred_element_type=F32, precision=precision)


def _make_bmm(precision):
    @jax.custom_vjp
    def nn(a, b):
        return _bdot(a, b, ((2,), (1,)), precision)

    @jax.custom_vjp
    def nt(a, b):
        return _bdot(a, b, ((2,), (2,)), precision)

    @jax.custom_vjp
    def tn(a, b):
        return _bdot(a, b, ((1,), (1,)), precision)

    nn.defvjp(lambda a, b: (nn(a, b), (a, b)), lambda r, g: (nt(g, r[1]), tn(r[0], g)))
    nt.defvjp(lambda a, b: (nt(a, b), (a, b)), lambda r, g: (nn(g, r[1]), tn(g, r[0])))
    tn.defvjp(lambda a, b: (tn(a, b), (a, b)), lambda r, g: (nt(r[1], g), nn(r[0], g)))
    return nn, nt, tn


BMM, BMM_NT, BMM_TN = _make_bmm(None)
BMMH, _, _ = _make_bmm(HI)
BMM3, _, _ = _make_bmm(lax.Precision.HIGH)


def _head_lanes(t, off):
    lane = lax.broadcasted_iota(jnp.int32, (1, 128), 1)
    return jnp.concatenate(
        [jnp.sum(t * (lane == off + h).astype(F32), axis=1, keepdims=True)[None] for h in range(NDH)], axis=0)


def _delta_chunk(qr, kr, vr, z, tail, alog_row, dt_row, nw, state):
    c = qr.shape[1]
    beta = _sigmoid(_head_lanes(tail, 0))
    g = -jnp.exp(_head_lanes(alog_row, 0)) * _softplus(_head_lanes(tail, NDH) + _head_lanes(dt_row, 0))

    q = qr * lax.rsqrt(jnp.sum(qr * qr, axis=2, keepdims=True) + EPS) * (128 ** -0.5)
    k = kr * lax.rsqrt(jnp.sum(kr * kr, axis=2, keepdims=True) + EPS)

    ri = lax.broadcasted_iota(jnp.int32, (c, c), 0)
    ci = lax.broadcasted_iota(jnp.int32, (c, c), 1)
    tril = ri >= ci
    eye = (ri == ci).astype(F32)
    lane = lax.broadcasted_iota(jnp.int32, (1, 128), 1)
    g_lanes = sum(g[h] * (lane == h).astype(F32) for h in range(NDH))
    gc = _head_lanes(MMH(tril.astype(F32), g_lanes), 0)
    g_row = BMMH(jnp.ones((NDH, c, c), F32), eye * gc)
    decay = jnp.where(tril, jnp.exp(jnp.where(tril, gc - g_row, 0.0)), 0.0)
    kb = k * beta
    a_mat = jnp.where(ri > ci, BMM_NT(kb, k) * decay, 0.0)
    power = -a_mat
    t_inv = eye + power
    for _ in range(5):
        power = BMM3(power, power)
        t_inv = t_inv + BMM3(t_inv, power)
    eg = jnp.exp(gc)
    u = BMM(t_inv, vr * beta)
    w = BMM(t_inv, kb * eg)
    qk = BMM_NT(q, k) * decay
    g_tot = jnp.sum(g, axis=1, keepdims=True)
    v_new = u - BMM(w, state)
    o = BMM(q * eg, state) + BMM(qk, v_new)
    new_state = state * jnp.exp(g_tot) + BMM_TN(k * jnp.exp(g_tot - gc), v_new)
    on = o * lax.rsqrt(jnp.mean(o * o, axis=2, keepdims=True) + EPS) * nw
    return on * _silu(z), new_state


def _heads(v, off=0):
    return jnp.concatenate([v[None, :, off + 128 * h:off + 128 * (h + 1)] for h in range(NDH)], axis=0)


def _unheads(t):
    return jnp.concatenate([t[h] for h in range(NDH)], axis=1)


def _delta_fwd(c_qkv, proj, alog_row, dt_row, nw, cat, exchanges=()):
    def body(c_ref, z_ref, tail_ref, al_ref, dt_ref, nw_ref, _, y_ref, st_ref, state):
        @pl.when(pl.program_id(0) == 0)
        def _():
            state[...] = jnp.zeros_like(state)

        cv = c_ref[...]
        st_ref[0] = state[...]
        y, new_state = _delta_chunk(_heads(cv), _heads(cv, 512), _heads(cv, 1024), _heads(z_ref[...]), tail_ref[...],
                                    al_ref[...], dt_ref[...], nw_ref[...], state[...])
        y_ref[...] = _unheads(y)
        state[...] = new_state

    row = pl.BlockSpec((1, 128), lambda n: (0, 0))
    return _hosted_call(
        body, name="delta_fwd", steps=NCH,
        in_specs=[pl.BlockSpec((CH, 1536), lambda n: (n, 0)), pl.BlockSpec((CH, 512), lambda n: (n, DN_Z_COL // 512)),
                  pl.BlockSpec((CH, 128), lambda n: (n, DN_TAIL_BLK)), row, row, row, pl.BlockSpec(memory_space=pl.ANY)],
        out_specs=[pl.BlockSpec((CH, 512), lambda n: (n, 1)),
                   pl.BlockSpec((1, NDH, 128, 128), lambda n: (n, 0, 0, 0))],
        out_shape=[jax.ShapeDtypeStruct((SEQ, 2 * ATTN_W), F32), jax.ShapeDtypeStruct((NCH, NDH, 128, 128), F32)],
        scratch_shapes=[pltpu.VMEM((NDH, 128, 128), F32)],
        operands=(c_qkv, proj, proj, alog_row, dt_row, nw, cat), exchanges=exchanges, aliases={6: 0})


def _delta_bwd(c_qkv, proj, alog_row, dt_row, nw, states, dcat, exchanges=()):
    def body(c_ref, z_ref, tail_ref, al_ref, dt_ref, nw_ref, st_ref, dy_ref,
             dp_ref, dc_ref, dal_ref, ddt_ref, dnw_ref, dstate):
        @pl.when(pl.program_id(0) == 0)
        def _():
            dstate[...] = jnp.zeros_like(dstate)
            dal_ref[...] = jnp.zeros_like(dal_ref)
            ddt_ref[...] = jnp.zeros_like(ddt_ref)
            dnw_ref[...] = jnp.zeros_like(dnw_ref)

        cv = c_ref[...]
        _, vjp = jax.vjp(_delta_chunk, _heads(cv), _heads(cv, 512), _heads(cv, 1024), _heads(z_ref[...]),
                         tail_ref[...], al_ref[...], dt_ref[...], nw_ref[...], st_ref[0])
        dq, dk, dv, dz, dtail, dal, ddt, dnw, dst = vjp((_heads(dy_ref[...]), dstate[...]))
        dstate[...] = dst
        dc_ref[...] = jnp.concatenate([_unheads(dq), _unheads(dk), _unheads(dv)], axis=1)
        dp_ref[...] = jnp.concatenate([_unheads(dz), dtail, jnp.zeros((CH, 128), F32)], axis=1)
        dal_ref[...] += dal
        ddt_ref[...] += ddt
        dnw_ref[...] += dnw

    rev = lambda n: NCH - 1 - n
    row = pl.BlockSpec((1, 128), lambda n: (0, 0))
    return _hosted_call(
        body, name="delta_bwd", steps=NCH,
        in_specs=[pl.BlockSpec((CH, 1536), lambda n: (rev(n), 0)),
                  pl.BlockSpec((CH, 512), lambda n: (rev(n), DN_Z_COL // 512)),
                  pl.BlockSpec((CH, 128), lambda n: (rev(n), DN_TAIL_BLK)), row, row, row,
                  pl.BlockSpec((1, NDH, 128, 128), lambda n: (rev(n), 0, 0, 0)),
                  pl.BlockSpec((CH, 512), lambda n: (rev(n), 1))],
        out_specs=[pl.BlockSpec((CH, 768), lambda n: (rev(n), DN_Z_COL // 768)),
                   pl.BlockSpec((CH, 1536), lambda n: (rev(n), 0)), row, row, row],
        out_shape=[jax.ShapeDtypeStruct((SEQ, IN_PAD), F32), jax.ShapeDtypeStruct((SEQ, 1536), F32)]
        + [jax.ShapeDtypeStruct((1, 128), F32)] * 3,
        scratch_shapes=[pltpu.VMEM((NDH, 128, 128), F32)],
        operands=(c_qkv, proj, proj, alog_row, dt_row, nw, states, dcat), exchanges=exchanges)


def _place():
    x, y, c = lax.axis_index("x"), lax.axis_index("y"), lax.axis_index("c")
    other_chips = [(1 - x, y), (x, 1 - y), (1 - x, 1 - y)]
    return x, y, c, other_chips


def _gather_exchange(shards):
    n = len(shards)

    def copies(ins, outs, sems):
        send_sems, recv_sems, local_sems = sems
        x, y, c, chips = _place()
        me, sibling = (x, y, c), (x, y, 1 - c)

        def copy(b, k, block, to, src=None):
            slot = outs[b].at[4 * block[0] + 2 * block[1] + block[2]]
            return pltpu.make_async_remote_copy(
                src_ref=slot if src is None else src, dst_ref=slot,
                send_sem=send_sems.at[b, k], recv_sem=recv_sems.at[b, k], device_id=to, device_id_type=MESH)

        mine = [pltpu.make_async_copy(ins[b], outs[b].at[4 * x + 2 * y + c], local_sems.at[b]) for b in range(n)]
        first = []
        for b in range(n):
            first.append(copy(b, 0, me, sibling, src=ins[b]))
            first += [copy(b, 1 + j, me, (*chip, c), src=ins[b]) for j, chip in enumerate(chips)]
        over_ici = [copy(b, 1 + j, (*chip, c), me) for b in range(n) for j, chip in enumerate(chips)]
        passed = [copy(b, 4 + j, (*chip, c), sibling) for b in range(n) for j, chip in enumerate(chips)]
        from_sibling = []
        for b in range(n):
            from_sibling.append(copy(b, 0, sibling, me))
            from_sibling += [copy(b, 4 + j, (*chip, 1 - c), me) for j, chip in enumerate(chips)]
        return mine, first, over_ici, passed, from_sibling

    def start(ins, outs, sems):
        mine, first, _, _, _ = copies(ins, outs, sems)
        for cp in mine + first:
            cp.start()

    def middle(ins, outs, sems):
        _, _, over_ici, passed, _ = copies(ins, outs, sems)
        for arrived, onward in zip(over_ici, passed):
            arrived.wait_recv()
            onward.start()

    def finish(ins, outs, sems):
        mine, first, _, passed, from_sibling = copies(ins, outs, sems)
        for cp in from_sibling:
            cp.wait_recv()
        for cp in first + passed:
            cp.wait_send()
        for cp in mine:
            cp.wait()

    return Exchange(shards, [jax.ShapeDtypeStruct((N_DEV,) + s.shape, s.dtype) for s in shards],
                    [pltpu.SemaphoreType.DMA((n, 7)), pltpu.SemaphoreType.DMA((n, 7)), pltpu.SemaphoreType.DMA((n,))],
                    start, middle, finish)


def _sibling_exchange(gs):
    n = len(gs)

    def copies(ins, outs, sems):
        send_sems, recv_sems = sems
        x, y, c, _ = _place()
        return [pltpu.make_async_remote_copy(
            src_ref=ins[b].at[2 * p + (1 - c)], dst_ref=outs[b].at[p],
            send_sem=send_sems.at[b, p], recv_sem=recv_sems.at[b, p],
            device_id=(x, y, 1 - c), device_id_type=MESH) for b in range(n) for p in range(4)]

    def start(ins, outs, sems):
        for cp in copies(ins, outs, sems):
            cp.start()

    def finish(ins, outs, sems):
        for cp in copies(ins, outs, sems):
            cp.wait()

    return Exchange(gs, [jax.ShapeDtypeStruct((4,) + g.shape[1:], g.dtype) for g in gs],
                    [pltpu.SemaphoreType.DMA((n, 4)), pltpu.SemaphoreType.DMA((n, 4))], start, None, finish)


def _chips_exchange(hs):
    n = len(hs)

    def copies(ins, outs, sems):
        send_sems, recv_sems, local_sems = sems
        x, y, c, chips = _place()
        my_chip = 2 * x + y
        local = [pltpu.make_async_copy(ins[b].at[my_chip], outs[b].at[my_chip], local_sems.at[b]) for b in range(n)]
        sends, arrivals = [], []
        for b in range(n):
            for k, (px, py) in enumerate(chips):
                peer = 2 * px + py
                sends.append(pltpu.make_async_remote_copy(
                    src_ref=ins[b].at[peer], dst_ref=outs[b].at[my_chip],
                    send_sem=send_sems.at[b, k], recv_sem=recv_sems.at[b, k],
                    device_id=(px, py, c), device_id_type=MESH))
                arrivals.append(pltpu.make_async_remote_copy(
                    src_ref=ins[b].at[peer], dst_ref=outs[b].at[peer],
                    send_sem=send_sems.at[b, k], recv_sem=recv_sems.at[b, k],
                    device_id=(px, py, c), device_id_type=MESH))
        return local, sends, arrivals

    def start(ins, outs, sems):
        local, sends, _ = copies(ins, outs, sems)
        for cp in local + sends:
            cp.start()

    def finish(ins, outs, sems):
        local, sends, arrivals = copies(ins, outs, sems)
        for cp in arrivals:
            cp.wait_recv()
        for cp in sends:
            cp.wait_send()
        for cp in local:
            cp.wait()

    return Exchange(hs, [jax.ShapeDtypeStruct(h.shape, h.dtype) for h in hs],
                    [pltpu.SemaphoreType.DMA((n, 3)), pltpu.SemaphoreType.DMA((n, 3)), pltpu.SemaphoreType.DMA((n,))],
                    start, None, finish)


def _run_exchange(exchange, name):
    n_in, n_out = len(exchange.operands), len(exchange.out_shapes)

    def body(*refs):
        ins, outs, sems = refs[:n_in], refs[n_in:n_in + n_out], refs[n_in + n_out:]
        exchange.start(ins, outs, sems)
        if exchange.middle is not None:
            exchange.middle(ins, outs, sems)
        exchange.finish(ins, outs, sems)

    return pl.pallas_call(
        body, name=name,
        in_specs=[HBM_SPEC] * n_in, out_specs=[HBM_SPEC] * n_out,
        out_shape=exchange.out_shapes, scratch_shapes=exchange.sems,
    )(*exchange.operands)


def _pair_add(g, r, core, name):
    _, nr, nc = g.shape
    tr = nr // 2 if nr % 32 == 0 else nr

    def body(core_ref, g_ref, r_ref, o_ref):
        o_ref[...] = (g_ref[...] + r_ref[...]).astype(BF16)

    return pl.pallas_call(
        body, name=name,
        grid_spec=pltpu.PrefetchScalarGridSpec(
            num_scalar_prefetch=1, grid=(4, nr // tr),
            in_specs=[pl.BlockSpec((1, tr, nc), lambda p, i, core: (2 * p + core[0], i, 0)),
                      pl.BlockSpec((1, tr, nc), lambda p, i, core: (p, i, 0))],
            out_specs=pl.BlockSpec((1, tr, nc), lambda p, i, core: (p, i, 0))),
        out_shape=jax.ShapeDtypeStruct(r.shape, BF16),
        compiler_params=_cp("parallel", "parallel"),
    )(core, g, r)


def _all_gather_sum_small(v):
    rows = v.shape[0]

    def body(x_ref, sum_ref, out_ref, send_sems, recv_sems, local_sem):
        x, y, c, chips = _place()
        me, sibling = (x, y, c), (x, y, 1 - c)

        def block(px, py, pc):
            return out_ref.at[pl.ds((4 * px + 2 * py + pc) * rows, rows), :]

        def copy(k, blk, to, src=None):
            return pltpu.make_async_remote_copy(
                src_ref=block(*blk) if src is None else src, dst_ref=block(*blk),
                send_sem=send_sems.at[k], recv_sem=recv_sems.at[k], device_id=to, device_id_type=MESH)

        mine = pltpu.make_async_copy(x_ref, block(*me), local_sem)
        mine.start()
        first = [copy(0, me, sibling, src=x_ref)]
        first += [copy(1 + j, me, (*chip, c), src=x_ref) for j, chip in enumerate(chips)]
        for cp in first:
            cp.start()
        passed = [copy(4 + j, (*chip, c), sibling) for j, chip in enumerate(chips)]
        for j, chip in enumerate(chips):
            copy(1 + j, (*chip, c), me).wait_recv()
            passed[j].start()
        copy(0, sibling, me).wait_recv()
        for j, chip in enumerate(chips):
            copy(4 + j, (*chip, 1 - c), me).wait_recv()
        for cp in first + passed:
            cp.wait_send()
        mine.wait()
        total = out_ref[pl.ds(0, rows), :]
        for d in range(1, N_DEV):
            total = total + out_ref[pl.ds(d * rows, rows), :]
        sum_ref[...] = total

    vm = pl.BlockSpec(memory_space=pltpu.VMEM)
    return pl.pallas_call(
        body, name="small_all_reduce",
        in_specs=[vm], out_specs=[vm],
        out_shape=[jax.ShapeDtypeStruct((rows, 128), F32)],
        scratch_shapes=[pltpu.VMEM((N_DEV * rows, 128), F32), pltpu.SemaphoreType.DMA((7,)),
                        pltpu.SemaphoreType.DMA((7,)), pltpu.SemaphoreType.DMA],
    )(v)[0]


def _adamw(w, g, m, v):
    m = ADAM_B1 * m + (1.0 - ADAM_B1) * g
    v = ADAM_B2 * v + (1.0 - ADAM_B2) * (g * g)
    m_hat = m / (1.0 - ADAM_B1 ** ADAM_STEP)
    v_hat = v / (1.0 - ADAM_B2 ** ADAM_STEP)
    delta = -ADAM_LR * (m_hat / (jnp.sqrt(v_hat) + ADAM_EPS) + ADAM_WD * w)
    return delta, m, v


ADAM_ROWS = dict(w_in=256, w_out=128, ffn_w_in=256, ffn_w_out=176)


def _sum_chips(p):
    p = p.astype(F32)
    return (p[0] + p[1]) + (p[2] + p[3])


def _sum_parts(parts, name):
    def body(p_ref, g_ref):
        g_ref[...] = _sum_chips(p_ref[...])

    return pl.pallas_call(body, name=name, out_shape=jax.ShapeDtypeStruct(parts.shape[1:], F32),
                          compiler_params=_cp())(parts)


def _adamw_sharded(parts, w, m, v, tr, name):
    nl, nr, nc = w.shape
    n_parts = parts[0].shape[0]

    def body(*refs):
        p_refs, (w_ref, m_ref, v_ref, g_ref, d_ref, nm_ref, nv_ref) = refs[:nl], refs[nl:]
        layer = pl.program_id(0)
        p = p_refs[0][...]
        for l in range(1, nl):
            p = jnp.where(layer == l, p_refs[l][...], p)
        g = _sum_chips(p) if n_parts == 4 else p[0]
        delta, nm, nv = _adamw(w_ref[0], g, m_ref[0], v_ref[0])
        g_ref[0] = g
        d_ref[0] = delta
        nm_ref[0] = nm
        nv_ref[0] = nv

    blk = pl.BlockSpec((1, tr, nc), lambda l, i: (l, i, 0))
    return pl.pallas_call(
        body, name=name, grid=(nl, nr // tr),
        in_specs=[pl.BlockSpec((n_parts, tr, nc), lambda l, i: (0, i, 0))] * nl + [blk, blk, blk],
        out_specs=[blk] * 4,
        out_shape=[jax.ShapeDtypeStruct(w.shape, F32)] * 4,
        compiler_params=_cp("parallel", "parallel"),
    )(*parts, w, m, v)


def _adamw_small(g, w, m, v):
    def body(g_ref, w_ref, m_ref, v_ref, d_ref, nm_ref, nv_ref):
        delta, nm, nv = _adamw(w_ref[...], g_ref[...], m_ref[...], v_ref[...])
        d_ref[...] = delta
        nm_ref[...] = nm
        nv_ref[...] = nv

    return pl.pallas_call(
        body, name="adamw_small",
        out_shape=[jax.ShapeDtypeStruct(g.shape, F32)] * 3,
    )(g, w, m, v)


def _pack(arrays, rows):
    flat = jnp.concatenate([a.reshape(-1).astype(F32) for a in arrays])
    return jnp.pad(flat, (0, rows * 128 - flat.shape[0])).reshape(rows, 128)


def _unpack(packed, shapes):
    flat = packed.reshape(-1)
    out, off = [], 0
    for s in shapes:
        n = math.prod(s)
        out.append(flat[off:off + n].reshape(s))
        off += n
    return out


def _row(v, width=None):
    v = v.reshape(1, -1)
    return v if width is None else jnp.pad(v, ((0, 0), (0, width - v.shape[1])))


def _layer_fwd(x, wts, tables, attn_exchanges=(), delta_exchanges=(), on_attn=None, on_delta=None):
    h = _norm_fwd(x, wts["norm_pre_mix"], "norm_pre_mix")
    proj = _matmul(h, wts["w_in"], tb=True, tm=512, tn=768, tk=1024, name="mm_proj")
    (cat, lse), got = _attn_fwd(proj, *tables, exchanges=attn_exchanges)
    if on_attn is not None:
        on_attn(got)
    c_qkv = _dnconv_fwd(proj, wts["dn_conv_w"])
    (cat, states), got = _delta_fwd(c_qkv, proj, wts["dn_a_log"], wts["dn_dt_bias"], wts["dn_norm_w"], cat,
                                    exchanges=delta_exchanges)
    if on_delta is not None:
        on_delta(got)
    mix = _matmul(cat, wts["w_out"], tm=512, tn=1024, tk=1024, name="mm_mix")
    x1 = _resnorm_fwd(x, mix, wts["norm_post_mix"], "norm_post_mix")
    h2 = _norm_fwd(x1, wts["norm_pre_ffn"], "norm_pre_ffn")
    pre = _matmul(h2, wts["ffn_w_in"], tb=True, tm=512, tn=512, tk=1024, name="mm_ffn_in")
    act = _ffact_fwd(pre, wts["ffn_conv_w"], wts["ffn_conv_b"])
    f = _matmul(act, wts["ffn_w_out"], tm=512, tn=1024, tk=D_FF, name="mm_ffn_out")
    x2 = _resnorm_fwd(x1, f, wts["norm_post_ffn"], "norm_post_ffn")
    saved = dict(x=x, h=h, proj=proj, lse=lse, c_qkv=c_qkv, states=states, cat=cat, mix=mix, x1=x1, h2=h2, pre=pre,
                 act=act, f=f)
    return x2, saved


def _layer_bwd(dx2, wts, s, tables, ffact_exchanges=(), delta_exchanges=None, attn_exchanges=None):
    g = {}
    df, g["norm_post_ffn"] = _norm_bwd(s["f"], wts["norm_post_ffn"], dx2, None, "norm_post_ffn_bwd")
    dact = _matmul(df, wts["ffn_w_out"], tb=True, tm=512, tn=1408, tk=1024, name="mm_dact", out_dtype=BF16)
    g["ffn_w_out"] = _matmul(s["act"], df, ta=True, tm=1408, tn=512, tk=SEQ, name="mm_dw_ffn_out")
    (dpre, g["ffn_conv_w"], g["ffn_conv_b"]), got = _ffact_bwd(s["pre"], wts["ffn_conv_w"], wts["ffn_conv_b"], dact,
                                                               exchanges=ffact_exchanges)
    dh2 = _matmul(dpre, wts["ffn_w_in"], tm=1024, tn=1024, tk=1408, name="mm_dh2")
    g["ffn_w_in"] = _matmul(dpre, s["h2"], ta=True, tm=512, tn=1024, tk=SEQ, name="mm_dw_ffn_in")
    dx1, g["norm_pre_ffn"] = _norm_bwd(s["x1"], wts["norm_pre_ffn"], dh2, dx2, "norm_pre_ffn_bwd")
    dmix, g["norm_post_mix"] = _norm_bwd(s["mix"], wts["norm_post_mix"], dx1, None, "norm_post_mix_bwd")
    dcat = _matmul(dmix, wts["w_out"], tb=True, tm=512, tn=1024, tk=1024, name="mm_dcat")
    g["w_out"] = _matmul(s["cat"], dmix, ta=True, tm=1024, tn=512, tk=SEQ, name="mm_dw_out")
    (dproj, dc, g["dn_a_log"], g["dn_dt_bias"], g["dn_norm_w"]), got = _delta_bwd(
        s["c_qkv"], s["proj"], wts["dn_a_log"], wts["dn_dt_bias"], wts["dn_norm_w"], s["states"], dcat,
        exchanges=delta_exchanges(g, got) if delta_exchanges is not None else ())
    dproj, got = _attn_bwd(s["proj"], *tables, s["cat"], s["lse"], dcat, dproj,
                           exchanges=attn_exchanges(got) if attn_exchanges is not None else ())
    dproj, g["dn_conv_w"] = _dnconv_bwd(s["proj"], wts["dn_conv_w"], dc, dproj)
    dh = _matmul(dproj, wts["w_in"], tm=1024, tn=1024, tk=1280, name="mm_dh")
    g["w_in"] = _matmul(dproj, s["h"], ta=True, tm=768, tn=1024, tk=SEQ, name="mm_dw_in")
    dx, g["norm_pre_mix"] = _norm_bwd(s["x"], wts["norm_pre_mix"], dh, dx1, "norm_pre_mix_bwd")
    return dx, g, got


BIG = ("w_in", "w_out", "ffn_w_in", "ffn_w_out")
COLUMN_SHARDED = ("w_in", "ffn_w_in")
SMALL_SHARDED = ("dn_conv_w", "ffn_conv_w")
REPLICATED = ("dn_a_log", "dn_dt_bias", "dn_norm_w", "ffn_conv_b", "norm_pre_mix", "norm_post_mix", "norm_pre_ffn",
              "norm_post_ffn")
WEIGHTS = ("w_in", "dn_conv_w", "dn_a_log", "dn_dt_bias", "dn_norm_w", "w_out", "ffn_w_in", "ffn_conv_w", "ffn_conv_b",
           "ffn_w_out", "norm_pre_mix", "norm_post_mix", "norm_pre_ffn", "norm_post_ffn")
FULL_SHAPE = dict(dn_conv_w=(DEPTH, 4, 1536), ffn_conv_w=(DEPTH, 3, 2 * D_FF), dn_a_log=(DEPTH, NDH),
                  dn_dt_bias=(DEPTH, NDH), dn_norm_w=(DEPTH, 128), ffn_conv_b=(DEPTH, 2 * D_FF),
                  norm_pre_mix=(DEPTH, D_MODEL), norm_post_mix=(DEPTH, D_MODEL), norm_pre_ffn=(DEPTH, D_MODEL),
                  norm_post_ffn=(DEPTH, D_MODEL))
SMALL_GRAD_ORDER = REPLICATED + SMALL_SHARDED
SMALL_GRAD_ROWS = 520
SMALL_W_ROWS = 48
SMALL_ADAM_ROWS = 200


def _w_in_rows_to_kernel_order(t):
    qkv = t[:QKV_W].reshape(3, N_PAIR, 128, -1).swapaxes(0, 1).reshape(QKV_W, -1)
    return jnp.pad(jnp.concatenate([qkv, t[QKV_W:]], axis=0), ((0, IN_PAD - IN_COLS), (0, 0)))


def _w_in_rows_from_kernel_order(t):
    qkv = t[:QKV_W].reshape(N_PAIR, 3, 128, -1).swapaxes(0, 1).reshape(QKV_W, -1)
    return jnp.concatenate([qkv, t[QKV_W:IN_COLS]], axis=0)


def _interleave_ff_rows(t):
    return t.reshape(2, FF_BLKS, 128, -1).swapaxes(0, 1).reshape(2 * D_FF, -1)


def _deinterleave_ff_rows(t):
    return t.reshape(FF_BLKS, 2, 128, -1).swapaxes(0, 1).reshape(2 * D_FF, -1)


def kernel(x, w_in, dn_conv_w, dn_a_log, dn_dt_bias, dn_norm_w, w_out, ffn_w_in, ffn_conv_w, ffn_conv_b, ffn_w_out, norm_pre_mix, norm_post_mix, norm_pre_ffn, norm_post_ffn, loss_target, m_w_in, m_dn_conv_w, m_dn_a_log, m_dn_dt_bias, m_dn_norm_w, m_w_out, m_ffn_w_in, m_ffn_conv_w, m_ffn_conv_b, m_ffn_w_out, m_norm_pre_mix, m_norm_post_mix, m_norm_pre_ffn, m_norm_post_ffn, v_w_in, v_dn_conv_w, v_dn_a_log, v_dn_dt_bias, v_dn_norm_w, v_w_out, v_ffn_w_in, v_ffn_conv_w, v_ffn_conv_b, v_ffn_w_out, v_norm_pre_mix, v_norm_post_mix, v_norm_pre_ffn, v_norm_post_ffn):
    local = dict(w_in=w_in, dn_conv_w=dn_conv_w, dn_a_log=dn_a_log, dn_dt_bias=dn_dt_bias, dn_norm_w=dn_norm_w,
                 w_out=w_out, ffn_w_in=ffn_w_in, ffn_conv_w=ffn_conv_w, ffn_conv_b=ffn_conv_b, ffn_w_out=ffn_w_out,
                 norm_pre_mix=norm_pre_mix, norm_post_mix=norm_post_mix, norm_pre_ffn=norm_pre_ffn,
                 norm_post_ffn=norm_post_ffn)
    mom_m = dict(w_in=m_w_in, dn_conv_w=m_dn_conv_w, dn_a_log=m_dn_a_log, dn_dt_bias=m_dn_dt_bias,
                 dn_norm_w=m_dn_norm_w, w_out=m_w_out, ffn_w_in=m_ffn_w_in, ffn_conv_w=m_ffn_conv_w,
                 ffn_conv_b=m_ffn_conv_b, ffn_w_out=m_ffn_w_out, norm_pre_mix=m_norm_pre_mix,
                 norm_post_mix=m_norm_post_mix, norm_pre_ffn=m_norm_pre_ffn, norm_post_ffn=m_norm_post_ffn)
    mom_v = dict(w_in=v_w_in, dn_conv_w=v_dn_conv_w, dn_a_log=v_dn_a_log, dn_dt_bias=v_dn_dt_bias,
                 dn_norm_w=v_dn_norm_w, w_out=v_w_out, ffn_w_in=v_ffn_w_in, ffn_conv_w=v_ffn_conv_w,
                 ffn_conv_b=v_ffn_conv_b, ffn_w_out=v_ffn_w_out, norm_pre_mix=v_norm_pre_mix,
                 norm_post_mix=v_norm_post_mix, norm_pre_ffn=v_norm_pre_ffn, norm_post_ffn=v_norm_post_ffn)
    dev = 4 * lax.axis_index("x") + 2 * lax.axis_index("y") + lax.axis_index("c")
    core = lax.axis_index("c").astype(jnp.int32).reshape(1)

    def shard(n, l):
        s = local[n][l].astype(BF16)
        return s.T if n in COLUMN_SHARDED else s

    def matrix(n, gathered):
        if n == "w_in":
            return _w_in_rows_to_kernel_order(gathered.reshape(IN_COLS, D_MODEL))
        if n == "ffn_w_in":
            return _interleave_ff_rows(gathered.reshape(2 * D_FF, D_MODEL))
        return gathered.reshape(-1, D_MODEL)

    small_w = _pack([dn_conv_w, ffn_conv_w], SMALL_W_ROWS)
    g_w_in0, g_small = _run_exchange(_gather_exchange([shard("w_in", 0), small_w]), "weights_all_gather")
    n_dn, n_ff = DEPTH * 4 * 192, DEPTH * 3 * 704
    sm = g_small.reshape(N_DEV, -1)
    full_dn_conv = sm[:, :n_dn].reshape(N_DEV, DEPTH, 4, 192).transpose(1, 2, 0, 3).reshape(DEPTH, 4, 1536)
    full_ff_conv = _interleave_ff(
        sm[:, n_dn:n_dn + n_ff].reshape(N_DEV, DEPTH, 3, 704).transpose(1, 2, 0, 3).reshape(DEPTH, 3, 2 * D_FF))

    def small_weights(l):
        wts = dict(dn_conv_w=full_dn_conv[l], ffn_conv_w=full_ff_conv[l], ffn_conv_b=_interleave_ff(_row(ffn_conv_b[l])),
                   dn_a_log=_row(dn_a_log[l], 128), dn_dt_bias=_row(dn_dt_bias[l], 128))
        for n in ("dn_norm_w", "norm_pre_mix", "norm_post_mix", "norm_pre_ffn", "norm_post_ffn"):
            wts[n] = _row(local[n][l])
        return wts

    weights = [small_weights(l) for l in range(DEPTH)]
    weights[0]["w_in"] = matrix("w_in", g_w_in0)

    def gather_behind(wanted):
        def deliver(got):
            for (n, l), g in zip(wanted, got[0]):
                weights[l][n] = matrix(n, g)

        return [_gather_exchange([shard(n, l) for n, l in wanted])], deliver

    tables = _rope_tables()
    ex_attn0, on_attn0 = gather_behind([("w_out", 0), ("ffn_w_in", 0), ("ffn_w_out", 0)])
    ex_delta0, on_delta0 = gather_behind([("w_in", 1), ("w_out", 1)])
    ex_attn1, on_attn1 = gather_behind([("ffn_w_in", 1), ("ffn_w_out", 1)])
    act, saved0 = _layer_fwd(x[0], weights[0], tables, ex_attn0, ex_delta0, on_attn0, on_delta0)
    act, saved1 = _layer_fwd(act, weights[1], tables, ex_attn1, (), on_attn1, None)
    loss_part, dact = _loss_fwd_bwd(act, loss_target[0])

    def to_devices(name, t):
        if name == "w_in":
            t = _w_in_rows_from_kernel_order(t)
        if name == "ffn_w_in":
            t = _deinterleave_ff_rows(t)
        return t.reshape(N_DEV, t.shape[0] // N_DEV, t.shape[1])

    def pair_sums(names, layer, to_dev, from_sibling):
        return [_pair_add(gd, r, core, "grads_pair_add_%s_%d" % (n, layer))
                for n, gd, r in zip(names, to_dev, from_sibling)]

    grads = [None] * DEPTH
    dact, grads[1], _ = _layer_bwd(dact, weights[1], saved1, tables)
    to_dev1 = [to_devices(n, grads[1][n]) for n in BIG]
    early = ("w_out", "ffn_w_in", "ffn_w_out")
    parts, stash = {}, {}

    def delta_exchanges(g, got_ffact):
        stash["to_dev0"] = [to_devices(n, g[n]) for n in early]
        return [_chips_exchange(pair_sums(BIG, 1, to_dev1, got_ffact[0])), _sibling_exchange(stash["to_dev0"])]

    def attn_exchanges(got_delta):
        for n, p in zip(BIG, got_delta[0]):
            parts[n, 1] = p
        return [_chips_exchange(pair_sums(early, 0, stash["to_dev0"], got_delta[1]))]

    dact, grads[0], got_attn = _layer_bwd(dact, weights[0], saved0, tables, [_sibling_exchange(to_dev1)],
                                          delta_exchanges, attn_exchanges)
    for n, p in zip(early, got_attn[0]):
        parts[n, 0] = p
    grad_x = dact[None]
    last = [to_devices("w_in", grads[0]["w_in"])]
    from_sibling = _run_exchange(_sibling_exchange(last), "grads_to_sibling")
    parts["w_in", 0], = _run_exchange(_chips_exchange(pair_sums(("w_in",), 0, last, from_sibling)), "grads_to_chips")

    def small_grad(name):
        t = jnp.stack([grads[l][name] for l in range(DEPTH)])
        if name in ("dn_a_log", "dn_dt_bias"):
            t = t[:, 0, :NDH]
        if name in ("ffn_conv_w", "ffn_conv_b"):
            t = _deinterleave_ff(t)
        return t.reshape(FULL_SHAPE[name])

    small_part = _pack([small_grad(n) for n in SMALL_GRAD_ORDER] + [loss_part[0, :1]], SMALL_GRAD_ROWS)
    small_sum = _all_gather_sum_small(small_part)
    small_g = dict(zip(SMALL_GRAD_ORDER + ("loss",), _unpack(small_sum, [FULL_SHAPE[n] for n in SMALL_GRAD_ORDER] + [(1,)])))
    loss = small_g["loss"][0]
    small_g["dn_conv_w"] = lax.dynamic_slice_in_dim(small_g["dn_conv_w"], dev * 192, 192, axis=2)
    small_g["ffn_conv_w"] = lax.dynamic_slice_in_dim(small_g["ffn_conv_w"], dev * 704, 704, axis=2)

    out_g, out_d, out_m, out_v = {}, {}, {}, {}
    for n in BIG:
        p = [parts[n, l] for l in range(DEPTH)]
        if n in COLUMN_SHARDED:
            p = [_sum_parts(t, "grad_sum_%s_%d" % (n, l)).T[None] for l, t in enumerate(p)]
        out_g[n], out_d[n], out_m[n], out_v[n] = _adamw_sharded(p, local[n], mom_m[n], mom_v[n], ADAM_ROWS[n], "adamw_" + n)
    shapes = [small_g[n].shape for n in SMALL_GRAD_ORDER]
    d_s, m_s, v_s = _adamw_small(_pack([small_g[n] for n in SMALL_GRAD_ORDER], SMALL_ADAM_ROWS),
                                 _pack([local[n] for n in SMALL_GRAD_ORDER], SMALL_ADAM_ROWS),
                                 _pack([mom_m[n] for n in SMALL_GRAD_ORDER], SMALL_ADAM_ROWS),
                                 _pack([mom_v[n] for n in SMALL_GRAD_ORDER], SMALL_ADAM_ROWS))
    for n, d, m, v in zip(SMALL_GRAD_ORDER, _unpack(d_s, shapes), _unpack(m_s, shapes), _unpack(v_s, shapes)):
        out_g[n], out_d[n], out_m[n], out_v[n] = small_g[n], d, m, v
    return (loss, grad_x, *[out_g[n] for n in WEIGHTS], *[out_d[n] for n in WEIGHTS],
            *[out_m[n] for n in WEIGHTS], *[out_v[n] for n in WEIGHTS])
```

```python
import functools
import math

import jax
import jax.numpy as jnp
from jax import lax
from jax.experimental import pallas as pl
from jax.experimental.pallas import tpu as pltpu

F32 = jnp.float32
BF16 = jnp.bfloat16
HI = lax.Precision.HIGHEST
MESH = pl.DeviceIdType.MESH

N_DEV = 8
SEQ = 2048
D_MODEL = 1024
DEPTH = 2
N_PAIR = 4
HEAD_DIM = 64
ATTN_W = 512
ATTN_BLK = 128
DILATIONS = (1, 4, 16)
SEGMENT_BLOCKS = (16, 4, 1)
N_BLK = SEQ // ATTN_BLK
NDH = 4
CH = 64
NCH = SEQ // CH
IN_COLS = 3592
IN_PAD = 3840
QKV_W = 3 * ATTN_W
DN_QKV_BLK0 = QKV_W // 128
DN_QKV_BLKS = 1536 // 128
DN_Z_COL = 3072
DN_TAIL_BLK = 3584 // 128
D_FF = 2816
FF_BLKS = D_FF // 128
EPS = 1e-6
NEG = -1e30
ROPE_THETA = 10000.0

ADAM_LR, ADAM_B1, ADAM_B2, ADAM_EPS, ADAM_WD, ADAM_STEP = 0.001, 0.9, 0.999, 1e-08, 0.01, 10

VMEM_LIMIT = 56 * 1024 * 1024


def _cp(*sem):
    return pltpu.CompilerParams(dimension_semantics=sem, vmem_limit_bytes=VMEM_LIMIT)


class Exchange:
    def __init__(self, operands, out_shapes, sems, start, middle, finish):
        self.operands, self.out_shapes, self.sems = list(operands), list(out_shapes), list(sems)
        self.start, self.middle, self.finish = start, middle, finish


HBM_SPEC = pl.BlockSpec(memory_space=pltpu.HBM)


def _hosted_call(body, *, name, steps, in_specs, out_specs, out_shape, scratch_shapes, operands, exchanges=(),
                 aliases=None):
    n_in, n_out, n_scr = len(in_specs), len(out_specs), len(scratch_shapes)

    def take(refs, pos, counts):
        groups = []
        for c in counts:
            groups.append(refs[pos:pos + c])
            pos += c
        return groups, pos

    def full_body(*refs):
        ins, pos = refs[:n_in], n_in
        ex_ins, pos = take(refs, pos, [len(e.operands) for e in exchanges])
        outs, pos = refs[pos:pos + n_out], pos + n_out
        ex_outs, pos = take(refs, pos, [len(e.out_shapes) for e in exchanges])
        scr, pos = refs[pos:pos + n_scr], pos + n_scr
        ex_sems, pos = take(refs, pos, [len(e.sems) for e in exchanges])
        step = pl.program_id(0)
        for e, a, b, s in zip(exchanges, ex_ins, ex_outs, ex_sems):
            pl.when(step == 0)(functools.partial(e.start, a, b, s))
            if e.middle is not None:
                pl.when(step == steps // 2)(functools.partial(e.middle, a, b, s))
        body(*ins, *outs, *scr)
        for e, a, b, s in zip(exchanges, ex_ins, ex_outs, ex_sems):
            pl.when(step == steps - 1)(functools.partial(e.finish, a, b, s))

    n_ex_in = sum(len(e.operands) for e in exchanges)
    n_ex_out = sum(len(e.out_shapes) for e in exchanges)
    results = pl.pallas_call(
        full_body, name=name, grid=(steps,),
        in_specs=list(in_specs) + [HBM_SPEC] * n_ex_in,
        out_specs=list(out_specs) + [HBM_SPEC] * n_ex_out,
        out_shape=list(out_shape) + [s for e in exchanges for s in e.out_shapes],
        scratch_shapes=list(scratch_shapes) + [s for e in exchanges for s in e.sems],
        input_output_aliases=aliases or {},
        compiler_params=_cp("arbitrary"),
    )(*operands, *[a for e in exchanges for a in e.operands])
    ex_results, _ = take(results, n_out, [len(e.out_shapes) for e in exchanges])
    return results[:n_out], ex_results


def _dot(a, b, dims, precision=None):
    if precision is None:
        a = a.astype(BF16)
        b = b.astype(BF16)
    return lax.dot_general(a, b, (dims, ((), ())), preferred_element_type=F32, precision=precision)


def _make_mm(precision):
    @jax.custom_vjp
    def nn(a, b):
        return _dot(a, b, ((1,), (0,)), precision)

    @jax.custom_vjp
    def nt(a, b):
        return _dot(a, b, ((1,), (1,)), precision)

    @jax.custom_vjp
    def tn(a, b):
        return _dot(a, b, ((0,), (0,)), precision)

    nn.defvjp(lambda a, b: (nn(a, b), (a, b)), lambda r, g: (nt(g, r[1]), tn(r[0], g)))
    nt.defvjp(lambda a, b: (nt(a, b), (a, b)), lambda r, g: (nn(g, r[1]), tn(g, r[0])))
    tn.defvjp(lambda a, b: (tn(a, b), (a, b)), lambda r, g: (nt(r[1], g), nn(r[0], g)))
    return nn, nt, tn


MM, MM_NT, MM_TN = _make_mm(None)


def _matmul(a, b, *, ta=False, tb=False, tm, tn, tk, name, out_dtype=F32):
    (k_dim, m_dim) = a.shape if ta else a.shape[::-1]
    (n_dim, k2) = b.shape if tb else b.shape[::-1]
    assert k_dim == k2 and m_dim % tm == 0 and n_dim % tn == 0 and k_dim % tk == 0, (a.shape, b.shape, tm, tn, tk)
    nk = k_dim // tk
    dims = ((0 if ta else 1,), (1 if tb else 0,))

    def body(a_ref, b_ref, o_ref, *acc):
        p = _dot(a_ref[...], b_ref[...], dims)
        if nk == 1:
            o_ref[...] = p.astype(out_dtype)
            return
        acc_ref, k = acc[0], pl.program_id(2)

        @pl.when(k == 0)
        def _():
            acc_ref[...] = p

        @pl.when(k > 0)
        def _():
            acc_ref[...] += p

        @pl.when(k == nk - 1)
        def _():
            o_ref[...] = acc_ref[...].astype(out_dtype)

    a_spec = pl.BlockSpec((tk, tm), lambda i, j, k: (k, i)) if ta else pl.BlockSpec((tm, tk), lambda i, j, k: (i, k))
    b_spec = pl.BlockSpec((tn, tk), lambda i, j, k: (j, k)) if tb else pl.BlockSpec((tk, tn), lambda i, j, k: (k, j))
    return pl.pallas_call(
        body, name=name,
        grid=(m_dim // tm, n_dim // tn, nk),
        in_specs=[a_spec, b_spec],
        out_specs=pl.BlockSpec((tm, tn), lambda i, j, k: (i, j)),
        out_shape=jax.ShapeDtypeStruct((m_dim, n_dim), out_dtype),
        scratch_shapes=[pltpu.VMEM((tm, tn), F32)] if nk > 1 else [],
        compiler_params=_cp("parallel", "parallel", "arbitrary"),
    )(a, b)


NORM_ROWS = 256


def _rms(x, w):
    return x * lax.rsqrt(jnp.mean(x * x, axis=1, keepdims=True) + EPS) * w


def _norm_fwd(x, w_row, name, out_dtype=BF16):
    def body(x_ref, w_ref, o_ref):
        o_ref[...] = _rms(x_ref[...], w_ref[...]).astype(out_dtype)

    return pl.pallas_call(
        body, name=name, grid=(SEQ // NORM_ROWS,),
        in_specs=[pl.BlockSpec((NORM_ROWS, D_MODEL), lambda i: (i, 0)), pl.BlockSpec((1, D_MODEL), lambda i: (0, 0))],
        out_specs=pl.BlockSpec((NORM_ROWS, D_MODEL), lambda i: (i, 0)),
        out_shape=jax.ShapeDtypeStruct((SEQ, D_MODEL), out_dtype),
        compiler_params=_cp("parallel"),
    )(x, w_row)


def _resnorm_fwd(x, f, w_row, name):
    def body(x_ref, f_ref, w_ref, o_ref):
        o_ref[...] = x_ref[...] + _rms(f_ref[...], w_ref[...])

    blk = pl.BlockSpec((NORM_ROWS, D_MODEL), lambda i: (i, 0))
    return pl.pallas_call(
        body, name=name, grid=(SEQ // NORM_ROWS,),
        in_specs=[blk, blk, pl.BlockSpec((1, D_MODEL), lambda i: (0, 0))],
        out_specs=blk, out_shape=jax.ShapeDtypeStruct((SEQ, D_MODEL), F32),
        compiler_params=_cp("parallel"),
    )(x, f, w_row)


def _norm_bwd(x, w_row, dy, add, name):
    has_add = add is not None

    def body(*refs):
        if has_add:
            x_ref, w_ref, dy_ref, add_ref, dx_ref, dw_ref = refs
        else:
            x_ref, w_ref, dy_ref, dx_ref, dw_ref = refs
        _, vjp = jax.vjp(_rms, x_ref[...], w_ref[...])
        dx, dw = vjp(dy_ref[...])
        dx_ref[...] = dx + add_ref[...] if has_add else dx

        @pl.when(pl.program_id(0) == 0)
        def _():
            dw_ref[...] = jnp.zeros_like(dw_ref)

        dw_ref[...] += dw

    blk = pl.BlockSpec((NORM_ROWS, D_MODEL), lambda i: (i, 0))
    row = pl.BlockSpec((1, D_MODEL), lambda i: (0, 0))
    ins = [x, w_row, dy] + ([add] if has_add else [])
    return pl.pallas_call(
        body, name=name, grid=(SEQ // NORM_ROWS,),
        in_specs=[blk, row, blk] + ([blk] if has_add else []),
        out_specs=[blk, row],
        out_shape=[jax.ShapeDtypeStruct((SEQ, D_MODEL), F32), jax.ShapeDtypeStruct((1, D_MODEL), F32)],
        compiler_params=_cp("arbitrary"),
    )(*ins)


def _loss_fwd_bwd(y, target):
    def body(y_ref, t_ref, loss_ref, dy_ref):
        err = y_ref[...] - t_ref[...]
        dy_ref[...] = err * (1.0 / D_MODEL)

        @pl.when(pl.program_id(0) == 0)
        def _():
            loss_ref[...] = jnp.zeros_like(loss_ref)

        part = jnp.sum(jnp.sum(err * err, axis=1, keepdims=True) * (1.0 / D_MODEL), axis=0, keepdims=True)
        loss_ref[...] += 0.5 * jnp.broadcast_to(part, loss_ref.shape)

    blk = pl.BlockSpec((NORM_ROWS, D_MODEL), lambda i: (i, 0))
    return pl.pallas_call(
        body, name="loss", grid=(SEQ // NORM_ROWS,),
        in_specs=[blk, blk],
        out_specs=[pl.BlockSpec((1, 128), lambda i: (0, 0)), blk],
        out_shape=[jax.ShapeDtypeStruct((1, 128), F32), jax.ShapeDtypeStruct((SEQ, D_MODEL), F32)],
        compiler_params=_cp("arbitrary"),
    )(y, target)


def _make_shift(j):
    def down(x):
        row = lax.broadcasted_iota(jnp.int32, x.shape, 0)
        return jnp.where(row >= j, pltpu.roll(x, j, 0), 0.0)

    def up(x):
        n = x.shape[0]
        row = lax.broadcasted_iota(jnp.int32, x.shape, 0)
        return jnp.where(row < n - j, pltpu.roll(x, n - j, 0), 0.0)

    f = jax.custom_vjp(down)
    f.defvjp(lambda x: (down(x), None), lambda _, g: (up(g),))
    return f


_SHIFT = {j: _make_shift(j) for j in (1, 2, 3)}


def _causal_conv(x, taps):
    n = len(taps)
    acc = x * taps[n - 1]
    for k in range(n - 1):
        acc = acc + _SHIFT[n - 1 - k](x) * taps[k]
    return acc


def _tap_rows(w_ref, lanes=slice(None)):
    return tuple(w_ref[k:k + 1, lanes] for k in range(w_ref.shape[0]))


def _sigmoid(x):
    return 1.0 / (1.0 + jnp.exp(-x))


def _silu(x):
    return x * _sigmoid(x)


def _softplus(x):
    return jnp.maximum(x, 0.0) + jnp.log(1.0 + jnp.exp(-jnp.abs(x)))


def _gelu_tanh(x):
    return 0.5 * x * (1.0 + jnp.tanh(math.sqrt(2.0 / math.pi) * (x + 0.044715 * (x * x * x))))


def _dnconv_fn(x, taps):
    return _silu(_causal_conv(x, taps))


def _dnconv_fwd(proj, conv_w):
    def body(x_ref, w_ref, o_ref):
        o_ref[...] = _dnconv_fn(x_ref[...], _tap_rows(w_ref))

    return pl.pallas_call(
        body, name="dnconv_fwd", grid=(DN_QKV_BLKS,),
        in_specs=[pl.BlockSpec((SEQ, 128), lambda j: (0, DN_QKV_BLK0 + j)), pl.BlockSpec((4, 128), lambda j: (0, j))],
        out_specs=pl.BlockSpec((SEQ, 128), lambda j: (0, j)),
        out_shape=jax.ShapeDtypeStruct((SEQ, 1536), F32),
        compiler_params=_cp("parallel"),
    )(proj, conv_w)


def _dnconv_bwd(proj, conv_w, dc, dproj):
    def body(x_ref, w_ref, dc_ref, _, dx_ref, dw_ref):
        _, vjp = jax.vjp(_dnconv_fn, x_ref[...], _tap_rows(w_ref))
        dx, dw = vjp(dc_ref[...])
        dx_ref[...] = dx
        for k, row in enumerate(dw):
            dw_ref[k:k + 1, :] = row

    return pl.pallas_call(
        body, name="dnconv_bwd", grid=(DN_QKV_BLKS,),
        in_specs=[pl.BlockSpec((SEQ, 128), lambda j: (0, DN_QKV_BLK0 + j)), pl.BlockSpec((4, 128), lambda j: (0, j)),
                  pl.BlockSpec((SEQ, 128), lambda j: (0, j)), pl.BlockSpec(memory_space=pl.ANY)],
        out_specs=[pl.BlockSpec((SEQ, 128), lambda j: (0, DN_QKV_BLK0 + j)), pl.BlockSpec((4, 128), lambda j: (0, j))],
        out_shape=[jax.ShapeDtypeStruct((SEQ, IN_PAD), F32), jax.ShapeDtypeStruct((4, 1536), F32)],
        input_output_aliases={3: 0},
        compiler_params=_cp("parallel"),
    )(proj, conv_w, dc, dproj)


def _ffact_fn(pg, pu, wg, wu, bg, bu):
    return _gelu_tanh(_causal_conv(pg, wg) + bg) * (_causal_conv(pu, wu) + bu)


def _ffact_args(p_ref, w_ref, b_ref):
    g, u = slice(0, 128), slice(128, 256)
    return (p_ref[:, g], p_ref[:, u], _tap_rows(w_ref, g), _tap_rows(w_ref, u), b_ref[:, g], b_ref[:, u])


def _ffact_fwd(pre, conv_w, conv_b):
    def body(p_ref, w_ref, b_ref, o_ref):
        o_ref[...] = _ffact_fn(*_ffact_args(p_ref, w_ref, b_ref)).astype(BF16)

    return pl.pallas_call(
        body, name="ffact_fwd", grid=(FF_BLKS,),
        in_specs=[pl.BlockSpec((SEQ, 256), lambda j: (0, j)), pl.BlockSpec((3, 256), lambda j: (0, j)),
                  pl.BlockSpec((1, 256), lambda j: (0, j))],
        out_specs=pl.BlockSpec((SEQ, 128), lambda j: (0, j)),
        out_shape=jax.ShapeDtypeStruct((SEQ, D_FF), BF16),
        compiler_params=_cp("parallel"),
    )(pre, conv_w, conv_b)


def _ffact_bwd(pre, conv_w, conv_b, dact, exchanges=()):
    def body(p_ref, w_ref, b_ref, da_ref, dp_ref, dw_ref, db_ref):
        _, vjp = jax.vjp(_ffact_fn, *_ffact_args(p_ref, w_ref, b_ref))
        dpg, dpu, dwg, dwu, dbg, dbu = vjp(da_ref[...].astype(F32))
        dp_ref[:, 0:128] = dpg
        dp_ref[:, 128:256] = dpu
        for k in range(3):
            dw_ref[k:k + 1, 0:128] = dwg[k]
            dw_ref[k:k + 1, 128:256] = dwu[k]
        db_ref[:, 0:128] = dbg
        db_ref[:, 128:256] = dbu

    return _hosted_call(
        body, name="ffact_bwd", steps=FF_BLKS,
        in_specs=[pl.BlockSpec((SEQ, 256), lambda j: (0, j)), pl.BlockSpec((3, 256), lambda j: (0, j)),
                  pl.BlockSpec((1, 256), lambda j: (0, j)), pl.BlockSpec((SEQ, 128), lambda j: (0, j))],
        out_specs=[pl.BlockSpec((SEQ, 256), lambda j: (0, j)), pl.BlockSpec((3, 256), lambda j: (0, j)),
                   pl.BlockSpec((1, 256), lambda j: (0, j))],
        out_shape=[jax.ShapeDtypeStruct((SEQ, 2 * D_FF), F32), jax.ShapeDtypeStruct((3, 2 * D_FF), F32),
                   jax.ShapeDtypeStruct((1, 2 * D_FF), F32)],
        scratch_shapes=[], operands=(pre, conv_w, conv_b, dact), exchanges=exchanges)


def _interleave_ff(t):
    lead = t.shape[:-1]
    return t.reshape(lead + (2, FF_BLKS, 128)).swapaxes(-3, -2).reshape(lead + (2 * D_FF,))


def _deinterleave_ff(t):
    lead = t.shape[:-1]
    return t.reshape(lead + (FF_BLKS, 2, 128)).swapaxes(-3, -2).reshape(lead + (2 * D_FF,))


def _rope_tables():
    inv = 1.0 / (ROPE_THETA ** (jnp.arange(0, HEAD_DIM, 2, dtype=F32) / HEAD_DIM))
    ang = jnp.arange(SEQ, dtype=F32)[:, None] * inv[None, :]
    cos = jnp.tile(jnp.cos(ang), (1, 4))
    sin = jnp.tile(jnp.sin(ang), (1, 4))
    sign = jnp.where((jnp.arange(128) % HEAD_DIM) < HEAD_DIM // 2, -1.0, 1.0).astype(F32)
    return cos, sin * sign[None, :]


def _rope(x, cos, sin_signed):
    lane = lax.broadcasted_iota(jnp.int32, x.shape, 1)
    partner = jnp.where((lane % HEAD_DIM) < HEAD_DIM // 2, pltpu.roll(x, 128 - HEAD_DIM // 2, 1),
                        pltpu.roll(x, HEAD_DIM // 2, 1))
    return x * cos + partner * sin_signed


def _pairs_from_qkv(t):
    lead = t.shape[:-1]
    return t.reshape(lead + (3, N_PAIR, 128)).swapaxes(-3, -2).reshape(lead + (QKV_W,))


def _qkv_from_pairs(t):
    lead = t.shape[:-1]
    return t.reshape(lead + (N_PAIR, 3, 128)).swapaxes(-3, -2).reshape(lead + (QKV_W,))


def _head_masks():
    lane = lax.broadcasted_iota(jnp.int32, (1, 128), 1)
    return [(lane // HEAD_DIM) == h for h in range(2)]


def _both_heads(x):
    return jnp.concatenate([jnp.where(hm, x, 0.0)[None] for hm in _head_masks()], axis=0)


def _block_keys(branch, k_s, v_s, rows, prows, has_prev):
    a = lax.broadcasted_iota(jnp.int32, (ATTN_BLK, ATTN_BLK), 0)
    c = lax.broadcasted_iota(jnp.int32, (ATTN_BLK, ATTN_BLK), 1)
    keys, values, mask = k_s[rows, :], v_s[rows, :], c <= a
    if SEGMENT_BLOCKS[branch] > 1:
        keys = jnp.concatenate([k_s[prows, :], keys], axis=0)
        values = jnp.concatenate([v_s[prows, :], values], axis=0)
        mask = jnp.concatenate([(c >= a) & has_prev, mask], axis=1)
    twice = lambda t: jnp.broadcast_to(t[None], (2,) + t.shape)
    return twice(keys), twice(values), mask


def _block_rows(branch, t):
    d, per_seg = DILATIONS[branch], SEGMENT_BLOCKS[branch]
    if d == 1:
        start = pl.multiple_of(t * ATTN_BLK, ATTN_BLK)
        prev = pl.multiple_of(jnp.maximum(t - 1, 0) * ATTN_BLK, ATTN_BLK)
        return pl.ds(start, ATTN_BLK), pl.ds(prev, ATTN_BLK), t > 0
    r, n = t // per_seg, t % per_seg
    start = n * (ATTN_BLK * d) + r
    prev = jnp.maximum(n - 1, 0) * (ATTN_BLK * d) + r
    return pl.ds(start, ATTN_BLK, stride=d), pl.ds(prev, ATTN_BLK, stride=d), n > 0


def _attn_fwd(proj, cos, sin_signed, exchanges=()):
    scale = HEAD_DIM ** -0.5

    def body(qkv_ref, cos_ref, sin_ref, out_ref, lse_ref, q_s, k_s, v_s, *branch_s):
        o_s, l_s = branch_s[:3], branch_s[3:]
        q_s[...] = _rope(qkv_ref[:, 0:128], cos_ref[...], sin_ref[...])
        k_s[...] = _rope(qkv_ref[:, 128:256], cos_ref[...], sin_ref[...])
        v_s[...] = qkv_ref[:, 256:384]
        heads = _head_masks()
        for branch in range(3):
            def block(t, carry, branch=branch):
                rows, prows, has_prev = _block_rows(branch, t)
                keys, values, mask = _block_keys(branch, k_s, v_s, rows, prows, has_prev)
                s = jnp.where(mask, BMM_NT(_both_heads(q_s[rows, :]), keys) * scale, NEG)
                m = jnp.max(s, axis=2, keepdims=True)
                e = jnp.exp(s - m)
                l = jnp.sum(e, axis=2, keepdims=True)
                o = BMM(e, values) / l
                lse_b = m + jnp.log(l)
                o_s[branch][rows, :] = jnp.where(heads[0], o[0], o[1])
                l_s[branch][rows, :] = jnp.where(heads[0], lse_b[0], lse_b[1])
                return carry

            lax.fori_loop(0, N_BLK, block, 0)
        l0, l1, l2 = l_s[0][...], l_s[1][...], l_s[2][...]
        m = jnp.maximum(jnp.maximum(l0, l1), l2)
        w0, w1, w2 = jnp.exp(l0 - m), jnp.exp(l1 - m), jnp.exp(l2 - m)
        den = w0 + w1 + w2
        out_ref[...] = (w0 * o_s[0][...] + w1 * o_s[1][...] + w2 * o_s[2][...]) / den
        lse_ref[...] = m + jnp.log(den)

    tab = pl.BlockSpec((SEQ, 128), lambda j: (0, 0))
    col = pl.BlockSpec((SEQ, 128), lambda j: (0, j))
    return _hosted_call(
        body, name="attn_fwd", steps=N_PAIR,
        in_specs=[pl.BlockSpec((SEQ, 384), lambda j: (0, j)), tab, tab],
        out_specs=[col, col],
        out_shape=[jax.ShapeDtypeStruct((SEQ, 2 * ATTN_W), F32), jax.ShapeDtypeStruct((SEQ, ATTN_W), F32)],
        scratch_shapes=[pltpu.VMEM((SEQ, 128), F32)] * 9,
        operands=(proj, cos, sin_signed), exchanges=exchanges)


def _attn_bwd(proj, cos, sin_signed, cat, lse, dcat, dproj, exchanges=()):
    scale = HEAD_DIM ** -0.5

    def body(qkv_ref, cos_ref, sin_ref, out_ref, lse_ref, do_ref, _, dqkv_ref, q_s, k_s, v_s, dq_s, dk_s, dv_s,
             dod_s):
        q_s[...] = _rope(qkv_ref[:, 0:128], cos_ref[...], sin_ref[...])
        k_s[...] = _rope(qkv_ref[:, 128:256], cos_ref[...], sin_ref[...])
        v_s[...] = qkv_ref[:, 256:384]
        dq_s[...] = jnp.zeros_like(dq_s)
        dk_s[...] = jnp.zeros_like(dk_s)
        dv_s[...] = jnp.zeros_like(dv_s)
        dod_s[...] = do_ref[...] * out_ref[...]
        heads = _head_masks()
        for branch in range(3):
            def block(t, carry, branch=branch):
                rows, prows, has_prev = _block_rows(branch, t)
                keys, values, mask = _block_keys(branch, k_s, v_s, rows, prows, has_prev)
                q2, do2 = _both_heads(q_s[rows, :]), _both_heads(do_ref[rows, :])
                lse_b, dod = lse_ref[rows, :], dod_s[rows, :]
                lse2 = jnp.concatenate(
                    [jnp.max(jnp.where(hm, lse_b, NEG), axis=1, keepdims=True)[None] for hm in heads], axis=0)
                delta = jnp.concatenate(
                    [jnp.sum(jnp.where(hm, dod, 0.0), axis=1, keepdims=True)[None] for hm in heads], axis=0)
                p = jnp.exp(jnp.where(mask, BMM_NT(q2, keys) * scale, NEG) - lse2)
                ds = p * (BMM_NT(do2, values) - delta) * scale
                dq = BMM(ds, keys)
                dk = BMM_TN(ds, q2)
                dv = BMM_TN(p, do2)
                dk, dv = dk[0] + dk[1], dv[0] + dv[1]
                dq_s[rows, :] += jnp.where(heads[0], dq[0], dq[1])
                if SEGMENT_BLOCKS[branch] > 1:
                    dk_s[rows, :] += dk[ATTN_BLK:]
                    dv_s[rows, :] += dv[ATTN_BLK:]

                    @pl.when(has_prev)
                    def _():
                        dk_s[prows, :] += dk[:ATTN_BLK]
                        dv_s[prows, :] += dv[:ATTN_BLK]
                else:
                    dk_s[rows, :] += dk
                    dv_s[rows, :] += dv
                return carry

            lax.fori_loop(0, N_BLK, block, 0)
        dqkv_ref[:, 0:128] = _rope(dq_s[...], cos_ref[...], -sin_ref[...])
        dqkv_ref[:, 128:256] = _rope(dk_s[...], cos_ref[...], -sin_ref[...])
        dqkv_ref[:, 256:384] = dv_s[...]

    tab = pl.BlockSpec((SEQ, 128), lambda j: (0, 0))
    col = pl.BlockSpec((SEQ, 128), lambda j: (0, j))
    qkv = pl.BlockSpec((SEQ, 384), lambda j: (0, j))
    (dproj,), results = _hosted_call(
        body, name="attn_bwd", steps=N_PAIR,
        in_specs=[qkv, tab, tab, col, col, col, pl.BlockSpec(memory_space=pl.ANY)],
        out_specs=[qkv],
        out_shape=[jax.ShapeDtypeStruct((SEQ, IN_PAD), F32)],
        scratch_shapes=[pltpu.VMEM((SEQ, 128), F32)] * 7,
        operands=(proj, cos, sin_signed, cat, lse, dcat, dproj), exchanges=exchanges, aliases={6: 0})
    return dproj, results


def _bdot(a, b, dims, precision=None):
    if precision is None:
        a = a.astype(BF16)
        b = b.astype(BF16)
    return lax.dot_general(a, b, (dims, ((0,), (0,))), preferred_element_type=F32, precision=precision)


def _make_bmm(precision):
    @jax.custom_vjp
    def nn(a, b):
        return _bdot(a, b, ((2,), (1,)), precision)

    @jax.custom_vjp
    def nt(a, b):
        return _bdot(a, b, ((2,), (2,)), precision)

    @jax.custom_vjp
    def tn(a, b):
        return _bdot(a, b, ((1,), (1,)), precision)

    nn.defvjp(lambda a, b: (nn(a, b), (a, b)), lambda r, g: (nt(g, r[1]), tn(r[0], g)))
    nt.defvjp(lambda a, b: (nt(a, b), (a, b)), lambda r, g: (nn(g, r[1]), tn(g, r[0])))
    tn.defvjp(lambda a, b: (tn(a, b), (a, b)), lambda r, g: (nt(r[1], g), nn(r[0], g)))
    return nn, nt, tn


BMM, BMM_NT, BMM_TN = _make_bmm(None)
BMM3, BMM3_NT, BMM3_TN = _make_bmm(lax.Precision.HIGH)
MM3, _, _ = _make_mm(lax.Precision.HIGH)


def _head_lanes(t, off):
    lane = lax.broadcasted_iota(jnp.int32, (1, 128), 1)
    return jnp.concatenate(
        [jnp.sum(t * (lane == off + h).astype(F32), axis=1, keepdims=True)[None] for h in range(NDH)], axis=0)


@jax.custom_vjp
def _unit_lower_inverse(a_mat):
    c = a_mat.shape[1]
    eye = (lax.broadcasted_iota(jnp.int32, (c, c), 0) == lax.broadcasted_iota(jnp.int32, (c, c), 1)).astype(F32)
    power = -a_mat
    t_inv = eye + power
    for _ in range(5):
        power = BMM3(power, power)
        t_inv = t_inv + BMM3(t_inv, power)
    return t_inv


def _unit_lower_inverse_fwd(a_mat):
    t_inv = _unit_lower_inverse(a_mat)
    return t_inv, t_inv


def _unit_lower_inverse_bwd(t_inv, d_inv):
    return (-BMM3_NT(BMM3_TN(t_inv, d_inv), t_inv),)


_unit_lower_inverse.defvjp(_unit_lower_inverse_fwd, _unit_lower_inverse_bwd)


def _delta_chunk(qr, kr, vr, z, tail, alog_row, dt_row, nw, state):
    c = qr.shape[1]
    beta = _sigmoid(_head_lanes(tail, 0))
    g = -jnp.exp(_head_lanes(alog_row, 0)) * _softplus(_head_lanes(tail, NDH) + _head_lanes(dt_row, 0))

    q = qr * lax.rsqrt(jnp.sum(qr * qr, axis=2, keepdims=True) + EPS) * (128 ** -0.5)
    k = kr * lax.rsqrt(jnp.sum(kr * kr, axis=2, keepdims=True) + EPS)

    ri = lax.broadcasted_iota(jnp.int32, (c, c), 0)
    ci = lax.broadcasted_iota(jnp.int32, (c, c), 1)
    tril = ri >= ci
    lane = lax.broadcasted_iota(jnp.int32, (1, 128), 1)
    g_lanes = sum(g[h] * (lane == h).astype(F32) for h in range(NDH))
    gc = _head_lanes(MM3(tril.astype(F32), g_lanes), 0)
    g_row = jnp.swapaxes(jnp.broadcast_to(gc, (NDH, c, c)), 1, 2)
    decay = jnp.where(tril, jnp.exp(jnp.where(tril, gc - g_row, 0.0)), 0.0)
    kb = k * beta
    t_inv = _unit_lower_inverse(jnp.where(ri > ci, BMM_NT(kb, k) * decay, 0.0))
    eg = jnp.exp(gc)
    u = BMM(t_inv, vr * beta)
    w = BMM(t_inv, kb * eg)
    qk = BMM_NT(q, k) * decay
    g_tot = jnp.sum(g, axis=1, keepdims=True)
    v_new = u - BMM(w, state)
    o = BMM(q * eg, state) + BMM(qk, v_new)
    new_state = state * jnp.exp(g_tot) + BMM_TN(k * jnp.exp(g_tot - gc), v_new)
    on = o * lax.rsqrt(jnp.mean(o * o, axis=2, keepdims=True) + EPS) * nw
    return on * _silu(z), new_state


def _heads(v, off=0):
    return jnp.concatenate([v[None, :, off + 128 * h:off + 128 * (h + 1)] for h in range(NDH)], axis=0)


def _unheads(t):
    return jnp.concatenate([t[h] for h in range(NDH)], axis=1)


def _delta_fwd(c_qkv, proj, alog_row, dt_row, nw, cat, exchanges=()):
    def body(c_ref, z_ref, tail_ref, al_ref, dt_ref, nw_ref, _, y_ref, st_ref, state):
        @pl.when(pl.program_id(0) == 0)
        def _():
            state[...] = jnp.zeros_like(state)

        cv = c_ref[...]
        st_ref[0] = state[...]
        y, new_state = _delta_chunk(_heads(cv), _heads(cv, 512), _heads(cv, 1024), _heads(z_ref[...]), tail_ref[...],
                                    al_ref[...], dt_ref[...], nw_ref[...], state[...])
        y_ref[...] = _unheads(y)
        state[...] = new_state

    row = pl.BlockSpec((1, 128), lambda n: (0, 0))
    return _hosted_call(
        body, name="delta_fwd", steps=NCH,
        in_specs=[pl.BlockSpec((CH, 1536), lambda n: (n, 0)), pl.BlockSpec((CH, 512), lambda n: (n, DN_Z_COL // 512)),
                  pl.BlockSpec((CH, 128), lambda n: (n, DN_TAIL_BLK)), row, row, row, pl.BlockSpec(memory_space=pl.ANY)],
        out_specs=[pl.BlockSpec((CH, 512), lambda n: (n, 1)),
                   pl.BlockSpec((1, NDH, 128, 128), lambda n: (n, 0, 0, 0))],
        out_shape=[jax.ShapeDtypeStruct((SEQ, 2 * ATTN_W), F32), jax.ShapeDtypeStruct((NCH, NDH, 128, 128), F32)],
        scratch_shapes=[pltpu.VMEM((NDH, 128, 128), F32)],
        operands=(c_qkv, proj, proj, alog_row, dt_row, nw, cat), exchanges=exchanges, aliases={6: 0})


def _delta_bwd(c_qkv, proj, alog_row, dt_row, nw, states, dcat, exchanges=()):
    def body(c_ref, z_ref, tail_ref, al_ref, dt_ref, nw_ref, st_ref, dy_ref,
             dp_ref, dc_ref, dal_ref, ddt_ref, dnw_ref, dstate):
        @pl.when(pl.program_id(0) == 0)
        def _():
            dstate[...] = jnp.zeros_like(dstate)
            dal_ref[...] = jnp.zeros_like(dal_ref)
            ddt_ref[...] = jnp.zeros_like(ddt_ref)
            dnw_ref[...] = jnp.zeros_like(dnw_ref)

        cv = c_ref[...]
        _, vjp = jax.vjp(_delta_chunk, _heads(cv), _heads(cv, 512), _heads(cv, 1024), _heads(z_ref[...]),
                         tail_ref[...], al_ref[...], dt_ref[...], nw_ref[...], st_ref[0])
        dq, dk, dv, dz, dtail, dal, ddt, dnw, dst = vjp((_heads(dy_ref[...]), dstate[...]))
        dstate[...] = dst
        dc_ref[...] = jnp.concatenate([_unheads(dq), _unheads(dk), _unheads(dv)], axis=1)
        dp_ref[...] = jnp.concatenate([_unheads(dz), dtail, jnp.zeros((CH, 128), F32)], axis=1)
        dal_ref[...] += dal
        ddt_ref[...] += ddt
        dnw_ref[...] += dnw

    rev = lambda n: NCH - 1 - n
    row = pl.BlockSpec((1, 128), lambda n: (0, 0))
    return _hosted_call(
        body, name="delta_bwd", steps=NCH,
        in_specs=[pl.BlockSpec((CH, 1536), lambda n: (rev(n), 0)),
                  pl.BlockSpec((CH, 512), lambda n: (rev(n), DN_Z_COL // 512)),
                  pl.BlockSpec((CH, 128), lambda n: (rev(n), DN_TAIL_BLK)), row, row, row,
                  pl.BlockSpec((1, NDH, 128, 128), lambda n: (rev(n), 0, 0, 0)),
                  pl.BlockSpec((CH, 512), lambda n: (rev(n), 1))],
        out_specs=[pl.BlockSpec((CH, 768), lambda n: (rev(n), DN_Z_COL // 768)),
                   pl.BlockSpec((CH, 1536), lambda n: (rev(n), 0)), row, row, row],
        out_shape=[jax.ShapeDtypeStruct((SEQ, IN_PAD), F32), jax.ShapeDtypeStruct((SEQ, 1536), F32)]
        + [jax.ShapeDtypeStruct((1, 128), F32)] * 3,
        scratch_shapes=[pltpu.VMEM((NDH, 128, 128), F32)],
        operands=(c_qkv, proj, proj, alog_row, dt_row, nw, states, dcat), exchanges=exchanges)


def _place():
    x, y, c = lax.axis_index("x"), lax.axis_index("y"), lax.axis_index("c")
    other_chips = [(1 - x, y), (x, 1 - y), (1 - x, 1 - y)]
    return x, y, c, other_chips


def _gather_exchange(shards):
    n = len(shards)

    def copies(ins, outs, sems):
        send_sems, recv_sems, local_sems = sems
        x, y, c, chips = _place()
        me, sibling = (x, y, c), (x, y, 1 - c)

        def copy(b, k, block, to, src=None):
            slot = outs[b].at[4 * block[0] + 2 * block[1] + block[2]]
            return pltpu.make_async_remote_copy(
                src_ref=slot if src is None else src, dst_ref=slot,
                send_sem=send_sems.at[b, k], recv_sem=recv_sems.at[b, k], device_id=to, device_id_type=MESH)

        mine = [pltpu.make_async_copy(ins[b], outs[b].at[4 * x + 2 * y + c], local_sems.at[b]) for b in range(n)]
        first = []
        for b in range(n):
            first.append(copy(b, 0, me, sibling, src=ins[b]))
            first += [copy(b, 1 + j, me, (*chip, c), src=ins[b]) for j, chip in enumerate(chips)]
        over_ici = [copy(b, 1 + j, (*chip, c), me) for b in range(n) for j, chip in enumerate(chips)]
        passed = [copy(b, 4 + j, (*chip, c), sibling) for b in range(n) for j, chip in enumerate(chips)]
        from_sibling = []
        for b in range(n):
            from_sibling.append(copy(b, 0, sibling, me))
            from_sibling += [copy(b, 4 + j, (*chip, 1 - c), me) for j, chip in enumerate(chips)]
        return mine, first, over_ici, passed, from_sibling

    def start(ins, outs, sems):
        mine, first, _, _, _ = copies(ins, outs, sems)
        for cp in mine + first:
            cp.start()

    def middle(ins, outs, sems):
        _, _, over_ici, passed, _ = copies(ins, outs, sems)
        for arrived, onward in zip(over_ici, passed):
            arrived.wait_recv()
            onward.start()

    def finish(ins, outs, sems):
        mine, first, _, passed, from_sibling = copies(ins, outs, sems)
        for cp in from_sibling:
            cp.wait_recv()
        for cp in first + passed:
            cp.wait_send()
        for cp in mine:
            cp.wait()

    return Exchange(shards, [jax.ShapeDtypeStruct((N_DEV,) + s.shape, s.dtype) for s in shards],
                    [pltpu.SemaphoreType.DMA((n, 7)), pltpu.SemaphoreType.DMA((n, 7)), pltpu.SemaphoreType.DMA((n,))],
                    start, middle, finish)


def _sibling_exchange(gs):
    n = len(gs)

    def copies(ins, outs, sems):
        send_sems, recv_sems = sems
        x, y, c, _ = _place()
        return [pltpu.make_async_remote_copy(
            src_ref=ins[b].at[2 * p + (1 - c)], dst_ref=outs[b].at[p],
            send_sem=send_sems.at[b, p], recv_sem=recv_sems.at[b, p],
            device_id=(x, y, 1 - c), device_id_type=MESH) for b in range(n) for p in range(4)]

    def start(ins, outs, sems):
        for cp in copies(ins, outs, sems):
            cp.start()

    def finish(ins, outs, sems):
        for cp in copies(ins, outs, sems):
            cp.wait()

    return Exchange(gs, [jax.ShapeDtypeStruct((4,) + g.shape[1:], g.dtype) for g in gs],
                    [pltpu.SemaphoreType.DMA((n, 4)), pltpu.SemaphoreType.DMA((n, 4))], start, None, finish)


def _chips_exchange(hs):
    n = len(hs)

    def copies(ins, outs, sems):
        send_sems, recv_sems, local_sems = sems
        x, y, c, chips = _place()
        my_chip = 2 * x + y
        local = [pltpu.make_async_copy(ins[b].at[my_chip], outs[b].at[my_chip], local_sems.at[b]) for b in range(n)]
        sends, arrivals = [], []
        for b in range(n):
            for k, (px, py) in enumerate(chips):
                peer = 2 * px + py
                sends.append(pltpu.make_async_remote_copy(
                    src_ref=ins[b].at[peer], dst_ref=outs[b].at[my_chip],
                    send_sem=send_sems.at[b, k], recv_sem=recv_sems.at[b, k],
                    device_id=(px, py, c), device_id_type=MESH))
                arrivals.append(pltpu.make_async_remote_copy(
                    src_ref=ins[b].at[peer], dst_ref=outs[b].at[peer],
                    send_sem=send_sems.at[b, k], recv_sem=recv_sems.at[b, k],
                    device_id=(px, py, c), device_id_type=MESH))
        return local, sends, arrivals

    def start(ins, outs, sems):
        local, sends, _ = copies(ins, outs, sems)
        for cp in local + sends:
            cp.start()

    def finish(ins, outs, sems):
        local, sends, arrivals = copies(ins, outs, sems)
        for cp in arrivals:
            cp.wait_recv()
        for cp in sends:
            cp.wait_send()
        for cp in local:
            cp.wait()

    return Exchange(hs, [jax.ShapeDtypeStruct(h.shape, h.dtype) for h in hs],
                    [pltpu.SemaphoreType.DMA((n, 3)), pltpu.SemaphoreType.DMA((n, 3)), pltpu.SemaphoreType.DMA((n,))],
                    start, None, finish)


def _run_exchange(exchange, name):
    n_in, n_out = len(exchange.operands), len(exchange.out_shapes)

    def body(*refs):
        ins, outs, sems = refs[:n_in], refs[n_in:n_in + n_out], refs[n_in + n_out:]
        exchange.start(ins, outs, sems)
        if exchange.middle is not None:
            exchange.middle(ins, outs, sems)
        exchange.finish(ins, outs, sems)

    return pl.pallas_call(
        body, name=name,
        in_specs=[HBM_SPEC] * n_in, out_specs=[HBM_SPEC] * n_out,
        out_shape=exchange.out_shapes, scratch_shapes=exchange.sems,
    )(*exchange.operands)


def _pair_add(g, r, core, name):
    _, nr, nc = g.shape
    tr = nr // 2 if nr % 32 == 0 else nr

    def body(core_ref, g_ref, r_ref, o_ref):
        o_ref[...] = (g_ref[...] + r_ref[...]).astype(BF16)

    return pl.pallas_call(
        body, name=name,
        grid_spec=pltpu.PrefetchScalarGridSpec(
            num_scalar_prefetch=1, grid=(4, nr // tr),
            in_specs=[pl.BlockSpec((1, tr, nc), lambda p, i, core: (2 * p + core[0], i, 0)),
                      pl.BlockSpec((1, tr, nc), lambda p, i, core: (p, i, 0))],
            out_specs=pl.BlockSpec((1, tr, nc), lambda p, i, core: (p, i, 0))),
        out_shape=jax.ShapeDtypeStruct(r.shape, BF16),
        compiler_params=_cp("parallel", "parallel"),
    )(core, g, r)


def _all_gather_sum_small(v):
    rows = v.shape[0]

    def body(x_ref, sum_ref, out_ref, send_sems, recv_sems, local_sem):
        x, y, c, chips = _place()
        me, sibling = (x, y, c), (x, y, 1 - c)

        def block(px, py, pc):
            return out_ref.at[pl.ds((4 * px + 2 * py + pc) * rows, rows), :]

        def copy(k, blk, to, src=None):
            return pltpu.make_async_remote_copy(
                src_ref=block(*blk) if src is None else src, dst_ref=block(*blk),
                send_sem=send_sems.at[k], recv_sem=recv_sems.at[k], device_id=to, device_id_type=MESH)

        mine = pltpu.make_async_copy(x_ref, block(*me), local_sem)
        mine.start()
        first = [copy(0, me, sibling, src=x_ref)]
        first += [copy(1 + j, me, (*chip, c), src=x_ref) for j, chip in enumerate(chips)]
        for cp in first:
            cp.start()
        passed = [copy(4 + j, (*chip, c), sibling) for j, chip in enumerate(chips)]
        for j, chip in enumerate(chips):
            copy(1 + j, (*chip, c), me).wait_recv()
            passed[j].start()
        copy(0, sibling, me).wait_recv()
        for j, chip in enumerate(chips):
            copy(4 + j, (*chip, 1 - c), me).wait_recv()
        for cp in first + passed:
            cp.wait_send()
        mine.wait()
        total = out_ref[pl.ds(0, rows), :]
        for d in range(1, N_DEV):
            total = total + out_ref[pl.ds(d * rows, rows), :]
        sum_ref[...] = total

    vm = pl.BlockSpec(memory_space=pltpu.VMEM)
    return pl.pallas_call(
        body, name="small_all_reduce",
        in_specs=[vm], out_specs=[vm],
        out_shape=[jax.ShapeDtypeStruct((rows, 128), F32)],
        scratch_shapes=[pltpu.VMEM((N_DEV * rows, 128), F32), pltpu.SemaphoreType.DMA((7,)),
                        pltpu.SemaphoreType.DMA((7,)), pltpu.SemaphoreType.DMA],
    )(v)[0]


def _adamw(w, g, m, v):
    m = ADAM_B1 * m + (1.0 - ADAM_B1) * g
    v = ADAM_B2 * v + (1.0 - ADAM_B2) * (g * g)
    m_hat = m / (1.0 - ADAM_B1 ** ADAM_STEP)
    v_hat = v / (1.0 - ADAM_B2 ** ADAM_STEP)
    delta = -ADAM_LR * (m_hat / (jnp.sqrt(v_hat) + ADAM_EPS) + ADAM_WD * w)
    return delta, m, v


ADAM_ROWS = dict(w_in=256, w_out=128, ffn_w_in=256, ffn_w_out=176)


def _sum_chips(p):
    p = p.astype(F32)
    return (p[0] + p[1]) + (p[2] + p[3])


def _sum_parts(parts, name):
    def body(p_ref, g_ref):
        g_ref[...] = _sum_chips(p_ref[...])

    return pl.pallas_call(body, name=name, out_shape=jax.ShapeDtypeStruct(parts.shape[1:], F32),
                          compiler_params=_cp())(parts)


def _adamw_sharded(parts, w, m, v, tr, name):
    nl, nr, nc = w.shape
    n_parts = parts[0].shape[0]

    def body(*refs):
        p_refs, (w_ref, m_ref, v_ref, g_ref, d_ref, nm_ref, nv_ref) = refs[:nl], refs[nl:]
        layer = pl.program_id(0)
        p = p_refs[0][...]
        for l in range(1, nl):
            p = jnp.where(layer == l, p_refs[l][...], p)
        g = _sum_chips(p) if n_parts == 4 else p[0]
        delta, nm, nv = _adamw(w_ref[0], g, m_ref[0], v_ref[0])
        g_ref[0] = g
        d_ref[0] = delta
        nm_ref[0] = nm
        nv_ref[0] = nv

    blk = pl.BlockSpec((1, tr, nc), lambda l, i: (l, i, 0))
    return pl.pallas_call(
        body, name=name, grid=(nl, nr // tr),
        in_specs=[pl.BlockSpec((n_parts, tr, nc), lambda l, i: (0, i, 0))] * nl + [blk, blk, blk],
        out_specs=[blk] * 4,
        out_shape=[jax.ShapeDtypeStruct(w.shape, F32)] * 4,
        compiler_params=_cp("parallel", "parallel"),
    )(*parts, w, m, v)


def _adamw_small(g, w, m, v):
    def body(g_ref, w_ref, m_ref, v_ref, d_ref, nm_ref, nv_ref):
        delta, nm, nv = _adamw(w_ref[...], g_ref[...], m_ref[...], v_ref[...])
        d_ref[...] = delta
        nm_ref[...] = nm
        nv_ref[...] = nv

    return pl.pallas_call(
        body, name="adamw_small",
        out_shape=[jax.ShapeDtypeStruct(g.shape, F32)] * 3,
    )(g, w, m, v)


def _pack(arrays, rows):
    flat = jnp.concatenate([a.reshape(-1).astype(F32) for a in arrays])
    return jnp.pad(flat, (0, rows * 128 - flat.shape[0])).reshape(rows, 128)


def _unpack(packed, shapes):
    flat = packed.reshape(-1)
    out, off = [], 0
    for s in shapes:
        n = math.prod(s)
        out.append(flat[off:off + n].reshape(s))
        off += n
    return out


def _row(v, width=None):
    v = v.reshape(1, -1)
    return v if width is None else jnp.pad(v, ((0, 0), (0, width - v.shape[1])))


def _layer_fwd(x, wts, tables, attn_exchanges=(), delta_exchanges=(), on_attn=None, on_delta=None):
    h = _norm_fwd(x, wts["norm_pre_mix"], "norm_pre_mix")
    proj = _matmul(h, wts["w_in"], tb=True, tm=512, tn=768, tk=1024, name="mm_proj")
    (cat, lse), got = _attn_fwd(proj, *tables, exchanges=attn_exchanges)
    if on_attn is not None:
        on_attn(got)
    c_qkv = _dnconv_fwd(proj, wts["dn_conv_w"])
    (cat, states), got = _delta_fwd(c_qkv, proj, wts["dn_a_log"], wts["dn_dt_bias"], wts["dn_norm_w"], cat,
                                    exchanges=delta_exchanges)
    if on_delta is not None:
        on_delta(got)
    mix = _matmul(cat, wts["w_out"], tm=512, tn=1024, tk=1024, name="mm_mix")
    x1 = _resnorm_fwd(x, mix, wts["norm_post_mix"], "norm_post_mix")
    h2 = _norm_fwd(x1, wts["norm_pre_ffn"], "norm_pre_ffn")
    pre = _matmul(h2, wts["ffn_w_in"], tb=True, tm=512, tn=512, tk=1024, name="mm_ffn_in")
    act = _ffact_fwd(pre, wts["ffn_conv_w"], wts["ffn_conv_b"])
    f = _matmul(act, wts["ffn_w_out"], tm=512, tn=1024, tk=D_FF, name="mm_ffn_out")
    x2 = _resnorm_fwd(x1, f, wts["norm_post_ffn"], "norm_post_ffn")
    saved = dict(x=x, h=h, proj=proj, lse=lse, c_qkv=c_qkv, states=states, cat=cat, mix=mix, x1=x1, h2=h2, pre=pre,
                 act=act, f=f)
    return x2, saved


def _layer_bwd(dx2, wts, s, tables, ffact_exchanges=(), delta_exchanges=None, attn_exchanges=None):
    g = {}
    df, g["norm_post_ffn"] = _norm_bwd(s["f"], wts["norm_post_ffn"], dx2, None, "norm_post_ffn_bwd")
    dact = _matmul(df, wts["ffn_w_out"], tb=True, tm=512, tn=1408, tk=1024, name="mm_dact", out_dtype=BF16)
    g["ffn_w_out"] = _matmul(s["act"], df, ta=True, tm=1408, tn=512, tk=SEQ, name="mm_dw_ffn_out")
    (dpre, g["ffn_conv_w"], g["ffn_conv_b"]), got = _ffact_bwd(s["pre"], wts["ffn_conv_w"], wts["ffn_conv_b"], dact,
                                                               exchanges=ffact_exchanges)
    dh2 = _matmul(dpre, wts["ffn_w_in"], tm=1024, tn=1024, tk=1408, name="mm_dh2")
    g["ffn_w_in"] = _matmul(dpre, s["h2"], ta=True, tm=512, tn=1024, tk=SEQ, name="mm_dw_ffn_in")
    dx1, g["norm_pre_ffn"] = _norm_bwd(s["x1"], wts["norm_pre_ffn"], dh2, dx2, "norm_pre_ffn_bwd")
    dmix, g["norm_post_mix"] = _norm_bwd(s["mix"], wts["norm_post_mix"], dx1, None, "norm_post_mix_bwd")
    dcat = _matmul(dmix, wts["w_out"], tb=True, tm=512, tn=1024, tk=1024, name="mm_dcat")
    g["w_out"] = _matmul(s["cat"], dmix, ta=True, tm=1024, tn=512, tk=SEQ, name="mm_dw_out")
    (dproj, dc, g["dn_a_log"], g["dn_dt_bias"], g["dn_norm_w"]), got = _delta_bwd(
        s["c_qkv"], s["proj"], wts["dn_a_log"], wts["dn_dt_bias"], wts["dn_norm_w"], s["states"], dcat,
        exchanges=delta_exchanges(g, got) if delta_exchanges is not None else ())
    dproj, got = _attn_bwd(s["proj"], *tables, s["cat"], s["lse"], dcat, dproj,
                           exchanges=attn_exchanges(got) if attn_exchanges is not None else ())
    dproj, g["dn_conv_w"] = _dnconv_bwd(s["proj"], wts["dn_conv_w"], dc, dproj)
    dh = _matmul(dproj, wts["w_in"], tm=1024, tn=1024, tk=1280, name="mm_dh")
    g["w_in"] = _matmul(dproj, s["h"], ta=True, tm=768, tn=1024, tk=SEQ, name="mm_dw_in")
    dx, g["norm_pre_mix"] = _norm_bwd(s["x"], wts["norm_pre_mix"], dh, dx1, "norm_pre_mix_bwd")
    return dx, g, got


BIG = ("w_in", "w_out", "ffn_w_in", "ffn_w_out")
COLUMN_SHARDED = ("w_in", "ffn_w_in")
SMALL_SHARDED = ("dn_conv_w", "ffn_conv_w")
REPLICATED = ("dn_a_log", "dn_dt_bias", "dn_norm_w", "ffn_conv_b", "norm_pre_mix", "norm_post_mix", "norm_pre_ffn",
              "norm_post_ffn")
WEIGHTS = ("w_in", "dn_conv_w", "dn_a_log", "dn_dt_bias", "dn_norm_w", "w_out", "ffn_w_in", "ffn_conv_w", "ffn_conv_b",
           "ffn_w_out", "norm_pre_mix", "norm_post_mix", "norm_pre_ffn", "norm_post_ffn")
FULL_SHAPE = dict(dn_conv_w=(DEPTH, 4, 1536), ffn_conv_w=(DEPTH, 3, 2 * D_FF), dn_a_log=(DEPTH, NDH),
                  dn_dt_bias=(DEPTH, NDH), dn_norm_w=(DEPTH, 128), ffn_conv_b=(DEPTH, 2 * D_FF),
                  norm_pre_mix=(DEPTH, D_MODEL), norm_post_mix=(DEPTH, D_MODEL), norm_pre_ffn=(DEPTH, D_MODEL),
                  norm_post_ffn=(DEPTH, D_MODEL))
SMALL_GRAD_ORDER = REPLICATED + SMALL_SHARDED
SMALL_GRAD_ROWS = 520
SMALL_W_ROWS = 48
SMALL_ADAM_ROWS = 200


def _w_in_rows_to_kernel_order(t):
    qkv = t[:QKV_W].reshape(3, N_PAIR, 128, -1).swapaxes(0, 1).reshape(QKV_W, -1)
    return jnp.pad(jnp.concatenate([qkv, t[QKV_W:]], axis=0), ((0, IN_PAD - IN_COLS), (0, 0)))


def _w_in_rows_from_kernel_order(t):
    qkv = t[:QKV_W].reshape(N_PAIR, 3, 128, -1).swapaxes(0, 1).reshape(QKV_W, -1)
    return jnp.concatenate([qkv, t[QKV_W:IN_COLS]], axis=0)


def _interleave_ff_rows(t):
    return t.reshape(2, FF_BLKS, 128, -1).swapaxes(0, 1).reshape(2 * D_FF, -1)


def _deinterleave_ff_rows(t):
    return t.reshape(FF_BLKS, 2, 128, -1).swapaxes(0, 1).reshape(2 * D_FF, -1)


def kernel(x, w_in, dn_conv_w, dn_a_log, dn_dt_bias, dn_norm_w, w_out, ffn_w_in, ffn_conv_w, ffn_conv_b, ffn_w_out, norm_pre_mix, norm_post_mix, norm_pre_ffn, norm_post_ffn, loss_target, m_w_in, m_dn_conv_w, m_dn_a_log, m_dn_dt_bias, m_dn_norm_w, m_w_out, m_ffn_w_in, m_ffn_conv_w, m_ffn_conv_b, m_ffn_w_out, m_norm_pre_mix, m_norm_post_mix, m_norm_pre_ffn, m_norm_post_ffn, v_w_in, v_dn_conv_w, v_dn_a_log, v_dn_dt_bias, v_dn_norm_w, v_w_out, v_ffn_w_in, v_ffn_conv_w, v_ffn_conv_b, v_ffn_w_out, v_norm_pre_mix, v_norm_post_mix, v_norm_pre_ffn, v_norm_post_ffn):
    local = dict(w_in=w_in, dn_conv_w=dn_conv_w, dn_a_log=dn_a_log, dn_dt_bias=dn_dt_bias, dn_norm_w=dn_norm_w,
                 w_out=w_out, ffn_w_in=ffn_w_in, ffn_conv_w=ffn_conv_w, ffn_conv_b=ffn_conv_b, ffn_w_out=ffn_w_out,
                 norm_pre_mix=norm_pre_mix, norm_post_mix=norm_post_mix, norm_pre_ffn=norm_pre_ffn,
                 norm_post_ffn=norm_post_ffn)
    mom_m = dict(w_in=m_w_in, dn_conv_w=m_dn_conv_w, dn_a_log=m_dn_a_log, dn_dt_bias=m_dn_dt_bias,
                 dn_norm_w=m_dn_norm_w, w_out=m_w_out, ffn_w_in=m_ffn_w_in, ffn_conv_w=m_ffn_conv_w,
                 ffn_conv_b=m_ffn_conv_b, ffn_w_out=m_ffn_w_out, norm_pre_mix=m_norm_pre_mix,
                 norm_post_mix=m_norm_post_mix, norm_pre_ffn=m_norm_pre_ffn, norm_post_ffn=m_norm_post_ffn)
    mom_v = dict(w_in=v_w_in, dn_conv_w=v_dn_conv_w, dn_a_log=v_dn_a_log, dn_dt_bias=v_dn_dt_bias,
                 dn_norm_w=v_dn_norm_w, w_out=v_w_out, ffn_w_in=v_ffn_w_in, ffn_conv_w=v_ffn_conv_w,
                 ffn_conv_b=v_ffn_conv_b, ffn_w_out=v_ffn_w_out, norm_pre_mix=v_norm_pre_mix,
                 norm_post_mix=v_norm_post_mix, norm_pre_ffn=v_norm_pre_ffn, norm_post_ffn=v_norm_post_ffn)
    dev = 4 * lax.axis_index("x") + 2 * lax.axis_index("y") + lax.axis_index("c")
    core = lax.axis_index("c").astype(jnp.int32).reshape(1)

    def shard(n, l):
        s = local[n][l].astype(BF16)
        return s.T if n in COLUMN_SHARDED else s

    def matrix(n, gathered):
        if n == "w_in":
            return _w_in_rows_to_kernel_order(gathered.reshape(IN_COLS, D_MODEL))
        if n == "ffn_w_in":
            return _interleave_ff_rows(gathered.reshape(2 * D_FF, D_MODEL))
        return gathered.reshape(-1, D_MODEL)

    small_w = _pack([dn_conv_w, ffn_conv_w], SMALL_W_ROWS)
    g_w_in0, g_small = _run_exchange(_gather_exchange([shard("w_in", 0), small_w]), "weights_all_gather")
    n_dn, n_ff = DEPTH * 4 * 192, DEPTH * 3 * 704
    sm = g_small.reshape(N_DEV, -1)
    full_dn_conv = sm[:, :n_dn].reshape(N_DEV, DEPTH, 4, 192).transpose(1, 2, 0, 3).reshape(DEPTH, 4, 1536)
    full_ff_conv = _interleave_ff(
        sm[:, n_dn:n_dn + n_ff].reshape(N_DEV, DEPTH, 3, 704).transpose(1, 2, 0, 3).reshape(DEPTH, 3, 2 * D_FF))

    def small_weights(l):
        wts = dict(dn_conv_w=full_dn_conv[l], ffn_conv_w=full_ff_conv[l], ffn_conv_b=_interleave_ff(_row(ffn_conv_b[l])),
                   dn_a_log=_row(dn_a_log[l], 128), dn_dt_bias=_row(dn_dt_bias[l], 128))
        for n in ("dn_norm_w", "norm_pre_mix", "norm_post_mix", "norm_pre_ffn", "norm_post_ffn"):
            wts[n] = _row(local[n][l])
        return wts

    weights = [small_weights(l) for l in range(DEPTH)]
    weights[0]["w_in"] = matrix("w_in", g_w_in0)

    def gather_behind(wanted):
        def deliver(got):
            for (n, l), g in zip(wanted, got[0]):
                weights[l][n] = matrix(n, g)

        return [_gather_exchange([shard(n, l) for n, l in wanted])], deliver

    tables = _rope_tables()
    ex_attn0, on_attn0 = gather_behind([("w_out", 0), ("ffn_w_in", 0)])
    ex_delta0, on_delta0 = gather_behind([("ffn_w_out", 0), ("w_in", 1)])
    ex_attn1, on_attn1 = gather_behind([("w_out", 1), ("ffn_w_in", 1)])
    ex_delta1, on_delta1 = gather_behind([("ffn_w_out", 1)])
    act, saved0 = _layer_fwd(x[0], weights[0], tables, ex_attn0, ex_delta0, on_attn0, on_delta0)
    act, saved1 = _layer_fwd(act, weights[1], tables, ex_attn1, ex_delta1, on_attn1, on_delta1)
    loss_part, dact = _loss_fwd_bwd(act, loss_target[0])

    def to_devices(name, t):
        if name == "w_in":
            t = _w_in_rows_from_kernel_order(t)
        if name == "ffn_w_in":
            t = _deinterleave_ff_rows(t)
        return t.reshape(N_DEV, t.shape[0] // N_DEV, t.shape[1])

    def pair_sums(names, layer, to_dev, from_sibling):
        return [_pair_add(gd, r, core, "grads_pair_add_%s_%d" % (n, layer))
                for n, gd, r in zip(names, to_dev, from_sibling)]

    grads = [None] * DEPTH
    dact, grads[1], _ = _layer_bwd(dact, weights[1], saved1, tables)
    to_dev1 = [to_devices(n, grads[1][n]) for n in BIG]
    early = ("w_out", "ffn_w_in", "ffn_w_out")
    parts, stash = {}, {}

    def delta_exchanges(g, got_ffact):
        stash["to_dev0"] = [to_devices(n, g[n]) for n in early]
        return [_chips_exchange(pair_sums(BIG, 1, to_dev1, got_ffact[0])), _sibling_exchange(stash["to_dev0"])]

    def attn_exchanges(got_delta):
        for n, p in zip(BIG, got_delta[0]):
            parts[n, 1] = p
        return [_chips_exchange(pair_sums(early, 0, stash["to_dev0"], got_delta[1]))]

    dact, grads[0], got_attn = _layer_bwd(dact, weights[0], saved0, tables, [_sibling_exchange(to_dev1)],
                                          delta_exchanges, attn_exchanges)
    for n, p in zip(early, got_attn[0]):
        parts[n, 0] = p
    grad_x = dact[None]
    last = [to_devices("w_in", grads[0]["w_in"])]
    from_sibling = _run_exchange(_sibling_exchange(last), "grads_to_sibling")
    parts["w_in", 0], = _run_exchange(_chips_exchange(pair_sums(("w_in",), 0, last, from_sibling)), "grads_to_chips")

    def small_grad(name):
        t = jnp.stack([grads[l][name] for l in range(DEPTH)])
        if name in ("dn_a_log", "dn_dt_bias"):
            t = t[:, 0, :NDH]
        if name in ("ffn_conv_w", "ffn_conv_b"):
            t = _deinterleave_ff(t)
        return t.reshape(FULL_SHAPE[name])

    small_part = _pack([small_grad(n) for n in SMALL_GRAD_ORDER] + [loss_part[0, :1]], SMALL_GRAD_ROWS)
    small_sum = _all_gather_sum_small(small_part)
    small_g = dict(zip(SMALL_GRAD_ORDER + ("loss",), _unpack(small_sum, [FULL_SHAPE[n] for n in SMALL_GRAD_ORDER] + [(1,)])))
    loss = small_g["loss"][0]
    small_g["dn_conv_w"] = lax.dynamic_slice_in_dim(small_g["dn_conv_w"], dev * 192, 192, axis=2)
    small_g["ffn_conv_w"] = lax.dynamic_slice_in_dim(small_g["ffn_conv_w"], dev * 704, 704, axis=2)

    out_g, out_d, out_m, out_v = {}, {}, {}, {}
    for n in BIG:
        p = [parts[n, l] for l in range(DEPTH)]
        if n in COLUMN_SHARDED:
            p = [_sum_parts(t, "grad_sum_%s_%d" % (n, l)).T[None] for l, t in enumerate(p)]
        out_g[n], out_d[n], out_m[n], out_v[n] = _adamw_sharded(p, local[n], mom_m[n], mom_v[n], ADAM_ROWS[n], "adamw_" + n)
    shapes = [small_g[n].shape for n in SMALL_GRAD_ORDER]
    d_s, m_s, v_s = _adamw_small(_pack([small_g[n] for n in SMALL_GRAD_ORDER], SMALL_ADAM_ROWS),
                                 _pack([local[n] for n in SMALL_GRAD_ORDER], SMALL_ADAM_ROWS),
                                 _pack([mom_m[n] for n in SMALL_GRAD_ORDER], SMALL_ADAM_ROWS),
                                 _pack([mom_v[n] for n in SMALL_GRAD_ORDER], SMALL_ADAM_ROWS))
    for n, d, m, v in zip(SMALL_GRAD_ORDER, _unpack(d_s, shapes), _unpack(m_s, shapes), _unpack(v_s, shapes)):
        out_g[n], out_d[n], out_m[n], out_v[n] = small_g[n], d, m, v
    return (loss, grad_x, *[out_g[n] for n in WEIGHTS], *[out_d[n] for n in WEIGHTS],
            *[out_m[n] for n in WEIGHTS], *[out_v[n] for n in WEIGHTS])
```

```python
import functools
import math

import jax
import jax.numpy as jnp
from jax import lax
from jax.experimental import pallas as pl
from jax.experimental.pallas import tpu as pltpu

F32 = jnp.float32
BF16 = jnp.bfloat16
HI = lax.Precision.HIGHEST
MESH = pl.DeviceIdType.MESH

N_DEV = 8
SEQ = 2048
D_MODEL = 1024
DEPTH = 2
N_PAIR = 4
HEAD_DIM = 64
ATTN_W = 512
ATTN_BLK = 128
DILATIONS = (1, 4, 16)
SEGMENT_BLOCKS = (16, 4, 1)
N_BLK = SEQ // ATTN_BLK
NDH = 4
CH = 64
NCH = SEQ // CH
IN_COLS = 3592
IN_PAD = 3840
QKV_W = 3 * ATTN_W
DN_QKV_BLK0 = QKV_W // 128
DN_QKV_BLKS = 1536 // 128
DN_Z_COL = 3072
DN_TAIL_BLK = 3584 // 128
D_FF = 2816
FF_BLKS = D_FF // 128
EPS = 1e-6
NEG = -1e30
ROPE_THETA = 10000.0

ADAM_LR, ADAM_B1, ADAM_B2, ADAM_EPS, ADAM_WD, ADAM_STEP = 0.001, 0.9, 0.999, 1e-08, 0.01, 10

VMEM_LIMIT = 56 * 1024 * 1024


def _cp(*sem):
    return pltpu.CompilerParams(dimension_semantics=sem, vmem_limit_bytes=VMEM_LIMIT)


class Exchange:
    def __init__(self, operands, out_shapes, sems, start, middle, finish):
        self.operands, self.out_shapes, self.sems = list(operands), list(out_shapes), list(sems)
        self.start, self.middle, self.finish = start, middle, finish


HBM_SPEC = pl.BlockSpec(memory_space=pltpu.HBM)


def _hosted_call(body, *, name, steps, in_specs, out_specs, out_shape, scratch_shapes, operands, exchanges=(),
                 aliases=None):
    n_in, n_out, n_scr = len(in_specs), len(out_specs), len(scratch_shapes)

    def take(refs, pos, counts):
        groups = []
        for c in counts:
            groups.append(refs[pos:pos + c])
            pos += c
        return groups, pos

    def full_body(*refs):
        ins, pos = refs[:n_in], n_in
        ex_ins, pos = take(refs, pos, [len(e.operands) for e in exchanges])
        outs, pos = refs[pos:pos + n_out], pos + n_out
        ex_outs, pos = take(refs, pos, [len(e.out_shapes) for e in exchanges])
        scr, pos = refs[pos:pos + n_scr], pos + n_scr
        ex_sems, pos = take(refs, pos, [len(e.sems) for e in exchanges])
        step = pl.program_id(0)
        for e, a, b, s in zip(exchanges, ex_ins, ex_outs, ex_sems):
            pl.when(step == 0)(functools.partial(e.start, a, b, s))
            if e.middle is not None:
                pl.when(step == steps // 2)(functools.partial(e.middle, a, b, s))
        body(*ins, *outs, *scr)
        for e, a, b, s in zip(exchanges, ex_ins, ex_outs, ex_sems):
            pl.when(step == steps - 1)(functools.partial(e.finish, a, b, s))

    n_ex_in = sum(len(e.operands) for e in exchanges)
    n_ex_out = sum(len(e.out_shapes) for e in exchanges)
    results = pl.pallas_call(
        full_body, name=name, grid=(steps,),
        in_specs=list(in_specs) + [HBM_SPEC] * n_ex_in,
        out_specs=list(out_specs) + [HBM_SPEC] * n_ex_out,
        out_shape=list(out_shape) + [s for e in exchanges for s in e.out_shapes],
        scratch_shapes=list(scratch_shapes) + [s for e in exchanges for s in e.sems],
        input_output_aliases=aliases or {},
        compiler_params=_cp("arbitrary"),
    )(*operands, *[a for e in exchanges for a in e.operands])
    ex_results, _ = take(results, n_out, [len(e.out_shapes) for e in exchanges])
    return results[:n_out], ex_results


def _dot(a, b, dims, precision=None):
    if precision is None:
        a = a.astype(BF16)
        b = b.astype(BF16)
    return lax.dot_general(a, b, (dims, ((), ())), preferred_element_type=F32, precision=precision)


def _make_mm(precision):
    @jax.custom_vjp
    def nn(a, b):
        return _dot(a, b, ((1,), (0,)), precision)

    @jax.custom_vjp
    def nt(a, b):
        return _dot(a, b, ((1,), (1,)), precision)

    @jax.custom_vjp
    def tn(a, b):
        return _dot(a, b, ((0,), (0,)), precision)

    nn.defvjp(lambda a, b: (nn(a, b), (a, b)), lambda r, g: (nt(g, r[1]), tn(r[0], g)))
    nt.defvjp(lambda a, b: (nt(a, b), (a, b)), lambda r, g: (nn(g, r[1]), tn(g, r[0])))
    tn.defvjp(lambda a, b: (tn(a, b), (a, b)), lambda r, g: (nt(r[1], g), nn(r[0], g)))
    return nn, nt, tn


MM, MM_NT, MM_TN = _make_mm(None)


def _matmul(a, b, *, ta=False, tb=False, tm, tn, tk, name, out_dtype=F32):
    (k_dim, m_dim) = a.shape if ta else a.shape[::-1]
    (n_dim, k2) = b.shape if tb else b.shape[::-1]
    assert k_dim == k2 and m_dim % tm == 0 and n_dim % tn == 0 and k_dim % tk == 0, (a.shape, b.shape, tm, tn, tk)
    nk = k_dim // tk
    dims = ((0 if ta else 1,), (1 if tb else 0,))

    def body(a_ref, b_ref, o_ref, *acc):
        p = _dot(a_ref[...], b_ref[...], dims)
        if nk == 1:
            o_ref[...] = p.astype(out_dtype)
            return
        acc_ref, k = acc[0], pl.program_id(2)

        @pl.when(k == 0)
        def _():
            acc_ref[...] = p

        @pl.when(k > 0)
        def _():
            acc_ref[...] += p

        @pl.when(k == nk - 1)
        def _():
            o_ref[...] = acc_ref[...].astype(out_dtype)

    a_spec = pl.BlockSpec((tk, tm), lambda i, j, k: (k, i)) if ta else pl.BlockSpec((tm, tk), lambda i, j, k: (i, k))
    b_spec = pl.BlockSpec((tn, tk), lambda i, j, k: (j, k)) if tb else pl.BlockSpec((tk, tn), lambda i, j, k: (k, j))
    return pl.pallas_call(
        body, name=name,
        grid=(m_dim // tm, n_dim // tn, nk),
        in_specs=[a_spec, b_spec],
        out_specs=pl.BlockSpec((tm, tn), lambda i, j, k: (i, j)),
        out_shape=jax.ShapeDtypeStruct((m_dim, n_dim), out_dtype),
        scratch_shapes=[pltpu.VMEM((tm, tn), F32)] if nk > 1 else [],
        compiler_params=_cp("parallel", "parallel", "arbitrary"),
    )(a, b)


NORM_ROWS = 256


def _rms(x, w):
    return x * lax.rsqrt(jnp.mean(x * x, axis=1, keepdims=True) + EPS) * w


def _norm_fwd(x, w_row, name, out_dtype=BF16):
    def body(x_ref, w_ref, o_ref):
        o_ref[...] = _rms(x_ref[...], w_ref[...]).astype(out_dtype)

    return pl.pallas_call(
        body, name=name, grid=(SEQ // NORM_ROWS,),
        in_specs=[pl.BlockSpec((NORM_ROWS, D_MODEL), lambda i: (i, 0)), pl.BlockSpec((1, D_MODEL), lambda i: (0, 0))],
        out_specs=pl.BlockSpec((NORM_ROWS, D_MODEL), lambda i: (i, 0)),
        out_shape=jax.ShapeDtypeStruct((SEQ, D_MODEL), out_dtype),
        compiler_params=_cp("parallel"),
    )(x, w_row)


def _resnorm_fwd(x, f, w_row, name):
    def body(x_ref, f_ref, w_ref, o_ref):
        o_ref[...] = x_ref[...] + _rms(f_ref[...], w_ref[...])

    blk = pl.BlockSpec((NORM_ROWS, D_MODEL), lambda i: (i, 0))
    return pl.pallas_call(
        body, name=name, grid=(SEQ // NORM_ROWS,),
        in_specs=[blk, blk, pl.BlockSpec((1, D_MODEL), lambda i: (0, 0))],
        out_specs=blk, out_shape=jax.ShapeDtypeStruct((SEQ, D_MODEL), F32),
        compiler_params=_cp("parallel"),
    )(x, f, w_row)


def _norm_bwd(x, w_row, dy, add, name):
    has_add = add is not None

    def body(*refs):
        if has_add:
            x_ref, w_ref, dy_ref, add_ref, dx_ref, dw_ref = refs
        else:
            x_ref, w_ref, dy_ref, dx_ref, dw_ref = refs
        _, vjp = jax.vjp(_rms, x_ref[...], w_ref[...])
        dx, dw = vjp(dy_ref[...])
        dx_ref[...] = dx + add_ref[...] if has_add else dx

        @pl.when(pl.program_id(0) == 0)
        def _():
            dw_ref[...] = jnp.zeros_like(dw_ref)

        dw_ref[...] += dw

    blk = pl.BlockSpec((NORM_ROWS, D_MODEL), lambda i: (i, 0))
    row = pl.BlockSpec((1, D_MODEL), lambda i: (0, 0))
    ins = [x, w_row, dy] + ([add] if has_add else [])
    return pl.pallas_call(
        body, name=name, grid=(SEQ // NORM_ROWS,),
        in_specs=[blk, row, blk] + ([blk] if has_add else []),
        out_specs=[blk, row],
        out_shape=[jax.ShapeDtypeStruct((SEQ, D_MODEL), F32), jax.ShapeDtypeStruct((1, D_MODEL), F32)],
        compiler_params=_cp("arbitrary"),
    )(*ins)


def _loss_fwd_bwd(y, target):
    def body(y_ref, t_ref, loss_ref, dy_ref):
        err = y_ref[...] - t_ref[...]
        dy_ref[...] = err * (1.0 / D_MODEL)

        @pl.when(pl.program_id(0) == 0)
        def _():
            loss_ref[...] = jnp.zeros_like(loss_ref)

        part = jnp.sum(jnp.sum(err * err, axis=1, keepdims=True) * (1.0 / D_MODEL), axis=0, keepdims=True)
        loss_ref[...] += 0.5 * jnp.broadcast_to(part, loss_ref.shape)

    blk = pl.BlockSpec((NORM_ROWS, D_MODEL), lambda i: (i, 0))
    return pl.pallas_call(
        body, name="loss", grid=(SEQ // NORM_ROWS,),
        in_specs=[blk, blk],
        out_specs=[pl.BlockSpec((1, 128), lambda i: (0, 0)), blk],
        out_shape=[jax.ShapeDtypeStruct((1, 128), F32), jax.ShapeDtypeStruct((SEQ, D_MODEL), F32)],
        compiler_params=_cp("arbitrary"),
    )(y, target)


def _make_shift(j):
    def down(x):
        row = lax.broadcasted_iota(jnp.int32, x.shape, 0)
        return jnp.where(row >= j, pltpu.roll(x, j, 0), 0.0)

    def up(x):
        n = x.shape[0]
        row = lax.broadcasted_iota(jnp.int32, x.shape, 0)
        return jnp.where(row < n - j, pltpu.roll(x, n - j, 0), 0.0)

    f = jax.custom_vjp(down)
    f.defvjp(lambda x: (down(x), None), lambda _, g: (up(g),))
    return f


_SHIFT = {j: _make_shift(j) for j in (1, 2, 3)}


def _causal_conv(x, taps):
    n = len(taps)
    acc = x * taps[n - 1]
    for k in range(n - 1):
        acc = acc + _SHIFT[n - 1 - k](x) * taps[k]
    return acc


def _tap_rows(w_ref, lanes=slice(None)):
    return tuple(w_ref[k:k + 1, lanes] for k in range(w_ref.shape[0]))


def _sigmoid(x):
    return 1.0 / (1.0 + jnp.exp(-x))


def _silu(x):
    return x * _sigmoid(x)


def _softplus(x):
    return jnp.maximum(x, 0.0) + jnp.log(1.0 + jnp.exp(-jnp.abs(x)))


def _gelu_tanh(x):
    return 0.5 * x * (1.0 + jnp.tanh(math.sqrt(2.0 / math.pi) * (x + 0.044715 * (x * x * x))))


def _dnconv_fn(x, taps):
    return _silu(_causal_conv(x, taps))


def _dnconv_fwd(proj, conv_w):
    def body(x_ref, w_ref, o_ref):
        o_ref[...] = _dnconv_fn(x_ref[...], _tap_rows(w_ref))

    return pl.pallas_call(
        body, name="dnconv_fwd", grid=(DN_QKV_BLKS,),
        in_specs=[pl.BlockSpec((SEQ, 128), lambda j: (0, DN_QKV_BLK0 + j)), pl.BlockSpec((4, 128), lambda j: (0, j))],
        out_specs=pl.BlockSpec((SEQ, 128), lambda j: (0, j)),
        out_shape=jax.ShapeDtypeStruct((SEQ, 1536), F32),
        compiler_params=_cp("parallel"),
    )(proj, conv_w)


def _dnconv_bwd(proj, conv_w, dc, dproj):
    def body(x_ref, w_ref, dc_ref, _, dx_ref, dw_ref):
        _, vjp = jax.vjp(_dnconv_fn, x_ref[...], _tap_rows(w_ref))
        dx, dw = vjp(dc_ref[...])
        dx_ref[...] = dx
        for k, row in enumerate(dw):
            dw_ref[k:k + 1, :] = row

    return pl.pallas_call(
        body, name="dnconv_bwd", grid=(DN_QKV_BLKS,),
        in_specs=[pl.BlockSpec((SEQ, 128), lambda j: (0, DN_QKV_BLK0 + j)), pl.BlockSpec((4, 128), lambda j: (0, j)),
                  pl.BlockSpec((SEQ, 128), lambda j: (0, j)), pl.BlockSpec(memory_space=pl.ANY)],
        out_specs=[pl.BlockSpec((SEQ, 128), lambda j: (0, DN_QKV_BLK0 + j)), pl.BlockSpec((4, 128), lambda j: (0, j))],
        out_shape=[jax.ShapeDtypeStruct((SEQ, IN_PAD), F32), jax.ShapeDtypeStruct((4, 1536), F32)],
        input_output_aliases={3: 0},
        compiler_params=_cp("parallel"),
    )(proj, conv_w, dc, dproj)


def _ffact_fn(pg, pu, wg, wu, bg, bu):
    return _gelu_tanh(_causal_conv(pg, wg) + bg) * (_causal_conv(pu, wu) + bu)


def _ffact_args(p_ref, w_ref, b_ref):
    g, u = slice(0, 128), slice(128, 256)
    return (p_ref[:, g], p_ref[:, u], _tap_rows(w_ref, g), _tap_rows(w_ref, u), b_ref[:, g], b_ref[:, u])


def _ffact_fwd(pre, conv_w, conv_b):
    def body(p_ref, w_ref, b_ref, o_ref):
        o_ref[...] = _ffact_fn(*_ffact_args(p_ref, w_ref, b_ref)).astype(BF16)

    return pl.pallas_call(
        body, name="ffact_fwd", grid=(FF_BLKS,),
        in_specs=[pl.BlockSpec((SEQ, 256), lambda j: (0, j)), pl.BlockSpec((3, 256), lambda j: (0, j)),
                  pl.BlockSpec((1, 256), lambda j: (0, j))],
        out_specs=pl.BlockSpec((SEQ, 128), lambda j: (0, j)),
        out_shape=jax.ShapeDtypeStruct((SEQ, D_FF), BF16),
        compiler_params=_cp("parallel"),
    )(pre, conv_w, conv_b)


def _ffact_bwd(pre, conv_w, conv_b, dact, exchanges=()):
    def body(p_ref, w_ref, b_ref, da_ref, dp_ref, dw_ref, db_ref):
        _, vjp = jax.vjp(_ffact_fn, *_ffact_args(p_ref, w_ref, b_ref))
        dpg, dpu, dwg, dwu, dbg, dbu = vjp(da_ref[...].astype(F32))
        dp_ref[:, 0:128] = dpg
        dp_ref[:, 128:256] = dpu
        for k in range(3):
            dw_ref[k:k + 1, 0:128] = dwg[k]
            dw_ref[k:k + 1, 128:256] = dwu[k]
        db_ref[:, 0:128] = dbg
        db_ref[:, 128:256] = dbu

    return _hosted_call(
        body, name="ffact_bwd", steps=FF_BLKS,
        in_specs=[pl.BlockSpec((SEQ, 256), lambda j: (0, j)), pl.BlockSpec((3, 256), lambda j: (0, j)),
                  pl.BlockSpec((1, 256), lambda j: (0, j)), pl.BlockSpec((SEQ, 128), lambda j: (0, j))],
        out_specs=[pl.BlockSpec((SEQ, 256), lambda j: (0, j)), pl.BlockSpec((3, 256), lambda j: (0, j)),
                   pl.BlockSpec((1, 256), lambda j: (0, j))],
        out_shape=[jax.ShapeDtypeStruct((SEQ, 2 * D_FF), F32), jax.ShapeDtypeStruct((3, 2 * D_FF), F32),
                   jax.ShapeDtypeStruct((1, 2 * D_FF), F32)],
        scratch_shapes=[], operands=(pre, conv_w, conv_b, dact), exchanges=exchanges)


def _interleave_ff(t):
    lead = t.shape[:-1]
    return t.reshape(lead + (2, FF_BLKS, 128)).swapaxes(-3, -2).reshape(lead + (2 * D_FF,))


def _deinterleave_ff(t):
    lead = t.shape[:-1]
    return t.reshape(lead + (FF_BLKS, 2, 128)).swapaxes(-3, -2).reshape(lead + (2 * D_FF,))


def _rope_tables():
    inv = 1.0 / (ROPE_THETA ** (jnp.arange(0, HEAD_DIM, 2, dtype=F32) / HEAD_DIM))
    ang = jnp.arange(SEQ, dtype=F32)[:, None] * inv[None, :]
    cos = jnp.tile(jnp.cos(ang), (1, 4))
    sin = jnp.tile(jnp.sin(ang), (1, 4))
    sign = jnp.where((jnp.arange(128) % HEAD_DIM) < HEAD_DIM // 2, -1.0, 1.0).astype(F32)
    return cos, sin * sign[None, :]


def _rope(x, cos, sin_signed):
    lane = lax.broadcasted_iota(jnp.int32, x.shape, 1)
    partner = jnp.where((lane % HEAD_DIM) < HEAD_DIM // 2, pltpu.roll(x, 128 - HEAD_DIM // 2, 1),
                        pltpu.roll(x, HEAD_DIM // 2, 1))
    return x * cos + partner * sin_signed


def _pairs_from_qkv(t):
    lead = t.shape[:-1]
    return t.reshape(lead + (3, N_PAIR, 128)).swapaxes(-3, -2).reshape(lead + (QKV_W,))


def _qkv_from_pairs(t):
    lead = t.shape[:-1]
    return t.reshape(lead + (N_PAIR, 3, 128)).swapaxes(-3, -2).reshape(lead + (QKV_W,))


def _head_masks():
    lane = lax.broadcasted_iota(jnp.int32, (1, 128), 1)
    return [(lane // HEAD_DIM) == h for h in range(2)]


def _both_heads(x):
    return jnp.concatenate([jnp.where(hm, x, 0.0)[None] for hm in _head_masks()], axis=0)


def _block_keys(branch, k_s, v_s, rows, prows, has_prev):
    a = lax.broadcasted_iota(jnp.int32, (ATTN_BLK, ATTN_BLK), 0)
    c = lax.broadcasted_iota(jnp.int32, (ATTN_BLK, ATTN_BLK), 1)
    keys, values, mask = k_s[rows, :], v_s[rows, :], c <= a
    if SEGMENT_BLOCKS[branch] > 1:
        keys = jnp.concatenate([k_s[prows, :], keys], axis=0)
        values = jnp.concatenate([v_s[prows, :], values], axis=0)
        mask = jnp.concatenate([(c >= a) & has_prev, mask], axis=1)
    twice = lambda t: jnp.broadcast_to(t[None], (2,) + t.shape)
    return twice(keys), twice(values), mask


def _block_rows(branch, t):
    d, per_seg = DILATIONS[branch], SEGMENT_BLOCKS[branch]
    if d == 1:
        start = pl.multiple_of(t * ATTN_BLK, ATTN_BLK)
        prev = pl.multiple_of(jnp.maximum(t - 1, 0) * ATTN_BLK, ATTN_BLK)
        return pl.ds(start, ATTN_BLK), pl.ds(prev, ATTN_BLK), t > 0
    r, n = t // per_seg, t % per_seg
    start = n * (ATTN_BLK * d) + r
    prev = jnp.maximum(n - 1, 0) * (ATTN_BLK * d) + r
    return pl.ds(start, ATTN_BLK, stride=d), pl.ds(prev, ATTN_BLK, stride=d), n > 0


def _attn_fwd(proj, cos, sin_signed, exchanges=()):
    scale = HEAD_DIM ** -0.5

    def body(qkv_ref, cos_ref, sin_ref, out_ref, lse_ref, q_s, k_s, v_s, *branch_s):
        o_s, l_s = branch_s[:3], branch_s[3:]
        q_s[...] = _rope(qkv_ref[:, 0:128], cos_ref[...], sin_ref[...])
        k_s[...] = _rope(qkv_ref[:, 128:256], cos_ref[...], sin_ref[...])
        v_s[...] = qkv_ref[:, 256:384]
        heads = _head_masks()
        for branch in range(3):
            def block(t, carry, branch=branch):
                rows, prows, has_prev = _block_rows(branch, t)
                keys, values, mask = _block_keys(branch, k_s, v_s, rows, prows, has_prev)
                s = jnp.where(mask, BMM_NT(_both_heads(q_s[rows, :]), keys) * scale, NEG)
                m = jnp.max(s, axis=2, keepdims=True)
                e = jnp.exp(s - m)
                l = jnp.sum(e, axis=2, keepdims=True)
                o = BMM(e, values) / l
                lse_b = m + jnp.log(l)
                o_s[branch][rows, :] = jnp.where(heads[0], o[0], o[1])
                l_s[branch][rows, :] = jnp.where(heads[0], lse_b[0], lse_b[1])
                return carry

            lax.fori_loop(0, N_BLK, block, 0, unroll=2)
        l0, l1, l2 = l_s[0][...], l_s[1][...], l_s[2][...]
        m = jnp.maximum(jnp.maximum(l0, l1), l2)
        w0, w1, w2 = jnp.exp(l0 - m), jnp.exp(l1 - m), jnp.exp(l2 - m)
        den = w0 + w1 + w2
        out_ref[...] = (w0 * o_s[0][...] + w1 * o_s[1][...] + w2 * o_s[2][...]) / den
        lse_ref[...] = m + jnp.log(den)

    tab = pl.BlockSpec((SEQ, 128), lambda j: (0, 0))
    col = pl.BlockSpec((SEQ, 128), lambda j: (0, j))
    return _hosted_call(
        body, name="attn_fwd", steps=N_PAIR,
        in_specs=[pl.BlockSpec((SEQ, 384), lambda j: (0, j)), tab, tab],
        out_specs=[col, col],
        out_shape=[jax.ShapeDtypeStruct((SEQ, 2 * ATTN_W), F32), jax.ShapeDtypeStruct((SEQ, ATTN_W), F32)],
        scratch_shapes=[pltpu.VMEM((SEQ, 128), F32)] * 9,
        operands=(proj, cos, sin_signed), exchanges=exchanges)


def _attn_bwd(proj, cos, sin_signed, cat, lse, dcat, dproj, exchanges=()):
    scale = HEAD_DIM ** -0.5

    def body(qkv_ref, cos_ref, sin_ref, out_ref, lse_ref, do_ref, _, dqkv_ref, q_s, k_s, v_s, dq_s, dk_s, dv_s,
             dod_s):
        q_s[...] = _rope(qkv_ref[:, 0:128], cos_ref[...], sin_ref[...])
        k_s[...] = _rope(qkv_ref[:, 128:256], cos_ref[...], sin_ref[...])
        v_s[...] = qkv_ref[:, 256:384]
        dq_s[...] = jnp.zeros_like(dq_s)
        dk_s[...] = jnp.zeros_like(dk_s)
        dv_s[...] = jnp.zeros_like(dv_s)
        dod_s[...] = do_ref[...] * out_ref[...]
        heads = _head_masks()
        for branch in range(3):
            def block(t, carry, branch=branch):
                rows, prows, has_prev = _block_rows(branch, t)
                keys, values, mask = _block_keys(branch, k_s, v_s, rows, prows, has_prev)
                q2, do2 = _both_heads(q_s[rows, :]), _both_heads(do_ref[rows, :])
                lse_b, dod = lse_ref[rows, :], dod_s[rows, :]
                lse2 = jnp.concatenate(
                    [jnp.max(jnp.where(hm, lse_b, NEG), axis=1, keepdims=True)[None] for hm in heads], axis=0)
                delta = jnp.concatenate(
                    [jnp.sum(jnp.where(hm, dod, 0.0), axis=1, keepdims=True)[None] for hm in heads], axis=0)
                p = jnp.exp(jnp.where(mask, BMM_NT(q2, keys) * scale, NEG) - lse2)
                ds = p * (BMM_NT(do2, values) - delta) * scale
                dq = BMM(ds, keys)
                dk = BMM_TN(ds, q2)
                dv = BMM_TN(p, do2)
                dk, dv = dk[0] + dk[1], dv[0] + dv[1]
                dq_s[rows, :] += jnp.where(heads[0], dq[0], dq[1])
                if SEGMENT_BLOCKS[branch] > 1:
                    dk_s[rows, :] += dk[ATTN_BLK:]
                    dv_s[rows, :] += dv[ATTN_BLK:]

                    @pl.when(has_prev)
                    def _():
                        dk_s[prows, :] += dk[:ATTN_BLK]
                        dv_s[prows, :] += dv[:ATTN_BLK]
                else:
                    dk_s[rows, :] += dk
                    dv_s[rows, :] += dv
                return carry

            lax.fori_loop(0, N_BLK, block, 0, unroll=2)
        dqkv_ref[:, 0:128] = _rope(dq_s[...], cos_ref[...], -sin_ref[...])
        dqkv_ref[:, 128:256] = _rope(dk_s[...], cos_ref[...], -sin_ref[...])
        dqkv_ref[:, 256:384] = dv_s[...]

    tab = pl.BlockSpec((SEQ, 128), lambda j: (0, 0))
    col = pl.BlockSpec((SEQ, 128), lambda j: (0, j))
    qkv = pl.BlockSpec((SEQ, 384), lambda j: (0, j))
    (dproj,), results = _hosted_call(
        body, name="attn_bwd", steps=N_PAIR,
        in_specs=[qkv, tab, tab, col, col, col, pl.BlockSpec(memory_space=pl.ANY)],
        out_specs=[qkv],
        out_shape=[jax.ShapeDtypeStruct((SEQ, IN_PAD), F32)],
        scratch_shapes=[pltpu.VMEM((SEQ, 128), F32)] * 7,
        operands=(proj, cos, sin_signed, cat, lse, dcat, dproj), exchanges=exchanges, aliases={6: 0})
    return dproj, results


def _bdot(a, b, dims, precision=None):
    if precision is None:
        a = a.astype(BF16)
        b = b.astype(BF16)
    return lax.dot_general(a, b, (dims, ((0,), (0,))), preferred_element_type=F32, precision=precision)


def _make_bmm(precision):
    @jax.custom_vjp
    def nn(a, b):
        return _bdot(a, b, ((2,), (1,)), precision)

    @jax.custom_vjp
    def nt(a, b):
        return _bdot(a, b, ((2,), (2,)), precision)

    @jax.custom_vjp
    def tn(a, b):
        return _bdot(a, b, ((1,), (1,)), precision)

    nn.defvjp(lambda a, b: (nn(a, b), (a, b)), lambda r, g: (nt(g, r[1]), tn(r[0], g)))
    nt.defvjp(lambda a, b: (nt(a, b), (a, b)), lambda r, g: (nn(g, r[1]), tn(g, r[0])))
    tn.defvjp(lambda a, b: (tn(a, b), (a, b)), lambda r, g: (nt(r[1], g), nn(r[0], g)))
    return nn, nt, tn


BMM, BMM_NT, BMM_TN = _make_bmm(None)
BMM3, BMM3_NT, BMM3_TN = _make_bmm(lax.Precision.HIGH)
MM3, _, _ = _make_mm(lax.Precision.HIGH)


def _head_lanes(t, off):
    lane = lax.broadcasted_iota(jnp.int32, (1, 128), 1)
    return jnp.concatenate(
        [jnp.sum(t * (lane == off + h).astype(F32), axis=1, keepdims=True)[None] for h in range(NDH)], axis=0)


@jax.custom_vjp
def _unit_lower_inverse(a_mat):
    c = a_mat.shape[1]
    eye = (lax.broadcasted_iota(jnp.int32, (c, c), 0) == lax.broadcasted_iota(jnp.int32, (c, c), 1)).astype(F32)
    power = -a_mat
    t_inv = eye + power
    for _ in range(5):
        power = BMM3(power, power)
        t_inv = t_inv + BMM3(t_inv, power)
    return t_inv


def _unit_lower_inverse_fwd(a_mat):
    t_inv = _unit_lower_inverse(a_mat)
    return t_inv, t_inv


def _unit_lower_inverse_bwd(t_inv, d_inv):
    return (-BMM3_NT(BMM3_TN(t_inv, d_inv), t_inv),)


_unit_lower_inverse.defvjp(_unit_lower_inverse_fwd, _unit_lower_inverse_bwd)


def _delta_chunk(qr, kr, vr, z, tail, alog_row, dt_row, nw, state):
    c = qr.shape[1]
    beta = _sigmoid(_head_lanes(tail, 0))
    g = -jnp.exp(_head_lanes(alog_row, 0)) * _softplus(_head_lanes(tail, NDH) + _head_lanes(dt_row, 0))

    q = qr * lax.rsqrt(jnp.sum(qr * qr, axis=2, keepdims=True) + EPS) * (128 ** -0.5)
    k = kr * lax.rsqrt(jnp.sum(kr * kr, axis=2, keepdims=True) + EPS)

    ri = lax.broadcasted_iota(jnp.int32, (c, c), 0)
    ci = lax.broadcasted_iota(jnp.int32, (c, c), 1)
    tril = ri >= ci
    lane = lax.broadcasted_iota(jnp.int32, (1, 128), 1)
    g_lanes = sum(g[h] * (lane == h).astype(F32) for h in range(NDH))
    gc = _head_lanes(MM3(tril.astype(F32), g_lanes), 0)
    g_row = jnp.swapaxes(jnp.broadcast_to(gc, (NDH, c, c)), 1, 2)
    decay = jnp.where(tril, jnp.exp(jnp.where(tril, gc - g_row, 0.0)), 0.0)
    kb = k * beta
    t_inv = _unit_lower_inverse(jnp.where(ri > ci, BMM_NT(kb, k) * decay, 0.0))
    eg = jnp.exp(gc)
    u = BMM(t_inv, vr * beta)
    w = BMM(t_inv, kb * eg)
    qk = BMM_NT(q, k) * decay
    g_tot = jnp.sum(g, axis=1, keepdims=True)
    v_new = u - BMM(w, state)
    o = BMM(q * eg, state) + BMM(qk, v_new)
    new_state = state * jnp.exp(g_tot) + BMM_TN(k * jnp.exp(g_tot - gc), v_new)
    on = o * lax.rsqrt(jnp.mean(o * o, axis=2, keepdims=True) + EPS) * nw
    return on * _silu(z), new_state


def _heads(v, off=0):
    return jnp.concatenate([v[None, :, off + 128 * h:off + 128 * (h + 1)] for h in range(NDH)], axis=0)


def _unheads(t):
    return jnp.concatenate([t[h] for h in range(NDH)], axis=1)


def _delta_fwd(c_qkv, proj, alog_row, dt_row, nw, cat, exchanges=()):
    def body(c_ref, z_ref, tail_ref, al_ref, dt_ref, nw_ref, _, y_ref, st_ref, state):
        @pl.when(pl.program_id(0) == 0)
        def _():
            state[...] = jnp.zeros_like(state)

        cv = c_ref[...]
        st_ref[0] = state[...]
        y, new_state = _delta_chunk(_heads(cv), _heads(cv, 512), _heads(cv, 1024), _heads(z_ref[...]), tail_ref[...],
                                    al_ref[...], dt_ref[...], nw_ref[...], state[...])
        y_ref[...] = _unheads(y)
        state[...] = new_state

    row = pl.BlockSpec((1, 128), lambda n: (0, 0))
    return _hosted_call(
        body, name="delta_fwd", steps=NCH,
        in_specs=[pl.BlockSpec((CH, 1536), lambda n: (n, 0)), pl.BlockSpec((CH, 512), lambda n: (n, DN_Z_COL // 512)),
                  pl.BlockSpec((CH, 128), lambda n: (n, DN_TAIL_BLK)), row, row, row, pl.BlockSpec(memory_space=pl.ANY)],
        out_specs=[pl.BlockSpec((CH, 512), lambda n: (n, 1)),
                   pl.BlockSpec((1, NDH, 128, 128), lambda n: (n, 0, 0, 0))],
        out_shape=[jax.ShapeDtypeStruct((SEQ, 2 * ATTN_W), F32), jax.ShapeDtypeStruct((NCH, NDH, 128, 128), F32)],
        scratch_shapes=[pltpu.VMEM((NDH, 128, 128), F32)],
        operands=(c_qkv, proj, proj, alog_row, dt_row, nw, cat), exchanges=exchanges, aliases={6: 0})


def _delta_bwd(c_qkv, proj, alog_row, dt_row, nw, states, dcat, exchanges=()):
    def body(c_ref, z_ref, tail_ref, al_ref, dt_ref, nw_ref, st_ref, dy_ref,
             dp_ref, dc_ref, dal_ref, ddt_ref, dnw_ref, dstate):
        @pl.when(pl.program_id(0) == 0)
        def _():
            dstate[...] = jnp.zeros_like(dstate)
            dal_ref[...] = jnp.zeros_like(dal_ref)
            ddt_ref[...] = jnp.zeros_like(ddt_ref)
            dnw_ref[...] = jnp.zeros_like(dnw_ref)

        cv = c_ref[...]
        _, vjp = jax.vjp(_delta_chunk, _heads(cv), _heads(cv, 512), _heads(cv, 1024), _heads(z_ref[...]),
                         tail_ref[...], al_ref[...], dt_ref[...], nw_ref[...], st_ref[0])
        dq, dk, dv, dz, dtail, dal, ddt, dnw, dst = vjp((_heads(dy_ref[...]), dstate[...]))
        dstate[...] = dst
        dc_ref[...] = jnp.concatenate([_unheads(dq), _unheads(dk), _unheads(dv)], axis=1)
        dp_ref[...] = jnp.concatenate([_unheads(dz), dtail, jnp.zeros((CH, 128), F32)], axis=1)
        dal_ref[...] += dal
        ddt_ref[...] += ddt
        dnw_ref[...] += dnw

    rev = lambda n: NCH - 1 - n
    row = pl.BlockSpec((1, 128), lambda n: (0, 0))
    return _hosted_call(
        body, name="delta_bwd", steps=NCH,
        in_specs=[pl.BlockSpec((CH, 1536), lambda n: (rev(n), 0)),
                  pl.BlockSpec((CH, 512), lambda n: (rev(n), DN_Z_COL // 512)),
                  pl.BlockSpec((CH, 128), lambda n: (rev(n), DN_TAIL_BLK)), row, row, row,
                  pl.BlockSpec((1, NDH, 128, 128), lambda n: (rev(n), 0, 0, 0)),
                  pl.BlockSpec((CH, 512), lambda n: (rev(n), 1))],
        out_specs=[pl.BlockSpec((CH, 768), lambda n: (rev(n), DN_Z_COL // 768)),
                   pl.BlockSpec((CH, 1536), lambda n: (rev(n), 0)), row, row, row],
        out_shape=[jax.ShapeDtypeStruct((SEQ, IN_PAD), F32), jax.ShapeDtypeStruct((SEQ, 1536), F32)]
        + [jax.ShapeDtypeStruct((1, 128), F32)] * 3,
        scratch_shapes=[pltpu.VMEM((NDH, 128, 128), F32)],
        operands=(c_qkv, proj, proj, alog_row, dt_row, nw, states, dcat), exchanges=exchanges)


def _place():
    x, y, c = lax.axis_index("x"), lax.axis_index("y"), lax.axis_index("c")
    other_chips = [(1 - x, y), (x, 1 - y), (1 - x, 1 - y)]
    return x, y, c, other_chips


def _gather_exchange(shards):
    n = len(shards)

    def copies(ins, outs, sems):
        send_sems, recv_sems, local_sems = sems
        x, y, c, chips = _place()
        me, sibling = (x, y, c), (x, y, 1 - c)

        def copy(b, k, block, to, src=None):
            slot = outs[b].at[4 * block[0] + 2 * block[1] + block[2]]
            return pltpu.make_async_remote_copy(
                src_ref=slot if src is None else src, dst_ref=slot,
                send_sem=send_sems.at[b, k], recv_sem=recv_sems.at[b, k], device_id=to, device_id_type=MESH)

        mine = [pltpu.make_async_copy(ins[b], outs[b].at[4 * x + 2 * y + c], local_sems.at[b]) for b in range(n)]
        first = []
        for b in range(n):
            first.append(copy(b, 0, me, sibling, src=ins[b]))
            first += [copy(b, 1 + j, me, (*chip, c), src=ins[b]) for j, chip in enumerate(chips)]
        over_ici = [copy(b, 1 + j, (*chip, c), me) for b in range(n) for j, chip in enumerate(chips)]
        passed = [copy(b, 4 + j, (*chip, c), sibling) for b in range(n) for j, chip in enumerate(chips)]
        from_sibling = []
        for b in range(n):
            from_sibling.append(copy(b, 0, sibling, me))
            from_sibling += [copy(b, 4 + j, (*chip, 1 - c), me) for j, chip in enumerate(chips)]
        return mine, first, over_ici, passed, from_sibling

    def start(ins, outs, sems):
        mine, first, _, _, _ = copies(ins, outs, sems)
        for cp in mine + first:
            cp.start()

    def middle(ins, outs, sems):
        _, _, over_ici, passed, _ = copies(ins, outs, sems)
        for arrived, onward in zip(over_ici, passed):
            arrived.wait_recv()
            onward.start()

    def finish(ins, outs, sems):
        mine, first, _, passed, from_sibling = copies(ins, outs, sems)
        for cp in from_sibling:
            cp.wait_recv()
        for cp in first + passed:
            cp.wait_send()
        for cp in mine:
            cp.wait()

    return Exchange(shards, [jax.ShapeDtypeStruct((N_DEV,) + s.shape, s.dtype) for s in shards],
                    [pltpu.SemaphoreType.DMA((n, 7)), pltpu.SemaphoreType.DMA((n, 7)), pltpu.SemaphoreType.DMA((n,))],
                    start, middle, finish)


def _sibling_exchange(gs):
    n = len(gs)

    def copies(ins, outs, sems):
        send_sems, recv_sems = sems
        x, y, c, _ = _place()
        return [pltpu.make_async_remote_copy(
            src_ref=ins[b].at[2 * p + (1 - c)], dst_ref=outs[b].at[p],
            send_sem=send_sems.at[b, p], recv_sem=recv_sems.at[b, p],
            device_id=(x, y, 1 - c), device_id_type=MESH) for b in range(n) for p in range(4)]

    def start(ins, outs, sems):
        for cp in copies(ins, outs, sems):
            cp.start()

    def finish(ins, outs, sems):
        for cp in copies(ins, outs, sems):
            cp.wait()

    return Exchange(gs, [jax.ShapeDtypeStruct((4,) + g.shape[1:], g.dtype) for g in gs],
                    [pltpu.SemaphoreType.DMA((n, 4)), pltpu.SemaphoreType.DMA((n, 4))], start, None, finish)


def _chips_exchange(hs):
    n = len(hs)

    def copies(ins, outs, sems):
        send_sems, recv_sems, local_sems = sems
        x, y, c, chips = _place()
        my_chip = 2 * x + y
        local = [pltpu.make_async_copy(ins[b].at[my_chip], outs[b].at[my_chip], local_sems.at[b]) for b in range(n)]
        sends, arrivals = [], []
        for b in range(n):
            for k, (px, py) in enumerate(chips):
                peer = 2 * px + py
                sends.append(pltpu.make_async_remote_copy(
                    src_ref=ins[b].at[peer], dst_ref=outs[b].at[my_chip],
                    send_sem=send_sems.at[b, k], recv_sem=recv_sems.at[b, k],
                    device_id=(px, py, c), device_id_type=MESH))
                arrivals.append(pltpu.make_async_remote_copy(
                    src_ref=ins[b].at[peer], dst_ref=outs[b].at[peer],
                    send_sem=send_sems.at[b, k], recv_sem=recv_sems.at[b, k],
                    device_id=(px, py, c), device_id_type=MESH))
        return local, sends, arrivals

    def start(ins, outs, sems):
        local, sends, _ = copies(ins, outs, sems)
        for cp in local + sends:
            cp.start()

    def finish(ins, outs, sems):
        local, sends, arrivals = copies(ins, outs, sems)
        for cp in arrivals:
            cp.wait_recv()
        for cp in sends:
            cp.wait_send()
        for cp in local:
            cp.wait()

    return Exchange(hs, [jax.ShapeDtypeStruct(h.shape, h.dtype) for h in hs],
                    [pltpu.SemaphoreType.DMA((n, 3)), pltpu.SemaphoreType.DMA((n, 3)), pltpu.SemaphoreType.DMA((n,))],
                    start, None, finish)


def _run_exchange(exchange, name):
    n_in, n_out = len(exchange.operands), len(exchange.out_shapes)

    def body(*refs):
        ins, outs, sems = refs[:n_in], refs[n_in:n_in + n_out], refs[n_in + n_out:]
        exchange.start(ins, outs, sems)
        if exchange.middle is not None:
            exchange.middle(ins, outs, sems)
        exchange.finish(ins, outs, sems)

    return pl.pallas_call(
        body, name=name,
        in_specs=[HBM_SPEC] * n_in, out_specs=[HBM_SPEC] * n_out,
        out_shape=exchange.out_shapes, scratch_shapes=exchange.sems,
    )(*exchange.operands)


def _pair_add(g, r, core, name):
    _, nr, nc = g.shape
    tr = nr // 2 if nr % 32 == 0 else nr

    def body(core_ref, g_ref, r_ref, o_ref):
        o_ref[...] = (g_ref[...].astype(F32) + r_ref[...].astype(F32)).astype(BF16)

    return pl.pallas_call(
        body, name=name,
        grid_spec=pltpu.PrefetchScalarGridSpec(
            num_scalar_prefetch=1, grid=(4, nr // tr),
            in_specs=[pl.BlockSpec((1, tr, nc), lambda p, i, core: (2 * p + core[0], i, 0)),
                      pl.BlockSpec((1, tr, nc), lambda p, i, core: (p, i, 0))],
            out_specs=pl.BlockSpec((1, tr, nc), lambda p, i, core: (p, i, 0))),
        out_shape=jax.ShapeDtypeStruct(r.shape, BF16),
        compiler_params=_cp("parallel", "parallel"),
    )(core, g, r)


def _all_gather_sum_small(v):
    rows = v.shape[0]

    def body(x_ref, sum_ref, out_ref, send_sems, recv_sems, local_sem):
        x, y, c, chips = _place()
        me, sibling = (x, y, c), (x, y, 1 - c)

        def block(px, py, pc):
            return out_ref.at[pl.ds((4 * px + 2 * py + pc) * rows, rows), :]

        def copy(k, blk, to, src=None):
            return pltpu.make_async_remote_copy(
                src_ref=block(*blk) if src is None else src, dst_ref=block(*blk),
                send_sem=send_sems.at[k], recv_sem=recv_sems.at[k], device_id=to, device_id_type=MESH)

        mine = pltpu.make_async_copy(x_ref, block(*me), local_sem)
        mine.start()
        first = [copy(0, me, sibling, src=x_ref)]
        first += [copy(1 + j, me, (*chip, c), src=x_ref) for j, chip in enumerate(chips)]
        for cp in first:
            cp.start()
        passed = [copy(4 + j, (*chip, c), sibling) for j, chip in enumerate(chips)]
        for j, chip in enumerate(chips):
            copy(1 + j, (*chip, c), me).wait_recv()
            passed[j].start()
        copy(0, sibling, me).wait_recv()
        for j, chip in enumerate(chips):
            copy(4 + j, (*chip, 1 - c), me).wait_recv()
        for cp in first + passed:
            cp.wait_send()
        mine.wait()
        total = out_ref[pl.ds(0, rows), :]
        for d in range(1, N_DEV):
            total = total + out_ref[pl.ds(d * rows, rows), :]
        sum_ref[...] = total

    vm = pl.BlockSpec(memory_space=pltpu.VMEM)
    return pl.pallas_call(
        body, name="small_all_reduce",
        in_specs=[vm], out_specs=[vm],
        out_shape=[jax.ShapeDtypeStruct((rows, 128), F32)],
        scratch_shapes=[pltpu.VMEM((N_DEV * rows, 128), F32), pltpu.SemaphoreType.DMA((7,)),
                        pltpu.SemaphoreType.DMA((7,)), pltpu.SemaphoreType.DMA],
    )(v)[0]


def _adamw(w, g, m, v):
    m = ADAM_B1 * m + (1.0 - ADAM_B1) * g
    v = ADAM_B2 * v + (1.0 - ADAM_B2) * (g * g)
    m_hat = m / (1.0 - ADAM_B1 ** ADAM_STEP)
    v_hat = v / (1.0 - ADAM_B2 ** ADAM_STEP)
    delta = -ADAM_LR * (m_hat / (jnp.sqrt(v_hat) + ADAM_EPS) + ADAM_WD * w)
    return delta, m, v


ADAM_ROWS = dict(w_in=256, w_out=128, ffn_w_in=256, ffn_w_out=176)


def _sum_chips(p):
    p = p.astype(F32)
    return (p[0] + p[1]) + (p[2] + p[3])


def _sum_parts(parts, name):
    def body(p_ref, g_ref):
        g_ref[...] = _sum_chips(p_ref[...])

    return pl.pallas_call(body, name=name, out_shape=jax.ShapeDtypeStruct(parts.shape[1:], F32),
                          compiler_params=_cp())(parts)


def _adamw_sharded(parts, w, m, v, tr, name):
    nl, nr, nc = w.shape
    n_parts = parts[0].shape[0]

    def body(*refs):
        p_refs, (w_ref, m_ref, v_ref, g_ref, d_ref, nm_ref, nv_ref) = refs[:nl], refs[nl:]
        layer = pl.program_id(0)
        p = p_refs[0][...]
        for l in range(1, nl):
            p = jnp.where(layer == l, p_refs[l][...], p)
        g = _sum_chips(p) if n_parts == 4 else p[0]
        delta, nm, nv = _adamw(w_ref[0], g, m_ref[0], v_ref[0])
        g_ref[0] = g
        d_ref[0] = delta
        nm_ref[0] = nm
        nv_ref[0] = nv

    blk = pl.BlockSpec((1, tr, nc), lambda l, i: (l, i, 0))
    return pl.pallas_call(
        body, name=name, grid=(nl, nr // tr),
        in_specs=[pl.BlockSpec((n_parts, tr, nc), lambda l, i: (0, i, 0))] * nl + [blk, blk, blk],
        out_specs=[blk] * 4,
        out_shape=[jax.ShapeDtypeStruct(w.shape, F32)] * 4,
        compiler_params=_cp("parallel", "parallel"),
    )(*parts, w, m, v)


def _adamw_small(g, w, m, v):
    def body(g_ref, w_ref, m_ref, v_ref, d_ref, nm_ref, nv_ref):
        delta, nm, nv = _adamw(w_ref[...], g_ref[...], m_ref[...], v_ref[...])
        d_ref[...] = delta
        nm_ref[...] = nm
        nv_ref[...] = nv

    return pl.pallas_call(
        body, name="adamw_small",
        out_shape=[jax.ShapeDtypeStruct(g.shape, F32)] * 3,
    )(g, w, m, v)


def _pack(arrays, rows):
    flat = jnp.concatenate([a.reshape(-1).astype(F32) for a in arrays])
    return jnp.pad(flat, (0, rows * 128 - flat.shape[0])).reshape(rows, 128)


def _unpack(packed, shapes):
    flat = packed.reshape(-1)
    out, off = [], 0
    for s in shapes:
        n = math.prod(s)
        out.append(flat[off:off + n].reshape(s))
        off += n
    return out


def _row(v, width=None):
    v = v.reshape(1, -1)
    return v if width is None else jnp.pad(v, ((0, 0), (0, width - v.shape[1])))


def _layer_fwd(x, wts, tables, attn_exchanges=(), delta_exchanges=(), on_attn=None, on_delta=None):
    h = _norm_fwd(x, wts["norm_pre_mix"], "norm_pre_mix")
    proj = _matmul(h, wts["w_in"], tb=True, tm=512, tn=768, tk=1024, name="mm_proj")
    (cat, lse), got = _attn_fwd(proj, *tables, exchanges=attn_exchanges)
    if on_attn is not None:
        on_attn(got)
    c_qkv = _dnconv_fwd(proj, wts["dn_conv_w"])
    (cat, states), got = _delta_fwd(c_qkv, proj, wts["dn_a_log"], wts["dn_dt_bias"], wts["dn_norm_w"], cat,
                                    exchanges=delta_exchanges)
    if on_delta is not None:
        on_delta(got)
    mix = _matmul(cat, wts["w_out"], tm=512, tn=1024, tk=1024, name="mm_mix")
    x1 = _resnorm_fwd(x, mix, wts["norm_post_mix"], "norm_post_mix")
    h2 = _norm_fwd(x1, wts["norm_pre_ffn"], "norm_pre_ffn")
    pre = _matmul(h2, wts["ffn_w_in"], tb=True, tm=512, tn=512, tk=1024, name="mm_ffn_in")
    act = _ffact_fwd(pre, wts["ffn_conv_w"], wts["ffn_conv_b"])
    f = _matmul(act, wts["ffn_w_out"], tm=512, tn=1024, tk=D_FF, name="mm_ffn_out")
    x2 = _resnorm_fwd(x1, f, wts["norm_post_ffn"], "norm_post_ffn")
    saved = dict(x=x, h=h, proj=proj, lse=lse, c_qkv=c_qkv, states=states, cat=cat, mix=mix, x1=x1, h2=h2, pre=pre,
                 act=act, f=f)
    return x2, saved


def _layer_bwd(dx2, wts, s, tables, ffact_exchanges=(), delta_exchanges=None, attn_exchanges=None):
    g = {}
    df, g["norm_post_ffn"] = _norm_bwd(s["f"], wts["norm_post_ffn"], dx2, None, "norm_post_ffn_bwd")
    dact = _matmul(df, wts["ffn_w_out"], tb=True, tm=512, tn=1408, tk=1024, name="mm_dact", out_dtype=BF16)
    g["ffn_w_out"] = _matmul(s["act"], df, ta=True, tm=1408, tn=512, tk=SEQ, name="mm_dw_ffn_out", out_dtype=BF16)
    (dpre, g["ffn_conv_w"], g["ffn_conv_b"]), got = _ffact_bwd(s["pre"], wts["ffn_conv_w"], wts["ffn_conv_b"], dact,
                                                               exchanges=ffact_exchanges)
    dh2 = _matmul(dpre, wts["ffn_w_in"], tm=1024, tn=1024, tk=1408, name="mm_dh2")
    g["ffn_w_in"] = _matmul(dpre, s["h2"], ta=True, tm=512, tn=1024, tk=SEQ, name="mm_dw_ffn_in", out_dtype=BF16)
    dx1, g["norm_pre_ffn"] = _norm_bwd(s["x1"], wts["norm_pre_ffn"], dh2, dx2, "norm_pre_ffn_bwd")
    dmix, g["norm_post_mix"] = _norm_bwd(s["mix"], wts["norm_post_mix"], dx1, None, "norm_post_mix_bwd")
    dcat = _matmul(dmix, wts["w_out"], tb=True, tm=512, tn=1024, tk=1024, name="mm_dcat")
    g["w_out"] = _matmul(s["cat"], dmix, ta=True, tm=1024, tn=512, tk=SEQ, name="mm_dw_out", out_dtype=BF16)
    (dproj, dc, g["dn_a_log"], g["dn_dt_bias"], g["dn_norm_w"]), got = _delta_bwd(
        s["c_qkv"], s["proj"], wts["dn_a_log"], wts["dn_dt_bias"], wts["dn_norm_w"], s["states"], dcat,
        exchanges=delta_exchanges(g, got) if delta_exchanges is not None else ())
    dproj, got = _attn_bwd(s["proj"], *tables, s["cat"], s["lse"], dcat, dproj,
                           exchanges=attn_exchanges(got) if attn_exchanges is not None else ())
    dproj, g["dn_conv_w"] = _dnconv_bwd(s["proj"], wts["dn_conv_w"], dc, dproj)
    dh = _matmul(dproj, wts["w_in"], tm=1024, tn=1024, tk=1280, name="mm_dh")
    g["w_in"] = _matmul(dproj, s["h"], ta=True, tm=768, tn=1024, tk=SEQ, name="mm_dw_in", out_dtype=BF16)
    dx, g["norm_pre_mix"] = _norm_bwd(s["x"], wts["norm_pre_mix"], dh, dx1, "norm_pre_mix_bwd")
    return dx, g, got


BIG = ("w_in", "w_out", "ffn_w_in", "ffn_w_out")
COLUMN_SHARDED = ("w_in", "ffn_w_in")
SMALL_SHARDED = ("dn_conv_w", "ffn_conv_w")
REPLICATED = ("dn_a_log", "dn_dt_bias", "dn_norm_w", "ffn_conv_b", "norm_pre_mix", "norm_post_mix", "norm_pre_ffn",
              "norm_post_ffn")
WEIGHTS = ("w_in", "dn_conv_w", "dn_a_log", "dn_dt_bias", "dn_norm_w", "w_out", "ffn_w_in", "ffn_conv_w", "ffn_conv_b",
           "ffn_w_out", "norm_pre_mix", "norm_post_mix", "norm_pre_ffn", "norm_post_ffn")
FULL_SHAPE = dict(dn_conv_w=(DEPTH, 4, 1536), ffn_conv_w=(DEPTH, 3, 2 * D_FF), dn_a_log=(DEPTH, NDH),
                  dn_dt_bias=(DEPTH, NDH), dn_norm_w=(DEPTH, 128), ffn_conv_b=(DEPTH, 2 * D_FF),
                  norm_pre_mix=(DEPTH, D_MODEL), norm_post_mix=(DEPTH, D_MODEL), norm_pre_ffn=(DEPTH, D_MODEL),
                  norm_post_ffn=(DEPTH, D_MODEL))
SMALL_GRAD_ORDER = REPLICATED + SMALL_SHARDED
SMALL_GRAD_ROWS = 520
SMALL_W_ROWS = 48
SMALL_ADAM_ROWS = 200


def _w_in_rows_to_kernel_order(t):
    qkv = t[:QKV_W].reshape(3, N_PAIR, 128, -1).swapaxes(0, 1).reshape(QKV_W, -1)
    return jnp.pad(jnp.concatenate([qkv, t[QKV_W:]], axis=0), ((0, IN_PAD - IN_COLS), (0, 0)))


def _w_in_rows_from_kernel_order(t):
    qkv = t[:QKV_W].reshape(N_PAIR, 3, 128, -1).swapaxes(0, 1).reshape(QKV_W, -1)
    return jnp.concatenate([qkv, t[QKV_W:IN_COLS]], axis=0)


def _interleave_ff_rows(t):
    return t.reshape(2, FF_BLKS, 128, -1).swapaxes(0, 1).reshape(2 * D_FF, -1)


def _deinterleave_ff_rows(t):
    return t.reshape(FF_BLKS, 2, 128, -1).swapaxes(0, 1).reshape(2 * D_FF, -1)


def kernel(x, w_in, dn_conv_w, dn_a_log, dn_dt_bias, dn_norm_w, w_out, ffn_w_in, ffn_conv_w, ffn_conv_b, ffn_w_out, norm_pre_mix, norm_post_mix, norm_pre_ffn, norm_post_ffn, loss_target, m_w_in, m_dn_conv_w, m_dn_a_log, m_dn_dt_bias, m_dn_norm_w, m_w_out, m_ffn_w_in, m_ffn_conv_w, m_ffn_conv_b, m_ffn_w_out, m_norm_pre_mix, m_norm_post_mix, m_norm_pre_ffn, m_norm_post_ffn, v_w_in, v_dn_conv_w, v_dn_a_log, v_dn_dt_bias, v_dn_norm_w, v_w_out, v_ffn_w_in, v_ffn_conv_w, v_ffn_conv_b, v_ffn_w_out, v_norm_pre_mix, v_norm_post_mix, v_norm_pre_ffn, v_norm_post_ffn):
    local = dict(w_in=w_in, dn_conv_w=dn_conv_w, dn_a_log=dn_a_log, dn_dt_bias=dn_dt_bias, dn_norm_w=dn_norm_w,
                 w_out=w_out, ffn_w_in=ffn_w_in, ffn_conv_w=ffn_conv_w, ffn_conv_b=ffn_conv_b, ffn_w_out=ffn_w_out,
                 norm_pre_mix=norm_pre_mix, norm_post_mix=norm_post_mix, norm_pre_ffn=norm_pre_ffn,
                 norm_post_ffn=norm_post_ffn)
    mom_m = dict(w_in=m_w_in, dn_conv_w=m_dn_conv_w, dn_a_log=m_dn_a_log, dn_dt_bias=m_dn_dt_bias,
                 dn_norm_w=m_dn_norm_w, w_out=m_w_out, ffn_w_in=m_ffn_w_in, ffn_conv_w=m_ffn_conv_w,
                 ffn_conv_b=m_ffn_conv_b, ffn_w_out=m_ffn_w_out, norm_pre_mix=m_norm_pre_mix,
                 norm_post_mix=m_norm_post_mix, norm_pre_ffn=m_norm_pre_ffn, norm_post_ffn=m_norm_post_ffn)
    mom_v = dict(w_in=v_w_in, dn_conv_w=v_dn_conv_w, dn_a_log=v_dn_a_log, dn_dt_bias=v_dn_dt_bias,
                 dn_norm_w=v_dn_norm_w, w_out=v_w_out, ffn_w_in=v_ffn_w_in, ffn_conv_w=v_ffn_conv_w,
                 ffn_conv_b=v_ffn_conv_b, ffn_w_out=v_ffn_w_out, norm_pre_mix=v_norm_pre_mix,
                 norm_post_mix=v_norm_post_mix, norm_pre_ffn=v_norm_pre_ffn, norm_post_ffn=v_norm_post_ffn)
    dev = 4 * lax.axis_index("x") + 2 * lax.axis_index("y") + lax.axis_index("c")
    core = lax.axis_index("c").astype(jnp.int32).reshape(1)

    def shard(n, l):
        s = local[n][l].astype(BF16)
        return s.T if n in COLUMN_SHARDED else s

    def matrix(n, gathered):
        if n == "w_in":
            return _w_in_rows_to_kernel_order(gathered.reshape(IN_COLS, D_MODEL))
        if n == "ffn_w_in":
            return _interleave_ff_rows(gathered.reshape(2 * D_FF, D_MODEL))
        return gathered.reshape(-1, D_MODEL)

    small_w = _pack([dn_conv_w, ffn_conv_w], SMALL_W_ROWS)
    g_w_in0, g_small = _run_exchange(_gather_exchange([shard("w_in", 0), small_w]), "weights_all_gather")
    n_dn, n_ff = DEPTH * 4 * 192, DEPTH * 3 * 704
    sm = g_small.reshape(N_DEV, -1)
    full_dn_conv = sm[:, :n_dn].reshape(N_DEV, DEPTH, 4, 192).transpose(1, 2, 0, 3).reshape(DEPTH, 4, 1536)
    full_ff_conv = _interleave_ff(
        sm[:, n_dn:n_dn + n_ff].reshape(N_DEV, DEPTH, 3, 704).transpose(1, 2, 0, 3).reshape(DEPTH, 3, 2 * D_FF))

    def small_weights(l):
        wts = dict(dn_conv_w=full_dn_conv[l], ffn_conv_w=full_ff_conv[l], ffn_conv_b=_interleave_ff(_row(ffn_conv_b[l])),
                   dn_a_log=_row(dn_a_log[l], 128), dn_dt_bias=_row(dn_dt_bias[l], 128))
        for n in ("dn_norm_w", "norm_pre_mix", "norm_post_mix", "norm_pre_ffn", "norm_post_ffn"):
            wts[n] = _row(local[n][l])
        return wts

    weights = [small_weights(l) for l in range(DEPTH)]
    weights[0]["w_in"] = matrix("w_in", g_w_in0)

    def gather_behind(wanted):
        def deliver(got):
            for (n, l), g in zip(wanted, got[0]):
                weights[l][n] = matrix(n, g)

        return [_gather_exchange([shard(n, l) for n, l in wanted])], deliver

    tables = _rope_tables()
    ex_attn0, on_attn0 = gather_behind([("w_out", 0), ("ffn_w_in", 0)])
    ex_delta0, on_delta0 = gather_behind([("ffn_w_out", 0), ("w_in", 1)])
    ex_attn1, on_attn1 = gather_behind([("w_out", 1), ("ffn_w_in", 1)])
    ex_delta1, on_delta1 = gather_behind([("ffn_w_out", 1)])
    act, saved0 = _layer_fwd(x[0], weights[0], tables, ex_attn0, ex_delta0, on_attn0, on_delta0)
    act, saved1 = _layer_fwd(act, weights[1], tables, ex_attn1, ex_delta1, on_attn1, on_delta1)
    loss_part, dact = _loss_fwd_bwd(act, loss_target[0])

    def to_devices(name, t):
        if name == "w_in":
            t = _w_in_rows_from_kernel_order(t)
        if name == "ffn_w_in":
            t = _deinterleave_ff_rows(t)
        return t.reshape(N_DEV, t.shape[0] // N_DEV, t.shape[1])

    def pair_sums(names, layer, to_dev, from_sibling):
        return [_pair_add(gd, r, core, "grads_pair_add_%s_%d" % (n, layer))
                for n, gd, r in zip(names, to_dev, from_sibling)]

    grads = [None] * DEPTH
    dact, grads[1], _ = _layer_bwd(dact, weights[1], saved1, tables)
    to_dev1 = [to_devices(n, grads[1][n]) for n in BIG]
    early = ("w_out", "ffn_w_in", "ffn_w_out")
    parts, stash = {}, {}

    def delta_exchanges(g, got_ffact):
        stash["to_dev0"] = [to_devices(n, g[n]) for n in early]
        return [_chips_exchange(pair_sums(BIG, 1, to_dev1, got_ffact[0])), _sibling_exchange(stash["to_dev0"])]

    def attn_exchanges(got_delta):
        for n, p in zip(BIG, got_delta[0]):
            parts[n, 1] = p
        return [_chips_exchange(pair_sums(early, 0, stash["to_dev0"], got_delta[1]))]

    dact, grads[0], got_attn = _layer_bwd(dact, weights[0], saved0, tables, [_sibling_exchange(to_dev1)],
                                          delta_exchanges, attn_exchanges)
    for n, p in zip(early, got_attn[0]):
        parts[n, 0] = p
    grad_x = dact[None]
    last = [to_devices("w_in", grads[0]["w_in"])]
    from_sibling = _run_exchange(_sibling_exchange(last), "grads_to_sibling")
    parts["w_in", 0], = _run_exchange(_chips_exchange(pair_sums(("w_in",), 0, last, from_sibling)), "grads_to_chips")

    def small_grad(name):
        t = jnp.stack([grads[l][name] for l in range(DEPTH)])
        if name in ("dn_a_log", "dn_dt_bias"):
            t = t[:, 0, :NDH]
        if name in ("ffn_conv_w", "ffn_conv_b"):
            t = _deinterleave_ff(t)
        return t.reshape(FULL_SHAPE[name])

    small_part = _pack([small_grad(n) for n in SMALL_GRAD_ORDER] + [loss_part[0, :1]], SMALL_GRAD_ROWS)
    small_sum = _all_gather_sum_small(small_part)
    small_g = dict(zip(SMALL_GRAD_ORDER + ("loss",), _unpack(small_sum, [FULL_SHAPE[n] for n in SMALL_GRAD_ORDER] + [(1,)])))
    loss = small_g["loss"][0]
    small_g["dn_conv_w"] = lax.dynamic_slice_in_dim(small_g["dn_conv_w"], dev * 192, 192, axis=2)
    small_g["ffn_conv_w"] = lax.dynamic_slice_in_dim(small_g["ffn_conv_w"], dev * 704, 704, axis=2)

    out_g, out_d, out_m, out_v = {}, {}, {}, {}
    for n in BIG:
        p = [parts[n, l] for l in range(DEPTH)]
        if n in COLUMN_SHARDED:
            p = [_sum_parts(t, "grad_sum_%s_%d" % (n, l)).T[None] for l, t in enumerate(p)]
        out_g[n], out_d[n], out_m[n], out_v[n] = _adamw_sharded(p, local[n], mom_m[n], mom_v[n], ADAM_ROWS[n], "adamw_" + n)
    shapes = [small_g[n].shape for n in SMALL_GRAD_ORDER]
    d_s, m_s, v_s = _adamw_small(_pack([small_g[n] for n in SMALL_GRAD_ORDER], SMALL_ADAM_ROWS),
                                 _pack([local[n] for n in SMALL_GRAD_ORDER], SMALL_ADAM_ROWS),
                                 _pack([mom_m[n] for n in SMALL_GRAD_ORDER], SMALL_ADAM_ROWS),
                                 _pack([mom_v[n] for n in SMALL_GRAD_ORDER], SMALL_ADAM_ROWS))
    for n, d, m, v in zip(SMALL_GRAD_ORDER, _unpack(d_s, shapes), _unpack(m_s, shapes), _unpack(v_s, shapes)):
        out_g[n], out_d[n], out_m[n], out_v[n] = small_g[n], d, m, v
    return (loss, grad_x, *[out_g[n] for n in WEIGHTS], *[out_d[n] for n in WEIGHTS],
            *[out_m[n] for n in WEIGHTS], *[out_v[n] for n in WEIGHTS])
```

```python
import functools
import math

import jax
import jax.numpy as jnp
from jax import lax
from jax.experimental import pallas as pl
from jax.experimental.pallas import tpu as pltpu

F32 = jnp.float32
BF16 = jnp.bfloat16
HI = lax.Precision.HIGHEST
MESH = pl.DeviceIdType.MESH

N_DEV = 8
SEQ = 2048
D_MODEL = 1024
DEPTH = 2
N_PAIR = 4
HEAD_DIM = 64
ATTN_W = 512
ATTN_BLK = 128
DILATIONS = (1, 4, 16)
SEGMENT_BLOCKS = (16, 4, 1)
N_BLK = SEQ // ATTN_BLK
NDH = 4
CH = 64
NCH = SEQ // CH
IN_COLS = 3592
IN_PAD = 3840
QKV_W = 3 * ATTN_W
DN_QKV_BLK0 = QKV_W // 128
DN_QKV_BLKS = 1536 // 128
DN_Z_COL = 3072
DN_TAIL_BLK = 3584 // 128
D_FF = 2816
FF_BLKS = D_FF // 128
EPS = 1e-6
NEG = -1e30
ROPE_THETA = 10000.0

ADAM_LR, ADAM_B1, ADAM_B2, ADAM_EPS, ADAM_WD, ADAM_STEP = 0.001, 0.9, 0.999, 1e-08, 0.01, 10

VMEM_LIMIT = 56 * 1024 * 1024


def _cp(*sem):
    return pltpu.CompilerParams(dimension_semantics=sem, vmem_limit_bytes=VMEM_LIMIT)


class Exchange:
    def __init__(self, operands, out_shapes, sems, start, middle, finish):
        self.operands, self.out_shapes, self.sems = list(operands), list(out_shapes), list(sems)
        self.start, self.middle, self.finish = start, middle, finish


HBM_SPEC = pl.BlockSpec(memory_space=pltpu.HBM)


def _hosted_call(body, *, name, steps, in_specs, out_specs, out_shape, scratch_shapes, operands, exchanges=(),
                 aliases=None):
    n_in, n_out, n_scr = len(in_specs), len(out_specs), len(scratch_shapes)

    def take(refs, pos, counts):
        groups = []
        for c in counts:
            groups.append(refs[pos:pos + c])
            pos += c
        return groups, pos

    def full_body(*refs):
        ins, pos = refs[:n_in], n_in
        ex_ins, pos = take(refs, pos, [len(e.operands) for e in exchanges])
        outs, pos = refs[pos:pos + n_out], pos + n_out
        ex_outs, pos = take(refs, pos, [len(e.out_shapes) for e in exchanges])
        scr, pos = refs[pos:pos + n_scr], pos + n_scr
        ex_sems, pos = take(refs, pos, [len(e.sems) for e in exchanges])
        step = pl.program_id(0)
        for e, a, b, s in zip(exchanges, ex_ins, ex_outs, ex_sems):
            pl.when(step == 0)(functools.partial(e.start, a, b, s))
            if e.middle is not None:
                pl.when(step == steps // 2)(functools.partial(e.middle, a, b, s))
        body(*ins, *outs, *scr)
        for e, a, b, s in zip(exchanges, ex_ins, ex_outs, ex_sems):
            pl.when(step == steps - 1)(functools.partial(e.finish, a, b, s))

    n_ex_in = sum(len(e.operands) for e in exchanges)
    n_ex_out = sum(len(e.out_shapes) for e in exchanges)
    results = pl.pallas_call(
        full_body, name=name, grid=(steps,),
        in_specs=list(in_specs) + [HBM_SPEC] * n_ex_in,
        out_specs=list(out_specs) + [HBM_SPEC] * n_ex_out,
        out_shape=list(out_shape) + [s for e in exchanges for s in e.out_shapes],
        scratch_shapes=list(scratch_shapes) + [s for e in exchanges for s in e.sems],
        input_output_aliases=aliases or {},
        compiler_params=_cp("arbitrary"),
    )(*operands, *[a for e in exchanges for a in e.operands])
    ex_results, _ = take(results, n_out, [len(e.out_shapes) for e in exchanges])
    return results[:n_out], ex_results


def _dot(a, b, dims, precision=None):
    if precision is None:
        a = a.astype(BF16)
        b = b.astype(BF16)
    return lax.dot_general(a, b, (dims, ((), ())), preferred_element_type=F32, precision=precision)


def _make_mm(precision):
    @jax.custom_vjp
    def nn(a, b):
        return _dot(a, b, ((1,), (0,)), precision)

    @jax.custom_vjp
    def nt(a, b):
        return _dot(a, b, ((1,), (1,)), precision)

    @jax.custom_vjp
    def tn(a, b):
        return _dot(a, b, ((0,), (0,)), precision)

    nn.defvjp(lambda a, b: (nn(a, b), (a, b)), lambda r, g: (nt(g, r[1]), tn(r[0], g)))
    nt.defvjp(lambda a, b: (nt(a, b), (a, b)), lambda r, g: (nn(g, r[1]), tn(g, r[0])))
    tn.defvjp(lambda a, b: (tn(a, b), (a, b)), lambda r, g: (nt(r[1], g), nn(r[0], g)))
    return nn, nt, tn


MM, MM_NT, MM_TN = _make_mm(None)


def _matmul(a, b, *, ta=False, tb=False, tm, tn, tk, name, out_dtype=F32):
    (k_dim, m_dim) = a.shape if ta else a.shape[::-1]
    (n_dim, k2) = b.shape if tb else b.shape[::-1]
    assert k_dim == k2 and m_dim % tm == 0 and n_dim % tn == 0 and k_dim % tk == 0, (a.shape, b.shape, tm, tn, tk)
    nk = k_dim // tk
    dims = ((0 if ta else 1,), (1 if tb else 0,))

    def body(a_ref, b_ref, o_ref, *acc):
        p = _dot(a_ref[...], b_ref[...], dims)
        if nk == 1:
            o_ref[...] = p.astype(out_dtype)
            return
        acc_ref, k = acc[0], pl.program_id(2)

        @pl.when(k == 0)
        def _():
            acc_ref[...] = p

        @pl.when(k > 0)
        def _():
            acc_ref[...] += p

        @pl.when(k == nk - 1)
        def _():
            o_ref[...] = acc_ref[...].astype(out_dtype)

    a_spec = pl.BlockSpec((tk, tm), lambda i, j, k: (k, i)) if ta else pl.BlockSpec((tm, tk), lambda i, j, k: (i, k))
    b_spec = pl.BlockSpec((tn, tk), lambda i, j, k: (j, k)) if tb else pl.BlockSpec((tk, tn), lambda i, j, k: (k, j))
    return pl.pallas_call(
        body, name=name,
        grid=(m_dim // tm, n_dim // tn, nk),
        in_specs=[a_spec, b_spec],
        out_specs=pl.BlockSpec((tm, tn), lambda i, j, k: (i, j)),
        out_shape=jax.ShapeDtypeStruct((m_dim, n_dim), out_dtype),
        scratch_shapes=[pltpu.VMEM((tm, tn), F32)] if nk > 1 else [],
        compiler_params=_cp("parallel", "parallel", "arbitrary"),
    )(a, b)


NORM_ROWS = 256


def _rms(x, w):
    return x * lax.rsqrt(jnp.mean(x * x, axis=1, keepdims=True) + EPS) * w


def _norm_fwd(x, w_row, name, out_dtype=BF16):
    def body(x_ref, w_ref, o_ref):
        o_ref[...] = _rms(x_ref[...], w_ref[...]).astype(out_dtype)

    return pl.pallas_call(
        body, name=name, grid=(SEQ // NORM_ROWS,),
        in_specs=[pl.BlockSpec((NORM_ROWS, D_MODEL), lambda i: (i, 0)), pl.BlockSpec((1, D_MODEL), lambda i: (0, 0))],
        out_specs=pl.BlockSpec((NORM_ROWS, D_MODEL), lambda i: (i, 0)),
        out_shape=jax.ShapeDtypeStruct((SEQ, D_MODEL), out_dtype),
        compiler_params=_cp("parallel"),
    )(x, w_row)


def _resnorm_fwd(x, f, w_row, name):
    def body(x_ref, f_ref, w_ref, o_ref):
        o_ref[...] = x_ref[...] + _rms(f_ref[...], w_ref[...])

    blk = pl.BlockSpec((NORM_ROWS, D_MODEL), lambda i: (i, 0))
    return pl.pallas_call(
        body, name=name, grid=(SEQ // NORM_ROWS,),
        in_specs=[blk, blk, pl.BlockSpec((1, D_MODEL), lambda i: (0, 0))],
        out_specs=blk, out_shape=jax.ShapeDtypeStruct((SEQ, D_MODEL), F32),
        compiler_params=_cp("parallel"),
    )(x, f, w_row)


def _norm_bwd(x, w_row, dy, add, name):
    has_add = add is not None

    def body(*refs):
        if has_add:
            x_ref, w_ref, dy_ref, add_ref, dx_ref, dw_ref = refs
        else:
            x_ref, w_ref, dy_ref, dx_ref, dw_ref = refs
        _, vjp = jax.vjp(_rms, x_ref[...], w_ref[...])
        dx, dw = vjp(dy_ref[...])
        dx_ref[...] = dx + add_ref[...] if has_add else dx

        @pl.when(pl.program_id(0) == 0)
        def _():
            dw_ref[...] = jnp.zeros_like(dw_ref)

        dw_ref[...] += dw

    blk = pl.BlockSpec((NORM_ROWS, D_MODEL), lambda i: (i, 0))
    row = pl.BlockSpec((1, D_MODEL), lambda i: (0, 0))
    ins = [x, w_row, dy] + ([add] if has_add else [])
    return pl.pallas_call(
        body, name=name, grid=(SEQ // NORM_ROWS,),
        in_specs=[blk, row, blk] + ([blk] if has_add else []),
        out_specs=[blk, row],
        out_shape=[jax.ShapeDtypeStruct((SEQ, D_MODEL), F32), jax.ShapeDtypeStruct((1, D_MODEL), F32)],
        compiler_params=_cp("arbitrary"),
    )(*ins)


def _loss_fwd_bwd(y, target):
    def body(y_ref, t_ref, loss_ref, dy_ref):
        err = y_ref[...] - t_ref[...]
        dy_ref[...] = err * (1.0 / D_MODEL)

        @pl.when(pl.program_id(0) == 0)
        def _():
            loss_ref[...] = jnp.zeros_like(loss_ref)

        part = jnp.sum(jnp.sum(err * err, axis=1, keepdims=True) * (1.0 / D_MODEL), axis=0, keepdims=True)
        loss_ref[...] += 0.5 * jnp.broadcast_to(part, loss_ref.shape)

    blk = pl.BlockSpec((NORM_ROWS, D_MODEL), lambda i: (i, 0))
    return pl.pallas_call(
        body, name="loss", grid=(SEQ // NORM_ROWS,),
        in_specs=[blk, blk],
        out_specs=[pl.BlockSpec((1, 128), lambda i: (0, 0)), blk],
        out_shape=[jax.ShapeDtypeStruct((1, 128), F32), jax.ShapeDtypeStruct((SEQ, D_MODEL), F32)],
        compiler_params=_cp("arbitrary"),
    )(y, target)


def _make_shift(j):
    def down(x):
        row = lax.broadcasted_iota(jnp.int32, x.shape, 0)
        return jnp.where(row >= j, pltpu.roll(x, j, 0), 0.0)

    def up(x):
        n = x.shape[0]
        row = lax.broadcasted_iota(jnp.int32, x.shape, 0)
        return jnp.where(row < n - j, pltpu.roll(x, n - j, 0), 0.0)

    f = jax.custom_vjp(down)
    f.defvjp(lambda x: (down(x), None), lambda _, g: (up(g),))
    return f


_SHIFT = {j: _make_shift(j) for j in (1, 2, 3)}


def _causal_conv(x, taps):
    n = len(taps)
    acc = x * taps[n - 1]
    for k in range(n - 1):
        acc = acc + _SHIFT[n - 1 - k](x) * taps[k]
    return acc


def _tap_rows(w_ref, lanes=slice(None)):
    return tuple(w_ref[k:k + 1, lanes] for k in range(w_ref.shape[0]))


def _sigmoid(x):
    return 1.0 / (1.0 + jnp.exp(-x))


def _silu(x):
    return x * _sigmoid(x)


def _softplus(x):
    return jnp.maximum(x, 0.0) + jnp.log(1.0 + jnp.exp(-jnp.abs(x)))


def _gelu_tanh(x):
    return 0.5 * x * (1.0 + jnp.tanh(math.sqrt(2.0 / math.pi) * (x + 0.044715 * (x * x * x))))


def _dnconv_fn(x, taps):
    return _silu(_causal_conv(x, taps))


def _dnconv_fwd(proj, conv_w):
    def body(x_ref, w_ref, o_ref):
        o_ref[...] = _dnconv_fn(x_ref[...], _tap_rows(w_ref))

    return pl.pallas_call(
        body, name="dnconv_fwd", grid=(DN_QKV_BLKS,),
        in_specs=[pl.BlockSpec((SEQ, 128), lambda j: (0, DN_QKV_BLK0 + j)), pl.BlockSpec((4, 128), lambda j: (0, j))],
        out_specs=pl.BlockSpec((SEQ, 128), lambda j: (0, j)),
        out_shape=jax.ShapeDtypeStruct((SEQ, 1536), F32),
        compiler_params=_cp("parallel"),
    )(proj, conv_w)


def _dnconv_bwd(proj, conv_w, dc, dproj):
    def body(x_ref, w_ref, dc_ref, _, dx_ref, dw_ref):
        _, vjp = jax.vjp(_dnconv_fn, x_ref[...], _tap_rows(w_ref))
        dx, dw = vjp(dc_ref[...])
        dx_ref[...] = dx
        for k, row in enumerate(dw):
            dw_ref[k:k + 1, :] = row

    return pl.pallas_call(
        body, name="dnconv_bwd", grid=(DN_QKV_BLKS,),
        in_specs=[pl.BlockSpec((SEQ, 128), lambda j: (0, DN_QKV_BLK0 + j)), pl.BlockSpec((4, 128), lambda j: (0, j)),
                  pl.BlockSpec((SEQ, 128), lambda j: (0, j)), pl.BlockSpec(memory_space=pl.ANY)],
        out_specs=[pl.BlockSpec((SEQ, 128), lambda j: (0, DN_QKV_BLK0 + j)), pl.BlockSpec((4, 128), lambda j: (0, j))],
        out_shape=[jax.ShapeDtypeStruct((SEQ, IN_PAD), F32), jax.ShapeDtypeStruct((4, 1536), F32)],
        input_output_aliases={3: 0},
        compiler_params=_cp("parallel"),
    )(proj, conv_w, dc, dproj)


def _ffact_fn(pg, pu, wg, wu, bg, bu):
    return _gelu_tanh(_causal_conv(pg, wg) + bg) * (_causal_conv(pu, wu) + bu)


def _ffact_args(p_ref, w_ref, b_ref):
    g, u = slice(0, 128), slice(128, 256)
    return (p_ref[:, g], p_ref[:, u], _tap_rows(w_ref, g), _tap_rows(w_ref, u), b_ref[:, g], b_ref[:, u])


def _ffact_fwd(pre, conv_w, conv_b):
    def body(p_ref, w_ref, b_ref, o_ref):
        o_ref[...] = _ffact_fn(*_ffact_args(p_ref, w_ref, b_ref)).astype(BF16)

    return pl.pallas_call(
        body, name="ffact_fwd", grid=(FF_BLKS,),
        in_specs=[pl.BlockSpec((SEQ, 256), lambda j: (0, j)), pl.BlockSpec((3, 256), lambda j: (0, j)),
                  pl.BlockSpec((1, 256), lambda j: (0, j))],
        out_specs=pl.BlockSpec((SEQ, 128), lambda j: (0, j)),
        out_shape=jax.ShapeDtypeStruct((SEQ, D_FF), BF16),
        compiler_params=_cp("parallel"),
    )(pre, conv_w, conv_b)


def _ffact_bwd(pre, conv_w, conv_b, dact, exchanges=()):
    def body(p_ref, w_ref, b_ref, da_ref, dp_ref, dw_ref, db_ref):
        _, vjp = jax.vjp(_ffact_fn, *_ffact_args(p_ref, w_ref, b_ref))
        dpg, dpu, dwg, dwu, dbg, dbu = vjp(da_ref[...].astype(F32))
        dp_ref[:, 0:128] = dpg
        dp_ref[:, 128:256] = dpu
        for k in range(3):
            dw_ref[k:k + 1, 0:128] = dwg[k]
            dw_ref[k:k + 1, 128:256] = dwu[k]
        db_ref[:, 0:128] = dbg
        db_ref[:, 128:256] = dbu

    return _hosted_call(
        body, name="ffact_bwd", steps=FF_BLKS,
        in_specs=[pl.BlockSpec((SEQ, 256), lambda j: (0, j)), pl.BlockSpec((3, 256), lambda j: (0, j)),
                  pl.BlockSpec((1, 256), lambda j: (0, j)), pl.BlockSpec((SEQ, 128), lambda j: (0, j))],
        out_specs=[pl.BlockSpec((SEQ, 256), lambda j: (0, j)), pl.BlockSpec((3, 256), lambda j: (0, j)),
                   pl.BlockSpec((1, 256), lambda j: (0, j))],
        out_shape=[jax.ShapeDtypeStruct((SEQ, 2 * D_FF), F32), jax.ShapeDtypeStruct((3, 2 * D_FF), F32),
                   jax.ShapeDtypeStruct((1, 2 * D_FF), F32)],
        scratch_shapes=[], operands=(pre, conv_w, conv_b, dact), exchanges=exchanges)


def _interleave_ff(t):
    lead = t.shape[:-1]
    return t.reshape(lead + (2, FF_BLKS, 128)).swapaxes(-3, -2).reshape(lead + (2 * D_FF,))


def _deinterleave_ff(t):
    lead = t.shape[:-1]
    return t.reshape(lead + (FF_BLKS, 2, 128)).swapaxes(-3, -2).reshape(lead + (2 * D_FF,))


def _rope_tables():
    inv = 1.0 / (ROPE_THETA ** (jnp.arange(0, HEAD_DIM, 2, dtype=F32) / HEAD_DIM))
    ang = jnp.arange(SEQ, dtype=F32)[:, None] * inv[None, :]
    cos = jnp.tile(jnp.cos(ang), (1, 4))
    sin = jnp.tile(jnp.sin(ang), (1, 4))
    sign = jnp.where((jnp.arange(128) % HEAD_DIM) < HEAD_DIM // 2, -1.0, 1.0).astype(F32)
    return cos, sin * sign[None, :]


def _rope(x, cos, sin_signed):
    lane = lax.broadcasted_iota(jnp.int32, x.shape, 1)
    partner = jnp.where((lane % HEAD_DIM) < HEAD_DIM // 2, pltpu.roll(x, 128 - HEAD_DIM // 2, 1),
                        pltpu.roll(x, HEAD_DIM // 2, 1))
    return x * cos + partner * sin_signed


def _pairs_from_qkv(t):
    lead = t.shape[:-1]
    return t.reshape(lead + (3, N_PAIR, 128)).swapaxes(-3, -2).reshape(lead + (QKV_W,))


def _qkv_from_pairs(t):
    lead = t.shape[:-1]
    return t.reshape(lead + (N_PAIR, 3, 128)).swapaxes(-3, -2).reshape(lead + (QKV_W,))


def _head_masks():
    lane = lax.broadcasted_iota(jnp.int32, (1, 128), 1)
    return [(lane // HEAD_DIM) == h for h in range(2)]


def _both_heads(x):
    return jnp.concatenate([jnp.where(hm, x, 0.0)[None] for hm in _head_masks()], axis=0)


def _block_keys(branch, k_s, v_s, rows, prows, has_prev):
    a = lax.broadcasted_iota(jnp.int32, (ATTN_BLK, ATTN_BLK), 0)
    c = lax.broadcasted_iota(jnp.int32, (ATTN_BLK, ATTN_BLK), 1)
    keys, values, mask = k_s[rows, :], v_s[rows, :], c <= a
    if SEGMENT_BLOCKS[branch] > 1:
        keys = jnp.concatenate([k_s[prows, :], keys], axis=0)
        values = jnp.concatenate([v_s[prows, :], values], axis=0)
        mask = jnp.concatenate([(c >= a) & has_prev, mask], axis=1)
    twice = lambda t: jnp.broadcast_to(t[None], (2,) + t.shape)
    return twice(keys), twice(values), mask


def _block_rows(branch, t):
    d, per_seg = DILATIONS[branch], SEGMENT_BLOCKS[branch]
    if d == 1:
        start = pl.multiple_of(t * ATTN_BLK, ATTN_BLK)
        prev = pl.multiple_of(jnp.maximum(t - 1, 0) * ATTN_BLK, ATTN_BLK)
        return pl.ds(start, ATTN_BLK), pl.ds(prev, ATTN_BLK), t > 0
    r, n = t // per_seg, t % per_seg
    start = n * (ATTN_BLK * d) + r
    prev = jnp.maximum(n - 1, 0) * (ATTN_BLK * d) + r
    return pl.ds(start, ATTN_BLK, stride=d), pl.ds(prev, ATTN_BLK, stride=d), n > 0


def _attn_fwd(proj, cos, sin_signed, exchanges=()):
    scale = HEAD_DIM ** -0.5

    def body(qkv_ref, cos_ref, sin_ref, out_ref, lse_ref, q_s, k_s, v_s, *branch_s):
        o_s, l_s = branch_s[:3], branch_s[3:]
        q_s[...] = _rope(qkv_ref[:, 0:128], cos_ref[...], sin_ref[...])
        k_s[...] = _rope(qkv_ref[:, 128:256], cos_ref[...], sin_ref[...])
        v_s[...] = qkv_ref[:, 256:384]
        heads = _head_masks()
        for branch in range(3):
            def block(t, carry, branch=branch):
                rows, prows, has_prev = _block_rows(branch, t)
                keys, values, mask = _block_keys(branch, k_s, v_s, rows, prows, has_prev)
                s = jnp.where(mask, BMM_NT(_both_heads(q_s[rows, :]), keys) * scale, NEG)
                m = jnp.max(s, axis=2, keepdims=True)
                e = jnp.exp(s - m)
                l = jnp.sum(e, axis=2, keepdims=True)
                o = BMM(e, values) / l
                lse_b = m + jnp.log(l)
                o_s[branch][rows, :] = jnp.where(heads[0], o[0], o[1])
                l_s[branch][rows, :] = jnp.where(heads[0], lse_b[0], lse_b[1])
                return carry

            lax.fori_loop(0, N_BLK, block, 0, unroll=2)
        l0, l1, l2 = l_s[0][...], l_s[1][...], l_s[2][...]
        m = jnp.maximum(jnp.maximum(l0, l1), l2)
        w0, w1, w2 = jnp.exp(l0 - m), jnp.exp(l1 - m), jnp.exp(l2 - m)
        den = w0 + w1 + w2
        out_ref[...] = (w0 * o_s[0][...] + w1 * o_s[1][...] + w2 * o_s[2][...]) / den
        lse_ref[...] = m + jnp.log(den)

    tab = pl.BlockSpec((SEQ, 128), lambda j: (0, 0))
    col = pl.BlockSpec((SEQ, 128), lambda j: (0, j))
    return _hosted_call(
        body, name="attn_fwd", steps=N_PAIR,
        in_specs=[pl.BlockSpec((SEQ, 384), lambda j: (0, j)), tab, tab],
        out_specs=[col, col],
        out_shape=[jax.ShapeDtypeStruct((SEQ, 2 * ATTN_W), F32), jax.ShapeDtypeStruct((SEQ, ATTN_W), F32)],
        scratch_shapes=[pltpu.VMEM((SEQ, 128), F32)] * 9,
        operands=(proj, cos, sin_signed), exchanges=exchanges)


def _attn_bwd(proj, cos, sin_signed, cat, lse, dcat, dproj, exchanges=()):
    scale = HEAD_DIM ** -0.5

    def body(qkv_ref, cos_ref, sin_ref, out_ref, lse_ref, do_ref, _, dqkv_ref, q_s, k_s, v_s, dq_s, dk_s, dv_s,
             dod_s):
        q_s[...] = _rope(qkv_ref[:, 0:128], cos_ref[...], sin_ref[...])
        k_s[...] = _rope(qkv_ref[:, 128:256], cos_ref[...], sin_ref[...])
        v_s[...] = qkv_ref[:, 256:384]
        dq_s[...] = jnp.zeros_like(dq_s)
        dk_s[...] = jnp.zeros_like(dk_s)
        dv_s[...] = jnp.zeros_like(dv_s)
        dod_s[...] = do_ref[...] * out_ref[...]
        heads = _head_masks()
        for branch in range(3):
            def block(t, carry, branch=branch):
                rows, prows, has_prev = _block_rows(branch, t)
                keys, values, mask = _block_keys(branch, k_s, v_s, rows, prows, has_prev)
                q2, do2 = _both_heads(q_s[rows, :]), _both_heads(do_ref[rows, :])
                lse_b, dod = lse_ref[rows, :], dod_s[rows, :]
                lse2 = jnp.concatenate(
                    [jnp.max(jnp.where(hm, lse_b, NEG), axis=1, keepdims=True)[None] for hm in heads], axis=0)
                delta = jnp.concatenate(
                    [jnp.sum(jnp.where(hm, dod, 0.0), axis=1, keepdims=True)[None] for hm in heads], axis=0)
                p = jnp.exp(jnp.where(mask, BMM_NT(q2, keys) * scale, NEG) - lse2)
                ds = p * (BMM_NT(do2, values) - delta) * scale
                dq = BMM(ds, keys)
                dk = BMM_TN(ds, q2)
                dv = BMM_TN(p, do2)
                dk, dv = dk[0] + dk[1], dv[0] + dv[1]
                dq_s[rows, :] += jnp.where(heads[0], dq[0], dq[1])
                if SEGMENT_BLOCKS[branch] > 1:
                    dk_s[rows, :] += dk[ATTN_BLK:]
                    dv_s[rows, :] += dv[ATTN_BLK:]

                    @pl.when(has_prev)
                    def _():
                        dk_s[prows, :] += dk[:ATTN_BLK]
                        dv_s[prows, :] += dv[:ATTN_BLK]
                else:
                    dk_s[rows, :] += dk
                    dv_s[rows, :] += dv
                return carry

            lax.fori_loop(0, N_BLK, block, 0, unroll=2)
        dqkv_ref[:, 0:128] = _rope(dq_s[...], cos_ref[...], -sin_ref[...])
        dqkv_ref[:, 128:256] = _rope(dk_s[...], cos_ref[...], -sin_ref[...])
        dqkv_ref[:, 256:384] = dv_s[...]

    tab = pl.BlockSpec((SEQ, 128), lambda j: (0, 0))
    col = pl.BlockSpec((SEQ, 128), lambda j: (0, j))
    qkv = pl.BlockSpec((SEQ, 384), lambda j: (0, j))
    (dproj,), results = _hosted_call(
        body, name="attn_bwd", steps=N_PAIR,
        in_specs=[qkv, tab, tab, col, col, col, pl.BlockSpec(memory_space=pl.ANY)],
        out_specs=[qkv],
        out_shape=[jax.ShapeDtypeStruct((SEQ, IN_PAD), F32)],
        scratch_shapes=[pltpu.VMEM((SEQ, 128), F32)] * 7,
        operands=(proj, cos, sin_signed, cat, lse, dcat, dproj), exchanges=exchanges, aliases={6: 0})
    return dproj, results


def _bdot(a, b, dims, precision=None):
    if precision is None:
        a = a.astype(BF16)
        b = b.astype(BF16)
    return lax.dot_general(a, b, (dims, ((0,), (0,))), preferred_element_type=F32, precision=precision)


def _make_bmm(precision):
    @jax.custom_vjp
    def nn(a, b):
        return _bdot(a, b, ((2,), (1,)), precision)

    @jax.custom_vjp
    def nt(a, b):
        return _bdot(a, b, ((2,), (2,)), precision)

    @jax.custom_vjp
    def tn(a, b):
        return _bdot(a, b, ((1,), (1,)), precision)

    nn.defvjp(lambda a, b: (nn(a, b), (a, b)), lambda r, g: (nt(g, r[1]), tn(r[0], g)))
    nt.defvjp(lambda a, b: (nt(a, b), (a, b)), lambda r, g: (nn(g, r[1]), tn(g, r[0])))
    tn.defvjp(lambda a, b: (tn(a, b), (a, b)), lambda r, g: (nt(r[1], g), nn(r[0], g)))
    return nn, nt, tn


BMM, BMM_NT, BMM_TN = _make_bmm(None)
BMM3, BMM3_NT, BMM3_TN = _make_bmm(lax.Precision.HIGH)
MM3, _, _ = _make_mm(lax.Precision.HIGH)


def _head_lanes(t, off):
    lane = lax.broadcasted_iota(jnp.int32, (1, 128), 1)
    return jnp.concatenate(
        [jnp.sum(t * (lane == off + h).astype(F32), axis=1, keepdims=True)[None] for h in range(NDH)], axis=0)


@jax.custom_vjp
def _unit_lower_inverse(a_mat):
    c = a_mat.shape[1]
    eye = (lax.broadcasted_iota(jnp.int32, (c, c), 0) == lax.broadcasted_iota(jnp.int32, (c, c), 1)).astype(F32)
    power = -a_mat
    t_inv = eye + power
    for _ in range(5):
        power = BMM3(power, power)
        t_inv = t_inv + BMM3(t_inv, power)
    return t_inv


def _unit_lower_inverse_fwd(a_mat):
    t_inv = _unit_lower_inverse(a_mat)
    return t_inv, t_inv


def _unit_lower_inverse_bwd(t_inv, d_inv):
    return (-BMM3_NT(BMM3_TN(t_inv, d_inv), t_inv),)


_unit_lower_inverse.defvjp(_unit_lower_inverse_fwd, _unit_lower_inverse_bwd)


def _delta_chunk(qr, kr, vr, z, tail, alog_row, dt_row, nw, state):
    c = qr.shape[1]
    beta = _sigmoid(_head_lanes(tail, 0))
    g = -jnp.exp(_head_lanes(alog_row, 0)) * _softplus(_head_lanes(tail, NDH) + _head_lanes(dt_row, 0))

    q = qr * lax.rsqrt(jnp.sum(qr * qr, axis=2, keepdims=True) + EPS) * (128 ** -0.5)
    k = kr * lax.rsqrt(jnp.sum(kr * kr, axis=2, keepdims=True) + EPS)

    ri = lax.broadcasted_iota(jnp.int32, (c, c), 0)
    ci = lax.broadcasted_iota(jnp.int32, (c, c), 1)
    tril = ri >= ci
    lane = lax.broadcasted_iota(jnp.int32, (1, 128), 1)
    g_lanes = sum(g[h] * (lane == h).astype(F32) for h in range(NDH))
    gc = _head_lanes(MM3(tril.astype(F32), g_lanes), 0)
    g_row = jnp.swapaxes(jnp.broadcast_to(gc, (NDH, c, c)), 1, 2)
    decay = jnp.where(tril, jnp.exp(jnp.where(tril, gc - g_row, 0.0)), 0.0)
    kb = k * beta
    t_inv = _unit_lower_inverse(jnp.where(ri > ci, BMM_NT(kb, k) * decay, 0.0))
    eg = jnp.exp(gc)
    u = BMM(t_inv, vr * beta)
    w = BMM(t_inv, kb * eg)
    qk = BMM_NT(q, k) * decay
    g_tot = jnp.sum(g, axis=1, keepdims=True)
    v_new = u - BMM(w, state)
    o = BMM(q * eg, state) + BMM(qk, v_new)
    new_state = state * jnp.exp(g_tot) + BMM_TN(k * jnp.exp(g_tot - gc), v_new)
    on = o * lax.rsqrt(jnp.mean(o * o, axis=2, keepdims=True) + EPS) * nw
    return on * _silu(z), new_state


def _heads(v, off=0):
    return jnp.concatenate([v[None, :, off + 128 * h:off + 128 * (h + 1)] for h in range(NDH)], axis=0)


def _unheads(t):
    return jnp.concatenate([t[h] for h in range(NDH)], axis=1)


def _delta_fwd(c_qkv, proj, alog_row, dt_row, nw, cat, exchanges=()):
    def body(c_ref, z_ref, tail_ref, al_ref, dt_ref, nw_ref, _, y_ref, st_ref, state):
        @pl.when(pl.program_id(0) == 0)
        def _():
            state[...] = jnp.zeros_like(state)

        cv = c_ref[...]
        st_ref[0] = state[...]
        y, new_state = _delta_chunk(_heads(cv), _heads(cv, 512), _heads(cv, 1024), _heads(z_ref[...]), tail_ref[...],
                                    al_ref[...], dt_ref[...], nw_ref[...], state[...])
        y_ref[...] = _unheads(y)
        state[...] = new_state

    row = pl.BlockSpec((1, 128), lambda n: (0, 0))
    return _hosted_call(
        body, name="delta_fwd", steps=NCH,
        in_specs=[pl.BlockSpec((CH, 1536), lambda n: (n, 0)), pl.BlockSpec((CH, 512), lambda n: (n, DN_Z_COL // 512)),
                  pl.BlockSpec((CH, 128), lambda n: (n, DN_TAIL_BLK)), row, row, row, pl.BlockSpec(memory_space=pl.ANY)],
        out_specs=[pl.BlockSpec((CH, 512), lambda n: (n, 1)),
                   pl.BlockSpec((1, NDH, 128, 128), lambda n: (n, 0, 0, 0))],
        out_shape=[jax.ShapeDtypeStruct((SEQ, 2 * ATTN_W), F32), jax.ShapeDtypeStruct((NCH, NDH, 128, 128), F32)],
        scratch_shapes=[pltpu.VMEM((NDH, 128, 128), F32)],
        operands=(c_qkv, proj, proj, alog_row, dt_row, nw, cat), exchanges=exchanges, aliases={6: 0})


def _delta_bwd(c_qkv, proj, alog_row, dt_row, nw, states, dcat, exchanges=()):
    def body(c_ref, z_ref, tail_ref, al_ref, dt_ref, nw_ref, st_ref, dy_ref,
             dp_ref, dc_ref, dal_ref, ddt_ref, dnw_ref, dstate):
        @pl.when(pl.program_id(0) == 0)
        def _():
            dstate[...] = jnp.zeros_like(dstate)
            dal_ref[...] = jnp.zeros_like(dal_ref)
            ddt_ref[...] = jnp.zeros_like(ddt_ref)
            dnw_ref[...] = jnp.zeros_like(dnw_ref)

        cv = c_ref[...]
        _, vjp = jax.vjp(_delta_chunk, _heads(cv), _heads(cv, 512), _heads(cv, 1024), _heads(z_ref[...]),
                         tail_ref[...], al_ref[...], dt_ref[...], nw_ref[...], st_ref[0])
        dq, dk, dv, dz, dtail, dal, ddt, dnw, dst = vjp((_heads(dy_ref[...]), dstate[...]))
        dstate[...] = dst
        dc_ref[...] = jnp.concatenate([_unheads(dq), _unheads(dk), _unheads(dv)], axis=1)
        dp_ref[...] = jnp.concatenate([_unheads(dz), dtail, jnp.zeros((CH, 128), F32)], axis=1)
        dal_ref[...] += dal
        ddt_ref[...] += ddt
        dnw_ref[...] += dnw

    rev = lambda n: NCH - 1 - n
    row = pl.BlockSpec((1, 128), lambda n: (0, 0))
    return _hosted_call(
        body, name="delta_bwd", steps=NCH,
        in_specs=[pl.BlockSpec((CH, 1536), lambda n: (rev(n), 0)),
                  pl.BlockSpec((CH, 512), lambda n: (rev(n), DN_Z_COL // 512)),
                  pl.BlockSpec((CH, 128), lambda n: (rev(n), DN_TAIL_BLK)), row, row, row,
                  pl.BlockSpec((1, NDH, 128, 128), lambda n: (rev(n), 0, 0, 0)),
                  pl.BlockSpec((CH, 512), lambda n: (rev(n), 1))],
        out_specs=[pl.BlockSpec((CH, 768), lambda n: (rev(n), DN_Z_COL // 768)),
                   pl.BlockSpec((CH, 1536), lambda n: (rev(n), 0)), row, row, row],
        out_shape=[jax.ShapeDtypeStruct((SEQ, IN_PAD), F32), jax.ShapeDtypeStruct((SEQ, 1536), F32)]
        + [jax.ShapeDtypeStruct((1, 128), F32)] * 3,
        scratch_shapes=[pltpu.VMEM((NDH, 128, 128), F32)],
        operands=(c_qkv, proj, proj, alog_row, dt_row, nw, states, dcat), exchanges=exchanges)


def _place():
    x, y, c = lax.axis_index("x"), lax.axis_index("y"), lax.axis_index("c")
    other_chips = [(1 - x, y), (x, 1 - y), (1 - x, 1 - y)]
    return x, y, c, other_chips


def _gather_exchange(shards):
    n = len(shards)

    def copies(ins, outs, sems):
        send_sems, recv_sems, local_sems = sems
        x, y, c, chips = _place()
        me, sibling = (x, y, c), (x, y, 1 - c)

        def copy(b, k, block, to, src=None):
            slot = outs[b].at[4 * block[0] + 2 * block[1] + block[2]]
            return pltpu.make_async_remote_copy(
                src_ref=slot if src is None else src, dst_ref=slot,
                send_sem=send_sems.at[b, k], recv_sem=recv_sems.at[b, k], device_id=to, device_id_type=MESH)

        mine = [pltpu.make_async_copy(ins[b], outs[b].at[4 * x + 2 * y + c], local_sems.at[b]) for b in range(n)]
        first = []
        for b in range(n):
            first.append(copy(b, 0, me, sibling, src=ins[b]))
            first += [copy(b, 1 + j, me, (*chip, c), src=ins[b]) for j, chip in enumerate(chips)]
        over_ici = [copy(b, 1 + j, (*chip, c), me) for b in range(n) for j, chip in enumerate(chips)]
        passed = [copy(b, 4 + j, (*chip, c), sibling) for b in range(n) for j, chip in enumerate(chips)]
        from_sibling = []
        for b in range(n):
            from_sibling.append(copy(b, 0, sibling, me))
            from_sibling += [copy(b, 4 + j, (*chip, 1 - c), me) for j, chip in enumerate(chips)]
        return mine, first, over_ici, passed, from_sibling

    def start(ins, outs, sems):
        mine, first, _, _, _ = copies(ins, outs, sems)
        for cp in mine + first:
            cp.start()

    def middle(ins, outs, sems):
        _, _, over_ici, passed, _ = copies(ins, outs, sems)
        for arrived, onward in zip(over_ici, passed):
            arrived.wait_recv()
            onward.start()

    def finish(ins, outs, sems):
        mine, first, _, passed, from_sibling = copies(ins, outs, sems)
        for cp in from_sibling:
            cp.wait_recv()
        for cp in first + passed:
            cp.wait_send()
        for cp in mine:
            cp.wait()

    return Exchange(shards, [jax.ShapeDtypeStruct((N_DEV,) + s.shape, s.dtype) for s in shards],
                    [pltpu.SemaphoreType.DMA((n, 7)), pltpu.SemaphoreType.DMA((n, 7)), pltpu.SemaphoreType.DMA((n,))],
                    start, middle, finish)


def _sibling_exchange(gs):
    n = len(gs)

    def copies(ins, outs, sems):
        send_sems, recv_sems = sems
        x, y, c, _ = _place()
        return [pltpu.make_async_remote_copy(
            src_ref=ins[b].at[2 * p + (1 - c)], dst_ref=outs[b].at[p],
            send_sem=send_sems.at[b, p], recv_sem=recv_sems.at[b, p],
            device_id=(x, y, 1 - c), device_id_type=MESH) for b in range(n) for p in range(4)]

    def start(ins, outs, sems):
        for cp in copies(ins, outs, sems):
            cp.start()

    def finish(ins, outs, sems):
        for cp in copies(ins, outs, sems):
            cp.wait()

    return Exchange(gs, [jax.ShapeDtypeStruct((4,) + g.shape[1:], g.dtype) for g in gs],
                    [pltpu.SemaphoreType.DMA((n, 4)), pltpu.SemaphoreType.DMA((n, 4))], start, None, finish)


def _chips_exchange(hs):
    n = len(hs)

    def copies(ins, outs, sems):
        send_sems, recv_sems, local_sems = sems
        x, y, c, chips = _place()
        my_chip = 2 * x + y
        local = [pltpu.make_async_copy(ins[b].at[my_chip], outs[b].at[my_chip], local_sems.at[b]) for b in range(n)]
        sends, arrivals = [], []
        for b in range(n):
            for k, (px, py) in enumerate(chips):
                peer = 2 * px + py
                sends.append(pltpu.make_async_remote_copy(
                    src_ref=ins[b].at[peer], dst_ref=outs[b].at[my_chip],
                    send_sem=send_sems.at[b, k], recv_sem=recv_sems.at[b, k],
                    device_id=(px, py, c), device_id_type=MESH))
                arrivals.append(pltpu.make_async_remote_copy(
                    src_ref=ins[b].at[peer], dst_ref=outs[b].at[peer],
                    send_sem=send_sems.at[b, k], recv_sem=recv_sems.at[b, k],
                    device_id=(px, py, c), device_id_type=MESH))
        return local, sends, arrivals

    def start(ins, outs, sems):
        local, sends, _ = copies(ins, outs, sems)
        for cp in local + sends:
            cp.start()

    def finish(ins, outs, sems):
        local, sends, arrivals = copies(ins, outs, sems)
        for cp in arrivals:
            cp.wait_recv()
        for cp in sends:
            cp.wait_send()
        for cp in local:
            cp.wait()

    return Exchange(hs, [jax.ShapeDtypeStruct(h.shape, h.dtype) for h in hs],
                    [pltpu.SemaphoreType.DMA((n, 3)), pltpu.SemaphoreType.DMA((n, 3)), pltpu.SemaphoreType.DMA((n,))],
                    start, None, finish)


def _run_exchange(exchange, name):
    n_in, n_out = len(exchange.operands), len(exchange.out_shapes)

    def body(*refs):
        ins, outs, sems = refs[:n_in], refs[n_in:n_in + n_out], refs[n_in + n_out:]
        exchange.start(ins, outs, sems)
        if exchange.middle is not None:
            exchange.middle(ins, outs, sems)
        exchange.finish(ins, outs, sems)

    return pl.pallas_call(
        body, name=name,
        in_specs=[HBM_SPEC] * n_in, out_specs=[HBM_SPEC] * n_out,
        out_shape=exchange.out_shapes, scratch_shapes=exchange.sems,
    )(*exchange.operands)


def _pair_add(g, r, core, name):
    _, nr, nc = g.shape
    tr = nr // 2 if nr % 32 == 0 else nr

    def body(core_ref, g_ref, r_ref, o_ref):
        o_ref[...] = (g_ref[...].astype(F32) + r_ref[...].astype(F32)).astype(BF16)

    return pl.pallas_call(
        body, name=name,
        grid_spec=pltpu.PrefetchScalarGridSpec(
            num_scalar_prefetch=1, grid=(4, nr // tr),
            in_specs=[pl.BlockSpec((1, tr, nc), lambda p, i, core: (2 * p + core[0], i, 0)),
                      pl.BlockSpec((1, tr, nc), lambda p, i, core: (p, i, 0))],
            out_specs=pl.BlockSpec((1, tr, nc), lambda p, i, core: (p, i, 0))),
        out_shape=jax.ShapeDtypeStruct(r.shape, BF16),
        compiler_params=_cp("parallel", "parallel"),
    )(core, g, r)


def _all_gather_sum_small(v):
    rows = v.shape[0]

    def body(x_ref, sum_ref, out_ref, send_sems, recv_sems, local_sem):
        x, y, c, chips = _place()
        me, sibling = (x, y, c), (x, y, 1 - c)

        def block(px, py, pc):
            return out_ref.at[pl.ds((4 * px + 2 * py + pc) * rows, rows), :]

        def copy(k, blk, to, src=None):
            return pltpu.make_async_remote_copy(
                src_ref=block(*blk) if src is None else src, dst_ref=block(*blk),
                send_sem=send_sems.at[k], recv_sem=recv_sems.at[k], device_id=to, device_id_type=MESH)

        mine = pltpu.make_async_copy(x_ref, block(*me), local_sem)
        mine.start()
        first = [copy(0, me, sibling, src=x_ref)]
        first += [copy(1 + j, me, (*chip, c), src=x_ref) for j, chip in enumerate(chips)]
        for cp in first:
            cp.start()
        passed = [copy(4 + j, (*chip, c), sibling) for j, chip in enumerate(chips)]
        for j, chip in enumerate(chips):
            copy(1 + j, (*chip, c), me).wait_recv()
            passed[j].start()
        copy(0, sibling, me).wait_recv()
        for j, chip in enumerate(chips):
            copy(4 + j, (*chip, 1 - c), me).wait_recv()
        for cp in first + passed:
            cp.wait_send()
        mine.wait()
        total = out_ref[pl.ds(0, rows), :]
        for d in range(1, N_DEV):
            total = total + out_ref[pl.ds(d * rows, rows), :]
        sum_ref[...] = total

    vm = pl.BlockSpec(memory_space=pltpu.VMEM)
    return pl.pallas_call(
        body, name="small_all_reduce",
        in_specs=[vm], out_specs=[vm],
        out_shape=[jax.ShapeDtypeStruct((rows, 128), F32)],
        scratch_shapes=[pltpu.VMEM((N_DEV * rows, 128), F32), pltpu.SemaphoreType.DMA((7,)),
                        pltpu.SemaphoreType.DMA((7,)), pltpu.SemaphoreType.DMA],
    )(v)[0]


def _adamw(w, g, m, v):
    m = ADAM_B1 * m + (1.0 - ADAM_B1) * g
    v = ADAM_B2 * v + (1.0 - ADAM_B2) * (g * g)
    m_hat = m / (1.0 - ADAM_B1 ** ADAM_STEP)
    v_hat = v / (1.0 - ADAM_B2 ** ADAM_STEP)
    delta = -ADAM_LR * (m_hat / (jnp.sqrt(v_hat) + ADAM_EPS) + ADAM_WD * w)
    return delta, m, v


ADAM_TILE = dict(w_in=(IN_COLS // N_DEV, 256), w_out=(128, D_MODEL), ffn_w_in=(176, D_MODEL), ffn_w_out=(176, D_MODEL))


def _sum_chips(p):
    p = p.astype(F32)
    return (p[0] + p[1]) + (p[2] + p[3])


def _adamw_sharded(parts, w, m, v, tile, name):
    nl, nr, nc = w.shape
    tr, tc = tile

    def body(*refs):
        p_refs, (w_ref, m_ref, v_ref, g_ref, d_ref, nm_ref, nv_ref) = refs[:nl], refs[nl:]
        layer = pl.program_id(0)
        p = p_refs[0][...]
        for l in range(1, nl):
            p = jnp.where(layer == l, p_refs[l][...], p)
        g = _sum_chips(p)
        delta, nm, nv = _adamw(w_ref[0], g, m_ref[0], v_ref[0])
        g_ref[0] = g
        d_ref[0] = delta
        nm_ref[0] = nm
        nv_ref[0] = nv

    blk = pl.BlockSpec((1, tr, tc), lambda l, i, j: (l, i, j))
    return pl.pallas_call(
        body, name=name, grid=(nl, nr // tr, nc // tc),
        in_specs=[pl.BlockSpec((4, tr, tc), lambda l, i, j: (0, i, j))] * nl + [blk, blk, blk],
        out_specs=[blk] * 4,
        out_shape=[jax.ShapeDtypeStruct(w.shape, F32)] * 4,
        compiler_params=_cp("parallel", "parallel", "parallel"),
    )(*parts, w, m, v)


def _adamw_small(g, w, m, v):
    def body(g_ref, w_ref, m_ref, v_ref, d_ref, nm_ref, nv_ref):
        delta, nm, nv = _adamw(w_ref[...], g_ref[...], m_ref[...], v_ref[...])
        d_ref[...] = delta
        nm_ref[...] = nm
        nv_ref[...] = nv

    return pl.pallas_call(
        body, name="adamw_small",
        out_shape=[jax.ShapeDtypeStruct(g.shape, F32)] * 3,
    )(g, w, m, v)


def _pack(arrays, rows):
    flat = jnp.concatenate([a.reshape(-1).astype(F32) for a in arrays])
    return jnp.pad(flat, (0, rows * 128 - flat.shape[0])).reshape(rows, 128)


def _unpack(packed, shapes):
    flat = packed.reshape(-1)
    out, off = [], 0
    for s in shapes:
        n = math.prod(s)
        out.append(flat[off:off + n].reshape(s))
        off += n
    return out


def _row(v, width=None):
    v = v.reshape(1, -1)
    return v if width is None else jnp.pad(v, ((0, 0), (0, width - v.shape[1])))


def _layer_fwd(x, wts, tables, attn_exchanges=(), delta_exchanges=(), on_attn=None, on_delta=None):
    h = _norm_fwd(x, wts["norm_pre_mix"], "norm_pre_mix")
    proj = _matmul(h, wts["w_in"], tb=True, tm=SEQ, tn=768, tk=1024, name="mm_proj")
    (cat, lse), got = _attn_fwd(proj, *tables, exchanges=attn_exchanges)
    if on_attn is not None:
        on_attn(got)
    c_qkv = _dnconv_fwd(proj, wts["dn_conv_w"])
    (cat, states), got = _delta_fwd(c_qkv, proj, wts["dn_a_log"], wts["dn_dt_bias"], wts["dn_norm_w"], cat,
                                    exchanges=delta_exchanges)
    if on_delta is not None:
        on_delta(got)
    mix = _matmul(cat, wts["w_out"], tm=512, tn=1024, tk=1024, name="mm_mix")
    x1 = _resnorm_fwd(x, mix, wts["norm_post_mix"], "norm_post_mix")
    h2 = _norm_fwd(x1, wts["norm_pre_ffn"], "norm_pre_ffn")
    pre = _matmul(h2, wts["ffn_w_in"], tb=True, tm=SEQ, tn=512, tk=1024, name="mm_ffn_in")
    act = _ffact_fwd(pre, wts["ffn_conv_w"], wts["ffn_conv_b"])
    f = _matmul(act, wts["ffn_w_out"], tm=512, tn=1024, tk=D_FF, name="mm_ffn_out")
    x2 = _resnorm_fwd(x1, f, wts["norm_post_ffn"], "norm_post_ffn")
    saved = dict(x=x, h=h, proj=proj, lse=lse, c_qkv=c_qkv, states=states, cat=cat, mix=mix, x1=x1, h2=h2, pre=pre,
                 act=act, f=f)
    return x2, saved


def _layer_bwd(dx2, wts, s, tables, ffact_exchanges=(), delta_exchanges=None, attn_exchanges=None):
    g = {}
    df, g["norm_post_ffn"] = _norm_bwd(s["f"], wts["norm_post_ffn"], dx2, None, "norm_post_ffn_bwd")
    dact = _matmul(df, wts["ffn_w_out"], tb=True, tm=SEQ, tn=1408, tk=1024, name="mm_dact", out_dtype=BF16)
    g["ffn_w_out"] = _matmul(s["act"], df, ta=True, tm=1408, tn=512, tk=SEQ, name="mm_dw_ffn_out", out_dtype=BF16)
    (dpre, g["ffn_conv_w"], g["ffn_conv_b"]), got = _ffact_bwd(s["pre"], wts["ffn_conv_w"], wts["ffn_conv_b"], dact,
                                                               exchanges=ffact_exchanges)
    dh2 = _matmul(dpre, wts["ffn_w_in"], tm=1024, tn=1024, tk=1408, name="mm_dh2")
    g["ffn_w_in"] = _matmul(dpre, s["h2"], ta=True, tm=512, tn=1024, tk=SEQ, name="mm_dw_ffn_in", out_dtype=BF16)
    dx1, g["norm_pre_ffn"] = _norm_bwd(s["x1"], wts["norm_pre_ffn"], dh2, dx2, "norm_pre_ffn_bwd")
    dmix, g["norm_post_mix"] = _norm_bwd(s["mix"], wts["norm_post_mix"], dx1, None, "norm_post_mix_bwd")
    dcat = _matmul(dmix, wts["w_out"], tb=True, tm=SEQ, tn=512, tk=1024, name="mm_dcat")
    g["w_out"] = _matmul(s["cat"], dmix, ta=True, tm=1024, tn=512, tk=SEQ, name="mm_dw_out", out_dtype=BF16)
    (dproj, dc, g["dn_a_log"], g["dn_dt_bias"], g["dn_norm_w"]), got = _delta_bwd(
        s["c_qkv"], s["proj"], wts["dn_a_log"], wts["dn_dt_bias"], wts["dn_norm_w"], s["states"], dcat,
        exchanges=delta_exchanges(g, got) if delta_exchanges is not None else ())
    dproj, got = _attn_bwd(s["proj"], *tables, s["cat"], s["lse"], dcat, dproj,
                           exchanges=attn_exchanges(got) if attn_exchanges is not None else ())
    dproj, g["dn_conv_w"] = _dnconv_bwd(s["proj"], wts["dn_conv_w"], dc, dproj)
    dh = _matmul(dproj, wts["w_in"], tm=1024, tn=1024, tk=1280, name="mm_dh")
    g["w_in"] = _matmul(dproj, s["h"], ta=True, tm=768, tn=1024, tk=SEQ, name="mm_dw_in", out_dtype=BF16)
    dx, g["norm_pre_mix"] = _norm_bwd(s["x"], wts["norm_pre_mix"], dh, dx1, "norm_pre_mix_bwd")
    return dx, g, got


BIG = ("w_in", "w_out", "ffn_w_in", "ffn_w_out")
COLUMN_SHARDED = ("w_in", "ffn_w_in")
SMALL_SHARDED = ("dn_conv_w", "ffn_conv_w")
REPLICATED = ("dn_a_log", "dn_dt_bias", "dn_norm_w", "ffn_conv_b", "norm_pre_mix", "norm_post_mix", "norm_pre_ffn",
              "norm_post_ffn")
WEIGHTS = ("w_in", "dn_conv_w", "dn_a_log", "dn_dt_bias", "dn_norm_w", "w_out", "ffn_w_in", "ffn_conv_w", "ffn_conv_b",
           "ffn_w_out", "norm_pre_mix", "norm_post_mix", "norm_pre_ffn", "norm_post_ffn")
FULL_SHAPE = dict(dn_conv_w=(DEPTH, 4, 1536), ffn_conv_w=(DEPTH, 3, 2 * D_FF), dn_a_log=(DEPTH, NDH),
                  dn_dt_bias=(DEPTH, NDH), dn_norm_w=(DEPTH, 128), ffn_conv_b=(DEPTH, 2 * D_FF),
                  norm_pre_mix=(DEPTH, D_MODEL), norm_post_mix=(DEPTH, D_MODEL), norm_pre_ffn=(DEPTH, D_MODEL),
                  norm_post_ffn=(DEPTH, D_MODEL))
SMALL_GRAD_ORDER = REPLICATED + SMALL_SHARDED
SMALL_GRAD_ROWS = 520
SMALL_W_ROWS = 48
SMALL_ADAM_ROWS = 200


def _w_in_rows_to_kernel_order(t):
    qkv = t[:QKV_W].reshape(3, N_PAIR, 128, -1).swapaxes(0, 1).reshape(QKV_W, -1)
    return jnp.pad(jnp.concatenate([qkv, t[QKV_W:]], axis=0), ((0, IN_PAD - IN_COLS), (0, 0)))


def _w_in_rows_from_kernel_order(t):
    qkv = t[:QKV_W].reshape(N_PAIR, 3, 128, -1).swapaxes(0, 1).reshape(QKV_W, -1)
    return jnp.concatenate([qkv, t[QKV_W:IN_COLS]], axis=0)


def _interleave_ff_rows(t):
    return t.reshape(2, FF_BLKS, 128, -1).swapaxes(0, 1).reshape(2 * D_FF, -1)


def _deinterleave_ff_rows(t):
    return t.reshape(FF_BLKS, 2, 128, -1).swapaxes(0, 1).reshape(2 * D_FF, -1)


def kernel(x, w_in, dn_conv_w, dn_a_log, dn_dt_bias, dn_norm_w, w_out, ffn_w_in, ffn_conv_w, ffn_conv_b, ffn_w_out, norm_pre_mix, norm_post_mix, norm_pre_ffn, norm_post_ffn, loss_target, m_w_in, m_dn_conv_w, m_dn_a_log, m_dn_dt_bias, m_dn_norm_w, m_w_out, m_ffn_w_in, m_ffn_conv_w, m_ffn_conv_b, m_ffn_w_out, m_norm_pre_mix, m_norm_post_mix, m_norm_pre_ffn, m_norm_post_ffn, v_w_in, v_dn_conv_w, v_dn_a_log, v_dn_dt_bias, v_dn_norm_w, v_w_out, v_ffn_w_in, v_ffn_conv_w, v_ffn_conv_b, v_ffn_w_out, v_norm_pre_mix, v_norm_post_mix, v_norm_pre_ffn, v_norm_post_ffn):
    local = dict(w_in=w_in, dn_conv_w=dn_conv_w, dn_a_log=dn_a_log, dn_dt_bias=dn_dt_bias, dn_norm_w=dn_norm_w,
                 w_out=w_out, ffn_w_in=ffn_w_in, ffn_conv_w=ffn_conv_w, ffn_conv_b=ffn_conv_b, ffn_w_out=ffn_w_out,
                 norm_pre_mix=norm_pre_mix, norm_post_mix=norm_post_mix, norm_pre_ffn=norm_pre_ffn,
                 norm_post_ffn=norm_post_ffn)
    mom_m = dict(w_in=m_w_in, dn_conv_w=m_dn_conv_w, dn_a_log=m_dn_a_log, dn_dt_bias=m_dn_dt_bias,
                 dn_norm_w=m_dn_norm_w, w_out=m_w_out, ffn_w_in=m_ffn_w_in, ffn_conv_w=m_ffn_conv_w,
                 ffn_conv_b=m_ffn_conv_b, ffn_w_out=m_ffn_w_out, norm_pre_mix=m_norm_pre_mix,
                 norm_post_mix=m_norm_post_mix, norm_pre_ffn=m_norm_pre_ffn, norm_post_ffn=m_norm_post_ffn)
    mom_v = dict(w_in=v_w_in, dn_conv_w=v_dn_conv_w, dn_a_log=v_dn_a_log, dn_dt_bias=v_dn_dt_bias,
                 dn_norm_w=v_dn_norm_w, w_out=v_w_out, ffn_w_in=v_ffn_w_in, ffn_conv_w=v_ffn_conv_w,
                 ffn_conv_b=v_ffn_conv_b, ffn_w_out=v_ffn_w_out, norm_pre_mix=v_norm_pre_mix,
                 norm_post_mix=v_norm_post_mix, norm_pre_ffn=v_norm_pre_ffn, norm_post_ffn=v_norm_post_ffn)
    dev = 4 * lax.axis_index("x") + 2 * lax.axis_index("y") + lax.axis_index("c")
    core = lax.axis_index("c").astype(jnp.int32).reshape(1)

    def shard(n, l):
        s = local[n].transpose(0, 2, 1) if n in COLUMN_SHARDED else local[n]
        return s[l].astype(BF16)

    def matrix(n, gathered):
        if n == "w_in":
            return _w_in_rows_to_kernel_order(gathered.reshape(IN_COLS, D_MODEL))
        if n == "ffn_w_in":
            return _interleave_ff_rows(gathered.reshape(2 * D_FF, D_MODEL))
        return gathered.reshape(-1, D_MODEL)

    small_w = _pack([dn_conv_w, ffn_conv_w], SMALL_W_ROWS)
    g_w_in0, g_small = _run_exchange(_gather_exchange([shard("w_in", 0), small_w]), "weights_all_gather")
    n_dn, n_ff = DEPTH * 4 * 192, DEPTH * 3 * 704
    sm = g_small.reshape(N_DEV, -1)
    full_dn_conv = sm[:, :n_dn].reshape(N_DEV, DEPTH, 4, 192).transpose(1, 2, 0, 3).reshape(DEPTH, 4, 1536)
    full_ff_conv = _interleave_ff(
        sm[:, n_dn:n_dn + n_ff].reshape(N_DEV, DEPTH, 3, 704).transpose(1, 2, 0, 3).reshape(DEPTH, 3, 2 * D_FF))

    def small_weights(l):
        wts = dict(dn_conv_w=full_dn_conv[l], ffn_conv_w=full_ff_conv[l], ffn_conv_b=_interleave_ff(_row(ffn_conv_b[l])),
                   dn_a_log=_row(dn_a_log[l], 128), dn_dt_bias=_row(dn_dt_bias[l], 128))
        for n in ("dn_norm_w", "norm_pre_mix", "norm_post_mix", "norm_pre_ffn", "norm_post_ffn"):
            wts[n] = _row(local[n][l])
        return wts

    weights = [small_weights(l) for l in range(DEPTH)]
    weights[0]["w_in"] = matrix("w_in", g_w_in0)

    def gather_behind(wanted):
        def deliver(got):
            for (n, l), g in zip(wanted, got[0]):
                weights[l][n] = matrix(n, g)

        return [_gather_exchange([shard(n, l) for n, l in wanted])], deliver

    tables = _rope_tables()
    ex_attn0, on_attn0 = gather_behind([("w_out", 0), ("ffn_w_in", 0)])
    ex_delta0, on_delta0 = gather_behind([("ffn_w_out", 0), ("w_in", 1)])
    ex_attn1, on_attn1 = gather_behind([("w_out", 1), ("ffn_w_in", 1)])
    ex_delta1, on_delta1 = gather_behind([("ffn_w_out", 1)])
    act, saved0 = _layer_fwd(x[0], weights[0], tables, ex_attn0, ex_delta0, on_attn0, on_delta0)
    act, saved1 = _layer_fwd(act, weights[1], tables, ex_attn1, ex_delta1, on_attn1, on_delta1)
    loss_part, dact = _loss_fwd_bwd(act, loss_target[0])

    def to_devices(name, t):
        if name == "w_in":
            t = _w_in_rows_from_kernel_order(t)
        if name == "ffn_w_in":
            t = _deinterleave_ff_rows(t)
        return t.reshape(N_DEV, t.shape[0] // N_DEV, t.shape[1])

    def pair_sums(names, layer, to_dev, from_sibling):
        return [_pair_add(gd, r, core, "grads_pair_add_%s_%d" % (n, layer))
                for n, gd, r in zip(names, to_dev, from_sibling)]

    grads = [None] * DEPTH
    dact, grads[1], _ = _layer_bwd(dact, weights[1], saved1, tables)
    to_dev1 = [to_devices(n, grads[1][n]) for n in BIG]
    early = ("w_out", "ffn_w_in", "ffn_w_out")
    parts, stash = {}, {}

    def delta_exchanges(g, got_ffact):
        stash["to_dev0"] = [to_devices(n, g[n]) for n in early]
        return [_chips_exchange(pair_sums(BIG, 1, to_dev1, got_ffact[0])), _sibling_exchange(stash["to_dev0"])]

    def attn_exchanges(got_delta):
        for n, p in zip(BIG, got_delta[0]):
            parts[n, 1] = p
        return [_chips_exchange(pair_sums(early, 0, stash["to_dev0"], got_delta[1]))]

    dact, grads[0], got_attn = _layer_bwd(dact, weights[0], saved0, tables, [_sibling_exchange(to_dev1)],
                                          delta_exchanges, attn_exchanges)
    for n, p in zip(early, got_attn[0]):
        parts[n, 0] = p
    grad_x = dact[None]
    last = [to_devices("w_in", grads[0]["w_in"])]
    from_sibling = _run_exchange(_sibling_exchange(last), "grads_to_sibling")
    parts["w_in", 0], = _run_exchange(_chips_exchange(pair_sums(("w_in",), 0, last, from_sibling)), "grads_to_chips")

    def small_grad(name):
        t = jnp.stack([grads[l][name] for l in range(DEPTH)])
        if name in ("dn_a_log", "dn_dt_bias"):
            t = t[:, 0, :NDH]
        if name in ("ffn_conv_w", "ffn_conv_b"):
            t = _deinterleave_ff(t)
        return t.reshape(FULL_SHAPE[name])

    small_part = _pack([small_grad(n) for n in SMALL_GRAD_ORDER] + [loss_part[0, :1]], SMALL_GRAD_ROWS)
    small_sum = _all_gather_sum_small(small_part)
    small_g = dict(zip(SMALL_GRAD_ORDER + ("loss",), _unpack(small_sum, [FULL_SHAPE[n] for n in SMALL_GRAD_ORDER] + [(1,)])))
    loss = small_g["loss"][0]
    small_g["dn_conv_w"] = lax.dynamic_slice_in_dim(small_g["dn_conv_w"], dev * 192, 192, axis=2)
    small_g["ffn_conv_w"] = lax.dynamic_slice_in_dim(small_g["ffn_conv_w"], dev * 704, 704, axis=2)

    out_g, out_d, out_m, out_v = {}, {}, {}, {}
    for n in BIG:
        turn = (lambda t: t.transpose(0, 2, 1)) if n in COLUMN_SHARDED else (lambda t: t)
        outs = _adamw_sharded([parts[n, l] for l in range(DEPTH)], turn(local[n]), turn(mom_m[n]), turn(mom_v[n]),
                              ADAM_TILE[n], "adamw_" + n)
        out_g[n], out_d[n], out_m[n], out_v[n] = [turn(t) for t in outs]
    shapes = [small_g[n].shape for n in SMALL_GRAD_ORDER]
    d_s, m_s, v_s = _adamw_small(_pack([small_g[n] for n in SMALL_GRAD_ORDER], SMALL_ADAM_ROWS),
                                 _pack([local[n] for n in SMALL_GRAD_ORDER], SMALL_ADAM_ROWS),
                                 _pack([mom_m[n] for n in SMALL_GRAD_ORDER], SMALL_ADAM_ROWS),
                                 _pack([mom_v[n] for n in SMALL_GRAD_ORDER], SMALL_ADAM_ROWS))
    for n, d, m, v in zip(SMALL_GRAD_ORDER, _unpack(d_s, shapes), _unpack(m_s, shapes), _unpack(v_s, shapes)):
        out_g[n], out_d[n], out_m[n], out_v[n] = small_g[n], d, m, v
    return (loss, grad_x, *[out_g[n] for n in WEIGHTS], *[out_d[n] for n in WEIGHTS],
            *[out_m[n] for n in WEIGHTS], *[out_v[n] for n in WEIGHTS])
```

```python
import functools
import math

import jax
import jax.numpy as jnp
from jax import lax
from jax.experimental import pallas as pl
from jax.experimental.pallas import tpu as pltpu

F32 = jnp.float32
BF16 = jnp.bfloat16
HI = lax.Precision.HIGHEST
MESH = pl.DeviceIdType.MESH

N_DEV = 8
SEQ = 2048
D_MODEL = 1024
DEPTH = 2
N_PAIR = 4
HEAD_DIM = 64
ATTN_W = 512
ATTN_BLK = 128
DILATIONS = (1, 4, 16)
SEGMENT_BLOCKS = (16, 4, 1)
N_BLK = SEQ // ATTN_BLK
NDH = 4
CH = 64
NCH = SEQ // CH
IN_COLS = 3592
IN_PAD = 3840
QKV_W = 3 * ATTN_W
DN_QKV_BLK0 = QKV_W // 128
DN_QKV_BLKS = 1536 // 128
DN_Z_COL = 3072
DN_TAIL_BLK = 3584 // 128
D_FF = 2816
FF_BLKS = D_FF // 128
EPS = 1e-6
NEG = -1e30
ROPE_THETA = 10000.0

ADAM_LR, ADAM_B1, ADAM_B2, ADAM_EPS, ADAM_WD, ADAM_STEP = 0.001, 0.9, 0.999, 1e-08, 0.01, 10

VMEM_LIMIT = 56 * 1024 * 1024


def _cp(*sem):
    return pltpu.CompilerParams(dimension_semantics=sem, vmem_limit_bytes=VMEM_LIMIT)


class Exchange:
    def __init__(self, operands, out_shapes, sems, start, middle, finish):
        self.operands, self.out_shapes, self.sems = list(operands), list(out_shapes), list(sems)
        self.start, self.middle, self.finish = start, middle, finish


HBM_SPEC = pl.BlockSpec(memory_space=pltpu.HBM)


def _hosted_call(body, *, name, steps, in_specs, out_specs, out_shape, scratch_shapes, operands, exchanges=(),
                 aliases=None):
    n_in, n_out, n_scr = len(in_specs), len(out_specs), len(scratch_shapes)

    def take(refs, pos, counts):
        groups = []
        for c in counts:
            groups.append(refs[pos:pos + c])
            pos += c
        return groups, pos

    def full_body(*refs):
        ins, pos = refs[:n_in], n_in
        ex_ins, pos = take(refs, pos, [len(e.operands) for e in exchanges])
        outs, pos = refs[pos:pos + n_out], pos + n_out
        ex_outs, pos = take(refs, pos, [len(e.out_shapes) for e in exchanges])
        scr, pos = refs[pos:pos + n_scr], pos + n_scr
        ex_sems, pos = take(refs, pos, [len(e.sems) for e in exchanges])
        step = pl.program_id(0)
        for e, a, b, s in zip(exchanges, ex_ins, ex_outs, ex_sems):
            pl.when(step == 0)(functools.partial(e.start, a, b, s))
            if e.middle is not None:
                pl.when(step == steps // 2)(functools.partial(e.middle, a, b, s))
        body(*ins, *outs, *scr)
        for e, a, b, s in zip(exchanges, ex_ins, ex_outs, ex_sems):
            pl.when(step == steps - 1)(functools.partial(e.finish, a, b, s))

    n_ex_in = sum(len(e.operands) for e in exchanges)
    n_ex_out = sum(len(e.out_shapes) for e in exchanges)
    results = pl.pallas_call(
        full_body, name=name, grid=(steps,),
        in_specs=list(in_specs) + [HBM_SPEC] * n_ex_in,
        out_specs=list(out_specs) + [HBM_SPEC] * n_ex_out,
        out_shape=list(out_shape) + [s for e in exchanges for s in e.out_shapes],
        scratch_shapes=list(scratch_shapes) + [s for e in exchanges for s in e.sems],
        input_output_aliases=aliases or {},
        compiler_params=_cp("arbitrary"),
    )(*operands, *[a for e in exchanges for a in e.operands])
    ex_results, _ = take(results, n_out, [len(e.out_shapes) for e in exchanges])
    return results[:n_out], ex_results


def _dot(a, b, dims, precision=None):
    if precision is None:
        a = a.astype(BF16)
        b = b.astype(BF16)
    return lax.dot_general(a, b, (dims, ((), ())), preferred_element_type=F32, precision=precision)


def _make_mm(precision):
    @jax.custom_vjp
    def nn(a, b):
        return _dot(a, b, ((1,), (0,)), precision)

    @jax.custom_vjp
    def nt(a, b):
        return _dot(a, b, ((1,), (1,)), precision)

    @jax.custom_vjp
    def tn(a, b):
        return _dot(a, b, ((0,), (0,)), precision)

    nn.defvjp(lambda a, b: (nn(a, b), (a, b)), lambda r, g: (nt(g, r[1]), tn(r[0], g)))
    nt.defvjp(lambda a, b: (nt(a, b), (a, b)), lambda r, g: (nn(g, r[1]), tn(g, r[0])))
    tn.defvjp(lambda a, b: (tn(a, b), (a, b)), lambda r, g: (nt(r[1], g), nn(r[0], g)))
    return nn, nt, tn


MM, MM_NT, MM_TN = _make_mm(None)


def _matmul(a, b, *, ta=False, tb=False, tm, tn, tk, name, out_dtype=F32):
    (k_dim, m_dim) = a.shape if ta else a.shape[::-1]
    (n_dim, k2) = b.shape if tb else b.shape[::-1]
    assert k_dim == k2 and m_dim % tm == 0 and n_dim % tn == 0 and k_dim % tk == 0, (a.shape, b.shape, tm, tn, tk)
    nk = k_dim // tk
    dims = ((0 if ta else 1,), (1 if tb else 0,))

    def body(a_ref, b_ref, o_ref, *acc):
        p = _dot(a_ref[...], b_ref[...], dims)
        if nk == 1:
            o_ref[...] = p.astype(out_dtype)
            return
        acc_ref, k = acc[0], pl.program_id(2)

        @pl.when(k == 0)
        def _():
            acc_ref[...] = p

        @pl.when(k > 0)
        def _():
            acc_ref[...] += p

        @pl.when(k == nk - 1)
        def _():
            o_ref[...] = acc_ref[...].astype(out_dtype)

    a_spec = pl.BlockSpec((tk, tm), lambda i, j, k: (k, i)) if ta else pl.BlockSpec((tm, tk), lambda i, j, k: (i, k))
    b_spec = pl.BlockSpec((tn, tk), lambda i, j, k: (j, k)) if tb else pl.BlockSpec((tk, tn), lambda i, j, k: (k, j))
    return pl.pallas_call(
        body, name=name,
        grid=(m_dim // tm, n_dim // tn, nk),
        in_specs=[a_spec, b_spec],
        out_specs=pl.BlockSpec((tm, tn), lambda i, j, k: (i, j)),
        out_shape=jax.ShapeDtypeStruct((m_dim, n_dim), out_dtype),
        scratch_shapes=[pltpu.VMEM((tm, tn), F32)] if nk > 1 else [],
        compiler_params=_cp("parallel", "parallel", "arbitrary"),
    )(a, b)


NORM_ROWS = 256


def _rms(x, w):
    return x * lax.rsqrt(jnp.mean(x * x, axis=1, keepdims=True) + EPS) * w


def _norm_fwd(x, w_row, name, out_dtype=BF16):
    def body(x_ref, w_ref, o_ref):
        o_ref[...] = _rms(x_ref[...], w_ref[...]).astype(out_dtype)

    return pl.pallas_call(
        body, name=name, grid=(SEQ // NORM_ROWS,),
        in_specs=[pl.BlockSpec((NORM_ROWS, D_MODEL), lambda i: (i, 0)), pl.BlockSpec((1, D_MODEL), lambda i: (0, 0))],
        out_specs=pl.BlockSpec((NORM_ROWS, D_MODEL), lambda i: (i, 0)),
        out_shape=jax.ShapeDtypeStruct((SEQ, D_MODEL), out_dtype),
        compiler_params=_cp("parallel"),
    )(x, w_row)


def _resnorm_fwd(x, f, w_row, name):
    def body(x_ref, f_ref, w_ref, o_ref):
        o_ref[...] = x_ref[...] + _rms(f_ref[...], w_ref[...])

    blk = pl.BlockSpec((NORM_ROWS, D_MODEL), lambda i: (i, 0))
    return pl.pallas_call(
        body, name=name, grid=(SEQ // NORM_ROWS,),
        in_specs=[blk, blk, pl.BlockSpec((1, D_MODEL), lambda i: (0, 0))],
        out_specs=blk, out_shape=jax.ShapeDtypeStruct((SEQ, D_MODEL), F32),
        compiler_params=_cp("parallel"),
    )(x, f, w_row)


def _norm_bwd(x, w_row, dy, add, name, dx_dtype=F32):
    has_add = add is not None

    def body(*refs):
        if has_add:
            x_ref, w_ref, dy_ref, add_ref, dx_ref, dw_ref = refs
        else:
            x_ref, w_ref, dy_ref, dx_ref, dw_ref = refs
        _, vjp = jax.vjp(_rms, x_ref[...], w_ref[...])
        dx, dw = vjp(dy_ref[...])
        dx_ref[...] = (dx + add_ref[...] if has_add else dx).astype(dx_dtype)

        @pl.when(pl.program_id(0) == 0)
        def _():
            dw_ref[...] = jnp.zeros_like(dw_ref)

        dw_ref[...] += dw

    blk = pl.BlockSpec((NORM_ROWS, D_MODEL), lambda i: (i, 0))
    row = pl.BlockSpec((1, D_MODEL), lambda i: (0, 0))
    ins = [x, w_row, dy] + ([add] if has_add else [])
    return pl.pallas_call(
        body, name=name, grid=(SEQ // NORM_ROWS,),
        in_specs=[blk, row, blk] + ([blk] if has_add else []),
        out_specs=[blk, row],
        out_shape=[jax.ShapeDtypeStruct((SEQ, D_MODEL), dx_dtype), jax.ShapeDtypeStruct((1, D_MODEL), F32)],
        compiler_params=_cp("arbitrary"),
    )(*ins)


def _loss_fwd_bwd(y, target):
    def body(y_ref, t_ref, loss_ref, dy_ref):
        err = y_ref[...] - t_ref[...]
        dy_ref[...] = err * (1.0 / D_MODEL)

        @pl.when(pl.program_id(0) == 0)
        def _():
            loss_ref[...] = jnp.zeros_like(loss_ref)

        part = jnp.sum(jnp.sum(err * err, axis=1, keepdims=True) * (1.0 / D_MODEL), axis=0, keepdims=True)
        loss_ref[...] += 0.5 * jnp.broadcast_to(part, loss_ref.shape)

    blk = pl.BlockSpec((NORM_ROWS, D_MODEL), lambda i: (i, 0))
    return pl.pallas_call(
        body, name="loss", grid=(SEQ // NORM_ROWS,),
        in_specs=[blk, blk],
        out_specs=[pl.BlockSpec((1, 128), lambda i: (0, 0)), blk],
        out_shape=[jax.ShapeDtypeStruct((1, 128), F32), jax.ShapeDtypeStruct((SEQ, D_MODEL), F32)],
        compiler_params=_cp("arbitrary"),
    )(y, target)


def _make_shift(j):
    def down(x):
        row = lax.broadcasted_iota(jnp.int32, x.shape, 0)
        return jnp.where(row >= j, pltpu.roll(x, j, 0), 0.0)

    def up(x):
        n = x.shape[0]
        row = lax.broadcasted_iota(jnp.int32, x.shape, 0)
        return jnp.where(row < n - j, pltpu.roll(x, n - j, 0), 0.0)

    f = jax.custom_vjp(down)
    f.defvjp(lambda x: (down(x), None), lambda _, g: (up(g),))
    return f


_SHIFT = {j: _make_shift(j) for j in (1, 2, 3)}


def _causal_conv(x, taps):
    n = len(taps)
    acc = x * taps[n - 1]
    for k in range(n - 1):
        acc = acc + _SHIFT[n - 1 - k](x) * taps[k]
    return acc


def _tap_rows(w_ref, lanes=slice(None)):
    return tuple(w_ref[k:k + 1, lanes] for k in range(w_ref.shape[0]))


def _sigmoid(x):
    return 1.0 / (1.0 + jnp.exp(-x))


def _silu(x):
    return x * _sigmoid(x)


def _softplus(x):
    return jnp.maximum(x, 0.0) + jnp.log(1.0 + jnp.exp(-jnp.abs(x)))


def _gelu_tanh(x):
    return 0.5 * x * (1.0 + jnp.tanh(math.sqrt(2.0 / math.pi) * (x + 0.044715 * (x * x * x))))


def _dnconv_fn(x, taps):
    return _silu(_causal_conv(x, taps))


def _dnconv_fwd(proj, conv_w):
    def body(x_ref, w_ref, o_ref):
        o_ref[...] = _dnconv_fn(x_ref[...], _tap_rows(w_ref))

    return pl.pallas_call(
        body, name="dnconv_fwd", grid=(DN_QKV_BLKS,),
        in_specs=[pl.BlockSpec((SEQ, 128), lambda j: (0, DN_QKV_BLK0 + j)), pl.BlockSpec((4, 128), lambda j: (0, j))],
        out_specs=pl.BlockSpec((SEQ, 128), lambda j: (0, j)),
        out_shape=jax.ShapeDtypeStruct((SEQ, 1536), F32),
        compiler_params=_cp("parallel"),
    )(proj, conv_w)


def _dnconv_bwd(proj, conv_w, dc, dproj):
    def body(x_ref, w_ref, dc_ref, _, dx_ref, dw_ref):
        _, vjp = jax.vjp(_dnconv_fn, x_ref[...], _tap_rows(w_ref))
        dx, dw = vjp(dc_ref[...])
        dx_ref[...] = dx.astype(BF16)
        for k, row in enumerate(dw):
            dw_ref[k:k + 1, :] = row

    return pl.pallas_call(
        body, name="dnconv_bwd", grid=(DN_QKV_BLKS,),
        in_specs=[pl.BlockSpec((SEQ, 128), lambda j: (0, DN_QKV_BLK0 + j)), pl.BlockSpec((4, 128), lambda j: (0, j)),
                  pl.BlockSpec((SEQ, 128), lambda j: (0, j)), pl.BlockSpec(memory_space=pl.ANY)],
        out_specs=[pl.BlockSpec((SEQ, 128), lambda j: (0, DN_QKV_BLK0 + j)), pl.BlockSpec((4, 128), lambda j: (0, j))],
        out_shape=[jax.ShapeDtypeStruct((SEQ, IN_PAD), BF16), jax.ShapeDtypeStruct((4, 1536), F32)],
        input_output_aliases={3: 0},
        compiler_params=_cp("parallel"),
    )(proj, conv_w, dc, dproj)


def _ffact_fn(pg, pu, wg, wu, bg, bu):
    return _gelu_tanh(_causal_conv(pg, wg) + bg) * (_causal_conv(pu, wu) + bu)


def _ffact_args(p_ref, w_ref, b_ref):
    g, u = slice(0, 128), slice(128, 256)
    return (p_ref[:, g], p_ref[:, u], _tap_rows(w_ref, g), _tap_rows(w_ref, u), b_ref[:, g], b_ref[:, u])


def _ffact_fwd(pre, conv_w, conv_b):
    def body(p_ref, w_ref, b_ref, o_ref):
        o_ref[...] = _ffact_fn(*_ffact_args(p_ref, w_ref, b_ref)).astype(BF16)

    return pl.pallas_call(
        body, name="ffact_fwd", grid=(FF_BLKS,),
        in_specs=[pl.BlockSpec((SEQ, 256), lambda j: (0, j)), pl.BlockSpec((3, 256), lambda j: (0, j)),
                  pl.BlockSpec((1, 256), lambda j: (0, j))],
        out_specs=pl.BlockSpec((SEQ, 128), lambda j: (0, j)),
        out_shape=jax.ShapeDtypeStruct((SEQ, D_FF), BF16),
        compiler_params=_cp("parallel"),
    )(pre, conv_w, conv_b)


def _ffact_bwd(pre, conv_w, conv_b, dact, exchanges=()):
    def body(p_ref, w_ref, b_ref, da_ref, dp_ref, dw_ref, db_ref):
        _, vjp = jax.vjp(_ffact_fn, *_ffact_args(p_ref, w_ref, b_ref))
        dpg, dpu, dwg, dwu, dbg, dbu = vjp(da_ref[...].astype(F32))
        dp_ref[:, 0:128] = dpg.astype(BF16)
        dp_ref[:, 128:256] = dpu.astype(BF16)
        for k in range(3):
            dw_ref[k:k + 1, 0:128] = dwg[k]
            dw_ref[k:k + 1, 128:256] = dwu[k]
        db_ref[:, 0:128] = dbg
        db_ref[:, 128:256] = dbu

    return _hosted_call(
        body, name="ffact_bwd", steps=FF_BLKS,
        in_specs=[pl.BlockSpec((SEQ, 256), lambda j: (0, j)), pl.BlockSpec((3, 256), lambda j: (0, j)),
                  pl.BlockSpec((1, 256), lambda j: (0, j)), pl.BlockSpec((SEQ, 128), lambda j: (0, j))],
        out_specs=[pl.BlockSpec((SEQ, 256), lambda j: (0, j)), pl.BlockSpec((3, 256), lambda j: (0, j)),
                   pl.BlockSpec((1, 256), lambda j: (0, j))],
        out_shape=[jax.ShapeDtypeStruct((SEQ, 2 * D_FF), BF16), jax.ShapeDtypeStruct((3, 2 * D_FF), F32),
                   jax.ShapeDtypeStruct((1, 2 * D_FF), F32)],
        scratch_shapes=[], operands=(pre, conv_w, conv_b, dact), exchanges=exchanges)


def _interleave_ff(t):
    lead = t.shape[:-1]
    return t.reshape(lead + (2, FF_BLKS, 128)).swapaxes(-3, -2).reshape(lead + (2 * D_FF,))


def _deinterleave_ff(t):
    lead = t.shape[:-1]
    return t.reshape(lead + (FF_BLKS, 2, 128)).swapaxes(-3, -2).reshape(lead + (2 * D_FF,))


def _rope_tables():
    inv = 1.0 / (ROPE_THETA ** (jnp.arange(0, HEAD_DIM, 2, dtype=F32) / HEAD_DIM))
    ang = jnp.arange(SEQ, dtype=F32)[:, None] * inv[None, :]
    cos = jnp.tile(jnp.cos(ang), (1, 4))
    sin = jnp.tile(jnp.sin(ang), (1, 4))
    sign = jnp.where((jnp.arange(128) % HEAD_DIM) < HEAD_DIM // 2, -1.0, 1.0).astype(F32)
    return cos, sin * sign[None, :]


def _rope(x, cos, sin_signed):
    lane = lax.broadcasted_iota(jnp.int32, x.shape, 1)
    partner = jnp.where((lane % HEAD_DIM) < HEAD_DIM // 2, pltpu.roll(x, 128 - HEAD_DIM // 2, 1),
                        pltpu.roll(x, HEAD_DIM // 2, 1))
    return x * cos + partner * sin_signed


def _pairs_from_qkv(t):
    lead = t.shape[:-1]
    return t.reshape(lead + (3, N_PAIR, 128)).swapaxes(-3, -2).reshape(lead + (QKV_W,))


def _qkv_from_pairs(t):
    lead = t.shape[:-1]
    return t.reshape(lead + (N_PAIR, 3, 128)).swapaxes(-3, -2).reshape(lead + (QKV_W,))


def _head_masks():
    lane = lax.broadcasted_iota(jnp.int32, (1, 128), 1)
    return [(lane // HEAD_DIM) == h for h in range(2)]


def _both_heads(x):
    return jnp.concatenate([jnp.where(hm, x, 0.0)[None] for hm in _head_masks()], axis=0)


def _block_keys(branch, k_s, v_s, rows, prows, has_prev):
    a = lax.broadcasted_iota(jnp.int32, (ATTN_BLK, ATTN_BLK), 0)
    c = lax.broadcasted_iota(jnp.int32, (ATTN_BLK, ATTN_BLK), 1)
    keys, values, mask = k_s[rows, :], v_s[rows, :], c <= a
    if SEGMENT_BLOCKS[branch] > 1:
        keys = jnp.concatenate([k_s[prows, :], keys], axis=0)
        values = jnp.concatenate([v_s[prows, :], values], axis=0)
        mask = jnp.concatenate([(c >= a) & has_prev, mask], axis=1)
    twice = lambda t: jnp.broadcast_to(t[None], (2,) + t.shape)
    return twice(keys), twice(values), mask


def _block_rows(branch, t):
    d, per_seg = DILATIONS[branch], SEGMENT_BLOCKS[branch]
    if d == 1:
        start = pl.multiple_of(t * ATTN_BLK, ATTN_BLK)
        prev = pl.multiple_of(jnp.maximum(t - 1, 0) * ATTN_BLK, ATTN_BLK)
        return pl.ds(start, ATTN_BLK), pl.ds(prev, ATTN_BLK), t > 0
    r, n = t // per_seg, t % per_seg
    start = n * (ATTN_BLK * d) + r
    prev = jnp.maximum(n - 1, 0) * (ATTN_BLK * d) + r
    return pl.ds(start, ATTN_BLK, stride=d), pl.ds(prev, ATTN_BLK, stride=d), n > 0


def _attn_fwd(proj, cos, sin_signed, exchanges=()):
    scale = HEAD_DIM ** -0.5

    def body(qkv_ref, cos_ref, sin_ref, out_ref, lse_ref, q_s, k_s, v_s, *branch_s):
        o_s, l_s = branch_s[:3], branch_s[3:]
        q_s[...] = _rope(qkv_ref[:, 0:128], cos_ref[...], sin_ref[...])
        k_s[...] = _rope(qkv_ref[:, 128:256], cos_ref[...], sin_ref[...])
        v_s[...] = qkv_ref[:, 256:384]
        heads = _head_masks()
        for branch in range(3):
            def block(t, carry, branch=branch):
                rows, prows, has_prev = _block_rows(branch, t)
                keys, values, mask = _block_keys(branch, k_s, v_s, rows, prows, has_prev)
                s = jnp.where(mask, BMM_NT(_both_heads(q_s[rows, :]), keys) * scale, NEG)
                m = jnp.max(s, axis=2, keepdims=True)
                e = jnp.exp(s - m)
                l = jnp.sum(e, axis=2, keepdims=True)
                o = BMM(e, values) / l
                lse_b = m + jnp.log(l)
                o_s[branch][rows, :] = jnp.where(heads[0], o[0], o[1])
                l_s[branch][rows, :] = jnp.where(heads[0], lse_b[0], lse_b[1])
                return carry

            lax.fori_loop(0, N_BLK, block, 0, unroll=2)
        l0, l1, l2 = l_s[0][...], l_s[1][...], l_s[2][...]
        m = jnp.maximum(jnp.maximum(l0, l1), l2)
        w0, w1, w2 = jnp.exp(l0 - m), jnp.exp(l1 - m), jnp.exp(l2 - m)
        den = w0 + w1 + w2
        out_ref[...] = (w0 * o_s[0][...] + w1 * o_s[1][...] + w2 * o_s[2][...]) / den
        lse_ref[...] = m + jnp.log(den)

    tab = pl.BlockSpec((SEQ, 128), lambda j: (0, 0))
    col = pl.BlockSpec((SEQ, 128), lambda j: (0, j))
    return _hosted_call(
        body, name="attn_fwd", steps=N_PAIR,
        in_specs=[pl.BlockSpec((SEQ, 384), lambda j: (0, j)), tab, tab],
        out_specs=[col, col],
        out_shape=[jax.ShapeDtypeStruct((SEQ, 2 * ATTN_W), F32), jax.ShapeDtypeStruct((SEQ, ATTN_W), F32)],
        scratch_shapes=[pltpu.VMEM((SEQ, 128), F32)] * 9,
        operands=(proj, cos, sin_signed), exchanges=exchanges)


def _attn_bwd(proj, cos, sin_signed, cat, lse, dcat, dproj, exchanges=()):
    scale = HEAD_DIM ** -0.5

    def body(qkv_ref, cos_ref, sin_ref, out_ref, lse_ref, do_ref, _, dqkv_ref, q_s, k_s, v_s, dq_s, dk_s, dv_s,
             dod_s):
        q_s[...] = _rope(qkv_ref[:, 0:128], cos_ref[...], sin_ref[...])
        k_s[...] = _rope(qkv_ref[:, 128:256], cos_ref[...], sin_ref[...])
        v_s[...] = qkv_ref[:, 256:384]
        dq_s[...] = jnp.zeros_like(dq_s)
        dk_s[...] = jnp.zeros_like(dk_s)
        dv_s[...] = jnp.zeros_like(dv_s)
        dod_s[...] = do_ref[...] * out_ref[...]
        heads = _head_masks()
        for branch in range(3):
            def block(t, carry, branch=branch):
                rows, prows, has_prev = _block_rows(branch, t)
                keys, values, mask = _block_keys(branch, k_s, v_s, rows, prows, has_prev)
                q2, do2 = _both_heads(q_s[rows, :]), _both_heads(do_ref[rows, :])
                lse_b, dod = lse_ref[rows, :], dod_s[rows, :]
                lse2 = jnp.concatenate(
                    [jnp.max(jnp.where(hm, lse_b, NEG), axis=1, keepdims=True)[None] for hm in heads], axis=0)
                delta = jnp.concatenate(
                    [jnp.sum(jnp.where(hm, dod, 0.0), axis=1, keepdims=True)[None] for hm in heads], axis=0)
                p = jnp.exp(jnp.where(mask, BMM_NT(q2, keys) * scale, NEG) - lse2)
                ds = p * (BMM_NT(do2, values) - delta) * scale
                dq = BMM(ds, keys)
                dk = BMM_TN(ds, q2)
                dv = BMM_TN(p, do2)
                dk, dv = dk[0] + dk[1], dv[0] + dv[1]
                dq_s[rows, :] += jnp.where(heads[0], dq[0], dq[1])
                if SEGMENT_BLOCKS[branch] > 1:
                    dk_s[rows, :] += dk[ATTN_BLK:]
                    dv_s[rows, :] += dv[ATTN_BLK:]

                    @pl.when(has_prev)
                    def _():
                        dk_s[prows, :] += dk[:ATTN_BLK]
                        dv_s[prows, :] += dv[:ATTN_BLK]
                else:
                    dk_s[rows, :] += dk
                    dv_s[rows, :] += dv
                return carry

            lax.fori_loop(0, N_BLK, block, 0, unroll=2)
        dqkv_ref[:, 0:128] = _rope(dq_s[...], cos_ref[...], -sin_ref[...]).astype(BF16)
        dqkv_ref[:, 128:256] = _rope(dk_s[...], cos_ref[...], -sin_ref[...]).astype(BF16)
        dqkv_ref[:, 256:384] = dv_s[...].astype(BF16)

    tab = pl.BlockSpec((SEQ, 128), lambda j: (0, 0))
    col = pl.BlockSpec((SEQ, 128), lambda j: (0, j))
    qkv = pl.BlockSpec((SEQ, 384), lambda j: (0, j))
    (dproj,), results = _hosted_call(
        body, name="attn_bwd", steps=N_PAIR,
        in_specs=[qkv, tab, tab, col, col, col, pl.BlockSpec(memory_space=pl.ANY)],
        out_specs=[qkv],
        out_shape=[jax.ShapeDtypeStruct((SEQ, IN_PAD), BF16)],
        scratch_shapes=[pltpu.VMEM((SEQ, 128), F32)] * 7,
        operands=(proj, cos, sin_signed, cat, lse, dcat, dproj), exchanges=exchanges, aliases={6: 0})
    return dproj, results


def _bdot(a, b, dims, precision=None):
    if precision is None:
        a = a.astype(BF16)
        b = b.astype(BF16)
    return lax.dot_general(a, b, (dims, ((0,), (0,))), preferred_element_type=F32, precision=precision)


def _make_bmm(precision):
    @jax.custom_vjp
    def nn(a, b):
        return _bdot(a, b, ((2,), (1,)), precision)

    @jax.custom_vjp
    def nt(a, b):
        return _bdot(a, b, ((2,), (2,)), precision)

    @jax.custom_vjp
    def tn(a, b):
        return _bdot(a, b, ((1,), (1,)), precision)

    nn.defvjp(lambda a, b: (nn(a, b), (a, b)), lambda r, g: (nt(g, r[1]), tn(r[0], g)))
    nt.defvjp(lambda a, b: (nt(a, b), (a, b)), lambda r, g: (nn(g, r[1]), tn(g, r[0])))
    tn.defvjp(lambda a, b: (tn(a, b), (a, b)), lambda r, g: (nt(r[1], g), nn(r[0], g)))
    return nn, nt, tn


BMM, BMM_NT, BMM_TN = _make_bmm(None)
BMM3, BMM3_NT, BMM3_TN = _make_bmm(lax.Precision.HIGH)
MM3, _, _ = _make_mm(lax.Precision.HIGH)


def _head_lanes(t, off):
    lane = lax.broadcasted_iota(jnp.int32, (1, 128), 1)
    return jnp.concatenate(
        [jnp.sum(t * (lane == off + h).astype(F32), axis=1, keepdims=True)[None] for h in range(NDH)], axis=0)


@jax.custom_vjp
def _unit_lower_inverse(a_mat):
    c = a_mat.shape[1]
    eye = (lax.broadcasted_iota(jnp.int32, (c, c), 0) == lax.broadcasted_iota(jnp.int32, (c, c), 1)).astype(F32)
    power = -a_mat
    t_inv = eye + power
    for _ in range(5):
        power = BMM3(power, power)
        t_inv = t_inv + BMM3(t_inv, power)
    return t_inv


def _unit_lower_inverse_fwd(a_mat):
    t_inv = _unit_lower_inverse(a_mat)
    return t_inv, t_inv


def _unit_lower_inverse_bwd(t_inv, d_inv):
    return (-BMM3_NT(BMM3_TN(t_inv, d_inv), t_inv),)


_unit_lower_inverse.defvjp(_unit_lower_inverse_fwd, _unit_lower_inverse_bwd)


def _delta_chunk(qr, kr, vr, z, tail, alog_row, dt_row, nw, state):
    c = qr.shape[1]
    beta = _sigmoid(_head_lanes(tail, 0))
    g = -jnp.exp(_head_lanes(alog_row, 0)) * _softplus(_head_lanes(tail, NDH) + _head_lanes(dt_row, 0))

    q = qr * lax.rsqrt(jnp.sum(qr * qr, axis=2, keepdims=True) + EPS) * (128 ** -0.5)
    k = kr * lax.rsqrt(jnp.sum(kr * kr, axis=2, keepdims=True) + EPS)

    ri = lax.broadcasted_iota(jnp.int32, (c, c), 0)
    ci = lax.broadcasted_iota(jnp.int32, (c, c), 1)
    tril = ri >= ci
    lane = lax.broadcasted_iota(jnp.int32, (1, 128), 1)
    g_lanes = sum(g[h] * (lane == h).astype(F32) for h in range(NDH))
    gc = _head_lanes(MM3(tril.astype(F32), g_lanes), 0)
    g_row = jnp.swapaxes(jnp.broadcast_to(gc, (NDH, c, c)), 1, 2)
    decay = jnp.where(tril, jnp.exp(jnp.where(tril, gc - g_row, 0.0)), 0.0)
    kb = k * beta
    t_inv = _unit_lower_inverse(jnp.where(ri > ci, BMM_NT(kb, k) * decay, 0.0))
    eg = jnp.exp(gc)
    u = BMM(t_inv, vr * beta)
    w = BMM(t_inv, kb * eg)
    qk = BMM_NT(q, k) * decay
    g_tot = jnp.sum(g, axis=1, keepdims=True)
    v_new = u - BMM(w, state)
    o = BMM(q * eg, state) + BMM(qk, v_new)
    new_state = state * jnp.exp(g_tot) + BMM_TN(k * jnp.exp(g_tot - gc), v_new)
    on = o * lax.rsqrt(jnp.mean(o * o, axis=2, keepdims=True) + EPS) * nw
    return on * _silu(z), new_state


def _heads(v, off=0):
    return jnp.concatenate([v[None, :, off + 128 * h:off + 128 * (h + 1)] for h in range(NDH)], axis=0)


def _unheads(t):
    return jnp.concatenate([t[h] for h in range(NDH)], axis=1)


def _delta_fwd(c_qkv, proj, alog_row, dt_row, nw, cat, exchanges=()):
    def body(c_ref, z_ref, tail_ref, al_ref, dt_ref, nw_ref, _, y_ref, st_ref, state):
        @pl.when(pl.program_id(0) == 0)
        def _():
            state[...] = jnp.zeros_like(state)

        cv = c_ref[...]
        st_ref[0] = state[...]
        y, new_state = _delta_chunk(_heads(cv), _heads(cv, 512), _heads(cv, 1024), _heads(z_ref[...]), tail_ref[...],
                                    al_ref[...], dt_ref[...], nw_ref[...], state[...])
        y_ref[...] = _unheads(y)
        state[...] = new_state

    row = pl.BlockSpec((1, 128), lambda n: (0, 0))
    return _hosted_call(
        body, name="delta_fwd", steps=NCH,
        in_specs=[pl.BlockSpec((CH, 1536), lambda n: (n, 0)), pl.BlockSpec((CH, 512), lambda n: (n, DN_Z_COL // 512)),
                  pl.BlockSpec((CH, 128), lambda n: (n, DN_TAIL_BLK)), row, row, row, pl.BlockSpec(memory_space=pl.ANY)],
        out_specs=[pl.BlockSpec((CH, 512), lambda n: (n, 1)),
                   pl.BlockSpec((1, NDH, 128, 128), lambda n: (n, 0, 0, 0))],
        out_shape=[jax.ShapeDtypeStruct((SEQ, 2 * ATTN_W), F32), jax.ShapeDtypeStruct((NCH, NDH, 128, 128), F32)],
        scratch_shapes=[pltpu.VMEM((NDH, 128, 128), F32)],
        operands=(c_qkv, proj, proj, alog_row, dt_row, nw, cat), exchanges=exchanges, aliases={6: 0})


def _delta_bwd(c_qkv, proj, alog_row, dt_row, nw, states, dcat, exchanges=()):
    def body(c_ref, z_ref, tail_ref, al_ref, dt_ref, nw_ref, st_ref, dy_ref,
             dp_ref, dc_ref, dal_ref, ddt_ref, dnw_ref, dstate):
        @pl.when(pl.program_id(0) == 0)
        def _():
            dstate[...] = jnp.zeros_like(dstate)
            dal_ref[...] = jnp.zeros_like(dal_ref)
            ddt_ref[...] = jnp.zeros_like(ddt_ref)
            dnw_ref[...] = jnp.zeros_like(dnw_ref)

        cv = c_ref[...]
        _, vjp = jax.vjp(_delta_chunk, _heads(cv), _heads(cv, 512), _heads(cv, 1024), _heads(z_ref[...]),
                         tail_ref[...], al_ref[...], dt_ref[...], nw_ref[...], st_ref[0])
        dq, dk, dv, dz, dtail, dal, ddt, dnw, dst = vjp((_heads(dy_ref[...]), dstate[...]))
        dstate[...] = dst
        dc_ref[...] = jnp.concatenate([_unheads(dq), _unheads(dk), _unheads(dv)], axis=1)
        dp_ref[...] = jnp.concatenate([_unheads(dz), dtail, jnp.zeros((CH, 128), F32)], axis=1).astype(BF16)
        dal_ref[...] += dal
        ddt_ref[...] += ddt
        dnw_ref[...] += dnw

    rev = lambda n: NCH - 1 - n
    row = pl.BlockSpec((1, 128), lambda n: (0, 0))
    return _hosted_call(
        body, name="delta_bwd", steps=NCH,
        in_specs=[pl.BlockSpec((CH, 1536), lambda n: (rev(n), 0)),
                  pl.BlockSpec((CH, 512), lambda n: (rev(n), DN_Z_COL // 512)),
                  pl.BlockSpec((CH, 128), lambda n: (rev(n), DN_TAIL_BLK)), row, row, row,
                  pl.BlockSpec((1, NDH, 128, 128), lambda n: (rev(n), 0, 0, 0)),
                  pl.BlockSpec((CH, 512), lambda n: (rev(n), 1))],
        out_specs=[pl.BlockSpec((CH, 768), lambda n: (rev(n), DN_Z_COL // 768)),
                   pl.BlockSpec((CH, 1536), lambda n: (rev(n), 0)), row, row, row],
        out_shape=[jax.ShapeDtypeStruct((SEQ, IN_PAD), BF16), jax.ShapeDtypeStruct((SEQ, 1536), F32)]
        + [jax.ShapeDtypeStruct((1, 128), F32)] * 3,
        scratch_shapes=[pltpu.VMEM((NDH, 128, 128), F32)],
        operands=(c_qkv, proj, proj, alog_row, dt_row, nw, states, dcat), exchanges=exchanges)


def _place():
    x, y, c = lax.axis_index("x"), lax.axis_index("y"), lax.axis_index("c")
    other_chips = [(1 - x, y), (x, 1 - y), (1 - x, 1 - y)]
    return x, y, c, other_chips


def _gather_exchange(shards):
    n = len(shards)

    def copies(ins, outs, sems):
        send_sems, recv_sems, local_sems = sems
        x, y, c, chips = _place()
        me, sibling = (x, y, c), (x, y, 1 - c)

        def copy(b, k, block, to, src=None):
            slot = outs[b].at[4 * block[0] + 2 * block[1] + block[2]]
            return pltpu.make_async_remote_copy(
                src_ref=slot if src is None else src, dst_ref=slot,
                send_sem=send_sems.at[b, k], recv_sem=recv_sems.at[b, k], device_id=to, device_id_type=MESH)

        mine = [pltpu.make_async_copy(ins[b], outs[b].at[4 * x + 2 * y + c], local_sems.at[b]) for b in range(n)]
        first = []
        for b in range(n):
            first.append(copy(b, 0, me, sibling, src=ins[b]))
            first += [copy(b, 1 + j, me, (*chip, c), src=ins[b]) for j, chip in enumerate(chips)]
        over_ici = [copy(b, 1 + j, (*chip, c), me) for b in range(n) for j, chip in enumerate(chips)]
        passed = [copy(b, 4 + j, (*chip, c), sibling) for b in range(n) for j, chip in enumerate(chips)]
        from_sibling = []
        for b in range(n):
            from_sibling.append(copy(b, 0, sibling, me))
            from_sibling += [copy(b, 4 + j, (*chip, 1 - c), me) for j, chip in enumerate(chips)]
        return mine, first, over_ici, passed, from_sibling

    def start(ins, outs, sems):
        mine, first, _, _, _ = copies(ins, outs, sems)
        for cp in mine + first:
            cp.start()

    def middle(ins, outs, sems):
        _, _, over_ici, passed, _ = copies(ins, outs, sems)
        for arrived, onward in zip(over_ici, passed):
            arrived.wait_recv()
            onward.start()

    def finish(ins, outs, sems):
        mine, first, _, passed, from_sibling = copies(ins, outs, sems)
        for cp in from_sibling:
            cp.wait_recv()
        for cp in first + passed:
            cp.wait_send()
        for cp in mine:
            cp.wait()

    return Exchange(shards, [jax.ShapeDtypeStruct((N_DEV,) + s.shape, s.dtype) for s in shards],
                    [pltpu.SemaphoreType.DMA((n, 7)), pltpu.SemaphoreType.DMA((n, 7)), pltpu.SemaphoreType.DMA((n,))],
                    start, middle, finish)


def _sibling_exchange(gs):
    n = len(gs)

    def copies(ins, outs, sems):
        send_sems, recv_sems = sems
        x, y, c, _ = _place()
        return [pltpu.make_async_remote_copy(
            src_ref=ins[b].at[2 * p + (1 - c)], dst_ref=outs[b].at[p],
            send_sem=send_sems.at[b, p], recv_sem=recv_sems.at[b, p],
            device_id=(x, y, 1 - c), device_id_type=MESH) for b in range(n) for p in range(4)]

    def start(ins, outs, sems):
        for cp in copies(ins, outs, sems):
            cp.start()

    def finish(ins, outs, sems):
        for cp in copies(ins, outs, sems):
            cp.wait()

    return Exchange(gs, [jax.ShapeDtypeStruct((4,) + g.shape[1:], g.dtype) for g in gs],
                    [pltpu.SemaphoreType.DMA((n, 4)), pltpu.SemaphoreType.DMA((n, 4))], start, None, finish)


def _chips_exchange(hs):
    n = len(hs)

    def copies(ins, outs, sems):
        send_sems, recv_sems, local_sems = sems
        x, y, c, chips = _place()
        my_chip = 2 * x + y
        local = [pltpu.make_async_copy(ins[b].at[my_chip], outs[b].at[my_chip], local_sems.at[b]) for b in range(n)]
        sends, arrivals = [], []
        for b in range(n):
            for k, (px, py) in enumerate(chips):
                peer = 2 * px + py
                sends.append(pltpu.make_async_remote_copy(
                    src_ref=ins[b].at[peer], dst_ref=outs[b].at[my_chip],
                    send_sem=send_sems.at[b, k], recv_sem=recv_sems.at[b, k],
                    device_id=(px, py, c), device_id_type=MESH))
                arrivals.append(pltpu.make_async_remote_copy(
                    src_ref=ins[b].at[peer], dst_ref=outs[b].at[peer],
                    send_sem=send_sems.at[b, k], recv_sem=recv_sems.at[b, k],
                    device_id=(px, py, c), device_id_type=MESH))
        return local, sends, arrivals

    def start(ins, outs, sems):
        local, sends, _ = copies(ins, outs, sems)
        for cp in local + sends:
            cp.start()

    def finish(ins, outs, sems):
        local, sends, arrivals = copies(ins, outs, sems)
        for cp in arrivals:
            cp.wait_recv()
        for cp in sends:
            cp.wait_send()
        for cp in local:
            cp.wait()

    return Exchange(hs, [jax.ShapeDtypeStruct(h.shape, h.dtype) for h in hs],
                    [pltpu.SemaphoreType.DMA((n, 3)), pltpu.SemaphoreType.DMA((n, 3)), pltpu.SemaphoreType.DMA((n,))],
                    start, None, finish)


def _run_exchange(exchange, name):
    n_in, n_out = len(exchange.operands), len(exchange.out_shapes)

    def body(*refs):
        ins, outs, sems = refs[:n_in], refs[n_in:n_in + n_out], refs[n_in + n_out:]
        exchange.start(ins, outs, sems)
        if exchange.middle is not None:
            exchange.middle(ins, outs, sems)
        exchange.finish(ins, outs, sems)

    return pl.pallas_call(
        body, name=name,
        in_specs=[HBM_SPEC] * n_in, out_specs=[HBM_SPEC] * n_out,
        out_shape=exchange.out_shapes, scratch_shapes=exchange.sems,
    )(*exchange.operands)


def _pair_add(g, r, core, name):
    _, nr, nc = g.shape
    tr = nr // 2 if nr % 32 == 0 else nr

    def body(core_ref, g_ref, r_ref, o_ref):
        o_ref[...] = (g_ref[...].astype(F32) + r_ref[...].astype(F32)).astype(BF16)

    return pl.pallas_call(
        body, name=name,
        grid_spec=pltpu.PrefetchScalarGridSpec(
            num_scalar_prefetch=1, grid=(4, nr // tr),
            in_specs=[pl.BlockSpec((1, tr, nc), lambda p, i, core: (2 * p + core[0], i, 0)),
                      pl.BlockSpec((1, tr, nc), lambda p, i, core: (p, i, 0))],
            out_specs=pl.BlockSpec((1, tr, nc), lambda p, i, core: (p, i, 0))),
        out_shape=jax.ShapeDtypeStruct(r.shape, BF16),
        compiler_params=_cp("parallel", "parallel"),
    )(core, g, r)


def _all_gather_sum_small(v):
    rows = v.shape[0]

    def body(x_ref, sum_ref, out_ref, send_sems, recv_sems, local_sem):
        x, y, c, chips = _place()
        me, sibling = (x, y, c), (x, y, 1 - c)

        def block(px, py, pc):
            return out_ref.at[pl.ds((4 * px + 2 * py + pc) * rows, rows), :]

        def copy(k, blk, to, src=None):
            return pltpu.make_async_remote_copy(
                src_ref=block(*blk) if src is None else src, dst_ref=block(*blk),
                send_sem=send_sems.at[k], recv_sem=recv_sems.at[k], device_id=to, device_id_type=MESH)

        mine = pltpu.make_async_copy(x_ref, block(*me), local_sem)
        mine.start()
        first = [copy(0, me, sibling, src=x_ref)]
        first += [copy(1 + j, me, (*chip, c), src=x_ref) for j, chip in enumerate(chips)]
        for cp in first:
            cp.start()
        passed = [copy(4 + j, (*chip, c), sibling) for j, chip in enumerate(chips)]
        for j, chip in enumerate(chips):
            copy(1 + j, (*chip, c), me).wait_recv()
            passed[j].start()
        copy(0, sibling, me).wait_recv()
        for j, chip in enumerate(chips):
            copy(4 + j, (*chip, 1 - c), me).wait_recv()
        for cp in first + passed:
            cp.wait_send()
        mine.wait()
        total = out_ref[pl.ds(0, rows), :]
        for d in range(1, N_DEV):
            total = total + out_ref[pl.ds(d * rows, rows), :]
        sum_ref[...] = total

    vm = pl.BlockSpec(memory_space=pltpu.VMEM)
    return pl.pallas_call(
        body, name="small_all_reduce",
        in_specs=[vm], out_specs=[vm],
        out_shape=[jax.ShapeDtypeStruct((rows, 128), F32)],
        scratch_shapes=[pltpu.VMEM((N_DEV * rows, 128), F32), pltpu.SemaphoreType.DMA((7,)),
                        pltpu.SemaphoreType.DMA((7,)), pltpu.SemaphoreType.DMA],
    )(v)[0]


def _adamw(w, g, m, v):
    m = ADAM_B1 * m + (1.0 - ADAM_B1) * g
    v = ADAM_B2 * v + (1.0 - ADAM_B2) * (g * g)
    m_hat = m / (1.0 - ADAM_B1 ** ADAM_STEP)
    v_hat = v / (1.0 - ADAM_B2 ** ADAM_STEP)
    delta = -ADAM_LR * (m_hat / (jnp.sqrt(v_hat) + ADAM_EPS) + ADAM_WD * w)
    return delta, m, v


ADAM_TILE = dict(w_in=(IN_COLS // N_DEV, 256), w_out=(128, D_MODEL), ffn_w_in=(176, D_MODEL), ffn_w_out=(176, D_MODEL))


def _sum_chips(p):
    p = p.astype(F32)
    return (p[0] + p[1]) + (p[2] + p[3])


def _adamw_sharded(parts, w, m, v, tile, name):
    nl, nr, nc = w.shape
    tr, tc = tile

    def body(*refs):
        p_refs, (w_ref, m_ref, v_ref, g_ref, d_ref, nm_ref, nv_ref) = refs[:nl], refs[nl:]
        layer = pl.program_id(0)
        p = p_refs[0][...]
        for l in range(1, nl):
            p = jnp.where(layer == l, p_refs[l][...], p)
        g = _sum_chips(p)
        delta, nm, nv = _adamw(w_ref[0], g, m_ref[0], v_ref[0])
        g_ref[0] = g
        d_ref[0] = delta
        nm_ref[0] = nm
        nv_ref[0] = nv

    blk = pl.BlockSpec((1, tr, tc), lambda l, i, j: (l, i, j))
    return pl.pallas_call(
        body, name=name, grid=(nl, nr // tr, nc // tc),
        in_specs=[pl.BlockSpec((4, tr, tc), lambda l, i, j: (0, i, j))] * nl + [blk, blk, blk],
        out_specs=[blk] * 4,
        out_shape=[jax.ShapeDtypeStruct(w.shape, F32)] * 4,
        compiler_params=_cp("parallel", "parallel", "parallel"),
    )(*parts, w, m, v)


def _adamw_small(g, w, m, v):
    def body(g_ref, w_ref, m_ref, v_ref, d_ref, nm_ref, nv_ref):
        delta, nm, nv = _adamw(w_ref[...], g_ref[...], m_ref[...], v_ref[...])
        d_ref[...] = delta
        nm_ref[...] = nm
        nv_ref[...] = nv

    return pl.pallas_call(
        body, name="adamw_small",
        out_shape=[jax.ShapeDtypeStruct(g.shape, F32)] * 3,
    )(g, w, m, v)


def _pack(arrays, rows):
    flat = jnp.concatenate([a.reshape(-1).astype(F32) for a in arrays])
    return jnp.pad(flat, (0, rows * 128 - flat.shape[0])).reshape(rows, 128)


def _unpack(packed, shapes):
    flat = packed.reshape(-1)
    out, off = [], 0
    for s in shapes:
        n = math.prod(s)
        out.append(flat[off:off + n].reshape(s))
        off += n
    return out


def _row(v, width=None):
    v = v.reshape(1, -1)
    return v if width is None else jnp.pad(v, ((0, 0), (0, width - v.shape[1])))


def _layer_fwd(x, wts, tables, attn_exchanges=(), delta_exchanges=(), on_attn=None, on_delta=None):
    h = _norm_fwd(x, wts["norm_pre_mix"], "norm_pre_mix")
    proj = _matmul(h, wts["w_in"], tb=True, tm=SEQ, tn=768, tk=1024, name="mm_proj")
    (cat, lse), got = _attn_fwd(proj, *tables, exchanges=attn_exchanges)
    if on_attn is not None:
        on_attn(got)
    c_qkv = _dnconv_fwd(proj, wts["dn_conv_w"])
    (cat, states), got = _delta_fwd(c_qkv, proj, wts["dn_a_log"], wts["dn_dt_bias"], wts["dn_norm_w"], cat,
                                    exchanges=delta_exchanges)
    if on_delta is not None:
        on_delta(got)
    mix = _matmul(cat, wts["w_out"], tm=512, tn=1024, tk=1024, name="mm_mix")
    x1 = _resnorm_fwd(x, mix, wts["norm_post_mix"], "norm_post_mix")
    h2 = _norm_fwd(x1, wts["norm_pre_ffn"], "norm_pre_ffn")
    pre = _matmul(h2, wts["ffn_w_in"], tb=True, tm=SEQ, tn=512, tk=1024, name="mm_ffn_in")
    act = _ffact_fwd(pre, wts["ffn_conv_w"], wts["ffn_conv_b"])
    f = _matmul(act, wts["ffn_w_out"], tm=512, tn=1024, tk=D_FF, name="mm_ffn_out")
    x2 = _resnorm_fwd(x1, f, wts["norm_post_ffn"], "norm_post_ffn")
    saved = dict(x=x, h=h, proj=proj, lse=lse, c_qkv=c_qkv, states=states, cat=cat, mix=mix, x1=x1, h2=h2, pre=pre,
                 act=act, f=f)
    return x2, saved


def _layer_bwd(dx2, wts, s, tables, ffact_exchanges=(), delta_exchanges=None, attn_exchanges=None):
    g = {}
    df, g["norm_post_ffn"] = _norm_bwd(s["f"], wts["norm_post_ffn"], dx2, None, "norm_post_ffn_bwd", BF16)
    dact = _matmul(df, wts["ffn_w_out"], tb=True, tm=SEQ, tn=1408, tk=1024, name="mm_dact", out_dtype=BF16)
    g["ffn_w_out"] = _matmul(s["act"], df, ta=True, tm=1408, tn=512, tk=SEQ, name="mm_dw_ffn_out", out_dtype=BF16)
    (dpre, g["ffn_conv_w"], g["ffn_conv_b"]), got = _ffact_bwd(s["pre"], wts["ffn_conv_w"], wts["ffn_conv_b"], dact,
                                                               exchanges=ffact_exchanges)
    dh2 = _matmul(dpre, wts["ffn_w_in"], tm=1024, tn=1024, tk=1408, name="mm_dh2")
    g["ffn_w_in"] = _matmul(dpre, s["h2"], ta=True, tm=512, tn=1024, tk=SEQ, name="mm_dw_ffn_in", out_dtype=BF16)
    dx1, g["norm_pre_ffn"] = _norm_bwd(s["x1"], wts["norm_pre_ffn"], dh2, dx2, "norm_pre_ffn_bwd")
    dmix, g["norm_post_mix"] = _norm_bwd(s["mix"], wts["norm_post_mix"], dx1, None, "norm_post_mix_bwd", BF16)
    dcat = _matmul(dmix, wts["w_out"], tb=True, tm=SEQ, tn=512, tk=1024, name="mm_dcat")
    g["w_out"] = _matmul(s["cat"], dmix, ta=True, tm=1024, tn=512, tk=SEQ, name="mm_dw_out", out_dtype=BF16)
    (dproj, dc, g["dn_a_log"], g["dn_dt_bias"], g["dn_norm_w"]), got = _delta_bwd(
        s["c_qkv"], s["proj"], wts["dn_a_log"], wts["dn_dt_bias"], wts["dn_norm_w"], s["states"], dcat,
        exchanges=delta_exchanges(g, got) if delta_exchanges is not None else ())
    dproj, got = _attn_bwd(s["proj"], *tables, s["cat"], s["lse"], dcat, dproj,
                           exchanges=attn_exchanges(got) if attn_exchanges is not None else ())
    dproj, g["dn_conv_w"] = _dnconv_bwd(s["proj"], wts["dn_conv_w"], dc, dproj)
    dh = _matmul(dproj, wts["w_in"], tm=1024, tn=1024, tk=1280, name="mm_dh")
    g["w_in"] = _matmul(dproj, s["h"], ta=True, tm=768, tn=1024, tk=SEQ, name="mm_dw_in", out_dtype=BF16)
    dx, g["norm_pre_mix"] = _norm_bwd(s["x"], wts["norm_pre_mix"], dh, dx1, "norm_pre_mix_bwd")
    return dx, g, got


BIG = ("w_in", "w_out", "ffn_w_in", "ffn_w_out")
COLUMN_SHARDED = ("w_in", "ffn_w_in")
SMALL_SHARDED = ("dn_conv_w", "ffn_conv_w")
REPLICATED = ("dn_a_log", "dn_dt_bias", "dn_norm_w", "ffn_conv_b", "norm_pre_mix", "norm_post_mix", "norm_pre_ffn",
              "norm_post_ffn")
WEIGHTS = ("w_in", "dn_conv_w", "dn_a_log", "dn_dt_bias", "dn_norm_w", "w_out", "ffn_w_in", "ffn_conv_w", "ffn_conv_b",
           "ffn_w_out", "norm_pre_mix", "norm_post_mix", "norm_pre_ffn", "norm_post_ffn")
FULL_SHAPE = dict(dn_conv_w=(DEPTH, 4, 1536), ffn_conv_w=(DEPTH, 3, 2 * D_FF), dn_a_log=(DEPTH, NDH),
                  dn_dt_bias=(DEPTH, NDH), dn_norm_w=(DEPTH, 128), ffn_conv_b=(DEPTH, 2 * D_FF),
                  norm_pre_mix=(DEPTH, D_MODEL), norm_post_mix=(DEPTH, D_MODEL), norm_pre_ffn=(DEPTH, D_MODEL),
                  norm_post_ffn=(DEPTH, D_MODEL))
SMALL_GRAD_ORDER = REPLICATED + SMALL_SHARDED
SMALL_GRAD_ROWS = 520
SMALL_W_ROWS = 48
SMALL_ADAM_ROWS = 200


def _w_in_rows_to_kernel_order(t):
    qkv = t[:QKV_W].reshape(3, N_PAIR, 128, -1).swapaxes(0, 1).reshape(QKV_W, -1)
    return jnp.pad(jnp.concatenate([qkv, t[QKV_W:]], axis=0), ((0, IN_PAD - IN_COLS), (0, 0)))


def _w_in_rows_from_kernel_order(t):
    qkv = t[:QKV_W].reshape(N_PAIR, 3, 128, -1).swapaxes(0, 1).reshape(QKV_W, -1)
    return jnp.concatenate([qkv, t[QKV_W:IN_COLS]], axis=0)


def _interleave_ff_rows(t):
    return t.reshape(2, FF_BLKS, 128, -1).swapaxes(0, 1).reshape(2 * D_FF, -1)


def _deinterleave_ff_rows(t):
    return t.reshape(FF_BLKS, 2, 128, -1).swapaxes(0, 1).reshape(2 * D_FF, -1)


def kernel(x, w_in, dn_conv_w, dn_a_log, dn_dt_bias, dn_norm_w, w_out, ffn_w_in, ffn_conv_w, ffn_conv_b, ffn_w_out, norm_pre_mix, norm_post_mix, norm_pre_ffn, norm_post_ffn, loss_target, m_w_in, m_dn_conv_w, m_dn_a_log, m_dn_dt_bias, m_dn_norm_w, m_w_out, m_ffn_w_in, m_ffn_conv_w, m_ffn_conv_b, m_ffn_w_out, m_norm_pre_mix, m_norm_post_mix, m_norm_pre_ffn, m_norm_post_ffn, v_w_in, v_dn_conv_w, v_dn_a_log, v_dn_dt_bias, v_dn_norm_w, v_w_out, v_ffn_w_in, v_ffn_conv_w, v_ffn_conv_b, v_ffn_w_out, v_norm_pre_mix, v_norm_post_mix, v_norm_pre_ffn, v_norm_post_ffn):
    local = dict(w_in=w_in, dn_conv_w=dn_conv_w, dn_a_log=dn_a_log, dn_dt_bias=dn_dt_bias, dn_norm_w=dn_norm_w,
                 w_out=w_out, ffn_w_in=ffn_w_in, ffn_conv_w=ffn_conv_w, ffn_conv_b=ffn_conv_b, ffn_w_out=ffn_w_out,
                 norm_pre_mix=norm_pre_mix, norm_post_mix=norm_post_mix, norm_pre_ffn=norm_pre_ffn,
                 norm_post_ffn=norm_post_ffn)
    mom_m = dict(w_in=m_w_in, dn_conv_w=m_dn_conv_w, dn_a_log=m_dn_a_log, dn_dt_bias=m_dn_dt_bias,
                 dn_norm_w=m_dn_norm_w, w_out=m_w_out, ffn_w_in=m_ffn_w_in, ffn_conv_w=m_ffn_conv_w,
                 ffn_conv_b=m_ffn_conv_b, ffn_w_out=m_ffn_w_out, norm_pre_mix=m_norm_pre_mix,
                 norm_post_mix=m_norm_post_mix, norm_pre_ffn=m_norm_pre_ffn, norm_post_ffn=m_norm_post_ffn)
    mom_v = dict(w_in=v_w_in, dn_conv_w=v_dn_conv_w, dn_a_log=v_dn_a_log, dn_dt_bias=v_dn_dt_bias,
                 dn_norm_w=v_dn_norm_w, w_out=v_w_out, ffn_w_in=v_ffn_w_in, ffn_conv_w=v_ffn_conv_w,
                 ffn_conv_b=v_ffn_conv_b, ffn_w_out=v_ffn_w_out, norm_pre_mix=v_norm_pre_mix,
                 norm_post_mix=v_norm_post_mix, norm_pre_ffn=v_norm_pre_ffn, norm_post_ffn=v_norm_post_ffn)
    dev = 4 * lax.axis_index("x") + 2 * lax.axis_index("y") + lax.axis_index("c")
    core = lax.axis_index("c").astype(jnp.int32).reshape(1)

    def shard(n, l):
        s = local[n].transpose(0, 2, 1) if n in COLUMN_SHARDED else local[n]
        return s[l].astype(BF16)

    def matrix(n, gathered):
        if n == "w_in":
            return _w_in_rows_to_kernel_order(gathered.reshape(IN_COLS, D_MODEL))
        if n == "ffn_w_in":
            return _interleave_ff_rows(gathered.reshape(2 * D_FF, D_MODEL))
        return gathered.reshape(-1, D_MODEL)

    small_w = _pack([dn_conv_w, ffn_conv_w], SMALL_W_ROWS)
    g_w_in0, g_small = _run_exchange(_gather_exchange([shard("w_in", 0), small_w]), "weights_all_gather")
    n_dn, n_ff = DEPTH * 4 * 192, DEPTH * 3 * 704
    sm = g_small.reshape(N_DEV, -1)
    full_dn_conv = sm[:, :n_dn].reshape(N_DEV, DEPTH, 4, 192).transpose(1, 2, 0, 3).reshape(DEPTH, 4, 1536)
    full_ff_conv = _interleave_ff(
        sm[:, n_dn:n_dn + n_ff].reshape(N_DEV, DEPTH, 3, 704).transpose(1, 2, 0, 3).reshape(DEPTH, 3, 2 * D_FF))

    def small_weights(l):
        wts = dict(dn_conv_w=full_dn_conv[l], ffn_conv_w=full_ff_conv[l], ffn_conv_b=_interleave_ff(_row(ffn_conv_b[l])),
                   dn_a_log=_row(dn_a_log[l], 128), dn_dt_bias=_row(dn_dt_bias[l], 128))
        for n in ("dn_norm_w", "norm_pre_mix", "norm_post_mix", "norm_pre_ffn", "norm_post_ffn"):
            wts[n] = _row(local[n][l])
        return wts

    weights = [small_weights(l) for l in range(DEPTH)]
    weights[0]["w_in"] = matrix("w_in", g_w_in0)

    def gather_behind(wanted):
        def deliver(got):
            for (n, l), g in zip(wanted, got[0]):
                weights[l][n] = matrix(n, g)

        return [_gather_exchange([shard(n, l) for n, l in wanted])], deliver

    tables = _rope_tables()
    ex_attn0, on_attn0 = gather_behind([("w_out", 0), ("ffn_w_in", 0)])
    ex_delta0, on_delta0 = gather_behind([("ffn_w_out", 0), ("w_in", 1)])
    ex_attn1, on_attn1 = gather_behind([("w_out", 1), ("ffn_w_in", 1)])
    ex_delta1, on_delta1 = gather_behind([("ffn_w_out", 1)])
    act, saved0 = _layer_fwd(x[0], weights[0], tables, ex_attn0, ex_delta0, on_attn0, on_delta0)
    act, saved1 = _layer_fwd(act, weights[1], tables, ex_attn1, ex_delta1, on_attn1, on_delta1)
    loss_part, dact = _loss_fwd_bwd(act, loss_target[0])

    def to_devices(name, t):
        if name == "w_in":
            t = _w_in_rows_from_kernel_order(t)
        if name == "ffn_w_in":
            t = _deinterleave_ff_rows(t)
        return t.reshape(N_DEV, t.shape[0] // N_DEV, t.shape[1])

    def pair_sums(names, layer, to_dev, from_sibling):
        return [_pair_add(gd, r, core, "grads_pair_add_%s_%d" % (n, layer))
                for n, gd, r in zip(names, to_dev, from_sibling)]

    grads = [None] * DEPTH
    dact, grads[1], _ = _layer_bwd(dact, weights[1], saved1, tables)
    to_dev1 = [to_devices(n, grads[1][n]) for n in BIG]
    early = ("w_out", "ffn_w_in", "ffn_w_out")
    parts, stash = {}, {}

    def delta_exchanges(g, got_ffact):
        stash["to_dev0"] = [to_devices(n, g[n]) for n in early]
        return [_chips_exchange(pair_sums(BIG, 1, to_dev1, got_ffact[0])), _sibling_exchange(stash["to_dev0"])]

    def attn_exchanges(got_delta):
        for n, p in zip(BIG, got_delta[0]):
            parts[n, 1] = p
        return [_chips_exchange(pair_sums(early, 0, stash["to_dev0"], got_delta[1]))]

    dact, grads[0], got_attn = _layer_bwd(dact, weights[0], saved0, tables, [_sibling_exchange(to_dev1)],
                                          delta_exchanges, attn_exchanges)
    for n, p in zip(early, got_attn[0]):
        parts[n, 0] = p
    grad_x = dact[None]
    last = [to_devices("w_in", grads[0]["w_in"])]
    from_sibling = _run_exchange(_sibling_exchange(last), "grads_to_sibling")
    parts["w_in", 0], = _run_exchange(_chips_exchange(pair_sums(("w_in",), 0, last, from_sibling)), "grads_to_chips")

    def small_grad(name):
        t = jnp.stack([grads[l][name] for l in range(DEPTH)])
        if name in ("dn_a_log", "dn_dt_bias"):
            t = t[:, 0, :NDH]
        if name in ("ffn_conv_w", "ffn_conv_b"):
            t = _deinterleave_ff(t)
        return t.reshape(FULL_SHAPE[name])

    small_part = _pack([small_grad(n) for n in SMALL_GRAD_ORDER] + [loss_part[0, :1]], SMALL_GRAD_ROWS)
    small_sum = _all_gather_sum_small(small_part)
    small_g = dict(zip(SMALL_GRAD_ORDER + ("loss",), _unpack(small_sum, [FULL_SHAPE[n] for n in SMALL_GRAD_ORDER] + [(1,)])))
    loss = small_g["loss"][0]
    small_g["dn_conv_w"] = lax.dynamic_slice_in_dim(small_g["dn_conv_w"], dev * 192, 192, axis=2)
    small_g["ffn_conv_w"] = lax.dynamic_slice_in_dim(small_g["ffn_conv_w"], dev * 704, 704, axis=2)

    out_g, out_d, out_m, out_v = {}, {}, {}, {}
    for n in BIG:
        turn = (lambda t: t.transpose(0, 2, 1)) if n in COLUMN_SHARDED else (lambda t: t)
        outs = _adamw_sharded([parts[n, l] for l in range(DEPTH)], turn(local[n]), turn(mom_m[n]), turn(mom_v[n]),
                              ADAM_TILE[n], "adamw_" + n)
        out_g[n], out_d[n], out_m[n], out_v[n] = [turn(t) for t in outs]
    shapes = [small_g[n].shape for n in SMALL_GRAD_ORDER]
    d_s, m_s, v_s = _adamw_small(_pack([small_g[n] for n in SMALL_GRAD_ORDER], SMALL_ADAM_ROWS),
                                 _pack([local[n] for n in SMALL_GRAD_ORDER], SMALL_ADAM_ROWS),
                                 _pack([mom_m[n] for n in SMALL_GRAD_ORDER], SMALL_ADAM_ROWS),
                                 _pack([mom_v[n] for n in SMALL_GRAD_ORDER], SMALL_ADAM_ROWS))
    for n, d, m, v in zip(SMALL_GRAD_ORDER, _unpack(d_s, shapes), _unpack(m_s, shapes), _unpack(v_s, shapes)):
        out_g[n], out_d[n], out_m[n], out_v[n] = small_g[n], d, m, v
    return (loss, grad_x, *[out_g[n] for n in WEIGHTS], *[out_d[n] for n in WEIGHTS],
            *[out_m[n] for n in WEIGHTS], *[out_v[n] for n in WEIGHTS])
```

```python
import functools
import math

import jax
import jax.numpy as jnp
from jax import lax
from jax.experimental import pallas as pl
from jax.experimental.pallas import tpu as pltpu

F32 = jnp.float32
BF16 = jnp.bfloat16
HI = lax.Precision.HIGHEST
MESH = pl.DeviceIdType.MESH

N_DEV = 8
SEQ = 2048
D_MODEL = 1024
DEPTH = 2
N_PAIR = 4
HEAD_DIM = 64
ATTN_W = 512
ATTN_BLK = 128
DILATIONS = (1, 4, 16)
SEGMENT_BLOCKS = (16, 4, 1)
N_BLK = SEQ // ATTN_BLK
NDH = 4
CH = 64
NCH = SEQ // CH
IN_COLS = 3592
IN_PAD = 3840
QKV_W = 3 * ATTN_W
DN_QKV_BLK0 = QKV_W // 128
DN_QKV_BLKS = 1536 // 128
DN_Z_COL = 3072
DN_TAIL_BLK = 3584 // 128
D_FF = 2816
FF_BLKS = D_FF // 128
EPS = 1e-6
NEG = -1e30
ROPE_THETA = 10000.0

ADAM_LR, ADAM_B1, ADAM_B2, ADAM_EPS, ADAM_WD, ADAM_STEP = 0.001, 0.9, 0.999, 1e-08, 0.01, 10

VMEM_LIMIT = 56 * 1024 * 1024


def _cp(*sem):
    return pltpu.CompilerParams(dimension_semantics=sem, vmem_limit_bytes=VMEM_LIMIT)


class Exchange:
    def __init__(self, operands, out_shapes, sems, start, middle, finish):
        self.operands, self.out_shapes, self.sems = list(operands), list(out_shapes), list(sems)
        self.start, self.middle, self.finish = start, middle, finish


HBM_SPEC = pl.BlockSpec(memory_space=pltpu.HBM)


def _hosted_call(body, *, name, steps, in_specs, out_specs, out_shape, scratch_shapes, operands, exchanges=(),
                 aliases=None):
    n_in, n_out, n_scr = len(in_specs), len(out_specs), len(scratch_shapes)

    def take(refs, pos, counts):
        groups = []
        for c in counts:
            groups.append(refs[pos:pos + c])
            pos += c
        return groups, pos

    def full_body(*refs):
        ins, pos = refs[:n_in], n_in
        ex_ins, pos = take(refs, pos, [len(e.operands) for e in exchanges])
        outs, pos = refs[pos:pos + n_out], pos + n_out
        ex_outs, pos = take(refs, pos, [len(e.out_shapes) for e in exchanges])
        scr, pos = refs[pos:pos + n_scr], pos + n_scr
        ex_sems, pos = take(refs, pos, [len(e.sems) for e in exchanges])
        step = pl.program_id(0)
        for e, a, b, s in zip(exchanges, ex_ins, ex_outs, ex_sems):
            pl.when(step == 0)(functools.partial(e.start, a, b, s))
            if e.middle is not None:
                pl.when(step == steps // 2)(functools.partial(e.middle, a, b, s))
        body(*ins, *outs, *scr)
        for e, a, b, s in zip(exchanges, ex_ins, ex_outs, ex_sems):
            pl.when(step == steps - 1)(functools.partial(e.finish, a, b, s))

    n_ex_in = sum(len(e.operands) for e in exchanges)
    n_ex_out = sum(len(e.out_shapes) for e in exchanges)
    results = pl.pallas_call(
        full_body, name=name, grid=(steps,),
        in_specs=list(in_specs) + [HBM_SPEC] * n_ex_in,
        out_specs=list(out_specs) + [HBM_SPEC] * n_ex_out,
        out_shape=list(out_shape) + [s for e in exchanges for s in e.out_shapes],
        scratch_shapes=list(scratch_shapes) + [s for e in exchanges for s in e.sems],
        input_output_aliases=aliases or {},
        compiler_params=_cp("arbitrary"),
    )(*operands, *[a for e in exchanges for a in e.operands])
    ex_results, _ = take(results, n_out, [len(e.out_shapes) for e in exchanges])
    return results[:n_out], ex_results


def _dot(a, b, dims, precision=None):
    if precision is None:
        a = a.astype(BF16)
        b = b.astype(BF16)
    return lax.dot_general(a, b, (dims, ((), ())), preferred_element_type=F32, precision=precision)


def _make_mm(precision):
    @jax.custom_vjp
    def nn(a, b):
        return _dot(a, b, ((1,), (0,)), precision)

    @jax.custom_vjp
    def nt(a, b):
        return _dot(a, b, ((1,), (1,)), precision)

    @jax.custom_vjp
    def tn(a, b):
        return _dot(a, b, ((0,), (0,)), precision)

    nn.defvjp(lambda a, b: (nn(a, b), (a, b)), lambda r, g: (nt(g, r[1]), tn(r[0], g)))
    nt.defvjp(lambda a, b: (nt(a, b), (a, b)), lambda r, g: (nn(g, r[1]), tn(g, r[0])))
    tn.defvjp(lambda a, b: (tn(a, b), (a, b)), lambda r, g: (nt(r[1], g), nn(r[0], g)))
    return nn, nt, tn


MM, MM_NT, MM_TN = _make_mm(None)


def _matmul(a, b, *, ta=False, tb=False, tm, tn, tk, name, out_dtype=F32):
    (k_dim, m_dim) = a.shape if ta else a.shape[::-1]
    (n_dim, k2) = b.shape if tb else b.shape[::-1]
    assert k_dim == k2 and m_dim % tm == 0 and n_dim % tn == 0 and k_dim % tk == 0, (a.shape, b.shape, tm, tn, tk)
    nk = k_dim // tk
    dims = ((0 if ta else 1,), (1 if tb else 0,))

    def body(a_ref, b_ref, o_ref, *acc):
        p = _dot(a_ref[...], b_ref[...], dims)
        if nk == 1:
            o_ref[...] = p.astype(out_dtype)
            return
        acc_ref, k = acc[0], pl.program_id(2)

        @pl.when(k == 0)
        def _():
            acc_ref[...] = p

        @pl.when(k > 0)
        def _():
            acc_ref[...] += p

        @pl.when(k == nk - 1)
        def _():
            o_ref[...] = acc_ref[...].astype(out_dtype)

    a_spec = pl.BlockSpec((tk, tm), lambda i, j, k: (k, i)) if ta else pl.BlockSpec((tm, tk), lambda i, j, k: (i, k))
    b_spec = pl.BlockSpec((tn, tk), lambda i, j, k: (j, k)) if tb else pl.BlockSpec((tk, tn), lambda i, j, k: (k, j))
    return pl.pallas_call(
        body, name=name,
        grid=(m_dim // tm, n_dim // tn, nk),
        in_specs=[a_spec, b_spec],
        out_specs=pl.BlockSpec((tm, tn), lambda i, j, k: (i, j)),
        out_shape=jax.ShapeDtypeStruct((m_dim, n_dim), out_dtype),
        scratch_shapes=[pltpu.VMEM((tm, tn), F32)] if nk > 1 else [],
        compiler_params=_cp("parallel", "parallel", "arbitrary"),
    )(a, b)


NORM_ROWS = 256


def _rms(x, w):
    return x * lax.rsqrt(jnp.mean(x * x, axis=1, keepdims=True) + EPS) * w


def _norm_fwd(x, w_row, name, out_dtype=BF16):
    def body(x_ref, w_ref, o_ref):
        o_ref[...] = _rms(x_ref[...], w_ref[...]).astype(out_dtype)

    return pl.pallas_call(
        body, name=name, grid=(SEQ // NORM_ROWS,),
        in_specs=[pl.BlockSpec((NORM_ROWS, D_MODEL), lambda i: (i, 0)), pl.BlockSpec((1, D_MODEL), lambda i: (0, 0))],
        out_specs=pl.BlockSpec((NORM_ROWS, D_MODEL), lambda i: (i, 0)),
        out_shape=jax.ShapeDtypeStruct((SEQ, D_MODEL), out_dtype),
        compiler_params=_cp("parallel"),
    )(x, w_row)


def _resnorm_fwd(x, f, w_row, name):
    def body(x_ref, f_ref, w_ref, o_ref):
        o_ref[...] = x_ref[...] + _rms(f_ref[...], w_ref[...])

    blk = pl.BlockSpec((NORM_ROWS, D_MODEL), lambda i: (i, 0))
    return pl.pallas_call(
        body, name=name, grid=(SEQ // NORM_ROWS,),
        in_specs=[blk, blk, pl.BlockSpec((1, D_MODEL), lambda i: (0, 0))],
        out_specs=blk, out_shape=jax.ShapeDtypeStruct((SEQ, D_MODEL), F32),
        compiler_params=_cp("parallel"),
    )(x, f, w_row)


def _norm_bwd(x, w_row, dy, add, name, dx_dtype=F32):
    has_add = add is not None

    def body(*refs):
        if has_add:
            x_ref, w_ref, dy_ref, add_ref, dx_ref, dw_ref = refs
        else:
            x_ref, w_ref, dy_ref, dx_ref, dw_ref = refs
        _, vjp = jax.vjp(_rms, x_ref[...], w_ref[...])
        dx, dw = vjp(dy_ref[...])
        dx_ref[...] = (dx + add_ref[...] if has_add else dx).astype(dx_dtype)

        @pl.when(pl.program_id(0) == 0)
        def _():
            dw_ref[...] = jnp.zeros_like(dw_ref)

        dw_ref[...] += dw

    blk = pl.BlockSpec((NORM_ROWS, D_MODEL), lambda i: (i, 0))
    row = pl.BlockSpec((1, D_MODEL), lambda i: (0, 0))
    ins = [x, w_row, dy] + ([add] if has_add else [])
    return pl.pallas_call(
        body, name=name, grid=(SEQ // NORM_ROWS,),
        in_specs=[blk, row, blk] + ([blk] if has_add else []),
        out_specs=[blk, row],
        out_shape=[jax.ShapeDtypeStruct((SEQ, D_MODEL), dx_dtype), jax.ShapeDtypeStruct((1, D_MODEL), F32)],
        compiler_params=_cp("arbitrary"),
    )(*ins)


def _loss_fwd_bwd(y, target):
    def body(y_ref, t_ref, loss_ref, dy_ref):
        err = y_ref[...] - t_ref[...]
        dy_ref[...] = err * (1.0 / D_MODEL)

        @pl.when(pl.program_id(0) == 0)
        def _():
            loss_ref[...] = jnp.zeros_like(loss_ref)

        part = jnp.sum(jnp.sum(err * err, axis=1, keepdims=True) * (1.0 / D_MODEL), axis=0, keepdims=True)
        loss_ref[...] += 0.5 * jnp.broadcast_to(part, loss_ref.shape)

    blk = pl.BlockSpec((NORM_ROWS, D_MODEL), lambda i: (i, 0))
    return pl.pallas_call(
        body, name="loss", grid=(SEQ // NORM_ROWS,),
        in_specs=[blk, blk],
        out_specs=[pl.BlockSpec((1, 128), lambda i: (0, 0)), blk],
        out_shape=[jax.ShapeDtypeStruct((1, 128), F32), jax.ShapeDtypeStruct((SEQ, D_MODEL), F32)],
        compiler_params=_cp("arbitrary"),
    )(y, target)


def _make_shift(j):
    def down(x):
        row = lax.broadcasted_iota(jnp.int32, x.shape, 0)
        return jnp.where(row >= j, pltpu.roll(x, j, 0), 0.0)

    def up(x):
        n = x.shape[0]
        row = lax.broadcasted_iota(jnp.int32, x.shape, 0)
        return jnp.where(row < n - j, pltpu.roll(x, n - j, 0), 0.0)

    f = jax.custom_vjp(down)
    f.defvjp(lambda x: (down(x), None), lambda _, g: (up(g),))
    return f


_SHIFT = {j: _make_shift(j) for j in (1, 2, 3)}


def _causal_conv(x, taps):
    n = len(taps)
    acc = x * taps[n - 1]
    for k in range(n - 1):
        acc = acc + _SHIFT[n - 1 - k](x) * taps[k]
    return acc


def _tap_rows(w_ref, lanes=slice(None)):
    return tuple(w_ref[k:k + 1, lanes] for k in range(w_ref.shape[0]))


def _sigmoid(x):
    return 1.0 / (1.0 + jnp.exp(-x))


def _silu(x):
    return x * _sigmoid(x)


def _softplus(x):
    return jnp.maximum(x, 0.0) + jnp.log(1.0 + jnp.exp(-jnp.abs(x)))


def _gelu_tanh(x):
    return 0.5 * x * (1.0 + jnp.tanh(math.sqrt(2.0 / math.pi) * (x + 0.044715 * (x * x * x))))


def _dnconv_fn(x, taps):
    return _silu(_causal_conv(x, taps))


def _dnconv_fwd(proj, conv_w):
    def body(x_ref, w_ref, o_ref):
        o_ref[...] = _dnconv_fn(x_ref[...], _tap_rows(w_ref))

    return pl.pallas_call(
        body, name="dnconv_fwd", grid=(DN_QKV_BLKS,),
        in_specs=[pl.BlockSpec((SEQ, 128), lambda j: (0, DN_QKV_BLK0 + j)), pl.BlockSpec((4, 128), lambda j: (0, j))],
        out_specs=pl.BlockSpec((SEQ, 128), lambda j: (0, j)),
        out_shape=jax.ShapeDtypeStruct((SEQ, 1536), F32),
        compiler_params=_cp("parallel"),
    )(proj, conv_w)


def _dnconv_bwd(proj, conv_w, dc, dproj):
    def body(x_ref, w_ref, dc_ref, _, dx_ref, dw_ref):
        _, vjp = jax.vjp(_dnconv_fn, x_ref[...], _tap_rows(w_ref))
        dx, dw = vjp(dc_ref[...])
        dx_ref[...] = dx.astype(BF16)
        for k, row in enumerate(dw):
            dw_ref[k:k + 1, :] = row

    return pl.pallas_call(
        body, name="dnconv_bwd", grid=(DN_QKV_BLKS,),
        in_specs=[pl.BlockSpec((SEQ, 128), lambda j: (0, DN_QKV_BLK0 + j)), pl.BlockSpec((4, 128), lambda j: (0, j)),
                  pl.BlockSpec((SEQ, 128), lambda j: (0, j)), pl.BlockSpec(memory_space=pl.ANY)],
        out_specs=[pl.BlockSpec((SEQ, 128), lambda j: (0, DN_QKV_BLK0 + j)), pl.BlockSpec((4, 128), lambda j: (0, j))],
        out_shape=[jax.ShapeDtypeStruct((SEQ, IN_PAD), BF16), jax.ShapeDtypeStruct((4, 1536), F32)],
        input_output_aliases={3: 0},
        compiler_params=_cp("parallel"),
    )(proj, conv_w, dc, dproj)


def _ffact_fn(pg, pu, wg, wu, bg, bu):
    return _gelu_tanh(_causal_conv(pg, wg) + bg) * (_causal_conv(pu, wu) + bu)


def _ffact_args(p_ref, w_ref, b_ref):
    g, u = slice(0, 128), slice(128, 256)
    return (p_ref[:, g].astype(F32), p_ref[:, u].astype(F32), _tap_rows(w_ref, g), _tap_rows(w_ref, u),
            b_ref[:, g], b_ref[:, u])


def _ffact_fwd(pre, conv_w, conv_b):
    def body(p_ref, w_ref, b_ref, o_ref):
        o_ref[...] = _ffact_fn(*_ffact_args(p_ref, w_ref, b_ref)).astype(BF16)

    return pl.pallas_call(
        body, name="ffact_fwd", grid=(FF_BLKS,),
        in_specs=[pl.BlockSpec((SEQ, 256), lambda j: (0, j)), pl.BlockSpec((3, 256), lambda j: (0, j)),
                  pl.BlockSpec((1, 256), lambda j: (0, j))],
        out_specs=pl.BlockSpec((SEQ, 128), lambda j: (0, j)),
        out_shape=jax.ShapeDtypeStruct((SEQ, D_FF), BF16),
        compiler_params=_cp("parallel"),
    )(pre, conv_w, conv_b)


def _ffact_bwd(pre, conv_w, conv_b, dact, exchanges=()):
    def body(p_ref, w_ref, b_ref, da_ref, dp_ref, dw_ref, db_ref):
        _, vjp = jax.vjp(_ffact_fn, *_ffact_args(p_ref, w_ref, b_ref))
        dpg, dpu, dwg, dwu, dbg, dbu = vjp(da_ref[...].astype(F32))
        dp_ref[:, 0:128] = dpg.astype(BF16)
        dp_ref[:, 128:256] = dpu.astype(BF16)
        for k in range(3):
            dw_ref[k:k + 1, 0:128] = dwg[k]
            dw_ref[k:k + 1, 128:256] = dwu[k]
        db_ref[:, 0:128] = dbg
        db_ref[:, 128:256] = dbu

    return _hosted_call(
        body, name="ffact_bwd", steps=FF_BLKS,
        in_specs=[pl.BlockSpec((SEQ, 256), lambda j: (0, j)), pl.BlockSpec((3, 256), lambda j: (0, j)),
                  pl.BlockSpec((1, 256), lambda j: (0, j)), pl.BlockSpec((SEQ, 128), lambda j: (0, j))],
        out_specs=[pl.BlockSpec((SEQ, 256), lambda j: (0, j)), pl.BlockSpec((3, 256), lambda j: (0, j)),
                   pl.BlockSpec((1, 256), lambda j: (0, j))],
        out_shape=[jax.ShapeDtypeStruct((SEQ, 2 * D_FF), BF16), jax.ShapeDtypeStruct((3, 2 * D_FF), F32),
                   jax.ShapeDtypeStruct((1, 2 * D_FF), F32)],
        scratch_shapes=[], operands=(pre, conv_w, conv_b, dact), exchanges=exchanges)


def _interleave_ff(t):
    lead = t.shape[:-1]
    return t.reshape(lead + (2, FF_BLKS, 128)).swapaxes(-3, -2).reshape(lead + (2 * D_FF,))


def _deinterleave_ff(t):
    lead = t.shape[:-1]
    return t.reshape(lead + (FF_BLKS, 2, 128)).swapaxes(-3, -2).reshape(lead + (2 * D_FF,))


def _rope_tables():
    inv = 1.0 / (ROPE_THETA ** (jnp.arange(0, HEAD_DIM, 2, dtype=F32) / HEAD_DIM))
    ang = jnp.arange(SEQ, dtype=F32)[:, None] * inv[None, :]
    cos = jnp.tile(jnp.cos(ang), (1, 4))
    sin = jnp.tile(jnp.sin(ang), (1, 4))
    sign = jnp.where((jnp.arange(128) % HEAD_DIM) < HEAD_DIM // 2, -1.0, 1.0).astype(F32)
    return cos, sin * sign[None, :]


def _rope(x, cos, sin_signed):
    lane = lax.broadcasted_iota(jnp.int32, x.shape, 1)
    partner = jnp.where((lane % HEAD_DIM) < HEAD_DIM // 2, pltpu.roll(x, 128 - HEAD_DIM // 2, 1),
                        pltpu.roll(x, HEAD_DIM // 2, 1))
    return x * cos + partner * sin_signed


def _pairs_from_qkv(t):
    lead = t.shape[:-1]
    return t.reshape(lead + (3, N_PAIR, 128)).swapaxes(-3, -2).reshape(lead + (QKV_W,))


def _qkv_from_pairs(t):
    lead = t.shape[:-1]
    return t.reshape(lead + (N_PAIR, 3, 128)).swapaxes(-3, -2).reshape(lead + (QKV_W,))


def _head_masks():
    lane = lax.broadcasted_iota(jnp.int32, (1, 128), 1)
    return [(lane // HEAD_DIM) == h for h in range(2)]


def _both_heads(x):
    return jnp.concatenate([jnp.where(hm, x, 0.0)[None] for hm in _head_masks()], axis=0)


def _block_keys(branch, k_s, v_s, rows, prows, has_prev):
    a = lax.broadcasted_iota(jnp.int32, (ATTN_BLK, ATTN_BLK), 0)
    c = lax.broadcasted_iota(jnp.int32, (ATTN_BLK, ATTN_BLK), 1)
    keys, values, mask = k_s[rows, :], v_s[rows, :], c <= a
    if SEGMENT_BLOCKS[branch] > 1:
        keys = jnp.concatenate([k_s[prows, :], keys], axis=0)
        values = jnp.concatenate([v_s[prows, :], values], axis=0)
        mask = jnp.concatenate([(c >= a) & has_prev, mask], axis=1)
    twice = lambda t: jnp.broadcast_to(t[None], (2,) + t.shape)
    return twice(keys), twice(values), mask


def _block_rows(branch, t):
    d, per_seg = DILATIONS[branch], SEGMENT_BLOCKS[branch]
    if d == 1:
        start = pl.multiple_of(t * ATTN_BLK, ATTN_BLK)
        prev = pl.multiple_of(jnp.maximum(t - 1, 0) * ATTN_BLK, ATTN_BLK)
        return pl.ds(start, ATTN_BLK), pl.ds(prev, ATTN_BLK), t > 0
    r, n = t // per_seg, t % per_seg
    start = n * (ATTN_BLK * d) + r
    prev = jnp.maximum(n - 1, 0) * (ATTN_BLK * d) + r
    return pl.ds(start, ATTN_BLK, stride=d), pl.ds(prev, ATTN_BLK, stride=d), n > 0


def _attn_fwd(proj, cos, sin_signed, exchanges=()):
    scale = HEAD_DIM ** -0.5

    def body(qkv_ref, cos_ref, sin_ref, out_ref, lse_ref, q_s, k_s, v_s, *branch_s):
        o_s, l_s = branch_s[:3], branch_s[3:]
        q_s[...] = _rope(qkv_ref[:, 0:128], cos_ref[...], sin_ref[...])
        k_s[...] = _rope(qkv_ref[:, 128:256], cos_ref[...], sin_ref[...])
        v_s[...] = qkv_ref[:, 256:384]
        heads = _head_masks()
        for branch in range(3):
            def block(t, carry, branch=branch):
                rows, prows, has_prev = _block_rows(branch, t)
                keys, values, mask = _block_keys(branch, k_s, v_s, rows, prows, has_prev)
                s = jnp.where(mask, BMM_NT(_both_heads(q_s[rows, :]), keys) * scale, NEG)
                m = jnp.max(s, axis=2, keepdims=True)
                e = jnp.exp(s - m)
                l = jnp.sum(e, axis=2, keepdims=True)
                o = BMM(e, values) / l
                lse_b = m + jnp.log(l)
                o_s[branch][rows, :] = jnp.where(heads[0], o[0], o[1])
                l_s[branch][rows, :] = jnp.where(heads[0], lse_b[0], lse_b[1])
                return carry

            lax.fori_loop(0, N_BLK, block, 0, unroll=2)
        l0, l1, l2 = l_s[0][...], l_s[1][...], l_s[2][...]
        m = jnp.maximum(jnp.maximum(l0, l1), l2)
        w0, w1, w2 = jnp.exp(l0 - m), jnp.exp(l1 - m), jnp.exp(l2 - m)
        den = w0 + w1 + w2
        out_ref[...] = (w0 * o_s[0][...] + w1 * o_s[1][...] + w2 * o_s[2][...]) / den
        lse_ref[...] = m + jnp.log(den)

    tab = pl.BlockSpec((SEQ, 128), lambda j: (0, 0))
    col = pl.BlockSpec((SEQ, 128), lambda j: (0, j))
    return _hosted_call(
        body, name="attn_fwd", steps=N_PAIR,
        in_specs=[pl.BlockSpec((SEQ, 384), lambda j: (0, j)), tab, tab],
        out_specs=[col, col],
        out_shape=[jax.ShapeDtypeStruct((SEQ, 2 * ATTN_W), F32), jax.ShapeDtypeStruct((SEQ, ATTN_W), F32)],
        scratch_shapes=[pltpu.VMEM((SEQ, 128), F32)] * 9,
        operands=(proj, cos, sin_signed), exchanges=exchanges)


def _attn_bwd(proj, cos, sin_signed, cat, lse, dcat, dproj, exchanges=()):
    scale = HEAD_DIM ** -0.5

    def body(qkv_ref, cos_ref, sin_ref, out_ref, lse_ref, do_ref, _, dqkv_ref, q_s, k_s, v_s, dq_s, dk_s, dv_s,
             dod_s):
        q_s[...] = _rope(qkv_ref[:, 0:128], cos_ref[...], sin_ref[...])
        k_s[...] = _rope(qkv_ref[:, 128:256], cos_ref[...], sin_ref[...])
        v_s[...] = qkv_ref[:, 256:384]
        dq_s[...] = jnp.zeros_like(dq_s)
        dk_s[...] = jnp.zeros_like(dk_s)
        dv_s[...] = jnp.zeros_like(dv_s)
        dod_s[...] = do_ref[...] * out_ref[...]
        heads = _head_masks()
        for branch in range(3):
            def block(t, carry, branch=branch):
                rows, prows, has_prev = _block_rows(branch, t)
                keys, values, mask = _block_keys(branch, k_s, v_s, rows, prows, has_prev)
                q2, do2 = _both_heads(q_s[rows, :]), _both_heads(do_ref[rows, :])
                lse_b, dod = lse_ref[rows, :], dod_s[rows, :]
                lse2 = jnp.concatenate(
                    [jnp.max(jnp.where(hm, lse_b, NEG), axis=1, keepdims=True)[None] for hm in heads], axis=0)
                delta = jnp.concatenate(
                    [jnp.sum(jnp.where(hm, dod, 0.0), axis=1, keepdims=True)[None] for hm in heads], axis=0)
                p = jnp.exp(jnp.where(mask, BMM_NT(q2, keys) * scale, NEG) - lse2)
                ds = p * (BMM_NT(do2, values) - delta) * scale
                dq = BMM(ds, keys)
                dk = BMM_TN(ds, q2)
                dv = BMM_TN(p, do2)
                dk, dv = dk[0] + dk[1], dv[0] + dv[1]
                dq_s[rows, :] += jnp.where(heads[0], dq[0], dq[1])
                if SEGMENT_BLOCKS[branch] > 1:
                    dk_s[rows, :] += dk[ATTN_BLK:]
                    dv_s[rows, :] += dv[ATTN_BLK:]

                    @pl.when(has_prev)
                    def _():
                        dk_s[prows, :] += dk[:ATTN_BLK]
                        dv_s[prows, :] += dv[:ATTN_BLK]
                else:
                    dk_s[rows, :] += dk
                    dv_s[rows, :] += dv
                return carry

            lax.fori_loop(0, N_BLK, block, 0, unroll=2)
        dqkv_ref[:, 0:128] = _rope(dq_s[...], cos_ref[...], -sin_ref[...]).astype(BF16)
        dqkv_ref[:, 128:256] = _rope(dk_s[...], cos_ref[...], -sin_ref[...]).astype(BF16)
        dqkv_ref[:, 256:384] = dv_s[...].astype(BF16)

    tab = pl.BlockSpec((SEQ, 128), lambda j: (0, 0))
    col = pl.BlockSpec((SEQ, 128), lambda j: (0, j))
    qkv = pl.BlockSpec((SEQ, 384), lambda j: (0, j))
    (dproj,), results = _hosted_call(
        body, name="attn_bwd", steps=N_PAIR,
        in_specs=[qkv, tab, tab, col, col, col, pl.BlockSpec(memory_space=pl.ANY)],
        out_specs=[qkv],
        out_shape=[jax.ShapeDtypeStruct((SEQ, IN_PAD), BF16)],
        scratch_shapes=[pltpu.VMEM((SEQ, 128), F32)] * 7,
        operands=(proj, cos, sin_signed, cat, lse, dcat, dproj), exchanges=exchanges, aliases={6: 0})
    return dproj, results


def _bdot(a, b, dims, precision=None):
    if precision is None:
        a = a.astype(BF16)
        b = b.astype(BF16)
    return lax.dot_general(a, b, (dims, ((0,), (0,))), preferred_element_type=F32, precision=precision)


def _make_bmm(precision):
    @jax.custom_vjp
    def nn(a, b):
        return _bdot(a, b, ((2,), (1,)), precision)

    @jax.custom_vjp
    def nt(a, b):
        return _bdot(a, b, ((2,), (2,)), precision)

    @jax.custom_vjp
    def tn(a, b):
        return _bdot(a, b, ((1,), (1,)), precision)

    nn.defvjp(lambda a, b: (nn(a, b), (a, b)), lambda r, g: (nt(g, r[1]), tn(r[0], g)))
    nt.defvjp(lambda a, b: (nt(a, b), (a, b)), lambda r, g: (nn(g, r[1]), tn(g, r[0])))
    tn.defvjp(lambda a, b: (tn(a, b), (a, b)), lambda r, g: (nt(r[1], g), nn(r[0], g)))
    return nn, nt, tn


BMM, BMM_NT, BMM_TN = _make_bmm(None)
BMM3, BMM3_NT, BMM3_TN = _make_bmm(lax.Precision.HIGH)
MM3, _, _ = _make_mm(lax.Precision.HIGH)


def _head_lanes(t, off):
    lane = lax.broadcasted_iota(jnp.int32, (1, 128), 1)
    return jnp.concatenate(
        [jnp.sum(t * (lane == off + h).astype(F32), axis=1, keepdims=True)[None] for h in range(NDH)], axis=0)


@jax.custom_vjp
def _unit_lower_inverse(a_mat):
    c = a_mat.shape[1]
    eye = (lax.broadcasted_iota(jnp.int32, (c, c), 0) == lax.broadcasted_iota(jnp.int32, (c, c), 1)).astype(F32)
    power = -a_mat
    t_inv = eye + power
    for _ in range(5):
        power = BMM3(power, power)
        t_inv = t_inv + BMM3(t_inv, power)
    return t_inv


def _unit_lower_inverse_fwd(a_mat):
    t_inv = _unit_lower_inverse(a_mat)
    return t_inv, t_inv


def _unit_lower_inverse_bwd(t_inv, d_inv):
    return (-BMM3_NT(BMM3_TN(t_inv, d_inv), t_inv),)


_unit_lower_inverse.defvjp(_unit_lower_inverse_fwd, _unit_lower_inverse_bwd)


DN_STEP_CHUNKS = 4
DN_STEP_ROWS = DN_STEP_CHUNKS * CH
DN_STEPS = NCH // DN_STEP_CHUNKS
DN_BATCH = DN_STEP_CHUNKS * NDH


def _delta_chunks(qr, kr, vr, z, tail, alog_row, dt_row, nw, state):
    c = qr.shape[1]
    tails = [tail[CH * n:CH * (n + 1)] for n in range(DN_STEP_CHUNKS)]
    per_chunk = lambda t: jnp.concatenate([t] * DN_STEP_CHUNKS, axis=0)
    beta = _sigmoid(jnp.concatenate([_head_lanes(t, 0) for t in tails], axis=0))
    a_raw = jnp.concatenate([_head_lanes(t, NDH) for t in tails], axis=0)
    g = -jnp.exp(per_chunk(_head_lanes(alog_row, 0))) * _softplus(a_raw + per_chunk(_head_lanes(dt_row, 0)))

    q = qr * lax.rsqrt(jnp.sum(qr * qr, axis=2, keepdims=True) + EPS) * (128 ** -0.5)
    k = kr * lax.rsqrt(jnp.sum(kr * kr, axis=2, keepdims=True) + EPS)

    ri = lax.broadcasted_iota(jnp.int32, (c, c), 0)
    ci = lax.broadcasted_iota(jnp.int32, (c, c), 1)
    tril = ri >= ci
    lane = lax.broadcasted_iota(jnp.int32, (1, 128), 1)
    pick = [(lane == b).astype(F32) for b in range(DN_BATCH)]
    g_lanes = sum(g[b] * pick[b] for b in range(DN_BATCH))
    g_sums = MM3(tril.astype(F32), g_lanes)
    gc = jnp.concatenate([jnp.sum(g_sums * pick[b], axis=1, keepdims=True)[None] for b in range(DN_BATCH)],
                         axis=0)
    g_row = jnp.swapaxes(jnp.broadcast_to(gc, (DN_BATCH, c, c)), 1, 2)
    decay = jnp.where(tril, jnp.exp(jnp.where(tril, gc - g_row, 0.0)), 0.0)
    kb = k * beta
    t_inv = _unit_lower_inverse(jnp.where(ri > ci, BMM_NT(kb, k) * decay, 0.0))
    eg = jnp.exp(gc)
    u = BMM(t_inv, vr * beta)
    w = BMM(t_inv, kb * eg)
    qk = BMM_NT(q, k) * decay
    g_tot = jnp.sum(g, axis=1, keepdims=True)
    q_dec = q * eg
    k_dec = k * jnp.exp(g_tot - gc)
    outs = []
    for n in range(DN_STEP_CHUNKS):
        heads = slice(NDH * n, NDH * (n + 1))
        v_new = u[heads] - BMM(w[heads], state)
        outs.append(BMM(q_dec[heads], state) + BMM(qk[heads], v_new))
        state = state * jnp.exp(g_tot[heads]) + BMM_TN(k_dec[heads], v_new)
    o = jnp.concatenate(outs, axis=0)
    on = o * lax.rsqrt(jnp.mean(o * o, axis=2, keepdims=True) + EPS) * nw
    return on * _silu(z), state


def _heads(v, off=0):
    return jnp.concatenate([v[None, CH * n:CH * (n + 1), off + 128 * h:off + 128 * (h + 1)]
                            for n in range(DN_STEP_CHUNKS) for h in range(NDH)], axis=0)


def _unheads(t):
    return jnp.concatenate([jnp.concatenate([t[NDH * n + h] for h in range(NDH)], axis=1)
                            for n in range(DN_STEP_CHUNKS)], axis=0)


def _delta_fwd(c_qkv, proj, alog_row, dt_row, nw, cat, exchanges=()):
    def body(c_ref, z_ref, tail_ref, al_ref, dt_ref, nw_ref, _, y_ref, st_ref, state):
        @pl.when(pl.program_id(0) == 0)
        def _():
            state[...] = jnp.zeros_like(state)

        cv = c_ref[...]
        st_ref[0] = state[...]
        y, new_state = _delta_chunks(_heads(cv), _heads(cv, 512), _heads(cv, 1024), _heads(z_ref[...]), tail_ref[...],
                                     al_ref[...], dt_ref[...], nw_ref[...], state[...])
        y_ref[...] = _unheads(y)
        state[...] = new_state

    row = pl.BlockSpec((1, 128), lambda n: (0, 0))
    rows = DN_STEP_ROWS
    return _hosted_call(
        body, name="delta_fwd", steps=DN_STEPS,
        in_specs=[pl.BlockSpec((rows, 1536), lambda n: (n, 0)), pl.BlockSpec((rows, 512), lambda n: (n, DN_Z_COL // 512)),
                  pl.BlockSpec((rows, 128), lambda n: (n, DN_TAIL_BLK)), row, row, row, pl.BlockSpec(memory_space=pl.ANY)],
        out_specs=[pl.BlockSpec((rows, 512), lambda n: (n, 1)),
                   pl.BlockSpec((1, NDH, 128, 128), lambda n: (n, 0, 0, 0))],
        out_shape=[jax.ShapeDtypeStruct((SEQ, 2 * ATTN_W), F32), jax.ShapeDtypeStruct((DN_STEPS, NDH, 128, 128), F32)],
        scratch_shapes=[pltpu.VMEM((NDH, 128, 128), F32)],
        operands=(c_qkv, proj, proj, alog_row, dt_row, nw, cat), exchanges=exchanges, aliases={6: 0})


def _delta_bwd(c_qkv, proj, alog_row, dt_row, nw, states, dcat, exchanges=()):
    def body(c_ref, z_ref, tail_ref, al_ref, dt_ref, nw_ref, st_ref, dy_ref,
             dp_ref, dc_ref, dal_ref, ddt_ref, dnw_ref, dstate):
        @pl.when(pl.program_id(0) == 0)
        def _():
            dstate[...] = jnp.zeros_like(dstate)
            dal_ref[...] = jnp.zeros_like(dal_ref)
            ddt_ref[...] = jnp.zeros_like(ddt_ref)
            dnw_ref[...] = jnp.zeros_like(dnw_ref)

        cv = c_ref[...]
        _, vjp = jax.vjp(_delta_chunks, _heads(cv), _heads(cv, 512), _heads(cv, 1024), _heads(z_ref[...]),
                         tail_ref[...], al_ref[...], dt_ref[...], nw_ref[...], st_ref[0])
        dq, dk, dv, dz, dtail, dal, ddt, dnw, dst = vjp((_heads(dy_ref[...]), dstate[...]))
        dstate[...] = dst
        dc_ref[...] = jnp.concatenate([_unheads(dq), _unheads(dk), _unheads(dv)], axis=1)
        dp_ref[...] = jnp.concatenate([_unheads(dz), dtail, jnp.zeros((DN_STEP_ROWS, 128), F32)], axis=1).astype(BF16)
        dal_ref[...] += dal
        ddt_ref[...] += ddt
        dnw_ref[...] += dnw

    rev = lambda n: DN_STEPS - 1 - n
    row = pl.BlockSpec((1, 128), lambda n: (0, 0))
    rows = DN_STEP_ROWS
    return _hosted_call(
        body, name="delta_bwd", steps=DN_STEPS,
        in_specs=[pl.BlockSpec((rows, 1536), lambda n: (rev(n), 0)),
                  pl.BlockSpec((rows, 512), lambda n: (rev(n), DN_Z_COL // 512)),
                  pl.BlockSpec((rows, 128), lambda n: (rev(n), DN_TAIL_BLK)), row, row, row,
                  pl.BlockSpec((1, NDH, 128, 128), lambda n: (rev(n), 0, 0, 0)),
                  pl.BlockSpec((rows, 512), lambda n: (rev(n), 1))],
        out_specs=[pl.BlockSpec((rows, 768), lambda n: (rev(n), DN_Z_COL // 768)),
                   pl.BlockSpec((rows, 1536), lambda n: (rev(n), 0)), row, row, row],
        out_shape=[jax.ShapeDtypeStruct((SEQ, IN_PAD), BF16), jax.ShapeDtypeStruct((SEQ, 1536), F32)]
        + [jax.ShapeDtypeStruct((1, 128), F32)] * 3,
        scratch_shapes=[pltpu.VMEM((NDH, 128, 128), F32)],
        operands=(c_qkv, proj, proj, alog_row, dt_row, nw, states, dcat), exchanges=exchanges)


def _place():
    x, y, c = lax.axis_index("x"), lax.axis_index("y"), lax.axis_index("c")
    other_chips = [(1 - x, y), (x, 1 - y), (1 - x, 1 - y)]
    return x, y, c, other_chips


def _gather_exchange(shards):
    n = len(shards)

    def copies(ins, outs, sems):
        send_sems, recv_sems, local_sems = sems
        x, y, c, chips = _place()
        me, sibling = (x, y, c), (x, y, 1 - c)

        def copy(b, k, block, to, src=None):
            slot = outs[b].at[4 * block[0] + 2 * block[1] + block[2]]
            return pltpu.make_async_remote_copy(
                src_ref=slot if src is None else src, dst_ref=slot,
                send_sem=send_sems.at[b, k], recv_sem=recv_sems.at[b, k], device_id=to, device_id_type=MESH)

        mine = [pltpu.make_async_copy(ins[b], outs[b].at[4 * x + 2 * y + c], local_sems.at[b]) for b in range(n)]
        first = []
        for b in range(n):
            first.append(copy(b, 0, me, sibling, src=ins[b]))
            first += [copy(b, 1 + j, me, (*chip, c), src=ins[b]) for j, chip in enumerate(chips)]
        over_ici = [copy(b, 1 + j, (*chip, c), me) for b in range(n) for j, chip in enumerate(chips)]
        passed = [copy(b, 4 + j, (*chip, c), sibling) for b in range(n) for j, chip in enumerate(chips)]
        from_sibling = []
        for b in range(n):
            from_sibling.append(copy(b, 0, sibling, me))
            from_sibling += [copy(b, 4 + j, (*chip, 1 - c), me) for j, chip in enumerate(chips)]
        return mine, first, over_ici, passed, from_sibling

    def start(ins, outs, sems):
        mine, first, _, _, _ = copies(ins, outs, sems)
        for cp in mine + first:
            cp.start()

    def middle(ins, outs, sems):
        _, _, over_ici, passed, _ = copies(ins, outs, sems)
        for arrived, onward in zip(over_ici, passed):
            arrived.wait_recv()
            onward.start()

    def finish(ins, outs, sems):
        mine, first, _, passed, from_sibling = copies(ins, outs, sems)
        for cp in from_sibling:
            cp.wait_recv()
        for cp in first + passed:
            cp.wait_send()
        for cp in mine:
            cp.wait()

    return Exchange(shards, [jax.ShapeDtypeStruct((N_DEV,) + s.shape, s.dtype) for s in shards],
                    [pltpu.SemaphoreType.DMA((n, 7)), pltpu.SemaphoreType.DMA((n, 7)), pltpu.SemaphoreType.DMA((n,))],
                    start, middle, finish)


def _sibling_exchange(gs):
    n = len(gs)

    def copies(ins, outs, sems):
        send_sems, recv_sems = sems
        x, y, c, _ = _place()
        return [pltpu.make_async_remote_copy(
            src_ref=ins[b].at[2 * p + (1 - c)], dst_ref=outs[b].at[p],
            send_sem=send_sems.at[b, p], recv_sem=recv_sems.at[b, p],
            device_id=(x, y, 1 - c), device_id_type=MESH) for b in range(n) for p in range(4)]

    def start(ins, outs, sems):
        for cp in copies(ins, outs, sems):
            cp.start()

    def finish(ins, outs, sems):
        for cp in copies(ins, outs, sems):
            cp.wait()

    return Exchange(gs, [jax.ShapeDtypeStruct((4,) + g.shape[1:], g.dtype) for g in gs],
                    [pltpu.SemaphoreType.DMA((n, 4)), pltpu.SemaphoreType.DMA((n, 4))], start, None, finish)


def _chips_exchange(hs):
    n = len(hs)

    def copies(ins, outs, sems):
        send_sems, recv_sems, local_sems = sems
        x, y, c, chips = _place()
        my_chip = 2 * x + y
        local = [pltpu.make_async_copy(ins[b].at[my_chip], outs[b].at[my_chip], local_sems.at[b]) for b in range(n)]
        sends, arrivals = [], []
        for b in range(n):
            for k, (px, py) in enumerate(chips):
                peer = 2 * px + py
                sends.append(pltpu.make_async_remote_copy(
                    src_ref=ins[b].at[peer], dst_ref=outs[b].at[my_chip],
                    send_sem=send_sems.at[b, k], recv_sem=recv_sems.at[b, k],
                    device_id=(px, py, c), device_id_type=MESH))
                arrivals.append(pltpu.make_async_remote_copy(
                    src_ref=ins[b].at[peer], dst_ref=outs[b].at[peer],
                    send_sem=send_sems.at[b, k], recv_sem=recv_sems.at[b, k],
                    device_id=(px, py, c), device_id_type=MESH))
        return local, sends, arrivals

    def start(ins, outs, sems):
        local, sends, _ = copies(ins, outs, sems)
        for cp in local + sends:
            cp.start()

    def finish(ins, outs, sems):
        local, sends, arrivals = copies(ins, outs, sems)
        for cp in arrivals:
            cp.wait_recv()
        for cp in sends:
            cp.wait_send()
        for cp in local:
            cp.wait()

    return Exchange(hs, [jax.ShapeDtypeStruct(h.shape, h.dtype) for h in hs],
                    [pltpu.SemaphoreType.DMA((n, 3)), pltpu.SemaphoreType.DMA((n, 3)), pltpu.SemaphoreType.DMA((n,))],
                    start, None, finish)


def _run_exchange(exchange, name):
    n_in, n_out = len(exchange.operands), len(exchange.out_shapes)

    def body(*refs):
        ins, outs, sems = refs[:n_in], refs[n_in:n_in + n_out], refs[n_in + n_out:]
        exchange.start(ins, outs, sems)
        if exchange.middle is not None:
            exchange.middle(ins, outs, sems)
        exchange.finish(ins, outs, sems)

    return pl.pallas_call(
        body, name=name,
        in_specs=[HBM_SPEC] * n_in, out_specs=[HBM_SPEC] * n_out,
        out_shape=exchange.out_shapes, scratch_shapes=exchange.sems,
    )(*exchange.operands)


def _pair_add(g, r, core, name):
    _, nr, nc = g.shape
    tr = nr // 2 if nr % 32 == 0 else nr

    def body(core_ref, g_ref, r_ref, o_ref):
        o_ref[...] = (g_ref[...].astype(F32) + r_ref[...].astype(F32)).astype(BF16)

    return pl.pallas_call(
        body, name=name,
        grid_spec=pltpu.PrefetchScalarGridSpec(
            num_scalar_prefetch=1, grid=(4, nr // tr),
            in_specs=[pl.BlockSpec((1, tr, nc), lambda p, i, core: (2 * p + core[0], i, 0)),
                      pl.BlockSpec((1, tr, nc), lambda p, i, core: (p, i, 0))],
            out_specs=pl.BlockSpec((1, tr, nc), lambda p, i, core: (p, i, 0))),
        out_shape=jax.ShapeDtypeStruct(r.shape, BF16),
        compiler_params=_cp("parallel", "parallel"),
    )(core, g, r)


def _all_gather_sum_small(v):
    rows = v.shape[0]

    def body(x_ref, sum_ref, out_ref, send_sems, recv_sems, local_sem):
        x, y, c, chips = _place()
        me, sibling = (x, y, c), (x, y, 1 - c)

        def block(px, py, pc):
            return out_ref.at[pl.ds((4 * px + 2 * py + pc) * rows, rows), :]

        def copy(k, blk, to, src=None):
            return pltpu.make_async_remote_copy(
                src_ref=block(*blk) if src is None else src, dst_ref=block(*blk),
                send_sem=send_sems.at[k], recv_sem=recv_sems.at[k], device_id=to, device_id_type=MESH)

        mine = pltpu.make_async_copy(x_ref, block(*me), local_sem)
        mine.start()
        first = [copy(0, me, sibling, src=x_ref)]
        first += [copy(1 + j, me, (*chip, c), src=x_ref) for j, chip in enumerate(chips)]
        for cp in first:
            cp.start()
        passed = [copy(4 + j, (*chip, c), sibling) for j, chip in enumerate(chips)]
        for j, chip in enumerate(chips):
            copy(1 + j, (*chip, c), me).wait_recv()
            passed[j].start()
        copy(0, sibling, me).wait_recv()
        for j, chip in enumerate(chips):
            copy(4 + j, (*chip, 1 - c), me).wait_recv()
        for cp in first + passed:
            cp.wait_send()
        mine.wait()
        total = out_ref[pl.ds(0, rows), :]
        for d in range(1, N_DEV):
            total = total + out_ref[pl.ds(d * rows, rows), :]
        sum_ref[...] = total

    vm = pl.BlockSpec(memory_space=pltpu.VMEM)
    return pl.pallas_call(
        body, name="small_all_reduce",
        in_specs=[vm], out_specs=[vm],
        out_shape=[jax.ShapeDtypeStruct((rows, 128), F32)],
        scratch_shapes=[pltpu.VMEM((N_DEV * rows, 128), F32), pltpu.SemaphoreType.DMA((7,)),
                        pltpu.SemaphoreType.DMA((7,)), pltpu.SemaphoreType.DMA],
    )(v)[0]


def _adamw(w, g, m, v):
    m = ADAM_B1 * m + (1.0 - ADAM_B1) * g
    v = ADAM_B2 * v + (1.0 - ADAM_B2) * (g * g)
    m_hat = m / (1.0 - ADAM_B1 ** ADAM_STEP)
    v_hat = v / (1.0 - ADAM_B2 ** ADAM_STEP)
    delta = -ADAM_LR * (m_hat / (jnp.sqrt(v_hat) + ADAM_EPS) + ADAM_WD * w)
    return delta, m, v


ADAM_TILE = dict(w_in=(IN_COLS // N_DEV, 256), w_out=(128, D_MODEL), ffn_w_in=(176, D_MODEL), ffn_w_out=(176, D_MODEL))


def _sum_chips(p):
    p = p.astype(F32)
    return (p[0] + p[1]) + (p[2] + p[3])


def _adamw_sharded(parts, w, m, v, tile, name):
    nl, nr, nc = w.shape
    tr, tc = tile

    def body(*refs):
        p_refs, (w_ref, m_ref, v_ref, g_ref, d_ref, nm_ref, nv_ref) = refs[:nl], refs[nl:]
        layer = pl.program_id(0)
        p = p_refs[0][...]
        for l in range(1, nl):
            p = jnp.where(layer == l, p_refs[l][...], p)
        g = _sum_chips(p)
        delta, nm, nv = _adamw(w_ref[0], g, m_ref[0], v_ref[0])
        g_ref[0] = g
        d_ref[0] = delta
        nm_ref[0] = nm
        nv_ref[0] = nv

    blk = pl.BlockSpec((1, tr, tc), lambda l, i, j: (l, i, j))
    return pl.pallas_call(
        body, name=name, grid=(nl, nr // tr, nc // tc),
        in_specs=[pl.BlockSpec((4, tr, tc), lambda l, i, j: (0, i, j))] * nl + [blk, blk, blk],
        out_specs=[blk] * 4,
        out_shape=[jax.ShapeDtypeStruct(w.shape, F32)] * 4,
        compiler_params=_cp("parallel", "parallel", "parallel"),
    )(*parts, w, m, v)


def _adamw_small(g, w, m, v):
    def body(g_ref, w_ref, m_ref, v_ref, d_ref, nm_ref, nv_ref):
        delta, nm, nv = _adamw(w_ref[...], g_ref[...], m_ref[...], v_ref[...])
        d_ref[...] = delta
        nm_ref[...] = nm
        nv_ref[...] = nv

    return pl.pallas_call(
        body, name="adamw_small",
        out_shape=[jax.ShapeDtypeStruct(g.shape, F32)] * 3,
    )(g, w, m, v)


def _pack(arrays, rows):
    flat = jnp.concatenate([a.reshape(-1).astype(F32) for a in arrays])
    return jnp.pad(flat, (0, rows * 128 - flat.shape[0])).reshape(rows, 128)


def _unpack(packed, shapes):
    flat = packed.reshape(-1)
    out, off = [], 0
    for s in shapes:
        n = math.prod(s)
        out.append(flat[off:off + n].reshape(s))
        off += n
    return out


def _row(v, width=None):
    v = v.reshape(1, -1)
    return v if width is None else jnp.pad(v, ((0, 0), (0, width - v.shape[1])))


def _layer_fwd(x, wts, tables, attn_exchanges=(), delta_exchanges=(), on_attn=None, on_delta=None):
    h = _norm_fwd(x, wts["norm_pre_mix"], "norm_pre_mix")
    proj = _matmul(h, wts["w_in"], tb=True, tm=SEQ, tn=768, tk=1024, name="mm_proj")
    (cat, lse), got = _attn_fwd(proj, *tables, exchanges=attn_exchanges)
    if on_attn is not None:
        on_attn(got)
    c_qkv = _dnconv_fwd(proj, wts["dn_conv_w"])
    (cat, states), got = _delta_fwd(c_qkv, proj, wts["dn_a_log"], wts["dn_dt_bias"], wts["dn_norm_w"], cat,
                                    exchanges=delta_exchanges)
    if on_delta is not None:
        on_delta(got)
    mix = _matmul(cat, wts["w_out"], tm=512, tn=1024, tk=1024, name="mm_mix")
    x1 = _resnorm_fwd(x, mix, wts["norm_post_mix"], "norm_post_mix")
    h2 = _norm_fwd(x1, wts["norm_pre_ffn"], "norm_pre_ffn")
    pre = _matmul(h2, wts["ffn_w_in"], tb=True, tm=SEQ, tn=512, tk=1024, name="mm_ffn_in", out_dtype=BF16)
    act = _ffact_fwd(pre, wts["ffn_conv_w"], wts["ffn_conv_b"])
    f = _matmul(act, wts["ffn_w_out"], tm=512, tn=1024, tk=D_FF, name="mm_ffn_out")
    x2 = _resnorm_fwd(x1, f, wts["norm_post_ffn"], "norm_post_ffn")
    saved = dict(x=x, h=h, proj=proj, lse=lse, c_qkv=c_qkv, states=states, cat=cat, mix=mix, x1=x1, h2=h2, pre=pre,
                 act=act, f=f)
    return x2, saved


def _layer_bwd(dx2, wts, s, tables, ffact_exchanges=(), delta_exchanges=None, attn_exchanges=None):
    g = {}
    df, g["norm_post_ffn"] = _norm_bwd(s["f"], wts["norm_post_ffn"], dx2, None, "norm_post_ffn_bwd", BF16)
    dact = _matmul(df, wts["ffn_w_out"], tb=True, tm=SEQ, tn=1408, tk=1024, name="mm_dact", out_dtype=BF16)
    g["ffn_w_out"] = _matmul(s["act"], df, ta=True, tm=1408, tn=512, tk=SEQ, name="mm_dw_ffn_out", out_dtype=BF16)
    (dpre, g["ffn_conv_w"], g["ffn_conv_b"]), got = _ffact_bwd(s["pre"], wts["ffn_conv_w"], wts["ffn_conv_b"], dact,
                                                               exchanges=ffact_exchanges)
    dh2 = _matmul(dpre, wts["ffn_w_in"], tm=1024, tn=1024, tk=1408, name="mm_dh2")
    g["ffn_w_in"] = _matmul(dpre, s["h2"], ta=True, tm=512, tn=1024, tk=SEQ, name="mm_dw_ffn_in", out_dtype=BF16)
    dx1, g["norm_pre_ffn"] = _norm_bwd(s["x1"], wts["norm_pre_ffn"], dh2, dx2, "norm_pre_ffn_bwd")
    dmix, g["norm_post_mix"] = _norm_bwd(s["mix"], wts["norm_post_mix"], dx1, None, "norm_post_mix_bwd", BF16)
    dcat = _matmul(dmix, wts["w_out"], tb=True, tm=SEQ, tn=512, tk=1024, name="mm_dcat")
    g["w_out"] = _matmul(s["cat"], dmix, ta=True, tm=1024, tn=512, tk=SEQ, name="mm_dw_out", out_dtype=BF16)
    (dproj, dc, g["dn_a_log"], g["dn_dt_bias"], g["dn_norm_w"]), got = _delta_bwd(
        s["c_qkv"], s["proj"], wts["dn_a_log"], wts["dn_dt_bias"], wts["dn_norm_w"], s["states"], dcat,
        exchanges=delta_exchanges(g, got) if delta_exchanges is not None else ())
    dproj, got = _attn_bwd(s["proj"], *tables, s["cat"], s["lse"], dcat, dproj,
                           exchanges=attn_exchanges(got) if attn_exchanges is not None else ())
    dproj, g["dn_conv_w"] = _dnconv_bwd(s["proj"], wts["dn_conv_w"], dc, dproj)
    dh = _matmul(dproj, wts["w_in"], tm=1024, tn=1024, tk=1280, name="mm_dh")
    g["w_in"] = _matmul(dproj, s["h"], ta=True, tm=768, tn=1024, tk=SEQ, name="mm_dw_in", out_dtype=BF16)
    dx, g["norm_pre_mix"] = _norm_bwd(s["x"], wts["norm_pre_mix"], dh, dx1, "norm_pre_mix_bwd")
    return dx, g, got


BIG = ("w_in", "w_out", "ffn_w_in", "ffn_w_out")
COLUMN_SHARDED = ("w_in", "ffn_w_in")
SMALL_SHARDED = ("dn_conv_w", "ffn_conv_w")
REPLICATED = ("dn_a_log", "dn_dt_bias", "dn_norm_w", "ffn_conv_b", "norm_pre_mix", "norm_post_mix", "norm_pre_ffn",
              "norm_post_ffn")
WEIGHTS = ("w_in", "dn_conv_w", "dn_a_log", "dn_dt_bias", "dn_norm_w", "w_out", "ffn_w_in", "ffn_conv_w", "ffn_conv_b",
           "ffn_w_out", "norm_pre_mix", "norm_post_mix", "norm_pre_ffn", "norm_post_ffn")
FULL_SHAPE = dict(dn_conv_w=(DEPTH, 4, 1536), ffn_conv_w=(DEPTH, 3, 2 * D_FF), dn_a_log=(DEPTH, NDH),
                  dn_dt_bias=(DEPTH, NDH), dn_norm_w=(DEPTH, 128), ffn_conv_b=(DEPTH, 2 * D_FF),
                  norm_pre_mix=(DEPTH, D_MODEL), norm_post_mix=(DEPTH, D_MODEL), norm_pre_ffn=(DEPTH, D_MODEL),
                  norm_post_ffn=(DEPTH, D_MODEL))
SMALL_GRAD_ORDER = REPLICATED + SMALL_SHARDED
SMALL_GRAD_ROWS = 520
SMALL_W_ROWS = 48
SMALL_ADAM_ROWS = 200


def _w_in_rows_to_kernel_order(t):
    qkv = t[:QKV_W].reshape(3, N_PAIR, 128, -1).swapaxes(0, 1).reshape(QKV_W, -1)
    return jnp.pad(jnp.concatenate([qkv, t[QKV_W:]], axis=0), ((0, IN_PAD - IN_COLS), (0, 0)))


def _w_in_rows_from_kernel_order(t):
    qkv = t[:QKV_W].reshape(N_PAIR, 3, 128, -1).swapaxes(0, 1).reshape(QKV_W, -1)
    return jnp.concatenate([qkv, t[QKV_W:IN_COLS]], axis=0)


def _interleave_ff_rows(t):
    return t.reshape(2, FF_BLKS, 128, -1).swapaxes(0, 1).reshape(2 * D_FF, -1)


def _deinterleave_ff_rows(t):
    return t.reshape(FF_BLKS, 2, 128, -1).swapaxes(0, 1).reshape(2 * D_FF, -1)


def kernel(x, w_in, dn_conv_w, dn_a_log, dn_dt_bias, dn_norm_w, w_out, ffn_w_in, ffn_conv_w, ffn_conv_b, ffn_w_out, norm_pre_mix, norm_post_mix, norm_pre_ffn, norm_post_ffn, loss_target, m_w_in, m_dn_conv_w, m_dn_a_log, m_dn_dt_bias, m_dn_norm_w, m_w_out, m_ffn_w_in, m_ffn_conv_w, m_ffn_conv_b, m_ffn_w_out, m_norm_pre_mix, m_norm_post_mix, m_norm_pre_ffn, m_norm_post_ffn, v_w_in, v_dn_conv_w, v_dn_a_log, v_dn_dt_bias, v_dn_norm_w, v_w_out, v_ffn_w_in, v_ffn_conv_w, v_ffn_conv_b, v_ffn_w_out, v_norm_pre_mix, v_norm_post_mix, v_norm_pre_ffn, v_norm_post_ffn):
    local = dict(w_in=w_in, dn_conv_w=dn_conv_w, dn_a_log=dn_a_log, dn_dt_bias=dn_dt_bias, dn_norm_w=dn_norm_w,
                 w_out=w_out, ffn_w_in=ffn_w_in, ffn_conv_w=ffn_conv_w, ffn_conv_b=ffn_conv_b, ffn_w_out=ffn_w_out,
                 norm_pre_mix=norm_pre_mix, norm_post_mix=norm_post_mix, norm_pre_ffn=norm_pre_ffn,
                 norm_post_ffn=norm_post_ffn)
    mom_m = dict(w_in=m_w_in, dn_conv_w=m_dn_conv_w, dn_a_log=m_dn_a_log, dn_dt_bias=m_dn_dt_bias,
                 dn_norm_w=m_dn_norm_w, w_out=m_w_out, ffn_w_in=m_ffn_w_in, ffn_conv_w=m_ffn_conv_w,
                 ffn_conv_b=m_ffn_conv_b, ffn_w_out=m_ffn_w_out, norm_pre_mix=m_norm_pre_mix,
                 norm_post_mix=m_norm_post_mix, norm_pre_ffn=m_norm_pre_ffn, norm_post_ffn=m_norm_post_ffn)
    mom_v = dict(w_in=v_w_in, dn_conv_w=v_dn_conv_w, dn_a_log=v_dn_a_log, dn_dt_bias=v_dn_dt_bias,
                 dn_norm_w=v_dn_norm_w, w_out=v_w_out, ffn_w_in=v_ffn_w_in, ffn_conv_w=v_ffn_conv_w,
                 ffn_conv_b=v_ffn_conv_b, ffn_w_out=v_ffn_w_out, norm_pre_mix=v_norm_pre_mix,
                 norm_post_mix=v_norm_post_mix, norm_pre_ffn=v_norm_pre_ffn, norm_post_ffn=v_norm_post_ffn)
    dev = 4 * lax.axis_index("x") + 2 * lax.axis_index("y") + lax.axis_index("c")
    core = lax.axis_index("c").astype(jnp.int32).reshape(1)

    def shard(n, l):
        s = local[n].transpose(0, 2, 1) if n in COLUMN_SHARDED else local[n]
        return s[l].astype(BF16)

    def matrix(n, gathered):
        if n == "w_in":
            return _w_in_rows_to_kernel_order(gathered.reshape(IN_COLS, D_MODEL))
        if n == "ffn_w_in":
            return _interleave_ff_rows(gathered.reshape(2 * D_FF, D_MODEL))
        return gathered.reshape(-1, D_MODEL)

    small_w = _pack([dn_conv_w, ffn_conv_w], SMALL_W_ROWS)
    g_w_in0, g_small = _run_exchange(_gather_exchange([shard("w_in", 0), small_w]), "weights_all_gather")
    n_dn, n_ff = DEPTH * 4 * 192, DEPTH * 3 * 704
    sm = g_small.reshape(N_DEV, -1)
    full_dn_conv = sm[:, :n_dn].reshape(N_DEV, DEPTH, 4, 192).transpose(1, 2, 0, 3).reshape(DEPTH, 4, 1536)
    full_ff_conv = _interleave_ff(
        sm[:, n_dn:n_dn + n_ff].reshape(N_DEV, DEPTH, 3, 704).transpose(1, 2, 0, 3).reshape(DEPTH, 3, 2 * D_FF))

    def small_weights(l):
        wts = dict(dn_conv_w=full_dn_conv[l], ffn_conv_w=full_ff_conv[l], ffn_conv_b=_interleave_ff(_row(ffn_conv_b[l])),
                   dn_a_log=_row(dn_a_log[l], 128), dn_dt_bias=_row(dn_dt_bias[l], 128))
        for n in ("dn_norm_w", "norm_pre_mix", "norm_post_mix", "norm_pre_ffn", "norm_post_ffn"):
            wts[n] = _row(local[n][l])
        return wts

    weights = [small_weights(l) for l in range(DEPTH)]
    weights[0]["w_in"] = matrix("w_in", g_w_in0)

    def gather_behind(wanted):
        def deliver(got):
            for (n, l), g in zip(wanted, got[0]):
                weights[l][n] = matrix(n, g)

        return [_gather_exchange([shard(n, l) for n, l in wanted])], deliver

    tables = _rope_tables()
    ex_attn0, on_attn0 = gather_behind([("w_out", 0), ("ffn_w_in", 0)])
    ex_delta0, on_delta0 = gather_behind([("ffn_w_out", 0), ("w_in", 1)])
    ex_attn1, on_attn1 = gather_behind([("w_out", 1), ("ffn_w_in", 1)])
    ex_delta1, on_delta1 = gather_behind([("ffn_w_out", 1)])
    act, saved0 = _layer_fwd(x[0], weights[0], tables, ex_attn0, ex_delta0, on_attn0, on_delta0)
    act, saved1 = _layer_fwd(act, weights[1], tables, ex_attn1, ex_delta1, on_attn1, on_delta1)
    loss_part, dact = _loss_fwd_bwd(act, loss_target[0])

    def to_devices(name, t):
        if name == "w_in":
            t = _w_in_rows_from_kernel_order(t)
        if name == "ffn_w_in":
            t = _deinterleave_ff_rows(t)
        return t.reshape(N_DEV, t.shape[0] // N_DEV, t.shape[1])

    def pair_sums(names, layer, to_dev, from_sibling):
        return [_pair_add(gd, r, core, "grads_pair_add_%s_%d" % (n, layer))
                for n, gd, r in zip(names, to_dev, from_sibling)]

    grads = [None] * DEPTH
    dact, grads[1], _ = _layer_bwd(dact, weights[1], saved1, tables)
    to_dev1 = [to_devices(n, grads[1][n]) for n in BIG]
    early = ("w_out", "ffn_w_in", "ffn_w_out")
    parts, stash = {}, {}

    def delta_exchanges(g, got_ffact):
        stash["to_dev0"] = [to_devices(n, g[n]) for n in early]
        return [_chips_exchange(pair_sums(BIG, 1, to_dev1, got_ffact[0])), _sibling_exchange(stash["to_dev0"])]

    def attn_exchanges(got_delta):
        for n, p in zip(BIG, got_delta[0]):
            parts[n, 1] = p
        return [_chips_exchange(pair_sums(early, 0, stash["to_dev0"], got_delta[1]))]

    dact, grads[0], got_attn = _layer_bwd(dact, weights[0], saved0, tables, [_sibling_exchange(to_dev1)],
                                          delta_exchanges, attn_exchanges)
    for n, p in zip(early, got_attn[0]):
        parts[n, 0] = p
    grad_x = dact[None]
    last = [to_devices("w_in", grads[0]["w_in"])]
    from_sibling = _run_exchange(_sibling_exchange(last), "grads_to_sibling")
    parts["w_in", 0], = _run_exchange(_chips_exchange(pair_sums(("w_in",), 0, last, from_sibling)), "grads_to_chips")

    def small_grad(name):
        t = jnp.stack([grads[l][name] for l in range(DEPTH)])
        if name in ("dn_a_log", "dn_dt_bias"):
            t = t[:, 0, :NDH]
        if name in ("ffn_conv_w", "ffn_conv_b"):
            t = _deinterleave_ff(t)
        return t.reshape(FULL_SHAPE[name])

    small_part = _pack([small_grad(n) for n in SMALL_GRAD_ORDER] + [loss_part[0, :1]], SMALL_GRAD_ROWS)
    small_sum = _all_gather_sum_small(small_part)
    small_g = dict(zip(SMALL_GRAD_ORDER + ("loss",), _unpack(small_sum, [FULL_SHAPE[n] for n in SMALL_GRAD_ORDER] + [(1,)])))
    loss = small_g["loss"][0]
    small_g["dn_conv_w"] = lax.dynamic_slice_in_dim(small_g["dn_conv_w"], dev * 192, 192, axis=2)
    small_g["ffn_conv_w"] = lax.dynamic_slice_in_dim(small_g["ffn_conv_w"], dev * 704, 704, axis=2)

    out_g, out_d, out_m, out_v = {}, {}, {}, {}
    for n in BIG:
        turn = (lambda t: t.transpose(0, 2, 1)) if n in COLUMN_SHARDED else (lambda t: t)
        outs = _adamw_sharded([parts[n, l] for l in range(DEPTH)], turn(local[n]), turn(mom_m[n]), turn(mom_v[n]),
                              ADAM_TILE[n], "adamw_" + n)
        out_g[n], out_d[n], out_m[n], out_v[n] = [turn(t) for t in outs]
    shapes = [small_g[n].shape for n in SMALL_GRAD_ORDER]
    d_s, m_s, v_s = _adamw_small(_pack([small_g[n] for n in SMALL_GRAD_ORDER], SMALL_ADAM_ROWS),
                                 _pack([local[n] for n in SMALL_GRAD_ORDER], SMALL_ADAM_ROWS),
                                 _pack([mom_m[n] for n in SMALL_GRAD_ORDER], SMALL_ADAM_ROWS),
                                 _pack([mom_v[n] for n in SMALL_GRAD_ORDER], SMALL_ADAM_ROWS))
    for n, d, m, v in zip(SMALL_GRAD_ORDER, _unpack(d_s, shapes), _unpack(m_s, shapes), _unpack(v_s, shapes)):
        out_g[n], out_d[n], out_m[n], out_v[n] = small_g[n], d, m, v
    return (loss, grad_x, *[out_g[n] for n in WEIGHTS], *[out_d[n] for n in WEIGHTS],
            *[out_m[n] for n in WEIGHTS], *[out_v[n] for n in WEIGHTS])
```

```python
import functools
import math

import jax
import jax.numpy as jnp
from jax import lax
from jax.experimental import pallas as pl
from jax.experimental.pallas import tpu as pltpu

F32 = jnp.float32
BF16 = jnp.bfloat16
HI = lax.Precision.HIGHEST
MESH = pl.DeviceIdType.MESH

N_DEV = 8
SEQ = 2048
D_MODEL = 1024
DEPTH = 2
N_PAIR = 4
HEAD_DIM = 64
ATTN_W = 512
ATTN_BLK = 128
DILATIONS = (1, 4, 16)
SEGMENT_BLOCKS = (16, 4, 1)
N_BLK = SEQ // ATTN_BLK
NDH = 4
CH = 64
NCH = SEQ // CH
IN_COLS = 3592
IN_PAD = 3840
QKV_W = 3 * ATTN_W
DN_QKV_BLK0 = QKV_W // 128
DN_QKV_BLKS = 1536 // 128
DN_Z_COL = 3072
DN_TAIL_BLK = 3584 // 128
D_FF = 2816
FF_BLKS = D_FF // 128
EPS = 1e-6
NEG = -1e30
ROPE_THETA = 10000.0

ADAM_LR, ADAM_B1, ADAM_B2, ADAM_EPS, ADAM_WD, ADAM_STEP = 0.001, 0.9, 0.999, 1e-08, 0.01, 10

VMEM_LIMIT = 56 * 1024 * 1024


def _cp(*sem):
    return pltpu.CompilerParams(dimension_semantics=sem, vmem_limit_bytes=VMEM_LIMIT)


class Exchange:
    def __init__(self, operands, out_shapes, sems, start, middle, finish):
        self.operands, self.out_shapes, self.sems = list(operands), list(out_shapes), list(sems)
        self.start, self.middle, self.finish = start, middle, finish


HBM_SPEC = pl.BlockSpec(memory_space=pltpu.HBM)


def _hosted_call(body, *, name, steps, in_specs, out_specs, out_shape, scratch_shapes, operands, exchanges=(),
                 aliases=None):
    n_in, n_out, n_scr = len(in_specs), len(out_specs), len(scratch_shapes)

    def take(refs, pos, counts):
        groups = []
        for c in counts:
            groups.append(refs[pos:pos + c])
            pos += c
        return groups, pos

    def full_body(*refs):
        ins, pos = refs[:n_in], n_in
        ex_ins, pos = take(refs, pos, [len(e.operands) for e in exchanges])
        outs, pos = refs[pos:pos + n_out], pos + n_out
        ex_outs, pos = take(refs, pos, [len(e.out_shapes) for e in exchanges])
        scr, pos = refs[pos:pos + n_scr], pos + n_scr
        ex_sems, pos = take(refs, pos, [len(e.sems) for e in exchanges])
        step = pl.program_id(0)
        for e, a, b, s in zip(exchanges, ex_ins, ex_outs, ex_sems):
            pl.when(step == 0)(functools.partial(e.start, a, b, s))
            if e.middle is not None:
                pl.when(step == (3 * steps) // 4)(functools.partial(e.middle, a, b, s))
        body(*ins, *outs, *scr)
        for e, a, b, s in zip(exchanges, ex_ins, ex_outs, ex_sems):
            pl.when(step == steps - 1)(functools.partial(e.finish, a, b, s))

    n_ex_in = sum(len(e.operands) for e in exchanges)
    n_ex_out = sum(len(e.out_shapes) for e in exchanges)
    results = pl.pallas_call(
        full_body, name=name, grid=(steps,),
        in_specs=list(in_specs) + [HBM_SPEC] * n_ex_in,
        out_specs=list(out_specs) + [HBM_SPEC] * n_ex_out,
        out_shape=list(out_shape) + [s for e in exchanges for s in e.out_shapes],
        scratch_shapes=list(scratch_shapes) + [s for e in exchanges for s in e.sems],
        input_output_aliases=aliases or {},
        compiler_params=_cp("arbitrary"),
    )(*operands, *[a for e in exchanges for a in e.operands])
    ex_results, _ = take(results, n_out, [len(e.out_shapes) for e in exchanges])
    return results[:n_out], ex_results


def _dot(a, b, dims, precision=None):
    if precision is None:
        a = a.astype(BF16)
        b = b.astype(BF16)
    return lax.dot_general(a, b, (dims, ((), ())), preferred_element_type=F32, precision=precision)


def _make_mm(precision):
    @jax.custom_vjp
    def nn(a, b):
        return _dot(a, b, ((1,), (0,)), precision)

    @jax.custom_vjp
    def nt(a, b):
        return _dot(a, b, ((1,), (1,)), precision)

    @jax.custom_vjp
    def tn(a, b):
        return _dot(a, b, ((0,), (0,)), precision)

    nn.defvjp(lambda a, b: (nn(a, b), (a, b)), lambda r, g: (nt(g, r[1]), tn(r[0], g)))
    nt.defvjp(lambda a, b: (nt(a, b), (a, b)), lambda r, g: (nn(g, r[1]), tn(g, r[0])))
    tn.defvjp(lambda a, b: (tn(a, b), (a, b)), lambda r, g: (nt(r[1], g), nn(r[0], g)))
    return nn, nt, tn


MM, MM_NT, MM_TN = _make_mm(None)


def _matmul(a, b, *, ta=False, tb=False, tm, tn, tk, name, out_dtype=F32):
    (k_dim, m_dim) = a.shape if ta else a.shape[::-1]
    (n_dim, k2) = b.shape if tb else b.shape[::-1]
    assert k_dim == k2 and m_dim % tm == 0 and n_dim % tn == 0 and k_dim % tk == 0, (a.shape, b.shape, tm, tn, tk)
    nk = k_dim // tk
    dims = ((0 if ta else 1,), (1 if tb else 0,))

    def body(a_ref, b_ref, o_ref, *acc):
        p = _dot(a_ref[...], b_ref[...], dims)
        if nk == 1:
            o_ref[...] = p.astype(out_dtype)
            return
        acc_ref, k = acc[0], pl.program_id(2)

        @pl.when(k == 0)
        def _():
            acc_ref[...] = p

        @pl.when(k > 0)
        def _():
            acc_ref[...] += p

        @pl.when(k == nk - 1)
        def _():
            o_ref[...] = acc_ref[...].astype(out_dtype)

    a_spec = pl.BlockSpec((tk, tm), lambda i, j, k: (k, i)) if ta else pl.BlockSpec((tm, tk), lambda i, j, k: (i, k))
    b_spec = pl.BlockSpec((tn, tk), lambda i, j, k: (j, k)) if tb else pl.BlockSpec((tk, tn), lambda i, j, k: (k, j))
    return pl.pallas_call(
        body, name=name,
        grid=(m_dim // tm, n_dim // tn, nk),
        in_specs=[a_spec, b_spec],
        out_specs=pl.BlockSpec((tm, tn), lambda i, j, k: (i, j)),
        out_shape=jax.ShapeDtypeStruct((m_dim, n_dim), out_dtype),
        scratch_shapes=[pltpu.VMEM((tm, tn), F32)] if nk > 1 else [],
        compiler_params=_cp("parallel", "parallel", "arbitrary"),
    )(a, b)


NORM_ROWS = 256


def _rms(x, w):
    return x * lax.rsqrt(jnp.mean(x * x, axis=1, keepdims=True) + EPS) * w


def _norm_fwd(x, w_row, name, out_dtype=BF16):
    def body(x_ref, w_ref, o_ref):
        o_ref[...] = _rms(x_ref[...], w_ref[...]).astype(out_dtype)

    return pl.pallas_call(
        body, name=name, grid=(SEQ // NORM_ROWS,),
        in_specs=[pl.BlockSpec((NORM_ROWS, D_MODEL), lambda i: (i, 0)), pl.BlockSpec((1, D_MODEL), lambda i: (0, 0))],
        out_specs=pl.BlockSpec((NORM_ROWS, D_MODEL), lambda i: (i, 0)),
        out_shape=jax.ShapeDtypeStruct((SEQ, D_MODEL), out_dtype),
        compiler_params=_cp("parallel"),
    )(x, w_row)


def _resnorm_fwd(x, f, w_row, name):
    def body(x_ref, f_ref, w_ref, o_ref):
        o_ref[...] = x_ref[...] + _rms(f_ref[...], w_ref[...])

    blk = pl.BlockSpec((NORM_ROWS, D_MODEL), lambda i: (i, 0))
    return pl.pallas_call(
        body, name=name, grid=(SEQ // NORM_ROWS,),
        in_specs=[blk, blk, pl.BlockSpec((1, D_MODEL), lambda i: (0, 0))],
        out_specs=blk, out_shape=jax.ShapeDtypeStruct((SEQ, D_MODEL), F32),
        compiler_params=_cp("parallel"),
    )(x, f, w_row)


def _norm_bwd(x, w_row, dy, add, name, dx_dtype=F32):
    has_add = add is not None

    def body(*refs):
        if has_add:
            x_ref, w_ref, dy_ref, add_ref, dx_ref, dw_ref = refs
        else:
            x_ref, w_ref, dy_ref, dx_ref, dw_ref = refs
        _, vjp = jax.vjp(_rms, x_ref[...], w_ref[...])
        dx, dw = vjp(dy_ref[...])
        dx_ref[...] = (dx + add_ref[...] if has_add else dx).astype(dx_dtype)

        @pl.when(pl.program_id(0) == 0)
        def _():
            dw_ref[...] = jnp.zeros_like(dw_ref)

        dw_ref[...] += dw

    blk = pl.BlockSpec((NORM_ROWS, D_MODEL), lambda i: (i, 0))
    row = pl.BlockSpec((1, D_MODEL), lambda i: (0, 0))
    ins = [x, w_row, dy] + ([add] if has_add else [])
    return pl.pallas_call(
        body, name=name, grid=(SEQ // NORM_ROWS,),
        in_specs=[blk, row, blk] + ([blk] if has_add else []),
        out_specs=[blk, row],
        out_shape=[jax.ShapeDtypeStruct((SEQ, D_MODEL), dx_dtype), jax.ShapeDtypeStruct((1, D_MODEL), F32)],
        compiler_params=_cp("arbitrary"),
    )(*ins)


def _loss_fwd_bwd(y, target):
    def body(y_ref, t_ref, loss_ref, dy_ref):
        err = y_ref[...] - t_ref[...]
        dy_ref[...] = err * (1.0 / D_MODEL)

        @pl.when(pl.program_id(0) == 0)
        def _():
            loss_ref[...] = jnp.zeros_like(loss_ref)

        part = jnp.sum(jnp.sum(err * err, axis=1, keepdims=True) * (1.0 / D_MODEL), axis=0, keepdims=True)
        loss_ref[...] += 0.5 * jnp.broadcast_to(part, loss_ref.shape)

    blk = pl.BlockSpec((NORM_ROWS, D_MODEL), lambda i: (i, 0))
    return pl.pallas_call(
        body, name="loss", grid=(SEQ // NORM_ROWS,),
        in_specs=[blk, blk],
        out_specs=[pl.BlockSpec((1, 128), lambda i: (0, 0)), blk],
        out_shape=[jax.ShapeDtypeStruct((1, 128), F32), jax.ShapeDtypeStruct((SEQ, D_MODEL), F32)],
        compiler_params=_cp("arbitrary"),
    )(y, target)


def _make_shift(j):
    def down(x):
        row = lax.broadcasted_iota(jnp.int32, x.shape, 0)
        return jnp.where(row >= j, pltpu.roll(x, j, 0), 0.0)

    def up(x):
        n = x.shape[0]
        row = lax.broadcasted_iota(jnp.int32, x.shape, 0)
        return jnp.where(row < n - j, pltpu.roll(x, n - j, 0), 0.0)

    f = jax.custom_vjp(down)
    f.defvjp(lambda x: (down(x), None), lambda _, g: (up(g),))
    return f


_SHIFT = {j: _make_shift(j) for j in (1, 2, 3)}


def _causal_conv(x, taps):
    n = len(taps)
    acc = x * taps[n - 1]
    for k in range(n - 1):
        acc = acc + _SHIFT[n - 1 - k](x) * taps[k]
    return acc


def _tap_rows(w_ref, lanes=slice(None)):
    return tuple(w_ref[k:k + 1, lanes] for k in range(w_ref.shape[0]))


def _sigmoid(x):
    return 1.0 / (1.0 + jnp.exp(-x))


def _silu(x):
    return x * _sigmoid(x)


def _softplus(x):
    return jnp.maximum(x, 0.0) + jnp.log(1.0 + jnp.exp(-jnp.abs(x)))


def _gelu_tanh(x):
    return 0.5 * x * (1.0 + jnp.tanh(math.sqrt(2.0 / math.pi) * (x + 0.044715 * (x * x * x))))


def _dnconv_fn(x, taps):
    return _silu(_causal_conv(x, taps))


def _dnconv_fwd(proj, conv_w):
    def body(x_ref, w_ref, o_ref):
        o_ref[...] = _dnconv_fn(x_ref[...], _tap_rows(w_ref))

    return pl.pallas_call(
        body, name="dnconv_fwd", grid=(DN_QKV_BLKS,),
        in_specs=[pl.BlockSpec((SEQ, 128), lambda j: (0, DN_QKV_BLK0 + j)), pl.BlockSpec((4, 128), lambda j: (0, j))],
        out_specs=pl.BlockSpec((SEQ, 128), lambda j: (0, j)),
        out_shape=jax.ShapeDtypeStruct((SEQ, 1536), F32),
        compiler_params=_cp("parallel"),
    )(proj, conv_w)


def _dnconv_bwd(proj, conv_w, dc, dproj):
    def body(x_ref, w_ref, dc_ref, _, dx_ref, dw_ref):
        _, vjp = jax.vjp(_dnconv_fn, x_ref[...], _tap_rows(w_ref))
        dx, dw = vjp(dc_ref[...])
        dx_ref[...] = dx.astype(BF16)
        for k, row in enumerate(dw):
            dw_ref[k:k + 1, :] = row

    return pl.pallas_call(
        body, name="dnconv_bwd", grid=(DN_QKV_BLKS,),
        in_specs=[pl.BlockSpec((SEQ, 128), lambda j: (0, DN_QKV_BLK0 + j)), pl.BlockSpec((4, 128), lambda j: (0, j)),
                  pl.BlockSpec((SEQ, 128), lambda j: (0, j)), pl.BlockSpec(memory_space=pl.ANY)],
        out_specs=[pl.BlockSpec((SEQ, 128), lambda j: (0, DN_QKV_BLK0 + j)), pl.BlockSpec((4, 128), lambda j: (0, j))],
        out_shape=[jax.ShapeDtypeStruct((SEQ, IN_PAD), BF16), jax.ShapeDtypeStruct((4, 1536), F32)],
        input_output_aliases={3: 0},
        compiler_params=_cp("parallel"),
    )(proj, conv_w, dc, dproj)


def _ffact_fn(pg, pu, wg, wu, bg, bu):
    return _gelu_tanh(_causal_conv(pg, wg) + bg) * (_causal_conv(pu, wu) + bu)


def _ffact_args(p_ref, w_ref, b_ref):
    g, u = slice(0, 128), slice(128, 256)
    return (p_ref[:, g].astype(F32), p_ref[:, u].astype(F32), _tap_rows(w_ref, g), _tap_rows(w_ref, u),
            b_ref[:, g], b_ref[:, u])


def _ffact_fwd(pre, conv_w, conv_b, exchanges=()):
    def body(p_ref, w_ref, b_ref, o_ref):
        o_ref[...] = _ffact_fn(*_ffact_args(p_ref, w_ref, b_ref)).astype(BF16)

    (act,), results = _hosted_call(
        body, name="ffact_fwd", steps=FF_BLKS,
        in_specs=[pl.BlockSpec((SEQ, 256), lambda j: (0, j)), pl.BlockSpec((3, 256), lambda j: (0, j)),
                  pl.BlockSpec((1, 256), lambda j: (0, j))],
        out_specs=[pl.BlockSpec((SEQ, 128), lambda j: (0, j))],
        out_shape=[jax.ShapeDtypeStruct((SEQ, D_FF), BF16)],
        scratch_shapes=[], operands=(pre, conv_w, conv_b), exchanges=exchanges)
    return act, results


def _ffact_bwd(pre, conv_w, conv_b, dact, exchanges=()):
    def body(p_ref, w_ref, b_ref, da_ref, dp_ref, dw_ref, db_ref):
        _, vjp = jax.vjp(_ffact_fn, *_ffact_args(p_ref, w_ref, b_ref))
        dpg, dpu, dwg, dwu, dbg, dbu = vjp(da_ref[...].astype(F32))
        dp_ref[:, 0:128] = dpg.astype(BF16)
        dp_ref[:, 128:256] = dpu.astype(BF16)
        for k in range(3):
            dw_ref[k:k + 1, 0:128] = dwg[k]
            dw_ref[k:k + 1, 128:256] = dwu[k]
        db_ref[:, 0:128] = dbg
        db_ref[:, 128:256] = dbu

    return _hosted_call(
        body, name="ffact_bwd", steps=FF_BLKS,
        in_specs=[pl.BlockSpec((SEQ, 256), lambda j: (0, j)), pl.BlockSpec((3, 256), lambda j: (0, j)),
                  pl.BlockSpec((1, 256), lambda j: (0, j)), pl.BlockSpec((SEQ, 128), lambda j: (0, j))],
        out_specs=[pl.BlockSpec((SEQ, 256), lambda j: (0, j)), pl.BlockSpec((3, 256), lambda j: (0, j)),
                   pl.BlockSpec((1, 256), lambda j: (0, j))],
        out_shape=[jax.ShapeDtypeStruct((SEQ, 2 * D_FF), BF16), jax.ShapeDtypeStruct((3, 2 * D_FF), F32),
                   jax.ShapeDtypeStruct((1, 2 * D_FF), F32)],
        scratch_shapes=[], operands=(pre, conv_w, conv_b, dact), exchanges=exchanges)


def _interleave_ff(t):
    lead = t.shape[:-1]
    return t.reshape(lead + (2, FF_BLKS, 128)).swapaxes(-3, -2).reshape(lead + (2 * D_FF,))


def _deinterleave_ff(t):
    lead = t.shape[:-1]
    return t.reshape(lead + (FF_BLKS, 2, 128)).swapaxes(-3, -2).reshape(lead + (2 * D_FF,))


def _rope_tables():
    inv = 1.0 / (ROPE_THETA ** (jnp.arange(0, HEAD_DIM, 2, dtype=F32) / HEAD_DIM))
    ang = jnp.arange(SEQ, dtype=F32)[:, None] * inv[None, :]
    cos = jnp.tile(jnp.cos(ang), (1, 4))
    sin = jnp.tile(jnp.sin(ang), (1, 4))
    sign = jnp.where((jnp.arange(128) % HEAD_DIM) < HEAD_DIM // 2, -1.0, 1.0).astype(F32)
    return cos, sin * sign[None, :]


def _rope(x, cos, sin_signed):
    lane = lax.broadcasted_iota(jnp.int32, x.shape, 1)
    partner = jnp.where((lane % HEAD_DIM) < HEAD_DIM // 2, pltpu.roll(x, 128 - HEAD_DIM // 2, 1),
                        pltpu.roll(x, HEAD_DIM // 2, 1))
    return x * cos + partner * sin_signed


def _pairs_from_qkv(t):
    lead = t.shape[:-1]
    return t.reshape(lead + (3, N_PAIR, 128)).swapaxes(-3, -2).reshape(lead + (QKV_W,))


def _qkv_from_pairs(t):
    lead = t.shape[:-1]
    return t.reshape(lead + (N_PAIR, 3, 128)).swapaxes(-3, -2).reshape(lead + (QKV_W,))


def _head_masks():
    lane = lax.broadcasted_iota(jnp.int32, (1, 128), 1)
    return [(lane // HEAD_DIM) == h for h in range(2)]


def _both_heads(x):
    return jnp.concatenate([jnp.where(hm, x, 0.0)[None] for hm in _head_masks()], axis=0)


def _block_keys(branch, k_s, v_s, rows, prows, has_prev):
    a = lax.broadcasted_iota(jnp.int32, (ATTN_BLK, ATTN_BLK), 0)
    c = lax.broadcasted_iota(jnp.int32, (ATTN_BLK, ATTN_BLK), 1)
    keys, values, mask = k_s[rows, :], v_s[rows, :], c <= a
    if SEGMENT_BLOCKS[branch] > 1:
        keys = jnp.concatenate([k_s[prows, :], keys], axis=0)
        values = jnp.concatenate([v_s[prows, :], values], axis=0)
        mask = jnp.concatenate([(c >= a) & has_prev, mask], axis=1)
    twice = lambda t: jnp.broadcast_to(t[None], (2,) + t.shape)
    return twice(keys), twice(values), mask


def _block_rows(branch, t):
    d, per_seg = DILATIONS[branch], SEGMENT_BLOCKS[branch]
    if d == 1:
        start = pl.multiple_of(t * ATTN_BLK, ATTN_BLK)
        prev = pl.multiple_of(jnp.maximum(t - 1, 0) * ATTN_BLK, ATTN_BLK)
        return pl.ds(start, ATTN_BLK), pl.ds(prev, ATTN_BLK), t > 0
    r, n = t // per_seg, t % per_seg
    start = n * (ATTN_BLK * d) + r
    prev = jnp.maximum(n - 1, 0) * (ATTN_BLK * d) + r
    return pl.ds(start, ATTN_BLK, stride=d), pl.ds(prev, ATTN_BLK, stride=d), n > 0


def _attn_fwd(proj, cos, sin_signed, exchanges=()):
    scale = HEAD_DIM ** -0.5

    def body(qkv_ref, cos_ref, sin_ref, out_ref, lse_ref, q_s, k_s, v_s, *branch_s):
        o_s, l_s = branch_s[:3], branch_s[3:]
        q_s[...] = _rope(qkv_ref[:, 0:128], cos_ref[...], sin_ref[...])
        k_s[...] = _rope(qkv_ref[:, 128:256], cos_ref[...], sin_ref[...])
        v_s[...] = qkv_ref[:, 256:384]
        heads = _head_masks()
        for branch in range(3):
            def block(t, carry, branch=branch):
                rows, prows, has_prev = _block_rows(branch, t)
                keys, values, mask = _block_keys(branch, k_s, v_s, rows, prows, has_prev)
                s = jnp.where(mask, BMM_NT(_both_heads(q_s[rows, :]), keys) * scale, NEG)
                m = jnp.max(s, axis=2, keepdims=True)
                e = jnp.exp(s - m)
                l = jnp.sum(e, axis=2, keepdims=True)
                o = BMM(e, values) / l
                lse_b = m + jnp.log(l)
                o_s[branch][rows, :] = jnp.where(heads[0], o[0], o[1])
                l_s[branch][rows, :] = jnp.where(heads[0], lse_b[0], lse_b[1])
                return carry

            lax.fori_loop(0, N_BLK, block, 0, unroll=4)
        l0, l1, l2 = l_s[0][...], l_s[1][...], l_s[2][...]
        m = jnp.maximum(jnp.maximum(l0, l1), l2)
        w0, w1, w2 = jnp.exp(l0 - m), jnp.exp(l1 - m), jnp.exp(l2 - m)
        den = w0 + w1 + w2
        out_ref[...] = (w0 * o_s[0][...] + w1 * o_s[1][...] + w2 * o_s[2][...]) / den
        lse_ref[...] = m + jnp.log(den)

    tab = pl.BlockSpec((SEQ, 128), lambda j: (0, 0))
    col = pl.BlockSpec((SEQ, 128), lambda j: (0, j))
    return _hosted_call(
        body, name="attn_fwd", steps=N_PAIR,
        in_specs=[pl.BlockSpec((SEQ, 384), lambda j: (0, j)), tab, tab],
        out_specs=[col, col],
        out_shape=[jax.ShapeDtypeStruct((SEQ, 2 * ATTN_W), F32), jax.ShapeDtypeStruct((SEQ, ATTN_W), F32)],
        scratch_shapes=[pltpu.VMEM((SEQ, 128), F32)] * 9,
        operands=(proj, cos, sin_signed), exchanges=exchanges)


def _attn_bwd(proj, cos, sin_signed, cat, lse, dcat, dproj, exchanges=()):
    scale = HEAD_DIM ** -0.5

    def body(qkv_ref, cos_ref, sin_ref, out_ref, lse_ref, do_ref, _, dqkv_ref, q_s, k_s, v_s, dq_s, dk_s, dv_s,
             dod_s):
        q_s[...] = _rope(qkv_ref[:, 0:128], cos_ref[...], sin_ref[...])
        k_s[...] = _rope(qkv_ref[:, 128:256], cos_ref[...], sin_ref[...])
        v_s[...] = qkv_ref[:, 256:384]
        dq_s[...] = jnp.zeros_like(dq_s)
        dk_s[...] = jnp.zeros_like(dk_s)
        dv_s[...] = jnp.zeros_like(dv_s)
        dod_s[...] = do_ref[...] * out_ref[...]
        heads = _head_masks()
        for branch in range(3):
            def block(t, carry, branch=branch):
                rows, prows, has_prev = _block_rows(branch, t)
                keys, values, mask = _block_keys(branch, k_s, v_s, rows, prows, has_prev)
                q2, do2 = _both_heads(q_s[rows, :]), _both_heads(do_ref[rows, :])
                lse_b, dod = lse_ref[rows, :], dod_s[rows, :]
                lse2 = jnp.concatenate(
                    [jnp.max(jnp.where(hm, lse_b, NEG), axis=1, keepdims=True)[None] for hm in heads], axis=0)
                delta = jnp.concatenate(
                    [jnp.sum(jnp.where(hm, dod, 0.0), axis=1, keepdims=True)[None] for hm in heads], axis=0)
                p = jnp.exp(jnp.where(mask, BMM_NT(q2, keys) * scale, NEG) - lse2)
                ds = p * (BMM_NT(do2, values) - delta) * scale
                dq = BMM(ds, keys)
                dk = BMM_TN(ds, q2)
                dv = BMM_TN(p, do2)
                dk, dv = dk[0] + dk[1], dv[0] + dv[1]
                dq_s[rows, :] += jnp.where(heads[0], dq[0], dq[1])
                if SEGMENT_BLOCKS[branch] > 1:
                    dk_s[rows, :] += dk[ATTN_BLK:]
                    dv_s[rows, :] += dv[ATTN_BLK:]

                    @pl.when(has_prev)
                    def _():
                        dk_s[prows, :] += dk[:ATTN_BLK]
                        dv_s[prows, :] += dv[:ATTN_BLK]
                else:
                    dk_s[rows, :] += dk
                    dv_s[rows, :] += dv
                return carry

            lax.fori_loop(0, N_BLK, block, 0, unroll=4)
        dqkv_ref[:, 0:128] = _rope(dq_s[...], cos_ref[...], -sin_ref[...]).astype(BF16)
        dqkv_ref[:, 128:256] = _rope(dk_s[...], cos_ref[...], -sin_ref[...]).astype(BF16)
        dqkv_ref[:, 256:384] = dv_s[...].astype(BF16)

    tab = pl.BlockSpec((SEQ, 128), lambda j: (0, 0))
    col = pl.BlockSpec((SEQ, 128), lambda j: (0, j))
    qkv = pl.BlockSpec((SEQ, 384), lambda j: (0, j))
    (dproj,), results = _hosted_call(
        body, name="attn_bwd", steps=N_PAIR,
        in_specs=[qkv, tab, tab, col, col, col, pl.BlockSpec(memory_space=pl.ANY)],
        out_specs=[qkv],
        out_shape=[jax.ShapeDtypeStruct((SEQ, IN_PAD), BF16)],
        scratch_shapes=[pltpu.VMEM((SEQ, 128), F32)] * 7,
        operands=(proj, cos, sin_signed, cat, lse, dcat, dproj), exchanges=exchanges, aliases={6: 0})
    return dproj, results


def _bdot(a, b, dims, precision=None):
    if precision is None:
        a = a.astype(BF16)
        b = b.astype(BF16)
    return lax.dot_general(a, b, (dims, ((0,), (0,))), preferred_element_type=F32, precision=precision)


def _make_bmm(precision):
    @jax.custom_vjp
    def nn(a, b):
        return _bdot(a, b, ((2,), (1,)), precision)

    @jax.custom_vjp
    def nt(a, b):
        return _bdot(a, b, ((2,), (2,)), precision)

    @jax.custom_vjp
    def tn(a, b):
        return _bdot(a, b, ((1,), (1,)), precision)

    nn.defvjp(lambda a, b: (nn(a, b), (a, b)), lambda r, g: (nt(g, r[1]), tn(r[0], g)))
    nt.defvjp(lambda a, b: (nt(a, b), (a, b)), lambda r, g: (nn(g, r[1]), tn(g, r[0])))
    tn.defvjp(lambda a, b: (tn(a, b), (a, b)), lambda r, g: (nt(r[1], g), nn(r[0], g)))
    return nn, nt, tn


BMM, BMM_NT, BMM_TN = _make_bmm(None)
BMM3, BMM3_NT, BMM3_TN = _make_bmm(lax.Precision.HIGH)
MM3, _, _ = _make_mm(lax.Precision.HIGH)


def _head_lanes(t, off):
    lane = lax.broadcasted_iota(jnp.int32, (1, 128), 1)
    return jnp.concatenate(
        [jnp.sum(t * (lane == off + h).astype(F32), axis=1, keepdims=True)[None] for h in range(NDH)], axis=0)


@jax.custom_vjp
def _unit_lower_inverse(a_mat):
    c = a_mat.shape[1]
    eye = (lax.broadcasted_iota(jnp.int32, (c, c), 0) == lax.broadcasted_iota(jnp.int32, (c, c), 1)).astype(F32)
    power = -a_mat
    t_inv = eye + power
    for _ in range(5):
        power = BMM3(power, power)
        t_inv = t_inv + BMM3(t_inv, power)
    return t_inv


def _unit_lower_inverse_fwd(a_mat):
    t_inv = _unit_lower_inverse(a_mat)
    return t_inv, t_inv


def _unit_lower_inverse_bwd(t_inv, d_inv):
    return (-BMM3_NT(BMM3_TN(t_inv, d_inv), t_inv),)


_unit_lower_inverse.defvjp(_unit_lower_inverse_fwd, _unit_lower_inverse_bwd)


DN_STEP_CHUNKS = 4
DN_STEP_ROWS = DN_STEP_CHUNKS * CH
DN_STEPS = NCH // DN_STEP_CHUNKS
DN_BATCH = DN_STEP_CHUNKS * NDH


def _delta_chunks(qr, kr, vr, z, tail, alog_row, dt_row, nw, state):
    c = qr.shape[1]
    tails = [tail[CH * n:CH * (n + 1)] for n in range(DN_STEP_CHUNKS)]
    per_chunk = lambda t: jnp.concatenate([t] * DN_STEP_CHUNKS, axis=0)
    beta = _sigmoid(jnp.concatenate([_head_lanes(t, 0) for t in tails], axis=0))
    a_raw = jnp.concatenate([_head_lanes(t, NDH) for t in tails], axis=0)
    g = -jnp.exp(per_chunk(_head_lanes(alog_row, 0))) * _softplus(a_raw + per_chunk(_head_lanes(dt_row, 0)))

    q = qr * lax.rsqrt(jnp.sum(qr * qr, axis=2, keepdims=True) + EPS) * (128 ** -0.5)
    k = kr * lax.rsqrt(jnp.sum(kr * kr, axis=2, keepdims=True) + EPS)

    ri = lax.broadcasted_iota(jnp.int32, (c, c), 0)
    ci = lax.broadcasted_iota(jnp.int32, (c, c), 1)
    tril = ri >= ci
    lane = lax.broadcasted_iota(jnp.int32, (1, 128), 1)
    pick = [(lane == b).astype(F32) for b in range(DN_BATCH)]
    g_lanes = sum(g[b] * pick[b] for b in range(DN_BATCH))
    g_sums = MM3(tril.astype(F32), g_lanes)
    gc = jnp.concatenate([jnp.sum(g_sums * pick[b], axis=1, keepdims=True)[None] for b in range(DN_BATCH)],
                         axis=0)
    g_row = jnp.swapaxes(jnp.broadcast_to(gc, (DN_BATCH, c, c)), 1, 2)
    decay = jnp.where(tril, jnp.exp(jnp.where(tril, gc - g_row, 0.0)), 0.0)
    kb = k * beta
    t_inv = _unit_lower_inverse(jnp.where(ri > ci, BMM_NT(kb, k) * decay, 0.0))
    eg = jnp.exp(gc)
    u = BMM(t_inv, vr * beta)
    w = BMM(t_inv, kb * eg)
    qk = BMM_NT(q, k) * decay
    g_tot = jnp.sum(g, axis=1, keepdims=True)
    q_dec = q * eg
    k_dec = k * jnp.exp(g_tot - gc)
    outs = []
    for n in range(DN_STEP_CHUNKS):
        heads = slice(NDH * n, NDH * (n + 1))
        v_new = u[heads] - BMM(w[heads], state)
        outs.append(BMM(q_dec[heads], state) + BMM(qk[heads], v_new))
        state = state * jnp.exp(g_tot[heads]) + BMM_TN(k_dec[heads], v_new)
    o = jnp.concatenate(outs, axis=0)
    on = o * lax.rsqrt(jnp.mean(o * o, axis=2, keepdims=True) + EPS) * nw
    return on * _silu(z), state


def _heads(v, off=0):
    return jnp.concatenate([v[None, CH * n:CH * (n + 1), off + 128 * h:off + 128 * (h + 1)]
                            for n in range(DN_STEP_CHUNKS) for h in range(NDH)], axis=0)


def _unheads(t):
    return jnp.concatenate([jnp.concatenate([t[NDH * n + h] for h in range(NDH)], axis=1)
                            for n in range(DN_STEP_CHUNKS)], axis=0)


def _delta_fwd(c_qkv, proj, alog_row, dt_row, nw, cat, exchanges=()):
    def body(c_ref, z_ref, tail_ref, al_ref, dt_ref, nw_ref, _, y_ref, st_ref, state):
        @pl.when(pl.program_id(0) == 0)
        def _():
            state[...] = jnp.zeros_like(state)

        cv = c_ref[...]
        st_ref[0] = state[...]
        y, new_state = _delta_chunks(_heads(cv), _heads(cv, 512), _heads(cv, 1024), _heads(z_ref[...]), tail_ref[...],
                                     al_ref[...], dt_ref[...], nw_ref[...], state[...])
        y_ref[...] = _unheads(y)
        state[...] = new_state

    row = pl.BlockSpec((1, 128), lambda n: (0, 0))
    rows = DN_STEP_ROWS
    return _hosted_call(
        body, name="delta_fwd", steps=DN_STEPS,
        in_specs=[pl.BlockSpec((rows, 1536), lambda n: (n, 0)), pl.BlockSpec((rows, 512), lambda n: (n, DN_Z_COL // 512)),
                  pl.BlockSpec((rows, 128), lambda n: (n, DN_TAIL_BLK)), row, row, row, pl.BlockSpec(memory_space=pl.ANY)],
        out_specs=[pl.BlockSpec((rows, 512), lambda n: (n, 1)),
                   pl.BlockSpec((1, NDH, 128, 128), lambda n: (n, 0, 0, 0))],
        out_shape=[jax.ShapeDtypeStruct((SEQ, 2 * ATTN_W), F32), jax.ShapeDtypeStruct((DN_STEPS, NDH, 128, 128), F32)],
        scratch_shapes=[pltpu.VMEM((NDH, 128, 128), F32)],
        operands=(c_qkv, proj, proj, alog_row, dt_row, nw, cat), exchanges=exchanges, aliases={6: 0})


def _delta_bwd(c_qkv, proj, alog_row, dt_row, nw, states, dcat, exchanges=()):
    def body(c_ref, z_ref, tail_ref, al_ref, dt_ref, nw_ref, st_ref, dy_ref,
             dp_ref, dc_ref, dal_ref, ddt_ref, dnw_ref, dstate):
        @pl.when(pl.program_id(0) == 0)
        def _():
            dstate[...] = jnp.zeros_like(dstate)
            dal_ref[...] = jnp.zeros_like(dal_ref)
            ddt_ref[...] = jnp.zeros_like(ddt_ref)
            dnw_ref[...] = jnp.zeros_like(dnw_ref)

        cv = c_ref[...]
        _, vjp = jax.vjp(_delta_chunks, _heads(cv), _heads(cv, 512), _heads(cv, 1024), _heads(z_ref[...]),
                         tail_ref[...], al_ref[...], dt_ref[...], nw_ref[...], st_ref[0])
        dq, dk, dv, dz, dtail, dal, ddt, dnw, dst = vjp((_heads(dy_ref[...]), dstate[...]))
        dstate[...] = dst
        dc_ref[...] = jnp.concatenate([_unheads(dq), _unheads(dk), _unheads(dv)], axis=1)
        dp_ref[...] = jnp.concatenate([_unheads(dz), dtail, jnp.zeros((DN_STEP_ROWS, 128), F32)], axis=1).astype(BF16)
        dal_ref[...] += dal
        ddt_ref[...] += ddt
        dnw_ref[...] += dnw

    rev = lambda n: DN_STEPS - 1 - n
    row = pl.BlockSpec((1, 128), lambda n: (0, 0))
    rows = DN_STEP_ROWS
    return _hosted_call(
        body, name="delta_bwd", steps=DN_STEPS,
        in_specs=[pl.BlockSpec((rows, 1536), lambda n: (rev(n), 0)),
                  pl.BlockSpec((rows, 512), lambda n: (rev(n), DN_Z_COL // 512)),
                  pl.BlockSpec((rows, 128), lambda n: (rev(n), DN_TAIL_BLK)), row, row, row,
                  pl.BlockSpec((1, NDH, 128, 128), lambda n: (rev(n), 0, 0, 0)),
                  pl.BlockSpec((rows, 512), lambda n: (rev(n), 1))],
        out_specs=[pl.BlockSpec((rows, 768), lambda n: (rev(n), DN_Z_COL // 768)),
                   pl.BlockSpec((rows, 1536), lambda n: (rev(n), 0)), row, row, row],
        out_shape=[jax.ShapeDtypeStruct((SEQ, IN_PAD), BF16), jax.ShapeDtypeStruct((SEQ, 1536), F32)]
        + [jax.ShapeDtypeStruct((1, 128), F32)] * 3,
        scratch_shapes=[pltpu.VMEM((NDH, 128, 128), F32)],
        operands=(c_qkv, proj, proj, alog_row, dt_row, nw, states, dcat), exchanges=exchanges)


def _place():
    x, y, c = lax.axis_index("x"), lax.axis_index("y"), lax.axis_index("c")
    other_chips = [(1 - x, y), (x, 1 - y), (1 - x, 1 - y)]
    return x, y, c, other_chips


def _gather_exchange(shards):
    n = len(shards)

    def copies(ins, outs, sems):
        send_sems, recv_sems, local_sems = sems
        x, y, c, chips = _place()
        me, sibling = (x, y, c), (x, y, 1 - c)

        def copy(b, k, block, to, src=None):
            slot = outs[b].at[4 * block[0] + 2 * block[1] + block[2]]
            return pltpu.make_async_remote_copy(
                src_ref=slot if src is None else src, dst_ref=slot,
                send_sem=send_sems.at[b, k], recv_sem=recv_sems.at[b, k], device_id=to, device_id_type=MESH)

        mine = [pltpu.make_async_copy(ins[b], outs[b].at[4 * x + 2 * y + c], local_sems.at[b]) for b in range(n)]
        first = []
        for b in range(n):
            first.append(copy(b, 0, me, sibling, src=ins[b]))
            first += [copy(b, 1 + j, me, (*chip, c), src=ins[b]) for j, chip in enumerate(chips)]
        over_ici = [copy(b, 1 + j, (*chip, c), me) for b in range(n) for j, chip in enumerate(chips)]
        passed = [copy(b, 4 + j, (*chip, c), sibling) for b in range(n) for j, chip in enumerate(chips)]
        from_sibling = []
        for b in range(n):
            from_sibling.append(copy(b, 0, sibling, me))
            from_sibling += [copy(b, 4 + j, (*chip, 1 - c), me) for j, chip in enumerate(chips)]
        return mine, first, over_ici, passed, from_sibling

    def start(ins, outs, sems):
        mine, first, _, _, _ = copies(ins, outs, sems)
        for cp in mine + first:
            cp.start()

    def middle(ins, outs, sems):
        _, _, over_ici, passed, _ = copies(ins, outs, sems)
        for arrived, onward in zip(over_ici, passed):
            arrived.wait_recv()
            onward.start()

    def finish(ins, outs, sems):
        mine, first, _, passed, from_sibling = copies(ins, outs, sems)
        for cp in from_sibling:
            cp.wait_recv()
        for cp in first + passed:
            cp.wait_send()
        for cp in mine:
            cp.wait()

    return Exchange(shards, [jax.ShapeDtypeStruct((N_DEV,) + s.shape, s.dtype) for s in shards],
                    [pltpu.SemaphoreType.DMA((n, 7)), pltpu.SemaphoreType.DMA((n, 7)), pltpu.SemaphoreType.DMA((n,))],
                    start, middle, finish)


def _sibling_exchange(gs):
    n = len(gs)

    def copies(ins, outs, sems):
        send_sems, recv_sems = sems
        x, y, c, _ = _place()
        return [pltpu.make_async_remote_copy(
            src_ref=ins[b].at[2 * p + (1 - c)], dst_ref=outs[b].at[p],
            send_sem=send_sems.at[b, p], recv_sem=recv_sems.at[b, p],
            device_id=(x, y, 1 - c), device_id_type=MESH) for b in range(n) for p in range(4)]

    def start(ins, outs, sems):
        for cp in copies(ins, outs, sems):
            cp.start()

    def finish(ins, outs, sems):
        for cp in copies(ins, outs, sems):
            cp.wait()

    return Exchange(gs, [jax.ShapeDtypeStruct((4,) + g.shape[1:], g.dtype) for g in gs],
                    [pltpu.SemaphoreType.DMA((n, 4)), pltpu.SemaphoreType.DMA((n, 4))], start, None, finish)


def _chips_exchange(hs):
    n = len(hs)

    def copies(ins, outs, sems):
        send_sems, recv_sems, local_sems = sems
        x, y, c, chips = _place()
        my_chip = 2 * x + y
        local = [pltpu.make_async_copy(ins[b].at[my_chip], outs[b].at[my_chip], local_sems.at[b]) for b in range(n)]
        sends, arrivals = [], []
        for b in range(n):
            for k, (px, py) in enumerate(chips):
                peer = 2 * px + py
                sends.append(pltpu.make_async_remote_copy(
                    src_ref=ins[b].at[peer], dst_ref=outs[b].at[my_chip],
                    send_sem=send_sems.at[b, k], recv_sem=recv_sems.at[b, k],
                    device_id=(px, py, c), device_id_type=MESH))
                arrivals.append(pltpu.make_async_remote_copy(
                    src_ref=ins[b].at[peer], dst_ref=outs[b].at[peer],
                    send_sem=send_sems.at[b, k], recv_sem=recv_sems.at[b, k],
                    device_id=(px, py, c), device_id_type=MESH))
        return local, sends, arrivals

    def start(ins, outs, sems):
        local, sends, _ = copies(ins, outs, sems)
        for cp in local + sends:
            cp.start()

    def finish(ins, outs, sems):
        local, sends, arrivals = copies(ins, outs, sems)
        for cp in arrivals:
            cp.wait_recv()
        for cp in sends:
            cp.wait_send()
        for cp in local:
            cp.wait()

    return Exchange(hs, [jax.ShapeDtypeStruct(h.shape, h.dtype) for h in hs],
                    [pltpu.SemaphoreType.DMA((n, 3)), pltpu.SemaphoreType.DMA((n, 3)), pltpu.SemaphoreType.DMA((n,))],
                    start, None, finish)


def _run_exchange(exchange, name):
    n_in, n_out = len(exchange.operands), len(exchange.out_shapes)

    def body(*refs):
        ins, outs, sems = refs[:n_in], refs[n_in:n_in + n_out], refs[n_in + n_out:]
        exchange.start(ins, outs, sems)
        if exchange.middle is not None:
            exchange.middle(ins, outs, sems)
        exchange.finish(ins, outs, sems)

    return pl.pallas_call(
        body, name=name,
        in_specs=[HBM_SPEC] * n_in, out_specs=[HBM_SPEC] * n_out,
        out_shape=exchange.out_shapes, scratch_shapes=exchange.sems,
    )(*exchange.operands)


def _pair_add(g, r, core, name):
    _, nr, nc = g.shape
    tr = nr // 2 if nr % 32 == 0 else nr

    def body(core_ref, g_ref, r_ref, o_ref):
        o_ref[...] = (g_ref[...].astype(F32) + r_ref[...].astype(F32)).astype(BF16)

    return pl.pallas_call(
        body, name=name,
        grid_spec=pltpu.PrefetchScalarGridSpec(
            num_scalar_prefetch=1, grid=(4, nr // tr),
            in_specs=[pl.BlockSpec((1, tr, nc), lambda p, i, core: (2 * p + core[0], i, 0)),
                      pl.BlockSpec((1, tr, nc), lambda p, i, core: (p, i, 0))],
            out_specs=pl.BlockSpec((1, tr, nc), lambda p, i, core: (p, i, 0))),
        out_shape=jax.ShapeDtypeStruct(r.shape, BF16),
        compiler_params=_cp("parallel", "parallel"),
    )(core, g, r)


def _all_gather_sum_small(v):
    rows = v.shape[0]

    def body(x_ref, sum_ref, out_ref, send_sems, recv_sems, local_sem):
        x, y, c, chips = _place()
        me, sibling = (x, y, c), (x, y, 1 - c)

        def block(px, py, pc):
            return out_ref.at[pl.ds((4 * px + 2 * py + pc) * rows, rows), :]

        def copy(k, blk, to, src=None):
            return pltpu.make_async_remote_copy(
                src_ref=block(*blk) if src is None else src, dst_ref=block(*blk),
                send_sem=send_sems.at[k], recv_sem=recv_sems.at[k], device_id=to, device_id_type=MESH)

        mine = pltpu.make_async_copy(x_ref, block(*me), local_sem)
        mine.start()
        first = [copy(0, me, sibling, src=x_ref)]
        first += [copy(1 + j, me, (*chip, c), src=x_ref) for j, chip in enumerate(chips)]
        for cp in first:
            cp.start()
        passed = [copy(4 + j, (*chip, c), sibling) for j, chip in enumerate(chips)]
        for j, chip in enumerate(chips):
            copy(1 + j, (*chip, c), me).wait_recv()
            passed[j].start()
        copy(0, sibling, me).wait_recv()
        for j, chip in enumerate(chips):
            copy(4 + j, (*chip, 1 - c), me).wait_recv()
        for cp in first + passed:
            cp.wait_send()
        mine.wait()
        total = out_ref[pl.ds(0, rows), :]
        for d in range(1, N_DEV):
            total = total + out_ref[pl.ds(d * rows, rows), :]
        sum_ref[...] = total

    vm = pl.BlockSpec(memory_space=pltpu.VMEM)
    return pl.pallas_call(
        body, name="small_all_reduce",
        in_specs=[vm], out_specs=[vm],
        out_shape=[jax.ShapeDtypeStruct((rows, 128), F32)],
        scratch_shapes=[pltpu.VMEM((N_DEV * rows, 128), F32), pltpu.SemaphoreType.DMA((7,)),
                        pltpu.SemaphoreType.DMA((7,)), pltpu.SemaphoreType.DMA],
    )(v)[0]


def _adamw(w, g, m, v):
    m = ADAM_B1 * m + (1.0 - ADAM_B1) * g
    v = ADAM_B2 * v + (1.0 - ADAM_B2) * (g * g)
    m_hat = m / (1.0 - ADAM_B1 ** ADAM_STEP)
    v_hat = v / (1.0 - ADAM_B2 ** ADAM_STEP)
    delta = -ADAM_LR * (m_hat / (jnp.sqrt(v_hat) + ADAM_EPS) + ADAM_WD * w)
    return delta, m, v


ADAM_TILE = dict(w_in=(IN_COLS // N_DEV, 256), w_out=(128, D_MODEL), ffn_w_in=(176, D_MODEL), ffn_w_out=(176, D_MODEL))


def _sum_chips(p):
    p = p.astype(F32)
    return (p[0] + p[1]) + (p[2] + p[3])


def _adamw_sharded(parts, w, m, v, tile, name):
    nl, nr, nc = w.shape
    tr, tc = tile

    def body(*refs):
        p_refs, (w_ref, m_ref, v_ref, g_ref, d_ref, nm_ref, nv_ref) = refs[:nl], refs[nl:]
        layer = pl.program_id(0)
        p = p_refs[0][...]
        for l in range(1, nl):
            p = jnp.where(layer == l, p_refs[l][...], p)
        g = _sum_chips(p)
        delta, nm, nv = _adamw(w_ref[0], g, m_ref[0], v_ref[0])
        g_ref[0] = g
        d_ref[0] = delta
        nm_ref[0] = nm
        nv_ref[0] = nv

    blk = pl.BlockSpec((1, tr, tc), lambda l, i, j: (l, i, j))
    return pl.pallas_call(
        body, name=name, grid=(nl, nr // tr, nc // tc),
        in_specs=[pl.BlockSpec((4, tr, tc), lambda l, i, j: (0, i, j))] * nl + [blk, blk, blk],
        out_specs=[blk] * 4,
        out_shape=[jax.ShapeDtypeStruct(w.shape, F32)] * 4,
        compiler_params=_cp("parallel", "parallel", "parallel"),
    )(*parts, w, m, v)


def _adamw_small(g, w, m, v):
    def body(g_ref, w_ref, m_ref, v_ref, d_ref, nm_ref, nv_ref):
        delta, nm, nv = _adamw(w_ref[...], g_ref[...], m_ref[...], v_ref[...])
        d_ref[...] = delta
        nm_ref[...] = nm
        nv_ref[...] = nv

    return pl.pallas_call(
        body, name="adamw_small",
        out_shape=[jax.ShapeDtypeStruct(g.shape, F32)] * 3,
    )(g, w, m, v)


def _pack(arrays, rows):
    flat = jnp.concatenate([a.reshape(-1).astype(F32) for a in arrays])
    return jnp.pad(flat, (0, rows * 128 - flat.shape[0])).reshape(rows, 128)


def _unpack(packed, shapes):
    flat = packed.reshape(-1)
    out, off = [], 0
    for s in shapes:
        n = math.prod(s)
        out.append(flat[off:off + n].reshape(s))
        off += n
    return out


def _row(v, width=None):
    v = v.reshape(1, -1)
    return v if width is None else jnp.pad(v, ((0, 0), (0, width - v.shape[1])))


def _layer_fwd(x, wts, tables, hosted):
    h = _norm_fwd(x, wts["norm_pre_mix"], "norm_pre_mix")
    proj = _matmul(h, wts["w_in"], tb=True, tm=SEQ, tn=768, tk=1024, name="mm_proj")
    (cat, lse), got = _attn_fwd(proj, *tables, exchanges=hosted["attn"][0])
    hosted["attn"][1](got)
    c_qkv = _dnconv_fwd(proj, wts["dn_conv_w"])
    (cat, states), got = _delta_fwd(c_qkv, proj, wts["dn_a_log"], wts["dn_dt_bias"], wts["dn_norm_w"], cat,
                                    exchanges=hosted["delta"][0])
    hosted["delta"][1](got)
    mix = _matmul(cat, wts["w_out"], tm=512, tn=1024, tk=1024, name="mm_mix")
    x1 = _resnorm_fwd(x, mix, wts["norm_post_mix"], "norm_post_mix")
    h2 = _norm_fwd(x1, wts["norm_pre_ffn"], "norm_pre_ffn")
    pre = _matmul(h2, wts["ffn_w_in"], tb=True, tm=SEQ, tn=512, tk=1024, name="mm_ffn_in", out_dtype=BF16)
    act, got = _ffact_fwd(pre, wts["ffn_conv_w"], wts["ffn_conv_b"], exchanges=hosted["ffact"][0])
    hosted["ffact"][1](got)
    f = _matmul(act, wts["ffn_w_out"], tm=512, tn=1024, tk=D_FF, name="mm_ffn_out")
    x2 = _resnorm_fwd(x1, f, wts["norm_post_ffn"], "norm_post_ffn")
    saved = dict(x=x, h=h, proj=proj, lse=lse, c_qkv=c_qkv, states=states, cat=cat, mix=mix, x1=x1, h2=h2, pre=pre,
                 act=act, f=f)
    return x2, saved


def _layer_bwd(dx2, wts, s, tables, ffact_exchanges=(), delta_exchanges=None, attn_exchanges=None):
    g = {}
    df, g["norm_post_ffn"] = _norm_bwd(s["f"], wts["norm_post_ffn"], dx2, None, "norm_post_ffn_bwd", BF16)
    dact = _matmul(df, wts["ffn_w_out"], tb=True, tm=SEQ, tn=1408, tk=1024, name="mm_dact", out_dtype=BF16)
    g["ffn_w_out"] = _matmul(s["act"], df, ta=True, tm=1408, tn=512, tk=SEQ, name="mm_dw_ffn_out", out_dtype=BF16)
    (dpre, g["ffn_conv_w"], g["ffn_conv_b"]), got = _ffact_bwd(s["pre"], wts["ffn_conv_w"], wts["ffn_conv_b"], dact,
                                                               exchanges=ffact_exchanges)
    dh2 = _matmul(dpre, wts["ffn_w_in"], tm=1024, tn=1024, tk=1408, name="mm_dh2")
    g["ffn_w_in"] = _matmul(dpre, s["h2"], ta=True, tm=512, tn=1024, tk=SEQ, name="mm_dw_ffn_in", out_dtype=BF16)
    dx1, g["norm_pre_ffn"] = _norm_bwd(s["x1"], wts["norm_pre_ffn"], dh2, dx2, "norm_pre_ffn_bwd")
    dmix, g["norm_post_mix"] = _norm_bwd(s["mix"], wts["norm_post_mix"], dx1, None, "norm_post_mix_bwd", BF16)
    dcat = _matmul(dmix, wts["w_out"], tb=True, tm=SEQ, tn=512, tk=1024, name="mm_dcat")
    g["w_out"] = _matmul(s["cat"], dmix, ta=True, tm=1024, tn=512, tk=SEQ, name="mm_dw_out", out_dtype=BF16)
    (dproj, dc, g["dn_a_log"], g["dn_dt_bias"], g["dn_norm_w"]), got = _delta_bwd(
        s["c_qkv"], s["proj"], wts["dn_a_log"], wts["dn_dt_bias"], wts["dn_norm_w"], s["states"], dcat,
        exchanges=delta_exchanges(g, got) if delta_exchanges is not None else ())
    dproj, got = _attn_bwd(s["proj"], *tables, s["cat"], s["lse"], dcat, dproj,
                           exchanges=attn_exchanges(got) if attn_exchanges is not None else ())
    dproj, g["dn_conv_w"] = _dnconv_bwd(s["proj"], wts["dn_conv_w"], dc, dproj)
    dh = _matmul(dproj, wts["w_in"], tm=1024, tn=1024, tk=1280, name="mm_dh")
    g["w_in"] = _matmul(dproj, s["h"], ta=True, tm=768, tn=1024, tk=SEQ, name="mm_dw_in", out_dtype=BF16)
    dx, g["norm_pre_mix"] = _norm_bwd(s["x"], wts["norm_pre_mix"], dh, dx1, "norm_pre_mix_bwd")
    return dx, g, got


BIG = ("w_in", "w_out", "ffn_w_in", "ffn_w_out")
COLUMN_SHARDED = ("w_in", "ffn_w_in")
SMALL_SHARDED = ("dn_conv_w", "ffn_conv_w")
REPLICATED = ("dn_a_log", "dn_dt_bias", "dn_norm_w", "ffn_conv_b", "norm_pre_mix", "norm_post_mix", "norm_pre_ffn",
              "norm_post_ffn")
WEIGHTS = ("w_in", "dn_conv_w", "dn_a_log", "dn_dt_bias", "dn_norm_w", "w_out", "ffn_w_in", "ffn_conv_w", "ffn_conv_b",
           "ffn_w_out", "norm_pre_mix", "norm_post_mix", "norm_pre_ffn", "norm_post_ffn")
FULL_SHAPE = dict(dn_conv_w=(DEPTH, 4, 1536), ffn_conv_w=(DEPTH, 3, 2 * D_FF), dn_a_log=(DEPTH, NDH),
                  dn_dt_bias=(DEPTH, NDH), dn_norm_w=(DEPTH, 128), ffn_conv_b=(DEPTH, 2 * D_FF),
                  norm_pre_mix=(DEPTH, D_MODEL), norm_post_mix=(DEPTH, D_MODEL), norm_pre_ffn=(DEPTH, D_MODEL),
                  norm_post_ffn=(DEPTH, D_MODEL))
SMALL_GRAD_ORDER = REPLICATED + SMALL_SHARDED
SMALL_GRAD_ROWS = 520
SMALL_W_ROWS = 48
SMALL_ADAM_ROWS = 200


def _w_in_rows_to_kernel_order(t):
    qkv = t[:QKV_W].reshape(3, N_PAIR, 128, -1).swapaxes(0, 1).reshape(QKV_W, -1)
    return jnp.pad(jnp.concatenate([qkv, t[QKV_W:]], axis=0), ((0, IN_PAD - IN_COLS), (0, 0)))


def _w_in_rows_from_kernel_order(t):
    qkv = t[:QKV_W].reshape(N_PAIR, 3, 128, -1).swapaxes(0, 1).reshape(QKV_W, -1)
    return jnp.concatenate([qkv, t[QKV_W:IN_COLS]], axis=0)


def _interleave_ff_rows(t):
    return t.reshape(2, FF_BLKS, 128, -1).swapaxes(0, 1).reshape(2 * D_FF, -1)


def _deinterleave_ff_rows(t):
    return t.reshape(FF_BLKS, 2, 128, -1).swapaxes(0, 1).reshape(2 * D_FF, -1)


def kernel(x, w_in, dn_conv_w, dn_a_log, dn_dt_bias, dn_norm_w, w_out, ffn_w_in, ffn_conv_w, ffn_conv_b, ffn_w_out, norm_pre_mix, norm_post_mix, norm_pre_ffn, norm_post_ffn, loss_target, m_w_in, m_dn_conv_w, m_dn_a_log, m_dn_dt_bias, m_dn_norm_w, m_w_out, m_ffn_w_in, m_ffn_conv_w, m_ffn_conv_b, m_ffn_w_out, m_norm_pre_mix, m_norm_post_mix, m_norm_pre_ffn, m_norm_post_ffn, v_w_in, v_dn_conv_w, v_dn_a_log, v_dn_dt_bias, v_dn_norm_w, v_w_out, v_ffn_w_in, v_ffn_conv_w, v_ffn_conv_b, v_ffn_w_out, v_norm_pre_mix, v_norm_post_mix, v_norm_pre_ffn, v_norm_post_ffn):
    local = dict(w_in=w_in, dn_conv_w=dn_conv_w, dn_a_log=dn_a_log, dn_dt_bias=dn_dt_bias, dn_norm_w=dn_norm_w,
                 w_out=w_out, ffn_w_in=ffn_w_in, ffn_conv_w=ffn_conv_w, ffn_conv_b=ffn_conv_b, ffn_w_out=ffn_w_out,
                 norm_pre_mix=norm_pre_mix, norm_post_mix=norm_post_mix, norm_pre_ffn=norm_pre_ffn,
                 norm_post_ffn=norm_post_ffn)
    mom_m = dict(w_in=m_w_in, dn_conv_w=m_dn_conv_w, dn_a_log=m_dn_a_log, dn_dt_bias=m_dn_dt_bias,
                 dn_norm_w=m_dn_norm_w, w_out=m_w_out, ffn_w_in=m_ffn_w_in, ffn_conv_w=m_ffn_conv_w,
                 ffn_conv_b=m_ffn_conv_b, ffn_w_out=m_ffn_w_out, norm_pre_mix=m_norm_pre_mix,
                 norm_post_mix=m_norm_post_mix, norm_pre_ffn=m_norm_pre_ffn, norm_post_ffn=m_norm_post_ffn)
    mom_v = dict(w_in=v_w_in, dn_conv_w=v_dn_conv_w, dn_a_log=v_dn_a_log, dn_dt_bias=v_dn_dt_bias,
                 dn_norm_w=v_dn_norm_w, w_out=v_w_out, ffn_w_in=v_ffn_w_in, ffn_conv_w=v_ffn_conv_w,
                 ffn_conv_b=v_ffn_conv_b, ffn_w_out=v_ffn_w_out, norm_pre_mix=v_norm_pre_mix,
                 norm_post_mix=v_norm_post_mix, norm_pre_ffn=v_norm_pre_ffn, norm_post_ffn=v_norm_post_ffn)
    dev = 4 * lax.axis_index("x") + 2 * lax.axis_index("y") + lax.axis_index("c")
    core = lax.axis_index("c").astype(jnp.int32).reshape(1)

    def shard(n, l):
        s = local[n].transpose(0, 2, 1) if n in COLUMN_SHARDED else local[n]
        return s[l].astype(BF16)

    def matrix(n, gathered):
        if n == "w_in":
            return _w_in_rows_to_kernel_order(gathered.reshape(IN_COLS, D_MODEL))
        if n == "ffn_w_in":
            return _interleave_ff_rows(gathered.reshape(2 * D_FF, D_MODEL))
        return gathered.reshape(-1, D_MODEL)

    small_w = _pack([dn_conv_w, ffn_conv_w], SMALL_W_ROWS)
    g_w_in0, g_small = _run_exchange(_gather_exchange([shard("w_in", 0), small_w]), "weights_all_gather")
    n_dn, n_ff = DEPTH * 4 * 192, DEPTH * 3 * 704
    sm = g_small.reshape(N_DEV, -1)
    full_dn_conv = sm[:, :n_dn].reshape(N_DEV, DEPTH, 4, 192).transpose(1, 2, 0, 3).reshape(DEPTH, 4, 1536)
    full_ff_conv = _interleave_ff(
        sm[:, n_dn:n_dn + n_ff].reshape(N_DEV, DEPTH, 3, 704).transpose(1, 2, 0, 3).reshape(DEPTH, 3, 2 * D_FF))

    def small_weights(l):
        wts = dict(dn_conv_w=full_dn_conv[l], ffn_conv_w=full_ff_conv[l], ffn_conv_b=_interleave_ff(_row(ffn_conv_b[l])),
                   dn_a_log=_row(dn_a_log[l], 128), dn_dt_bias=_row(dn_dt_bias[l], 128))
        for n in ("dn_norm_w", "norm_pre_mix", "norm_post_mix", "norm_pre_ffn", "norm_post_ffn"):
            wts[n] = _row(local[n][l])
        return wts

    weights = [small_weights(l) for l in range(DEPTH)]
    weights[0]["w_in"] = matrix("w_in", g_w_in0)

    def gather_behind(wanted):
        def deliver(got):
            for (n, l), g in zip(wanted, got[0]):
                weights[l][n] = matrix(n, g)

        return [_gather_exchange([shard(n, l) for n, l in wanted])], deliver

    nothing = ((), lambda got: None)

    tables = _rope_tables()
    act, saved0 = _layer_fwd(x[0], weights[0], tables, dict(
        attn=gather_behind([("ffn_w_in", 0)]), delta=gather_behind([("w_out", 0), ("ffn_w_out", 0)]),
        ffact=gather_behind([("w_in", 1)])))
    act, saved1 = _layer_fwd(act, weights[1], tables, dict(
        attn=gather_behind([("ffn_w_in", 1)]), delta=gather_behind([("w_out", 1), ("ffn_w_out", 1)]), ffact=nothing))
    loss_part, dact = _loss_fwd_bwd(act, loss_target[0])

    def to_devices(name, t):
        if name == "w_in":
            t = _w_in_rows_from_kernel_order(t)
        if name == "ffn_w_in":
            t = _deinterleave_ff_rows(t)
        return t.reshape(N_DEV, t.shape[0] // N_DEV, t.shape[1])

    def pair_sums(names, layer, to_dev, from_sibling):
        return [_pair_add(gd, r, core, "grads_pair_add_%s_%d" % (n, layer))
                for n, gd, r in zip(names, to_dev, from_sibling)]

    early = ("w_out", "ffn_w_in", "ffn_w_out")
    grads, parts, stash = [None] * DEPTH, {}, {}

    def delta_exchanges1(g, got_ffact):
        stash["early1"] = [to_devices(n, g[n]) for n in early]
        return [_sibling_exchange(stash["early1"])]

    def attn_exchanges1(got_delta):
        return [_chips_exchange(pair_sums(early, 1, stash["early1"], got_delta[0]))]

    dact, grads[1], got_attn = _layer_bwd(dact, weights[1], saved1, tables, (), delta_exchanges1, attn_exchanges1)
    for n, p in zip(early, got_attn[0]):
        parts[n, 1] = p
    w_in1 = [to_devices("w_in", grads[1]["w_in"])]

    def delta_exchanges0(g, got_ffact):
        stash["early0"] = [to_devices(n, g[n]) for n in early]
        return [_chips_exchange(pair_sums(("w_in",), 1, w_in1, got_ffact[0])), _sibling_exchange(stash["early0"])]

    def attn_exchanges0(got_delta):
        parts["w_in", 1], = got_delta[0]
        return [_chips_exchange(pair_sums(early, 0, stash["early0"], got_delta[1]))]

    dact, grads[0], got_attn = _layer_bwd(dact, weights[0], saved0, tables, [_sibling_exchange(w_in1)],
                                          delta_exchanges0, attn_exchanges0)
    for n, p in zip(early, got_attn[0]):
        parts[n, 0] = p
    grad_x = dact[None]
    last = [to_devices("w_in", grads[0]["w_in"])]
    from_sibling = _run_exchange(_sibling_exchange(last), "grads_to_sibling")
    parts["w_in", 0], = _run_exchange(_chips_exchange(pair_sums(("w_in",), 0, last, from_sibling)), "grads_to_chips")

    def small_grad(name):
        t = jnp.stack([grads[l][name] for l in range(DEPTH)])
        if name in ("dn_a_log", "dn_dt_bias"):
            t = t[:, 0, :NDH]
        if name in ("ffn_conv_w", "ffn_conv_b"):
            t = _deinterleave_ff(t)
        return t.reshape(FULL_SHAPE[name])

    small_part = _pack([small_grad(n) for n in SMALL_GRAD_ORDER] + [loss_part[0, :1]], SMALL_GRAD_ROWS)
    small_sum = _all_gather_sum_small(small_part)
    small_g = dict(zip(SMALL_GRAD_ORDER + ("loss",), _unpack(small_sum, [FULL_SHAPE[n] for n in SMALL_GRAD_ORDER] + [(1,)])))
    loss = small_g["loss"][0]
    small_g["dn_conv_w"] = lax.dynamic_slice_in_dim(small_g["dn_conv_w"], dev * 192, 192, axis=2)
    small_g["ffn_conv_w"] = lax.dynamic_slice_in_dim(small_g["ffn_conv_w"], dev * 704, 704, axis=2)

    out_g, out_d, out_m, out_v = {}, {}, {}, {}
    for n in BIG:
        turn = (lambda t: t.transpose(0, 2, 1)) if n in COLUMN_SHARDED else (lambda t: t)
        outs = _adamw_sharded([parts[n, l] for l in range(DEPTH)], turn(local[n]), turn(mom_m[n]), turn(mom_v[n]),
                              ADAM_TILE[n], "adamw_" + n)
        out_g[n], out_d[n], out_m[n], out_v[n] = [turn(t) for t in outs]
    shapes = [small_g[n].shape for n in SMALL_GRAD_ORDER]
    d_s, m_s, v_s = _adamw_small(_pack([small_g[n] for n in SMALL_GRAD_ORDER], SMALL_ADAM_ROWS),
                                 _pack([local[n] for n in SMALL_GRAD_ORDER], SMALL_ADAM_ROWS),
                                 _pack([mom_m[n] for n in SMALL_GRAD_ORDER], SMALL_ADAM_ROWS),
                                 _pack([mom_v[n] for n in SMALL_GRAD_ORDER], SMALL_ADAM_ROWS))
    for n, d, m, v in zip(SMALL_GRAD_ORDER, _unpack(d_s, shapes), _unpack(m_s, shapes), _unpack(v_s, shapes)):
        out_g[n], out_d[n], out_m[n], out_v[n] = small_g[n], d, m, v
    return (loss, grad_x, *[out_g[n] for n in WEIGHTS], *[out_d[n] for n in WEIGHTS],
            *[out_m[n] for n in WEIGHTS], *[out_v[n] for n in WEIGHTS])
```

```python
import functools
import math

import jax
import jax.numpy as jnp
from jax import lax
from jax.experimental import pallas as pl
from jax.experimental.pallas import tpu as pltpu

F32 = jnp.float32
BF16 = jnp.bfloat16
HI = lax.Precision.HIGHEST
MESH = pl.DeviceIdType.MESH

N_DEV = 8
SEQ = 2048
D_MODEL = 1024
DEPTH = 2
N_PAIR = 4
HEAD_DIM = 64
ATTN_W = 512
ATTN_BLK = 128
DILATIONS = (1, 4, 16)
SEGMENT_BLOCKS = (16, 4, 1)
N_BLK = SEQ // ATTN_BLK
NDH = 4
CH = 64
NCH = SEQ // CH
IN_COLS = 3592
IN_PAD = 3840
QKV_W = 3 * ATTN_W
DN_QKV_BLK0 = QKV_W // 128
DN_QKV_BLKS = 1536 // 128
DN_Z_COL = 3072
DN_TAIL_BLK = 3584 // 128
D_FF = 2816
FF_BLKS = D_FF // 128
EPS = 1e-6
NEG = -1e30
ROPE_THETA = 10000.0

ADAM_LR, ADAM_B1, ADAM_B2, ADAM_EPS, ADAM_WD, ADAM_STEP = 0.001, 0.9, 0.999, 1e-08, 0.01, 10

VMEM_LIMIT = 56 * 1024 * 1024


def _cp(*sem):
    return pltpu.CompilerParams(dimension_semantics=sem, vmem_limit_bytes=VMEM_LIMIT)


class Exchange:
    def __init__(self, operands, out_shapes, sems, start, middle, finish):
        self.operands, self.out_shapes, self.sems = list(operands), list(out_shapes), list(sems)
        self.start, self.middle, self.finish = start, middle, finish


HBM_SPEC = pl.BlockSpec(memory_space=pltpu.HBM)


def _hosted_call(body, *, name, steps, in_specs, out_specs, out_shape, scratch_shapes, operands, exchanges=(),
                 aliases=None):
    n_in, n_out, n_scr = len(in_specs), len(out_specs), len(scratch_shapes)

    def take(refs, pos, counts):
        groups = []
        for c in counts:
            groups.append(refs[pos:pos + c])
            pos += c
        return groups, pos

    def full_body(*refs):
        ins, pos = refs[:n_in], n_in
        ex_ins, pos = take(refs, pos, [len(e.operands) for e in exchanges])
        outs, pos = refs[pos:pos + n_out], pos + n_out
        ex_outs, pos = take(refs, pos, [len(e.out_shapes) for e in exchanges])
        scr, pos = refs[pos:pos + n_scr], pos + n_scr
        ex_sems, pos = take(refs, pos, [len(e.sems) for e in exchanges])
        step = pl.program_id(0)
        for e, a, b, s in zip(exchanges, ex_ins, ex_outs, ex_sems):
            pl.when(step == 0)(functools.partial(e.start, a, b, s))
            if e.middle is not None:
                pl.when(step == (3 * steps) // 4)(functools.partial(e.middle, a, b, s))
        body(*ins, *outs, *scr)
        for e, a, b, s in zip(exchanges, ex_ins, ex_outs, ex_sems):
            pl.when(step == steps - 1)(functools.partial(e.finish, a, b, s))

    n_ex_in = sum(len(e.operands) for e in exchanges)
    n_ex_out = sum(len(e.out_shapes) for e in exchanges)
    results = pl.pallas_call(
        full_body, name=name, grid=(steps,),
        in_specs=list(in_specs) + [HBM_SPEC] * n_ex_in,
        out_specs=list(out_specs) + [HBM_SPEC] * n_ex_out,
        out_shape=list(out_shape) + [s for e in exchanges for s in e.out_shapes],
        scratch_shapes=list(scratch_shapes) + [s for e in exchanges for s in e.sems],
        input_output_aliases=aliases or {},
        compiler_params=_cp("arbitrary"),
    )(*operands, *[a for e in exchanges for a in e.operands])
    ex_results, _ = take(results, n_out, [len(e.out_shapes) for e in exchanges])
    return results[:n_out], ex_results


def _dot(a, b, dims, precision=None):
    if precision is None:
        a = a.astype(BF16)
        b = b.astype(BF16)
    return lax.dot_general(a, b, (dims, ((), ())), preferred_element_type=F32, precision=precision)


def _make_mm(precision):
    @jax.custom_vjp
    def nn(a, b):
        return _dot(a, b, ((1,), (0,)), precision)

    @jax.custom_vjp
    def nt(a, b):
        return _dot(a, b, ((1,), (1,)), precision)

    @jax.custom_vjp
    def tn(a, b):
        return _dot(a, b, ((0,), (0,)), precision)

    nn.defvjp(lambda a, b: (nn(a, b), (a, b)), lambda r, g: (nt(g, r[1]), tn(r[0], g)))
    nt.defvjp(lambda a, b: (nt(a, b), (a, b)), lambda r, g: (nn(g, r[1]), tn(g, r[0])))
    tn.defvjp(lambda a, b: (tn(a, b), (a, b)), lambda r, g: (nt(r[1], g), nn(r[0], g)))
    return nn, nt, tn


MM, MM_NT, MM_TN = _make_mm(None)


def _matmul(a, b, *, ta=False, tb=False, tm, tn, tk, name, out_dtype=F32):
    (k_dim, m_dim) = a.shape if ta else a.shape[::-1]
    (n_dim, k2) = b.shape if tb else b.shape[::-1]
    assert k_dim == k2 and m_dim % tm == 0 and n_dim % tn == 0 and k_dim % tk == 0, (a.shape, b.shape, tm, tn, tk)
    nk = k_dim // tk
    dims = ((0 if ta else 1,), (1 if tb else 0,))

    def body(a_ref, b_ref, o_ref, *acc):
        p = _dot(a_ref[...], b_ref[...], dims)
        if nk == 1:
            o_ref[...] = p.astype(out_dtype)
            return
        acc_ref, k = acc[0], pl.program_id(2)

        @pl.when(k == 0)
        def _():
            acc_ref[...] = p

        @pl.when(k > 0)
        def _():
            acc_ref[...] += p

        @pl.when(k == nk - 1)
        def _():
            o_ref[...] = acc_ref[...].astype(out_dtype)

    a_spec = pl.BlockSpec((tk, tm), lambda i, j, k: (k, i)) if ta else pl.BlockSpec((tm, tk), lambda i, j, k: (i, k))
    b_spec = pl.BlockSpec((tn, tk), lambda i, j, k: (j, k)) if tb else pl.BlockSpec((tk, tn), lambda i, j, k: (k, j))
    return pl.pallas_call(
        body, name=name,
        grid=(m_dim // tm, n_dim // tn, nk),
        in_specs=[a_spec, b_spec],
        out_specs=pl.BlockSpec((tm, tn), lambda i, j, k: (i, j)),
        out_shape=jax.ShapeDtypeStruct((m_dim, n_dim), out_dtype),
        scratch_shapes=[pltpu.VMEM((tm, tn), F32)] if nk > 1 else [],
        compiler_params=_cp("parallel", "parallel", "arbitrary"),
    )(a, b)


NORM_ROWS = 256


def _rms(x, w):
    return x * lax.rsqrt(jnp.mean(x * x, axis=1, keepdims=True) + EPS) * w


def _norm_fwd(x, w_row, name, out_dtype=BF16):
    def body(x_ref, w_ref, o_ref):
        o_ref[...] = _rms(x_ref[...], w_ref[...]).astype(out_dtype)

    return pl.pallas_call(
        body, name=name, grid=(SEQ // NORM_ROWS,),
        in_specs=[pl.BlockSpec((NORM_ROWS, D_MODEL), lambda i: (i, 0)), pl.BlockSpec((1, D_MODEL), lambda i: (0, 0))],
        out_specs=pl.BlockSpec((NORM_ROWS, D_MODEL), lambda i: (i, 0)),
        out_shape=jax.ShapeDtypeStruct((SEQ, D_MODEL), out_dtype),
        compiler_params=_cp("parallel"),
    )(x, w_row)


def _resnorm_fwd(x, f, w_row, name):
    def body(x_ref, f_ref, w_ref, o_ref):
        o_ref[...] = x_ref[...] + _rms(f_ref[...], w_ref[...])

    blk = pl.BlockSpec((NORM_ROWS, D_MODEL), lambda i: (i, 0))
    return pl.pallas_call(
        body, name=name, grid=(SEQ // NORM_ROWS,),
        in_specs=[blk, blk, pl.BlockSpec((1, D_MODEL), lambda i: (0, 0))],
        out_specs=blk, out_shape=jax.ShapeDtypeStruct((SEQ, D_MODEL), F32),
        compiler_params=_cp("parallel"),
    )(x, f, w_row)


def _norm_bwd(x, w_row, dy, add, name, dx_dtype=F32):
    has_add = add is not None

    def body(*refs):
        if has_add:
            x_ref, w_ref, dy_ref, add_ref, dx_ref, dw_ref = refs
        else:
            x_ref, w_ref, dy_ref, dx_ref, dw_ref = refs
        _, vjp = jax.vjp(_rms, x_ref[...], w_ref[...])
        dx, dw = vjp(dy_ref[...])
        dx_ref[...] = (dx + add_ref[...] if has_add else dx).astype(dx_dtype)

        @pl.when(pl.program_id(0) == 0)
        def _():
            dw_ref[...] = jnp.zeros_like(dw_ref)

        dw_ref[...] += dw

    blk = pl.BlockSpec((NORM_ROWS, D_MODEL), lambda i: (i, 0))
    row = pl.BlockSpec((1, D_MODEL), lambda i: (0, 0))
    ins = [x, w_row, dy] + ([add] if has_add else [])
    return pl.pallas_call(
        body, name=name, grid=(SEQ // NORM_ROWS,),
        in_specs=[blk, row, blk] + ([blk] if has_add else []),
        out_specs=[blk, row],
        out_shape=[jax.ShapeDtypeStruct((SEQ, D_MODEL), dx_dtype), jax.ShapeDtypeStruct((1, D_MODEL), F32)],
        compiler_params=_cp("arbitrary"),
    )(*ins)


def _loss_fwd_bwd(y, target):
    def body(y_ref, t_ref, loss_ref, dy_ref):
        err = y_ref[...] - t_ref[...]
        dy_ref[...] = err * (1.0 / D_MODEL)

        @pl.when(pl.program_id(0) == 0)
        def _():
            loss_ref[...] = jnp.zeros_like(loss_ref)

        part = jnp.sum(jnp.sum(err * err, axis=1, keepdims=True) * (1.0 / D_MODEL), axis=0, keepdims=True)
        loss_ref[...] += 0.5 * jnp.broadcast_to(part, loss_ref.shape)

    blk = pl.BlockSpec((NORM_ROWS, D_MODEL), lambda i: (i, 0))
    return pl.pallas_call(
        body, name="loss", grid=(SEQ // NORM_ROWS,),
        in_specs=[blk, blk],
        out_specs=[pl.BlockSpec((1, 128), lambda i: (0, 0)), blk],
        out_shape=[jax.ShapeDtypeStruct((1, 128), F32), jax.ShapeDtypeStruct((SEQ, D_MODEL), F32)],
        compiler_params=_cp("arbitrary"),
    )(y, target)


def _make_shift(j):
    def down(x):
        row = lax.broadcasted_iota(jnp.int32, x.shape, 0)
        return jnp.where(row >= j, pltpu.roll(x, j, 0), 0.0)

    def up(x):
        n = x.shape[0]
        row = lax.broadcasted_iota(jnp.int32, x.shape, 0)
        return jnp.where(row < n - j, pltpu.roll(x, n - j, 0), 0.0)

    f = jax.custom_vjp(down)
    f.defvjp(lambda x: (down(x), None), lambda _, g: (up(g),))
    return f


_SHIFT = {j: _make_shift(j) for j in (1, 2, 3)}


def _causal_conv(x, taps):
    n = len(taps)
    acc = x * taps[n - 1]
    for k in range(n - 1):
        acc = acc + _SHIFT[n - 1 - k](x) * taps[k]
    return acc


def _tap_rows(w_ref, lanes=slice(None)):
    return tuple(w_ref[k:k + 1, lanes] for k in range(w_ref.shape[0]))


def _sigmoid(x):
    return 1.0 / (1.0 + jnp.exp(-x))


def _silu(x):
    return x * _sigmoid(x)


def _softplus(x):
    return jnp.maximum(x, 0.0) + jnp.log(1.0 + jnp.exp(-jnp.abs(x)))


def _gelu_tanh(x):
    return 0.5 * x * (1.0 + jnp.tanh(math.sqrt(2.0 / math.pi) * (x + 0.044715 * (x * x * x))))


def _dnconv_fn(x, taps):
    return _silu(_causal_conv(x, taps))


def _dnconv_fwd(proj, conv_w):
    def body(x_ref, w_ref, o_ref):
        o_ref[...] = _dnconv_fn(x_ref[...], _tap_rows(w_ref))

    return pl.pallas_call(
        body, name="dnconv_fwd", grid=(DN_QKV_BLKS,),
        in_specs=[pl.BlockSpec((SEQ, 128), lambda j: (0, DN_QKV_BLK0 + j)), pl.BlockSpec((4, 128), lambda j: (0, j))],
        out_specs=pl.BlockSpec((SEQ, 128), lambda j: (0, j)),
        out_shape=jax.ShapeDtypeStruct((SEQ, 1536), F32),
        compiler_params=_cp("parallel"),
    )(proj, conv_w)


def _dnconv_bwd(proj, conv_w, dc, dproj):
    def body(x_ref, w_ref, dc_ref, _, dx_ref, dw_ref):
        _, vjp = jax.vjp(_dnconv_fn, x_ref[...], _tap_rows(w_ref))
        dx, dw = vjp(dc_ref[...])
        dx_ref[...] = dx.astype(BF16)
        for k, row in enumerate(dw):
            dw_ref[k:k + 1, :] = row

    return pl.pallas_call(
        body, name="dnconv_bwd", grid=(DN_QKV_BLKS,),
        in_specs=[pl.BlockSpec((SEQ, 128), lambda j: (0, DN_QKV_BLK0 + j)), pl.BlockSpec((4, 128), lambda j: (0, j)),
                  pl.BlockSpec((SEQ, 128), lambda j: (0, j)), pl.BlockSpec(memory_space=pl.ANY)],
        out_specs=[pl.BlockSpec((SEQ, 128), lambda j: (0, DN_QKV_BLK0 + j)), pl.BlockSpec((4, 128), lambda j: (0, j))],
        out_shape=[jax.ShapeDtypeStruct((SEQ, IN_PAD), BF16), jax.ShapeDtypeStruct((4, 1536), F32)],
        input_output_aliases={3: 0},
        compiler_params=_cp("parallel"),
    )(proj, conv_w, dc, dproj)


def _ffact_fn(pg, pu, wg, wu, bg, bu):
    return _gelu_tanh(_causal_conv(pg, wg) + bg) * (_causal_conv(pu, wu) + bu)


def _ffact_args(p_ref, w_ref, b_ref):
    g, u = slice(0, 128), slice(128, 256)
    return (p_ref[:, g].astype(F32), p_ref[:, u].astype(F32), _tap_rows(w_ref, g), _tap_rows(w_ref, u),
            b_ref[:, g], b_ref[:, u])


def _ffact_fwd(pre, conv_w, conv_b, exchanges=()):
    def body(p_ref, w_ref, b_ref, o_ref):
        o_ref[...] = _ffact_fn(*_ffact_args(p_ref, w_ref, b_ref)).astype(BF16)

    (act,), results = _hosted_call(
        body, name="ffact_fwd", steps=FF_BLKS,
        in_specs=[pl.BlockSpec((SEQ, 256), lambda j: (0, j)), pl.BlockSpec((3, 256), lambda j: (0, j)),
                  pl.BlockSpec((1, 256), lambda j: (0, j))],
        out_specs=[pl.BlockSpec((SEQ, 128), lambda j: (0, j))],
        out_shape=[jax.ShapeDtypeStruct((SEQ, D_FF), BF16)],
        scratch_shapes=[], operands=(pre, conv_w, conv_b), exchanges=exchanges)
    return act, results


def _ffact_bwd(pre, conv_w, conv_b, dact, exchanges=()):
    def body(p_ref, w_ref, b_ref, da_ref, dp_ref, dw_ref, db_ref):
        _, vjp = jax.vjp(_ffact_fn, *_ffact_args(p_ref, w_ref, b_ref))
        dpg, dpu, dwg, dwu, dbg, dbu = vjp(da_ref[...].astype(F32))
        dp_ref[:, 0:128] = dpg.astype(BF16)
        dp_ref[:, 128:256] = dpu.astype(BF16)
        for k in range(3):
            dw_ref[k:k + 1, 0:128] = dwg[k]
            dw_ref[k:k + 1, 128:256] = dwu[k]
        db_ref[:, 0:128] = dbg
        db_ref[:, 128:256] = dbu

    return _hosted_call(
        body, name="ffact_bwd", steps=FF_BLKS,
        in_specs=[pl.BlockSpec((SEQ, 256), lambda j: (0, j)), pl.BlockSpec((3, 256), lambda j: (0, j)),
                  pl.BlockSpec((1, 256), lambda j: (0, j)), pl.BlockSpec((SEQ, 128), lambda j: (0, j))],
        out_specs=[pl.BlockSpec((SEQ, 256), lambda j: (0, j)), pl.BlockSpec((3, 256), lambda j: (0, j)),
                   pl.BlockSpec((1, 256), lambda j: (0, j))],
        out_shape=[jax.ShapeDtypeStruct((SEQ, 2 * D_FF), BF16), jax.ShapeDtypeStruct((3, 2 * D_FF), F32),
                   jax.ShapeDtypeStruct((1, 2 * D_FF), F32)],
        scratch_shapes=[], operands=(pre, conv_w, conv_b, dact), exchanges=exchanges)


def _interleave_ff(t):
    lead = t.shape[:-1]
    return t.reshape(lead + (2, FF_BLKS, 128)).swapaxes(-3, -2).reshape(lead + (2 * D_FF,))


def _deinterleave_ff(t):
    lead = t.shape[:-1]
    return t.reshape(lead + (FF_BLKS, 2, 128)).swapaxes(-3, -2).reshape(lead + (2 * D_FF,))


def _rope_tables():
    inv = 1.0 / (ROPE_THETA ** (jnp.arange(0, HEAD_DIM, 2, dtype=F32) / HEAD_DIM))
    ang = jnp.arange(SEQ, dtype=F32)[:, None] * inv[None, :]
    cos = jnp.tile(jnp.cos(ang), (1, 4))
    sin = jnp.tile(jnp.sin(ang), (1, 4))
    sign = jnp.where((jnp.arange(128) % HEAD_DIM) < HEAD_DIM // 2, -1.0, 1.0).astype(F32)
    return cos, sin * sign[None, :]


def _rope(x, cos, sin_signed):
    lane = lax.broadcasted_iota(jnp.int32, x.shape, 1)
    partner = jnp.where((lane % HEAD_DIM) < HEAD_DIM // 2, pltpu.roll(x, 128 - HEAD_DIM // 2, 1),
                        pltpu.roll(x, HEAD_DIM // 2, 1))
    return x * cos + partner * sin_signed


def _pairs_from_qkv(t):
    lead = t.shape[:-1]
    return t.reshape(lead + (3, N_PAIR, 128)).swapaxes(-3, -2).reshape(lead + (QKV_W,))


def _qkv_from_pairs(t):
    lead = t.shape[:-1]
    return t.reshape(lead + (N_PAIR, 3, 128)).swapaxes(-3, -2).reshape(lead + (QKV_W,))


def _head_masks():
    lane = lax.broadcasted_iota(jnp.int32, (1, 128), 1)
    return [(lane // HEAD_DIM) == h for h in range(2)]


def _both_heads(x):
    return jnp.concatenate([jnp.where(hm, x, 0.0)[None] for hm in _head_masks()], axis=0)


def _block_keys(branch, k_s, v_s, rows, prows, has_prev):
    a = lax.broadcasted_iota(jnp.int32, (ATTN_BLK, ATTN_BLK), 0)
    c = lax.broadcasted_iota(jnp.int32, (ATTN_BLK, ATTN_BLK), 1)
    keys, values, mask = k_s[rows, :], v_s[rows, :], c <= a
    if SEGMENT_BLOCKS[branch] > 1:
        keys = jnp.concatenate([k_s[prows, :], keys], axis=0)
        values = jnp.concatenate([v_s[prows, :], values], axis=0)
        mask = jnp.concatenate([(c >= a) & has_prev, mask], axis=1)
    twice = lambda t: jnp.broadcast_to(t[None], (2,) + t.shape)
    return twice(keys), twice(values), mask


def _block_rows(branch, t):
    d, per_seg = DILATIONS[branch], SEGMENT_BLOCKS[branch]
    if d == 1:
        start = pl.multiple_of(t * ATTN_BLK, ATTN_BLK)
        prev = pl.multiple_of(jnp.maximum(t - 1, 0) * ATTN_BLK, ATTN_BLK)
        return pl.ds(start, ATTN_BLK), pl.ds(prev, ATTN_BLK), t > 0
    r, n = t // per_seg, t % per_seg
    start = n * (ATTN_BLK * d) + r
    prev = jnp.maximum(n - 1, 0) * (ATTN_BLK * d) + r
    return pl.ds(start, ATTN_BLK, stride=d), pl.ds(prev, ATTN_BLK, stride=d), n > 0


def _attn_fwd(proj, cos, sin_signed, exchanges=()):
    scale = HEAD_DIM ** -0.5

    def body(qkv_ref, cos_ref, sin_ref, out_ref, lse_ref, q_s, k_s, v_s, *branch_s):
        o_s, l_s = branch_s[:3], branch_s[3:]
        q_s[...] = _rope(qkv_ref[:, 0:128], cos_ref[...], sin_ref[...])
        k_s[...] = _rope(qkv_ref[:, 128:256], cos_ref[...], sin_ref[...])
        v_s[...] = qkv_ref[:, 256:384]
        heads = _head_masks()
        for branch in range(3):
            def block(t, carry, branch=branch):
                rows, prows, has_prev = _block_rows(branch, t)
                keys, values, mask = _block_keys(branch, k_s, v_s, rows, prows, has_prev)
                s = jnp.where(mask, BMM_NT(_both_heads(q_s[rows, :]), keys) * scale, NEG)
                m = jnp.max(s, axis=2, keepdims=True)
                e = jnp.exp(s - m)
                l = jnp.sum(e, axis=2, keepdims=True)
                o = BMM(e, values) / l
                lse_b = m + jnp.log(l)
                o_s[branch][rows, :] = jnp.where(heads[0], o[0], o[1])
                l_s[branch][rows, :] = jnp.where(heads[0], lse_b[0], lse_b[1])
                return carry

            lax.fori_loop(0, N_BLK, block, 0, unroll=4)
        l0, l1, l2 = l_s[0][...], l_s[1][...], l_s[2][...]
        m = jnp.maximum(jnp.maximum(l0, l1), l2)
        w0, w1, w2 = jnp.exp(l0 - m), jnp.exp(l1 - m), jnp.exp(l2 - m)
        den = w0 + w1 + w2
        out_ref[...] = (w0 * o_s[0][...] + w1 * o_s[1][...] + w2 * o_s[2][...]) / den
        lse_ref[...] = m + jnp.log(den)

    tab = pl.BlockSpec((SEQ, 128), lambda j: (0, 0))
    col = pl.BlockSpec((SEQ, 128), lambda j: (0, j))
    return _hosted_call(
        body, name="attn_fwd", steps=N_PAIR,
        in_specs=[pl.BlockSpec((SEQ, 384), lambda j: (0, j)), tab, tab],
        out_specs=[col, col],
        out_shape=[jax.ShapeDtypeStruct((SEQ, 2 * ATTN_W), F32), jax.ShapeDtypeStruct((SEQ, ATTN_W), F32)],
        scratch_shapes=[pltpu.VMEM((SEQ, 128), F32)] * 9,
        operands=(proj, cos, sin_signed), exchanges=exchanges)


def _attn_bwd(proj, cos, sin_signed, cat, lse, dcat, dproj, exchanges=()):
    scale = HEAD_DIM ** -0.5

    def body(qkv_ref, cos_ref, sin_ref, out_ref, lse_ref, do_ref, _, dqkv_ref, q_s, k_s, v_s, dq_s, dk_s, dv_s,
             dod_s):
        q_s[...] = _rope(qkv_ref[:, 0:128], cos_ref[...], sin_ref[...])
        k_s[...] = _rope(qkv_ref[:, 128:256], cos_ref[...], sin_ref[...])
        v_s[...] = qkv_ref[:, 256:384]
        dq_s[...] = jnp.zeros_like(dq_s)
        dk_s[...] = jnp.zeros_like(dk_s)
        dv_s[...] = jnp.zeros_like(dv_s)
        dod_s[...] = do_ref[...] * out_ref[...]
        heads = _head_masks()
        for branch in range(3):
            def block(t, carry, branch=branch):
                rows, prows, has_prev = _block_rows(branch, t)
                keys, values, mask = _block_keys(branch, k_s, v_s, rows, prows, has_prev)
                q2, do2 = _both_heads(q_s[rows, :]), _both_heads(do_ref[rows, :])
                lse_b, dod = lse_ref[rows, :], dod_s[rows, :]
                lse2 = jnp.concatenate(
                    [jnp.max(jnp.where(hm, lse_b, NEG), axis=1, keepdims=True)[None] for hm in heads], axis=0)
                delta = jnp.concatenate(
                    [jnp.sum(jnp.where(hm, dod, 0.0), axis=1, keepdims=True)[None] for hm in heads], axis=0)
                p = jnp.exp(jnp.where(mask, BMM_NT(q2, keys) * scale, NEG) - lse2)
                ds = p * (BMM_NT(do2, values) - delta) * scale
                dq = BMM(ds, keys)
                dk = BMM_TN(ds, q2)
                dv = BMM_TN(p, do2)
                dk, dv = dk[0] + dk[1], dv[0] + dv[1]
                dq_s[rows, :] += jnp.where(heads[0], dq[0], dq[1])
                if SEGMENT_BLOCKS[branch] > 1:
                    dk_s[rows, :] += dk[ATTN_BLK:]
                    dv_s[rows, :] += dv[ATTN_BLK:]

                    @pl.when(has_prev)
                    def _():
                        dk_s[prows, :] += dk[:ATTN_BLK]
                        dv_s[prows, :] += dv[:ATTN_BLK]
                else:
                    dk_s[rows, :] += dk
                    dv_s[rows, :] += dv
                return carry

            lax.fori_loop(0, N_BLK, block, 0, unroll=4)
        dqkv_ref[:, 0:128] = _rope(dq_s[...], cos_ref[...], -sin_ref[...]).astype(BF16)
        dqkv_ref[:, 128:256] = _rope(dk_s[...], cos_ref[...], -sin_ref[...]).astype(BF16)
        dqkv_ref[:, 256:384] = dv_s[...].astype(BF16)

    tab = pl.BlockSpec((SEQ, 128), lambda j: (0, 0))
    col = pl.BlockSpec((SEQ, 128), lambda j: (0, j))
    qkv = pl.BlockSpec((SEQ, 384), lambda j: (0, j))
    (dproj,), results = _hosted_call(
        body, name="attn_bwd", steps=N_PAIR,
        in_specs=[qkv, tab, tab, col, col, col, pl.BlockSpec(memory_space=pl.ANY)],
        out_specs=[qkv],
        out_shape=[jax.ShapeDtypeStruct((SEQ, IN_PAD), BF16)],
        scratch_shapes=[pltpu.VMEM((SEQ, 128), F32)] * 7,
        operands=(proj, cos, sin_signed, cat, lse, dcat, dproj), exchanges=exchanges, aliases={6: 0})
    return dproj, results


def _bdot(a, b, dims, precision=None):
    if precision is None:
        a = a.astype(BF16)
        b = b.astype(BF16)
    return lax.dot_general(a, b, (dims, ((0,), (0,))), preferred_element_type=F32, precision=precision)


def _make_bmm(precision):
    @jax.custom_vjp
    def nn(a, b):
        return _bdot(a, b, ((2,), (1,)), precision)

    @jax.custom_vjp
    def nt(a, b):
        return _bdot(a, b, ((2,), (2,)), precision)

    @jax.custom_vjp
    def tn(a, b):
        return _bdot(a, b, ((1,), (1,)), precision)

    nn.defvjp(lambda a, b: (nn(a, b), (a, b)), lambda r, g: (nt(g, r[1]), tn(r[0], g)))
    nt.defvjp(lambda a, b: (nt(a, b), (a, b)), lambda r, g: (nn(g, r[1]), tn(g, r[0])))
    tn.defvjp(lambda a, b: (tn(a, b), (a, b)), lambda r, g: (nt(r[1], g), nn(r[0], g)))
    return nn, nt, tn


BMM, BMM_NT, BMM_TN = _make_bmm(None)
BMM3, BMM3_NT, BMM3_TN = _make_bmm(lax.Precision.HIGH)
MM3, _, _ = _make_mm(lax.Precision.HIGH)


def _head_lanes(t, off):
    lane = lax.broadcasted_iota(jnp.int32, (1, 128), 1)
    return jnp.concatenate(
        [jnp.sum(t * (lane == off + h).astype(F32), axis=1, keepdims=True)[None] for h in range(NDH)], axis=0)


@jax.custom_vjp
def _unit_lower_inverse(a_mat):
    c = a_mat.shape[1]
    eye = (lax.broadcasted_iota(jnp.int32, (c, c), 0) == lax.broadcasted_iota(jnp.int32, (c, c), 1)).astype(F32)
    power = -a_mat
    t_inv = eye + power
    for _ in range(5):
        power = BMM3(power, power)
        t_inv = t_inv + BMM3(t_inv, power)
    return t_inv


def _unit_lower_inverse_fwd(a_mat):
    t_inv = _unit_lower_inverse(a_mat)
    return t_inv, t_inv


def _unit_lower_inverse_bwd(t_inv, d_inv):
    return (-BMM3_NT(BMM3_TN(t_inv, d_inv), t_inv),)


_unit_lower_inverse.defvjp(_unit_lower_inverse_fwd, _unit_lower_inverse_bwd)


DN_STEP_CHUNKS = 4
DN_STEP_ROWS = DN_STEP_CHUNKS * CH
DN_STEPS = NCH // DN_STEP_CHUNKS
DN_BATCH = DN_STEP_CHUNKS * NDH


def _delta_chunks(qr, kr, vr, z, tail, alog_row, dt_row, nw, state):
    c = qr.shape[1]
    tails = [tail[CH * n:CH * (n + 1)] for n in range(DN_STEP_CHUNKS)]
    per_chunk = lambda t: jnp.concatenate([t] * DN_STEP_CHUNKS, axis=0)
    beta = _sigmoid(jnp.concatenate([_head_lanes(t, 0) for t in tails], axis=0))
    a_raw = jnp.concatenate([_head_lanes(t, NDH) for t in tails], axis=0)
    g = -jnp.exp(per_chunk(_head_lanes(alog_row, 0))) * _softplus(a_raw + per_chunk(_head_lanes(dt_row, 0)))

    q = qr * lax.rsqrt(jnp.sum(qr * qr, axis=2, keepdims=True) + EPS) * (128 ** -0.5)
    k = kr * lax.rsqrt(jnp.sum(kr * kr, axis=2, keepdims=True) + EPS)

    ri = lax.broadcasted_iota(jnp.int32, (c, c), 0)
    ci = lax.broadcasted_iota(jnp.int32, (c, c), 1)
    tril = ri >= ci
    lane = lax.broadcasted_iota(jnp.int32, (1, 128), 1)
    pick = [(lane == b).astype(F32) for b in range(DN_BATCH)]
    g_lanes = sum(g[b] * pick[b] for b in range(DN_BATCH))
    g_sums = MM3(tril.astype(F32), g_lanes)
    gc = jnp.concatenate([jnp.sum(g_sums * pick[b], axis=1, keepdims=True)[None] for b in range(DN_BATCH)],
                         axis=0)
    g_row = jnp.swapaxes(jnp.broadcast_to(gc, (DN_BATCH, c, c)), 1, 2)
    decay = jnp.where(tril, jnp.exp(jnp.where(tril, gc - g_row, 0.0)), 0.0)
    kb = k * beta
    t_inv = _unit_lower_inverse(jnp.where(ri > ci, BMM_NT(kb, k) * decay, 0.0))
    eg = jnp.exp(gc)
    u = BMM(t_inv, vr * beta)
    w = BMM(t_inv, kb * eg)
    qk = BMM_NT(q, k) * decay
    g_tot = jnp.sum(g, axis=1, keepdims=True)
    q_dec = q * eg
    k_dec = k * jnp.exp(g_tot - gc)
    outs = []
    for n in range(DN_STEP_CHUNKS):
        heads = slice(NDH * n, NDH * (n + 1))
        v_new = u[heads] - BMM(w[heads], state)
        outs.append(BMM(q_dec[heads], state) + BMM(qk[heads], v_new))
        state = state * jnp.exp(g_tot[heads]) + BMM_TN(k_dec[heads], v_new)
    o = jnp.concatenate(outs, axis=0)
    on = o * lax.rsqrt(jnp.mean(o * o, axis=2, keepdims=True) + EPS) * nw
    return on * _silu(z), state


def _heads(v, off=0):
    return jnp.concatenate([v[None, CH * n:CH * (n + 1), off + 128 * h:off + 128 * (h + 1)]
                            for n in range(DN_STEP_CHUNKS) for h in range(NDH)], axis=0)


def _unheads(t):
    return jnp.concatenate([jnp.concatenate([t[NDH * n + h] for h in range(NDH)], axis=1)
                            for n in range(DN_STEP_CHUNKS)], axis=0)


def _delta_fwd(c_qkv, proj, alog_row, dt_row, nw, cat, exchanges=()):
    def body(c_ref, z_ref, tail_ref, al_ref, dt_ref, nw_ref, _, y_ref, st_ref, state):
        @pl.when(pl.program_id(0) == 0)
        def _():
            state[...] = jnp.zeros_like(state)

        cv = c_ref[...]
        st_ref[0] = state[...]
        y, new_state = _delta_chunks(_heads(cv), _heads(cv, 512), _heads(cv, 1024), _heads(z_ref[...]), tail_ref[...],
                                     al_ref[...], dt_ref[...], nw_ref[...], state[...])
        y_ref[...] = _unheads(y)
        state[...] = new_state

    row = pl.BlockSpec((1, 128), lambda n: (0, 0))
    rows = DN_STEP_ROWS
    return _hosted_call(
        body, name="delta_fwd", steps=DN_STEPS,
        in_specs=[pl.BlockSpec((rows, 1536), lambda n: (n, 0)), pl.BlockSpec((rows, 512), lambda n: (n, DN_Z_COL // 512)),
                  pl.BlockSpec((rows, 128), lambda n: (n, DN_TAIL_BLK)), row, row, row, pl.BlockSpec(memory_space=pl.ANY)],
        out_specs=[pl.BlockSpec((rows, 512), lambda n: (n, 1)),
                   pl.BlockSpec((1, NDH, 128, 128), lambda n: (n, 0, 0, 0))],
        out_shape=[jax.ShapeDtypeStruct((SEQ, 2 * ATTN_W), F32), jax.ShapeDtypeStruct((DN_STEPS, NDH, 128, 128), F32)],
        scratch_shapes=[pltpu.VMEM((NDH, 128, 128), F32)],
        operands=(c_qkv, proj, proj, alog_row, dt_row, nw, cat), exchanges=exchanges, aliases={6: 0})


def _delta_bwd(c_qkv, proj, alog_row, dt_row, nw, states, dcat, exchanges=()):
    def body(c_ref, z_ref, tail_ref, al_ref, dt_ref, nw_ref, st_ref, dy_ref,
             dp_ref, dc_ref, dal_ref, ddt_ref, dnw_ref, dstate):
        @pl.when(pl.program_id(0) == 0)
        def _():
            dstate[...] = jnp.zeros_like(dstate)
            dal_ref[...] = jnp.zeros_like(dal_ref)
            ddt_ref[...] = jnp.zeros_like(ddt_ref)
            dnw_ref[...] = jnp.zeros_like(dnw_ref)

        cv = c_ref[...]
        _, vjp = jax.vjp(_delta_chunks, _heads(cv), _heads(cv, 512), _heads(cv, 1024), _heads(z_ref[...]),
                         tail_ref[...], al_ref[...], dt_ref[...], nw_ref[...], st_ref[0])
        dq, dk, dv, dz, dtail, dal, ddt, dnw, dst = vjp((_heads(dy_ref[...]), dstate[...]))
        dstate[...] = dst
        dc_ref[...] = jnp.concatenate([_unheads(dq), _unheads(dk), _unheads(dv)], axis=1)
        dp_ref[...] = jnp.concatenate([_unheads(dz), dtail, jnp.zeros((DN_STEP_ROWS, 128), F32)], axis=1).astype(BF16)
        dal_ref[...] += dal
        ddt_ref[...] += ddt
        dnw_ref[...] += dnw

    rev = lambda n: DN_STEPS - 1 - n
    row = pl.BlockSpec((1, 128), lambda n: (0, 0))
    rows = DN_STEP_ROWS
    return _hosted_call(
        body, name="delta_bwd", steps=DN_STEPS,
        in_specs=[pl.BlockSpec((rows, 1536), lambda n: (rev(n), 0)),
                  pl.BlockSpec((rows, 512), lambda n: (rev(n), DN_Z_COL // 512)),
                  pl.BlockSpec((rows, 128), lambda n: (rev(n), DN_TAIL_BLK)), row, row, row,
                  pl.BlockSpec((1, NDH, 128, 128), lambda n: (rev(n), 0, 0, 0)),
                  pl.BlockSpec((rows, 512), lambda n: (rev(n), 1))],
        out_specs=[pl.BlockSpec((rows, 768), lambda n: (rev(n), DN_Z_COL // 768)),
                   pl.BlockSpec((rows, 1536), lambda n: (rev(n), 0)), row, row, row],
        out_shape=[jax.ShapeDtypeStruct((SEQ, IN_PAD), BF16), jax.ShapeDtypeStruct((SEQ, 1536), F32)]
        + [jax.ShapeDtypeStruct((1, 128), F32)] * 3,
        scratch_shapes=[pltpu.VMEM((NDH, 128, 128), F32)],
        operands=(c_qkv, proj, proj, alog_row, dt_row, nw, states, dcat), exchanges=exchanges)


def _place():
    x, y, c = lax.axis_index("x"), lax.axis_index("y"), lax.axis_index("c")
    other_chips = [(1 - x, y), (x, 1 - y), (1 - x, 1 - y)]
    return x, y, c, other_chips


def _gather_exchange(shards):
    n = len(shards)

    def copies(ins, outs, sems):
        send_sems, recv_sems, local_sems = sems
        x, y, c, chips = _place()
        me, sibling = (x, y, c), (x, y, 1 - c)

        def copy(b, k, block, to, src=None):
            slot = outs[b].at[4 * block[0] + 2 * block[1] + block[2]]
            return pltpu.make_async_remote_copy(
                src_ref=slot if src is None else src, dst_ref=slot,
                send_sem=send_sems.at[b, k], recv_sem=recv_sems.at[b, k], device_id=to, device_id_type=MESH)

        mine = [pltpu.make_async_copy(ins[b], outs[b].at[4 * x + 2 * y + c], local_sems.at[b]) for b in range(n)]
        first = []
        for b in range(n):
            first.append(copy(b, 0, me, sibling, src=ins[b]))
            first += [copy(b, 1 + j, me, (*chip, c), src=ins[b]) for j, chip in enumerate(chips)]
        over_ici = [copy(b, 1 + j, (*chip, c), me) for b in range(n) for j, chip in enumerate(chips)]
        passed = [copy(b, 4 + j, (*chip, c), sibling) for b in range(n) for j, chip in enumerate(chips)]
        from_sibling = []
        for b in range(n):
            from_sibling.append(copy(b, 0, sibling, me))
            from_sibling += [copy(b, 4 + j, (*chip, 1 - c), me) for j, chip in enumerate(chips)]
        return mine, first, over_ici, passed, from_sibling

    def start(ins, outs, sems):
        mine, first, _, _, _ = copies(ins, outs, sems)
        for cp in mine + first:
            cp.start()

    def middle(ins, outs, sems):
        _, _, over_ici, passed, _ = copies(ins, outs, sems)
        for arrived, onward in zip(over_ici, passed):
            arrived.wait_recv()
            onward.start()

    def finish(ins, outs, sems):
        mine, first, _, passed, from_sibling = copies(ins, outs, sems)
        for cp in from_sibling:
            cp.wait_recv()
        for cp in first + passed:
            cp.wait_send()
        for cp in mine:
            cp.wait()

    return Exchange(shards, [jax.ShapeDtypeStruct((N_DEV,) + s.shape, s.dtype) for s in shards],
                    [pltpu.SemaphoreType.DMA((n, 7)), pltpu.SemaphoreType.DMA((n, 7)), pltpu.SemaphoreType.DMA((n,))],
                    start, middle, finish)


def _sibling_exchange(gs):
    n = len(gs)

    def copies(ins, outs, sems):
        send_sems, recv_sems = sems
        x, y, c, _ = _place()
        return [pltpu.make_async_remote_copy(
            src_ref=ins[b].at[2 * p + (1 - c)], dst_ref=outs[b].at[p],
            send_sem=send_sems.at[b, p], recv_sem=recv_sems.at[b, p],
            device_id=(x, y, 1 - c), device_id_type=MESH) for b in range(n) for p in range(4)]

    def start(ins, outs, sems):
        for cp in copies(ins, outs, sems):
            cp.start()

    def finish(ins, outs, sems):
        for cp in copies(ins, outs, sems):
            cp.wait()

    return Exchange(gs, [jax.ShapeDtypeStruct((4,) + g.shape[1:], g.dtype) for g in gs],
                    [pltpu.SemaphoreType.DMA((n, 4)), pltpu.SemaphoreType.DMA((n, 4))], start, None, finish)


def _chips_exchange(hs):
    n = len(hs)

    def copies(ins, outs, sems):
        send_sems, recv_sems, local_sems = sems
        x, y, c, chips = _place()
        my_chip = 2 * x + y
        local = [pltpu.make_async_copy(ins[b].at[my_chip], outs[b].at[my_chip], local_sems.at[b]) for b in range(n)]
        sends, arrivals = [], []
        for b in range(n):
            for k, (px, py) in enumerate(chips):
                peer = 2 * px + py
                sends.append(pltpu.make_async_remote_copy(
                    src_ref=ins[b].at[peer], dst_ref=outs[b].at[my_chip],
                    send_sem=send_sems.at[b, k], recv_sem=recv_sems.at[b, k],
                    device_id=(px, py, c), device_id_type=MESH))
                arrivals.append(pltpu.make_async_remote_copy(
                    src_ref=ins[b].at[peer], dst_ref=outs[b].at[peer],
                    send_sem=send_sems.at[b, k], recv_sem=recv_sems.at[b, k],
                    device_id=(px, py, c), device_id_type=MESH))
        return local, sends, arrivals

    def start(ins, outs, sems):
        local, sends, _ = copies(ins, outs, sems)
        for cp in local + sends:
            cp.start()

    def finish(ins, outs, sems):
        local, sends, arrivals = copies(ins, outs, sems)
        for cp in arrivals:
            cp.wait_recv()
        for cp in sends:
            cp.wait_send()
        for cp in local:
            cp.wait()

    return Exchange(hs, [jax.ShapeDtypeStruct(h.shape, h.dtype) for h in hs],
                    [pltpu.SemaphoreType.DMA((n, 3)), pltpu.SemaphoreType.DMA((n, 3)), pltpu.SemaphoreType.DMA((n,))],
                    start, None, finish)


def _run_exchange(exchange, name):
    n_in, n_out = len(exchange.operands), len(exchange.out_shapes)

    def body(*refs):
        ins, outs, sems = refs[:n_in], refs[n_in:n_in + n_out], refs[n_in + n_out:]
        exchange.start(ins, outs, sems)
        if exchange.middle is not None:
            exchange.middle(ins, outs, sems)
        exchange.finish(ins, outs, sems)

    return pl.pallas_call(
        body, name=name,
        in_specs=[HBM_SPEC] * n_in, out_specs=[HBM_SPEC] * n_out,
        out_shape=exchange.out_shapes, scratch_shapes=exchange.sems,
    )(*exchange.operands)


def _pair_add(g, r, core, name):
    _, nr, nc = g.shape
    tr = nr // 2 if nr % 32 == 0 else nr

    def body(core_ref, g_ref, r_ref, o_ref):
        o_ref[...] = (g_ref[...].astype(F32) + r_ref[...].astype(F32)).astype(BF16)

    return pl.pallas_call(
        body, name=name,
        grid_spec=pltpu.PrefetchScalarGridSpec(
            num_scalar_prefetch=1, grid=(4, nr // tr),
            in_specs=[pl.BlockSpec((1, tr, nc), lambda p, i, core: (2 * p + core[0], i, 0)),
                      pl.BlockSpec((1, tr, nc), lambda p, i, core: (p, i, 0))],
            out_specs=pl.BlockSpec((1, tr, nc), lambda p, i, core: (p, i, 0))),
        out_shape=jax.ShapeDtypeStruct(r.shape, BF16),
        compiler_params=_cp("parallel", "parallel"),
    )(core, g, r)


def _all_gather_sum_small(v):
    rows = v.shape[0]

    def body(x_ref, sum_ref, out_ref, send_sems, recv_sems, local_sem):
        x, y, c, chips = _place()
        me, sibling = (x, y, c), (x, y, 1 - c)

        def block(px, py, pc):
            return out_ref.at[pl.ds((4 * px + 2 * py + pc) * rows, rows), :]

        def copy(k, blk, to, src=None):
            return pltpu.make_async_remote_copy(
                src_ref=block(*blk) if src is None else src, dst_ref=block(*blk),
                send_sem=send_sems.at[k], recv_sem=recv_sems.at[k], device_id=to, device_id_type=MESH)

        mine = pltpu.make_async_copy(x_ref, block(*me), local_sem)
        mine.start()
        first = [copy(0, me, sibling, src=x_ref)]
        first += [copy(1 + j, me, (*chip, c), src=x_ref) for j, chip in enumerate(chips)]
        for cp in first:
            cp.start()
        passed = [copy(4 + j, (*chip, c), sibling) for j, chip in enumerate(chips)]
        for j, chip in enumerate(chips):
            copy(1 + j, (*chip, c), me).wait_recv()
            passed[j].start()
        copy(0, sibling, me).wait_recv()
        for j, chip in enumerate(chips):
            copy(4 + j, (*chip, 1 - c), me).wait_recv()
        for cp in first + passed:
            cp.wait_send()
        mine.wait()
        total = out_ref[pl.ds(0, rows), :]
        for d in range(1, N_DEV):
            total = total + out_ref[pl.ds(d * rows, rows), :]
        sum_ref[...] = total

    vm = pl.BlockSpec(memory_space=pltpu.VMEM)
    return pl.pallas_call(
        body, name="small_all_reduce",
        in_specs=[vm], out_specs=[vm],
        out_shape=[jax.ShapeDtypeStruct((rows, 128), F32)],
        scratch_shapes=[pltpu.VMEM((N_DEV * rows, 128), F32), pltpu.SemaphoreType.DMA((7,)),
                        pltpu.SemaphoreType.DMA((7,)), pltpu.SemaphoreType.DMA],
    )(v)[0]


def _adamw(w, g, m, v):
    m = ADAM_B1 * m + (1.0 - ADAM_B1) * g
    v = ADAM_B2 * v + (1.0 - ADAM_B2) * (g * g)
    m_hat = m / (1.0 - ADAM_B1 ** ADAM_STEP)
    v_hat = v / (1.0 - ADAM_B2 ** ADAM_STEP)
    delta = -ADAM_LR * (m_hat / (jnp.sqrt(v_hat) + ADAM_EPS) + ADAM_WD * w)
    return delta, m, v


ADAM_TILE = dict(w_in=(IN_COLS // N_DEV, 256), w_out=(128, D_MODEL), ffn_w_in=(176, D_MODEL), ffn_w_out=(176, D_MODEL))


def _sum_chips(p):
    p = p.astype(F32)
    return (p[0] + p[1]) + (p[2] + p[3])


def _adamw_sharded(parts, w, m, v, tile, name):
    nl, nr, nc = w.shape
    tr, tc = tile

    def body(*refs):
        p_refs, (w_ref, m_ref, v_ref, g_ref, d_ref, nm_ref, nv_ref) = refs[:nl], refs[nl:]
        layer = pl.program_id(0)
        p = p_refs[0][...]
        for l in range(1, nl):
            p = jnp.where(layer == l, p_refs[l][...], p)
        g = _sum_chips(p)
        delta, nm, nv = _adamw(w_ref[0], g, m_ref[0], v_ref[0])
        g_ref[0] = g
        d_ref[0] = delta
        nm_ref[0] = nm
        nv_ref[0] = nv

    blk = pl.BlockSpec((1, tr, tc), lambda l, i, j: (l, i, j))
    return pl.pallas_call(
        body, name=name, grid=(nl, nr // tr, nc // tc),
        in_specs=[pl.BlockSpec((4, tr, tc), lambda l, i, j: (0, i, j))] * nl + [blk, blk, blk],
        out_specs=[blk] * 4,
        out_shape=[jax.ShapeDtypeStruct(w.shape, F32)] * 4,
        compiler_params=_cp("parallel", "parallel", "parallel"),
    )(*parts, w, m, v)


def _adamw_small(g, w, m, v):
    def body(g_ref, w_ref, m_ref, v_ref, d_ref, nm_ref, nv_ref):
        delta, nm, nv = _adamw(w_ref[...], g_ref[...], m_ref[...], v_ref[...])
        d_ref[...] = delta
        nm_ref[...] = nm
        nv_ref[...] = nv

    return pl.pallas_call(
        body, name="adamw_small",
        out_shape=[jax.ShapeDtypeStruct(g.shape, F32)] * 3,
    )(g, w, m, v)


def _packed_rows(n):
    return -(-n // 1024) * 8


def _pack(arrays, rows):
    pieces = []
    for a in arrays:
        flat = a.reshape(-1).astype(F32)
        nr = _packed_rows(flat.shape[0])
        pieces.append(jnp.pad(flat, (0, nr * 128 - flat.shape[0])).reshape(nr, 128))
    used = sum(p.shape[0] for p in pieces)
    return jnp.concatenate(pieces + [jnp.zeros((rows - used, 128), F32)] * (rows > used), axis=0)


def _unpack(packed, shapes):
    out, row = [], 0
    for s in shapes:
        n = math.prod(s)
        out.append(packed[row:row + _packed_rows(n)].reshape(-1)[:n].reshape(s))
        row += _packed_rows(n)
    return out


def _row(v, width=None):
    v = v.reshape(1, -1)
    return v if width is None else jnp.pad(v, ((0, 0), (0, width - v.shape[1])))


def _layer_fwd(x, wts, tables, hosted):
    h = _norm_fwd(x, wts["norm_pre_mix"], "norm_pre_mix")
    proj = _matmul(h, wts["w_in"], tb=True, tm=SEQ, tn=768, tk=1024, name="mm_proj")
    (cat, lse), got = _attn_fwd(proj, *tables, exchanges=hosted["attn"][0])
    hosted["attn"][1](got)
    c_qkv = _dnconv_fwd(proj, wts["dn_conv_w"])
    (cat, states), got = _delta_fwd(c_qkv, proj, wts["dn_a_log"], wts["dn_dt_bias"], wts["dn_norm_w"], cat,
                                    exchanges=hosted["delta"][0])
    hosted["delta"][1](got)
    mix = _matmul(cat, wts["w_out"], tm=512, tn=1024, tk=1024, name="mm_mix")
    x1 = _resnorm_fwd(x, mix, wts["norm_post_mix"], "norm_post_mix")
    h2 = _norm_fwd(x1, wts["norm_pre_ffn"], "norm_pre_ffn")
    pre = _matmul(h2, wts["ffn_w_in"], tb=True, tm=SEQ, tn=512, tk=1024, name="mm_ffn_in", out_dtype=BF16)
    act, got = _ffact_fwd(pre, wts["ffn_conv_w"], wts["ffn_conv_b"], exchanges=hosted["ffact"][0])
    hosted["ffact"][1](got)
    f = _matmul(act, wts["ffn_w_out"], tm=512, tn=1024, tk=D_FF, name="mm_ffn_out")
    x2 = _resnorm_fwd(x1, f, wts["norm_post_ffn"], "norm_post_ffn")
    saved = dict(x=x, h=h, proj=proj, lse=lse, c_qkv=c_qkv, states=states, cat=cat, mix=mix, x1=x1, h2=h2, pre=pre,
                 act=act, f=f)
    return x2, saved


def _layer_bwd(dx2, wts, s, tables, ffact_exchanges=(), delta_exchanges=None, attn_exchanges=None):
    g = {}
    df, g["norm_post_ffn"] = _norm_bwd(s["f"], wts["norm_post_ffn"], dx2, None, "norm_post_ffn_bwd", BF16)
    dact = _matmul(df, wts["ffn_w_out"], tb=True, tm=SEQ, tn=1408, tk=1024, name="mm_dact", out_dtype=BF16)
    g["ffn_w_out"] = _matmul(s["act"], df, ta=True, tm=1408, tn=512, tk=SEQ, name="mm_dw_ffn_out", out_dtype=BF16)
    (dpre, g["ffn_conv_w"], g["ffn_conv_b"]), got = _ffact_bwd(s["pre"], wts["ffn_conv_w"], wts["ffn_conv_b"], dact,
                                                               exchanges=ffact_exchanges)
    dh2 = _matmul(dpre, wts["ffn_w_in"], tm=1024, tn=1024, tk=1408, name="mm_dh2")
    g["ffn_w_in"] = _matmul(dpre, s["h2"], ta=True, tm=512, tn=1024, tk=SEQ, name="mm_dw_ffn_in", out_dtype=BF16)
    dx1, g["norm_pre_ffn"] = _norm_bwd(s["x1"], wts["norm_pre_ffn"], dh2, dx2, "norm_pre_ffn_bwd")
    dmix, g["norm_post_mix"] = _norm_bwd(s["mix"], wts["norm_post_mix"], dx1, None, "norm_post_mix_bwd", BF16)
    dcat = _matmul(dmix, wts["w_out"], tb=True, tm=SEQ, tn=512, tk=1024, name="mm_dcat")
    g["w_out"] = _matmul(s["cat"], dmix, ta=True, tm=1024, tn=512, tk=SEQ, name="mm_dw_out", out_dtype=BF16)
    (dproj, dc, g["dn_a_log"], g["dn_dt_bias"], g["dn_norm_w"]), got = _delta_bwd(
        s["c_qkv"], s["proj"], wts["dn_a_log"], wts["dn_dt_bias"], wts["dn_norm_w"], s["states"], dcat,
        exchanges=delta_exchanges(g, got) if delta_exchanges is not None else ())
    dproj, got = _attn_bwd(s["proj"], *tables, s["cat"], s["lse"], dcat, dproj,
                           exchanges=attn_exchanges(got) if attn_exchanges is not None else ())
    dproj, g["dn_conv_w"] = _dnconv_bwd(s["proj"], wts["dn_conv_w"], dc, dproj)
    dh = _matmul(dproj, wts["w_in"], tm=1024, tn=1024, tk=1280, name="mm_dh")
    g["w_in"] = _matmul(dproj, s["h"], ta=True, tm=768, tn=1024, tk=SEQ, name="mm_dw_in", out_dtype=BF16)
    dx, g["norm_pre_mix"] = _norm_bwd(s["x"], wts["norm_pre_mix"], dh, dx1, "norm_pre_mix_bwd")
    return dx, g, got


BIG = ("w_in", "w_out", "ffn_w_in", "ffn_w_out")
COLUMN_SHARDED = ("w_in", "ffn_w_in")
SMALL_SHARDED = ("dn_conv_w", "ffn_conv_w")
REPLICATED = ("dn_a_log", "dn_dt_bias", "dn_norm_w", "ffn_conv_b", "norm_pre_mix", "norm_post_mix", "norm_pre_ffn",
              "norm_post_ffn")
WEIGHTS = ("w_in", "dn_conv_w", "dn_a_log", "dn_dt_bias", "dn_norm_w", "w_out", "ffn_w_in", "ffn_conv_w", "ffn_conv_b",
           "ffn_w_out", "norm_pre_mix", "norm_post_mix", "norm_pre_ffn", "norm_post_ffn")
FULL_SHAPE = dict(dn_conv_w=(DEPTH, 4, 1536), ffn_conv_w=(DEPTH, 3, 2 * D_FF), dn_a_log=(DEPTH, NDH),
                  dn_dt_bias=(DEPTH, NDH), dn_norm_w=(DEPTH, 128), ffn_conv_b=(DEPTH, 2 * D_FF),
                  norm_pre_mix=(DEPTH, D_MODEL), norm_post_mix=(DEPTH, D_MODEL), norm_pre_ffn=(DEPTH, D_MODEL),
                  norm_post_ffn=(DEPTH, D_MODEL))
SMALL_GRAD_ORDER = REPLICATED + SMALL_SHARDED
SMALL_GRAD_ROWS = 544
SMALL_W_ROWS = 56
SMALL_ADAM_ROWS = 232


def _w_in_rows_to_kernel_order(t):
    qkv = t[:QKV_W].reshape(3, N_PAIR, 128, -1).swapaxes(0, 1).reshape(QKV_W, -1)
    return jnp.pad(jnp.concatenate([qkv, t[QKV_W:]], axis=0), ((0, IN_PAD - IN_COLS), (0, 0)))


def _w_in_rows_from_kernel_order(t):
    qkv = t[:QKV_W].reshape(N_PAIR, 3, 128, -1).swapaxes(0, 1).reshape(QKV_W, -1)
    return jnp.concatenate([qkv, t[QKV_W:IN_COLS]], axis=0)


def _interleave_ff_rows(t):
    return t.reshape(2, FF_BLKS, 128, -1).swapaxes(0, 1).reshape(2 * D_FF, -1)


def _deinterleave_ff_rows(t):
    return t.reshape(FF_BLKS, 2, 128, -1).swapaxes(0, 1).reshape(2 * D_FF, -1)


def kernel(x, w_in, dn_conv_w, dn_a_log, dn_dt_bias, dn_norm_w, w_out, ffn_w_in, ffn_conv_w, ffn_conv_b, ffn_w_out, norm_pre_mix, norm_post_mix, norm_pre_ffn, norm_post_ffn, loss_target, m_w_in, m_dn_conv_w, m_dn_a_log, m_dn_dt_bias, m_dn_norm_w, m_w_out, m_ffn_w_in, m_ffn_conv_w, m_ffn_conv_b, m_ffn_w_out, m_norm_pre_mix, m_norm_post_mix, m_norm_pre_ffn, m_norm_post_ffn, v_w_in, v_dn_conv_w, v_dn_a_log, v_dn_dt_bias, v_dn_norm_w, v_w_out, v_ffn_w_in, v_ffn_conv_w, v_ffn_conv_b, v_ffn_w_out, v_norm_pre_mix, v_norm_post_mix, v_norm_pre_ffn, v_norm_post_ffn):
    local = dict(w_in=w_in, dn_conv_w=dn_conv_w, dn_a_log=dn_a_log, dn_dt_bias=dn_dt_bias, dn_norm_w=dn_norm_w,
                 w_out=w_out, ffn_w_in=ffn_w_in, ffn_conv_w=ffn_conv_w, ffn_conv_b=ffn_conv_b, ffn_w_out=ffn_w_out,
                 norm_pre_mix=norm_pre_mix, norm_post_mix=norm_post_mix, norm_pre_ffn=norm_pre_ffn,
                 norm_post_ffn=norm_post_ffn)
    mom_m = dict(w_in=m_w_in, dn_conv_w=m_dn_conv_w, dn_a_log=m_dn_a_log, dn_dt_bias=m_dn_dt_bias,
                 dn_norm_w=m_dn_norm_w, w_out=m_w_out, ffn_w_in=m_ffn_w_in, ffn_conv_w=m_ffn_conv_w,
                 ffn_conv_b=m_ffn_conv_b, ffn_w_out=m_ffn_w_out, norm_pre_mix=m_norm_pre_mix,
                 norm_post_mix=m_norm_post_mix, norm_pre_ffn=m_norm_pre_ffn, norm_post_ffn=m_norm_post_ffn)
    mom_v = dict(w_in=v_w_in, dn_conv_w=v_dn_conv_w, dn_a_log=v_dn_a_log, dn_dt_bias=v_dn_dt_bias,
                 dn_norm_w=v_dn_norm_w, w_out=v_w_out, ffn_w_in=v_ffn_w_in, ffn_conv_w=v_ffn_conv_w,
                 ffn_conv_b=v_ffn_conv_b, ffn_w_out=v_ffn_w_out, norm_pre_mix=v_norm_pre_mix,
                 norm_post_mix=v_norm_post_mix, norm_pre_ffn=v_norm_pre_ffn, norm_post_ffn=v_norm_post_ffn)
    dev = 4 * lax.axis_index("x") + 2 * lax.axis_index("y") + lax.axis_index("c")
    core = lax.axis_index("c").astype(jnp.int32).reshape(1)

    def shard(n, l):
        s = local[n].transpose(0, 2, 1) if n in COLUMN_SHARDED else local[n]
        return s[l].astype(BF16)

    def matrix(n, gathered):
        if n == "w_in":
            return _w_in_rows_to_kernel_order(gathered.reshape(IN_COLS, D_MODEL))
        if n == "ffn_w_in":
            return _interleave_ff_rows(gathered.reshape(2 * D_FF, D_MODEL))
        return gathered.reshape(-1, D_MODEL)

    small_w = _pack([dn_conv_w, ffn_conv_w], SMALL_W_ROWS)
    g_w_in0, g_small = _run_exchange(_gather_exchange([shard("w_in", 0), small_w]), "weights_all_gather")
    n_dn, n_ff = DEPTH * 4 * 192, DEPTH * 3 * 704
    dn_rows = _packed_rows(n_dn)
    sm_dn = g_small[:, :dn_rows].reshape(N_DEV, -1)[:, :n_dn]
    sm_ff = g_small[:, dn_rows:].reshape(N_DEV, -1)[:, :n_ff]
    full_dn_conv = sm_dn.reshape(N_DEV, DEPTH, 4, 192).transpose(1, 2, 0, 3).reshape(DEPTH, 4, 1536)
    full_ff_conv = _interleave_ff(sm_ff.reshape(N_DEV, DEPTH, 3, 704).transpose(1, 2, 0, 3).reshape(DEPTH, 3, 2 * D_FF))

    def small_weights(l):
        wts = dict(dn_conv_w=full_dn_conv[l], ffn_conv_w=full_ff_conv[l], ffn_conv_b=_interleave_ff(_row(ffn_conv_b[l])),
                   dn_a_log=_row(dn_a_log[l], 128), dn_dt_bias=_row(dn_dt_bias[l], 128))
        for n in ("dn_norm_w", "norm_pre_mix", "norm_post_mix", "norm_pre_ffn", "norm_post_ffn"):
            wts[n] = _row(local[n][l])
        return wts

    weights = [small_weights(l) for l in range(DEPTH)]
    weights[0]["w_in"] = matrix("w_in", g_w_in0)

    def gather_behind(wanted):
        def deliver(got):
            for (n, l), g in zip(wanted, got[0]):
                weights[l][n] = matrix(n, g)

        return [_gather_exchange([shard(n, l) for n, l in wanted])], deliver

    nothing = ((), lambda got: None)

    tables = _rope_tables()
    act, saved0 = _layer_fwd(x[0], weights[0], tables, dict(
        attn=gather_behind([("ffn_w_in", 0)]), delta=gather_behind([("w_out", 0), ("ffn_w_out", 0)]),
        ffact=gather_behind([("w_in", 1)])))
    act, saved1 = _layer_fwd(act, weights[1], tables, dict(
        attn=gather_behind([("ffn_w_in", 1)]), delta=gather_behind([("w_out", 1), ("ffn_w_out", 1)]), ffact=nothing))
    loss_part, dact = _loss_fwd_bwd(act, loss_target[0])

    def to_devices(name, t):
        if name == "w_in":
            t = _w_in_rows_from_kernel_order(t)
        if name == "ffn_w_in":
            t = _deinterleave_ff_rows(t)
        return t.reshape(N_DEV, t.shape[0] // N_DEV, t.shape[1])

    def pair_sums(names, layer, to_dev, from_sibling):
        return [_pair_add(gd, r, core, "grads_pair_add_%s_%d" % (n, layer))
                for n, gd, r in zip(names, to_dev, from_sibling)]

    early = ("w_out", "ffn_w_in", "ffn_w_out")
    grads, parts, stash = [None] * DEPTH, {}, {}

    def delta_exchanges1(g, got_ffact):
        stash["early1"] = [to_devices(n, g[n]) for n in early]
        return [_sibling_exchange(stash["early1"])]

    def attn_exchanges1(got_delta):
        return [_chips_exchange(pair_sums(early, 1, stash["early1"], got_delta[0]))]

    dact, grads[1], got_attn = _layer_bwd(dact, weights[1], saved1, tables, (), delta_exchanges1, attn_exchanges1)
    for n, p in zip(early, got_attn[0]):
        parts[n, 1] = p
    w_in1 = [to_devices("w_in", grads[1]["w_in"])]

    def delta_exchanges0(g, got_ffact):
        stash["early0"] = [to_devices(n, g[n]) for n in early]
        return [_chips_exchange(pair_sums(("w_in",), 1, w_in1, got_ffact[0])), _sibling_exchange(stash["early0"])]

    def attn_exchanges0(got_delta):
        parts["w_in", 1], = got_delta[0]
        return [_chips_exchange(pair_sums(early, 0, stash["early0"], got_delta[1]))]

    dact, grads[0], got_attn = _layer_bwd(dact, weights[0], saved0, tables, [_sibling_exchange(w_in1)],
                                          delta_exchanges0, attn_exchanges0)
    for n, p in zip(early, got_attn[0]):
        parts[n, 0] = p
    grad_x = dact[None]
    last = [to_devices("w_in", grads[0]["w_in"])]
    from_sibling = _run_exchange(_sibling_exchange(last), "grads_to_sibling")
    parts["w_in", 0], = _run_exchange(_chips_exchange(pair_sums(("w_in",), 0, last, from_sibling)), "grads_to_chips")

    def small_grad(name):
        t = jnp.stack([grads[l][name] for l in range(DEPTH)])
        if name in ("dn_a_log", "dn_dt_bias"):
            t = t[:, 0, :NDH]
        if name in ("ffn_conv_w", "ffn_conv_b"):
            t = _deinterleave_ff(t)
        return t.reshape(FULL_SHAPE[name])

    small_part = _pack([small_grad(n) for n in SMALL_GRAD_ORDER] + [loss_part[0, :1]], SMALL_GRAD_ROWS)
    small_sum = _all_gather_sum_small(small_part)
    small_g = dict(zip(SMALL_GRAD_ORDER + ("loss",), _unpack(small_sum, [FULL_SHAPE[n] for n in SMALL_GRAD_ORDER] + [(1,)])))
    loss = small_g["loss"][0]
    small_g["dn_conv_w"] = lax.dynamic_slice_in_dim(small_g["dn_conv_w"], dev * 192, 192, axis=2)
    small_g["ffn_conv_w"] = lax.dynamic_slice_in_dim(small_g["ffn_conv_w"], dev * 704, 704, axis=2)

    out_g, out_d, out_m, out_v = {}, {}, {}, {}
    for n in BIG:
        turn = (lambda t: t.transpose(0, 2, 1)) if n in COLUMN_SHARDED else (lambda t: t)
        outs = _adamw_sharded([parts[n, l] for l in range(DEPTH)], turn(local[n]), turn(mom_m[n]), turn(mom_v[n]),
                              ADAM_TILE[n], "adamw_" + n)
        out_g[n], out_d[n], out_m[n], out_v[n] = [turn(t) for t in outs]
    shapes = [small_g[n].shape for n in SMALL_GRAD_ORDER]
    d_s, m_s, v_s = _adamw_small(_pack([small_g[n] for n in SMALL_GRAD_ORDER], SMALL_ADAM_ROWS),
                                 _pack([local[n] for n in SMALL_GRAD_ORDER], SMALL_ADAM_ROWS),
                                 _pack([mom_m[n] for n in SMALL_GRAD_ORDER], SMALL_ADAM_ROWS),
                                 _pack([mom_v[n] for n in SMALL_GRAD_ORDER], SMALL_ADAM_ROWS))
    for n, d, m, v in zip(SMALL_GRAD_ORDER, _unpack(d_s, shapes), _unpack(m_s, shapes), _unpack(v_s, shapes)):
        out_g[n], out_d[n], out_m[n], out_v[n] = small_g[n], d, m, v
    return (loss, grad_x, *[out_g[n] for n in WEIGHTS], *[out_d[n] for n in WEIGHTS],
            *[out_m[n] for n in WEIGHTS], *[out_v[n] for n in WEIGHTS])
```

```python
import functools
import math

import jax
import jax.numpy as jnp
from jax import lax
from jax.experimental import pallas as pl
from jax.experimental.pallas import tpu as pltpu

F32 = jnp.float32
BF16 = jnp.bfloat16
MESH = pl.DeviceIdType.MESH

N_DEV = 8
SEQ = 2048
D_MODEL = 1024
DEPTH = 2
N_PAIR = 4
HEAD_DIM = 64
ATTN_W = 512
ATTN_BLK = 128
DILATIONS = (1, 4, 16)
SEGMENT_BLOCKS = (16, 4, 1)
N_BLK = SEQ // ATTN_BLK
NDH = 4
CH = 64
NCH = SEQ // CH
IN_COLS = 3592
IN_PAD = 3840
QKV_W = 3 * ATTN_W
DN_QKV_BLK0 = QKV_W // 128
DN_QKV_BLKS = 1536 // 128
DN_Z_COL = 3072
DN_TAIL_BLK = 3584 // 128
D_FF = 2816
FF_BLKS = D_FF // 128
EPS = 1e-6
NEG = -1e30
ROPE_THETA = 10000.0

ADAM_LR, ADAM_B1, ADAM_B2, ADAM_EPS, ADAM_WD, ADAM_STEP = 0.001, 0.9, 0.999, 1e-08, 0.01, 10

VMEM_LIMIT = 56 * 1024 * 1024


def _cp(*sem):
    return pltpu.CompilerParams(dimension_semantics=sem, vmem_limit_bytes=VMEM_LIMIT)


class Exchange:
    def __init__(self, operands, out_shapes, sems, start, middle, finish):
        self.operands, self.out_shapes, self.sems = list(operands), list(out_shapes), list(sems)
        self.start, self.middle, self.finish = start, middle, finish


HBM_SPEC = pl.BlockSpec(memory_space=pltpu.HBM)


def _hosted_call(body, *, name, steps, in_specs, out_specs, out_shape, scratch_shapes, operands, exchanges=(),
                 aliases=None):
    n_in, n_out, n_scr = len(in_specs), len(out_specs), len(scratch_shapes)

    def take(refs, pos, counts):
        groups = []
        for c in counts:
            groups.append(refs[pos:pos + c])
            pos += c
        return groups, pos

    def full_body(*refs):
        ins, pos = refs[:n_in], n_in
        ex_ins, pos = take(refs, pos, [len(e.operands) for e in exchanges])
        outs, pos = refs[pos:pos + n_out], pos + n_out
        ex_outs, pos = take(refs, pos, [len(e.out_shapes) for e in exchanges])
        scr, pos = refs[pos:pos + n_scr], pos + n_scr
        ex_sems, pos = take(refs, pos, [len(e.sems) for e in exchanges])
        step = pl.program_id(0)
        for e, a, b, s in zip(exchanges, ex_ins, ex_outs, ex_sems):
            pl.when(step == 0)(functools.partial(e.start, a, b, s))
            if e.middle is not None:
                pl.when(step == (3 * steps) // 4)(functools.partial(e.middle, a, b, s))
        body(*ins, *outs, *scr)
        for e, a, b, s in zip(exchanges, ex_ins, ex_outs, ex_sems):
            pl.when(step == steps - 1)(functools.partial(e.finish, a, b, s))

    n_ex_in = sum(len(e.operands) for e in exchanges)
    n_ex_out = sum(len(e.out_shapes) for e in exchanges)
    results = pl.pallas_call(
        full_body, name=name, grid=(steps,),
        in_specs=list(in_specs) + [HBM_SPEC] * n_ex_in,
        out_specs=list(out_specs) + [HBM_SPEC] * n_ex_out,
        out_shape=list(out_shape) + [s for e in exchanges for s in e.out_shapes],
        scratch_shapes=list(scratch_shapes) + [s for e in exchanges for s in e.sems],
        input_output_aliases=aliases or {},
        compiler_params=_cp("arbitrary"),
    )(*operands, *[a for e in exchanges for a in e.operands])
    ex_results, _ = take(results, n_out, [len(e.out_shapes) for e in exchanges])
    return results[:n_out], ex_results


def _dot(a, b, dims, precision=None):
    if precision is None:
        a = a.astype(BF16)
        b = b.astype(BF16)
    return lax.dot_general(a, b, (dims, ((), ())), preferred_element_type=F32, precision=precision)


def _make_mm(precision):
    @jax.custom_vjp
    def nn(a, b):
        return _dot(a, b, ((1,), (0,)), precision)

    @jax.custom_vjp
    def nt(a, b):
        return _dot(a, b, ((1,), (1,)), precision)

    @jax.custom_vjp
    def tn(a, b):
        return _dot(a, b, ((0,), (0,)), precision)

    nn.defvjp(lambda a, b: (nn(a, b), (a, b)), lambda r, g: (nt(g, r[1]), tn(r[0], g)))
    nt.defvjp(lambda a, b: (nt(a, b), (a, b)), lambda r, g: (nn(g, r[1]), tn(g, r[0])))
    tn.defvjp(lambda a, b: (tn(a, b), (a, b)), lambda r, g: (nt(r[1], g), nn(r[0], g)))
    return nn, nt, tn


def _matmul(a, b, *, ta=False, tb=False, tm, tn, tk, name, out_dtype=F32):
    (k_dim, m_dim) = a.shape if ta else a.shape[::-1]
    (n_dim, k2) = b.shape if tb else b.shape[::-1]
    assert k_dim == k2 and m_dim % tm == 0 and n_dim % tn == 0 and k_dim % tk == 0, (a.shape, b.shape, tm, tn, tk)
    nk = k_dim // tk
    dims = ((0 if ta else 1,), (1 if tb else 0,))

    def body(a_ref, b_ref, o_ref, *acc):
        p = _dot(a_ref[...], b_ref[...], dims)
        if nk == 1:
            o_ref[...] = p.astype(out_dtype)
            return
        acc_ref, k = acc[0], pl.program_id(2)

        @pl.when(k == 0)
        def _():
            acc_ref[...] = p

        @pl.when(k > 0)
        def _():
            acc_ref[...] += p

        @pl.when(k == nk - 1)
        def _():
            o_ref[...] = acc_ref[...].astype(out_dtype)

    a_spec = pl.BlockSpec((tk, tm), lambda i, j, k: (k, i)) if ta else pl.BlockSpec((tm, tk), lambda i, j, k: (i, k))
    b_spec = pl.BlockSpec((tn, tk), lambda i, j, k: (j, k)) if tb else pl.BlockSpec((tk, tn), lambda i, j, k: (k, j))
    return pl.pallas_call(
        body, name=name,
        grid=(m_dim // tm, n_dim // tn, nk),
        in_specs=[a_spec, b_spec],
        out_specs=pl.BlockSpec((tm, tn), lambda i, j, k: (i, j)),
        out_shape=jax.ShapeDtypeStruct((m_dim, n_dim), out_dtype),
        scratch_shapes=[pltpu.VMEM((tm, tn), F32)] if nk > 1 else [],
        compiler_params=_cp("parallel", "parallel", "arbitrary"),
    )(a, b)


NORM_ROWS = 256


def _rms(x, w):
    return x * lax.rsqrt(jnp.mean(x * x, axis=1, keepdims=True) + EPS) * w


def _norm_fwd(x, w_row, name, out_dtype=BF16):
    def body(x_ref, w_ref, o_ref):
        o_ref[...] = _rms(x_ref[...], w_ref[...]).astype(out_dtype)

    return pl.pallas_call(
        body, name=name, grid=(SEQ // NORM_ROWS,),
        in_specs=[pl.BlockSpec((NORM_ROWS, D_MODEL), lambda i: (i, 0)), pl.BlockSpec((1, D_MODEL), lambda i: (0, 0))],
        out_specs=pl.BlockSpec((NORM_ROWS, D_MODEL), lambda i: (i, 0)),
        out_shape=jax.ShapeDtypeStruct((SEQ, D_MODEL), out_dtype),
        compiler_params=_cp("parallel"),
    )(x, w_row)


def _resnorm_fwd(x, f, w_row, name):
    def body(x_ref, f_ref, w_ref, o_ref):
        o_ref[...] = x_ref[...] + _rms(f_ref[...], w_ref[...])

    blk = pl.BlockSpec((NORM_ROWS, D_MODEL), lambda i: (i, 0))
    return pl.pallas_call(
        body, name=name, grid=(SEQ // NORM_ROWS,),
        in_specs=[blk, blk, pl.BlockSpec((1, D_MODEL), lambda i: (0, 0))],
        out_specs=blk, out_shape=jax.ShapeDtypeStruct((SEQ, D_MODEL), F32),
        compiler_params=_cp("parallel"),
    )(x, f, w_row)


def _norm_bwd(x, w_row, dy, add, name, dx_dtype=F32):
    has_add = add is not None

    def body(*refs):
        if has_add:
            x_ref, w_ref, dy_ref, add_ref, dx_ref, dw_ref = refs
        else:
            x_ref, w_ref, dy_ref, dx_ref, dw_ref = refs
        _, vjp = jax.vjp(_rms, x_ref[...], w_ref[...])
        dx, dw = vjp(dy_ref[...])
        dx_ref[...] = (dx + add_ref[...] if has_add else dx).astype(dx_dtype)

        @pl.when(pl.program_id(0) == 0)
        def _():
            dw_ref[...] = jnp.zeros_like(dw_ref)

        dw_ref[...] += dw

    blk = pl.BlockSpec((NORM_ROWS, D_MODEL), lambda i: (i, 0))
    row = pl.BlockSpec((1, D_MODEL), lambda i: (0, 0))
    ins = [x, w_row, dy] + ([add] if has_add else [])
    return pl.pallas_call(
        body, name=name, grid=(SEQ // NORM_ROWS,),
        in_specs=[blk, row, blk] + ([blk] if has_add else []),
        out_specs=[blk, row],
        out_shape=[jax.ShapeDtypeStruct((SEQ, D_MODEL), dx_dtype), jax.ShapeDtypeStruct((1, D_MODEL), F32)],
        compiler_params=_cp("arbitrary"),
    )(*ins)


def _loss_fwd_bwd(y, target):
    def body(y_ref, t_ref, loss_ref, dy_ref):
        err = y_ref[...] - t_ref[...]
        dy_ref[...] = err * (1.0 / D_MODEL)

        @pl.when(pl.program_id(0) == 0)
        def _():
            loss_ref[...] = jnp.zeros_like(loss_ref)

        part = jnp.sum(jnp.sum(err * err, axis=1, keepdims=True) * (1.0 / D_MODEL), axis=0, keepdims=True)
        loss_ref[...] += 0.5 * jnp.broadcast_to(part, loss_ref.shape)

    blk = pl.BlockSpec((NORM_ROWS, D_MODEL), lambda i: (i, 0))
    return pl.pallas_call(
        body, name="loss", grid=(SEQ // NORM_ROWS,),
        in_specs=[blk, blk],
        out_specs=[pl.BlockSpec((1, 128), lambda i: (0, 0)), blk],
        out_shape=[jax.ShapeDtypeStruct((1, 128), F32), jax.ShapeDtypeStruct((SEQ, D_MODEL), F32)],
        compiler_params=_cp("arbitrary"),
    )(y, target)


def _make_shift(j):
    def down(x):
        row = lax.broadcasted_iota(jnp.int32, x.shape, 0)
        return jnp.where(row >= j, pltpu.roll(x, j, 0), 0.0)

    def up(x):
        n = x.shape[0]
        row = lax.broadcasted_iota(jnp.int32, x.shape, 0)
        return jnp.where(row < n - j, pltpu.roll(x, n - j, 0), 0.0)

    f = jax.custom_vjp(down)
    f.defvjp(lambda x: (down(x), None), lambda _, g: (up(g),))
    return f


_SHIFT = {j: _make_shift(j) for j in (1, 2, 3)}


def _causal_conv(x, taps):
    n = len(taps)
    acc = x * taps[n - 1]
    for k in range(n - 1):
        acc = acc + _SHIFT[n - 1 - k](x) * taps[k]
    return acc


def _tap_rows(w_ref, lanes=slice(None)):
    return tuple(w_ref[k:k + 1, lanes] for k in range(w_ref.shape[0]))


def _sigmoid(x):
    return 1.0 / (1.0 + jnp.exp(-x))


def _silu(x):
    return x * _sigmoid(x)


def _softplus(x):
    return jnp.maximum(x, 0.0) + jnp.log(1.0 + jnp.exp(-jnp.abs(x)))


def _gelu_tanh(x):
    return 0.5 * x * (1.0 + jnp.tanh(math.sqrt(2.0 / math.pi) * (x + 0.044715 * (x * x * x))))


def _dnconv_fn(x, taps):
    return _silu(_causal_conv(x, taps))


def _dnconv_fwd(proj, conv_w):
    def body(x_ref, w_ref, o_ref):
        o_ref[...] = _dnconv_fn(x_ref[...], _tap_rows(w_ref)).astype(BF16)

    return pl.pallas_call(
        body, name="dnconv_fwd", grid=(DN_QKV_BLKS,),
        in_specs=[pl.BlockSpec((SEQ, 128), lambda j: (0, DN_QKV_BLK0 + j)), pl.BlockSpec((4, 128), lambda j: (0, j))],
        out_specs=pl.BlockSpec((SEQ, 128), lambda j: (0, j)),
        out_shape=jax.ShapeDtypeStruct((SEQ, 1536), BF16),
        compiler_params=_cp("parallel"),
    )(proj, conv_w)


def _dnconv_bwd(proj, conv_w, dc, dproj):
    def body(x_ref, w_ref, dc_ref, _, dx_ref, dw_ref):
        _, vjp = jax.vjp(_dnconv_fn, x_ref[...], _tap_rows(w_ref))
        dx, dw = vjp(dc_ref[...])
        dx_ref[...] = dx.astype(BF16)
        for k, row in enumerate(dw):
            dw_ref[k:k + 1, :] = row

    return pl.pallas_call(
        body, name="dnconv_bwd", grid=(DN_QKV_BLKS,),
        in_specs=[pl.BlockSpec((SEQ, 128), lambda j: (0, DN_QKV_BLK0 + j)), pl.BlockSpec((4, 128), lambda j: (0, j)),
                  pl.BlockSpec((SEQ, 128), lambda j: (0, j)), pl.BlockSpec(memory_space=pl.ANY)],
        out_specs=[pl.BlockSpec((SEQ, 128), lambda j: (0, DN_QKV_BLK0 + j)), pl.BlockSpec((4, 128), lambda j: (0, j))],
        out_shape=[jax.ShapeDtypeStruct((SEQ, IN_PAD), BF16), jax.ShapeDtypeStruct((4, 1536), F32)],
        input_output_aliases={3: 0},
        compiler_params=_cp("parallel"),
    )(proj, conv_w, dc, dproj)


def _ffact_fn(pg, pu, wg, wu, bg, bu):
    return _gelu_tanh(_causal_conv(pg, wg) + bg) * (_causal_conv(pu, wu) + bu)


def _ffact_args(p_ref, w_ref, b_ref):
    g, u = slice(0, 128), slice(128, 256)
    return (p_ref[:, g].astype(F32), p_ref[:, u].astype(F32), _tap_rows(w_ref, g), _tap_rows(w_ref, u),
            b_ref[:, g], b_ref[:, u])


def _ffact_fwd(pre, conv_w, conv_b, exchanges=()):
    def body(p_ref, w_ref, b_ref, o_ref):
        o_ref[...] = _ffact_fn(*_ffact_args(p_ref, w_ref, b_ref)).astype(BF16)

    (act,), results = _hosted_call(
        body, name="ffact_fwd", steps=FF_BLKS,
        in_specs=[pl.BlockSpec((SEQ, 256), lambda j: (0, j)), pl.BlockSpec((3, 256), lambda j: (0, j)),
                  pl.BlockSpec((1, 256), lambda j: (0, j))],
        out_specs=[pl.BlockSpec((SEQ, 128), lambda j: (0, j))],
        out_shape=[jax.ShapeDtypeStruct((SEQ, D_FF), BF16)],
        scratch_shapes=[], operands=(pre, conv_w, conv_b), exchanges=exchanges)
    return act, results


def _ffact_bwd(pre, conv_w, conv_b, dact, exchanges=()):
    def body(p_ref, w_ref, b_ref, da_ref, dp_ref, dw_ref, db_ref):
        _, vjp = jax.vjp(_ffact_fn, *_ffact_args(p_ref, w_ref, b_ref))
        dpg, dpu, dwg, dwu, dbg, dbu = vjp(da_ref[...].astype(F32))
        dp_ref[:, 0:128] = dpg.astype(BF16)
        dp_ref[:, 128:256] = dpu.astype(BF16)
        for k in range(3):
            dw_ref[k:k + 1, 0:128] = dwg[k]
            dw_ref[k:k + 1, 128:256] = dwu[k]
        db_ref[:, 0:128] = dbg
        db_ref[:, 128:256] = dbu

    return _hosted_call(
        body, name="ffact_bwd", steps=FF_BLKS,
        in_specs=[pl.BlockSpec((SEQ, 256), lambda j: (0, j)), pl.BlockSpec((3, 256), lambda j: (0, j)),
                  pl.BlockSpec((1, 256), lambda j: (0, j)), pl.BlockSpec((SEQ, 128), lambda j: (0, j))],
        out_specs=[pl.BlockSpec((SEQ, 256), lambda j: (0, j)), pl.BlockSpec((3, 256), lambda j: (0, j)),
                   pl.BlockSpec((1, 256), lambda j: (0, j))],
        out_shape=[jax.ShapeDtypeStruct((SEQ, 2 * D_FF), BF16), jax.ShapeDtypeStruct((3, 2 * D_FF), F32),
                   jax.ShapeDtypeStruct((1, 2 * D_FF), F32)],
        scratch_shapes=[], operands=(pre, conv_w, conv_b, dact), exchanges=exchanges)


def _interleave_ff(t):
    lead = t.shape[:-1]
    return t.reshape(lead + (2, FF_BLKS, 128)).swapaxes(-3, -2).reshape(lead + (2 * D_FF,))


def _deinterleave_ff(t):
    lead = t.shape[:-1]
    return t.reshape(lead + (FF_BLKS, 2, 128)).swapaxes(-3, -2).reshape(lead + (2 * D_FF,))


def _rope_tables():
    inv = 1.0 / (ROPE_THETA ** (jnp.arange(0, HEAD_DIM, 2, dtype=F32) / HEAD_DIM))
    ang = jnp.arange(SEQ, dtype=F32)[:, None] * inv[None, :]
    cos = jnp.tile(jnp.cos(ang), (1, 4))
    sin = jnp.tile(jnp.sin(ang), (1, 4))
    sign = jnp.where((jnp.arange(128) % HEAD_DIM) < HEAD_DIM // 2, -1.0, 1.0).astype(F32)
    return cos, sin * sign[None, :]


def _rope(x, cos, sin_signed):
    lane = lax.broadcasted_iota(jnp.int32, x.shape, 1)
    partner = jnp.where((lane % HEAD_DIM) < HEAD_DIM // 2, pltpu.roll(x, 128 - HEAD_DIM // 2, 1),
                        pltpu.roll(x, HEAD_DIM // 2, 1))
    return x * cos + partner * sin_signed


def _head_masks():
    lane = lax.broadcasted_iota(jnp.int32, (1, 128), 1)
    return [(lane // HEAD_DIM) == h for h in range(2)]


def _both_heads(x):
    return jnp.concatenate([jnp.where(hm, x, 0.0)[None] for hm in _head_masks()], axis=0)


def _block_keys(branch, k_s, v_s, rows, prows, has_prev):
    a = lax.broadcasted_iota(jnp.int32, (ATTN_BLK, ATTN_BLK), 0)
    c = lax.broadcasted_iota(jnp.int32, (ATTN_BLK, ATTN_BLK), 1)
    keys, values, mask = k_s[rows, :], v_s[rows, :], c <= a
    if SEGMENT_BLOCKS[branch] > 1:
        keys = jnp.concatenate([k_s[prows, :], keys], axis=0)
        values = jnp.concatenate([v_s[prows, :], values], axis=0)
        mask = jnp.concatenate([(c >= a) & has_prev, mask], axis=1)
    twice = lambda t: jnp.broadcast_to(t[None], (2,) + t.shape)
    return twice(keys), twice(values), mask


def _block_rows(branch, t):
    d, per_seg = DILATIONS[branch], SEGMENT_BLOCKS[branch]
    if d == 1:
        start = pl.multiple_of(t * ATTN_BLK, ATTN_BLK)
        prev = pl.multiple_of(jnp.maximum(t - 1, 0) * ATTN_BLK, ATTN_BLK)
        return pl.ds(start, ATTN_BLK), pl.ds(prev, ATTN_BLK), t > 0
    r, n = t // per_seg, t % per_seg
    start = n * (ATTN_BLK * d) + r
    prev = jnp.maximum(n - 1, 0) * (ATTN_BLK * d) + r
    return pl.ds(start, ATTN_BLK, stride=d), pl.ds(prev, ATTN_BLK, stride=d), n > 0


def _attn_fwd(proj, cos, sin_signed, exchanges=()):
    scale = HEAD_DIM ** -0.5

    def body(qkv_ref, cos_ref, sin_ref, out_ref, lse_ref, q_s, k_s, v_s, *branch_s):
        o_s, l_s = branch_s[:3], branch_s[3:]
        q_s[...] = _rope(qkv_ref[:, 0:128], cos_ref[...], sin_ref[...])
        k_s[...] = _rope(qkv_ref[:, 128:256], cos_ref[...], sin_ref[...])
        v_s[...] = qkv_ref[:, 256:384]
        heads = _head_masks()
        for branch in range(3):
            def block(t, carry, branch=branch):
                rows, prows, has_prev = _block_rows(branch, t)
                keys, values, mask = _block_keys(branch, k_s, v_s, rows, prows, has_prev)
                s = jnp.where(mask, BMM_NT(_both_heads(q_s[rows, :]), keys) * scale, NEG)
                m = jnp.max(s, axis=2, keepdims=True)
                e = jnp.exp(s - m)
                l = jnp.sum(e, axis=2, keepdims=True)
                o = BMM(e, values) / l
                lse_b = m + jnp.log(l)
                o_s[branch][rows, :] = jnp.where(heads[0], o[0], o[1])
                l_s[branch][rows, :] = jnp.where(heads[0], lse_b[0], lse_b[1])
                return carry

            lax.fori_loop(0, N_BLK, block, 0, unroll=4)
        l0, l1, l2 = l_s[0][...], l_s[1][...], l_s[2][...]
        m = jnp.maximum(jnp.maximum(l0, l1), l2)
        w0, w1, w2 = jnp.exp(l0 - m), jnp.exp(l1 - m), jnp.exp(l2 - m)
        den = w0 + w1 + w2
        out_ref[...] = (w0 * o_s[0][...] + w1 * o_s[1][...] + w2 * o_s[2][...]) / den
        lse_ref[...] = m + jnp.log(den)

    tab = pl.BlockSpec((SEQ, 128), lambda j: (0, 0))
    col = pl.BlockSpec((SEQ, 128), lambda j: (0, j))
    return _hosted_call(
        body, name="attn_fwd", steps=N_PAIR,
        in_specs=[pl.BlockSpec((SEQ, 384), lambda j: (0, j)), tab, tab],
        out_specs=[col, col],
        out_shape=[jax.ShapeDtypeStruct((SEQ, 2 * ATTN_W), F32), jax.ShapeDtypeStruct((SEQ, ATTN_W), F32)],
        scratch_shapes=[pltpu.VMEM((SEQ, 128), F32)] * 9,
        operands=(proj, cos, sin_signed), exchanges=exchanges)


def _attn_bwd(proj, cos, sin_signed, cat, lse, dcat, dproj, exchanges=()):
    scale = HEAD_DIM ** -0.5

    def body(qkv_ref, cos_ref, sin_ref, out_ref, lse_ref, do_ref, _, dqkv_ref, q_s, k_s, v_s, dq_s, dk_s, dv_s,
             dod_s):
        q_s[...] = _rope(qkv_ref[:, 0:128], cos_ref[...], sin_ref[...])
        k_s[...] = _rope(qkv_ref[:, 128:256], cos_ref[...], sin_ref[...])
        v_s[...] = qkv_ref[:, 256:384]
        dq_s[...] = jnp.zeros_like(dq_s)
        dk_s[...] = jnp.zeros_like(dk_s)
        dv_s[...] = jnp.zeros_like(dv_s)
        dod_s[...] = do_ref[...] * out_ref[...]
        heads = _head_masks()
        for branch in range(3):
            def block(t, carry, branch=branch):
                rows, prows, has_prev = _block_rows(branch, t)
                keys, values, mask = _block_keys(branch, k_s, v_s, rows, prows, has_prev)
                q2, do2 = _both_heads(q_s[rows, :]), _both_heads(do_ref[rows, :])
                lse_b, dod = lse_ref[rows, :], dod_s[rows, :]
                lse2 = jnp.concatenate(
                    [jnp.max(jnp.where(hm, lse_b, NEG), axis=1, keepdims=True)[None] for hm in heads], axis=0)
                delta = jnp.concatenate(
                    [jnp.sum(jnp.where(hm, dod, 0.0), axis=1, keepdims=True)[None] for hm in heads], axis=0)
                p = jnp.exp(jnp.where(mask, BMM_NT(q2, keys) * scale, NEG) - lse2)
                ds = p * (BMM_NT(do2, values) - delta) * scale
                dq = BMM(ds, keys)
                dk = BMM_TN(ds, q2)
                dv = BMM_TN(p, do2)
                dk, dv = dk[0] + dk[1], dv[0] + dv[1]
                dq_s[rows, :] += jnp.where(heads[0], dq[0], dq[1])
                if SEGMENT_BLOCKS[branch] > 1:
                    dk_s[rows, :] += dk[ATTN_BLK:]
                    dv_s[rows, :] += dv[ATTN_BLK:]

                    @pl.when(has_prev)
                    def _():
                        dk_s[prows, :] += dk[:ATTN_BLK]
                        dv_s[prows, :] += dv[:ATTN_BLK]
                else:
                    dk_s[rows, :] += dk
                    dv_s[rows, :] += dv
                return carry

            lax.fori_loop(0, N_BLK, block, 0, unroll=4)
        dqkv_ref[:, 0:128] = _rope(dq_s[...], cos_ref[...], -sin_ref[...]).astype(BF16)
        dqkv_ref[:, 128:256] = _rope(dk_s[...], cos_ref[...], -sin_ref[...]).astype(BF16)
        dqkv_ref[:, 256:384] = dv_s[...].astype(BF16)

    tab = pl.BlockSpec((SEQ, 128), lambda j: (0, 0))
    col = pl.BlockSpec((SEQ, 128), lambda j: (0, j))
    qkv = pl.BlockSpec((SEQ, 384), lambda j: (0, j))
    (dproj,), results = _hosted_call(
        body, name="attn_bwd", steps=N_PAIR,
        in_specs=[qkv, tab, tab, col, col, col, pl.BlockSpec(memory_space=pl.ANY)],
        out_specs=[qkv],
        out_shape=[jax.ShapeDtypeStruct((SEQ, IN_PAD), BF16)],
        scratch_shapes=[pltpu.VMEM((SEQ, 128), F32)] * 7,
        operands=(proj, cos, sin_signed, cat, lse, dcat, dproj), exchanges=exchanges, aliases={6: 0})
    return dproj, results


def _bdot(a, b, dims, precision=None):
    if precision is None:
        a = a.astype(BF16)
        b = b.astype(BF16)
    return lax.dot_general(a, b, (dims, ((0,), (0,))), preferred_element_type=F32, precision=precision)


def _make_bmm(precision):
    @jax.custom_vjp
    def nn(a, b):
        return _bdot(a, b, ((2,), (1,)), precision)

    @jax.custom_vjp
    def nt(a, b):
        return _bdot(a, b, ((2,), (2,)), precision)

    @jax.custom_vjp
    def tn(a, b):
        return _bdot(a, b, ((1,), (1,)), precision)

    nn.defvjp(lambda a, b: (nn(a, b), (a, b)), lambda r, g: (nt(g, r[1]), tn(r[0], g)))
    nt.defvjp(lambda a, b: (nt(a, b), (a, b)), lambda r, g: (nn(g, r[1]), tn(g, r[0])))
    tn.defvjp(lambda a, b: (tn(a, b), (a, b)), lambda r, g: (nt(r[1], g), nn(r[0], g)))
    return nn, nt, tn


BMM, BMM_NT, BMM_TN = _make_bmm(None)
BMM3, BMM3_NT, BMM3_TN = _make_bmm(lax.Precision.HIGH)
MM3, _, _ = _make_mm(lax.Precision.HIGH)


def _head_lanes(t, off):
    lane = lax.broadcasted_iota(jnp.int32, (1, 128), 1)
    return jnp.concatenate(
        [jnp.sum(t * (lane == off + h).astype(F32), axis=1, keepdims=True)[None] for h in range(NDH)], axis=0)


@jax.custom_vjp
def _unit_lower_inverse(a_mat):
    c = a_mat.shape[1]
    eye = (lax.broadcasted_iota(jnp.int32, (c, c), 0) == lax.broadcasted_iota(jnp.int32, (c, c), 1)).astype(F32)
    power = -a_mat
    t_inv = eye + power
    for _ in range(5):
        power = BMM3(power, power)
        t_inv = t_inv + BMM3(t_inv, power)
    return t_inv


def _unit_lower_inverse_fwd(a_mat):
    t_inv = _unit_lower_inverse(a_mat)
    return t_inv, t_inv


def _unit_lower_inverse_bwd(t_inv, d_inv):
    return (-BMM3_NT(BMM3_TN(t_inv, d_inv), t_inv),)


_unit_lower_inverse.defvjp(_unit_lower_inverse_fwd, _unit_lower_inverse_bwd)


DN_STEP_CHUNKS = 4
DN_STEP_ROWS = DN_STEP_CHUNKS * CH
DN_STEPS = NCH // DN_STEP_CHUNKS
DN_BATCH = DN_STEP_CHUNKS * NDH


def _delta_chunks(qr, kr, vr, z, tail, alog_row, dt_row, nw, state):
    c = qr.shape[1]
    tails = [tail[CH * n:CH * (n + 1)] for n in range(DN_STEP_CHUNKS)]
    per_chunk = lambda t: jnp.concatenate([t] * DN_STEP_CHUNKS, axis=0)
    beta = _sigmoid(jnp.concatenate([_head_lanes(t, 0) for t in tails], axis=0))
    a_raw = jnp.concatenate([_head_lanes(t, NDH) for t in tails], axis=0)
    g = -jnp.exp(per_chunk(_head_lanes(alog_row, 0))) * _softplus(a_raw + per_chunk(_head_lanes(dt_row, 0)))

    q = qr * lax.rsqrt(jnp.sum(qr * qr, axis=2, keepdims=True) + EPS) * (128 ** -0.5)
    k = kr * lax.rsqrt(jnp.sum(kr * kr, axis=2, keepdims=True) + EPS)

    ri = lax.broadcasted_iota(jnp.int32, (c, c), 0)
    ci = lax.broadcasted_iota(jnp.int32, (c, c), 1)
    tril = ri >= ci
    lane = lax.broadcasted_iota(jnp.int32, (1, 128), 1)
    pick = [(lane == b).astype(F32) for b in range(DN_BATCH)]
    g_lanes = sum(g[b] * pick[b] for b in range(DN_BATCH))
    g_sums = MM3(tril.astype(F32), g_lanes)
    gc = jnp.concatenate([jnp.sum(g_sums * pick[b], axis=1, keepdims=True)[None] for b in range(DN_BATCH)],
                         axis=0)
    g_row = jnp.swapaxes(jnp.broadcast_to(gc, (DN_BATCH, c, c)), 1, 2)
    decay = jnp.where(tril, jnp.exp(jnp.where(tril, gc - g_row, 0.0)), 0.0)
    kb = k * beta
    t_inv = _unit_lower_inverse(jnp.where(ri > ci, BMM_NT(kb, k) * decay, 0.0))
    eg = jnp.exp(gc)
    u = BMM(t_inv, vr * beta)
    w = BMM(t_inv, kb * eg)
    qk = BMM_NT(q, k) * decay
    g_tot = jnp.sum(g, axis=1, keepdims=True)
    q_dec = q * eg
    k_dec = k * jnp.exp(g_tot - gc)
    outs = []
    for n in range(DN_STEP_CHUNKS):
        heads = slice(NDH * n, NDH * (n + 1))
        v_new = u[heads] - BMM(w[heads], state)
        outs.append(BMM(q_dec[heads], state) + BMM(qk[heads], v_new))
        state = state * jnp.exp(g_tot[heads]) + BMM_TN(k_dec[heads], v_new)
    o = jnp.concatenate(outs, axis=0)
    on = o * lax.rsqrt(jnp.mean(o * o, axis=2, keepdims=True) + EPS) * nw
    return on * _silu(z), state


def _heads(v, off=0):
    return jnp.concatenate([v[None, CH * n:CH * (n + 1), off + 128 * h:off + 128 * (h + 1)]
                            for n in range(DN_STEP_CHUNKS) for h in range(NDH)], axis=0)


def _unheads(t):
    return jnp.concatenate([jnp.concatenate([t[NDH * n + h] for h in range(NDH)], axis=1)
                            for n in range(DN_STEP_CHUNKS)], axis=0)


def _delta_fwd(c_qkv, proj, alog_row, dt_row, nw, cat, exchanges=()):
    def body(c_ref, z_ref, tail_ref, al_ref, dt_ref, nw_ref, _, y_ref, st_ref, state):
        @pl.when(pl.program_id(0) == 0)
        def _():
            state[...] = jnp.zeros_like(state)

        cv = c_ref[...].astype(F32)
        st_ref[0] = state[...]
        y, new_state = _delta_chunks(_heads(cv), _heads(cv, 512), _heads(cv, 1024), _heads(z_ref[...]), tail_ref[...],
                                     al_ref[...], dt_ref[...], nw_ref[...], state[...])
        y_ref[...] = _unheads(y)
        state[...] = new_state

    row = pl.BlockSpec((1, 128), lambda n: (0, 0))
    rows = DN_STEP_ROWS
    return _hosted_call(
        body, name="delta_fwd", steps=DN_STEPS,
        in_specs=[pl.BlockSpec((rows, 1536), lambda n: (n, 0)), pl.BlockSpec((rows, 512), lambda n: (n, DN_Z_COL // 512)),
                  pl.BlockSpec((rows, 128), lambda n: (n, DN_TAIL_BLK)), row, row, row, pl.BlockSpec(memory_space=pl.ANY)],
        out_specs=[pl.BlockSpec((rows, 512), lambda n: (n, 1)),
                   pl.BlockSpec((1, NDH, 128, 128), lambda n: (n, 0, 0, 0))],
        out_shape=[jax.ShapeDtypeStruct((SEQ, 2 * ATTN_W), F32), jax.ShapeDtypeStruct((DN_STEPS, NDH, 128, 128), F32)],
        scratch_shapes=[pltpu.VMEM((NDH, 128, 128), F32)],
        operands=(c_qkv, proj, proj, alog_row, dt_row, nw, cat), exchanges=exchanges, aliases={6: 0})


def _delta_bwd(c_qkv, proj, alog_row, dt_row, nw, states, dcat, exchanges=()):
    def body(c_ref, z_ref, tail_ref, al_ref, dt_ref, nw_ref, st_ref, dy_ref,
             dp_ref, dc_ref, dal_ref, ddt_ref, dnw_ref, dstate):
        @pl.when(pl.program_id(0) == 0)
        def _():
            dstate[...] = jnp.zeros_like(dstate)
            dal_ref[...] = jnp.zeros_like(dal_ref)
            ddt_ref[...] = jnp.zeros_like(ddt_ref)
            dnw_ref[...] = jnp.zeros_like(dnw_ref)

        cv = c_ref[...].astype(F32)
        _, vjp = jax.vjp(_delta_chunks, _heads(cv), _heads(cv, 512), _heads(cv, 1024), _heads(z_ref[...]),
                         tail_ref[...], al_ref[...], dt_ref[...], nw_ref[...], st_ref[0])
        dq, dk, dv, dz, dtail, dal, ddt, dnw, dst = vjp((_heads(dy_ref[...]), dstate[...]))
        dstate[...] = dst
        dc_ref[...] = jnp.concatenate([_unheads(dq), _unheads(dk), _unheads(dv)], axis=1)
        dp_ref[...] = jnp.concatenate([_unheads(dz), dtail, jnp.zeros((DN_STEP_ROWS, 128), F32)], axis=1).astype(BF16)
        dal_ref[...] += dal
        ddt_ref[...] += ddt
        dnw_ref[...] += dnw

    rev = lambda n: DN_STEPS - 1 - n
    row = pl.BlockSpec((1, 128), lambda n: (0, 0))
    rows = DN_STEP_ROWS
    return _hosted_call(
        body, name="delta_bwd", steps=DN_STEPS,
        in_specs=[pl.BlockSpec((rows, 1536), lambda n: (rev(n), 0)),
                  pl.BlockSpec((rows, 512), lambda n: (rev(n), DN_Z_COL // 512)),
                  pl.BlockSpec((rows, 128), lambda n: (rev(n), DN_TAIL_BLK)), row, row, row,
                  pl.BlockSpec((1, NDH, 128, 128), lambda n: (rev(n), 0, 0, 0)),
                  pl.BlockSpec((rows, 512), lambda n: (rev(n), 1))],
        out_specs=[pl.BlockSpec((rows, 768), lambda n: (rev(n), DN_Z_COL // 768)),
                   pl.BlockSpec((rows, 1536), lambda n: (rev(n), 0)), row, row, row],
        out_shape=[jax.ShapeDtypeStruct((SEQ, IN_PAD), BF16), jax.ShapeDtypeStruct((SEQ, 1536), F32)]
        + [jax.ShapeDtypeStruct((1, 128), F32)] * 3,
        scratch_shapes=[pltpu.VMEM((NDH, 128, 128), F32)],
        operands=(c_qkv, proj, proj, alog_row, dt_row, nw, states, dcat), exchanges=exchanges)


def _place():
    x, y, c = lax.axis_index("x"), lax.axis_index("y"), lax.axis_index("c")
    other_chips = [(1 - x, y), (x, 1 - y), (1 - x, 1 - y)]
    return x, y, c, other_chips


def _gather_exchange(shards):
    n = len(shards)

    def copies(ins, outs, sems):
        send_sems, recv_sems, local_sems = sems
        x, y, c, chips = _place()
        me, sibling = (x, y, c), (x, y, 1 - c)

        def copy(b, k, block, to, src=None):
            slot = outs[b].at[4 * block[0] + 2 * block[1] + block[2]]
            return pltpu.make_async_remote_copy(
                src_ref=slot if src is None else src, dst_ref=slot,
                send_sem=send_sems.at[b, k], recv_sem=recv_sems.at[b, k], device_id=to, device_id_type=MESH)

        mine = [pltpu.make_async_copy(ins[b], outs[b].at[4 * x + 2 * y + c], local_sems.at[b]) for b in range(n)]
        first = []
        for b in range(n):
            first.append(copy(b, 0, me, sibling, src=ins[b]))
            first += [copy(b, 1 + j, me, (*chip, c), src=ins[b]) for j, chip in enumerate(chips)]
        over_ici = [copy(b, 1 + j, (*chip, c), me) for b in range(n) for j, chip in enumerate(chips)]
        passed = [copy(b, 4 + j, (*chip, c), sibling) for b in range(n) for j, chip in enumerate(chips)]
        from_sibling = []
        for b in range(n):
            from_sibling.append(copy(b, 0, sibling, me))
            from_sibling += [copy(b, 4 + j, (*chip, 1 - c), me) for j, chip in enumerate(chips)]
        return mine, first, over_ici, passed, from_sibling

    def start(ins, outs, sems):
        mine, first, _, _, _ = copies(ins, outs, sems)
        for cp in mine + first:
            cp.start()

    def middle(ins, outs, sems):
        _, _, over_ici, passed, _ = copies(ins, outs, sems)
        for arrived, onward in zip(over_ici, passed):
            arrived.wait_recv()
            onward.start()

    def finish(ins, outs, sems):
        mine, first, _, passed, from_sibling = copies(ins, outs, sems)
        for cp in from_sibling:
            cp.wait_recv()
        for cp in first + passed:
            cp.wait_send()
        for cp in mine:
            cp.wait()

    return Exchange(shards, [jax.ShapeDtypeStruct((N_DEV,) + s.shape, s.dtype) for s in shards],
                    [pltpu.SemaphoreType.DMA((n, 7)), pltpu.SemaphoreType.DMA((n, 7)), pltpu.SemaphoreType.DMA((n,))],
                    start, middle, finish)


def _sibling_exchange(gs):
    n = len(gs)

    def copies(ins, outs, sems):
        send_sems, recv_sems = sems
        x, y, c, _ = _place()
        return [pltpu.make_async_remote_copy(
            src_ref=ins[b].at[2 * p + (1 - c)], dst_ref=outs[b].at[p],
            send_sem=send_sems.at[b, p], recv_sem=recv_sems.at[b, p],
            device_id=(x, y, 1 - c), device_id_type=MESH) for b in range(n) for p in range(4)]

    def start(ins, outs, sems):
        for cp in copies(ins, outs, sems):
            cp.start()

    def finish(ins, outs, sems):
        for cp in copies(ins, outs, sems):
            cp.wait()

    return Exchange(gs, [jax.ShapeDtypeStruct((4,) + g.shape[1:], g.dtype) for g in gs],
                    [pltpu.SemaphoreType.DMA((n, 4)), pltpu.SemaphoreType.DMA((n, 4))], start, None, finish)


def _chips_exchange(hs):
    n = len(hs)

    def copies(ins, outs, sems):
        send_sems, recv_sems, local_sems = sems
        x, y, c, chips = _place()
        my_chip = 2 * x + y
        local = [pltpu.make_async_copy(ins[b].at[my_chip], outs[b].at[my_chip], local_sems.at[b]) for b in range(n)]
        sends, arrivals = [], []
        for b in range(n):
            for k, (px, py) in enumerate(chips):
                peer = 2 * px + py
                sends.append(pltpu.make_async_remote_copy(
                    src_ref=ins[b].at[peer], dst_ref=outs[b].at[my_chip],
                    send_sem=send_sems.at[b, k], recv_sem=recv_sems.at[b, k],
                    device_id=(px, py, c), device_id_type=MESH))
                arrivals.append(pltpu.make_async_remote_copy(
                    src_ref=ins[b].at[peer], dst_ref=outs[b].at[peer],
                    send_sem=send_sems.at[b, k], recv_sem=recv_sems.at[b, k],
                    device_id=(px, py, c), device_id_type=MESH))
        return local, sends, arrivals

    def start(ins, outs, sems):
        local, sends, _ = copies(ins, outs, sems)
        for cp in local + sends:
            cp.start()

    def finish(ins, outs, sems):
        local, sends, arrivals = copies(ins, outs, sems)
        for cp in arrivals:
            cp.wait_recv()
        for cp in sends:
            cp.wait_send()
        for cp in local:
            cp.wait()

    return Exchange(hs, [jax.ShapeDtypeStruct(h.shape, h.dtype) for h in hs],
                    [pltpu.SemaphoreType.DMA((n, 3)), pltpu.SemaphoreType.DMA((n, 3)), pltpu.SemaphoreType.DMA((n,))],
                    start, None, finish)


def _run_exchange(exchange, name):
    n_in, n_out = len(exchange.operands), len(exchange.out_shapes)

    def body(*refs):
        ins, outs, sems = refs[:n_in], refs[n_in:n_in + n_out], refs[n_in + n_out:]
        exchange.start(ins, outs, sems)
        if exchange.middle is not None:
            exchange.middle(ins, outs, sems)
        exchange.finish(ins, outs, sems)

    return pl.pallas_call(
        body, name=name,
        in_specs=[HBM_SPEC] * n_in, out_specs=[HBM_SPEC] * n_out,
        out_shape=exchange.out_shapes, scratch_shapes=exchange.sems,
    )(*exchange.operands)


def _pair_add(g, r, core, name):
    _, nr, nc = g.shape
    tr = nr // 2 if nr % 32 == 0 else nr

    def body(core_ref, g_ref, r_ref, o_ref):
        o_ref[...] = (g_ref[...].astype(F32) + r_ref[...].astype(F32)).astype(BF16)

    return pl.pallas_call(
        body, name=name,
        grid_spec=pltpu.PrefetchScalarGridSpec(
            num_scalar_prefetch=1, grid=(4, nr // tr),
            in_specs=[pl.BlockSpec((1, tr, nc), lambda p, i, core: (2 * p + core[0], i, 0)),
                      pl.BlockSpec((1, tr, nc), lambda p, i, core: (p, i, 0))],
            out_specs=pl.BlockSpec((1, tr, nc), lambda p, i, core: (p, i, 0))),
        out_shape=jax.ShapeDtypeStruct(r.shape, BF16),
        compiler_params=_cp("parallel", "parallel"),
    )(core, g, r)


def _all_gather_sum_small(v):
    rows = v.shape[0]

    def body(x_ref, sum_ref, out_ref, send_sems, recv_sems, local_sem):
        x, y, c, chips = _place()
        me, sibling = (x, y, c), (x, y, 1 - c)

        def block(px, py, pc):
            return out_ref.at[pl.ds((4 * px + 2 * py + pc) * rows, rows), :]

        def copy(k, blk, to, src=None):
            return pltpu.make_async_remote_copy(
                src_ref=block(*blk) if src is None else src, dst_ref=block(*blk),
                send_sem=send_sems.at[k], recv_sem=recv_sems.at[k], device_id=to, device_id_type=MESH)

        mine = pltpu.make_async_copy(x_ref, block(*me), local_sem)
        mine.start()
        first = [copy(0, me, sibling, src=x_ref)]
        first += [copy(1 + j, me, (*chip, c), src=x_ref) for j, chip in enumerate(chips)]
        for cp in first:
            cp.start()
        passed = [copy(4 + j, (*chip, c), sibling) for j, chip in enumerate(chips)]
        for j, chip in enumerate(chips):
            copy(1 + j, (*chip, c), me).wait_recv()
            passed[j].start()
        copy(0, sibling, me).wait_recv()
        for j, chip in enumerate(chips):
            copy(4 + j, (*chip, 1 - c), me).wait_recv()
        for cp in first + passed:
            cp.wait_send()
        mine.wait()
        total = out_ref[pl.ds(0, rows), :]
        for d in range(1, N_DEV):
            total = total + out_ref[pl.ds(d * rows, rows), :]
        sum_ref[...] = total

    vm = pl.BlockSpec(memory_space=pltpu.VMEM)
    return pl.pallas_call(
        body, name="small_all_reduce",
        in_specs=[vm], out_specs=[vm],
        out_shape=[jax.ShapeDtypeStruct((rows, 128), F32)],
        scratch_shapes=[pltpu.VMEM((N_DEV * rows, 128), F32), pltpu.SemaphoreType.DMA((7,)),
                        pltpu.SemaphoreType.DMA((7,)), pltpu.SemaphoreType.DMA],
    )(v)[0]


def _adamw(w, g, m, v):
    m = ADAM_B1 * m + (1.0 - ADAM_B1) * g
    v = ADAM_B2 * v + (1.0 - ADAM_B2) * (g * g)
    m_hat = m / (1.0 - ADAM_B1 ** ADAM_STEP)
    v_hat = v / (1.0 - ADAM_B2 ** ADAM_STEP)
    delta = -ADAM_LR * (m_hat / (jnp.sqrt(v_hat) + ADAM_EPS) + ADAM_WD * w)
    return delta, m, v


ADAM_TILE = dict(w_in=(IN_COLS // N_DEV, 256), w_out=(128, D_MODEL), ffn_w_in=(176, D_MODEL), ffn_w_out=(176, D_MODEL))


def _sum_chips(p):
    p = p.astype(F32)
    return (p[0] + p[1]) + (p[2] + p[3])


def _adamw_sharded(parts, w, m, v, tile, name):
    nl, nr, nc = w.shape
    tr, tc = tile

    def body(*refs):
        p_refs, (w_ref, m_ref, v_ref, g_ref, d_ref, nm_ref, nv_ref) = refs[:nl], refs[nl:]
        layer = pl.program_id(0)
        p = p_refs[0][...]
        for l in range(1, nl):
            p = jnp.where(layer == l, p_refs[l][...], p)
        g = _sum_chips(p)
        delta, nm, nv = _adamw(w_ref[0], g, m_ref[0], v_ref[0])
        g_ref[0] = g
        d_ref[0] = delta
        nm_ref[0] = nm
        nv_ref[0] = nv

    blk = pl.BlockSpec((1, tr, tc), lambda l, i, j: (l, i, j))
    return pl.pallas_call(
        body, name=name, grid=(nl, nr // tr, nc // tc),
        in_specs=[pl.BlockSpec((4, tr, tc), lambda l, i, j: (0, i, j))] * nl + [blk, blk, blk],
        out_specs=[blk] * 4,
        out_shape=[jax.ShapeDtypeStruct(w.shape, F32)] * 4,
        compiler_params=_cp("parallel", "parallel", "parallel"),
    )(*parts, w, m, v)


def _adamw_small(g, w, m, v):
    def body(g_ref, w_ref, m_ref, v_ref, d_ref, nm_ref, nv_ref):
        delta, nm, nv = _adamw(w_ref[...], g_ref[...], m_ref[...], v_ref[...])
        d_ref[...] = delta
        nm_ref[...] = nm
        nv_ref[...] = nv

    return pl.pallas_call(
        body, name="adamw_small",
        out_shape=[jax.ShapeDtypeStruct(g.shape, F32)] * 3,
    )(g, w, m, v)


def _packed_rows(n):
    return -(-n // 1024) * 8


def _pack(arrays, rows):
    pieces = []
    for a in arrays:
        flat = a.reshape(-1).astype(F32)
        nr = _packed_rows(flat.shape[0])
        pieces.append(jnp.pad(flat, (0, nr * 128 - flat.shape[0])).reshape(nr, 128))
    used = sum(p.shape[0] for p in pieces)
    return jnp.concatenate(pieces + [jnp.zeros((rows - used, 128), F32)] * (rows > used), axis=0)


def _unpack(packed, shapes):
    out, row = [], 0
    for s in shapes:
        n = math.prod(s)
        out.append(packed[row:row + _packed_rows(n)].reshape(-1)[:n].reshape(s))
        row += _packed_rows(n)
    return out


def _row(v, width=None):
    v = v.reshape(1, -1)
    return v if width is None else jnp.pad(v, ((0, 0), (0, width - v.shape[1])))


def _layer_fwd(x, wts, tables, hosted):
    h = _norm_fwd(x, wts["norm_pre_mix"], "norm_pre_mix")
    proj = _matmul(h, wts["w_in"], tb=True, tm=SEQ, tn=768, tk=1024, name="mm_proj")
    (cat, lse), got = _attn_fwd(proj, *tables, exchanges=hosted["attn"][0])
    hosted["attn"][1](got)
    c_qkv = _dnconv_fwd(proj, wts["dn_conv_w"])
    (cat, states), got = _delta_fwd(c_qkv, proj, wts["dn_a_log"], wts["dn_dt_bias"], wts["dn_norm_w"], cat,
                                    exchanges=hosted["delta"][0])
    hosted["delta"][1](got)
    mix = _matmul(cat, wts["w_out"], tm=512, tn=1024, tk=1024, name="mm_mix")
    x1 = _resnorm_fwd(x, mix, wts["norm_post_mix"], "norm_post_mix")
    h2 = _norm_fwd(x1, wts["norm_pre_ffn"], "norm_pre_ffn")
    pre = _matmul(h2, wts["ffn_w_in"], tb=True, tm=SEQ, tn=512, tk=1024, name="mm_ffn_in", out_dtype=BF16)
    act, got = _ffact_fwd(pre, wts["ffn_conv_w"], wts["ffn_conv_b"], exchanges=hosted["ffact"][0])
    hosted["ffact"][1](got)
    f = _matmul(act, wts["ffn_w_out"], tm=512, tn=1024, tk=D_FF, name="mm_ffn_out")
    x2 = _resnorm_fwd(x1, f, wts["norm_post_ffn"], "norm_post_ffn")
    saved = dict(x=x, h=h, proj=proj, lse=lse, c_qkv=c_qkv, states=states, cat=cat, mix=mix, x1=x1, h2=h2, pre=pre,
                 act=act, f=f)
    return x2, saved


def _layer_bwd(dx2, wts, s, tables, ffact_exchanges=(), delta_exchanges=None, attn_exchanges=None):
    g = {}
    df, g["norm_post_ffn"] = _norm_bwd(s["f"], wts["norm_post_ffn"], dx2, None, "norm_post_ffn_bwd", BF16)
    dact = _matmul(df, wts["ffn_w_out"], tb=True, tm=SEQ, tn=1408, tk=1024, name="mm_dact", out_dtype=BF16)
    g["ffn_w_out"] = _matmul(s["act"], df, ta=True, tm=1408, tn=512, tk=SEQ, name="mm_dw_ffn_out", out_dtype=BF16)
    (dpre, g["ffn_conv_w"], g["ffn_conv_b"]), got = _ffact_bwd(s["pre"], wts["ffn_conv_w"], wts["ffn_conv_b"], dact,
                                                               exchanges=ffact_exchanges)
    dh2 = _matmul(dpre, wts["ffn_w_in"], tm=1024, tn=1024, tk=1408, name="mm_dh2")
    g["ffn_w_in"] = _matmul(dpre, s["h2"], ta=True, tm=512, tn=1024, tk=SEQ, name="mm_dw_ffn_in", out_dtype=BF16)
    dx1, g["norm_pre_ffn"] = _norm_bwd(s["x1"], wts["norm_pre_ffn"], dh2, dx2, "norm_pre_ffn_bwd")
    dmix, g["norm_post_mix"] = _norm_bwd(s["mix"], wts["norm_post_mix"], dx1, None, "norm_post_mix_bwd", BF16)
    dcat = _matmul(dmix, wts["w_out"], tb=True, tm=SEQ, tn=512, tk=1024, name="mm_dcat")
    g["w_out"] = _matmul(s["cat"], dmix, ta=True, tm=1024, tn=512, tk=SEQ, name="mm_dw_out", out_dtype=BF16)
    (dproj, dc, g["dn_a_log"], g["dn_dt_bias"], g["dn_norm_w"]), got = _delta_bwd(
        s["c_qkv"], s["proj"], wts["dn_a_log"], wts["dn_dt_bias"], wts["dn_norm_w"], s["states"], dcat,
        exchanges=delta_exchanges(g, got) if delta_exchanges is not None else ())
    dproj, got = _attn_bwd(s["proj"], *tables, s["cat"], s["lse"], dcat, dproj,
                           exchanges=attn_exchanges(got) if attn_exchanges is not None else ())
    dproj, g["dn_conv_w"] = _dnconv_bwd(s["proj"], wts["dn_conv_w"], dc, dproj)
    dh = _matmul(dproj, wts["w_in"], tm=1024, tn=1024, tk=1280, name="mm_dh")
    g["w_in"] = _matmul(dproj, s["h"], ta=True, tm=768, tn=1024, tk=SEQ, name="mm_dw_in", out_dtype=BF16)
    dx, g["norm_pre_mix"] = _norm_bwd(s["x"], wts["norm_pre_mix"], dh, dx1, "norm_pre_mix_bwd")
    return dx, g, got


BIG = ("w_in", "w_out", "ffn_w_in", "ffn_w_out")
COLUMN_SHARDED = ("w_in", "ffn_w_in")
SMALL_SHARDED = ("dn_conv_w", "ffn_conv_w")
REPLICATED = ("dn_a_log", "dn_dt_bias", "dn_norm_w", "ffn_conv_b", "norm_pre_mix", "norm_post_mix", "norm_pre_ffn",
              "norm_post_ffn")
WEIGHTS = ("w_in", "dn_conv_w", "dn_a_log", "dn_dt_bias", "dn_norm_w", "w_out", "ffn_w_in", "ffn_conv_w", "ffn_conv_b",
           "ffn_w_out", "norm_pre_mix", "norm_post_mix", "norm_pre_ffn", "norm_post_ffn")
FULL_SHAPE = dict(dn_conv_w=(DEPTH, 4, 1536), ffn_conv_w=(DEPTH, 3, 2 * D_FF), dn_a_log=(DEPTH, NDH),
                  dn_dt_bias=(DEPTH, NDH), dn_norm_w=(DEPTH, 128), ffn_conv_b=(DEPTH, 2 * D_FF),
                  norm_pre_mix=(DEPTH, D_MODEL), norm_post_mix=(DEPTH, D_MODEL), norm_pre_ffn=(DEPTH, D_MODEL),
                  norm_post_ffn=(DEPTH, D_MODEL))
SMALL_GRAD_ORDER = REPLICATED + SMALL_SHARDED
SMALL_GRAD_ROWS = 544
SMALL_W_ROWS = 56
SMALL_ADAM_ROWS = 232


def _w_in_rows_to_kernel_order(t):
    qkv = t[:QKV_W].reshape(3, N_PAIR, 128, -1).swapaxes(0, 1).reshape(QKV_W, -1)
    return jnp.pad(jnp.concatenate([qkv, t[QKV_W:]], axis=0), ((0, IN_PAD - IN_COLS), (0, 0)))


def _w_in_rows_from_kernel_order(t):
    qkv = t[:QKV_W].reshape(N_PAIR, 3, 128, -1).swapaxes(0, 1).reshape(QKV_W, -1)
    return jnp.concatenate([qkv, t[QKV_W:IN_COLS]], axis=0)


def _interleave_ff_rows(t):
    return t.reshape(2, FF_BLKS, 128, -1).swapaxes(0, 1).reshape(2 * D_FF, -1)


def _deinterleave_ff_rows(t):
    return t.reshape(FF_BLKS, 2, 128, -1).swapaxes(0, 1).reshape(2 * D_FF, -1)


def kernel(x, w_in, dn_conv_w, dn_a_log, dn_dt_bias, dn_norm_w, w_out, ffn_w_in, ffn_conv_w, ffn_conv_b, ffn_w_out, norm_pre_mix, norm_post_mix, norm_pre_ffn, norm_post_ffn, loss_target, m_w_in, m_dn_conv_w, m_dn_a_log, m_dn_dt_bias, m_dn_norm_w, m_w_out, m_ffn_w_in, m_ffn_conv_w, m_ffn_conv_b, m_ffn_w_out, m_norm_pre_mix, m_norm_post_mix, m_norm_pre_ffn, m_norm_post_ffn, v_w_in, v_dn_conv_w, v_dn_a_log, v_dn_dt_bias, v_dn_norm_w, v_w_out, v_ffn_w_in, v_ffn_conv_w, v_ffn_conv_b, v_ffn_w_out, v_norm_pre_mix, v_norm_post_mix, v_norm_pre_ffn, v_norm_post_ffn):
    local = dict(w_in=w_in, dn_conv_w=dn_conv_w, dn_a_log=dn_a_log, dn_dt_bias=dn_dt_bias, dn_norm_w=dn_norm_w,
                 w_out=w_out, ffn_w_in=ffn_w_in, ffn_conv_w=ffn_conv_w, ffn_conv_b=ffn_conv_b, ffn_w_out=ffn_w_out,
                 norm_pre_mix=norm_pre_mix, norm_post_mix=norm_post_mix, norm_pre_ffn=norm_pre_ffn,
                 norm_post_ffn=norm_post_ffn)
    mom_m = dict(w_in=m_w_in, dn_conv_w=m_dn_conv_w, dn_a_log=m_dn_a_log, dn_dt_bias=m_dn_dt_bias,
                 dn_norm_w=m_dn_norm_w, w_out=m_w_out, ffn_w_in=m_ffn_w_in, ffn_conv_w=m_ffn_conv_w,
                 ffn_conv_b=m_ffn_conv_b, ffn_w_out=m_ffn_w_out, norm_pre_mix=m_norm_pre_mix,
                 norm_post_mix=m_norm_post_mix, norm_pre_ffn=m_norm_pre_ffn, norm_post_ffn=m_norm_post_ffn)
    mom_v = dict(w_in=v_w_in, dn_conv_w=v_dn_conv_w, dn_a_log=v_dn_a_log, dn_dt_bias=v_dn_dt_bias,
                 dn_norm_w=v_dn_norm_w, w_out=v_w_out, ffn_w_in=v_ffn_w_in, ffn_conv_w=v_ffn_conv_w,
                 ffn_conv_b=v_ffn_conv_b, ffn_w_out=v_ffn_w_out, norm_pre_mix=v_norm_pre_mix,
                 norm_post_mix=v_norm_post_mix, norm_pre_ffn=v_norm_pre_ffn, norm_post_ffn=v_norm_post_ffn)
    dev = 4 * lax.axis_index("x") + 2 * lax.axis_index("y") + lax.axis_index("c")
    core = lax.axis_index("c").astype(jnp.int32).reshape(1)

    def shard(n, l):
        s = local[n].transpose(0, 2, 1) if n in COLUMN_SHARDED else local[n]
        return s[l].astype(BF16)

    def matrix(n, gathered):
        if n == "w_in":
            return _w_in_rows_to_kernel_order(gathered.reshape(IN_COLS, D_MODEL))
        if n == "ffn_w_in":
            return _interleave_ff_rows(gathered.reshape(2 * D_FF, D_MODEL))
        return gathered.reshape(-1, D_MODEL)

    small_w = _pack([dn_conv_w, ffn_conv_w], SMALL_W_ROWS)
    g_w_in0, g_small = _run_exchange(_gather_exchange([shard("w_in", 0), small_w]), "weights_all_gather")
    n_dn, n_ff = DEPTH * 4 * 192, DEPTH * 3 * 704
    dn_rows = _packed_rows(n_dn)
    sm_dn = g_small[:, :dn_rows].reshape(N_DEV, -1)[:, :n_dn]
    sm_ff = g_small[:, dn_rows:].reshape(N_DEV, -1)[:, :n_ff]
    full_dn_conv = sm_dn.reshape(N_DEV, DEPTH, 4, 192).transpose(1, 2, 0, 3).reshape(DEPTH, 4, 1536)
    full_ff_conv = _interleave_ff(sm_ff.reshape(N_DEV, DEPTH, 3, 704).transpose(1, 2, 0, 3).reshape(DEPTH, 3, 2 * D_FF))

    def small_weights(l):
        wts = dict(dn_conv_w=full_dn_conv[l], ffn_conv_w=full_ff_conv[l], ffn_conv_b=_interleave_ff(_row(ffn_conv_b[l])),
                   dn_a_log=_row(dn_a_log[l], 128), dn_dt_bias=_row(dn_dt_bias[l], 128))
        for n in ("dn_norm_w", "norm_pre_mix", "norm_post_mix", "norm_pre_ffn", "norm_post_ffn"):
            wts[n] = _row(local[n][l])
        return wts

    weights = [small_weights(l) for l in range(DEPTH)]
    weights[0]["w_in"] = matrix("w_in", g_w_in0)

    def gather_behind(wanted):
        def deliver(got):
            for (n, l), g in zip(wanted, got[0]):
                weights[l][n] = matrix(n, g)

        return [_gather_exchange([shard(n, l) for n, l in wanted])], deliver

    nothing = ((), lambda got: None)

    tables = _rope_tables()
    act, saved0 = _layer_fwd(x[0], weights[0], tables, dict(
        attn=gather_behind([("ffn_w_in", 0)]), delta=gather_behind([("w_out", 0), ("ffn_w_out", 0)]),
        ffact=gather_behind([("w_in", 1)])))
    act, saved1 = _layer_fwd(act, weights[1], tables, dict(
        attn=gather_behind([("ffn_w_in", 1)]), delta=gather_behind([("w_out", 1), ("ffn_w_out", 1)]), ffact=nothing))
    loss_part, dact = _loss_fwd_bwd(act, loss_target[0])

    def to_devices(name, t):
        if name == "w_in":
            t = _w_in_rows_from_kernel_order(t)
        if name == "ffn_w_in":
            t = _deinterleave_ff_rows(t)
        return t.reshape(N_DEV, t.shape[0] // N_DEV, t.shape[1])

    def pair_sums(names, layer, to_dev, from_sibling):
        return [_pair_add(gd, r, core, "grads_pair_add_%s_%d" % (n, layer))
                for n, gd, r in zip(names, to_dev, from_sibling)]

    early = ("w_out", "ffn_w_in", "ffn_w_out")
    grads, parts, stash = [None] * DEPTH, {}, {}

    def delta_exchanges1(g, got_ffact):
        stash["early1"] = [to_devices(n, g[n]) for n in early]
        return [_sibling_exchange(stash["early1"])]

    def attn_exchanges1(got_delta):
        return [_chips_exchange(pair_sums(early, 1, stash["early1"], got_delta[0]))]

    dact, grads[1], got_attn = _layer_bwd(dact, weights[1], saved1, tables, (), delta_exchanges1, attn_exchanges1)
    for n, p in zip(early, got_attn[0]):
        parts[n, 1] = p
    w_in1 = [to_devices("w_in", grads[1]["w_in"])]

    def delta_exchanges0(g, got_ffact):
        stash["early0"] = [to_devices(n, g[n]) for n in early]
        return [_chips_exchange(pair_sums(("w_in",), 1, w_in1, got_ffact[0])), _sibling_exchange(stash["early0"])]

    def attn_exchanges0(got_delta):
        parts["w_in", 1], = got_delta[0]
        return [_chips_exchange(pair_sums(early, 0, stash["early0"], got_delta[1]))]

    dact, grads[0], got_attn = _layer_bwd(dact, weights[0], saved0, tables, [_sibling_exchange(w_in1)],
                                          delta_exchanges0, attn_exchanges0)
    for n, p in zip(early, got_attn[0]):
        parts[n, 0] = p
    grad_x = dact[None]
    last = [to_devices("w_in", grads[0]["w_in"])]
    from_sibling = _run_exchange(_sibling_exchange(last), "grads_to_sibling")
    parts["w_in", 0], = _run_exchange(_chips_exchange(pair_sums(("w_in",), 0, last, from_sibling)), "grads_to_chips")

    def small_grad(name):
        t = jnp.stack([grads[l][name] for l in range(DEPTH)])
        if name in ("dn_a_log", "dn_dt_bias"):
            t = t[:, 0, :NDH]
        if name in ("ffn_conv_w", "ffn_conv_b"):
            t = _deinterleave_ff(t)
        return t.reshape(FULL_SHAPE[name])

    small_part = _pack([small_grad(n) for n in SMALL_GRAD_ORDER] + [loss_part[0, :1]], SMALL_GRAD_ROWS)
    small_sum = _all_gather_sum_small(small_part)
    small_g = dict(zip(SMALL_GRAD_ORDER + ("loss",), _unpack(small_sum, [FULL_SHAPE[n] for n in SMALL_GRAD_ORDER] + [(1,)])))
    loss = small_g["loss"][0]
    small_g["dn_conv_w"] = lax.dynamic_slice_in_dim(small_g["dn_conv_w"], dev * 192, 192, axis=2)
    small_g["ffn_conv_w"] = lax.dynamic_slice_in_dim(small_g["ffn_conv_w"], dev * 704, 704, axis=2)

    out_g, out_d, out_m, out_v = {}, {}, {}, {}
    for n in BIG:
        turn = (lambda t: t.transpose(0, 2, 1)) if n in COLUMN_SHARDED else (lambda t: t)
        outs = _adamw_sharded([parts[n, l] for l in range(DEPTH)], turn(local[n]), turn(mom_m[n]), turn(mom_v[n]),
                              ADAM_TILE[n], "adamw_" + n)
        out_g[n], out_d[n], out_m[n], out_v[n] = [turn(t) for t in outs]
    shapes = [small_g[n].shape for n in SMALL_GRAD_ORDER]
    d_s, m_s, v_s = _adamw_small(_pack([small_g[n] for n in SMALL_GRAD_ORDER], SMALL_ADAM_ROWS),
                                 _pack([local[n] for n in SMALL_GRAD_ORDER], SMALL_ADAM_ROWS),
                                 _pack([mom_m[n] for n in SMALL_GRAD_ORDER], SMALL_ADAM_ROWS),
                                 _pack([mom_v[n] for n in SMALL_GRAD_ORDER], SMALL_ADAM_ROWS))
    for n, d, m, v in zip(SMALL_GRAD_ORDER, _unpack(d_s, shapes), _unpack(m_s, shapes), _unpack(v_s, shapes)):
        out_g[n], out_d[n], out_m[n], out_v[n] = small_g[n], d, m, v
    return (loss, grad_x, *[out_g[n] for n in WEIGHTS], *[out_d[n] for n in WEIGHTS],
            *[out_m[n] for n in WEIGHTS], *[out_v[n] for n in WEIGHTS])
```

```python
import functools
import math

import jax
import jax.numpy as jnp
from jax import lax
from jax.experimental import pallas as pl
from jax.experimental.pallas import tpu as pltpu

F32 = jnp.float32
BF16 = jnp.bfloat16
MESH = pl.DeviceIdType.MESH

N_DEV = 8
SEQ = 2048
D_MODEL = 1024
DEPTH = 2
N_PAIR = 4
HEAD_DIM = 64
ATTN_W = 512
ATTN_BLK = 128
DILATIONS = (1, 4, 16)
SEGMENT_BLOCKS = (16, 4, 1)
N_BLK = SEQ // ATTN_BLK
NDH = 4
CH = 64
NCH = SEQ // CH
IN_COLS = 3592
IN_PAD = 3840
QKV_W = 3 * ATTN_W
DN_QKV_BLK0 = QKV_W // 128
DN_QKV_BLKS = 1536 // 128
DN_Z_COL = 3072
DN_TAIL_BLK = 3584 // 128
D_FF = 2816
FF_BLKS = D_FF // 128
EPS = 1e-6
NEG = -1e30
ROPE_THETA = 10000.0

ADAM_LR, ADAM_B1, ADAM_B2, ADAM_EPS, ADAM_WD, ADAM_STEP = 0.001, 0.9, 0.999, 1e-08, 0.01, 10

VMEM_LIMIT = 56 * 1024 * 1024


def _cp(*sem):
    return pltpu.CompilerParams(dimension_semantics=sem, vmem_limit_bytes=VMEM_LIMIT)


class Exchange:
    def __init__(self, operands, out_shapes, sems, start, middle, finish):
        self.operands, self.out_shapes, self.sems = list(operands), list(out_shapes), list(sems)
        self.start, self.middle, self.finish = start, middle, finish


HBM_SPEC = pl.BlockSpec(memory_space=pltpu.HBM)


def _hosted_call(body, *, name, steps, in_specs, out_specs, out_shape, scratch_shapes, operands, exchanges=(),
                 aliases=None):
    n_in, n_out, n_scr = len(in_specs), len(out_specs), len(scratch_shapes)

    def take(refs, pos, counts):
        groups = []
        for c in counts:
            groups.append(refs[pos:pos + c])
            pos += c
        return groups, pos

    def full_body(*refs):
        ins, pos = refs[:n_in], n_in
        ex_ins, pos = take(refs, pos, [len(e.operands) for e in exchanges])
        outs, pos = refs[pos:pos + n_out], pos + n_out
        ex_outs, pos = take(refs, pos, [len(e.out_shapes) for e in exchanges])
        scr, pos = refs[pos:pos + n_scr], pos + n_scr
        ex_sems, pos = take(refs, pos, [len(e.sems) for e in exchanges])
        step = pl.program_id(0)
        for e, a, b, s in zip(exchanges, ex_ins, ex_outs, ex_sems):
            pl.when(step == 0)(functools.partial(e.start, a, b, s))
            if e.middle is not None:
                pl.when(step == (3 * steps) // 4)(functools.partial(e.middle, a, b, s))
        body(*ins, *outs, *scr)
        for e, a, b, s in zip(exchanges, ex_ins, ex_outs, ex_sems):
            pl.when(step == steps - 1)(functools.partial(e.finish, a, b, s))

    n_ex_in = sum(len(e.operands) for e in exchanges)
    n_ex_out = sum(len(e.out_shapes) for e in exchanges)
    results = pl.pallas_call(
        full_body, name=name, grid=(steps,),
        in_specs=list(in_specs) + [HBM_SPEC] * n_ex_in,
        out_specs=list(out_specs) + [HBM_SPEC] * n_ex_out,
        out_shape=list(out_shape) + [s for e in exchanges for s in e.out_shapes],
        scratch_shapes=list(scratch_shapes) + [s for e in exchanges for s in e.sems],
        input_output_aliases=aliases or {},
        compiler_params=_cp("arbitrary"),
    )(*operands, *[a for e in exchanges for a in e.operands])
    ex_results, _ = take(results, n_out, [len(e.out_shapes) for e in exchanges])
    return results[:n_out], ex_results


def _dot(a, b, dims, precision=None):
    if precision is None:
        a = a.astype(BF16)
        b = b.astype(BF16)
    return lax.dot_general(a, b, (dims, ((), ())), preferred_element_type=F32, precision=precision)


def _make_mm(precision):
    @jax.custom_vjp
    def nn(a, b):
        return _dot(a, b, ((1,), (0,)), precision)

    @jax.custom_vjp
    def nt(a, b):
        return _dot(a, b, ((1,), (1,)), precision)

    @jax.custom_vjp
    def tn(a, b):
        return _dot(a, b, ((0,), (0,)), precision)

    nn.defvjp(lambda a, b: (nn(a, b), (a, b)), lambda r, g: (nt(g, r[1]), tn(r[0], g)))
    nt.defvjp(lambda a, b: (nt(a, b), (a, b)), lambda r, g: (nn(g, r[1]), tn(g, r[0])))
    tn.defvjp(lambda a, b: (tn(a, b), (a, b)), lambda r, g: (nt(r[1], g), nn(r[0], g)))
    return nn, nt, tn


def _matmul(a, b, *, ta=False, tb=False, tm, tn, tk, name, out_dtype=F32):
    (k_dim, m_dim) = a.shape if ta else a.shape[::-1]
    (n_dim, k2) = b.shape if tb else b.shape[::-1]
    assert k_dim == k2 and m_dim % tm == 0 and n_dim % tn == 0 and k_dim % tk == 0, (a.shape, b.shape, tm, tn, tk)
    nk = k_dim // tk
    dims = ((0 if ta else 1,), (1 if tb else 0,))

    def body(a_ref, b_ref, o_ref, *acc):
        p = _dot(a_ref[...], b_ref[...], dims)
        if nk == 1:
            o_ref[...] = p.astype(out_dtype)
            return
        acc_ref, k = acc[0], pl.program_id(2)

        @pl.when(k == 0)
        def _():
            acc_ref[...] = p

        @pl.when(k > 0)
        def _():
            acc_ref[...] += p

        @pl.when(k == nk - 1)
        def _():
            o_ref[...] = acc_ref[...].astype(out_dtype)

    a_spec = pl.BlockSpec((tk, tm), lambda i, j, k: (k, i)) if ta else pl.BlockSpec((tm, tk), lambda i, j, k: (i, k))
    b_spec = pl.BlockSpec((tn, tk), lambda i, j, k: (j, k)) if tb else pl.BlockSpec((tk, tn), lambda i, j, k: (k, j))
    return pl.pallas_call(
        body, name=name,
        grid=(m_dim // tm, n_dim // tn, nk),
        in_specs=[a_spec, b_spec],
        out_specs=pl.BlockSpec((tm, tn), lambda i, j, k: (i, j)),
        out_shape=jax.ShapeDtypeStruct((m_dim, n_dim), out_dtype),
        scratch_shapes=[pltpu.VMEM((tm, tn), F32)] if nk > 1 else [],
        compiler_params=_cp("parallel", "parallel", "arbitrary"),
    )(a, b)


NORM_ROWS = 256


def _rms(x, w):
    return x * lax.rsqrt(jnp.mean(x * x, axis=1, keepdims=True) + EPS) * w


def _norm_fwd(x, w_row, name, out_dtype=BF16):
    def body(x_ref, w_ref, o_ref):
        o_ref[...] = _rms(x_ref[...], w_ref[...]).astype(out_dtype)

    return pl.pallas_call(
        body, name=name, grid=(SEQ // NORM_ROWS,),
        in_specs=[pl.BlockSpec((NORM_ROWS, D_MODEL), lambda i: (i, 0)), pl.BlockSpec((1, D_MODEL), lambda i: (0, 0))],
        out_specs=pl.BlockSpec((NORM_ROWS, D_MODEL), lambda i: (i, 0)),
        out_shape=jax.ShapeDtypeStruct((SEQ, D_MODEL), out_dtype),
        compiler_params=_cp("parallel"),
    )(x, w_row)


def _resnorm_fwd(x, f, w_row, name):
    def body(x_ref, f_ref, w_ref, o_ref):
        o_ref[...] = x_ref[...] + _rms(f_ref[...], w_ref[...])

    blk = pl.BlockSpec((NORM_ROWS, D_MODEL), lambda i: (i, 0))
    return pl.pallas_call(
        body, name=name, grid=(SEQ // NORM_ROWS,),
        in_specs=[blk, blk, pl.BlockSpec((1, D_MODEL), lambda i: (0, 0))],
        out_specs=blk, out_shape=jax.ShapeDtypeStruct((SEQ, D_MODEL), F32),
        compiler_params=_cp("parallel"),
    )(x, f, w_row)


def _norm_bwd(x, w_row, dy, add, name, dx_dtype=F32):
    has_add = add is not None

    def body(*refs):
        if has_add:
            x_ref, w_ref, dy_ref, add_ref, dx_ref, dw_ref = refs
        else:
            x_ref, w_ref, dy_ref, dx_ref, dw_ref = refs
        _, vjp = jax.vjp(_rms, x_ref[...], w_ref[...])
        dx, dw = vjp(dy_ref[...])
        dx_ref[...] = (dx + add_ref[...] if has_add else dx).astype(dx_dtype)

        @pl.when(pl.program_id(0) == 0)
        def _():
            dw_ref[...] = jnp.zeros_like(dw_ref)

        dw_ref[...] += dw

    blk = pl.BlockSpec((NORM_ROWS, D_MODEL), lambda i: (i, 0))
    row = pl.BlockSpec((1, D_MODEL), lambda i: (0, 0))
    ins = [x, w_row, dy] + ([add] if has_add else [])
    return pl.pallas_call(
        body, name=name, grid=(SEQ // NORM_ROWS,),
        in_specs=[blk, row, blk] + ([blk] if has_add else []),
        out_specs=[blk, row],
        out_shape=[jax.ShapeDtypeStruct((SEQ, D_MODEL), dx_dtype), jax.ShapeDtypeStruct((1, D_MODEL), F32)],
        compiler_params=_cp("arbitrary"),
    )(*ins)


def _loss_fwd_bwd(y, target):
    def body(y_ref, t_ref, loss_ref, dy_ref):
        err = y_ref[...] - t_ref[...]
        dy_ref[...] = err * (1.0 / D_MODEL)

        @pl.when(pl.program_id(0) == 0)
        def _():
            loss_ref[...] = jnp.zeros_like(loss_ref)

        part = jnp.sum(jnp.sum(err * err, axis=1, keepdims=True) * (1.0 / D_MODEL), axis=0, keepdims=True)
        loss_ref[...] += 0.5 * jnp.broadcast_to(part, loss_ref.shape)

    blk = pl.BlockSpec((NORM_ROWS, D_MODEL), lambda i: (i, 0))
    return pl.pallas_call(
        body, name="loss", grid=(SEQ // NORM_ROWS,),
        in_specs=[blk, blk],
        out_specs=[pl.BlockSpec((1, 128), lambda i: (0, 0)), blk],
        out_shape=[jax.ShapeDtypeStruct((1, 128), F32), jax.ShapeDtypeStruct((SEQ, D_MODEL), F32)],
        compiler_params=_cp("arbitrary"),
    )(y, target)


def _make_shift(j):
    def down(x):
        row = lax.broadcasted_iota(jnp.int32, x.shape, 0)
        return jnp.where(row >= j, pltpu.roll(x, j, 0), 0.0)

    def up(x):
        n = x.shape[0]
        row = lax.broadcasted_iota(jnp.int32, x.shape, 0)
        return jnp.where(row < n - j, pltpu.roll(x, n - j, 0), 0.0)

    f = jax.custom_vjp(down)
    f.defvjp(lambda x: (down(x), None), lambda _, g: (up(g),))
    return f


_SHIFT = {j: _make_shift(j) for j in (1, 2, 3)}


def _causal_conv(x, taps):
    n = len(taps)
    acc = x * taps[n - 1]
    for k in range(n - 1):
        acc = acc + _SHIFT[n - 1 - k](x) * taps[k]
    return acc


def _tap_rows(w_ref, lanes=slice(None)):
    return tuple(w_ref[k:k + 1, lanes] for k in range(w_ref.shape[0]))


def _sigmoid(x):
    return 1.0 / (1.0 + jnp.exp(-x))


def _silu(x):
    return x * _sigmoid(x)


def _softplus(x):
    return jnp.maximum(x, 0.0) + jnp.log(1.0 + jnp.exp(-jnp.abs(x)))


def _gelu_tanh(x):
    return 0.5 * x * (1.0 + jnp.tanh(math.sqrt(2.0 / math.pi) * (x + 0.044715 * (x * x * x))))


def _dnconv_fn(x, taps):
    return _silu(_causal_conv(x, taps))


def _dnconv_fwd(proj, conv_w):
    def body(x_ref, w_ref, o_ref):
        o_ref[...] = _dnconv_fn(x_ref[...], _tap_rows(w_ref)).astype(BF16)

    return pl.pallas_call(
        body, name="dnconv_fwd", grid=(DN_QKV_BLKS,),
        in_specs=[pl.BlockSpec((SEQ, 128), lambda j: (0, DN_QKV_BLK0 + j)), pl.BlockSpec((4, 128), lambda j: (0, j))],
        out_specs=pl.BlockSpec((SEQ, 128), lambda j: (0, j)),
        out_shape=jax.ShapeDtypeStruct((SEQ, 1536), BF16),
        compiler_params=_cp("parallel"),
    )(proj, conv_w)


def _dnconv_bwd(proj, conv_w, dc, dproj):
    def body(x_ref, w_ref, dc_ref, _, dx_ref, dw_ref):
        _, vjp = jax.vjp(_dnconv_fn, x_ref[...], _tap_rows(w_ref))
        dx, dw = vjp(dc_ref[...])
        dx_ref[...] = dx.astype(BF16)
        for k, row in enumerate(dw):
            dw_ref[k:k + 1, :] = row

    return pl.pallas_call(
        body, name="dnconv_bwd", grid=(DN_QKV_BLKS,),
        in_specs=[pl.BlockSpec((SEQ, 128), lambda j: (0, DN_QKV_BLK0 + j)), pl.BlockSpec((4, 128), lambda j: (0, j)),
                  pl.BlockSpec((SEQ, 128), lambda j: (0, j)), pl.BlockSpec(memory_space=pl.ANY)],
        out_specs=[pl.BlockSpec((SEQ, 128), lambda j: (0, DN_QKV_BLK0 + j)), pl.BlockSpec((4, 128), lambda j: (0, j))],
        out_shape=[jax.ShapeDtypeStruct((SEQ, IN_PAD), BF16), jax.ShapeDtypeStruct((4, 1536), F32)],
        input_output_aliases={3: 0},
        compiler_params=_cp("parallel"),
    )(proj, conv_w, dc, dproj)


def _ffact_fn(pg, pu, wg, wu, bg, bu):
    return _gelu_tanh(_causal_conv(pg, wg) + bg) * (_causal_conv(pu, wu) + bu)


def _ffact_args(p_ref, w_ref, b_ref):
    g, u = slice(0, 128), slice(128, 256)
    return (p_ref[:, g].astype(F32), p_ref[:, u].astype(F32), _tap_rows(w_ref, g), _tap_rows(w_ref, u),
            b_ref[:, g], b_ref[:, u])


def _ffact_fwd(pre, conv_w, conv_b, exchanges=()):
    def body(p_ref, w_ref, b_ref, o_ref):
        o_ref[...] = _ffact_fn(*_ffact_args(p_ref, w_ref, b_ref)).astype(BF16)

    (act,), results = _hosted_call(
        body, name="ffact_fwd", steps=FF_BLKS,
        in_specs=[pl.BlockSpec((SEQ, 256), lambda j: (0, j)), pl.BlockSpec((3, 256), lambda j: (0, j)),
                  pl.BlockSpec((1, 256), lambda j: (0, j))],
        out_specs=[pl.BlockSpec((SEQ, 128), lambda j: (0, j))],
        out_shape=[jax.ShapeDtypeStruct((SEQ, D_FF), BF16)],
        scratch_shapes=[], operands=(pre, conv_w, conv_b), exchanges=exchanges)
    return act, results


def _ffact_bwd(pre, conv_w, conv_b, dact, exchanges=()):
    def body(p_ref, w_ref, b_ref, da_ref, dp_ref, dw_ref, db_ref):
        _, vjp = jax.vjp(_ffact_fn, *_ffact_args(p_ref, w_ref, b_ref))
        dpg, dpu, dwg, dwu, dbg, dbu = vjp(da_ref[...].astype(F32))
        dp_ref[:, 0:128] = dpg.astype(BF16)
        dp_ref[:, 128:256] = dpu.astype(BF16)
        for k in range(3):
            dw_ref[k:k + 1, 0:128] = dwg[k]
            dw_ref[k:k + 1, 128:256] = dwu[k]
        db_ref[:, 0:128] = dbg
        db_ref[:, 128:256] = dbu

    return _hosted_call(
        body, name="ffact_bwd", steps=FF_BLKS,
        in_specs=[pl.BlockSpec((SEQ, 256), lambda j: (0, j)), pl.BlockSpec((3, 256), lambda j: (0, j)),
                  pl.BlockSpec((1, 256), lambda j: (0, j)), pl.BlockSpec((SEQ, 128), lambda j: (0, j))],
        out_specs=[pl.BlockSpec((SEQ, 256), lambda j: (0, j)), pl.BlockSpec((3, 256), lambda j: (0, j)),
                   pl.BlockSpec((1, 256), lambda j: (0, j))],
        out_shape=[jax.ShapeDtypeStruct((SEQ, 2 * D_FF), BF16), jax.ShapeDtypeStruct((3, 2 * D_FF), F32),
                   jax.ShapeDtypeStruct((1, 2 * D_FF), F32)],
        scratch_shapes=[], operands=(pre, conv_w, conv_b, dact), exchanges=exchanges)


def _interleave_ff(t):
    lead = t.shape[:-1]
    return t.reshape(lead + (2, FF_BLKS, 128)).swapaxes(-3, -2).reshape(lead + (2 * D_FF,))


def _deinterleave_ff(t):
    lead = t.shape[:-1]
    return t.reshape(lead + (FF_BLKS, 2, 128)).swapaxes(-3, -2).reshape(lead + (2 * D_FF,))


def _rope_tables():
    inv = 1.0 / (ROPE_THETA ** (jnp.arange(0, HEAD_DIM, 2, dtype=F32) / HEAD_DIM))
    ang = jnp.arange(SEQ, dtype=F32)[:, None] * inv[None, :]
    cos = jnp.tile(jnp.cos(ang), (1, 4))
    sin = jnp.tile(jnp.sin(ang), (1, 4))
    sign = jnp.where((jnp.arange(128) % HEAD_DIM) < HEAD_DIM // 2, -1.0, 1.0).astype(F32)
    return cos, sin * sign[None, :]


def _rope(x, cos, sin_signed):
    lane = lax.broadcasted_iota(jnp.int32, x.shape, 1)
    partner = jnp.where((lane % HEAD_DIM) < HEAD_DIM // 2, pltpu.roll(x, 128 - HEAD_DIM // 2, 1),
                        pltpu.roll(x, HEAD_DIM // 2, 1))
    return x * cos + partner * sin_signed


def _head_masks():
    lane = lax.broadcasted_iota(jnp.int32, (1, 128), 1)
    return [(lane // HEAD_DIM) == h for h in range(2)]


def _both_heads(x):
    return jnp.concatenate([jnp.where(hm, x, 0.0)[None] for hm in _head_masks()], axis=0)


def _block_keys(branch, k_s, v_s, rows, prows, has_prev):
    a = lax.broadcasted_iota(jnp.int32, (ATTN_BLK, ATTN_BLK), 0)
    c = lax.broadcasted_iota(jnp.int32, (ATTN_BLK, ATTN_BLK), 1)
    keys, values, mask = k_s[rows, :], v_s[rows, :], c <= a
    if SEGMENT_BLOCKS[branch] > 1:
        keys = jnp.concatenate([k_s[prows, :], keys], axis=0)
        values = jnp.concatenate([v_s[prows, :], values], axis=0)
        mask = jnp.concatenate([(c >= a) & has_prev, mask], axis=1)
    twice = lambda t: jnp.broadcast_to(t[None], (2,) + t.shape)
    return twice(keys), twice(values), mask


def _block_rows(branch, t):
    d, per_seg = DILATIONS[branch], SEGMENT_BLOCKS[branch]
    if d == 1:
        start = pl.multiple_of(t * ATTN_BLK, ATTN_BLK)
        prev = pl.multiple_of(jnp.maximum(t - 1, 0) * ATTN_BLK, ATTN_BLK)
        return pl.ds(start, ATTN_BLK), pl.ds(prev, ATTN_BLK), t > 0
    r, n = t // per_seg, t % per_seg
    start = n * (ATTN_BLK * d) + r
    prev = jnp.maximum(n - 1, 0) * (ATTN_BLK * d) + r
    return pl.ds(start, ATTN_BLK, stride=d), pl.ds(prev, ATTN_BLK, stride=d), n > 0


def _attn_fwd(proj, cos, sin_signed, exchanges=()):
    scale = HEAD_DIM ** -0.5

    def body(qkv_ref, cos_ref, sin_ref, out_ref, lse_ref, q_s, k_s, v_s, *branch_s):
        o_s, l_s = branch_s[:3], branch_s[3:]
        q_s[...] = _rope(qkv_ref[:, 0:128], cos_ref[...], sin_ref[...])
        k_s[...] = _rope(qkv_ref[:, 128:256], cos_ref[...], sin_ref[...])
        v_s[...] = qkv_ref[:, 256:384]
        heads = _head_masks()
        for branch in range(3):
            def block(t, carry, branch=branch):
                rows, prows, has_prev = _block_rows(branch, t)
                keys, values, mask = _block_keys(branch, k_s, v_s, rows, prows, has_prev)
                s = jnp.where(mask, BMM_NT(_both_heads(q_s[rows, :]), keys) * scale, NEG)
                m = jnp.max(s, axis=2, keepdims=True)
                e = jnp.exp(s - m)
                l = jnp.sum(e, axis=2, keepdims=True)
                o = BMM(e, values) / l
                lse_b = m + jnp.log(l)
                o_s[branch][rows, :] = jnp.where(heads[0], o[0], o[1])
                l_s[branch][rows, :] = jnp.where(heads[0], lse_b[0], lse_b[1])
                return carry

            lax.fori_loop(0, N_BLK, block, 0, unroll=4)
        l0, l1, l2 = l_s[0][...], l_s[1][...], l_s[2][...]
        m = jnp.maximum(jnp.maximum(l0, l1), l2)
        w0, w1, w2 = jnp.exp(l0 - m), jnp.exp(l1 - m), jnp.exp(l2 - m)
        den = w0 + w1 + w2
        out_ref[...] = (w0 * o_s[0][...] + w1 * o_s[1][...] + w2 * o_s[2][...]) / den
        lse_ref[...] = m + jnp.log(den)

    tab = pl.BlockSpec((SEQ, 128), lambda j: (0, 0))
    col = pl.BlockSpec((SEQ, 128), lambda j: (0, j))
    return _hosted_call(
        body, name="attn_fwd", steps=N_PAIR,
        in_specs=[pl.BlockSpec((SEQ, 384), lambda j: (0, j)), tab, tab],
        out_specs=[col, col],
        out_shape=[jax.ShapeDtypeStruct((SEQ, 2 * ATTN_W), F32), jax.ShapeDtypeStruct((SEQ, ATTN_W), F32)],
        scratch_shapes=[pltpu.VMEM((SEQ, 128), F32)] * 9,
        operands=(proj, cos, sin_signed), exchanges=exchanges)


def _attn_bwd(proj, cos, sin_signed, cat, lse, dcat, dproj, exchanges=()):
    scale = HEAD_DIM ** -0.5

    def body(qkv_ref, cos_ref, sin_ref, out_ref, lse_ref, do_ref, _, dqkv_ref, q_s, k_s, v_s, dq_s, dk_s, dv_s,
             dod_s):
        q_s[...] = _rope(qkv_ref[:, 0:128], cos_ref[...], sin_ref[...])
        k_s[...] = _rope(qkv_ref[:, 128:256], cos_ref[...], sin_ref[...])
        v_s[...] = qkv_ref[:, 256:384]
        dq_s[...] = jnp.zeros_like(dq_s)
        dk_s[...] = jnp.zeros_like(dk_s)
        dv_s[...] = jnp.zeros_like(dv_s)
        dod_s[...] = do_ref[...] * out_ref[...]
        heads = _head_masks()
        for branch in range(3):
            def block(t, carry, branch=branch):
                rows, prows, has_prev = _block_rows(branch, t)
                keys, values, mask = _block_keys(branch, k_s, v_s, rows, prows, has_prev)
                q2, do2 = _both_heads(q_s[rows, :]), _both_heads(do_ref[rows, :])
                lse_b, dod = lse_ref[rows, :], dod_s[rows, :]
                lse2 = jnp.concatenate(
                    [jnp.max(jnp.where(hm, lse_b, NEG), axis=1, keepdims=True)[None] for hm in heads], axis=0)
                delta = jnp.concatenate(
                    [jnp.sum(jnp.where(hm, dod, 0.0), axis=1, keepdims=True)[None] for hm in heads], axis=0)
                p = jnp.exp(jnp.where(mask, BMM_NT(q2, keys) * scale, NEG) - lse2)
                ds = p * (BMM_NT(do2, values) - delta) * scale
                dq = BMM(ds, keys)
                dk = BMM_TN(ds, q2)
                dv = BMM_TN(p, do2)
                dk, dv = dk[0] + dk[1], dv[0] + dv[1]
                dq_s[rows, :] += jnp.where(heads[0], dq[0], dq[1])
                if SEGMENT_BLOCKS[branch] > 1:
                    dk_s[rows, :] += dk[ATTN_BLK:]
                    dv_s[rows, :] += dv[ATTN_BLK:]

                    @pl.when(has_prev)
                    def _():
                        dk_s[prows, :] += dk[:ATTN_BLK]
                        dv_s[prows, :] += dv[:ATTN_BLK]
                else:
                    dk_s[rows, :] += dk
                    dv_s[rows, :] += dv
                return carry

            lax.fori_loop(0, N_BLK, block, 0, unroll=4)
        dqkv_ref[:, 0:128] = _rope(dq_s[...], cos_ref[...], -sin_ref[...]).astype(BF16)
        dqkv_ref[:, 128:256] = _rope(dk_s[...], cos_ref[...], -sin_ref[...]).astype(BF16)
        dqkv_ref[:, 256:384] = dv_s[...].astype(BF16)

    tab = pl.BlockSpec((SEQ, 128), lambda j: (0, 0))
    col = pl.BlockSpec((SEQ, 128), lambda j: (0, j))
    qkv = pl.BlockSpec((SEQ, 384), lambda j: (0, j))
    (dproj,), results = _hosted_call(
        body, name="attn_bwd", steps=N_PAIR,
        in_specs=[qkv, tab, tab, col, col, col, pl.BlockSpec(memory_space=pl.ANY)],
        out_specs=[qkv],
        out_shape=[jax.ShapeDtypeStruct((SEQ, IN_PAD), BF16)],
        scratch_shapes=[pltpu.VMEM((SEQ, 128), F32)] * 7,
        operands=(proj, cos, sin_signed, cat, lse, dcat, dproj), exchanges=exchanges, aliases={6: 0})
    return dproj, results


def _bdot(a, b, dims, precision=None):
    if precision is None:
        a = a.astype(BF16)
        b = b.astype(BF16)
    return lax.dot_general(a, b, (dims, ((0,), (0,))), preferred_element_type=F32, precision=precision)


def _make_bmm(precision):
    @jax.custom_vjp
    def nn(a, b):
        return _bdot(a, b, ((2,), (1,)), precision)

    @jax.custom_vjp
    def nt(a, b):
        return _bdot(a, b, ((2,), (2,)), precision)

    @jax.custom_vjp
    def tn(a, b):
        return _bdot(a, b, ((1,), (1,)), precision)

    nn.defvjp(lambda a, b: (nn(a, b), (a, b)), lambda r, g: (nt(g, r[1]), tn(r[0], g)))
    nt.defvjp(lambda a, b: (nt(a, b), (a, b)), lambda r, g: (nn(g, r[1]), tn(g, r[0])))
    tn.defvjp(lambda a, b: (tn(a, b), (a, b)), lambda r, g: (nt(r[1], g), nn(r[0], g)))
    return nn, nt, tn


BMM, BMM_NT, BMM_TN = _make_bmm(None)
BMM3, BMM3_NT, BMM3_TN = _make_bmm(lax.Precision.HIGH)
MM3, _, _ = _make_mm(lax.Precision.HIGH)


def _head_lanes(t, off):
    lane = lax.broadcasted_iota(jnp.int32, (1, 128), 1)
    return jnp.concatenate(
        [jnp.sum(t * (lane == off + h).astype(F32), axis=1, keepdims=True)[None] for h in range(NDH)], axis=0)


@jax.custom_vjp
def _unit_lower_inverse(a_mat):
    c = a_mat.shape[1]
    eye = (lax.broadcasted_iota(jnp.int32, (c, c), 0) == lax.broadcasted_iota(jnp.int32, (c, c), 1)).astype(F32)
    power = -a_mat
    t_inv = eye + power
    for _ in range(5):
        power = BMM3(power, power)
        t_inv = t_inv + BMM3(t_inv, power)
    return t_inv


def _unit_lower_inverse_fwd(a_mat):
    t_inv = _unit_lower_inverse(a_mat)
    return t_inv, t_inv


def _unit_lower_inverse_bwd(t_inv, d_inv):
    return (-BMM3_NT(BMM3_TN(t_inv, d_inv), t_inv),)


_unit_lower_inverse.defvjp(_unit_lower_inverse_fwd, _unit_lower_inverse_bwd)


DN_STEP_CHUNKS = 4
DN_STEP_ROWS = DN_STEP_CHUNKS * CH
DN_STEPS = NCH // DN_STEP_CHUNKS
DN_BATCH = DN_STEP_CHUNKS * NDH


def _delta_chunks(qr, kr, vr, z, tail, alog_row, dt_row, nw, state):
    c = qr.shape[1]
    tails = [tail[CH * n:CH * (n + 1)] for n in range(DN_STEP_CHUNKS)]
    per_chunk = lambda t: jnp.concatenate([t] * DN_STEP_CHUNKS, axis=0)
    beta = _sigmoid(jnp.concatenate([_head_lanes(t, 0) for t in tails], axis=0))
    a_raw = jnp.concatenate([_head_lanes(t, NDH) for t in tails], axis=0)
    g = -jnp.exp(per_chunk(_head_lanes(alog_row, 0))) * _softplus(a_raw + per_chunk(_head_lanes(dt_row, 0)))

    q = qr * lax.rsqrt(jnp.sum(qr * qr, axis=2, keepdims=True) + EPS) * (128 ** -0.5)
    k = kr * lax.rsqrt(jnp.sum(kr * kr, axis=2, keepdims=True) + EPS)

    ri = lax.broadcasted_iota(jnp.int32, (c, c), 0)
    ci = lax.broadcasted_iota(jnp.int32, (c, c), 1)
    tril = ri >= ci
    lane = lax.broadcasted_iota(jnp.int32, (1, 128), 1)
    pick = [(lane == b).astype(F32) for b in range(DN_BATCH)]
    g_lanes = sum(g[b] * pick[b] for b in range(DN_BATCH))
    g_sums = MM3(tril.astype(F32), g_lanes)
    gc = jnp.concatenate([jnp.sum(g_sums * pick[b], axis=1, keepdims=True)[None] for b in range(DN_BATCH)],
                         axis=0)
    g_row = jnp.swapaxes(jnp.broadcast_to(gc, (DN_BATCH, c, c)), 1, 2)
    decay = jnp.where(tril, jnp.exp(jnp.where(tril, gc - g_row, 0.0)), 0.0)
    kb = k * beta
    t_inv = _unit_lower_inverse(jnp.where(ri > ci, BMM_NT(kb, k) * decay, 0.0))
    eg = jnp.exp(gc)
    u = BMM(t_inv, vr * beta)
    w = BMM(t_inv, kb * eg)
    qk = BMM_NT(q, k) * decay
    g_tot = jnp.sum(g, axis=1, keepdims=True)
    q_dec = q * eg
    k_dec = k * jnp.exp(g_tot - gc)
    outs = []
    for n in range(DN_STEP_CHUNKS):
        heads = slice(NDH * n, NDH * (n + 1))
        v_new = u[heads] - BMM(w[heads], state)
        outs.append(BMM(q_dec[heads], state) + BMM(qk[heads], v_new))
        state = state * jnp.exp(g_tot[heads]) + BMM_TN(k_dec[heads], v_new)
    o = jnp.concatenate(outs, axis=0)
    on = o * lax.rsqrt(jnp.mean(o * o, axis=2, keepdims=True) + EPS) * nw
    return on * _silu(z), state


def _heads(v, off=0):
    return jnp.concatenate([v[None, CH * n:CH * (n + 1), off + 128 * h:off + 128 * (h + 1)]
                            for n in range(DN_STEP_CHUNKS) for h in range(NDH)], axis=0)


def _unheads(t):
    return jnp.concatenate([jnp.concatenate([t[NDH * n + h] for h in range(NDH)], axis=1)
                            for n in range(DN_STEP_CHUNKS)], axis=0)


def _delta_fwd(c_qkv, proj, alog_row, dt_row, nw, cat, exchanges=()):
    def body(c_ref, z_ref, tail_ref, al_ref, dt_ref, nw_ref, _, y_ref, st_ref, state):
        @pl.when(pl.program_id(0) == 0)
        def _():
            state[...] = jnp.zeros_like(state)

        cv = c_ref[...].astype(F32)
        st_ref[0] = state[...]
        y, new_state = _delta_chunks(_heads(cv), _heads(cv, 512), _heads(cv, 1024), _heads(z_ref[...]), tail_ref[...],
                                     al_ref[...], dt_ref[...], nw_ref[...], state[...])
        y_ref[...] = _unheads(y)
        state[...] = new_state

    row = pl.BlockSpec((1, 128), lambda n: (0, 0))
    rows = DN_STEP_ROWS
    return _hosted_call(
        body, name="delta_fwd", steps=DN_STEPS,
        in_specs=[pl.BlockSpec((rows, 1536), lambda n: (n, 0)), pl.BlockSpec((rows, 512), lambda n: (n, DN_Z_COL // 512)),
                  pl.BlockSpec((rows, 128), lambda n: (n, DN_TAIL_BLK)), row, row, row, pl.BlockSpec(memory_space=pl.ANY)],
        out_specs=[pl.BlockSpec((rows, 512), lambda n: (n, 1)),
                   pl.BlockSpec((1, NDH, 128, 128), lambda n: (n, 0, 0, 0))],
        out_shape=[jax.ShapeDtypeStruct((SEQ, 2 * ATTN_W), F32), jax.ShapeDtypeStruct((DN_STEPS, NDH, 128, 128), F32)],
        scratch_shapes=[pltpu.VMEM((NDH, 128, 128), F32)],
        operands=(c_qkv, proj, proj, alog_row, dt_row, nw, cat), exchanges=exchanges, aliases={6: 0})


def _delta_bwd(c_qkv, proj, alog_row, dt_row, nw, states, dcat, exchanges=()):
    def body(c_ref, z_ref, tail_ref, al_ref, dt_ref, nw_ref, st_ref, dy_ref,
             dp_ref, dc_ref, dal_ref, ddt_ref, dnw_ref, dstate):
        @pl.when(pl.program_id(0) == 0)
        def _():
            dstate[...] = jnp.zeros_like(dstate)
            dal_ref[...] = jnp.zeros_like(dal_ref)
            ddt_ref[...] = jnp.zeros_like(ddt_ref)
            dnw_ref[...] = jnp.zeros_like(dnw_ref)

        cv = c_ref[...].astype(F32)
        _, vjp = jax.vjp(_delta_chunks, _heads(cv), _heads(cv, 512), _heads(cv, 1024), _heads(z_ref[...]),
                         tail_ref[...], al_ref[...], dt_ref[...], nw_ref[...], st_ref[0])
        dq, dk, dv, dz, dtail, dal, ddt, dnw, dst = vjp((_heads(dy_ref[...]), dstate[...]))
        dstate[...] = dst
        dc_ref[...] = jnp.concatenate([_unheads(dq), _unheads(dk), _unheads(dv)], axis=1)
        dp_ref[...] = jnp.concatenate([_unheads(dz), dtail, jnp.zeros((DN_STEP_ROWS, 128), F32)], axis=1).astype(BF16)
        dal_ref[...] += dal
        ddt_ref[...] += ddt
        dnw_ref[...] += dnw

    rev = lambda n: DN_STEPS - 1 - n
    row = pl.BlockSpec((1, 128), lambda n: (0, 0))
    rows = DN_STEP_ROWS
    return _hosted_call(
        body, name="delta_bwd", steps=DN_STEPS,
        in_specs=[pl.BlockSpec((rows, 1536), lambda n: (rev(n), 0)),
                  pl.BlockSpec((rows, 512), lambda n: (rev(n), DN_Z_COL // 512)),
                  pl.BlockSpec((rows, 128), lambda n: (rev(n), DN_TAIL_BLK)), row, row, row,
                  pl.BlockSpec((1, NDH, 128, 128), lambda n: (rev(n), 0, 0, 0)),
                  pl.BlockSpec((rows, 512), lambda n: (rev(n), 1))],
        out_specs=[pl.BlockSpec((rows, 768), lambda n: (rev(n), DN_Z_COL // 768)),
                   pl.BlockSpec((rows, 1536), lambda n: (rev(n), 0)), row, row, row],
        out_shape=[jax.ShapeDtypeStruct((SEQ, IN_PAD), BF16), jax.ShapeDtypeStruct((SEQ, 1536), F32)]
        + [jax.ShapeDtypeStruct((1, 128), F32)] * 3,
        scratch_shapes=[pltpu.VMEM((NDH, 128, 128), F32)],
        operands=(c_qkv, proj, proj, alog_row, dt_row, nw, states, dcat), exchanges=exchanges)


def _place():
    x, y, c = lax.axis_index("x"), lax.axis_index("y"), lax.axis_index("c")
    other_chips = [(1 - x, y), (x, 1 - y), (1 - x, 1 - y)]
    return x, y, c, other_chips


def _gather_exchange(shards):
    n = len(shards)

    def copies(ins, outs, sems):
        send_sems, recv_sems, local_sems = sems
        x, y, c, chips = _place()
        me, sibling = (x, y, c), (x, y, 1 - c)

        def copy(b, k, block, to, src=None):
            slot = outs[b].at[4 * block[0] + 2 * block[1] + block[2]]
            return pltpu.make_async_remote_copy(
                src_ref=slot if src is None else src, dst_ref=slot,
                send_sem=send_sems.at[b, k], recv_sem=recv_sems.at[b, k], device_id=to, device_id_type=MESH)

        mine = [pltpu.make_async_copy(ins[b], outs[b].at[4 * x + 2 * y + c], local_sems.at[b]) for b in range(n)]
        first = []
        for b in range(n):
            first.append(copy(b, 0, me, sibling, src=ins[b]))
            first += [copy(b, 1 + j, me, (*chip, c), src=ins[b]) for j, chip in enumerate(chips)]
        over_ici = [copy(b, 1 + j, (*chip, c), me) for b in range(n) for j, chip in enumerate(chips)]
        passed = [copy(b, 4 + j, (*chip, c), sibling) for b in range(n) for j, chip in enumerate(chips)]
        from_sibling = []
        for b in range(n):
            from_sibling.append(copy(b, 0, sibling, me))
            from_sibling += [copy(b, 4 + j, (*chip, 1 - c), me) for j, chip in enumerate(chips)]
        return mine, first, over_ici, passed, from_sibling

    def start(ins, outs, sems):
        mine, first, _, _, _ = copies(ins, outs, sems)
        for cp in mine + first:
            cp.start()

    def middle(ins, outs, sems):
        _, _, over_ici, passed, _ = copies(ins, outs, sems)
        for arrived, onward in zip(over_ici, passed):
            arrived.wait_recv()
            onward.start()

    def finish(ins, outs, sems):
        mine, first, _, passed, from_sibling = copies(ins, outs, sems)
        for cp in from_sibling:
            cp.wait_recv()
        for cp in first + passed:
            cp.wait_send()
        for cp in mine:
            cp.wait()

    return Exchange(shards, [jax.ShapeDtypeStruct((N_DEV,) + s.shape, s.dtype) for s in shards],
                    [pltpu.SemaphoreType.DMA((n, 7)), pltpu.SemaphoreType.DMA((n, 7)), pltpu.SemaphoreType.DMA((n,))],
                    start, middle, finish)


def _sibling_exchange(gs):
    n = len(gs)

    def copies(ins, outs, sems):
        send_sems, recv_sems = sems
        x, y, c, _ = _place()
        return [pltpu.make_async_remote_copy(
            src_ref=ins[b].at[2 * p + (1 - c)], dst_ref=outs[b].at[p],
            send_sem=send_sems.at[b, p], recv_sem=recv_sems.at[b, p],
            device_id=(x, y, 1 - c), device_id_type=MESH) for b in range(n) for p in range(4)]

    def start(ins, outs, sems):
        for cp in copies(ins, outs, sems):
            cp.start()

    def finish(ins, outs, sems):
        for cp in copies(ins, outs, sems):
            cp.wait()

    return Exchange(gs, [jax.ShapeDtypeStruct((4,) + g.shape[1:], g.dtype) for g in gs],
                    [pltpu.SemaphoreType.DMA((n, 4)), pltpu.SemaphoreType.DMA((n, 4))], start, None, finish)


def _chips_exchange(hs):
    n = len(hs)

    def copies(ins, outs, sems):
        send_sems, recv_sems, local_sems = sems
        x, y, c, chips = _place()
        my_chip = 2 * x + y
        local = [pltpu.make_async_copy(ins[b].at[my_chip], outs[b].at[my_chip], local_sems.at[b]) for b in range(n)]
        sends, arrivals = [], []
        for b in range(n):
            for k, (px, py) in enumerate(chips):
                peer = 2 * px + py
                sends.append(pltpu.make_async_remote_copy(
                    src_ref=ins[b].at[peer], dst_ref=outs[b].at[my_chip],
                    send_sem=send_sems.at[b, k], recv_sem=recv_sems.at[b, k],
                    device_id=(px, py, c), device_id_type=MESH))
                arrivals.append(pltpu.make_async_remote_copy(
                    src_ref=ins[b].at[peer], dst_ref=outs[b].at[peer],
                    send_sem=send_sems.at[b, k], recv_sem=recv_sems.at[b, k],
                    device_id=(px, py, c), device_id_type=MESH))
        return local, sends, arrivals

    def start(ins, outs, sems):
        local, sends, _ = copies(ins, outs, sems)
        for cp in local + sends:
            cp.start()

    def finish(ins, outs, sems):
        local, sends, arrivals = copies(ins, outs, sems)
        for cp in arrivals:
            cp.wait_recv()
        for cp in sends:
            cp.wait_send()
        for cp in local:
            cp.wait()

    return Exchange(hs, [jax.ShapeDtypeStruct(h.shape, h.dtype) for h in hs],
                    [pltpu.SemaphoreType.DMA((n, 3)), pltpu.SemaphoreType.DMA((n, 3)), pltpu.SemaphoreType.DMA((n,))],
                    start, None, finish)


def _run_exchange(exchange, name):
    n_in, n_out = len(exchange.operands), len(exchange.out_shapes)

    def body(*refs):
        ins, outs, sems = refs[:n_in], refs[n_in:n_in + n_out], refs[n_in + n_out:]
        exchange.start(ins, outs, sems)
        if exchange.middle is not None:
            exchange.middle(ins, outs, sems)
        exchange.finish(ins, outs, sems)

    return pl.pallas_call(
        body, name=name,
        in_specs=[HBM_SPEC] * n_in, out_specs=[HBM_SPEC] * n_out,
        out_shape=exchange.out_shapes, scratch_shapes=exchange.sems,
    )(*exchange.operands)


def _pair_add(g, r, core, name):
    _, nr, nc = g.shape
    tr = nr // 2 if nr % 32 == 0 else nr

    def body(core_ref, g_ref, r_ref, o_ref):
        o_ref[...] = (g_ref[...].astype(F32) + r_ref[...].astype(F32)).astype(BF16)

    return pl.pallas_call(
        body, name=name,
        grid_spec=pltpu.PrefetchScalarGridSpec(
            num_scalar_prefetch=1, grid=(4, nr // tr),
            in_specs=[pl.BlockSpec((1, tr, nc), lambda p, i, core: (2 * p + core[0], i, 0)),
                      pl.BlockSpec((1, tr, nc), lambda p, i, core: (p, i, 0))],
            out_specs=pl.BlockSpec((1, tr, nc), lambda p, i, core: (p, i, 0))),
        out_shape=jax.ShapeDtypeStruct(r.shape, BF16),
        compiler_params=_cp("parallel", "parallel"),
    )(core, g, r)


def _all_gather_sum_small(v):
    rows = v.shape[0]

    def body(x_ref, sum_ref, out_ref, send_sems, recv_sems, local_sem):
        x, y, c, chips = _place()
        me, sibling = (x, y, c), (x, y, 1 - c)

        def block(px, py, pc):
            return out_ref.at[pl.ds((4 * px + 2 * py + pc) * rows, rows), :]

        def copy(k, blk, to, src=None):
            return pltpu.make_async_remote_copy(
                src_ref=block(*blk) if src is None else src, dst_ref=block(*blk),
                send_sem=send_sems.at[k], recv_sem=recv_sems.at[k], device_id=to, device_id_type=MESH)

        mine = pltpu.make_async_copy(x_ref, block(*me), local_sem)
        mine.start()
        first = [copy(0, me, sibling, src=x_ref)]
        first += [copy(1 + j, me, (*chip, c), src=x_ref) for j, chip in enumerate(chips)]
        for cp in first:
            cp.start()
        passed = [copy(4 + j, (*chip, c), sibling) for j, chip in enumerate(chips)]
        for j, chip in enumerate(chips):
            copy(1 + j, (*chip, c), me).wait_recv()
            passed[j].start()
        copy(0, sibling, me).wait_recv()
        for j, chip in enumerate(chips):
            copy(4 + j, (*chip, 1 - c), me).wait_recv()
        for cp in first + passed:
            cp.wait_send()
        mine.wait()
        total = out_ref[pl.ds(0, rows), :]
        for d in range(1, N_DEV):
            total = total + out_ref[pl.ds(d * rows, rows), :]
        sum_ref[...] = total

    vm = pl.BlockSpec(memory_space=pltpu.VMEM)
    return pl.pallas_call(
        body, name="small_all_reduce",
        in_specs=[vm], out_specs=[vm],
        out_shape=[jax.ShapeDtypeStruct((rows, 128), F32)],
        scratch_shapes=[pltpu.VMEM((N_DEV * rows, 128), F32), pltpu.SemaphoreType.DMA((7,)),
                        pltpu.SemaphoreType.DMA((7,)), pltpu.SemaphoreType.DMA],
    )(v)[0]


def _adamw(w, g, m, v):
    m = ADAM_B1 * m + (1.0 - ADAM_B1) * g
    v = ADAM_B2 * v + (1.0 - ADAM_B2) * (g * g)
    m_hat = m / (1.0 - ADAM_B1 ** ADAM_STEP)
    v_hat = v / (1.0 - ADAM_B2 ** ADAM_STEP)
    delta = -ADAM_LR * (m_hat / (jnp.sqrt(v_hat) + ADAM_EPS) + ADAM_WD * w)
    return delta, m, v


ADAM_TILE = dict(w_in=(IN_COLS // N_DEV, 256), w_out=(128, D_MODEL), ffn_w_in=(176, D_MODEL), ffn_w_out=(176, D_MODEL))


def _sum_chips(p):
    p = p.astype(F32)
    return (p[0] + p[1]) + (p[2] + p[3])


def _adamw_sharded(parts, w, m, v, tile, name):
    nl, nr, nc = w.shape
    tr, tc = tile

    def body(*refs):
        p_refs, (w_ref, m_ref, v_ref, g_ref, d_ref, nm_ref, nv_ref) = refs[:nl], refs[nl:]
        layer = pl.program_id(0)
        p = p_refs[0][...]
        for l in range(1, nl):
            p = jnp.where(layer == l, p_refs[l][...], p)
        g = _sum_chips(p)
        delta, nm, nv = _adamw(w_ref[0], g, m_ref[0], v_ref[0])
        g_ref[0] = g
        d_ref[0] = delta
        nm_ref[0] = nm
        nv_ref[0] = nv

    blk = pl.BlockSpec((1, tr, tc), lambda l, i, j: (l, i, j))
    return pl.pallas_call(
        body, name=name, grid=(nl, nr // tr, nc // tc),
        in_specs=[pl.BlockSpec((4, tr, tc), lambda l, i, j, own=own: (0, jnp.where(l == own, i, 0), j))
                  for own in range(nl)] + [blk, blk, blk],
        out_specs=[blk] * 4,
        out_shape=[jax.ShapeDtypeStruct(w.shape, F32)] * 4,
        compiler_params=_cp("parallel", "parallel", "parallel"),
    )(*parts, w, m, v)


def _adamw_small(g, w, m, v):
    def body(g_ref, w_ref, m_ref, v_ref, d_ref, nm_ref, nv_ref):
        delta, nm, nv = _adamw(w_ref[...], g_ref[...], m_ref[...], v_ref[...])
        d_ref[...] = delta
        nm_ref[...] = nm
        nv_ref[...] = nv

    return pl.pallas_call(
        body, name="adamw_small",
        out_shape=[jax.ShapeDtypeStruct(g.shape, F32)] * 3,
    )(g, w, m, v)


def _packed_rows(n):
    return -(-n // 1024) * 8


def _pack(arrays, rows):
    pieces = []
    for a in arrays:
        flat = a.reshape(-1).astype(F32)
        nr = _packed_rows(flat.shape[0])
        pieces.append(jnp.pad(flat, (0, nr * 128 - flat.shape[0])).reshape(nr, 128))
    used = sum(p.shape[0] for p in pieces)
    return jnp.concatenate(pieces + [jnp.zeros((rows - used, 128), F32)] * (rows > used), axis=0)


def _unpack(packed, shapes):
    out, row = [], 0
    for s in shapes:
        n = math.prod(s)
        out.append(packed[row:row + _packed_rows(n)].reshape(-1)[:n].reshape(s))
        row += _packed_rows(n)
    return out


def _row(v, width=None):
    v = v.reshape(1, -1)
    return v if width is None else jnp.pad(v, ((0, 0), (0, width - v.shape[1])))


def _layer_fwd(x, wts, tables, hosted):
    h = _norm_fwd(x, wts["norm_pre_mix"], "norm_pre_mix")
    proj = _matmul(h, wts["w_in"], tb=True, tm=SEQ, tn=768, tk=1024, name="mm_proj")
    (cat, lse), got = _attn_fwd(proj, *tables, exchanges=hosted["attn"][0])
    hosted["attn"][1](got)
    c_qkv = _dnconv_fwd(proj, wts["dn_conv_w"])
    (cat, states), got = _delta_fwd(c_qkv, proj, wts["dn_a_log"], wts["dn_dt_bias"], wts["dn_norm_w"], cat,
                                    exchanges=hosted["delta"][0])
    hosted["delta"][1](got)
    mix = _matmul(cat, wts["w_out"], tm=512, tn=1024, tk=1024, name="mm_mix")
    x1 = _resnorm_fwd(x, mix, wts["norm_post_mix"], "norm_post_mix")
    h2 = _norm_fwd(x1, wts["norm_pre_ffn"], "norm_pre_ffn")
    pre = _matmul(h2, wts["ffn_w_in"], tb=True, tm=SEQ, tn=512, tk=1024, name="mm_ffn_in", out_dtype=BF16)
    act, got = _ffact_fwd(pre, wts["ffn_conv_w"], wts["ffn_conv_b"], exchanges=hosted["ffact"][0])
    hosted["ffact"][1](got)
    f = _matmul(act, wts["ffn_w_out"], tm=512, tn=1024, tk=D_FF, name="mm_ffn_out")
    x2 = _resnorm_fwd(x1, f, wts["norm_post_ffn"], "norm_post_ffn")
    saved = dict(x=x, h=h, proj=proj, lse=lse, c_qkv=c_qkv, states=states, cat=cat, mix=mix, x1=x1, h2=h2, pre=pre,
                 act=act, f=f)
    return x2, saved


def _layer_bwd(dx2, wts, s, tables, ffact_exchanges=(), delta_exchanges=None, attn_exchanges=None):
    g = {}
    df, g["norm_post_ffn"] = _norm_bwd(s["f"], wts["norm_post_ffn"], dx2, None, "norm_post_ffn_bwd", BF16)
    dact = _matmul(df, wts["ffn_w_out"], tb=True, tm=SEQ, tn=1408, tk=1024, name="mm_dact", out_dtype=BF16)
    g["ffn_w_out"] = _matmul(s["act"], df, ta=True, tm=1408, tn=512, tk=SEQ, name="mm_dw_ffn_out", out_dtype=BF16)
    (dpre, g["ffn_conv_w"], g["ffn_conv_b"]), got = _ffact_bwd(s["pre"], wts["ffn_conv_w"], wts["ffn_conv_b"], dact,
                                                               exchanges=ffact_exchanges)
    dh2 = _matmul(dpre, wts["ffn_w_in"], tm=1024, tn=1024, tk=D_FF, name="mm_dh2")
    g["ffn_w_in"] = _matmul(dpre, s["h2"], ta=True, tm=512, tn=1024, tk=SEQ, name="mm_dw_ffn_in", out_dtype=BF16)
    dx1, g["norm_pre_ffn"] = _norm_bwd(s["x1"], wts["norm_pre_ffn"], dh2, dx2, "norm_pre_ffn_bwd")
    dmix, g["norm_post_mix"] = _norm_bwd(s["mix"], wts["norm_post_mix"], dx1, None, "norm_post_mix_bwd", BF16)
    dcat = _matmul(dmix, wts["w_out"], tb=True, tm=SEQ, tn=512, tk=1024, name="mm_dcat")
    g["w_out"] = _matmul(s["cat"], dmix, ta=True, tm=1024, tn=512, tk=SEQ, name="mm_dw_out", out_dtype=BF16)
    (dproj, dc, g["dn_a_log"], g["dn_dt_bias"], g["dn_norm_w"]), got = _delta_bwd(
        s["c_qkv"], s["proj"], wts["dn_a_log"], wts["dn_dt_bias"], wts["dn_norm_w"], s["states"], dcat,
        exchanges=delta_exchanges(g, got) if delta_exchanges is not None else ())
    dproj, got = _attn_bwd(s["proj"], *tables, s["cat"], s["lse"], dcat, dproj,
                           exchanges=attn_exchanges(got) if attn_exchanges is not None else ())
    dproj, g["dn_conv_w"] = _dnconv_bwd(s["proj"], wts["dn_conv_w"], dc, dproj)
    dh = _matmul(dproj, wts["w_in"], tm=1024, tn=1024, tk=IN_PAD // 2, name="mm_dh")
    g["w_in"] = _matmul(dproj, s["h"], ta=True, tm=768, tn=1024, tk=SEQ, name="mm_dw_in", out_dtype=BF16)
    dx, g["norm_pre_mix"] = _norm_bwd(s["x"], wts["norm_pre_mix"], dh, dx1, "norm_pre_mix_bwd")
    return dx, g, got


BIG = ("w_in", "w_out", "ffn_w_in", "ffn_w_out")
COLUMN_SHARDED = ("w_in", "ffn_w_in")
SMALL_SHARDED = ("dn_conv_w", "ffn_conv_w")
REPLICATED = ("dn_a_log", "dn_dt_bias", "dn_norm_w", "ffn_conv_b", "norm_pre_mix", "norm_post_mix", "norm_pre_ffn",
              "norm_post_ffn")
WEIGHTS = ("w_in", "dn_conv_w", "dn_a_log", "dn_dt_bias", "dn_norm_w", "w_out", "ffn_w_in", "ffn_conv_w", "ffn_conv_b",
           "ffn_w_out", "norm_pre_mix", "norm_post_mix", "norm_pre_ffn", "norm_post_ffn")
FULL_SHAPE = dict(dn_conv_w=(DEPTH, 4, 1536), ffn_conv_w=(DEPTH, 3, 2 * D_FF), dn_a_log=(DEPTH, NDH),
                  dn_dt_bias=(DEPTH, NDH), dn_norm_w=(DEPTH, 128), ffn_conv_b=(DEPTH, 2 * D_FF),
                  norm_pre_mix=(DEPTH, D_MODEL), norm_post_mix=(DEPTH, D_MODEL), norm_pre_ffn=(DEPTH, D_MODEL),
                  norm_post_ffn=(DEPTH, D_MODEL))
SMALL_GRAD_ORDER = REPLICATED + SMALL_SHARDED
SMALL_GRAD_ROWS = 544
SMALL_W_ROWS = 56
SMALL_ADAM_ROWS = 232


def _w_in_rows_to_kernel_order(t):
    qkv = t[:QKV_W].reshape(3, N_PAIR, 128, -1).swapaxes(0, 1).reshape(QKV_W, -1)
    return jnp.pad(jnp.concatenate([qkv, t[QKV_W:]], axis=0), ((0, IN_PAD - IN_COLS), (0, 0)))


def _w_in_rows_from_kernel_order(t):
    qkv = t[:QKV_W].reshape(N_PAIR, 3, 128, -1).swapaxes(0, 1).reshape(QKV_W, -1)
    return jnp.concatenate([qkv, t[QKV_W:IN_COLS]], axis=0)


def _interleave_ff_rows(t):
    return t.reshape(2, FF_BLKS, 128, -1).swapaxes(0, 1).reshape(2 * D_FF, -1)


def _deinterleave_ff_rows(t):
    return t.reshape(FF_BLKS, 2, 128, -1).swapaxes(0, 1).reshape(2 * D_FF, -1)


def kernel(x, w_in, dn_conv_w, dn_a_log, dn_dt_bias, dn_norm_w, w_out, ffn_w_in, ffn_conv_w, ffn_conv_b, ffn_w_out, norm_pre_mix, norm_post_mix, norm_pre_ffn, norm_post_ffn, loss_target, m_w_in, m_dn_conv_w, m_dn_a_log, m_dn_dt_bias, m_dn_norm_w, m_w_out, m_ffn_w_in, m_ffn_conv_w, m_ffn_conv_b, m_ffn_w_out, m_norm_pre_mix, m_norm_post_mix, m_norm_pre_ffn, m_norm_post_ffn, v_w_in, v_dn_conv_w, v_dn_a_log, v_dn_dt_bias, v_dn_norm_w, v_w_out, v_ffn_w_in, v_ffn_conv_w, v_ffn_conv_b, v_ffn_w_out, v_norm_pre_mix, v_norm_post_mix, v_norm_pre_ffn, v_norm_post_ffn):
    local = dict(w_in=w_in, dn_conv_w=dn_conv_w, dn_a_log=dn_a_log, dn_dt_bias=dn_dt_bias, dn_norm_w=dn_norm_w,
                 w_out=w_out, ffn_w_in=ffn_w_in, ffn_conv_w=ffn_conv_w, ffn_conv_b=ffn_conv_b, ffn_w_out=ffn_w_out,
                 norm_pre_mix=norm_pre_mix, norm_post_mix=norm_post_mix, norm_pre_ffn=norm_pre_ffn,
                 norm_post_ffn=norm_post_ffn)
    mom_m = dict(w_in=m_w_in, dn_conv_w=m_dn_conv_w, dn_a_log=m_dn_a_log, dn_dt_bias=m_dn_dt_bias,
                 dn_norm_w=m_dn_norm_w, w_out=m_w_out, ffn_w_in=m_ffn_w_in, ffn_conv_w=m_ffn_conv_w,
                 ffn_conv_b=m_ffn_conv_b, ffn_w_out=m_ffn_w_out, norm_pre_mix=m_norm_pre_mix,
                 norm_post_mix=m_norm_post_mix, norm_pre_ffn=m_norm_pre_ffn, norm_post_ffn=m_norm_post_ffn)
    mom_v = dict(w_in=v_w_in, dn_conv_w=v_dn_conv_w, dn_a_log=v_dn_a_log, dn_dt_bias=v_dn_dt_bias,
                 dn_norm_w=v_dn_norm_w, w_out=v_w_out, ffn_w_in=v_ffn_w_in, ffn_conv_w=v_ffn_conv_w,
                 ffn_conv_b=v_ffn_conv_b, ffn_w_out=v_ffn_w_out, norm_pre_mix=v_norm_pre_mix,
                 norm_post_mix=v_norm_post_mix, norm_pre_ffn=v_norm_pre_ffn, norm_post_ffn=v_norm_post_ffn)
    dev = 4 * lax.axis_index("x") + 2 * lax.axis_index("y") + lax.axis_index("c")
    core = lax.axis_index("c").astype(jnp.int32).reshape(1)

    def shard(n, l):
        s = local[n].transpose(0, 2, 1) if n in COLUMN_SHARDED else local[n]
        return s[l].astype(BF16)

    def matrix(n, gathered):
        if n == "w_in":
            return _w_in_rows_to_kernel_order(gathered.reshape(IN_COLS, D_MODEL))
        if n == "ffn_w_in":
            return _interleave_ff_rows(gathered.reshape(2 * D_FF, D_MODEL))
        return gathered.reshape(-1, D_MODEL)

    small_w = _pack([dn_conv_w, ffn_conv_w], SMALL_W_ROWS)
    g_w_in0, g_small = _run_exchange(_gather_exchange([shard("w_in", 0), small_w]), "weights_all_gather")
    n_dn, n_ff = DEPTH * 4 * 192, DEPTH * 3 * 704
    dn_rows = _packed_rows(n_dn)
    sm_dn = g_small[:, :dn_rows].reshape(N_DEV, -1)[:, :n_dn]
    sm_ff = g_small[:, dn_rows:].reshape(N_DEV, -1)[:, :n_ff]
    full_dn_conv = sm_dn.reshape(N_DEV, DEPTH, 4, 192).transpose(1, 2, 0, 3).reshape(DEPTH, 4, 1536)
    full_ff_conv = _interleave_ff(sm_ff.reshape(N_DEV, DEPTH, 3, 704).transpose(1, 2, 0, 3).reshape(DEPTH, 3, 2 * D_FF))

    def small_weights(l):
        wts = dict(dn_conv_w=full_dn_conv[l], ffn_conv_w=full_ff_conv[l], ffn_conv_b=_interleave_ff(_row(ffn_conv_b[l])),
                   dn_a_log=_row(dn_a_log[l], 128), dn_dt_bias=_row(dn_dt_bias[l], 128))
        for n in ("dn_norm_w", "norm_pre_mix", "norm_post_mix", "norm_pre_ffn", "norm_post_ffn"):
            wts[n] = _row(local[n][l])
        return wts

    weights = [small_weights(l) for l in range(DEPTH)]
    weights[0]["w_in"] = matrix("w_in", g_w_in0)

    def gather_behind(wanted):
        def deliver(got):
            for (n, l), g in zip(wanted, got[0]):
                weights[l][n] = matrix(n, g)

        return [_gather_exchange([shard(n, l) for n, l in wanted])], deliver

    nothing = ((), lambda got: None)

    tables = _rope_tables()
    act, saved0 = _layer_fwd(x[0], weights[0], tables, dict(
        attn=gather_behind([("ffn_w_in", 0)]), delta=gather_behind([("w_out", 0), ("ffn_w_out", 0)]),
        ffact=gather_behind([("w_in", 1)])))
    act, saved1 = _layer_fwd(act, weights[1], tables, dict(
        attn=gather_behind([("ffn_w_in", 1)]), delta=gather_behind([("w_out", 1), ("ffn_w_out", 1)]), ffact=nothing))
    loss_part, dact = _loss_fwd_bwd(act, loss_target[0])

    def to_devices(name, t):
        if name == "w_in":
            t = _w_in_rows_from_kernel_order(t)
        if name == "ffn_w_in":
            t = _deinterleave_ff_rows(t)
        return t.reshape(N_DEV, t.shape[0] // N_DEV, t.shape[1])

    def pair_sums(names, layer, to_dev, from_sibling):
        return [_pair_add(gd, r, core, "grads_pair_add_%s_%d" % (n, layer))
                for n, gd, r in zip(names, to_dev, from_sibling)]

    early = ("w_out", "ffn_w_in", "ffn_w_out")
    grads, parts, stash = [None] * DEPTH, {}, {}

    def delta_exchanges1(g, got_ffact):
        stash["early1"] = [to_devices(n, g[n]) for n in early]
        return [_sibling_exchange(stash["early1"])]

    def attn_exchanges1(got_delta):
        return [_chips_exchange(pair_sums(early, 1, stash["early1"], got_delta[0]))]

    dact, grads[1], got_attn = _layer_bwd(dact, weights[1], saved1, tables, (), delta_exchanges1, attn_exchanges1)
    for n, p in zip(early, got_attn[0]):
        parts[n, 1] = p
    w_in1 = [to_devices("w_in", grads[1]["w_in"])]

    def delta_exchanges0(g, got_ffact):
        stash["early0"] = [to_devices(n, g[n]) for n in early]
        return [_chips_exchange(pair_sums(("w_in",), 1, w_in1, got_ffact[0])), _sibling_exchange(stash["early0"])]

    def attn_exchanges0(got_delta):
        parts["w_in", 1], = got_delta[0]
        return [_chips_exchange(pair_sums(early, 0, stash["early0"], got_delta[1]))]

    dact, grads[0], got_attn = _layer_bwd(dact, weights[0], saved0, tables, [_sibling_exchange(w_in1)],
                                          delta_exchanges0, attn_exchanges0)
    for n, p in zip(early, got_attn[0]):
        parts[n, 0] = p
    grad_x = dact[None]
    last = [to_devices("w_in", grads[0]["w_in"])]
    from_sibling = _run_exchange(_sibling_exchange(last), "grads_to_sibling")
    parts["w_in", 0], = _run_exchange(_chips_exchange(pair_sums(("w_in",), 0, last, from_sibling)), "grads_to_chips")

    def small_grad(name):
        t = jnp.stack([grads[l][name] for l in range(DEPTH)])
        if name in ("dn_a_log", "dn_dt_bias"):
            t = t[:, 0, :NDH]
        if name in ("ffn_conv_w", "ffn_conv_b"):
            t = _deinterleave_ff(t)
        return t.reshape(FULL_SHAPE[name])

    small_part = _pack([small_grad(n) for n in SMALL_GRAD_ORDER] + [loss_part[0, :1]], SMALL_GRAD_ROWS)
    small_sum = _all_gather_sum_small(small_part)
    small_g = dict(zip(SMALL_GRAD_ORDER + ("loss",), _unpack(small_sum, [FULL_SHAPE[n] for n in SMALL_GRAD_ORDER] + [(1,)])))
    loss = small_g["loss"][0]
    small_g["dn_conv_w"] = lax.dynamic_slice_in_dim(small_g["dn_conv_w"], dev * 192, 192, axis=2)
    small_g["ffn_conv_w"] = lax.dynamic_slice_in_dim(small_g["ffn_conv_w"], dev * 704, 704, axis=2)

    out_g, out_d, out_m, out_v = {}, {}, {}, {}
    for n in BIG:
        turn = (lambda t: t.transpose(0, 2, 1)) if n in COLUMN_SHARDED else (lambda t: t)
        outs = _adamw_sharded([parts[n, l] for l in range(DEPTH)], turn(local[n]), turn(mom_m[n]), turn(mom_v[n]),
                              ADAM_TILE[n], "adamw_" + n)
        out_g[n], out_d[n], out_m[n], out_v[n] = [turn(t) for t in outs]
    shapes = [small_g[n].shape for n in SMALL_GRAD_ORDER]
    d_s, m_s, v_s = _adamw_small(_pack([small_g[n] for n in SMALL_GRAD_ORDER], SMALL_ADAM_ROWS),
                                 _pack([local[n] for n in SMALL_GRAD_ORDER], SMALL_ADAM_ROWS),
                                 _pack([mom_m[n] for n in SMALL_GRAD_ORDER], SMALL_ADAM_ROWS),
                                 _pack([mom_v[n] for n in SMALL_GRAD_ORDER], SMALL_ADAM_ROWS))
    for n, d, m, v in zip(SMALL_GRAD_ORDER, _unpack(d_s, shapes), _unpack(m_s, shapes), _unpack(v_s, shapes)):
        out_g[n], out_d[n], out_m[n], out_v[n] = small_g[n], d, m, v
    return (loss, grad_x, *[out_g[n] for n in WEIGHTS], *[out_d[n] for n in WEIGHTS],
            *[out_m[n] for n in WEIGHTS], *[out_v[n] for n in WEIGHTS])
```

```python
import functools
import math

import jax
import jax.numpy as jnp
from jax import lax
from jax.experimental import pallas as pl
from jax.experimental.pallas import tpu as pltpu

F32 = jnp.float32
BF16 = jnp.bfloat16
MESH = pl.DeviceIdType.MESH

N_DEV = 8
SEQ = 2048
D_MODEL = 1024
DEPTH = 2
N_PAIR = 4
HEAD_DIM = 64
ATTN_W = 512
ATTN_BLK = 128
DILATIONS = (1, 4, 16)
SEGMENT_BLOCKS = (16, 4, 1)
N_BLK = SEQ // ATTN_BLK
NDH = 4
CH = 64
NCH = SEQ // CH
IN_COLS = 3592
IN_PAD = 3840
QKV_W = 3 * ATTN_W
DN_QKV_BLK0 = QKV_W // 128
DN_QKV_BLKS = 1536 // 128
DN_Z_COL = 3072
DN_TAIL_BLK = 3584 // 128
D_FF = 2816
FF_BLKS = D_FF // 128
EPS = 1e-6
NEG = -1e30
ROPE_THETA = 10000.0

ADAM_LR, ADAM_B1, ADAM_B2, ADAM_EPS, ADAM_WD, ADAM_STEP = 0.001, 0.9, 0.999, 1e-08, 0.01, 10

VMEM_LIMIT = 56 * 1024 * 1024


def _cp(*sem):
    return pltpu.CompilerParams(dimension_semantics=sem, vmem_limit_bytes=VMEM_LIMIT)


class Exchange:
    def __init__(self, operands, out_shapes, sems, start, middle, finish):
        self.operands, self.out_shapes, self.sems = list(operands), list(out_shapes), list(sems)
        self.start, self.middle, self.finish = start, middle, finish


HBM_SPEC = pl.BlockSpec(memory_space=pltpu.HBM)


def _hosted_call(body, *, name, steps, in_specs, out_specs, out_shape, scratch_shapes, operands, exchanges=(),
                 aliases=None):
    n_in, n_out, n_scr = len(in_specs), len(out_specs), len(scratch_shapes)

    def take(refs, pos, counts):
        groups = []
        for c in counts:
            groups.append(refs[pos:pos + c])
            pos += c
        return groups, pos

    def full_body(*refs):
        ins, pos = refs[:n_in], n_in
        ex_ins, pos = take(refs, pos, [len(e.operands) for e in exchanges])
        outs, pos = refs[pos:pos + n_out], pos + n_out
        ex_outs, pos = take(refs, pos, [len(e.out_shapes) for e in exchanges])
        scr, pos = refs[pos:pos + n_scr], pos + n_scr
        ex_sems, pos = take(refs, pos, [len(e.sems) for e in exchanges])
        step = pl.program_id(0)
        for e, a, b, s in zip(exchanges, ex_ins, ex_outs, ex_sems):
            pl.when(step == 0)(functools.partial(e.start, a, b, s))
            if e.middle is not None:
                pl.when(step == (3 * steps) // 4)(functools.partial(e.middle, a, b, s))
        body(*ins, *outs, *scr)
        for e, a, b, s in zip(exchanges, ex_ins, ex_outs, ex_sems):
            pl.when(step == steps - 1)(functools.partial(e.finish, a, b, s))

    n_ex_in = sum(len(e.operands) for e in exchanges)
    n_ex_out = sum(len(e.out_shapes) for e in exchanges)
    results = pl.pallas_call(
        full_body, name=name, grid=(steps,),
        in_specs=list(in_specs) + [HBM_SPEC] * n_ex_in,
        out_specs=list(out_specs) + [HBM_SPEC] * n_ex_out,
        out_shape=list(out_shape) + [s for e in exchanges for s in e.out_shapes],
        scratch_shapes=list(scratch_shapes) + [s for e in exchanges for s in e.sems],
        input_output_aliases=aliases or {},
        compiler_params=_cp("arbitrary"),
    )(*operands, *[a for e in exchanges for a in e.operands])
    ex_results, _ = take(results, n_out, [len(e.out_shapes) for e in exchanges])
    return results[:n_out], ex_results


def _dot(a, b, dims, precision=None):
    if precision is None:
        a = a.astype(BF16)
        b = b.astype(BF16)
    return lax.dot_general(a, b, (dims, ((), ())), preferred_element_type=F32, precision=precision)


def _make_mm(precision):
    @jax.custom_vjp
    def nn(a, b):
        return _dot(a, b, ((1,), (0,)), precision)

    @jax.custom_vjp
    def nt(a, b):
        return _dot(a, b, ((1,), (1,)), precision)

    @jax.custom_vjp
    def tn(a, b):
        return _dot(a, b, ((0,), (0,)), precision)

    nn.defvjp(lambda a, b: (nn(a, b), (a, b)), lambda r, g: (nt(g, r[1]), tn(r[0], g)))
    nt.defvjp(lambda a, b: (nt(a, b), (a, b)), lambda r, g: (nn(g, r[1]), tn(g, r[0])))
    tn.defvjp(lambda a, b: (tn(a, b), (a, b)), lambda r, g: (nt(r[1], g), nn(r[0], g)))
    return nn, nt, tn


def _matmul(a, b, *, ta=False, tb=False, tm, tn, tk, name, out_dtype=F32):
    (k_dim, m_dim) = a.shape if ta else a.shape[::-1]
    (n_dim, k2) = b.shape if tb else b.shape[::-1]
    assert k_dim == k2 and m_dim % tm == 0 and n_dim % tn == 0 and k_dim % tk == 0, (a.shape, b.shape, tm, tn, tk)
    nk = k_dim // tk
    dims = ((0 if ta else 1,), (1 if tb else 0,))

    def body(a_ref, b_ref, o_ref, *acc):
        p = _dot(a_ref[...], b_ref[...], dims)
        if nk == 1:
            o_ref[...] = p.astype(out_dtype)
            return
        acc_ref, k = acc[0], pl.program_id(2)

        @pl.when(k == 0)
        def _():
            acc_ref[...] = p

        @pl.when(k > 0)
        def _():
            acc_ref[...] += p

        @pl.when(k == nk - 1)
        def _():
            o_ref[...] = acc_ref[...].astype(out_dtype)

    a_spec = pl.BlockSpec((tk, tm), lambda i, j, k: (k, i)) if ta else pl.BlockSpec((tm, tk), lambda i, j, k: (i, k))
    b_spec = pl.BlockSpec((tn, tk), lambda i, j, k: (j, k)) if tb else pl.BlockSpec((tk, tn), lambda i, j, k: (k, j))
    return pl.pallas_call(
        body, name=name,
        grid=(m_dim // tm, n_dim // tn, nk),
        in_specs=[a_spec, b_spec],
        out_specs=pl.BlockSpec((tm, tn), lambda i, j, k: (i, j)),
        out_shape=jax.ShapeDtypeStruct((m_dim, n_dim), out_dtype),
        scratch_shapes=[pltpu.VMEM((tm, tn), F32)] if nk > 1 else [],
        compiler_params=_cp("parallel", "parallel", "arbitrary"),
    )(a, b)


NORM_ROWS = 256


def _rms(x, w):
    return x * lax.rsqrt(jnp.mean(x * x, axis=1, keepdims=True) + EPS) * w


def _norm_fwd(x, w_row, name, out_dtype=BF16):
    def body(x_ref, w_ref, o_ref):
        o_ref[...] = _rms(x_ref[...], w_ref[...]).astype(out_dtype)

    return pl.pallas_call(
        body, name=name, grid=(SEQ // NORM_ROWS,),
        in_specs=[pl.BlockSpec((NORM_ROWS, D_MODEL), lambda i: (i, 0)), pl.BlockSpec((1, D_MODEL), lambda i: (0, 0))],
        out_specs=pl.BlockSpec((NORM_ROWS, D_MODEL), lambda i: (i, 0)),
        out_shape=jax.ShapeDtypeStruct((SEQ, D_MODEL), out_dtype),
        compiler_params=_cp("parallel"),
    )(x, w_row)


def _resnorm_fwd(x, f, w_row, name):
    def body(x_ref, f_ref, w_ref, o_ref):
        o_ref[...] = x_ref[...] + _rms(f_ref[...], w_ref[...])

    blk = pl.BlockSpec((NORM_ROWS, D_MODEL), lambda i: (i, 0))
    return pl.pallas_call(
        body, name=name, grid=(SEQ // NORM_ROWS,),
        in_specs=[blk, blk, pl.BlockSpec((1, D_MODEL), lambda i: (0, 0))],
        out_specs=blk, out_shape=jax.ShapeDtypeStruct((SEQ, D_MODEL), F32),
        compiler_params=_cp("parallel"),
    )(x, f, w_row)


def _norm_bwd(x, w_row, dy, add, name, dx_dtype=F32, exchanges=()):
    has_add = add is not None

    def body(*refs):
        if has_add:
            x_ref, w_ref, dy_ref, add_ref, dx_ref, dw_ref = refs
        else:
            x_ref, w_ref, dy_ref, dx_ref, dw_ref = refs
        _, vjp = jax.vjp(_rms, x_ref[...], w_ref[...])
        dx, dw = vjp(dy_ref[...])
        dx_ref[...] = (dx + add_ref[...] if has_add else dx).astype(dx_dtype)

        @pl.when(pl.program_id(0) == 0)
        def _():
            dw_ref[...] = jnp.zeros_like(dw_ref)

        dw_ref[...] += dw

    blk = pl.BlockSpec((NORM_ROWS, D_MODEL), lambda i: (i, 0))
    row = pl.BlockSpec((1, D_MODEL), lambda i: (0, 0))
    ins = [x, w_row, dy] + ([add] if has_add else [])
    (dx, dw), results = _hosted_call(
        body, name=name, steps=SEQ // NORM_ROWS,
        in_specs=[blk, row, blk] + ([blk] if has_add else []),
        out_specs=[blk, row],
        out_shape=[jax.ShapeDtypeStruct((SEQ, D_MODEL), dx_dtype), jax.ShapeDtypeStruct((1, D_MODEL), F32)],
        scratch_shapes=[], operands=ins, exchanges=exchanges)
    return dx, dw, results


def _loss_fwd_bwd(y, target):
    def body(y_ref, t_ref, loss_ref, dy_ref):
        err = y_ref[...] - t_ref[...]
        dy_ref[...] = err * (1.0 / D_MODEL)

        @pl.when(pl.program_id(0) == 0)
        def _():
            loss_ref[...] = jnp.zeros_like(loss_ref)

        part = jnp.sum(jnp.sum(err * err, axis=1, keepdims=True) * (1.0 / D_MODEL), axis=0, keepdims=True)
        loss_ref[...] += 0.5 * jnp.broadcast_to(part, loss_ref.shape)

    blk = pl.BlockSpec((NORM_ROWS, D_MODEL), lambda i: (i, 0))
    return pl.pallas_call(
        body, name="loss", grid=(SEQ // NORM_ROWS,),
        in_specs=[blk, blk],
        out_specs=[pl.BlockSpec((1, 128), lambda i: (0, 0)), blk],
        out_shape=[jax.ShapeDtypeStruct((1, 128), F32), jax.ShapeDtypeStruct((SEQ, D_MODEL), F32)],
        compiler_params=_cp("arbitrary"),
    )(y, target)


def _make_shift(j):
    def down(x):
        row = lax.broadcasted_iota(jnp.int32, x.shape, 0)
        return jnp.where(row >= j, pltpu.roll(x, j, 0), 0.0)

    def up(x):
        n = x.shape[0]
        row = lax.broadcasted_iota(jnp.int32, x.shape, 0)
        return jnp.where(row < n - j, pltpu.roll(x, n - j, 0), 0.0)

    f = jax.custom_vjp(down)
    f.defvjp(lambda x: (down(x), None), lambda _, g: (up(g),))
    return f


_SHIFT = {j: _make_shift(j) for j in (1, 2, 3)}


def _causal_conv(x, taps):
    n = len(taps)
    acc = x * taps[n - 1]
    for k in range(n - 1):
        acc = acc + _SHIFT[n - 1 - k](x) * taps[k]
    return acc


def _tap_rows(w_ref, lanes=slice(None)):
    return tuple(w_ref[k:k + 1, lanes] for k in range(w_ref.shape[0]))


def _sigmoid(x):
    return 1.0 / (1.0 + jnp.exp(-x))


def _silu(x):
    return x * _sigmoid(x)


def _softplus(x):
    return jnp.maximum(x, 0.0) + jnp.log(1.0 + jnp.exp(-jnp.abs(x)))


def _gelu_tanh(x):
    return 0.5 * x * (1.0 + jnp.tanh(math.sqrt(2.0 / math.pi) * (x + 0.044715 * (x * x * x))))


def _dnconv_fn(x, taps):
    return _silu(_causal_conv(x, taps))


def _dnconv_fwd(proj, conv_w):
    def body(x_ref, w_ref, o_ref):
        o_ref[...] = _dnconv_fn(x_ref[...], _tap_rows(w_ref)).astype(BF16)

    return pl.pallas_call(
        body, name="dnconv_fwd", grid=(DN_QKV_BLKS,),
        in_specs=[pl.BlockSpec((SEQ, 128), lambda j: (0, DN_QKV_BLK0 + j)), pl.BlockSpec((4, 128), lambda j: (0, j))],
        out_specs=pl.BlockSpec((SEQ, 128), lambda j: (0, j)),
        out_shape=jax.ShapeDtypeStruct((SEQ, 1536), BF16),
        compiler_params=_cp("parallel"),
    )(proj, conv_w)


def _dnconv_bwd(proj, conv_w, dc, dproj):
    def body(x_ref, w_ref, dc_ref, _, dx_ref, dw_ref):
        _, vjp = jax.vjp(_dnconv_fn, x_ref[...], _tap_rows(w_ref))
        dx, dw = vjp(dc_ref[...])
        dx_ref[...] = dx.astype(BF16)
        for k, row in enumerate(dw):
            dw_ref[k:k + 1, :] = row

    return pl.pallas_call(
        body, name="dnconv_bwd", grid=(DN_QKV_BLKS,),
        in_specs=[pl.BlockSpec((SEQ, 128), lambda j: (0, DN_QKV_BLK0 + j)), pl.BlockSpec((4, 128), lambda j: (0, j)),
                  pl.BlockSpec((SEQ, 128), lambda j: (0, j)), pl.BlockSpec(memory_space=pl.ANY)],
        out_specs=[pl.BlockSpec((SEQ, 128), lambda j: (0, DN_QKV_BLK0 + j)), pl.BlockSpec((4, 128), lambda j: (0, j))],
        out_shape=[jax.ShapeDtypeStruct((SEQ, IN_PAD), BF16), jax.ShapeDtypeStruct((4, 1536), F32)],
        input_output_aliases={3: 0},
        compiler_params=_cp("parallel"),
    )(proj, conv_w, dc, dproj)


def _ffact_fn(pg, pu, wg, wu, bg, bu):
    return _gelu_tanh(_causal_conv(pg, wg) + bg) * (_causal_conv(pu, wu) + bu)


def _ffact_args(p_ref, w_ref, b_ref):
    g, u = slice(0, 128), slice(128, 256)
    return (p_ref[:, g].astype(F32), p_ref[:, u].astype(F32), _tap_rows(w_ref, g), _tap_rows(w_ref, u),
            b_ref[:, g], b_ref[:, u])


def _ffact_fwd(pre, conv_w, conv_b, exchanges=()):
    def body(p_ref, w_ref, b_ref, o_ref):
        o_ref[...] = _ffact_fn(*_ffact_args(p_ref, w_ref, b_ref)).astype(BF16)

    (act,), results = _hosted_call(
        body, name="ffact_fwd", steps=FF_BLKS,
        in_specs=[pl.BlockSpec((SEQ, 256), lambda j: (0, j)), pl.BlockSpec((3, 256), lambda j: (0, j)),
                  pl.BlockSpec((1, 256), lambda j: (0, j))],
        out_specs=[pl.BlockSpec((SEQ, 128), lambda j: (0, j))],
        out_shape=[jax.ShapeDtypeStruct((SEQ, D_FF), BF16)],
        scratch_shapes=[], operands=(pre, conv_w, conv_b), exchanges=exchanges)
    return act, results


def _ffact_bwd(pre, conv_w, conv_b, dact, exchanges=()):
    def body(p_ref, w_ref, b_ref, da_ref, dp_ref, dw_ref, db_ref):
        _, vjp = jax.vjp(_ffact_fn, *_ffact_args(p_ref, w_ref, b_ref))
        dpg, dpu, dwg, dwu, dbg, dbu = vjp(da_ref[...].astype(F32))
        dp_ref[:, 0:128] = dpg.astype(BF16)
        dp_ref[:, 128:256] = dpu.astype(BF16)
        for k in range(3):
            dw_ref[k:k + 1, 0:128] = dwg[k]
            dw_ref[k:k + 1, 128:256] = dwu[k]
        db_ref[:, 0:128] = dbg
        db_ref[:, 128:256] = dbu

    return _hosted_call(
        body, name="ffact_bwd", steps=FF_BLKS,
        in_specs=[pl.BlockSpec((SEQ, 256), lambda j: (0, j)), pl.BlockSpec((3, 256), lambda j: (0, j)),
                  pl.BlockSpec((1, 256), lambda j: (0, j)), pl.BlockSpec((SEQ, 128), lambda j: (0, j))],
        out_specs=[pl.BlockSpec((SEQ, 256), lambda j: (0, j)), pl.BlockSpec((3, 256), lambda j: (0, j)),
                   pl.BlockSpec((1, 256), lambda j: (0, j))],
        out_shape=[jax.ShapeDtypeStruct((SEQ, 2 * D_FF), BF16), jax.ShapeDtypeStruct((3, 2 * D_FF), F32),
                   jax.ShapeDtypeStruct((1, 2 * D_FF), F32)],
        scratch_shapes=[], operands=(pre, conv_w, conv_b, dact), exchanges=exchanges)


def _interleave_ff(t):
    lead = t.shape[:-1]
    return t.reshape(lead + (2, FF_BLKS, 128)).swapaxes(-3, -2).reshape(lead + (2 * D_FF,))


def _deinterleave_ff(t):
    lead = t.shape[:-1]
    return t.reshape(lead + (FF_BLKS, 2, 128)).swapaxes(-3, -2).reshape(lead + (2 * D_FF,))


def _rope_tables():
    inv = 1.0 / (ROPE_THETA ** (jnp.arange(0, HEAD_DIM, 2, dtype=F32) / HEAD_DIM))
    ang = jnp.arange(SEQ, dtype=F32)[:, None] * inv[None, :]
    cos = jnp.tile(jnp.cos(ang), (1, 4))
    sin = jnp.tile(jnp.sin(ang), (1, 4))
    sign = jnp.where((jnp.arange(128) % HEAD_DIM) < HEAD_DIM // 2, -1.0, 1.0).astype(F32)
    return cos, sin * sign[None, :]


def _rope(x, cos, sin_signed):
    lane = lax.broadcasted_iota(jnp.int32, x.shape, 1)
    partner = jnp.where((lane % HEAD_DIM) < HEAD_DIM // 2, pltpu.roll(x, 128 - HEAD_DIM // 2, 1),
                        pltpu.roll(x, HEAD_DIM // 2, 1))
    return x * cos + partner * sin_signed


def _head_masks():
    lane = lax.broadcasted_iota(jnp.int32, (1, 128), 1)
    return [(lane // HEAD_DIM) == h for h in range(2)]


def _both_heads(x):
    return jnp.concatenate([jnp.where(hm, x, 0.0)[None] for hm in _head_masks()], axis=0)


def _block_keys(branch, k_s, v_s, rows, prows, has_prev):
    a = lax.broadcasted_iota(jnp.int32, (ATTN_BLK, ATTN_BLK), 0)
    c = lax.broadcasted_iota(jnp.int32, (ATTN_BLK, ATTN_BLK), 1)
    keys, values, mask = k_s[rows, :], v_s[rows, :], c <= a
    if SEGMENT_BLOCKS[branch] > 1:
        keys = jnp.concatenate([k_s[prows, :], keys], axis=0)
        values = jnp.concatenate([v_s[prows, :], values], axis=0)
        mask = jnp.concatenate([(c >= a) & has_prev, mask], axis=1)
    twice = lambda t: jnp.broadcast_to(t[None], (2,) + t.shape)
    return twice(keys), twice(values), mask


def _block_rows(branch, t):
    d, per_seg = DILATIONS[branch], SEGMENT_BLOCKS[branch]
    if d == 1:
        start = pl.multiple_of(t * ATTN_BLK, ATTN_BLK)
        prev = pl.multiple_of(jnp.maximum(t - 1, 0) * ATTN_BLK, ATTN_BLK)
        return pl.ds(start, ATTN_BLK), pl.ds(prev, ATTN_BLK), t > 0
    r, n = t // per_seg, t % per_seg
    start = n * (ATTN_BLK * d) + r
    prev = jnp.maximum(n - 1, 0) * (ATTN_BLK * d) + r
    return pl.ds(start, ATTN_BLK, stride=d), pl.ds(prev, ATTN_BLK, stride=d), n > 0


def _attn_fwd(proj, cos, sin_signed, exchanges=()):
    scale = HEAD_DIM ** -0.5

    def body(qkv_ref, cos_ref, sin_ref, out_ref, lse_ref, q_s, k_s, v_s, *branch_s):
        o_s, l_s = branch_s[:3], branch_s[3:]
        q_s[...] = _rope(qkv_ref[:, 0:128], cos_ref[...], sin_ref[...])
        k_s[...] = _rope(qkv_ref[:, 128:256], cos_ref[...], sin_ref[...])
        v_s[...] = qkv_ref[:, 256:384]
        heads = _head_masks()
        for branch in range(3):
            def block(t, carry, branch=branch):
                rows, prows, has_prev = _block_rows(branch, t)
                keys, values, mask = _block_keys(branch, k_s, v_s, rows, prows, has_prev)
                s = jnp.where(mask, BMM_NT(_both_heads(q_s[rows, :]), keys) * scale, NEG)
                m = jnp.max(s, axis=2, keepdims=True)
                e = jnp.exp(s - m)
                l = jnp.sum(e, axis=2, keepdims=True)
                o = BMM(e, values) / l
                lse_b = m + jnp.log(l)
                o_s[branch][rows, :] = jnp.where(heads[0], o[0], o[1])
                l_s[branch][rows, :] = jnp.where(heads[0], lse_b[0], lse_b[1])
                return carry

            lax.fori_loop(0, N_BLK, block, 0, unroll=4)
        l0, l1, l2 = l_s[0][...], l_s[1][...], l_s[2][...]
        m = jnp.maximum(jnp.maximum(l0, l1), l2)
        w0, w1, w2 = jnp.exp(l0 - m), jnp.exp(l1 - m), jnp.exp(l2 - m)
        den = w0 + w1 + w2
        out_ref[...] = (w0 * o_s[0][...] + w1 * o_s[1][...] + w2 * o_s[2][...]) / den
        lse_ref[...] = m + jnp.log(den)

    tab = pl.BlockSpec((SEQ, 128), lambda j: (0, 0))
    col = pl.BlockSpec((SEQ, 128), lambda j: (0, j))
    return _hosted_call(
        body, name="attn_fwd", steps=N_PAIR,
        in_specs=[pl.BlockSpec((SEQ, 384), lambda j: (0, j)), tab, tab],
        out_specs=[col, col],
        out_shape=[jax.ShapeDtypeStruct((SEQ, 2 * ATTN_W), F32), jax.ShapeDtypeStruct((SEQ, ATTN_W), F32)],
        scratch_shapes=[pltpu.VMEM((SEQ, 128), F32)] * 9,
        operands=(proj, cos, sin_signed), exchanges=exchanges)


def _attn_bwd(proj, cos, sin_signed, cat, lse, dcat, dproj, exchanges=()):
    scale = HEAD_DIM ** -0.5

    def body(qkv_ref, cos_ref, sin_ref, out_ref, lse_ref, do_ref, _, dqkv_ref, q_s, k_s, v_s, dq_s, dk_s, dv_s,
             dod_s):
        q_s[...] = _rope(qkv_ref[:, 0:128], cos_ref[...], sin_ref[...])
        k_s[...] = _rope(qkv_ref[:, 128:256], cos_ref[...], sin_ref[...])
        v_s[...] = qkv_ref[:, 256:384]
        dq_s[...] = jnp.zeros_like(dq_s)
        dk_s[...] = jnp.zeros_like(dk_s)
        dv_s[...] = jnp.zeros_like(dv_s)
        dod_s[...] = do_ref[...] * out_ref[...]
        heads = _head_masks()
        for branch in range(3):
            def block(t, carry, branch=branch):
                rows, prows, has_prev = _block_rows(branch, t)
                keys, values, mask = _block_keys(branch, k_s, v_s, rows, prows, has_prev)
                q2, do2 = _both_heads(q_s[rows, :]), _both_heads(do_ref[rows, :])
                lse_b, dod = lse_ref[rows, :], dod_s[rows, :]
                lse2 = jnp.concatenate(
                    [jnp.max(jnp.where(hm, lse_b, NEG), axis=1, keepdims=True)[None] for hm in heads], axis=0)
                delta = jnp.concatenate(
                    [jnp.sum(jnp.where(hm, dod, 0.0), axis=1, keepdims=True)[None] for hm in heads], axis=0)
                p = jnp.exp(jnp.where(mask, BMM_NT(q2, keys) * scale, NEG) - lse2)
                ds = p * (BMM_NT(do2, values) - delta) * scale
                dq = BMM(ds, keys)
                dk = BMM_TN(ds, q2)
                dv = BMM_TN(p, do2)
                dk, dv = dk[0] + dk[1], dv[0] + dv[1]
                dq_s[rows, :] += jnp.where(heads[0], dq[0], dq[1])
                if SEGMENT_BLOCKS[branch] > 1:
                    dk_s[rows, :] += dk[ATTN_BLK:]
                    dv_s[rows, :] += dv[ATTN_BLK:]

                    @pl.when(has_prev)
                    def _():
                        dk_s[prows, :] += dk[:ATTN_BLK]
                        dv_s[prows, :] += dv[:ATTN_BLK]
                else:
                    dk_s[rows, :] += dk
                    dv_s[rows, :] += dv
                return carry

            lax.fori_loop(0, N_BLK, block, 0, unroll=4)
        dqkv_ref[:, 0:128] = _rope(dq_s[...], cos_ref[...], -sin_ref[...]).astype(BF16)
        dqkv_ref[:, 128:256] = _rope(dk_s[...], cos_ref[...], -sin_ref[...]).astype(BF16)
        dqkv_ref[:, 256:384] = dv_s[...].astype(BF16)

    tab = pl.BlockSpec((SEQ, 128), lambda j: (0, 0))
    col = pl.BlockSpec((SEQ, 128), lambda j: (0, j))
    qkv = pl.BlockSpec((SEQ, 384), lambda j: (0, j))
    (dproj,), results = _hosted_call(
        body, name="attn_bwd", steps=N_PAIR,
        in_specs=[qkv, tab, tab, col, col, col, pl.BlockSpec(memory_space=pl.ANY)],
        out_specs=[qkv],
        out_shape=[jax.ShapeDtypeStruct((SEQ, IN_PAD), BF16)],
        scratch_shapes=[pltpu.VMEM((SEQ, 128), F32)] * 7,
        operands=(proj, cos, sin_signed, cat, lse, dcat, dproj), exchanges=exchanges, aliases={6: 0})
    return dproj, results


def _bdot(a, b, dims, precision=None):
    if precision is None:
        a = a.astype(BF16)
        b = b.astype(BF16)
    return lax.dot_general(a, b, (dims, ((0,), (0,))), preferred_element_type=F32, precision=precision)


def _make_bmm(precision):
    @jax.custom_vjp
    def nn(a, b):
        return _bdot(a, b, ((2,), (1,)), precision)

    @jax.custom_vjp
    def nt(a, b):
        return _bdot(a, b, ((2,), (2,)), precision)

    @jax.custom_vjp
    def tn(a, b):
        return _bdot(a, b, ((1,), (1,)), precision)

    nn.defvjp(lambda a, b: (nn(a, b), (a, b)), lambda r, g: (nt(g, r[1]), tn(r[0], g)))
    nt.defvjp(lambda a, b: (nt(a, b), (a, b)), lambda r, g: (nn(g, r[1]), tn(g, r[0])))
    tn.defvjp(lambda a, b: (tn(a, b), (a, b)), lambda r, g: (nt(r[1], g), nn(r[0], g)))
    return nn, nt, tn


BMM, BMM_NT, BMM_TN = _make_bmm(None)
BMM3, BMM3_NT, BMM3_TN = _make_bmm(lax.Precision.HIGH)
MM3, _, _ = _make_mm(lax.Precision.HIGH)


def _head_lanes(t, off):
    lane = lax.broadcasted_iota(jnp.int32, (1, 128), 1)
    return jnp.concatenate(
        [jnp.sum(t * (lane == off + h).astype(F32), axis=1, keepdims=True)[None] for h in range(NDH)], axis=0)


@jax.custom_vjp
def _unit_lower_inverse(a_mat):
    c = a_mat.shape[1]
    eye = (lax.broadcasted_iota(jnp.int32, (c, c), 0) == lax.broadcasted_iota(jnp.int32, (c, c), 1)).astype(F32)
    power = -a_mat
    t_inv = eye + power
    for _ in range(5):
        power = BMM3(power, power)
        t_inv = t_inv + BMM3(t_inv, power)
    return t_inv


def _unit_lower_inverse_fwd(a_mat):
    t_inv = _unit_lower_inverse(a_mat)
    return t_inv, t_inv


def _unit_lower_inverse_bwd(t_inv, d_inv):
    return (-BMM3_NT(BMM3_TN(t_inv, d_inv), t_inv),)


_unit_lower_inverse.defvjp(_unit_lower_inverse_fwd, _unit_lower_inverse_bwd)


DN_STEP_CHUNKS = 4
DN_STEP_ROWS = DN_STEP_CHUNKS * CH
DN_STEPS = NCH // DN_STEP_CHUNKS
DN_BATCH = DN_STEP_CHUNKS * NDH


def _delta_chunks(qr, kr, vr, z, tail, alog_row, dt_row, nw, state):
    c = qr.shape[1]
    tails = [tail[CH * n:CH * (n + 1)] for n in range(DN_STEP_CHUNKS)]
    per_chunk = lambda t: jnp.concatenate([t] * DN_STEP_CHUNKS, axis=0)
    beta = _sigmoid(jnp.concatenate([_head_lanes(t, 0) for t in tails], axis=0))
    a_raw = jnp.concatenate([_head_lanes(t, NDH) for t in tails], axis=0)
    g = -jnp.exp(per_chunk(_head_lanes(alog_row, 0))) * _softplus(a_raw + per_chunk(_head_lanes(dt_row, 0)))

    q = qr * lax.rsqrt(jnp.sum(qr * qr, axis=2, keepdims=True) + EPS) * (128 ** -0.5)
    k = kr * lax.rsqrt(jnp.sum(kr * kr, axis=2, keepdims=True) + EPS)

    ri = lax.broadcasted_iota(jnp.int32, (c, c), 0)
    ci = lax.broadcasted_iota(jnp.int32, (c, c), 1)
    tril = ri >= ci
    lane = lax.broadcasted_iota(jnp.int32, (1, 128), 1)
    pick = [(lane == b).astype(F32) for b in range(DN_BATCH)]
    g_lanes = sum(g[b] * pick[b] for b in range(DN_BATCH))
    g_sums = MM3(tril.astype(F32), g_lanes)
    gc = jnp.concatenate([jnp.sum(g_sums * pick[b], axis=1, keepdims=True)[None] for b in range(DN_BATCH)],
                         axis=0)
    g_row = jnp.swapaxes(jnp.broadcast_to(gc, (DN_BATCH, c, c)), 1, 2)
    decay = jnp.where(tril, jnp.exp(jnp.where(tril, gc - g_row, 0.0)), 0.0)
    kb = k * beta
    t_inv = _unit_lower_inverse(jnp.where(ri > ci, BMM_NT(kb, k) * decay, 0.0))
    eg = jnp.exp(gc)
    u = BMM(t_inv, vr * beta)
    w = BMM(t_inv, kb * eg)
    qk = BMM_NT(q, k) * decay
    g_tot = jnp.sum(g, axis=1, keepdims=True)
    q_dec = q * eg
    k_dec = k * jnp.exp(g_tot - gc)
    outs = []
    for n in range(DN_STEP_CHUNKS):
        heads = slice(NDH * n, NDH * (n + 1))
        v_new = u[heads] - BMM(w[heads], state)
        outs.append(BMM(q_dec[heads], state) + BMM(qk[heads], v_new))
        state = state * jnp.exp(g_tot[heads]) + BMM_TN(k_dec[heads], v_new)
    o = jnp.concatenate(outs, axis=0)
    on = o * lax.rsqrt(jnp.mean(o * o, axis=2, keepdims=True) + EPS) * nw
    return on * _silu(z), state


def _heads(v, off=0):
    return jnp.concatenate([v[None, CH * n:CH * (n + 1), off + 128 * h:off + 128 * (h + 1)]
                            for n in range(DN_STEP_CHUNKS) for h in range(NDH)], axis=0)


def _unheads(t):
    return jnp.concatenate([jnp.concatenate([t[NDH * n + h] for h in range(NDH)], axis=1)
                            for n in range(DN_STEP_CHUNKS)], axis=0)


def _delta_fwd(c_qkv, proj, alog_row, dt_row, nw, cat, exchanges=()):
    def body(c_ref, z_ref, tail_ref, al_ref, dt_ref, nw_ref, _, y_ref, st_ref, state):
        @pl.when(pl.program_id(0) == 0)
        def _():
            state[...] = jnp.zeros_like(state)

        cv = c_ref[...].astype(F32)
        st_ref[0] = state[...]
        y, new_state = _delta_chunks(_heads(cv), _heads(cv, 512), _heads(cv, 1024), _heads(z_ref[...]), tail_ref[...],
                                     al_ref[...], dt_ref[...], nw_ref[...], state[...])
        y_ref[...] = _unheads(y)
        state[...] = new_state

    row = pl.BlockSpec((1, 128), lambda n: (0, 0))
    rows = DN_STEP_ROWS
    return _hosted_call(
        body, name="delta_fwd", steps=DN_STEPS,
        in_specs=[pl.BlockSpec((rows, 1536), lambda n: (n, 0)), pl.BlockSpec((rows, 512), lambda n: (n, DN_Z_COL // 512)),
                  pl.BlockSpec((rows, 128), lambda n: (n, DN_TAIL_BLK)), row, row, row, pl.BlockSpec(memory_space=pl.ANY)],
        out_specs=[pl.BlockSpec((rows, 512), lambda n: (n, 1)),
                   pl.BlockSpec((1, NDH, 128, 128), lambda n: (n, 0, 0, 0))],
        out_shape=[jax.ShapeDtypeStruct((SEQ, 2 * ATTN_W), F32), jax.ShapeDtypeStruct((DN_STEPS, NDH, 128, 128), F32)],
        scratch_shapes=[pltpu.VMEM((NDH, 128, 128), F32)],
        operands=(c_qkv, proj, proj, alog_row, dt_row, nw, cat), exchanges=exchanges, aliases={6: 0})


def _delta_bwd(c_qkv, proj, alog_row, dt_row, nw, states, dcat, exchanges=()):
    def body(c_ref, z_ref, tail_ref, al_ref, dt_ref, nw_ref, st_ref, dy_ref,
             dp_ref, dc_ref, dal_ref, ddt_ref, dnw_ref, dstate):
        @pl.when(pl.program_id(0) == 0)
        def _():
            dstate[...] = jnp.zeros_like(dstate)
            dal_ref[...] = jnp.zeros_like(dal_ref)
            ddt_ref[...] = jnp.zeros_like(ddt_ref)
            dnw_ref[...] = jnp.zeros_like(dnw_ref)

        cv = c_ref[...].astype(F32)
        _, vjp = jax.vjp(_delta_chunks, _heads(cv), _heads(cv, 512), _heads(cv, 1024), _heads(z_ref[...]),
                         tail_ref[...], al_ref[...], dt_ref[...], nw_ref[...], st_ref[0])
        dq, dk, dv, dz, dtail, dal, ddt, dnw, dst = vjp((_heads(dy_ref[...]), dstate[...]))
        dstate[...] = dst
        dc_ref[...] = jnp.concatenate([_unheads(dq), _unheads(dk), _unheads(dv)], axis=1)
        dp_ref[...] = jnp.concatenate([_unheads(dz), dtail, jnp.zeros((DN_STEP_ROWS, 128), F32)], axis=1).astype(BF16)
        dal_ref[...] += dal
        ddt_ref[...] += ddt
        dnw_ref[...] += dnw

    rev = lambda n: DN_STEPS - 1 - n
    row = pl.BlockSpec((1, 128), lambda n: (0, 0))
    rows = DN_STEP_ROWS
    return _hosted_call(
        body, name="delta_bwd", steps=DN_STEPS,
        in_specs=[pl.BlockSpec((rows, 1536), lambda n: (rev(n), 0)),
                  pl.BlockSpec((rows, 512), lambda n: (rev(n), DN_Z_COL // 512)),
                  pl.BlockSpec((rows, 128), lambda n: (rev(n), DN_TAIL_BLK)), row, row, row,
                  pl.BlockSpec((1, NDH, 128, 128), lambda n: (rev(n), 0, 0, 0)),
                  pl.BlockSpec((rows, 512), lambda n: (rev(n), 1))],
        out_specs=[pl.BlockSpec((rows, 768), lambda n: (rev(n), DN_Z_COL // 768)),
                   pl.BlockSpec((rows, 1536), lambda n: (rev(n), 0)), row, row, row],
        out_shape=[jax.ShapeDtypeStruct((SEQ, IN_PAD), BF16), jax.ShapeDtypeStruct((SEQ, 1536), F32)]
        + [jax.ShapeDtypeStruct((1, 128), F32)] * 3,
        scratch_shapes=[pltpu.VMEM((NDH, 128, 128), F32)],
        operands=(c_qkv, proj, proj, alog_row, dt_row, nw, states, dcat), exchanges=exchanges)


def _place():
    x, y, c = lax.axis_index("x"), lax.axis_index("y"), lax.axis_index("c")
    other_chips = [(1 - x, y), (x, 1 - y), (1 - x, 1 - y)]
    return x, y, c, other_chips


def _gather_exchange(shards):
    n = len(shards)

    def copies(ins, outs, sems):
        send_sems, recv_sems, local_sems = sems
        x, y, c, chips = _place()
        me, sibling = (x, y, c), (x, y, 1 - c)

        def copy(b, k, block, to, src=None):
            slot = outs[b].at[4 * block[0] + 2 * block[1] + block[2]]
            return pltpu.make_async_remote_copy(
                src_ref=slot if src is None else src, dst_ref=slot,
                send_sem=send_sems.at[b, k], recv_sem=recv_sems.at[b, k], device_id=to, device_id_type=MESH)

        mine = [pltpu.make_async_copy(ins[b], outs[b].at[4 * x + 2 * y + c], local_sems.at[b]) for b in range(n)]
        first = []
        for b in range(n):
            first.append(copy(b, 0, me, sibling, src=ins[b]))
            first += [copy(b, 1 + j, me, (*chip, c), src=ins[b]) for j, chip in enumerate(chips)]
        over_ici = [copy(b, 1 + j, (*chip, c), me) for b in range(n) for j, chip in enumerate(chips)]
        passed = [copy(b, 4 + j, (*chip, c), sibling) for b in range(n) for j, chip in enumerate(chips)]
        from_sibling = []
        for b in range(n):
            from_sibling.append(copy(b, 0, sibling, me))
            from_sibling += [copy(b, 4 + j, (*chip, 1 - c), me) for j, chip in enumerate(chips)]
        return mine, first, over_ici, passed, from_sibling

    def start(ins, outs, sems):
        mine, first, _, _, _ = copies(ins, outs, sems)
        for cp in mine + first:
            cp.start()

    def middle(ins, outs, sems):
        _, _, over_ici, passed, _ = copies(ins, outs, sems)
        for arrived, onward in zip(over_ici, passed):
            arrived.wait_recv()
            onward.start()

    def finish(ins, outs, sems):
        mine, first, _, passed, from_sibling = copies(ins, outs, sems)
        for cp in from_sibling:
            cp.wait_recv()
        for cp in first + passed:
            cp.wait_send()
        for cp in mine:
            cp.wait()

    return Exchange(shards, [jax.ShapeDtypeStruct((N_DEV,) + s.shape, s.dtype) for s in shards],
                    [pltpu.SemaphoreType.DMA((n, 7)), pltpu.SemaphoreType.DMA((n, 7)), pltpu.SemaphoreType.DMA((n,))],
                    start, middle, finish)


def _sibling_exchange(gs):
    n = len(gs)

    def copies(ins, outs, sems):
        send_sems, recv_sems = sems
        x, y, c, _ = _place()
        return [pltpu.make_async_remote_copy(
            src_ref=ins[b].at[2 * p + (1 - c)], dst_ref=outs[b].at[p],
            send_sem=send_sems.at[b, p], recv_sem=recv_sems.at[b, p],
            device_id=(x, y, 1 - c), device_id_type=MESH) for b in range(n) for p in range(4)]

    def start(ins, outs, sems):
        for cp in copies(ins, outs, sems):
            cp.start()

    def finish(ins, outs, sems):
        for cp in copies(ins, outs, sems):
            cp.wait()

    return Exchange(gs, [jax.ShapeDtypeStruct((4,) + g.shape[1:], g.dtype) for g in gs],
                    [pltpu.SemaphoreType.DMA((n, 4)), pltpu.SemaphoreType.DMA((n, 4))], start, None, finish)


def _chips_exchange(hs):
    n = len(hs)

    def copies(ins, outs, sems):
        send_sems, recv_sems, local_sems = sems
        x, y, c, chips = _place()
        my_chip = 2 * x + y
        local = [pltpu.make_async_copy(ins[b].at[my_chip], outs[b].at[my_chip], local_sems.at[b]) for b in range(n)]
        sends, arrivals = [], []
        for b in range(n):
            for k, (px, py) in enumerate(chips):
                peer = 2 * px + py
                sends.append(pltpu.make_async_remote_copy(
                    src_ref=ins[b].at[peer], dst_ref=outs[b].at[my_chip],
                    send_sem=send_sems.at[b, k], recv_sem=recv_sems.at[b, k],
                    device_id=(px, py, c), device_id_type=MESH))
                arrivals.append(pltpu.make_async_remote_copy(
                    src_ref=ins[b].at[peer], dst_ref=outs[b].at[peer],
                    send_sem=send_sems.at[b, k], recv_sem=recv_sems.at[b, k],
                    device_id=(px, py, c), device_id_type=MESH))
        return local, sends, arrivals

    def start(ins, outs, sems):
        local, sends, _ = copies(ins, outs, sems)
        for cp in local + sends:
            cp.start()

    def finish(ins, outs, sems):
        local, sends, arrivals = copies(ins, outs, sems)
        for cp in arrivals:
            cp.wait_recv()
        for cp in sends:
            cp.wait_send()
        for cp in local:
            cp.wait()

    return Exchange(hs, [jax.ShapeDtypeStruct(h.shape, h.dtype) for h in hs],
                    [pltpu.SemaphoreType.DMA((n, 3)), pltpu.SemaphoreType.DMA((n, 3)), pltpu.SemaphoreType.DMA((n,))],
                    start, None, finish)


def _run_exchange(exchange, name):
    n_in, n_out = len(exchange.operands), len(exchange.out_shapes)

    def body(*refs):
        ins, outs, sems = refs[:n_in], refs[n_in:n_in + n_out], refs[n_in + n_out:]
        exchange.start(ins, outs, sems)
        if exchange.middle is not None:
            exchange.middle(ins, outs, sems)
        exchange.finish(ins, outs, sems)

    return pl.pallas_call(
        body, name=name,
        in_specs=[HBM_SPEC] * n_in, out_specs=[HBM_SPEC] * n_out,
        out_shape=exchange.out_shapes, scratch_shapes=exchange.sems,
    )(*exchange.operands)


def _pair_add(g, r, core, name):
    _, nr, nc = g.shape
    tr = nr // 2 if nr % 32 == 0 else nr

    def body(core_ref, g_ref, r_ref, o_ref):
        o_ref[...] = (g_ref[...].astype(F32) + r_ref[...].astype(F32)).astype(BF16)

    return pl.pallas_call(
        body, name=name,
        grid_spec=pltpu.PrefetchScalarGridSpec(
            num_scalar_prefetch=1, grid=(4, nr // tr),
            in_specs=[pl.BlockSpec((1, tr, nc), lambda p, i, core: (2 * p + core[0], i, 0)),
                      pl.BlockSpec((1, tr, nc), lambda p, i, core: (p, i, 0))],
            out_specs=pl.BlockSpec((1, tr, nc), lambda p, i, core: (p, i, 0))),
        out_shape=jax.ShapeDtypeStruct(r.shape, BF16),
        compiler_params=_cp("parallel", "parallel"),
    )(core, g, r)


def _all_gather_sum_small(v):
    rows = v.shape[0]

    def body(x_ref, sum_ref, out_ref, send_sems, recv_sems, local_sem):
        x, y, c, chips = _place()
        me, sibling = (x, y, c), (x, y, 1 - c)

        def block(px, py, pc):
            return out_ref.at[pl.ds((4 * px + 2 * py + pc) * rows, rows), :]

        def copy(k, blk, to, src=None):
            return pltpu.make_async_remote_copy(
                src_ref=block(*blk) if src is None else src, dst_ref=block(*blk),
                send_sem=send_sems.at[k], recv_sem=recv_sems.at[k], device_id=to, device_id_type=MESH)

        mine = pltpu.make_async_copy(x_ref, block(*me), local_sem)
        mine.start()
        first = [copy(0, me, sibling, src=x_ref)]
        first += [copy(1 + j, me, (*chip, c), src=x_ref) for j, chip in enumerate(chips)]
        for cp in first:
            cp.start()
        passed = [copy(4 + j, (*chip, c), sibling) for j, chip in enumerate(chips)]
        for j, chip in enumerate(chips):
            copy(1 + j, (*chip, c), me).wait_recv()
            passed[j].start()
        copy(0, sibling, me).wait_recv()
        for j, chip in enumerate(chips):
            copy(4 + j, (*chip, 1 - c), me).wait_recv()
        for cp in first + passed:
            cp.wait_send()
        mine.wait()
        total = out_ref[pl.ds(0, rows), :]
        for d in range(1, N_DEV):
            total = total + out_ref[pl.ds(d * rows, rows), :]
        sum_ref[...] = total

    vm = pl.BlockSpec(memory_space=pltpu.VMEM)
    return pl.pallas_call(
        body, name="small_all_reduce",
        in_specs=[vm], out_specs=[vm],
        out_shape=[jax.ShapeDtypeStruct((rows, 128), F32)],
        scratch_shapes=[pltpu.VMEM((N_DEV * rows, 128), F32), pltpu.SemaphoreType.DMA((7,)),
                        pltpu.SemaphoreType.DMA((7,)), pltpu.SemaphoreType.DMA],
    )(v)[0]


def _adamw(w, g, m, v):
    m = ADAM_B1 * m + (1.0 - ADAM_B1) * g
    v = ADAM_B2 * v + (1.0 - ADAM_B2) * (g * g)
    m_hat = m / (1.0 - ADAM_B1 ** ADAM_STEP)
    v_hat = v / (1.0 - ADAM_B2 ** ADAM_STEP)
    delta = -ADAM_LR * (m_hat / (jnp.sqrt(v_hat) + ADAM_EPS) + ADAM_WD * w)
    return delta, m, v


ADAM_TILE = dict(w_in=(IN_COLS // N_DEV, 256), w_out=(128, D_MODEL), ffn_w_in=(176, D_MODEL), ffn_w_out=(176, D_MODEL))


def _sum_chips(p):
    p = p.astype(F32)
    return (p[0] + p[1]) + (p[2] + p[3])


def _adamw_sharded(parts, w, m, v, tile, name, exchanges=()):
    nl, nr, nc = w.shape
    tr, tc = tile
    ni, nj = nr // tr, nc // tc
    where = lambda s: (s // (ni * nj), (s // nj) % ni, s % nj)

    def body(*refs):
        p_refs, (w_ref, m_ref, v_ref, g_ref, d_ref, nm_ref, nv_ref) = refs[:nl], refs[nl:]
        layer = where(pl.program_id(0))[0]
        p = p_refs[0][...]
        for l in range(1, nl):
            p = jnp.where(layer == l, p_refs[l][...], p)
        g = _sum_chips(p)
        delta, nm, nv = _adamw(w_ref[0], g, m_ref[0], v_ref[0])
        g_ref[0] = g
        d_ref[0] = delta
        nm_ref[0] = nm
        nv_ref[0] = nv

    def part_index(s, own):
        l, i, j = where(s)
        return 0, jnp.where(l == own, i, 0), j

    blk = pl.BlockSpec((1, tr, tc), where)
    return _hosted_call(
        body, name=name, steps=nl * ni * nj,
        in_specs=[pl.BlockSpec((4, tr, tc), functools.partial(part_index, own=own)) for own in range(nl)]
        + [blk, blk, blk],
        out_specs=[blk] * 4,
        out_shape=[jax.ShapeDtypeStruct(w.shape, F32)] * 4,
        scratch_shapes=[], operands=(*parts, w, m, v), exchanges=exchanges)


def _adamw_small(g, w, m, v):
    def body(g_ref, w_ref, m_ref, v_ref, d_ref, nm_ref, nv_ref):
        delta, nm, nv = _adamw(w_ref[...], g_ref[...], m_ref[...], v_ref[...])
        d_ref[...] = delta
        nm_ref[...] = nm
        nv_ref[...] = nv

    return pl.pallas_call(
        body, name="adamw_small",
        out_shape=[jax.ShapeDtypeStruct(g.shape, F32)] * 3,
    )(g, w, m, v)


def _packed_rows(n):
    return -(-n // 1024) * 8


def _pack(arrays, rows):
    pieces = []
    for a in arrays:
        flat = a.reshape(-1).astype(F32)
        nr = _packed_rows(flat.shape[0])
        pieces.append(jnp.pad(flat, (0, nr * 128 - flat.shape[0])).reshape(nr, 128))
    used = sum(p.shape[0] for p in pieces)
    return jnp.concatenate(pieces + [jnp.zeros((rows - used, 128), F32)] * (rows > used), axis=0)


def _unpack(packed, shapes):
    out, row = [], 0
    for s in shapes:
        n = math.prod(s)
        out.append(packed[row:row + _packed_rows(n)].reshape(-1)[:n].reshape(s))
        row += _packed_rows(n)
    return out


def _row(v, width=None):
    v = v.reshape(1, -1)
    return v if width is None else jnp.pad(v, ((0, 0), (0, width - v.shape[1])))


def _layer_fwd(x, wts, tables, hosted):
    h = _norm_fwd(x, wts["norm_pre_mix"], "norm_pre_mix")
    proj = _matmul(h, wts["w_in"], tb=True, tm=SEQ, tn=768, tk=1024, name="mm_proj")
    (cat, lse), got = _attn_fwd(proj, *tables, exchanges=hosted["attn"][0])
    hosted["attn"][1](got)
    c_qkv = _dnconv_fwd(proj, wts["dn_conv_w"])
    (cat, states), got = _delta_fwd(c_qkv, proj, wts["dn_a_log"], wts["dn_dt_bias"], wts["dn_norm_w"], cat,
                                    exchanges=hosted["delta"][0])
    hosted["delta"][1](got)
    mix = _matmul(cat, wts["w_out"], tm=512, tn=1024, tk=1024, name="mm_mix")
    x1 = _resnorm_fwd(x, mix, wts["norm_post_mix"], "norm_post_mix")
    h2 = _norm_fwd(x1, wts["norm_pre_ffn"], "norm_pre_ffn")
    pre = _matmul(h2, wts["ffn_w_in"], tb=True, tm=SEQ, tn=512, tk=1024, name="mm_ffn_in", out_dtype=BF16)
    act, got = _ffact_fwd(pre, wts["ffn_conv_w"], wts["ffn_conv_b"], exchanges=hosted["ffact"][0])
    hosted["ffact"][1](got)
    f = _matmul(act, wts["ffn_w_out"], tm=512, tn=1024, tk=D_FF, name="mm_ffn_out")
    x2 = _resnorm_fwd(x1, f, wts["norm_post_ffn"], "norm_post_ffn")
    saved = dict(x=x, h=h, proj=proj, lse=lse, c_qkv=c_qkv, states=states, cat=cat, mix=mix, x1=x1, h2=h2, pre=pre,
                 act=act, f=f)
    return x2, saved


def _layer_bwd(dx2, wts, s, tables, ffact_exchanges=(), delta_exchanges=None, attn_exchanges=None,
               norm_exchanges=None):
    g = {}
    df, g["norm_post_ffn"], _ = _norm_bwd(s["f"], wts["norm_post_ffn"], dx2, None, "norm_post_ffn_bwd", BF16)
    dact = _matmul(df, wts["ffn_w_out"], tb=True, tm=SEQ, tn=1408, tk=1024, name="mm_dact", out_dtype=BF16)
    g["ffn_w_out"] = _matmul(s["act"], df, ta=True, tm=1408, tn=512, tk=SEQ, name="mm_dw_ffn_out", out_dtype=BF16)
    (dpre, g["ffn_conv_w"], g["ffn_conv_b"]), got = _ffact_bwd(s["pre"], wts["ffn_conv_w"], wts["ffn_conv_b"], dact,
                                                               exchanges=ffact_exchanges)
    dh2 = _matmul(dpre, wts["ffn_w_in"], tm=1024, tn=1024, tk=D_FF, name="mm_dh2")
    g["ffn_w_in"] = _matmul(dpre, s["h2"], ta=True, tm=512, tn=1024, tk=SEQ, name="mm_dw_ffn_in", out_dtype=BF16)
    dx1, g["norm_pre_ffn"], _ = _norm_bwd(s["x1"], wts["norm_pre_ffn"], dh2, dx2, "norm_pre_ffn_bwd")
    dmix, g["norm_post_mix"], _ = _norm_bwd(s["mix"], wts["norm_post_mix"], dx1, None, "norm_post_mix_bwd", BF16)
    dcat = _matmul(dmix, wts["w_out"], tb=True, tm=SEQ, tn=512, tk=1024, name="mm_dcat")
    g["w_out"] = _matmul(s["cat"], dmix, ta=True, tm=1024, tn=512, tk=SEQ, name="mm_dw_out", out_dtype=BF16)
    (dproj, dc, g["dn_a_log"], g["dn_dt_bias"], g["dn_norm_w"]), got = _delta_bwd(
        s["c_qkv"], s["proj"], wts["dn_a_log"], wts["dn_dt_bias"], wts["dn_norm_w"], s["states"], dcat,
        exchanges=delta_exchanges(g, got) if delta_exchanges is not None else ())
    dproj, got = _attn_bwd(s["proj"], *tables, s["cat"], s["lse"], dcat, dproj,
                           exchanges=attn_exchanges(got) if attn_exchanges is not None else ())
    dproj, g["dn_conv_w"] = _dnconv_bwd(s["proj"], wts["dn_conv_w"], dc, dproj)
    dh = _matmul(dproj, wts["w_in"], tm=1024, tn=1024, tk=IN_PAD // 2, name="mm_dh")
    g["w_in"] = _matmul(dproj, s["h"], ta=True, tm=768, tn=1024, tk=SEQ, name="mm_dw_in", out_dtype=BF16)
    dx, g["norm_pre_mix"], got_norm = _norm_bwd(s["x"], wts["norm_pre_mix"], dh, dx1, "norm_pre_mix_bwd",
                                                exchanges=norm_exchanges(g) if norm_exchanges is not None else ())
    return dx, g, got, got_norm


BIG = ("w_in", "w_out", "ffn_w_in", "ffn_w_out")
COLUMN_SHARDED = ("w_in", "ffn_w_in")
SMALL_SHARDED = ("dn_conv_w", "ffn_conv_w")
REPLICATED = ("dn_a_log", "dn_dt_bias", "dn_norm_w", "ffn_conv_b", "norm_pre_mix", "norm_post_mix", "norm_pre_ffn",
              "norm_post_ffn")
WEIGHTS = ("w_in", "dn_conv_w", "dn_a_log", "dn_dt_bias", "dn_norm_w", "w_out", "ffn_w_in", "ffn_conv_w", "ffn_conv_b",
           "ffn_w_out", "norm_pre_mix", "norm_post_mix", "norm_pre_ffn", "norm_post_ffn")
FULL_SHAPE = dict(dn_conv_w=(DEPTH, 4, 1536), ffn_conv_w=(DEPTH, 3, 2 * D_FF), dn_a_log=(DEPTH, NDH),
                  dn_dt_bias=(DEPTH, NDH), dn_norm_w=(DEPTH, 128), ffn_conv_b=(DEPTH, 2 * D_FF),
                  norm_pre_mix=(DEPTH, D_MODEL), norm_post_mix=(DEPTH, D_MODEL), norm_pre_ffn=(DEPTH, D_MODEL),
                  norm_post_ffn=(DEPTH, D_MODEL))
SMALL_GRAD_ORDER = REPLICATED + SMALL_SHARDED
SMALL_GRAD_ROWS = 544
SMALL_W_ROWS = 56
SMALL_ADAM_ROWS = 232


def _w_in_rows_to_kernel_order(t):
    qkv = t[:QKV_W].reshape(3, N_PAIR, 128, -1).swapaxes(0, 1).reshape(QKV_W, -1)
    return jnp.pad(jnp.concatenate([qkv, t[QKV_W:]], axis=0), ((0, IN_PAD - IN_COLS), (0, 0)))


def _w_in_rows_from_kernel_order(t):
    qkv = t[:QKV_W].reshape(N_PAIR, 3, 128, -1).swapaxes(0, 1).reshape(QKV_W, -1)
    return jnp.concatenate([qkv, t[QKV_W:IN_COLS]], axis=0)


def _interleave_ff_rows(t):
    return t.reshape(2, FF_BLKS, 128, -1).swapaxes(0, 1).reshape(2 * D_FF, -1)


def _deinterleave_ff_rows(t):
    return t.reshape(FF_BLKS, 2, 128, -1).swapaxes(0, 1).reshape(2 * D_FF, -1)


def kernel(x, w_in, dn_conv_w, dn_a_log, dn_dt_bias, dn_norm_w, w_out, ffn_w_in, ffn_conv_w, ffn_conv_b, ffn_w_out, norm_pre_mix, norm_post_mix, norm_pre_ffn, norm_post_ffn, loss_target, m_w_in, m_dn_conv_w, m_dn_a_log, m_dn_dt_bias, m_dn_norm_w, m_w_out, m_ffn_w_in, m_ffn_conv_w, m_ffn_conv_b, m_ffn_w_out, m_norm_pre_mix, m_norm_post_mix, m_norm_pre_ffn, m_norm_post_ffn, v_w_in, v_dn_conv_w, v_dn_a_log, v_dn_dt_bias, v_dn_norm_w, v_w_out, v_ffn_w_in, v_ffn_conv_w, v_ffn_conv_b, v_ffn_w_out, v_norm_pre_mix, v_norm_post_mix, v_norm_pre_ffn, v_norm_post_ffn):
    local = dict(w_in=w_in, dn_conv_w=dn_conv_w, dn_a_log=dn_a_log, dn_dt_bias=dn_dt_bias, dn_norm_w=dn_norm_w,
                 w_out=w_out, ffn_w_in=ffn_w_in, ffn_conv_w=ffn_conv_w, ffn_conv_b=ffn_conv_b, ffn_w_out=ffn_w_out,
                 norm_pre_mix=norm_pre_mix, norm_post_mix=norm_post_mix, norm_pre_ffn=norm_pre_ffn,
                 norm_post_ffn=norm_post_ffn)
    mom_m = dict(w_in=m_w_in, dn_conv_w=m_dn_conv_w, dn_a_log=m_dn_a_log, dn_dt_bias=m_dn_dt_bias,
                 dn_norm_w=m_dn_norm_w, w_out=m_w_out, ffn_w_in=m_ffn_w_in, ffn_conv_w=m_ffn_conv_w,
                 ffn_conv_b=m_ffn_conv_b, ffn_w_out=m_ffn_w_out, norm_pre_mix=m_norm_pre_mix,
                 norm_post_mix=m_norm_post_mix, norm_pre_ffn=m_norm_pre_ffn, norm_post_ffn=m_norm_post_ffn)
    mom_v = dict(w_in=v_w_in, dn_conv_w=v_dn_conv_w, dn_a_log=v_dn_a_log, dn_dt_bias=v_dn_dt_bias,
                 dn_norm_w=v_dn_norm_w, w_out=v_w_out, ffn_w_in=v_ffn_w_in, ffn_conv_w=v_ffn_conv_w,
                 ffn_conv_b=v_ffn_conv_b, ffn_w_out=v_ffn_w_out, norm_pre_mix=v_norm_pre_mix,
                 norm_post_mix=v_norm_post_mix, norm_pre_ffn=v_norm_pre_ffn, norm_post_ffn=v_norm_post_ffn)
    dev = 4 * lax.axis_index("x") + 2 * lax.axis_index("y") + lax.axis_index("c")
    core = lax.axis_index("c").astype(jnp.int32).reshape(1)

    def shard(n, l):
        s = local[n].transpose(0, 2, 1) if n in COLUMN_SHARDED else local[n]
        return s[l].astype(BF16)

    def matrix(n, gathered):
        if n == "w_in":
            return _w_in_rows_to_kernel_order(gathered.reshape(IN_COLS, D_MODEL))
        if n == "ffn_w_in":
            return _interleave_ff_rows(gathered.reshape(2 * D_FF, D_MODEL))
        return gathered.reshape(-1, D_MODEL)

    small_w = _pack([dn_conv_w, ffn_conv_w], SMALL_W_ROWS)
    g_w_in0, g_small = _run_exchange(_gather_exchange([shard("w_in", 0), small_w]), "weights_all_gather")
    n_dn, n_ff = DEPTH * 4 * 192, DEPTH * 3 * 704
    dn_rows = _packed_rows(n_dn)
    sm_dn = g_small[:, :dn_rows].reshape(N_DEV, -1)[:, :n_dn]
    sm_ff = g_small[:, dn_rows:].reshape(N_DEV, -1)[:, :n_ff]
    full_dn_conv = sm_dn.reshape(N_DEV, DEPTH, 4, 192).transpose(1, 2, 0, 3).reshape(DEPTH, 4, 1536)
    full_ff_conv = _interleave_ff(sm_ff.reshape(N_DEV, DEPTH, 3, 704).transpose(1, 2, 0, 3).reshape(DEPTH, 3, 2 * D_FF))

    def small_weights(l):
        wts = dict(dn_conv_w=full_dn_conv[l], ffn_conv_w=full_ff_conv[l], ffn_conv_b=_interleave_ff(_row(ffn_conv_b[l])),
                   dn_a_log=_row(dn_a_log[l], 128), dn_dt_bias=_row(dn_dt_bias[l], 128))
        for n in ("dn_norm_w", "norm_pre_mix", "norm_post_mix", "norm_pre_ffn", "norm_post_ffn"):
            wts[n] = _row(local[n][l])
        return wts

    weights = [small_weights(l) for l in range(DEPTH)]
    weights[0]["w_in"] = matrix("w_in", g_w_in0)

    def gather_behind(wanted):
        def deliver(got):
            for (n, l), g in zip(wanted, got[0]):
                weights[l][n] = matrix(n, g)

        return [_gather_exchange([shard(n, l) for n, l in wanted])], deliver

    nothing = ((), lambda got: None)

    tables = _rope_tables()
    act, saved0 = _layer_fwd(x[0], weights[0], tables, dict(
        attn=gather_behind([("ffn_w_in", 0)]), delta=gather_behind([("w_out", 0), ("ffn_w_out", 0)]),
        ffact=gather_behind([("w_in", 1)])))
    act, saved1 = _layer_fwd(act, weights[1], tables, dict(
        attn=gather_behind([("ffn_w_in", 1)]), delta=gather_behind([("w_out", 1), ("ffn_w_out", 1)]), ffact=nothing))
    loss_part, dact = _loss_fwd_bwd(act, loss_target[0])

    def to_devices(name, t):
        if name == "w_in":
            t = _w_in_rows_from_kernel_order(t)
        if name == "ffn_w_in":
            t = _deinterleave_ff_rows(t)
        return t.reshape(N_DEV, t.shape[0] // N_DEV, t.shape[1])

    def pair_sums(names, layer, to_dev, from_sibling):
        return [_pair_add(gd, r, core, "grads_pair_add_%s_%d" % (n, layer))
                for n, gd, r in zip(names, to_dev, from_sibling)]

    early = ("w_out", "ffn_w_in", "ffn_w_out")
    grads, parts, stash = [None] * DEPTH, {}, {}

    def delta_exchanges1(g, got_ffact):
        stash["early1"] = [to_devices(n, g[n]) for n in early]
        return [_sibling_exchange(stash["early1"])]

    def attn_exchanges1(got_delta):
        return [_chips_exchange(pair_sums(early, 1, stash["early1"], got_delta[0]))]

    dact, grads[1], got_attn, _ = _layer_bwd(dact, weights[1], saved1, tables, (), delta_exchanges1, attn_exchanges1)
    for n, p in zip(early, got_attn[0]):
        parts[n, 1] = p
    w_in1 = [to_devices("w_in", grads[1]["w_in"])]

    def delta_exchanges0(g, got_ffact):
        stash["early0"] = [to_devices(n, g[n]) for n in early]
        return [_chips_exchange(pair_sums(("w_in",), 1, w_in1, got_ffact[0])), _sibling_exchange(stash["early0"])]

    def attn_exchanges0(got_delta):
        parts["w_in", 1], = got_delta[0]
        return [_chips_exchange(pair_sums(early, 0, stash["early0"], got_delta[1]))]

    def norm_exchanges0(g):
        stash["w_in0"] = [to_devices("w_in", g["w_in"])]
        return [_sibling_exchange(stash["w_in0"])]

    dact, grads[0], got_attn, got_norm = _layer_bwd(dact, weights[0], saved0, tables, [_sibling_exchange(w_in1)],
                                                    delta_exchanges0, attn_exchanges0, norm_exchanges0)
    for n, p in zip(early, got_attn[0]):
        parts[n, 0] = p
    grad_x = dact[None]
    last_to_chips = [_chips_exchange(pair_sums(("w_in",), 0, stash["w_in0"], got_norm[0]))]

    def small_grad(name):
        t = jnp.stack([grads[l][name] for l in range(DEPTH)])
        if name in ("dn_a_log", "dn_dt_bias"):
            t = t[:, 0, :NDH]
        if name in ("ffn_conv_w", "ffn_conv_b"):
            t = _deinterleave_ff(t)
        return t.reshape(FULL_SHAPE[name])

    small_part = _pack([small_grad(n) for n in SMALL_GRAD_ORDER] + [loss_part[0, :1]], SMALL_GRAD_ROWS)
    small_sum = _all_gather_sum_small(small_part)
    small_g = dict(zip(SMALL_GRAD_ORDER + ("loss",), _unpack(small_sum, [FULL_SHAPE[n] for n in SMALL_GRAD_ORDER] + [(1,)])))
    loss = small_g["loss"][0]
    small_g["dn_conv_w"] = lax.dynamic_slice_in_dim(small_g["dn_conv_w"], dev * 192, 192, axis=2)
    small_g["ffn_conv_w"] = lax.dynamic_slice_in_dim(small_g["ffn_conv_w"], dev * 704, 704, axis=2)

    out_g, out_d, out_m, out_v = {}, {}, {}, {}
    for n in ("ffn_w_in", "w_out", "ffn_w_out", "w_in"):
        turn = (lambda t: t.transpose(0, 2, 1)) if n in COLUMN_SHARDED else (lambda t: t)
        outs, got = _adamw_sharded([parts[n, l] for l in range(DEPTH)], turn(local[n]), turn(mom_m[n]), turn(mom_v[n]),
                                   ADAM_TILE[n], "adamw_" + n, last_to_chips if n == "ffn_w_in" else ())
        if n == "ffn_w_in":
            parts["w_in", 0], = got[0]
        out_g[n], out_d[n], out_m[n], out_v[n] = [turn(t) for t in outs]
    shapes = [small_g[n].shape for n in SMALL_GRAD_ORDER]
    d_s, m_s, v_s = _adamw_small(_pack([small_g[n] for n in SMALL_GRAD_ORDER], SMALL_ADAM_ROWS),
                                 _pack([local[n] for n in SMALL_GRAD_ORDER], SMALL_ADAM_ROWS),
                                 _pack([mom_m[n] for n in SMALL_GRAD_ORDER], SMALL_ADAM_ROWS),
                                 _pack([mom_v[n] for n in SMALL_GRAD_ORDER], SMALL_ADAM_ROWS))
    for n, d, m, v in zip(SMALL_GRAD_ORDER, _unpack(d_s, shapes), _unpack(m_s, shapes), _unpack(v_s, shapes)):
        out_g[n], out_d[n], out_m[n], out_v[n] = small_g[n], d, m, v
    return (loss, grad_x, *[out_g[n] for n in WEIGHTS], *[out_d[n] for n in WEIGHTS],
            *[out_m[n] for n in WEIGHTS], *[out_v[n] for n in WEIGHTS])
```

```python
import functools
import math

import jax
import jax.numpy as jnp
from jax import lax
from jax.experimental import pallas as pl
from jax.experimental.pallas import tpu as pltpu

F32 = jnp.float32
BF16 = jnp.bfloat16
MESH = pl.DeviceIdType.MESH

N_DEV = 8
SEQ = 2048
D_MODEL = 1024
DEPTH = 2
N_PAIR = 4
HEAD_DIM = 64
ATTN_W = 512
ATTN_BLK = 128
DILATIONS = (1, 4, 16)
SEGMENT_BLOCKS = (16, 4, 1)
N_BLK = SEQ // ATTN_BLK
NDH = 4
CH = 64
NCH = SEQ // CH
IN_COLS = 3592
IN_PAD = 3840
QKV_W = 3 * ATTN_W
DN_QKV_BLK0 = QKV_W // 128
DN_QKV_BLKS = 1536 // 128
DN_Z_COL = 3072
DN_TAIL_BLK = 3584 // 128
D_FF = 2816
FF_BLKS = D_FF // 128
EPS = 1e-6
NEG = -1e30
ROPE_THETA = 10000.0

ADAM_LR, ADAM_B1, ADAM_B2, ADAM_EPS, ADAM_WD, ADAM_STEP = 0.001, 0.9, 0.999, 1e-08, 0.01, 10

VMEM_LIMIT = 56 * 1024 * 1024


def _cp(*sem):
    return pltpu.CompilerParams(dimension_semantics=sem, vmem_limit_bytes=VMEM_LIMIT)


class Exchange:
    def __init__(self, operands, out_shapes, sems, start, middle, finish):
        self.operands, self.out_shapes, self.sems = list(operands), list(out_shapes), list(sems)
        self.start, self.middle, self.finish = start, middle, finish


HBM_SPEC = pl.BlockSpec(memory_space=pltpu.HBM)


def _hosted_call(body, *, name, steps, in_specs, out_specs, out_shape, scratch_shapes, operands, exchanges=(),
                 aliases=None):
    n_in, n_out, n_scr = len(in_specs), len(out_specs), len(scratch_shapes)

    def take(refs, pos, counts):
        groups = []
        for c in counts:
            groups.append(refs[pos:pos + c])
            pos += c
        return groups, pos

    def full_body(*refs):
        ins, pos = refs[:n_in], n_in
        ex_ins, pos = take(refs, pos, [len(e.operands) for e in exchanges])
        outs, pos = refs[pos:pos + n_out], pos + n_out
        ex_outs, pos = take(refs, pos, [len(e.out_shapes) for e in exchanges])
        scr, pos = refs[pos:pos + n_scr], pos + n_scr
        ex_sems, pos = take(refs, pos, [len(e.sems) for e in exchanges])
        step = pl.program_id(0)
        for e, a, b, s in zip(exchanges, ex_ins, ex_outs, ex_sems):
            pl.when(step == 0)(functools.partial(e.start, a, b, s))
            if e.middle is not None:
                pl.when(step == (3 * steps) // 4)(functools.partial(e.middle, a, b, s))
        body(*ins, *outs, *scr)
        for e, a, b, s in zip(exchanges, ex_ins, ex_outs, ex_sems):
            pl.when(step == steps - 1)(functools.partial(e.finish, a, b, s))

    n_ex_in = sum(len(e.operands) for e in exchanges)
    n_ex_out = sum(len(e.out_shapes) for e in exchanges)
    results = pl.pallas_call(
        full_body, name=name, grid=(steps,),
        in_specs=list(in_specs) + [HBM_SPEC] * n_ex_in,
        out_specs=list(out_specs) + [HBM_SPEC] * n_ex_out,
        out_shape=list(out_shape) + [s for e in exchanges for s in e.out_shapes],
        scratch_shapes=list(scratch_shapes) + [s for e in exchanges for s in e.sems],
        input_output_aliases=aliases or {},
        compiler_params=_cp("arbitrary"),
    )(*operands, *[a for e in exchanges for a in e.operands])
    ex_results, _ = take(results, n_out, [len(e.out_shapes) for e in exchanges])
    return results[:n_out], ex_results


def _dot(a, b, dims, precision=None):
    if precision is None:
        a = a.astype(BF16)
        b = b.astype(BF16)
    return lax.dot_general(a, b, (dims, ((), ())), preferred_element_type=F32, precision=precision)


def _make_mm(precision):
    @jax.custom_vjp
    def nn(a, b):
        return _dot(a, b, ((1,), (0,)), precision)

    @jax.custom_vjp
    def nt(a, b):
        return _dot(a, b, ((1,), (1,)), precision)

    @jax.custom_vjp
    def tn(a, b):
        return _dot(a, b, ((0,), (0,)), precision)

    nn.defvjp(lambda a, b: (nn(a, b), (a, b)), lambda r, g: (nt(g, r[1]), tn(r[0], g)))
    nt.defvjp(lambda a, b: (nt(a, b), (a, b)), lambda r, g: (nn(g, r[1]), tn(g, r[0])))
    tn.defvjp(lambda a, b: (tn(a, b), (a, b)), lambda r, g: (nt(r[1], g), nn(r[0], g)))
    return nn, nt, tn


def _matmul(a, b, *, ta=False, tb=False, tm, tn, tk, name, out_dtype=F32):
    (k_dim, m_dim) = a.shape if ta else a.shape[::-1]
    (n_dim, k2) = b.shape if tb else b.shape[::-1]
    assert k_dim == k2 and m_dim % tm == 0 and n_dim % tn == 0 and k_dim % tk == 0, (a.shape, b.shape, tm, tn, tk)
    nk = k_dim // tk
    dims = ((0 if ta else 1,), (1 if tb else 0,))

    def body(a_ref, b_ref, o_ref, *acc):
        p = _dot(a_ref[...], b_ref[...], dims)
        if nk == 1:
            o_ref[...] = p.astype(out_dtype)
            return
        acc_ref, k = acc[0], pl.program_id(2)

        @pl.when(k == 0)
        def _():
            acc_ref[...] = p

        @pl.when(k > 0)
        def _():
            acc_ref[...] += p

        @pl.when(k == nk - 1)
        def _():
            o_ref[...] = acc_ref[...].astype(out_dtype)

    a_spec = pl.BlockSpec((tk, tm), lambda i, j, k: (k, i)) if ta else pl.BlockSpec((tm, tk), lambda i, j, k: (i, k))
    b_spec = pl.BlockSpec((tn, tk), lambda i, j, k: (j, k)) if tb else pl.BlockSpec((tk, tn), lambda i, j, k: (k, j))
    return pl.pallas_call(
        body, name=name,
        grid=(m_dim // tm, n_dim // tn, nk),
        in_specs=[a_spec, b_spec],
        out_specs=pl.BlockSpec((tm, tn), lambda i, j, k: (i, j)),
        out_shape=jax.ShapeDtypeStruct((m_dim, n_dim), out_dtype),
        scratch_shapes=[pltpu.VMEM((tm, tn), F32)] if nk > 1 else [],
        compiler_params=_cp("parallel", "parallel", "arbitrary"),
    )(a, b)


NORM_ROWS = 256


def _rms(x, w):
    return x * lax.rsqrt(jnp.mean(x * x, axis=1, keepdims=True) + EPS) * w


def _norm_fwd(x, w_row, name, out_dtype=BF16):
    def body(x_ref, w_ref, o_ref):
        o_ref[...] = _rms(x_ref[...], w_ref[...]).astype(out_dtype)

    return pl.pallas_call(
        body, name=name, grid=(SEQ // NORM_ROWS,),
        in_specs=[pl.BlockSpec((NORM_ROWS, D_MODEL), lambda i: (i, 0)), pl.BlockSpec((1, D_MODEL), lambda i: (0, 0))],
        out_specs=pl.BlockSpec((NORM_ROWS, D_MODEL), lambda i: (i, 0)),
        out_shape=jax.ShapeDtypeStruct((SEQ, D_MODEL), out_dtype),
        compiler_params=_cp("parallel"),
    )(x, w_row)


def _resnorm_fwd(x, f, w_row, name):
    def body(x_ref, f_ref, w_ref, o_ref):
        o_ref[...] = x_ref[...] + _rms(f_ref[...], w_ref[...])

    blk = pl.BlockSpec((NORM_ROWS, D_MODEL), lambda i: (i, 0))
    return pl.pallas_call(
        body, name=name, grid=(SEQ // NORM_ROWS,),
        in_specs=[blk, blk, pl.BlockSpec((1, D_MODEL), lambda i: (0, 0))],
        out_specs=blk, out_shape=jax.ShapeDtypeStruct((SEQ, D_MODEL), F32),
        compiler_params=_cp("parallel"),
    )(x, f, w_row)


def _norm_bwd(x, w_row, dy, add, name, dx_dtype=F32):
    has_add = add is not None

    def body(*refs):
        if has_add:
            x_ref, w_ref, dy_ref, add_ref, dx_ref, dw_ref = refs
        else:
            x_ref, w_ref, dy_ref, dx_ref, dw_ref = refs
        _, vjp = jax.vjp(_rms, x_ref[...], w_ref[...])
        dx, dw = vjp(dy_ref[...])
        dx_ref[...] = (dx + add_ref[...] if has_add else dx).astype(dx_dtype)

        @pl.when(pl.program_id(0) == 0)
        def _():
            dw_ref[...] = jnp.zeros_like(dw_ref)

        dw_ref[...] += dw

    blk = pl.BlockSpec((NORM_ROWS, D_MODEL), lambda i: (i, 0))
    row = pl.BlockSpec((1, D_MODEL), lambda i: (0, 0))
    ins = [x, w_row, dy] + ([add] if has_add else [])
    return pl.pallas_call(
        body, name=name, grid=(SEQ // NORM_ROWS,),
        in_specs=[blk, row, blk] + ([blk] if has_add else []),
        out_specs=[blk, row],
        out_shape=[jax.ShapeDtypeStruct((SEQ, D_MODEL), dx_dtype), jax.ShapeDtypeStruct((1, D_MODEL), F32)],
        compiler_params=_cp("arbitrary"),
    )(*ins)


def _loss_fwd_bwd(y, target):
    def body(y_ref, t_ref, loss_ref, dy_ref):
        err = y_ref[...] - t_ref[...]
        dy_ref[...] = err * (1.0 / D_MODEL)

        @pl.when(pl.program_id(0) == 0)
        def _():
            loss_ref[...] = jnp.zeros_like(loss_ref)

        part = jnp.sum(jnp.sum(err * err, axis=1, keepdims=True) * (1.0 / D_MODEL), axis=0, keepdims=True)
        loss_ref[...] += 0.5 * jnp.broadcast_to(part, loss_ref.shape)

    blk = pl.BlockSpec((NORM_ROWS, D_MODEL), lambda i: (i, 0))
    return pl.pallas_call(
        body, name="loss", grid=(SEQ // NORM_ROWS,),
        in_specs=[blk, blk],
        out_specs=[pl.BlockSpec((1, 128), lambda i: (0, 0)), blk],
        out_shape=[jax.ShapeDtypeStruct((1, 128), F32), jax.ShapeDtypeStruct((SEQ, D_MODEL), F32)],
        compiler_params=_cp("arbitrary"),
    )(y, target)


def _make_shift(j):
    def down(x):
        row = lax.broadcasted_iota(jnp.int32, x.shape, 0)
        return jnp.where(row >= j, pltpu.roll(x, j, 0), 0.0)

    def up(x):
        n = x.shape[0]
        row = lax.broadcasted_iota(jnp.int32, x.shape, 0)
        return jnp.where(row < n - j, pltpu.roll(x, n - j, 0), 0.0)

    f = jax.custom_vjp(down)
    f.defvjp(lambda x: (down(x), None), lambda _, g: (up(g),))
    return f


_SHIFT = {j: _make_shift(j) for j in (1, 2, 3)}


def _causal_conv(x, taps):
    n = len(taps)
    acc = x * taps[n - 1]
    for k in range(n - 1):
        acc = acc + _SHIFT[n - 1 - k](x) * taps[k]
    return acc


def _tap_rows(w_ref, lanes=slice(None)):
    return tuple(w_ref[k:k + 1, lanes] for k in range(w_ref.shape[0]))


def _sigmoid(x):
    return 1.0 / (1.0 + jnp.exp(-x))


def _silu(x):
    return x * _sigmoid(x)


def _softplus(x):
    return jnp.maximum(x, 0.0) + jnp.log(1.0 + jnp.exp(-jnp.abs(x)))


def _gelu_tanh(x):
    return 0.5 * x * (1.0 + jnp.tanh(math.sqrt(2.0 / math.pi) * (x + 0.044715 * (x * x * x))))


def _dnconv_fn(x, taps):
    return _silu(_causal_conv(x, taps))


def _dnconv_fwd(proj, conv_w):
    def body(x_ref, w_ref, o_ref):
        o_ref[...] = _dnconv_fn(x_ref[...], _tap_rows(w_ref)).astype(BF16)

    return pl.pallas_call(
        body, name="dnconv_fwd", grid=(DN_QKV_BLKS,),
        in_specs=[pl.BlockSpec((SEQ, 128), lambda j: (0, DN_QKV_BLK0 + j)), pl.BlockSpec((4, 128), lambda j: (0, j))],
        out_specs=pl.BlockSpec((SEQ, 128), lambda j: (0, j)),
        out_shape=jax.ShapeDtypeStruct((SEQ, 1536), BF16),
        compiler_params=_cp("parallel"),
    )(proj, conv_w)


def _dnconv_bwd(proj, conv_w, dc, dproj):
    def body(x_ref, w_ref, dc_ref, _, dx_ref, dw_ref):
        _, vjp = jax.vjp(_dnconv_fn, x_ref[...], _tap_rows(w_ref))
        dx, dw = vjp(dc_ref[...])
        dx_ref[...] = dx.astype(BF16)
        for k, row in enumerate(dw):
            dw_ref[k:k + 1, :] = row

    return pl.pallas_call(
        body, name="dnconv_bwd", grid=(DN_QKV_BLKS,),
        in_specs=[pl.BlockSpec((SEQ, 128), lambda j: (0, DN_QKV_BLK0 + j)), pl.BlockSpec((4, 128), lambda j: (0, j)),
                  pl.BlockSpec((SEQ, 128), lambda j: (0, j)), pl.BlockSpec(memory_space=pl.ANY)],
        out_specs=[pl.BlockSpec((SEQ, 128), lambda j: (0, DN_QKV_BLK0 + j)), pl.BlockSpec((4, 128), lambda j: (0, j))],
        out_shape=[jax.ShapeDtypeStruct((SEQ, IN_PAD), BF16), jax.ShapeDtypeStruct((4, 1536), F32)],
        input_output_aliases={3: 0},
        compiler_params=_cp("parallel"),
    )(proj, conv_w, dc, dproj)


def _ffact_fn(pg, pu, wg, wu, bg, bu):
    return _gelu_tanh(_causal_conv(pg, wg) + bg) * (_causal_conv(pu, wu) + bu)


def _ffact_args(p_ref, w_ref, b_ref):
    g, u = slice(0, 128), slice(128, 256)
    return (p_ref[:, g].astype(F32), p_ref[:, u].astype(F32), _tap_rows(w_ref, g), _tap_rows(w_ref, u),
            b_ref[:, g], b_ref[:, u])


def _ffact_fwd(pre, conv_w, conv_b, exchanges=()):
    def body(p_ref, w_ref, b_ref, o_ref):
        o_ref[...] = _ffact_fn(*_ffact_args(p_ref, w_ref, b_ref)).astype(BF16)

    (act,), results = _hosted_call(
        body, name="ffact_fwd", steps=FF_BLKS,
        in_specs=[pl.BlockSpec((SEQ, 256), lambda j: (0, j)), pl.BlockSpec((3, 256), lambda j: (0, j)),
                  pl.BlockSpec((1, 256), lambda j: (0, j))],
        out_specs=[pl.BlockSpec((SEQ, 128), lambda j: (0, j))],
        out_shape=[jax.ShapeDtypeStruct((SEQ, D_FF), BF16)],
        scratch_shapes=[], operands=(pre, conv_w, conv_b), exchanges=exchanges)
    return act, results


def _ffact_bwd(pre, conv_w, conv_b, dact, exchanges=()):
    def body(p_ref, w_ref, b_ref, da_ref, dp_ref, dw_ref, db_ref):
        _, vjp = jax.vjp(_ffact_fn, *_ffact_args(p_ref, w_ref, b_ref))
        dpg, dpu, dwg, dwu, dbg, dbu = vjp(da_ref[...].astype(F32))
        dp_ref[:, 0:128] = dpg.astype(BF16)
        dp_ref[:, 128:256] = dpu.astype(BF16)
        for k in range(3):
            dw_ref[k:k + 1, 0:128] = dwg[k]
            dw_ref[k:k + 1, 128:256] = dwu[k]
        db_ref[:, 0:128] = dbg
        db_ref[:, 128:256] = dbu

    return _hosted_call(
        body, name="ffact_bwd", steps=FF_BLKS,
        in_specs=[pl.BlockSpec((SEQ, 256), lambda j: (0, j)), pl.BlockSpec((3, 256), lambda j: (0, j)),
                  pl.BlockSpec((1, 256), lambda j: (0, j)), pl.BlockSpec((SEQ, 128), lambda j: (0, j))],
        out_specs=[pl.BlockSpec((SEQ, 256), lambda j: (0, j)), pl.BlockSpec((3, 256), lambda j: (0, j)),
                   pl.BlockSpec((1, 256), lambda j: (0, j))],
        out_shape=[jax.ShapeDtypeStruct((SEQ, 2 * D_FF), BF16), jax.ShapeDtypeStruct((3, 2 * D_FF), F32),
                   jax.ShapeDtypeStruct((1, 2 * D_FF), F32)],
        scratch_shapes=[], operands=(pre, conv_w, conv_b, dact), exchanges=exchanges)


def _interleave_ff(t):
    lead = t.shape[:-1]
    return t.reshape(lead + (2, FF_BLKS, 128)).swapaxes(-3, -2).reshape(lead + (2 * D_FF,))


def _deinterleave_ff(t):
    lead = t.shape[:-1]
    return t.reshape(lead + (FF_BLKS, 2, 128)).swapaxes(-3, -2).reshape(lead + (2 * D_FF,))


def _rope_tables():
    inv = 1.0 / (ROPE_THETA ** (jnp.arange(0, HEAD_DIM, 2, dtype=F32) / HEAD_DIM))
    ang = jnp.arange(SEQ, dtype=F32)[:, None] * inv[None, :]
    cos = jnp.tile(jnp.cos(ang), (1, 4))
    sin = jnp.tile(jnp.sin(ang), (1, 4))
    sign = jnp.where((jnp.arange(128) % HEAD_DIM) < HEAD_DIM // 2, -1.0, 1.0).astype(F32)
    return cos, sin * sign[None, :]


def _rope(x, cos, sin_signed):
    lane = lax.broadcasted_iota(jnp.int32, x.shape, 1)
    partner = jnp.where((lane % HEAD_DIM) < HEAD_DIM // 2, pltpu.roll(x, 128 - HEAD_DIM // 2, 1),
                        pltpu.roll(x, HEAD_DIM // 2, 1))
    return x * cos + partner * sin_signed


def _head_masks():
    lane = lax.broadcasted_iota(jnp.int32, (1, 128), 1)
    return [(lane // HEAD_DIM) == h for h in range(2)]


def _both_heads(x):
    return jnp.concatenate([jnp.where(hm, x, 0.0)[None] for hm in _head_masks()], axis=0)


def _block_keys(branch, k_s, v_s, rows, prows, has_prev):
    a = lax.broadcasted_iota(jnp.int32, (ATTN_BLK, ATTN_BLK), 0)
    c = lax.broadcasted_iota(jnp.int32, (ATTN_BLK, ATTN_BLK), 1)
    keys, values, mask = k_s[rows, :], v_s[rows, :], c <= a
    if SEGMENT_BLOCKS[branch] > 1:
        keys = jnp.concatenate([k_s[prows, :], keys], axis=0)
        values = jnp.concatenate([v_s[prows, :], values], axis=0)
        mask = jnp.concatenate([(c >= a) & has_prev, mask], axis=1)
    twice = lambda t: jnp.broadcast_to(t[None], (2,) + t.shape)
    return twice(keys), twice(values), mask


def _block_rows(branch, t):
    d, per_seg = DILATIONS[branch], SEGMENT_BLOCKS[branch]
    if d == 1:
        start = pl.multiple_of(t * ATTN_BLK, ATTN_BLK)
        prev = pl.multiple_of(jnp.maximum(t - 1, 0) * ATTN_BLK, ATTN_BLK)
        return pl.ds(start, ATTN_BLK), pl.ds(prev, ATTN_BLK), t > 0
    r, n = t // per_seg, t % per_seg
    start = n * (ATTN_BLK * d) + r
    prev = jnp.maximum(n - 1, 0) * (ATTN_BLK * d) + r
    return pl.ds(start, ATTN_BLK, stride=d), pl.ds(prev, ATTN_BLK, stride=d), n > 0


def _attn_fwd(proj, cos, sin_signed, exchanges=()):
    scale = HEAD_DIM ** -0.5

    def body(qkv_ref, cos_ref, sin_ref, out_ref, lse_ref, q_s, k_s, v_s, *branch_s):
        o_s, l_s = branch_s[:3], branch_s[3:]
        q_s[...] = _rope(qkv_ref[:, 0:128], cos_ref[...], sin_ref[...])
        k_s[...] = _rope(qkv_ref[:, 128:256], cos_ref[...], sin_ref[...])
        v_s[...] = qkv_ref[:, 256:384]
        heads = _head_masks()
        for branch in range(3):
            def block(t, carry, branch=branch):
                rows, prows, has_prev = _block_rows(branch, t)
                keys, values, mask = _block_keys(branch, k_s, v_s, rows, prows, has_prev)
                s = jnp.where(mask, BMM_NT(_both_heads(q_s[rows, :]), keys) * scale, NEG)
                m = jnp.max(s, axis=2, keepdims=True)
                e = jnp.exp(s - m)
                l = jnp.sum(e, axis=2, keepdims=True)
                o = BMM(e, values) / l
                lse_b = m + jnp.log(l)
                o_s[branch][rows, :] = jnp.where(heads[0], o[0], o[1])
                l_s[branch][rows, :] = jnp.where(heads[0], lse_b[0], lse_b[1])
                return carry

            lax.fori_loop(0, N_BLK, block, 0, unroll=4)
        l0, l1, l2 = l_s[0][...], l_s[1][...], l_s[2][...]
        m = jnp.maximum(jnp.maximum(l0, l1), l2)
        w0, w1, w2 = jnp.exp(l0 - m), jnp.exp(l1 - m), jnp.exp(l2 - m)
        den = w0 + w1 + w2
        out_ref[...] = (w0 * o_s[0][...] + w1 * o_s[1][...] + w2 * o_s[2][...]) / den
        lse_ref[...] = m + jnp.log(den)

    tab = pl.BlockSpec((SEQ, 128), lambda j: (0, 0))
    col = pl.BlockSpec((SEQ, 128), lambda j: (0, j))
    return _hosted_call(
        body, name="attn_fwd", steps=N_PAIR,
        in_specs=[pl.BlockSpec((SEQ, 384), lambda j: (0, j)), tab, tab],
        out_specs=[col, col],
        out_shape=[jax.ShapeDtypeStruct((SEQ, 2 * ATTN_W), F32), jax.ShapeDtypeStruct((SEQ, ATTN_W), F32)],
        scratch_shapes=[pltpu.VMEM((SEQ, 128), F32)] * 9,
        operands=(proj, cos, sin_signed), exchanges=exchanges)


def _attn_bwd(proj, cos, sin_signed, cat, lse, dcat, dproj, exchanges=()):
    scale = HEAD_DIM ** -0.5

    def body(qkv_ref, cos_ref, sin_ref, out_ref, lse_ref, do_ref, _, dqkv_ref, q_s, k_s, v_s, dq_s, dk_s, dv_s,
             dod_s):
        q_s[...] = _rope(qkv_ref[:, 0:128], cos_ref[...], sin_ref[...])
        k_s[...] = _rope(qkv_ref[:, 128:256], cos_ref[...], sin_ref[...])
        v_s[...] = qkv_ref[:, 256:384]
        dq_s[...] = jnp.zeros_like(dq_s)
        dk_s[...] = jnp.zeros_like(dk_s)
        dv_s[...] = jnp.zeros_like(dv_s)
        dod_s[...] = do_ref[...] * out_ref[...]
        heads = _head_masks()
        for branch in range(3):
            def block(t, carry, branch=branch):
                rows, prows, has_prev = _block_rows(branch, t)
                keys, values, mask = _block_keys(branch, k_s, v_s, rows, prows, has_prev)
                q2, do2 = _both_heads(q_s[rows, :]), _both_heads(do_ref[rows, :])
                lse_b, dod = lse_ref[rows, :], dod_s[rows, :]
                lse2 = jnp.concatenate(
                    [jnp.max(jnp.where(hm, lse_b, NEG), axis=1, keepdims=True)[None] for hm in heads], axis=0)
                delta = jnp.concatenate(
                    [jnp.sum(jnp.where(hm, dod, 0.0), axis=1, keepdims=True)[None] for hm in heads], axis=0)
                p = jnp.exp(jnp.where(mask, BMM_NT(q2, keys) * scale, NEG) - lse2)
                ds = p * (BMM_NT(do2, values) - delta) * scale
                dq = BMM(ds, keys)
                dk = BMM_TN(ds, q2)
                dv = BMM_TN(p, do2)
                dk, dv = dk[0] + dk[1], dv[0] + dv[1]
                dq_s[rows, :] += jnp.where(heads[0], dq[0], dq[1])
                if SEGMENT_BLOCKS[branch] > 1:
                    dk_s[rows, :] += dk[ATTN_BLK:]
                    dv_s[rows, :] += dv[ATTN_BLK:]

                    @pl.when(has_prev)
                    def _():
                        dk_s[prows, :] += dk[:ATTN_BLK]
                        dv_s[prows, :] += dv[:ATTN_BLK]
                else:
                    dk_s[rows, :] += dk
                    dv_s[rows, :] += dv
                return carry

            lax.fori_loop(0, N_BLK, block, 0, unroll=4)
        dqkv_ref[:, 0:128] = _rope(dq_s[...], cos_ref[...], -sin_ref[...]).astype(BF16)
        dqkv_ref[:, 128:256] = _rope(dk_s[...], cos_ref[...], -sin_ref[...]).astype(BF16)
        dqkv_ref[:, 256:384] = dv_s[...].astype(BF16)

    tab = pl.BlockSpec((SEQ, 128), lambda j: (0, 0))
    col = pl.BlockSpec((SEQ, 128), lambda j: (0, j))
    qkv = pl.BlockSpec((SEQ, 384), lambda j: (0, j))
    (dproj,), results = _hosted_call(
        body, name="attn_bwd", steps=N_PAIR,
        in_specs=[qkv, tab, tab, col, col, col, pl.BlockSpec(memory_space=pl.ANY)],
        out_specs=[qkv],
        out_shape=[jax.ShapeDtypeStruct((SEQ, IN_PAD), BF16)],
        scratch_shapes=[pltpu.VMEM((SEQ, 128), F32)] * 7,
        operands=(proj, cos, sin_signed, cat, lse, dcat, dproj), exchanges=exchanges, aliases={6: 0})
    return dproj, results


def _bdot(a, b, dims, precision=None):
    if precision is None:
        a = a.astype(BF16)
        b = b.astype(BF16)
    return lax.dot_general(a, b, (dims, ((0,), (0,))), preferred_element_type=F32, precision=precision)


def _make_bmm(precision):
    @jax.custom_vjp
    def nn(a, b):
        return _bdot(a, b, ((2,), (1,)), precision)

    @jax.custom_vjp
    def nt(a, b):
        return _bdot(a, b, ((2,), (2,)), precision)

    @jax.custom_vjp
    def tn(a, b):
        return _bdot(a, b, ((1,), (1,)), precision)

    nn.defvjp(lambda a, b: (nn(a, b), (a, b)), lambda r, g: (nt(g, r[1]), tn(r[0], g)))
    nt.defvjp(lambda a, b: (nt(a, b), (a, b)), lambda r, g: (nn(g, r[1]), tn(g, r[0])))
    tn.defvjp(lambda a, b: (tn(a, b), (a, b)), lambda r, g: (nt(r[1], g), nn(r[0], g)))
    return nn, nt, tn


BMM, BMM_NT, BMM_TN = _make_bmm(None)
BMM3, BMM3_NT, BMM3_TN = _make_bmm(lax.Precision.HIGH)
MM3, _, _ = _make_mm(lax.Precision.HIGH)


def _head_lanes(t, off):
    lane = lax.broadcasted_iota(jnp.int32, (1, 128), 1)
    return jnp.concatenate(
        [jnp.sum(t * (lane == off + h).astype(F32), axis=1, keepdims=True)[None] for h in range(NDH)], axis=0)


@jax.custom_vjp
def _unit_lower_inverse(a_mat):
    c = a_mat.shape[1]
    eye = (lax.broadcasted_iota(jnp.int32, (c, c), 0) == lax.broadcasted_iota(jnp.int32, (c, c), 1)).astype(F32)
    power = -a_mat
    t_inv = eye + power
    for _ in range(5):
        power = BMM3(power, power)
        t_inv = t_inv + BMM3(t_inv, power)
    return t_inv


def _unit_lower_inverse_fwd(a_mat):
    t_inv = _unit_lower_inverse(a_mat)
    return t_inv, t_inv


def _unit_lower_inverse_bwd(t_inv, d_inv):
    return (-BMM3_NT(BMM3_TN(t_inv, d_inv), t_inv),)


_unit_lower_inverse.defvjp(_unit_lower_inverse_fwd, _unit_lower_inverse_bwd)


DN_STEP_CHUNKS = 4
DN_STEP_ROWS = DN_STEP_CHUNKS * CH
DN_STEPS = NCH // DN_STEP_CHUNKS
DN_BATCH = DN_STEP_CHUNKS * NDH


def _delta_chunks(qr, kr, vr, z, tail, alog_row, dt_row, nw, state):
    c = qr.shape[1]
    tails = [tail[CH * n:CH * (n + 1)] for n in range(DN_STEP_CHUNKS)]
    per_chunk = lambda t: jnp.concatenate([t] * DN_STEP_CHUNKS, axis=0)
    beta = _sigmoid(jnp.concatenate([_head_lanes(t, 0) for t in tails], axis=0))
    a_raw = jnp.concatenate([_head_lanes(t, NDH) for t in tails], axis=0)
    g = -jnp.exp(per_chunk(_head_lanes(alog_row, 0))) * _softplus(a_raw + per_chunk(_head_lanes(dt_row, 0)))

    q = qr * lax.rsqrt(jnp.sum(qr * qr, axis=2, keepdims=True) + EPS) * (128 ** -0.5)
    k = kr * lax.rsqrt(jnp.sum(kr * kr, axis=2, keepdims=True) + EPS)

    ri = lax.broadcasted_iota(jnp.int32, (c, c), 0)
    ci = lax.broadcasted_iota(jnp.int32, (c, c), 1)
    tril = ri >= ci
    lane = lax.broadcasted_iota(jnp.int32, (1, 128), 1)
    pick = [(lane == b).astype(F32) for b in range(DN_BATCH)]
    g_lanes = sum(g[b] * pick[b] for b in range(DN_BATCH))
    g_sums = MM3(tril.astype(F32), g_lanes)
    gc = jnp.concatenate([jnp.sum(g_sums * pick[b], axis=1, keepdims=True)[None] for b in range(DN_BATCH)],
                         axis=0)
    g_row = jnp.swapaxes(jnp.broadcast_to(gc, (DN_BATCH, c, c)), 1, 2)
    decay = jnp.where(tril, jnp.exp(jnp.where(tril, gc - g_row, 0.0)), 0.0)
    kb = k * beta
    t_inv = _unit_lower_inverse(jnp.where(ri > ci, BMM_NT(kb, k) * decay, 0.0))
    eg = jnp.exp(gc)
    u = BMM(t_inv, vr * beta)
    w = BMM(t_inv, kb * eg)
    qk = BMM_NT(q, k) * decay
    g_tot = jnp.sum(g, axis=1, keepdims=True)
    q_dec = q * eg
    k_dec = k * jnp.exp(g_tot - gc)
    outs = []
    for n in range(DN_STEP_CHUNKS):
        heads = slice(NDH * n, NDH * (n + 1))
        v_new = u[heads] - BMM(w[heads], state)
        outs.append(BMM(q_dec[heads], state) + BMM(qk[heads], v_new))
        state = state * jnp.exp(g_tot[heads]) + BMM_TN(k_dec[heads], v_new)
    o = jnp.concatenate(outs, axis=0)
    on = o * lax.rsqrt(jnp.mean(o * o, axis=2, keepdims=True) + EPS) * nw
    return on * _silu(z), state


def _heads(v, off=0):
    return jnp.concatenate([v[None, CH * n:CH * (n + 1), off + 128 * h:off + 128 * (h + 1)]
                            for n in range(DN_STEP_CHUNKS) for h in range(NDH)], axis=0)


def _unheads(t):
    return jnp.concatenate([jnp.concatenate([t[NDH * n + h] for h in range(NDH)], axis=1)
                            for n in range(DN_STEP_CHUNKS)], axis=0)


def _delta_fwd(c_qkv, proj, alog_row, dt_row, nw, cat, exchanges=()):
    def body(c_ref, z_ref, tail_ref, al_ref, dt_ref, nw_ref, _, y_ref, st_ref, state):
        @pl.when(pl.program_id(0) == 0)
        def _():
            state[...] = jnp.zeros_like(state)

        cv = c_ref[...].astype(F32)
        st_ref[0] = state[...]
        y, new_state = _delta_chunks(_heads(cv), _heads(cv, 512), _heads(cv, 1024), _heads(z_ref[...]), tail_ref[...],
                                     al_ref[...], dt_ref[...], nw_ref[...], state[...])
        y_ref[...] = _unheads(y)
        state[...] = new_state

    row = pl.BlockSpec((1, 128), lambda n: (0, 0))
    rows = DN_STEP_ROWS
    return _hosted_call(
        body, name="delta_fwd", steps=DN_STEPS,
        in_specs=[pl.BlockSpec((rows, 1536), lambda n: (n, 0)), pl.BlockSpec((rows, 512), lambda n: (n, DN_Z_COL // 512)),
                  pl.BlockSpec((rows, 128), lambda n: (n, DN_TAIL_BLK)), row, row, row, pl.BlockSpec(memory_space=pl.ANY)],
        out_specs=[pl.BlockSpec((rows, 512), lambda n: (n, 1)),
                   pl.BlockSpec((1, NDH, 128, 128), lambda n: (n, 0, 0, 0))],
        out_shape=[jax.ShapeDtypeStruct((SEQ, 2 * ATTN_W), F32), jax.ShapeDtypeStruct((DN_STEPS, NDH, 128, 128), F32)],
        scratch_shapes=[pltpu.VMEM((NDH, 128, 128), F32)],
        operands=(c_qkv, proj, proj, alog_row, dt_row, nw, cat), exchanges=exchanges, aliases={6: 0})


def _delta_bwd(c_qkv, proj, alog_row, dt_row, nw, states, dcat, exchanges=()):
    def body(c_ref, z_ref, tail_ref, al_ref, dt_ref, nw_ref, st_ref, dy_ref,
             dp_ref, dc_ref, dal_ref, ddt_ref, dnw_ref, dstate):
        @pl.when(pl.program_id(0) == 0)
        def _():
            dstate[...] = jnp.zeros_like(dstate)
            dal_ref[...] = jnp.zeros_like(dal_ref)
            ddt_ref[...] = jnp.zeros_like(ddt_ref)
            dnw_ref[...] = jnp.zeros_like(dnw_ref)

        cv = c_ref[...].astype(F32)
        _, vjp = jax.vjp(_delta_chunks, _heads(cv), _heads(cv, 512), _heads(cv, 1024), _heads(z_ref[...]),
                         tail_ref[...], al_ref[...], dt_ref[...], nw_ref[...], st_ref[0])
        dq, dk, dv, dz, dtail, dal, ddt, dnw, dst = vjp((_heads(dy_ref[...]), dstate[...]))
        dstate[...] = dst
        dc_ref[...] = jnp.concatenate([_unheads(dq), _unheads(dk), _unheads(dv)], axis=1)
        dp_ref[...] = jnp.concatenate([_unheads(dz), dtail, jnp.zeros((DN_STEP_ROWS, 128), F32)], axis=1).astype(BF16)
        dal_ref[...] += dal
        ddt_ref[...] += ddt
        dnw_ref[...] += dnw

    rev = lambda n: DN_STEPS - 1 - n
    row = pl.BlockSpec((1, 128), lambda n: (0, 0))
    rows = DN_STEP_ROWS
    return _hosted_call(
        body, name="delta_bwd", steps=DN_STEPS,
        in_specs=[pl.BlockSpec((rows, 1536), lambda n: (rev(n), 0)),
                  pl.BlockSpec((rows, 512), lambda n: (rev(n), DN_Z_COL // 512)),
                  pl.BlockSpec((rows, 128), lambda n: (rev(n), DN_TAIL_BLK)), row, row, row,
                  pl.BlockSpec((1, NDH, 128, 128), lambda n: (rev(n), 0, 0, 0)),
                  pl.BlockSpec((rows, 512), lambda n: (rev(n), 1))],
        out_specs=[pl.BlockSpec((rows, 768), lambda n: (rev(n), DN_Z_COL // 768)),
                   pl.BlockSpec((rows, 1536), lambda n: (rev(n), 0)), row, row, row],
        out_shape=[jax.ShapeDtypeStruct((SEQ, IN_PAD), BF16), jax.ShapeDtypeStruct((SEQ, 1536), F32)]
        + [jax.ShapeDtypeStruct((1, 128), F32)] * 3,
        scratch_shapes=[pltpu.VMEM((NDH, 128, 128), F32)],
        operands=(c_qkv, proj, proj, alog_row, dt_row, nw, states, dcat), exchanges=exchanges)


def _place():
    x, y, c = lax.axis_index("x"), lax.axis_index("y"), lax.axis_index("c")
    other_chips = [(1 - x, y), (x, 1 - y), (1 - x, 1 - y)]
    return x, y, c, other_chips


def _gather_exchange(shards):
    n = len(shards)

    def copies(ins, outs, sems):
        send_sems, recv_sems, local_sems = sems
        x, y, c, chips = _place()
        me, sibling = (x, y, c), (x, y, 1 - c)

        def copy(b, k, block, to, src=None):
            slot = outs[b].at[4 * block[0] + 2 * block[1] + block[2]]
            return pltpu.make_async_remote_copy(
                src_ref=slot if src is None else src, dst_ref=slot,
                send_sem=send_sems.at[b, k], recv_sem=recv_sems.at[b, k], device_id=to, device_id_type=MESH)

        mine = [pltpu.make_async_copy(ins[b], outs[b].at[4 * x + 2 * y + c], local_sems.at[b]) for b in range(n)]
        first = []
        for b in range(n):
            first.append(copy(b, 0, me, sibling, src=ins[b]))
            first += [copy(b, 1 + j, me, (*chip, c), src=ins[b]) for j, chip in enumerate(chips)]
        over_ici = [copy(b, 1 + j, (*chip, c), me) for b in range(n) for j, chip in enumerate(chips)]
        passed = [copy(b, 4 + j, (*chip, c), sibling) for b in range(n) for j, chip in enumerate(chips)]
        from_sibling = []
        for b in range(n):
            from_sibling.append(copy(b, 0, sibling, me))
            from_sibling += [copy(b, 4 + j, (*chip, 1 - c), me) for j, chip in enumerate(chips)]
        return mine, first, over_ici, passed, from_sibling

    def start(ins, outs, sems):
        mine, first, _, _, _ = copies(ins, outs, sems)
        for cp in mine + first:
            cp.start()

    def middle(ins, outs, sems):
        _, _, over_ici, passed, _ = copies(ins, outs, sems)
        for arrived, onward in zip(over_ici, passed):
            arrived.wait_recv()
            onward.start()

    def finish(ins, outs, sems):
        mine, first, _, passed, from_sibling = copies(ins, outs, sems)
        for cp in from_sibling:
            cp.wait_recv()
        for cp in first + passed:
            cp.wait_send()
        for cp in mine:
            cp.wait()

    return Exchange(shards, [jax.ShapeDtypeStruct((N_DEV,) + s.shape, s.dtype) for s in shards],
                    [pltpu.SemaphoreType.DMA((n, 7)), pltpu.SemaphoreType.DMA((n, 7)), pltpu.SemaphoreType.DMA((n,))],
                    start, middle, finish)


def _sibling_exchange(gs):
    n = len(gs)

    def copies(ins, outs, sems):
        send_sems, recv_sems = sems
        x, y, c, _ = _place()
        return [pltpu.make_async_remote_copy(
            src_ref=ins[b].at[2 * p + (1 - c)], dst_ref=outs[b].at[p],
            send_sem=send_sems.at[b, p], recv_sem=recv_sems.at[b, p],
            device_id=(x, y, 1 - c), device_id_type=MESH) for b in range(n) for p in range(4)]

    def start(ins, outs, sems):
        for cp in copies(ins, outs, sems):
            cp.start()

    def finish(ins, outs, sems):
        for cp in copies(ins, outs, sems):
            cp.wait()

    return Exchange(gs, [jax.ShapeDtypeStruct((4,) + g.shape[1:], g.dtype) for g in gs],
                    [pltpu.SemaphoreType.DMA((n, 4)), pltpu.SemaphoreType.DMA((n, 4))], start, None, finish)


def _chips_exchange(hs):
    n = len(hs)

    def copies(ins, outs, sems):
        send_sems, recv_sems, local_sems = sems
        x, y, c, chips = _place()
        my_chip = 2 * x + y
        local = [pltpu.make_async_copy(ins[b].at[my_chip], outs[b].at[my_chip], local_sems.at[b]) for b in range(n)]
        sends, arrivals = [], []
        for b in range(n):
            for k, (px, py) in enumerate(chips):
                peer = 2 * px + py
                sends.append(pltpu.make_async_remote_copy(
                    src_ref=ins[b].at[peer], dst_ref=outs[b].at[my_chip],
                    send_sem=send_sems.at[b, k], recv_sem=recv_sems.at[b, k],
                    device_id=(px, py, c), device_id_type=MESH))
                arrivals.append(pltpu.make_async_remote_copy(
                    src_ref=ins[b].at[peer], dst_ref=outs[b].at[peer],
                    send_sem=send_sems.at[b, k], recv_sem=recv_sems.at[b, k],
                    device_id=(px, py, c), device_id_type=MESH))
        return local, sends, arrivals

    def start(ins, outs, sems):
        local, sends, _ = copies(ins, outs, sems)
        for cp in local + sends:
            cp.start()

    def finish(ins, outs, sems):
        local, sends, arrivals = copies(ins, outs, sems)
        for cp in arrivals:
            cp.wait_recv()
        for cp in sends:
            cp.wait_send()
        for cp in local:
            cp.wait()

    return Exchange(hs, [jax.ShapeDtypeStruct(h.shape, h.dtype) for h in hs],
                    [pltpu.SemaphoreType.DMA((n, 3)), pltpu.SemaphoreType.DMA((n, 3)), pltpu.SemaphoreType.DMA((n,))],
                    start, None, finish)


def _run_exchange(exchange, name):
    n_in, n_out = len(exchange.operands), len(exchange.out_shapes)

    def body(*refs):
        ins, outs, sems = refs[:n_in], refs[n_in:n_in + n_out], refs[n_in + n_out:]
        exchange.start(ins, outs, sems)
        if exchange.middle is not None:
            exchange.middle(ins, outs, sems)
        exchange.finish(ins, outs, sems)

    return pl.pallas_call(
        body, name=name,
        in_specs=[HBM_SPEC] * n_in, out_specs=[HBM_SPEC] * n_out,
        out_shape=exchange.out_shapes, scratch_shapes=exchange.sems,
    )(*exchange.operands)


def _pair_add(g, r, core, name):
    _, nr, nc = g.shape
    tr = nr // 2 if nr % 32 == 0 else nr

    def body(core_ref, g_ref, r_ref, o_ref):
        o_ref[...] = (g_ref[...].astype(F32) + r_ref[...].astype(F32)).astype(BF16)

    return pl.pallas_call(
        body, name=name,
        grid_spec=pltpu.PrefetchScalarGridSpec(
            num_scalar_prefetch=1, grid=(4, nr // tr),
            in_specs=[pl.BlockSpec((1, tr, nc), lambda p, i, core: (2 * p + core[0], i, 0)),
                      pl.BlockSpec((1, tr, nc), lambda p, i, core: (p, i, 0))],
            out_specs=pl.BlockSpec((1, tr, nc), lambda p, i, core: (p, i, 0))),
        out_shape=jax.ShapeDtypeStruct(r.shape, BF16),
        compiler_params=_cp("parallel", "parallel"),
    )(core, g, r)


def _all_gather_sum_small(v):
    rows = v.shape[0]

    def body(x_ref, sum_ref, out_ref, send_sems, recv_sems, local_sem):
        x, y, c, chips = _place()
        me, sibling = (x, y, c), (x, y, 1 - c)

        def block(px, py, pc):
            return out_ref.at[pl.ds((4 * px + 2 * py + pc) * rows, rows), :]

        def copy(k, blk, to, src=None):
            return pltpu.make_async_remote_copy(
                src_ref=block(*blk) if src is None else src, dst_ref=block(*blk),
                send_sem=send_sems.at[k], recv_sem=recv_sems.at[k], device_id=to, device_id_type=MESH)

        mine = pltpu.make_async_copy(x_ref, block(*me), local_sem)
        mine.start()
        first = [copy(0, me, sibling, src=x_ref)]
        first += [copy(1 + j, me, (*chip, c), src=x_ref) for j, chip in enumerate(chips)]
        for cp in first:
            cp.start()
        passed = [copy(4 + j, (*chip, c), sibling) for j, chip in enumerate(chips)]
        for j, chip in enumerate(chips):
            copy(1 + j, (*chip, c), me).wait_recv()
            passed[j].start()
        copy(0, sibling, me).wait_recv()
        for j, chip in enumerate(chips):
            copy(4 + j, (*chip, 1 - c), me).wait_recv()
        for cp in first + passed:
            cp.wait_send()
        mine.wait()
        total = out_ref[pl.ds(0, rows), :]
        for d in range(1, N_DEV):
            total = total + out_ref[pl.ds(d * rows, rows), :]
        sum_ref[...] = total

    vm = pl.BlockSpec(memory_space=pltpu.VMEM)
    return pl.pallas_call(
        body, name="small_all_reduce",
        in_specs=[vm], out_specs=[vm],
        out_shape=[jax.ShapeDtypeStruct((rows, 128), F32)],
        scratch_shapes=[pltpu.VMEM((N_DEV * rows, 128), F32), pltpu.SemaphoreType.DMA((7,)),
                        pltpu.SemaphoreType.DMA((7,)), pltpu.SemaphoreType.DMA],
    )(v)[0]


def _adamw(w, g, m, v):
    m = ADAM_B1 * m + (1.0 - ADAM_B1) * g
    v = ADAM_B2 * v + (1.0 - ADAM_B2) * (g * g)
    m_hat = m / (1.0 - ADAM_B1 ** ADAM_STEP)
    v_hat = v / (1.0 - ADAM_B2 ** ADAM_STEP)
    delta = -ADAM_LR * (m_hat / (jnp.sqrt(v_hat) + ADAM_EPS) + ADAM_WD * w)
    return delta, m, v


ADAM_TILE = dict(w_in=(IN_COLS // N_DEV, 256), w_out=(128, D_MODEL), ffn_w_in=(176, D_MODEL), ffn_w_out=(176, D_MODEL))


def _sum_chips(p):
    p = p.astype(F32)
    return (p[0] + p[1]) + (p[2] + p[3])


def _adamw_sharded(parts, w, m, v, tile, name):
    nl, nr, nc = w.shape
    tr, tc = tile

    def body(*refs):
        p_refs, (w_ref, m_ref, v_ref, g_ref, d_ref, nm_ref, nv_ref) = refs[:nl], refs[nl:]
        layer = pl.program_id(0)
        p = p_refs[0][...]
        for l in range(1, nl):
            p = jnp.where(layer == l, p_refs[l][...], p)
        g = _sum_chips(p)
        delta, nm, nv = _adamw(w_ref[0], g, m_ref[0], v_ref[0])
        g_ref[0] = g
        d_ref[0] = delta
        nm_ref[0] = nm
        nv_ref[0] = nv

    blk = pl.BlockSpec((1, tr, tc), lambda l, i, j: (l, i, j))
    return pl.pallas_call(
        body, name=name, grid=(nl, nr // tr, nc // tc),
        in_specs=[pl.BlockSpec((4, tr, tc), lambda l, i, j, own=own: (0, jnp.where(l == own, i, 0), j))
                  for own in range(nl)] + [blk, blk, blk],
        out_specs=[blk] * 4,
        out_shape=[jax.ShapeDtypeStruct(w.shape, F32)] * 4,
        compiler_params=_cp("parallel", "parallel", "parallel"),
    )(*parts, w, m, v)


def _adamw_small(g, w, m, v):
    def body(g_ref, w_ref, m_ref, v_ref, d_ref, nm_ref, nv_ref):
        delta, nm, nv = _adamw(w_ref[...], g_ref[...], m_ref[...], v_ref[...])
        d_ref[...] = delta
        nm_ref[...] = nm
        nv_ref[...] = nv

    return pl.pallas_call(
        body, name="adamw_small",
        out_shape=[jax.ShapeDtypeStruct(g.shape, F32)] * 3,
    )(g, w, m, v)


def _packed_rows(n):
    return -(-n // 1024) * 8


def _pack(arrays, rows):
    pieces = []
    for a in arrays:
        flat = a.reshape(-1).astype(F32)
        nr = _packed_rows(flat.shape[0])
        pieces.append(jnp.pad(flat, (0, nr * 128 - flat.shape[0])).reshape(nr, 128))
    used = sum(p.shape[0] for p in pieces)
    return jnp.concatenate(pieces + [jnp.zeros((rows - used, 128), F32)] * (rows > used), axis=0)


def _unpack(packed, shapes):
    out, row = [], 0
    for s in shapes:
        n = math.prod(s)
        out.append(packed[row:row + _packed_rows(n)].reshape(-1)[:n].reshape(s))
        row += _packed_rows(n)
    return out


def _row(v, width=None):
    v = v.reshape(1, -1)
    return v if width is None else jnp.pad(v, ((0, 0), (0, width - v.shape[1])))


def _layer_fwd(x, wts, tables, hosted):
    h = _norm_fwd(x, wts["norm_pre_mix"], "norm_pre_mix")
    proj = _matmul(h, wts["w_in"], tb=True, tm=SEQ, tn=768, tk=1024, name="mm_proj")
    (cat, lse), got = _attn_fwd(proj, *tables, exchanges=hosted["attn"][0])
    hosted["attn"][1](got)
    c_qkv = _dnconv_fwd(proj, wts["dn_conv_w"])
    (cat, states), got = _delta_fwd(c_qkv, proj, wts["dn_a_log"], wts["dn_dt_bias"], wts["dn_norm_w"], cat,
                                    exchanges=hosted["delta"][0])
    hosted["delta"][1](got)
    mix = _matmul(cat, wts["w_out"], tm=512, tn=1024, tk=1024, name="mm_mix")
    x1 = _resnorm_fwd(x, mix, wts["norm_post_mix"], "norm_post_mix")
    h2 = _norm_fwd(x1, wts["norm_pre_ffn"], "norm_pre_ffn")
    pre = _matmul(h2, wts["ffn_w_in"], tb=True, tm=SEQ, tn=512, tk=1024, name="mm_ffn_in", out_dtype=BF16)
    act, got = _ffact_fwd(pre, wts["ffn_conv_w"], wts["ffn_conv_b"], exchanges=hosted["ffact"][0])
    hosted["ffact"][1](got)
    f = _matmul(act, wts["ffn_w_out"], tm=512, tn=1024, tk=D_FF, name="mm_ffn_out")
    x2 = _resnorm_fwd(x1, f, wts["norm_post_ffn"], "norm_post_ffn")
    saved = dict(x=x, h=h, proj=proj, lse=lse, c_qkv=c_qkv, states=states, cat=cat, mix=mix, x1=x1, h2=h2, pre=pre,
                 act=act, f=f)
    return x2, saved


def _layer_bwd(dx2, wts, s, tables, ffact_exchanges=(), delta_exchanges=None, attn_exchanges=None):
    g = {}
    df, g["norm_post_ffn"] = _norm_bwd(s["f"], wts["norm_post_ffn"], dx2, None, "norm_post_ffn_bwd", BF16)
    dact = _matmul(df, wts["ffn_w_out"], tb=True, tm=SEQ, tn=1408, tk=1024, name="mm_dact", out_dtype=BF16)
    g["ffn_w_out"] = _matmul(s["act"], df, ta=True, tm=1408, tn=512, tk=SEQ, name="mm_dw_ffn_out", out_dtype=BF16)
    (dpre, g["ffn_conv_w"], g["ffn_conv_b"]), got = _ffact_bwd(s["pre"], wts["ffn_conv_w"], wts["ffn_conv_b"], dact,
                                                               exchanges=ffact_exchanges)
    dh2 = _matmul(dpre, wts["ffn_w_in"], tm=512, tn=1024, tk=2 * D_FF, name="mm_dh2")
    g["ffn_w_in"] = _matmul(dpre, s["h2"], ta=True, tm=512, tn=1024, tk=SEQ, name="mm_dw_ffn_in", out_dtype=BF16)
    dx1, g["norm_pre_ffn"] = _norm_bwd(s["x1"], wts["norm_pre_ffn"], dh2, dx2, "norm_pre_ffn_bwd")
    dmix, g["norm_post_mix"] = _norm_bwd(s["mix"], wts["norm_post_mix"], dx1, None, "norm_post_mix_bwd", BF16)
    dcat = _matmul(dmix, wts["w_out"], tb=True, tm=SEQ, tn=512, tk=1024, name="mm_dcat")
    g["w_out"] = _matmul(s["cat"], dmix, ta=True, tm=1024, tn=512, tk=SEQ, name="mm_dw_out", out_dtype=BF16)
    (dproj, dc, g["dn_a_log"], g["dn_dt_bias"], g["dn_norm_w"]), got = _delta_bwd(
        s["c_qkv"], s["proj"], wts["dn_a_log"], wts["dn_dt_bias"], wts["dn_norm_w"], s["states"], dcat,
        exchanges=delta_exchanges(g, got) if delta_exchanges is not None else ())
    dproj, got = _attn_bwd(s["proj"], *tables, s["cat"], s["lse"], dcat, dproj,
                           exchanges=attn_exchanges(got) if attn_exchanges is not None else ())
    dproj, g["dn_conv_w"] = _dnconv_bwd(s["proj"], wts["dn_conv_w"], dc, dproj)
    dh = _matmul(dproj, wts["w_in"], tm=512, tn=1024, tk=IN_PAD, name="mm_dh")
    g["w_in"] = _matmul(dproj, s["h"], ta=True, tm=768, tn=1024, tk=SEQ, name="mm_dw_in", out_dtype=BF16)
    dx, g["norm_pre_mix"] = _norm_bwd(s["x"], wts["norm_pre_mix"], dh, dx1, "norm_pre_mix_bwd")
    return dx, g, got


BIG = ("w_in", "w_out", "ffn_w_in", "ffn_w_out")
COLUMN_SHARDED = ("w_in", "ffn_w_in")
SMALL_SHARDED = ("dn_conv_w", "ffn_conv_w")
REPLICATED = ("dn_a_log", "dn_dt_bias", "dn_norm_w", "ffn_conv_b", "norm_pre_mix", "norm_post_mix", "norm_pre_ffn",
              "norm_post_ffn")
WEIGHTS = ("w_in", "dn_conv_w", "dn_a_log", "dn_dt_bias", "dn_norm_w", "w_out", "ffn_w_in", "ffn_conv_w", "ffn_conv_b",
           "ffn_w_out", "norm_pre_mix", "norm_post_mix", "norm_pre_ffn", "norm_post_ffn")
FULL_SHAPE = dict(dn_conv_w=(DEPTH, 4, 1536), ffn_conv_w=(DEPTH, 3, 2 * D_FF), dn_a_log=(DEPTH, NDH),
                  dn_dt_bias=(DEPTH, NDH), dn_norm_w=(DEPTH, 128), ffn_conv_b=(DEPTH, 2 * D_FF),
                  norm_pre_mix=(DEPTH, D_MODEL), norm_post_mix=(DEPTH, D_MODEL), norm_pre_ffn=(DEPTH, D_MODEL),
                  norm_post_ffn=(DEPTH, D_MODEL))
SMALL_GRAD_ORDER = REPLICATED + SMALL_SHARDED
SMALL_GRAD_ROWS = 544
SMALL_W_ROWS = 56
SMALL_ADAM_ROWS = 232


def _w_in_rows_to_kernel_order(t):
    qkv = t[:QKV_W].reshape(3, N_PAIR, 128, -1).swapaxes(0, 1).reshape(QKV_W, -1)
    return jnp.pad(jnp.concatenate([qkv, t[QKV_W:]], axis=0), ((0, IN_PAD - IN_COLS), (0, 0)))


def _w_in_rows_from_kernel_order(t):
    qkv = t[:QKV_W].reshape(N_PAIR, 3, 128, -1).swapaxes(0, 1).reshape(QKV_W, -1)
    return jnp.concatenate([qkv, t[QKV_W:IN_COLS]], axis=0)


def _interleave_ff_rows(t):
    return t.reshape(2, FF_BLKS, 128, -1).swapaxes(0, 1).reshape(2 * D_FF, -1)


def _deinterleave_ff_rows(t):
    return t.reshape(FF_BLKS, 2, 128, -1).swapaxes(0, 1).reshape(2 * D_FF, -1)


def kernel(x, w_in, dn_conv_w, dn_a_log, dn_dt_bias, dn_norm_w, w_out, ffn_w_in, ffn_conv_w, ffn_conv_b, ffn_w_out, norm_pre_mix, norm_post_mix, norm_pre_ffn, norm_post_ffn, loss_target, m_w_in, m_dn_conv_w, m_dn_a_log, m_dn_dt_bias, m_dn_norm_w, m_w_out, m_ffn_w_in, m_ffn_conv_w, m_ffn_conv_b, m_ffn_w_out, m_norm_pre_mix, m_norm_post_mix, m_norm_pre_ffn, m_norm_post_ffn, v_w_in, v_dn_conv_w, v_dn_a_log, v_dn_dt_bias, v_dn_norm_w, v_w_out, v_ffn_w_in, v_ffn_conv_w, v_ffn_conv_b, v_ffn_w_out, v_norm_pre_mix, v_norm_post_mix, v_norm_pre_ffn, v_norm_post_ffn):
    local = dict(w_in=w_in, dn_conv_w=dn_conv_w, dn_a_log=dn_a_log, dn_dt_bias=dn_dt_bias, dn_norm_w=dn_norm_w,
                 w_out=w_out, ffn_w_in=ffn_w_in, ffn_conv_w=ffn_conv_w, ffn_conv_b=ffn_conv_b, ffn_w_out=ffn_w_out,
                 norm_pre_mix=norm_pre_mix, norm_post_mix=norm_post_mix, norm_pre_ffn=norm_pre_ffn,
                 norm_post_ffn=norm_post_ffn)
    mom_m = dict(w_in=m_w_in, dn_conv_w=m_dn_conv_w, dn_a_log=m_dn_a_log, dn_dt_bias=m_dn_dt_bias,
                 dn_norm_w=m_dn_norm_w, w_out=m_w_out, ffn_w_in=m_ffn_w_in, ffn_conv_w=m_ffn_conv_w,
                 ffn_conv_b=m_ffn_conv_b, ffn_w_out=m_ffn_w_out, norm_pre_mix=m_norm_pre_mix,
                 norm_post_mix=m_norm_post_mix, norm_pre_ffn=m_norm_pre_ffn, norm_post_ffn=m_norm_post_ffn)
    mom_v = dict(w_in=v_w_in, dn_conv_w=v_dn_conv_w, dn_a_log=v_dn_a_log, dn_dt_bias=v_dn_dt_bias,
                 dn_norm_w=v_dn_norm_w, w_out=v_w_out, ffn_w_in=v_ffn_w_in, ffn_conv_w=v_ffn_conv_w,
                 ffn_conv_b=v_ffn_conv_b, ffn_w_out=v_ffn_w_out, norm_pre_mix=v_norm_pre_mix,
                 norm_post_mix=v_norm_post_mix, norm_pre_ffn=v_norm_pre_ffn, norm_post_ffn=v_norm_post_ffn)
    dev = 4 * lax.axis_index("x") + 2 * lax.axis_index("y") + lax.axis_index("c")
    core = lax.axis_index("c").astype(jnp.int32).reshape(1)

    def shard(n, l):
        s = local[n].transpose(0, 2, 1) if n in COLUMN_SHARDED else local[n]
        return s[l].astype(BF16)

    def matrix(n, gathered):
        if n == "w_in":
            return _w_in_rows_to_kernel_order(gathered.reshape(IN_COLS, D_MODEL))
        if n == "ffn_w_in":
            return _interleave_ff_rows(gathered.reshape(2 * D_FF, D_MODEL))
        return gathered.reshape(-1, D_MODEL)

    small_w = _pack([dn_conv_w, ffn_conv_w], SMALL_W_ROWS)
    g_w_in0, g_small = _run_exchange(_gather_exchange([shard("w_in", 0), small_w]), "weights_all_gather")
    n_dn, n_ff = DEPTH * 4 * 192, DEPTH * 3 * 704
    dn_rows = _packed_rows(n_dn)
    sm_dn = g_small[:, :dn_rows].reshape(N_DEV, -1)[:, :n_dn]
    sm_ff = g_small[:, dn_rows:].reshape(N_DEV, -1)[:, :n_ff]
    full_dn_conv = sm_dn.reshape(N_DEV, DEPTH, 4, 192).transpose(1, 2, 0, 3).reshape(DEPTH, 4, 1536)
    full_ff_conv = _interleave_ff(sm_ff.reshape(N_DEV, DEPTH, 3, 704).transpose(1, 2, 0, 3).reshape(DEPTH, 3, 2 * D_FF))

    def small_weights(l):
        wts = dict(dn_conv_w=full_dn_conv[l], ffn_conv_w=full_ff_conv[l], ffn_conv_b=_interleave_ff(_row(ffn_conv_b[l])),
                   dn_a_log=_row(dn_a_log[l], 128), dn_dt_bias=_row(dn_dt_bias[l], 128))
        for n in ("dn_norm_w", "norm_pre_mix", "norm_post_mix", "norm_pre_ffn", "norm_post_ffn"):
            wts[n] = _row(local[n][l])
        return wts

    weights = [small_weights(l) for l in range(DEPTH)]
    weights[0]["w_in"] = matrix("w_in", g_w_in0)

    def gather_behind(wanted):
        def deliver(got):
            for (n, l), g in zip(wanted, got[0]):
                weights[l][n] = matrix(n, g)

        return [_gather_exchange([shard(n, l) for n, l in wanted])], deliver

    nothing = ((), lambda got: None)

    tables = _rope_tables()
    act, saved0 = _layer_fwd(x[0], weights[0], tables, dict(
        attn=gather_behind([("ffn_w_in", 0)]), delta=gather_behind([("w_out", 0), ("ffn_w_out", 0)]),
        ffact=gather_behind([("w_in", 1)])))
    act, saved1 = _layer_fwd(act, weights[1], tables, dict(
        attn=gather_behind([("ffn_w_in", 1)]), delta=gather_behind([("w_out", 1), ("ffn_w_out", 1)]), ffact=nothing))
    loss_part, dact = _loss_fwd_bwd(act, loss_target[0])

    def to_devices(name, t):
        if name == "w_in":
            t = _w_in_rows_from_kernel_order(t)
        if name == "ffn_w_in":
            t = _deinterleave_ff_rows(t)
        return t.reshape(N_DEV, t.shape[0] // N_DEV, t.shape[1])

    def pair_sums(names, layer, to_dev, from_sibling):
        return [_pair_add(gd, r, core, "grads_pair_add_%s_%d" % (n, layer))
                for n, gd, r in zip(names, to_dev, from_sibling)]

    early = ("w_out", "ffn_w_in", "ffn_w_out")
    grads, parts, stash = [None] * DEPTH, {}, {}

    def delta_exchanges1(g, got_ffact):
        stash["early1"] = [to_devices(n, g[n]) for n in early]
        return [_sibling_exchange(stash["early1"])]

    def attn_exchanges1(got_delta):
        return [_chips_exchange(pair_sums(early, 1, stash["early1"], got_delta[0]))]

    dact, grads[1], got_attn = _layer_bwd(dact, weights[1], saved1, tables, (), delta_exchanges1, attn_exchanges1)
    for n, p in zip(early, got_attn[0]):
        parts[n, 1] = p
    w_in1 = [to_devices("w_in", grads[1]["w_in"])]

    def delta_exchanges0(g, got_ffact):
        stash["early0"] = [to_devices(n, g[n]) for n in early]
        return [_chips_exchange(pair_sums(("w_in",), 1, w_in1, got_ffact[0])), _sibling_exchange(stash["early0"])]

    def attn_exchanges0(got_delta):
        parts["w_in", 1], = got_delta[0]
        return [_chips_exchange(pair_sums(early, 0, stash["early0"], got_delta[1]))]

    dact, grads[0], got_attn = _layer_bwd(dact, weights[0], saved0, tables, [_sibling_exchange(w_in1)],
                                          delta_exchanges0, attn_exchanges0)
    for n, p in zip(early, got_attn[0]):
        parts[n, 0] = p
    grad_x = dact[None]
    last = [to_devices("w_in", grads[0]["w_in"])]
    from_sibling = _run_exchange(_sibling_exchange(last), "grads_to_sibling")
    parts["w_in", 0], = _run_exchange(_chips_exchange(pair_sums(("w_in",), 0, last, from_sibling)), "grads_to_chips")

    def small_grad(name):
        t = jnp.stack([grads[l][name] for l in range(DEPTH)])
        if name in ("dn_a_log", "dn_dt_bias"):
            t = t[:, 0, :NDH]
        if name in ("ffn_conv_w", "ffn_conv_b"):
            t = _deinterleave_ff(t)
        return t.reshape(FULL_SHAPE[name])

    small_part = _pack([small_grad(n) for n in SMALL_GRAD_ORDER] + [loss_part[0, :1]], SMALL_GRAD_ROWS)
    small_sum = _all_gather_sum_small(small_part)
    small_g = dict(zip(SMALL_GRAD_ORDER + ("loss",), _unpack(small_sum, [FULL_SHAPE[n] for n in SMALL_GRAD_ORDER] + [(1,)])))
    loss = small_g["loss"][0]
    small_g["dn_conv_w"] = lax.dynamic_slice_in_dim(small_g["dn_conv_w"], dev * 192, 192, axis=2)
    small_g["ffn_conv_w"] = lax.dynamic_slice_in_dim(small_g["ffn_conv_w"], dev * 704, 704, axis=2)

    out_g, out_d, out_m, out_v = {}, {}, {}, {}
    for n in BIG:
        turn = (lambda t: t.transpose(0, 2, 1)) if n in COLUMN_SHARDED else (lambda t: t)
        outs = _adamw_sharded([parts[n, l] for l in range(DEPTH)], turn(local[n]), turn(mom_m[n]), turn(mom_v[n]),
                              ADAM_TILE[n], "adamw_" + n)
        out_g[n], out_d[n], out_m[n], out_v[n] = [turn(t) for t in outs]
    shapes = [small_g[n].shape for n in SMALL_GRAD_ORDER]
    d_s, m_s, v_s = _adamw_small(_pack([small_g[n] for n in SMALL_GRAD_ORDER], SMALL_ADAM_ROWS),
                                 _pack([local[n] for n in SMALL_GRAD_ORDER], SMALL_ADAM_ROWS),
                                 _pack([mom_m[n] for n in SMALL_GRAD_ORDER], SMALL_ADAM_ROWS),
                                 _pack([mom_v[n] for n in SMALL_GRAD_ORDER], SMALL_ADAM_ROWS))
    for n, d, m, v in zip(SMALL_GRAD_ORDER, _unpack(d_s, shapes), _unpack(m_s, shapes), _unpack(v_s, shapes)):
        out_g[n], out_d[n], out_m[n], out_v[n] = small_g[n], d, m, v
    return (loss, grad_x, *[out_g[n] for n in WEIGHTS], *[out_d[n] for n in WEIGHTS],
            *[out_m[n] for n in WEIGHTS], *[out_v[n] for n in WEIGHTS])
```

```python
import functools
import math

import jax
import jax.numpy as jnp
from jax import lax
from jax.experimental import pallas as pl
from jax.experimental.pallas import tpu as pltpu

F32 = jnp.float32
BF16 = jnp.bfloat16
MESH = pl.DeviceIdType.MESH

N_DEV = 8
SEQ = 2048
D_MODEL = 1024
DEPTH = 2
N_PAIR = 4
HEAD_DIM = 64
ATTN_W = 512
ATTN_BLK = 128
DILATIONS = (1, 4, 16)
SEGMENT_BLOCKS = (16, 4, 1)
N_BLK = SEQ // ATTN_BLK
NDH = 4
CH = 64
NCH = SEQ // CH
IN_COLS = 3592
IN_PAD = 3840
QKV_W = 3 * ATTN_W
DN_QKV_BLK0 = QKV_W // 128
DN_QKV_BLKS = 1536 // 128
DN_Z_COL = 3072
DN_TAIL_BLK = 3584 // 128
D_FF = 2816
FF_BLKS = D_FF // 128
EPS = 1e-6
NEG = -1e30
ROPE_THETA = 10000.0

ADAM_LR, ADAM_B1, ADAM_B2, ADAM_EPS, ADAM_WD, ADAM_STEP = 0.001, 0.9, 0.999, 1e-08, 0.01, 10

VMEM_LIMIT = 56 * 1024 * 1024


def _cp(*sem):
    return pltpu.CompilerParams(dimension_semantics=sem, vmem_limit_bytes=VMEM_LIMIT)


class Exchange:
    def __init__(self, operands, out_shapes, sems, start, middle, finish):
        self.operands, self.out_shapes, self.sems = list(operands), list(out_shapes), list(sems)
        self.start, self.middle, self.finish = start, middle, finish


HBM_SPEC = pl.BlockSpec(memory_space=pltpu.HBM)


def _hosted_call(body, *, name, steps, in_specs, out_specs, out_shape, scratch_shapes, operands, exchanges=(),
                 aliases=None):
    n_in, n_out, n_scr = len(in_specs), len(out_specs), len(scratch_shapes)

    def take(refs, pos, counts):
        groups = []
        for c in counts:
            groups.append(refs[pos:pos + c])
            pos += c
        return groups, pos

    def full_body(*refs):
        ins, pos = refs[:n_in], n_in
        ex_ins, pos = take(refs, pos, [len(e.operands) for e in exchanges])
        outs, pos = refs[pos:pos + n_out], pos + n_out
        ex_outs, pos = take(refs, pos, [len(e.out_shapes) for e in exchanges])
        scr, pos = refs[pos:pos + n_scr], pos + n_scr
        ex_sems, pos = take(refs, pos, [len(e.sems) for e in exchanges])
        step = pl.program_id(0)
        for e, a, b, s in zip(exchanges, ex_ins, ex_outs, ex_sems):
            pl.when(step == 0)(functools.partial(e.start, a, b, s))
            if e.middle is not None:
                pl.when(step == (3 * steps) // 4)(functools.partial(e.middle, a, b, s))
        body(*ins, *outs, *scr)
        for e, a, b, s in zip(exchanges, ex_ins, ex_outs, ex_sems):
            pl.when(step == steps - 1)(functools.partial(e.finish, a, b, s))

    n_ex_in = sum(len(e.operands) for e in exchanges)
    n_ex_out = sum(len(e.out_shapes) for e in exchanges)
    results = pl.pallas_call(
        full_body, name=name, grid=(steps,),
        in_specs=list(in_specs) + [HBM_SPEC] * n_ex_in,
        out_specs=list(out_specs) + [HBM_SPEC] * n_ex_out,
        out_shape=list(out_shape) + [s for e in exchanges for s in e.out_shapes],
        scratch_shapes=list(scratch_shapes) + [s for e in exchanges for s in e.sems],
        input_output_aliases=aliases or {},
        compiler_params=_cp("arbitrary"),
    )(*operands, *[a for e in exchanges for a in e.operands])
    ex_results, _ = take(results, n_out, [len(e.out_shapes) for e in exchanges])
    return results[:n_out], ex_results


def _dot(a, b, dims, precision=None):
    if precision is None:
        a = a.astype(BF16)
        b = b.astype(BF16)
    return lax.dot_general(a, b, (dims, ((), ())), preferred_element_type=F32, precision=precision)


def _make_mm(precision):
    @jax.custom_vjp
    def nn(a, b):
        return _dot(a, b, ((1,), (0,)), precision)

    @jax.custom_vjp
    def nt(a, b):
        return _dot(a, b, ((1,), (1,)), precision)

    @jax.custom_vjp
    def tn(a, b):
        return _dot(a, b, ((0,), (0,)), precision)

    nn.defvjp(lambda a, b: (nn(a, b), (a, b)), lambda r, g: (nt(g, r[1]), tn(r[0], g)))
    nt.defvjp(lambda a, b: (nt(a, b), (a, b)), lambda r, g: (nn(g, r[1]), tn(g, r[0])))
    tn.defvjp(lambda a, b: (tn(a, b), (a, b)), lambda r, g: (nt(r[1], g), nn(r[0], g)))
    return nn, nt, tn


def _matmul(a, b, *, ta=False, tb=False, tm, tn, tk, name, out_dtype=F32):
    (k_dim, m_dim) = a.shape if ta else a.shape[::-1]
    (n_dim, k2) = b.shape if tb else b.shape[::-1]
    assert k_dim == k2 and m_dim % tm == 0 and n_dim % tn == 0 and k_dim % tk == 0, (a.shape, b.shape, tm, tn, tk)
    nk = k_dim // tk
    dims = ((0 if ta else 1,), (1 if tb else 0,))

    def body(a_ref, b_ref, o_ref, *acc):
        p = _dot(a_ref[...], b_ref[...], dims)
        if nk == 1:
            o_ref[...] = p.astype(out_dtype)
            return
        acc_ref, k = acc[0], pl.program_id(2)

        @pl.when(k == 0)
        def _():
            acc_ref[...] = p

        @pl.when(k > 0)
        def _():
            acc_ref[...] += p

        @pl.when(k == nk - 1)
        def _():
            o_ref[...] = acc_ref[...].astype(out_dtype)

    a_spec = pl.BlockSpec((tk, tm), lambda i, j, k: (k, i)) if ta else pl.BlockSpec((tm, tk), lambda i, j, k: (i, k))
    b_spec = pl.BlockSpec((tn, tk), lambda i, j, k: (j, k)) if tb else pl.BlockSpec((tk, tn), lambda i, j, k: (k, j))
    return pl.pallas_call(
        body, name=name,
        grid=(m_dim // tm, n_dim // tn, nk),
        in_specs=[a_spec, b_spec],
        out_specs=pl.BlockSpec((tm, tn), lambda i, j, k: (i, j)),
        out_shape=jax.ShapeDtypeStruct((m_dim, n_dim), out_dtype),
        scratch_shapes=[pltpu.VMEM((tm, tn), F32)] if nk > 1 else [],
        compiler_params=_cp("parallel", "parallel", "arbitrary"),
    )(a, b)


NORM_ROWS = 256


def _rms(x, w):
    return x * lax.rsqrt(jnp.mean(x * x, axis=1, keepdims=True) + EPS) * w


def _norm_fwd(x, w_row, name, out_dtype=BF16):
    def body(x_ref, w_ref, o_ref):
        o_ref[...] = _rms(x_ref[...], w_ref[...]).astype(out_dtype)

    return pl.pallas_call(
        body, name=name, grid=(SEQ // NORM_ROWS,),
        in_specs=[pl.BlockSpec((NORM_ROWS, D_MODEL), lambda i: (i, 0)), pl.BlockSpec((1, D_MODEL), lambda i: (0, 0))],
        out_specs=pl.BlockSpec((NORM_ROWS, D_MODEL), lambda i: (i, 0)),
        out_shape=jax.ShapeDtypeStruct((SEQ, D_MODEL), out_dtype),
        compiler_params=_cp("parallel"),
    )(x, w_row)


def _resnorm_norm_fwd(x, f, w_row, next_w_row, name):
    def body(x_ref, f_ref, w_ref, nw_ref, o_ref, h_ref):
        out = x_ref[...] + _rms(f_ref[...], w_ref[...])
        o_ref[...] = out
        h_ref[...] = _rms(out, nw_ref[...]).astype(BF16)

    blk = pl.BlockSpec((NORM_ROWS, D_MODEL), lambda i: (i, 0))
    row = pl.BlockSpec((1, D_MODEL), lambda i: (0, 0))
    return pl.pallas_call(
        body, name=name, grid=(SEQ // NORM_ROWS,),
        in_specs=[blk, blk, row, row],
        out_specs=[blk, blk],
        out_shape=[jax.ShapeDtypeStruct((SEQ, D_MODEL), F32), jax.ShapeDtypeStruct((SEQ, D_MODEL), BF16)],
        compiler_params=_cp("parallel"),
    )(x, f, w_row, next_w_row)


def _norm_bwd(x, w_row, dy, add, name, dx_dtype=F32):
    has_add = add is not None

    def body(*refs):
        if has_add:
            x_ref, w_ref, dy_ref, add_ref, dx_ref, dw_ref = refs
        else:
            x_ref, w_ref, dy_ref, dx_ref, dw_ref = refs
        _, vjp = jax.vjp(_rms, x_ref[...], w_ref[...])
        dx, dw = vjp(dy_ref[...])
        dx_ref[...] = (dx + add_ref[...] if has_add else dx).astype(dx_dtype)

        @pl.when(pl.program_id(0) == 0)
        def _():
            dw_ref[...] = jnp.zeros_like(dw_ref)

        dw_ref[...] += dw

    blk = pl.BlockSpec((NORM_ROWS, D_MODEL), lambda i: (i, 0))
    row = pl.BlockSpec((1, D_MODEL), lambda i: (0, 0))
    ins = [x, w_row, dy] + ([add] if has_add else [])
    return pl.pallas_call(
        body, name=name, grid=(SEQ // NORM_ROWS,),
        in_specs=[blk, row, blk] + ([blk] if has_add else []),
        out_specs=[blk, row],
        out_shape=[jax.ShapeDtypeStruct((SEQ, D_MODEL), dx_dtype), jax.ShapeDtypeStruct((1, D_MODEL), F32)],
        compiler_params=_cp("arbitrary"),
    )(*ins)


def _resnorm_loss(x, f, w_row, target):
    def body(x_ref, f_ref, w_ref, t_ref, loss_ref, dy_ref):
        err = x_ref[...] + _rms(f_ref[...], w_ref[...]) - t_ref[...]
        dy_ref[...] = err * (1.0 / D_MODEL)

        @pl.when(pl.program_id(0) == 0)
        def _():
            loss_ref[...] = jnp.zeros_like(loss_ref)

        part = jnp.sum(jnp.sum(err * err, axis=1, keepdims=True) * (1.0 / D_MODEL), axis=0, keepdims=True)
        loss_ref[...] += 0.5 * jnp.broadcast_to(part, loss_ref.shape)

    blk = pl.BlockSpec((NORM_ROWS, D_MODEL), lambda i: (i, 0))
    return pl.pallas_call(
        body, name="norm_post_ffn_loss", grid=(SEQ // NORM_ROWS,),
        in_specs=[blk, blk, pl.BlockSpec((1, D_MODEL), lambda i: (0, 0)), blk],
        out_specs=[pl.BlockSpec((1, 128), lambda i: (0, 0)), blk],
        out_shape=[jax.ShapeDtypeStruct((1, 128), F32), jax.ShapeDtypeStruct((SEQ, D_MODEL), F32)],
        compiler_params=_cp("arbitrary"),
    )(x, f, w_row, target)


def _make_shift(j):
    def down(x):
        row = lax.broadcasted_iota(jnp.int32, x.shape, 0)
        return jnp.where(row >= j, pltpu.roll(x, j, 0), 0.0)

    def up(x):
        n = x.shape[0]
        row = lax.broadcasted_iota(jnp.int32, x.shape, 0)
        return jnp.where(row < n - j, pltpu.roll(x, n - j, 0), 0.0)

    f = jax.custom_vjp(down)
    f.defvjp(lambda x: (down(x), None), lambda _, g: (up(g),))
    return f


_SHIFT = {j: _make_shift(j) for j in (1, 2, 3)}


def _causal_conv(x, taps):
    n = len(taps)
    acc = x * taps[n - 1]
    for k in range(n - 1):
        acc = acc + _SHIFT[n - 1 - k](x) * taps[k]
    return acc


def _tap_rows(w_ref, lanes=slice(None)):
    return tuple(w_ref[k:k + 1, lanes] for k in range(w_ref.shape[0]))


def _sigmoid(x):
    return 1.0 / (1.0 + jnp.exp(-x))


def _silu(x):
    return x * _sigmoid(x)


def _softplus(x):
    return jnp.maximum(x, 0.0) + jnp.log(1.0 + jnp.exp(-jnp.abs(x)))


def _gelu_tanh(x):
    return 0.5 * x * (1.0 + jnp.tanh(math.sqrt(2.0 / math.pi) * (x + 0.044715 * (x * x * x))))


def _dnconv_fn(x, taps):
    return _silu(_causal_conv(x, taps))


def _dnconv_fwd(proj, conv_w):
    def body(x_ref, w_ref, o_ref):
        o_ref[...] = _dnconv_fn(x_ref[...], _tap_rows(w_ref)).astype(BF16)

    return pl.pallas_call(
        body, name="dnconv_fwd", grid=(DN_QKV_BLKS,),
        in_specs=[pl.BlockSpec((SEQ, 128), lambda j: (0, DN_QKV_BLK0 + j)), pl.BlockSpec((4, 128), lambda j: (0, j))],
        out_specs=pl.BlockSpec((SEQ, 128), lambda j: (0, j)),
        out_shape=jax.ShapeDtypeStruct((SEQ, 1536), BF16),
        compiler_params=_cp("parallel"),
    )(proj, conv_w)


def _dnconv_bwd(proj, conv_w, dc, dproj):
    def body(x_ref, w_ref, dc_ref, _, dx_ref, dw_ref):
        _, vjp = jax.vjp(_dnconv_fn, x_ref[...], _tap_rows(w_ref))
        dx, dw = vjp(dc_ref[...])
        dx_ref[...] = dx.astype(BF16)
        for k, row in enumerate(dw):
            dw_ref[k:k + 1, :] = row

    return pl.pallas_call(
        body, name="dnconv_bwd", grid=(DN_QKV_BLKS,),
        in_specs=[pl.BlockSpec((SEQ, 128), lambda j: (0, DN_QKV_BLK0 + j)), pl.BlockSpec((4, 128), lambda j: (0, j)),
                  pl.BlockSpec((SEQ, 128), lambda j: (0, j)), pl.BlockSpec(memory_space=pl.ANY)],
        out_specs=[pl.BlockSpec((SEQ, 128), lambda j: (0, DN_QKV_BLK0 + j)), pl.BlockSpec((4, 128), lambda j: (0, j))],
        out_shape=[jax.ShapeDtypeStruct((SEQ, IN_PAD), BF16), jax.ShapeDtypeStruct((4, 1536), F32)],
        input_output_aliases={3: 0},
        compiler_params=_cp("parallel"),
    )(proj, conv_w, dc, dproj)


def _ffact_fn(pg, pu, wg, wu, bg, bu):
    return _gelu_tanh(_causal_conv(pg, wg) + bg) * (_causal_conv(pu, wu) + bu)


def _ffact_args(p_ref, w_ref, b_ref):
    g, u = slice(0, 128), slice(128, 256)
    return (p_ref[:, g].astype(F32), p_ref[:, u].astype(F32), _tap_rows(w_ref, g), _tap_rows(w_ref, u),
            b_ref[:, g], b_ref[:, u])


def _ffact_fwd(pre, conv_w, conv_b, exchanges=()):
    def body(p_ref, w_ref, b_ref, o_ref):
        o_ref[...] = _ffact_fn(*_ffact_args(p_ref, w_ref, b_ref)).astype(BF16)

    (act,), results = _hosted_call(
        body, name="ffact_fwd", steps=FF_BLKS,
        in_specs=[pl.BlockSpec((SEQ, 256), lambda j: (0, j)), pl.BlockSpec((3, 256), lambda j: (0, j)),
                  pl.BlockSpec((1, 256), lambda j: (0, j))],
        out_specs=[pl.BlockSpec((SEQ, 128), lambda j: (0, j))],
        out_shape=[jax.ShapeDtypeStruct((SEQ, D_FF), BF16)],
        scratch_shapes=[], operands=(pre, conv_w, conv_b), exchanges=exchanges)
    return act, results


def _ffact_bwd(pre, conv_w, conv_b, dact, exchanges=()):
    def body(p_ref, w_ref, b_ref, da_ref, dp_ref, dw_ref, db_ref):
        _, vjp = jax.vjp(_ffact_fn, *_ffact_args(p_ref, w_ref, b_ref))
        dpg, dpu, dwg, dwu, dbg, dbu = vjp(da_ref[...].astype(F32))
        dp_ref[:, 0:128] = dpg.astype(BF16)
        dp_ref[:, 128:256] = dpu.astype(BF16)
        for k in range(3):
            dw_ref[k:k + 1, 0:128] = dwg[k]
            dw_ref[k:k + 1, 128:256] = dwu[k]
        db_ref[:, 0:128] = dbg
        db_ref[:, 128:256] = dbu

    return _hosted_call(
        body, name="ffact_bwd", steps=FF_BLKS,
        in_specs=[pl.BlockSpec((SEQ, 256), lambda j: (0, j)), pl.BlockSpec((3, 256), lambda j: (0, j)),
                  pl.BlockSpec((1, 256), lambda j: (0, j)), pl.BlockSpec((SEQ, 128), lambda j: (0, j))],
        out_specs=[pl.BlockSpec((SEQ, 256), lambda j: (0, j)), pl.BlockSpec((3, 256), lambda j: (0, j)),
                   pl.BlockSpec((1, 256), lambda j: (0, j))],
        out_shape=[jax.ShapeDtypeStruct((SEQ, 2 * D_FF), BF16), jax.ShapeDtypeStruct((3, 2 * D_FF), F32),
                   jax.ShapeDtypeStruct((1, 2 * D_FF), F32)],
        scratch_shapes=[], operands=(pre, conv_w, conv_b, dact), exchanges=exchanges)


def _interleave_ff(t):
    lead = t.shape[:-1]
    return t.reshape(lead + (2, FF_BLKS, 128)).swapaxes(-3, -2).reshape(lead + (2 * D_FF,))


def _deinterleave_ff(t):
    lead = t.shape[:-1]
    return t.reshape(lead + (FF_BLKS, 2, 128)).swapaxes(-3, -2).reshape(lead + (2 * D_FF,))


def _rope_tables():
    inv = 1.0 / (ROPE_THETA ** (jnp.arange(0, HEAD_DIM, 2, dtype=F32) / HEAD_DIM))
    ang = jnp.arange(SEQ, dtype=F32)[:, None] * inv[None, :]
    cos = jnp.tile(jnp.cos(ang), (1, 4))
    sin = jnp.tile(jnp.sin(ang), (1, 4))
    sign = jnp.where((jnp.arange(128) % HEAD_DIM) < HEAD_DIM // 2, -1.0, 1.0).astype(F32)
    return cos, sin * sign[None, :]


def _rope(x, cos, sin_signed):
    lane = lax.broadcasted_iota(jnp.int32, x.shape, 1)
    partner = jnp.where((lane % HEAD_DIM) < HEAD_DIM // 2, pltpu.roll(x, 128 - HEAD_DIM // 2, 1),
                        pltpu.roll(x, HEAD_DIM // 2, 1))
    return x * cos + partner * sin_signed


def _head_masks():
    lane = lax.broadcasted_iota(jnp.int32, (1, 128), 1)
    return [(lane // HEAD_DIM) == h for h in range(2)]


def _both_heads(x):
    return jnp.concatenate([jnp.where(hm, x, 0.0)[None] for hm in _head_masks()], axis=0)


def _block_keys(branch, k_s, v_s, rows, prows, has_prev):
    a = lax.broadcasted_iota(jnp.int32, (ATTN_BLK, ATTN_BLK), 0)
    c = lax.broadcasted_iota(jnp.int32, (ATTN_BLK, ATTN_BLK), 1)
    keys, values, mask = k_s[rows, :], v_s[rows, :], c <= a
    if SEGMENT_BLOCKS[branch] > 1:
        keys = jnp.concatenate([k_s[prows, :], keys], axis=0)
        values = jnp.concatenate([v_s[prows, :], values], axis=0)
        mask = jnp.concatenate([(c >= a) & has_prev, mask], axis=1)
    twice = lambda t: jnp.broadcast_to(t[None], (2,) + t.shape)
    return twice(keys), twice(values), mask


def _block_rows(branch, t):
    d, per_seg = DILATIONS[branch], SEGMENT_BLOCKS[branch]
    if d == 1:
        start = pl.multiple_of(t * ATTN_BLK, ATTN_BLK)
        prev = pl.multiple_of(jnp.maximum(t - 1, 0) * ATTN_BLK, ATTN_BLK)
        return pl.ds(start, ATTN_BLK), pl.ds(prev, ATTN_BLK), t > 0
    r, n = t // per_seg, t % per_seg
    start = n * (ATTN_BLK * d) + r
    prev = jnp.maximum(n - 1, 0) * (ATTN_BLK * d) + r
    return pl.ds(start, ATTN_BLK, stride=d), pl.ds(prev, ATTN_BLK, stride=d), n > 0


def _attn_fwd(proj, cos, sin_signed, exchanges=()):
    scale = HEAD_DIM ** -0.5

    def body(qkv_ref, cos_ref, sin_ref, out_ref, lse_ref, q_s, k_s, v_s, *branch_s):
        o_s, l_s = branch_s[:3], branch_s[3:]
        q_s[...] = _rope(qkv_ref[:, 0:128], cos_ref[...], sin_ref[...])
        k_s[...] = _rope(qkv_ref[:, 128:256], cos_ref[...], sin_ref[...])
        v_s[...] = qkv_ref[:, 256:384]
        heads = _head_masks()
        for branch in range(3):
            def block(t, carry, branch=branch):
                rows, prows, has_prev = _block_rows(branch, t)
                keys, values, mask = _block_keys(branch, k_s, v_s, rows, prows, has_prev)
                s = jnp.where(mask, BMM_NT(_both_heads(q_s[rows, :]), keys) * scale, NEG)
                m = jnp.max(s, axis=2, keepdims=True)
                e = jnp.exp(s - m)
                l = jnp.sum(e, axis=2, keepdims=True)
                o = BMM(e, values) / l
                lse_b = m + jnp.log(l)
                o_s[branch][rows, :] = jnp.where(heads[0], o[0], o[1])
                l_s[branch][rows, :] = jnp.where(heads[0], lse_b[0], lse_b[1])
                return carry

            lax.fori_loop(0, N_BLK, block, 0, unroll=4)
        l0, l1, l2 = l_s[0][...], l_s[1][...], l_s[2][...]
        m = jnp.maximum(jnp.maximum(l0, l1), l2)
        w0, w1, w2 = jnp.exp(l0 - m), jnp.exp(l1 - m), jnp.exp(l2 - m)
        den = w0 + w1 + w2
        out_ref[...] = (w0 * o_s[0][...] + w1 * o_s[1][...] + w2 * o_s[2][...]) / den
        lse_ref[...] = m + jnp.log(den)

    tab = pl.BlockSpec((SEQ, 128), lambda j: (0, 0))
    col = pl.BlockSpec((SEQ, 128), lambda j: (0, j))
    return _hosted_call(
        body, name="attn_fwd", steps=N_PAIR,
        in_specs=[pl.BlockSpec((SEQ, 384), lambda j: (0, j)), tab, tab],
        out_specs=[col, col],
        out_shape=[jax.ShapeDtypeStruct((SEQ, 2 * ATTN_W), F32), jax.ShapeDtypeStruct((SEQ, ATTN_W), F32)],
        scratch_shapes=[pltpu.VMEM((SEQ, 128), F32)] * 9,
        operands=(proj, cos, sin_signed), exchanges=exchanges)


def _attn_bwd(proj, cos, sin_signed, cat, lse, dcat, dproj, exchanges=()):
    scale = HEAD_DIM ** -0.5

    def body(qkv_ref, cos_ref, sin_ref, out_ref, lse_ref, do_ref, _, dqkv_ref, q_s, k_s, v_s, dq_s, dk_s, dv_s,
             dod_s):
        q_s[...] = _rope(qkv_ref[:, 0:128], cos_ref[...], sin_ref[...])
        k_s[...] = _rope(qkv_ref[:, 128:256], cos_ref[...], sin_ref[...])
        v_s[...] = qkv_ref[:, 256:384]
        dq_s[...] = jnp.zeros_like(dq_s)
        dk_s[...] = jnp.zeros_like(dk_s)
        dv_s[...] = jnp.zeros_like(dv_s)
        dod_s[...] = do_ref[...] * out_ref[...]
        heads = _head_masks()
        for branch in range(3):
            def block(t, carry, branch=branch):
                rows, prows, has_prev = _block_rows(branch, t)
                keys, values, mask = _block_keys(branch, k_s, v_s, rows, prows, has_prev)
                q2, do2 = _both_heads(q_s[rows, :]), _both_heads(do_ref[rows, :])
                lse_b, dod = lse_ref[rows, :], dod_s[rows, :]
                lse2 = jnp.concatenate(
                    [jnp.max(jnp.where(hm, lse_b, NEG), axis=1, keepdims=True)[None] for hm in heads], axis=0)
                delta = jnp.concatenate(
                    [jnp.sum(jnp.where(hm, dod, 0.0), axis=1, keepdims=True)[None] for hm in heads], axis=0)
                p = jnp.exp(jnp.where(mask, BMM_NT(q2, keys) * scale, NEG) - lse2)
                ds = p * (BMM_NT(do2, values) - delta) * scale
                dq = BMM(ds, keys)
                dk = BMM_TN(ds, q2)
                dv = BMM_TN(p, do2)
                dk, dv = dk[0] + dk[1], dv[0] + dv[1]
                dq_s[rows, :] += jnp.where(heads[0], dq[0], dq[1])
                if SEGMENT_BLOCKS[branch] > 1:
                    dk_s[rows, :] += dk[ATTN_BLK:]
                    dv_s[rows, :] += dv[ATTN_BLK:]

                    @pl.when(has_prev)
                    def _():
                        dk_s[prows, :] += dk[:ATTN_BLK]
                        dv_s[prows, :] += dv[:ATTN_BLK]
                else:
                    dk_s[rows, :] += dk
                    dv_s[rows, :] += dv
                return carry

            lax.fori_loop(0, N_BLK, block, 0, unroll=4)
        dqkv_ref[:, 0:128] = _rope(dq_s[...], cos_ref[...], -sin_ref[...]).astype(BF16)
        dqkv_ref[:, 128:256] = _rope(dk_s[...], cos_ref[...], -sin_ref[...]).astype(BF16)
        dqkv_ref[:, 256:384] = dv_s[...].astype(BF16)

    tab = pl.BlockSpec((SEQ, 128), lambda j: (0, 0))
    col = pl.BlockSpec((SEQ, 128), lambda j: (0, j))
    qkv = pl.BlockSpec((SEQ, 384), lambda j: (0, j))
    (dproj,), results = _hosted_call(
        body, name="attn_bwd", steps=N_PAIR,
        in_specs=[qkv, tab, tab, col, col, col, pl.BlockSpec(memory_space=pl.ANY)],
        out_specs=[qkv],
        out_shape=[jax.ShapeDtypeStruct((SEQ, IN_PAD), BF16)],
        scratch_shapes=[pltpu.VMEM((SEQ, 128), F32)] * 7,
        operands=(proj, cos, sin_signed, cat, lse, dcat, dproj), exchanges=exchanges, aliases={6: 0})
    return dproj, results


def _bdot(a, b, dims, precision=None):
    if precision is None:
        a = a.astype(BF16)
        b = b.astype(BF16)
    return lax.dot_general(a, b, (dims, ((0,), (0,))), preferred_element_type=F32, precision=precision)


def _make_bmm(precision):
    @jax.custom_vjp
    def nn(a, b):
        return _bdot(a, b, ((2,), (1,)), precision)

    @jax.custom_vjp
    def nt(a, b):
        return _bdot(a, b, ((2,), (2,)), precision)

    @jax.custom_vjp
    def tn(a, b):
        return _bdot(a, b, ((1,), (1,)), precision)

    nn.defvjp(lambda a, b: (nn(a, b), (a, b)), lambda r, g: (nt(g, r[1]), tn(r[0], g)))
    nt.defvjp(lambda a, b: (nt(a, b), (a, b)), lambda r, g: (nn(g, r[1]), tn(g, r[0])))
    tn.defvjp(lambda a, b: (tn(a, b), (a, b)), lambda r, g: (nt(r[1], g), nn(r[0], g)))
    return nn, nt, tn


BMM, BMM_NT, BMM_TN = _make_bmm(None)
BMM3, BMM3_NT, BMM3_TN = _make_bmm(lax.Precision.HIGH)
MM3, _, _ = _make_mm(lax.Precision.HIGH)


def _head_lanes(t, off):
    lane = lax.broadcasted_iota(jnp.int32, (1, 128), 1)
    return jnp.concatenate(
        [jnp.sum(t * (lane == off + h).astype(F32), axis=1, keepdims=True)[None] for h in range(NDH)], axis=0)


@jax.custom_vjp
def _unit_lower_inverse(a_mat):
    c = a_mat.shape[1]
    eye = (lax.broadcasted_iota(jnp.int32, (c, c), 0) == lax.broadcasted_iota(jnp.int32, (c, c), 1)).astype(F32)
    power = -a_mat
    t_inv = eye + power
    for _ in range(5):
        power = BMM3(power, power)
        t_inv = t_inv + BMM3(t_inv, power)
    return t_inv


def _unit_lower_inverse_fwd(a_mat):
    t_inv = _unit_lower_inverse(a_mat)
    return t_inv, t_inv


def _unit_lower_inverse_bwd(t_inv, d_inv):
    return (-BMM3_NT(BMM3_TN(t_inv, d_inv), t_inv),)


_unit_lower_inverse.defvjp(_unit_lower_inverse_fwd, _unit_lower_inverse_bwd)


DN_STEP_CHUNKS = 4
DN_STEP_ROWS = DN_STEP_CHUNKS * CH
DN_STEPS = NCH // DN_STEP_CHUNKS
DN_BATCH = DN_STEP_CHUNKS * NDH


def _delta_chunks(qr, kr, vr, z, tail, alog_row, dt_row, nw, state):
    c = qr.shape[1]
    tails = [tail[CH * n:CH * (n + 1)] for n in range(DN_STEP_CHUNKS)]
    per_chunk = lambda t: jnp.concatenate([t] * DN_STEP_CHUNKS, axis=0)
    beta = _sigmoid(jnp.concatenate([_head_lanes(t, 0) for t in tails], axis=0))
    a_raw = jnp.concatenate([_head_lanes(t, NDH) for t in tails], axis=0)
    g = -jnp.exp(per_chunk(_head_lanes(alog_row, 0))) * _softplus(a_raw + per_chunk(_head_lanes(dt_row, 0)))

    q = qr * lax.rsqrt(jnp.sum(qr * qr, axis=2, keepdims=True) + EPS) * (128 ** -0.5)
    k = kr * lax.rsqrt(jnp.sum(kr * kr, axis=2, keepdims=True) + EPS)

    ri = lax.broadcasted_iota(jnp.int32, (c, c), 0)
    ci = lax.broadcasted_iota(jnp.int32, (c, c), 1)
    tril = ri >= ci
    lane = lax.broadcasted_iota(jnp.int32, (1, 128), 1)
    pick = [(lane == b).astype(F32) for b in range(DN_BATCH)]
    g_lanes = sum(g[b] * pick[b] for b in range(DN_BATCH))
    g_sums = MM3(tril.astype(F32), g_lanes)
    gc = jnp.concatenate([jnp.sum(g_sums * pick[b], axis=1, keepdims=True)[None] for b in range(DN_BATCH)],
                         axis=0)
    g_row = jnp.swapaxes(jnp.broadcast_to(gc, (DN_BATCH, c, c)), 1, 2)
    decay = jnp.where(tril, jnp.exp(jnp.where(tril, gc - g_row, 0.0)), 0.0)
    kb = k * beta
    t_inv = _unit_lower_inverse(jnp.where(ri > ci, BMM_NT(kb, k) * decay, 0.0))
    eg = jnp.exp(gc)
    u = BMM(t_inv, vr * beta)
    w = BMM(t_inv, kb * eg)
    qk = BMM_NT(q, k) * decay
    g_tot = jnp.sum(g, axis=1, keepdims=True)
    q_dec = q * eg
    k_dec = k * jnp.exp(g_tot - gc)
    outs = []
    for n in range(DN_STEP_CHUNKS):
        heads = slice(NDH * n, NDH * (n + 1))
        v_new = u[heads] - BMM(w[heads], state)
        outs.append(BMM(q_dec[heads], state) + BMM(qk[heads], v_new))
        state = state * jnp.exp(g_tot[heads]) + BMM_TN(k_dec[heads], v_new)
    o = jnp.concatenate(outs, axis=0)
    on = o * lax.rsqrt(jnp.mean(o * o, axis=2, keepdims=True) + EPS) * nw
    return on * _silu(z), state


def _heads(v, off=0):
    return jnp.concatenate([v[None, CH * n:CH * (n + 1), off + 128 * h:off + 128 * (h + 1)]
                            for n in range(DN_STEP_CHUNKS) for h in range(NDH)], axis=0)


def _unheads(t):
    return jnp.concatenate([jnp.concatenate([t[NDH * n + h] for h in range(NDH)], axis=1)
                            for n in range(DN_STEP_CHUNKS)], axis=0)


def _delta_fwd(c_qkv, proj, alog_row, dt_row, nw, cat, exchanges=()):
    def body(c_ref, z_ref, tail_ref, al_ref, dt_ref, nw_ref, _, y_ref, st_ref, state):
        @pl.when(pl.program_id(0) == 0)
        def _():
            state[...] = jnp.zeros_like(state)

        cv = c_ref[...].astype(F32)
        st_ref[0] = state[...]
        y, new_state = _delta_chunks(_heads(cv), _heads(cv, 512), _heads(cv, 1024), _heads(z_ref[...]), tail_ref[...],
                                     al_ref[...], dt_ref[...], nw_ref[...], state[...])
        y_ref[...] = _unheads(y)
        state[...] = new_state

    row = pl.BlockSpec((1, 128), lambda n: (0, 0))
    rows = DN_STEP_ROWS
    return _hosted_call(
        body, name="delta_fwd", steps=DN_STEPS,
        in_specs=[pl.BlockSpec((rows, 1536), lambda n: (n, 0)), pl.BlockSpec((rows, 512), lambda n: (n, DN_Z_COL // 512)),
                  pl.BlockSpec((rows, 128), lambda n: (n, DN_TAIL_BLK)), row, row, row, pl.BlockSpec(memory_space=pl.ANY)],
        out_specs=[pl.BlockSpec((rows, 512), lambda n: (n, 1)),
                   pl.BlockSpec((1, NDH, 128, 128), lambda n: (n, 0, 0, 0))],
        out_shape=[jax.ShapeDtypeStruct((SEQ, 2 * ATTN_W), F32), jax.ShapeDtypeStruct((DN_STEPS, NDH, 128, 128), F32)],
        scratch_shapes=[pltpu.VMEM((NDH, 128, 128), F32)],
        operands=(c_qkv, proj, proj, alog_row, dt_row, nw, cat), exchanges=exchanges, aliases={6: 0})


def _delta_bwd(c_qkv, proj, alog_row, dt_row, nw, states, dcat, exchanges=()):
    def body(c_ref, z_ref, tail_ref, al_ref, dt_ref, nw_ref, st_ref, dy_ref,
             dp_ref, dc_ref, dal_ref, ddt_ref, dnw_ref, dstate):
        @pl.when(pl.program_id(0) == 0)
        def _():
            dstate[...] = jnp.zeros_like(dstate)
            dal_ref[...] = jnp.zeros_like(dal_ref)
            ddt_ref[...] = jnp.zeros_like(ddt_ref)
            dnw_ref[...] = jnp.zeros_like(dnw_ref)

        cv = c_ref[...].astype(F32)
        _, vjp = jax.vjp(_delta_chunks, _heads(cv), _heads(cv, 512), _heads(cv, 1024), _heads(z_ref[...]),
                         tail_ref[...], al_ref[...], dt_ref[...], nw_ref[...], st_ref[0])
        dq, dk, dv, dz, dtail, dal, ddt, dnw, dst = vjp((_heads(dy_ref[...]), dstate[...]))
        dstate[...] = dst
        dc_ref[...] = jnp.concatenate([_unheads(dq), _unheads(dk), _unheads(dv)], axis=1)
        dp_ref[...] = jnp.concatenate([_unheads(dz), dtail, jnp.zeros((DN_STEP_ROWS, 128), F32)], axis=1).astype(BF16)
        dal_ref[...] += dal
        ddt_ref[...] += ddt
        dnw_ref[...] += dnw

    rev = lambda n: DN_STEPS - 1 - n
    row = pl.BlockSpec((1, 128), lambda n: (0, 0))
    rows = DN_STEP_ROWS
    return _hosted_call(
        body, name="delta_bwd", steps=DN_STEPS,
        in_specs=[pl.BlockSpec((rows, 1536), lambda n: (rev(n), 0)),
                  pl.BlockSpec((rows, 512), lambda n: (rev(n), DN_Z_COL // 512)),
                  pl.BlockSpec((rows, 128), lambda n: (rev(n), DN_TAIL_BLK)), row, row, row,
                  pl.BlockSpec((1, NDH, 128, 128), lambda n: (rev(n), 0, 0, 0)),
                  pl.BlockSpec((rows, 512), lambda n: (rev(n), 1))],
        out_specs=[pl.BlockSpec((rows, 768), lambda n: (rev(n), DN_Z_COL // 768)),
                   pl.BlockSpec((rows, 1536), lambda n: (rev(n), 0)), row, row, row],
        out_shape=[jax.ShapeDtypeStruct((SEQ, IN_PAD), BF16), jax.ShapeDtypeStruct((SEQ, 1536), F32)]
        + [jax.ShapeDtypeStruct((1, 128), F32)] * 3,
        scratch_shapes=[pltpu.VMEM((NDH, 128, 128), F32)],
        operands=(c_qkv, proj, proj, alog_row, dt_row, nw, states, dcat), exchanges=exchanges)


def _place():
    x, y, c = lax.axis_index("x"), lax.axis_index("y"), lax.axis_index("c")
    other_chips = [(1 - x, y), (x, 1 - y), (1 - x, 1 - y)]
    return x, y, c, other_chips


def _gather_exchange(shards):
    n = len(shards)

    def copies(ins, outs, sems):
        send_sems, recv_sems, local_sems = sems
        x, y, c, chips = _place()
        me, sibling = (x, y, c), (x, y, 1 - c)

        def copy(b, k, block, to, src=None):
            slot = outs[b].at[4 * block[0] + 2 * block[1] + block[2]]
            return pltpu.make_async_remote_copy(
                src_ref=slot if src is None else src, dst_ref=slot,
                send_sem=send_sems.at[b, k], recv_sem=recv_sems.at[b, k], device_id=to, device_id_type=MESH)

        mine = [pltpu.make_async_copy(ins[b], outs[b].at[4 * x + 2 * y + c], local_sems.at[b]) for b in range(n)]
        first = []
        for b in range(n):
            first.append(copy(b, 0, me, sibling, src=ins[b]))
            first += [copy(b, 1 + j, me, (*chip, c), src=ins[b]) for j, chip in enumerate(chips)]
        over_ici = [copy(b, 1 + j, (*chip, c), me) for b in range(n) for j, chip in enumerate(chips)]
        passed = [copy(b, 4 + j, (*chip, c), sibling) for b in range(n) for j, chip in enumerate(chips)]
        from_sibling = []
        for b in range(n):
            from_sibling.append(copy(b, 0, sibling, me))
            from_sibling += [copy(b, 4 + j, (*chip, 1 - c), me) for j, chip in enumerate(chips)]
        return mine, first, over_ici, passed, from_sibling

    def start(ins, outs, sems):
        mine, first, _, _, _ = copies(ins, outs, sems)
        for cp in mine + first:
            cp.start()

    def middle(ins, outs, sems):
        _, _, over_ici, passed, _ = copies(ins, outs, sems)
        for arrived, onward in zip(over_ici, passed):
            arrived.wait_recv()
            onward.start()

    def finish(ins, outs, sems):
        mine, first, _, passed, from_sibling = copies(ins, outs, sems)
        for cp in from_sibling:
            cp.wait_recv()
        for cp in first + passed:
            cp.wait_send()
        for cp in mine:
            cp.wait()

    return Exchange(shards, [jax.ShapeDtypeStruct((N_DEV,) + s.shape, s.dtype) for s in shards],
                    [pltpu.SemaphoreType.DMA((n, 7)), pltpu.SemaphoreType.DMA((n, 7)), pltpu.SemaphoreType.DMA((n,))],
                    start, middle, finish)


def _sibling_exchange(gs):
    n = len(gs)

    def copies(ins, outs, sems):
        send_sems, recv_sems = sems
        x, y, c, _ = _place()
        return [pltpu.make_async_remote_copy(
            src_ref=ins[b].at[2 * p + (1 - c)], dst_ref=outs[b].at[p],
            send_sem=send_sems.at[b, p], recv_sem=recv_sems.at[b, p],
            device_id=(x, y, 1 - c), device_id_type=MESH) for b in range(n) for p in range(4)]

    def start(ins, outs, sems):
        for cp in copies(ins, outs, sems):
            cp.start()

    def finish(ins, outs, sems):
        for cp in copies(ins, outs, sems):
            cp.wait()

    return Exchange(gs, [jax.ShapeDtypeStruct((4,) + g.shape[1:], g.dtype) for g in gs],
                    [pltpu.SemaphoreType.DMA((n, 4)), pltpu.SemaphoreType.DMA((n, 4))], start, None, finish)


def _chips_exchange(hs):
    n = len(hs)

    def copies(ins, outs, sems):
        send_sems, recv_sems, local_sems = sems
        x, y, c, chips = _place()
        my_chip = 2 * x + y
        local = [pltpu.make_async_copy(ins[b].at[my_chip], outs[b].at[my_chip], local_sems.at[b]) for b in range(n)]
        sends, arrivals = [], []
        for b in range(n):
            for k, (px, py) in enumerate(chips):
                peer = 2 * px + py
                sends.append(pltpu.make_async_remote_copy(
                    src_ref=ins[b].at[peer], dst_ref=outs[b].at[my_chip],
                    send_sem=send_sems.at[b, k], recv_sem=recv_sems.at[b, k],
                    device_id=(px, py, c), device_id_type=MESH))
                arrivals.append(pltpu.make_async_remote_copy(
                    src_ref=ins[b].at[peer], dst_ref=outs[b].at[peer],
                    send_sem=send_sems.at[b, k], recv_sem=recv_sems.at[b, k],
                    device_id=(px, py, c), device_id_type=MESH))
        return local, sends, arrivals

    def start(ins, outs, sems):
        local, sends, _ = copies(ins, outs, sems)
        for cp in local + sends:
            cp.start()

    def finish(ins, outs, sems):
        local, sends, arrivals = copies(ins, outs, sems)
        for cp in arrivals:
            cp.wait_recv()
        for cp in sends:
            cp.wait_send()
        for cp in local:
            cp.wait()

    return Exchange(hs, [jax.ShapeDtypeStruct(h.shape, h.dtype) for h in hs],
                    [pltpu.SemaphoreType.DMA((n, 3)), pltpu.SemaphoreType.DMA((n, 3)), pltpu.SemaphoreType.DMA((n,))],
                    start, None, finish)


def _run_exchange(exchange, name):
    n_in, n_out = len(exchange.operands), len(exchange.out_shapes)

    def body(*refs):
        ins, outs, sems = refs[:n_in], refs[n_in:n_in + n_out], refs[n_in + n_out:]
        exchange.start(ins, outs, sems)
        if exchange.middle is not None:
            exchange.middle(ins, outs, sems)
        exchange.finish(ins, outs, sems)

    return pl.pallas_call(
        body, name=name,
        in_specs=[HBM_SPEC] * n_in, out_specs=[HBM_SPEC] * n_out,
        out_shape=exchange.out_shapes, scratch_shapes=exchange.sems,
    )(*exchange.operands)


def _pair_add(g, r, core, name):
    _, nr, nc = g.shape
    tr = nr // 2 if nr % 32 == 0 else nr

    def body(core_ref, g_ref, r_ref, o_ref):
        o_ref[...] = (g_ref[...].astype(F32) + r_ref[...].astype(F32)).astype(BF16)

    return pl.pallas_call(
        body, name=name,
        grid_spec=pltpu.PrefetchScalarGridSpec(
            num_scalar_prefetch=1, grid=(4, nr // tr),
            in_specs=[pl.BlockSpec((1, tr, nc), lambda p, i, core: (2 * p + core[0], i, 0)),
                      pl.BlockSpec((1, tr, nc), lambda p, i, core: (p, i, 0))],
            out_specs=pl.BlockSpec((1, tr, nc), lambda p, i, core: (p, i, 0))),
        out_shape=jax.ShapeDtypeStruct(r.shape, BF16),
        compiler_params=_cp("parallel", "parallel"),
    )(core, g, r)


def _all_gather_sum_small(v):
    rows = v.shape[0]

    def body(x_ref, sum_ref, out_ref, send_sems, recv_sems, local_sem):
        x, y, c, chips = _place()
        me, sibling = (x, y, c), (x, y, 1 - c)

        def block(px, py, pc):
            return out_ref.at[pl.ds((4 * px + 2 * py + pc) * rows, rows), :]

        def copy(k, blk, to, src=None):
            return pltpu.make_async_remote_copy(
                src_ref=block(*blk) if src is None else src, dst_ref=block(*blk),
                send_sem=send_sems.at[k], recv_sem=recv_sems.at[k], device_id=to, device_id_type=MESH)

        mine = pltpu.make_async_copy(x_ref, block(*me), local_sem)
        mine.start()
        first = [copy(0, me, sibling, src=x_ref)]
        first += [copy(1 + j, me, (*chip, c), src=x_ref) for j, chip in enumerate(chips)]
        for cp in first:
            cp.start()
        passed = [copy(4 + j, (*chip, c), sibling) for j, chip in enumerate(chips)]
        for j, chip in enumerate(chips):
            copy(1 + j, (*chip, c), me).wait_recv()
            passed[j].start()
        copy(0, sibling, me).wait_recv()
        for j, chip in enumerate(chips):
            copy(4 + j, (*chip, 1 - c), me).wait_recv()
        for cp in first + passed:
            cp.wait_send()
        mine.wait()
        total = out_ref[pl.ds(0, rows), :]
        for d in range(1, N_DEV):
            total = total + out_ref[pl.ds(d * rows, rows), :]
        sum_ref[...] = total

    vm = pl.BlockSpec(memory_space=pltpu.VMEM)
    return pl.pallas_call(
        body, name="small_all_reduce",
        in_specs=[vm], out_specs=[vm],
        out_shape=[jax.ShapeDtypeStruct((rows, 128), F32)],
        scratch_shapes=[pltpu.VMEM((N_DEV * rows, 128), F32), pltpu.SemaphoreType.DMA((7,)),
                        pltpu.SemaphoreType.DMA((7,)), pltpu.SemaphoreType.DMA],
    )(v)[0]


def _adamw(w, g, m, v):
    m = ADAM_B1 * m + (1.0 - ADAM_B1) * g
    v = ADAM_B2 * v + (1.0 - ADAM_B2) * (g * g)
    m_hat = m / (1.0 - ADAM_B1 ** ADAM_STEP)
    v_hat = v / (1.0 - ADAM_B2 ** ADAM_STEP)
    delta = -ADAM_LR * (m_hat / (jnp.sqrt(v_hat) + ADAM_EPS) + ADAM_WD * w)
    return delta, m, v


ADAM_TILE = dict(w_in=(IN_COLS // N_DEV, 256), w_out=(128, D_MODEL), ffn_w_in=(176, D_MODEL), ffn_w_out=(176, D_MODEL))


def _sum_chips(p):
    p = p.astype(F32)
    return (p[0] + p[1]) + (p[2] + p[3])


def _adamw_sharded(parts, w, m, v, tile, name):
    nl, nr, nc = w.shape
    tr, tc = tile

    def body(*refs):
        p_refs, (w_ref, m_ref, v_ref, g_ref, d_ref, nm_ref, nv_ref) = refs[:nl], refs[nl:]
        layer = pl.program_id(0)
        p = p_refs[0][...]
        for l in range(1, nl):
            p = jnp.where(layer == l, p_refs[l][...], p)
        g = _sum_chips(p)
        delta, nm, nv = _adamw(w_ref[0], g, m_ref[0], v_ref[0])
        g_ref[0] = g
        d_ref[0] = delta
        nm_ref[0] = nm
        nv_ref[0] = nv

    blk = pl.BlockSpec((1, tr, tc), lambda l, i, j: (l, i, j))
    return pl.pallas_call(
        body, name=name, grid=(nl, nr // tr, nc // tc),
        in_specs=[pl.BlockSpec((4, tr, tc), lambda l, i, j, own=own: (0, jnp.where(l == own, i, 0), j))
                  for own in range(nl)] + [blk, blk, blk],
        out_specs=[blk] * 4,
        out_shape=[jax.ShapeDtypeStruct(w.shape, F32)] * 4,
        compiler_params=_cp("parallel", "parallel", "parallel"),
    )(*parts, w, m, v)


def _adamw_small(g, w, m, v):
    def body(g_ref, w_ref, m_ref, v_ref, d_ref, nm_ref, nv_ref):
        delta, nm, nv = _adamw(w_ref[...], g_ref[...], m_ref[...], v_ref[...])
        d_ref[...] = delta
        nm_ref[...] = nm
        nv_ref[...] = nv

    return pl.pallas_call(
        body, name="adamw_small",
        out_shape=[jax.ShapeDtypeStruct(g.shape, F32)] * 3,
    )(g, w, m, v)


def _packed_rows(n):
    return -(-n // 1024) * 8


def _pack(arrays, rows):
    pieces = []
    for a in arrays:
        flat = a.reshape(-1).astype(F32)
        nr = _packed_rows(flat.shape[0])
        pieces.append(jnp.pad(flat, (0, nr * 128 - flat.shape[0])).reshape(nr, 128))
    used = sum(p.shape[0] for p in pieces)
    return jnp.concatenate(pieces + [jnp.zeros((rows - used, 128), F32)] * (rows > used), axis=0)


def _unpack(packed, shapes):
    out, row = [], 0
    for s in shapes:
        n = math.prod(s)
        out.append(packed[row:row + _packed_rows(n)].reshape(-1)[:n].reshape(s))
        row += _packed_rows(n)
    return out


def _row(v, width=None):
    v = v.reshape(1, -1)
    return v if width is None else jnp.pad(v, ((0, 0), (0, width - v.shape[1])))


def _layer_fwd(x, h, wts, tables, hosted, last_step):
    proj = _matmul(h, wts["w_in"], tb=True, tm=SEQ, tn=768, tk=1024, name="mm_proj")
    (cat, lse), got = _attn_fwd(proj, *tables, exchanges=hosted["attn"][0])
    hosted["attn"][1](got)
    c_qkv = _dnconv_fwd(proj, wts["dn_conv_w"])
    (cat, states), got = _delta_fwd(c_qkv, proj, wts["dn_a_log"], wts["dn_dt_bias"], wts["dn_norm_w"], cat,
                                    exchanges=hosted["delta"][0])
    hosted["delta"][1](got)
    mix = _matmul(cat, wts["w_out"], tm=512, tn=1024, tk=1024, name="mm_mix")
    x1, h2 = _resnorm_norm_fwd(x, mix, wts["norm_post_mix"], wts["norm_pre_ffn"], "norm_post_mix")
    pre = _matmul(h2, wts["ffn_w_in"], tb=True, tm=SEQ, tn=512, tk=1024, name="mm_ffn_in", out_dtype=BF16)
    act, got = _ffact_fwd(pre, wts["ffn_conv_w"], wts["ffn_conv_b"], exchanges=hosted["ffact"][0])
    hosted["ffact"][1](got)
    f = _matmul(act, wts["ffn_w_out"], tm=512, tn=1024, tk=D_FF, name="mm_ffn_out")
    saved = dict(x=x, h=h, proj=proj, lse=lse, c_qkv=c_qkv, states=states, cat=cat, mix=mix, x1=x1, h2=h2, pre=pre,
                 act=act, f=f)
    return last_step(x1, f), saved


def _layer_bwd(dx2, wts, s, tables, ffact_exchanges=(), delta_exchanges=None, attn_exchanges=None):
    g = {}
    df, g["norm_post_ffn"] = _norm_bwd(s["f"], wts["norm_post_ffn"], dx2, None, "norm_post_ffn_bwd", BF16)
    dact = _matmul(df, wts["ffn_w_out"], tb=True, tm=SEQ, tn=1408, tk=1024, name="mm_dact", out_dtype=BF16)
    g["ffn_w_out"] = _matmul(s["act"], df, ta=True, tm=1408, tn=512, tk=SEQ, name="mm_dw_ffn_out", out_dtype=BF16)
    (dpre, g["ffn_conv_w"], g["ffn_conv_b"]), got = _ffact_bwd(s["pre"], wts["ffn_conv_w"], wts["ffn_conv_b"], dact,
                                                               exchanges=ffact_exchanges)
    dh2 = _matmul(dpre, wts["ffn_w_in"], tm=512, tn=1024, tk=2 * D_FF, name="mm_dh2")
    g["ffn_w_in"] = _matmul(dpre, s["h2"], ta=True, tm=512, tn=1024, tk=SEQ, name="mm_dw_ffn_in", out_dtype=BF16)
    dx1, g["norm_pre_ffn"] = _norm_bwd(s["x1"], wts["norm_pre_ffn"], dh2, dx2, "norm_pre_ffn_bwd")
    dmix, g["norm_post_mix"] = _norm_bwd(s["mix"], wts["norm_post_mix"], dx1, None, "norm_post_mix_bwd", BF16)
    dcat = _matmul(dmix, wts["w_out"], tb=True, tm=SEQ, tn=512, tk=1024, name="mm_dcat")
    g["w_out"] = _matmul(s["cat"], dmix, ta=True, tm=1024, tn=512, tk=SEQ, name="mm_dw_out", out_dtype=BF16)
    (dproj, dc, g["dn_a_log"], g["dn_dt_bias"], g["dn_norm_w"]), got = _delta_bwd(
        s["c_qkv"], s["proj"], wts["dn_a_log"], wts["dn_dt_bias"], wts["dn_norm_w"], s["states"], dcat,
        exchanges=delta_exchanges(g, got) if delta_exchanges is not None else ())
    dproj, got = _attn_bwd(s["proj"], *tables, s["cat"], s["lse"], dcat, dproj,
                           exchanges=attn_exchanges(got) if attn_exchanges is not None else ())
    dproj, g["dn_conv_w"] = _dnconv_bwd(s["proj"], wts["dn_conv_w"], dc, dproj)
    dh = _matmul(dproj, wts["w_in"], tm=512, tn=1024, tk=IN_PAD, name="mm_dh")
    g["w_in"] = _matmul(dproj, s["h"], ta=True, tm=768, tn=1024, tk=SEQ, name="mm_dw_in", out_dtype=BF16)
    dx, g["norm_pre_mix"] = _norm_bwd(s["x"], wts["norm_pre_mix"], dh, dx1, "norm_pre_mix_bwd")
    return dx, g, got


BIG = ("w_in", "w_out", "ffn_w_in", "ffn_w_out")
COLUMN_SHARDED = ("w_in", "ffn_w_in")
SMALL_SHARDED = ("dn_conv_w", "ffn_conv_w")
REPLICATED = ("dn_a_log", "dn_dt_bias", "dn_norm_w", "ffn_conv_b", "norm_pre_mix", "norm_post_mix", "norm_pre_ffn",
              "norm_post_ffn")
WEIGHTS = ("w_in", "dn_conv_w", "dn_a_log", "dn_dt_bias", "dn_norm_w", "w_out", "ffn_w_in", "ffn_conv_w", "ffn_conv_b",
           "ffn_w_out", "norm_pre_mix", "norm_post_mix", "norm_pre_ffn", "norm_post_ffn")
FULL_SHAPE = dict(dn_conv_w=(DEPTH, 4, 1536), ffn_conv_w=(DEPTH, 3, 2 * D_FF), dn_a_log=(DEPTH, NDH),
                  dn_dt_bias=(DEPTH, NDH), dn_norm_w=(DEPTH, 128), ffn_conv_b=(DEPTH, 2 * D_FF),
                  norm_pre_mix=(DEPTH, D_MODEL), norm_post_mix=(DEPTH, D_MODEL), norm_pre_ffn=(DEPTH, D_MODEL),
                  norm_post_ffn=(DEPTH, D_MODEL))
SMALL_GRAD_ORDER = REPLICATED + SMALL_SHARDED
SMALL_GRAD_ROWS = 544
SMALL_W_ROWS = 56
SMALL_ADAM_ROWS = 232


def _w_in_rows_to_kernel_order(t):
    qkv = t[:QKV_W].reshape(3, N_PAIR, 128, -1).swapaxes(0, 1).reshape(QKV_W, -1)
    return jnp.pad(jnp.concatenate([qkv, t[QKV_W:]], axis=0), ((0, IN_PAD - IN_COLS), (0, 0)))


def _w_in_rows_from_kernel_order(t):
    qkv = t[:QKV_W].reshape(N_PAIR, 3, 128, -1).swapaxes(0, 1).reshape(QKV_W, -1)
    return jnp.concatenate([qkv, t[QKV_W:IN_COLS]], axis=0)


def _interleave_ff_rows(t):
    return t.reshape(2, FF_BLKS, 128, -1).swapaxes(0, 1).reshape(2 * D_FF, -1)


def _deinterleave_ff_rows(t):
    return t.reshape(FF_BLKS, 2, 128, -1).swapaxes(0, 1).reshape(2 * D_FF, -1)


def kernel(x, w_in, dn_conv_w, dn_a_log, dn_dt_bias, dn_norm_w, w_out, ffn_w_in, ffn_conv_w, ffn_conv_b, ffn_w_out, norm_pre_mix, norm_post_mix, norm_pre_ffn, norm_post_ffn, loss_target, m_w_in, m_dn_conv_w, m_dn_a_log, m_dn_dt_bias, m_dn_norm_w, m_w_out, m_ffn_w_in, m_ffn_conv_w, m_ffn_conv_b, m_ffn_w_out, m_norm_pre_mix, m_norm_post_mix, m_norm_pre_ffn, m_norm_post_ffn, v_w_in, v_dn_conv_w, v_dn_a_log, v_dn_dt_bias, v_dn_norm_w, v_w_out, v_ffn_w_in, v_ffn_conv_w, v_ffn_conv_b, v_ffn_w_out, v_norm_pre_mix, v_norm_post_mix, v_norm_pre_ffn, v_norm_post_ffn):
    local = dict(w_in=w_in, dn_conv_w=dn_conv_w, dn_a_log=dn_a_log, dn_dt_bias=dn_dt_bias, dn_norm_w=dn_norm_w,
                 w_out=w_out, ffn_w_in=ffn_w_in, ffn_conv_w=ffn_conv_w, ffn_conv_b=ffn_conv_b, ffn_w_out=ffn_w_out,
                 norm_pre_mix=norm_pre_mix, norm_post_mix=norm_post_mix, norm_pre_ffn=norm_pre_ffn,
                 norm_post_ffn=norm_post_ffn)
    mom_m = dict(w_in=m_w_in, dn_conv_w=m_dn_conv_w, dn_a_log=m_dn_a_log, dn_dt_bias=m_dn_dt_bias,
                 dn_norm_w=m_dn_norm_w, w_out=m_w_out, ffn_w_in=m_ffn_w_in, ffn_conv_w=m_ffn_conv_w,
                 ffn_conv_b=m_ffn_conv_b, ffn_w_out=m_ffn_w_out, norm_pre_mix=m_norm_pre_mix,
                 norm_post_mix=m_norm_post_mix, norm_pre_ffn=m_norm_pre_ffn, norm_post_ffn=m_norm_post_ffn)
    mom_v = dict(w_in=v_w_in, dn_conv_w=v_dn_conv_w, dn_a_log=v_dn_a_log, dn_dt_bias=v_dn_dt_bias,
                 dn_norm_w=v_dn_norm_w, w_out=v_w_out, ffn_w_in=v_ffn_w_in, ffn_conv_w=v_ffn_conv_w,
                 ffn_conv_b=v_ffn_conv_b, ffn_w_out=v_ffn_w_out, norm_pre_mix=v_norm_pre_mix,
                 norm_post_mix=v_norm_post_mix, norm_pre_ffn=v_norm_pre_ffn, norm_post_ffn=v_norm_post_ffn)
    dev = 4 * lax.axis_index("x") + 2 * lax.axis_index("y") + lax.axis_index("c")
    core = lax.axis_index("c").astype(jnp.int32).reshape(1)

    def shard(n, l):
        s = local[n].transpose(0, 2, 1) if n in COLUMN_SHARDED else local[n]
        return s[l].astype(BF16)

    def matrix(n, gathered):
        if n == "w_in":
            return _w_in_rows_to_kernel_order(gathered.reshape(IN_COLS, D_MODEL))
        if n == "ffn_w_in":
            return _interleave_ff_rows(gathered.reshape(2 * D_FF, D_MODEL))
        return gathered.reshape(-1, D_MODEL)

    small_w = _pack([dn_conv_w, ffn_conv_w], SMALL_W_ROWS)
    g_w_in0, g_small = _run_exchange(_gather_exchange([shard("w_in", 0), small_w]), "weights_all_gather")
    n_dn, n_ff = DEPTH * 4 * 192, DEPTH * 3 * 704
    dn_rows = _packed_rows(n_dn)
    sm_dn = g_small[:, :dn_rows].reshape(N_DEV, -1)[:, :n_dn]
    sm_ff = g_small[:, dn_rows:].reshape(N_DEV, -1)[:, :n_ff]
    full_dn_conv = sm_dn.reshape(N_DEV, DEPTH, 4, 192).transpose(1, 2, 0, 3).reshape(DEPTH, 4, 1536)
    full_ff_conv = _interleave_ff(sm_ff.reshape(N_DEV, DEPTH, 3, 704).transpose(1, 2, 0, 3).reshape(DEPTH, 3, 2 * D_FF))

    def small_weights(l):
        wts = dict(dn_conv_w=full_dn_conv[l], ffn_conv_w=full_ff_conv[l], ffn_conv_b=_interleave_ff(_row(ffn_conv_b[l])),
                   dn_a_log=_row(dn_a_log[l], 128), dn_dt_bias=_row(dn_dt_bias[l], 128))
        for n in ("dn_norm_w", "norm_pre_mix", "norm_post_mix", "norm_pre_ffn", "norm_post_ffn"):
            wts[n] = _row(local[n][l])
        return wts

    weights = [small_weights(l) for l in range(DEPTH)]
    weights[0]["w_in"] = matrix("w_in", g_w_in0)

    def gather_behind(wanted):
        def deliver(got):
            for (n, l), g in zip(wanted, got[0]):
                weights[l][n] = matrix(n, g)

        return [_gather_exchange([shard(n, l) for n, l in wanted])], deliver

    nothing = ((), lambda got: None)

    tables = _rope_tables()
    h0 = _norm_fwd(x[0], weights[0]["norm_pre_mix"], "norm_pre_mix")
    (act, h1), saved0 = _layer_fwd(
        x[0], h0, weights[0], tables,
        dict(attn=gather_behind([("ffn_w_in", 0)]), delta=gather_behind([("w_out", 0), ("ffn_w_out", 0)]),
             ffact=gather_behind([("w_in", 1)])),
        lambda x1, f: _resnorm_norm_fwd(x1, f, weights[0]["norm_post_ffn"], weights[1]["norm_pre_mix"], "norm_post_ffn"))
    (loss_part, dact), saved1 = _layer_fwd(
        act, h1, weights[1], tables,
        dict(attn=gather_behind([("ffn_w_in", 1)]), delta=gather_behind([("w_out", 1), ("ffn_w_out", 1)]), ffact=nothing),
        lambda x1, f: _resnorm_loss(x1, f, weights[1]["norm_post_ffn"], loss_target[0]))

    def to_devices(name, t):
        if name == "w_in":
            t = _w_in_rows_from_kernel_order(t)
        if name == "ffn_w_in":
            t = _deinterleave_ff_rows(t)
        return t.reshape(N_DEV, t.shape[0] // N_DEV, t.shape[1])

    def pair_sums(names, layer, to_dev, from_sibling):
        return [_pair_add(gd, r, core, "grads_pair_add_%s_%d" % (n, layer))
                for n, gd, r in zip(names, to_dev, from_sibling)]

    early = ("w_out", "ffn_w_in", "ffn_w_out")
    grads, parts, stash = [None] * DEPTH, {}, {}

    def delta_exchanges1(g, got_ffact):
        stash["early1"] = [to_devices(n, g[n]) for n in early]
        return [_sibling_exchange(stash["early1"])]

    def attn_exchanges1(got_delta):
        return [_chips_exchange(pair_sums(early, 1, stash["early1"], got_delta[0]))]

    dact, grads[1], got_attn = _layer_bwd(dact, weights[1], saved1, tables, (), delta_exchanges1, attn_exchanges1)
    for n, p in zip(early, got_attn[0]):
        parts[n, 1] = p
    w_in1 = [to_devices("w_in", grads[1]["w_in"])]

    def delta_exchanges0(g, got_ffact):
        stash["early0"] = [to_devices(n, g[n]) for n in early]
        return [_chips_exchange(pair_sums(("w_in",), 1, w_in1, got_ffact[0])), _sibling_exchange(stash["early0"])]

    def attn_exchanges0(got_delta):
        parts["w_in", 1], = got_delta[0]
        return [_chips_exchange(pair_sums(early, 0, stash["early0"], got_delta[1]))]

    dact, grads[0], got_attn = _layer_bwd(dact, weights[0], saved0, tables, [_sibling_exchange(w_in1)],
                                          delta_exchanges0, attn_exchanges0)
    for n, p in zip(early, got_attn[0]):
        parts[n, 0] = p
    grad_x = dact[None]
    last = [to_devices("w_in", grads[0]["w_in"])]
    from_sibling = _run_exchange(_sibling_exchange(last), "grads_to_sibling")
    parts["w_in", 0], = _run_exchange(_chips_exchange(pair_sums(("w_in",), 0, last, from_sibling)), "grads_to_chips")

    def small_grad(name):
        t = jnp.stack([grads[l][name] for l in range(DEPTH)])
        if name in ("dn_a_log", "dn_dt_bias"):
            t = t[:, 0, :NDH]
        if name in ("ffn_conv_w", "ffn_conv_b"):
            t = _deinterleave_ff(t)
        return t.reshape(FULL_SHAPE[name])

    small_part = _pack([small_grad(n) for n in SMALL_GRAD_ORDER] + [loss_part[0, :1]], SMALL_GRAD_ROWS)
    small_sum = _all_gather_sum_small(small_part)
    small_g = dict(zip(SMALL_GRAD_ORDER + ("loss",), _unpack(small_sum, [FULL_SHAPE[n] for n in SMALL_GRAD_ORDER] + [(1,)])))
    loss = small_g["loss"][0]
    small_g["dn_conv_w"] = lax.dynamic_slice_in_dim(small_g["dn_conv_w"], dev * 192, 192, axis=2)
    small_g["ffn_conv_w"] = lax.dynamic_slice_in_dim(small_g["ffn_conv_w"], dev * 704, 704, axis=2)

    out_g, out_d, out_m, out_v = {}, {}, {}, {}
    for n in BIG:
        turn = (lambda t: t.transpose(0, 2, 1)) if n in COLUMN_SHARDED else (lambda t: t)
        outs = _adamw_sharded([parts[n, l] for l in range(DEPTH)], turn(local[n]), turn(mom_m[n]), turn(mom_v[n]),
                              ADAM_TILE[n], "adamw_" + n)
        out_g[n], out_d[n], out_m[n], out_v[n] = [turn(t) for t in outs]
    shapes = [small_g[n].shape for n in SMALL_GRAD_ORDER]
    d_s, m_s, v_s = _adamw_small(_pack([small_g[n] for n in SMALL_GRAD_ORDER], SMALL_ADAM_ROWS),
                                 _pack([local[n] for n in SMALL_GRAD_ORDER], SMALL_ADAM_ROWS),
                                 _pack([mom_m[n] for n in SMALL_GRAD_ORDER], SMALL_ADAM_ROWS),
                                 _pack([mom_v[n] for n in SMALL_GRAD_ORDER], SMALL_ADAM_ROWS))
    for n, d, m, v in zip(SMALL_GRAD_ORDER, _unpack(d_s, shapes), _unpack(m_s, shapes), _unpack(v_s, shapes)):
        out_g[n], out_d[n], out_m[n], out_v[n] = small_g[n], d, m, v
    return (loss, grad_x, *[out_g[n] for n in WEIGHTS], *[out_d[n] for n in WEIGHTS],
            *[out_m[n] for n in WEIGHTS], *[out_v[n] for n in WEIGHTS])
```

```python
import functools
import math

import jax
import jax.numpy as jnp
from jax import lax
from jax.experimental import pallas as pl
from jax.experimental.pallas import tpu as pltpu

F32 = jnp.float32
BF16 = jnp.bfloat16
MESH = pl.DeviceIdType.MESH

N_DEV = 8
SEQ = 2048
D_MODEL = 1024
DEPTH = 2
N_PAIR = 4
HEAD_DIM = 64
ATTN_W = 512
ATTN_BLK = 128
DILATIONS = (1, 4, 16)
SEGMENT_BLOCKS = (16, 4, 1)
N_BLK = SEQ // ATTN_BLK
NDH = 4
CH = 64
NCH = SEQ // CH
IN_COLS = 3592
IN_PAD = 3840
QKV_W = 3 * ATTN_W
DN_QKV_BLK0 = QKV_W // 128
DN_QKV_BLKS = 1536 // 128
DN_Z_COL = 3072
DN_TAIL_BLK = 3584 // 128
D_FF = 2816
FF_BLKS = D_FF // 128
EPS = 1e-6
NEG = -1e30
ROPE_THETA = 10000.0

ADAM_LR, ADAM_B1, ADAM_B2, ADAM_EPS, ADAM_WD, ADAM_STEP = 0.001, 0.9, 0.999, 1e-08, 0.01, 10

VMEM_LIMIT = 56 * 1024 * 1024


def _cp(*sem):
    return pltpu.CompilerParams(dimension_semantics=sem, vmem_limit_bytes=VMEM_LIMIT)


class Exchange:
    def __init__(self, operands, out_shapes, sems, start, middle, finish):
        self.operands, self.out_shapes, self.sems = list(operands), list(out_shapes), list(sems)
        self.start, self.middle, self.finish = start, middle, finish


HBM_SPEC = pl.BlockSpec(memory_space=pltpu.HBM)


def _hosted_call(body, *, name, steps, in_specs, out_specs, out_shape, scratch_shapes, operands, exchanges=(),
                 aliases=None):
    n_in, n_out, n_scr = len(in_specs), len(out_specs), len(scratch_shapes)

    def take(refs, pos, counts):
        groups = []
        for c in counts:
            groups.append(refs[pos:pos + c])
            pos += c
        return groups, pos

    def full_body(*refs):
        ins, pos = refs[:n_in], n_in
        ex_ins, pos = take(refs, pos, [len(e.operands) for e in exchanges])
        outs, pos = refs[pos:pos + n_out], pos + n_out
        ex_outs, pos = take(refs, pos, [len(e.out_shapes) for e in exchanges])
        scr, pos = refs[pos:pos + n_scr], pos + n_scr
        ex_sems, pos = take(refs, pos, [len(e.sems) for e in exchanges])
        step = pl.program_id(0)
        for e, a, b, s in zip(exchanges, ex_ins, ex_outs, ex_sems):
            pl.when(step == 0)(functools.partial(e.start, a, b, s))
            if e.middle is not None:
                pl.when(step == (3 * steps) // 4)(functools.partial(e.middle, a, b, s))
        body(*ins, *outs, *scr)
        for e, a, b, s in zip(exchanges, ex_ins, ex_outs, ex_sems):
            pl.when(step == steps - 1)(functools.partial(e.finish, a, b, s))

    n_ex_in = sum(len(e.operands) for e in exchanges)
    n_ex_out = sum(len(e.out_shapes) for e in exchanges)
    results = pl.pallas_call(
        full_body, name=name, grid=(steps,),
        in_specs=list(in_specs) + [HBM_SPEC] * n_ex_in,
        out_specs=list(out_specs) + [HBM_SPEC] * n_ex_out,
        out_shape=list(out_shape) + [s for e in exchanges for s in e.out_shapes],
        scratch_shapes=list(scratch_shapes) + [s for e in exchanges for s in e.sems],
        input_output_aliases=aliases or {},
        compiler_params=_cp("arbitrary"),
    )(*operands, *[a for e in exchanges for a in e.operands])
    ex_results, _ = take(results, n_out, [len(e.out_shapes) for e in exchanges])
    return results[:n_out], ex_results


def _dot(a, b, dims, precision=None):
    if precision is None:
        a = a.astype(BF16)
        b = b.astype(BF16)
    return lax.dot_general(a, b, (dims, ((), ())), preferred_element_type=F32, precision=precision)


def _make_mm(precision):
    @jax.custom_vjp
    def nn(a, b):
        return _dot(a, b, ((1,), (0,)), precision)

    @jax.custom_vjp
    def nt(a, b):
        return _dot(a, b, ((1,), (1,)), precision)

    @jax.custom_vjp
    def tn(a, b):
        return _dot(a, b, ((0,), (0,)), precision)

    nn.defvjp(lambda a, b: (nn(a, b), (a, b)), lambda r, g: (nt(g, r[1]), tn(r[0], g)))
    nt.defvjp(lambda a, b: (nt(a, b), (a, b)), lambda r, g: (nn(g, r[1]), tn(g, r[0])))
    tn.defvjp(lambda a, b: (tn(a, b), (a, b)), lambda r, g: (nt(r[1], g), nn(r[0], g)))
    return nn, nt, tn


def _matmul(a, b, *, ta=False, tb=False, tm, tn, tk, name, out_dtype=F32):
    (k_dim, m_dim) = a.shape if ta else a.shape[::-1]
    (n_dim, k2) = b.shape if tb else b.shape[::-1]
    assert k_dim == k2 and m_dim % tm == 0 and n_dim % tn == 0 and k_dim % tk == 0, (a.shape, b.shape, tm, tn, tk)
    nk = k_dim // tk
    dims = ((0 if ta else 1,), (1 if tb else 0,))

    def body(a_ref, b_ref, o_ref, *acc):
        p = _dot(a_ref[...], b_ref[...], dims)
        if nk == 1:
            o_ref[...] = p.astype(out_dtype)
            return
        acc_ref, k = acc[0], pl.program_id(2)

        @pl.when(k == 0)
        def _():
            acc_ref[...] = p

        @pl.when(k > 0)
        def _():
            acc_ref[...] += p

        @pl.when(k == nk - 1)
        def _():
            o_ref[...] = acc_ref[...].astype(out_dtype)

    a_spec = pl.BlockSpec((tk, tm), lambda i, j, k: (k, i)) if ta else pl.BlockSpec((tm, tk), lambda i, j, k: (i, k))
    b_spec = pl.BlockSpec((tn, tk), lambda i, j, k: (j, k)) if tb else pl.BlockSpec((tk, tn), lambda i, j, k: (k, j))
    return pl.pallas_call(
        body, name=name,
        grid=(m_dim // tm, n_dim // tn, nk),
        in_specs=[a_spec, b_spec],
        out_specs=pl.BlockSpec((tm, tn), lambda i, j, k: (i, j)),
        out_shape=jax.ShapeDtypeStruct((m_dim, n_dim), out_dtype),
        scratch_shapes=[pltpu.VMEM((tm, tn), F32)] if nk > 1 else [],
        compiler_params=_cp("parallel", "parallel", "arbitrary"),
    )(a, b)


NORM_ROWS = 256


def _rms(x, w):
    return x * lax.rsqrt(jnp.mean(x * x, axis=1, keepdims=True) + EPS) * w


def _norm_fwd(x, w_row, name, out_dtype=BF16):
    def body(x_ref, w_ref, o_ref):
        o_ref[...] = _rms(x_ref[...], w_ref[...]).astype(out_dtype)

    return pl.pallas_call(
        body, name=name, grid=(SEQ // NORM_ROWS,),
        in_specs=[pl.BlockSpec((NORM_ROWS, D_MODEL), lambda i: (i, 0)), pl.BlockSpec((1, D_MODEL), lambda i: (0, 0))],
        out_specs=pl.BlockSpec((NORM_ROWS, D_MODEL), lambda i: (i, 0)),
        out_shape=jax.ShapeDtypeStruct((SEQ, D_MODEL), out_dtype),
        compiler_params=_cp("parallel"),
    )(x, w_row)


def _resnorm_norm_fwd(x, f, w_row, next_w_row, name):
    def body(x_ref, f_ref, w_ref, nw_ref, o_ref, h_ref):
        out = x_ref[...] + _rms(f_ref[...], w_ref[...])
        o_ref[...] = out
        h_ref[...] = _rms(out, nw_ref[...]).astype(BF16)

    blk = pl.BlockSpec((NORM_ROWS, D_MODEL), lambda i: (i, 0))
    row = pl.BlockSpec((1, D_MODEL), lambda i: (0, 0))
    return pl.pallas_call(
        body, name=name, grid=(SEQ // NORM_ROWS,),
        in_specs=[blk, blk, row, row],
        out_specs=[blk, blk],
        out_shape=[jax.ShapeDtypeStruct((SEQ, D_MODEL), F32), jax.ShapeDtypeStruct((SEQ, D_MODEL), BF16)],
        compiler_params=_cp("parallel"),
    )(x, f, w_row, next_w_row)


def _norm_bwd(x, w_row, dy, add, name, dx_dtype=F32):
    has_add = add is not None

    def body(*refs):
        if has_add:
            x_ref, w_ref, dy_ref, add_ref, dx_ref, dw_ref = refs
        else:
            x_ref, w_ref, dy_ref, dx_ref, dw_ref = refs
        _, vjp = jax.vjp(_rms, x_ref[...], w_ref[...])
        dx, dw = vjp(dy_ref[...])
        dx_ref[...] = (dx + add_ref[...] if has_add else dx).astype(dx_dtype)

        @pl.when(pl.program_id(0) == 0)
        def _():
            dw_ref[...] = jnp.zeros_like(dw_ref)

        dw_ref[...] += dw

    blk = pl.BlockSpec((NORM_ROWS, D_MODEL), lambda i: (i, 0))
    row = pl.BlockSpec((1, D_MODEL), lambda i: (0, 0))
    ins = [x, w_row, dy] + ([add] if has_add else [])
    return pl.pallas_call(
        body, name=name, grid=(SEQ // NORM_ROWS,),
        in_specs=[blk, row, blk] + ([blk] if has_add else []),
        out_specs=[blk, row],
        out_shape=[jax.ShapeDtypeStruct((SEQ, D_MODEL), dx_dtype), jax.ShapeDtypeStruct((1, D_MODEL), F32)],
        compiler_params=_cp("arbitrary"),
    )(*ins)


def _norm_bwd_pair(x_a, w_a, dy_a, add, x_b, w_b, name):
    def body(xa_ref, wa_ref, dya_ref, add_ref, xb_ref, wb_ref, dxa_ref, dxb_ref, dwa_ref, dwb_ref):
        _, vjp_a = jax.vjp(_rms, xa_ref[...], wa_ref[...])
        dxa, dwa = vjp_a(dya_ref[...])
        dxa = dxa + add_ref[...]
        _, vjp_b = jax.vjp(_rms, xb_ref[...], wb_ref[...])
        dxb, dwb = vjp_b(dxa)
        dxa_ref[...] = dxa
        dxb_ref[...] = dxb.astype(BF16)

        @pl.when(pl.program_id(0) == 0)
        def _():
            dwa_ref[...] = jnp.zeros_like(dwa_ref)
            dwb_ref[...] = jnp.zeros_like(dwb_ref)

        dwa_ref[...] += dwa
        dwb_ref[...] += dwb

    blk = pl.BlockSpec((NORM_ROWS, D_MODEL), lambda i: (i, 0))
    row = pl.BlockSpec((1, D_MODEL), lambda i: (0, 0))
    return pl.pallas_call(
        body, name=name, grid=(SEQ // NORM_ROWS,),
        in_specs=[blk, row, blk, blk, blk, row],
        out_specs=[blk, blk, row, row],
        out_shape=[jax.ShapeDtypeStruct((SEQ, D_MODEL), F32), jax.ShapeDtypeStruct((SEQ, D_MODEL), BF16),
                   jax.ShapeDtypeStruct((1, D_MODEL), F32), jax.ShapeDtypeStruct((1, D_MODEL), F32)],
        compiler_params=_cp("arbitrary"),
    )(x_a, w_a, dy_a, add, x_b, w_b)


def _resnorm_loss(x, f, w_row, target):
    def body(x_ref, f_ref, w_ref, t_ref, loss_ref, dy_ref):
        err = x_ref[...] + _rms(f_ref[...], w_ref[...]) - t_ref[...]
        dy_ref[...] = err * (1.0 / D_MODEL)

        @pl.when(pl.program_id(0) == 0)
        def _():
            loss_ref[...] = jnp.zeros_like(loss_ref)

        part = jnp.sum(jnp.sum(err * err, axis=1, keepdims=True) * (1.0 / D_MODEL), axis=0, keepdims=True)
        loss_ref[...] += 0.5 * jnp.broadcast_to(part, loss_ref.shape)

    blk = pl.BlockSpec((NORM_ROWS, D_MODEL), lambda i: (i, 0))
    return pl.pallas_call(
        body, name="norm_post_ffn_loss", grid=(SEQ // NORM_ROWS,),
        in_specs=[blk, blk, pl.BlockSpec((1, D_MODEL), lambda i: (0, 0)), blk],
        out_specs=[pl.BlockSpec((1, 128), lambda i: (0, 0)), blk],
        out_shape=[jax.ShapeDtypeStruct((1, 128), F32), jax.ShapeDtypeStruct((SEQ, D_MODEL), F32)],
        compiler_params=_cp("arbitrary"),
    )(x, f, w_row, target)


def _make_shift(j):
    def down(x):
        row = lax.broadcasted_iota(jnp.int32, x.shape, 0)
        return jnp.where(row >= j, pltpu.roll(x, j, 0), 0.0)

    def up(x):
        n = x.shape[0]
        row = lax.broadcasted_iota(jnp.int32, x.shape, 0)
        return jnp.where(row < n - j, pltpu.roll(x, n - j, 0), 0.0)

    f = jax.custom_vjp(down)
    f.defvjp(lambda x: (down(x), None), lambda _, g: (up(g),))
    return f


_SHIFT = {j: _make_shift(j) for j in (1, 2, 3)}


def _causal_conv(x, taps):
    n = len(taps)
    acc = x * taps[n - 1]
    for k in range(n - 1):
        acc = acc + _SHIFT[n - 1 - k](x) * taps[k]
    return acc


def _tap_rows(w_ref, lanes=slice(None)):
    return tuple(w_ref[k:k + 1, lanes] for k in range(w_ref.shape[0]))


def _sigmoid(x):
    return 1.0 / (1.0 + jnp.exp(-x))


def _silu(x):
    return x * _sigmoid(x)


def _softplus(x):
    return jnp.maximum(x, 0.0) + jnp.log(1.0 + jnp.exp(-jnp.abs(x)))


def _gelu_tanh(x):
    return 0.5 * x * (1.0 + jnp.tanh(math.sqrt(2.0 / math.pi) * (x + 0.044715 * (x * x * x))))


def _dnconv_fn(x, taps):
    return _silu(_causal_conv(x, taps))


def _dnconv_fwd(proj, conv_w):
    def body(x_ref, w_ref, o_ref):
        o_ref[...] = _dnconv_fn(x_ref[...], _tap_rows(w_ref)).astype(BF16)

    return pl.pallas_call(
        body, name="dnconv_fwd", grid=(DN_QKV_BLKS,),
        in_specs=[pl.BlockSpec((SEQ, 128), lambda j: (0, DN_QKV_BLK0 + j)), pl.BlockSpec((4, 128), lambda j: (0, j))],
        out_specs=pl.BlockSpec((SEQ, 128), lambda j: (0, j)),
        out_shape=jax.ShapeDtypeStruct((SEQ, 1536), BF16),
        compiler_params=_cp("parallel"),
    )(proj, conv_w)


def _dnconv_bwd(proj, conv_w, dc, dproj):
    def body(x_ref, w_ref, dc_ref, _, dx_ref, dw_ref):
        _, vjp = jax.vjp(_dnconv_fn, x_ref[...], _tap_rows(w_ref))
        dx, dw = vjp(dc_ref[...])
        dx_ref[...] = dx.astype(BF16)
        for k, row in enumerate(dw):
            dw_ref[k:k + 1, :] = row

    return pl.pallas_call(
        body, name="dnconv_bwd", grid=(DN_QKV_BLKS,),
        in_specs=[pl.BlockSpec((SEQ, 128), lambda j: (0, DN_QKV_BLK0 + j)), pl.BlockSpec((4, 128), lambda j: (0, j)),
                  pl.BlockSpec((SEQ, 128), lambda j: (0, j)), pl.BlockSpec(memory_space=pl.ANY)],
        out_specs=[pl.BlockSpec((SEQ, 128), lambda j: (0, DN_QKV_BLK0 + j)), pl.BlockSpec((4, 128), lambda j: (0, j))],
        out_shape=[jax.ShapeDtypeStruct((SEQ, IN_PAD), BF16), jax.ShapeDtypeStruct((4, 1536), F32)],
        input_output_aliases={3: 0},
        compiler_params=_cp("parallel"),
    )(proj, conv_w, dc, dproj)


def _ffact_fn(pg, pu, wg, wu, bg, bu):
    return _gelu_tanh(_causal_conv(pg, wg) + bg) * (_causal_conv(pu, wu) + bu)


def _ffact_args(p_ref, w_ref, b_ref):
    g, u = slice(0, 128), slice(128, 256)
    return (p_ref[:, g].astype(F32), p_ref[:, u].astype(F32), _tap_rows(w_ref, g), _tap_rows(w_ref, u),
            b_ref[:, g], b_ref[:, u])


def _ffact_fwd(pre, conv_w, conv_b, exchanges=()):
    def body(p_ref, w_ref, b_ref, o_ref):
        o_ref[...] = _ffact_fn(*_ffact_args(p_ref, w_ref, b_ref)).astype(BF16)

    (act,), results = _hosted_call(
        body, name="ffact_fwd", steps=FF_BLKS,
        in_specs=[pl.BlockSpec((SEQ, 256), lambda j: (0, j)), pl.BlockSpec((3, 256), lambda j: (0, j)),
                  pl.BlockSpec((1, 256), lambda j: (0, j))],
        out_specs=[pl.BlockSpec((SEQ, 128), lambda j: (0, j))],
        out_shape=[jax.ShapeDtypeStruct((SEQ, D_FF), BF16)],
        scratch_shapes=[], operands=(pre, conv_w, conv_b), exchanges=exchanges)
    return act, results


def _ffact_bwd(pre, conv_w, conv_b, dact, exchanges=()):
    def body(p_ref, w_ref, b_ref, da_ref, dp_ref, dw_ref, db_ref):
        _, vjp = jax.vjp(_ffact_fn, *_ffact_args(p_ref, w_ref, b_ref))
        dpg, dpu, dwg, dwu, dbg, dbu = vjp(da_ref[...].astype(F32))
        dp_ref[:, 0:128] = dpg.astype(BF16)
        dp_ref[:, 128:256] = dpu.astype(BF16)
        for k in range(3):
            dw_ref[k:k + 1, 0:128] = dwg[k]
            dw_ref[k:k + 1, 128:256] = dwu[k]
        db_ref[:, 0:128] = dbg
        db_ref[:, 128:256] = dbu

    return _hosted_call(
        body, name="ffact_bwd", steps=FF_BLKS,
        in_specs=[pl.BlockSpec((SEQ, 256), lambda j: (0, j)), pl.BlockSpec((3, 256), lambda j: (0, j)),
                  pl.BlockSpec((1, 256), lambda j: (0, j)), pl.BlockSpec((SEQ, 128), lambda j: (0, j))],
        out_specs=[pl.BlockSpec((SEQ, 256), lambda j: (0, j)), pl.BlockSpec((3, 256), lambda j: (0, j)),
                   pl.BlockSpec((1, 256), lambda j: (0, j))],
        out_shape=[jax.ShapeDtypeStruct((SEQ, 2 * D_FF), BF16), jax.ShapeDtypeStruct((3, 2 * D_FF), F32),
                   jax.ShapeDtypeStruct((1, 2 * D_FF), F32)],
        scratch_shapes=[], operands=(pre, conv_w, conv_b, dact), exchanges=exchanges)


def _interleave_ff(t):
    lead = t.shape[:-1]
    return t.reshape(lead + (2, FF_BLKS, 128)).swapaxes(-3, -2).reshape(lead + (2 * D_FF,))


def _deinterleave_ff(t):
    lead = t.shape[:-1]
    return t.reshape(lead + (FF_BLKS, 2, 128)).swapaxes(-3, -2).reshape(lead + (2 * D_FF,))


def _rope_tables():
    inv = 1.0 / (ROPE_THETA ** (jnp.arange(0, HEAD_DIM, 2, dtype=F32) / HEAD_DIM))
    ang = jnp.arange(SEQ, dtype=F32)[:, None] * inv[None, :]
    cos = jnp.tile(jnp.cos(ang), (1, 4))
    sin = jnp.tile(jnp.sin(ang), (1, 4))
    sign = jnp.where((jnp.arange(128) % HEAD_DIM) < HEAD_DIM // 2, -1.0, 1.0).astype(F32)
    return cos, sin * sign[None, :]


def _rope(x, cos, sin_signed):
    lane = lax.broadcasted_iota(jnp.int32, x.shape, 1)
    partner = jnp.where((lane % HEAD_DIM) < HEAD_DIM // 2, pltpu.roll(x, 128 - HEAD_DIM // 2, 1),
                        pltpu.roll(x, HEAD_DIM // 2, 1))
    return x * cos + partner * sin_signed


def _head_masks():
    lane = lax.broadcasted_iota(jnp.int32, (1, 128), 1)
    return [(lane // HEAD_DIM) == h for h in range(2)]


def _both_heads(x):
    return jnp.concatenate([jnp.where(hm, x, 0.0)[None] for hm in _head_masks()], axis=0)


def _block_keys(branch, k_s, v_s, rows, prows, has_prev):
    a = lax.broadcasted_iota(jnp.int32, (ATTN_BLK, ATTN_BLK), 0)
    c = lax.broadcasted_iota(jnp.int32, (ATTN_BLK, ATTN_BLK), 1)
    keys, values, mask = k_s[rows, :], v_s[rows, :], c <= a
    if SEGMENT_BLOCKS[branch] > 1:
        keys = jnp.concatenate([k_s[prows, :], keys], axis=0)
        values = jnp.concatenate([v_s[prows, :], values], axis=0)
        mask = jnp.concatenate([(c >= a) & has_prev, mask], axis=1)
    twice = lambda t: jnp.broadcast_to(t[None], (2,) + t.shape)
    return twice(keys), twice(values), mask


def _block_rows(branch, t):
    d, per_seg = DILATIONS[branch], SEGMENT_BLOCKS[branch]
    if d == 1:
        start = pl.multiple_of(t * ATTN_BLK, ATTN_BLK)
        prev = pl.multiple_of(jnp.maximum(t - 1, 0) * ATTN_BLK, ATTN_BLK)
        return pl.ds(start, ATTN_BLK), pl.ds(prev, ATTN_BLK), t > 0
    r, n = t // per_seg, t % per_seg
    start = n * (ATTN_BLK * d) + r
    prev = jnp.maximum(n - 1, 0) * (ATTN_BLK * d) + r
    return pl.ds(start, ATTN_BLK, stride=d), pl.ds(prev, ATTN_BLK, stride=d), n > 0


def _attn_fwd(proj, cos, sin_signed, exchanges=()):
    scale = HEAD_DIM ** -0.5

    def body(qkv_ref, cos_ref, sin_ref, out_ref, lse_ref, q_s, k_s, v_s, *branch_s):
        o_s, l_s = branch_s[:3], branch_s[3:]
        q_s[...] = _rope(qkv_ref[:, 0:128], cos_ref[...], sin_ref[...])
        k_s[...] = _rope(qkv_ref[:, 128:256], cos_ref[...], sin_ref[...])
        v_s[...] = qkv_ref[:, 256:384]
        heads = _head_masks()
        for branch in range(3):
            def block(t, carry, branch=branch):
                rows, prows, has_prev = _block_rows(branch, t)
                keys, values, mask = _block_keys(branch, k_s, v_s, rows, prows, has_prev)
                s = jnp.where(mask, BMM_NT(_both_heads(q_s[rows, :]), keys) * scale, NEG)
                m = jnp.max(s, axis=2, keepdims=True)
                e = jnp.exp(s - m)
                l = jnp.sum(e, axis=2, keepdims=True)
                o = BMM(e, values) / l
                lse_b = m + jnp.log(l)
                o_s[branch][rows, :] = jnp.where(heads[0], o[0], o[1])
                l_s[branch][rows, :] = jnp.where(heads[0], lse_b[0], lse_b[1])
                return carry

            lax.fori_loop(0, N_BLK, block, 0, unroll=4)
        l0, l1, l2 = l_s[0][...], l_s[1][...], l_s[2][...]
        m = jnp.maximum(jnp.maximum(l0, l1), l2)
        w0, w1, w2 = jnp.exp(l0 - m), jnp.exp(l1 - m), jnp.exp(l2 - m)
        den = w0 + w1 + w2
        out_ref[...] = (w0 * o_s[0][...] + w1 * o_s[1][...] + w2 * o_s[2][...]) / den
        lse_ref[...] = m + jnp.log(den)

    tab = pl.BlockSpec((SEQ, 128), lambda j: (0, 0))
    col = pl.BlockSpec((SEQ, 128), lambda j: (0, j))
    return _hosted_call(
        body, name="attn_fwd", steps=N_PAIR,
        in_specs=[pl.BlockSpec((SEQ, 384), lambda j: (0, j)), tab, tab],
        out_specs=[col, col],
        out_shape=[jax.ShapeDtypeStruct((SEQ, 2 * ATTN_W), F32), jax.ShapeDtypeStruct((SEQ, ATTN_W), F32)],
        scratch_shapes=[pltpu.VMEM((SEQ, 128), F32)] * 9,
        operands=(proj, cos, sin_signed), exchanges=exchanges)


def _attn_bwd(proj, cos, sin_signed, cat, lse, dcat, dproj, exchanges=()):
    scale = HEAD_DIM ** -0.5

    def body(qkv_ref, cos_ref, sin_ref, out_ref, lse_ref, do_ref, _, dqkv_ref, q_s, k_s, v_s, dq_s, dk_s, dv_s,
             dod_s):
        q_s[...] = _rope(qkv_ref[:, 0:128], cos_ref[...], sin_ref[...])
        k_s[...] = _rope(qkv_ref[:, 128:256], cos_ref[...], sin_ref[...])
        v_s[...] = qkv_ref[:, 256:384]
        dq_s[...] = jnp.zeros_like(dq_s)
        dk_s[...] = jnp.zeros_like(dk_s)
        dv_s[...] = jnp.zeros_like(dv_s)
        dod_s[...] = do_ref[...] * out_ref[...]
        heads = _head_masks()
        for branch in range(3):
            def block(t, carry, branch=branch):
                rows, prows, has_prev = _block_rows(branch, t)
                keys, values, mask = _block_keys(branch, k_s, v_s, rows, prows, has_prev)
                q2, do2 = _both_heads(q_s[rows, :]), _both_heads(do_ref[rows, :])
                lse_b, dod = lse_ref[rows, :], dod_s[rows, :]
                lse2 = jnp.concatenate(
                    [jnp.max(jnp.where(hm, lse_b, NEG), axis=1, keepdims=True)[None] for hm in heads], axis=0)
                delta = jnp.concatenate(
                    [jnp.sum(jnp.where(hm, dod, 0.0), axis=1, keepdims=True)[None] for hm in heads], axis=0)
                p = jnp.exp(jnp.where(mask, BMM_NT(q2, keys) * scale, NEG) - lse2)
                ds = p * (BMM_NT(do2, values) - delta) * scale
                dq = BMM(ds, keys)
                dk = BMM_TN(ds, q2)
                dv = BMM_TN(p, do2)
                dk, dv = dk[0] + dk[1], dv[0] + dv[1]
                dq_s[rows, :] += jnp.where(heads[0], dq[0], dq[1])
                if SEGMENT_BLOCKS[branch] > 1:
                    dk_s[rows, :] += dk[ATTN_BLK:]
                    dv_s[rows, :] += dv[ATTN_BLK:]

                    @pl.when(has_prev)
                    def _():
                        dk_s[prows, :] += dk[:ATTN_BLK]
                        dv_s[prows, :] += dv[:ATTN_BLK]
                else:
                    dk_s[rows, :] += dk
                    dv_s[rows, :] += dv
                return carry

            lax.fori_loop(0, N_BLK, block, 0, unroll=4)
        dqkv_ref[:, 0:128] = _rope(dq_s[...], cos_ref[...], -sin_ref[...]).astype(BF16)
        dqkv_ref[:, 128:256] = _rope(dk_s[...], cos_ref[...], -sin_ref[...]).astype(BF16)
        dqkv_ref[:, 256:384] = dv_s[...].astype(BF16)

    tab = pl.BlockSpec((SEQ, 128), lambda j: (0, 0))
    col = pl.BlockSpec((SEQ, 128), lambda j: (0, j))
    qkv = pl.BlockSpec((SEQ, 384), lambda j: (0, j))
    (dproj,), results = _hosted_call(
        body, name="attn_bwd", steps=N_PAIR,
        in_specs=[qkv, tab, tab, col, col, col, pl.BlockSpec(memory_space=pl.ANY)],
        out_specs=[qkv],
        out_shape=[jax.ShapeDtypeStruct((SEQ, IN_PAD), BF16)],
        scratch_shapes=[pltpu.VMEM((SEQ, 128), F32)] * 7,
        operands=(proj, cos, sin_signed, cat, lse, dcat, dproj), exchanges=exchanges, aliases={6: 0})
    return dproj, results


def _bdot(a, b, dims, precision=None):
    if precision is None:
        a = a.astype(BF16)
        b = b.astype(BF16)
    return lax.dot_general(a, b, (dims, ((0,), (0,))), preferred_element_type=F32, precision=precision)


def _make_bmm(precision):
    @jax.custom_vjp
    def nn(a, b):
        return _bdot(a, b, ((2,), (1,)), precision)

    @jax.custom_vjp
    def nt(a, b):
        return _bdot(a, b, ((2,), (2,)), precision)

    @jax.custom_vjp
    def tn(a, b):
        return _bdot(a, b, ((1,), (1,)), precision)

    nn.defvjp(lambda a, b: (nn(a, b), (a, b)), lambda r, g: (nt(g, r[1]), tn(r[0], g)))
    nt.defvjp(lambda a, b: (nt(a, b), (a, b)), lambda r, g: (nn(g, r[1]), tn(g, r[0])))
    tn.defvjp(lambda a, b: (tn(a, b), (a, b)), lambda r, g: (nt(r[1], g), nn(r[0], g)))
    return nn, nt, tn


BMM, BMM_NT, BMM_TN = _make_bmm(None)
BMM3, BMM3_NT, BMM3_TN = _make_bmm(lax.Precision.HIGH)
MM3, _, _ = _make_mm(lax.Precision.HIGH)


def _head_lanes(t, off):
    lane = lax.broadcasted_iota(jnp.int32, (1, 128), 1)
    return jnp.concatenate(
        [jnp.sum(t * (lane == off + h).astype(F32), axis=1, keepdims=True)[None] for h in range(NDH)], axis=0)


@jax.custom_vjp
def _unit_lower_inverse(a_mat):
    c = a_mat.shape[1]
    eye = (lax.broadcasted_iota(jnp.int32, (c, c), 0) == lax.broadcasted_iota(jnp.int32, (c, c), 1)).astype(F32)
    power = -a_mat
    t_inv = eye + power
    for _ in range(5):
        power = BMM3(power, power)
        t_inv = t_inv + BMM3(t_inv, power)
    return t_inv


def _unit_lower_inverse_fwd(a_mat):
    t_inv = _unit_lower_inverse(a_mat)
    return t_inv, t_inv


def _unit_lower_inverse_bwd(t_inv, d_inv):
    return (-BMM3_NT(BMM3_TN(t_inv, d_inv), t_inv),)


_unit_lower_inverse.defvjp(_unit_lower_inverse_fwd, _unit_lower_inverse_bwd)


DN_STEP_CHUNKS = 4
DN_STEP_ROWS = DN_STEP_CHUNKS * CH
DN_STEPS = NCH // DN_STEP_CHUNKS
DN_BATCH = DN_STEP_CHUNKS * NDH


def _delta_chunks(qr, kr, vr, z, tail, alog_row, dt_row, nw, state):
    c = qr.shape[1]
    tails = [tail[CH * n:CH * (n + 1)] for n in range(DN_STEP_CHUNKS)]
    per_chunk = lambda t: jnp.concatenate([t] * DN_STEP_CHUNKS, axis=0)
    beta = _sigmoid(jnp.concatenate([_head_lanes(t, 0) for t in tails], axis=0))
    a_raw = jnp.concatenate([_head_lanes(t, NDH) for t in tails], axis=0)
    g = -jnp.exp(per_chunk(_head_lanes(alog_row, 0))) * _softplus(a_raw + per_chunk(_head_lanes(dt_row, 0)))

    q = qr * lax.rsqrt(jnp.sum(qr * qr, axis=2, keepdims=True) + EPS) * (128 ** -0.5)
    k = kr * lax.rsqrt(jnp.sum(kr * kr, axis=2, keepdims=True) + EPS)

    ri = lax.broadcasted_iota(jnp.int32, (c, c), 0)
    ci = lax.broadcasted_iota(jnp.int32, (c, c), 1)
    tril = ri >= ci
    lane = lax.broadcasted_iota(jnp.int32, (1, 128), 1)
    pick = [(lane == b).astype(F32) for b in range(DN_BATCH)]
    g_lanes = sum(g[b] * pick[b] for b in range(DN_BATCH))
    g_sums = MM3(tril.astype(F32), g_lanes)
    gc = jnp.concatenate([jnp.sum(g_sums * pick[b], axis=1, keepdims=True)[None] for b in range(DN_BATCH)],
                         axis=0)
    g_row = jnp.swapaxes(jnp.broadcast_to(gc, (DN_BATCH, c, c)), 1, 2)
    decay = jnp.where(tril, jnp.exp(jnp.where(tril, gc - g_row, 0.0)), 0.0)
    kb = k * beta
    t_inv = _unit_lower_inverse(jnp.where(ri > ci, BMM_NT(kb, k) * decay, 0.0))
    eg = jnp.exp(gc)
    u = BMM(t_inv, vr * beta)
    w = BMM(t_inv, kb * eg)
    qk = BMM_NT(q, k) * decay
    g_tot = jnp.sum(g, axis=1, keepdims=True)
    q_dec = q * eg
    k_dec = k * jnp.exp(g_tot - gc)
    outs = []
    for n in range(DN_STEP_CHUNKS):
        heads = slice(NDH * n, NDH * (n + 1))
        v_new = u[heads] - BMM(w[heads], state)
        outs.append(BMM(q_dec[heads], state) + BMM(qk[heads], v_new))
        state = state * jnp.exp(g_tot[heads]) + BMM_TN(k_dec[heads], v_new)
    o = jnp.concatenate(outs, axis=0)
    on = o * lax.rsqrt(jnp.mean(o * o, axis=2, keepdims=True) + EPS) * nw
    return on * _silu(z), state


def _heads(v, off=0):
    return jnp.concatenate([v[None, CH * n:CH * (n + 1), off + 128 * h:off + 128 * (h + 1)]
                            for n in range(DN_STEP_CHUNKS) for h in range(NDH)], axis=0)


def _unheads(t):
    return jnp.concatenate([jnp.concatenate([t[NDH * n + h] for h in range(NDH)], axis=1)
                            for n in range(DN_STEP_CHUNKS)], axis=0)


def _delta_fwd(c_qkv, proj, alog_row, dt_row, nw, cat, exchanges=()):
    def body(c_ref, z_ref, tail_ref, al_ref, dt_ref, nw_ref, _, y_ref, st_ref, state):
        @pl.when(pl.program_id(0) == 0)
        def _():
            state[...] = jnp.zeros_like(state)

        cv = c_ref[...].astype(F32)
        st_ref[0] = state[...]
        y, new_state = _delta_chunks(_heads(cv), _heads(cv, 512), _heads(cv, 1024), _heads(z_ref[...]), tail_ref[...],
                                     al_ref[...], dt_ref[...], nw_ref[...], state[...])
        y_ref[...] = _unheads(y)
        state[...] = new_state

    row = pl.BlockSpec((1, 128), lambda n: (0, 0))
    rows = DN_STEP_ROWS
    return _hosted_call(
        body, name="delta_fwd", steps=DN_STEPS,
        in_specs=[pl.BlockSpec((rows, 1536), lambda n: (n, 0)), pl.BlockSpec((rows, 512), lambda n: (n, DN_Z_COL // 512)),
                  pl.BlockSpec((rows, 128), lambda n: (n, DN_TAIL_BLK)), row, row, row, pl.BlockSpec(memory_space=pl.ANY)],
        out_specs=[pl.BlockSpec((rows, 512), lambda n: (n, 1)),
                   pl.BlockSpec((1, NDH, 128, 128), lambda n: (n, 0, 0, 0))],
        out_shape=[jax.ShapeDtypeStruct((SEQ, 2 * ATTN_W), F32), jax.ShapeDtypeStruct((DN_STEPS, NDH, 128, 128), F32)],
        scratch_shapes=[pltpu.VMEM((NDH, 128, 128), F32)],
        operands=(c_qkv, proj, proj, alog_row, dt_row, nw, cat), exchanges=exchanges, aliases={6: 0})


def _delta_bwd(c_qkv, proj, alog_row, dt_row, nw, states, dcat, exchanges=()):
    def body(c_ref, z_ref, tail_ref, al_ref, dt_ref, nw_ref, st_ref, dy_ref,
             dp_ref, dc_ref, dal_ref, ddt_ref, dnw_ref, dstate):
        @pl.when(pl.program_id(0) == 0)
        def _():
            dstate[...] = jnp.zeros_like(dstate)
            dal_ref[...] = jnp.zeros_like(dal_ref)
            ddt_ref[...] = jnp.zeros_like(ddt_ref)
            dnw_ref[...] = jnp.zeros_like(dnw_ref)

        cv = c_ref[...].astype(F32)
        _, vjp = jax.vjp(_delta_chunks, _heads(cv), _heads(cv, 512), _heads(cv, 1024), _heads(z_ref[...]),
                         tail_ref[...], al_ref[...], dt_ref[...], nw_ref[...], st_ref[0])
        dq, dk, dv, dz, dtail, dal, ddt, dnw, dst = vjp((_heads(dy_ref[...]), dstate[...]))
        dstate[...] = dst
        dc_ref[...] = jnp.concatenate([_unheads(dq), _unheads(dk), _unheads(dv)], axis=1)
        dp_ref[...] = jnp.concatenate([_unheads(dz), dtail, jnp.zeros((DN_STEP_ROWS, 128), F32)], axis=1).astype(BF16)
        dal_ref[...] += dal
        ddt_ref[...] += ddt
        dnw_ref[...] += dnw

    rev = lambda n: DN_STEPS - 1 - n
    row = pl.BlockSpec((1, 128), lambda n: (0, 0))
    rows = DN_STEP_ROWS
    return _hosted_call(
        body, name="delta_bwd", steps=DN_STEPS,
        in_specs=[pl.BlockSpec((rows, 1536), lambda n: (rev(n), 0)),
                  pl.BlockSpec((rows, 512), lambda n: (rev(n), DN_Z_COL // 512)),
                  pl.BlockSpec((rows, 128), lambda n: (rev(n), DN_TAIL_BLK)), row, row, row,
                  pl.BlockSpec((1, NDH, 128, 128), lambda n: (rev(n), 0, 0, 0)),
                  pl.BlockSpec((rows, 512), lambda n: (rev(n), 1))],
        out_specs=[pl.BlockSpec((rows, 768), lambda n: (rev(n), DN_Z_COL // 768)),
                   pl.BlockSpec((rows, 1536), lambda n: (rev(n), 0)), row, row, row],
        out_shape=[jax.ShapeDtypeStruct((SEQ, IN_PAD), BF16), jax.ShapeDtypeStruct((SEQ, 1536), F32)]
        + [jax.ShapeDtypeStruct((1, 128), F32)] * 3,
        scratch_shapes=[pltpu.VMEM((NDH, 128, 128), F32)],
        operands=(c_qkv, proj, proj, alog_row, dt_row, nw, states, dcat), exchanges=exchanges)


def _place():
    x, y, c = lax.axis_index("x"), lax.axis_index("y"), lax.axis_index("c")
    other_chips = [(1 - x, y), (x, 1 - y), (1 - x, 1 - y)]
    return x, y, c, other_chips


def _gather_exchange(shards):
    n = len(shards)

    def copies(ins, outs, sems):
        send_sems, recv_sems, local_sems = sems
        x, y, c, chips = _place()
        me, sibling = (x, y, c), (x, y, 1 - c)

        def copy(b, k, block, to, src=None):
            slot = outs[b].at[4 * block[0] + 2 * block[1] + block[2]]
            return pltpu.make_async_remote_copy(
                src_ref=slot if src is None else src, dst_ref=slot,
                send_sem=send_sems.at[b, k], recv_sem=recv_sems.at[b, k], device_id=to, device_id_type=MESH)

        mine = [pltpu.make_async_copy(ins[b], outs[b].at[4 * x + 2 * y + c], local_sems.at[b]) for b in range(n)]
        first = []
        for b in range(n):
            first.append(copy(b, 0, me, sibling, src=ins[b]))
            first += [copy(b, 1 + j, me, (*chip, c), src=ins[b]) for j, chip in enumerate(chips)]
        over_ici = [copy(b, 1 + j, (*chip, c), me) for b in range(n) for j, chip in enumerate(chips)]
        passed = [copy(b, 4 + j, (*chip, c), sibling) for b in range(n) for j, chip in enumerate(chips)]
        from_sibling = []
        for b in range(n):
            from_sibling.append(copy(b, 0, sibling, me))
            from_sibling += [copy(b, 4 + j, (*chip, 1 - c), me) for j, chip in enumerate(chips)]
        return mine, first, over_ici, passed, from_sibling

    def start(ins, outs, sems):
        mine, first, _, _, _ = copies(ins, outs, sems)
        for cp in mine + first:
            cp.start()

    def middle(ins, outs, sems):
        _, _, over_ici, passed, _ = copies(ins, outs, sems)
        for arrived, onward in zip(over_ici, passed):
            arrived.wait_recv()
            onward.start()

    def finish(ins, outs, sems):
        mine, first, _, passed, from_sibling = copies(ins, outs, sems)
        for cp in from_sibling:
            cp.wait_recv()
        for cp in first + passed:
            cp.wait_send()
        for cp in mine:
            cp.wait()

    return Exchange(shards, [jax.ShapeDtypeStruct((N_DEV,) + s.shape, s.dtype) for s in shards],
                    [pltpu.SemaphoreType.DMA((n, 7)), pltpu.SemaphoreType.DMA((n, 7)), pltpu.SemaphoreType.DMA((n,))],
                    start, middle, finish)


def _sibling_exchange(gs):
    n = len(gs)

    def copies(ins, outs, sems):
        send_sems, recv_sems = sems
        x, y, c, _ = _place()
        return [pltpu.make_async_remote_copy(
            src_ref=ins[b].at[2 * p + (1 - c)], dst_ref=outs[b].at[p],
            send_sem=send_sems.at[b, p], recv_sem=recv_sems.at[b, p],
            device_id=(x, y, 1 - c), device_id_type=MESH) for b in range(n) for p in range(4)]

    def start(ins, outs, sems):
        for cp in copies(ins, outs, sems):
            cp.start()

    def finish(ins, outs, sems):
        for cp in copies(ins, outs, sems):
            cp.wait()

    return Exchange(gs, [jax.ShapeDtypeStruct((4,) + g.shape[1:], g.dtype) for g in gs],
                    [pltpu.SemaphoreType.DMA((n, 4)), pltpu.SemaphoreType.DMA((n, 4))], start, None, finish)


def _chips_exchange(hs):
    n = len(hs)

    def copies(ins, outs, sems):
        send_sems, recv_sems, local_sems = sems
        x, y, c, chips = _place()
        my_chip = 2 * x + y
        local = [pltpu.make_async_copy(ins[b].at[my_chip], outs[b].at[my_chip], local_sems.at[b]) for b in range(n)]
        sends, arrivals = [], []
        for b in range(n):
            for k, (px, py) in enumerate(chips):
                peer = 2 * px + py
                sends.append(pltpu.make_async_remote_copy(
                    src_ref=ins[b].at[peer], dst_ref=outs[b].at[my_chip],
                    send_sem=send_sems.at[b, k], recv_sem=recv_sems.at[b, k],
                    device_id=(px, py, c), device_id_type=MESH))
                arrivals.append(pltpu.make_async_remote_copy(
                    src_ref=ins[b].at[peer], dst_ref=outs[b].at[peer],
                    send_sem=send_sems.at[b, k], recv_sem=recv_sems.at[b, k],
                    device_id=(px, py, c), device_id_type=MESH))
        return local, sends, arrivals

    def start(ins, outs, sems):
        local, sends, _ = copies(ins, outs, sems)
        for cp in local + sends:
            cp.start()

    def finish(ins, outs, sems):
        local, sends, arrivals = copies(ins, outs, sems)
        for cp in arrivals:
            cp.wait_recv()
        for cp in sends:
            cp.wait_send()
        for cp in local:
            cp.wait()

    return Exchange(hs, [jax.ShapeDtypeStruct(h.shape, h.dtype) for h in hs],
                    [pltpu.SemaphoreType.DMA((n, 3)), pltpu.SemaphoreType.DMA((n, 3)), pltpu.SemaphoreType.DMA((n,))],
                    start, None, finish)


def _run_exchange(exchange, name):
    n_in, n_out = len(exchange.operands), len(exchange.out_shapes)

    def body(*refs):
        ins, outs, sems = refs[:n_in], refs[n_in:n_in + n_out], refs[n_in + n_out:]
        exchange.start(ins, outs, sems)
        if exchange.middle is not None:
            exchange.middle(ins, outs, sems)
        exchange.finish(ins, outs, sems)

    return pl.pallas_call(
        body, name=name,
        in_specs=[HBM_SPEC] * n_in, out_specs=[HBM_SPEC] * n_out,
        out_shape=exchange.out_shapes, scratch_shapes=exchange.sems,
    )(*exchange.operands)


def _pair_add(g, r, core, name):
    _, nr, nc = g.shape
    tr = nr // 2 if nr % 32 == 0 else nr

    def body(core_ref, g_ref, r_ref, o_ref):
        o_ref[...] = (g_ref[...].astype(F32) + r_ref[...].astype(F32)).astype(BF16)

    return pl.pallas_call(
        body, name=name,
        grid_spec=pltpu.PrefetchScalarGridSpec(
            num_scalar_prefetch=1, grid=(4, nr // tr),
            in_specs=[pl.BlockSpec((1, tr, nc), lambda p, i, core: (2 * p + core[0], i, 0)),
                      pl.BlockSpec((1, tr, nc), lambda p, i, core: (p, i, 0))],
            out_specs=pl.BlockSpec((1, tr, nc), lambda p, i, core: (p, i, 0))),
        out_shape=jax.ShapeDtypeStruct(r.shape, BF16),
        compiler_params=_cp("parallel", "parallel"),
    )(core, g, r)


def _all_gather_sum_small(v):
    rows = v.shape[0]

    def body(x_ref, sum_ref, out_ref, send_sems, recv_sems, local_sem):
        x, y, c, chips = _place()
        me, sibling = (x, y, c), (x, y, 1 - c)

        def block(px, py, pc):
            return out_ref.at[pl.ds((4 * px + 2 * py + pc) * rows, rows), :]

        def copy(k, blk, to, src=None):
            return pltpu.make_async_remote_copy(
                src_ref=block(*blk) if src is None else src, dst_ref=block(*blk),
                send_sem=send_sems.at[k], recv_sem=recv_sems.at[k], device_id=to, device_id_type=MESH)

        mine = pltpu.make_async_copy(x_ref, block(*me), local_sem)
        mine.start()
        first = [copy(0, me, sibling, src=x_ref)]
        first += [copy(1 + j, me, (*chip, c), src=x_ref) for j, chip in enumerate(chips)]
        for cp in first:
            cp.start()
        passed = [copy(4 + j, (*chip, c), sibling) for j, chip in enumerate(chips)]
        for j, chip in enumerate(chips):
            copy(1 + j, (*chip, c), me).wait_recv()
            passed[j].start()
        copy(0, sibling, me).wait_recv()
        for j, chip in enumerate(chips):
            copy(4 + j, (*chip, 1 - c), me).wait_recv()
        for cp in first + passed:
            cp.wait_send()
        mine.wait()
        total = out_ref[pl.ds(0, rows), :]
        for d in range(1, N_DEV):
            total = total + out_ref[pl.ds(d * rows, rows), :]
        sum_ref[...] = total

    vm = pl.BlockSpec(memory_space=pltpu.VMEM)
    return pl.pallas_call(
        body, name="small_all_reduce",
        in_specs=[vm], out_specs=[vm],
        out_shape=[jax.ShapeDtypeStruct((rows, 128), F32)],
        scratch_shapes=[pltpu.VMEM((N_DEV * rows, 128), F32), pltpu.SemaphoreType.DMA((7,)),
                        pltpu.SemaphoreType.DMA((7,)), pltpu.SemaphoreType.DMA],
    )(v)[0]


def _adamw(w, g, m, v):
    m = ADAM_B1 * m + (1.0 - ADAM_B1) * g
    v = ADAM_B2 * v + (1.0 - ADAM_B2) * (g * g)
    m_hat = m / (1.0 - ADAM_B1 ** ADAM_STEP)
    v_hat = v / (1.0 - ADAM_B2 ** ADAM_STEP)
    delta = -ADAM_LR * (m_hat / (jnp.sqrt(v_hat) + ADAM_EPS) + ADAM_WD * w)
    return delta, m, v


ADAM_TILE = dict(w_in=(IN_COLS // N_DEV, 256), w_out=(128, D_MODEL), ffn_w_in=(176, D_MODEL), ffn_w_out=(176, D_MODEL))


def _sum_chips(p):
    p = p.astype(F32)
    return (p[0] + p[1]) + (p[2] + p[3])


def _adamw_sharded(parts, w, m, v, tile, name):
    nl, nr, nc = w.shape
    tr, tc = tile

    def body(*refs):
        p_refs, (w_ref, m_ref, v_ref, g_ref, d_ref, nm_ref, nv_ref) = refs[:nl], refs[nl:]
        layer = pl.program_id(0)
        p = p_refs[0][...]
        for l in range(1, nl):
            p = jnp.where(layer == l, p_refs[l][...], p)
        g = _sum_chips(p)
        delta, nm, nv = _adamw(w_ref[0], g, m_ref[0], v_ref[0])
        g_ref[0] = g
        d_ref[0] = delta
        nm_ref[0] = nm
        nv_ref[0] = nv

    blk = pl.BlockSpec((1, tr, tc), lambda l, i, j: (l, i, j))
    return pl.pallas_call(
        body, name=name, grid=(nl, nr // tr, nc // tc),
        in_specs=[pl.BlockSpec((4, tr, tc), lambda l, i, j, own=own: (0, jnp.where(l == own, i, 0), j))
                  for own in range(nl)] + [blk, blk, blk],
        out_specs=[blk] * 4,
        out_shape=[jax.ShapeDtypeStruct(w.shape, F32)] * 4,
        compiler_params=_cp("parallel", "parallel", "parallel"),
    )(*parts, w, m, v)


def _adamw_small(g, w, m, v):
    def body(g_ref, w_ref, m_ref, v_ref, d_ref, nm_ref, nv_ref):
        delta, nm, nv = _adamw(w_ref[...], g_ref[...], m_ref[...], v_ref[...])
        d_ref[...] = delta
        nm_ref[...] = nm
        nv_ref[...] = nv

    return pl.pallas_call(
        body, name="adamw_small",
        out_shape=[jax.ShapeDtypeStruct(g.shape, F32)] * 3,
    )(g, w, m, v)


def _packed_rows(n):
    return -(-n // 1024) * 8


def _pack(arrays, rows):
    pieces = []
    for a in arrays:
        flat = a.reshape(-1).astype(F32)
        nr = _packed_rows(flat.shape[0])
        pieces.append(jnp.pad(flat, (0, nr * 128 - flat.shape[0])).reshape(nr, 128))
    used = sum(p.shape[0] for p in pieces)
    return jnp.concatenate(pieces + [jnp.zeros((rows - used, 128), F32)] * (rows > used), axis=0)


def _unpack(packed, shapes):
    out, row = [], 0
    for s in shapes:
        n = math.prod(s)
        out.append(packed[row:row + _packed_rows(n)].reshape(-1)[:n].reshape(s))
        row += _packed_rows(n)
    return out


def _row(v, width=None):
    v = v.reshape(1, -1)
    return v if width is None else jnp.pad(v, ((0, 0), (0, width - v.shape[1])))


def _layer_fwd(x, h, wts, tables, hosted, last_step):
    proj = _matmul(h, wts["w_in"], tb=True, tm=SEQ, tn=768, tk=1024, name="mm_proj")
    (cat, lse), got = _attn_fwd(proj, *tables, exchanges=hosted["attn"][0])
    hosted["attn"][1](got)
    c_qkv = _dnconv_fwd(proj, wts["dn_conv_w"])
    (cat, states), got = _delta_fwd(c_qkv, proj, wts["dn_a_log"], wts["dn_dt_bias"], wts["dn_norm_w"], cat,
                                    exchanges=hosted["delta"][0])
    hosted["delta"][1](got)
    mix = _matmul(cat, wts["w_out"], tm=512, tn=1024, tk=1024, name="mm_mix")
    x1, h2 = _resnorm_norm_fwd(x, mix, wts["norm_post_mix"], wts["norm_pre_ffn"], "norm_post_mix")
    pre = _matmul(h2, wts["ffn_w_in"], tb=True, tm=SEQ, tn=512, tk=1024, name="mm_ffn_in", out_dtype=BF16)
    act, got = _ffact_fwd(pre, wts["ffn_conv_w"], wts["ffn_conv_b"], exchanges=hosted["ffact"][0])
    hosted["ffact"][1](got)
    f = _matmul(act, wts["ffn_w_out"], tm=512, tn=1024, tk=D_FF, name="mm_ffn_out")
    saved = dict(x=x, h=h, proj=proj, lse=lse, c_qkv=c_qkv, states=states, cat=cat, mix=mix, x1=x1, h2=h2, pre=pre,
                 act=act, f=f)
    return last_step(x1, f), saved


def _layer_bwd(dx2, wts, s, tables, ffact_exchanges=(), delta_exchanges=None, attn_exchanges=None):
    g = {}
    df, g["norm_post_ffn"] = _norm_bwd(s["f"], wts["norm_post_ffn"], dx2, None, "norm_post_ffn_bwd", BF16)
    dact = _matmul(df, wts["ffn_w_out"], tb=True, tm=SEQ, tn=1408, tk=1024, name="mm_dact", out_dtype=BF16)
    g["ffn_w_out"] = _matmul(s["act"], df, ta=True, tm=1408, tn=512, tk=SEQ, name="mm_dw_ffn_out", out_dtype=BF16)
    (dpre, g["ffn_conv_w"], g["ffn_conv_b"]), got = _ffact_bwd(s["pre"], wts["ffn_conv_w"], wts["ffn_conv_b"], dact,
                                                               exchanges=ffact_exchanges)
    dh2 = _matmul(dpre, wts["ffn_w_in"], tm=512, tn=1024, tk=2 * D_FF, name="mm_dh2")
    g["ffn_w_in"] = _matmul(dpre, s["h2"], ta=True, tm=512, tn=1024, tk=SEQ, name="mm_dw_ffn_in", out_dtype=BF16)
    dx1, dmix, g["norm_pre_ffn"], g["norm_post_mix"] = _norm_bwd_pair(
        s["x1"], wts["norm_pre_ffn"], dh2, dx2, s["mix"], wts["norm_post_mix"], "norm_pre_ffn_bwd")
    dcat = _matmul(dmix, wts["w_out"], tb=True, tm=SEQ, tn=512, tk=1024, name="mm_dcat")
    g["w_out"] = _matmul(s["cat"], dmix, ta=True, tm=1024, tn=512, tk=SEQ, name="mm_dw_out", out_dtype=BF16)
    (dproj, dc, g["dn_a_log"], g["dn_dt_bias"], g["dn_norm_w"]), got = _delta_bwd(
        s["c_qkv"], s["proj"], wts["dn_a_log"], wts["dn_dt_bias"], wts["dn_norm_w"], s["states"], dcat,
        exchanges=delta_exchanges(g, got) if delta_exchanges is not None else ())
    dproj, got = _attn_bwd(s["proj"], *tables, s["cat"], s["lse"], dcat, dproj,
                           exchanges=attn_exchanges(got) if attn_exchanges is not None else ())
    dproj, g["dn_conv_w"] = _dnconv_bwd(s["proj"], wts["dn_conv_w"], dc, dproj)
    dh = _matmul(dproj, wts["w_in"], tm=512, tn=1024, tk=IN_PAD, name="mm_dh")
    g["w_in"] = _matmul(dproj, s["h"], ta=True, tm=768, tn=1024, tk=SEQ, name="mm_dw_in", out_dtype=BF16)
    dx, g["norm_pre_mix"] = _norm_bwd(s["x"], wts["norm_pre_mix"], dh, dx1, "norm_pre_mix_bwd")
    return dx, g, got


BIG = ("w_in", "w_out", "ffn_w_in", "ffn_w_out")
COLUMN_SHARDED = ("w_in", "ffn_w_in")
SMALL_SHARDED = ("dn_conv_w", "ffn_conv_w")
REPLICATED = ("dn_a_log", "dn_dt_bias", "dn_norm_w", "ffn_conv_b", "norm_pre_mix", "norm_post_mix", "norm_pre_ffn",
              "norm_post_ffn")
WEIGHTS = ("w_in", "dn_conv_w", "dn_a_log", "dn_dt_bias", "dn_norm_w", "w_out", "ffn_w_in", "ffn_conv_w", "ffn_conv_b",
           "ffn_w_out", "norm_pre_mix", "norm_post_mix", "norm_pre_ffn", "norm_post_ffn")
FULL_SHAPE = dict(dn_conv_w=(DEPTH, 4, 1536), ffn_conv_w=(DEPTH, 3, 2 * D_FF), dn_a_log=(DEPTH, NDH),
                  dn_dt_bias=(DEPTH, NDH), dn_norm_w=(DEPTH, 128), ffn_conv_b=(DEPTH, 2 * D_FF),
                  norm_pre_mix=(DEPTH, D_MODEL), norm_post_mix=(DEPTH, D_MODEL), norm_pre_ffn=(DEPTH, D_MODEL),
                  norm_post_ffn=(DEPTH, D_MODEL))
SMALL_GRAD_ORDER = REPLICATED + SMALL_SHARDED
SMALL_GRAD_ROWS = 544
SMALL_W_ROWS = 56
SMALL_ADAM_ROWS = 232


def _w_in_rows_to_kernel_order(t):
    qkv = t[:QKV_W].reshape(3, N_PAIR, 128, -1).swapaxes(0, 1).reshape(QKV_W, -1)
    return jnp.pad(jnp.concatenate([qkv, t[QKV_W:]], axis=0), ((0, IN_PAD - IN_COLS), (0, 0)))


def _w_in_rows_from_kernel_order(t):
    qkv = t[:QKV_W].reshape(N_PAIR, 3, 128, -1).swapaxes(0, 1).reshape(QKV_W, -1)
    return jnp.concatenate([qkv, t[QKV_W:IN_COLS]], axis=0)


def _interleave_ff_rows(t):
    return t.reshape(2, FF_BLKS, 128, -1).swapaxes(0, 1).reshape(2 * D_FF, -1)


def _deinterleave_ff_rows(t):
    return t.reshape(FF_BLKS, 2, 128, -1).swapaxes(0, 1).reshape(2 * D_FF, -1)


def kernel(x, w_in, dn_conv_w, dn_a_log, dn_dt_bias, dn_norm_w, w_out, ffn_w_in, ffn_conv_w, ffn_conv_b, ffn_w_out, norm_pre_mix, norm_post_mix, norm_pre_ffn, norm_post_ffn, loss_target, m_w_in, m_dn_conv_w, m_dn_a_log, m_dn_dt_bias, m_dn_norm_w, m_w_out, m_ffn_w_in, m_ffn_conv_w, m_ffn_conv_b, m_ffn_w_out, m_norm_pre_mix, m_norm_post_mix, m_norm_pre_ffn, m_norm_post_ffn, v_w_in, v_dn_conv_w, v_dn_a_log, v_dn_dt_bias, v_dn_norm_w, v_w_out, v_ffn_w_in, v_ffn_conv_w, v_ffn_conv_b, v_ffn_w_out, v_norm_pre_mix, v_norm_post_mix, v_norm_pre_ffn, v_norm_post_ffn):
    local = dict(w_in=w_in, dn_conv_w=dn_conv_w, dn_a_log=dn_a_log, dn_dt_bias=dn_dt_bias, dn_norm_w=dn_norm_w,
                 w_out=w_out, ffn_w_in=ffn_w_in, ffn_conv_w=ffn_conv_w, ffn_conv_b=ffn_conv_b, ffn_w_out=ffn_w_out,
                 norm_pre_mix=norm_pre_mix, norm_post_mix=norm_post_mix, norm_pre_ffn=norm_pre_ffn,
                 norm_post_ffn=norm_post_ffn)
    mom_m = dict(w_in=m_w_in, dn_conv_w=m_dn_conv_w, dn_a_log=m_dn_a_log, dn_dt_bias=m_dn_dt_bias,
                 dn_norm_w=m_dn_norm_w, w_out=m_w_out, ffn_w_in=m_ffn_w_in, ffn_conv_w=m_ffn_conv_w,
                 ffn_conv_b=m_ffn_conv_b, ffn_w_out=m_ffn_w_out, norm_pre_mix=m_norm_pre_mix,
                 norm_post_mix=m_norm_post_mix, norm_pre_ffn=m_norm_pre_ffn, norm_post_ffn=m_norm_post_ffn)
    mom_v = dict(w_in=v_w_in, dn_conv_w=v_dn_conv_w, dn_a_log=v_dn_a_log, dn_dt_bias=v_dn_dt_bias,
                 dn_norm_w=v_dn_norm_w, w_out=v_w_out, ffn_w_in=v_ffn_w_in, ffn_conv_w=v_ffn_conv_w,
                 ffn_conv_b=v_ffn_conv_b, ffn_w_out=v_ffn_w_out, norm_pre_mix=v_norm_pre_mix,
                 norm_post_mix=v_norm_post_mix, norm_pre_ffn=v_norm_pre_ffn, norm_post_ffn=v_norm_post_ffn)
    dev = 4 * lax.axis_index("x") + 2 * lax.axis_index("y") + lax.axis_index("c")
    core = lax.axis_index("c").astype(jnp.int32).reshape(1)

    def shard(n, l):
        s = local[n].transpose(0, 2, 1) if n in COLUMN_SHARDED else local[n]
        return s[l].astype(BF16)

    def matrix(n, gathered):
        if n == "w_in":
            return _w_in_rows_to_kernel_order(gathered.reshape(IN_COLS, D_MODEL))
        if n == "ffn_w_in":
            return _interleave_ff_rows(gathered.reshape(2 * D_FF, D_MODEL))
        return gathered.reshape(-1, D_MODEL)

    small_w = _pack([dn_conv_w, ffn_conv_w], SMALL_W_ROWS)
    g_w_in0, g_small = _run_exchange(_gather_exchange([shard("w_in", 0), small_w]), "weights_all_gather")
    n_dn, n_ff = DEPTH * 4 * 192, DEPTH * 3 * 704
    dn_rows = _packed_rows(n_dn)
    sm_dn = g_small[:, :dn_rows].reshape(N_DEV, -1)[:, :n_dn]
    sm_ff = g_small[:, dn_rows:].reshape(N_DEV, -1)[:, :n_ff]
    full_dn_conv = sm_dn.reshape(N_DEV, DEPTH, 4, 192).transpose(1, 2, 0, 3).reshape(DEPTH, 4, 1536)
    full_ff_conv = _interleave_ff(sm_ff.reshape(N_DEV, DEPTH, 3, 704).transpose(1, 2, 0, 3).reshape(DEPTH, 3, 2 * D_FF))

    def small_weights(l):
        wts = dict(dn_conv_w=full_dn_conv[l], ffn_conv_w=full_ff_conv[l], ffn_conv_b=_interleave_ff(_row(ffn_conv_b[l])),
                   dn_a_log=_row(dn_a_log[l], 128), dn_dt_bias=_row(dn_dt_bias[l], 128))
        for n in ("dn_norm_w", "norm_pre_mix", "norm_post_mix", "norm_pre_ffn", "norm_post_ffn"):
            wts[n] = _row(local[n][l])
        return wts

    weights = [small_weights(l) for l in range(DEPTH)]
    weights[0]["w_in"] = matrix("w_in", g_w_in0)

    def gather_behind(wanted):
        def deliver(got):
            for (n, l), g in zip(wanted, got[0]):
                weights[l][n] = matrix(n, g)

        return [_gather_exchange([shard(n, l) for n, l in wanted])], deliver

    nothing = ((), lambda got: None)

    tables = _rope_tables()
    h0 = _norm_fwd(x[0], weights[0]["norm_pre_mix"], "norm_pre_mix")
    (act, h1), saved0 = _layer_fwd(
        x[0], h0, weights[0], tables,
        dict(attn=gather_behind([("ffn_w_in", 0)]), delta=gather_behind([("w_out", 0), ("ffn_w_out", 0)]),
             ffact=gather_behind([("w_in", 1)])),
        lambda x1, f: _resnorm_norm_fwd(x1, f, weights[0]["norm_post_ffn"], weights[1]["norm_pre_mix"], "norm_post_ffn"))
    (loss_part, dact), saved1 = _layer_fwd(
        act, h1, weights[1], tables,
        dict(attn=gather_behind([("ffn_w_in", 1)]), delta=gather_behind([("w_out", 1), ("ffn_w_out", 1)]), ffact=nothing),
        lambda x1, f: _resnorm_loss(x1, f, weights[1]["norm_post_ffn"], loss_target[0]))

    def to_devices(name, t):
        if name == "w_in":
            t = _w_in_rows_from_kernel_order(t)
        if name == "ffn_w_in":
            t = _deinterleave_ff_rows(t)
        return t.reshape(N_DEV, t.shape[0] // N_DEV, t.shape[1])

    def pair_sums(names, layer, to_dev, from_sibling):
        return [_pair_add(gd, r, core, "grads_pair_add_%s_%d" % (n, layer))
                for n, gd, r in zip(names, to_dev, from_sibling)]

    early = ("w_out", "ffn_w_in", "ffn_w_out")
    grads, parts, stash = [None] * DEPTH, {}, {}

    def delta_exchanges1(g, got_ffact):
        stash["early1"] = [to_devices(n, g[n]) for n in early]
        return [_sibling_exchange(stash["early1"])]

    def attn_exchanges1(got_delta):
        return [_chips_exchange(pair_sums(early, 1, stash["early1"], got_delta[0]))]

    dact, grads[1], got_attn = _layer_bwd(dact, weights[1], saved1, tables, (), delta_exchanges1, attn_exchanges1)
    for n, p in zip(early, got_attn[0]):
        parts[n, 1] = p
    w_in1 = [to_devices("w_in", grads[1]["w_in"])]

    def delta_exchanges0(g, got_ffact):
        stash["early0"] = [to_devices(n, g[n]) for n in early]
        return [_chips_exchange(pair_sums(("w_in",), 1, w_in1, got_ffact[0])), _sibling_exchange(stash["early0"])]

    def attn_exchanges0(got_delta):
        parts["w_in", 1], = got_delta[0]
        return [_chips_exchange(pair_sums(early, 0, stash["early0"], got_delta[1]))]

    dact, grads[0], got_attn = _layer_bwd(dact, weights[0], saved0, tables, [_sibling_exchange(w_in1)],
                                          delta_exchanges0, attn_exchanges0)
    for n, p in zip(early, got_attn[0]):
        parts[n, 0] = p
    grad_x = dact[None]
    last = [to_devices("w_in", grads[0]["w_in"])]
    from_sibling = _run_exchange(_sibling_exchange(last), "grads_to_sibling")
    parts["w_in", 0], = _run_exchange(_chips_exchange(pair_sums(("w_in",), 0, last, from_sibling)), "grads_to_chips")

    def small_grad(name):
        t = jnp.stack([grads[l][name] for l in range(DEPTH)])
        if name in ("dn_a_log", "dn_dt_bias"):
            t = t[:, 0, :NDH]
        if name in ("ffn_conv_w", "ffn_conv_b"):
            t = _deinterleave_ff(t)
        return t.reshape(FULL_SHAPE[name])

    small_part = _pack([small_grad(n) for n in SMALL_GRAD_ORDER] + [loss_part[0, :1]], SMALL_GRAD_ROWS)
    small_sum = _all_gather_sum_small(small_part)
    small_g = dict(zip(SMALL_GRAD_ORDER + ("loss",), _unpack(small_sum, [FULL_SHAPE[n] for n in SMALL_GRAD_ORDER] + [(1,)])))
    loss = small_g["loss"][0]
    small_g["dn_conv_w"] = lax.dynamic_slice_in_dim(small_g["dn_conv_w"], dev * 192, 192, axis=2)
    small_g["ffn_conv_w"] = lax.dynamic_slice_in_dim(small_g["ffn_conv_w"], dev * 704, 704, axis=2)

    out_g, out_d, out_m, out_v = {}, {}, {}, {}
    for n in BIG:
        turn = (lambda t: t.transpose(0, 2, 1)) if n in COLUMN_SHARDED else (lambda t: t)
        outs = _adamw_sharded([parts[n, l] for l in range(DEPTH)], turn(local[n]), turn(mom_m[n]), turn(mom_v[n]),
                              ADAM_TILE[n], "adamw_" + n)
        out_g[n], out_d[n], out_m[n], out_v[n] = [turn(t) for t in outs]
    shapes = [small_g[n].shape for n in SMALL_GRAD_ORDER]
    d_s, m_s, v_s = _adamw_small(_pack([small_g[n] for n in SMALL_GRAD_ORDER], SMALL_ADAM_ROWS),
                                 _pack([local[n] for n in SMALL_GRAD_ORDER], SMALL_ADAM_ROWS),
                                 _pack([mom_m[n] for n in SMALL_GRAD_ORDER], SMALL_ADAM_ROWS),
                                 _pack([mom_v[n] for n in SMALL_GRAD_ORDER], SMALL_ADAM_ROWS))
    for n, d, m, v in zip(SMALL_GRAD_ORDER, _unpack(d_s, shapes), _unpack(m_s, shapes), _unpack(v_s, shapes)):
        out_g[n], out_d[n], out_m[n], out_v[n] = small_g[n], d, m, v
    return (loss, grad_x, *[out_g[n] for n in WEIGHTS], *[out_d[n] for n in WEIGHTS],
            *[out_m[n] for n in WEIGHTS], *[out_v[n] for n in WEIGHTS])
```

```python
import functools
import math

import jax
import jax.numpy as jnp
from jax import lax
from jax.experimental import pallas as pl
from jax.experimental.pallas import tpu as pltpu

F32 = jnp.float32
BF16 = jnp.bfloat16
MESH = pl.DeviceIdType.MESH

N_DEV = 8
SEQ = 2048
D_MODEL = 1024
DEPTH = 2
N_PAIR = 4
HEAD_DIM = 64
ATTN_W = 512
ATTN_BLK = 128
DILATIONS = (1, 4, 16)
SEGMENT_BLOCKS = (16, 4, 1)
N_BLK = SEQ // ATTN_BLK
NDH = 4
CH = 64
NCH = SEQ // CH
IN_COLS = 3592
IN_PAD = 3840
QKV_W = 3 * ATTN_W
DN_QKV_BLK0 = QKV_W // 128
DN_QKV_BLKS = 1536 // 128
DN_Z_COL = 3072
DN_TAIL_BLK = 3584 // 128
D_FF = 2816
FF_BLKS = D_FF // 128
EPS = 1e-6
NEG = -1e30
ROPE_THETA = 10000.0

ADAM_LR, ADAM_B1, ADAM_B2, ADAM_EPS, ADAM_WD, ADAM_STEP = 0.001, 0.9, 0.999, 1e-08, 0.01, 10

VMEM_LIMIT = 56 * 1024 * 1024


def _cp(*sem):
    return pltpu.CompilerParams(dimension_semantics=sem, vmem_limit_bytes=VMEM_LIMIT)


class Exchange:
    def __init__(self, operands, out_shapes, sems, start, middle, finish):
        self.operands, self.out_shapes, self.sems = list(operands), list(out_shapes), list(sems)
        self.start, self.middle, self.finish = start, middle, finish


HBM_SPEC = pl.BlockSpec(memory_space=pltpu.HBM)


def _hosted_call(body, *, name, steps, in_specs, out_specs, out_shape, scratch_shapes, operands, exchanges=(),
                 aliases=None):
    n_in, n_out, n_scr = len(in_specs), len(out_specs), len(scratch_shapes)

    def take(refs, pos, counts):
        groups = []
        for c in counts:
            groups.append(refs[pos:pos + c])
            pos += c
        return groups, pos

    def full_body(*refs):
        ins, pos = refs[:n_in], n_in
        ex_ins, pos = take(refs, pos, [len(e.operands) for e in exchanges])
        outs, pos = refs[pos:pos + n_out], pos + n_out
        ex_outs, pos = take(refs, pos, [len(e.out_shapes) for e in exchanges])
        scr, pos = refs[pos:pos + n_scr], pos + n_scr
        ex_sems, pos = take(refs, pos, [len(e.sems) for e in exchanges])
        step = pl.program_id(0)
        for e, a, b, s in zip(exchanges, ex_ins, ex_outs, ex_sems):
            pl.when(step == 0)(functools.partial(e.start, a, b, s))
            if e.middle is not None:
                pl.when(step == (3 * steps) // 4)(functools.partial(e.middle, a, b, s))
        body(*ins, *outs, *scr)
        for e, a, b, s in zip(exchanges, ex_ins, ex_outs, ex_sems):
            pl.when(step == steps - 1)(functools.partial(e.finish, a, b, s))

    n_ex_in = sum(len(e.operands) for e in exchanges)
    n_ex_out = sum(len(e.out_shapes) for e in exchanges)
    results = pl.pallas_call(
        full_body, name=name, grid=(steps,),
        in_specs=list(in_specs) + [HBM_SPEC] * n_ex_in,
        out_specs=list(out_specs) + [HBM_SPEC] * n_ex_out,
        out_shape=list(out_shape) + [s for e in exchanges for s in e.out_shapes],
        scratch_shapes=list(scratch_shapes) + [s for e in exchanges for s in e.sems],
        input_output_aliases=aliases or {},
        compiler_params=_cp("arbitrary"),
    )(*operands, *[a for e in exchanges for a in e.operands])
    ex_results, _ = take(results, n_out, [len(e.out_shapes) for e in exchanges])
    return results[:n_out], ex_results


def _dot(a, b, dims, precision=None):
    if precision is None:
        a = a.astype(BF16)
        b = b.astype(BF16)
    return lax.dot_general(a, b, (dims, ((), ())), preferred_element_type=F32, precision=precision)


def _make_mm(precision):
    @jax.custom_vjp
    def nn(a, b):
        return _dot(a, b, ((1,), (0,)), precision)

    @jax.custom_vjp
    def nt(a, b):
        return _dot(a, b, ((1,), (1,)), precision)

    @jax.custom_vjp
    def tn(a, b):
        return _dot(a, b, ((0,), (0,)), precision)

    nn.defvjp(lambda a, b: (nn(a, b), (a, b)), lambda r, g: (nt(g, r[1]), tn(r[0], g)))
    nt.defvjp(lambda a, b: (nt(a, b), (a, b)), lambda r, g: (nn(g, r[1]), tn(g, r[0])))
    tn.defvjp(lambda a, b: (tn(a, b), (a, b)), lambda r, g: (nt(r[1], g), nn(r[0], g)))
    return nn, nt, tn


def _matmul(a, b, *, ta=False, tb=False, tm, tn, tk, name, out_dtype=F32):
    (k_dim, m_dim) = a.shape if ta else a.shape[::-1]
    (n_dim, k2) = b.shape if tb else b.shape[::-1]
    assert k_dim == k2 and m_dim % tm == 0 and n_dim % tn == 0 and k_dim % tk == 0, (a.shape, b.shape, tm, tn, tk)
    nk = k_dim // tk
    dims = ((0 if ta else 1,), (1 if tb else 0,))

    def body(a_ref, b_ref, o_ref, *acc):
        p = _dot(a_ref[...], b_ref[...], dims)
        if nk == 1:
            o_ref[...] = p.astype(out_dtype)
            return
        acc_ref, k = acc[0], pl.program_id(2)

        @pl.when(k == 0)
        def _():
            acc_ref[...] = p

        @pl.when(k > 0)
        def _():
            acc_ref[...] += p

        @pl.when(k == nk - 1)
        def _():
            o_ref[...] = acc_ref[...].astype(out_dtype)

    a_spec = pl.BlockSpec((tk, tm), lambda i, j, k: (k, i)) if ta else pl.BlockSpec((tm, tk), lambda i, j, k: (i, k))
    b_spec = pl.BlockSpec((tn, tk), lambda i, j, k: (j, k)) if tb else pl.BlockSpec((tk, tn), lambda i, j, k: (k, j))
    return pl.pallas_call(
        body, name=name,
        grid=(m_dim // tm, n_dim // tn, nk),
        in_specs=[a_spec, b_spec],
        out_specs=pl.BlockSpec((tm, tn), lambda i, j, k: (i, j)),
        out_shape=jax.ShapeDtypeStruct((m_dim, n_dim), out_dtype),
        scratch_shapes=[pltpu.VMEM((tm, tn), F32)] if nk > 1 else [],
        compiler_params=_cp("parallel", "parallel", "arbitrary"),
    )(a, b)


NORM_ROWS = 256


def _rms(x, w):
    return x * lax.rsqrt(jnp.mean(x * x, axis=1, keepdims=True) + EPS) * w


def _norm_fwd(x, w_row, name, out_dtype=BF16):
    def body(x_ref, w_ref, o_ref):
        o_ref[...] = _rms(x_ref[...], w_ref[...]).astype(out_dtype)

    return pl.pallas_call(
        body, name=name, grid=(SEQ // NORM_ROWS,),
        in_specs=[pl.BlockSpec((NORM_ROWS, D_MODEL), lambda i: (i, 0)), pl.BlockSpec((1, D_MODEL), lambda i: (0, 0))],
        out_specs=pl.BlockSpec((NORM_ROWS, D_MODEL), lambda i: (i, 0)),
        out_shape=jax.ShapeDtypeStruct((SEQ, D_MODEL), out_dtype),
        compiler_params=_cp("parallel"),
    )(x, w_row)


def _resnorm_norm_fwd(x, f, w_row, next_w_row, name):
    def body(x_ref, f_ref, w_ref, nw_ref, o_ref, h_ref):
        out = x_ref[...] + _rms(f_ref[...], w_ref[...])
        o_ref[...] = out
        h_ref[...] = _rms(out, nw_ref[...]).astype(BF16)

    blk = pl.BlockSpec((NORM_ROWS, D_MODEL), lambda i: (i, 0))
    row = pl.BlockSpec((1, D_MODEL), lambda i: (0, 0))
    return pl.pallas_call(
        body, name=name, grid=(SEQ // NORM_ROWS,),
        in_specs=[blk, blk, row, row],
        out_specs=[blk, blk],
        out_shape=[jax.ShapeDtypeStruct((SEQ, D_MODEL), F32), jax.ShapeDtypeStruct((SEQ, D_MODEL), BF16)],
        compiler_params=_cp("parallel"),
    )(x, f, w_row, next_w_row)


def _norm_bwd(x, w_row, dy, add, name, dx_dtype=F32):
    has_add = add is not None

    def body(*refs):
        if has_add:
            x_ref, w_ref, dy_ref, add_ref, dx_ref, dw_ref = refs
        else:
            x_ref, w_ref, dy_ref, dx_ref, dw_ref = refs
        _, vjp = jax.vjp(_rms, x_ref[...], w_ref[...])
        dx, dw = vjp(dy_ref[...])
        dx_ref[...] = (dx + add_ref[...] if has_add else dx).astype(dx_dtype)

        @pl.when(pl.program_id(0) == 0)
        def _():
            dw_ref[...] = jnp.zeros_like(dw_ref)

        dw_ref[...] += dw

    blk = pl.BlockSpec((NORM_ROWS, D_MODEL), lambda i: (i, 0))
    row = pl.BlockSpec((1, D_MODEL), lambda i: (0, 0))
    ins = [x, w_row, dy] + ([add] if has_add else [])
    return pl.pallas_call(
        body, name=name, grid=(SEQ // NORM_ROWS,),
        in_specs=[blk, row, blk] + ([blk] if has_add else []),
        out_specs=[blk, row],
        out_shape=[jax.ShapeDtypeStruct((SEQ, D_MODEL), dx_dtype), jax.ShapeDtypeStruct((1, D_MODEL), F32)],
        compiler_params=_cp("arbitrary"),
    )(*ins)


def _norm_bwd_pair(x_a, w_a, dy_a, add, x_b, w_b, name):
    def body(xa_ref, wa_ref, dya_ref, add_ref, xb_ref, wb_ref, dxa_ref, dxb_ref, dwa_ref, dwb_ref):
        _, vjp_a = jax.vjp(_rms, xa_ref[...], wa_ref[...])
        dxa, dwa = vjp_a(dya_ref[...])
        dxa = dxa + add_ref[...]
        _, vjp_b = jax.vjp(_rms, xb_ref[...], wb_ref[...])
        dxb, dwb = vjp_b(dxa)
        dxa_ref[...] = dxa
        dxb_ref[...] = dxb.astype(BF16)

        @pl.when(pl.program_id(0) == 0)
        def _():
            dwa_ref[...] = jnp.zeros_like(dwa_ref)
            dwb_ref[...] = jnp.zeros_like(dwb_ref)

        dwa_ref[...] += dwa
        dwb_ref[...] += dwb

    blk = pl.BlockSpec((NORM_ROWS, D_MODEL), lambda i: (i, 0))
    row = pl.BlockSpec((1, D_MODEL), lambda i: (0, 0))
    return pl.pallas_call(
        body, name=name, grid=(SEQ // NORM_ROWS,),
        in_specs=[blk, row, blk, blk, blk, row],
        out_specs=[blk, blk, row, row],
        out_shape=[jax.ShapeDtypeStruct((SEQ, D_MODEL), F32), jax.ShapeDtypeStruct((SEQ, D_MODEL), BF16),
                   jax.ShapeDtypeStruct((1, D_MODEL), F32), jax.ShapeDtypeStruct((1, D_MODEL), F32)],
        compiler_params=_cp("arbitrary"),
    )(x_a, w_a, dy_a, add, x_b, w_b)


def _resnorm_loss(x, f, w_row, target):
    def body(x_ref, f_ref, w_ref, t_ref, loss_ref, dy_ref):
        err = x_ref[...] + _rms(f_ref[...], w_ref[...]) - t_ref[...]
        dy_ref[...] = err * (1.0 / D_MODEL)

        @pl.when(pl.program_id(0) == 0)
        def _():
            loss_ref[...] = jnp.zeros_like(loss_ref)

        part = jnp.sum(jnp.sum(err * err, axis=1, keepdims=True) * (1.0 / D_MODEL), axis=0, keepdims=True)
        loss_ref[...] += 0.5 * jnp.broadcast_to(part, loss_ref.shape)

    blk = pl.BlockSpec((NORM_ROWS, D_MODEL), lambda i: (i, 0))
    return pl.pallas_call(
        body, name="norm_post_ffn_loss", grid=(SEQ // NORM_ROWS,),
        in_specs=[blk, blk, pl.BlockSpec((1, D_MODEL), lambda i: (0, 0)), blk],
        out_specs=[pl.BlockSpec((1, 128), lambda i: (0, 0)), blk],
        out_shape=[jax.ShapeDtypeStruct((1, 128), F32), jax.ShapeDtypeStruct((SEQ, D_MODEL), F32)],
        compiler_params=_cp("arbitrary"),
    )(x, f, w_row, target)


def _make_shift(j):
    def down(x):
        row = lax.broadcasted_iota(jnp.int32, x.shape, 0)
        return jnp.where(row >= j, pltpu.roll(x, j, 0), 0.0)

    def up(x):
        n = x.shape[0]
        row = lax.broadcasted_iota(jnp.int32, x.shape, 0)
        return jnp.where(row < n - j, pltpu.roll(x, n - j, 0), 0.0)

    f = jax.custom_vjp(down)
    f.defvjp(lambda x: (down(x), None), lambda _, g: (up(g),))
    return f


_SHIFT = {j: _make_shift(j) for j in (1, 2, 3)}


def _causal_conv(x, taps):
    n = len(taps)
    acc = x * taps[n - 1]
    for k in range(n - 1):
        acc = acc + _SHIFT[n - 1 - k](x) * taps[k]
    return acc


def _tap_rows(w_ref, lanes=slice(None)):
    return tuple(w_ref[k:k + 1, lanes] for k in range(w_ref.shape[0]))


def _sigmoid(x):
    return 1.0 / (1.0 + jnp.exp(-x))


def _silu(x):
    return x * _sigmoid(x)


def _softplus(x):
    return jnp.maximum(x, 0.0) + jnp.log(1.0 + jnp.exp(-jnp.abs(x)))


def _gelu_tanh(x):
    return 0.5 * x * (1.0 + jnp.tanh(math.sqrt(2.0 / math.pi) * (x + 0.044715 * (x * x * x))))


def _dnconv_fn(x, taps):
    return _silu(_causal_conv(x, taps))


def _dnconv_fwd(proj, conv_w):
    def body(x_ref, w_ref, o_ref):
        o_ref[...] = _dnconv_fn(x_ref[...], _tap_rows(w_ref)).astype(BF16)

    return pl.pallas_call(
        body, name="dnconv_fwd", grid=(DN_QKV_BLKS,),
        in_specs=[pl.BlockSpec((SEQ, 128), lambda j: (0, DN_QKV_BLK0 + j)), pl.BlockSpec((4, 128), lambda j: (0, j))],
        out_specs=pl.BlockSpec((SEQ, 128), lambda j: (0, j)),
        out_shape=jax.ShapeDtypeStruct((SEQ, 1536), BF16),
        compiler_params=_cp("parallel"),
    )(proj, conv_w)


def _dnconv_bwd(proj, conv_w, dc, dproj):
    def body(x_ref, w_ref, dc_ref, _, dx_ref, dw_ref):
        _, vjp = jax.vjp(_dnconv_fn, x_ref[...], _tap_rows(w_ref))
        dx, dw = vjp(dc_ref[...])
        dx_ref[...] = dx.astype(BF16)
        for k, row in enumerate(dw):
            dw_ref[k:k + 1, :] = row

    return pl.pallas_call(
        body, name="dnconv_bwd", grid=(DN_QKV_BLKS,),
        in_specs=[pl.BlockSpec((SEQ, 128), lambda j: (0, DN_QKV_BLK0 + j)), pl.BlockSpec((4, 128), lambda j: (0, j)),
                  pl.BlockSpec((SEQ, 128), lambda j: (0, j)), pl.BlockSpec(memory_space=pl.ANY)],
        out_specs=[pl.BlockSpec((SEQ, 128), lambda j: (0, DN_QKV_BLK0 + j)), pl.BlockSpec((4, 128), lambda j: (0, j))],
        out_shape=[jax.ShapeDtypeStruct((SEQ, IN_PAD), BF16), jax.ShapeDtypeStruct((4, 1536), F32)],
        input_output_aliases={3: 0},
        compiler_params=_cp("parallel"),
    )(proj, conv_w, dc, dproj)


def _ffact_fn(pg, pu, wg, wu, bg, bu):
    return _gelu_tanh(_causal_conv(pg, wg) + bg) * (_causal_conv(pu, wu) + bu)


def _ffact_args(p_ref, w_ref, b_ref):
    g, u = slice(0, 128), slice(128, 256)
    return (p_ref[:, g].astype(F32), p_ref[:, u].astype(F32), _tap_rows(w_ref, g), _tap_rows(w_ref, u),
            b_ref[:, g], b_ref[:, u])


def _ffact_fwd(pre, conv_w, conv_b, exchanges=()):
    def body(p_ref, w_ref, b_ref, o_ref):
        o_ref[...] = _ffact_fn(*_ffact_args(p_ref, w_ref, b_ref)).astype(BF16)

    (act,), results = _hosted_call(
        body, name="ffact_fwd", steps=FF_BLKS,
        in_specs=[pl.BlockSpec((SEQ, 256), lambda j: (0, j)), pl.BlockSpec((3, 256), lambda j: (0, j)),
                  pl.BlockSpec((1, 256), lambda j: (0, j))],
        out_specs=[pl.BlockSpec((SEQ, 128), lambda j: (0, j))],
        out_shape=[jax.ShapeDtypeStruct((SEQ, D_FF), BF16)],
        scratch_shapes=[], operands=(pre, conv_w, conv_b), exchanges=exchanges)
    return act, results


def _ffact_bwd(pre, conv_w, conv_b, dact, exchanges=()):
    def body(p_ref, w_ref, b_ref, da_ref, dp_ref, dw_ref, db_ref):
        _, vjp = jax.vjp(_ffact_fn, *_ffact_args(p_ref, w_ref, b_ref))
        dpg, dpu, dwg, dwu, dbg, dbu = vjp(da_ref[...].astype(F32))
        dp_ref[:, 0:128] = dpg.astype(BF16)
        dp_ref[:, 128:256] = dpu.astype(BF16)
        for k in range(3):
            dw_ref[k:k + 1, 0:128] = dwg[k]
            dw_ref[k:k + 1, 128:256] = dwu[k]
        db_ref[:, 0:128] = dbg
        db_ref[:, 128:256] = dbu

    return _hosted_call(
        body, name="ffact_bwd", steps=FF_BLKS,
        in_specs=[pl.BlockSpec((SEQ, 256), lambda j: (0, j)), pl.BlockSpec((3, 256), lambda j: (0, j)),
                  pl.BlockSpec((1, 256), lambda j: (0, j)), pl.BlockSpec((SEQ, 128), lambda j: (0, j))],
        out_specs=[pl.BlockSpec((SEQ, 256), lambda j: (0, j)), pl.BlockSpec((3, 256), lambda j: (0, j)),
                   pl.BlockSpec((1, 256), lambda j: (0, j))],
        out_shape=[jax.ShapeDtypeStruct((SEQ, 2 * D_FF), BF16), jax.ShapeDtypeStruct((3, 2 * D_FF), F32),
                   jax.ShapeDtypeStruct((1, 2 * D_FF), F32)],
        scratch_shapes=[], operands=(pre, conv_w, conv_b, dact), exchanges=exchanges)


def _interleave_ff(t):
    lead = t.shape[:-1]
    return t.reshape(lead + (2, FF_BLKS, 128)).swapaxes(-3, -2).reshape(lead + (2 * D_FF,))


def _deinterleave_ff(t):
    lead = t.shape[:-1]
    return t.reshape(lead + (FF_BLKS, 2, 128)).swapaxes(-3, -2).reshape(lead + (2 * D_FF,))


def _rope_tables():
    inv = 1.0 / (ROPE_THETA ** (jnp.arange(0, HEAD_DIM, 2, dtype=F32) / HEAD_DIM))
    ang = jnp.arange(SEQ, dtype=F32)[:, None] * inv[None, :]
    cos = jnp.tile(jnp.cos(ang), (1, 4))
    sin = jnp.tile(jnp.sin(ang), (1, 4))
    sign = jnp.where((jnp.arange(128) % HEAD_DIM) < HEAD_DIM // 2, -1.0, 1.0).astype(F32)
    return cos, sin * sign[None, :]


def _rope(x, cos, sin_signed):
    lane = lax.broadcasted_iota(jnp.int32, x.shape, 1)
    partner = jnp.where((lane % HEAD_DIM) < HEAD_DIM // 2, pltpu.roll(x, 128 - HEAD_DIM // 2, 1),
                        pltpu.roll(x, HEAD_DIM // 2, 1))
    return x * cos + partner * sin_signed


def _head_masks():
    lane = lax.broadcasted_iota(jnp.int32, (1, 128), 1)
    return [(lane // HEAD_DIM) == h for h in range(2)]


def _both_heads(x):
    return jnp.concatenate([jnp.where(hm, x, 0.0)[None] for hm in _head_masks()], axis=0)


def _block_keys(branch, k_s, v_s, rows, prows, has_prev):
    a = lax.broadcasted_iota(jnp.int32, (ATTN_BLK, ATTN_BLK), 0)
    c = lax.broadcasted_iota(jnp.int32, (ATTN_BLK, ATTN_BLK), 1)
    keys, values, mask = k_s[rows, :], v_s[rows, :], c <= a
    if SEGMENT_BLOCKS[branch] > 1:
        keys = jnp.concatenate([k_s[prows, :], keys], axis=0)
        values = jnp.concatenate([v_s[prows, :], values], axis=0)
        mask = jnp.concatenate([(c >= a) & has_prev, mask], axis=1)
    twice = lambda t: jnp.broadcast_to(t[None], (2,) + t.shape)
    return twice(keys), twice(values), mask


def _block_rows(branch, t):
    d, per_seg = DILATIONS[branch], SEGMENT_BLOCKS[branch]
    if d == 1:
        start = pl.multiple_of(t * ATTN_BLK, ATTN_BLK)
        prev = pl.multiple_of(jnp.maximum(t - 1, 0) * ATTN_BLK, ATTN_BLK)
        return pl.ds(start, ATTN_BLK), pl.ds(prev, ATTN_BLK), t > 0
    r, n = t // per_seg, t % per_seg
    start = n * (ATTN_BLK * d) + r
    prev = jnp.maximum(n - 1, 0) * (ATTN_BLK * d) + r
    return pl.ds(start, ATTN_BLK, stride=d), pl.ds(prev, ATTN_BLK, stride=d), n > 0


def _attn_fwd(proj, cos, sin_signed, exchanges=()):
    scale = HEAD_DIM ** -0.5

    def body(qkv_ref, cos_ref, sin_ref, out_ref, lse_ref, q_s, k_s, v_s, *branch_s):
        o_s, l_s = branch_s[:3], branch_s[3:]
        q_s[...] = _rope(qkv_ref[:, 0:128], cos_ref[...], sin_ref[...])
        k_s[...] = _rope(qkv_ref[:, 128:256], cos_ref[...], sin_ref[...])
        v_s[...] = qkv_ref[:, 256:384]
        heads = _head_masks()
        for branch in range(3):
            def block(t, carry, branch=branch):
                rows, prows, has_prev = _block_rows(branch, t)
                keys, values, mask = _block_keys(branch, k_s, v_s, rows, prows, has_prev)
                s = jnp.where(mask, BMM_NT(_both_heads(q_s[rows, :]), keys) * scale, NEG)
                m = jnp.max(s, axis=2, keepdims=True)
                e = jnp.exp(s - m)
                l = jnp.sum(e, axis=2, keepdims=True)
                o = BMM(e, values) / l
                lse_b = m + jnp.log(l)
                o_s[branch][rows, :] = jnp.where(heads[0], o[0], o[1])
                l_s[branch][rows, :] = jnp.where(heads[0], lse_b[0], lse_b[1])
                return carry

            lax.fori_loop(0, N_BLK, block, 0, unroll=4)
        l0, l1, l2 = l_s[0][...], l_s[1][...], l_s[2][...]
        m = jnp.maximum(jnp.maximum(l0, l1), l2)
        w0, w1, w2 = jnp.exp(l0 - m), jnp.exp(l1 - m), jnp.exp(l2 - m)
        den = w0 + w1 + w2
        out_ref[...] = (w0 * o_s[0][...] + w1 * o_s[1][...] + w2 * o_s[2][...]) / den
        lse_ref[...] = m + jnp.log(den)

    tab = pl.BlockSpec((SEQ, 128), lambda j: (0, 0))
    col = pl.BlockSpec((SEQ, 128), lambda j: (0, j))
    return _hosted_call(
        body, name="attn_fwd", steps=N_PAIR,
        in_specs=[pl.BlockSpec((SEQ, 384), lambda j: (0, j)), tab, tab],
        out_specs=[col, col],
        out_shape=[jax.ShapeDtypeStruct((SEQ, 2 * ATTN_W), F32), jax.ShapeDtypeStruct((SEQ, ATTN_W), F32)],
        scratch_shapes=[pltpu.VMEM((SEQ, 128), F32)] * 9,
        operands=(proj, cos, sin_signed), exchanges=exchanges)


def _attn_bwd(proj, cos, sin_signed, cat, lse, dcat, dproj, exchanges=()):
    scale = HEAD_DIM ** -0.5

    def body(qkv_ref, cos_ref, sin_ref, out_ref, lse_ref, do_ref, _, dqkv_ref, q_s, k_s, v_s, dq_s, dk_s, dv_s,
             dod_s):
        q_s[...] = _rope(qkv_ref[:, 0:128], cos_ref[...], sin_ref[...])
        k_s[...] = _rope(qkv_ref[:, 128:256], cos_ref[...], sin_ref[...])
        v_s[...] = qkv_ref[:, 256:384]
        dq_s[...] = jnp.zeros_like(dq_s)
        dk_s[...] = jnp.zeros_like(dk_s)
        dv_s[...] = jnp.zeros_like(dv_s)
        dod_s[...] = do_ref[...] * out_ref[...]
        heads = _head_masks()
        for branch in range(3):
            def block(t, carry, branch=branch):
                rows, prows, has_prev = _block_rows(branch, t)
                keys, values, mask = _block_keys(branch, k_s, v_s, rows, prows, has_prev)
                q2, do2 = _both_heads(q_s[rows, :]), _both_heads(do_ref[rows, :])
                lse_b, dod = lse_ref[rows, :], dod_s[rows, :]
                lse2 = jnp.concatenate(
                    [jnp.max(jnp.where(hm, lse_b, NEG), axis=1, keepdims=True)[None] for hm in heads], axis=0)
                delta = jnp.concatenate(
                    [jnp.sum(jnp.where(hm, dod, 0.0), axis=1, keepdims=True)[None] for hm in heads], axis=0)
                p = jnp.exp(jnp.where(mask, BMM_NT(q2, keys) * scale, NEG) - lse2)
                ds = p * (BMM_NT(do2, values) - delta) * scale
                dq = BMM(ds, keys)
                dk = BMM_TN(ds, q2)
                dv = BMM_TN(p, do2)
                dk, dv = dk[0] + dk[1], dv[0] + dv[1]
                dq_s[rows, :] += jnp.where(heads[0], dq[0], dq[1])
                if SEGMENT_BLOCKS[branch] > 1:
                    dk_s[rows, :] += dk[ATTN_BLK:]
                    dv_s[rows, :] += dv[ATTN_BLK:]

                    @pl.when(has_prev)
                    def _():
                        dk_s[prows, :] += dk[:ATTN_BLK]
                        dv_s[prows, :] += dv[:ATTN_BLK]
                else:
                    dk_s[rows, :] += dk
                    dv_s[rows, :] += dv
                return carry

            lax.fori_loop(0, N_BLK, block, 0, unroll=4)
        dqkv_ref[:, 0:128] = _rope(dq_s[...], cos_ref[...], -sin_ref[...]).astype(BF16)
        dqkv_ref[:, 128:256] = _rope(dk_s[...], cos_ref[...], -sin_ref[...]).astype(BF16)
        dqkv_ref[:, 256:384] = dv_s[...].astype(BF16)

    tab = pl.BlockSpec((SEQ, 128), lambda j: (0, 0))
    col = pl.BlockSpec((SEQ, 128), lambda j: (0, j))
    qkv = pl.BlockSpec((SEQ, 384), lambda j: (0, j))
    (dproj,), results = _hosted_call(
        body, name="attn_bwd", steps=N_PAIR,
        in_specs=[qkv, tab, tab, col, col, col, pl.BlockSpec(memory_space=pl.ANY)],
        out_specs=[qkv],
        out_shape=[jax.ShapeDtypeStruct((SEQ, IN_PAD), BF16)],
        scratch_shapes=[pltpu.VMEM((SEQ, 128), F32)] * 7,
        operands=(proj, cos, sin_signed, cat, lse, dcat, dproj), exchanges=exchanges, aliases={6: 0})
    return dproj, results


def _bdot(a, b, dims, precision=None):
    if precision is None:
        a = a.astype(BF16)
        b = b.astype(BF16)
    return lax.dot_general(a, b, (dims, ((0,), (0,))), preferred_element_type=F32, precision=precision)


def _make_bmm(precision):
    @jax.custom_vjp
    def nn(a, b):
        return _bdot(a, b, ((2,), (1,)), precision)

    @jax.custom_vjp
    def nt(a, b):
        return _bdot(a, b, ((2,), (2,)), precision)

    @jax.custom_vjp
    def tn(a, b):
        return _bdot(a, b, ((1,), (1,)), precision)

    nn.defvjp(lambda a, b: (nn(a, b), (a, b)), lambda r, g: (nt(g, r[1]), tn(r[0], g)))
    nt.defvjp(lambda a, b: (nt(a, b), (a, b)), lambda r, g: (nn(g, r[1]), tn(g, r[0])))
    tn.defvjp(lambda a, b: (tn(a, b), (a, b)), lambda r, g: (nt(r[1], g), nn(r[0], g)))
    return nn, nt, tn


BMM, BMM_NT, BMM_TN = _make_bmm(None)
BMM3, BMM3_NT, BMM3_TN = _make_bmm(lax.Precision.HIGH)
MM3, _, _ = _make_mm(lax.Precision.HIGH)


def _head_lanes(t, off):
    lane = lax.broadcasted_iota(jnp.int32, (1, 128), 1)
    return jnp.concatenate(
        [jnp.sum(t * (lane == off + h).astype(F32), axis=1, keepdims=True)[None] for h in range(NDH)], axis=0)


@jax.custom_vjp
def _unit_lower_inverse(a_mat):
    c = a_mat.shape[1]
    eye = (lax.broadcasted_iota(jnp.int32, (c, c), 0) == lax.broadcasted_iota(jnp.int32, (c, c), 1)).astype(F32)
    power = -a_mat
    t_inv = eye + power
    for _ in range(5):
        power = BMM3(power, power)
        t_inv = t_inv + BMM3(t_inv, power)
    return t_inv


def _unit_lower_inverse_fwd(a_mat):
    t_inv = _unit_lower_inverse(a_mat)
    return t_inv, t_inv


def _unit_lower_inverse_bwd(t_inv, d_inv):
    return (-BMM3_NT(BMM3_TN(t_inv, d_inv), t_inv),)


_unit_lower_inverse.defvjp(_unit_lower_inverse_fwd, _unit_lower_inverse_bwd)


DN_STEP_CHUNKS = 4
DN_STEP_ROWS = DN_STEP_CHUNKS * CH
DN_STEPS = NCH // DN_STEP_CHUNKS
DN_BATCH = DN_STEP_CHUNKS * NDH


def _delta_chunks(qr, kr, vr, z, tail, alog_row, dt_row, nw, state):
    c = qr.shape[1]
    tails = [tail[CH * n:CH * (n + 1)] for n in range(DN_STEP_CHUNKS)]
    per_chunk = lambda t: jnp.concatenate([t] * DN_STEP_CHUNKS, axis=0)
    beta = _sigmoid(jnp.concatenate([_head_lanes(t, 0) for t in tails], axis=0))
    a_raw = jnp.concatenate([_head_lanes(t, NDH) for t in tails], axis=0)
    g = -jnp.exp(per_chunk(_head_lanes(alog_row, 0))) * _softplus(a_raw + per_chunk(_head_lanes(dt_row, 0)))

    q = qr * lax.rsqrt(jnp.sum(qr * qr, axis=2, keepdims=True) + EPS) * (128 ** -0.5)
    k = kr * lax.rsqrt(jnp.sum(kr * kr, axis=2, keepdims=True) + EPS)

    ri = lax.broadcasted_iota(jnp.int32, (c, c), 0)
    ci = lax.broadcasted_iota(jnp.int32, (c, c), 1)
    tril = ri >= ci
    lane = lax.broadcasted_iota(jnp.int32, (1, 128), 1)
    pick = [(lane == b).astype(F32) for b in range(DN_BATCH)]
    g_lanes = sum(g[b] * pick[b] for b in range(DN_BATCH))
    g_sums = MM3(tril.astype(F32), g_lanes)
    gc = jnp.concatenate([jnp.sum(g_sums * pick[b], axis=1, keepdims=True)[None] for b in range(DN_BATCH)],
                         axis=0)
    g_row = jnp.swapaxes(jnp.broadcast_to(gc, (DN_BATCH, c, c)), 1, 2)
    decay = jnp.where(tril, jnp.exp(jnp.where(tril, gc - g_row, 0.0)), 0.0)
    kb = k * beta
    t_inv = _unit_lower_inverse(jnp.where(ri > ci, BMM_NT(kb, k) * decay, 0.0))
    eg = jnp.exp(gc)
    u = BMM(t_inv, vr * beta)
    w = BMM(t_inv, kb * eg)
    qk = BMM_NT(q, k) * decay
    g_tot = jnp.sum(g, axis=1, keepdims=True)
    q_dec = q * eg
    k_dec = k * jnp.exp(g_tot - gc)
    outs = []
    for n in range(DN_STEP_CHUNKS):
        heads = slice(NDH * n, NDH * (n + 1))
        v_new = u[heads] - BMM(w[heads], state)
        outs.append(BMM(q_dec[heads], state) + BMM(qk[heads], v_new))
        state = state * jnp.exp(g_tot[heads]) + BMM_TN(k_dec[heads], v_new)
    o = jnp.concatenate(outs, axis=0)
    on = o * lax.rsqrt(jnp.mean(o * o, axis=2, keepdims=True) + EPS) * nw
    return on * _silu(z), state


def _heads(v, off=0):
    return jnp.concatenate([v[None, CH * n:CH * (n + 1), off + 128 * h:off + 128 * (h + 1)]
                            for n in range(DN_STEP_CHUNKS) for h in range(NDH)], axis=0)


def _unheads(t):
    return jnp.concatenate([jnp.concatenate([t[NDH * n + h] for h in range(NDH)], axis=1)
                            for n in range(DN_STEP_CHUNKS)], axis=0)


def _delta_fwd(c_qkv, proj, alog_row, dt_row, nw, cat, exchanges=()):
    def body(c_ref, z_ref, tail_ref, al_ref, dt_ref, nw_ref, _, y_ref, st_ref, state):
        @pl.when(pl.program_id(0) == 0)
        def _():
            state[...] = jnp.zeros_like(state)

        cv = c_ref[...].astype(F32)
        st_ref[0] = state[...]
        y, new_state = _delta_chunks(_heads(cv), _heads(cv, 512), _heads(cv, 1024), _heads(z_ref[...]), tail_ref[...],
                                     al_ref[...], dt_ref[...], nw_ref[...], state[...])
        y_ref[...] = _unheads(y)
        state[...] = new_state

    row = pl.BlockSpec((1, 128), lambda n: (0, 0))
    rows = DN_STEP_ROWS
    return _hosted_call(
        body, name="delta_fwd", steps=DN_STEPS,
        in_specs=[pl.BlockSpec((rows, 1536), lambda n: (n, 0)), pl.BlockSpec((rows, 512), lambda n: (n, DN_Z_COL // 512)),
                  pl.BlockSpec((rows, 128), lambda n: (n, DN_TAIL_BLK)), row, row, row, pl.BlockSpec(memory_space=pl.ANY)],
        out_specs=[pl.BlockSpec((rows, 512), lambda n: (n, 1)),
                   pl.BlockSpec((1, NDH, 128, 128), lambda n: (n, 0, 0, 0))],
        out_shape=[jax.ShapeDtypeStruct((SEQ, 2 * ATTN_W), F32), jax.ShapeDtypeStruct((DN_STEPS, NDH, 128, 128), F32)],
        scratch_shapes=[pltpu.VMEM((NDH, 128, 128), F32)],
        operands=(c_qkv, proj, proj, alog_row, dt_row, nw, cat), exchanges=exchanges, aliases={6: 0})


def _delta_bwd(c_qkv, proj, alog_row, dt_row, nw, states, dcat, exchanges=()):
    def body(c_ref, z_ref, tail_ref, al_ref, dt_ref, nw_ref, st_ref, dy_ref,
             dp_ref, dc_ref, dal_ref, ddt_ref, dnw_ref, dstate):
        @pl.when(pl.program_id(0) == 0)
        def _():
            dstate[...] = jnp.zeros_like(dstate)
            dal_ref[...] = jnp.zeros_like(dal_ref)
            ddt_ref[...] = jnp.zeros_like(ddt_ref)
            dnw_ref[...] = jnp.zeros_like(dnw_ref)

        cv = c_ref[...].astype(F32)
        _, vjp = jax.vjp(_delta_chunks, _heads(cv), _heads(cv, 512), _heads(cv, 1024), _heads(z_ref[...]),
                         tail_ref[...], al_ref[...], dt_ref[...], nw_ref[...], st_ref[0])
        dq, dk, dv, dz, dtail, dal, ddt, dnw, dst = vjp((_heads(dy_ref[...]), dstate[...]))
        dstate[...] = dst
        dc_ref[...] = jnp.concatenate([_unheads(dq), _unheads(dk), _unheads(dv)], axis=1)
        dp_ref[...] = jnp.concatenate([_unheads(dz), dtail, jnp.zeros((DN_STEP_ROWS, 128), F32)], axis=1).astype(BF16)
        dal_ref[...] += dal
        ddt_ref[...] += ddt
        dnw_ref[...] += dnw

    rev = lambda n: DN_STEPS - 1 - n
    row = pl.BlockSpec((1, 128), lambda n: (0, 0))
    rows = DN_STEP_ROWS
    return _hosted_call(
        body, name="delta_bwd", steps=DN_STEPS,
        in_specs=[pl.BlockSpec((rows, 1536), lambda n: (rev(n), 0)),
                  pl.BlockSpec((rows, 512), lambda n: (rev(n), DN_Z_COL // 512)),
                  pl.BlockSpec((rows, 128), lambda n: (rev(n), DN_TAIL_BLK)), row, row, row,
                  pl.BlockSpec((1, NDH, 128, 128), lambda n: (rev(n), 0, 0, 0)),
                  pl.BlockSpec((rows, 512), lambda n: (rev(n), 1))],
        out_specs=[pl.BlockSpec((rows, 768), lambda n: (rev(n), DN_Z_COL // 768)),
                   pl.BlockSpec((rows, 1536), lambda n: (rev(n), 0)), row, row, row],
        out_shape=[jax.ShapeDtypeStruct((SEQ, IN_PAD), BF16), jax.ShapeDtypeStruct((SEQ, 1536), F32)]
        + [jax.ShapeDtypeStruct((1, 128), F32)] * 3,
        scratch_shapes=[pltpu.VMEM((NDH, 128, 128), F32)],
        operands=(c_qkv, proj, proj, alog_row, dt_row, nw, states, dcat), exchanges=exchanges)


def _place():
    x, y, c = lax.axis_index("x"), lax.axis_index("y"), lax.axis_index("c")
    other_chips = [(1 - x, y), (x, 1 - y), (1 - x, 1 - y)]
    return x, y, c, other_chips


def _gather_exchange(shards):
    n = len(shards)

    def copies(ins, outs, sems):
        send_sems, recv_sems, local_sems = sems
        x, y, c, chips = _place()
        me, sibling = (x, y, c), (x, y, 1 - c)

        def copy(b, k, block, to, src=None):
            slot = outs[b].at[4 * block[0] + 2 * block[1] + block[2]]
            return pltpu.make_async_remote_copy(
                src_ref=slot if src is None else src, dst_ref=slot,
                send_sem=send_sems.at[b, k], recv_sem=recv_sems.at[b, k], device_id=to, device_id_type=MESH)

        mine = [pltpu.make_async_copy(ins[b], outs[b].at[4 * x + 2 * y + c], local_sems.at[b]) for b in range(n)]
        first = []
        for b in range(n):
            first.append(copy(b, 0, me, sibling, src=ins[b]))
            first += [copy(b, 1 + j, me, (*chip, c), src=ins[b]) for j, chip in enumerate(chips)]
        over_ici = [copy(b, 1 + j, (*chip, c), me) for b in range(n) for j, chip in enumerate(chips)]
        passed = [copy(b, 4 + j, (*chip, c), sibling) for b in range(n) for j, chip in enumerate(chips)]
        from_sibling = []
        for b in range(n):
            from_sibling.append(copy(b, 0, sibling, me))
            from_sibling += [copy(b, 4 + j, (*chip, 1 - c), me) for j, chip in enumerate(chips)]
        return mine, first, over_ici, passed, from_sibling

    def start(ins, outs, sems):
        mine, first, _, _, _ = copies(ins, outs, sems)
        for cp in mine + first:
            cp.start()

    def middle(ins, outs, sems):
        _, _, over_ici, passed, _ = copies(ins, outs, sems)
        for arrived, onward in zip(over_ici, passed):
            arrived.wait_recv()
            onward.start()

    def finish(ins, outs, sems):
        mine, first, _, passed, from_sibling = copies(ins, outs, sems)
        for cp in from_sibling:
            cp.wait_recv()
        for cp in first + passed:
            cp.wait_send()
        for cp in mine:
            cp.wait()

    return Exchange(shards, [jax.ShapeDtypeStruct((N_DEV,) + s.shape, s.dtype) for s in shards],
                    [pltpu.SemaphoreType.DMA((n, 7)), pltpu.SemaphoreType.DMA((n, 7)), pltpu.SemaphoreType.DMA((n,))],
                    start, middle, finish)


def _sibling_exchange(gs):
    n = len(gs)

    def copies(ins, outs, sems):
        send_sems, recv_sems = sems
        x, y, c, _ = _place()
        return [pltpu.make_async_remote_copy(
            src_ref=ins[b].at[2 * p + (1 - c)], dst_ref=outs[b].at[p],
            send_sem=send_sems.at[b, p], recv_sem=recv_sems.at[b, p],
            device_id=(x, y, 1 - c), device_id_type=MESH) for b in range(n) for p in range(4)]

    def start(ins, outs, sems):
        for cp in copies(ins, outs, sems):
            cp.start()

    def finish(ins, outs, sems):
        for cp in copies(ins, outs, sems):
            cp.wait()

    return Exchange(gs, [jax.ShapeDtypeStruct((4,) + g.shape[1:], g.dtype) for g in gs],
                    [pltpu.SemaphoreType.DMA((n, 4)), pltpu.SemaphoreType.DMA((n, 4))], start, None, finish)


def _chips_exchange(hs):
    n = len(hs)

    def copies(ins, outs, sems):
        send_sems, recv_sems, local_sems = sems
        x, y, c, chips = _place()
        my_chip = 2 * x + y
        local = [pltpu.make_async_copy(ins[b].at[my_chip], outs[b].at[my_chip], local_sems.at[b]) for b in range(n)]
        sends, arrivals = [], []
        for b in range(n):
            for k, (px, py) in enumerate(chips):
                peer = 2 * px + py
                sends.append(pltpu.make_async_remote_copy(
                    src_ref=ins[b].at[peer], dst_ref=outs[b].at[my_chip],
                    send_sem=send_sems.at[b, k], recv_sem=recv_sems.at[b, k],
                    device_id=(px, py, c), device_id_type=MESH))
                arrivals.append(pltpu.make_async_remote_copy(
                    src_ref=ins[b].at[peer], dst_ref=outs[b].at[peer],
                    send_sem=send_sems.at[b, k], recv_sem=recv_sems.at[b, k],
                    device_id=(px, py, c), device_id_type=MESH))
        return local, sends, arrivals

    def start(ins, outs, sems):
        local, sends, _ = copies(ins, outs, sems)
        for cp in local + sends:
            cp.start()

    def finish(ins, outs, sems):
        local, sends, arrivals = copies(ins, outs, sems)
        for cp in arrivals:
            cp.wait_recv()
        for cp in sends:
            cp.wait_send()
        for cp in local:
            cp.wait()

    return Exchange(hs, [jax.ShapeDtypeStruct(h.shape, h.dtype) for h in hs],
                    [pltpu.SemaphoreType.DMA((n, 3)), pltpu.SemaphoreType.DMA((n, 3)), pltpu.SemaphoreType.DMA((n,))],
                    start, None, finish)


def _run_exchange(exchange, name):
    n_in, n_out = len(exchange.operands), len(exchange.out_shapes)

    def body(*refs):
        ins, outs, sems = refs[:n_in], refs[n_in:n_in + n_out], refs[n_in + n_out:]
        exchange.start(ins, outs, sems)
        if exchange.middle is not None:
            exchange.middle(ins, outs, sems)
        exchange.finish(ins, outs, sems)

    return pl.pallas_call(
        body, name=name,
        in_specs=[HBM_SPEC] * n_in, out_specs=[HBM_SPEC] * n_out,
        out_shape=exchange.out_shapes, scratch_shapes=exchange.sems,
    )(*exchange.operands)


def _pair_add(g, r, core, name):
    _, nr, nc = g.shape
    tr = nr // 2 if nr % 32 == 0 else nr

    def body(core_ref, g_ref, r_ref, o_ref):
        o_ref[...] = (g_ref[...].astype(F32) + r_ref[...].astype(F32)).astype(BF16)

    return pl.pallas_call(
        body, name=name,
        grid_spec=pltpu.PrefetchScalarGridSpec(
            num_scalar_prefetch=1, grid=(4, nr // tr),
            in_specs=[pl.BlockSpec((1, tr, nc), lambda p, i, core: (2 * p + core[0], i, 0)),
                      pl.BlockSpec((1, tr, nc), lambda p, i, core: (p, i, 0))],
            out_specs=pl.BlockSpec((1, tr, nc), lambda p, i, core: (p, i, 0))),
        out_shape=jax.ShapeDtypeStruct(r.shape, BF16),
        compiler_params=_cp("parallel", "parallel"),
    )(core, g, r)


def _all_gather_sum_small(v):
    rows = v.shape[0]

    def body(x_ref, sum_ref, out_ref, send_sems, recv_sems, local_sem):
        x, y, c, chips = _place()
        me, sibling = (x, y, c), (x, y, 1 - c)

        def block(px, py, pc):
            return out_ref.at[pl.ds((4 * px + 2 * py + pc) * rows, rows), :]

        def copy(k, blk, to, src=None):
            return pltpu.make_async_remote_copy(
                src_ref=block(*blk) if src is None else src, dst_ref=block(*blk),
                send_sem=send_sems.at[k], recv_sem=recv_sems.at[k], device_id=to, device_id_type=MESH)

        mine = pltpu.make_async_copy(x_ref, block(*me), local_sem)
        mine.start()
        first = [copy(0, me, sibling, src=x_ref)]
        first += [copy(1 + j, me, (*chip, c), src=x_ref) for j, chip in enumerate(chips)]
        for cp in first:
            cp.start()
        passed = [copy(4 + j, (*chip, c), sibling) for j, chip in enumerate(chips)]
        for j, chip in enumerate(chips):
            copy(1 + j, (*chip, c), me).wait_recv()
            passed[j].start()
        copy(0, sibling, me).wait_recv()
        for j, chip in enumerate(chips):
            copy(4 + j, (*chip, 1 - c), me).wait_recv()
        for cp in first + passed:
            cp.wait_send()
        mine.wait()
        total = out_ref[pl.ds(0, rows), :]
        for d in range(1, N_DEV):
            total = total + out_ref[pl.ds(d * rows, rows), :]
        sum_ref[...] = total

    vm = pl.BlockSpec(memory_space=pltpu.VMEM)
    return pl.pallas_call(
        body, name="small_all_reduce",
        in_specs=[vm], out_specs=[vm],
        out_shape=[jax.ShapeDtypeStruct((rows, 128), F32)],
        scratch_shapes=[pltpu.VMEM((N_DEV * rows, 128), F32), pltpu.SemaphoreType.DMA((7,)),
                        pltpu.SemaphoreType.DMA((7,)), pltpu.SemaphoreType.DMA],
    )(v)[0]


def _adamw(w, g, m, v):
    m = ADAM_B1 * m + (1.0 - ADAM_B1) * g
    v = ADAM_B2 * v + (1.0 - ADAM_B2) * (g * g)
    m_hat = m / (1.0 - ADAM_B1 ** ADAM_STEP)
    v_hat = v / (1.0 - ADAM_B2 ** ADAM_STEP)
    delta = -ADAM_LR * (m_hat / (jnp.sqrt(v_hat) + ADAM_EPS) + ADAM_WD * w)
    return delta, m, v


ADAM_TILE = dict(w_in=(IN_COLS // N_DEV, 256), w_out=(128, D_MODEL), ffn_w_in=(176, D_MODEL), ffn_w_out=(176, D_MODEL))


def _sum_chips(p):
    p = p.astype(F32)
    return (p[0] + p[1]) + (p[2] + p[3])


def _adamw_sharded(parts, w, m, v, tile, name):
    nl, nr, nc = w.shape
    tr, tc = tile

    def body(*refs):
        p_refs, (w_ref, m_ref, v_ref, g_ref, d_ref, nm_ref, nv_ref) = refs[:nl], refs[nl:]
        layer = pl.program_id(0)
        p = p_refs[0][...]
        for l in range(1, nl):
            p = jnp.where(layer == l, p_refs[l][...], p)
        g = _sum_chips(p)
        delta, nm, nv = _adamw(w_ref[0], g, m_ref[0], v_ref[0])
        g_ref[0] = g
        d_ref[0] = delta
        nm_ref[0] = nm
        nv_ref[0] = nv

    blk = pl.BlockSpec((1, tr, tc), lambda l, i, j: (l, i, j))
    return pl.pallas_call(
        body, name=name, grid=(nl, nr // tr, nc // tc),
        in_specs=[pl.BlockSpec((4, tr, tc), lambda l, i, j, own=own: (0, jnp.where(l == own, i, 0), j))
                  for own in range(nl)] + [blk, blk, blk],
        out_specs=[blk] * 4,
        out_shape=[jax.ShapeDtypeStruct(w.shape, F32)] * 4,
        compiler_params=_cp("parallel", "parallel", "parallel"),
    )(*parts, w, m, v)


def _adamw_small(g, w, m, v):
    def body(g_ref, w_ref, m_ref, v_ref, d_ref, nm_ref, nv_ref):
        delta, nm, nv = _adamw(w_ref[...], g_ref[...], m_ref[...], v_ref[...])
        d_ref[...] = delta
        nm_ref[...] = nm
        nv_ref[...] = nv

    return pl.pallas_call(
        body, name="adamw_small",
        out_shape=[jax.ShapeDtypeStruct(g.shape, F32)] * 3,
    )(g, w, m, v)


def _packed_rows(n):
    return -(-n // 1024) * 8


def _pack(arrays, rows):
    pieces = []
    for a in arrays:
        flat = a.reshape(-1).astype(F32)
        nr = _packed_rows(flat.shape[0])
        pieces.append(jnp.pad(flat, (0, nr * 128 - flat.shape[0])).reshape(nr, 128))
    used = sum(p.shape[0] for p in pieces)
    return jnp.concatenate(pieces + [jnp.zeros((rows - used, 128), F32)] * (rows > used), axis=0)


def _unpack(packed, shapes):
    out, row = [], 0
    for s in shapes:
        n = math.prod(s)
        out.append(packed[row:row + _packed_rows(n)].reshape(-1)[:n].reshape(s))
        row += _packed_rows(n)
    return out


def _row(v, width=None):
    v = v.reshape(1, -1)
    return v if width is None else jnp.pad(v, ((0, 0), (0, width - v.shape[1])))


def _layer_fwd(x, h, wts, tables, hosted, last_step):
    proj = _matmul(h, wts["w_in"], tb=True, tm=SEQ, tn=768, tk=1024, name="mm_proj")
    (cat, lse), got = _attn_fwd(proj, *tables, exchanges=hosted["attn"][0])
    hosted["attn"][1](got)
    c_qkv = _dnconv_fwd(proj, wts["dn_conv_w"])
    (cat, states), got = _delta_fwd(c_qkv, proj, wts["dn_a_log"], wts["dn_dt_bias"], wts["dn_norm_w"], cat,
                                    exchanges=hosted["delta"][0])
    hosted["delta"][1](got)
    mix = _matmul(cat, wts["w_out"], tm=512, tn=1024, tk=1024, name="mm_mix")
    x1, h2 = _resnorm_norm_fwd(x, mix, wts["norm_post_mix"], wts["norm_pre_ffn"], "norm_post_mix")
    pre = _matmul(h2, wts["ffn_w_in"], tb=True, tm=SEQ, tn=512, tk=1024, name="mm_ffn_in", out_dtype=BF16)
    act, got = _ffact_fwd(pre, wts["ffn_conv_w"], wts["ffn_conv_b"], exchanges=hosted["ffact"][0])
    hosted["ffact"][1](got)
    f = _matmul(act, wts["ffn_w_out"], tm=512, tn=1024, tk=D_FF, name="mm_ffn_out")
    saved = dict(x=x, h=h, proj=proj, lse=lse, c_qkv=c_qkv, states=states, cat=cat, mix=mix, x1=x1, h2=h2, pre=pre,
                 act=act, f=f)
    return last_step(x1, f), saved


def _layer_bwd(dx2, wts, s, tables, ffact_exchanges=(), delta_exchanges=None, attn_exchanges=None, head=None,
               below=None):
    g = {}
    if head is None:
        head = _norm_bwd(s["f"], wts["norm_post_ffn"], dx2, None, "norm_post_ffn_bwd", BF16)
    df, g["norm_post_ffn"] = head
    dact = _matmul(df, wts["ffn_w_out"], tb=True, tm=SEQ, tn=1408, tk=1024, name="mm_dact", out_dtype=BF16)
    g["ffn_w_out"] = _matmul(s["act"], df, ta=True, tm=1408, tn=512, tk=SEQ, name="mm_dw_ffn_out", out_dtype=BF16)
    (dpre, g["ffn_conv_w"], g["ffn_conv_b"]), got = _ffact_bwd(s["pre"], wts["ffn_conv_w"], wts["ffn_conv_b"], dact,
                                                               exchanges=ffact_exchanges)
    dh2 = _matmul(dpre, wts["ffn_w_in"], tm=512, tn=1024, tk=2 * D_FF, name="mm_dh2")
    g["ffn_w_in"] = _matmul(dpre, s["h2"], ta=True, tm=512, tn=1024, tk=SEQ, name="mm_dw_ffn_in", out_dtype=BF16)
    dx1, dmix, g["norm_pre_ffn"], g["norm_post_mix"] = _norm_bwd_pair(
        s["x1"], wts["norm_pre_ffn"], dh2, dx2, s["mix"], wts["norm_post_mix"], "norm_pre_ffn_bwd")
    dcat = _matmul(dmix, wts["w_out"], tb=True, tm=SEQ, tn=512, tk=1024, name="mm_dcat")
    g["w_out"] = _matmul(s["cat"], dmix, ta=True, tm=1024, tn=512, tk=SEQ, name="mm_dw_out", out_dtype=BF16)
    (dproj, dc, g["dn_a_log"], g["dn_dt_bias"], g["dn_norm_w"]), got = _delta_bwd(
        s["c_qkv"], s["proj"], wts["dn_a_log"], wts["dn_dt_bias"], wts["dn_norm_w"], s["states"], dcat,
        exchanges=delta_exchanges(g, got) if delta_exchanges is not None else ())
    dproj, got = _attn_bwd(s["proj"], *tables, s["cat"], s["lse"], dcat, dproj,
                           exchanges=attn_exchanges(got) if attn_exchanges is not None else ())
    dproj, g["dn_conv_w"] = _dnconv_bwd(s["proj"], wts["dn_conv_w"], dc, dproj)
    dh = _matmul(dproj, wts["w_in"], tm=512, tn=1024, tk=IN_PAD, name="mm_dh")
    g["w_in"] = _matmul(dproj, s["h"], ta=True, tm=768, tn=1024, tk=SEQ, name="mm_dw_in", out_dtype=BF16)
    if below is None:
        dx, g["norm_pre_mix"] = _norm_bwd(s["x"], wts["norm_pre_mix"], dh, dx1, "norm_pre_mix_bwd")
        return dx, g, got, None
    dx, df_below, g["norm_pre_mix"], dw_below = _norm_bwd_pair(s["x"], wts["norm_pre_mix"], dh, dx1, *below,
                                                               "norm_pre_mix_bwd")
    return dx, g, got, (df_below, dw_below)


BIG = ("w_in", "w_out", "ffn_w_in", "ffn_w_out")
COLUMN_SHARDED = ("w_in", "ffn_w_in")
SMALL_SHARDED = ("dn_conv_w", "ffn_conv_w")
REPLICATED = ("dn_a_log", "dn_dt_bias", "dn_norm_w", "ffn_conv_b", "norm_pre_mix", "norm_post_mix", "norm_pre_ffn",
              "norm_post_ffn")
WEIGHTS = ("w_in", "dn_conv_w", "dn_a_log", "dn_dt_bias", "dn_norm_w", "w_out", "ffn_w_in", "ffn_conv_w", "ffn_conv_b",
           "ffn_w_out", "norm_pre_mix", "norm_post_mix", "norm_pre_ffn", "norm_post_ffn")
FULL_SHAPE = dict(dn_conv_w=(DEPTH, 4, 1536), ffn_conv_w=(DEPTH, 3, 2 * D_FF), dn_a_log=(DEPTH, NDH),
                  dn_dt_bias=(DEPTH, NDH), dn_norm_w=(DEPTH, 128), ffn_conv_b=(DEPTH, 2 * D_FF),
                  norm_pre_mix=(DEPTH, D_MODEL), norm_post_mix=(DEPTH, D_MODEL), norm_pre_ffn=(DEPTH, D_MODEL),
                  norm_post_ffn=(DEPTH, D_MODEL))
SMALL_GRAD_ORDER = REPLICATED + SMALL_SHARDED
SMALL_GRAD_ROWS = 544
SMALL_W_ROWS = 56
SMALL_ADAM_ROWS = 232


def _w_in_rows_to_kernel_order(t):
    qkv = t[:QKV_W].reshape(3, N_PAIR, 128, -1).swapaxes(0, 1).reshape(QKV_W, -1)
    return jnp.pad(jnp.concatenate([qkv, t[QKV_W:]], axis=0), ((0, IN_PAD - IN_COLS), (0, 0)))


def _w_in_rows_from_kernel_order(t):
    qkv = t[:QKV_W].reshape(N_PAIR, 3, 128, -1).swapaxes(0, 1).reshape(QKV_W, -1)
    return jnp.concatenate([qkv, t[QKV_W:IN_COLS]], axis=0)


def _interleave_ff_rows(t):
    return t.reshape(2, FF_BLKS, 128, -1).swapaxes(0, 1).reshape(2 * D_FF, -1)


def _deinterleave_ff_rows(t):
    return t.reshape(FF_BLKS, 2, 128, -1).swapaxes(0, 1).reshape(2 * D_FF, -1)


def kernel(x, w_in, dn_conv_w, dn_a_log, dn_dt_bias, dn_norm_w, w_out, ffn_w_in, ffn_conv_w, ffn_conv_b, ffn_w_out, norm_pre_mix, norm_post_mix, norm_pre_ffn, norm_post_ffn, loss_target, m_w_in, m_dn_conv_w, m_dn_a_log, m_dn_dt_bias, m_dn_norm_w, m_w_out, m_ffn_w_in, m_ffn_conv_w, m_ffn_conv_b, m_ffn_w_out, m_norm_pre_mix, m_norm_post_mix, m_norm_pre_ffn, m_norm_post_ffn, v_w_in, v_dn_conv_w, v_dn_a_log, v_dn_dt_bias, v_dn_norm_w, v_w_out, v_ffn_w_in, v_ffn_conv_w, v_ffn_conv_b, v_ffn_w_out, v_norm_pre_mix, v_norm_post_mix, v_norm_pre_ffn, v_norm_post_ffn):
    local = dict(w_in=w_in, dn_conv_w=dn_conv_w, dn_a_log=dn_a_log, dn_dt_bias=dn_dt_bias, dn_norm_w=dn_norm_w,
                 w_out=w_out, ffn_w_in=ffn_w_in, ffn_conv_w=ffn_conv_w, ffn_conv_b=ffn_conv_b, ffn_w_out=ffn_w_out,
                 norm_pre_mix=norm_pre_mix, norm_post_mix=norm_post_mix, norm_pre_ffn=norm_pre_ffn,
                 norm_post_ffn=norm_post_ffn)
    mom_m = dict(w_in=m_w_in, dn_conv_w=m_dn_conv_w, dn_a_log=m_dn_a_log, dn_dt_bias=m_dn_dt_bias,
                 dn_norm_w=m_dn_norm_w, w_out=m_w_out, ffn_w_in=m_ffn_w_in, ffn_conv_w=m_ffn_conv_w,
                 ffn_conv_b=m_ffn_conv_b, ffn_w_out=m_ffn_w_out, norm_pre_mix=m_norm_pre_mix,
                 norm_post_mix=m_norm_post_mix, norm_pre_ffn=m_norm_pre_ffn, norm_post_ffn=m_norm_post_ffn)
    mom_v = dict(w_in=v_w_in, dn_conv_w=v_dn_conv_w, dn_a_log=v_dn_a_log, dn_dt_bias=v_dn_dt_bias,
                 dn_norm_w=v_dn_norm_w, w_out=v_w_out, ffn_w_in=v_ffn_w_in, ffn_conv_w=v_ffn_conv_w,
                 ffn_conv_b=v_ffn_conv_b, ffn_w_out=v_ffn_w_out, norm_pre_mix=v_norm_pre_mix,
                 norm_post_mix=v_norm_post_mix, norm_pre_ffn=v_norm_pre_ffn, norm_post_ffn=v_norm_post_ffn)
    dev = 4 * lax.axis_index("x") + 2 * lax.axis_index("y") + lax.axis_index("c")
    core = lax.axis_index("c").astype(jnp.int32).reshape(1)

    def shard(n, l):
        s = local[n].transpose(0, 2, 1) if n in COLUMN_SHARDED else local[n]
        return s[l].astype(BF16)

    def matrix(n, gathered):
        if n == "w_in":
            return _w_in_rows_to_kernel_order(gathered.reshape(IN_COLS, D_MODEL))
        if n == "ffn_w_in":
            return _interleave_ff_rows(gathered.reshape(2 * D_FF, D_MODEL))
        return gathered.reshape(-1, D_MODEL)

    small_w = _pack([dn_conv_w, ffn_conv_w], SMALL_W_ROWS)
    g_w_in0, g_small = _run_exchange(_gather_exchange([shard("w_in", 0), small_w]), "weights_all_gather")
    n_dn, n_ff = DEPTH * 4 * 192, DEPTH * 3 * 704
    dn_rows = _packed_rows(n_dn)
    sm_dn = g_small[:, :dn_rows].reshape(N_DEV, -1)[:, :n_dn]
    sm_ff = g_small[:, dn_rows:].reshape(N_DEV, -1)[:, :n_ff]
    full_dn_conv = sm_dn.reshape(N_DEV, DEPTH, 4, 192).transpose(1, 2, 0, 3).reshape(DEPTH, 4, 1536)
    full_ff_conv = _interleave_ff(sm_ff.reshape(N_DEV, DEPTH, 3, 704).transpose(1, 2, 0, 3).reshape(DEPTH, 3, 2 * D_FF))

    def small_weights(l):
        wts = dict(dn_conv_w=full_dn_conv[l], ffn_conv_w=full_ff_conv[l], ffn_conv_b=_interleave_ff(_row(ffn_conv_b[l])),
                   dn_a_log=_row(dn_a_log[l], 128), dn_dt_bias=_row(dn_dt_bias[l], 128))
        for n in ("dn_norm_w", "norm_pre_mix", "norm_post_mix", "norm_pre_ffn", "norm_post_ffn"):
            wts[n] = _row(local[n][l])
        return wts

    weights = [small_weights(l) for l in range(DEPTH)]
    weights[0]["w_in"] = matrix("w_in", g_w_in0)

    def gather_behind(wanted):
        def deliver(got):
            for (n, l), g in zip(wanted, got[0]):
                weights[l][n] = matrix(n, g)

        return [_gather_exchange([shard(n, l) for n, l in wanted])], deliver

    nothing = ((), lambda got: None)

    tables = _rope_tables()
    h0 = _norm_fwd(x[0], weights[0]["norm_pre_mix"], "norm_pre_mix")
    (act, h1), saved0 = _layer_fwd(
        x[0], h0, weights[0], tables,
        dict(attn=gather_behind([("ffn_w_in", 0)]), delta=gather_behind([("w_out", 0), ("ffn_w_out", 0)]),
             ffact=gather_behind([("w_in", 1)])),
        lambda x1, f: _resnorm_norm_fwd(x1, f, weights[0]["norm_post_ffn"], weights[1]["norm_pre_mix"], "norm_post_ffn"))
    (loss_part, dact), saved1 = _layer_fwd(
        act, h1, weights[1], tables,
        dict(attn=gather_behind([("ffn_w_in", 1)]), delta=gather_behind([("w_out", 1), ("ffn_w_out", 1)]), ffact=nothing),
        lambda x1, f: _resnorm_loss(x1, f, weights[1]["norm_post_ffn"], loss_target[0]))

    def to_devices(name, t):
        if name == "w_in":
            t = _w_in_rows_from_kernel_order(t)
        if name == "ffn_w_in":
            t = _deinterleave_ff_rows(t)
        return t.reshape(N_DEV, t.shape[0] // N_DEV, t.shape[1])

    def pair_sums(names, layer, to_dev, from_sibling):
        return [_pair_add(gd, r, core, "grads_pair_add_%s_%d" % (n, layer))
                for n, gd, r in zip(names, to_dev, from_sibling)]

    early = ("w_out", "ffn_w_in", "ffn_w_out")
    grads, parts, stash = [None] * DEPTH, {}, {}

    def delta_exchanges1(g, got_ffact):
        stash["early1"] = [to_devices(n, g[n]) for n in early]
        return [_sibling_exchange(stash["early1"])]

    def attn_exchanges1(got_delta):
        return [_chips_exchange(pair_sums(early, 1, stash["early1"], got_delta[0]))]

    dact, grads[1], got_attn, head0 = _layer_bwd(dact, weights[1], saved1, tables, (), delta_exchanges1, attn_exchanges1,
                                                 below=(saved0["f"], weights[0]["norm_post_ffn"]))
    for n, p in zip(early, got_attn[0]):
        parts[n, 1] = p
    w_in1 = [to_devices("w_in", grads[1]["w_in"])]

    def delta_exchanges0(g, got_ffact):
        stash["early0"] = [to_devices(n, g[n]) for n in early]
        return [_chips_exchange(pair_sums(("w_in",), 1, w_in1, got_ffact[0])), _sibling_exchange(stash["early0"])]

    def attn_exchanges0(got_delta):
        parts["w_in", 1], = got_delta[0]
        return [_chips_exchange(pair_sums(early, 0, stash["early0"], got_delta[1]))]

    dact, grads[0], got_attn, _ = _layer_bwd(dact, weights[0], saved0, tables, [_sibling_exchange(w_in1)],
                                             delta_exchanges0, attn_exchanges0, head=head0)
    for n, p in zip(early, got_attn[0]):
        parts[n, 0] = p
    grad_x = dact[None]
    last = [to_devices("w_in", grads[0]["w_in"])]
    from_sibling = _run_exchange(_sibling_exchange(last), "grads_to_sibling")
    parts["w_in", 0], = _run_exchange(_chips_exchange(pair_sums(("w_in",), 0, last, from_sibling)), "grads_to_chips")

    def small_grad(name):
        t = jnp.stack([grads[l][name] for l in range(DEPTH)])
        if name in ("dn_a_log", "dn_dt_bias"):
            t = t[:, 0, :NDH]
        if name in ("ffn_conv_w", "ffn_conv_b"):
            t = _deinterleave_ff(t)
        return t.reshape(FULL_SHAPE[name])

    small_part = _pack([small_grad(n) for n in SMALL_GRAD_ORDER] + [loss_part[0, :1]], SMALL_GRAD_ROWS)
    small_sum = _all_gather_sum_small(small_part)
    small_g = dict(zip(SMALL_GRAD_ORDER + ("loss",), _unpack(small_sum, [FULL_SHAPE[n] for n in SMALL_GRAD_ORDER] + [(1,)])))
    loss = small_g["loss"][0]
    small_g["dn_conv_w"] = lax.dynamic_slice_in_dim(small_g["dn_conv_w"], dev * 192, 192, axis=2)
    small_g["ffn_conv_w"] = lax.dynamic_slice_in_dim(small_g["ffn_conv_w"], dev * 704, 704, axis=2)

    out_g, out_d, out_m, out_v = {}, {}, {}, {}
    for n in BIG:
        turn = (lambda t: t.transpose(0, 2, 1)) if n in COLUMN_SHARDED else (lambda t: t)
        outs = _adamw_sharded([parts[n, l] for l in range(DEPTH)], turn(local[n]), turn(mom_m[n]), turn(mom_v[n]),
                              ADAM_TILE[n], "adamw_" + n)
        out_g[n], out_d[n], out_m[n], out_v[n] = [turn(t) for t in outs]
    shapes = [small_g[n].shape for n in SMALL_GRAD_ORDER]
    d_s, m_s, v_s = _adamw_small(_pack([small_g[n] for n in SMALL_GRAD_ORDER], SMALL_ADAM_ROWS),
                                 _pack([local[n] for n in SMALL_GRAD_ORDER], SMALL_ADAM_ROWS),
                                 _pack([mom_m[n] for n in SMALL_GRAD_ORDER], SMALL_ADAM_ROWS),
                                 _pack([mom_v[n] for n in SMALL_GRAD_ORDER], SMALL_ADAM_ROWS))
    for n, d, m, v in zip(SMALL_GRAD_ORDER, _unpack(d_s, shapes), _unpack(m_s, shapes), _unpack(v_s, shapes)):
        out_g[n], out_d[n], out_m[n], out_v[n] = small_g[n], d, m, v
    return (loss, grad_x, *[out_g[n] for n in WEIGHTS], *[out_d[n] for n in WEIGHTS],
            *[out_m[n] for n in WEIGHTS], *[out_v[n] for n in WEIGHTS])
```

```python
import functools
import math

import jax
import jax.numpy as jnp
from jax import lax
from jax.experimental import pallas as pl
from jax.experimental.pallas import tpu as pltpu

F32 = jnp.float32
BF16 = jnp.bfloat16
MESH = pl.DeviceIdType.MESH

N_DEV = 8
SEQ = 2048
D_MODEL = 1024
DEPTH = 2
N_PAIR = 4
HEAD_DIM = 64
ATTN_W = 512
ATTN_BLK = 128
DILATIONS = (1, 4, 16)
SEGMENT_BLOCKS = (16, 4, 1)
N_BLK = SEQ // ATTN_BLK
NDH = 4
CH = 64
NCH = SEQ // CH
IN_COLS = 3592
IN_PAD = 3840
QKV_W = 3 * ATTN_W
DN_QKV_BLK0 = QKV_W // 128
DN_QKV_BLKS = 1536 // 128
DN_Z_COL = 3072
DN_TAIL_BLK = 3584 // 128
D_FF = 2816
FF_BLKS = D_FF // 128
EPS = 1e-6
NEG = -1e30
ROPE_THETA = 10000.0

ADAM_LR, ADAM_B1, ADAM_B2, ADAM_EPS, ADAM_WD, ADAM_STEP = 0.001, 0.9, 0.999, 1e-08, 0.01, 10

VMEM_LIMIT = 56 * 1024 * 1024


def _cp(*sem):
    return pltpu.CompilerParams(dimension_semantics=sem, vmem_limit_bytes=VMEM_LIMIT)


class Exchange:
    def __init__(self, operands, out_shapes, sems, start, middle, finish):
        self.operands, self.out_shapes, self.sems = list(operands), list(out_shapes), list(sems)
        self.start, self.middle, self.finish = start, middle, finish


HBM_SPEC = pl.BlockSpec(memory_space=pltpu.HBM)


def _hosted_call(body, *, name, steps, in_specs, out_specs, out_shape, scratch_shapes, operands, exchanges=(),
                 aliases=None):
    n_in, n_out, n_scr = len(in_specs), len(out_specs), len(scratch_shapes)

    def take(refs, pos, counts):
        groups = []
        for c in counts:
            groups.append(refs[pos:pos + c])
            pos += c
        return groups, pos

    def full_body(*refs):
        ins, pos = refs[:n_in], n_in
        ex_ins, pos = take(refs, pos, [len(e.operands) for e in exchanges])
        outs, pos = refs[pos:pos + n_out], pos + n_out
        ex_outs, pos = take(refs, pos, [len(e.out_shapes) for e in exchanges])
        scr, pos = refs[pos:pos + n_scr], pos + n_scr
        ex_sems, pos = take(refs, pos, [len(e.sems) for e in exchanges])
        step = pl.program_id(0)
        for e, a, b, s in zip(exchanges, ex_ins, ex_outs, ex_sems):
            pl.when(step == 0)(functools.partial(e.start, a, b, s))
            if e.middle is not None:
                pl.when(step == (3 * steps) // 4)(functools.partial(e.middle, a, b, s))
        body(*ins, *outs, *scr)
        for e, a, b, s in zip(exchanges, ex_ins, ex_outs, ex_sems):
            pl.when(step == steps - 1)(functools.partial(e.finish, a, b, s))

    n_ex_in = sum(len(e.operands) for e in exchanges)
    n_ex_out = sum(len(e.out_shapes) for e in exchanges)
    results = pl.pallas_call(
        full_body, name=name, grid=(steps,),
        in_specs=list(in_specs) + [HBM_SPEC] * n_ex_in,
        out_specs=list(out_specs) + [HBM_SPEC] * n_ex_out,
        out_shape=list(out_shape) + [s for e in exchanges for s in e.out_shapes],
        scratch_shapes=list(scratch_shapes) + [s for e in exchanges for s in e.sems],
        input_output_aliases=aliases or {},
        compiler_params=_cp("arbitrary"),
    )(*operands, *[a for e in exchanges for a in e.operands])
    ex_results, _ = take(results, n_out, [len(e.out_shapes) for e in exchanges])
    return results[:n_out], ex_results


def _dot(a, b, dims, precision=None):
    if precision is None:
        a = a.astype(BF16)
        b = b.astype(BF16)
    return lax.dot_general(a, b, (dims, ((), ())), preferred_element_type=F32, precision=precision)


def _make_mm(precision):
    @jax.custom_vjp
    def nn(a, b):
        return _dot(a, b, ((1,), (0,)), precision)

    @jax.custom_vjp
    def nt(a, b):
        return _dot(a, b, ((1,), (1,)), precision)

    @jax.custom_vjp
    def tn(a, b):
        return _dot(a, b, ((0,), (0,)), precision)

    nn.defvjp(lambda a, b: (nn(a, b), (a, b)), lambda r, g: (nt(g, r[1]), tn(r[0], g)))
    nt.defvjp(lambda a, b: (nt(a, b), (a, b)), lambda r, g: (nn(g, r[1]), tn(g, r[0])))
    tn.defvjp(lambda a, b: (tn(a, b), (a, b)), lambda r, g: (nt(r[1], g), nn(r[0], g)))
    return nn, nt, tn


def _matmul(a, b, *, ta=False, tb=False, tm, tn, tk, name, out_dtype=F32):
    (k_dim, m_dim) = a.shape if ta else a.shape[::-1]
    (n_dim, k2) = b.shape if tb else b.shape[::-1]
    assert k_dim == k2 and m_dim % tm == 0 and n_dim % tn == 0 and k_dim % tk == 0, (a.shape, b.shape, tm, tn, tk)
    nk = k_dim // tk
    dims = ((0 if ta else 1,), (1 if tb else 0,))

    def body(a_ref, b_ref, o_ref, *acc):
        p = _dot(a_ref[...], b_ref[...], dims)
        if nk == 1:
            o_ref[...] = p.astype(out_dtype)
            return
        acc_ref, k = acc[0], pl.program_id(2)

        @pl.when(k == 0)
        def _():
            acc_ref[...] = p

        @pl.when(k > 0)
        def _():
            acc_ref[...] += p

        @pl.when(k == nk - 1)
        def _():
            o_ref[...] = acc_ref[...].astype(out_dtype)

    a_spec = pl.BlockSpec((tk, tm), lambda i, j, k: (k, i)) if ta else pl.BlockSpec((tm, tk), lambda i, j, k: (i, k))
    b_spec = pl.BlockSpec((tn, tk), lambda i, j, k: (j, k)) if tb else pl.BlockSpec((tk, tn), lambda i, j, k: (k, j))
    return pl.pallas_call(
        body, name=name,
        grid=(m_dim // tm, n_dim // tn, nk),
        in_specs=[a_spec, b_spec],
        out_specs=pl.BlockSpec((tm, tn), lambda i, j, k: (i, j)),
        out_shape=jax.ShapeDtypeStruct((m_dim, n_dim), out_dtype),
        scratch_shapes=[pltpu.VMEM((tm, tn), F32)] if nk > 1 else [],
        compiler_params=_cp("parallel", "parallel", "arbitrary"),
    )(a, b)


NORM_ROWS = 256


def _rms(x, w):
    return x * lax.rsqrt(jnp.mean(x * x, axis=1, keepdims=True) + EPS) * w


def _norm_fwd(x, w_row, name, out_dtype=BF16):
    def body(x_ref, w_ref, o_ref):
        o_ref[...] = _rms(x_ref[...], w_ref[...]).astype(out_dtype)

    return pl.pallas_call(
        body, name=name, grid=(SEQ // NORM_ROWS,),
        in_specs=[pl.BlockSpec((NORM_ROWS, D_MODEL), lambda i: (i, 0)), pl.BlockSpec((1, D_MODEL), lambda i: (0, 0))],
        out_specs=pl.BlockSpec((NORM_ROWS, D_MODEL), lambda i: (i, 0)),
        out_shape=jax.ShapeDtypeStruct((SEQ, D_MODEL), out_dtype),
        compiler_params=_cp("parallel"),
    )(x, w_row)


def _resnorm_norm_fwd(x, f, w_row, next_w_row, name):
    def body(x_ref, f_ref, w_ref, nw_ref, o_ref, h_ref):
        out = x_ref[...] + _rms(f_ref[...], w_ref[...])
        o_ref[...] = out
        h_ref[...] = _rms(out, nw_ref[...]).astype(BF16)

    blk = pl.BlockSpec((NORM_ROWS, D_MODEL), lambda i: (i, 0))
    row = pl.BlockSpec((1, D_MODEL), lambda i: (0, 0))
    return pl.pallas_call(
        body, name=name, grid=(SEQ // NORM_ROWS,),
        in_specs=[blk, blk, row, row],
        out_specs=[blk, blk],
        out_shape=[jax.ShapeDtypeStruct((SEQ, D_MODEL), F32), jax.ShapeDtypeStruct((SEQ, D_MODEL), BF16)],
        compiler_params=_cp("parallel"),
    )(x, f, w_row, next_w_row)


def _norm_bwd(x, w_row, dy, add, name, dx_dtype=F32):
    has_add = add is not None

    def body(*refs):
        if has_add:
            x_ref, w_ref, dy_ref, add_ref, dx_ref, dw_ref = refs
        else:
            x_ref, w_ref, dy_ref, dx_ref, dw_ref = refs
        _, vjp = jax.vjp(_rms, x_ref[...], w_ref[...])
        dx, dw = vjp(dy_ref[...])
        dx_ref[...] = (dx + add_ref[...] if has_add else dx).astype(dx_dtype)

        @pl.when(pl.program_id(0) == 0)
        def _():
            dw_ref[...] = jnp.zeros_like(dw_ref)

        dw_ref[...] += dw

    blk = pl.BlockSpec((NORM_ROWS, D_MODEL), lambda i: (i, 0))
    row = pl.BlockSpec((1, D_MODEL), lambda i: (0, 0))
    ins = [x, w_row, dy] + ([add] if has_add else [])
    return pl.pallas_call(
        body, name=name, grid=(SEQ // NORM_ROWS,),
        in_specs=[blk, row, blk] + ([blk] if has_add else []),
        out_specs=[blk, row],
        out_shape=[jax.ShapeDtypeStruct((SEQ, D_MODEL), dx_dtype), jax.ShapeDtypeStruct((1, D_MODEL), F32)],
        compiler_params=_cp("arbitrary"),
    )(*ins)


def _norm_bwd_pair(x_a, w_a, dy_a, add, x_b, w_b, name):
    def body(xa_ref, wa_ref, dya_ref, add_ref, xb_ref, wb_ref, dxa_ref, dxb_ref, dwa_ref, dwb_ref):
        _, vjp_a = jax.vjp(_rms, xa_ref[...], wa_ref[...])
        dxa, dwa = vjp_a(dya_ref[...])
        dxa = dxa + add_ref[...]
        _, vjp_b = jax.vjp(_rms, xb_ref[...], wb_ref[...])
        dxb, dwb = vjp_b(dxa)
        dxa_ref[...] = dxa
        dxb_ref[...] = dxb.astype(BF16)

        @pl.when(pl.program_id(0) == 0)
        def _():
            dwa_ref[...] = jnp.zeros_like(dwa_ref)
            dwb_ref[...] = jnp.zeros_like(dwb_ref)

        dwa_ref[...] += dwa
        dwb_ref[...] += dwb

    blk = pl.BlockSpec((NORM_ROWS, D_MODEL), lambda i: (i, 0))
    row = pl.BlockSpec((1, D_MODEL), lambda i: (0, 0))
    return pl.pallas_call(
        body, name=name, grid=(SEQ // NORM_ROWS,),
        in_specs=[blk, row, blk, blk, blk, row],
        out_specs=[blk, blk, row, row],
        out_shape=[jax.ShapeDtypeStruct((SEQ, D_MODEL), F32), jax.ShapeDtypeStruct((SEQ, D_MODEL), BF16),
                   jax.ShapeDtypeStruct((1, D_MODEL), F32), jax.ShapeDtypeStruct((1, D_MODEL), F32)],
        compiler_params=_cp("arbitrary"),
    )(x_a, w_a, dy_a, add, x_b, w_b)


def _resnorm_loss(x, f, w_row, target):
    def body(x_ref, f_ref, w_ref, t_ref, loss_ref, dy_ref):
        err = x_ref[...] + _rms(f_ref[...], w_ref[...]) - t_ref[...]
        dy_ref[...] = err * (1.0 / D_MODEL)

        @pl.when(pl.program_id(0) == 0)
        def _():
            loss_ref[...] = jnp.zeros_like(loss_ref)

        part = jnp.sum(jnp.sum(err * err, axis=1, keepdims=True) * (1.0 / D_MODEL), axis=0, keepdims=True)
        loss_ref[...] += 0.5 * jnp.broadcast_to(part, loss_ref.shape)

    blk = pl.BlockSpec((NORM_ROWS, D_MODEL), lambda i: (i, 0))
    return pl.pallas_call(
        body, name="norm_post_ffn_loss", grid=(SEQ // NORM_ROWS,),
        in_specs=[blk, blk, pl.BlockSpec((1, D_MODEL), lambda i: (0, 0)), blk],
        out_specs=[pl.BlockSpec((1, 128), lambda i: (0, 0)), blk],
        out_shape=[jax.ShapeDtypeStruct((1, 128), F32), jax.ShapeDtypeStruct((SEQ, D_MODEL), F32)],
        compiler_params=_cp("arbitrary"),
    )(x, f, w_row, target)


def _make_shift(j):
    def down(x):
        row = lax.broadcasted_iota(jnp.int32, x.shape, 0)
        return jnp.where(row >= j, pltpu.roll(x, j, 0), 0.0)

    def up(x):
        n = x.shape[0]
        row = lax.broadcasted_iota(jnp.int32, x.shape, 0)
        return jnp.where(row < n - j, pltpu.roll(x, n - j, 0), 0.0)

    f = jax.custom_vjp(down)
    f.defvjp(lambda x: (down(x), None), lambda _, g: (up(g),))
    return f


_SHIFT = {j: _make_shift(j) for j in (1, 2, 3)}


def _causal_conv(x, taps):
    n = len(taps)
    acc = x * taps[n - 1]
    for k in range(n - 1):
        acc = acc + _SHIFT[n - 1 - k](x) * taps[k]
    return acc


def _tap_rows(w_ref, lanes=slice(None)):
    return tuple(w_ref[k:k + 1, lanes] for k in range(w_ref.shape[0]))


def _sigmoid(x):
    return 1.0 / (1.0 + jnp.exp(-x))


def _silu(x):
    return x * _sigmoid(x)


def _softplus(x):
    return jnp.maximum(x, 0.0) + jnp.log(1.0 + jnp.exp(-jnp.abs(x)))


def _gelu_tanh(x):
    return 0.5 * x * (1.0 + jnp.tanh(math.sqrt(2.0 / math.pi) * (x + 0.044715 * (x * x * x))))


def _dnconv_fn(x, taps):
    return _silu(_causal_conv(x, taps))


def _dnconv_fwd(proj, conv_w):
    def body(x_ref, w_ref, o_ref):
        o_ref[...] = _dnconv_fn(x_ref[...], _tap_rows(w_ref)).astype(BF16)

    return pl.pallas_call(
        body, name="dnconv_fwd", grid=(DN_QKV_BLKS,),
        in_specs=[pl.BlockSpec((SEQ, 128), lambda j: (0, DN_QKV_BLK0 + j)), pl.BlockSpec((4, 128), lambda j: (0, j))],
        out_specs=pl.BlockSpec((SEQ, 128), lambda j: (0, j)),
        out_shape=jax.ShapeDtypeStruct((SEQ, 1536), BF16),
        compiler_params=_cp("parallel"),
    )(proj, conv_w)


def _dnconv_bwd(proj, conv_w, dc, dproj):
    def body(x_ref, w_ref, dc_ref, _, dx_ref, dw_ref):
        _, vjp = jax.vjp(_dnconv_fn, x_ref[...], _tap_rows(w_ref))
        dx, dw = vjp(dc_ref[...])
        dx_ref[...] = dx.astype(BF16)
        for k, row in enumerate(dw):
            dw_ref[k:k + 1, :] = row

    return pl.pallas_call(
        body, name="dnconv_bwd", grid=(DN_QKV_BLKS,),
        in_specs=[pl.BlockSpec((SEQ, 128), lambda j: (0, DN_QKV_BLK0 + j)), pl.BlockSpec((4, 128), lambda j: (0, j)),
                  pl.BlockSpec((SEQ, 128), lambda j: (0, j)), pl.BlockSpec(memory_space=pl.ANY)],
        out_specs=[pl.BlockSpec((SEQ, 128), lambda j: (0, DN_QKV_BLK0 + j)), pl.BlockSpec((4, 128), lambda j: (0, j))],
        out_shape=[jax.ShapeDtypeStruct((SEQ, IN_PAD), BF16), jax.ShapeDtypeStruct((4, 1536), F32)],
        input_output_aliases={3: 0},
        compiler_params=_cp("parallel"),
    )(proj, conv_w, dc, dproj)


def _ffact_fn(pg, pu, wg, wu, bg, bu):
    return _gelu_tanh(_causal_conv(pg, wg) + bg) * (_causal_conv(pu, wu) + bu)


def _ffact_args(p_ref, w_ref, b_ref):
    g, u = slice(0, 128), slice(128, 256)
    return (p_ref[:, g].astype(F32), p_ref[:, u].astype(F32), _tap_rows(w_ref, g), _tap_rows(w_ref, u),
            b_ref[:, g], b_ref[:, u])


def _ffact_fwd(pre, conv_w, conv_b, exchanges=()):
    def body(p_ref, w_ref, b_ref, o_ref):
        o_ref[...] = _ffact_fn(*_ffact_args(p_ref, w_ref, b_ref)).astype(BF16)

    (act,), results = _hosted_call(
        body, name="ffact_fwd", steps=FF_BLKS,
        in_specs=[pl.BlockSpec((SEQ, 256), lambda j: (0, j)), pl.BlockSpec((3, 256), lambda j: (0, j)),
                  pl.BlockSpec((1, 256), lambda j: (0, j))],
        out_specs=[pl.BlockSpec((SEQ, 128), lambda j: (0, j))],
        out_shape=[jax.ShapeDtypeStruct((SEQ, D_FF), BF16)],
        scratch_shapes=[], operands=(pre, conv_w, conv_b), exchanges=exchanges)
    return act, results


def _ffact_bwd(pre, conv_w, conv_b, dact, exchanges=()):
    def body(p_ref, w_ref, b_ref, da_ref, dp_ref, dw_ref, db_ref):
        _, vjp = jax.vjp(_ffact_fn, *_ffact_args(p_ref, w_ref, b_ref))
        dpg, dpu, dwg, dwu, dbg, dbu = vjp(da_ref[...].astype(F32))
        dp_ref[:, 0:128] = dpg.astype(BF16)
        dp_ref[:, 128:256] = dpu.astype(BF16)
        for k in range(3):
            dw_ref[k:k + 1, 0:128] = dwg[k]
            dw_ref[k:k + 1, 128:256] = dwu[k]
        db_ref[:, 0:128] = dbg
        db_ref[:, 128:256] = dbu

    return _hosted_call(
        body, name="ffact_bwd", steps=FF_BLKS,
        in_specs=[pl.BlockSpec((SEQ, 256), lambda j: (0, j)), pl.BlockSpec((3, 256), lambda j: (0, j)),
                  pl.BlockSpec((1, 256), lambda j: (0, j)), pl.BlockSpec((SEQ, 128), lambda j: (0, j))],
        out_specs=[pl.BlockSpec((SEQ, 256), lambda j: (0, j)), pl.BlockSpec((3, 256), lambda j: (0, j)),
                   pl.BlockSpec((1, 256), lambda j: (0, j))],
        out_shape=[jax.ShapeDtypeStruct((SEQ, 2 * D_FF), BF16), jax.ShapeDtypeStruct((3, 2 * D_FF), F32),
                   jax.ShapeDtypeStruct((1, 2 * D_FF), F32)],
        scratch_shapes=[], operands=(pre, conv_w, conv_b, dact), exchanges=exchanges)


def _interleave_ff(t):
    lead = t.shape[:-1]
    return t.reshape(lead + (2, FF_BLKS, 128)).swapaxes(-3, -2).reshape(lead + (2 * D_FF,))


def _deinterleave_ff(t):
    lead = t.shape[:-1]
    return t.reshape(lead + (FF_BLKS, 2, 128)).swapaxes(-3, -2).reshape(lead + (2 * D_FF,))


def _rope_tables():
    inv = 1.0 / (ROPE_THETA ** (jnp.arange(0, HEAD_DIM, 2, dtype=F32) / HEAD_DIM))
    ang = jnp.arange(SEQ, dtype=F32)[:, None] * inv[None, :]
    cos = jnp.tile(jnp.cos(ang), (1, 4))
    sin = jnp.tile(jnp.sin(ang), (1, 4))
    sign = jnp.where((jnp.arange(128) % HEAD_DIM) < HEAD_DIM // 2, -1.0, 1.0).astype(F32)
    return cos, sin * sign[None, :]


def _rope(x, cos, sin_signed):
    lane = lax.broadcasted_iota(jnp.int32, x.shape, 1)
    partner = jnp.where((lane % HEAD_DIM) < HEAD_DIM // 2, pltpu.roll(x, 128 - HEAD_DIM // 2, 1),
                        pltpu.roll(x, HEAD_DIM // 2, 1))
    return x * cos + partner * sin_signed


def _head_masks():
    lane = lax.broadcasted_iota(jnp.int32, (1, 128), 1)
    return [(lane // HEAD_DIM) == h for h in range(2)]


def _both_heads(x):
    return jnp.concatenate([jnp.where(hm, x, 0.0)[None] for hm in _head_masks()], axis=0)


def _block_keys(branch, k_s, v_s, rows, prows, has_prev):
    a = lax.broadcasted_iota(jnp.int32, (ATTN_BLK, ATTN_BLK), 0)
    c = lax.broadcasted_iota(jnp.int32, (ATTN_BLK, ATTN_BLK), 1)
    keys, values, mask = k_s[rows, :], v_s[rows, :], c <= a
    if SEGMENT_BLOCKS[branch] > 1:
        keys = jnp.concatenate([k_s[prows, :], keys], axis=0)
        values = jnp.concatenate([v_s[prows, :], values], axis=0)
        mask = jnp.concatenate([(c >= a) & has_prev, mask], axis=1)
    twice = lambda t: jnp.broadcast_to(t[None], (2,) + t.shape)
    return twice(keys), twice(values), mask


def _block_rows(branch, t):
    d, per_seg = DILATIONS[branch], SEGMENT_BLOCKS[branch]
    if d == 1:
        start = pl.multiple_of(t * ATTN_BLK, ATTN_BLK)
        prev = pl.multiple_of(jnp.maximum(t - 1, 0) * ATTN_BLK, ATTN_BLK)
        return pl.ds(start, ATTN_BLK), pl.ds(prev, ATTN_BLK), t > 0
    r, n = t // per_seg, t % per_seg
    start = n * (ATTN_BLK * d) + r
    prev = jnp.maximum(n - 1, 0) * (ATTN_BLK * d) + r
    return pl.ds(start, ATTN_BLK, stride=d), pl.ds(prev, ATTN_BLK, stride=d), n > 0


def _attn_fwd(proj, cos, sin_signed, exchanges=()):
    scale = HEAD_DIM ** -0.5

    def body(qkv_ref, cos_ref, sin_ref, out_ref, lse_ref, q_s, k_s, v_s, *branch_s):
        o_s, l_s = branch_s[:3], branch_s[3:]
        q_s[...] = _rope(qkv_ref[:, 0:128], cos_ref[...], sin_ref[...])
        k_s[...] = _rope(qkv_ref[:, 128:256], cos_ref[...], sin_ref[...])
        v_s[...] = qkv_ref[:, 256:384]
        heads = _head_masks()
        for branch in range(3):
            def block(t, carry, branch=branch):
                rows, prows, has_prev = _block_rows(branch, t)
                keys, values, mask = _block_keys(branch, k_s, v_s, rows, prows, has_prev)
                s = jnp.where(mask, BMM_NT(_both_heads(q_s[rows, :]), keys) * scale, NEG)
                m = jnp.max(s, axis=2, keepdims=True)
                e = jnp.exp(s - m)
                l = jnp.sum(e, axis=2, keepdims=True)
                o = BMM(e, values) / l
                lse_b = m + jnp.log(l)
                o_s[branch][rows, :] = jnp.where(heads[0], o[0], o[1])
                l_s[branch][rows, :] = jnp.where(heads[0], lse_b[0], lse_b[1])
                return carry

            lax.fori_loop(0, N_BLK, block, 0, unroll=4)
        l0, l1, l2 = l_s[0][...], l_s[1][...], l_s[2][...]
        m = jnp.maximum(jnp.maximum(l0, l1), l2)
        w0, w1, w2 = jnp.exp(l0 - m), jnp.exp(l1 - m), jnp.exp(l2 - m)
        den = w0 + w1 + w2
        out_ref[...] = (w0 * o_s[0][...] + w1 * o_s[1][...] + w2 * o_s[2][...]) / den
        lse_ref[...] = m + jnp.log(den)

    tab = pl.BlockSpec((SEQ, 128), lambda j: (0, 0))
    col = pl.BlockSpec((SEQ, 128), lambda j: (0, j))
    return _hosted_call(
        body, name="attn_fwd", steps=N_PAIR,
        in_specs=[pl.BlockSpec((SEQ, 384), lambda j: (0, j)), tab, tab],
        out_specs=[col, col],
        out_shape=[jax.ShapeDtypeStruct((SEQ, 2 * ATTN_W), F32), jax.ShapeDtypeStruct((SEQ, ATTN_W), F32)],
        scratch_shapes=[pltpu.VMEM((SEQ, 128), F32)] * 9,
        operands=(proj, cos, sin_signed), exchanges=exchanges)


def _attn_bwd(proj, cos, sin_signed, cat, lse, dcat, dproj, exchanges=()):
    scale = HEAD_DIM ** -0.5

    def body(qkv_ref, cos_ref, sin_ref, out_ref, lse_ref, do_ref, _, dqkv_ref, q_s, k_s, v_s, dq_s, dk_s, dv_s,
             dod_s):
        q_s[...] = _rope(qkv_ref[:, 0:128], cos_ref[...], sin_ref[...])
        k_s[...] = _rope(qkv_ref[:, 128:256], cos_ref[...], sin_ref[...])
        v_s[...] = qkv_ref[:, 256:384]
        dq_s[...] = jnp.zeros_like(dq_s)
        dk_s[...] = jnp.zeros_like(dk_s)
        dv_s[...] = jnp.zeros_like(dv_s)
        dod_s[...] = do_ref[...] * out_ref[...]
        heads = _head_masks()
        for branch in range(3):
            def block(t, carry, branch=branch):
                rows, prows, has_prev = _block_rows(branch, t)
                keys, values, mask = _block_keys(branch, k_s, v_s, rows, prows, has_prev)
                q2, do2 = _both_heads(q_s[rows, :]), _both_heads(do_ref[rows, :])
                lse_b, dod = lse_ref[rows, :], dod_s[rows, :]
                lse2 = jnp.concatenate(
                    [jnp.max(jnp.where(hm, lse_b, NEG), axis=1, keepdims=True)[None] for hm in heads], axis=0)
                delta = jnp.concatenate(
                    [jnp.sum(jnp.where(hm, dod, 0.0), axis=1, keepdims=True)[None] for hm in heads], axis=0)
                p = jnp.exp(jnp.where(mask, BMM_NT(q2, keys) * scale, NEG) - lse2)
                ds = p * (BMM_NT(do2, values) - delta) * scale
                dq = BMM(ds, keys)
                dk = BMM_TN(ds, q2)
                dv = BMM_TN(p, do2)
                dk, dv = dk[0] + dk[1], dv[0] + dv[1]
                dq_s[rows, :] += jnp.where(heads[0], dq[0], dq[1])
                if SEGMENT_BLOCKS[branch] > 1:
                    dk_s[rows, :] += dk[ATTN_BLK:]
                    dv_s[rows, :] += dv[ATTN_BLK:]

                    @pl.when(has_prev)
                    def _():
                        dk_s[prows, :] += dk[:ATTN_BLK]
                        dv_s[prows, :] += dv[:ATTN_BLK]
                else:
                    dk_s[rows, :] += dk
                    dv_s[rows, :] += dv
                return carry

            lax.fori_loop(0, N_BLK, block, 0, unroll=4)
        dqkv_ref[:, 0:128] = _rope(dq_s[...], cos_ref[...], -sin_ref[...]).astype(BF16)
        dqkv_ref[:, 128:256] = _rope(dk_s[...], cos_ref[...], -sin_ref[...]).astype(BF16)
        dqkv_ref[:, 256:384] = dv_s[...].astype(BF16)

    tab = pl.BlockSpec((SEQ, 128), lambda j: (0, 0))
    col = pl.BlockSpec((SEQ, 128), lambda j: (0, j))
    qkv = pl.BlockSpec((SEQ, 384), lambda j: (0, j))
    (dproj,), results = _hosted_call(
        body, name="attn_bwd", steps=N_PAIR,
        in_specs=[qkv, tab, tab, col, col, col, pl.BlockSpec(memory_space=pl.ANY)],
        out_specs=[qkv],
        out_shape=[jax.ShapeDtypeStruct((SEQ, IN_PAD), BF16)],
        scratch_shapes=[pltpu.VMEM((SEQ, 128), F32)] * 7,
        operands=(proj, cos, sin_signed, cat, lse, dcat, dproj), exchanges=exchanges, aliases={6: 0})
    return dproj, results


def _bdot(a, b, dims, precision=None):
    if precision is None:
        a = a.astype(BF16)
        b = b.astype(BF16)
    return lax.dot_general(a, b, (dims, ((0,), (0,))), preferred_element_type=F32, precision=precision)


def _make_bmm(precision):
    @jax.custom_vjp
    def nn(a, b):
        return _bdot(a, b, ((2,), (1,)), precision)

    @jax.custom_vjp
    def nt(a, b):
        return _bdot(a, b, ((2,), (2,)), precision)

    @jax.custom_vjp
    def tn(a, b):
        return _bdot(a, b, ((1,), (1,)), precision)

    nn.defvjp(lambda a, b: (nn(a, b), (a, b)), lambda r, g: (nt(g, r[1]), tn(r[0], g)))
    nt.defvjp(lambda a, b: (nt(a, b), (a, b)), lambda r, g: (nn(g, r[1]), tn(g, r[0])))
    tn.defvjp(lambda a, b: (tn(a, b), (a, b)), lambda r, g: (nt(r[1], g), nn(r[0], g)))
    return nn, nt, tn


BMM, BMM_NT, BMM_TN = _make_bmm(None)
BMM3, BMM3_NT, BMM3_TN = _make_bmm(lax.Precision.HIGH)
MM3, _, _ = _make_mm(lax.Precision.HIGH)


def _head_lanes(t, off):
    lane = lax.broadcasted_iota(jnp.int32, (1, 128), 1)
    return jnp.concatenate(
        [jnp.sum(t * (lane == off + h).astype(F32), axis=1, keepdims=True)[None] for h in range(NDH)], axis=0)


@jax.custom_vjp
def _unit_lower_inverse(a_mat):
    c = a_mat.shape[1]
    eye = (lax.broadcasted_iota(jnp.int32, (c, c), 0) == lax.broadcasted_iota(jnp.int32, (c, c), 1)).astype(F32)
    power = -a_mat
    t_inv = eye + power
    for _ in range(5):
        power = BMM3(power, power)
        t_inv = t_inv + BMM3(t_inv, power)
    return t_inv


def _unit_lower_inverse_fwd(a_mat):
    t_inv = _unit_lower_inverse(a_mat)
    return t_inv, t_inv


def _unit_lower_inverse_bwd(t_inv, d_inv):
    return (-BMM3_NT(BMM3_TN(t_inv, d_inv), t_inv),)


_unit_lower_inverse.defvjp(_unit_lower_inverse_fwd, _unit_lower_inverse_bwd)


DN_STEP_CHUNKS = 4
DN_STEP_ROWS = DN_STEP_CHUNKS * CH
DN_STEPS = NCH // DN_STEP_CHUNKS
DN_BATCH = DN_STEP_CHUNKS * NDH


def _delta_chunks(qr, kr, vr, z, tail, alog_row, dt_row, nw, state):
    c = qr.shape[1]
    tails = [tail[CH * n:CH * (n + 1)] for n in range(DN_STEP_CHUNKS)]
    per_chunk = lambda t: jnp.concatenate([t] * DN_STEP_CHUNKS, axis=0)
    beta = _sigmoid(jnp.concatenate([_head_lanes(t, 0) for t in tails], axis=0))
    a_raw = jnp.concatenate([_head_lanes(t, NDH) for t in tails], axis=0)
    g = -jnp.exp(per_chunk(_head_lanes(alog_row, 0))) * _softplus(a_raw + per_chunk(_head_lanes(dt_row, 0)))

    q = qr * lax.rsqrt(jnp.sum(qr * qr, axis=2, keepdims=True) + EPS) * (128 ** -0.5)
    k = kr * lax.rsqrt(jnp.sum(kr * kr, axis=2, keepdims=True) + EPS)

    ri = lax.broadcasted_iota(jnp.int32, (c, c), 0)
    ci = lax.broadcasted_iota(jnp.int32, (c, c), 1)
    tril = ri >= ci
    lane = lax.broadcasted_iota(jnp.int32, (1, 128), 1)
    pick = [(lane == b).astype(F32) for b in range(DN_BATCH)]
    g_lanes = sum(g[b] * pick[b] for b in range(DN_BATCH))
    g_sums = MM3(tril.astype(F32), g_lanes)
    gc = jnp.concatenate([jnp.sum(g_sums * pick[b], axis=1, keepdims=True)[None] for b in range(DN_BATCH)],
                         axis=0)
    g_row = jnp.swapaxes(jnp.broadcast_to(gc, (DN_BATCH, c, c)), 1, 2)
    decay = jnp.where(tril, jnp.exp(jnp.where(tril, gc - g_row, 0.0)), 0.0)
    kb = k * beta
    t_inv = _unit_lower_inverse(jnp.where(ri > ci, BMM_NT(kb, k) * decay, 0.0))
    eg = jnp.exp(gc)
    u = BMM(t_inv, vr * beta)
    w = BMM(t_inv, kb * eg)
    qk = BMM_NT(q, k) * decay
    g_tot = jnp.sum(g, axis=1, keepdims=True)
    q_dec = q * eg
    k_dec = k * jnp.exp(g_tot - gc)
    outs = []
    for n in range(DN_STEP_CHUNKS):
        heads = slice(NDH * n, NDH * (n + 1))
        v_new = u[heads] - BMM(w[heads], state)
        outs.append(BMM(q_dec[heads], state) + BMM(qk[heads], v_new))
        state = state * jnp.exp(g_tot[heads]) + BMM_TN(k_dec[heads], v_new)
    o = jnp.concatenate(outs, axis=0)
    on = o * lax.rsqrt(jnp.mean(o * o, axis=2, keepdims=True) + EPS) * nw
    return on * _silu(z), state


def _heads(v, off=0):
    return jnp.concatenate([v[None, CH * n:CH * (n + 1), off + 128 * h:off + 128 * (h + 1)]
                            for n in range(DN_STEP_CHUNKS) for h in range(NDH)], axis=0)


def _unheads(t):
    return jnp.concatenate([jnp.concatenate([t[NDH * n + h] for h in range(NDH)], axis=1)
                            for n in range(DN_STEP_CHUNKS)], axis=0)


def _delta_fwd(c_qkv, proj, alog_row, dt_row, nw, cat, exchanges=()):
    def body(c_ref, z_ref, tail_ref, al_ref, dt_ref, nw_ref, _, y_ref, st_ref, state):
        @pl.when(pl.program_id(0) == 0)
        def _():
            state[...] = jnp.zeros_like(state)

        cv = c_ref[...].astype(F32)
        st_ref[0] = state[...]
        y, new_state = _delta_chunks(_heads(cv), _heads(cv, 512), _heads(cv, 1024), _heads(z_ref[...]), tail_ref[...],
                                     al_ref[...], dt_ref[...], nw_ref[...], state[...])
        y_ref[...] = _unheads(y)
        state[...] = new_state

    row = pl.BlockSpec((1, 128), lambda n: (0, 0))
    rows = DN_STEP_ROWS
    return _hosted_call(
        body, name="delta_fwd", steps=DN_STEPS,
        in_specs=[pl.BlockSpec((rows, 1536), lambda n: (n, 0)), pl.BlockSpec((rows, 512), lambda n: (n, DN_Z_COL // 512)),
                  pl.BlockSpec((rows, 128), lambda n: (n, DN_TAIL_BLK)), row, row, row, pl.BlockSpec(memory_space=pl.ANY)],
        out_specs=[pl.BlockSpec((rows, 512), lambda n: (n, 1)),
                   pl.BlockSpec((1, NDH, 128, 128), lambda n: (n, 0, 0, 0))],
        out_shape=[jax.ShapeDtypeStruct((SEQ, 2 * ATTN_W), F32), jax.ShapeDtypeStruct((DN_STEPS, NDH, 128, 128), F32)],
        scratch_shapes=[pltpu.VMEM((NDH, 128, 128), F32)],
        operands=(c_qkv, proj, proj, alog_row, dt_row, nw, cat), exchanges=exchanges, aliases={6: 0})


def _delta_bwd(c_qkv, proj, alog_row, dt_row, nw, states, dcat, exchanges=()):
    def body(c_ref, z_ref, tail_ref, al_ref, dt_ref, nw_ref, st_ref, dy_ref,
             dp_ref, dc_ref, dal_ref, ddt_ref, dnw_ref, dstate):
        @pl.when(pl.program_id(0) == 0)
        def _():
            dstate[...] = jnp.zeros_like(dstate)
            dal_ref[...] = jnp.zeros_like(dal_ref)
            ddt_ref[...] = jnp.zeros_like(ddt_ref)
            dnw_ref[...] = jnp.zeros_like(dnw_ref)

        cv = c_ref[...].astype(F32)
        _, vjp = jax.vjp(_delta_chunks, _heads(cv), _heads(cv, 512), _heads(cv, 1024), _heads(z_ref[...]),
                         tail_ref[...], al_ref[...], dt_ref[...], nw_ref[...], st_ref[0])
        dq, dk, dv, dz, dtail, dal, ddt, dnw, dst = vjp((_heads(dy_ref[...]), dstate[...]))
        dstate[...] = dst
        dc_ref[...] = jnp.concatenate([_unheads(dq), _unheads(dk), _unheads(dv)], axis=1)
        dp_ref[...] = jnp.concatenate([_unheads(dz), dtail, jnp.zeros((DN_STEP_ROWS, 128), F32)], axis=1).astype(BF16)
        dal_ref[...] += dal
        ddt_ref[...] += ddt
        dnw_ref[...] += dnw

    rev = lambda n: DN_STEPS - 1 - n
    row = pl.BlockSpec((1, 128), lambda n: (0, 0))
    rows = DN_STEP_ROWS
    return _hosted_call(
        body, name="delta_bwd", steps=DN_STEPS,
        in_specs=[pl.BlockSpec((rows, 1536), lambda n: (rev(n), 0)),
                  pl.BlockSpec((rows, 512), lambda n: (rev(n), DN_Z_COL // 512)),
                  pl.BlockSpec((rows, 128), lambda n: (rev(n), DN_TAIL_BLK)), row, row, row,
                  pl.BlockSpec((1, NDH, 128, 128), lambda n: (rev(n), 0, 0, 0)),
                  pl.BlockSpec((rows, 512), lambda n: (rev(n), 1))],
        out_specs=[pl.BlockSpec((rows, 768), lambda n: (rev(n), DN_Z_COL // 768)),
                   pl.BlockSpec((rows, 1536), lambda n: (rev(n), 0)), row, row, row],
        out_shape=[jax.ShapeDtypeStruct((SEQ, IN_PAD), BF16), jax.ShapeDtypeStruct((SEQ, 1536), F32)]
        + [jax.ShapeDtypeStruct((1, 128), F32)] * 3,
        scratch_shapes=[pltpu.VMEM((NDH, 128, 128), F32)],
        operands=(c_qkv, proj, proj, alog_row, dt_row, nw, states, dcat), exchanges=exchanges)


def _place():
    x, y, c = lax.axis_index("x"), lax.axis_index("y"), lax.axis_index("c")
    other_chips = [(1 - x, y), (x, 1 - y), (1 - x, 1 - y)]
    return x, y, c, other_chips


def _gather_exchange(shards):
    n = len(shards)

    def copies(ins, outs, sems):
        send_sems, recv_sems, local_sems = sems
        x, y, c, chips = _place()
        me, sibling = (x, y, c), (x, y, 1 - c)

        def copy(b, k, block, to, src=None):
            slot = outs[b].at[4 * block[0] + 2 * block[1] + block[2]]
            return pltpu.make_async_remote_copy(
                src_ref=slot if src is None else src, dst_ref=slot,
                send_sem=send_sems.at[b, k], recv_sem=recv_sems.at[b, k], device_id=to, device_id_type=MESH)

        mine = [pltpu.make_async_copy(ins[b], outs[b].at[4 * x + 2 * y + c], local_sems.at[b]) for b in range(n)]
        first = []
        for b in range(n):
            first.append(copy(b, 0, me, sibling, src=ins[b]))
            first += [copy(b, 1 + j, me, (*chip, c), src=ins[b]) for j, chip in enumerate(chips)]
        over_ici = [copy(b, 1 + j, (*chip, c), me) for b in range(n) for j, chip in enumerate(chips)]
        passed = [copy(b, 4 + j, (*chip, c), sibling) for b in range(n) for j, chip in enumerate(chips)]
        from_sibling = []
        for b in range(n):
            from_sibling.append(copy(b, 0, sibling, me))
            from_sibling += [copy(b, 4 + j, (*chip, 1 - c), me) for j, chip in enumerate(chips)]
        return mine, first, over_ici, passed, from_sibling

    def start(ins, outs, sems):
        mine, first, _, _, _ = copies(ins, outs, sems)
        for cp in mine + first:
            cp.start()

    def middle(ins, outs, sems):
        _, _, over_ici, passed, _ = copies(ins, outs, sems)
        for arrived, onward in zip(over_ici, passed):
            arrived.wait_recv()
            onward.start()

    def finish(ins, outs, sems):
        mine, first, _, passed, from_sibling = copies(ins, outs, sems)
        for cp in from_sibling:
            cp.wait_recv()
        for cp in first + passed:
            cp.wait_send()
        for cp in mine:
            cp.wait()

    return Exchange(shards, [jax.ShapeDtypeStruct((N_DEV,) + s.shape, s.dtype) for s in shards],
                    [pltpu.SemaphoreType.DMA((n, 7)), pltpu.SemaphoreType.DMA((n, 7)), pltpu.SemaphoreType.DMA((n,))],
                    start, middle, finish)


def _sibling_exchange(gs):
    n = len(gs)

    def copies(ins, outs, sems):
        send_sems, recv_sems = sems
        x, y, c, _ = _place()
        return [pltpu.make_async_remote_copy(
            src_ref=ins[b].at[2 * p + (1 - c)], dst_ref=outs[b].at[p],
            send_sem=send_sems.at[b, p], recv_sem=recv_sems.at[b, p],
            device_id=(x, y, 1 - c), device_id_type=MESH) for b in range(n) for p in range(4)]

    def start(ins, outs, sems):
        for cp in copies(ins, outs, sems):
            cp.start()

    def finish(ins, outs, sems):
        for cp in copies(ins, outs, sems):
            cp.wait()

    return Exchange(gs, [jax.ShapeDtypeStruct((4,) + g.shape[1:], g.dtype) for g in gs],
                    [pltpu.SemaphoreType.DMA((n, 4)), pltpu.SemaphoreType.DMA((n, 4))], start, None, finish)


def _chips_exchange(hs):
    n = len(hs)

    def copies(ins, outs, sems):
        send_sems, recv_sems, local_sems = sems
        x, y, c, chips = _place()
        my_chip = 2 * x + y
        local = [pltpu.make_async_copy(ins[b].at[my_chip], outs[b].at[my_chip], local_sems.at[b]) for b in range(n)]
        sends, arrivals = [], []
        for b in range(n):
            for k, (px, py) in enumerate(chips):
                peer = 2 * px + py
                sends.append(pltpu.make_async_remote_copy(
                    src_ref=ins[b].at[peer], dst_ref=outs[b].at[my_chip],
                    send_sem=send_sems.at[b, k], recv_sem=recv_sems.at[b, k],
                    device_id=(px, py, c), device_id_type=MESH))
                arrivals.append(pltpu.make_async_remote_copy(
                    src_ref=ins[b].at[peer], dst_ref=outs[b].at[peer],
                    send_sem=send_sems.at[b, k], recv_sem=recv_sems.at[b, k],
                    device_id=(px, py, c), device_id_type=MESH))
        return local, sends, arrivals

    def start(ins, outs, sems):
        local, sends, _ = copies(ins, outs, sems)
        for cp in local + sends:
            cp.start()

    def finish(ins, outs, sems):
        local, sends, arrivals = copies(ins, outs, sems)
        for cp in arrivals:
            cp.wait_recv()
        for cp in sends:
            cp.wait_send()
        for cp in local:
            cp.wait()

    return Exchange(hs, [jax.ShapeDtypeStruct(h.shape, h.dtype) for h in hs],
                    [pltpu.SemaphoreType.DMA((n, 3)), pltpu.SemaphoreType.DMA((n, 3)), pltpu.SemaphoreType.DMA((n,))],
                    start, None, finish)


def _run_exchange(exchange, name):
    n_in, n_out = len(exchange.operands), len(exchange.out_shapes)

    def body(*refs):
        ins, outs, sems = refs[:n_in], refs[n_in:n_in + n_out], refs[n_in + n_out:]
        exchange.start(ins, outs, sems)
        if exchange.middle is not None:
            exchange.middle(ins, outs, sems)
        exchange.finish(ins, outs, sems)

    return pl.pallas_call(
        body, name=name,
        in_specs=[HBM_SPEC] * n_in, out_specs=[HBM_SPEC] * n_out,
        out_shape=exchange.out_shapes, scratch_shapes=exchange.sems,
    )(*exchange.operands)


def _pair_add(g, r, core, name):
    _, nr, nc = g.shape
    tr = nr // 2 if nr % 32 == 0 else nr

    def body(core_ref, g_ref, r_ref, o_ref):
        o_ref[...] = (g_ref[...].astype(F32) + r_ref[...].astype(F32)).astype(BF16)

    return pl.pallas_call(
        body, name=name,
        grid_spec=pltpu.PrefetchScalarGridSpec(
            num_scalar_prefetch=1, grid=(4, nr // tr),
            in_specs=[pl.BlockSpec((1, tr, nc), lambda p, i, core: (2 * p + core[0], i, 0)),
                      pl.BlockSpec((1, tr, nc), lambda p, i, core: (p, i, 0))],
            out_specs=pl.BlockSpec((1, tr, nc), lambda p, i, core: (p, i, 0))),
        out_shape=jax.ShapeDtypeStruct(r.shape, BF16),
        compiler_params=_cp("parallel", "parallel"),
    )(core, g, r)


def _all_gather_sum_small(v):
    rows = v.shape[0]

    def body(x_ref, sum_ref, out_ref, send_sems, recv_sems, local_sem):
        x, y, c, chips = _place()
        me, sibling = (x, y, c), (x, y, 1 - c)

        def block(px, py, pc):
            return out_ref.at[pl.ds((4 * px + 2 * py + pc) * rows, rows), :]

        def copy(k, blk, to, src=None):
            return pltpu.make_async_remote_copy(
                src_ref=block(*blk) if src is None else src, dst_ref=block(*blk),
                send_sem=send_sems.at[k], recv_sem=recv_sems.at[k], device_id=to, device_id_type=MESH)

        mine = pltpu.make_async_copy(x_ref, block(*me), local_sem)
        mine.start()
        first = [copy(0, me, sibling, src=x_ref)]
        first += [copy(1 + j, me, (*chip, c), src=x_ref) for j, chip in enumerate(chips)]
        for cp in first:
            cp.start()
        passed = [copy(4 + j, (*chip, c), sibling) for j, chip in enumerate(chips)]
        for j, chip in enumerate(chips):
            copy(1 + j, (*chip, c), me).wait_recv()
            passed[j].start()
        copy(0, sibling, me).wait_recv()
        for j, chip in enumerate(chips):
            copy(4 + j, (*chip, 1 - c), me).wait_recv()
        for cp in first + passed:
            cp.wait_send()
        mine.wait()
        total = out_ref[pl.ds(0, rows), :]
        for d in range(1, N_DEV):
            total = total + out_ref[pl.ds(d * rows, rows), :]
        sum_ref[...] = total

    vm = pl.BlockSpec(memory_space=pltpu.VMEM)
    return pl.pallas_call(
        body, name="small_all_reduce",
        in_specs=[vm], out_specs=[vm],
        out_shape=[jax.ShapeDtypeStruct((rows, 128), F32)],
        scratch_shapes=[pltpu.VMEM((N_DEV * rows, 128), F32), pltpu.SemaphoreType.DMA((7,)),
                        pltpu.SemaphoreType.DMA((7,)), pltpu.SemaphoreType.DMA],
    )(v)[0]


def _adamw(w, g, m, v):
    m = ADAM_B1 * m + (1.0 - ADAM_B1) * g
    v = ADAM_B2 * v + (1.0 - ADAM_B2) * (g * g)
    m_hat = m / (1.0 - ADAM_B1 ** ADAM_STEP)
    v_hat = v / (1.0 - ADAM_B2 ** ADAM_STEP)
    delta = -ADAM_LR * (m_hat / (jnp.sqrt(v_hat) + ADAM_EPS) + ADAM_WD * w)
    return delta, m, v


ADAM_TILE = dict(w_in=(IN_COLS // N_DEV, 256), w_out=(128, D_MODEL), ffn_w_in=(176, D_MODEL), ffn_w_out=(176, D_MODEL))


def _sum_chips(p):
    p = p.astype(F32)
    return (p[0] + p[1]) + (p[2] + p[3])


def _adamw_sharded(parts, w, m, v, tile, name):
    nl, nr, nc = w.shape
    tr, tc = tile

    def body(*refs):
        p_refs, (w_ref, m_ref, v_ref, g_ref, d_ref, nm_ref, nv_ref) = refs[:nl], refs[nl:]
        layer = pl.program_id(0)
        p = p_refs[0][...]
        for l in range(1, nl):
            p = jnp.where(layer == l, p_refs[l][...], p)
        g = _sum_chips(p)
        delta, nm, nv = _adamw(w_ref[0], g, m_ref[0], v_ref[0])
        g_ref[0] = g
        d_ref[0] = delta
        nm_ref[0] = nm
        nv_ref[0] = nv

    blk = pl.BlockSpec((1, tr, tc), lambda l, i, j: (l, i, j))
    return pl.pallas_call(
        body, name=name, grid=(nl, nr // tr, nc // tc),
        in_specs=[pl.BlockSpec((4, tr, tc), lambda l, i, j, own=own: (0, jnp.where(l == own, i, 0), j))
                  for own in range(nl)] + [blk, blk, blk],
        out_specs=[blk] * 4,
        out_shape=[jax.ShapeDtypeStruct(w.shape, F32)] * 4,
        compiler_params=_cp("parallel", "parallel", "parallel"),
    )(*parts, w, m, v)


def _adamw_small(g, w, m, v):
    def body(g_ref, w_ref, m_ref, v_ref, d_ref, nm_ref, nv_ref):
        delta, nm, nv = _adamw(w_ref[...], g_ref[...], m_ref[...], v_ref[...])
        d_ref[...] = delta
        nm_ref[...] = nm
        nv_ref[...] = nv

    return pl.pallas_call(
        body, name="adamw_small",
        out_shape=[jax.ShapeDtypeStruct(g.shape, F32)] * 3,
    )(g, w, m, v)


def _packed_rows(n):
    return -(-n // 1024) * 8


def _pack(arrays, rows):
    pieces = []
    for a in arrays:
        flat = a.reshape(-1).astype(F32)
        nr = _packed_rows(flat.shape[0])
        pieces.append(jnp.pad(flat, (0, nr * 128 - flat.shape[0])).reshape(nr, 128))
    used = sum(p.shape[0] for p in pieces)
    return jnp.concatenate(pieces + [jnp.zeros((rows - used, 128), F32)] * (rows > used), axis=0)


def _unpack(packed, shapes):
    out, row = [], 0
    for s in shapes:
        n = math.prod(s)
        out.append(packed[row:row + _packed_rows(n)].reshape(-1)[:n].reshape(s))
        row += _packed_rows(n)
    return out


def _row(v, width=None):
    v = v.reshape(1, -1)
    return v if width is None else jnp.pad(v, ((0, 0), (0, width - v.shape[1])))


def _layer_fwd(x, h, wts, tables, hosted, last_step):
    proj = _matmul(h, wts["w_in"], tb=True, tm=SEQ, tn=768, tk=1024, name="mm_proj")
    (cat, lse), got = _attn_fwd(proj, *tables, exchanges=hosted["attn"][0])
    hosted["attn"][1](got)
    c_qkv = _dnconv_fwd(proj, wts["dn_conv_w"])
    (cat, states), got = _delta_fwd(c_qkv, proj, wts["dn_a_log"], wts["dn_dt_bias"], wts["dn_norm_w"], cat,
                                    exchanges=hosted["delta"][0])
    hosted["delta"][1](got)
    mix = _matmul(cat, wts["w_out"], tm=512, tn=1024, tk=1024, name="mm_mix")
    x1, h2 = _resnorm_norm_fwd(x, mix, wts["norm_post_mix"], wts["norm_pre_ffn"], "norm_post_mix")
    pre = _matmul(h2, wts["ffn_w_in"], tb=True, tm=SEQ, tn=512, tk=1024, name="mm_ffn_in", out_dtype=BF16)
    act, got = _ffact_fwd(pre, wts["ffn_conv_w"], wts["ffn_conv_b"], exchanges=hosted["ffact"][0])
    hosted["ffact"][1](got)
    f = _matmul(act, wts["ffn_w_out"], tm=512, tn=1024, tk=D_FF, name="mm_ffn_out")
    saved = dict(x=x, h=h, proj=proj, lse=lse, c_qkv=c_qkv, states=states, cat=cat, mix=mix, x1=x1, h2=h2, pre=pre,
                 act=act, f=f)
    return last_step(x1, f), saved


def _layer_bwd(dx2, wts, s, tables, ffact_exchanges=(), delta_exchanges=None, attn_exchanges=None, head=None,
               below=None):
    g = {}
    if head is None:
        head = _norm_bwd(s["f"], wts["norm_post_ffn"], dx2, None, "norm_post_ffn_bwd", BF16)
    df, g["norm_post_ffn"] = head
    dact = _matmul(df, wts["ffn_w_out"], tb=True, tm=SEQ, tn=1408, tk=1024, name="mm_dact", out_dtype=BF16)
    g["ffn_w_out"] = _matmul(s["act"], df, ta=True, tm=1408, tn=512, tk=SEQ, name="mm_dw_ffn_out", out_dtype=BF16)
    (dpre, g["ffn_conv_w"], g["ffn_conv_b"]), got = _ffact_bwd(s["pre"], wts["ffn_conv_w"], wts["ffn_conv_b"], dact,
                                                               exchanges=ffact_exchanges)
    dh2 = _matmul(dpre, wts["ffn_w_in"], tm=512, tn=1024, tk=2 * D_FF, name="mm_dh2")
    g["ffn_w_in"] = _matmul(dpre, s["h2"], ta=True, tm=512, tn=1024, tk=SEQ, name="mm_dw_ffn_in", out_dtype=BF16)
    dx1, dmix, g["norm_pre_ffn"], g["norm_post_mix"] = _norm_bwd_pair(
        s["x1"], wts["norm_pre_ffn"], dh2, dx2, s["mix"], wts["norm_post_mix"], "norm_pre_ffn_bwd")
    dcat = _matmul(dmix, wts["w_out"], tb=True, tm=SEQ, tn=512, tk=1024, name="mm_dcat")
    g["w_out"] = _matmul(s["cat"], dmix, ta=True, tm=1024, tn=512, tk=SEQ, name="mm_dw_out", out_dtype=BF16)
    (dproj, dc, g["dn_a_log"], g["dn_dt_bias"], g["dn_norm_w"]), got = _delta_bwd(
        s["c_qkv"], s["proj"], wts["dn_a_log"], wts["dn_dt_bias"], wts["dn_norm_w"], s["states"], dcat,
        exchanges=delta_exchanges(g, got) if delta_exchanges is not None else ())
    dproj, got = _attn_bwd(s["proj"], *tables, s["cat"], s["lse"], dcat, dproj,
                           exchanges=attn_exchanges(got) if attn_exchanges is not None else ())
    dproj, g["dn_conv_w"] = _dnconv_bwd(s["proj"], wts["dn_conv_w"], dc, dproj)
    dh = _matmul(dproj, wts["w_in"], tm=512, tn=1024, tk=IN_PAD, name="mm_dh")
    g["w_in"] = _matmul(dproj, s["h"], ta=True, tm=768, tn=1024, tk=SEQ, name="mm_dw_in", out_dtype=BF16)
    if below is None:
        dx, g["norm_pre_mix"] = _norm_bwd(s["x"], wts["norm_pre_mix"], dh, dx1, "norm_pre_mix_bwd")
        return dx, g, got, None
    dx, df_below, g["norm_pre_mix"], dw_below = _norm_bwd_pair(s["x"], wts["norm_pre_mix"], dh, dx1, *below,
                                                               "norm_pre_mix_bwd")
    return dx, g, got, (df_below, dw_below)


BIG = ("w_in", "w_out", "ffn_w_in", "ffn_w_out")
COLUMN_SHARDED = ("w_in", "ffn_w_in")
SMALL_SHARDED = ("dn_conv_w", "ffn_conv_w")
REPLICATED = ("dn_a_log", "dn_dt_bias", "dn_norm_w", "ffn_conv_b", "norm_pre_mix", "norm_post_mix", "norm_pre_ffn",
              "norm_post_ffn")
WEIGHTS = ("w_in", "dn_conv_w", "dn_a_log", "dn_dt_bias", "dn_norm_w", "w_out", "ffn_w_in", "ffn_conv_w", "ffn_conv_b",
           "ffn_w_out", "norm_pre_mix", "norm_post_mix", "norm_pre_ffn", "norm_post_ffn")
FULL_SHAPE = dict(dn_conv_w=(DEPTH, 4, 1536), ffn_conv_w=(DEPTH, 3, 2 * D_FF), dn_a_log=(DEPTH, NDH),
                  dn_dt_bias=(DEPTH, NDH), dn_norm_w=(DEPTH, 128), ffn_conv_b=(DEPTH, 2 * D_FF),
                  norm_pre_mix=(DEPTH, D_MODEL), norm_post_mix=(DEPTH, D_MODEL), norm_pre_ffn=(DEPTH, D_MODEL),
                  norm_post_ffn=(DEPTH, D_MODEL))
SMALL_GRAD_ORDER = REPLICATED + SMALL_SHARDED
SMALL_GRAD_ROWS = 544
SMALL_W_ROWS = 56
SMALL_ADAM_ROWS = 232


def _w_in_rows_to_kernel_order(t):
    qkv = t[:QKV_W].reshape(3, N_PAIR, 128, -1).swapaxes(0, 1).reshape(QKV_W, -1)
    return jnp.pad(jnp.concatenate([qkv, t[QKV_W:]], axis=0), ((0, IN_PAD - IN_COLS), (0, 0)))


def _w_in_rows_from_kernel_order(t):
    qkv = t[:QKV_W].reshape(N_PAIR, 3, 128, -1).swapaxes(0, 1).reshape(QKV_W, -1)
    return jnp.concatenate([qkv, t[QKV_W:IN_COLS]], axis=0)


def _interleave_ff_rows(t):
    return t.reshape(2, FF_BLKS, 128, -1).swapaxes(0, 1).reshape(2 * D_FF, -1)


def _deinterleave_ff_rows(t):
    return t.reshape(FF_BLKS, 2, 128, -1).swapaxes(0, 1).reshape(2 * D_FF, -1)


def kernel(x, w_in, dn_conv_w, dn_a_log, dn_dt_bias, dn_norm_w, w_out, ffn_w_in, ffn_conv_w, ffn_conv_b, ffn_w_out, norm_pre_mix, norm_post_mix, norm_pre_ffn, norm_post_ffn, loss_target, m_w_in, m_dn_conv_w, m_dn_a_log, m_dn_dt_bias, m_dn_norm_w, m_w_out, m_ffn_w_in, m_ffn_conv_w, m_ffn_conv_b, m_ffn_w_out, m_norm_pre_mix, m_norm_post_mix, m_norm_pre_ffn, m_norm_post_ffn, v_w_in, v_dn_conv_w, v_dn_a_log, v_dn_dt_bias, v_dn_norm_w, v_w_out, v_ffn_w_in, v_ffn_conv_w, v_ffn_conv_b, v_ffn_w_out, v_norm_pre_mix, v_norm_post_mix, v_norm_pre_ffn, v_norm_post_ffn):
    local = dict(w_in=w_in, dn_conv_w=dn_conv_w, dn_a_log=dn_a_log, dn_dt_bias=dn_dt_bias, dn_norm_w=dn_norm_w,
                 w_out=w_out, ffn_w_in=ffn_w_in, ffn_conv_w=ffn_conv_w, ffn_conv_b=ffn_conv_b, ffn_w_out=ffn_w_out,
                 norm_pre_mix=norm_pre_mix, norm_post_mix=norm_post_mix, norm_pre_ffn=norm_pre_ffn,
                 norm_post_ffn=norm_post_ffn)
    mom_m = dict(w_in=m_w_in, dn_conv_w=m_dn_conv_w, dn_a_log=m_dn_a_log, dn_dt_bias=m_dn_dt_bias,
                 dn_norm_w=m_dn_norm_w, w_out=m_w_out, ffn_w_in=m_ffn_w_in, ffn_conv_w=m_ffn_conv_w,
                 ffn_conv_b=m_ffn_conv_b, ffn_w_out=m_ffn_w_out, norm_pre_mix=m_norm_pre_mix,
                 norm_post_mix=m_norm_post_mix, norm_pre_ffn=m_norm_pre_ffn, norm_post_ffn=m_norm_post_ffn)
    mom_v = dict(w_in=v_w_in, dn_conv_w=v_dn_conv_w, dn_a_log=v_dn_a_log, dn_dt_bias=v_dn_dt_bias,
                 dn_norm_w=v_dn_norm_w, w_out=v_w_out, ffn_w_in=v_ffn_w_in, ffn_conv_w=v_ffn_conv_w,
                 ffn_conv_b=v_ffn_conv_b, ffn_w_out=v_ffn_w_out, norm_pre_mix=v_norm_pre_mix,
                 norm_post_mix=v_norm_post_mix, norm_pre_ffn=v_norm_pre_ffn, norm_post_ffn=v_norm_post_ffn)
    dev = 4 * lax.axis_index("x") + 2 * lax.axis_index("y") + lax.axis_index("c")
    core = lax.axis_index("c").astype(jnp.int32).reshape(1)

    def shard(n, l):
        s = local[n].transpose(0, 2, 1) if n in COLUMN_SHARDED else local[n]
        return s[l].astype(BF16)

    def matrix(n, gathered):
        if n == "w_in":
            return _w_in_rows_to_kernel_order(gathered.reshape(IN_COLS, D_MODEL))
        if n == "ffn_w_in":
            return _interleave_ff_rows(gathered.reshape(2 * D_FF, D_MODEL))
        return gathered.reshape(-1, D_MODEL)

    small_w = _pack([dn_conv_w, ffn_conv_w], SMALL_W_ROWS)
    g_w_in0, g_small = _run_exchange(_gather_exchange([shard("w_in", 0), small_w]), "weights_all_gather")
    n_dn, n_ff = DEPTH * 4 * 192, DEPTH * 3 * 704
    dn_rows = _packed_rows(n_dn)
    sm_dn = g_small[:, :dn_rows].reshape(N_DEV, -1)[:, :n_dn]
    sm_ff = g_small[:, dn_rows:].reshape(N_DEV, -1)[:, :n_ff]
    full_dn_conv = sm_dn.reshape(N_DEV, DEPTH, 4, 192).transpose(1, 2, 0, 3).reshape(DEPTH, 4, 1536)
    full_ff_conv = _interleave_ff(sm_ff.reshape(N_DEV, DEPTH, 3, 704).transpose(1, 2, 0, 3).reshape(DEPTH, 3, 2 * D_FF))

    def small_weights(l):
        wts = dict(dn_conv_w=full_dn_conv[l], ffn_conv_w=full_ff_conv[l], ffn_conv_b=_interleave_ff(_row(ffn_conv_b[l])),
                   dn_a_log=_row(dn_a_log[l], 128), dn_dt_bias=_row(dn_dt_bias[l], 128))
        for n in ("dn_norm_w", "norm_pre_mix", "norm_post_mix", "norm_pre_ffn", "norm_post_ffn"):
            wts[n] = _row(local[n][l])
        return wts

    weights = [small_weights(l) for l in range(DEPTH)]
    weights[0]["w_in"] = matrix("w_in", g_w_in0)

    def gather_behind(wanted):
        def deliver(got):
            for (n, l), g in zip(wanted, got[0]):
                weights[l][n] = matrix(n, g)

        return [_gather_exchange([shard(n, l) for n, l in wanted])], deliver

    nothing = ((), lambda got: None)

    tables = _rope_tables()
    h0 = _norm_fwd(x[0], weights[0]["norm_pre_mix"], "norm_pre_mix")
    (act, h1), saved0 = _layer_fwd(
        x[0], h0, weights[0], tables,
        dict(attn=gather_behind([("ffn_w_in", 0)]), delta=gather_behind([("w_out", 0), ("w_in", 1)]),
             ffact=gather_behind([("ffn_w_out", 0)])),
        lambda x1, f: _resnorm_norm_fwd(x1, f, weights[0]["norm_post_ffn"], weights[1]["norm_pre_mix"], "norm_post_ffn"))
    (loss_part, dact), saved1 = _layer_fwd(
        act, h1, weights[1], tables,
        dict(attn=gather_behind([("ffn_w_in", 1)]), delta=gather_behind([("w_out", 1), ("ffn_w_out", 1)]), ffact=nothing),
        lambda x1, f: _resnorm_loss(x1, f, weights[1]["norm_post_ffn"], loss_target[0]))

    def to_devices(name, t):
        if name == "w_in":
            t = _w_in_rows_from_kernel_order(t)
        if name == "ffn_w_in":
            t = _deinterleave_ff_rows(t)
        return t.reshape(N_DEV, t.shape[0] // N_DEV, t.shape[1])

    def pair_sums(names, layer, to_dev, from_sibling):
        return [_pair_add(gd, r, core, "grads_pair_add_%s_%d" % (n, layer))
                for n, gd, r in zip(names, to_dev, from_sibling)]

    early = ("w_out", "ffn_w_in", "ffn_w_out")
    grads, parts, stash = [None] * DEPTH, {}, {}

    def delta_exchanges1(g, got_ffact):
        stash["early1"] = [to_devices(n, g[n]) for n in early]
        return [_sibling_exchange(stash["early1"])]

    def attn_exchanges1(got_delta):
        return [_chips_exchange(pair_sums(early, 1, stash["early1"], got_delta[0]))]

    dact, grads[1], got_attn, head0 = _layer_bwd(dact, weights[1], saved1, tables, (), delta_exchanges1, attn_exchanges1,
                                                 below=(saved0["f"], weights[0]["norm_post_ffn"]))
    for n, p in zip(early, got_attn[0]):
        parts[n, 1] = p
    w_in1 = [to_devices("w_in", grads[1]["w_in"])]

    def delta_exchanges0(g, got_ffact):
        stash["early0"] = [to_devices(n, g[n]) for n in early]
        return [_chips_exchange(pair_sums(("w_in",), 1, w_in1, got_ffact[0])), _sibling_exchange(stash["early0"])]

    def attn_exchanges0(got_delta):
        parts["w_in", 1], = got_delta[0]
        return [_chips_exchange(pair_sums(early, 0, stash["early0"], got_delta[1]))]

    dact, grads[0], got_attn, _ = _layer_bwd(dact, weights[0], saved0, tables, [_sibling_exchange(w_in1)],
                                             delta_exchanges0, attn_exchanges0, head=head0)
    for n, p in zip(early, got_attn[0]):
        parts[n, 0] = p
    grad_x = dact[None]
    last = [to_devices("w_in", grads[0]["w_in"])]
    from_sibling = _run_exchange(_sibling_exchange(last), "grads_to_sibling")
    parts["w_in", 0], = _run_exchange(_chips_exchange(pair_sums(("w_in",), 0, last, from_sibling)), "grads_to_chips")

    def small_grad(name):
        t = jnp.stack([grads[l][name] for l in range(DEPTH)])
        if name in ("dn_a_log", "dn_dt_bias"):
            t = t[:, 0, :NDH]
        if name in ("ffn_conv_w", "ffn_conv_b"):
            t = _deinterleave_ff(t)
        return t.reshape(FULL_SHAPE[name])

    small_part = _pack([small_grad(n) for n in SMALL_GRAD_ORDER] + [loss_part[0, :1]], SMALL_GRAD_ROWS)
    small_sum = _all_gather_sum_small(small_part)
    small_g = dict(zip(SMALL_GRAD_ORDER + ("loss",), _unpack(small_sum, [FULL_SHAPE[n] for n in SMALL_GRAD_ORDER] + [(1,)])))
    loss = small_g["loss"][0]
    small_g["dn_conv_w"] = lax.dynamic_slice_in_dim(small_g["dn_conv_w"], dev * 192, 192, axis=2)
    small_g["ffn_conv_w"] = lax.dynamic_slice_in_dim(small_g["ffn_conv_w"], dev * 704, 704, axis=2)

    out_g, out_d, out_m, out_v = {}, {}, {}, {}
    for n in BIG:
        turn = (lambda t: t.transpose(0, 2, 1)) if n in COLUMN_SHARDED else (lambda t: t)
        outs = _adamw_sharded([parts[n, l] for l in range(DEPTH)], turn(local[n]), turn(mom_m[n]), turn(mom_v[n]),
                              ADAM_TILE[n], "adamw_" + n)
        out_g[n], out_d[n], out_m[n], out_v[n] = [turn(t) for t in outs]
    shapes = [small_g[n].shape for n in SMALL_GRAD_ORDER]
    d_s, m_s, v_s = _adamw_small(_pack([small_g[n] for n in SMALL_GRAD_ORDER], SMALL_ADAM_ROWS),
                                 _pack([local[n] for n in SMALL_GRAD_ORDER], SMALL_ADAM_ROWS),
                                 _pack([mom_m[n] for n in SMALL_GRAD_ORDER], SMALL_ADAM_ROWS),
                                 _pack([mom_v[n] for n in SMALL_GRAD_ORDER], SMALL_ADAM_ROWS))
    for n, d, m, v in zip(SMALL_GRAD_ORDER, _unpack(d_s, shapes), _unpack(m_s, shapes), _unpack(v_s, shapes)):
        out_g[n], out_d[n], out_m[n], out_v[n] = small_g[n], d, m, v
    return (loss, grad_x, *[out_g[n] for n in WEIGHTS], *[out_d[n] for n in WEIGHTS],
            *[out_m[n] for n in WEIGHTS], *[out_v[n] for n in WEIGHTS])
```

```python
import functools
import math

import jax
import jax.numpy as jnp
from jax import lax
from jax.experimental import pallas as pl
from jax.experimental.pallas import tpu as pltpu

F32 = jnp.float32
BF16 = jnp.bfloat16
MESH = pl.DeviceIdType.MESH

N_DEV = 8
SEQ = 2048
D_MODEL = 1024
DEPTH = 2
N_PAIR = 4
HEAD_DIM = 64
ATTN_W = 512
ATTN_BLK = 128
DILATIONS = (1, 4, 16)
SEGMENT_BLOCKS = (16, 4, 1)
N_BLK = SEQ // ATTN_BLK
NDH = 4
CH = 64
NCH = SEQ // CH
IN_COLS = 3592
IN_PAD = 3840
QKV_W = 3 * ATTN_W
DN_QKV_BLK0 = QKV_W // 128
DN_QKV_BLKS = 1536 // 128
DN_Z_COL = 3072
DN_TAIL_BLK = 3584 // 128
D_FF = 2816
FF_BLKS = D_FF // 128
EPS = 1e-6
NEG = -1e30
ROPE_THETA = 10000.0

ADAM_LR, ADAM_B1, ADAM_B2, ADAM_EPS, ADAM_WD, ADAM_STEP = 0.001, 0.9, 0.999, 1e-08, 0.01, 10

VMEM_LIMIT = 56 * 1024 * 1024


def _cp(*sem):
    return pltpu.CompilerParams(dimension_semantics=sem, vmem_limit_bytes=VMEM_LIMIT)


class Exchange:
    def __init__(self, operands, out_shapes, sems, start, middle, finish):
        self.operands, self.out_shapes, self.sems = list(operands), list(out_shapes), list(sems)
        self.start, self.middle, self.finish = start, middle, finish


HBM_SPEC = pl.BlockSpec(memory_space=pltpu.HBM)


def _hosted_call(body, *, name, steps, in_specs, out_specs, out_shape, scratch_shapes, operands, exchanges=(),
                 aliases=None):
    n_in, n_out, n_scr = len(in_specs), len(out_specs), len(scratch_shapes)

    def take(refs, pos, counts):
        groups = []
        for c in counts:
            groups.append(refs[pos:pos + c])
            pos += c
        return groups, pos

    def full_body(*refs):
        ins, pos = refs[:n_in], n_in
        ex_ins, pos = take(refs, pos, [len(e.operands) for e in exchanges])
        outs, pos = refs[pos:pos + n_out], pos + n_out
        ex_outs, pos = take(refs, pos, [len(e.out_shapes) for e in exchanges])
        scr, pos = refs[pos:pos + n_scr], pos + n_scr
        ex_sems, pos = take(refs, pos, [len(e.sems) for e in exchanges])
        step = pl.program_id(0)
        for e, a, b, s in zip(exchanges, ex_ins, ex_outs, ex_sems):
            pl.when(step == 0)(functools.partial(e.start, a, b, s))
            if e.middle is not None:
                pl.when(step == (3 * steps) // 4)(functools.partial(e.middle, a, b, s))
        body(*ins, *outs, *scr)
        for e, a, b, s in zip(exchanges, ex_ins, ex_outs, ex_sems):
            pl.when(step == steps - 1)(functools.partial(e.finish, a, b, s))

    n_ex_in = sum(len(e.operands) for e in exchanges)
    n_ex_out = sum(len(e.out_shapes) for e in exchanges)
    results = pl.pallas_call(
        full_body, name=name, grid=(steps,),
        in_specs=list(in_specs) + [HBM_SPEC] * n_ex_in,
        out_specs=list(out_specs) + [HBM_SPEC] * n_ex_out,
        out_shape=list(out_shape) + [s for e in exchanges for s in e.out_shapes],
        scratch_shapes=list(scratch_shapes) + [s for e in exchanges for s in e.sems],
        input_output_aliases=aliases or {},
        compiler_params=_cp("arbitrary"),
    )(*operands, *[a for e in exchanges for a in e.operands])
    ex_results, _ = take(results, n_out, [len(e.out_shapes) for e in exchanges])
    return results[:n_out], ex_results


def _dot(a, b, dims, precision=None):
    if precision is None:
        a = a.astype(BF16)
        b = b.astype(BF16)
    return lax.dot_general(a, b, (dims, ((), ())), preferred_element_type=F32, precision=precision)


def _make_mm(precision):
    @jax.custom_vjp
    def nn(a, b):
        return _dot(a, b, ((1,), (0,)), precision)

    @jax.custom_vjp
    def nt(a, b):
        return _dot(a, b, ((1,), (1,)), precision)

    @jax.custom_vjp
    def tn(a, b):
        return _dot(a, b, ((0,), (0,)), precision)

    nn.defvjp(lambda a, b: (nn(a, b), (a, b)), lambda r, g: (nt(g, r[1]), tn(r[0], g)))
    nt.defvjp(lambda a, b: (nt(a, b), (a, b)), lambda r, g: (nn(g, r[1]), tn(g, r[0])))
    tn.defvjp(lambda a, b: (tn(a, b), (a, b)), lambda r, g: (nt(r[1], g), nn(r[0], g)))
    return nn, nt, tn


def _matmul(a, b, *, ta=False, tb=False, tm, tn, tk, name, out_dtype=F32):
    (k_dim, m_dim) = a.shape if ta else a.shape[::-1]
    (n_dim, k2) = b.shape if tb else b.shape[::-1]
    assert k_dim == k2 and m_dim % tm == 0 and n_dim % tn == 0 and k_dim % tk == 0, (a.shape, b.shape, tm, tn, tk)
    nk = k_dim // tk
    dims = ((0 if ta else 1,), (1 if tb else 0,))

    def body(a_ref, b_ref, o_ref, *acc):
        p = _dot(a_ref[...], b_ref[...], dims)
        if nk == 1:
            o_ref[...] = p.astype(out_dtype)
            return
        acc_ref, k = acc[0], pl.program_id(2)

        @pl.when(k == 0)
        def _():
            acc_ref[...] = p

        @pl.when(k > 0)
        def _():
            acc_ref[...] += p

        @pl.when(k == nk - 1)
        def _():
            o_ref[...] = acc_ref[...].astype(out_dtype)

    a_spec = pl.BlockSpec((tk, tm), lambda i, j, k: (k, i)) if ta else pl.BlockSpec((tm, tk), lambda i, j, k: (i, k))
    b_spec = pl.BlockSpec((tn, tk), lambda i, j, k: (j, k)) if tb else pl.BlockSpec((tk, tn), lambda i, j, k: (k, j))
    return pl.pallas_call(
        body, name=name,
        grid=(m_dim // tm, n_dim // tn, nk),
        in_specs=[a_spec, b_spec],
        out_specs=pl.BlockSpec((tm, tn), lambda i, j, k: (i, j)),
        out_shape=jax.ShapeDtypeStruct((m_dim, n_dim), out_dtype),
        scratch_shapes=[pltpu.VMEM((tm, tn), F32)] if nk > 1 else [],
        compiler_params=_cp("parallel", "parallel", "arbitrary"),
    )(a, b)


NORM_ROWS = 256


def _rms(x, w):
    return x * lax.rsqrt(jnp.mean(x * x, axis=1, keepdims=True) + EPS) * w


def _norm_fwd(x, w_row, name, out_dtype=BF16):
    def body(x_ref, w_ref, o_ref):
        o_ref[...] = _rms(x_ref[...], w_ref[...]).astype(out_dtype)

    return pl.pallas_call(
        body, name=name, grid=(SEQ // NORM_ROWS,),
        in_specs=[pl.BlockSpec((NORM_ROWS, D_MODEL), lambda i: (i, 0)), pl.BlockSpec((1, D_MODEL), lambda i: (0, 0))],
        out_specs=pl.BlockSpec((NORM_ROWS, D_MODEL), lambda i: (i, 0)),
        out_shape=jax.ShapeDtypeStruct((SEQ, D_MODEL), out_dtype),
        compiler_params=_cp("parallel"),
    )(x, w_row)


def _resnorm_norm_fwd(x, f, w_row, next_w_row, name):
    def body(x_ref, f_ref, w_ref, nw_ref, o_ref, h_ref):
        out = x_ref[...] + _rms(f_ref[...], w_ref[...])
        o_ref[...] = out
        h_ref[...] = _rms(out, nw_ref[...]).astype(BF16)

    blk = pl.BlockSpec((NORM_ROWS, D_MODEL), lambda i: (i, 0))
    row = pl.BlockSpec((1, D_MODEL), lambda i: (0, 0))
    return pl.pallas_call(
        body, name=name, grid=(SEQ // NORM_ROWS,),
        in_specs=[blk, blk, row, row],
        out_specs=[blk, blk],
        out_shape=[jax.ShapeDtypeStruct((SEQ, D_MODEL), F32), jax.ShapeDtypeStruct((SEQ, D_MODEL), BF16)],
        compiler_params=_cp("parallel"),
    )(x, f, w_row, next_w_row)


def _norm_bwd(x, w_row, dy, add, name, dx_dtype=F32):
    has_add = add is not None

    def body(*refs):
        if has_add:
            x_ref, w_ref, dy_ref, add_ref, dx_ref, dw_ref = refs
        else:
            x_ref, w_ref, dy_ref, dx_ref, dw_ref = refs
        _, vjp = jax.vjp(_rms, x_ref[...], w_ref[...])
        dx, dw = vjp(dy_ref[...])
        dx_ref[...] = (dx + add_ref[...] if has_add else dx).astype(dx_dtype)

        @pl.when(pl.program_id(0) == 0)
        def _():
            dw_ref[...] = jnp.zeros_like(dw_ref)

        dw_ref[...] += dw

    blk = pl.BlockSpec((NORM_ROWS, D_MODEL), lambda i: (i, 0))
    row = pl.BlockSpec((1, D_MODEL), lambda i: (0, 0))
    ins = [x, w_row, dy] + ([add] if has_add else [])
    return pl.pallas_call(
        body, name=name, grid=(SEQ // NORM_ROWS,),
        in_specs=[blk, row, blk] + ([blk] if has_add else []),
        out_specs=[blk, row],
        out_shape=[jax.ShapeDtypeStruct((SEQ, D_MODEL), dx_dtype), jax.ShapeDtypeStruct((1, D_MODEL), F32)],
        compiler_params=_cp("arbitrary"),
    )(*ins)


def _norm_bwd_pair(x_a, w_a, dy_a, add, x_b, w_b, name):
    def body(xa_ref, wa_ref, dya_ref, add_ref, xb_ref, wb_ref, dxa_ref, dxb_ref, dwa_ref, dwb_ref):
        _, vjp_a = jax.vjp(_rms, xa_ref[...], wa_ref[...])
        dxa, dwa = vjp_a(dya_ref[...])
        dxa = dxa + add_ref[...]
        _, vjp_b = jax.vjp(_rms, xb_ref[...], wb_ref[...])
        dxb, dwb = vjp_b(dxa)
        dxa_ref[...] = dxa
        dxb_ref[...] = dxb.astype(BF16)

        @pl.when(pl.program_id(0) == 0)
        def _():
            dwa_ref[...] = jnp.zeros_like(dwa_ref)
            dwb_ref[...] = jnp.zeros_like(dwb_ref)

        dwa_ref[...] += dwa
        dwb_ref[...] += dwb

    blk = pl.BlockSpec((NORM_ROWS, D_MODEL), lambda i: (i, 0))
    row = pl.BlockSpec((1, D_MODEL), lambda i: (0, 0))
    return pl.pallas_call(
        body, name=name, grid=(SEQ // NORM_ROWS,),
        in_specs=[blk, row, blk, blk, blk, row],
        out_specs=[blk, blk, row, row],
        out_shape=[jax.ShapeDtypeStruct((SEQ, D_MODEL), F32), jax.ShapeDtypeStruct((SEQ, D_MODEL), BF16),
                   jax.ShapeDtypeStruct((1, D_MODEL), F32), jax.ShapeDtypeStruct((1, D_MODEL), F32)],
        compiler_params=_cp("arbitrary"),
    )(x_a, w_a, dy_a, add, x_b, w_b)


def _resnorm_loss(x, f, w_row, target):
    def body(x_ref, f_ref, w_ref, t_ref, loss_ref, dy_ref):
        err = x_ref[...] + _rms(f_ref[...], w_ref[...]) - t_ref[...]
        dy_ref[...] = err * (1.0 / D_MODEL)

        @pl.when(pl.program_id(0) == 0)
        def _():
            loss_ref[...] = jnp.zeros_like(loss_ref)

        part = jnp.sum(jnp.sum(err * err, axis=1, keepdims=True) * (1.0 / D_MODEL), axis=0, keepdims=True)
        loss_ref[...] += 0.5 * jnp.broadcast_to(part, loss_ref.shape)

    blk = pl.BlockSpec((NORM_ROWS, D_MODEL), lambda i: (i, 0))
    return pl.pallas_call(
        body, name="norm_post_ffn_loss", grid=(SEQ // NORM_ROWS,),
        in_specs=[blk, blk, pl.BlockSpec((1, D_MODEL), lambda i: (0, 0)), blk],
        out_specs=[pl.BlockSpec((1, 128), lambda i: (0, 0)), blk],
        out_shape=[jax.ShapeDtypeStruct((1, 128), F32), jax.ShapeDtypeStruct((SEQ, D_MODEL), F32)],
        compiler_params=_cp("arbitrary"),
    )(x, f, w_row, target)


def _make_shift(j):
    def down(x):
        row = lax.broadcasted_iota(jnp.int32, x.shape, 0)
        return jnp.where(row >= j, pltpu.roll(x, j, 0), 0.0)

    def up(x):
        n = x.shape[0]
        row = lax.broadcasted_iota(jnp.int32, x.shape, 0)
        return jnp.where(row < n - j, pltpu.roll(x, n - j, 0), 0.0)

    f = jax.custom_vjp(down)
    f.defvjp(lambda x: (down(x), None), lambda _, g: (up(g),))
    return f


_SHIFT = {j: _make_shift(j) for j in (1, 2, 3)}


def _causal_conv(x, taps):
    n = len(taps)
    acc = x * taps[n - 1]
    for k in range(n - 1):
        acc = acc + _SHIFT[n - 1 - k](x) * taps[k]
    return acc


def _tap_rows(w_ref, lanes=slice(None)):
    return tuple(w_ref[k:k + 1, lanes] for k in range(w_ref.shape[0]))


def _sigmoid(x):
    return 1.0 / (1.0 + jnp.exp(-x))


def _silu(x):
    return x * _sigmoid(x)


def _softplus(x):
    return jnp.maximum(x, 0.0) + jnp.log(1.0 + jnp.exp(-jnp.abs(x)))


def _gelu_tanh(x):
    return 0.5 * x * (1.0 + jnp.tanh(math.sqrt(2.0 / math.pi) * (x + 0.044715 * (x * x * x))))


def _dnconv_fn(x, taps):
    return _silu(_causal_conv(x, taps))


def _dnconv_fwd(proj, conv_w):
    def body(x_ref, w_ref, o_ref):
        o_ref[...] = _dnconv_fn(x_ref[...], _tap_rows(w_ref)).astype(BF16)

    return pl.pallas_call(
        body, name="dnconv_fwd", grid=(DN_QKV_BLKS,),
        in_specs=[pl.BlockSpec((SEQ, 128), lambda j: (0, DN_QKV_BLK0 + j)), pl.BlockSpec((4, 128), lambda j: (0, j))],
        out_specs=pl.BlockSpec((SEQ, 128), lambda j: (0, j)),
        out_shape=jax.ShapeDtypeStruct((SEQ, 1536), BF16),
        compiler_params=_cp("parallel"),
    )(proj, conv_w)


def _dnconv_bwd(proj, conv_w, dc, dproj):
    def body(x_ref, w_ref, dc_ref, _, dx_ref, dw_ref):
        _, vjp = jax.vjp(_dnconv_fn, x_ref[...], _tap_rows(w_ref))
        dx, dw = vjp(dc_ref[...])
        dx_ref[...] = dx.astype(BF16)
        for k, row in enumerate(dw):
            dw_ref[k:k + 1, :] = row

    return pl.pallas_call(
        body, name="dnconv_bwd", grid=(DN_QKV_BLKS,),
        in_specs=[pl.BlockSpec((SEQ, 128), lambda j: (0, DN_QKV_BLK0 + j)), pl.BlockSpec((4, 128), lambda j: (0, j)),
                  pl.BlockSpec((SEQ, 128), lambda j: (0, j)), pl.BlockSpec(memory_space=pl.ANY)],
        out_specs=[pl.BlockSpec((SEQ, 128), lambda j: (0, DN_QKV_BLK0 + j)), pl.BlockSpec((4, 128), lambda j: (0, j))],
        out_shape=[jax.ShapeDtypeStruct((SEQ, IN_PAD), BF16), jax.ShapeDtypeStruct((4, 1536), F32)],
        input_output_aliases={3: 0},
        compiler_params=_cp("parallel"),
    )(proj, conv_w, dc, dproj)


def _ffact_fn(pg, pu, wg, wu, bg, bu):
    return _gelu_tanh(_causal_conv(pg, wg) + bg) * (_causal_conv(pu, wu) + bu)


def _ffact_args(p_ref, w_ref, b_ref):
    g, u = slice(0, 128), slice(128, 256)
    return (p_ref[:, g].astype(F32), p_ref[:, u].astype(F32), _tap_rows(w_ref, g), _tap_rows(w_ref, u),
            b_ref[:, g], b_ref[:, u])


def _ffact_fwd(pre, conv_w, conv_b, exchanges=()):
    def body(p_ref, w_ref, b_ref, o_ref):
        o_ref[...] = _ffact_fn(*_ffact_args(p_ref, w_ref, b_ref)).astype(BF16)

    (act,), results = _hosted_call(
        body, name="ffact_fwd", steps=FF_BLKS,
        in_specs=[pl.BlockSpec((SEQ, 256), lambda j: (0, j)), pl.BlockSpec((3, 256), lambda j: (0, j)),
                  pl.BlockSpec((1, 256), lambda j: (0, j))],
        out_specs=[pl.BlockSpec((SEQ, 128), lambda j: (0, j))],
        out_shape=[jax.ShapeDtypeStruct((SEQ, D_FF), BF16)],
        scratch_shapes=[], operands=(pre, conv_w, conv_b), exchanges=exchanges)
    return act, results


def _ffact_bwd(pre, conv_w, conv_b, dact, exchanges=()):
    def body(p_ref, w_ref, b_ref, da_ref, dp_ref, dw_ref, db_ref):
        _, vjp = jax.vjp(_ffact_fn, *_ffact_args(p_ref, w_ref, b_ref))
        dpg, dpu, dwg, dwu, dbg, dbu = vjp(da_ref[...].astype(F32))
        dp_ref[:, 0:128] = dpg.astype(BF16)
        dp_ref[:, 128:256] = dpu.astype(BF16)
        for k in range(3):
            dw_ref[k:k + 1, 0:128] = dwg[k]
            dw_ref[k:k + 1, 128:256] = dwu[k]
        db_ref[:, 0:128] = dbg
        db_ref[:, 128:256] = dbu

    return _hosted_call(
        body, name="ffact_bwd", steps=FF_BLKS,
        in_specs=[pl.BlockSpec((SEQ, 256), lambda j: (0, j)), pl.BlockSpec((3, 256), lambda j: (0, j)),
                  pl.BlockSpec((1, 256), lambda j: (0, j)), pl.BlockSpec((SEQ, 128), lambda j: (0, j))],
        out_specs=[pl.BlockSpec((SEQ, 256), lambda j: (0, j)), pl.BlockSpec((3, 256), lambda j: (0, j)),
                   pl.BlockSpec((1, 256), lambda j: (0, j))],
        out_shape=[jax.ShapeDtypeStruct((SEQ, 2 * D_FF), BF16), jax.ShapeDtypeStruct((3, 2 * D_FF), F32),
                   jax.ShapeDtypeStruct((1, 2 * D_FF), F32)],
        scratch_shapes=[], operands=(pre, conv_w, conv_b, dact), exchanges=exchanges)


def _interleave_ff(t):
    lead = t.shape[:-1]
    return t.reshape(lead + (2, FF_BLKS, 128)).swapaxes(-3, -2).reshape(lead + (2 * D_FF,))


def _deinterleave_ff(t):
    lead = t.shape[:-1]
    return t.reshape(lead + (FF_BLKS, 2, 128)).swapaxes(-3, -2).reshape(lead + (2 * D_FF,))


def _rope_tables():
    inv = 1.0 / (ROPE_THETA ** (jnp.arange(0, HEAD_DIM, 2, dtype=F32) / HEAD_DIM))
    ang = jnp.arange(SEQ, dtype=F32)[:, None] * inv[None, :]
    cos = jnp.tile(jnp.cos(ang), (1, 4))
    sin = jnp.tile(jnp.sin(ang), (1, 4))
    sign = jnp.where((jnp.arange(128) % HEAD_DIM) < HEAD_DIM // 2, -1.0, 1.0).astype(F32)
    return cos, sin * sign[None, :]


def _rope(x, cos, sin_signed):
    lane = lax.broadcasted_iota(jnp.int32, x.shape, 1)
    partner = jnp.where((lane % HEAD_DIM) < HEAD_DIM // 2, pltpu.roll(x, 128 - HEAD_DIM // 2, 1),
                        pltpu.roll(x, HEAD_DIM // 2, 1))
    return x * cos + partner * sin_signed


def _head_masks():
    lane = lax.broadcasted_iota(jnp.int32, (1, 128), 1)
    return [(lane // HEAD_DIM) == h for h in range(2)]


def _both_heads(x):
    return jnp.concatenate([jnp.where(hm, x, 0.0)[None] for hm in _head_masks()], axis=0)


def _block_keys(branch, k_s, v_s, rows, prows, has_prev):
    a = lax.broadcasted_iota(jnp.int32, (ATTN_BLK, ATTN_BLK), 0)
    c = lax.broadcasted_iota(jnp.int32, (ATTN_BLK, ATTN_BLK), 1)
    keys, values, mask = k_s[rows, :], v_s[rows, :], c <= a
    if SEGMENT_BLOCKS[branch] > 1:
        keys = jnp.concatenate([k_s[prows, :], keys], axis=0)
        values = jnp.concatenate([v_s[prows, :], values], axis=0)
        mask = jnp.concatenate([(c >= a) & has_prev, mask], axis=1)
    twice = lambda t: jnp.broadcast_to(t[None], (2,) + t.shape)
    return twice(keys), twice(values), mask


def _block_rows(branch, t):
    d, per_seg = DILATIONS[branch], SEGMENT_BLOCKS[branch]
    if d == 1:
        start = pl.multiple_of(t * ATTN_BLK, ATTN_BLK)
        prev = pl.multiple_of(jnp.maximum(t - 1, 0) * ATTN_BLK, ATTN_BLK)
        return pl.ds(start, ATTN_BLK), pl.ds(prev, ATTN_BLK), t > 0
    r, n = t // per_seg, t % per_seg
    start = n * (ATTN_BLK * d) + r
    prev = jnp.maximum(n - 1, 0) * (ATTN_BLK * d) + r
    return pl.ds(start, ATTN_BLK, stride=d), pl.ds(prev, ATTN_BLK, stride=d), n > 0


def _attn_fwd(proj, cos, sin_signed, exchanges=()):
    scale = HEAD_DIM ** -0.5

    def body(qkv_ref, cos_ref, sin_ref, out_ref, lse_ref, q_s, k_s, v_s, *branch_s):
        o_s, l_s = branch_s[:3], branch_s[3:]
        q_s[...] = _rope(qkv_ref[:, 0:128], cos_ref[...], sin_ref[...])
        k_s[...] = _rope(qkv_ref[:, 128:256], cos_ref[...], sin_ref[...])
        v_s[...] = qkv_ref[:, 256:384]
        heads = _head_masks()
        for branch in range(3):
            def block(t, carry, branch=branch):
                rows, prows, has_prev = _block_rows(branch, t)
                keys, values, mask = _block_keys(branch, k_s, v_s, rows, prows, has_prev)
                s = jnp.where(mask, BMM_NT(_both_heads(q_s[rows, :]), keys) * scale, NEG)
                m = jnp.max(s, axis=2, keepdims=True)
                e = jnp.exp(s - m)
                l = jnp.sum(e, axis=2, keepdims=True)
                o = BMM(e, values) / l
                lse_b = m + jnp.log(l)
                o_s[branch][rows, :] = jnp.where(heads[0], o[0], o[1])
                l_s[branch][rows, :] = jnp.where(heads[0], lse_b[0], lse_b[1])
                return carry

            lax.fori_loop(0, N_BLK, block, 0, unroll=8)
        l0, l1, l2 = l_s[0][...], l_s[1][...], l_s[2][...]
        m = jnp.maximum(jnp.maximum(l0, l1), l2)
        w0, w1, w2 = jnp.exp(l0 - m), jnp.exp(l1 - m), jnp.exp(l2 - m)
        den = w0 + w1 + w2
        out_ref[...] = (w0 * o_s[0][...] + w1 * o_s[1][...] + w2 * o_s[2][...]) / den
        lse_ref[...] = m + jnp.log(den)

    tab = pl.BlockSpec((SEQ, 128), lambda j: (0, 0))
    col = pl.BlockSpec((SEQ, 128), lambda j: (0, j))
    return _hosted_call(
        body, name="attn_fwd", steps=N_PAIR,
        in_specs=[pl.BlockSpec((SEQ, 384), lambda j: (0, j)), tab, tab],
        out_specs=[col, col],
        out_shape=[jax.ShapeDtypeStruct((SEQ, 2 * ATTN_W), F32), jax.ShapeDtypeStruct((SEQ, ATTN_W), F32)],
        scratch_shapes=[pltpu.VMEM((SEQ, 128), F32)] * 9,
        operands=(proj, cos, sin_signed), exchanges=exchanges)


def _attn_bwd(proj, cos, sin_signed, cat, lse, dcat, dproj, exchanges=()):
    scale = HEAD_DIM ** -0.5

    def body(qkv_ref, cos_ref, sin_ref, out_ref, lse_ref, do_ref, _, dqkv_ref, q_s, k_s, v_s, dq_s, dk_s, dv_s,
             dod_s):
        q_s[...] = _rope(qkv_ref[:, 0:128], cos_ref[...], sin_ref[...])
        k_s[...] = _rope(qkv_ref[:, 128:256], cos_ref[...], sin_ref[...])
        v_s[...] = qkv_ref[:, 256:384]
        dq_s[...] = jnp.zeros_like(dq_s)
        dk_s[...] = jnp.zeros_like(dk_s)
        dv_s[...] = jnp.zeros_like(dv_s)
        dod_s[...] = do_ref[...] * out_ref[...]
        heads = _head_masks()
        for branch in range(3):
            def block(t, carry, branch=branch):
                rows, prows, has_prev = _block_rows(branch, t)
                keys, values, mask = _block_keys(branch, k_s, v_s, rows, prows, has_prev)
                q2, do2 = _both_heads(q_s[rows, :]), _both_heads(do_ref[rows, :])
                lse_b, dod = lse_ref[rows, :], dod_s[rows, :]
                lse2 = jnp.concatenate(
                    [jnp.max(jnp.where(hm, lse_b, NEG), axis=1, keepdims=True)[None] for hm in heads], axis=0)
                delta = jnp.concatenate(
                    [jnp.sum(jnp.where(hm, dod, 0.0), axis=1, keepdims=True)[None] for hm in heads], axis=0)
                p = jnp.exp(jnp.where(mask, BMM_NT(q2, keys) * scale, NEG) - lse2)
                ds = p * (BMM_NT(do2, values) - delta) * scale
                dq = BMM(ds, keys)
                dk = BMM_TN(ds, q2)
                dv = BMM_TN(p, do2)
                dk, dv = dk[0] + dk[1], dv[0] + dv[1]
                dq_s[rows, :] += jnp.where(heads[0], dq[0], dq[1])
                if SEGMENT_BLOCKS[branch] > 1:
                    dk_s[rows, :] += dk[ATTN_BLK:]
                    dv_s[rows, :] += dv[ATTN_BLK:]

                    @pl.when(has_prev)
                    def _():
                        dk_s[prows, :] += dk[:ATTN_BLK]
                        dv_s[prows, :] += dv[:ATTN_BLK]
                else:
                    dk_s[rows, :] += dk
                    dv_s[rows, :] += dv
                return carry

            lax.fori_loop(0, N_BLK, block, 0, unroll=4)
        dqkv_ref[:, 0:128] = _rope(dq_s[...], cos_ref[...], -sin_ref[...]).astype(BF16)
        dqkv_ref[:, 128:256] = _rope(dk_s[...], cos_ref[...], -sin_ref[...]).astype(BF16)
        dqkv_ref[:, 256:384] = dv_s[...].astype(BF16)

    tab = pl.BlockSpec((SEQ, 128), lambda j: (0, 0))
    col = pl.BlockSpec((SEQ, 128), lambda j: (0, j))
    qkv = pl.BlockSpec((SEQ, 384), lambda j: (0, j))
    (dproj,), results = _hosted_call(
        body, name="attn_bwd", steps=N_PAIR,
        in_specs=[qkv, tab, tab, col, col, col, pl.BlockSpec(memory_space=pl.ANY)],
        out_specs=[qkv],
        out_shape=[jax.ShapeDtypeStruct((SEQ, IN_PAD), BF16)],
        scratch_shapes=[pltpu.VMEM((SEQ, 128), F32)] * 7,
        operands=(proj, cos, sin_signed, cat, lse, dcat, dproj), exchanges=exchanges, aliases={6: 0})
    return dproj, results


def _bdot(a, b, dims, precision=None):
    if precision is None:
        a = a.astype(BF16)
        b = b.astype(BF16)
    return lax.dot_general(a, b, (dims, ((0,), (0,))), preferred_element_type=F32, precision=precision)


def _make_bmm(precision):
    @jax.custom_vjp
    def nn(a, b):
        return _bdot(a, b, ((2,), (1,)), precision)

    @jax.custom_vjp
    def nt(a, b):
        return _bdot(a, b, ((2,), (2,)), precision)

    @jax.custom_vjp
    def tn(a, b):
        return _bdot(a, b, ((1,), (1,)), precision)

    nn.defvjp(lambda a, b: (nn(a, b), (a, b)), lambda r, g: (nt(g, r[1]), tn(r[0], g)))
    nt.defvjp(lambda a, b: (nt(a, b), (a, b)), lambda r, g: (nn(g, r[1]), tn(g, r[0])))
    tn.defvjp(lambda a, b: (tn(a, b), (a, b)), lambda r, g: (nt(r[1], g), nn(r[0], g)))
    return nn, nt, tn


BMM, BMM_NT, BMM_TN = _make_bmm(None)
BMM3, BMM3_NT, BMM3_TN = _make_bmm(lax.Precision.HIGH)
MM3, _, _ = _make_mm(lax.Precision.HIGH)


def _head_lanes(t, off):
    lane = lax.broadcasted_iota(jnp.int32, (1, 128), 1)
    return jnp.concatenate(
        [jnp.sum(t * (lane == off + h).astype(F32), axis=1, keepdims=True)[None] for h in range(NDH)], axis=0)


@jax.custom_vjp
def _unit_lower_inverse(a_mat):
    c = a_mat.shape[1]
    eye = (lax.broadcasted_iota(jnp.int32, (c, c), 0) == lax.broadcasted_iota(jnp.int32, (c, c), 1)).astype(F32)
    power = -a_mat
    t_inv = eye + power
    for _ in range(5):
        power = BMM3(power, power)
        t_inv = t_inv + BMM3(t_inv, power)
    return t_inv


def _unit_lower_inverse_fwd(a_mat):
    t_inv = _unit_lower_inverse(a_mat)
    return t_inv, t_inv


def _unit_lower_inverse_bwd(t_inv, d_inv):
    return (-BMM3_NT(BMM3_TN(t_inv, d_inv), t_inv),)


_unit_lower_inverse.defvjp(_unit_lower_inverse_fwd, _unit_lower_inverse_bwd)


DN_STEP_CHUNKS = 4
DN_STEP_ROWS = DN_STEP_CHUNKS * CH
DN_STEPS = NCH // DN_STEP_CHUNKS
DN_BATCH = DN_STEP_CHUNKS * NDH


def _delta_chunks(qr, kr, vr, z, tail, alog_row, dt_row, nw, state):
    c = qr.shape[1]
    tails = [tail[CH * n:CH * (n + 1)] for n in range(DN_STEP_CHUNKS)]
    per_chunk = lambda t: jnp.concatenate([t] * DN_STEP_CHUNKS, axis=0)
    beta = _sigmoid(jnp.concatenate([_head_lanes(t, 0) for t in tails], axis=0))
    a_raw = jnp.concatenate([_head_lanes(t, NDH) for t in tails], axis=0)
    g = -jnp.exp(per_chunk(_head_lanes(alog_row, 0))) * _softplus(a_raw + per_chunk(_head_lanes(dt_row, 0)))

    q = qr * lax.rsqrt(jnp.sum(qr * qr, axis=2, keepdims=True) + EPS) * (128 ** -0.5)
    k = kr * lax.rsqrt(jnp.sum(kr * kr, axis=2, keepdims=True) + EPS)

    ri = lax.broadcasted_iota(jnp.int32, (c, c), 0)
    ci = lax.broadcasted_iota(jnp.int32, (c, c), 1)
    tril = ri >= ci
    lane = lax.broadcasted_iota(jnp.int32, (1, 128), 1)
    pick = [(lane == b).astype(F32) for b in range(DN_BATCH)]
    g_lanes = sum(g[b] * pick[b] for b in range(DN_BATCH))
    g_sums = MM3(tril.astype(F32), g_lanes)
    gc = jnp.concatenate([jnp.sum(g_sums * pick[b], axis=1, keepdims=True)[None] for b in range(DN_BATCH)],
                         axis=0)
    g_row = jnp.swapaxes(jnp.broadcast_to(gc, (DN_BATCH, c, c)), 1, 2)
    decay = jnp.where(tril, jnp.exp(jnp.where(tril, gc - g_row, 0.0)), 0.0)
    kb = k * beta
    t_inv = _unit_lower_inverse(jnp.where(ri > ci, BMM_NT(kb, k) * decay, 0.0))
    eg = jnp.exp(gc)
    u = BMM(t_inv, vr * beta)
    w = BMM(t_inv, kb * eg)
    qk = BMM_NT(q, k) * decay
    g_tot = jnp.sum(g, axis=1, keepdims=True)
    q_dec = q * eg
    k_dec = k * jnp.exp(g_tot - gc)
    outs = []
    for n in range(DN_STEP_CHUNKS):
        heads = slice(NDH * n, NDH * (n + 1))
        v_new = u[heads] - BMM(w[heads], state)
        outs.append(BMM(q_dec[heads], state) + BMM(qk[heads], v_new))
        state = state * jnp.exp(g_tot[heads]) + BMM_TN(k_dec[heads], v_new)
    o = jnp.concatenate(outs, axis=0)
    on = o * lax.rsqrt(jnp.mean(o * o, axis=2, keepdims=True) + EPS) * nw
    return on * _silu(z), state


def _heads(v, off=0):
    return jnp.concatenate([v[None, CH * n:CH * (n + 1), off + 128 * h:off + 128 * (h + 1)]
                            for n in range(DN_STEP_CHUNKS) for h in range(NDH)], axis=0)


def _unheads(t):
    return jnp.concatenate([jnp.concatenate([t[NDH * n + h] for h in range(NDH)], axis=1)
                            for n in range(DN_STEP_CHUNKS)], axis=0)


def _delta_fwd(c_qkv, proj, alog_row, dt_row, nw, cat, exchanges=()):
    def body(c_ref, z_ref, tail_ref, al_ref, dt_ref, nw_ref, _, y_ref, st_ref, state):
        @pl.when(pl.program_id(0) == 0)
        def _():
            state[...] = jnp.zeros_like(state)

        cv = c_ref[...].astype(F32)
        st_ref[0] = state[...]
        y, new_state = _delta_chunks(_heads(cv), _heads(cv, 512), _heads(cv, 1024), _heads(z_ref[...]), tail_ref[...],
                                     al_ref[...], dt_ref[...], nw_ref[...], state[...])
        y_ref[...] = _unheads(y)
        state[...] = new_state

    row = pl.BlockSpec((1, 128), lambda n: (0, 0))
    rows = DN_STEP_ROWS
    return _hosted_call(
        body, name="delta_fwd", steps=DN_STEPS,
        in_specs=[pl.BlockSpec((rows, 1536), lambda n: (n, 0)), pl.BlockSpec((rows, 512), lambda n: (n, DN_Z_COL // 512)),
                  pl.BlockSpec((rows, 128), lambda n: (n, DN_TAIL_BLK)), row, row, row, pl.BlockSpec(memory_space=pl.ANY)],
        out_specs=[pl.BlockSpec((rows, 512), lambda n: (n, 1)),
                   pl.BlockSpec((1, NDH, 128, 128), lambda n: (n, 0, 0, 0))],
        out_shape=[jax.ShapeDtypeStruct((SEQ, 2 * ATTN_W), F32), jax.ShapeDtypeStruct((DN_STEPS, NDH, 128, 128), F32)],
        scratch_shapes=[pltpu.VMEM((NDH, 128, 128), F32)],
        operands=(c_qkv, proj, proj, alog_row, dt_row, nw, cat), exchanges=exchanges, aliases={6: 0})


def _delta_bwd(c_qkv, proj, alog_row, dt_row, nw, states, dcat, exchanges=()):
    def body(c_ref, z_ref, tail_ref, al_ref, dt_ref, nw_ref, st_ref, dy_ref,
             dp_ref, dc_ref, dal_ref, ddt_ref, dnw_ref, dstate):
        @pl.when(pl.program_id(0) == 0)
        def _():
            dstate[...] = jnp.zeros_like(dstate)
            dal_ref[...] = jnp.zeros_like(dal_ref)
            ddt_ref[...] = jnp.zeros_like(ddt_ref)
            dnw_ref[...] = jnp.zeros_like(dnw_ref)

        cv = c_ref[...].astype(F32)
        _, vjp = jax.vjp(_delta_chunks, _heads(cv), _heads(cv, 512), _heads(cv, 1024), _heads(z_ref[...]),
                         tail_ref[...], al_ref[...], dt_ref[...], nw_ref[...], st_ref[0])
        dq, dk, dv, dz, dtail, dal, ddt, dnw, dst = vjp((_heads(dy_ref[...]), dstate[...]))
        dstate[...] = dst
        dc_ref[...] = jnp.concatenate([_unheads(dq), _unheads(dk), _unheads(dv)], axis=1)
        dp_ref[...] = jnp.concatenate([_unheads(dz), dtail, jnp.zeros((DN_STEP_ROWS, 128), F32)], axis=1).astype(BF16)
        dal_ref[...] += dal
        ddt_ref[...] += ddt
        dnw_ref[...] += dnw

    rev = lambda n: DN_STEPS - 1 - n
    row = pl.BlockSpec((1, 128), lambda n: (0, 0))
    rows = DN_STEP_ROWS
    return _hosted_call(
        body, name="delta_bwd", steps=DN_STEPS,
        in_specs=[pl.BlockSpec((rows, 1536), lambda n: (rev(n), 0)),
                  pl.BlockSpec((rows, 512), lambda n: (rev(n), DN_Z_COL // 512)),
                  pl.BlockSpec((rows, 128), lambda n: (rev(n), DN_TAIL_BLK)), row, row, row,
                  pl.BlockSpec((1, NDH, 128, 128), lambda n: (rev(n), 0, 0, 0)),
                  pl.BlockSpec((rows, 512), lambda n: (rev(n), 1))],
        out_specs=[pl.BlockSpec((rows, 768), lambda n: (rev(n), DN_Z_COL // 768)),
                   pl.BlockSpec((rows, 1536), lambda n: (rev(n), 0)), row, row, row],
        out_shape=[jax.ShapeDtypeStruct((SEQ, IN_PAD), BF16), jax.ShapeDtypeStruct((SEQ, 1536), F32)]
        + [jax.ShapeDtypeStruct((1, 128), F32)] * 3,
        scratch_shapes=[pltpu.VMEM((NDH, 128, 128), F32)],
        operands=(c_qkv, proj, proj, alog_row, dt_row, nw, states, dcat), exchanges=exchanges)


def _place():
    x, y, c = lax.axis_index("x"), lax.axis_index("y"), lax.axis_index("c")
    other_chips = [(1 - x, y), (x, 1 - y), (1 - x, 1 - y)]
    return x, y, c, other_chips


def _gather_exchange(shards):
    n = len(shards)

    def copies(ins, outs, sems):
        send_sems, recv_sems, local_sems = sems
        x, y, c, chips = _place()
        me, sibling = (x, y, c), (x, y, 1 - c)

        def copy(b, k, block, to, src=None):
            slot = outs[b].at[4 * block[0] + 2 * block[1] + block[2]]
            return pltpu.make_async_remote_copy(
                src_ref=slot if src is None else src, dst_ref=slot,
                send_sem=send_sems.at[b, k], recv_sem=recv_sems.at[b, k], device_id=to, device_id_type=MESH)

        mine = [pltpu.make_async_copy(ins[b], outs[b].at[4 * x + 2 * y + c], local_sems.at[b]) for b in range(n)]
        first = []
        for b in range(n):
            first.append(copy(b, 0, me, sibling, src=ins[b]))
            first += [copy(b, 1 + j, me, (*chip, c), src=ins[b]) for j, chip in enumerate(chips)]
        over_ici = [copy(b, 1 + j, (*chip, c), me) for b in range(n) for j, chip in enumerate(chips)]
        passed = [copy(b, 4 + j, (*chip, c), sibling) for b in range(n) for j, chip in enumerate(chips)]
        from_sibling = []
        for b in range(n):
            from_sibling.append(copy(b, 0, sibling, me))
            from_sibling += [copy(b, 4 + j, (*chip, 1 - c), me) for j, chip in enumerate(chips)]
        return mine, first, over_ici, passed, from_sibling

    def start(ins, outs, sems):
        mine, first, _, _, _ = copies(ins, outs, sems)
        for cp in mine + first:
            cp.start()

    def middle(ins, outs, sems):
        _, _, over_ici, passed, _ = copies(ins, outs, sems)
        for arrived, onward in zip(over_ici, passed):
            arrived.wait_recv()
            onward.start()

    def finish(ins, outs, sems):
        mine, first, _, passed, from_sibling = copies(ins, outs, sems)
        for cp in from_sibling:
            cp.wait_recv()
        for cp in first + passed:
            cp.wait_send()
        for cp in mine:
            cp.wait()

    return Exchange(shards, [jax.ShapeDtypeStruct((N_DEV,) + s.shape, s.dtype) for s in shards],
                    [pltpu.SemaphoreType.DMA((n, 7)), pltpu.SemaphoreType.DMA((n, 7)), pltpu.SemaphoreType.DMA((n,))],
                    start, middle, finish)


def _sibling_exchange(gs):
    n = len(gs)

    def copies(ins, outs, sems):
        send_sems, recv_sems = sems
        x, y, c, _ = _place()
        return [pltpu.make_async_remote_copy(
            src_ref=ins[b].at[2 * p + (1 - c)], dst_ref=outs[b].at[p],
            send_sem=send_sems.at[b, p], recv_sem=recv_sems.at[b, p],
            device_id=(x, y, 1 - c), device_id_type=MESH) for b in range(n) for p in range(4)]

    def start(ins, outs, sems):
        for cp in copies(ins, outs, sems):
            cp.start()

    def finish(ins, outs, sems):
        for cp in copies(ins, outs, sems):
            cp.wait()

    return Exchange(gs, [jax.ShapeDtypeStruct((4,) + g.shape[1:], g.dtype) for g in gs],
                    [pltpu.SemaphoreType.DMA((n, 4)), pltpu.SemaphoreType.DMA((n, 4))], start, None, finish)


def _chips_exchange(hs):
    n = len(hs)

    def copies(ins, outs, sems):
        send_sems, recv_sems, local_sems = sems
        x, y, c, chips = _place()
        my_chip = 2 * x + y
        local = [pltpu.make_async_copy(ins[b].at[my_chip], outs[b].at[my_chip], local_sems.at[b]) for b in range(n)]
        sends, arrivals = [], []
        for b in range(n):
            for k, (px, py) in enumerate(chips):
                peer = 2 * px + py
                sends.append(pltpu.make_async_remote_copy(
                    src_ref=ins[b].at[peer], dst_ref=outs[b].at[my_chip],
                    send_sem=send_sems.at[b, k], recv_sem=recv_sems.at[b, k],
                    device_id=(px, py, c), device_id_type=MESH))
                arrivals.append(pltpu.make_async_remote_copy(
                    src_ref=ins[b].at[peer], dst_ref=outs[b].at[peer],
                    send_sem=send_sems.at[b, k], recv_sem=recv_sems.at[b, k],
                    device_id=(px, py, c), device_id_type=MESH))
        return local, sends, arrivals

    def start(ins, outs, sems):
        local, sends, _ = copies(ins, outs, sems)
        for cp in local + sends:
            cp.start()

    def finish(ins, outs, sems):
        local, sends, arrivals = copies(ins, outs, sems)
        for cp in arrivals:
            cp.wait_recv()
        for cp in sends:
            cp.wait_send()
        for cp in local:
            cp.wait()

    return Exchange(hs, [jax.ShapeDtypeStruct(h.shape, h.dtype) for h in hs],
                    [pltpu.SemaphoreType.DMA((n, 3)), pltpu.SemaphoreType.DMA((n, 3)), pltpu.SemaphoreType.DMA((n,))],
                    start, None, finish)


def _run_exchange(exchange, name):
    n_in, n_out = len(exchange.operands), len(exchange.out_shapes)

    def body(*refs):
        ins, outs, sems = refs[:n_in], refs[n_in:n_in + n_out], refs[n_in + n_out:]
        exchange.start(ins, outs, sems)
        if exchange.middle is not None:
            exchange.middle(ins, outs, sems)
        exchange.finish(ins, outs, sems)

    return pl.pallas_call(
        body, name=name,
        in_specs=[HBM_SPEC] * n_in, out_specs=[HBM_SPEC] * n_out,
        out_shape=exchange.out_shapes, scratch_shapes=exchange.sems,
    )(*exchange.operands)


def _pair_add(g, r, core, name):
    _, nr, nc = g.shape
    tr = nr // 2 if nr % 32 == 0 else nr

    def body(core_ref, g_ref, r_ref, o_ref):
        o_ref[...] = (g_ref[...].astype(F32) + r_ref[...].astype(F32)).astype(BF16)

    return pl.pallas_call(
        body, name=name,
        grid_spec=pltpu.PrefetchScalarGridSpec(
            num_scalar_prefetch=1, grid=(4, nr // tr),
            in_specs=[pl.BlockSpec((1, tr, nc), lambda p, i, core: (2 * p + core[0], i, 0)),
                      pl.BlockSpec((1, tr, nc), lambda p, i, core: (p, i, 0))],
            out_specs=pl.BlockSpec((1, tr, nc), lambda p, i, core: (p, i, 0))),
        out_shape=jax.ShapeDtypeStruct(r.shape, BF16),
        compiler_params=_cp("parallel", "parallel"),
    )(core, g, r)


def _all_gather_sum_small(v):
    rows = v.shape[0]

    def body(x_ref, sum_ref, out_ref, send_sems, recv_sems, local_sem):
        x, y, c, chips = _place()
        me, sibling = (x, y, c), (x, y, 1 - c)

        def block(px, py, pc):
            return out_ref.at[pl.ds((4 * px + 2 * py + pc) * rows, rows), :]

        def copy(k, blk, to, src=None):
            return pltpu.make_async_remote_copy(
                src_ref=block(*blk) if src is None else src, dst_ref=block(*blk),
                send_sem=send_sems.at[k], recv_sem=recv_sems.at[k], device_id=to, device_id_type=MESH)

        mine = pltpu.make_async_copy(x_ref, block(*me), local_sem)
        mine.start()
        first = [copy(0, me, sibling, src=x_ref)]
        first += [copy(1 + j, me, (*chip, c), src=x_ref) for j, chip in enumerate(chips)]
        for cp in first:
            cp.start()
        passed = [copy(4 + j, (*chip, c), sibling) for j, chip in enumerate(chips)]
        for j, chip in enumerate(chips):
            copy(1 + j, (*chip, c), me).wait_recv()
            passed[j].start()
        copy(0, sibling, me).wait_recv()
        for j, chip in enumerate(chips):
            copy(4 + j, (*chip, 1 - c), me).wait_recv()
        for cp in first + passed:
            cp.wait_send()
        mine.wait()
        total = out_ref[pl.ds(0, rows), :]
        for d in range(1, N_DEV):
            total = total + out_ref[pl.ds(d * rows, rows), :]
        sum_ref[...] = total

    vm = pl.BlockSpec(memory_space=pltpu.VMEM)
    return pl.pallas_call(
        body, name="small_all_reduce",
        in_specs=[vm], out_specs=[vm],
        out_shape=[jax.ShapeDtypeStruct((rows, 128), F32)],
        scratch_shapes=[pltpu.VMEM((N_DEV * rows, 128), F32), pltpu.SemaphoreType.DMA((7,)),
                        pltpu.SemaphoreType.DMA((7,)), pltpu.SemaphoreType.DMA],
    )(v)[0]


def _adamw(w, g, m, v):
    m = ADAM_B1 * m + (1.0 - ADAM_B1) * g
    v = ADAM_B2 * v + (1.0 - ADAM_B2) * (g * g)
    m_hat = m / (1.0 - ADAM_B1 ** ADAM_STEP)
    v_hat = v / (1.0 - ADAM_B2 ** ADAM_STEP)
    delta = -ADAM_LR * (m_hat / (jnp.sqrt(v_hat) + ADAM_EPS) + ADAM_WD * w)
    return delta, m, v


ADAM_TILE = dict(w_in=(IN_COLS // N_DEV, 256), w_out=(128, D_MODEL), ffn_w_in=(176, D_MODEL), ffn_w_out=(176, D_MODEL))


def _sum_chips(p):
    p = p.astype(F32)
    return (p[0] + p[1]) + (p[2] + p[3])


def _adamw_sharded(parts, w, m, v, tile, name):
    nl, nr, nc = w.shape
    tr, tc = tile

    def body(*refs):
        p_refs, (w_ref, m_ref, v_ref, g_ref, d_ref, nm_ref, nv_ref) = refs[:nl], refs[nl:]
        layer = pl.program_id(0)
        p = p_refs[0][...]
        for l in range(1, nl):
            p = jnp.where(layer == l, p_refs[l][...], p)
        g = _sum_chips(p)
        delta, nm, nv = _adamw(w_ref[0], g, m_ref[0], v_ref[0])
        g_ref[0] = g
        d_ref[0] = delta
        nm_ref[0] = nm
        nv_ref[0] = nv

    blk = pl.BlockSpec((1, tr, tc), lambda l, i, j: (l, i, j))
    return pl.pallas_call(
        body, name=name, grid=(nl, nr // tr, nc // tc),
        in_specs=[pl.BlockSpec((4, tr, tc), lambda l, i, j, own=own: (0, jnp.where(l == own, i, 0), j))
                  for own in range(nl)] + [blk, blk, blk],
        out_specs=[blk] * 4,
        out_shape=[jax.ShapeDtypeStruct(w.shape, F32)] * 4,
        compiler_params=_cp("parallel", "parallel", "parallel"),
    )(*parts, w, m, v)


def _adamw_small(g, w, m, v):
    def body(g_ref, w_ref, m_ref, v_ref, d_ref, nm_ref, nv_ref):
        delta, nm, nv = _adamw(w_ref[...], g_ref[...], m_ref[...], v_ref[...])
        d_ref[...] = delta
        nm_ref[...] = nm
        nv_ref[...] = nv

    return pl.pallas_call(
        body, name="adamw_small",
        out_shape=[jax.ShapeDtypeStruct(g.shape, F32)] * 3,
    )(g, w, m, v)


def _packed_rows(n):
    return -(-n // 1024) * 8


def _pack(arrays, rows):
    pieces = []
    for a in arrays:
        flat = a.reshape(-1).astype(F32)
        nr = _packed_rows(flat.shape[0])
        pieces.append(jnp.pad(flat, (0, nr * 128 - flat.shape[0])).reshape(nr, 128))
    used = sum(p.shape[0] for p in pieces)
    return jnp.concatenate(pieces + [jnp.zeros((rows - used, 128), F32)] * (rows > used), axis=0)


def _unpack(packed, shapes):
    out, row = [], 0
    for s in shapes:
        n = math.prod(s)
        out.append(packed[row:row + _packed_rows(n)].reshape(-1)[:n].reshape(s))
        row += _packed_rows(n)
    return out


def _row(v, width=None):
    v = v.reshape(1, -1)
    return v if width is None else jnp.pad(v, ((0, 0), (0, width - v.shape[1])))


def _layer_fwd(x, h, wts, tables, hosted, last_step):
    proj = _matmul(h, wts["w_in"], tb=True, tm=SEQ, tn=768, tk=1024, name="mm_proj")
    (cat, lse), got = _attn_fwd(proj, *tables, exchanges=hosted["attn"][0])
    hosted["attn"][1](got)
    c_qkv = _dnconv_fwd(proj, wts["dn_conv_w"])
    (cat, states), got = _delta_fwd(c_qkv, proj, wts["dn_a_log"], wts["dn_dt_bias"], wts["dn_norm_w"], cat,
                                    exchanges=hosted["delta"][0])
    hosted["delta"][1](got)
    mix = _matmul(cat, wts["w_out"], tm=512, tn=1024, tk=1024, name="mm_mix")
    x1, h2 = _resnorm_norm_fwd(x, mix, wts["norm_post_mix"], wts["norm_pre_ffn"], "norm_post_mix")
    pre = _matmul(h2, wts["ffn_w_in"], tb=True, tm=SEQ, tn=512, tk=1024, name="mm_ffn_in", out_dtype=BF16)
    act, got = _ffact_fwd(pre, wts["ffn_conv_w"], wts["ffn_conv_b"], exchanges=hosted["ffact"][0])
    hosted["ffact"][1](got)
    f = _matmul(act, wts["ffn_w_out"], tm=512, tn=1024, tk=D_FF, name="mm_ffn_out")
    saved = dict(x=x, h=h, proj=proj, lse=lse, c_qkv=c_qkv, states=states, cat=cat, mix=mix, x1=x1, h2=h2, pre=pre,
                 act=act, f=f)
    return last_step(x1, f), saved


def _layer_bwd(dx2, wts, s, tables, ffact_exchanges=(), delta_exchanges=None, attn_exchanges=None, head=None,
               below=None):
    g = {}
    if head is None:
        head = _norm_bwd(s["f"], wts["norm_post_ffn"], dx2, None, "norm_post_ffn_bwd", BF16)
    df, g["norm_post_ffn"] = head
    dact = _matmul(df, wts["ffn_w_out"], tb=True, tm=SEQ, tn=1408, tk=1024, name="mm_dact", out_dtype=BF16)
    g["ffn_w_out"] = _matmul(s["act"], df, ta=True, tm=1408, tn=512, tk=SEQ, name="mm_dw_ffn_out", out_dtype=BF16)
    (dpre, g["ffn_conv_w"], g["ffn_conv_b"]), got = _ffact_bwd(s["pre"], wts["ffn_conv_w"], wts["ffn_conv_b"], dact,
                                                               exchanges=ffact_exchanges)
    dh2 = _matmul(dpre, wts["ffn_w_in"], tm=512, tn=1024, tk=2 * D_FF, name="mm_dh2")
    g["ffn_w_in"] = _matmul(dpre, s["h2"], ta=True, tm=512, tn=1024, tk=SEQ, name="mm_dw_ffn_in", out_dtype=BF16)
    dx1, dmix, g["norm_pre_ffn"], g["norm_post_mix"] = _norm_bwd_pair(
        s["x1"], wts["norm_pre_ffn"], dh2, dx2, s["mix"], wts["norm_post_mix"], "norm_pre_ffn_bwd")
    dcat = _matmul(dmix, wts["w_out"], tb=True, tm=SEQ, tn=512, tk=1024, name="mm_dcat")
    g["w_out"] = _matmul(s["cat"], dmix, ta=True, tm=1024, tn=512, tk=SEQ, name="mm_dw_out", out_dtype=BF16)
    (dproj, dc, g["dn_a_log"], g["dn_dt_bias"], g["dn_norm_w"]), got = _delta_bwd(
        s["c_qkv"], s["proj"], wts["dn_a_log"], wts["dn_dt_bias"], wts["dn_norm_w"], s["states"], dcat,
        exchanges=delta_exchanges(g, got) if delta_exchanges is not None else ())
    dproj, got = _attn_bwd(s["proj"], *tables, s["cat"], s["lse"], dcat, dproj,
                           exchanges=attn_exchanges(got) if attn_exchanges is not None else ())
    dproj, g["dn_conv_w"] = _dnconv_bwd(s["proj"], wts["dn_conv_w"], dc, dproj)
    dh = _matmul(dproj, wts["w_in"], tm=512, tn=1024, tk=IN_PAD, name="mm_dh")
    g["w_in"] = _matmul(dproj, s["h"], ta=True, tm=768, tn=1024, tk=SEQ, name="mm_dw_in", out_dtype=BF16)
    if below is None:
        dx, g["norm_pre_mix"] = _norm_bwd(s["x"], wts["norm_pre_mix"], dh, dx1, "norm_pre_mix_bwd")
        return dx, g, got, None
    dx, df_below, g["norm_pre_mix"], dw_below = _norm_bwd_pair(s["x"], wts["norm_pre_mix"], dh, dx1, *below,
                                                               "norm_pre_mix_bwd")
    return dx, g, got, (df_below, dw_below)


BIG = ("w_in", "w_out", "ffn_w_in", "ffn_w_out")
COLUMN_SHARDED = ("w_in", "ffn_w_in")
SMALL_SHARDED = ("dn_conv_w", "ffn_conv_w")
REPLICATED = ("dn_a_log", "dn_dt_bias", "dn_norm_w", "ffn_conv_b", "norm_pre_mix", "norm_post_mix", "norm_pre_ffn",
              "norm_post_ffn")
WEIGHTS = ("w_in", "dn_conv_w", "dn_a_log", "dn_dt_bias", "dn_norm_w", "w_out", "ffn_w_in", "ffn_conv_w", "ffn_conv_b",
           "ffn_w_out", "norm_pre_mix", "norm_post_mix", "norm_pre_ffn", "norm_post_ffn")
FULL_SHAPE = dict(dn_conv_w=(DEPTH, 4, 1536), ffn_conv_w=(DEPTH, 3, 2 * D_FF), dn_a_log=(DEPTH, NDH),
                  dn_dt_bias=(DEPTH, NDH), dn_norm_w=(DEPTH, 128), ffn_conv_b=(DEPTH, 2 * D_FF),
                  norm_pre_mix=(DEPTH, D_MODEL), norm_post_mix=(DEPTH, D_MODEL), norm_pre_ffn=(DEPTH, D_MODEL),
                  norm_post_ffn=(DEPTH, D_MODEL))
SMALL_GRAD_ORDER = REPLICATED + SMALL_SHARDED
SMALL_GRAD_ROWS = 544
SMALL_W_ROWS = 56
SMALL_ADAM_ROWS = 232


def _w_in_rows_to_kernel_order(t):
    qkv = t[:QKV_W].reshape(3, N_PAIR, 128, -1).swapaxes(0, 1).reshape(QKV_W, -1)
    return jnp.pad(jnp.concatenate([qkv, t[QKV_W:]], axis=0), ((0, IN_PAD - IN_COLS), (0, 0)))


def _w_in_rows_from_kernel_order(t):
    qkv = t[:QKV_W].reshape(N_PAIR, 3, 128, -1).swapaxes(0, 1).reshape(QKV_W, -1)
    return jnp.concatenate([qkv, t[QKV_W:IN_COLS]], axis=0)


def _interleave_ff_rows(t):
    return t.reshape(2, FF_BLKS, 128, -1).swapaxes(0, 1).reshape(2 * D_FF, -1)


def _deinterleave_ff_rows(t):
    return t.reshape(FF_BLKS, 2, 128, -1).swapaxes(0, 1).reshape(2 * D_FF, -1)


def kernel(x, w_in, dn_conv_w, dn_a_log, dn_dt_bias, dn_norm_w, w_out, ffn_w_in, ffn_conv_w, ffn_conv_b, ffn_w_out, norm_pre_mix, norm_post_mix, norm_pre_ffn, norm_post_ffn, loss_target, m_w_in, m_dn_conv_w, m_dn_a_log, m_dn_dt_bias, m_dn_norm_w, m_w_out, m_ffn_w_in, m_ffn_conv_w, m_ffn_conv_b, m_ffn_w_out, m_norm_pre_mix, m_norm_post_mix, m_norm_pre_ffn, m_norm_post_ffn, v_w_in, v_dn_conv_w, v_dn_a_log, v_dn_dt_bias, v_dn_norm_w, v_w_out, v_ffn_w_in, v_ffn_conv_w, v_ffn_conv_b, v_ffn_w_out, v_norm_pre_mix, v_norm_post_mix, v_norm_pre_ffn, v_norm_post_ffn):
    local = dict(w_in=w_in, dn_conv_w=dn_conv_w, dn_a_log=dn_a_log, dn_dt_bias=dn_dt_bias, dn_norm_w=dn_norm_w,
                 w_out=w_out, ffn_w_in=ffn_w_in, ffn_conv_w=ffn_conv_w, ffn_conv_b=ffn_conv_b, ffn_w_out=ffn_w_out,
                 norm_pre_mix=norm_pre_mix, norm_post_mix=norm_post_mix, norm_pre_ffn=norm_pre_ffn,
                 norm_post_ffn=norm_post_ffn)
    mom_m = dict(w_in=m_w_in, dn_conv_w=m_dn_conv_w, dn_a_log=m_dn_a_log, dn_dt_bias=m_dn_dt_bias,
                 dn_norm_w=m_dn_norm_w, w_out=m_w_out, ffn_w_in=m_ffn_w_in, ffn_conv_w=m_ffn_conv_w,
                 ffn_conv_b=m_ffn_conv_b, ffn_w_out=m_ffn_w_out, norm_pre_mix=m_norm_pre_mix,
                 norm_post_mix=m_norm_post_mix, norm_pre_ffn=m_norm_pre_ffn, norm_post_ffn=m_norm_post_ffn)
    mom_v = dict(w_in=v_w_in, dn_conv_w=v_dn_conv_w, dn_a_log=v_dn_a_log, dn_dt_bias=v_dn_dt_bias,
                 dn_norm_w=v_dn_norm_w, w_out=v_w_out, ffn_w_in=v_ffn_w_in, ffn_conv_w=v_ffn_conv_w,
                 ffn_conv_b=v_ffn_conv_b, ffn_w_out=v_ffn_w_out, norm_pre_mix=v_norm_pre_mix,
                 norm_post_mix=v_norm_post_mix, norm_pre_ffn=v_norm_pre_ffn, norm_post_ffn=v_norm_post_ffn)
    dev = 4 * lax.axis_index("x") + 2 * lax.axis_index("y") + lax.axis_index("c")
    core = lax.axis_index("c").astype(jnp.int32).reshape(1)

    def shard(n, l):
        s = local[n].transpose(0, 2, 1) if n in COLUMN_SHARDED else local[n]
        return s[l].astype(BF16)

    def matrix(n, gathered):
        if n == "w_in":
            return _w_in_rows_to_kernel_order(gathered.reshape(IN_COLS, D_MODEL))
        if n == "ffn_w_in":
            return _interleave_ff_rows(gathered.reshape(2 * D_FF, D_MODEL))
        return gathered.reshape(-1, D_MODEL)

    small_w = _pack([dn_conv_w, ffn_conv_w], SMALL_W_ROWS)
    g_w_in0, g_small = _run_exchange(_gather_exchange([shard("w_in", 0), small_w]), "weights_all_gather")
    n_dn, n_ff = DEPTH * 4 * 192, DEPTH * 3 * 704
    dn_rows = _packed_rows(n_dn)
    sm_dn = g_small[:, :dn_rows].reshape(N_DEV, -1)[:, :n_dn]
    sm_ff = g_small[:, dn_rows:].reshape(N_DEV, -1)[:, :n_ff]
    full_dn_conv = sm_dn.reshape(N_DEV, DEPTH, 4, 192).transpose(1, 2, 0, 3).reshape(DEPTH, 4, 1536)
    full_ff_conv = _interleave_ff(sm_ff.reshape(N_DEV, DEPTH, 3, 704).transpose(1, 2, 0, 3).reshape(DEPTH, 3, 2 * D_FF))

    def small_weights(l):
        wts = dict(dn_conv_w=full_dn_conv[l], ffn_conv_w=full_ff_conv[l], ffn_conv_b=_interleave_ff(_row(ffn_conv_b[l])),
                   dn_a_log=_row(dn_a_log[l], 128), dn_dt_bias=_row(dn_dt_bias[l], 128))
        for n in ("dn_norm_w", "norm_pre_mix", "norm_post_mix", "norm_pre_ffn", "norm_post_ffn"):
            wts[n] = _row(local[n][l])
        return wts

    weights = [small_weights(l) for l in range(DEPTH)]
    weights[0]["w_in"] = matrix("w_in", g_w_in0)

    def gather_behind(wanted):
        def deliver(got):
            for (n, l), g in zip(wanted, got[0]):
                weights[l][n] = matrix(n, g)

        return [_gather_exchange([shard(n, l) for n, l in wanted])], deliver

    nothing = ((), lambda got: None)

    tables = _rope_tables()
    h0 = _norm_fwd(x[0], weights[0]["norm_pre_mix"], "norm_pre_mix")
    (act, h1), saved0 = _layer_fwd(
        x[0], h0, weights[0], tables,
        dict(attn=gather_behind([("ffn_w_in", 0)]), delta=gather_behind([("w_out", 0), ("ffn_w_out", 0)]),
             ffact=gather_behind([("w_in", 1)])),
        lambda x1, f: _resnorm_norm_fwd(x1, f, weights[0]["norm_post_ffn"], weights[1]["norm_pre_mix"], "norm_post_ffn"))
    (loss_part, dact), saved1 = _layer_fwd(
        act, h1, weights[1], tables,
        dict(attn=gather_behind([("ffn_w_in", 1)]), delta=gather_behind([("w_out", 1), ("ffn_w_out", 1)]), ffact=nothing),
        lambda x1, f: _resnorm_loss(x1, f, weights[1]["norm_post_ffn"], loss_target[0]))

    def to_devices(name, t):
        if name == "w_in":
            t = _w_in_rows_from_kernel_order(t)
        if name == "ffn_w_in":
            t = _deinterleave_ff_rows(t)
        return t.reshape(N_DEV, t.shape[0] // N_DEV, t.shape[1])

    def pair_sums(names, layer, to_dev, from_sibling):
        return [_pair_add(gd, r, core, "grads_pair_add_%s_%d" % (n, layer))
                for n, gd, r in zip(names, to_dev, from_sibling)]

    early = ("w_out", "ffn_w_in", "ffn_w_out")
    grads, parts, stash = [None] * DEPTH, {}, {}

    def delta_exchanges1(g, got_ffact):
        stash["early1"] = [to_devices(n, g[n]) for n in early]
        return [_sibling_exchange(stash["early1"])]

    def attn_exchanges1(got_delta):
        return [_chips_exchange(pair_sums(early, 1, stash["early1"], got_delta[0]))]

    dact, grads[1], got_attn, head0 = _layer_bwd(dact, weights[1], saved1, tables, (), delta_exchanges1, attn_exchanges1,
                                                 below=(saved0["f"], weights[0]["norm_post_ffn"]))
    for n, p in zip(early, got_attn[0]):
        parts[n, 1] = p
    w_in1 = [to_devices("w_in", grads[1]["w_in"])]

    def delta_exchanges0(g, got_ffact):
        stash["early0"] = [to_devices(n, g[n]) for n in early]
        return [_chips_exchange(pair_sums(("w_in",), 1, w_in1, got_ffact[0])), _sibling_exchange(stash["early0"])]

    def attn_exchanges0(got_delta):
        parts["w_in", 1], = got_delta[0]
        return [_chips_exchange(pair_sums(early, 0, stash["early0"], got_delta[1]))]

    dact, grads[0], got_attn, _ = _layer_bwd(dact, weights[0], saved0, tables, [_sibling_exchange(w_in1)],
                                             delta_exchanges0, attn_exchanges0, head=head0)
    for n, p in zip(early, got_attn[0]):
        parts[n, 0] = p
    grad_x = dact[None]
    last = [to_devices("w_in", grads[0]["w_in"])]
    from_sibling = _run_exchange(_sibling_exchange(last), "grads_to_sibling")
    parts["w_in", 0], = _run_exchange(_chips_exchange(pair_sums(("w_in",), 0, last, from_sibling)), "grads_to_chips")

    def small_grad(name):
        t = jnp.stack([grads[l][name] for l in range(DEPTH)])
        if name in ("dn_a_log", "dn_dt_bias"):
            t = t[:, 0, :NDH]
        if name in ("ffn_conv_w", "ffn_conv_b"):
            t = _deinterleave_ff(t)
        return t.reshape(FULL_SHAPE[name])

    small_part = _pack([small_grad(n) for n in SMALL_GRAD_ORDER] + [loss_part[0, :1]], SMALL_GRAD_ROWS)
    small_sum = _all_gather_sum_small(small_part)
    small_g = dict(zip(SMALL_GRAD_ORDER + ("loss",), _unpack(small_sum, [FULL_SHAPE[n] for n in SMALL_GRAD_ORDER] + [(1,)])))
    loss = small_g["loss"][0]
    small_g["dn_conv_w"] = lax.dynamic_slice_in_dim(small_g["dn_conv_w"], dev * 192, 192, axis=2)
    small_g["ffn_conv_w"] = lax.dynamic_slice_in_dim(small_g["ffn_conv_w"], dev * 704, 704, axis=2)

    out_g, out_d, out_m, out_v = {}, {}, {}, {}
    for n in BIG:
        turn = (lambda t: t.transpose(0, 2, 1)) if n in COLUMN_SHARDED else (lambda t: t)
        outs = _adamw_sharded([parts[n, l] for l in range(DEPTH)], turn(local[n]), turn(mom_m[n]), turn(mom_v[n]),
                              ADAM_TILE[n], "adamw_" + n)
        out_g[n], out_d[n], out_m[n], out_v[n] = [turn(t) for t in outs]
    shapes = [small_g[n].shape for n in SMALL_GRAD_ORDER]
    d_s, m_s, v_s = _adamw_small(_pack([small_g[n] for n in SMALL_GRAD_ORDER], SMALL_ADAM_ROWS),
                                 _pack([local[n] for n in SMALL_GRAD_ORDER], SMALL_ADAM_ROWS),
                                 _pack([mom_m[n] for n in SMALL_GRAD_ORDER], SMALL_ADAM_ROWS),
                                 _pack([mom_v[n] for n in SMALL_GRAD_ORDER], SMALL_ADAM_ROWS))
    for n, d, m, v in zip(SMALL_GRAD_ORDER, _unpack(d_s, shapes), _unpack(m_s, shapes), _unpack(v_s, shapes)):
        out_g[n], out_d[n], out_m[n], out_v[n] = small_g[n], d, m, v
    return (loss, grad_x, *[out_g[n] for n in WEIGHTS], *[out_d[n] for n in WEIGHTS],
            *[out_m[n] for n in WEIGHTS], *[out_v[n] for n in WEIGHTS])
```

```python
import functools
import math

import jax
import jax.numpy as jnp
from jax import lax
from jax.experimental import pallas as pl
from jax.experimental.pallas import tpu as pltpu

F32 = jnp.float32
BF16 = jnp.bfloat16
MESH = pl.DeviceIdType.MESH

N_DEV = 8
SEQ = 2048
D_MODEL = 1024
DEPTH = 2
N_PAIR = 4
HEAD_DIM = 64
ATTN_W = 512
ATTN_BLK = 128
DILATIONS = (1, 4, 16)
SEGMENT_BLOCKS = (16, 4, 1)
N_BLK = SEQ // ATTN_BLK
NDH = 4
CH = 64
NCH = SEQ // CH
IN_COLS = 3592
IN_PAD = 3840
QKV_W = 3 * ATTN_W
DN_QKV_BLK0 = QKV_W // 128
DN_QKV_BLKS = 1536 // 128
DN_Z_COL = 3072
DN_TAIL_BLK = 3584 // 128
D_FF = 2816
FF_BLKS = D_FF // 128
EPS = 1e-6
NEG = -1e30
ROPE_THETA = 10000.0

ADAM_LR, ADAM_B1, ADAM_B2, ADAM_EPS, ADAM_WD, ADAM_STEP = 0.001, 0.9, 0.999, 1e-08, 0.01, 10

VMEM_LIMIT = 56 * 1024 * 1024


def _cp(*sem):
    return pltpu.CompilerParams(dimension_semantics=sem, vmem_limit_bytes=VMEM_LIMIT)


class Exchange:
    def __init__(self, operands, out_shapes, sems, start, middle, finish):
        self.operands, self.out_shapes, self.sems = list(operands), list(out_shapes), list(sems)
        self.start, self.middle, self.finish = start, middle, finish


HBM_SPEC = pl.BlockSpec(memory_space=pltpu.HBM)


def _hosted_call(body, *, name, steps, in_specs, out_specs, out_shape, scratch_shapes, operands, exchanges=(),
                 aliases=None):
    n_in, n_out, n_scr = len(in_specs), len(out_specs), len(scratch_shapes)

    def take(refs, pos, counts):
        groups = []
        for c in counts:
            groups.append(refs[pos:pos + c])
            pos += c
        return groups, pos

    def full_body(*refs):
        ins, pos = refs[:n_in], n_in
        ex_ins, pos = take(refs, pos, [len(e.operands) for e in exchanges])
        outs, pos = refs[pos:pos + n_out], pos + n_out
        ex_outs, pos = take(refs, pos, [len(e.out_shapes) for e in exchanges])
        scr, pos = refs[pos:pos + n_scr], pos + n_scr
        ex_sems, pos = take(refs, pos, [len(e.sems) for e in exchanges])
        step = pl.program_id(0)
        for e, a, b, s in zip(exchanges, ex_ins, ex_outs, ex_sems):
            pl.when(step == 0)(functools.partial(e.start, a, b, s))
            if e.middle is not None:
                pl.when(step == (3 * steps) // 4)(functools.partial(e.middle, a, b, s))
        body(*ins, *outs, *scr)
        for e, a, b, s in zip(exchanges, ex_ins, ex_outs, ex_sems):
            pl.when(step == steps - 1)(functools.partial(e.finish, a, b, s))

    n_ex_in = sum(len(e.operands) for e in exchanges)
    n_ex_out = sum(len(e.out_shapes) for e in exchanges)
    results = pl.pallas_call(
        full_body, name=name, grid=(steps,),
        in_specs=list(in_specs) + [HBM_SPEC] * n_ex_in,
        out_specs=list(out_specs) + [HBM_SPEC] * n_ex_out,
        out_shape=list(out_shape) + [s for e in exchanges for s in e.out_shapes],
        scratch_shapes=list(scratch_shapes) + [s for e in exchanges for s in e.sems],
        input_output_aliases=aliases or {},
        compiler_params=_cp("arbitrary"),
    )(*operands, *[a for e in exchanges for a in e.operands])
    ex_results, _ = take(results, n_out, [len(e.out_shapes) for e in exchanges])
    return results[:n_out], ex_results


def _dot(a, b, dims, precision=None):
    if precision is None:
        a = a.astype(BF16)
        b = b.astype(BF16)
    return lax.dot_general(a, b, (dims, ((), ())), preferred_element_type=F32, precision=precision)


def _make_mm(precision):
    @jax.custom_vjp
    def nn(a, b):
        return _dot(a, b, ((1,), (0,)), precision)

    @jax.custom_vjp
    def nt(a, b):
        return _dot(a, b, ((1,), (1,)), precision)

    @jax.custom_vjp
    def tn(a, b):
        return _dot(a, b, ((0,), (0,)), precision)

    nn.defvjp(lambda a, b: (nn(a, b), (a, b)), lambda r, g: (nt(g, r[1]), tn(r[0], g)))
    nt.defvjp(lambda a, b: (nt(a, b), (a, b)), lambda r, g: (nn(g, r[1]), tn(g, r[0])))
    tn.defvjp(lambda a, b: (tn(a, b), (a, b)), lambda r, g: (nt(r[1], g), nn(r[0], g)))
    return nn, nt, tn


def _matmul(a, b, *, ta=False, tb=False, tm, tn, tk, name, out_dtype=F32):
    (k_dim, m_dim) = a.shape if ta else a.shape[::-1]
    (n_dim, k2) = b.shape if tb else b.shape[::-1]
    assert k_dim == k2 and m_dim % tm == 0 and n_dim % tn == 0 and k_dim % tk == 0, (a.shape, b.shape, tm, tn, tk)
    nk = k_dim // tk
    dims = ((0 if ta else 1,), (1 if tb else 0,))

    def body(a_ref, b_ref, o_ref, *acc):
        p = _dot(a_ref[...], b_ref[...], dims)
        if nk == 1:
            o_ref[...] = p.astype(out_dtype)
            return
        acc_ref, k = acc[0], pl.program_id(2)

        @pl.when(k == 0)
        def _():
            acc_ref[...] = p

        @pl.when(k > 0)
        def _():
            acc_ref[...] += p

        @pl.when(k == nk - 1)
        def _():
            o_ref[...] = acc_ref[...].astype(out_dtype)

    a_spec = pl.BlockSpec((tk, tm), lambda i, j, k: (k, i)) if ta else pl.BlockSpec((tm, tk), lambda i, j, k: (i, k))
    b_spec = pl.BlockSpec((tn, tk), lambda i, j, k: (j, k)) if tb else pl.BlockSpec((tk, tn), lambda i, j, k: (k, j))
    return pl.pallas_call(
        body, name=name,
        grid=(m_dim // tm, n_dim // tn, nk),
        in_specs=[a_spec, b_spec],
        out_specs=pl.BlockSpec((tm, tn), lambda i, j, k: (i, j)),
        out_shape=jax.ShapeDtypeStruct((m_dim, n_dim), out_dtype),
        scratch_shapes=[pltpu.VMEM((tm, tn), F32)] if nk > 1 else [],
        compiler_params=_cp("parallel", "parallel", "arbitrary"),
    )(a, b)


NORM_ROWS = 256


def _streamed_rows():
    return pl.BlockSpec((NORM_ROWS, D_MODEL), lambda i: (i, 0))


RING = 3


def _rms(x, w):
    return x * lax.rsqrt(jnp.mean(x * x, axis=1, keepdims=True) + EPS) * w


def _norm_fwd(x, w_row, name, out_dtype=BF16):
    def body(x_ref, w_ref, o_ref):
        o_ref[...] = _rms(x_ref[...], w_ref[...]).astype(out_dtype)

    return pl.pallas_call(
        body, name=name, grid=(SEQ // NORM_ROWS,),
        in_specs=[pl.BlockSpec((NORM_ROWS, D_MODEL), lambda i: (i, 0)), pl.BlockSpec((1, D_MODEL), lambda i: (0, 0))],
        out_specs=pl.BlockSpec((NORM_ROWS, D_MODEL), lambda i: (i, 0)),
        out_shape=jax.ShapeDtypeStruct((SEQ, D_MODEL), out_dtype),
        compiler_params=_cp("parallel"),
    )(x, w_row)


def _resnorm_norm_fwd(x, f, w_row, next_w_row, name):
    def body(x_ref, f_ref, w_ref, nw_ref, o_ref, h_ref):
        out = x_ref[...] + _rms(f_ref[...], w_ref[...])
        o_ref[...] = out
        h_ref[...] = _rms(out, nw_ref[...]).astype(BF16)

    blk = pl.BlockSpec((NORM_ROWS, D_MODEL), lambda i: (i, 0))
    row = pl.BlockSpec((1, D_MODEL), lambda i: (0, 0))
    return pl.pallas_call(
        body, name=name, grid=(SEQ // NORM_ROWS,),
        in_specs=[_streamed_rows(), _streamed_rows(), row, row],
        out_specs=[blk, blk],
        out_shape=[jax.ShapeDtypeStruct((SEQ, D_MODEL), F32), jax.ShapeDtypeStruct((SEQ, D_MODEL), BF16)],
        compiler_params=_cp("parallel"),
    )(x, f, w_row, next_w_row)


def _norm_bwd(x, w_row, dy, add, name, dx_dtype=F32):
    has_add = add is not None

    def body(*refs):
        if has_add:
            x_ref, w_ref, dy_ref, add_ref, dx_ref, dw_ref = refs
        else:
            x_ref, w_ref, dy_ref, dx_ref, dw_ref = refs
        _, vjp = jax.vjp(_rms, x_ref[...], w_ref[...])
        dx, dw = vjp(dy_ref[...])
        dx_ref[...] = (dx + add_ref[...] if has_add else dx).astype(dx_dtype)

        @pl.when(pl.program_id(0) == 0)
        def _():
            dw_ref[...] = jnp.zeros_like(dw_ref)

        dw_ref[...] += dw

    blk = pl.BlockSpec((NORM_ROWS, D_MODEL), lambda i: (i, 0))
    row = pl.BlockSpec((1, D_MODEL), lambda i: (0, 0))
    ins = [x, w_row, dy] + ([add] if has_add else [])
    return pl.pallas_call(
        body, name=name, grid=(SEQ // NORM_ROWS,),
        in_specs=[_streamed_rows(), row, _streamed_rows()] + ([_streamed_rows()] if has_add else []),
        out_specs=[blk, row],
        out_shape=[jax.ShapeDtypeStruct((SEQ, D_MODEL), dx_dtype), jax.ShapeDtypeStruct((1, D_MODEL), F32)],
        compiler_params=_cp("arbitrary"),
    )(*ins)


def _norm_bwd_pair(x_a, w_a, dy_a, add, x_b, w_b, name):
    steps = SEQ // NORM_ROWS

    def body(xa_hbm, wa_ref, dya_hbm, add_hbm, xb_hbm, wb_ref, dxa_ref, dxb_ref, dwa_ref, dwb_ref, ring, sems):
        streams = (xa_hbm, dya_hbm, add_hbm, xb_hbm)
        i = pl.program_id(0)

        def fetch(step, k):
            rows = pl.ds(pl.multiple_of(step * NORM_ROWS, NORM_ROWS), NORM_ROWS)
            return pltpu.make_async_copy(streams[k].at[rows, :], ring.at[k, step % RING], sems.at[k, step % RING])

        @pl.when(i == 0)
        def _():
            for step in range(RING - 1):
                for k in range(len(streams)):
                    fetch(step, k).start()

        @pl.when(i + RING - 1 < steps)
        def _():
            for k in range(len(streams)):
                fetch(i + RING - 1, k).start()

        for k in range(len(streams)):
            fetch(i, k).wait()
        slot = i % RING
        _, vjp_a = jax.vjp(_rms, ring[0, slot], wa_ref[...])
        dxa, dwa = vjp_a(ring[1, slot])
        dxa = dxa + ring[2, slot]
        _, vjp_b = jax.vjp(_rms, ring[3, slot], wb_ref[...])
        dxb, dwb = vjp_b(dxa)
        dxa_ref[...] = dxa
        dxb_ref[...] = dxb.astype(BF16)

        @pl.when(pl.program_id(0) == 0)
        def _():
            dwa_ref[...] = jnp.zeros_like(dwa_ref)
            dwb_ref[...] = jnp.zeros_like(dwb_ref)

        dwa_ref[...] += dwa
        dwb_ref[...] += dwb

    blk = pl.BlockSpec((NORM_ROWS, D_MODEL), lambda i: (i, 0))
    row = pl.BlockSpec((1, D_MODEL), lambda i: (0, 0))
    hbm = pl.BlockSpec(memory_space=pl.ANY)
    return pl.pallas_call(
        body, name=name, grid=(steps,),
        in_specs=[hbm, row, hbm, hbm, hbm, row],
        out_specs=[blk, blk, row, row],
        out_shape=[jax.ShapeDtypeStruct((SEQ, D_MODEL), F32), jax.ShapeDtypeStruct((SEQ, D_MODEL), BF16),
                   jax.ShapeDtypeStruct((1, D_MODEL), F32), jax.ShapeDtypeStruct((1, D_MODEL), F32)],
        scratch_shapes=[pltpu.VMEM((4, RING, NORM_ROWS, D_MODEL), F32), pltpu.SemaphoreType.DMA((4, RING))],
        compiler_params=_cp("arbitrary"),
    )(x_a, w_a, dy_a, add, x_b, w_b)


def _resnorm_loss(x, f, w_row, target):
    def body(x_ref, f_ref, w_ref, t_ref, loss_ref, dy_ref):
        err = x_ref[...] + _rms(f_ref[...], w_ref[...]) - t_ref[...]
        dy_ref[...] = err * (1.0 / D_MODEL)

        @pl.when(pl.program_id(0) == 0)
        def _():
            loss_ref[...] = jnp.zeros_like(loss_ref)

        part = jnp.sum(jnp.sum(err * err, axis=1, keepdims=True) * (1.0 / D_MODEL), axis=0, keepdims=True)
        loss_ref[...] += 0.5 * jnp.broadcast_to(part, loss_ref.shape)

    blk = pl.BlockSpec((NORM_ROWS, D_MODEL), lambda i: (i, 0))
    return pl.pallas_call(
        body, name="norm_post_ffn_loss", grid=(SEQ // NORM_ROWS,),
        in_specs=[_streamed_rows(), _streamed_rows(), pl.BlockSpec((1, D_MODEL), lambda i: (0, 0)), _streamed_rows()],
        out_specs=[pl.BlockSpec((1, 128), lambda i: (0, 0)), blk],
        out_shape=[jax.ShapeDtypeStruct((1, 128), F32), jax.ShapeDtypeStruct((SEQ, D_MODEL), F32)],
        compiler_params=_cp("arbitrary"),
    )(x, f, w_row, target)


def _make_shift(j):
    def down(x):
        row = lax.broadcasted_iota(jnp.int32, x.shape, 0)
        return jnp.where(row >= j, pltpu.roll(x, j, 0), 0.0)

    def up(x):
        n = x.shape[0]
        row = lax.broadcasted_iota(jnp.int32, x.shape, 0)
        return jnp.where(row < n - j, pltpu.roll(x, n - j, 0), 0.0)

    f = jax.custom_vjp(down)
    f.defvjp(lambda x: (down(x), None), lambda _, g: (up(g),))
    return f


_SHIFT = {j: _make_shift(j) for j in (1, 2, 3)}


def _causal_conv(x, taps):
    n = len(taps)
    acc = x * taps[n - 1]
    for k in range(n - 1):
        acc = acc + _SHIFT[n - 1 - k](x) * taps[k]
    return acc


def _tap_rows(w_ref, lanes=slice(None)):
    return tuple(w_ref[k:k + 1, lanes] for k in range(w_ref.shape[0]))


def _sigmoid(x):
    return 1.0 / (1.0 + jnp.exp(-x))


def _silu(x):
    return x * _sigmoid(x)


def _softplus(x):
    return jnp.maximum(x, 0.0) + jnp.log(1.0 + jnp.exp(-jnp.abs(x)))


def _gelu_tanh(x):
    return 0.5 * x * (1.0 + jnp.tanh(math.sqrt(2.0 / math.pi) * (x + 0.044715 * (x * x * x))))


def _dnconv_fn(x, taps):
    return _silu(_causal_conv(x, taps))


def _dnconv_fwd(proj, conv_w):
    def body(x_ref, w_ref, o_ref):
        o_ref[...] = _dnconv_fn(x_ref[...], _tap_rows(w_ref)).astype(BF16)

    return pl.pallas_call(
        body, name="dnconv_fwd", grid=(DN_QKV_BLKS,),
        in_specs=[pl.BlockSpec((SEQ, 128), lambda j: (0, DN_QKV_BLK0 + j)), pl.BlockSpec((4, 128), lambda j: (0, j))],
        out_specs=pl.BlockSpec((SEQ, 128), lambda j: (0, j)),
        out_shape=jax.ShapeDtypeStruct((SEQ, 1536), BF16),
        compiler_params=_cp("parallel"),
    )(proj, conv_w)


def _dnconv_bwd(proj, conv_w, dc, dproj):
    def body(x_ref, w_ref, dc_ref, _, dx_ref, dw_ref):
        _, vjp = jax.vjp(_dnconv_fn, x_ref[...], _tap_rows(w_ref))
        dx, dw = vjp(dc_ref[...])
        dx_ref[...] = dx.astype(BF16)
        for k, row in enumerate(dw):
            dw_ref[k:k + 1, :] = row

    return pl.pallas_call(
        body, name="dnconv_bwd", grid=(DN_QKV_BLKS,),
        in_specs=[pl.BlockSpec((SEQ, 128), lambda j: (0, DN_QKV_BLK0 + j)), pl.BlockSpec((4, 128), lambda j: (0, j)),
                  pl.BlockSpec((SEQ, 128), lambda j: (0, j)), pl.BlockSpec(memory_space=pl.ANY)],
        out_specs=[pl.BlockSpec((SEQ, 128), lambda j: (0, DN_QKV_BLK0 + j)), pl.BlockSpec((4, 128), lambda j: (0, j))],
        out_shape=[jax.ShapeDtypeStruct((SEQ, IN_PAD), BF16), jax.ShapeDtypeStruct((4, 1536), F32)],
        input_output_aliases={3: 0},
        compiler_params=_cp("parallel"),
    )(proj, conv_w, dc, dproj)


def _ffact_fn(pg, pu, wg, wu, bg, bu):
    return _gelu_tanh(_causal_conv(pg, wg) + bg) * (_causal_conv(pu, wu) + bu)


def _ffact_args(p_ref, w_ref, b_ref):
    g, u = slice(0, 128), slice(128, 256)
    return (p_ref[:, g].astype(F32), p_ref[:, u].astype(F32), _tap_rows(w_ref, g), _tap_rows(w_ref, u),
            b_ref[:, g], b_ref[:, u])


def _ffact_fwd(pre, conv_w, conv_b, exchanges=()):
    def body(p_ref, w_ref, b_ref, o_ref):
        o_ref[...] = _ffact_fn(*_ffact_args(p_ref, w_ref, b_ref)).astype(BF16)

    (act,), results = _hosted_call(
        body, name="ffact_fwd", steps=FF_BLKS,
        in_specs=[pl.BlockSpec((SEQ, 256), lambda j: (0, j)), pl.BlockSpec((3, 256), lambda j: (0, j)),
                  pl.BlockSpec((1, 256), lambda j: (0, j))],
        out_specs=[pl.BlockSpec((SEQ, 128), lambda j: (0, j))],
        out_shape=[jax.ShapeDtypeStruct((SEQ, D_FF), BF16)],
        scratch_shapes=[], operands=(pre, conv_w, conv_b), exchanges=exchanges)
    return act, results


def _ffact_bwd(pre, conv_w, conv_b, dact, exchanges=()):
    def body(p_ref, w_ref, b_ref, da_ref, dp_ref, dw_ref, db_ref):
        _, vjp = jax.vjp(_ffact_fn, *_ffact_args(p_ref, w_ref, b_ref))
        dpg, dpu, dwg, dwu, dbg, dbu = vjp(da_ref[...].astype(F32))
        dp_ref[:, 0:128] = dpg.astype(BF16)
        dp_ref[:, 128:256] = dpu.astype(BF16)
        for k in range(3):
            dw_ref[k:k + 1, 0:128] = dwg[k]
            dw_ref[k:k + 1, 128:256] = dwu[k]
        db_ref[:, 0:128] = dbg
        db_ref[:, 128:256] = dbu

    return _hosted_call(
        body, name="ffact_bwd", steps=FF_BLKS,
        in_specs=[pl.BlockSpec((SEQ, 256), lambda j: (0, j)), pl.BlockSpec((3, 256), lambda j: (0, j)),
                  pl.BlockSpec((1, 256), lambda j: (0, j)), pl.BlockSpec((SEQ, 128), lambda j: (0, j))],
        out_specs=[pl.BlockSpec((SEQ, 256), lambda j: (0, j)), pl.BlockSpec((3, 256), lambda j: (0, j)),
                   pl.BlockSpec((1, 256), lambda j: (0, j))],
        out_shape=[jax.ShapeDtypeStruct((SEQ, 2 * D_FF), BF16), jax.ShapeDtypeStruct((3, 2 * D_FF), F32),
                   jax.ShapeDtypeStruct((1, 2 * D_FF), F32)],
        scratch_shapes=[], operands=(pre, conv_w, conv_b, dact), exchanges=exchanges)


def _interleave_ff(t):
    lead = t.shape[:-1]
    return t.reshape(lead + (2, FF_BLKS, 128)).swapaxes(-3, -2).reshape(lead + (2 * D_FF,))


def _deinterleave_ff(t):
    lead = t.shape[:-1]
    return t.reshape(lead + (FF_BLKS, 2, 128)).swapaxes(-3, -2).reshape(lead + (2 * D_FF,))


def _rope_tables():
    inv = 1.0 / (ROPE_THETA ** (jnp.arange(0, HEAD_DIM, 2, dtype=F32) / HEAD_DIM))
    ang = jnp.arange(SEQ, dtype=F32)[:, None] * inv[None, :]
    cos = jnp.tile(jnp.cos(ang), (1, 4))
    sin = jnp.tile(jnp.sin(ang), (1, 4))
    sign = jnp.where((jnp.arange(128) % HEAD_DIM) < HEAD_DIM // 2, -1.0, 1.0).astype(F32)
    return cos, sin * sign[None, :]


def _rope(x, cos, sin_signed):
    lane = lax.broadcasted_iota(jnp.int32, x.shape, 1)
    partner = jnp.where((lane % HEAD_DIM) < HEAD_DIM // 2, pltpu.roll(x, 128 - HEAD_DIM // 2, 1),
                        pltpu.roll(x, HEAD_DIM // 2, 1))
    return x * cos + partner * sin_signed


def _head_masks():
    lane = lax.broadcasted_iota(jnp.int32, (1, 128), 1)
    return [(lane // HEAD_DIM) == h for h in range(2)]


def _both_heads(x):
    return jnp.concatenate([jnp.where(hm, x, 0.0)[None] for hm in _head_masks()], axis=0)


def _block_keys(branch, k_s, v_s, rows, prows, has_prev):
    a = lax.broadcasted_iota(jnp.int32, (ATTN_BLK, ATTN_BLK), 0)
    c = lax.broadcasted_iota(jnp.int32, (ATTN_BLK, ATTN_BLK), 1)
    keys, values, mask = k_s[rows, :], v_s[rows, :], c <= a
    if SEGMENT_BLOCKS[branch] > 1:
        keys = jnp.concatenate([k_s[prows, :], keys], axis=0)
        values = jnp.concatenate([v_s[prows, :], values], axis=0)
        mask = jnp.concatenate([(c >= a) & has_prev, mask], axis=1)
    twice = lambda t: jnp.broadcast_to(t[None], (2,) + t.shape)
    return twice(keys), twice(values), mask


def _block_rows(branch, t):
    d, per_seg = DILATIONS[branch], SEGMENT_BLOCKS[branch]
    if d == 1:
        start = pl.multiple_of(t * ATTN_BLK, ATTN_BLK)
        prev = pl.multiple_of(jnp.maximum(t - 1, 0) * ATTN_BLK, ATTN_BLK)
        return pl.ds(start, ATTN_BLK), pl.ds(prev, ATTN_BLK), t > 0
    r, n = t // per_seg, t % per_seg
    start = n * (ATTN_BLK * d) + r
    prev = jnp.maximum(n - 1, 0) * (ATTN_BLK * d) + r
    return pl.ds(start, ATTN_BLK, stride=d), pl.ds(prev, ATTN_BLK, stride=d), n > 0


def _attn_fwd(proj, cos, sin_signed, exchanges=()):
    scale = HEAD_DIM ** -0.5

    def body(qkv_ref, cos_ref, sin_ref, out_ref, lse_ref, q_s, k_s, v_s, *branch_s):
        o_s, l_s = branch_s[:3], branch_s[3:]
        q_s[...] = _rope(qkv_ref[:, 0:128], cos_ref[...], sin_ref[...])
        k_s[...] = _rope(qkv_ref[:, 128:256], cos_ref[...], sin_ref[...])
        v_s[...] = qkv_ref[:, 256:384]
        heads = _head_masks()
        for branch in range(3):
            def block(t, carry, branch=branch):
                rows, prows, has_prev = _block_rows(branch, t)
                keys, values, mask = _block_keys(branch, k_s, v_s, rows, prows, has_prev)
                s = jnp.where(mask, BMM_NT(_both_heads(q_s[rows, :]), keys) * scale, NEG)
                m = jnp.max(s, axis=2, keepdims=True)
                e = jnp.exp(s - m)
                l = jnp.sum(e, axis=2, keepdims=True)
                o = BMM(e, values) / l
                lse_b = m + jnp.log(l)
                o_s[branch][rows, :] = jnp.where(heads[0], o[0], o[1])
                l_s[branch][rows, :] = jnp.where(heads[0], lse_b[0], lse_b[1])
                return carry

            lax.fori_loop(0, N_BLK, block, 0, unroll=8)
        l0, l1, l2 = l_s[0][...], l_s[1][...], l_s[2][...]
        m = jnp.maximum(jnp.maximum(l0, l1), l2)
        w0, w1, w2 = jnp.exp(l0 - m), jnp.exp(l1 - m), jnp.exp(l2 - m)
        den = w0 + w1 + w2
        out_ref[...] = (w0 * o_s[0][...] + w1 * o_s[1][...] + w2 * o_s[2][...]) / den
        lse_ref[...] = m + jnp.log(den)

    tab = pl.BlockSpec((SEQ, 128), lambda j: (0, 0))
    col = pl.BlockSpec((SEQ, 128), lambda j: (0, j))
    return _hosted_call(
        body, name="attn_fwd", steps=N_PAIR,
        in_specs=[pl.BlockSpec((SEQ, 384), lambda j: (0, j)), tab, tab],
        out_specs=[col, col],
        out_shape=[jax.ShapeDtypeStruct((SEQ, 2 * ATTN_W), F32), jax.ShapeDtypeStruct((SEQ, ATTN_W), F32)],
        scratch_shapes=[pltpu.VMEM((SEQ, 128), F32)] * 9,
        operands=(proj, cos, sin_signed), exchanges=exchanges)


def _attn_bwd(proj, cos, sin_signed, cat, lse, dcat, dproj, exchanges=()):
    scale = HEAD_DIM ** -0.5

    def body(qkv_ref, cos_ref, sin_ref, out_ref, lse_ref, do_ref, _, dqkv_ref, q_s, k_s, v_s, dq_s, dk_s, dv_s,
             dod_s):
        q_s[...] = _rope(qkv_ref[:, 0:128], cos_ref[...], sin_ref[...])
        k_s[...] = _rope(qkv_ref[:, 128:256], cos_ref[...], sin_ref[...])
        v_s[...] = qkv_ref[:, 256:384]
        dq_s[...] = jnp.zeros_like(dq_s)
        dk_s[...] = jnp.zeros_like(dk_s)
        dv_s[...] = jnp.zeros_like(dv_s)
        dod_s[...] = do_ref[...] * out_ref[...]
        heads = _head_masks()
        for branch in range(3):
            def block(t, carry, branch=branch):
                rows, prows, has_prev = _block_rows(branch, t)
                keys, values, mask = _block_keys(branch, k_s, v_s, rows, prows, has_prev)
                q2, do2 = _both_heads(q_s[rows, :]), _both_heads(do_ref[rows, :])
                lse_b, dod = lse_ref[rows, :], dod_s[rows, :]
                lse2 = jnp.concatenate(
                    [jnp.max(jnp.where(hm, lse_b, NEG), axis=1, keepdims=True)[None] for hm in heads], axis=0)
                delta = jnp.concatenate(
                    [jnp.sum(jnp.where(hm, dod, 0.0), axis=1, keepdims=True)[None] for hm in heads], axis=0)
                p = jnp.exp(jnp.where(mask, BMM_NT(q2, keys) * scale, NEG) - lse2)
                ds = p * (BMM_NT(do2, values) - delta) * scale
                dq = BMM(ds, keys)
                dk = BMM_TN(ds, q2)
                dv = BMM_TN(p, do2)
                dk, dv = dk[0] + dk[1], dv[0] + dv[1]
                dq_s[rows, :] += jnp.where(heads[0], dq[0], dq[1])
                if SEGMENT_BLOCKS[branch] > 1:
                    dk_s[rows, :] += dk[ATTN_BLK:]
                    dv_s[rows, :] += dv[ATTN_BLK:]

                    @pl.when(has_prev)
                    def _():
                        dk_s[prows, :] += dk[:ATTN_BLK]
                        dv_s[prows, :] += dv[:ATTN_BLK]
                else:
                    dk_s[rows, :] += dk
                    dv_s[rows, :] += dv
                return carry

            lax.fori_loop(0, N_BLK, block, 0, unroll=4)
        dqkv_ref[:, 0:128] = _rope(dq_s[...], cos_ref[...], -sin_ref[...]).astype(BF16)
        dqkv_ref[:, 128:256] = _rope(dk_s[...], cos_ref[...], -sin_ref[...]).astype(BF16)
        dqkv_ref[:, 256:384] = dv_s[...].astype(BF16)

    tab = pl.BlockSpec((SEQ, 128), lambda j: (0, 0))
    col = pl.BlockSpec((SEQ, 128), lambda j: (0, j))
    qkv = pl.BlockSpec((SEQ, 384), lambda j: (0, j))
    (dproj,), results = _hosted_call(
        body, name="attn_bwd", steps=N_PAIR,
        in_specs=[qkv, tab, tab, col, col, col, pl.BlockSpec(memory_space=pl.ANY)],
        out_specs=[qkv],
        out_shape=[jax.ShapeDtypeStruct((SEQ, IN_PAD), BF16)],
        scratch_shapes=[pltpu.VMEM((SEQ, 128), F32)] * 7,
        operands=(proj, cos, sin_signed, cat, lse, dcat, dproj), exchanges=exchanges, aliases={6: 0})
    return dproj, results


def _bdot(a, b, dims, precision=None):
    if precision is None:
        a = a.astype(BF16)
        b = b.astype(BF16)
    return lax.dot_general(a, b, (dims, ((0,), (0,))), preferred_element_type=F32, precision=precision)


def _make_bmm(precision):
    @jax.custom_vjp
    def nn(a, b):
        return _bdot(a, b, ((2,), (1,)), precision)

    @jax.custom_vjp
    def nt(a, b):
        return _bdot(a, b, ((2,), (2,)), precision)

    @jax.custom_vjp
    def tn(a, b):
        return _bdot(a, b, ((1,), (1,)), precision)

    nn.defvjp(lambda a, b: (nn(a, b), (a, b)), lambda r, g: (nt(g, r[1]), tn(r[0], g)))
    nt.defvjp(lambda a, b: (nt(a, b), (a, b)), lambda r, g: (nn(g, r[1]), tn(g, r[0])))
    tn.defvjp(lambda a, b: (tn(a, b), (a, b)), lambda r, g: (nt(r[1], g), nn(r[0], g)))
    return nn, nt, tn


BMM, BMM_NT, BMM_TN = _make_bmm(None)
BMM3, BMM3_NT, BMM3_TN = _make_bmm(lax.Precision.HIGH)
MM3, _, _ = _make_mm(lax.Precision.HIGH)


def _head_lanes(t, off):
    lane = lax.broadcasted_iota(jnp.int32, (1, 128), 1)
    return jnp.concatenate(
        [jnp.sum(t * (lane == off + h).astype(F32), axis=1, keepdims=True)[None] for h in range(NDH)], axis=0)


@jax.custom_vjp
def _unit_lower_inverse(a_mat):
    c = a_mat.shape[1]
    eye = (lax.broadcasted_iota(jnp.int32, (c, c), 0) == lax.broadcasted_iota(jnp.int32, (c, c), 1)).astype(F32)
    power = -a_mat
    t_inv = eye + power
    for _ in range(5):
        power = BMM3(power, power)
        t_inv = t_inv + BMM3(t_inv, power)
    return t_inv


def _unit_lower_inverse_fwd(a_mat):
    t_inv = _unit_lower_inverse(a_mat)
    return t_inv, t_inv


def _unit_lower_inverse_bwd(t_inv, d_inv):
    return (-BMM3_NT(BMM3_TN(t_inv, d_inv), t_inv),)


_unit_lower_inverse.defvjp(_unit_lower_inverse_fwd, _unit_lower_inverse_bwd)


DN_STEP_CHUNKS = 4
DN_STEP_ROWS = DN_STEP_CHUNKS * CH
DN_STEPS = NCH // DN_STEP_CHUNKS
DN_BATCH = DN_STEP_CHUNKS * NDH


def _delta_chunks(qr, kr, vr, z, tail, alog_row, dt_row, nw, state):
    c = qr.shape[1]
    tails = [tail[CH * n:CH * (n + 1)] for n in range(DN_STEP_CHUNKS)]
    per_chunk = lambda t: jnp.concatenate([t] * DN_STEP_CHUNKS, axis=0)
    beta = _sigmoid(jnp.concatenate([_head_lanes(t, 0) for t in tails], axis=0))
    a_raw = jnp.concatenate([_head_lanes(t, NDH) for t in tails], axis=0)
    g = -jnp.exp(per_chunk(_head_lanes(alog_row, 0))) * _softplus(a_raw + per_chunk(_head_lanes(dt_row, 0)))

    q = qr * lax.rsqrt(jnp.sum(qr * qr, axis=2, keepdims=True) + EPS) * (128 ** -0.5)
    k = kr * lax.rsqrt(jnp.sum(kr * kr, axis=2, keepdims=True) + EPS)

    ri = lax.broadcasted_iota(jnp.int32, (c, c), 0)
    ci = lax.broadcasted_iota(jnp.int32, (c, c), 1)
    tril = ri >= ci
    lane = lax.broadcasted_iota(jnp.int32, (1, 128), 1)
    pick = [(lane == b).astype(F32) for b in range(DN_BATCH)]
    g_lanes = sum(g[b] * pick[b] for b in range(DN_BATCH))
    g_sums = MM3(tril.astype(F32), g_lanes)
    gc = jnp.concatenate([jnp.sum(g_sums * pick[b], axis=1, keepdims=True)[None] for b in range(DN_BATCH)],
                         axis=0)
    g_row = jnp.swapaxes(jnp.broadcast_to(gc, (DN_BATCH, c, c)), 1, 2)
    decay = jnp.where(tril, jnp.exp(jnp.where(tril, gc - g_row, 0.0)), 0.0)
    kb = k * beta
    t_inv = _unit_lower_inverse(jnp.where(ri > ci, BMM_NT(kb, k) * decay, 0.0))
    eg = jnp.exp(gc)
    u = BMM(t_inv, vr * beta)
    w = BMM(t_inv, kb * eg)
    qk = BMM_NT(q, k) * decay
    g_tot = jnp.sum(g, axis=1, keepdims=True)
    q_dec = q * eg
    k_dec = k * jnp.exp(g_tot - gc)
    outs = []
    for n in range(DN_STEP_CHUNKS):
        heads = slice(NDH * n, NDH * (n + 1))
        v_new = u[heads] - BMM(w[heads], state)
        outs.append(BMM(q_dec[heads], state) + BMM(qk[heads], v_new))
        state = state * jnp.exp(g_tot[heads]) + BMM_TN(k_dec[heads], v_new)
    o = jnp.concatenate(outs, axis=0)
    on = o * lax.rsqrt(jnp.mean(o * o, axis=2, keepdims=True) + EPS) * nw
    return on * _silu(z), state


def _heads(v, off=0):
    return jnp.concatenate([v[None, CH * n:CH * (n + 1), off + 128 * h:off + 128 * (h + 1)]
                            for n in range(DN_STEP_CHUNKS) for h in range(NDH)], axis=0)


def _unheads(t):
    return jnp.concatenate([jnp.concatenate([t[NDH * n + h] for h in range(NDH)], axis=1)
                            for n in range(DN_STEP_CHUNKS)], axis=0)


def _delta_fwd(c_qkv, proj, alog_row, dt_row, nw, cat, exchanges=()):
    def body(c_ref, z_ref, tail_ref, al_ref, dt_ref, nw_ref, _, y_ref, st_ref, state):
        @pl.when(pl.program_id(0) == 0)
        def _():
            state[...] = jnp.zeros_like(state)

        cv = c_ref[...].astype(F32)
        st_ref[0] = state[...]
        y, new_state = _delta_chunks(_heads(cv), _heads(cv, 512), _heads(cv, 1024), _heads(z_ref[...]), tail_ref[...],
                                     al_ref[...], dt_ref[...], nw_ref[...], state[...])
        y_ref[...] = _unheads(y)
        state[...] = new_state

    row = pl.BlockSpec((1, 128), lambda n: (0, 0))
    rows = DN_STEP_ROWS
    return _hosted_call(
        body, name="delta_fwd", steps=DN_STEPS,
        in_specs=[pl.BlockSpec((rows, 1536), lambda n: (n, 0)), pl.BlockSpec((rows, 512), lambda n: (n, DN_Z_COL // 512)),
                  pl.BlockSpec((rows, 128), lambda n: (n, DN_TAIL_BLK)), row, row, row, pl.BlockSpec(memory_space=pl.ANY)],
        out_specs=[pl.BlockSpec((rows, 512), lambda n: (n, 1)),
                   pl.BlockSpec((1, NDH, 128, 128), lambda n: (n, 0, 0, 0))],
        out_shape=[jax.ShapeDtypeStruct((SEQ, 2 * ATTN_W), F32), jax.ShapeDtypeStruct((DN_STEPS, NDH, 128, 128), F32)],
        scratch_shapes=[pltpu.VMEM((NDH, 128, 128), F32)],
        operands=(c_qkv, proj, proj, alog_row, dt_row, nw, cat), exchanges=exchanges, aliases={6: 0})


def _delta_bwd(c_qkv, proj, alog_row, dt_row, nw, states, dcat, exchanges=()):
    def body(c_ref, z_ref, tail_ref, al_ref, dt_ref, nw_ref, st_ref, dy_ref,
             dp_ref, dc_ref, dal_ref, ddt_ref, dnw_ref, dstate):
        @pl.when(pl.program_id(0) == 0)
        def _():
            dstate[...] = jnp.zeros_like(dstate)
            dal_ref[...] = jnp.zeros_like(dal_ref)
            ddt_ref[...] = jnp.zeros_like(ddt_ref)
            dnw_ref[...] = jnp.zeros_like(dnw_ref)

        cv = c_ref[...].astype(F32)
        _, vjp = jax.vjp(_delta_chunks, _heads(cv), _heads(cv, 512), _heads(cv, 1024), _heads(z_ref[...]),
                         tail_ref[...], al_ref[...], dt_ref[...], nw_ref[...], st_ref[0])
        dq, dk, dv, dz, dtail, dal, ddt, dnw, dst = vjp((_heads(dy_ref[...]), dstate[...]))
        dstate[...] = dst
        dc_ref[...] = jnp.concatenate([_unheads(dq), _unheads(dk), _unheads(dv)], axis=1)
        dp_ref[...] = jnp.concatenate([_unheads(dz), dtail, jnp.zeros((DN_STEP_ROWS, 128), F32)], axis=1).astype(BF16)
        dal_ref[...] += dal
        ddt_ref[...] += ddt
        dnw_ref[...] += dnw

    rev = lambda n: DN_STEPS - 1 - n
    row = pl.BlockSpec((1, 128), lambda n: (0, 0))
    rows = DN_STEP_ROWS
    return _hosted_call(
        body, name="delta_bwd", steps=DN_STEPS,
        in_specs=[pl.BlockSpec((rows, 1536), lambda n: (rev(n), 0)),
                  pl.BlockSpec((rows, 512), lambda n: (rev(n), DN_Z_COL // 512)),
                  pl.BlockSpec((rows, 128), lambda n: (rev(n), DN_TAIL_BLK)), row, row, row,
                  pl.BlockSpec((1, NDH, 128, 128), lambda n: (rev(n), 0, 0, 0)),
                  pl.BlockSpec((rows, 512), lambda n: (rev(n), 1))],
        out_specs=[pl.BlockSpec((rows, 768), lambda n: (rev(n), DN_Z_COL // 768)),
                   pl.BlockSpec((rows, 1536), lambda n: (rev(n), 0)), row, row, row],
        out_shape=[jax.ShapeDtypeStruct((SEQ, IN_PAD), BF16), jax.ShapeDtypeStruct((SEQ, 1536), F32)]
        + [jax.ShapeDtypeStruct((1, 128), F32)] * 3,
        scratch_shapes=[pltpu.VMEM((NDH, 128, 128), F32)],
        operands=(c_qkv, proj, proj, alog_row, dt_row, nw, states, dcat), exchanges=exchanges)


def _place():
    x, y, c = lax.axis_index("x"), lax.axis_index("y"), lax.axis_index("c")
    other_chips = [(1 - x, y), (x, 1 - y), (1 - x, 1 - y)]
    return x, y, c, other_chips


def _gather_exchange(shards):
    n = len(shards)

    def copies(ins, outs, sems):
        send_sems, recv_sems, local_sems = sems
        x, y, c, chips = _place()
        me, sibling = (x, y, c), (x, y, 1 - c)

        def copy(b, k, block, to, src=None):
            slot = outs[b].at[4 * block[0] + 2 * block[1] + block[2]]
            return pltpu.make_async_remote_copy(
                src_ref=slot if src is None else src, dst_ref=slot,
                send_sem=send_sems.at[b, k], recv_sem=recv_sems.at[b, k], device_id=to, device_id_type=MESH)

        mine = [pltpu.make_async_copy(ins[b], outs[b].at[4 * x + 2 * y + c], local_sems.at[b]) for b in range(n)]
        first = []
        for b in range(n):
            first.append(copy(b, 0, me, sibling, src=ins[b]))
            first += [copy(b, 1 + j, me, (*chip, c), src=ins[b]) for j, chip in enumerate(chips)]
        over_ici = [copy(b, 1 + j, (*chip, c), me) for b in range(n) for j, chip in enumerate(chips)]
        passed = [copy(b, 4 + j, (*chip, c), sibling) for b in range(n) for j, chip in enumerate(chips)]
        from_sibling = []
        for b in range(n):
            from_sibling.append(copy(b, 0, sibling, me))
            from_sibling += [copy(b, 4 + j, (*chip, 1 - c), me) for j, chip in enumerate(chips)]
        return mine, first, over_ici, passed, from_sibling

    def start(ins, outs, sems):
        mine, first, _, _, _ = copies(ins, outs, sems)
        for cp in mine + first:
            cp.start()

    def middle(ins, outs, sems):
        _, _, over_ici, passed, _ = copies(ins, outs, sems)
        for arrived, onward in zip(over_ici, passed):
            arrived.wait_recv()
            onward.start()

    def finish(ins, outs, sems):
        mine, first, _, passed, from_sibling = copies(ins, outs, sems)
        for cp in from_sibling:
            cp.wait_recv()
        for cp in first + passed:
            cp.wait_send()
        for cp in mine:
            cp.wait()

    return Exchange(shards, [jax.ShapeDtypeStruct((N_DEV,) + s.shape, s.dtype) for s in shards],
                    [pltpu.SemaphoreType.DMA((n, 7)), pltpu.SemaphoreType.DMA((n, 7)), pltpu.SemaphoreType.DMA((n,))],
                    start, middle, finish)


def _sibling_exchange(gs):
    n = len(gs)

    def copies(ins, outs, sems):
        send_sems, recv_sems = sems
        x, y, c, _ = _place()
        return [pltpu.make_async_remote_copy(
            src_ref=ins[b].at[2 * p + (1 - c)], dst_ref=outs[b].at[p],
            send_sem=send_sems.at[b, p], recv_sem=recv_sems.at[b, p],
            device_id=(x, y, 1 - c), device_id_type=MESH) for b in range(n) for p in range(4)]

    def start(ins, outs, sems):
        for cp in copies(ins, outs, sems):
            cp.start()

    def finish(ins, outs, sems):
        for cp in copies(ins, outs, sems):
            cp.wait()

    return Exchange(gs, [jax.ShapeDtypeStruct((4,) + g.shape[1:], g.dtype) for g in gs],
                    [pltpu.SemaphoreType.DMA((n, 4)), pltpu.SemaphoreType.DMA((n, 4))], start, None, finish)


def _chips_exchange(hs):
    n = len(hs)

    def copies(ins, outs, sems):
        send_sems, recv_sems, local_sems = sems
        x, y, c, chips = _place()
        my_chip = 2 * x + y
        local = [pltpu.make_async_copy(ins[b].at[my_chip], outs[b].at[my_chip], local_sems.at[b]) for b in range(n)]
        sends, arrivals = [], []
        for b in range(n):
            for k, (px, py) in enumerate(chips):
                peer = 2 * px + py
                sends.append(pltpu.make_async_remote_copy(
                    src_ref=ins[b].at[peer], dst_ref=outs[b].at[my_chip],
                    send_sem=send_sems.at[b, k], recv_sem=recv_sems.at[b, k],
                    device_id=(px, py, c), device_id_type=MESH))
                arrivals.append(pltpu.make_async_remote_copy(
                    src_ref=ins[b].at[peer], dst_ref=outs[b].at[peer],
                    send_sem=send_sems.at[b, k], recv_sem=recv_sems.at[b, k],
                    device_id=(px, py, c), device_id_type=MESH))
        return local, sends, arrivals

    def start(ins, outs, sems):
        local, sends, _ = copies(ins, outs, sems)
        for cp in local + sends:
            cp.start()

    def finish(ins, outs, sems):
        local, sends, arrivals = copies(ins, outs, sems)
        for cp in arrivals:
            cp.wait_recv()
        for cp in sends:
            cp.wait_send()
        for cp in local:
            cp.wait()

    return Exchange(hs, [jax.ShapeDtypeStruct(h.shape, h.dtype) for h in hs],
                    [pltpu.SemaphoreType.DMA((n, 3)), pltpu.SemaphoreType.DMA((n, 3)), pltpu.SemaphoreType.DMA((n,))],
                    start, None, finish)


def _run_exchange(exchange, name):
    n_in, n_out = len(exchange.operands), len(exchange.out_shapes)

    def body(*refs):
        ins, outs, sems = refs[:n_in], refs[n_in:n_in + n_out], refs[n_in + n_out:]
        exchange.start(ins, outs, sems)
        if exchange.middle is not None:
            exchange.middle(ins, outs, sems)
        exchange.finish(ins, outs, sems)

    return pl.pallas_call(
        body, name=name,
        in_specs=[HBM_SPEC] * n_in, out_specs=[HBM_SPEC] * n_out,
        out_shape=exchange.out_shapes, scratch_shapes=exchange.sems,
    )(*exchange.operands)


def _pair_add(g, r, core, name):
    _, nr, nc = g.shape
    tr = nr // 2 if nr % 32 == 0 else nr

    def body(core_ref, g_ref, r_ref, o_ref):
        o_ref[...] = (g_ref[...].astype(F32) + r_ref[...].astype(F32)).astype(BF16)

    return pl.pallas_call(
        body, name=name,
        grid_spec=pltpu.PrefetchScalarGridSpec(
            num_scalar_prefetch=1, grid=(4, nr // tr),
            in_specs=[pl.BlockSpec((1, tr, nc), lambda p, i, core: (2 * p + core[0], i, 0)),
                      pl.BlockSpec((1, tr, nc), lambda p, i, core: (p, i, 0))],
            out_specs=pl.BlockSpec((1, tr, nc), lambda p, i, core: (p, i, 0))),
        out_shape=jax.ShapeDtypeStruct(r.shape, BF16),
        compiler_params=_cp("parallel", "parallel"),
    )(core, g, r)


def _all_gather_sum_small(v):
    rows = v.shape[0]

    def body(x_ref, sum_ref, out_ref, send_sems, recv_sems, local_sem):
        x, y, c, chips = _place()
        me, sibling = (x, y, c), (x, y, 1 - c)

        def block(px, py, pc):
            return out_ref.at[pl.ds((4 * px + 2 * py + pc) * rows, rows), :]

        def copy(k, blk, to, src=None):
            return pltpu.make_async_remote_copy(
                src_ref=block(*blk) if src is None else src, dst_ref=block(*blk),
                send_sem=send_sems.at[k], recv_sem=recv_sems.at[k], device_id=to, device_id_type=MESH)

        mine = pltpu.make_async_copy(x_ref, block(*me), local_sem)
        mine.start()
        first = [copy(0, me, sibling, src=x_ref)]
        first += [copy(1 + j, me, (*chip, c), src=x_ref) for j, chip in enumerate(chips)]
        for cp in first:
            cp.start()
        passed = [copy(4 + j, (*chip, c), sibling) for j, chip in enumerate(chips)]
        for j, chip in enumerate(chips):
            copy(1 + j, (*chip, c), me).wait_recv()
            passed[j].start()
        copy(0, sibling, me).wait_recv()
        for j, chip in enumerate(chips):
            copy(4 + j, (*chip, 1 - c), me).wait_recv()
        for cp in first + passed:
            cp.wait_send()
        mine.wait()
        total = out_ref[pl.ds(0, rows), :]
        for d in range(1, N_DEV):
            total = total + out_ref[pl.ds(d * rows, rows), :]
        sum_ref[...] = total

    vm = pl.BlockSpec(memory_space=pltpu.VMEM)
    return pl.pallas_call(
        body, name="small_all_reduce",
        in_specs=[vm], out_specs=[vm],
        out_shape=[jax.ShapeDtypeStruct((rows, 128), F32)],
        scratch_shapes=[pltpu.VMEM((N_DEV * rows, 128), F32), pltpu.SemaphoreType.DMA((7,)),
                        pltpu.SemaphoreType.DMA((7,)), pltpu.SemaphoreType.DMA],
    )(v)[0]


def _adamw(w, g, m, v):
    m = ADAM_B1 * m + (1.0 - ADAM_B1) * g
    v = ADAM_B2 * v + (1.0 - ADAM_B2) * (g * g)
    m_hat = m / (1.0 - ADAM_B1 ** ADAM_STEP)
    v_hat = v / (1.0 - ADAM_B2 ** ADAM_STEP)
    delta = -ADAM_LR * (m_hat / (jnp.sqrt(v_hat) + ADAM_EPS) + ADAM_WD * w)
    return delta, m, v


ADAM_TILE = dict(w_in=(IN_COLS // N_DEV, 256), w_out=(128, D_MODEL), ffn_w_in=(176, D_MODEL), ffn_w_out=(176, D_MODEL))


def _sum_chips(p):
    p = p.astype(F32)
    return (p[0] + p[1]) + (p[2] + p[3])


def _adamw_sharded(parts, w, m, v, tile, name):
    nl, nr, nc = w.shape
    tr, tc = tile

    def body(*refs):
        p_refs, (w_ref, m_ref, v_ref, g_ref, d_ref, nm_ref, nv_ref) = refs[:nl], refs[nl:]
        layer = pl.program_id(0)
        p = p_refs[0][...]
        for l in range(1, nl):
            p = jnp.where(layer == l, p_refs[l][...], p)
        g = _sum_chips(p)
        delta, nm, nv = _adamw(w_ref[0], g, m_ref[0], v_ref[0])
        g_ref[0] = g
        d_ref[0] = delta
        nm_ref[0] = nm
        nv_ref[0] = nv

    blk = pl.BlockSpec((1, tr, tc), lambda l, i, j: (l, i, j))
    return pl.pallas_call(
        body, name=name, grid=(nl, nr // tr, nc // tc),
        in_specs=[pl.BlockSpec((4, tr, tc), lambda l, i, j, own=own: (0, jnp.where(l == own, i, 0), j))
                  for own in range(nl)] + [blk, blk, blk],
        out_specs=[blk] * 4,
        out_shape=[jax.ShapeDtypeStruct(w.shape, F32)] * 4,
        compiler_params=_cp("parallel", "parallel", "parallel"),
    )(*parts, w, m, v)


def _adamw_small(g, w, m, v):
    def body(g_ref, w_ref, m_ref, v_ref, d_ref, nm_ref, nv_ref):
        delta, nm, nv = _adamw(w_ref[...], g_ref[...], m_ref[...], v_ref[...])
        d_ref[...] = delta
        nm_ref[...] = nm
        nv_ref[...] = nv

    return pl.pallas_call(
        body, name="adamw_small",
        out_shape=[jax.ShapeDtypeStruct(g.shape, F32)] * 3,
    )(g, w, m, v)


def _packed_rows(n):
    return -(-n // 1024) * 8


def _pack(arrays, rows):
    pieces = []
    for a in arrays:
        flat = a.reshape(-1).astype(F32)
        nr = _packed_rows(flat.shape[0])
        pieces.append(jnp.pad(flat, (0, nr * 128 - flat.shape[0])).reshape(nr, 128))
    used = sum(p.shape[0] for p in pieces)
    return jnp.concatenate(pieces + [jnp.zeros((rows - used, 128), F32)] * (rows > used), axis=0)


def _unpack(packed, shapes):
    out, row = [], 0
    for s in shapes:
        n = math.prod(s)
        out.append(packed[row:row + _packed_rows(n)].reshape(-1)[:n].reshape(s))
        row += _packed_rows(n)
    return out


def _row(v, width=None):
    v = v.reshape(1, -1)
    return v if width is None else jnp.pad(v, ((0, 0), (0, width - v.shape[1])))


def _layer_fwd(x, h, wts, tables, hosted, last_step):
    proj = _matmul(h, wts["w_in"], tb=True, tm=SEQ, tn=768, tk=1024, name="mm_proj")
    (cat, lse), got = _attn_fwd(proj, *tables, exchanges=hosted["attn"][0])
    hosted["attn"][1](got)
    c_qkv = _dnconv_fwd(proj, wts["dn_conv_w"])
    (cat, states), got = _delta_fwd(c_qkv, proj, wts["dn_a_log"], wts["dn_dt_bias"], wts["dn_norm_w"], cat,
                                    exchanges=hosted["delta"][0])
    hosted["delta"][1](got)
    mix = _matmul(cat, wts["w_out"], tm=512, tn=1024, tk=1024, name="mm_mix")
    x1, h2 = _resnorm_norm_fwd(x, mix, wts["norm_post_mix"], wts["norm_pre_ffn"], "norm_post_mix")
    pre = _matmul(h2, wts["ffn_w_in"], tb=True, tm=SEQ, tn=512, tk=1024, name="mm_ffn_in", out_dtype=BF16)
    act, got = _ffact_fwd(pre, wts["ffn_conv_w"], wts["ffn_conv_b"], exchanges=hosted["ffact"][0])
    hosted["ffact"][1](got)
    f = _matmul(act, wts["ffn_w_out"], tm=512, tn=1024, tk=D_FF, name="mm_ffn_out")
    saved = dict(x=x, h=h, proj=proj, lse=lse, c_qkv=c_qkv, states=states, cat=cat, mix=mix, x1=x1, h2=h2, pre=pre,
                 act=act, f=f)
    return last_step(x1, f), saved


def _layer_bwd(dx2, wts, s, tables, ffact_exchanges=(), delta_exchanges=None, attn_exchanges=None, head=None,
               below=None):
    g = {}
    if head is None:
        head = _norm_bwd(s["f"], wts["norm_post_ffn"], dx2, None, "norm_post_ffn_bwd", BF16)
    df, g["norm_post_ffn"] = head
    dact = _matmul(df, wts["ffn_w_out"], tb=True, tm=SEQ, tn=1408, tk=1024, name="mm_dact", out_dtype=BF16)
    g["ffn_w_out"] = _matmul(s["act"], df, ta=True, tm=1408, tn=512, tk=SEQ, name="mm_dw_ffn_out", out_dtype=BF16)
    (dpre, g["ffn_conv_w"], g["ffn_conv_b"]), got = _ffact_bwd(s["pre"], wts["ffn_conv_w"], wts["ffn_conv_b"], dact,
                                                               exchanges=ffact_exchanges)
    dh2 = _matmul(dpre, wts["ffn_w_in"], tm=512, tn=1024, tk=2 * D_FF, name="mm_dh2")
    g["ffn_w_in"] = _matmul(dpre, s["h2"], ta=True, tm=512, tn=1024, tk=SEQ, name="mm_dw_ffn_in", out_dtype=BF16)
    dx1, dmix, g["norm_pre_ffn"], g["norm_post_mix"] = _norm_bwd_pair(
        s["x1"], wts["norm_pre_ffn"], dh2, dx2, s["mix"], wts["norm_post_mix"], "norm_pre_ffn_bwd")
    dcat = _matmul(dmix, wts["w_out"], tb=True, tm=SEQ, tn=512, tk=1024, name="mm_dcat")
    g["w_out"] = _matmul(s["cat"], dmix, ta=True, tm=1024, tn=512, tk=SEQ, name="mm_dw_out", out_dtype=BF16)
    (dproj, dc, g["dn_a_log"], g["dn_dt_bias"], g["dn_norm_w"]), got = _delta_bwd(
        s["c_qkv"], s["proj"], wts["dn_a_log"], wts["dn_dt_bias"], wts["dn_norm_w"], s["states"], dcat,
        exchanges=delta_exchanges(g, got) if delta_exchanges is not None else ())
    dproj, got = _attn_bwd(s["proj"], *tables, s["cat"], s["lse"], dcat, dproj,
                           exchanges=attn_exchanges(got) if attn_exchanges is not None else ())
    dproj, g["dn_conv_w"] = _dnconv_bwd(s["proj"], wts["dn_conv_w"], dc, dproj)
    dh = _matmul(dproj, wts["w_in"], tm=512, tn=1024, tk=IN_PAD, name="mm_dh")
    g["w_in"] = _matmul(dproj, s["h"], ta=True, tm=768, tn=1024, tk=SEQ, name="mm_dw_in", out_dtype=BF16)
    if below is None:
        dx, g["norm_pre_mix"] = _norm_bwd(s["x"], wts["norm_pre_mix"], dh, dx1, "norm_pre_mix_bwd")
        return dx, g, got, None
    dx, df_below, g["norm_pre_mix"], dw_below = _norm_bwd_pair(s["x"], wts["norm_pre_mix"], dh, dx1, *below,
                                                               "norm_pre_mix_bwd")
    return dx, g, got, (df_below, dw_below)


BIG = ("w_in", "w_out", "ffn_w_in", "ffn_w_out")
COLUMN_SHARDED = ("w_in", "ffn_w_in")
SMALL_SHARDED = ("dn_conv_w", "ffn_conv_w")
REPLICATED = ("dn_a_log", "dn_dt_bias", "dn_norm_w", "ffn_conv_b", "norm_pre_mix", "norm_post_mix", "norm_pre_ffn",
              "norm_post_ffn")
WEIGHTS = ("w_in", "dn_conv_w", "dn_a_log", "dn_dt_bias", "dn_norm_w", "w_out", "ffn_w_in", "ffn_conv_w", "ffn_conv_b",
           "ffn_w_out", "norm_pre_mix", "norm_post_mix", "norm_pre_ffn", "norm_post_ffn")
FULL_SHAPE = dict(dn_conv_w=(DEPTH, 4, 1536), ffn_conv_w=(DEPTH, 3, 2 * D_FF), dn_a_log=(DEPTH, NDH),
                  dn_dt_bias=(DEPTH, NDH), dn_norm_w=(DEPTH, 128), ffn_conv_b=(DEPTH, 2 * D_FF),
                  norm_pre_mix=(DEPTH, D_MODEL), norm_post_mix=(DEPTH, D_MODEL), norm_pre_ffn=(DEPTH, D_MODEL),
                  norm_post_ffn=(DEPTH, D_MODEL))
SMALL_GRAD_ORDER = REPLICATED + SMALL_SHARDED
SMALL_GRAD_ROWS = 544
SMALL_W_ROWS = 56
SMALL_ADAM_ROWS = 232


def _w_in_rows_to_kernel_order(t):
    qkv = t[:QKV_W].reshape(3, N_PAIR, 128, -1).swapaxes(0, 1).reshape(QKV_W, -1)
    return jnp.pad(jnp.concatenate([qkv, t[QKV_W:]], axis=0), ((0, IN_PAD - IN_COLS), (0, 0)))


def _w_in_rows_from_kernel_order(t):
    qkv = t[:QKV_W].reshape(N_PAIR, 3, 128, -1).swapaxes(0, 1).reshape(QKV_W, -1)
    return jnp.concatenate([qkv, t[QKV_W:IN_COLS]], axis=0)


def _interleave_ff_rows(t):
    return t.reshape(2, FF_BLKS, 128, -1).swapaxes(0, 1).reshape(2 * D_FF, -1)


def _deinterleave_ff_rows(t):
    return t.reshape(FF_BLKS, 2, 128, -1).swapaxes(0, 1).reshape(2 * D_FF, -1)


def kernel(x, w_in, dn_conv_w, dn_a_log, dn_dt_bias, dn_norm_w, w_out, ffn_w_in, ffn_conv_w, ffn_conv_b, ffn_w_out, norm_pre_mix, norm_post_mix, norm_pre_ffn, norm_post_ffn, loss_target, m_w_in, m_dn_conv_w, m_dn_a_log, m_dn_dt_bias, m_dn_norm_w, m_w_out, m_ffn_w_in, m_ffn_conv_w, m_ffn_conv_b, m_ffn_w_out, m_norm_pre_mix, m_norm_post_mix, m_norm_pre_ffn, m_norm_post_ffn, v_w_in, v_dn_conv_w, v_dn_a_log, v_dn_dt_bias, v_dn_norm_w, v_w_out, v_ffn_w_in, v_ffn_conv_w, v_ffn_conv_b, v_ffn_w_out, v_norm_pre_mix, v_norm_post_mix, v_norm_pre_ffn, v_norm_post_ffn):
    local = dict(w_in=w_in, dn_conv_w=dn_conv_w, dn_a_log=dn_a_log, dn_dt_bias=dn_dt_bias, dn_norm_w=dn_norm_w,
                 w_out=w_out, ffn_w_in=ffn_w_in, ffn_conv_w=ffn_conv_w, ffn_conv_b=ffn_conv_b, ffn_w_out=ffn_w_out,
                 norm_pre_mix=norm_pre_mix, norm_post_mix=norm_post_mix, norm_pre_ffn=norm_pre_ffn,
                 norm_post_ffn=norm_post_ffn)
    mom_m = dict(w_in=m_w_in, dn_conv_w=m_dn_conv_w, dn_a_log=m_dn_a_log, dn_dt_bias=m_dn_dt_bias,
                 dn_norm_w=m_dn_norm_w, w_out=m_w_out, ffn_w_in=m_ffn_w_in, ffn_conv_w=m_ffn_conv_w,
                 ffn_conv_b=m_ffn_conv_b, ffn_w_out=m_ffn_w_out, norm_pre_mix=m_norm_pre_mix,
                 norm_post_mix=m_norm_post_mix, norm_pre_ffn=m_norm_pre_ffn, norm_post_ffn=m_norm_post_ffn)
    mom_v = dict(w_in=v_w_in, dn_conv_w=v_dn_conv_w, dn_a_log=v_dn_a_log, dn_dt_bias=v_dn_dt_bias,
                 dn_norm_w=v_dn_norm_w, w_out=v_w_out, ffn_w_in=v_ffn_w_in, ffn_conv_w=v_ffn_conv_w,
                 ffn_conv_b=v_ffn_conv_b, ffn_w_out=v_ffn_w_out, norm_pre_mix=v_norm_pre_mix,
                 norm_post_mix=v_norm_post_mix, norm_pre_ffn=v_norm_pre_ffn, norm_post_ffn=v_norm_post_ffn)
    dev = 4 * lax.axis_index("x") + 2 * lax.axis_index("y") + lax.axis_index("c")
    core = lax.axis_index("c").astype(jnp.int32).reshape(1)

    def shard(n, l):
        s = local[n].transpose(0, 2, 1) if n in COLUMN_SHARDED else local[n]
        return s[l].astype(BF16)

    def matrix(n, gathered):
        if n == "w_in":
            return _w_in_rows_to_kernel_order(gathered.reshape(IN_COLS, D_MODEL))
        if n == "ffn_w_in":
            return _interleave_ff_rows(gathered.reshape(2 * D_FF, D_MODEL))
        return gathered.reshape(-1, D_MODEL)

    small_w = _pack([dn_conv_w, ffn_conv_w], SMALL_W_ROWS)
    g_w_in0, g_small = _run_exchange(_gather_exchange([shard("w_in", 0), small_w]), "weights_all_gather")
    n_dn, n_ff = DEPTH * 4 * 192, DEPTH * 3 * 704
    dn_rows = _packed_rows(n_dn)
    sm_dn = g_small[:, :dn_rows].reshape(N_DEV, -1)[:, :n_dn]
    sm_ff = g_small[:, dn_rows:].reshape(N_DEV, -1)[:, :n_ff]
    full_dn_conv = sm_dn.reshape(N_DEV, DEPTH, 4, 192).transpose(1, 2, 0, 3).reshape(DEPTH, 4, 1536)
    full_ff_conv = _interleave_ff(sm_ff.reshape(N_DEV, DEPTH, 3, 704).transpose(1, 2, 0, 3).reshape(DEPTH, 3, 2 * D_FF))

    def small_weights(l):
        wts = dict(dn_conv_w=full_dn_conv[l], ffn_conv_w=full_ff_conv[l], ffn_conv_b=_interleave_ff(_row(ffn_conv_b[l])),
                   dn_a_log=_row(dn_a_log[l], 128), dn_dt_bias=_row(dn_dt_bias[l], 128))
        for n in ("dn_norm_w", "norm_pre_mix", "norm_post_mix", "norm_pre_ffn", "norm_post_ffn"):
            wts[n] = _row(local[n][l])
        return wts

    weights = [small_weights(l) for l in range(DEPTH)]
    weights[0]["w_in"] = matrix("w_in", g_w_in0)

    def gather_behind(wanted):
        def deliver(got):
            for (n, l), g in zip(wanted, got[0]):
                weights[l][n] = matrix(n, g)

        return [_gather_exchange([shard(n, l) for n, l in wanted])], deliver

    nothing = ((), lambda got: None)

    tables = _rope_tables()
    h0 = _norm_fwd(x[0], weights[0]["norm_pre_mix"], "norm_pre_mix")
    (act, h1), saved0 = _layer_fwd(
        x[0], h0, weights[0], tables,
        dict(attn=gather_behind([("ffn_w_in", 0)]), delta=gather_behind([("w_out", 0), ("ffn_w_out", 0)]),
             ffact=gather_behind([("w_in", 1)])),
        lambda x1, f: _resnorm_norm_fwd(x1, f, weights[0]["norm_post_ffn"], weights[1]["norm_pre_mix"], "norm_post_ffn"))
    (loss_part, dact), saved1 = _layer_fwd(
        act, h1, weights[1], tables,
        dict(attn=gather_behind([("ffn_w_in", 1)]), delta=gather_behind([("w_out", 1), ("ffn_w_out", 1)]), ffact=nothing),
        lambda x1, f: _resnorm_loss(x1, f, weights[1]["norm_post_ffn"], loss_target[0]))

    def to_devices(name, t):
        if name == "w_in":
            t = _w_in_rows_from_kernel_order(t)
        if name == "ffn_w_in":
            t = _deinterleave_ff_rows(t)
        return t.reshape(N_DEV, t.shape[0] // N_DEV, t.shape[1])

    def pair_sums(names, layer, to_dev, from_sibling):
        return [_pair_add(gd, r, core, "grads_pair_add_%s_%d" % (n, layer))
                for n, gd, r in zip(names, to_dev, from_sibling)]

    early = ("w_out", "ffn_w_in", "ffn_w_out")
    grads, parts, stash = [None] * DEPTH, {}, {}

    def delta_exchanges1(g, got_ffact):
        stash["early1"] = [to_devices(n, g[n]) for n in early]
        return [_sibling_exchange(stash["early1"])]

    def attn_exchanges1(got_delta):
        return [_chips_exchange(pair_sums(early, 1, stash["early1"], got_delta[0]))]

    dact, grads[1], got_attn, head0 = _layer_bwd(dact, weights[1], saved1, tables, (), delta_exchanges1, attn_exchanges1,
                                                 below=(saved0["f"], weights[0]["norm_post_ffn"]))
    for n, p in zip(early, got_attn[0]):
        parts[n, 1] = p
    w_in1 = [to_devices("w_in", grads[1]["w_in"])]

    def delta_exchanges0(g, got_ffact):
        stash["early0"] = [to_devices(n, g[n]) for n in early]
        return [_chips_exchange(pair_sums(("w_in",), 1, w_in1, got_ffact[0])), _sibling_exchange(stash["early0"])]

    def attn_exchanges0(got_delta):
        parts["w_in", 1], = got_delta[0]
        return [_chips_exchange(pair_sums(early, 0, stash["early0"], got_delta[1]))]

    dact, grads[0], got_attn, _ = _layer_bwd(dact, weights[0], saved0, tables, [_sibling_exchange(w_in1)],
                                             delta_exchanges0, attn_exchanges0, head=head0)
    for n, p in zip(early, got_attn[0]):
        parts[n, 0] = p
    grad_x = dact[None]
    last = [to_devices("w_in", grads[0]["w_in"])]
    from_sibling = _run_exchange(_sibling_exchange(last), "grads_to_sibling")
    parts["w_in", 0], = _run_exchange(_chips_exchange(pair_sums(("w_in",), 0, last, from_sibling)), "grads_to_chips")

    def small_grad(name):
        t = jnp.stack([grads[l][name] for l in range(DEPTH)])
        if name in ("dn_a_log", "dn_dt_bias"):
            t = t[:, 0, :NDH]
        if name in ("ffn_conv_w", "ffn_conv_b"):
            t = _deinterleave_ff(t)
        return t.reshape(FULL_SHAPE[name])

    small_part = _pack([small_grad(n) for n in SMALL_GRAD_ORDER] + [loss_part[0, :1]], SMALL_GRAD_ROWS)
    small_sum = _all_gather_sum_small(small_part)
    small_g = dict(zip(SMALL_GRAD_ORDER + ("loss",), _unpack(small_sum, [FULL_SHAPE[n] for n in SMALL_GRAD_ORDER] + [(1,)])))
    loss = small_g["loss"][0]
    small_g["dn_conv_w"] = lax.dynamic_slice_in_dim(small_g["dn_conv_w"], dev * 192, 192, axis=2)
    small_g["ffn_conv_w"] = lax.dynamic_slice_in_dim(small_g["ffn_conv_w"], dev * 704, 704, axis=2)

    out_g, out_d, out_m, out_v = {}, {}, {}, {}
    for n in BIG:
        turn = (lambda t: t.transpose(0, 2, 1)) if n in COLUMN_SHARDED else (lambda t: t)
        outs = _adamw_sharded([parts[n, l] for l in range(DEPTH)], turn(local[n]), turn(mom_m[n]), turn(mom_v[n]),
                              ADAM_TILE[n], "adamw_" + n)
        out_g[n], out_d[n], out_m[n], out_v[n] = [turn(t) for t in outs]
    shapes = [small_g[n].shape for n in SMALL_GRAD_ORDER]
    d_s, m_s, v_s = _adamw_small(_pack([small_g[n] for n in SMALL_GRAD_ORDER], SMALL_ADAM_ROWS),
                                 _pack([local[n] for n in SMALL_GRAD_ORDER], SMALL_ADAM_ROWS),
                                 _pack([mom_m[n] for n in SMALL_GRAD_ORDER], SMALL_ADAM_ROWS),
                                 _pack([mom_v[n] for n in SMALL_GRAD_ORDER], SMALL_ADAM_ROWS))
    for n, d, m, v in zip(SMALL_GRAD_ORDER, _unpack(d_s, shapes), _unpack(m_s, shapes), _unpack(v_s, shapes)):
        out_g[n], out_d[n], out_m[n], out_v[n] = small_g[n], d, m, v
    return (loss, grad_x, *[out_g[n] for n in WEIGHTS], *[out_d[n] for n in WEIGHTS],
            *[out_m[n] for n in WEIGHTS], *[out_v[n] for n in WEIGHTS])
```
